```python
import jax, jax.numpy as jnp
from jax import lax
import numpy as np

D_MODEL = 1024
BATCH = 16
SEQ = 2048
DEPTH = 1

MEM_LEN = 256
EPS = 1e-6

HG_HEADS = 4
HG_DK = 128
HG_DV = 128
HG_WIDTH = HG_HEADS * HG_DV
HG_KEY_WIDTH = HG_HEADS * HG_DK
HG_CHUNK = 64

SW_HEADS = 8
SW_KV_HEADS = 2
SW_GROUP = SW_HEADS // SW_KV_HEADS
SW_HEAD_DIM = 64
SW_WIDTH = SW_HEADS * SW_HEAD_DIM
WINDOW = 128
SW_BLOCK = 128
ROPE_THETA = 500000.0
ROT_DIM = SW_HEAD_DIM // 4

MIX_WIDTH = HG_WIDTH + SW_WIDTH
IN_SPLITS = (HG_KEY_WIDTH, HG_KEY_WIDTH, HG_WIDTH, HG_WIDTH,
             SW_WIDTH, SW_KV_HEADS * SW_HEAD_DIM, SW_KV_HEADS * SW_HEAD_DIM)
IN_WIDTH = sum(IN_SPLITS)

XA_HEADS = 4
XA_HEAD_DIM = 128
XA_WIDTH = XA_HEADS * XA_HEAD_DIM

D_FF = 4 * D_MODEL

kernel_name = "hymba_style_hgrn2_swa_sink_hybrid"


def rms_norm(x, g):
    xf = x.astype(jnp.float32)
    y = xf * lax.rsqrt(jnp.mean(xf * xf, axis=-1, keepdims=True) + EPS)
    return (y * g.astype(jnp.float32)).astype(x.dtype)


def head_rms(t, g):
    return t * lax.rsqrt(jnp.mean(t * t, axis=-1, keepdims=True) + EPS) * g.astype(jnp.float32)


def partial_rope(t, positions):
    half = ROT_DIM // 2
    inv_freq = ROPE_THETA ** (-(jnp.arange(half, dtype=jnp.float32) * 2.0 / ROT_DIM))
    ang = positions.astype(jnp.float32)[..., None] * inv_freq
    cos = jnp.cos(ang)[:, :, None, :]
    sin = jnp.sin(ang)[:, :, None, :]
    x1 = t[..., :half]
    x2 = t[..., half:ROT_DIM]
    return jnp.concatenate([x1 * cos - x2 * sin, x2 * cos + x1 * sin, t[..., ROT_DIM:]], axis=-1)


def hgrn2_group(q, f_logit, i, g, lb, norm_g):
    B, S, _ = q.shape
    H, C = HG_HEADS, HG_CHUNK
    nc = S // C
    qf = q.astype(jnp.float32).reshape(B, S, H, HG_DK)
    vf = i.astype(jnp.float32).reshape(B, S, H, HG_DV)
    lbh = lb.astype(jnp.float32).reshape(H, HG_DK)
    f = lbh + (1.0 - lbh) * jax.nn.sigmoid(f_logit.astype(jnp.float32).reshape(B, S, H, HG_DK))
    kf = 1.0 - f
    log_f = jnp.log(f)

    def to_chunks(t):
        return t.reshape(B, nc, C, H, t.shape[-1]).transpose(1, 0, 3, 2, 4)

    causal = jnp.tril(jnp.ones((C, C), dtype=bool))[:, :, None]

    def step(state, inp):
        qc, kc, vc, lfc = inp
        b = jnp.cumsum(lfc, axis=2)
        o_inter = jnp.einsum('bhik,bhkv->bhiv', qc * jnp.exp(b), state)
        diff = b[:, :, :, None, :] - b[:, :, None, :, :]
        decay = jnp.exp(jnp.where(causal, diff, -jnp.inf))
        scores = jnp.einsum('bhik,bhjk,bhijk->bhij', qc, kc, decay)
        o_intra = jnp.einsum('bhij,bhjv->bhiv', scores, vc)
        b_last = b[:, :, -1:, :]
        new_state = (jnp.exp(b_last[:, :, 0, :])[..., None] * state
                     + jnp.einsum('bhjk,bhjv->bhkv', kc * jnp.exp(b_last - b), vc))
        return new_state, o_inter + o_intra

    s0 = jnp.zeros((B, H, HG_DK, HG_DV), jnp.float32)
    _, o = lax.scan(step, s0, (to_chunks(qf), to_chunks(kf), to_chunks(vf), to_chunks(log_f)))
    o = o.transpose(1, 0, 3, 2, 4).reshape(B, S, H, HG_DV)
    gate = jax.nn.silu(g.astype(jnp.float32)).reshape(B, S, H, HG_DV)
    return (head_rms(o, norm_g) * gate).reshape(B, S, HG_WIDTH)


def swa_group(q, k, v, positions, q_norm_g, k_norm_g, sinks):
    B, S, _ = q.shape
    nb = S // SW_BLOCK
    qf = partial_rope(head_rms(q.astype(jnp.float32).reshape(B, S, SW_HEADS, SW_HEAD_DIM), q_norm_g), positions)
    kf = partial_rope(head_rms(k.astype(jnp.float32).reshape(B, S, SW_KV_HEADS, SW_HEAD_DIM), k_norm_g), positions)
    vf = v.astype(jnp.float32).reshape(B, S, SW_KV_HEADS, SW_HEAD_DIM)

    qb = qf.reshape(B, nb, SW_BLOCK, SW_KV_HEADS, SW_GROUP, SW_HEAD_DIM)

    def band(t):
        tp = jnp.pad(t, ((0, 0), (SW_BLOCK, 0), (0, 0), (0, 0))).reshape(B, nb + 1, SW_BLOCK, SW_KV_HEADS, SW_HEAD_DIM)
        return jnp.concatenate([tp[:, :-1], tp[:, 1:]], axis=2)

    kw, vw = band(kf), band(vf)
    s = jnp.einsum('bnqhgd,bnkhd->bnhgqk', qb, kw) * (SW_HEAD_DIM ** -0.5)

    qi = jnp.arange(SW_BLOCK)[:, None]
    kj = jnp.arange(2 * SW_BLOCK)[None, :]
    dist = qi + SW_BLOCK - kj
    in_band = (dist >= 0) & (dist < WINDOW)
    blk = jnp.arange(nb)[:, None, None]
    valid = in_band[None] & (blk * SW_BLOCK + kj[None] - SW_BLOCK >= 0)
    s = jnp.where(valid[None, :, None, None], s, -jnp.inf)

    sink = sinks.astype(jnp.float32).reshape(SW_KV_HEADS, SW_GROUP)[None, None, :, :, None, None]
    m = jnp.maximum(jnp.max(s, axis=-1, keepdims=True), sink)
    p = jnp.exp(s - m)
    denom = jnp.sum(p, axis=-1, keepdims=True) + jnp.exp(sink - m)
    o = jnp.einsum('bnhgqk,bnkhd->bnqhgd', p / denom, vw)
    return o.reshape(B, S, SW_WIDTH)


def memory_cross_attention(hn, mn, wq, wkv, q_norm_g, k_norm_g, wo):
    B, S, _ = hn.shape
    M = mn.shape[1]
    q = head_rms((hn @ wq).astype(jnp.float32).reshape(B, S, XA_HEADS, XA_HEAD_DIM), q_norm_g)
    kv = (mn @ wkv).astype(jnp.float32)
    k = head_rms(kv[..., :XA_WIDTH].reshape(B, M, XA_HEADS, XA_HEAD_DIM), k_norm_g)
    v = kv[..., XA_WIDTH:].reshape(B, M, XA_HEADS, XA_HEAD_DIM)
    s = jnp.einsum('bshd,bmhd->bhsm', q, k) * (XA_HEAD_DIM ** -0.5)
    p = jax.nn.softmax(s, axis=-1)
    o = jnp.einsum('bhsm,bmhd->bshd', p, v).reshape(B, S, XA_WIDTH)
    return o.astype(hn.dtype) @ wo


def _fwd_setup_inputs(seed: int = 0) -> dict:
    key = jax.random.key(seed)
    ks = jax.random.split(key, 24)
    f32 = jnp.float32

    def nrm(k, shape, scale):
        return jax.random.normal(k, shape, f32) * scale

    def gain(k, shape):
        return 1.0 + 0.02 * jax.random.normal(k, shape, f32)

    offs = jax.random.randint(ks[2], (BATCH, 1), 0, 4096, dtype=jnp.int32)
    positions = (offs + jnp.arange(SEQ, dtype=jnp.int32)[None, :]).astype(jnp.int32)
    return {
        "x": nrm(ks[0], (BATCH, SEQ, D_MODEL), 1.0),
        "mem": nrm(ks[1], (BATCH, MEM_LEN, D_MODEL), 1.0),
        "positions": positions,
        "norm1_g": gain(ks[3], (DEPTH, D_MODEL)),
        "w_in": nrm(ks[4], (DEPTH, D_MODEL, IN_WIDTH), D_MODEL ** -0.5),
        "hg_lower_bounds": nrm(ks[5], (DEPTH + 1, HG_KEY_WIDTH), 0.1),
        "hg_norm_g": gain(ks[6], (DEPTH, HG_DV)),
        "sw_q_norm_g": gain(ks[7], (DEPTH, SW_HEAD_DIM)),
        "sw_k_norm_g": gain(ks[8], (DEPTH, SW_HEAD_DIM)),
        "sw_sinks": nrm(ks[9], (DEPTH, SW_HEADS), 0.5),
        "w_out": nrm(ks[10], (DEPTH, MIX_WIDTH, D_MODEL), MIX_WIDTH ** -0.5),
        "norm2_g": gain(ks[11], (DEPTH, D_MODEL)),
        "mem_norm_g": gain(ks[12], (DEPTH, D_MODEL)),
        "xa_wq": nrm(ks[13], (DEPTH, D_MODEL, XA_WIDTH), D_MODEL ** -0.5),
        "xa_wkv": nrm(ks[14], (DEPTH, D_MODEL, 2 * XA_WIDTH), D_MODEL ** -0.5),
        "xa_q_norm_g": gain(ks[15], (DEPTH, XA_HEAD_DIM)),
        "xa_k_norm_g": gain(ks[16], (DEPTH, XA_HEAD_DIM)),
        "xa_wo": nrm(ks[17], (DEPTH, XA_WIDTH, D_MODEL), XA_WIDTH ** -0.5),
        "norm3_g": gain(ks[18], (DEPTH, D_MODEL)),
        "mlp_up": nrm(ks[19], (DEPTH, D_MODEL, D_FF), D_MODEL ** -0.5),
        "mlp_down": nrm(ks[20], (DEPTH, D_FF, D_MODEL), D_FF ** -0.5),
    }


def _fwd_reference(x, mem, positions, norm1_g, w_in, hg_lower_bounds, hg_norm_g, sw_q_norm_g,
              sw_k_norm_g, sw_sinks, w_out, norm2_g, mem_norm_g, xa_wq, xa_wkv, xa_q_norm_g,
              xa_k_norm_g, xa_wo, norm3_g, mlp_up, mlp_down):
    lb_all = jnp.cumsum(jax.nn.softmax(hg_lower_bounds.astype(jnp.float32), axis=0), axis=0)
    split_points = [int(p) for p in np.cumsum(IN_SPLITS)[:-1]]
    h = x
    for l in range(DEPTH):
        hn = rms_norm(h, norm1_g[l])
        proj = hn @ w_in[l]
        hq, hf, hi, hg, sq, sk, sv = jnp.split(proj, split_points, axis=-1)
        y_hg = hgrn2_group(hq, hf, hi, hg, lb_all[l], hg_norm_g[l])
        y_sw = swa_group(sq, sk, sv, positions, sw_q_norm_g[l], sw_k_norm_g[l], sw_sinks[l])
        mix = jnp.concatenate([y_hg, y_sw], axis=-1).astype(h.dtype) @ w_out[l]
        h = h + mix
        hn = rms_norm(h, norm2_g[l])
        mn = rms_norm(mem, mem_norm_g[l])
        h = h + memory_cross_attention(hn, mn, xa_wq[l], xa_wkv[l], xa_q_norm_g[l], xa_k_norm_g[l], xa_wo[l])
        hn = rms_norm(h, norm3_g[l])
        a = jax.nn.relu(hn @ mlp_up[l])
        h = h + (a * a) @ mlp_down[l]
    return h


import jax as _jax
import jax.numpy as _jnp

TWIN_FORMAT = 'train_step'
FWD_PARAMS = ['x', 'mem', 'positions', 'norm1_g', 'w_in', 'hg_lower_bounds', 'hg_norm_g', 'sw_q_norm_g', 'sw_k_norm_g', 'sw_sinks', 'w_out', 'norm2_g', 'mem_norm_g', 'xa_wq', 'xa_wkv', 'xa_q_norm_g', 'xa_k_norm_g', 'xa_wo', 'norm3_g', 'mlp_up', 'mlp_down']
TWIN_WEIGHTS = ['norm1_g', 'w_in', 'hg_lower_bounds', 'hg_norm_g', 'sw_q_norm_g', 'sw_k_norm_g', 'sw_sinks', 'w_out', 'norm2_g', 'mem_norm_g', 'xa_wq', 'xa_wkv', 'xa_q_norm_g', 'xa_k_norm_g', 'xa_wo', 'norm3_g', 'mlp_up', 'mlp_down']
TWIN_DIFF_INPUT = 'x'
TWIN_INPUTS = ['x', 'mem', 'positions', 'norm1_g', 'w_in', 'hg_lower_bounds', 'hg_norm_g', 'sw_q_norm_g', 'sw_k_norm_g', 'sw_sinks', 'w_out', 'norm2_g', 'mem_norm_g', 'xa_wq', 'xa_wkv', 'xa_q_norm_g', 'xa_k_norm_g', 'xa_wo', 'norm3_g', 'mlp_up', 'mlp_down', 'loss_target', 'm_norm1_g', 'm_w_in', 'm_hg_lower_bounds', 'm_hg_norm_g', 'm_sw_q_norm_g', 'm_sw_k_norm_g', 'm_sw_sinks', 'm_w_out', 'm_norm2_g', 'm_mem_norm_g', 'm_xa_wq', 'm_xa_wkv', 'm_xa_q_norm_g', 'm_xa_k_norm_g', 'm_xa_wo', 'm_norm3_g', 'm_mlp_up', 'm_mlp_down', 'v_norm1_g', 'v_w_in', 'v_hg_lower_bounds', 'v_hg_norm_g', 'v_sw_q_norm_g', 'v_sw_k_norm_g', 'v_sw_sinks', 'v_w_out', 'v_norm2_g', 'v_mem_norm_g', 'v_xa_wq', 'v_xa_wkv', 'v_xa_q_norm_g', 'v_xa_k_norm_g', 'v_xa_wo', 'v_norm3_g', 'v_mlp_up', 'v_mlp_down']
TWIN_OUTPUTS = ['loss', 'grad_x', 'grad_norm1_g', 'grad_w_in', 'grad_hg_lower_bounds', 'grad_hg_norm_g', 'grad_sw_q_norm_g', 'grad_sw_k_norm_g', 'grad_sw_sinks', 'grad_w_out', 'grad_norm2_g', 'grad_mem_norm_g', 'grad_xa_wq', 'grad_xa_wkv', 'grad_xa_q_norm_g', 'grad_xa_k_norm_g', 'grad_xa_wo', 'grad_norm3_g', 'grad_mlp_up', 'grad_mlp_down', 'delta_norm1_g', 'delta_w_in', 'delta_hg_lower_bounds', 'delta_hg_norm_g', 'delta_sw_q_norm_g', 'delta_sw_k_norm_g', 'delta_sw_sinks', 'delta_w_out', 'delta_norm2_g', 'delta_mem_norm_g', 'delta_xa_wq', 'delta_xa_wkv', 'delta_xa_q_norm_g', 'delta_xa_k_norm_g', 'delta_xa_wo', 'delta_norm3_g', 'delta_mlp_up', 'delta_mlp_down', 'new_m_norm1_g', 'new_m_w_in', 'new_m_hg_lower_bounds', 'new_m_hg_norm_g', 'new_m_sw_q_norm_g', 'new_m_sw_k_norm_g', 'new_m_sw_sinks', 'new_m_w_out', 'new_m_norm2_g', 'new_m_mem_norm_g', 'new_m_xa_wq', 'new_m_xa_wkv', 'new_m_xa_q_norm_g', 'new_m_xa_k_norm_g', 'new_m_xa_wo', 'new_m_norm3_g', 'new_m_mlp_up', 'new_m_mlp_down', 'new_v_norm1_g', 'new_v_w_in', 'new_v_hg_lower_bounds', 'new_v_hg_norm_g', 'new_v_sw_q_norm_g', 'new_v_sw_k_norm_g', 'new_v_sw_sinks', 'new_v_w_out', 'new_v_norm2_g', 'new_v_mem_norm_g', 'new_v_xa_wq', 'new_v_xa_wkv', 'new_v_xa_q_norm_g', 'new_v_xa_k_norm_g', 'new_v_xa_wo', 'new_v_norm3_g', 'new_v_mlp_up', 'new_v_mlp_down']
TWIN_LEAF_KINDS = {'loss': 'loss', 'grad_x': 'grad_x', 'grad_norm1_g': 'grad_w', 'grad_w_in': 'grad_w', 'grad_hg_lower_bounds': 'grad_w', 'grad_hg_norm_g': 'grad_w', 'grad_sw_q_norm_g': 'grad_w', 'grad_sw_k_norm_g': 'grad_w', 'grad_sw_sinks': 'grad_w', 'grad_w_out': 'grad_w', 'grad_norm2_g': 'grad_w', 'grad_mem_norm_g': 'grad_w', 'grad_xa_wq': 'grad_w', 'grad_xa_wkv': 'grad_w', 'grad_xa_q_norm_g': 'grad_w', 'grad_xa_k_norm_g': 'grad_w', 'grad_xa_wo': 'grad_w', 'grad_norm3_g': 'grad_w', 'grad_mlp_up': 'grad_w', 'grad_mlp_down': 'grad_w', 'delta_norm1_g': 'delta_w', 'delta_w_in': 'delta_w', 'delta_hg_lower_bounds': 'delta_w', 'delta_hg_norm_g': 'delta_w', 'delta_sw_q_norm_g': 'delta_w', 'delta_sw_k_norm_g': 'delta_w', 'delta_sw_sinks': 'delta_w', 'delta_w_out': 'delta_w', 'delta_norm2_g': 'delta_w', 'delta_mem_norm_g': 'delta_w', 'delta_xa_wq': 'delta_w', 'delta_xa_wkv': 'delta_w', 'delta_xa_q_norm_g': 'delta_w', 'delta_xa_k_norm_g': 'delta_w', 'delta_xa_wo': 'delta_w', 'delta_norm3_g': 'delta_w', 'delta_mlp_up': 'delta_w', 'delta_mlp_down': 'delta_w', 'new_m_norm1_g': 'new_m', 'new_m_w_in': 'new_m', 'new_m_hg_lower_bounds': 'new_m', 'new_m_hg_norm_g': 'new_m', 'new_m_sw_q_norm_g': 'new_m', 'new_m_sw_k_norm_g': 'new_m', 'new_m_sw_sinks': 'new_m', 'new_m_w_out': 'new_m', 'new_m_norm2_g': 'new_m', 'new_m_mem_norm_g': 'new_m', 'new_m_xa_wq': 'new_m', 'new_m_xa_wkv': 'new_m', 'new_m_xa_q_norm_g': 'new_m', 'new_m_xa_k_norm_g': 'new_m', 'new_m_xa_wo': 'new_m', 'new_m_norm3_g': 'new_m', 'new_m_mlp_up': 'new_m', 'new_m_mlp_down': 'new_m', 'new_v_norm1_g': 'new_v', 'new_v_w_in': 'new_v', 'new_v_hg_lower_bounds': 'new_v', 'new_v_hg_norm_g': 'new_v', 'new_v_sw_q_norm_g': 'new_v', 'new_v_sw_k_norm_g': 'new_v', 'new_v_sw_sinks': 'new_v', 'new_v_w_out': 'new_v', 'new_v_norm2_g': 'new_v', 'new_v_mem_norm_g': 'new_v', 'new_v_xa_wq': 'new_v', 'new_v_xa_wkv': 'new_v', 'new_v_xa_q_norm_g': 'new_v', 'new_v_xa_k_norm_g': 'new_v', 'new_v_xa_wo': 'new_v', 'new_v_norm3_g': 'new_v', 'new_v_mlp_up': 'new_v', 'new_v_mlp_down': 'new_v'}


def _forward(args):
    return _fwd_reference(*[args[k] for k in FWD_PARAMS])


def _output_shape():
    out = _jax.eval_shape(lambda: _forward(_fwd_setup_inputs(0)))
    return out.shape, out.dtype

N_MICROBATCH = 1
ADAM_LR = 0.001
ADAM_B1 = 0.9
ADAM_B2 = 0.999
ADAM_EPS = 1e-08
ADAM_WD = 0.01
ADAM_STEP = 10
PER_EXAMPLE_BATCH_AXIS = {'x': 0, 'mem': 0, 'positions': 0, 'loss_target': 0}
SHARED_INPUTS = []
_WEIGHT_DTYPES = {'norm1_g': _jnp.float32, 'w_in': _jnp.float32, 'hg_lower_bounds': _jnp.float32, 'hg_norm_g': _jnp.float32, 'sw_q_norm_g': _jnp.float32, 'sw_k_norm_g': _jnp.float32, 'sw_sinks': _jnp.float32, 'w_out': _jnp.float32, 'norm2_g': _jnp.float32, 'mem_norm_g': _jnp.float32, 'xa_wq': _jnp.float32, 'xa_wkv': _jnp.float32, 'xa_q_norm_g': _jnp.float32, 'xa_k_norm_g': _jnp.float32, 'xa_wo': _jnp.float32, 'norm3_g': _jnp.float32, 'mlp_up': _jnp.float32, 'mlp_down': _jnp.float32}
MOMENT_SCALE = {'norm1_g': 5.969832e+00, 'w_in': 4.416045e-01, 'hg_lower_bounds': 2.910783e-01, 'hg_norm_g': 4.427764e+01, 'sw_q_norm_g': 2.809565e+00, 'sw_k_norm_g': 2.835555e+00, 'sw_sinks': 8.232736e-01, 'w_out': 4.140409e-01, 'norm2_g': 8.956253e-02, 'mem_norm_g': 7.275753e-01, 'xa_wq': 1.302606e-01, 'xa_wkv': 6.850310e-01, 'xa_q_norm_g': 2.300679e+00, 'xa_k_norm_g': 2.295864e+00, 'xa_wo': 7.738588e-01, 'norm3_g': 9.542341e+01, 'mlp_up': 7.318839e-01, 'mlp_down': 7.687267e+00}


def _to_microbatches(a, axis):
    t = _jnp.moveaxis(a, axis, 0)
    t = t.reshape((N_MICROBATCH, t.shape[0] // N_MICROBATCH) + t.shape[1:])
    return _jnp.moveaxis(t, 1, axis + 1)


def setup_inputs(seed: int = 0) -> dict:
    inp = _fwd_setup_inputs(seed)
    key = _jax.random.fold_in(_jax.random.key(seed), 7919)
    shape, _ = _output_shape()
    out = dict(inp)
    out["loss_target"] = _jax.random.normal(_jax.random.fold_in(key, 0), shape, _jnp.float32)
    for i, name in enumerate(TWIN_WEIGHTS):
        w = inp[name].astype(_jnp.float32)
        if MOMENT_SCALE is None:
            s = _jnp.sqrt(_jnp.mean(_jnp.square(w)) + 1e-30)
        else:
            s = MOMENT_SCALE[name]
        km, kv = _jax.random.split(_jax.random.fold_in(key, i + 1))
        out[name] = w
        out["m_" + name] = s * _jax.random.normal(km, w.shape, _jnp.float32)
        out["v_" + name] = (s * s) * _jax.random.uniform(kv, w.shape, _jnp.float32, 0.5, 1.5)
    if N_MICROBATCH > 1:
        for name, axis in PER_EXAMPLE_BATCH_AXIS.items():
            out[name] = _to_microbatches(out[name], axis)
    return {'x': out['x'], 'mem': out['mem'], 'positions': out['positions'], 'norm1_g': out['norm1_g'], 'w_in': out['w_in'], 'hg_lower_bounds': out['hg_lower_bounds'], 'hg_norm_g': out['hg_norm_g'], 'sw_q_norm_g': out['sw_q_norm_g'], 'sw_k_norm_g': out['sw_k_norm_g'], 'sw_sinks': out['sw_sinks'], 'w_out': out['w_out'], 'norm2_g': out['norm2_g'], 'mem_norm_g': out['mem_norm_g'], 'xa_wq': out['xa_wq'], 'xa_wkv': out['xa_wkv'], 'xa_q_norm_g': out['xa_q_norm_g'], 'xa_k_norm_g': out['xa_k_norm_g'], 'xa_wo': out['xa_wo'], 'norm3_g': out['norm3_g'], 'mlp_up': out['mlp_up'], 'mlp_down': out['mlp_down'], 'loss_target': out['loss_target'], 'm_norm1_g': out['m_norm1_g'], 'm_w_in': out['m_w_in'], 'm_hg_lower_bounds': out['m_hg_lower_bounds'], 'm_hg_norm_g': out['m_hg_norm_g'], 'm_sw_q_norm_g': out['m_sw_q_norm_g'], 'm_sw_k_norm_g': out['m_sw_k_norm_g'], 'm_sw_sinks': out['m_sw_sinks'], 'm_w_out': out['m_w_out'], 'm_norm2_g': out['m_norm2_g'], 'm_mem_norm_g': out['m_mem_norm_g'], 'm_xa_wq': out['m_xa_wq'], 'm_xa_wkv': out['m_xa_wkv'], 'm_xa_q_norm_g': out['m_xa_q_norm_g'], 'm_xa_k_norm_g': out['m_xa_k_norm_g'], 'm_xa_wo': out['m_xa_wo'], 'm_norm3_g': out['m_norm3_g'], 'm_mlp_up': out['m_mlp_up'], 'm_mlp_down': out['m_mlp_down'], 'v_norm1_g': out['v_norm1_g'], 'v_w_in': out['v_w_in'], 'v_hg_lower_bounds': out['v_hg_lower_bounds'], 'v_hg_norm_g': out['v_hg_norm_g'], 'v_sw_q_norm_g': out['v_sw_q_norm_g'], 'v_sw_k_norm_g': out['v_sw_k_norm_g'], 'v_sw_sinks': out['v_sw_sinks'], 'v_w_out': out['v_w_out'], 'v_norm2_g': out['v_norm2_g'], 'v_mem_norm_g': out['v_mem_norm_g'], 'v_xa_wq': out['v_xa_wq'], 'v_xa_wkv': out['v_xa_wkv'], 'v_xa_q_norm_g': out['v_xa_q_norm_g'], 'v_xa_k_norm_g': out['v_xa_k_norm_g'], 'v_xa_wo': out['v_xa_wo'], 'v_norm3_g': out['v_norm3_g'], 'v_mlp_up': out['v_mlp_up'], 'v_mlp_down': out['v_mlp_down']}


def _loss(weights, diff, rest, loss_target):
    with _jax.named_scope("forward"):
        args = {**rest, TWIN_DIFF_INPUT: diff, **{k: w.astype(_WEIGHT_DTYPES[k]) for k, w in weights.items()}}
        y = _forward(args)
    with _jax.named_scope("loss_head"):
        err = _jnp.square(y.astype(_jnp.float32) - loss_target)
        return 0.5 * _jnp.sum(_jnp.mean(err, axis=-1)) if err.ndim else 0.5 * err


def _adamw(w, g, m, v):
    m = ADAM_B1 * m + (1.0 - ADAM_B1) * g
    v = ADAM_B2 * v + (1.0 - ADAM_B2) * _jnp.square(g)
    m_hat = m / (1.0 - ADAM_B1 ** ADAM_STEP)
    v_hat = v / (1.0 - ADAM_B2 ** ADAM_STEP)
    delta = -ADAM_LR * (m_hat / (_jnp.sqrt(v_hat) + ADAM_EPS) + ADAM_WD * w)
    return delta, m, v


def reference(x, mem, positions, norm1_g, w_in, hg_lower_bounds, hg_norm_g, sw_q_norm_g, sw_k_norm_g, sw_sinks, w_out, norm2_g, mem_norm_g, xa_wq, xa_wkv, xa_q_norm_g, xa_k_norm_g, xa_wo, norm3_g, mlp_up, mlp_down, loss_target, m_norm1_g, m_w_in, m_hg_lower_bounds, m_hg_norm_g, m_sw_q_norm_g, m_sw_k_norm_g, m_sw_sinks, m_w_out, m_norm2_g, m_mem_norm_g, m_xa_wq, m_xa_wkv, m_xa_q_norm_g, m_xa_k_norm_g, m_xa_wo, m_norm3_g, m_mlp_up, m_mlp_down, v_norm1_g, v_w_in, v_hg_lower_bounds, v_hg_norm_g, v_sw_q_norm_g, v_sw_k_norm_g, v_sw_sinks, v_w_out, v_norm2_g, v_mem_norm_g, v_xa_wq, v_xa_wkv, v_xa_q_norm_g, v_xa_k_norm_g, v_xa_wo, v_norm3_g, v_mlp_up, v_mlp_down):
    given = dict(x=x, mem=mem, positions=positions, norm1_g=norm1_g, w_in=w_in, hg_lower_bounds=hg_lower_bounds, hg_norm_g=hg_norm_g, sw_q_norm_g=sw_q_norm_g, sw_k_norm_g=sw_k_norm_g, sw_sinks=sw_sinks, w_out=w_out, norm2_g=norm2_g, mem_norm_g=mem_norm_g, xa_wq=xa_wq, xa_wkv=xa_wkv, xa_q_norm_g=xa_q_norm_g, xa_k_norm_g=xa_k_norm_g, xa_wo=xa_wo, norm3_g=norm3_g, mlp_up=mlp_up, mlp_down=mlp_down, loss_target=loss_target, m_norm1_g=m_norm1_g, m_w_in=m_w_in, m_hg_lower_bounds=m_hg_lower_bounds, m_hg_norm_g=m_hg_norm_g, m_sw_q_norm_g=m_sw_q_norm_g, m_sw_k_norm_g=m_sw_k_norm_g, m_sw_sinks=m_sw_sinks, m_w_out=m_w_out, m_norm2_g=m_norm2_g, m_mem_norm_g=m_mem_norm_g, m_xa_wq=m_xa_wq, m_xa_wkv=m_xa_wkv, m_xa_q_norm_g=m_xa_q_norm_g, m_xa_k_norm_g=m_xa_k_norm_g, m_xa_wo=m_xa_wo, m_norm3_g=m_norm3_g, m_mlp_up=m_mlp_up, m_mlp_down=m_mlp_down, v_norm1_g=v_norm1_g, v_w_in=v_w_in, v_hg_lower_bounds=v_hg_lower_bounds, v_hg_norm_g=v_hg_norm_g, v_sw_q_norm_g=v_sw_q_norm_g, v_sw_k_norm_g=v_sw_k_norm_g, v_sw_sinks=v_sw_sinks, v_w_out=v_w_out, v_norm2_g=v_norm2_g, v_mem_norm_g=v_mem_norm_g, v_xa_wq=v_xa_wq, v_xa_wkv=v_xa_wkv, v_xa_q_norm_g=v_xa_q_norm_g, v_xa_k_norm_g=v_xa_k_norm_g, v_xa_wo=v_xa_wo, v_norm3_g=v_norm3_g, v_mlp_up=v_mlp_up, v_mlp_down=v_mlp_down)
    weights = {n: given[n] for n in TWIN_WEIGHTS}
    shared = {n: given[n] for n in SHARED_INPUTS}
    per_example = {n: given[n] for n in ['x', 'mem', 'positions']}
    grad_fn = _jax.value_and_grad(_loss, argnums=(0, 1))

    def one_microbatch(ex, loss_target):
        ex = dict(ex)
        diff = ex.pop(TWIN_DIFF_INPUT)
        return grad_fn(weights, diff, {**shared, **ex}, loss_target)

    if N_MICROBATCH == 1:
        loss, (grad_w, grad_x) = one_microbatch(per_example, given["loss_target"])
    else:
        def body(carry, xs):
            loss_sum, grad_sum = carry
            l_k, (gw_k, gx_k) = one_microbatch(xs[0], xs[1])
            with _jax.named_scope("update"):
                return (loss_sum + l_k, _jax.tree.map(_jnp.add, grad_sum, gw_k)), gx_k

        init = (_jnp.zeros((), _jnp.float32), _jax.tree.map(_jnp.zeros_like, weights))
        (loss, grad_w), grad_x = _jax.lax.scan(body, init, (per_example, given["loss_target"]))
    with _jax.named_scope("update"):
        delta_w, new_m, new_v = {}, {}, {}
        for n in TWIN_WEIGHTS:
            delta_w[n], new_m[n], new_v[n] = _adamw(weights[n], grad_w[n], given["m_" + n], given["v_" + n])
    return (loss, grad_x, *[grad_w[n] for n in TWIN_WEIGHTS], *[delta_w[n] for n in TWIN_WEIGHTS],
            *[new_m[n] for n in TWIN_WEIGHTS], *[new_v[n] for n in TWIN_WEIGHTS])
```

```python
import functools

import numpy as np
import jax
import jax.numpy as jnp
from jax import lax
from jax.experimental import pallas as pl
from jax.experimental.pallas import tpu as pltpu

F32 = jnp.float32
_MXU_DTYPE = jnp.bfloat16

EPS = 1e-6
HG_HEADS = 4
HG_D = 128
HG_CHUNK = 64
HG_LEVELS = (32, 16, 8, 4, 2, 1)
SW_HEADS = 8
SW_KV_HEADS = 2
SW_GROUP = SW_HEADS // SW_KV_HEADS
SW_HD = 64
SW_BLOCK = 128
ROPE_THETA = 500000.0
ROT_DIM = SW_HD // 4
XA_HEADS = 4
XA_HD = 128
HG_COLS = 4 * HG_HEADS * HG_D
SW_COLS = (SW_HEADS + 2 * SW_KV_HEADS) * SW_HD

ADAM_LR = 0.001
ADAM_B1 = 0.9
ADAM_B2 = 0.999
ADAM_EPS = 1e-08
ADAM_WD = 0.01
ADAM_STEP = 10

VMEM_LIMIT = 56 * 1024 * 1024
MESH = pl.DeviceIdType.MESH

NN = ((1,), (0,))
NT = ((1,), (1,))
TN = ((0,), (0,))


def _mx(v):
    return v.astype(_MXU_DTYPE)


def _dot(a, b, dims=NN):
    return lax.dot_general(_mx(a), _mx(b), (dims, ((), ())), preferred_element_type=F32)


def _split_dot(a, v, dims, parts):
    acc = None
    rest = v
    for p in range(parts):
        piece = _mx(rest)
        term = lax.dot_general(a, piece, (dims, ((), ())), preferred_element_type=F32)
        acc = term if acc is None else acc + term
        if p + 1 < parts:
            rest = rest - piece.astype(F32)
    return acc


def _params(sem):
    return pltpu.CompilerParams(dimension_semantics=sem, vmem_limit_bytes=VMEM_LIMIT)


def _mm(a, b, mode, m, n, k, *, name, tm=1024, tn=1024, tk=512, a_spec=None, b_spec=None, extras=(), epilogue=None,
        out_dtypes=(F32,), out_shape=None, out_spec=None):
    tm, tn, tk = min(tm, m), min(tn, n), min(tk, k)
    assert m % tm == 0 and n % tn == 0 and k % tk == 0, (name, m, n, k, tm, tn, tk)
    gi, gj, gk = m // tm, n // tn, k // tk
    if a_spec is None:
        a_spec = (pl.BlockSpec((tk, tm), lambda i, j, kk: (kk, i)) if mode == TN
                  else pl.BlockSpec((tm, tk), lambda i, j, kk: (i, kk)))
    if b_spec is None:
        b_spec = (pl.BlockSpec((tn, tk), lambda i, j, kk: (j, kk)) if mode == NT
                  else pl.BlockSpec((tk, tn), lambda i, j, kk: (kk, j)))
    mn_spec = pl.BlockSpec((tm, tn), lambda i, j, kk: (i, j))
    if epilogue is None:
        epilogue = lambda acc: (acc,)
    n_ex, n_out = len(extras), len(out_dtypes)
    if out_shape is None:
        out_shape = tuple(jax.ShapeDtypeStruct((m, n), d) for d in out_dtypes)
        out_spec = tuple(mn_spec for _ in out_dtypes)

    def body(*refs):
        a_ref, b_ref = refs[0], refs[1]
        ex = refs[2:2 + n_ex]
        outs = refs[2 + n_ex:2 + n_ex + n_out]

        def finish(acc):
            res = epilogue(acc, *[e[...] for e in ex])
            for o, r in zip(outs, res):
                o[...] = r.astype(o.dtype)

        if gk == 1:
            finish(_dot(a_ref[...], b_ref[...], mode))
        else:
            acc_ref = refs[-1]
            kk = pl.program_id(2)

            @pl.when(kk == 0)
            def _():
                acc_ref[...] = jnp.zeros_like(acc_ref)

            acc_ref[...] += _dot(a_ref[...], b_ref[...], mode)

            @pl.when(kk == gk - 1)
            def _():
                finish(acc_ref[...])

    return pl.pallas_call(
        body, name=name, grid=(gi, gj, gk),
        in_specs=[a_spec, b_spec] + [mn_spec] * n_ex,
        out_specs=out_spec, out_shape=out_shape,
        scratch_shapes=[pltpu.VMEM((tm, tn), F32)] if gk > 1 else [],
        compiler_params=_params(("parallel", "parallel", "arbitrary")),
    )(a, b, *extras)


def _rms_fwd(x, g, *, name, tm=512):
    t, d = x.shape
    tm = min(tm, t)

    def body(x_ref, g_ref, o_ref):
        xv = x_ref[...]
        r = lax.rsqrt(jnp.mean(xv * xv, axis=1, keepdims=True) + EPS)
        o_ref[...] = (xv * r * g_ref[...]).astype(o_ref.dtype)

    return pl.pallas_call(
        body, name=name, grid=(t // tm,),
        in_specs=[pl.BlockSpec((tm, d), lambda i: (i, 0)), pl.BlockSpec((1, d), lambda i: (0, 0))],
        out_specs=pl.BlockSpec((tm, d), lambda i: (i, 0)),
        out_shape=jax.ShapeDtypeStruct((t, d), _MXU_DTYPE),
        compiler_params=_params(("parallel",)),
    )(x, g)


def _rms_bwd(x, g, dy, dres, *, name, tm=512):
    t, d = x.shape
    tm = min(tm, t)
    has_res = dres is not None

    def body(*refs):
        x_ref, g_ref, dy_ref = refs[:3]
        dx_ref, dg_ref = refs[-2:]
        xv, dyv = x_ref[...], dy_ref[...]
        r = lax.rsqrt(jnp.mean(xv * xv, axis=1, keepdims=True) + EPS)
        u = dyv * g_ref[...]
        dx = r * u - xv * (r * r * r) * jnp.mean(u * xv, axis=1, keepdims=True)
        if has_res:
            dx = dx + refs[3][...]
        dx_ref[...] = dx

        @pl.when(pl.program_id(0) == 0)
        def _():
            dg_ref[...] = jnp.zeros_like(dg_ref)

        dg_ref[...] += jnp.sum(dyv * xv * r, axis=0, keepdims=True)

    row = pl.BlockSpec((tm, d), lambda i: (i, 0))
    vec = pl.BlockSpec((1, d), lambda i: (0, 0))
    return pl.pallas_call(
        body, name=name, grid=(t // tm,),
        in_specs=[row, vec, row] + ([row] if has_res else []),
        out_specs=(row, vec),
        out_shape=(jax.ShapeDtypeStruct((t, d), F32), jax.ShapeDtypeStruct((1, d), F32)),
        compiler_params=_params(("arbitrary",)),
    )(*([x, g, dy] + ([dres] if has_res else [])))


def _hg_constants():
    c = HG_CHUNK
    t = np.arange(c)
    sums = [t[None, :] <= t[:, None]]
    masks = []
    for m in HG_LEVELS:
        base = (t // (2 * m)) * (2 * m)
        mid = base + m - 1
        second = (t - base) >= m
        upper = (t[None, :] > mid[:, None]) & (t[None, :] <= t[:, None])
        lower = (t[None, :] > t[:, None]) & (t[None, :] <= mid[:, None])
        sums.append(np.where(second[:, None], upper, lower))
        masks.append(second[:, None] & (~second)[None, :] & (base[:, None] == base[None, :]))
    return (np.concatenate(sums, axis=0).astype(np.float32), np.stack(masks).astype(np.float32))


def _hg_gates(blk, lbp):
    q, x, v, gl = blk[:, 0:128], blk[:, 128:256], blk[:, 256:384], blk[:, 384:512]
    mx = jnp.max(lbp, axis=0, keepdims=True)
    e = jnp.exp(lbp - mx)
    lb = e[0:1, :] / jnp.sum(e, axis=0, keepdims=True)
    sig = jax.nn.sigmoid(x)
    f = lb + (1.0 - lb) * sig
    return q, v, gl, lb, sig, f, 1.0 - f, jnp.log(f)


def _hg_fwd(proj, lbp, ng, bsz, seq, *, y_width):
    t = proj.shape[0]
    nc = seq // HG_CHUNK
    a_np, m_np = _hg_constants()
    a_all = jnp.asarray(a_np, _MXU_DTYPE)
    masks = jnp.asarray(m_np, F32)
    nl = len(HG_LEVELS)

    def body(p_ref, lb_ref, ng_ref, a_ref, m_ref, y_ref, o_ref, st_ref):
        a_mat = a_ref[...]
        ngv = ng_ref[...]
        lbp_v = lb_ref[...]

        def chunk(c, st):
            rows = pl.ds(pl.multiple_of(c * HG_CHUNK, HG_CHUNK), HG_CHUNK)
            q, v, gl, lb, sig, f, k, g = _hg_gates(p_ref[rows, :], lbp_v)
            e_all = _split_dot(a_mat, g, NN, 3)
            b = e_all[0:HG_CHUNK]
            st_ref[c] = st
            o = _dot(q * jnp.exp(b), st, NT)
            p = jnp.zeros((HG_CHUNK, HG_CHUNK), F32)
            for li in range(nl):
                e = jnp.exp(e_all[HG_CHUNK * (li + 1):HG_CHUNK * (li + 2)])
                p = p + m_ref[li] * _dot(q * e, k * e, NT)
            o = o + _dot(p, v) + jnp.sum(q * k, axis=1, keepdims=True) * v
            bl = b[HG_CHUNK - 1:HG_CHUNK, :]
            st_new = st * jnp.exp(bl) + _dot(v, k * jnp.exp(bl - b), TN)
            r = lax.rsqrt(jnp.mean(o * o, axis=1, keepdims=True) + EPS)
            o_ref[rows, :] = o
            y_ref[rows, :] = (o * r * ngv) * (gl * jax.nn.sigmoid(gl))
            return st_new

        lax.fori_loop(0, nc, chunk, jnp.zeros((HG_D, HG_D), F32))

    return pl.pallas_call(
        body, name="hgrn2_fwd", grid=(bsz, HG_HEADS),
        in_specs=[pl.BlockSpec((seq, 512), lambda b, h: (b, h)),
                  pl.BlockSpec((2, HG_D), lambda b, h: (0, h)),
                  pl.BlockSpec((1, HG_D), lambda b, h: (0, 0)),
                  pl.BlockSpec(a_all.shape, lambda b, h: (0, 0)),
                  pl.BlockSpec(masks.shape, lambda b, h: (0, 0, 0))],
        out_specs=(pl.BlockSpec((seq, HG_D), lambda b, h: (b, h)),
                   pl.BlockSpec((seq, HG_D), lambda b, h: (b, h)),
                   pl.BlockSpec((None, None, nc, HG_D, HG_D), lambda b, h: (b, h, 0, 0, 0))),
        out_shape=(jax.ShapeDtypeStruct((t, y_width), F32),
                   jax.ShapeDtypeStruct((t, HG_HEADS * HG_D), F32),
                   jax.ShapeDtypeStruct((bsz, HG_HEADS, nc, HG_D, HG_D), F32)),
        compiler_params=_params(("parallel", "parallel")),
    )(proj, lbp, ng, a_all, masks)


def _hg_bwd(proj, lbp, ng, o_all, states, dy, bsz, seq):
    t = proj.shape[0]
    nc = seq // HG_CHUNK
    a_np, m_np = _hg_constants()
    a_all = jnp.asarray(a_np, _MXU_DTYPE)
    masks = jnp.asarray(m_np, F32)
    nl = len(HG_LEVELS)
    cs = HG_CHUNK

    def body(p_ref, lb_ref, ng_ref, a_ref, m_ref, o_ref, st_ref, dy_ref, dp_ref, dlb_ref, dng_ref):
        a_mat = a_ref[...]
        ngv = ng_ref[...]
        lbp_v = lb_ref[...]
        last_row = lax.broadcasted_iota(jnp.int32, (cs, HG_D), 0) == cs - 1

        def chunk(i, carry):
            dst, dlb_acc, dng_acc = carry
            c = nc - 1 - i
            rows = pl.ds(pl.multiple_of(c * cs, cs), cs)
            q, v, gl, lb, sig, f, k, g = _hg_gates(p_ref[rows, :], lbp_v)
            o = o_ref[rows, :]
            dyv = dy_ref[rows, :]
            st = st_ref[c]
            e_all = _split_dot(a_mat, g, NN, 3)
            b = e_all[0:cs]
            eb = jnp.exp(b)
            bl = b[cs - 1:cs, :]
            ebl = jnp.exp(bl)
            ekd = jnp.exp(bl - b)
            qb, kd = q * eb, k * ekd
            sg = jax.nn.sigmoid(gl)
            r = lax.rsqrt(jnp.mean(o * o, axis=1, keepdims=True) + EPS)
            dgl = dyv * (o * r * ngv) * (sg * (1.0 + gl * (1.0 - sg)))
            u = dyv * (gl * sg) * ngv
            do = r * u - o * (r * r * r) * jnp.mean(u * o, axis=1, keepdims=True)
            dng_acc = dng_acc + jnp.sum(dyv * (gl * sg) * o * r, axis=0, keepdims=True)
            es, qm, km = [], [], []
            p = jnp.zeros((cs, cs), F32)
            for li in range(nl):
                e = jnp.exp(e_all[cs * (li + 1):cs * (li + 2)])
                es.append(e)
                qm.append(q * e)
                km.append(k * e)
                p = p + m_ref[li] * _dot(qm[li], km[li], NT)
            qk = jnp.sum(q * k, axis=1, keepdims=True)
            dp = _dot(do, v, NT)
            dv = _dot(p, do, TN) + qk * do + _dot(kd, dst, NT)
            dqb = _dot(do, st)
            dkd = _dot(v, dst)
            dq = dqb * eb
            dk = dkd * ekd
            db = dqb * qb - dkd * kd
            dbl = jnp.sum(dkd * kd, axis=0, keepdims=True) + jnp.sum(dst * st, axis=0, keepdims=True) * ebl
            de = [db + jnp.where(last_row, dbl, 0.0)]
            for li in range(nl):
                dpm = m_ref[li] * dp
                dqm = _dot(dpm, km[li])
                dkm = _dot(dpm, qm[li], TN)
                dq = dq + dqm * es[li]
                dk = dk + dkm * es[li]
                de.append(dqm * qm[li] + dkm * km[li])
            dpd = jnp.sum(do * v, axis=1, keepdims=True)
            dq = dq + dpd * k
            dk = dk + dpd * q
            dg = _split_dot(a_mat, jnp.concatenate(de, axis=0), TN, 2)
            df = dg / f - dk
            dx = df * (1.0 - lb) * sig * (1.0 - sig)
            dlb_acc = dlb_acc + jnp.sum(df * (1.0 - sig), axis=0, keepdims=True)
            dp_ref[rows, 0:128] = dq
            dp_ref[rows, 128:256] = dx
            dp_ref[rows, 256:384] = dv
            dp_ref[rows, 384:512] = dgl
            return dst * ebl + _dot(do, qb, TN), dlb_acc, dng_acc

        zrow = jnp.zeros((1, HG_D), F32)
        _, dlb, dng = lax.fori_loop(0, nc, chunk, (jnp.zeros((HG_D, HG_D), F32), zrow, zrow))
        mx = jnp.max(lbp_v, axis=0, keepdims=True)
        e = jnp.exp(lbp_v - mx)
        s0 = e[0:1, :] / jnp.sum(e, axis=0, keepdims=True)
        da0 = dlb * s0 * (1.0 - s0)
        bi = pl.program_id(1)
        first = jnp.logical_and(pl.program_id(0) == 0, bi == 0)

        @pl.when(bi == 0)
        def _():
            dlb_ref[...] = jnp.zeros_like(dlb_ref)

        @pl.when(first)
        def _():
            dng_ref[...] = jnp.zeros_like(dng_ref)

        dlb_ref[...] += jnp.concatenate([da0, -da0], axis=0)
        dng_ref[...] += dng

    return pl.pallas_call(
        body, name="hgrn2_bwd", grid=(HG_HEADS, bsz),
        in_specs=[pl.BlockSpec((seq, 512), lambda h, b: (b, h)),
                  pl.BlockSpec((2, HG_D), lambda h, b: (0, h)),
                  pl.BlockSpec((1, HG_D), lambda h, b: (0, 0)),
                  pl.BlockSpec(a_all.shape, lambda h, b: (0, 0)),
                  pl.BlockSpec(masks.shape, lambda h, b: (0, 0, 0)),
                  pl.BlockSpec((seq, HG_D), lambda h, b: (b, h)),
                  pl.BlockSpec((None, None, nc, HG_D, HG_D), lambda h, b: (b, h, 0, 0, 0)),
                  pl.BlockSpec((seq, HG_D), lambda h, b: (b, h))],
        out_specs=(pl.BlockSpec((seq, 512), lambda h, b: (b, h)),
                   pl.BlockSpec((2, HG_D), lambda h, b: (0, h)),
                   pl.BlockSpec((1, HG_D), lambda h, b: (0, 0))),
        out_shape=(jax.ShapeDtypeStruct((t, HG_COLS), F32),
                   jax.ShapeDtypeStruct((2, HG_HEADS * HG_D), F32),
                   jax.ShapeDtypeStruct((1, HG_D), F32)),
        compiler_params=_params(("arbitrary", "arbitrary")),
    )(proj, lbp, ng, a_all, masks, o_all, states, dy)


def _sw_constants():
    half = ROT_DIM // 2
    inv = (np.float32(ROPE_THETA) ** (-(np.arange(half, dtype=np.float32) * np.float32(2.0) / np.float32(ROT_DIM)))
           ).astype(np.float32)
    freq = np.zeros((1, 128), np.float32)
    sign = np.zeros((1, 128), np.float32)
    for h in range(2):
        freq[0, 64 * h:64 * h + half] = inv
        freq[0, 64 * h + half:64 * h + 2 * half] = inv
        sign[0, 64 * h:64 * h + half] = -1.0
        sign[0, 64 * h + half:64 * h + 2 * half] = 1.0
    seg = np.kron(np.eye(8, dtype=np.float32), np.full((64, 64), 1.0 / 64.0, np.float32))
    return freq, sign, seg


def _rope_tables(pos, freq, sign):
    ang = pos.astype(F32) * freq
    return jnp.cos(ang), jnp.sin(ang) * sign


def _tile_lanes(v, times):
    return v if times == 1 else jnp.concatenate([v] * times, axis=1)


def _swap_halves(v):
    w = v.shape[1]
    half = ROT_DIM // 2
    lane = lax.broadcasted_iota(jnp.int32, v.shape, 1) % SW_HD
    return jnp.where(lane < half, pltpu.roll(v, w - half, 1), jnp.where(lane < 2 * half, pltpu.roll(v, half, 1), 0.0))


def _sw_norm_rope(tv, gain, seg, cosv, sinv):
    w = tv.shape[1]
    ms = _split_dot_rhs(tv * tv, seg[0:w, 0:w])
    r = lax.rsqrt(ms + EPS)
    tn = tv * r * gain
    reps = w // 128
    return tn * _tile_lanes(cosv, reps) + _swap_halves(tn) * _tile_lanes(sinv, reps), r


def _split_dot_rhs(v, a):
    hi = _mx(v)
    lo = _mx(v - hi.astype(F32))
    return (lax.dot_general(hi, a, (NN, ((), ())), preferred_element_type=F32)
            + lax.dot_general(lo, a, (NN, ((), ())), preferred_element_type=F32))


def _sw_norm_rope_bwd(dt, tv, r, gain, seg, cosv, sinv):
    w = tv.shape[1]
    reps = w // 128
    dtn = dt * _tile_lanes(cosv, reps) + _swap_halves(dt * _tile_lanes(sinv, reps))
    u = dtn * gain
    dtv = r * u - tv * (r * r * r) * _split_dot_rhs(u * tv, seg[0:w, 0:w])
    return dtv, jnp.sum(dtn * tv * r, axis=0, keepdims=True)


def _sw_probs(qh, kp, kc, sink, first_block):
    scale = SW_HD ** -0.5
    qi = lax.broadcasted_iota(jnp.int32, (SW_BLOCK, SW_BLOCK), 0)
    kj = lax.broadcasted_iota(jnp.int32, (SW_BLOCK, SW_BLOCK), 1)
    ok_prev = jnp.logical_and(kj > qi, jnp.logical_not(first_block))
    ok_cur = kj <= qi
    sp = jnp.where(ok_prev, _dot(qh, kp, NT) * scale, -jnp.inf)
    sc = jnp.where(ok_cur, _dot(qh, kc, NT) * scale, -jnp.inf)
    m = jnp.maximum(jnp.maximum(jnp.max(sp, axis=1, keepdims=True), jnp.max(sc, axis=1, keepdims=True)), sink)
    pp, pc = jnp.exp(sp - m), jnp.exp(sc - m)
    es = jnp.exp(sink - m)
    den = jnp.sum(pp, axis=1, keepdims=True) + jnp.sum(pc, axis=1, keepdims=True) + es
    return pp / den, pc / den, es / den


def _sw_specs(nb):
    def cur(b, n):
        return b * nb + jnp.minimum(n, nb - 1)

    def prev(b, n):
        return b * nb + jnp.maximum(jnp.minimum(n, nb - 1) - 1, 0)

    return cur, prev


def _sw_fwd(proj, pos, qg, kg, sinks, y_in, bsz, seq):
    t = proj.shape[0]
    nb = seq // SW_BLOCK
    freq_np, sign_np, seg_np = _sw_constants()
    freq, sign = jnp.asarray(freq_np), jnp.asarray(sign_np)
    seg = jnp.asarray(seg_np, _MXU_DTYPE)
    cur, prev = _sw_specs(nb)

    def body(q_ref, kc_ref, kp_ref, vc_ref, vp_ref, pc_ref, pp_ref, qg_ref, kg_ref, sk_ref, fr_ref, sn_ref, seg_ref,
             yin_ref, y_ref):
        del yin_ref
        n = pl.program_id(1)
        segv = seg_ref[...]
        cos_c, sin_c = _rope_tables(pc_ref[...], fr_ref[...], sn_ref[...])
        cos_p, sin_p = _rope_tables(pp_ref[...], fr_ref[...], sn_ref[...])
        qr, _ = _sw_norm_rope(q_ref[...], qg_ref[...], segv, cos_c, sin_c)
        kcr, _ = _sw_norm_rope(kc_ref[...], kg_ref[...], segv, cos_c, sin_c)
        kpr, _ = _sw_norm_rope(kp_ref[...], kg_ref[...], segv, cos_p, sin_p)
        vc, vp = vc_ref[...], vp_ref[...]
        for h in range(SW_HEADS):
            kv = h // SW_GROUP
            ks = slice(SW_HD * kv, SW_HD * (kv + 1))
            pp, pc, _ = _sw_probs(qr[:, SW_HD * h:SW_HD * (h + 1)], kpr[:, ks], kcr[:, ks], sk_ref[0, h], n == 0)
            y_ref[:, SW_HD * h:SW_HD * (h + 1)] = _dot(pp, vp[:, ks]) + _dot(pc, vc[:, ks])

    rowq = pl.BlockSpec((SW_BLOCK, 512), lambda b, n: (cur(b, n), 0))
    full = lambda a: pl.BlockSpec(a.shape, lambda b, n: (0,) * a.ndim)
    yw = y_in.shape[1]
    return pl.pallas_call(
        body, name="swa_fwd", grid=(bsz, nb),
        in_specs=[rowq,
                  pl.BlockSpec((SW_BLOCK, 128), lambda b, n: (cur(b, n), 4)),
                  pl.BlockSpec((SW_BLOCK, 128), lambda b, n: (prev(b, n), 4)),
                  pl.BlockSpec((SW_BLOCK, 128), lambda b, n: (cur(b, n), 5)),
                  pl.BlockSpec((SW_BLOCK, 128), lambda b, n: (prev(b, n), 5)),
                  pl.BlockSpec((SW_BLOCK, 1), lambda b, n: (cur(b, n), 0)),
                  pl.BlockSpec((SW_BLOCK, 1), lambda b, n: (prev(b, n), 0)),
                  full(qg), full(kg),
                  pl.BlockSpec(memory_space=pltpu.SMEM),
                  full(freq), full(sign), full(seg),
                  pl.BlockSpec(memory_space=pl.ANY)],
        out_specs=pl.BlockSpec((SW_BLOCK, 512), lambda b, n: (cur(b, n), 1)),
        out_shape=jax.ShapeDtypeStruct((t, yw), F32),
        input_output_aliases={13: 0},
        compiler_params=_params(("parallel", "parallel")),
    )(proj, proj, proj, proj, proj, pos, pos, qg, kg, sinks, freq, sign, seg, y_in)


def _sw_bwd(proj, pos, qg, kg, sinks, y, dy, bsz, seq):
    t = proj.shape[0]
    nb = seq // SW_BLOCK
    freq_np, sign_np, seg_np = _sw_constants()
    freq, sign = jnp.asarray(freq_np), jnp.asarray(sign_np)
    seg = jnp.asarray(seg_np, _MXU_DTYPE)
    cur, prev = _sw_specs(nb)
    scale = SW_HD ** -0.5

    def body(q_ref, kc_ref, kp_ref, vc_ref, vp_ref, pc_ref, pp_ref, qg_ref, kg_ref, sk_ref, fr_ref, sn_ref, seg_ref,
             y_ref, dy_ref, dp_ref, dqg_ref, dkg_ref, dsk_ref,
             dq_car, dkv_car, dqr_s, dkc_s, dkp_s, dvc_s, dvp_s, gq_acc, gk_acc, sk_acc):
        b, n = pl.program_id(0), pl.program_id(1)
        first = jnp.logical_and(b == 0, n == 0)
        last = jnp.logical_and(b == pl.num_programs(0) - 1, n == nb)

        @pl.when(first)
        def _():
            gq_acc[...] = jnp.zeros_like(gq_acc)
            gk_acc[...] = jnp.zeros_like(gk_acc)
            sk_acc[...] = jnp.zeros_like(sk_acc)

        @pl.when(n < nb)
        def _():
            segv = seg_ref[...]
            cos_c, sin_c = _rope_tables(pc_ref[...], fr_ref[...], sn_ref[...])
            cos_p, sin_p = _rope_tables(pp_ref[...], fr_ref[...], sn_ref[...])
            qv, kcv, kpv = q_ref[...], kc_ref[...], kp_ref[...]
            qr, rq = _sw_norm_rope(qv, qg_ref[...], segv, cos_c, sin_c)
            kcr, rkc = _sw_norm_rope(kcv, kg_ref[...], segv, cos_c, sin_c)
            kpr, rkp = _sw_norm_rope(kpv, kg_ref[...], segv, cos_p, sin_p)
            vc, vp = vc_ref[...], vp_ref[...]
            dkc_s[...] = jnp.zeros_like(dkc_s)
            dkp_s[...] = jnp.zeros_like(dkp_s)
            dvc_s[...] = jnp.zeros_like(dvc_s)
            dvp_s[...] = jnp.zeros_like(dvp_s)
            lane = lax.broadcasted_iota(jnp.int32, (1, 128), 1)
            dsk = jnp.zeros((1, 128), F32)
            for h in range(SW_HEADS):
                kv = h // SW_GROUP
                ks = slice(SW_HD * kv, SW_HD * (kv + 1))
                hs = slice(SW_HD * h, SW_HD * (h + 1))
                qh = qr[:, hs]
                pp, pc, ps = _sw_probs(qh, kpr[:, ks], kcr[:, ks], sk_ref[0, h], n == 0)
                doh = dy_ref[:, hs]
                delta = jnp.sum(doh * y_ref[:, hs], axis=1, keepdims=True)
                dsp = pp * (_dot(doh, vp[:, ks], NT) - delta) * scale
                dsc = pc * (_dot(doh, vc[:, ks], NT) - delta) * scale
                dsk = dsk + jnp.where(lane == h, -jnp.sum(ps * delta), 0.0)
                dvp_s[:, ks] += _dot(pp, doh, TN)
                dvc_s[:, ks] += _dot(pc, doh, TN)
                dqr_s[:, hs] = _dot(dsp, kpr[:, ks]) + _dot(dsc, kcr[:, ks])
                dkp_s[:, ks] += _dot(dsp, qh, TN)
                dkc_s[:, ks] += _dot(dsc, qh, TN)
            dq, gq = _sw_norm_rope_bwd(dqr_s[...], qv, rq, qg_ref[...], segv, cos_c, sin_c)
            dkc, gkc = _sw_norm_rope_bwd(dkc_s[...], kcv, rkc, kg_ref[...], segv, cos_c, sin_c)
            dkp, gkp = _sw_norm_rope_bwd(dkp_s[...], kpv, rkp, kg_ref[...], segv, cos_p, sin_p)
            gq_acc[...] += gq
            gk_acc[...] += gkc + gkp
            sk_acc[...] += dsk

            @pl.when(n > 0)
            def _():
                dp_ref[:, 0:512] = dq_car[...]
                dp_ref[:, 512:640] = dkv_car[:, 0:128] + dkp
                dp_ref[:, 640:768] = dkv_car[:, 128:256] + dvp_s[...]

            dq_car[...] = dq
            dkv_car[:, 0:128] = dkc
            dkv_car[:, 128:256] = dvc_s[...]

        @pl.when(n == nb)
        def _():
            dp_ref[:, 0:512] = dq_car[...]
            dp_ref[:, 512:768] = dkv_car[...]

        @pl.when(last)
        def _():
            gq = gq_acc[...]
            acc = gq[:, 0:SW_HD]
            for h in range(1, SW_HEADS):
                acc = acc + gq[:, SW_HD * h:SW_HD * (h + 1)]
            dqg_ref[...] = acc
            gk = gk_acc[...]
            dkg_ref[...] = gk[:, 0:SW_HD] + gk[:, SW_HD:2 * SW_HD]
            dsk_ref[...] = sk_acc[...]

    rowq = pl.BlockSpec((SW_BLOCK, 512), lambda b, n: (cur(b, n), 0))
    full = lambda a: pl.BlockSpec(a.shape, lambda b, n: (0,) * a.ndim)

    def out_row(b, n):
        return b * nb + jnp.maximum(n - 1, 0)

    return pl.pallas_call(
        body, name="swa_bwd", grid=(bsz, nb + 1),
        in_specs=[rowq,
                  pl.BlockSpec((SW_BLOCK, 128), lambda b, n: (cur(b, n), 4)),
                  pl.BlockSpec((SW_BLOCK, 128), lambda b, n: (prev(b, n), 4)),
                  pl.BlockSpec((SW_BLOCK, 128), lambda b, n: (cur(b, n), 5)),
                  pl.BlockSpec((SW_BLOCK, 128), lambda b, n: (prev(b, n), 5)),
                  pl.BlockSpec((SW_BLOCK, 1), lambda b, n: (cur(b, n), 0)),
                  pl.BlockSpec((SW_BLOCK, 1), lambda b, n: (prev(b, n), 0)),
                  full(qg), full(kg),
                  pl.BlockSpec(memory_space=pltpu.SMEM),
                  full(freq), full(sign), full(seg),
                  pl.BlockSpec((SW_BLOCK, 512), lambda b, n: (cur(b, n), 1)),
                  pl.BlockSpec((SW_BLOCK, 512), lambda b, n: (cur(b, n), 1))],
        out_specs=(pl.BlockSpec((SW_BLOCK, SW_COLS), lambda b, n: (out_row(b, n), 0)),
                   pl.BlockSpec((1, SW_HD), lambda b, n: (0, 0)),
                   pl.BlockSpec((1, SW_HD), lambda b, n: (0, 0)),
                   pl.BlockSpec((1, 128), lambda b, n: (0, 0))),
        out_shape=(jax.ShapeDtypeStruct((t, SW_COLS), F32),
                   jax.ShapeDtypeStruct((1, SW_HD), F32),
                   jax.ShapeDtypeStruct((1, SW_HD), F32),
                   jax.ShapeDtypeStruct((1, 128), F32)),
        scratch_shapes=[pltpu.VMEM((SW_BLOCK, 512), F32), pltpu.VMEM((SW_BLOCK, 256), F32),
                        pltpu.VMEM((SW_BLOCK, 512), F32),
                        pltpu.VMEM((SW_BLOCK, 128), F32), pltpu.VMEM((SW_BLOCK, 128), F32),
                        pltpu.VMEM((SW_BLOCK, 128), F32), pltpu.VMEM((SW_BLOCK, 128), F32),
                        pltpu.VMEM((1, 512), F32), pltpu.VMEM((1, 128), F32), pltpu.VMEM((1, 128), F32)],
        compiler_params=_params(("arbitrary", "arbitrary")),
    )(proj, proj, proj, proj, proj, pos, pos, qg, kg, sinks, freq, sign, seg, y, dy)


def _head_rms(tv, gain):
    r = lax.rsqrt(jnp.mean(tv * tv, axis=1, keepdims=True) + EPS)
    return tv * r * gain, r


def _head_rms_bwd(dtn, tv, r, gain):
    u = dtn * gain
    return r * u - tv * (r * r * r) * jnp.mean(u * tv, axis=1, keepdims=True), jnp.sum(dtn * tv * r, axis=0, keepdims=True)


def _xa_probs(qn, kn):
    s = _dot(qn, kn, NT) * (XA_HD ** -0.5)
    e = jnp.exp(s - jnp.max(s, axis=1, keepdims=True))
    return e / jnp.sum(e, axis=1, keepdims=True)


def _xa_fwd(qx, kvx, qg, kg, bsz, seq, mlen, *, tq=512):
    t = qx.shape[0]
    tq = min(tq, seq)
    nq = seq // tq
    w = XA_HEADS * XA_HD

    def body(q_ref, kv_ref, qg_ref, kg_ref, o_ref):
        for h in range(XA_HEADS):
            hs = slice(XA_HD * h, XA_HD * (h + 1))
            qn, _ = _head_rms(q_ref[:, hs], qg_ref[...])
            kn, _ = _head_rms(kv_ref[:, hs], kg_ref[...])
            o_ref[:, hs] = _dot(_xa_probs(qn, kn), kv_ref[:, w + XA_HD * h:w + XA_HD * (h + 1)])

    vec = pl.BlockSpec((1, XA_HD), lambda b, i: (0, 0))
    return pl.pallas_call(
        body, name="xattn_fwd", grid=(bsz, nq),
        in_specs=[pl.BlockSpec((tq, w), lambda b, i: (b * nq + i, 0)),
                  pl.BlockSpec((mlen, 2 * w), lambda b, i: (b, 0)), vec, vec],
        out_specs=pl.BlockSpec((tq, w), lambda b, i: (b * nq + i, 0)),
        out_shape=jax.ShapeDtypeStruct((t, w), F32),
        compiler_params=_params(("parallel", "parallel")),
    )(qx, kvx, qg, kg)


def _xa_bwd(qx, kvx, qg, kg, do, bsz, seq, mlen, *, tq=512):
    t = qx.shape[0]
    tq = min(tq, seq)
    nq = seq // tq
    w = XA_HEADS * XA_HD
    scale = XA_HD ** -0.5

    def body(q_ref, kv_ref, qg_ref, kg_ref, do_ref, dq_ref, dkv_ref, dqg_ref, dkg_ref):
        b, i = pl.program_id(0), pl.program_id(1)

        @pl.when(jnp.logical_and(b == 0, i == 0))
        def _():
            dqg_ref[...] = jnp.zeros_like(dqg_ref)
            dkg_ref[...] = jnp.zeros_like(dkg_ref)

        @pl.when(i == 0)
        def _():
            dkv_ref[...] = jnp.zeros_like(dkv_ref)

        gq_sum = jnp.zeros((1, XA_HD), F32)
        gk_sum = jnp.zeros((1, XA_HD), F32)
        for h in range(XA_HEADS):
            hs = slice(XA_HD * h, XA_HD * (h + 1))
            vs = slice(w + XA_HD * h, w + XA_HD * (h + 1))
            qv, kv, vv = q_ref[:, hs], kv_ref[:, hs], kv_ref[:, vs]
            qn, rq = _head_rms(qv, qg_ref[...])
            kn, rk = _head_rms(kv, kg_ref[...])
            p = _xa_probs(qn, kn)
            doh = do_ref[:, hs]
            dp = _dot(doh, vv, NT)
            ds = p * (dp - jnp.sum(p * dp, axis=1, keepdims=True)) * scale
            dqv, gq = _head_rms_bwd(_dot(ds, kn), qv, rq, qg_ref[...])
            dkv, gk = _head_rms_bwd(_dot(ds, qn, TN), kv, rk, kg_ref[...])
            dq_ref[:, hs] = dqv
            dkv_ref[:, hs] += dkv
            dkv_ref[:, vs] += _dot(p, doh, TN)
            gq_sum = gq_sum + gq
            gk_sum = gk_sum + gk
        dqg_ref[...] += gq_sum
        dkg_ref[...] += gk_sum

    vec = pl.BlockSpec((1, XA_HD), lambda b, i: (0, 0))
    row = pl.BlockSpec((tq, w), lambda b, i: (b * nq + i, 0))
    mem = pl.BlockSpec((mlen, 2 * w), lambda b, i: (b, 0))
    return pl.pallas_call(
        body, name="xattn_bwd", grid=(bsz, nq),
        in_specs=[row, mem, vec, vec, row],
        out_specs=(row, mem, vec, vec),
        out_shape=(jax.ShapeDtypeStruct((t, w), F32), jax.ShapeDtypeStruct((bsz * mlen, 2 * w), F32),
                   jax.ShapeDtypeStruct((1, XA_HD), F32), jax.ShapeDtypeStruct((1, XA_HD), F32)),
        compiler_params=_params(("arbitrary", "arbitrary")),
    )(qx, kvx, qg, kg, do)


def _loss_sum(dy, d_model, *, tm=512):
    t, d = dy.shape
    tm = min(tm, t)
    steps = t // tm

    def body(dy_ref, o_ref, acc_ref):
        i = pl.program_id(0)

        @pl.when(i == 0)
        def _():
            acc_ref[...] = jnp.zeros_like(acc_ref)

        diff = dy_ref[...] * float(d_model)
        acc_ref[...] += jnp.sum(diff * diff, axis=0, keepdims=True)

        @pl.when(i == steps - 1)
        def _():
            o_ref[...] = jnp.zeros_like(o_ref) + 0.5 * jnp.sum(acc_ref[...]) / float(d_model)

    return pl.pallas_call(
        body, name="loss_sum", grid=(steps,),
        in_specs=[pl.BlockSpec((tm, d), lambda i: (i, 0))],
        out_specs=pl.BlockSpec((1, 128), lambda i: (0, 0)),
        out_shape=jax.ShapeDtypeStruct((1, 128), F32),
        scratch_shapes=[pltpu.VMEM((1, d), F32)],
        compiler_params=_params(("arbitrary",)),
    )(dy)


def _adamw_math(w, g, m, v):
    m = ADAM_B1 * m + (1.0 - ADAM_B1) * g
    v = ADAM_B2 * v + (1.0 - ADAM_B2) * (g * g)
    m_hat = m / (1.0 - ADAM_B1 ** ADAM_STEP)
    v_hat = v / (1.0 - ADAM_B2 ** ADAM_STEP)
    return -ADAM_LR * (m_hat / (jnp.sqrt(v_hat) + ADAM_EPS) + ADAM_WD * w), m, v


def _adamw_big(w, g, m, v, *, name, tr=256):
    r, c = w.shape
    tr = min(tr, r)

    def body(w_ref, g_ref, m_ref, v_ref, d_ref, mo_ref, vo_ref):
        d, mn, vn = _adamw_math(w_ref[...], g_ref[...], m_ref[...], v_ref[...])
        d_ref[...] = d
        mo_ref[...] = mn
        vo_ref[...] = vn

    spec = pl.BlockSpec((tr, c), lambda i: (i, 0))
    shp = jax.ShapeDtypeStruct((r, c), F32)
    return pl.pallas_call(
        body, name=name, grid=(r // tr,), in_specs=[spec] * 4, out_specs=(spec,) * 3, out_shape=(shp,) * 3,
        compiler_params=_params(("parallel",)),
    )(w, g, m, v)


def _adamw_small(ws, gs, ms, vs):
    n = len(ws)

    def body(*refs):
        for i in range(n):
            d, mn, vn = _adamw_math(refs[i][...], refs[n + i][...], refs[2 * n + i][...], refs[3 * n + i][...])
            refs[4 * n + i][...] = d
            refs[5 * n + i][...] = mn
            refs[6 * n + i][...] = vn

    shapes = tuple(jax.ShapeDtypeStruct(w.shape, F32) for w in ws)
    return pl.pallas_call(body, name="adamw_small", out_shape=shapes * 3)(*ws, *gs, *ms, *vs)


def _add_halves(g, recv, c_idx, *, name, tr=256):
    _, r, c = g.shape
    h = r // 2
    tr = min(tr, h)
    nt = h // tr

    def body(c_ref, g_ref, r_ref, o_ref):
        del c_ref
        o_ref[...] = g_ref[...] + r_ref[...]

    return pl.pallas_call(
        body, name=name,
        grid_spec=pltpu.PrefetchScalarGridSpec(
            num_scalar_prefetch=1, grid=(4, nt),
            in_specs=[pl.BlockSpec((None, tr, c), lambda k, i, cr: (k, cr[0] * nt + i, 0)),
                      pl.BlockSpec((None, tr, c), lambda k, i, cr: (k, i, 0))],
            out_specs=pl.BlockSpec((None, tr, c), lambda k, i, cr: (k, i, 0))),
        out_shape=jax.ShapeDtypeStruct((4, h, c), F32),
        compiler_params=_params(("parallel", "parallel")),
    )(c_idx, g, recv)


def _add_chips(p, recv, me_idx, *, name, tr=256):
    _, h, c = p.shape
    tr = min(tr, h)

    def body(me_ref, p_ref, r_ref, o_ref):
        del me_ref
        o_ref[...] = ((p_ref[...] + r_ref[0]) + r_ref[1]) + r_ref[2]

    return pl.pallas_call(
        body, name=name,
        grid_spec=pltpu.PrefetchScalarGridSpec(
            num_scalar_prefetch=1, grid=(h // tr,),
            in_specs=[pl.BlockSpec((None, tr, c), lambda i, mr: (mr[0], i, 0)),
                      pl.BlockSpec((3, tr, c), lambda i, mr: (0, i, 0))],
            out_specs=pl.BlockSpec((tr, c), lambda i, mr: (i, 0))),
        out_shape=jax.ShapeDtypeStruct((h, c), F32),
        compiler_params=_params(("parallel",)),
    )(me_idx, p, recv)


def _place():
    x, y, c = lax.axis_index("x"), lax.axis_index("y"), lax.axis_index("c")
    chips = [(1 - x, y), (x, 1 - y), (1 - x, 1 - y)]
    return x, y, c, chips


ANY = pl.BlockSpec(memory_space=pl.ANY)


def _all_gather_weights(shards):
    n = len(shards)

    def body(*refs):
        ins, outs = refs[:n], refs[n:2 * n]
        send_sems, recv_sems, local_sems = refs[2 * n:]
        x, y, c, chips = _place()
        me = 2 * x + y

        def half(a, chip_idx, which):
            h = ins[a].shape[0] // 2
            return outs[a].at[chip_idx, pl.ds(which * h, h), :]

        def copy(a, j, chip_idx, which, to, src=None):
            return pltpu.make_async_remote_copy(
                src_ref=half(a, chip_idx, which) if src is None else src, dst_ref=half(a, chip_idx, which),
                send_sem=send_sems.at[a * 6 + j], recv_sem=recv_sems.at[a * 6 + j], device_id=to, device_id_type=MESH)

        for a in range(n):
            h = ins[a].shape[0] // 2
            pltpu.make_async_copy(ins[a], outs[a].at[me], local_sems.at[a]).start()
            for j, (px, py) in enumerate(chips):
                copy(a, j, me, c, (px, py, c), src=ins[a].at[pl.ds(c * h, h), :]).start()
        for a in range(n):
            for j, (px, py) in enumerate(chips):
                copy(a, j, 2 * px + py, c, (x, y, c)).wait_recv()
                copy(a, 3 + j, 2 * px + py, c, (x, y, 1 - c)).start()
        for a in range(n):
            for j, (px, py) in enumerate(chips):
                copy(a, 3 + j, 2 * px + py, 1 - c, (x, y, c)).wait_recv()
        for a in range(n):
            pltpu.make_async_copy(ins[a], outs[a].at[me], local_sems.at[a]).wait()
            h = ins[a].shape[0] // 2
            for j, (px, py) in enumerate(chips):
                copy(a, j, me, c, (px, py, c), src=ins[a].at[pl.ds(c * h, h), :]).wait_send()
                copy(a, 3 + j, 2 * px + py, c, (x, y, 1 - c)).wait_send()

    return pl.pallas_call(
        body, name="all_gather_weights",
        in_specs=[ANY] * n, out_specs=tuple([ANY] * n),
        out_shape=tuple(jax.ShapeDtypeStruct((4,) + s.shape, s.dtype) for s in shards),
        scratch_shapes=[pltpu.SemaphoreType.DMA((6 * n,)), pltpu.SemaphoreType.DMA((6 * n,)),
                        pltpu.SemaphoreType.DMA((n,))],
    )(*shards)


def _exchange_halves(grads):
    n = len(grads)

    def body(*refs):
        ins, outs = refs[:n], refs[n:2 * n]
        send_sems, recv_sems = refs[2 * n:]
        x, y, c, _ = _place()

        def copy(a):
            h = ins[a].shape[1] // 2
            return pltpu.make_async_remote_copy(
                src_ref=ins[a].at[:, pl.ds((1 - c) * h, h), :], dst_ref=outs[a],
                send_sem=send_sems.at[a], recv_sem=recv_sems.at[a], device_id=(x, y, 1 - c), device_id_type=MESH)

        for a in range(n):
            copy(a).start()
        for a in range(n):
            copy(a).wait_recv()
        for a in range(n):
            copy(a).wait_send()

    return pl.pallas_call(
        body, name="rs_exchange_halves",
        in_specs=[ANY] * n, out_specs=tuple([ANY] * n),
        out_shape=tuple(jax.ShapeDtypeStruct((4, g.shape[1] // 2, g.shape[2]), g.dtype) for g in grads),
        scratch_shapes=[pltpu.SemaphoreType.DMA((n,)), pltpu.SemaphoreType.DMA((n,))],
    )(*grads)


def _scatter_chips(parts):
    n = len(parts)

    def body(*refs):
        ins, outs = refs[:n], refs[n:2 * n]
        send_sems, recv_sems = refs[2 * n:]
        x, y, c, chips = _place()

        def copy(a, j, chip_idx, to):
            return pltpu.make_async_remote_copy(
                src_ref=ins[a].at[chip_idx], dst_ref=outs[a].at[j],
                send_sem=send_sems.at[a * 3 + j], recv_sem=recv_sems.at[a * 3 + j], device_id=to, device_id_type=MESH)

        for a in range(n):
            for j, (px, py) in enumerate(chips):
                copy(a, j, 2 * px + py, (px, py, c)).start()
        for a in range(n):
            for j, (px, py) in enumerate(chips):
                copy(a, j, 2 * px + py, (px, py, c)).wait_recv()
        for a in range(n):
            for j, (px, py) in enumerate(chips):
                copy(a, j, 2 * px + py, (px, py, c)).wait_send()

    return pl.pallas_call(
        body, name="rs_scatter_chips",
        in_specs=[ANY] * n, out_specs=tuple([ANY] * n),
        out_shape=tuple(jax.ShapeDtypeStruct((3,) + p.shape[1:], p.dtype) for p in parts),
        scratch_shapes=[pltpu.SemaphoreType.DMA((3 * n,)), pltpu.SemaphoreType.DMA((3 * n,))],
    )(*parts)


def _join_halves(halves):
    n = len(halves)

    def body(*refs):
        ins, outs = refs[:n], refs[n:2 * n]
        send_sems, recv_sems, local_sems = refs[2 * n:]
        x, y, c, _ = _place()

        def rows(a):
            h = ins[a].shape[0]
            return outs[a].at[pl.ds(c * h, h), :]

        def copy(a):
            return pltpu.make_async_remote_copy(
                src_ref=ins[a], dst_ref=rows(a), send_sem=send_sems.at[a], recv_sem=recv_sems.at[a],
                device_id=(x, y, 1 - c), device_id_type=MESH)

        for a in range(n):
            pltpu.make_async_copy(ins[a], rows(a), local_sems.at[a]).start()
            copy(a).start()
        for a in range(n):
            h = ins[a].shape[0]
            other = outs[a].at[pl.ds((1 - c) * h, h), :]
            pltpu.make_async_remote_copy(src_ref=ins[a], dst_ref=other, send_sem=send_sems.at[a],
                                         recv_sem=recv_sems.at[a], device_id=(x, y, 1 - c),
                                         device_id_type=MESH).wait_recv()
        for a in range(n):
            copy(a).wait_send()
            pltpu.make_async_copy(ins[a], rows(a), local_sems.at[a]).wait()

    return pl.pallas_call(
        body, name="rs_join_halves",
        in_specs=[ANY] * n, out_specs=tuple([ANY] * n),
        out_shape=tuple(jax.ShapeDtypeStruct((2 * p.shape[0], p.shape[1]), p.dtype) for p in halves),
        scratch_shapes=[pltpu.SemaphoreType.DMA((n,)), pltpu.SemaphoreType.DMA((n,)), pltpu.SemaphoreType.DMA((n,))],
    )(*halves)


def _all_reduce_small(sm):
    r, w = sm.shape

    def body(sm_ref, o_ref, buf, send_sems, recv_sems):
        x, y, c, _ = _place()
        me = 4 * x + 2 * y + c
        buf[me] = sm_ref[...]
        rel = [(dx, dy, dc) for dx in (0, 1) for dy in (0, 1) for dc in (0, 1)][1:]

        def copy(k, slot, to):
            return pltpu.make_async_remote_copy(
                src_ref=sm_ref, dst_ref=buf.at[slot], send_sem=send_sems.at[k], recv_sem=recv_sems.at[k],
                device_id=to, device_id_type=MESH)

        peers = []
        for k, (dx, dy, dc) in enumerate(rel):
            px = 1 - x if dx else x
            py = 1 - y if dy else y
            pc = 1 - c if dc else c
            peers.append((px, py, pc))
            copy(k, me, (px, py, pc)).start()
        for k, (px, py, pc) in enumerate(peers):
            copy(k, 4 * px + 2 * py + pc, (px, py, pc)).wait_recv()
        for k, (px, py, pc) in enumerate(peers):
            copy(k, me, (px, py, pc)).wait_send()
        acc = buf[0]
        for d in range(1, 8):
            acc = acc + buf[d]
        o_ref[...] = acc

    vm = pl.BlockSpec(memory_space=pltpu.VMEM)
    return pl.pallas_call(
        body, name="all_reduce_small", in_specs=[vm], out_specs=vm,
        out_shape=jax.ShapeDtypeStruct((r, w), F32),
        scratch_shapes=[pltpu.VMEM((8, r, w), F32), pltpu.SemaphoreType.DMA((7,)), pltpu.SemaphoreType.DMA((7,))],
    )(sm)


def _local_step(x3, mem3, pos2, target3, small, w):
    bsz, seq, d = x3.shape
    mlen = mem3.shape[1]
    t = bsz * seq
    ff = w["down"].shape[0]
    ffs = ff // 4
    ds = d // 4
    x = x3.reshape(t, d)
    mem = mem3.reshape(bsz * mlen, d)
    target = target3.reshape(t, d)
    pos = pos2.reshape(t, 1)
    qg_t = jnp.tile(small["sw_q_norm_g"], (1, SW_HEADS))
    kg_t = jnp.tile(small["sw_k_norm_g"], (1, SW_KV_HEADS))

    hn1 = _rms_fwd(x, small["norm1_g"], name="rms1_fwd")
    proj_hg = _mm(hn1, w["w_in_hg"], NN, t, HG_COLS, d, name="proj_hg", tk=d)[0]
    proj_sw = _mm(hn1, w["w_in_sw"], NN, t, SW_COLS, d, name="proj_sw", tk=d)[0]
    y_mix, o_hg, states = _hg_fwd(proj_hg, small["hg_lower_bounds"], small["hg_norm_g"], bsz, seq, y_width=1024)
    y_mix = _sw_fwd(proj_sw, pos, qg_t, kg_t, small["sw_sinks"], y_mix, bsz, seq)
    h1 = _mm(y_mix, w["w_out"], NN, t, d, 1024, name="out_proj", tk=1024, extras=(x,),
             epilogue=lambda acc, res: (acc + res,))[0]
    hn2 = _rms_fwd(h1, small["norm2_g"], name="rms2_fwd")
    mn = _rms_fwd(mem, small["mem_norm_g"], name="rms_mem_fwd")
    qx = _mm(hn2, w["wq"], NN, t, 512, d, name="xa_q", tk=d)[0]
    kvx = _mm(mn, w["wkv"], NN, bsz * mlen, 1024, d, name="xa_kv", tk=d)[0]
    ox = _xa_fwd(qx, kvx, small["xa_q_norm_g"], small["xa_k_norm_g"], bsz, seq, mlen)
    h2 = _mm(ox, w["wo"], NN, t, d, 512, name="xa_o", tn=ds, tk=512, extras=(h1,),
             b_spec=pl.BlockSpec((None, 512, ds), lambda i, j, kk: (j, 0, 0)),
             epilogue=lambda acc, res: (acc + res,))[0]
    hn3 = _rms_fwd(h2, small["norm3_g"], name="rms3_fwd")

    def relu_sq(acc):
        a = jnp.maximum(acc, 0.0)
        return a, a * a

    act, act2 = _mm(hn3, w["up"], NN, t, ff, d, name="mlp_up", tn=ffs, tk=d,
                    b_spec=pl.BlockSpec((None, d, ffs), lambda i, j, kk: (j, 0, 0)),
                    epilogue=relu_sq, out_dtypes=(_MXU_DTYPE, _MXU_DTYPE))
    inv_d = 1.0 / d
    dy = _mm(act2, w["down"], NN, t, d, ff, name="mlp_down", extras=(h2, target),
             epilogue=lambda acc, res, tgt: ((acc + res - tgt) * inv_d,))[0]
    loss_row = _loss_sum(dy, d)

    dz = _mm(dy, w["down"], NT, t, ff, d, name="d_act", tk=d, extras=(act,),
             epilogue=lambda acc, a: (acc * (2.0 * a.astype(F32)),), out_dtypes=(_MXU_DTYPE,))[0]
    g_down = _mm(act2, dy, TN, ff, d, t, name="g_down")[0]
    g_up = _mm(hn3, dz, TN, d, ff, t, name="g_up", tn=ffs,
               out_shape=(jax.ShapeDtypeStruct((4, d, ffs), F32),),
               out_spec=(pl.BlockSpec((None, min(1024, d), ffs), lambda i, j, kk: (j, i, 0)),))[0]
    dhn3 = _mm(dz, w["up"], NT, t, d, ff, name="d_hn3", tk=ffs,
               b_spec=pl.BlockSpec((None, min(1024, d), ffs), lambda i, j, kk: (kk, j, 0)))[0]
    dh2, g_norm3 = _rms_bwd(h2, small["norm3_g"], dhn3, dy, name="rms3_bwd")
    d_ox = _mm(dh2, w["wo"], NT, t, 512, d, name="d_ox", tk=ds,
               b_spec=pl.BlockSpec((None, 512, ds), lambda i, j, kk: (kk, 0, 0)))[0]
    g_wo = _mm(ox, dh2, TN, 512, d, t, name="g_wo", tn=ds,
               out_shape=(jax.ShapeDtypeStruct((4, 512, ds), F32),),
               out_spec=(pl.BlockSpec((None, 512, ds), lambda i, j, kk: (j, 0, 0)),))[0]
    d_qx, d_kvx, g_xq, g_xk = _xa_bwd(qx, kvx, small["xa_q_norm_g"], small["xa_k_norm_g"], d_ox, bsz, seq, mlen)
    g_wq = _mm(hn2, d_qx, TN, d, 512, t, name="g_wq")[0]
    g_wkv = _mm(mn, d_kvx, TN, d, 1024, bsz * mlen, name="g_wkv")[0]
    dhn2 = _mm(d_qx, w["wq"], NT, t, d, 512, name="d_hn2", tk=512)[0]
    dmn = _mm(d_kvx, w["wkv"], NT, bsz * mlen, d, 1024, name="d_mn", tk=1024)[0]
    dh1, g_norm2 = _rms_bwd(h1, small["norm2_g"], dhn2, dh2, name="rms2_bwd")
    _, g_memn = _rms_bwd(mem, small["mem_norm_g"], dmn, None, name="rms_mem_bwd")
    d_mix = _mm(dh1, w["w_out"], NT, t, 1024, d, name="d_mix", tk=d)[0]
    g_wout = _mm(y_mix, dh1, TN, 1024, d, t, name="g_wout")[0]
    dproj_sw, g_swq, g_swk, g_sinks = _sw_bwd(proj_sw, pos, qg_t, kg_t, small["sw_sinks"], y_mix, d_mix, bsz, seq)
    dproj_hg, g_lb, g_hgn = _hg_bwd(proj_hg, small["hg_lower_bounds"], small["hg_norm_g"], o_hg, states, d_mix, bsz, seq)
    g_in_hg = _mm(hn1, dproj_hg, TN, d, HG_COLS, t, name="g_in_hg")[0]
    g_in_sw = _mm(hn1, dproj_sw, TN, d, SW_COLS, t, name="g_in_sw")[0]
    dhn1_a = _mm(dproj_hg, w["w_in_hg"], NT, t, d, HG_COLS, name="d_hn1_hg", tk=1024)[0]
    dhn1 = _mm(dproj_sw, w["w_in_sw"], NT, t, d, SW_COLS, name="d_hn1_sw", tk=SW_COLS, extras=(dhn1_a,),
               epilogue=lambda acc, prev: (acc + prev,))[0]
    grad_x, g_norm1 = _rms_bwd(x, small["norm1_g"], dhn1, dh1, name="rms1_bwd")

    g_big = dict(w_in_hg=g_in_hg, w_in_sw=g_in_sw, w_out=g_wout, wq=g_wq, wkv=g_wkv, wo=g_wo, up=g_up, down=g_down)
    g_small = dict(norm1_g=g_norm1, hg_lower_bounds=g_lb, hg_norm_g=g_hgn, sw_q_norm_g=g_swq, sw_k_norm_g=g_swk,
                   sw_sinks=g_sinks[:, 0:SW_HEADS], norm2_g=g_norm2, mem_norm_g=g_memn, xa_q_norm_g=g_xq,
                   xa_k_norm_g=g_xk, norm3_g=g_norm3)
    return loss_row, grad_x.reshape(bsz, seq, d), g_big, g_small


SMALL_NAMES = ("norm1_g", "hg_lower_bounds", "hg_norm_g", "sw_q_norm_g", "sw_k_norm_g", "sw_sinks", "norm2_g",
               "mem_norm_g", "xa_q_norm_g", "xa_k_norm_g", "norm3_g")
BIG_NAMES = ("w_in", "w_out", "xa_wq", "xa_wkv", "xa_wo", "mlp_up", "mlp_down")
WEIGHT_ORDER = ("norm1_g", "w_in", "hg_lower_bounds", "hg_norm_g", "sw_q_norm_g", "sw_k_norm_g", "sw_sinks", "w_out",
                "norm2_g", "mem_norm_g", "xa_wq", "xa_wkv", "xa_q_norm_g", "xa_k_norm_g", "xa_wo", "norm3_g",
                "mlp_up", "mlp_down")


def _head_major_blocks():
    return [kind * HG_HEADS + h for h in range(HG_HEADS) for kind in range(4)]


def _permute_col_blocks(a, blocks):
    return jnp.concatenate([a[:, 128 * b:128 * (b + 1)] for b in blocks], axis=1)


def _pack_rows(vals, width):
    starts, at = [], 0
    for v in vals:
        starts.append(at)
        at += v.shape[0]
    total = at + (-at) % 8
    out = None
    for v, s in zip(vals, starts):
        placed = jnp.pad(v, ((s, total - s - v.shape[0]), (0, width - v.shape[1])))
        out = placed if out is None else out + placed
    return out, starts


def kernel(x, mem, positions, norm1_g, w_in, hg_lower_bounds, hg_norm_g, sw_q_norm_g, sw_k_norm_g, sw_sinks, w_out, norm2_g, mem_norm_g, xa_wq, xa_wkv, xa_q_norm_g, xa_k_norm_g, xa_wo, norm3_g, mlp_up, mlp_down, loss_target, m_norm1_g, m_w_in, m_hg_lower_bounds, m_hg_norm_g, m_sw_q_norm_g, m_sw_k_norm_g, m_sw_sinks, m_w_out, m_norm2_g, m_mem_norm_g, m_xa_wq, m_xa_wkv, m_xa_q_norm_g, m_xa_k_norm_g, m_xa_wo, m_norm3_g, m_mlp_up, m_mlp_down, v_norm1_g, v_w_in, v_hg_lower_bounds, v_hg_norm_g, v_sw_q_norm_g, v_sw_k_norm_g, v_sw_sinks, v_w_out, v_norm2_g, v_mem_norm_g, v_xa_wq, v_xa_wkv, v_xa_q_norm_g, v_xa_k_norm_g, v_xa_wo, v_norm3_g, v_mlp_up, v_mlp_down):
    given = dict(locals())
    weights = {n: given[n] for n in WEIGHT_ORDER}
    moms = {n: given["m_" + n] for n in WEIGHT_ORDER}
    vars_ = {n: given["v_" + n] for n in WEIGHT_ORDER}
    d = x.shape[-1]
    ff = mlp_down.shape[1] * 4
    small = {n: weights[n] for n in SMALL_NAMES}

    shards = [weights[n][0].astype(_MXU_DTYPE) for n in BIG_NAMES]
    g_in, g_out, g_q, g_kv, g_o, g_up, g_dn = _all_gather_weights(shards)
    w_in_full = jnp.concatenate([g_in[k] for k in range(4)], axis=1)
    whole = dict(w_in_hg=_permute_col_blocks(w_in_full, _head_major_blocks()), w_in_sw=w_in_full[:, HG_COLS:],
                 w_out=g_out.reshape(-1, d), wq=g_q.reshape(d, -1), wkv=g_kv.reshape(d, -1),
                 wo=g_o, up=g_up, down=g_dn.reshape(ff, d))

    loss_row, grad_x, g_big, g_small = _local_step(x, mem, positions, loss_target, small, whole)

    inv = [int(b) for b in np.argsort(_head_major_blocks())]
    g_in_full = jnp.concatenate([_permute_col_blocks(g_big["w_in_hg"], inv), g_big["w_in_sw"]], axis=1)
    ws = g_in_full.shape[1] // 4
    partial = [jnp.stack([g_in_full[:, ws * k:ws * (k + 1)] for k in range(4)]),
               g_big["w_out"].reshape(4, -1, d), g_big["wq"].reshape(4, d // 4, -1), g_big["wkv"].reshape(4, d // 4, -1),
               g_big["wo"], g_big["up"], g_big["down"].reshape(4, ff // 4, d)]
    c_idx = lax.axis_index("c").astype(jnp.int32).reshape(1)
    me_idx = (2 * lax.axis_index("x") + lax.axis_index("y")).astype(jnp.int32).reshape(1)
    from_sibling = _exchange_halves(partial)
    chip_part = [_add_halves(g, r, c_idx, name="rs_add_halves_%d" % i) for i, (g, r) in enumerate(zip(partial, from_sibling))]
    from_chips = _scatter_chips(chip_part)
    my_half = [_add_chips(p, r, me_idx, name="rs_add_chips_%d" % i) for i, (p, r) in enumerate(zip(chip_part, from_chips))]
    big_grads = dict(zip(BIG_NAMES, _join_halves(my_half)))

    packed, starts = _pack_rows([g_small[n] for n in SMALL_NAMES] + [loss_row], 1024)
    summed = _all_reduce_small(packed)
    small_grads = {}
    for n, s in zip(SMALL_NAMES, starts):
        r, c = weights[n].shape
        small_grads[n] = summed[s:s + r, 0:c]
    loss = summed[starts[-1], 0]

    grads, deltas, new_m, new_v = {}, {}, {}, {}
    for n in BIG_NAMES:
        shp = weights[n].shape
        g2 = big_grads[n]
        dl, mo, vo = _adamw_big(weights[n][0], g2, moms[n][0], vars_[n][0], name="adamw_" + n)
        grads[n], deltas[n], new_m[n], new_v[n] = (a.reshape(shp) for a in (g2, dl, mo, vo))
    sm_out = _adamw_small([weights[n] for n in SMALL_NAMES], [small_grads[n] for n in SMALL_NAMES],
                          [moms[n] for n in SMALL_NAMES], [vars_[n] for n in SMALL_NAMES])
    ns = len(SMALL_NAMES)
    for i, n in enumerate(SMALL_NAMES):
        grads[n], deltas[n], new_m[n], new_v[n] = small_grads[n], sm_out[i], sm_out[ns + i], sm_out[2 * ns + i]

    return (loss, grad_x, *[grads[n] for n in WEIGHT_ORDER], *[deltas[n] for n in WEIGHT_ORDER],
            *[new_m[n] for n in WEIGHT_ORDER], *[new_v[n] for n in WEIGHT_ORDER])
```

```python
import functools

import numpy as np
import jax
import jax.numpy as jnp
from jax import lax
from jax.experimental import pallas as pl
from jax.experimental.pallas import tpu as pltpu

F32 = jnp.float32
_MXU_DTYPE = jnp.bfloat16

EPS = 1e-6
HG_HEADS = 4
HG_D = 128
HG_CHUNK = 64
HG_LEVELS = (32, 16, 8, 4, 2, 1)
SW_HEADS = 8
SW_KV_HEADS = 2
SW_GROUP = SW_HEADS // SW_KV_HEADS
SW_HD = 64
SW_BLOCK = 128
ROPE_THETA = 500000.0
ROT_DIM = SW_HD // 4
XA_HEADS = 4
XA_HD = 128
HG_COLS = 4 * HG_HEADS * HG_D
SW_COLS = (SW_HEADS + 2 * SW_KV_HEADS) * SW_HD

ADAM_LR = 0.001
ADAM_B1 = 0.9
ADAM_B2 = 0.999
ADAM_EPS = 1e-08
ADAM_WD = 0.01
ADAM_STEP = 10

VMEM_LIMIT = 56 * 1024 * 1024
MESH = pl.DeviceIdType.MESH

NN = ((1,), (0,))
NT = ((1,), (1,))
TN = ((0,), (0,))


def _mx(v):
    return v.astype(_MXU_DTYPE)


def _dot(a, b, dims=NN):
    return lax.dot_general(_mx(a), _mx(b), (dims, ((), ())), preferred_element_type=F32)


def _split_dot(a, v, dims, parts):
    acc = None
    rest = v
    for p in range(parts):
        piece = _mx(rest)
        term = lax.dot_general(a, piece, (dims, ((), ())), preferred_element_type=F32)
        acc = term if acc is None else acc + term
        if p + 1 < parts:
            rest = rest - piece.astype(F32)
    return acc


def _params(sem):
    return pltpu.CompilerParams(dimension_semantics=sem, vmem_limit_bytes=VMEM_LIMIT)


def _mm(a, b, mode, m, n, k, *, name, tm=1024, tn=1024, tk=512, a_spec=None, b_spec=None, extras=(), epilogue=None,
        out_dtypes=(F32,), out_shape=None, out_spec=None, after=()):
    after = tuple(t for t in after if t is not None)
    tm, tn, tk = min(tm, m), min(tn, n), min(tk, k)
    assert m % tm == 0 and n % tn == 0 and k % tk == 0, (name, m, n, k, tm, tn, tk)
    gi, gj, gk = m // tm, n // tn, k // tk
    if a_spec is None:
        a_spec = (pl.BlockSpec((tk, tm), lambda i, j, kk: (kk, i)) if mode == TN
                  else pl.BlockSpec((tm, tk), lambda i, j, kk: (i, kk)))
    if b_spec is None:
        b_spec = (pl.BlockSpec((tn, tk), lambda i, j, kk: (j, kk)) if mode == NT
                  else pl.BlockSpec((tk, tn), lambda i, j, kk: (kk, j)))
    mn_spec = pl.BlockSpec((tm, tn), lambda i, j, kk: (i, j))
    if epilogue is None:
        epilogue = lambda acc: (acc,)
    n_ex, n_out = len(extras), len(out_dtypes)
    if out_shape is None:
        out_shape = tuple(jax.ShapeDtypeStruct((m, n), d) for d in out_dtypes)
        out_spec = tuple(mn_spec for _ in out_dtypes)

    n_after = len(after)

    def body(*refs):
        a_ref, b_ref = refs[0], refs[1]
        ex = refs[2:2 + n_ex]
        outs = refs[2 + n_ex + n_after:2 + n_ex + n_after + n_out]

        def finish(acc):
            res = epilogue(acc, *[e[...] for e in ex])
            for o, r in zip(outs, res):
                o[...] = r.astype(o.dtype)

        if gk == 1:
            finish(_dot(a_ref[...], b_ref[...], mode))
        else:
            acc_ref = refs[-1]
            kk = pl.program_id(2)

            @pl.when(kk == 0)
            def _():
                acc_ref[...] = jnp.zeros_like(acc_ref)

            acc_ref[...] += _dot(a_ref[...], b_ref[...], mode)

            @pl.when(kk == gk - 1)
            def _():
                finish(acc_ref[...])

    return pl.pallas_call(
        body, name=name, grid=(gi, gj, gk),
        in_specs=[a_spec, b_spec] + [mn_spec] * n_ex + [pl.BlockSpec(memory_space=pl.ANY)] * n_after,
        out_specs=out_spec, out_shape=out_shape,
        scratch_shapes=[pltpu.VMEM((tm, tn), F32)] if gk > 1 else [],
        compiler_params=_params(("parallel", "parallel", "arbitrary")),
    )(a, b, *extras, *after)


def _rms_fwd(x, g, *, name, tm=512):
    t, d = x.shape
    tm = min(tm, t)

    def body(x_ref, g_ref, o_ref):
        xv = x_ref[...]
        r = lax.rsqrt(jnp.mean(xv * xv, axis=1, keepdims=True) + EPS)
        o_ref[...] = (xv * r * g_ref[...]).astype(o_ref.dtype)

    return pl.pallas_call(
        body, name=name, grid=(t // tm,),
        in_specs=[pl.BlockSpec((tm, d), lambda i: (i, 0)), pl.BlockSpec((1, d), lambda i: (0, 0))],
        out_specs=pl.BlockSpec((tm, d), lambda i: (i, 0)),
        out_shape=jax.ShapeDtypeStruct((t, d), _MXU_DTYPE),
        compiler_params=_params(("parallel",)),
    )(x, g)


def _rms_bwd(x, g, dy, dres, *, name, tm=512):
    t, d = x.shape
    tm = min(tm, t)
    has_res = dres is not None

    def body(*refs):
        x_ref, g_ref, dy_ref = refs[:3]
        dx_ref, dg_ref = refs[-2:]
        xv, dyv = x_ref[...], dy_ref[...]
        r = lax.rsqrt(jnp.mean(xv * xv, axis=1, keepdims=True) + EPS)
        u = dyv * g_ref[...]
        dx = r * u - xv * (r * r * r) * jnp.mean(u * xv, axis=1, keepdims=True)
        if has_res:
            dx = dx + refs[3][...]
        dx_ref[...] = dx

        @pl.when(pl.program_id(0) == 0)
        def _():
            dg_ref[...] = jnp.zeros_like(dg_ref)

        dg_ref[...] += jnp.sum(dyv * xv * r, axis=0, keepdims=True)

    row = pl.BlockSpec((tm, d), lambda i: (i, 0))
    vec = pl.BlockSpec((1, d), lambda i: (0, 0))
    return pl.pallas_call(
        body, name=name, grid=(t // tm,),
        in_specs=[row, vec, row] + ([row] if has_res else []),
        out_specs=(row, vec),
        out_shape=(jax.ShapeDtypeStruct((t, d), F32), jax.ShapeDtypeStruct((1, d), F32)),
        compiler_params=_params(("arbitrary",)),
    )(*([x, g, dy] + ([dres] if has_res else [])))


def _hg_constants():
    c = HG_CHUNK
    t = np.arange(c)
    sums = [t[None, :] <= t[:, None]]
    masks = []
    for m in HG_LEVELS:
        base = (t // (2 * m)) * (2 * m)
        mid = base + m - 1
        second = (t - base) >= m
        upper = (t[None, :] > mid[:, None]) & (t[None, :] <= t[:, None])
        lower = (t[None, :] > t[:, None]) & (t[None, :] <= mid[:, None])
        sums.append(np.where(second[:, None], upper, lower))
        masks.append(second[:, None] & (~second)[None, :] & (base[:, None] == base[None, :]))
    return (np.concatenate(sums, axis=0).astype(np.float32), np.stack(masks).astype(np.float32))


def _hg_gates(blk, lbp):
    q, x, v, gl = blk[:, 0:128], blk[:, 128:256], blk[:, 256:384], blk[:, 384:512]
    mx = jnp.max(lbp, axis=0, keepdims=True)
    e = jnp.exp(lbp - mx)
    lb = e[0:1, :] / jnp.sum(e, axis=0, keepdims=True)
    sig = jax.nn.sigmoid(x)
    f = lb + (1.0 - lb) * sig
    return q, v, gl, lb, sig, f, 1.0 - f, jnp.log(f)


def _hg_fwd(proj, lbp, ng, bsz, seq, *, y_width):
    t = proj.shape[0]
    nc = seq // HG_CHUNK
    a_np, m_np = _hg_constants()
    a_all = jnp.asarray(a_np, _MXU_DTYPE)
    masks = jnp.asarray(m_np, F32)
    nl = len(HG_LEVELS)

    def body(p_ref, lb_ref, ng_ref, a_ref, m_ref, y_ref, o_ref, st_ref):
        a_mat = a_ref[...]
        ngv = ng_ref[...]
        lbp_v = lb_ref[...]

        def chunk(c, st):
            rows = pl.ds(pl.multiple_of(c * HG_CHUNK, HG_CHUNK), HG_CHUNK)
            q, v, gl, lb, sig, f, k, g = _hg_gates(p_ref[rows, :], lbp_v)
            e_all = _split_dot(a_mat, g, NN, 3)
            b = e_all[0:HG_CHUNK]
            st_ref[c] = st
            o = _dot(q * jnp.exp(b), st, NT)
            p = jnp.zeros((HG_CHUNK, HG_CHUNK), F32)
            for li in range(nl):
                e = jnp.exp(e_all[HG_CHUNK * (li + 1):HG_CHUNK * (li + 2)])
                p = p + m_ref[li] * _dot(q * e, k * e, NT)
            o = o + _dot(p, v) + jnp.sum(q * k, axis=1, keepdims=True) * v
            bl = b[HG_CHUNK - 1:HG_CHUNK, :]
            st_new = st * jnp.exp(bl) + _dot(v, k * jnp.exp(bl - b), TN)
            r = lax.rsqrt(jnp.mean(o * o, axis=1, keepdims=True) + EPS)
            o_ref[rows, :] = o
            y_ref[rows, :] = (o * r * ngv) * (gl * jax.nn.sigmoid(gl))
            return st_new

        lax.fori_loop(0, nc, chunk, jnp.zeros((HG_D, HG_D), F32))

    return pl.pallas_call(
        body, name="hgrn2_fwd", grid=(bsz, HG_HEADS),
        in_specs=[pl.BlockSpec((seq, 512), lambda b, h: (b, h)),
                  pl.BlockSpec((2, HG_D), lambda b, h: (0, h)),
                  pl.BlockSpec((1, HG_D), lambda b, h: (0, 0)),
                  pl.BlockSpec(a_all.shape, lambda b, h: (0, 0)),
                  pl.BlockSpec(masks.shape, lambda b, h: (0, 0, 0))],
        out_specs=(pl.BlockSpec((seq, HG_D), lambda b, h: (b, h)),
                   pl.BlockSpec((seq, HG_D), lambda b, h: (b, h)),
                   pl.BlockSpec((None, None, nc, HG_D, HG_D), lambda b, h: (b, h, 0, 0, 0))),
        out_shape=(jax.ShapeDtypeStruct((t, y_width), F32),
                   jax.ShapeDtypeStruct((t, HG_HEADS * HG_D), F32),
                   jax.ShapeDtypeStruct((bsz, HG_HEADS, nc, HG_D, HG_D), F32)),
        compiler_params=_params(("parallel", "parallel")),
    )(proj, lbp, ng, a_all, masks)


def _hg_bwd(proj, lbp, ng, o_all, states, dy, bsz, seq):
    t = proj.shape[0]
    nc = seq // HG_CHUNK
    a_np, m_np = _hg_constants()
    a_all = jnp.asarray(a_np, _MXU_DTYPE)
    masks = jnp.asarray(m_np, F32)
    nl = len(HG_LEVELS)
    cs = HG_CHUNK

    def body(p_ref, lb_ref, ng_ref, a_ref, m_ref, o_ref, st_ref, dy_ref, dp_ref, dlb_ref, dng_ref):
        a_mat = a_ref[...]
        ngv = ng_ref[...]
        lbp_v = lb_ref[...]
        last_row = lax.broadcasted_iota(jnp.int32, (cs, HG_D), 0) == cs - 1

        def chunk(i, carry):
            dst, dlb_acc, dng_acc = carry
            c = nc - 1 - i
            rows = pl.ds(pl.multiple_of(c * cs, cs), cs)
            q, v, gl, lb, sig, f, k, g = _hg_gates(p_ref[rows, :], lbp_v)
            o = o_ref[rows, :]
            dyv = dy_ref[rows, :]
            st = st_ref[c]
            e_all = _split_dot(a_mat, g, NN, 3)
            b = e_all[0:cs]
            eb = jnp.exp(b)
            bl = b[cs - 1:cs, :]
            ebl = jnp.exp(bl)
            ekd = jnp.exp(bl - b)
            qb, kd = q * eb, k * ekd
            sg = jax.nn.sigmoid(gl)
            r = lax.rsqrt(jnp.mean(o * o, axis=1, keepdims=True) + EPS)
            dgl = dyv * (o * r * ngv) * (sg * (1.0 + gl * (1.0 - sg)))
            u = dyv * (gl * sg) * ngv
            do = r * u - o * (r * r * r) * jnp.mean(u * o, axis=1, keepdims=True)
            dng_acc = dng_acc + jnp.sum(dyv * (gl * sg) * o * r, axis=0, keepdims=True)
            es, qm, km = [], [], []
            p = jnp.zeros((cs, cs), F32)
            for li in range(nl):
                e = jnp.exp(e_all[cs * (li + 1):cs * (li + 2)])
                es.append(e)
                qm.append(q * e)
                km.append(k * e)
                p = p + m_ref[li] * _dot(qm[li], km[li], NT)
            qk = jnp.sum(q * k, axis=1, keepdims=True)
            dp = _dot(do, v, NT)
            dv = _dot(p, do, TN) + qk * do + _dot(kd, dst, NT)
            dqb = _dot(do, st)
            dkd = _dot(v, dst)
            dq = dqb * eb
            dk = dkd * ekd
            db = dqb * qb - dkd * kd
            dbl = jnp.sum(dkd * kd, axis=0, keepdims=True) + jnp.sum(dst * st, axis=0, keepdims=True) * ebl
            de = [db + jnp.where(last_row, dbl, 0.0)]
            for li in range(nl):
                dpm = m_ref[li] * dp
                dqm = _dot(dpm, km[li])
                dkm = _dot(dpm, qm[li], TN)
                dq = dq + dqm * es[li]
                dk = dk + dkm * es[li]
                de.append(dqm * qm[li] + dkm * km[li])
            dpd = jnp.sum(do * v, axis=1, keepdims=True)
            dq = dq + dpd * k
            dk = dk + dpd * q
            dg = _split_dot(a_mat, jnp.concatenate(de, axis=0), TN, 2)
            df = dg / f - dk
            dx = df * (1.0 - lb) * sig * (1.0 - sig)
            dlb_acc = dlb_acc + jnp.sum(df * (1.0 - sig), axis=0, keepdims=True)
            dp_ref[rows, 0:128] = dq
            dp_ref[rows, 128:256] = dx
            dp_ref[rows, 256:384] = dv
            dp_ref[rows, 384:512] = dgl
            return dst * ebl + _dot(do, qb, TN), dlb_acc, dng_acc

        zrow = jnp.zeros((1, HG_D), F32)
        _, dlb, dng = lax.fori_loop(0, nc, chunk, (jnp.zeros((HG_D, HG_D), F32), zrow, zrow))
        mx = jnp.max(lbp_v, axis=0, keepdims=True)
        e = jnp.exp(lbp_v - mx)
        s0 = e[0:1, :] / jnp.sum(e, axis=0, keepdims=True)
        da0 = dlb * s0 * (1.0 - s0)
        bi = pl.program_id(1)
        first = jnp.logical_and(pl.program_id(0) == 0, bi == 0)

        @pl.when(bi == 0)
        def _():
            dlb_ref[...] = jnp.zeros_like(dlb_ref)

        @pl.when(first)
        def _():
            dng_ref[...] = jnp.zeros_like(dng_ref)

        dlb_ref[...] += jnp.concatenate([da0, -da0], axis=0)
        dng_ref[...] += dng

    return pl.pallas_call(
        body, name="hgrn2_bwd", grid=(HG_HEADS, bsz),
        in_specs=[pl.BlockSpec((seq, 512), lambda h, b: (b, h)),
                  pl.BlockSpec((2, HG_D), lambda h, b: (0, h)),
                  pl.BlockSpec((1, HG_D), lambda h, b: (0, 0)),
                  pl.BlockSpec(a_all.shape, lambda h, b: (0, 0)),
                  pl.BlockSpec(masks.shape, lambda h, b: (0, 0, 0)),
                  pl.BlockSpec((seq, HG_D), lambda h, b: (b, h)),
                  pl.BlockSpec((None, None, nc, HG_D, HG_D), lambda h, b: (b, h, 0, 0, 0)),
                  pl.BlockSpec((seq, HG_D), lambda h, b: (b, h))],
        out_specs=(pl.BlockSpec((seq, 512), lambda h, b: (b, h)),
                   pl.BlockSpec((2, HG_D), lambda h, b: (0, h)),
                   pl.BlockSpec((1, HG_D), lambda h, b: (0, 0))),
        out_shape=(jax.ShapeDtypeStruct((t, HG_COLS), F32),
                   jax.ShapeDtypeStruct((2, HG_HEADS * HG_D), F32),
                   jax.ShapeDtypeStruct((1, HG_D), F32)),
        compiler_params=_params(("arbitrary", "arbitrary")),
    )(proj, lbp, ng, a_all, masks, o_all, states, dy)


def _sw_constants():
    half = ROT_DIM // 2
    inv = (np.float32(ROPE_THETA) ** (-(np.arange(half, dtype=np.float32) * np.float32(2.0) / np.float32(ROT_DIM)))
           ).astype(np.float32)
    freq = np.zeros((1, 128), np.float32)
    sign = np.zeros((1, 128), np.float32)
    for h in range(2):
        freq[0, 64 * h:64 * h + half] = inv
        freq[0, 64 * h + half:64 * h + 2 * half] = inv
        sign[0, 64 * h:64 * h + half] = -1.0
        sign[0, 64 * h + half:64 * h + 2 * half] = 1.0
    seg = np.kron(np.eye(8, dtype=np.float32), np.full((64, 64), 1.0 / 64.0, np.float32))
    return freq, sign, seg


def _rope_tables(pos, freq, sign):
    ang = pos.astype(F32) * freq
    return jnp.cos(ang), jnp.sin(ang) * sign


def _tile_lanes(v, times):
    return v if times == 1 else jnp.concatenate([v] * times, axis=1)


def _swap_halves(v):
    w = v.shape[1]
    half = ROT_DIM // 2
    lane = lax.broadcasted_iota(jnp.int32, v.shape, 1) % SW_HD
    return jnp.where(lane < half, pltpu.roll(v, w - half, 1), jnp.where(lane < 2 * half, pltpu.roll(v, half, 1), 0.0))


def _sw_norm_rope(tv, gain, seg, cosv, sinv):
    w = tv.shape[1]
    ms = _split_dot_rhs(tv * tv, seg[0:w, 0:w])
    r = lax.rsqrt(ms + EPS)
    tn = tv * r * gain
    reps = w // 128
    return tn * _tile_lanes(cosv, reps) + _swap_halves(tn) * _tile_lanes(sinv, reps), r


def _split_dot_rhs(v, a):
    hi = _mx(v)
    lo = _mx(v - hi.astype(F32))
    return (lax.dot_general(hi, a, (NN, ((), ())), preferred_element_type=F32)
            + lax.dot_general(lo, a, (NN, ((), ())), preferred_element_type=F32))


def _sw_norm_rope_bwd(dt, tv, r, gain, seg, cosv, sinv):
    w = tv.shape[1]
    reps = w // 128
    dtn = dt * _tile_lanes(cosv, reps) + _swap_halves(dt * _tile_lanes(sinv, reps))
    u = dtn * gain
    dtv = r * u - tv * (r * r * r) * _split_dot_rhs(u * tv, seg[0:w, 0:w])
    return dtv, jnp.sum(dtn * tv * r, axis=0, keepdims=True)


def _sw_probs(qh, kp, kc, sink, first_block):
    scale = SW_HD ** -0.5
    qi = lax.broadcasted_iota(jnp.int32, (SW_BLOCK, SW_BLOCK), 0)
    kj = lax.broadcasted_iota(jnp.int32, (SW_BLOCK, SW_BLOCK), 1)
    ok_prev = jnp.logical_and(kj > qi, jnp.logical_not(first_block))
    ok_cur = kj <= qi
    sp = jnp.where(ok_prev, _dot(qh, kp, NT) * scale, -jnp.inf)
    sc = jnp.where(ok_cur, _dot(qh, kc, NT) * scale, -jnp.inf)
    m = jnp.maximum(jnp.maximum(jnp.max(sp, axis=1, keepdims=True), jnp.max(sc, axis=1, keepdims=True)), sink)
    pp, pc = jnp.exp(sp - m), jnp.exp(sc - m)
    es = jnp.exp(sink - m)
    den = jnp.sum(pp, axis=1, keepdims=True) + jnp.sum(pc, axis=1, keepdims=True) + es
    return pp / den, pc / den, es / den


def _sw_specs(nb):
    def cur(b, n):
        return b * nb + jnp.minimum(n, nb - 1)

    def prev(b, n):
        return b * nb + jnp.maximum(jnp.minimum(n, nb - 1) - 1, 0)

    return cur, prev


def _sw_fwd(proj, pos, qg, kg, sinks, y_in, bsz, seq):
    t = proj.shape[0]
    nb = seq // SW_BLOCK
    freq_np, sign_np, seg_np = _sw_constants()
    freq, sign = jnp.asarray(freq_np), jnp.asarray(sign_np)
    seg = jnp.asarray(seg_np, _MXU_DTYPE)
    cur, prev = _sw_specs(nb)

    def body(q_ref, kc_ref, kp_ref, vc_ref, vp_ref, pc_ref, pp_ref, qg_ref, kg_ref, sk_ref, fr_ref, sn_ref, seg_ref,
             yin_ref, y_ref):
        del yin_ref
        n = pl.program_id(1)
        segv = seg_ref[...]
        cos_c, sin_c = _rope_tables(pc_ref[...], fr_ref[...], sn_ref[...])
        cos_p, sin_p = _rope_tables(pp_ref[...], fr_ref[...], sn_ref[...])
        qr, _ = _sw_norm_rope(q_ref[...], qg_ref[...], segv, cos_c, sin_c)
        kcr, _ = _sw_norm_rope(kc_ref[...], kg_ref[...], segv, cos_c, sin_c)
        kpr, _ = _sw_norm_rope(kp_ref[...], kg_ref[...], segv, cos_p, sin_p)
        vc, vp = vc_ref[...], vp_ref[...]
        for h in range(SW_HEADS):
            kv = h // SW_GROUP
            ks = slice(SW_HD * kv, SW_HD * (kv + 1))
            pp, pc, _ = _sw_probs(qr[:, SW_HD * h:SW_HD * (h + 1)], kpr[:, ks], kcr[:, ks], sk_ref[0, h], n == 0)
            y_ref[:, SW_HD * h:SW_HD * (h + 1)] = _dot(pp, vp[:, ks]) + _dot(pc, vc[:, ks])

    rowq = pl.BlockSpec((SW_BLOCK, 512), lambda b, n: (cur(b, n), 0))
    full = lambda a: pl.BlockSpec(a.shape, lambda b, n: (0,) * a.ndim)
    yw = y_in.shape[1]
    return pl.pallas_call(
        body, name="swa_fwd", grid=(bsz, nb),
        in_specs=[rowq,
                  pl.BlockSpec((SW_BLOCK, 128), lambda b, n: (cur(b, n), 4)),
                  pl.BlockSpec((SW_BLOCK, 128), lambda b, n: (prev(b, n), 4)),
                  pl.BlockSpec((SW_BLOCK, 128), lambda b, n: (cur(b, n), 5)),
                  pl.BlockSpec((SW_BLOCK, 128), lambda b, n: (prev(b, n), 5)),
                  pl.BlockSpec((SW_BLOCK, 1), lambda b, n: (cur(b, n), 0)),
                  pl.BlockSpec((SW_BLOCK, 1), lambda b, n: (prev(b, n), 0)),
                  full(qg), full(kg),
                  pl.BlockSpec(memory_space=pltpu.SMEM),
                  full(freq), full(sign), full(seg),
                  pl.BlockSpec(memory_space=pl.ANY)],
        out_specs=pl.BlockSpec((SW_BLOCK, 512), lambda b, n: (cur(b, n), 1)),
        out_shape=jax.ShapeDtypeStruct((t, yw), F32),
        input_output_aliases={13: 0},
        compiler_params=_params(("parallel", "parallel")),
    )(proj, proj, proj, proj, proj, pos, pos, qg, kg, sinks, freq, sign, seg, y_in)


def _sw_bwd(proj, pos, qg, kg, sinks, y, dy, bsz, seq):
    t = proj.shape[0]
    nb = seq // SW_BLOCK
    freq_np, sign_np, seg_np = _sw_constants()
    freq, sign = jnp.asarray(freq_np), jnp.asarray(sign_np)
    seg = jnp.asarray(seg_np, _MXU_DTYPE)
    cur, prev = _sw_specs(nb)
    scale = SW_HD ** -0.5

    def body(q_ref, kc_ref, kp_ref, vc_ref, vp_ref, pc_ref, pp_ref, qg_ref, kg_ref, sk_ref, fr_ref, sn_ref, seg_ref,
             y_ref, dy_ref, dp_ref, dqg_ref, dkg_ref, dsk_ref,
             dq_car, dkv_car, dqr_s, dkc_s, dkp_s, dvc_s, dvp_s, gq_acc, gk_acc, sk_acc):
        b, n = pl.program_id(0), pl.program_id(1)
        first = jnp.logical_and(b == 0, n == 0)
        last = jnp.logical_and(b == pl.num_programs(0) - 1, n == nb)

        @pl.when(first)
        def _():
            gq_acc[...] = jnp.zeros_like(gq_acc)
            gk_acc[...] = jnp.zeros_like(gk_acc)
            sk_acc[...] = jnp.zeros_like(sk_acc)

        @pl.when(n < nb)
        def _():
            segv = seg_ref[...]
            cos_c, sin_c = _rope_tables(pc_ref[...], fr_ref[...], sn_ref[...])
            cos_p, sin_p = _rope_tables(pp_ref[...], fr_ref[...], sn_ref[...])
            qv, kcv, kpv = q_ref[...], kc_ref[...], kp_ref[...]
            qr, rq = _sw_norm_rope(qv, qg_ref[...], segv, cos_c, sin_c)
            kcr, rkc = _sw_norm_rope(kcv, kg_ref[...], segv, cos_c, sin_c)
            kpr, rkp = _sw_norm_rope(kpv, kg_ref[...], segv, cos_p, sin_p)
            vc, vp = vc_ref[...], vp_ref[...]
            dkc_s[...] = jnp.zeros_like(dkc_s)
            dkp_s[...] = jnp.zeros_like(dkp_s)
            dvc_s[...] = jnp.zeros_like(dvc_s)
            dvp_s[...] = jnp.zeros_like(dvp_s)
            lane = lax.broadcasted_iota(jnp.int32, (1, 128), 1)
            dsk = jnp.zeros((1, 128), F32)
            for h in range(SW_HEADS):
                kv = h // SW_GROUP
                ks = slice(SW_HD * kv, SW_HD * (kv + 1))
                hs = slice(SW_HD * h, SW_HD * (h + 1))
                qh = qr[:, hs]
                pp, pc, ps = _sw_probs(qh, kpr[:, ks], kcr[:, ks], sk_ref[0, h], n == 0)
                doh = dy_ref[:, hs]
                delta = jnp.sum(doh * y_ref[:, hs], axis=1, keepdims=True)
                dsp = pp * (_dot(doh, vp[:, ks], NT) - delta) * scale
                dsc = pc * (_dot(doh, vc[:, ks], NT) - delta) * scale
                dsk = dsk + jnp.where(lane == h, -jnp.sum(ps * delta), 0.0)
                dvp_s[:, ks] += _dot(pp, doh, TN)
                dvc_s[:, ks] += _dot(pc, doh, TN)
                dqr_s[:, hs] = _dot(dsp, kpr[:, ks]) + _dot(dsc, kcr[:, ks])
                dkp_s[:, ks] += _dot(dsp, qh, TN)
                dkc_s[:, ks] += _dot(dsc, qh, TN)
            dq, gq = _sw_norm_rope_bwd(dqr_s[...], qv, rq, qg_ref[...], segv, cos_c, sin_c)
            dkc, gkc = _sw_norm_rope_bwd(dkc_s[...], kcv, rkc, kg_ref[...], segv, cos_c, sin_c)
            dkp, gkp = _sw_norm_rope_bwd(dkp_s[...], kpv, rkp, kg_ref[...], segv, cos_p, sin_p)
            gq_acc[...] += gq
            gk_acc[...] += gkc + gkp
            sk_acc[...] += dsk

            @pl.when(n > 0)
            def _():
                dp_ref[:, 0:512] = dq_car[...]
                dp_ref[:, 512:640] = dkv_car[:, 0:128] + dkp
                dp_ref[:, 640:768] = dkv_car[:, 128:256] + dvp_s[...]

            dq_car[...] = dq
            dkv_car[:, 0:128] = dkc
            dkv_car[:, 128:256] = dvc_s[...]

        @pl.when(n == nb)
        def _():
            dp_ref[:, 0:512] = dq_car[...]
            dp_ref[:, 512:768] = dkv_car[...]

        @pl.when(last)
        def _():
            gq = gq_acc[...]
            acc = gq[:, 0:SW_HD]
            for h in range(1, SW_HEADS):
                acc = acc + gq[:, SW_HD * h:SW_HD * (h + 1)]
            dqg_ref[...] = acc
            gk = gk_acc[...]
            dkg_ref[...] = gk[:, 0:SW_HD] + gk[:, SW_HD:2 * SW_HD]
            dsk_ref[...] = sk_acc[...]

    rowq = pl.BlockSpec((SW_BLOCK, 512), lambda b, n: (cur(b, n), 0))
    full = lambda a: pl.BlockSpec(a.shape, lambda b, n: (0,) * a.ndim)

    def out_row(b, n):
        return b * nb + jnp.maximum(n - 1, 0)

    return pl.pallas_call(
        body, name="swa_bwd", grid=(bsz, nb + 1),
        in_specs=[rowq,
                  pl.BlockSpec((SW_BLOCK, 128), lambda b, n: (cur(b, n), 4)),
                  pl.BlockSpec((SW_BLOCK, 128), lambda b, n: (prev(b, n), 4)),
                  pl.BlockSpec((SW_BLOCK, 128), lambda b, n: (cur(b, n), 5)),
                  pl.BlockSpec((SW_BLOCK, 128), lambda b, n: (prev(b, n), 5)),
                  pl.BlockSpec((SW_BLOCK, 1), lambda b, n: (cur(b, n), 0)),
                  pl.BlockSpec((SW_BLOCK, 1), lambda b, n: (prev(b, n), 0)),
                  full(qg), full(kg),
                  pl.BlockSpec(memory_space=pltpu.SMEM),
                  full(freq), full(sign), full(seg),
                  pl.BlockSpec((SW_BLOCK, 512), lambda b, n: (cur(b, n), 1)),
                  pl.BlockSpec((SW_BLOCK, 512), lambda b, n: (cur(b, n), 1))],
        out_specs=(pl.BlockSpec((SW_BLOCK, SW_COLS), lambda b, n: (out_row(b, n), 0)),
                   pl.BlockSpec((1, SW_HD), lambda b, n: (0, 0)),
                   pl.BlockSpec((1, SW_HD), lambda b, n: (0, 0)),
                   pl.BlockSpec((1, 128), lambda b, n: (0, 0))),
        out_shape=(jax.ShapeDtypeStruct((t, SW_COLS), F32),
                   jax.ShapeDtypeStruct((1, SW_HD), F32),
                   jax.ShapeDtypeStruct((1, SW_HD), F32),
                   jax.ShapeDtypeStruct((1, 128), F32)),
        scratch_shapes=[pltpu.VMEM((SW_BLOCK, 512), F32), pltpu.VMEM((SW_BLOCK, 256), F32),
                        pltpu.VMEM((SW_BLOCK, 512), F32),
                        pltpu.VMEM((SW_BLOCK, 128), F32), pltpu.VMEM((SW_BLOCK, 128), F32),
                        pltpu.VMEM((SW_BLOCK, 128), F32), pltpu.VMEM((SW_BLOCK, 128), F32),
                        pltpu.VMEM((1, 512), F32), pltpu.VMEM((1, 128), F32), pltpu.VMEM((1, 128), F32)],
        compiler_params=_params(("arbitrary", "arbitrary")),
    )(proj, proj, proj, proj, proj, pos, pos, qg, kg, sinks, freq, sign, seg, y, dy)


def _head_rms(tv, gain):
    r = lax.rsqrt(jnp.mean(tv * tv, axis=1, keepdims=True) + EPS)
    return tv * r * gain, r


def _head_rms_bwd(dtn, tv, r, gain):
    u = dtn * gain
    return r * u - tv * (r * r * r) * jnp.mean(u * tv, axis=1, keepdims=True), jnp.sum(dtn * tv * r, axis=0, keepdims=True)


def _xa_probs(qn, kn):
    s = _dot(qn, kn, NT) * (XA_HD ** -0.5)
    e = jnp.exp(s - jnp.max(s, axis=1, keepdims=True))
    return e / jnp.sum(e, axis=1, keepdims=True)


def _xa_fwd(qx, kvx, qg, kg, bsz, seq, mlen, *, tq=512):
    t = qx.shape[0]
    tq = min(tq, seq)
    nq = seq // tq
    w = XA_HEADS * XA_HD

    def body(q_ref, kv_ref, qg_ref, kg_ref, o_ref):
        for h in range(XA_HEADS):
            hs = slice(XA_HD * h, XA_HD * (h + 1))
            qn, _ = _head_rms(q_ref[:, hs], qg_ref[...])
            kn, _ = _head_rms(kv_ref[:, hs], kg_ref[...])
            o_ref[:, hs] = _dot(_xa_probs(qn, kn), kv_ref[:, w + XA_HD * h:w + XA_HD * (h + 1)])

    vec = pl.BlockSpec((1, XA_HD), lambda b, i: (0, 0))
    return pl.pallas_call(
        body, name="xattn_fwd", grid=(bsz, nq),
        in_specs=[pl.BlockSpec((tq, w), lambda b, i: (b * nq + i, 0)),
                  pl.BlockSpec((mlen, 2 * w), lambda b, i: (b, 0)), vec, vec],
        out_specs=pl.BlockSpec((tq, w), lambda b, i: (b * nq + i, 0)),
        out_shape=jax.ShapeDtypeStruct((t, w), F32),
        compiler_params=_params(("parallel", "parallel")),
    )(qx, kvx, qg, kg)


def _xa_bwd(qx, kvx, qg, kg, do, bsz, seq, mlen, *, tq=512):
    t = qx.shape[0]
    tq = min(tq, seq)
    nq = seq // tq
    w = XA_HEADS * XA_HD
    scale = XA_HD ** -0.5

    def body(q_ref, kv_ref, qg_ref, kg_ref, do_ref, dq_ref, dkv_ref, dqg_ref, dkg_ref):
        b, i = pl.program_id(0), pl.program_id(1)

        @pl.when(jnp.logical_and(b == 0, i == 0))
        def _():
            dqg_ref[...] = jnp.zeros_like(dqg_ref)
            dkg_ref[...] = jnp.zeros_like(dkg_ref)

        @pl.when(i == 0)
        def _():
            dkv_ref[...] = jnp.zeros_like(dkv_ref)

        gq_sum = jnp.zeros((1, XA_HD), F32)
        gk_sum = jnp.zeros((1, XA_HD), F32)
        for h in range(XA_HEADS):
            hs = slice(XA_HD * h, XA_HD * (h + 1))
            vs = slice(w + XA_HD * h, w + XA_HD * (h + 1))
            qv, kv, vv = q_ref[:, hs], kv_ref[:, hs], kv_ref[:, vs]
            qn, rq = _head_rms(qv, qg_ref[...])
            kn, rk = _head_rms(kv, kg_ref[...])
            p = _xa_probs(qn, kn)
            doh = do_ref[:, hs]
            dp = _dot(doh, vv, NT)
            ds = p * (dp - jnp.sum(p * dp, axis=1, keepdims=True)) * scale
            dqv, gq = _head_rms_bwd(_dot(ds, kn), qv, rq, qg_ref[...])
            dkv, gk = _head_rms_bwd(_dot(ds, qn, TN), kv, rk, kg_ref[...])
            dq_ref[:, hs] = dqv
            dkv_ref[:, hs] += dkv
            dkv_ref[:, vs] += _dot(p, doh, TN)
            gq_sum = gq_sum + gq
            gk_sum = gk_sum + gk
        dqg_ref[...] += gq_sum
        dkg_ref[...] += gk_sum

    vec = pl.BlockSpec((1, XA_HD), lambda b, i: (0, 0))
    row = pl.BlockSpec((tq, w), lambda b, i: (b * nq + i, 0))
    mem = pl.BlockSpec((mlen, 2 * w), lambda b, i: (b, 0))
    return pl.pallas_call(
        body, name="xattn_bwd", grid=(bsz, nq),
        in_specs=[row, mem, vec, vec, row],
        out_specs=(row, mem, vec, vec),
        out_shape=(jax.ShapeDtypeStruct((t, w), F32), jax.ShapeDtypeStruct((bsz * mlen, 2 * w), F32),
                   jax.ShapeDtypeStruct((1, XA_HD), F32), jax.ShapeDtypeStruct((1, XA_HD), F32)),
        compiler_params=_params(("arbitrary", "arbitrary")),
    )(qx, kvx, qg, kg, do)


def _loss_sum(dy, d_model, *, tm=512):
    t, d = dy.shape
    tm = min(tm, t)
    steps = t // tm

    def body(dy_ref, o_ref, acc_ref):
        i = pl.program_id(0)

        @pl.when(i == 0)
        def _():
            acc_ref[...] = jnp.zeros_like(acc_ref)

        diff = dy_ref[...] * float(d_model)
        acc_ref[...] += jnp.sum(diff * diff, axis=0, keepdims=True)

        @pl.when(i == steps - 1)
        def _():
            o_ref[...] = jnp.zeros_like(o_ref) + 0.5 * jnp.sum(acc_ref[...]) / float(d_model)

    return pl.pallas_call(
        body, name="loss_sum", grid=(steps,),
        in_specs=[pl.BlockSpec((tm, d), lambda i: (i, 0))],
        out_specs=pl.BlockSpec((1, 128), lambda i: (0, 0)),
        out_shape=jax.ShapeDtypeStruct((1, 128), F32),
        scratch_shapes=[pltpu.VMEM((1, d), F32)],
        compiler_params=_params(("arbitrary",)),
    )(dy)


def _adamw_math(w, g, m, v):
    m = ADAM_B1 * m + (1.0 - ADAM_B1) * g
    v = ADAM_B2 * v + (1.0 - ADAM_B2) * (g * g)
    m_hat = m / (1.0 - ADAM_B1 ** ADAM_STEP)
    v_hat = v / (1.0 - ADAM_B2 ** ADAM_STEP)
    return -ADAM_LR * (m_hat / (jnp.sqrt(v_hat) + ADAM_EPS) + ADAM_WD * w), m, v


def _adamw_big(w, g, m, v, *, name, tr=256):
    r, c = w.shape
    tr = min(tr, r)

    def body(w_ref, g_ref, m_ref, v_ref, d_ref, mo_ref, vo_ref):
        d, mn, vn = _adamw_math(w_ref[...], g_ref[...], m_ref[...], v_ref[...])
        d_ref[...] = d
        mo_ref[...] = mn
        vo_ref[...] = vn

    spec = pl.BlockSpec((tr, c), lambda i: (i, 0))
    shp = jax.ShapeDtypeStruct((r, c), F32)
    return pl.pallas_call(
        body, name=name, grid=(r // tr,), in_specs=[spec] * 4, out_specs=(spec,) * 3, out_shape=(shp,) * 3,
        compiler_params=_params(("parallel",)),
    )(w, g, m, v)


def _adamw_small(ws, gs, ms, vs):
    n = len(ws)

    def body(*refs):
        for i in range(n):
            d, mn, vn = _adamw_math(refs[i][...], refs[n + i][...], refs[2 * n + i][...], refs[3 * n + i][...])
            refs[4 * n + i][...] = d
            refs[5 * n + i][...] = mn
            refs[6 * n + i][...] = vn

    shapes = tuple(jax.ShapeDtypeStruct(w.shape, F32) for w in ws)
    return pl.pallas_call(body, name="adamw_small", out_shape=shapes * 3)(*ws, *gs, *ms, *vs)


def _add_halves(g, recv, c_idx, *, name, tr=256):
    _, r, c = g.shape
    h = r // 2
    tr = min(tr, h)
    nt = h // tr

    def body(c_ref, g_ref, r_ref, o_ref):
        del c_ref
        o_ref[...] = g_ref[...] + r_ref[...]

    return pl.pallas_call(
        body, name=name,
        grid_spec=pltpu.PrefetchScalarGridSpec(
            num_scalar_prefetch=1, grid=(4, nt),
            in_specs=[pl.BlockSpec((None, tr, c), lambda k, i, cr: (k, cr[0] * nt + i, 0)),
                      pl.BlockSpec((None, tr, c), lambda k, i, cr: (k, i, 0))],
            out_specs=pl.BlockSpec((None, tr, c), lambda k, i, cr: (k, i, 0))),
        out_shape=jax.ShapeDtypeStruct((4, h, c), F32),
        compiler_params=_params(("parallel", "parallel")),
    )(c_idx, g, recv)


def _add_chips(p, recv, me_idx, *, name, tr=256):
    _, h, c = p.shape
    tr = min(tr, h)

    def body(me_ref, p_ref, r_ref, o_ref):
        del me_ref
        o_ref[...] = ((p_ref[...] + r_ref[0]) + r_ref[1]) + r_ref[2]

    return pl.pallas_call(
        body, name=name,
        grid_spec=pltpu.PrefetchScalarGridSpec(
            num_scalar_prefetch=1, grid=(h // tr,),
            in_specs=[pl.BlockSpec((None, tr, c), lambda i, mr: (mr[0], i, 0)),
                      pl.BlockSpec((3, tr, c), lambda i, mr: (0, i, 0))],
            out_specs=pl.BlockSpec((tr, c), lambda i, mr: (i, 0))),
        out_shape=jax.ShapeDtypeStruct((h, c), F32),
        compiler_params=_params(("parallel",)),
    )(me_idx, p, recv)


def _place():
    x, y, c = lax.axis_index("x"), lax.axis_index("y"), lax.axis_index("c")
    chips = [(1 - x, y), (x, 1 - y), (1 - x, 1 - y)]
    return x, y, c, chips


ANY = pl.BlockSpec(memory_space=pl.ANY)


def _all_gather_weights(shards):
    n = len(shards)

    def body(*refs):
        ins, outs = refs[:n], refs[n:2 * n]
        send_sems, recv_sems, local_sems = refs[2 * n:]
        x, y, c, chips = _place()
        me = 2 * x + y

        def half(a, chip_idx, which):
            h = ins[a].shape[0] // 2
            return outs[a].at[chip_idx, pl.ds(which * h, h), :]

        def copy(a, j, chip_idx, which, to, src=None):
            return pltpu.make_async_remote_copy(
                src_ref=half(a, chip_idx, which) if src is None else src, dst_ref=half(a, chip_idx, which),
                send_sem=send_sems.at[a * 6 + j], recv_sem=recv_sems.at[a * 6 + j], device_id=to, device_id_type=MESH)

        for a in range(n):
            h = ins[a].shape[0] // 2
            pltpu.make_async_copy(ins[a], outs[a].at[me], local_sems.at[a]).start()
            for j, (px, py) in enumerate(chips):
                copy(a, j, me, c, (px, py, c), src=ins[a].at[pl.ds(c * h, h), :]).start()
        for a in range(n):
            for j, (px, py) in enumerate(chips):
                copy(a, j, 2 * px + py, c, (x, y, c)).wait_recv()
                copy(a, 3 + j, 2 * px + py, c, (x, y, 1 - c)).start()
        for a in range(n):
            for j, (px, py) in enumerate(chips):
                copy(a, 3 + j, 2 * px + py, 1 - c, (x, y, c)).wait_recv()
        for a in range(n):
            pltpu.make_async_copy(ins[a], outs[a].at[me], local_sems.at[a]).wait()
            h = ins[a].shape[0] // 2
            for j, (px, py) in enumerate(chips):
                copy(a, j, me, c, (px, py, c), src=ins[a].at[pl.ds(c * h, h), :]).wait_send()
                copy(a, 3 + j, 2 * px + py, c, (x, y, 1 - c)).wait_send()

    return pl.pallas_call(
        body, name="all_gather_weights",
        in_specs=[ANY] * n, out_specs=tuple([ANY] * n),
        out_shape=tuple(jax.ShapeDtypeStruct((4,) + s.shape, s.dtype) for s in shards),
        scratch_shapes=[pltpu.SemaphoreType.DMA((6 * n,)), pltpu.SemaphoreType.DMA((6 * n,)),
                        pltpu.SemaphoreType.DMA((n,))],
    )(*shards)


def _exchange_halves(grads, name):
    n = len(grads)

    def body(*refs):
        ins, outs = refs[:n], refs[n:2 * n]
        send_sems, recv_sems = refs[2 * n:]
        x, y, c, _ = _place()

        def copy(a):
            h = ins[a].shape[1] // 2
            return pltpu.make_async_remote_copy(
                src_ref=ins[a].at[:, pl.ds((1 - c) * h, h), :], dst_ref=outs[a],
                send_sem=send_sems.at[a], recv_sem=recv_sems.at[a], device_id=(x, y, 1 - c), device_id_type=MESH)

        for a in range(n):
            copy(a).start()
        for a in range(n):
            copy(a).wait_recv()
        for a in range(n):
            copy(a).wait_send()

    return pl.pallas_call(
        body, name=name,
        in_specs=[ANY] * n, out_specs=tuple([ANY] * n),
        out_shape=tuple(jax.ShapeDtypeStruct((4, g.shape[1] // 2, g.shape[2]), g.dtype) for g in grads),
        scratch_shapes=[pltpu.SemaphoreType.DMA((n,)), pltpu.SemaphoreType.DMA((n,))],
    )(*grads)


def _scatter_chips(parts, name):
    n = len(parts)

    def body(*refs):
        ins, outs = refs[:n], refs[n:2 * n]
        send_sems, recv_sems = refs[2 * n:]
        x, y, c, chips = _place()

        def copy(a, j, chip_idx, to):
            return pltpu.make_async_remote_copy(
                src_ref=ins[a].at[chip_idx], dst_ref=outs[a].at[j],
                send_sem=send_sems.at[a * 3 + j], recv_sem=recv_sems.at[a * 3 + j], device_id=to, device_id_type=MESH)

        for a in range(n):
            for j, (px, py) in enumerate(chips):
                copy(a, j, 2 * px + py, (px, py, c)).start()
        for a in range(n):
            for j, (px, py) in enumerate(chips):
                copy(a, j, 2 * px + py, (px, py, c)).wait_recv()
        for a in range(n):
            for j, (px, py) in enumerate(chips):
                copy(a, j, 2 * px + py, (px, py, c)).wait_send()

    return pl.pallas_call(
        body, name=name,
        in_specs=[ANY] * n, out_specs=tuple([ANY] * n),
        out_shape=tuple(jax.ShapeDtypeStruct((3,) + p.shape[1:], p.dtype) for p in parts),
        scratch_shapes=[pltpu.SemaphoreType.DMA((3 * n,)), pltpu.SemaphoreType.DMA((3 * n,))],
    )(*parts)


HBM = pl.BlockSpec(memory_space=pltpu.HBM)
SEM = pl.BlockSpec(memory_space=pltpu.SEMAPHORE)
EFFECT = pltpu.SideEffectType.DATAFLOW_SIDE_EFFECTING


def _in_hbm(a):
    return pltpu.with_memory_space_constraint(a, pltpu.HBM)


def _split_copy_calls(name, srcs, lands, n_copies, make_copies):
    ns, nl = len(srcs), len(lands)
    nb = ns + nl

    def start():
        def body(*refs):
            copies = make_copies(refs[:ns], refs[ns:nb], refs[nb], refs[nb + 1])
            for cp in copies:
                cp.start()
            token = refs[-1]
            token[...] = jnp.zeros_like(token)

        bufs = [_in_hbm(a) for a in list(srcs) + list(lands)]
        out = pl.pallas_call(
            body, name=name + "_start",
            out_shape=(pltpu.SemaphoreType.DMA((n_copies,)), pltpu.SemaphoreType.DMA((n_copies,)),
                       *[pltpu.HBM(a.shape, a.dtype) for a in bufs], jax.ShapeDtypeStruct((8, 128), F32)),
            in_specs=[HBM] * nb, out_specs=(SEM, SEM, *[HBM] * nb, pl.BlockSpec(memory_space=pltpu.VMEM)),
            input_output_aliases={i: 2 + i for i in range(nb)},
            compiler_params=pltpu.CompilerParams(has_side_effects=EFFECT),
        )(*bufs)
        return dict(send=out[0], recv=out[1], bufs=list(out[2:2 + nb]), token=out[-1])

    def wait(state, after):
        def body(*refs):
            copies = make_copies(refs[:ns], refs[ns:nb], refs[nb], refs[nb + 1])
            for cp in copies:
                cp.wait_send()
            for cp in copies:
                cp.wait_recv()

        bufs = state["bufs"]
        out = pl.pallas_call(
            body, name=name + "_wait",
            out_shape=tuple(pltpu.HBM(a.shape, a.dtype) for a in bufs),
            in_specs=[HBM] * nb + [SEM, SEM, pl.BlockSpec(memory_space=pl.ANY)], out_specs=tuple([HBM] * nb),
            input_output_aliases={i: i for i in range(nb)},
            compiler_params=pltpu.CompilerParams(has_side_effects=EFFECT),
        )(*bufs, state["send"], state["recv"], after)
        return list(out[:ns]), list(out[ns:])

    return start, wait


def _scatter_chips_split(name, parts):
    n = len(parts)
    lands = [lax.empty((3,) + p.shape[1:], p.dtype) for p in parts]

    def make_copies(srcs, lnds, send_sems, recv_sems):
        _, _, c, chips = _place()
        return [pltpu.make_async_remote_copy(
            src_ref=srcs[a].at[2 * px + py], dst_ref=lnds[a].at[j], send_sem=send_sems.at[a * 3 + j],
            recv_sem=recv_sems.at[a * 3 + j], device_id=(px, py, c), device_id_type=MESH)
            for a in range(n) for j, (px, py) in enumerate(chips)]

    return _split_copy_calls(name, parts, lands, 3 * n, make_copies)


def _gather_chips_split(name, shards):
    n = len(shards)
    lands = [lax.empty((4,) + s.shape, s.dtype) for s in shards]

    def make_copies(srcs, lnds, send_sems, recv_sems):
        x, y, c, chips = _place()
        out = []
        for a in range(n):
            h = srcs[a].shape[0] // 2
            for j, (px, py) in enumerate(chips):
                out.append(pltpu.make_async_remote_copy(
                    src_ref=srcs[a].at[pl.ds(c * h, h), :], dst_ref=lnds[a].at[2 * x + y, pl.ds(c * h, h), :],
                    send_sem=send_sems.at[a * 3 + j], recv_sem=recv_sems.at[a * 3 + j],
                    device_id=(px, py, c), device_id_type=MESH))
        return out

    return _split_copy_calls(name, shards, lands, 3 * n, make_copies)


def _gather_finish(shards, gathered):
    n = len(shards)

    def body(*refs):
        ins, got = refs[:n], refs[n:2 * n]
        outs = refs[2 * n:3 * n]
        send_sems, recv_sems, local_sems = refs[3 * n:]
        del got
        x, y, c, chips = _place()
        me = 2 * x + y

        def copy(a, j, chip_idx, which):
            h = ins[a].shape[0] // 2
            rows = outs[a].at[chip_idx, pl.ds(which * h, h), :]
            return pltpu.make_async_remote_copy(
                src_ref=rows, dst_ref=rows, send_sem=send_sems.at[a * 3 + j], recv_sem=recv_sems.at[a * 3 + j],
                device_id=(x, y, 1 - c), device_id_type=MESH)

        for a in range(n):
            pltpu.make_async_copy(ins[a], outs[a].at[me], local_sems.at[a]).start()
            for j, (px, py) in enumerate(chips):
                copy(a, j, 2 * px + py, c).start()
        for a in range(n):
            for j, (px, py) in enumerate(chips):
                copy(a, j, 2 * px + py, 1 - c).wait_recv()
        for a in range(n):
            pltpu.make_async_copy(ins[a], outs[a].at[me], local_sems.at[a]).wait()
            for j, (px, py) in enumerate(chips):
                copy(a, j, 2 * px + py, c).wait_send()

    return pl.pallas_call(
        body, name="gather_finish",
        in_specs=[ANY] * (2 * n), out_specs=tuple([ANY] * n),
        out_shape=tuple(jax.ShapeDtypeStruct(g.shape, g.dtype) for g in gathered),
        input_output_aliases={n + i: i for i in range(n)},
        scratch_shapes=[pltpu.SemaphoreType.DMA((3 * n,)), pltpu.SemaphoreType.DMA((3 * n,)),
                        pltpu.SemaphoreType.DMA((n,))],
    )(*shards, *gathered)


def _join_halves(halves):
    n = len(halves)

    def body(*refs):
        ins, outs = refs[:n], refs[n:2 * n]
        send_sems, recv_sems, local_sems = refs[2 * n:]
        x, y, c, _ = _place()

        def rows(a):
            h = ins[a].shape[0]
            return outs[a].at[pl.ds(c * h, h), :]

        def copy(a):
            return pltpu.make_async_remote_copy(
                src_ref=ins[a], dst_ref=rows(a), send_sem=send_sems.at[a], recv_sem=recv_sems.at[a],
                device_id=(x, y, 1 - c), device_id_type=MESH)

        for a in range(n):
            pltpu.make_async_copy(ins[a], rows(a), local_sems.at[a]).start()
            copy(a).start()
        for a in range(n):
            h = ins[a].shape[0]
            other = outs[a].at[pl.ds((1 - c) * h, h), :]
            pltpu.make_async_remote_copy(src_ref=ins[a], dst_ref=other, send_sem=send_sems.at[a],
                                         recv_sem=recv_sems.at[a], device_id=(x, y, 1 - c),
                                         device_id_type=MESH).wait_recv()
        for a in range(n):
            copy(a).wait_send()
            pltpu.make_async_copy(ins[a], rows(a), local_sems.at[a]).wait()

    return pl.pallas_call(
        body, name="rs_join_halves",
        in_specs=[ANY] * n, out_specs=tuple([ANY] * n),
        out_shape=tuple(jax.ShapeDtypeStruct((2 * p.shape[0], p.shape[1]), p.dtype) for p in halves),
        scratch_shapes=[pltpu.SemaphoreType.DMA((n,)), pltpu.SemaphoreType.DMA((n,)), pltpu.SemaphoreType.DMA((n,))],
    )(*halves)


def _all_reduce_small(sm):
    r, w = sm.shape

    def body(sm_ref, o_ref, buf, send_sems, recv_sems):
        x, y, c, _ = _place()
        me = 4 * x + 2 * y + c
        buf[me] = sm_ref[...]
        rel = [(dx, dy, dc) for dx in (0, 1) for dy in (0, 1) for dc in (0, 1)][1:]

        def copy(k, slot, to):
            return pltpu.make_async_remote_copy(
                src_ref=sm_ref, dst_ref=buf.at[slot], send_sem=send_sems.at[k], recv_sem=recv_sems.at[k],
                device_id=to, device_id_type=MESH)

        peers = []
        for k, (dx, dy, dc) in enumerate(rel):
            px = 1 - x if dx else x
            py = 1 - y if dy else y
            pc = 1 - c if dc else c
            peers.append((px, py, pc))
            copy(k, me, (px, py, pc)).start()
        for k, (px, py, pc) in enumerate(peers):
            copy(k, 4 * px + 2 * py + pc, (px, py, pc)).wait_recv()
        for k, (px, py, pc) in enumerate(peers):
            copy(k, me, (px, py, pc)).wait_send()
        acc = buf[0]
        for d in range(1, 8):
            acc = acc + buf[d]
        o_ref[...] = acc

    vm = pl.BlockSpec(memory_space=pltpu.VMEM)
    return pl.pallas_call(
        body, name="all_reduce_small", in_specs=[vm], out_specs=vm,
        out_shape=jax.ShapeDtypeStruct((r, w), F32),
        scratch_shapes=[pltpu.VMEM((8, r, w), F32), pltpu.SemaphoreType.DMA((7,)), pltpu.SemaphoreType.DMA((7,))],
    )(sm)


class _LocalWeights:
    def __init__(self, w):
        self.w = w
        self.g = {}

    def first(self):
        return self.w

    def rest(self, after):
        del after
        return self.w

    def grads(self, tag, g):
        del tag
        self.g.update(g)
        return None


def _local_step(x3, mem3, pos2, target3, small, comm):
    bsz, seq, d = x3.shape
    mlen = mem3.shape[1]
    t = bsz * seq
    ds = d // 4
    w = comm.first()
    x = x3.reshape(t, d)
    mem = mem3.reshape(bsz * mlen, d)
    target = target3.reshape(t, d)
    pos = pos2.reshape(t, 1)
    qg_t = jnp.tile(small["sw_q_norm_g"], (1, SW_HEADS))
    kg_t = jnp.tile(small["sw_k_norm_g"], (1, SW_KV_HEADS))

    hn1 = _rms_fwd(x, small["norm1_g"], name="rms1_fwd")
    proj_hg = _mm(hn1, w["w_in_hg"], NN, t, HG_COLS, d, name="proj_hg", tk=d, after=(w.get("token"),))[0]
    proj_sw = _mm(hn1, w["w_in_sw"], NN, t, SW_COLS, d, name="proj_sw", tk=d)[0]
    y_mix, o_hg, states = _hg_fwd(proj_hg, small["hg_lower_bounds"], small["hg_norm_g"], bsz, seq, y_width=1024)
    y_mix = _sw_fwd(proj_sw, pos, qg_t, kg_t, small["sw_sinks"], y_mix, bsz, seq)
    w_in_hg, w_in_sw = w["w_in_hg"], w["w_in_sw"]
    w = comm.rest(y_mix)
    ff = w["down"].shape[0]
    ffs = ff // 4
    h1 = _mm(y_mix, w["w_out"], NN, t, d, 1024, name="out_proj", tk=1024, extras=(x,),
             epilogue=lambda acc, res: (acc + res,))[0]
    hn2 = _rms_fwd(h1, small["norm2_g"], name="rms2_fwd")
    mn = _rms_fwd(mem, small["mem_norm_g"], name="rms_mem_fwd")
    qx = _mm(hn2, w["wq"], NN, t, 512, d, name="xa_q", tk=d)[0]
    kvx = _mm(mn, w["wkv"], NN, bsz * mlen, 1024, d, name="xa_kv", tk=d)[0]
    ox = _xa_fwd(qx, kvx, small["xa_q_norm_g"], small["xa_k_norm_g"], bsz, seq, mlen)
    h2 = _mm(ox, w["wo"], NN, t, d, 512, name="xa_o", tn=ds, tk=512, extras=(h1,),
             b_spec=pl.BlockSpec((None, 512, ds), lambda i, j, kk: (j, 0, 0)),
             epilogue=lambda acc, res: (acc + res,))[0]
    hn3 = _rms_fwd(h2, small["norm3_g"], name="rms3_fwd")

    def relu_sq(acc):
        a = jnp.maximum(acc, 0.0)
        return a, a * a

    act, act2 = _mm(hn3, w["up"], NN, t, ff, d, name="mlp_up", tn=ffs, tk=d,
                    b_spec=pl.BlockSpec((None, d, ffs), lambda i, j, kk: (j, 0, 0)),
                    epilogue=relu_sq, out_dtypes=(_MXU_DTYPE, _MXU_DTYPE))
    inv_d = 1.0 / d
    dy = _mm(act2, w["down"], NN, t, d, ff, name="mlp_down", extras=(h2, target),
             epilogue=lambda acc, res, tgt: ((acc + res - tgt) * inv_d,))[0]
    loss_row = _loss_sum(dy, d)

    dz = _mm(dy, w["down"], NT, t, ff, d, name="d_act", tk=d, extras=(act,),
             epilogue=lambda acc, a: (acc * (2.0 * a.astype(F32)),), out_dtypes=(_MXU_DTYPE,))[0]
    g_down = _mm(act2, dy, TN, ff, d, t, name="g_down")[0]
    g_up = _mm(hn3, dz, TN, d, ff, t, name="g_up", tn=ffs,
               out_shape=(jax.ShapeDtypeStruct((4, d, ffs), F32),),
               out_spec=(pl.BlockSpec((None, min(1024, d), ffs), lambda i, j, kk: (j, i, 0)),))[0]
    tok = comm.grads("mlp", dict(up=g_up, down=g_down))
    dhn3 = _mm(dz, w["up"], NT, t, d, ff, name="d_hn3", tk=ffs, after=(tok,),
               b_spec=pl.BlockSpec((None, min(1024, d), ffs), lambda i, j, kk: (kk, j, 0)))[0]
    dh2, g_norm3 = _rms_bwd(h2, small["norm3_g"], dhn3, dy, name="rms3_bwd")
    d_ox = _mm(dh2, w["wo"], NT, t, 512, d, name="d_ox", tk=ds,
               b_spec=pl.BlockSpec((None, 512, ds), lambda i, j, kk: (kk, 0, 0)))[0]
    g_wo = _mm(ox, dh2, TN, 512, d, t, name="g_wo", tn=ds,
               out_shape=(jax.ShapeDtypeStruct((4, 512, ds), F32),),
               out_spec=(pl.BlockSpec((None, 512, ds), lambda i, j, kk: (j, 0, 0)),))[0]
    d_qx, d_kvx, g_xq, g_xk = _xa_bwd(qx, kvx, small["xa_q_norm_g"], small["xa_k_norm_g"], d_ox, bsz, seq, mlen)
    g_wq = _mm(hn2, d_qx, TN, d, 512, t, name="g_wq")[0]
    g_wkv = _mm(mn, d_kvx, TN, d, 1024, bsz * mlen, name="g_wkv")[0]
    dhn2 = _mm(d_qx, w["wq"], NT, t, d, 512, name="d_hn2", tk=512)[0]
    dmn = _mm(d_kvx, w["wkv"], NT, bsz * mlen, d, 1024, name="d_mn", tk=1024)[0]
    dh1, g_norm2 = _rms_bwd(h1, small["norm2_g"], dhn2, dh2, name="rms2_bwd")
    _, g_memn = _rms_bwd(mem, small["mem_norm_g"], dmn, None, name="rms_mem_bwd")
    g_wout = _mm(y_mix, dh1, TN, 1024, d, t, name="g_wout")[0]
    tok = comm.grads("mid", dict(w_out=g_wout, wq=g_wq, wkv=g_wkv, wo=g_wo))
    d_mix = _mm(dh1, w["w_out"], NT, t, 1024, d, name="d_mix", tk=d, after=(tok,))[0]
    dproj_sw, g_swq, g_swk, g_sinks = _sw_bwd(proj_sw, pos, qg_t, kg_t, small["sw_sinks"], y_mix, d_mix, bsz, seq)
    dproj_hg, g_lb, g_hgn = _hg_bwd(proj_hg, small["hg_lower_bounds"], small["hg_norm_g"], o_hg, states, d_mix, bsz, seq)
    g_in_hg = _mm(hn1, dproj_hg, TN, d, HG_COLS, t, name="g_in_hg")[0]
    g_in_sw = _mm(hn1, dproj_sw, TN, d, SW_COLS, t, name="g_in_sw")[0]
    comm.grads("in", dict(w_in_hg=g_in_hg, w_in_sw=g_in_sw))
    dhn1_a = _mm(dproj_hg, w_in_hg, NT, t, d, HG_COLS, name="d_hn1_hg", tk=1024)[0]
    dhn1 = _mm(dproj_sw, w_in_sw, NT, t, d, SW_COLS, name="d_hn1_sw", tk=SW_COLS, extras=(dhn1_a,),
               epilogue=lambda acc, prev: (acc + prev,))[0]
    grad_x, g_norm1 = _rms_bwd(x, small["norm1_g"], dhn1, dh1, name="rms1_bwd")

    g_small = dict(norm1_g=g_norm1, hg_lower_bounds=g_lb, hg_norm_g=g_hgn, sw_q_norm_g=g_swq, sw_k_norm_g=g_swk,
                   sw_sinks=g_sinks[:, 0:SW_HEADS], norm2_g=g_norm2, mem_norm_g=g_memn, xa_q_norm_g=g_xq,
                   xa_k_norm_g=g_xk, norm3_g=g_norm3)
    return loss_row, grad_x.reshape(bsz, seq, d), g_small


SMALL_NAMES = ("norm1_g", "hg_lower_bounds", "hg_norm_g", "sw_q_norm_g", "sw_k_norm_g", "sw_sinks", "norm2_g",
               "mem_norm_g", "xa_q_norm_g", "xa_k_norm_g", "norm3_g")
BIG_NAMES = ("w_in", "w_out", "xa_wq", "xa_wkv", "xa_wo", "mlp_up", "mlp_down")
WEIGHT_ORDER = ("norm1_g", "w_in", "hg_lower_bounds", "hg_norm_g", "sw_q_norm_g", "sw_k_norm_g", "sw_sinks", "w_out",
                "norm2_g", "mem_norm_g", "xa_wq", "xa_wkv", "xa_q_norm_g", "xa_k_norm_g", "xa_wo", "norm3_g",
                "mlp_up", "mlp_down")


def _head_major_blocks():
    return [kind * HG_HEADS + h for h in range(HG_HEADS) for kind in range(4)]


def _permute_col_blocks(a, blocks):
    return jnp.concatenate([a[:, 128 * b:128 * (b + 1)] for b in blocks], axis=1)


def _pack_rows(vals, width):
    starts, at = [], 0
    for v in vals:
        starts.append(at)
        at += v.shape[0]
    total = at + (-at) % 8
    out = None
    for v, s in zip(vals, starts):
        placed = jnp.pad(v, ((s, total - s - v.shape[0]), (0, width - v.shape[1])))
        out = placed if out is None else out + placed
    return out, starts


class _MeshWeights:
    LATE = ("w_out", "xa_wq", "xa_wkv", "xa_wo", "mlp_up", "mlp_down")

    def __init__(self, shards, d, ff):
        self.shards, self.d, self.ff = shards, d, ff
        self.c_idx = lax.axis_index("c").astype(jnp.int32).reshape(1)
        self.me_idx = (2 * lax.axis_index("x") + lax.axis_index("y")).astype(jnp.int32).reshape(1)
        self.pending = []
        self.halves = {}

    def first(self):
        (g_in,) = _all_gather_weights([self.shards["w_in"]])
        start, self.late_wait = _gather_chips_split("gather_late", [self.shards[n] for n in self.LATE])
        self.late_state = start()
        full = jnp.concatenate([g_in[k] for k in range(4)], axis=1)
        return dict(w_in_hg=_permute_col_blocks(full, _head_major_blocks()), w_in_sw=full[:, HG_COLS:],
                    token=self.late_state["token"])

    def rest(self, after):
        srcs, lands = self.late_wait(self.late_state, after)
        g_out, g_q, g_kv, g_o, g_up, g_dn = _gather_finish(srcs, lands)
        d = self.d
        return dict(w_out=g_out.reshape(-1, d), wq=g_q.reshape(d, -1), wkv=g_kv.reshape(d, -1), wo=g_o, up=g_up,
                    down=g_dn.reshape(self.ff, d))

    def _chip_partials(self, tag, names, arrays):
        recv = _exchange_halves(arrays, "rs_exchange_" + tag)
        return [_add_halves(g, r, self.c_idx, name="rs_add_halves_" + n) for n, g, r in zip(names, arrays, recv)]

    def grads(self, tag, g):
        d, ff = self.d, self.ff
        if tag == "mlp":
            names, arrays = ("mlp_up", "mlp_down"), [g["up"], g["down"].reshape(4, ff // 4, d)]
        elif tag == "mid":
            names = ("w_out", "xa_wq", "xa_wkv", "xa_wo")
            arrays = [g["w_out"].reshape(4, -1, d), g["wq"].reshape(4, d // 4, -1), g["wkv"].reshape(4, d // 4, -1), g["wo"]]
        else:
            self.g_in = g
            return None
        parts = self._chip_partials(tag, names, arrays)
        start, wait = _scatter_chips_split("rs_scatter_" + tag, parts)
        state = start()
        self.pending.append((names, wait, state))
        return state["token"]

    def finish(self):
        inv = [int(b) for b in np.argsort(_head_major_blocks())]
        full = jnp.concatenate([_permute_col_blocks(self.g_in["w_in_hg"], inv), self.g_in["w_in_sw"]], axis=1)
        ws = full.shape[1] // 4
        parts = self._chip_partials("in", ("w_in",), [jnp.stack([full[:, ws * k:ws * (k + 1)] for k in range(4)])])
        (recv,) = _scatter_chips(parts, "rs_scatter_in")
        self.halves["w_in"] = _add_chips(parts[0], recv, self.me_idx, name="rs_add_chips_w_in")
        for names, wait, state in self.pending:
            srcs, lands = wait(state, self.halves["w_in"])
            for n, p, r in zip(names, srcs, lands):
                self.halves[n] = _add_chips(p, r, self.me_idx, name="rs_add_chips_" + n)
        return dict(zip(BIG_NAMES, _join_halves([self.halves[n] for n in BIG_NAMES])))


def kernel(x, mem, positions, norm1_g, w_in, hg_lower_bounds, hg_norm_g, sw_q_norm_g, sw_k_norm_g, sw_sinks, w_out, norm2_g, mem_norm_g, xa_wq, xa_wkv, xa_q_norm_g, xa_k_norm_g, xa_wo, norm3_g, mlp_up, mlp_down, loss_target, m_norm1_g, m_w_in, m_hg_lower_bounds, m_hg_norm_g, m_sw_q_norm_g, m_sw_k_norm_g, m_sw_sinks, m_w_out, m_norm2_g, m_mem_norm_g, m_xa_wq, m_xa_wkv, m_xa_q_norm_g, m_xa_k_norm_g, m_xa_wo, m_norm3_g, m_mlp_up, m_mlp_down, v_norm1_g, v_w_in, v_hg_lower_bounds, v_hg_norm_g, v_sw_q_norm_g, v_sw_k_norm_g, v_sw_sinks, v_w_out, v_norm2_g, v_mem_norm_g, v_xa_wq, v_xa_wkv, v_xa_q_norm_g, v_xa_k_norm_g, v_xa_wo, v_norm3_g, v_mlp_up, v_mlp_down):
    given = dict(locals())
    weights = {n: given[n] for n in WEIGHT_ORDER}
    moms = {n: given["m_" + n] for n in WEIGHT_ORDER}
    vars_ = {n: given["v_" + n] for n in WEIGHT_ORDER}
    d = x.shape[-1]
    ff = mlp_down.shape[1] * 4
    small = {n: weights[n] for n in SMALL_NAMES}

    comm = _MeshWeights({n: weights[n][0].astype(_MXU_DTYPE) for n in BIG_NAMES}, d, ff)
    loss_row, grad_x, g_small = _local_step(x, mem, positions, loss_target, small, comm)
    big_grads = comm.finish()

    packed, starts = _pack_rows([g_small[n] for n in SMALL_NAMES] + [loss_row], 1024)
    summed = _all_reduce_small(packed)
    small_grads = {}
    for n, s in zip(SMALL_NAMES, starts):
        r, c = weights[n].shape
        small_grads[n] = summed[s:s + r, 0:c]
    loss = summed[starts[-1], 0]

    grads, deltas, new_m, new_v = {}, {}, {}, {}
    for n in BIG_NAMES:
        shp = weights[n].shape
        g2 = big_grads[n]
        dl, mo, vo = _adamw_big(weights[n][0], g2, moms[n][0], vars_[n][0], name="adamw_" + n)
        grads[n], deltas[n], new_m[n], new_v[n] = (a.reshape(shp) for a in (g2, dl, mo, vo))
    sm_out = _adamw_small([weights[n] for n in SMALL_NAMES], [small_grads[n] for n in SMALL_NAMES],
                          [moms[n] for n in SMALL_NAMES], [vars_[n] for n in SMALL_NAMES])
    ns = len(SMALL_NAMES)
    for i, n in enumerate(SMALL_NAMES):
        grads[n], deltas[n], new_m[n], new_v[n] = small_grads[n], sm_out[i], sm_out[ns + i], sm_out[2 * ns + i]

    return (loss, grad_x, *[grads[n] for n in WEIGHT_ORDER], *[deltas[n] for n in WEIGHT_ORDER],
            *[new_m[n] for n in WEIGHT_ORDER], *[new_v[n] for n in WEIGHT_ORDER])
```

```python
import functools

import numpy as np
import jax
import jax.numpy as jnp
from jax import lax
from jax.experimental import pallas as pl
from jax.experimental.pallas import tpu as pltpu

F32 = jnp.float32
_MXU_DTYPE = jnp.bfloat16

EPS = 1e-6
HG_HEADS = 4
HG_D = 128
HG_CHUNK = 64
HG_LEVELS = (32, 16, 8, 4, 2, 1)
SW_HEADS = 8
SW_KV_HEADS = 2
SW_GROUP = SW_HEADS // SW_KV_HEADS
SW_HD = 64
SW_BLOCK = 128
ROPE_THETA = 500000.0
ROT_DIM = SW_HD // 4
XA_HEADS = 4
XA_HD = 128
HG_COLS = 4 * HG_HEADS * HG_D
SW_COLS = (SW_HEADS + 2 * SW_KV_HEADS) * SW_HD

ADAM_LR = 0.001
ADAM_B1 = 0.9
ADAM_B2 = 0.999
ADAM_EPS = 1e-08
ADAM_WD = 0.01
ADAM_STEP = 10

VMEM_LIMIT = 56 * 1024 * 1024
MESH = pl.DeviceIdType.MESH

NN = ((1,), (0,))
NT = ((1,), (1,))
TN = ((0,), (0,))


def _mx(v):
    return v.astype(_MXU_DTYPE)


def _dot(a, b, dims=NN):
    return lax.dot_general(_mx(a), _mx(b), (dims, ((), ())), preferred_element_type=F32)


def _split_dot(a, v, dims, parts):
    acc = None
    rest = v
    for p in range(parts):
        piece = _mx(rest)
        term = lax.dot_general(a, piece, (dims, ((), ())), preferred_element_type=F32)
        acc = term if acc is None else acc + term
        if p + 1 < parts:
            rest = rest - piece.astype(F32)
    return acc


def _params(sem):
    return pltpu.CompilerParams(dimension_semantics=sem, vmem_limit_bytes=VMEM_LIMIT)


def _mm(a, b, mode, m, n, k, *, name, tm=1024, tn=1024, tk=512, a_spec=None, b_spec=None, extras=(), epilogue=None,
        out_dtypes=(F32,), out_shape=None, out_spec=None, after=()):
    after = tuple(t for t in after if t is not None)
    tm, tn, tk = min(tm, m), min(tn, n), min(tk, k)
    assert m % tm == 0 and n % tn == 0 and k % tk == 0, (name, m, n, k, tm, tn, tk)
    gi, gj, gk = m // tm, n // tn, k // tk
    if a_spec is None:
        a_spec = (pl.BlockSpec((tk, tm), lambda i, j, kk: (kk, i)) if mode == TN
                  else pl.BlockSpec((tm, tk), lambda i, j, kk: (i, kk)))
    if b_spec is None:
        b_spec = (pl.BlockSpec((tn, tk), lambda i, j, kk: (j, kk)) if mode == NT
                  else pl.BlockSpec((tk, tn), lambda i, j, kk: (kk, j)))
    mn_spec = pl.BlockSpec((tm, tn), lambda i, j, kk: (i, j))
    if epilogue is None:
        epilogue = lambda acc: (acc,)
    n_ex, n_out = len(extras), len(out_dtypes)
    if out_shape is None:
        out_shape = tuple(jax.ShapeDtypeStruct((m, n), d) for d in out_dtypes)
        out_spec = tuple(mn_spec for _ in out_dtypes)

    n_after = len(after)

    def body(*refs):
        a_ref, b_ref = refs[0], refs[1]
        ex = refs[2:2 + n_ex]
        outs = refs[2 + n_ex + n_after:2 + n_ex + n_after + n_out]

        def finish(acc):
            res = epilogue(acc, *[e[...] for e in ex])
            for o, r in zip(outs, res):
                o[...] = r.astype(o.dtype)

        if gk == 1:
            finish(_dot(a_ref[...], b_ref[...], mode))
        else:
            acc_ref = refs[-1]
            kk = pl.program_id(2)

            @pl.when(kk == 0)
            def _():
                acc_ref[...] = jnp.zeros_like(acc_ref)

            acc_ref[...] += _dot(a_ref[...], b_ref[...], mode)

            @pl.when(kk == gk - 1)
            def _():
                finish(acc_ref[...])

    return pl.pallas_call(
        body, name=name, grid=(gi, gj, gk),
        in_specs=[a_spec, b_spec] + [mn_spec] * n_ex + [pl.BlockSpec(memory_space=pl.ANY)] * n_after,
        out_specs=out_spec, out_shape=out_shape,
        scratch_shapes=[pltpu.VMEM((tm, tn), F32)] if gk > 1 else [],
        compiler_params=_params(("parallel", "parallel", "arbitrary")),
    )(a, b, *extras, *after)


def _rms_fwd(x, g, *, name, tm=512):
    t, d = x.shape
    tm = min(tm, t)

    def body(x_ref, g_ref, o_ref):
        xv = x_ref[...]
        r = lax.rsqrt(jnp.mean(xv * xv, axis=1, keepdims=True) + EPS)
        o_ref[...] = (xv * r * g_ref[...]).astype(o_ref.dtype)

    return pl.pallas_call(
        body, name=name, grid=(t // tm,),
        in_specs=[pl.BlockSpec((tm, d), lambda i: (i, 0)), pl.BlockSpec((1, d), lambda i: (0, 0))],
        out_specs=pl.BlockSpec((tm, d), lambda i: (i, 0)),
        out_shape=jax.ShapeDtypeStruct((t, d), _MXU_DTYPE),
        compiler_params=_params(("parallel",)),
    )(x, g)


def _rms_bwd(x, g, dy, dres, *, name, tm=512):
    t, d = x.shape
    tm = min(tm, t)
    has_res = dres is not None

    def body(*refs):
        x_ref, g_ref, dy_ref = refs[:3]
        dx_ref, dg_ref = refs[-2:]
        xv, dyv = x_ref[...], dy_ref[...]
        r = lax.rsqrt(jnp.mean(xv * xv, axis=1, keepdims=True) + EPS)
        u = dyv * g_ref[...]
        dx = r * u - xv * (r * r * r) * jnp.mean(u * xv, axis=1, keepdims=True)
        if has_res:
            dx = dx + refs[3][...]
        dx_ref[...] = dx

        @pl.when(pl.program_id(0) == 0)
        def _():
            dg_ref[...] = jnp.zeros_like(dg_ref)

        dg_ref[...] += jnp.sum(dyv * xv * r, axis=0, keepdims=True)

    row = pl.BlockSpec((tm, d), lambda i: (i, 0))
    vec = pl.BlockSpec((1, d), lambda i: (0, 0))
    return pl.pallas_call(
        body, name=name, grid=(t // tm,),
        in_specs=[row, vec, row] + ([row] if has_res else []),
        out_specs=(row, vec),
        out_shape=(jax.ShapeDtypeStruct((t, d), F32), jax.ShapeDtypeStruct((1, d), F32)),
        compiler_params=_params(("arbitrary",)),
    )(*([x, g, dy] + ([dres] if has_res else [])))


def _hg_constants():
    c = HG_CHUNK
    t = np.arange(c)
    sums = [t[None, :] <= t[:, None]]
    masks = []
    for m in HG_LEVELS:
        base = (t // (2 * m)) * (2 * m)
        mid = base + m - 1
        second = (t - base) >= m
        upper = (t[None, :] > mid[:, None]) & (t[None, :] <= t[:, None])
        lower = (t[None, :] > t[:, None]) & (t[None, :] <= mid[:, None])
        sums.append(np.where(second[:, None], upper, lower))
        masks.append(second[:, None] & (~second)[None, :] & (base[:, None] == base[None, :]))
    return (np.concatenate(sums, axis=0).astype(np.float32), np.stack(masks).astype(np.float32))


def _hg_gates(blk, lbp):
    q, x, v, gl = blk[:, 0:128], blk[:, 128:256], blk[:, 256:384], blk[:, 384:512]
    mx = jnp.max(lbp, axis=0, keepdims=True)
    e = jnp.exp(lbp - mx)
    lb = e[0:1, :] / jnp.sum(e, axis=0, keepdims=True)
    sig = jax.nn.sigmoid(x)
    f = lb + (1.0 - lb) * sig
    return q, v, gl, lb, sig, f, 1.0 - f, jnp.log(f)


def _hg_fwd(proj, lbp, ng, bsz, seq, *, y_width):
    t = proj.shape[0]
    nc = seq // HG_CHUNK
    a_np, m_np = _hg_constants()
    a_all = jnp.asarray(a_np, _MXU_DTYPE)
    masks = jnp.asarray(m_np, F32)
    nl = len(HG_LEVELS)

    def body(p_ref, lb_ref, ng_ref, a_ref, m_ref, y_ref, o_ref, st_ref):
        a_mat = a_ref[...]
        ngv = ng_ref[...]
        lbp_v = lb_ref[...]

        def chunk(c, st):
            rows = pl.ds(pl.multiple_of(c * HG_CHUNK, HG_CHUNK), HG_CHUNK)
            q, v, gl, lb, sig, f, k, g = _hg_gates(p_ref[rows, :], lbp_v)
            e_all = _split_dot(a_mat, g, NN, 3)
            b = e_all[0:HG_CHUNK]
            st_ref[c] = st
            o = _dot(q * jnp.exp(b), st, NT)
            p = jnp.zeros((HG_CHUNK, HG_CHUNK), F32)
            for li in range(nl):
                e = jnp.exp(e_all[HG_CHUNK * (li + 1):HG_CHUNK * (li + 2)])
                p = p + m_ref[li] * _dot(q * e, k * e, NT)
            o = o + _dot(p, v) + jnp.sum(q * k, axis=1, keepdims=True) * v
            bl = b[HG_CHUNK - 1:HG_CHUNK, :]
            st_new = st * jnp.exp(bl) + _dot(v, k * jnp.exp(bl - b), TN)
            r = lax.rsqrt(jnp.mean(o * o, axis=1, keepdims=True) + EPS)
            o_ref[rows, :] = o
            y_ref[rows, :] = (o * r * ngv) * (gl * jax.nn.sigmoid(gl))
            return st_new

        lax.fori_loop(0, nc, chunk, jnp.zeros((HG_D, HG_D), F32))

    return pl.pallas_call(
        body, name="hgrn2_fwd", grid=(bsz, HG_HEADS),
        in_specs=[pl.BlockSpec((seq, 512), lambda b, h: (b, h)),
                  pl.BlockSpec((2, HG_D), lambda b, h: (0, h)),
                  pl.BlockSpec((1, HG_D), lambda b, h: (0, 0)),
                  pl.BlockSpec(a_all.shape, lambda b, h: (0, 0)),
                  pl.BlockSpec(masks.shape, lambda b, h: (0, 0, 0))],
        out_specs=(pl.BlockSpec((seq, HG_D), lambda b, h: (b, h)),
                   pl.BlockSpec((seq, HG_D), lambda b, h: (b, h)),
                   pl.BlockSpec((None, None, nc, HG_D, HG_D), lambda b, h: (b, h, 0, 0, 0))),
        out_shape=(jax.ShapeDtypeStruct((t, y_width), F32),
                   jax.ShapeDtypeStruct((t, HG_HEADS * HG_D), F32),
                   jax.ShapeDtypeStruct((bsz, HG_HEADS, nc, HG_D, HG_D), F32)),
        compiler_params=_params(("parallel", "parallel")),
    )(proj, lbp, ng, a_all, masks)


def _hg_bwd(proj, lbp, ng, o_all, states, dy, bsz, seq):
    t = proj.shape[0]
    nc = seq // HG_CHUNK
    a_np, m_np = _hg_constants()
    a_all = jnp.asarray(a_np, _MXU_DTYPE)
    masks = jnp.asarray(m_np, F32)
    nl = len(HG_LEVELS)
    cs = HG_CHUNK

    def body(p_ref, lb_ref, ng_ref, a_ref, m_ref, o_ref, st_ref, dy_ref, dp_ref, dlb_ref, dng_ref):
        a_mat = a_ref[...]
        ngv = ng_ref[...]
        lbp_v = lb_ref[...]
        last_row = lax.broadcasted_iota(jnp.int32, (cs, HG_D), 0) == cs - 1

        def chunk(i, carry):
            dst, dlb_acc, dng_acc = carry
            c = nc - 1 - i
            rows = pl.ds(pl.multiple_of(c * cs, cs), cs)
            q, v, gl, lb, sig, f, k, g = _hg_gates(p_ref[rows, :], lbp_v)
            o = o_ref[rows, :]
            dyv = dy_ref[rows, :]
            st = st_ref[c]
            e_all = _split_dot(a_mat, g, NN, 3)
            b = e_all[0:cs]
            eb = jnp.exp(b)
            bl = b[cs - 1:cs, :]
            ebl = jnp.exp(bl)
            ekd = jnp.exp(bl - b)
            qb, kd = q * eb, k * ekd
            sg = jax.nn.sigmoid(gl)
            r = lax.rsqrt(jnp.mean(o * o, axis=1, keepdims=True) + EPS)
            dgl = dyv * (o * r * ngv) * (sg * (1.0 + gl * (1.0 - sg)))
            u = dyv * (gl * sg) * ngv
            do = r * u - o * (r * r * r) * jnp.mean(u * o, axis=1, keepdims=True)
            dng_acc = dng_acc + jnp.sum(dyv * (gl * sg) * o * r, axis=0, keepdims=True)
            es, qm, km = [], [], []
            p = jnp.zeros((cs, cs), F32)
            for li in range(nl):
                e = jnp.exp(e_all[cs * (li + 1):cs * (li + 2)])
                es.append(e)
                qm.append(q * e)
                km.append(k * e)
                p = p + m_ref[li] * _dot(qm[li], km[li], NT)
            qk = jnp.sum(q * k, axis=1, keepdims=True)
            dp = _dot(do, v, NT)
            dv = _dot(p, do, TN) + qk * do + _dot(kd, dst, NT)
            dqb = _dot(do, st)
            dkd = _dot(v, dst)
            dq = dqb * eb
            dk = dkd * ekd
            db = dqb * qb - dkd * kd
            dbl = jnp.sum(dkd * kd, axis=0, keepdims=True) + jnp.sum(dst * st, axis=0, keepdims=True) * ebl
            de = [db + jnp.where(last_row, dbl, 0.0)]
            for li in range(nl):
                dpm = m_ref[li] * dp
                dqm = _dot(dpm, km[li])
                dkm = _dot(dpm, qm[li], TN)
                dq = dq + dqm * es[li]
                dk = dk + dkm * es[li]
                de.append(dqm * qm[li] + dkm * km[li])
            dpd = jnp.sum(do * v, axis=1, keepdims=True)
            dq = dq + dpd * k
            dk = dk + dpd * q
            dg = _split_dot(a_mat, jnp.concatenate(de, axis=0), TN, 2)
            df = dg / f - dk
            dx = df * (1.0 - lb) * sig * (1.0 - sig)
            dlb_acc = dlb_acc + jnp.sum(df * (1.0 - sig), axis=0, keepdims=True)
            dp_ref[rows, 0:128] = dq
            dp_ref[rows, 128:256] = dx
            dp_ref[rows, 256:384] = dv
            dp_ref[rows, 384:512] = dgl
            return dst * ebl + _dot(do, qb, TN), dlb_acc, dng_acc

        zrow = jnp.zeros((1, HG_D), F32)
        _, dlb, dng = lax.fori_loop(0, nc, chunk, (jnp.zeros((HG_D, HG_D), F32), zrow, zrow))
        mx = jnp.max(lbp_v, axis=0, keepdims=True)
        e = jnp.exp(lbp_v - mx)
        s0 = e[0:1, :] / jnp.sum(e, axis=0, keepdims=True)
        da0 = dlb * s0 * (1.0 - s0)
        bi = pl.program_id(1)
        first = jnp.logical_and(pl.program_id(0) == 0, bi == 0)

        @pl.when(bi == 0)
        def _():
            dlb_ref[...] = jnp.zeros_like(dlb_ref)

        @pl.when(first)
        def _():
            dng_ref[...] = jnp.zeros_like(dng_ref)

        dlb_ref[...] += jnp.concatenate([da0, -da0], axis=0)
        dng_ref[...] += dng

    return pl.pallas_call(
        body, name="hgrn2_bwd", grid=(HG_HEADS, bsz),
        in_specs=[pl.BlockSpec((seq, 512), lambda h, b: (b, h)),
                  pl.BlockSpec((2, HG_D), lambda h, b: (0, h)),
                  pl.BlockSpec((1, HG_D), lambda h, b: (0, 0)),
                  pl.BlockSpec(a_all.shape, lambda h, b: (0, 0)),
                  pl.BlockSpec(masks.shape, lambda h, b: (0, 0, 0)),
                  pl.BlockSpec((seq, HG_D), lambda h, b: (b, h)),
                  pl.BlockSpec((None, None, nc, HG_D, HG_D), lambda h, b: (b, h, 0, 0, 0)),
                  pl.BlockSpec((seq, HG_D), lambda h, b: (b, h))],
        out_specs=(pl.BlockSpec((seq, 512), lambda h, b: (b, h)),
                   pl.BlockSpec((2, HG_D), lambda h, b: (0, h)),
                   pl.BlockSpec((1, HG_D), lambda h, b: (0, 0))),
        out_shape=(jax.ShapeDtypeStruct((t, HG_COLS), F32),
                   jax.ShapeDtypeStruct((2, HG_HEADS * HG_D), F32),
                   jax.ShapeDtypeStruct((1, HG_D), F32)),
        compiler_params=_params(("arbitrary", "arbitrary")),
    )(proj, lbp, ng, a_all, masks, o_all, states, dy)


def _sw_constants():
    half = ROT_DIM // 2
    inv = (np.float32(ROPE_THETA) ** (-(np.arange(half, dtype=np.float32) * np.float32(2.0) / np.float32(ROT_DIM)))
           ).astype(np.float32)
    freq = np.zeros((1, 128), np.float32)
    sign = np.zeros((1, 128), np.float32)
    for h in range(2):
        freq[0, 64 * h:64 * h + half] = inv
        freq[0, 64 * h + half:64 * h + 2 * half] = inv
        sign[0, 64 * h:64 * h + half] = -1.0
        sign[0, 64 * h + half:64 * h + 2 * half] = 1.0
    seg = np.kron(np.eye(8, dtype=np.float32), np.full((64, 64), 1.0 / 64.0, np.float32))
    return freq, sign, seg


def _rope_tables(pos, freq, sign):
    ang = pos.astype(F32) * freq
    return jnp.cos(ang), jnp.sin(ang) * sign


def _tile_lanes(v, times):
    return v if times == 1 else jnp.concatenate([v] * times, axis=1)


def _swap_halves(v):
    w = v.shape[1]
    half = ROT_DIM // 2
    lane = lax.broadcasted_iota(jnp.int32, v.shape, 1) % SW_HD
    return jnp.where(lane < half, pltpu.roll(v, w - half, 1), jnp.where(lane < 2 * half, pltpu.roll(v, half, 1), 0.0))


def _sw_norm_rope(tv, gain, seg, cosv, sinv):
    w = tv.shape[1]
    ms = _split_dot_rhs(tv * tv, seg[0:w, 0:w])
    r = lax.rsqrt(ms + EPS)
    tn = tv * r * gain
    reps = w // 128
    return tn * _tile_lanes(cosv, reps) + _swap_halves(tn) * _tile_lanes(sinv, reps), r


def _split_dot_rhs(v, a):
    hi = _mx(v)
    lo = _mx(v - hi.astype(F32))
    return (lax.dot_general(hi, a, (NN, ((), ())), preferred_element_type=F32)
            + lax.dot_general(lo, a, (NN, ((), ())), preferred_element_type=F32))


def _sw_norm_rope_bwd(dt, tv, r, gain, seg, cosv, sinv):
    w = tv.shape[1]
    reps = w // 128
    dtn = dt * _tile_lanes(cosv, reps) + _swap_halves(dt * _tile_lanes(sinv, reps))
    u = dtn * gain
    dtv = r * u - tv * (r * r * r) * _split_dot_rhs(u * tv, seg[0:w, 0:w])
    return dtv, jnp.sum(dtn * tv * r, axis=0, keepdims=True)


def _sw_probs(qh, kp, kc, sink, first_block):
    scale = SW_HD ** -0.5
    qi = lax.broadcasted_iota(jnp.int32, (SW_BLOCK, SW_BLOCK), 0)
    kj = lax.broadcasted_iota(jnp.int32, (SW_BLOCK, SW_BLOCK), 1)
    ok_prev = jnp.logical_and(kj > qi, jnp.logical_not(first_block))
    ok_cur = kj <= qi
    sp = jnp.where(ok_prev, _dot(qh, kp, NT) * scale, -jnp.inf)
    sc = jnp.where(ok_cur, _dot(qh, kc, NT) * scale, -jnp.inf)
    m = jnp.maximum(jnp.maximum(jnp.max(sp, axis=1, keepdims=True), jnp.max(sc, axis=1, keepdims=True)), sink)
    pp, pc = jnp.exp(sp - m), jnp.exp(sc - m)
    es = jnp.exp(sink - m)
    den = jnp.sum(pp, axis=1, keepdims=True) + jnp.sum(pc, axis=1, keepdims=True) + es
    return pp / den, pc / den, es / den


def _sw_specs(nb):
    def cur(b, n):
        return b * nb + jnp.minimum(n, nb - 1)

    def prev(b, n):
        return b * nb + jnp.maximum(jnp.minimum(n, nb - 1) - 1, 0)

    return cur, prev


def _sw_fwd(proj, pos, qg, kg, sinks, y_in, bsz, seq):
    t = proj.shape[0]
    nb = seq // SW_BLOCK
    freq_np, sign_np, seg_np = _sw_constants()
    freq, sign = jnp.asarray(freq_np), jnp.asarray(sign_np)
    seg = jnp.asarray(seg_np, _MXU_DTYPE)
    cur, prev = _sw_specs(nb)

    def body(q_ref, kc_ref, kp_ref, vc_ref, vp_ref, pc_ref, pp_ref, qg_ref, kg_ref, sk_ref, fr_ref, sn_ref, seg_ref,
             yin_ref, y_ref):
        del yin_ref
        n = pl.program_id(1)
        segv = seg_ref[...]
        cos_c, sin_c = _rope_tables(pc_ref[...], fr_ref[...], sn_ref[...])
        cos_p, sin_p = _rope_tables(pp_ref[...], fr_ref[...], sn_ref[...])
        qr, _ = _sw_norm_rope(q_ref[...], qg_ref[...], segv, cos_c, sin_c)
        kcr, _ = _sw_norm_rope(kc_ref[...], kg_ref[...], segv, cos_c, sin_c)
        kpr, _ = _sw_norm_rope(kp_ref[...], kg_ref[...], segv, cos_p, sin_p)
        vc, vp = vc_ref[...], vp_ref[...]
        for h in range(SW_HEADS):
            kv = h // SW_GROUP
            ks = slice(SW_HD * kv, SW_HD * (kv + 1))
            pp, pc, _ = _sw_probs(qr[:, SW_HD * h:SW_HD * (h + 1)], kpr[:, ks], kcr[:, ks], sk_ref[0, h], n == 0)
            y_ref[:, SW_HD * h:SW_HD * (h + 1)] = _dot(pp, vp[:, ks]) + _dot(pc, vc[:, ks])

    rowq = pl.BlockSpec((SW_BLOCK, 512), lambda b, n: (cur(b, n), 0))
    full = lambda a: pl.BlockSpec(a.shape, lambda b, n: (0,) * a.ndim)
    yw = y_in.shape[1]
    return pl.pallas_call(
        body, name="swa_fwd", grid=(bsz, nb),
        in_specs=[rowq,
                  pl.BlockSpec((SW_BLOCK, 128), lambda b, n: (cur(b, n), 4)),
                  pl.BlockSpec((SW_BLOCK, 128), lambda b, n: (prev(b, n), 4)),
                  pl.BlockSpec((SW_BLOCK, 128), lambda b, n: (cur(b, n), 5)),
                  pl.BlockSpec((SW_BLOCK, 128), lambda b, n: (prev(b, n), 5)),
                  pl.BlockSpec((SW_BLOCK, 1), lambda b, n: (cur(b, n), 0)),
                  pl.BlockSpec((SW_BLOCK, 1), lambda b, n: (prev(b, n), 0)),
                  full(qg), full(kg),
                  pl.BlockSpec(memory_space=pltpu.SMEM),
                  full(freq), full(sign), full(seg),
                  pl.BlockSpec(memory_space=pl.ANY)],
        out_specs=pl.BlockSpec((SW_BLOCK, 512), lambda b, n: (cur(b, n), 1)),
        out_shape=jax.ShapeDtypeStruct((t, yw), F32),
        input_output_aliases={13: 0},
        compiler_params=_params(("parallel", "parallel")),
    )(proj, proj, proj, proj, proj, pos, pos, qg, kg, sinks, freq, sign, seg, y_in)


def _sw_bwd(proj, pos, qg, kg, sinks, y, dy, bsz, seq):
    t = proj.shape[0]
    nb = seq // SW_BLOCK
    freq_np, sign_np, seg_np = _sw_constants()
    freq, sign = jnp.asarray(freq_np), jnp.asarray(sign_np)
    seg = jnp.asarray(seg_np, _MXU_DTYPE)
    cur, prev = _sw_specs(nb)
    scale = SW_HD ** -0.5

    def body(q_ref, kc_ref, kp_ref, vc_ref, vp_ref, pc_ref, pp_ref, qg_ref, kg_ref, sk_ref, fr_ref, sn_ref, seg_ref,
             y_ref, dy_ref, dp_ref, dqg_ref, dkg_ref, dsk_ref,
             dq_car, dkv_car, dqr_s, dkc_s, dkp_s, dvc_s, dvp_s, gq_acc, gk_acc, sk_acc):
        b, n = pl.program_id(0), pl.program_id(1)
        first = jnp.logical_and(b == 0, n == 0)
        last = jnp.logical_and(b == pl.num_programs(0) - 1, n == nb)

        @pl.when(first)
        def _():
            gq_acc[...] = jnp.zeros_like(gq_acc)
            gk_acc[...] = jnp.zeros_like(gk_acc)
            sk_acc[...] = jnp.zeros_like(sk_acc)

        @pl.when(n < nb)
        def _():
            segv = seg_ref[...]
            cos_c, sin_c = _rope_tables(pc_ref[...], fr_ref[...], sn_ref[...])
            cos_p, sin_p = _rope_tables(pp_ref[...], fr_ref[...], sn_ref[...])
            qv, kcv, kpv = q_ref[...], kc_ref[...], kp_ref[...]
            qr, rq = _sw_norm_rope(qv, qg_ref[...], segv, cos_c, sin_c)
            kcr, rkc = _sw_norm_rope(kcv, kg_ref[...], segv, cos_c, sin_c)
            kpr, rkp = _sw_norm_rope(kpv, kg_ref[...], segv, cos_p, sin_p)
            vc, vp = vc_ref[...], vp_ref[...]
            dkc_s[...] = jnp.zeros_like(dkc_s)
            dkp_s[...] = jnp.zeros_like(dkp_s)
            dvc_s[...] = jnp.zeros_like(dvc_s)
            dvp_s[...] = jnp.zeros_like(dvp_s)
            lane = lax.broadcasted_iota(jnp.int32, (1, 128), 1)
            dsk = jnp.zeros((1, 128), F32)
            for h in range(SW_HEADS):
                kv = h // SW_GROUP
                ks = slice(SW_HD * kv, SW_HD * (kv + 1))
                hs = slice(SW_HD * h, SW_HD * (h + 1))
                qh = qr[:, hs]
                pp, pc, ps = _sw_probs(qh, kpr[:, ks], kcr[:, ks], sk_ref[0, h], n == 0)
                doh = dy_ref[:, hs]
                delta = jnp.sum(doh * y_ref[:, hs], axis=1, keepdims=True)
                dsp = pp * (_dot(doh, vp[:, ks], NT) - delta) * scale
                dsc = pc * (_dot(doh, vc[:, ks], NT) - delta) * scale
                dsk = dsk + jnp.where(lane == h, -jnp.sum(ps * delta), 0.0)
                dvp_s[:, ks] += _dot(pp, doh, TN)
                dvc_s[:, ks] += _dot(pc, doh, TN)
                dqr_s[:, hs] = _dot(dsp, kpr[:, ks]) + _dot(dsc, kcr[:, ks])
                dkp_s[:, ks] += _dot(dsp, qh, TN)
                dkc_s[:, ks] += _dot(dsc, qh, TN)
            dq, gq = _sw_norm_rope_bwd(dqr_s[...], qv, rq, qg_ref[...], segv, cos_c, sin_c)
            dkc, gkc = _sw_norm_rope_bwd(dkc_s[...], kcv, rkc, kg_ref[...], segv, cos_c, sin_c)
            dkp, gkp = _sw_norm_rope_bwd(dkp_s[...], kpv, rkp, kg_ref[...], segv, cos_p, sin_p)
            gq_acc[...] += gq
            gk_acc[...] += gkc + gkp
            sk_acc[...] += dsk

            @pl.when(n > 0)
            def _():
                dp_ref[:, 0:512] = dq_car[...]
                dp_ref[:, 512:640] = dkv_car[:, 0:128] + dkp
                dp_ref[:, 640:768] = dkv_car[:, 128:256] + dvp_s[...]

            dq_car[...] = dq
            dkv_car[:, 0:128] = dkc
            dkv_car[:, 128:256] = dvc_s[...]

        @pl.when(n == nb)
        def _():
            dp_ref[:, 0:512] = dq_car[...]
            dp_ref[:, 512:768] = dkv_car[...]

        @pl.when(last)
        def _():
            gq = gq_acc[...]
            acc = gq[:, 0:SW_HD]
            for h in range(1, SW_HEADS):
                acc = acc + gq[:, SW_HD * h:SW_HD * (h + 1)]
            dqg_ref[...] = acc
            gk = gk_acc[...]
            dkg_ref[...] = gk[:, 0:SW_HD] + gk[:, SW_HD:2 * SW_HD]
            dsk_ref[...] = sk_acc[...]

    rowq = pl.BlockSpec((SW_BLOCK, 512), lambda b, n: (cur(b, n), 0))
    full = lambda a: pl.BlockSpec(a.shape, lambda b, n: (0,) * a.ndim)

    def out_row(b, n):
        return b * nb + jnp.maximum(n - 1, 0)

    return pl.pallas_call(
        body, name="swa_bwd", grid=(bsz, nb + 1),
        in_specs=[rowq,
                  pl.BlockSpec((SW_BLOCK, 128), lambda b, n: (cur(b, n), 4)),
                  pl.BlockSpec((SW_BLOCK, 128), lambda b, n: (prev(b, n), 4)),
                  pl.BlockSpec((SW_BLOCK, 128), lambda b, n: (cur(b, n), 5)),
                  pl.BlockSpec((SW_BLOCK, 128), lambda b, n: (prev(b, n), 5)),
                  pl.BlockSpec((SW_BLOCK, 1), lambda b, n: (cur(b, n), 0)),
                  pl.BlockSpec((SW_BLOCK, 1), lambda b, n: (prev(b, n), 0)),
                  full(qg), full(kg),
                  pl.BlockSpec(memory_space=pltpu.SMEM),
                  full(freq), full(sign), full(seg),
                  pl.BlockSpec((SW_BLOCK, 512), lambda b, n: (cur(b, n), 1)),
                  pl.BlockSpec((SW_BLOCK, 512), lambda b, n: (cur(b, n), 1))],
        out_specs=(pl.BlockSpec((SW_BLOCK, SW_COLS), lambda b, n: (out_row(b, n), 0)),
                   pl.BlockSpec((1, SW_HD), lambda b, n: (0, 0)),
                   pl.BlockSpec((1, SW_HD), lambda b, n: (0, 0)),
                   pl.BlockSpec((1, 128), lambda b, n: (0, 0))),
        out_shape=(jax.ShapeDtypeStruct((t, SW_COLS), F32),
                   jax.ShapeDtypeStruct((1, SW_HD), F32),
                   jax.ShapeDtypeStruct((1, SW_HD), F32),
                   jax.ShapeDtypeStruct((1, 128), F32)),
        scratch_shapes=[pltpu.VMEM((SW_BLOCK, 512), F32), pltpu.VMEM((SW_BLOCK, 256), F32),
                        pltpu.VMEM((SW_BLOCK, 512), F32),
                        pltpu.VMEM((SW_BLOCK, 128), F32), pltpu.VMEM((SW_BLOCK, 128), F32),
                        pltpu.VMEM((SW_BLOCK, 128), F32), pltpu.VMEM((SW_BLOCK, 128), F32),
                        pltpu.VMEM((1, 512), F32), pltpu.VMEM((1, 128), F32), pltpu.VMEM((1, 128), F32)],
        compiler_params=_params(("arbitrary", "arbitrary")),
    )(proj, proj, proj, proj, proj, pos, pos, qg, kg, sinks, freq, sign, seg, y, dy)


def _head_rms(tv, gain):
    r = lax.rsqrt(jnp.mean(tv * tv, axis=1, keepdims=True) + EPS)
    return tv * r * gain, r


def _head_rms_bwd(dtn, tv, r, gain):
    u = dtn * gain
    return r * u - tv * (r * r * r) * jnp.mean(u * tv, axis=1, keepdims=True), jnp.sum(dtn * tv * r, axis=0, keepdims=True)


def _xa_probs(qn, kn):
    s = _dot(qn, kn, NT) * (XA_HD ** -0.5)
    e = jnp.exp(s - jnp.max(s, axis=1, keepdims=True))
    return e / jnp.sum(e, axis=1, keepdims=True)


def _xa_fwd(qx, kvx, qg, kg, bsz, seq, mlen, *, tq=512):
    t = qx.shape[0]
    tq = min(tq, seq)
    nq = seq // tq
    w = XA_HEADS * XA_HD

    def body(q_ref, kv_ref, qg_ref, kg_ref, o_ref):
        for h in range(XA_HEADS):
            hs = slice(XA_HD * h, XA_HD * (h + 1))
            qn, _ = _head_rms(q_ref[:, hs], qg_ref[...])
            kn, _ = _head_rms(kv_ref[:, hs], kg_ref[...])
            o_ref[:, hs] = _dot(_xa_probs(qn, kn), kv_ref[:, w + XA_HD * h:w + XA_HD * (h + 1)])

    vec = pl.BlockSpec((1, XA_HD), lambda b, i: (0, 0))
    return pl.pallas_call(
        body, name="xattn_fwd", grid=(bsz, nq),
        in_specs=[pl.BlockSpec((tq, w), lambda b, i: (b * nq + i, 0)),
                  pl.BlockSpec((mlen, 2 * w), lambda b, i: (b, 0)), vec, vec],
        out_specs=pl.BlockSpec((tq, w), lambda b, i: (b * nq + i, 0)),
        out_shape=jax.ShapeDtypeStruct((t, w), F32),
        compiler_params=_params(("parallel", "parallel")),
    )(qx, kvx, qg, kg)


def _xa_bwd(qx, kvx, qg, kg, do, bsz, seq, mlen, *, tq=512):
    t = qx.shape[0]
    tq = min(tq, seq)
    nq = seq // tq
    w = XA_HEADS * XA_HD
    scale = XA_HD ** -0.5

    def body(q_ref, kv_ref, qg_ref, kg_ref, do_ref, dq_ref, dkv_ref, dqg_ref, dkg_ref):
        b, i = pl.program_id(0), pl.program_id(1)

        @pl.when(jnp.logical_and(b == 0, i == 0))
        def _():
            dqg_ref[...] = jnp.zeros_like(dqg_ref)
            dkg_ref[...] = jnp.zeros_like(dkg_ref)

        @pl.when(i == 0)
        def _():
            dkv_ref[...] = jnp.zeros_like(dkv_ref)

        gq_sum = jnp.zeros((1, XA_HD), F32)
        gk_sum = jnp.zeros((1, XA_HD), F32)
        for h in range(XA_HEADS):
            hs = slice(XA_HD * h, XA_HD * (h + 1))
            vs = slice(w + XA_HD * h, w + XA_HD * (h + 1))
            qv, kv, vv = q_ref[:, hs], kv_ref[:, hs], kv_ref[:, vs]
            qn, rq = _head_rms(qv, qg_ref[...])
            kn, rk = _head_rms(kv, kg_ref[...])
            p = _xa_probs(qn, kn)
            doh = do_ref[:, hs]
            dp = _dot(doh, vv, NT)
            ds = p * (dp - jnp.sum(p * dp, axis=1, keepdims=True)) * scale
            dqv, gq = _head_rms_bwd(_dot(ds, kn), qv, rq, qg_ref[...])
            dkv, gk = _head_rms_bwd(_dot(ds, qn, TN), kv, rk, kg_ref[...])
            dq_ref[:, hs] = dqv
            dkv_ref[:, hs] += dkv
            dkv_ref[:, vs] += _dot(p, doh, TN)
            gq_sum = gq_sum + gq
            gk_sum = gk_sum + gk
        dqg_ref[...] += gq_sum
        dkg_ref[...] += gk_sum

    vec = pl.BlockSpec((1, XA_HD), lambda b, i: (0, 0))
    row = pl.BlockSpec((tq, w), lambda b, i: (b * nq + i, 0))
    mem = pl.BlockSpec((mlen, 2 * w), lambda b, i: (b, 0))
    return pl.pallas_call(
        body, name="xattn_bwd", grid=(bsz, nq),
        in_specs=[row, mem, vec, vec, row],
        out_specs=(row, mem, vec, vec),
        out_shape=(jax.ShapeDtypeStruct((t, w), F32), jax.ShapeDtypeStruct((bsz * mlen, 2 * w), F32),
                   jax.ShapeDtypeStruct((1, XA_HD), F32), jax.ShapeDtypeStruct((1, XA_HD), F32)),
        compiler_params=_params(("arbitrary", "arbitrary")),
    )(qx, kvx, qg, kg, do)


def _loss_sum(dy, d_model, *, tm=512):
    t, d = dy.shape
    tm = min(tm, t)
    steps = t // tm

    def body(dy_ref, o_ref, acc_ref):
        i = pl.program_id(0)

        @pl.when(i == 0)
        def _():
            acc_ref[...] = jnp.zeros_like(acc_ref)

        diff = dy_ref[...] * float(d_model)
        acc_ref[...] += jnp.sum(diff * diff, axis=0, keepdims=True)

        @pl.when(i == steps - 1)
        def _():
            o_ref[...] = jnp.zeros_like(o_ref) + 0.5 * jnp.sum(acc_ref[...]) / float(d_model)

    return pl.pallas_call(
        body, name="loss_sum", grid=(steps,),
        in_specs=[pl.BlockSpec((tm, d), lambda i: (i, 0))],
        out_specs=pl.BlockSpec((1, 128), lambda i: (0, 0)),
        out_shape=jax.ShapeDtypeStruct((1, 128), F32),
        scratch_shapes=[pltpu.VMEM((1, d), F32)],
        compiler_params=_params(("arbitrary",)),
    )(dy)


def _adamw_math(w, g, m, v):
    m = ADAM_B1 * m + (1.0 - ADAM_B1) * g
    v = ADAM_B2 * v + (1.0 - ADAM_B2) * (g * g)
    m_hat = m / (1.0 - ADAM_B1 ** ADAM_STEP)
    v_hat = v / (1.0 - ADAM_B2 ** ADAM_STEP)
    return -ADAM_LR * (m_hat / (jnp.sqrt(v_hat) + ADAM_EPS) + ADAM_WD * w), m, v


def _adamw_big(w, g, m, v, *, name, tr=256):
    r, c = w.shape
    tr = min(tr, r)

    def body(w_ref, g_ref, m_ref, v_ref, d_ref, mo_ref, vo_ref):
        d, mn, vn = _adamw_math(w_ref[...], g_ref[...], m_ref[...], v_ref[...])
        d_ref[...] = d
        mo_ref[...] = mn
        vo_ref[...] = vn

    spec = pl.BlockSpec((tr, c), lambda i: (i, 0))
    shp = jax.ShapeDtypeStruct((r, c), F32)
    return pl.pallas_call(
        body, name=name, grid=(r // tr,), in_specs=[spec] * 4, out_specs=(spec,) * 3, out_shape=(shp,) * 3,
        compiler_params=_params(("parallel",)),
    )(w, g, m, v)


def _adamw_small(ws, gs, ms, vs):
    n = len(ws)

    def body(*refs):
        for i in range(n):
            d, mn, vn = _adamw_math(refs[i][...], refs[n + i][...], refs[2 * n + i][...], refs[3 * n + i][...])
            refs[4 * n + i][...] = d
            refs[5 * n + i][...] = mn
            refs[6 * n + i][...] = vn

    shapes = tuple(jax.ShapeDtypeStruct(w.shape, F32) for w in ws)
    return pl.pallas_call(body, name="adamw_small", out_shape=shapes * 3)(*ws, *gs, *ms, *vs)


def _add_halves(g, recv, c_idx, *, name, tr=256):
    _, r, c = g.shape
    h = r // 2
    tr = min(tr, h)
    nt = h // tr

    def body(c_ref, g_ref, r_ref, o_ref):
        del c_ref
        o_ref[...] = g_ref[...] + r_ref[...]

    return pl.pallas_call(
        body, name=name,
        grid_spec=pltpu.PrefetchScalarGridSpec(
            num_scalar_prefetch=1, grid=(4, nt),
            in_specs=[pl.BlockSpec((None, tr, c), lambda k, i, cr: (k, cr[0] * nt + i, 0)),
                      pl.BlockSpec((None, tr, c), lambda k, i, cr: (k, i, 0))],
            out_specs=pl.BlockSpec((None, tr, c), lambda k, i, cr: (k, i, 0))),
        out_shape=jax.ShapeDtypeStruct((4, h, c), F32),
        compiler_params=_params(("parallel", "parallel")),
    )(c_idx, g, recv)


def _add_chips(p, recv, place_idx, *, name, tr=256):
    _, h, c = p.shape
    tr = min(tr, h)
    nt = h // tr

    def body(pi_ref, p_ref, r_ref, o_ref):
        del pi_ref
        o_ref[...] = ((p_ref[...] + r_ref[0]) + r_ref[1]) + r_ref[2]

    return pl.pallas_call(
        body, name=name,
        grid_spec=pltpu.PrefetchScalarGridSpec(
            num_scalar_prefetch=1, grid=(nt,),
            in_specs=[pl.BlockSpec((None, tr, c), lambda i, pi: (pi[0], i, 0)),
                      pl.BlockSpec((3, tr, c), lambda i, pi: (0, i, 0))],
            out_specs=pl.BlockSpec((tr, c), lambda i, pi: (pi[1] * nt + i, 0))),
        out_shape=jax.ShapeDtypeStruct((2 * h, c), F32),
        compiler_params=_params(("parallel",)),
    )(place_idx, p, recv)


def _place_shard(shard, place_idx, *, name, tr=256):
    r, c = shard.shape
    tr = min(tr, r)

    def body(pi_ref, s_ref, o_ref):
        del pi_ref
        o_ref[...] = s_ref[...]

    return pl.pallas_call(
        body, name=name,
        grid_spec=pltpu.PrefetchScalarGridSpec(
            num_scalar_prefetch=1, grid=(r // tr,),
            in_specs=[pl.BlockSpec((tr, c), lambda i, pi: (i, 0))],
            out_specs=pl.BlockSpec((None, tr, c), lambda i, pi: (pi[0], i, 0))),
        out_shape=jax.ShapeDtypeStruct((4, r, c), shard.dtype),
        compiler_params=_params(("parallel",)),
    )(place_idx, shard)


def _place():
    x, y, c = lax.axis_index("x"), lax.axis_index("y"), lax.axis_index("c")
    chips = [(1 - x, y), (x, 1 - y), (1 - x, 1 - y)]
    return x, y, c, chips


ANY = pl.BlockSpec(memory_space=pl.ANY)


def _all_gather_weights(shards, placed):
    n = len(shards)

    def body(*refs):
        ins, outs = refs[:n], refs[2 * n:3 * n]
        send_sems, recv_sems = refs[3 * n:]
        x, y, c, chips = _place()
        me = 2 * x + y

        def half(a, chip_idx, which):
            h = ins[a].shape[0] // 2
            return outs[a].at[chip_idx, pl.ds(which * h, h), :]

        def copy(a, j, chip_idx, which, to, src=None):
            return pltpu.make_async_remote_copy(
                src_ref=half(a, chip_idx, which) if src is None else src, dst_ref=half(a, chip_idx, which),
                send_sem=send_sems.at[a * 6 + j], recv_sem=recv_sems.at[a * 6 + j], device_id=to, device_id_type=MESH)

        for a in range(n):
            h = ins[a].shape[0] // 2
            for j, (px, py) in enumerate(chips):
                copy(a, j, me, c, (px, py, c), src=ins[a].at[pl.ds(c * h, h), :]).start()
        for a in range(n):
            for j, (px, py) in enumerate(chips):
                copy(a, j, 2 * px + py, c, (x, y, c)).wait_recv()
                copy(a, 3 + j, 2 * px + py, c, (x, y, 1 - c)).start()
        for a in range(n):
            for j, (px, py) in enumerate(chips):
                copy(a, 3 + j, 2 * px + py, 1 - c, (x, y, c)).wait_recv()
        for a in range(n):
            h = ins[a].shape[0] // 2
            for j, (px, py) in enumerate(chips):
                copy(a, j, me, c, (px, py, c), src=ins[a].at[pl.ds(c * h, h), :]).wait_send()
                copy(a, 3 + j, 2 * px + py, c, (x, y, 1 - c)).wait_send()

    return pl.pallas_call(
        body, name="all_gather_weights",
        in_specs=[ANY] * (2 * n), out_specs=tuple([ANY] * n),
        out_shape=tuple(jax.ShapeDtypeStruct(p.shape, p.dtype) for p in placed),
        input_output_aliases={n + i: i for i in range(n)},
        scratch_shapes=[pltpu.SemaphoreType.DMA((6 * n,)), pltpu.SemaphoreType.DMA((6 * n,))],
    )(*shards, *placed)


def _exchange_halves(grads, name):
    n = len(grads)

    def body(*refs):
        ins, outs = refs[:n], refs[n:2 * n]
        send_sems, recv_sems = refs[2 * n:]
        x, y, c, _ = _place()

        def copy(a):
            h = ins[a].shape[1] // 2
            return pltpu.make_async_remote_copy(
                src_ref=ins[a].at[:, pl.ds((1 - c) * h, h), :], dst_ref=outs[a],
                send_sem=send_sems.at[a], recv_sem=recv_sems.at[a], device_id=(x, y, 1 - c), device_id_type=MESH)

        for a in range(n):
            copy(a).start()
        for a in range(n):
            copy(a).wait_recv()
        for a in range(n):
            copy(a).wait_send()

    return pl.pallas_call(
        body, name=name,
        in_specs=[ANY] * n, out_specs=tuple([ANY] * n),
        out_shape=tuple(jax.ShapeDtypeStruct((4, g.shape[1] // 2, g.shape[2]), g.dtype) for g in grads),
        scratch_shapes=[pltpu.SemaphoreType.DMA((n,)), pltpu.SemaphoreType.DMA((n,))],
    )(*grads)


def _scatter_chips(parts, name):
    n = len(parts)

    def body(*refs):
        ins, outs = refs[:n], refs[n:2 * n]
        send_sems, recv_sems = refs[2 * n:]
        x, y, c, chips = _place()

        def copy(a, j, chip_idx, to):
            return pltpu.make_async_remote_copy(
                src_ref=ins[a].at[chip_idx], dst_ref=outs[a].at[j],
                send_sem=send_sems.at[a * 3 + j], recv_sem=recv_sems.at[a * 3 + j], device_id=to, device_id_type=MESH)

        for a in range(n):
            for j, (px, py) in enumerate(chips):
                copy(a, j, 2 * px + py, (px, py, c)).start()
        for a in range(n):
            for j, (px, py) in enumerate(chips):
                copy(a, j, 2 * px + py, (px, py, c)).wait_recv()
        for a in range(n):
            for j, (px, py) in enumerate(chips):
                copy(a, j, 2 * px + py, (px, py, c)).wait_send()

    return pl.pallas_call(
        body, name=name,
        in_specs=[ANY] * n, out_specs=tuple([ANY] * n),
        out_shape=tuple(jax.ShapeDtypeStruct((3,) + p.shape[1:], p.dtype) for p in parts),
        scratch_shapes=[pltpu.SemaphoreType.DMA((3 * n,)), pltpu.SemaphoreType.DMA((3 * n,))],
    )(*parts)


HBM = pl.BlockSpec(memory_space=pltpu.HBM)
SEM = pl.BlockSpec(memory_space=pltpu.SEMAPHORE)
EFFECT = pltpu.SideEffectType.DATAFLOW_SIDE_EFFECTING


def _in_hbm(a):
    return pltpu.with_memory_space_constraint(a, pltpu.HBM)


def _split_copy_calls(name, srcs, lands, n_copies, make_copies):
    ns, nl = len(srcs), len(lands)
    nb = ns + nl

    def start():
        def body(*refs):
            copies = make_copies(refs[:ns], refs[ns:nb], refs[nb], refs[nb + 1])
            for cp in copies:
                cp.start()
            token = refs[-1]
            token[...] = jnp.zeros_like(token)

        bufs = [_in_hbm(a) for a in list(srcs) + list(lands)]
        out = pl.pallas_call(
            body, name=name + "_start",
            out_shape=(pltpu.SemaphoreType.DMA((n_copies,)), pltpu.SemaphoreType.DMA((n_copies,)),
                       *[pltpu.HBM(a.shape, a.dtype) for a in bufs], jax.ShapeDtypeStruct((8, 128), F32)),
            in_specs=[HBM] * nb, out_specs=(SEM, SEM, *[HBM] * nb, pl.BlockSpec(memory_space=pltpu.VMEM)),
            input_output_aliases={i: 2 + i for i in range(nb)},
            compiler_params=pltpu.CompilerParams(has_side_effects=EFFECT),
        )(*bufs)
        return dict(send=out[0], recv=out[1], bufs=list(out[2:2 + nb]), token=out[-1])

    def wait(state, after):
        def body(*refs):
            copies = make_copies(refs[:ns], refs[ns:nb], refs[nb], refs[nb + 1])
            for cp in copies:
                cp.wait_send()
            for cp in copies:
                cp.wait_recv()

        bufs = state["bufs"]
        out = pl.pallas_call(
            body, name=name + "_wait",
            out_shape=tuple(pltpu.HBM(a.shape, a.dtype) for a in bufs),
            in_specs=[HBM] * nb + [SEM, SEM, pl.BlockSpec(memory_space=pl.ANY)], out_specs=tuple([HBM] * nb),
            input_output_aliases={i: i for i in range(nb)},
            compiler_params=pltpu.CompilerParams(has_side_effects=EFFECT),
        )(*bufs, state["send"], state["recv"], after)
        return list(out[:ns]), list(out[ns:])

    return start, wait


def _scatter_chips_split(name, parts):
    n = len(parts)
    lands = [lax.empty((3,) + p.shape[1:], p.dtype) for p in parts]

    def make_copies(srcs, lnds, send_sems, recv_sems):
        _, _, c, chips = _place()
        return [pltpu.make_async_remote_copy(
            src_ref=srcs[a].at[2 * px + py], dst_ref=lnds[a].at[j], send_sem=send_sems.at[a * 3 + j],
            recv_sem=recv_sems.at[a * 3 + j], device_id=(px, py, c), device_id_type=MESH)
            for a in range(n) for j, (px, py) in enumerate(chips)]

    return _split_copy_calls(name, parts, lands, 3 * n, make_copies)


def _gather_chips_split(name, shards, lands):
    n = len(shards)

    def make_copies(srcs, lnds, send_sems, recv_sems):
        x, y, c, chips = _place()
        out = []
        for a in range(n):
            h = srcs[a].shape[0] // 2
            for j, (px, py) in enumerate(chips):
                out.append(pltpu.make_async_remote_copy(
                    src_ref=srcs[a].at[pl.ds(c * h, h), :], dst_ref=lnds[a].at[2 * x + y, pl.ds(c * h, h), :],
                    send_sem=send_sems.at[a * 3 + j], recv_sem=recv_sems.at[a * 3 + j],
                    device_id=(px, py, c), device_id_type=MESH))
        return out

    return _split_copy_calls(name, shards, lands, 3 * n, make_copies)


def _gather_finish(gathered):
    n = len(gathered)

    def body(*refs):
        outs = refs[n:2 * n]
        send_sems, recv_sems = refs[2 * n:]
        x, y, c, chips = _place()

        def copy(a, j, chip_idx, which):
            h = outs[a].shape[1] // 2
            rows = outs[a].at[chip_idx, pl.ds(which * h, h), :]
            return pltpu.make_async_remote_copy(
                src_ref=rows, dst_ref=rows, send_sem=send_sems.at[a * 3 + j], recv_sem=recv_sems.at[a * 3 + j],
                device_id=(x, y, 1 - c), device_id_type=MESH)

        for a in range(n):
            for j, (px, py) in enumerate(chips):
                copy(a, j, 2 * px + py, c).start()
        for a in range(n):
            for j, (px, py) in enumerate(chips):
                copy(a, j, 2 * px + py, 1 - c).wait_recv()
        for a in range(n):
            for j, (px, py) in enumerate(chips):
                copy(a, j, 2 * px + py, c).wait_send()

    return pl.pallas_call(
        body, name="gather_finish",
        in_specs=[ANY] * n, out_specs=tuple([ANY] * n),
        out_shape=tuple(jax.ShapeDtypeStruct(g.shape, g.dtype) for g in gathered),
        input_output_aliases={i: i for i in range(n)},
        scratch_shapes=[pltpu.SemaphoreType.DMA((3 * n,)), pltpu.SemaphoreType.DMA((3 * n,))],
    )(*gathered)


def _join_halves(fulls):
    n = len(fulls)

    def body(*refs):
        outs = refs[n:2 * n]
        send_sems, recv_sems = refs[2 * n:]
        x, y, c, _ = _place()

        def copy(a, which):
            h = outs[a].shape[0] // 2
            rows = outs[a].at[pl.ds(which * h, h), :]
            return pltpu.make_async_remote_copy(
                src_ref=rows, dst_ref=rows, send_sem=send_sems.at[a], recv_sem=recv_sems.at[a],
                device_id=(x, y, 1 - c), device_id_type=MESH)

        for a in range(n):
            copy(a, c).start()
        for a in range(n):
            copy(a, 1 - c).wait_recv()
        for a in range(n):
            copy(a, c).wait_send()

    return pl.pallas_call(
        body, name="rs_join_halves",
        in_specs=[ANY] * n, out_specs=tuple([ANY] * n),
        out_shape=tuple(jax.ShapeDtypeStruct(p.shape, p.dtype) for p in fulls),
        input_output_aliases={i: i for i in range(n)},
        scratch_shapes=[pltpu.SemaphoreType.DMA((n,)), pltpu.SemaphoreType.DMA((n,))],
    )(*fulls)


def _all_reduce_small(sm):
    r, w = sm.shape

    def body(sm_ref, o_ref, buf, send_sems, recv_sems):
        x, y, c, _ = _place()
        me = 4 * x + 2 * y + c
        buf[me] = sm_ref[...]
        rel = [(dx, dy, dc) for dx in (0, 1) for dy in (0, 1) for dc in (0, 1)][1:]

        def copy(k, slot, to):
            return pltpu.make_async_remote_copy(
                src_ref=sm_ref, dst_ref=buf.at[slot], send_sem=send_sems.at[k], recv_sem=recv_sems.at[k],
                device_id=to, device_id_type=MESH)

        peers = []
        for k, (dx, dy, dc) in enumerate(rel):
            px = 1 - x if dx else x
            py = 1 - y if dy else y
            pc = 1 - c if dc else c
            peers.append((px, py, pc))
            copy(k, me, (px, py, pc)).start()
        for k, (px, py, pc) in enumerate(peers):
            copy(k, 4 * px + 2 * py + pc, (px, py, pc)).wait_recv()
        for k, (px, py, pc) in enumerate(peers):
            copy(k, me, (px, py, pc)).wait_send()
        acc = buf[0]
        for d in range(1, 8):
            acc = acc + buf[d]
        o_ref[...] = acc

    vm = pl.BlockSpec(memory_space=pltpu.VMEM)
    return pl.pallas_call(
        body, name="all_reduce_small", in_specs=[vm], out_specs=vm,
        out_shape=jax.ShapeDtypeStruct((r, w), F32),
        scratch_shapes=[pltpu.VMEM((8, r, w), F32), pltpu.SemaphoreType.DMA((7,)), pltpu.SemaphoreType.DMA((7,))],
    )(sm)


class _LocalWeights:
    def __init__(self, w):
        self.w = w
        self.g = {}

    def first(self):
        return self.w

    def rest(self, after):
        del after
        return self.w

    def grads(self, tag, g):
        del tag
        self.g.update(g)
        return None


def _local_step(x3, mem3, pos2, target3, small, comm):
    bsz, seq, d = x3.shape
    mlen = mem3.shape[1]
    t = bsz * seq
    ds = d // 4
    w = comm.first()
    x = x3.reshape(t, d)
    mem = mem3.reshape(bsz * mlen, d)
    target = target3.reshape(t, d)
    pos = pos2.reshape(t, 1)
    qg_t = jnp.tile(small["sw_q_norm_g"], (1, SW_HEADS))
    kg_t = jnp.tile(small["sw_k_norm_g"], (1, SW_KV_HEADS))

    hn1 = _rms_fwd(x, small["norm1_g"], name="rms1_fwd")
    proj_hg = _mm(hn1, w["w_in_hg"], NN, t, HG_COLS, d, name="proj_hg", tk=d, after=(w.get("token"),))[0]
    proj_sw = _mm(hn1, w["w_in_sw"], NN, t, SW_COLS, d, name="proj_sw", tk=d)[0]
    y_mix, o_hg, states = _hg_fwd(proj_hg, small["hg_lower_bounds"], small["hg_norm_g"], bsz, seq, y_width=1024)
    y_mix = _sw_fwd(proj_sw, pos, qg_t, kg_t, small["sw_sinks"], y_mix, bsz, seq)
    w_in_hg, w_in_sw = w["w_in_hg"], w["w_in_sw"]
    w = comm.rest(y_mix)
    ff = w["down"].shape[0]
    ffs = ff // 4
    h1 = _mm(y_mix, w["w_out"], NN, t, d, 1024, name="out_proj", tk=1024, extras=(x,),
             epilogue=lambda acc, res: (acc + res,))[0]
    hn2 = _rms_fwd(h1, small["norm2_g"], name="rms2_fwd")
    mn = _rms_fwd(mem, small["mem_norm_g"], name="rms_mem_fwd")
    qx = _mm(hn2, w["wq"], NN, t, 512, d, name="xa_q", tk=d)[0]
    kvx = _mm(mn, w["wkv"], NN, bsz * mlen, 1024, d, name="xa_kv", tk=d)[0]
    ox = _xa_fwd(qx, kvx, small["xa_q_norm_g"], small["xa_k_norm_g"], bsz, seq, mlen)
    h2 = _mm(ox, w["wo"], NN, t, d, 512, name="xa_o", tn=ds, tk=512, extras=(h1,),
             b_spec=pl.BlockSpec((None, 512, ds), lambda i, j, kk: (j, 0, 0)),
             epilogue=lambda acc, res: (acc + res,))[0]
    hn3 = _rms_fwd(h2, small["norm3_g"], name="rms3_fwd")

    def relu_sq(acc):
        a = jnp.maximum(acc, 0.0)
        return a, a * a

    act, act2 = _mm(hn3, w["up"], NN, t, ff, d, name="mlp_up", tn=ffs, tk=d,
                    b_spec=pl.BlockSpec((None, d, ffs), lambda i, j, kk: (j, 0, 0)),
                    epilogue=relu_sq, out_dtypes=(_MXU_DTYPE, _MXU_DTYPE))
    inv_d = 1.0 / d
    dy = _mm(act2, w["down"], NN, t, d, ff, name="mlp_down", extras=(h2, target),
             epilogue=lambda acc, res, tgt: ((acc + res - tgt) * inv_d,))[0]
    loss_row = _loss_sum(dy, d)

    dz = _mm(dy, w["down"], NT, t, ff, d, name="d_act", tk=d, extras=(act,),
             epilogue=lambda acc, a: (acc * (2.0 * a.astype(F32)),), out_dtypes=(_MXU_DTYPE,))[0]
    g_down = _mm(act2, dy, TN, ff, d, t, name="g_down")[0]
    g_up = _mm(hn3, dz, TN, d, ff, t, name="g_up", tn=ffs,
               out_shape=(jax.ShapeDtypeStruct((4, d, ffs), F32),),
               out_spec=(pl.BlockSpec((None, min(1024, d), ffs), lambda i, j, kk: (j, i, 0)),))[0]
    tok = comm.grads("mlp", dict(up=g_up, down=g_down))
    dhn3 = _mm(dz, w["up"], NT, t, d, ff, name="d_hn3", tk=ffs, after=(tok,),
               b_spec=pl.BlockSpec((None, min(1024, d), ffs), lambda i, j, kk: (kk, j, 0)))[0]
    dh2, g_norm3 = _rms_bwd(h2, small["norm3_g"], dhn3, dy, name="rms3_bwd")
    d_ox = _mm(dh2, w["wo"], NT, t, 512, d, name="d_ox", tk=ds,
               b_spec=pl.BlockSpec((None, 512, ds), lambda i, j, kk: (kk, 0, 0)))[0]
    g_wo = _mm(ox, dh2, TN, 512, d, t, name="g_wo", tn=ds,
               out_shape=(jax.ShapeDtypeStruct((4, 512, ds), F32),),
               out_spec=(pl.BlockSpec((None, 512, ds), lambda i, j, kk: (j, 0, 0)),))[0]
    d_qx, d_kvx, g_xq, g_xk = _xa_bwd(qx, kvx, small["xa_q_norm_g"], small["xa_k_norm_g"], d_ox, bsz, seq, mlen)
    g_wq = _mm(hn2, d_qx, TN, d, 512, t, name="g_wq")[0]
    g_wkv = _mm(mn, d_kvx, TN, d, 1024, bsz * mlen, name="g_wkv")[0]
    dhn2 = _mm(d_qx, w["wq"], NT, t, d, 512, name="d_hn2", tk=512)[0]
    dmn = _mm(d_kvx, w["wkv"], NT, bsz * mlen, d, 1024, name="d_mn", tk=1024)[0]
    dh1, g_norm2 = _rms_bwd(h1, small["norm2_g"], dhn2, dh2, name="rms2_bwd")
    _, g_memn = _rms_bwd(mem, small["mem_norm_g"], dmn, None, name="rms_mem_bwd")
    g_wout = _mm(y_mix, dh1, TN, 1024, d, t, name="g_wout")[0]
    tok = comm.grads("mid", dict(w_out=g_wout, wq=g_wq, wkv=g_wkv, wo=g_wo))
    d_mix = _mm(dh1, w["w_out"], NT, t, 1024, d, name="d_mix", tk=d, after=(tok,))[0]
    dproj_sw, g_swq, g_swk, g_sinks = _sw_bwd(proj_sw, pos, qg_t, kg_t, small["sw_sinks"], y_mix, d_mix, bsz, seq)
    dproj_hg, g_lb, g_hgn = _hg_bwd(proj_hg, small["hg_lower_bounds"], small["hg_norm_g"], o_hg, states, d_mix, bsz, seq)
    g_in_hg = _mm(hn1, dproj_hg, TN, d, HG_COLS, t, name="g_in_hg")[0]
    g_in_sw = _mm(hn1, dproj_sw, TN, d, SW_COLS, t, name="g_in_sw")[0]
    comm.grads("in", dict(w_in_hg=g_in_hg, w_in_sw=g_in_sw))
    dhn1_a = _mm(dproj_hg, w_in_hg, NT, t, d, HG_COLS, name="d_hn1_hg", tk=1024)[0]
    dhn1 = _mm(dproj_sw, w_in_sw, NT, t, d, SW_COLS, name="d_hn1_sw", tk=SW_COLS, extras=(dhn1_a,),
               epilogue=lambda acc, prev: (acc + prev,))[0]
    grad_x, g_norm1 = _rms_bwd(x, small["norm1_g"], dhn1, dh1, name="rms1_bwd")

    g_small = dict(norm1_g=g_norm1, hg_lower_bounds=g_lb, hg_norm_g=g_hgn, sw_q_norm_g=g_swq, sw_k_norm_g=g_swk,
                   sw_sinks=g_sinks[:, 0:SW_HEADS], norm2_g=g_norm2, mem_norm_g=g_memn, xa_q_norm_g=g_xq,
                   xa_k_norm_g=g_xk, norm3_g=g_norm3)
    return loss_row, grad_x.reshape(bsz, seq, d), g_small


SMALL_NAMES = ("norm1_g", "hg_lower_bounds", "hg_norm_g", "sw_q_norm_g", "sw_k_norm_g", "sw_sinks", "norm2_g",
               "mem_norm_g", "xa_q_norm_g", "xa_k_norm_g", "norm3_g")
BIG_NAMES = ("w_in", "w_out", "xa_wq", "xa_wkv", "xa_wo", "mlp_up", "mlp_down")
WEIGHT_ORDER = ("norm1_g", "w_in", "hg_lower_bounds", "hg_norm_g", "sw_q_norm_g", "sw_k_norm_g", "sw_sinks", "w_out",
                "norm2_g", "mem_norm_g", "xa_wq", "xa_wkv", "xa_q_norm_g", "xa_k_norm_g", "xa_wo", "norm3_g",
                "mlp_up", "mlp_down")


def _head_major_blocks():
    return [kind * HG_HEADS + h for h in range(HG_HEADS) for kind in range(4)]


def _permute_col_blocks(a, blocks):
    return jnp.concatenate([a[:, 128 * b:128 * (b + 1)] for b in blocks], axis=1)


def _pack_rows(vals, width):
    starts, at = [], 0
    for v in vals:
        starts.append(at)
        at += v.shape[0]
    total = at + (-at) % 8
    out = None
    for v, s in zip(vals, starts):
        placed = jnp.pad(v, ((s, total - s - v.shape[0]), (0, width - v.shape[1])))
        out = placed if out is None else out + placed
    return out, starts


class _MeshWeights:
    LATE = ("w_out", "xa_wq", "xa_wkv", "xa_wo", "mlp_up", "mlp_down")

    def __init__(self, shards, d, ff):
        self.shards, self.d, self.ff = shards, d, ff
        self.c_idx = lax.axis_index("c").astype(jnp.int32).reshape(1)
        chip = (2 * lax.axis_index("x") + lax.axis_index("y")).astype(jnp.int32)
        self.place_idx = jnp.stack([chip, lax.axis_index("c").astype(jnp.int32)])
        self.pending = []
        self.halves = {}

    def first(self):
        placed = {n: _place_shard(s, self.place_idx, name="place_" + n) for n, s in self.shards.items()}
        (g_in,) = _all_gather_weights([self.shards["w_in"]], [placed["w_in"]])
        start, self.late_wait = _gather_chips_split("gather_late", [self.shards[n] for n in self.LATE],
                                                    [placed[n] for n in self.LATE])
        self.late_state = start()
        full = jnp.concatenate([g_in[k] for k in range(4)], axis=1)
        return dict(w_in_hg=_permute_col_blocks(full, _head_major_blocks()), w_in_sw=full[:, HG_COLS:],
                    token=self.late_state["token"])

    def rest(self, after):
        _, lands = self.late_wait(self.late_state, after)
        g_out, g_q, g_kv, g_o, g_up, g_dn = _gather_finish(lands)
        d = self.d
        return dict(w_out=g_out.reshape(-1, d), wq=g_q.reshape(d, -1), wkv=g_kv.reshape(d, -1), wo=g_o, up=g_up,
                    down=g_dn.reshape(self.ff, d))

    def _chip_partials(self, tag, names, arrays):
        recv = _exchange_halves(arrays, "rs_exchange_" + tag)
        return [_add_halves(g, r, self.c_idx, name="rs_add_halves_" + n) for n, g, r in zip(names, arrays, recv)]

    def grads(self, tag, g):
        d, ff = self.d, self.ff
        if tag == "mlp":
            names, arrays = ("mlp_up", "mlp_down"), [g["up"], g["down"].reshape(4, ff // 4, d)]
        elif tag == "mid":
            names = ("w_out", "xa_wq", "xa_wkv", "xa_wo")
            arrays = [g["w_out"].reshape(4, -1, d), g["wq"].reshape(4, d // 4, -1), g["wkv"].reshape(4, d // 4, -1), g["wo"]]
        else:
            self.g_in = g
            return None
        parts = self._chip_partials(tag, names, arrays)
        start, wait = _scatter_chips_split("rs_scatter_" + tag, parts)
        state = start()
        self.pending.append((names, wait, state))
        return state["token"]

    def finish(self):
        inv = [int(b) for b in np.argsort(_head_major_blocks())]
        full = jnp.concatenate([_permute_col_blocks(self.g_in["w_in_hg"], inv), self.g_in["w_in_sw"]], axis=1)
        ws = full.shape[1] // 4
        parts = self._chip_partials("in", ("w_in",), [jnp.stack([full[:, ws * k:ws * (k + 1)] for k in range(4)])])
        (recv,) = _scatter_chips(parts, "rs_scatter_in")
        self.halves["w_in"] = _add_chips(parts[0], recv, self.place_idx, name="rs_add_chips_w_in")
        for names, wait, state in self.pending:
            srcs, lands = wait(state, self.halves["w_in"])
            for n, p, r in zip(names, srcs, lands):
                self.halves[n] = _add_chips(p, r, self.place_idx, name="rs_add_chips_" + n)
        return dict(zip(BIG_NAMES, _join_halves([self.halves[n] for n in BIG_NAMES])))


def kernel(x, mem, positions, norm1_g, w_in, hg_lower_bounds, hg_norm_g, sw_q_norm_g, sw_k_norm_g, sw_sinks, w_out, norm2_g, mem_norm_g, xa_wq, xa_wkv, xa_q_norm_g, xa_k_norm_g, xa_wo, norm3_g, mlp_up, mlp_down, loss_target, m_norm1_g, m_w_in, m_hg_lower_bounds, m_hg_norm_g, m_sw_q_norm_g, m_sw_k_norm_g, m_sw_sinks, m_w_out, m_norm2_g, m_mem_norm_g, m_xa_wq, m_xa_wkv, m_xa_q_norm_g, m_xa_k_norm_g, m_xa_wo, m_norm3_g, m_mlp_up, m_mlp_down, v_norm1_g, v_w_in, v_hg_lower_bounds, v_hg_norm_g, v_sw_q_norm_g, v_sw_k_norm_g, v_sw_sinks, v_w_out, v_norm2_g, v_mem_norm_g, v_xa_wq, v_xa_wkv, v_xa_q_norm_g, v_xa_k_norm_g, v_xa_wo, v_norm3_g, v_mlp_up, v_mlp_down):
    given = dict(locals())
    weights = {n: given[n] for n in WEIGHT_ORDER}
    moms = {n: given["m_" + n] for n in WEIGHT_ORDER}
    vars_ = {n: given["v_" + n] for n in WEIGHT_ORDER}
    d = x.shape[-1]
    ff = mlp_down.shape[1] * 4
    small = {n: weights[n] for n in SMALL_NAMES}

    comm = _MeshWeights({n: weights[n][0].astype(_MXU_DTYPE) for n in BIG_NAMES}, d, ff)
    loss_row, grad_x, g_small = _local_step(x, mem, positions, loss_target, small, comm)
    big_grads = comm.finish()

    packed, starts = _pack_rows([g_small[n] for n in SMALL_NAMES] + [loss_row], 1024)
    summed = _all_reduce_small(packed)
    small_grads = {}
    for n, s in zip(SMALL_NAMES, starts):
        r, c = weights[n].shape
        small_grads[n] = summed[s:s + r, 0:c]
    loss = summed[starts[-1], 0]

    grads, deltas, new_m, new_v = {}, {}, {}, {}
    for n in BIG_NAMES:
        shp = weights[n].shape
        g2 = big_grads[n]
        dl, mo, vo = _adamw_big(weights[n][0], g2, moms[n][0], vars_[n][0], name="adamw_" + n)
        grads[n], deltas[n], new_m[n], new_v[n] = (a.reshape(shp) for a in (g2, dl, mo, vo))
    sm_out = _adamw_small([weights[n] for n in SMALL_NAMES], [small_grads[n] for n in SMALL_NAMES],
                          [moms[n] for n in SMALL_NAMES], [vars_[n] for n in SMALL_NAMES])
    ns = len(SMALL_NAMES)
    for i, n in enumerate(SMALL_NAMES):
        grads[n], deltas[n], new_m[n], new_v[n] = small_grads[n], sm_out[i], sm_out[ns + i], sm_out[2 * ns + i]

    return (loss, grad_x, *[grads[n] for n in WEIGHT_ORDER], *[deltas[n] for n in WEIGHT_ORDER],
            *[new_m[n] for n in WEIGHT_ORDER], *[new_v[n] for n in WEIGHT_ORDER])
```

```python
import functools

import numpy as np
import jax
import jax.numpy as jnp
from jax import lax
from jax.experimental import pallas as pl
from jax.experimental.pallas import tpu as pltpu

F32 = jnp.float32
_MXU_DTYPE = jnp.bfloat16

EPS = 1e-6
HG_HEADS = 4
HG_D = 128
HG_CHUNK = 64
HG_TILE = 512
HG_LEVELS = (32, 16, 8, 4, 2, 1)
SW_HEADS = 8
SW_KV_HEADS = 2
SW_GROUP = SW_HEADS // SW_KV_HEADS
SW_HD = 64
SW_BLOCK = 128
ROPE_THETA = 500000.0
ROT_DIM = SW_HD // 4
XA_HEADS = 4
XA_HD = 128
HG_COLS = 4 * HG_HEADS * HG_D
SW_COLS = (SW_HEADS + 2 * SW_KV_HEADS) * SW_HD

ADAM_LR = 0.001
ADAM_B1 = 0.9
ADAM_B2 = 0.999
ADAM_EPS = 1e-08
ADAM_WD = 0.01
ADAM_STEP = 10

VMEM_LIMIT = 56 * 1024 * 1024
MESH = pl.DeviceIdType.MESH

NN = ((1,), (0,))
NT = ((1,), (1,))
TN = ((0,), (0,))


def _mx(v):
    return v.astype(_MXU_DTYPE)


def _dot(a, b, dims=NN):
    return lax.dot_general(_mx(a), _mx(b), (dims, ((), ())), preferred_element_type=F32)


def _split_dot(a, v, dims, parts):
    acc = None
    rest = v
    for p in range(parts):
        piece = _mx(rest)
        term = lax.dot_general(a, piece, (dims, ((), ())), preferred_element_type=F32)
        acc = term if acc is None else acc + term
        if p + 1 < parts:
            rest = rest - piece.astype(F32)
    return acc


def _params(sem):
    return pltpu.CompilerParams(dimension_semantics=sem, vmem_limit_bytes=VMEM_LIMIT)


def _mm(a, b, mode, m, n, k, *, name, tm=1024, tn=1024, tk=512, a_spec=None, b_spec=None, extras=(), epilogue=None,
        out_dtypes=(F32,), out_shape=None, out_spec=None, after=()):
    after = tuple(t for t in after if t is not None)
    tm, tn, tk = min(tm, m), min(tn, n), min(tk, k)
    assert m % tm == 0 and n % tn == 0 and k % tk == 0, (name, m, n, k, tm, tn, tk)
    gi, gj, gk = m // tm, n // tn, k // tk
    if a_spec is None:
        a_spec = (pl.BlockSpec((tk, tm), lambda i, j, kk: (kk, i)) if mode == TN
                  else pl.BlockSpec((tm, tk), lambda i, j, kk: (i, kk)))
    if b_spec is None:
        b_spec = (pl.BlockSpec((tn, tk), lambda i, j, kk: (j, kk)) if mode == NT
                  else pl.BlockSpec((tk, tn), lambda i, j, kk: (kk, j)))
    mn_spec = pl.BlockSpec((tm, tn), lambda i, j, kk: (i, j))
    if epilogue is None:
        epilogue = lambda acc: (acc,)
    n_ex, n_out = len(extras), len(out_dtypes)
    if out_shape is None:
        out_shape = tuple(jax.ShapeDtypeStruct((m, n), d) for d in out_dtypes)
        out_spec = tuple(mn_spec for _ in out_dtypes)

    n_after = len(after)

    def body(*refs):
        a_ref, b_ref = refs[0], refs[1]
        ex = refs[2:2 + n_ex]
        outs = refs[2 + n_ex + n_after:2 + n_ex + n_after + n_out]

        def finish(acc):
            res = epilogue(acc, *[e[...] for e in ex])
            for o, r in zip(outs, res):
                o[...] = r.astype(o.dtype)

        if gk == 1:
            finish(_dot(a_ref[...], b_ref[...], mode))
        else:
            acc_ref = refs[-1]
            kk = pl.program_id(2)

            @pl.when(kk == 0)
            def _():
                acc_ref[...] = jnp.zeros_like(acc_ref)

            acc_ref[...] += _dot(a_ref[...], b_ref[...], mode)

            @pl.when(kk == gk - 1)
            def _():
                finish(acc_ref[...])

    return pl.pallas_call(
        body, name=name, grid=(gi, gj, gk),
        in_specs=[a_spec, b_spec] + [mn_spec] * n_ex + [pl.BlockSpec(memory_space=pl.ANY)] * n_after,
        out_specs=out_spec, out_shape=out_shape,
        scratch_shapes=[pltpu.VMEM((tm, tn), F32)] if gk > 1 else [],
        compiler_params=_params(("parallel", "parallel", "arbitrary")),
    )(a, b, *extras, *after)


def _rms_fwd(x, g, *, name, tm=512):
    t, d = x.shape
    tm = min(tm, t)

    def body(x_ref, g_ref, o_ref):
        xv = x_ref[...]
        r = lax.rsqrt(jnp.mean(xv * xv, axis=1, keepdims=True) + EPS)
        o_ref[...] = (xv * r * g_ref[...]).astype(o_ref.dtype)

    return pl.pallas_call(
        body, name=name, grid=(t // tm,),
        in_specs=[pl.BlockSpec((tm, d), lambda i: (i, 0)), pl.BlockSpec((1, d), lambda i: (0, 0))],
        out_specs=pl.BlockSpec((tm, d), lambda i: (i, 0)),
        out_shape=jax.ShapeDtypeStruct((t, d), _MXU_DTYPE),
        compiler_params=_params(("parallel",)),
    )(x, g)


def _rms_bwd(x, g, dy, dres, *, name, tm=512):
    t, d = x.shape
    tm = min(tm, t)
    has_res = dres is not None

    def body(*refs):
        x_ref, g_ref, dy_ref = refs[:3]
        dx_ref, dg_ref = refs[-2:]
        xv, dyv = x_ref[...], dy_ref[...]
        r = lax.rsqrt(jnp.mean(xv * xv, axis=1, keepdims=True) + EPS)
        u = dyv * g_ref[...]
        dx = r * u - xv * (r * r * r) * jnp.mean(u * xv, axis=1, keepdims=True)
        if has_res:
            dx = dx + refs[3][...]
        dx_ref[...] = dx

        @pl.when(pl.program_id(0) == 0)
        def _():
            dg_ref[...] = jnp.zeros_like(dg_ref)

        dg_ref[...] += jnp.sum(dyv * xv * r, axis=0, keepdims=True)

    row = pl.BlockSpec((tm, d), lambda i: (i, 0))
    vec = pl.BlockSpec((1, d), lambda i: (0, 0))
    return pl.pallas_call(
        body, name=name, grid=(t // tm,),
        in_specs=[row, vec, row] + ([row] if has_res else []),
        out_specs=(row, vec),
        out_shape=(jax.ShapeDtypeStruct((t, d), F32), jax.ShapeDtypeStruct((1, d), F32)),
        compiler_params=_params(("arbitrary",)),
    )(*([x, g, dy] + ([dres] if has_res else [])))


def _hg_constants():
    c = HG_CHUNK
    t = np.arange(c)
    sums = [t[None, :] <= t[:, None]]
    masks = []
    for m in HG_LEVELS:
        base = (t // (2 * m)) * (2 * m)
        mid = base + m - 1
        second = (t - base) >= m
        upper = (t[None, :] > mid[:, None]) & (t[None, :] <= t[:, None])
        lower = (t[None, :] > t[:, None]) & (t[None, :] <= mid[:, None])
        sums.append(np.where(second[:, None], upper, lower))
        masks.append(second[:, None] & (~second)[None, :] & (base[:, None] == base[None, :]))
    return (np.concatenate(sums, axis=0).astype(np.float32), np.stack(masks).astype(np.float32))


def _hg_gates(blk, lbp):
    q, x, v, gl = blk[:, 0:128], blk[:, 128:256], blk[:, 256:384], blk[:, 384:512]
    mx = jnp.max(lbp, axis=0, keepdims=True)
    e = jnp.exp(lbp - mx)
    lb = e[0:1, :] / jnp.sum(e, axis=0, keepdims=True)
    sig = jax.nn.sigmoid(x)
    f = lb + (1.0 - lb) * sig
    return q, v, gl, lb, sig, f, 1.0 - f, jnp.log(f)


def _hg_fwd(proj, lbp, ng, bsz, seq, *, y_width):
    t = proj.shape[0]
    nc = seq // HG_CHUNK
    a_np, m_np = _hg_constants()
    a_all = jnp.asarray(a_np, _MXU_DTYPE)
    masks = jnp.asarray(m_np, F32)
    nl = len(HG_LEVELS)

    ts = min(HG_TILE, seq)
    ns, nct = seq // ts, ts // HG_CHUNK
    hw = HG_HEADS * HG_D

    def body(p_ref, lb_ref, ng_ref, a_ref, m_ref, y_ref, o_ref, st_ref, carry):
        a_mat = a_ref[...]
        ngv = ng_ref[...]

        @pl.when(pl.program_id(1) == 0)
        def _():
            carry[...] = jnp.zeros_like(carry)

        def chunk(c, _):
            rows = pl.ds(pl.multiple_of(c * HG_CHUNK, HG_CHUNK), HG_CHUNK)
            for h in range(HG_HEADS):
                hs = slice(HG_D * h, HG_D * (h + 1))
                q, v, gl, lb, sig, f, k, g = _hg_gates(p_ref[rows, 512 * h:512 * (h + 1)], lb_ref[:, hs])
                st = carry[h]
                e_all = _split_dot(a_mat, g, NN, 3)
                b = e_all[0:HG_CHUNK]
                st_ref[h, c] = st
                o = _dot(q * jnp.exp(b), st, NT)
                p = jnp.zeros((HG_CHUNK, HG_CHUNK), F32)
                for li in range(nl):
                    e = jnp.exp(e_all[HG_CHUNK * (li + 1):HG_CHUNK * (li + 2)])
                    p = p + m_ref[li] * _dot(q * e, k * e, NT)
                o = o + _dot(p, v) + jnp.sum(q * k, axis=1, keepdims=True) * v
                bl = b[HG_CHUNK - 1:HG_CHUNK, :]
                carry[h] = st * jnp.exp(bl) + _dot(v, k * jnp.exp(bl - b), TN)
                r = lax.rsqrt(jnp.mean(o * o, axis=1, keepdims=True) + EPS)
                o_ref[rows, hs] = o
                y_ref[rows, hs] = (o * r * ngv) * (gl * jax.nn.sigmoid(gl))
            return 0

        lax.fori_loop(0, nct, chunk, 0)

    return pl.pallas_call(
        body, name="hgrn2_fwd", grid=(bsz, ns),
        in_specs=[pl.BlockSpec((ts, HG_COLS), lambda b, s: (b * ns + s, 0)),
                  pl.BlockSpec((2, hw), lambda b, s: (0, 0)),
                  pl.BlockSpec((1, HG_D), lambda b, s: (0, 0)),
                  pl.BlockSpec(a_all.shape, lambda b, s: (0, 0)),
                  pl.BlockSpec(masks.shape, lambda b, s: (0, 0, 0))],
        out_specs=(pl.BlockSpec((ts, hw), lambda b, s: (b * ns + s, 0)),
                   pl.BlockSpec((ts, hw), lambda b, s: (b * ns + s, 0)),
                   pl.BlockSpec((None, HG_HEADS, nct, HG_D, HG_D), lambda b, s: (b, 0, s, 0, 0))),
        out_shape=(jax.ShapeDtypeStruct((t, y_width), F32),
                   jax.ShapeDtypeStruct((t, hw), F32),
                   jax.ShapeDtypeStruct((bsz, HG_HEADS, nc, HG_D, HG_D), F32)),
        scratch_shapes=[pltpu.VMEM((HG_HEADS, HG_D, HG_D), F32)],
        compiler_params=_params(("parallel", "arbitrary")),
    )(proj, lbp, ng, a_all, masks)


def _hg_bwd(proj, lbp, ng, o_all, states, dy, bsz, seq):
    t = proj.shape[0]
    nc = seq // HG_CHUNK
    a_np, m_np = _hg_constants()
    a_all = jnp.asarray(a_np, _MXU_DTYPE)
    masks = jnp.asarray(m_np, F32)
    nl = len(HG_LEVELS)
    cs = HG_CHUNK

    ts = min(HG_TILE, seq)
    ns, nct = seq // ts, ts // cs
    hw = HG_HEADS * HG_D

    def body(p_ref, lb_ref, ng_ref, a_ref, m_ref, o_ref, st_ref, dy_ref, dp_ref, dlb_ref, dng_ref, dst_ref):
        a_mat = a_ref[...]
        ngv = ng_ref[...]
        last_row = lax.broadcasted_iota(jnp.int32, (cs, HG_D), 0) == cs - 1
        si = pl.program_id(1)
        first = jnp.logical_and(pl.program_id(0) == 0, si == 0)

        @pl.when(si == 0)
        def _():
            dst_ref[...] = jnp.zeros_like(dst_ref)

        def head_chunk(h, c, rows):
            hs = slice(HG_D * h, HG_D * (h + 1))
            q, v, gl, lb, sig, f, k, g = _hg_gates(p_ref[rows, 512 * h:512 * (h + 1)], lb_ref[:, hs])
            o = o_ref[rows, hs]
            dyv = dy_ref[rows, hs]
            st = st_ref[h, c]
            dst = dst_ref[h]
            e_all = _split_dot(a_mat, g, NN, 3)
            b = e_all[0:cs]
            eb = jnp.exp(b)
            bl = b[cs - 1:cs, :]
            ebl = jnp.exp(bl)
            ekd = jnp.exp(bl - b)
            qb, kd = q * eb, k * ekd
            sg = jax.nn.sigmoid(gl)
            r = lax.rsqrt(jnp.mean(o * o, axis=1, keepdims=True) + EPS)
            dgl = dyv * (o * r * ngv) * (sg * (1.0 + gl * (1.0 - sg)))
            u = dyv * (gl * sg) * ngv
            do = r * u - o * (r * r * r) * jnp.mean(u * o, axis=1, keepdims=True)
            dng_row = jnp.sum(dyv * (gl * sg) * o * r, axis=0, keepdims=True)
            es, qm, km = [], [], []
            p = jnp.zeros((cs, cs), F32)
            for li in range(nl):
                e = jnp.exp(e_all[cs * (li + 1):cs * (li + 2)])
                es.append(e)
                qm.append(q * e)
                km.append(k * e)
                p = p + m_ref[li] * _dot(qm[li], km[li], NT)
            qk = jnp.sum(q * k, axis=1, keepdims=True)
            dp = _dot(do, v, NT)
            dv = _dot(p, do, TN) + qk * do + _dot(kd, dst, NT)
            dqb = _dot(do, st)
            dkd = _dot(v, dst)
            dq = dqb * eb
            dk = dkd * ekd
            db = dqb * qb - dkd * kd
            dbl = jnp.sum(dkd * kd, axis=0, keepdims=True) + jnp.sum(dst * st, axis=0, keepdims=True) * ebl
            de = [db + jnp.where(last_row, dbl, 0.0)]
            for li in range(nl):
                dpm = m_ref[li] * dp
                dqm = _dot(dpm, km[li])
                dkm = _dot(dpm, qm[li], TN)
                dq = dq + dqm * es[li]
                dk = dk + dkm * es[li]
                de.append(dqm * qm[li] + dkm * km[li])
            dpd = jnp.sum(do * v, axis=1, keepdims=True)
            dq = dq + dpd * k
            dk = dk + dpd * q
            dg = _split_dot(a_mat, jnp.concatenate(de, axis=0), TN, 2)
            df = dg / f - dk
            dx = df * (1.0 - lb) * sig * (1.0 - sig)
            base = 512 * h
            dp_ref[rows, base:base + 128] = dq
            dp_ref[rows, base + 128:base + 256] = dx
            dp_ref[rows, base + 256:base + 384] = dv
            dp_ref[rows, base + 384:base + 512] = dgl
            dst_ref[h] = dst * ebl + _dot(do, qb, TN)
            return jnp.sum(df * (1.0 - sig), axis=0, keepdims=True), dng_row

        def chunk(i, carry):
            c = nct - 1 - i
            rows = pl.ds(pl.multiple_of(c * cs, cs), cs)
            out = []
            for h in range(HG_HEADS):
                dlb_row, dng_row = head_chunk(h, c, rows)
                out.append(carry[h] + dlb_row)
                out.append(carry[HG_HEADS + h] + dng_row)
            return tuple(out[0::2]) + tuple(out[1::2])

        zrow = jnp.zeros((1, HG_D), F32)
        acc = lax.fori_loop(0, nct, chunk, (zrow,) * (2 * HG_HEADS))

        @pl.when(first)
        def _():
            dlb_ref[...] = jnp.zeros_like(dlb_ref)
            dng_ref[...] = jnp.zeros_like(dng_ref)

        lbp_v = lb_ref[...]
        mx = jnp.max(lbp_v, axis=0, keepdims=True)
        e = jnp.exp(lbp_v - mx)
        s0 = e[0:1, :] / jnp.sum(e, axis=0, keepdims=True)
        da0 = jnp.concatenate(acc[0:HG_HEADS], axis=1) * s0 * (1.0 - s0)
        dlb_ref[...] += jnp.concatenate([da0, -da0], axis=0)
        dng_ref[...] += (acc[HG_HEADS] + acc[HG_HEADS + 1]) + (acc[HG_HEADS + 2] + acc[HG_HEADS + 3])

    def tile(b, s):
        return b * ns + (ns - 1 - s)

    return pl.pallas_call(
        body, name="hgrn2_bwd", grid=(bsz, ns),
        in_specs=[pl.BlockSpec((ts, HG_COLS), lambda b, s: (tile(b, s), 0)),
                  pl.BlockSpec((2, hw), lambda b, s: (0, 0)),
                  pl.BlockSpec((1, HG_D), lambda b, s: (0, 0)),
                  pl.BlockSpec(a_all.shape, lambda b, s: (0, 0)),
                  pl.BlockSpec(masks.shape, lambda b, s: (0, 0, 0)),
                  pl.BlockSpec((ts, hw), lambda b, s: (tile(b, s), 0)),
                  pl.BlockSpec((None, HG_HEADS, nct, HG_D, HG_D), lambda b, s: (b, 0, ns - 1 - s, 0, 0)),
                  pl.BlockSpec((ts, hw), lambda b, s: (tile(b, s), 0))],
        out_specs=(pl.BlockSpec((ts, HG_COLS), lambda b, s: (tile(b, s), 0)),
                   pl.BlockSpec((2, hw), lambda b, s: (0, 0)),
                   pl.BlockSpec((1, HG_D), lambda b, s: (0, 0))),
        out_shape=(jax.ShapeDtypeStruct((t, HG_COLS), F32),
                   jax.ShapeDtypeStruct((2, hw), F32),
                   jax.ShapeDtypeStruct((1, HG_D), F32)),
        scratch_shapes=[pltpu.VMEM((HG_HEADS, HG_D, HG_D), F32)],
        compiler_params=_params(("arbitrary", "arbitrary")),
    )(proj, lbp, ng, a_all, masks, o_all, states, dy)


def _sw_constants():
    half = ROT_DIM // 2
    inv = (np.float32(ROPE_THETA) ** (-(np.arange(half, dtype=np.float32) * np.float32(2.0) / np.float32(ROT_DIM)))
           ).astype(np.float32)
    freq = np.zeros((1, 128), np.float32)
    sign = np.zeros((1, 128), np.float32)
    for h in range(2):
        freq[0, 64 * h:64 * h + half] = inv
        freq[0, 64 * h + half:64 * h + 2 * half] = inv
        sign[0, 64 * h:64 * h + half] = -1.0
        sign[0, 64 * h + half:64 * h + 2 * half] = 1.0
    seg = np.kron(np.eye(8, dtype=np.float32), np.full((64, 64), 1.0 / 64.0, np.float32))
    return freq, sign, seg


def _rope_tables(pos, freq, sign):
    ang = pos.astype(F32) * freq
    return jnp.cos(ang), jnp.sin(ang) * sign


def _tile_lanes(v, times):
    return v if times == 1 else jnp.concatenate([v] * times, axis=1)


def _swap_halves(v):
    w = v.shape[1]
    half = ROT_DIM // 2
    lane = lax.broadcasted_iota(jnp.int32, v.shape, 1) % SW_HD
    return jnp.where(lane < half, pltpu.roll(v, w - half, 1), jnp.where(lane < 2 * half, pltpu.roll(v, half, 1), 0.0))


def _sw_norm_rope(tv, gain, seg, cosv, sinv):
    w = tv.shape[1]
    ms = _split_dot_rhs(tv * tv, seg[0:w, 0:w])
    r = lax.rsqrt(ms + EPS)
    tn = tv * r * gain
    reps = w // 128
    return tn * _tile_lanes(cosv, reps) + _swap_halves(tn) * _tile_lanes(sinv, reps), r


def _split_dot_rhs(v, a):
    hi = _mx(v)
    lo = _mx(v - hi.astype(F32))
    return (lax.dot_general(hi, a, (NN, ((), ())), preferred_element_type=F32)
            + lax.dot_general(lo, a, (NN, ((), ())), preferred_element_type=F32))


def _sw_norm_rope_bwd(dt, tv, r, gain, seg, cosv, sinv):
    w = tv.shape[1]
    reps = w // 128
    dtn = dt * _tile_lanes(cosv, reps) + _swap_halves(dt * _tile_lanes(sinv, reps))
    u = dtn * gain
    dtv = r * u - tv * (r * r * r) * _split_dot_rhs(u * tv, seg[0:w, 0:w])
    return dtv, jnp.sum(dtn * tv * r, axis=0, keepdims=True)


def _sw_probs(qh, kp, kc, sink, first_block):
    scale = SW_HD ** -0.5
    qi = lax.broadcasted_iota(jnp.int32, (SW_BLOCK, SW_BLOCK), 0)
    kj = lax.broadcasted_iota(jnp.int32, (SW_BLOCK, SW_BLOCK), 1)
    ok_prev = jnp.logical_and(kj > qi, jnp.logical_not(first_block))
    ok_cur = kj <= qi
    sp = jnp.where(ok_prev, _dot(qh, kp, NT) * scale, -jnp.inf)
    sc = jnp.where(ok_cur, _dot(qh, kc, NT) * scale, -jnp.inf)
    m = jnp.maximum(jnp.maximum(jnp.max(sp, axis=1, keepdims=True), jnp.max(sc, axis=1, keepdims=True)), sink)
    pp, pc = jnp.exp(sp - m), jnp.exp(sc - m)
    es = jnp.exp(sink - m)
    den = jnp.sum(pp, axis=1, keepdims=True) + jnp.sum(pc, axis=1, keepdims=True) + es
    return pp / den, pc / den, es / den


def _sw_specs(nb):
    def cur(b, n):
        return b * nb + jnp.minimum(n, nb - 1)

    def prev(b, n):
        return b * nb + jnp.maximum(jnp.minimum(n, nb - 1) - 1, 0)

    return cur, prev


def _sw_fwd(proj, pos, qg, kg, sinks, y_in, bsz, seq):
    t = proj.shape[0]
    nb = seq // SW_BLOCK
    freq_np, sign_np, seg_np = _sw_constants()
    freq, sign = jnp.asarray(freq_np), jnp.asarray(sign_np)
    seg = jnp.asarray(seg_np, _MXU_DTYPE)
    cur, prev = _sw_specs(nb)

    def body(q_ref, kc_ref, kp_ref, vc_ref, vp_ref, pc_ref, pp_ref, qg_ref, kg_ref, sk_ref, fr_ref, sn_ref, seg_ref,
             yin_ref, y_ref):
        del yin_ref
        n = pl.program_id(1)
        segv = seg_ref[...]
        cos_c, sin_c = _rope_tables(pc_ref[...], fr_ref[...], sn_ref[...])
        cos_p, sin_p = _rope_tables(pp_ref[...], fr_ref[...], sn_ref[...])
        qr, _ = _sw_norm_rope(q_ref[...], qg_ref[...], segv, cos_c, sin_c)
        kcr, _ = _sw_norm_rope(kc_ref[...], kg_ref[...], segv, cos_c, sin_c)
        kpr, _ = _sw_norm_rope(kp_ref[...], kg_ref[...], segv, cos_p, sin_p)
        vc, vp = vc_ref[...], vp_ref[...]
        for h in range(SW_HEADS):
            kv = h // SW_GROUP
            ks = slice(SW_HD * kv, SW_HD * (kv + 1))
            pp, pc, _ = _sw_probs(qr[:, SW_HD * h:SW_HD * (h + 1)], kpr[:, ks], kcr[:, ks], sk_ref[0, h], n == 0)
            y_ref[:, SW_HD * h:SW_HD * (h + 1)] = _dot(pp, vp[:, ks]) + _dot(pc, vc[:, ks])

    rowq = pl.BlockSpec((SW_BLOCK, 512), lambda b, n: (cur(b, n), 0))
    full = lambda a: pl.BlockSpec(a.shape, lambda b, n: (0,) * a.ndim)
    yw = y_in.shape[1]
    return pl.pallas_call(
        body, name="swa_fwd", grid=(bsz, nb),
        in_specs=[rowq,
                  pl.BlockSpec((SW_BLOCK, 128), lambda b, n: (cur(b, n), 4)),
                  pl.BlockSpec((SW_BLOCK, 128), lambda b, n: (prev(b, n), 4)),
                  pl.BlockSpec((SW_BLOCK, 128), lambda b, n: (cur(b, n), 5)),
                  pl.BlockSpec((SW_BLOCK, 128), lambda b, n: (prev(b, n), 5)),
                  pl.BlockSpec((SW_BLOCK, 1), lambda b, n: (cur(b, n), 0)),
                  pl.BlockSpec((SW_BLOCK, 1), lambda b, n: (prev(b, n), 0)),
                  full(qg), full(kg),
                  pl.BlockSpec(memory_space=pltpu.SMEM),
                  full(freq), full(sign), full(seg),
                  pl.BlockSpec(memory_space=pl.ANY)],
        out_specs=pl.BlockSpec((SW_BLOCK, 512), lambda b, n: (cur(b, n), 1)),
        out_shape=jax.ShapeDtypeStruct((t, yw), F32),
        input_output_aliases={13: 0},
        compiler_params=_params(("parallel", "parallel")),
    )(proj, proj, proj, proj, proj, pos, pos, qg, kg, sinks, freq, sign, seg, y_in)


def _sw_bwd(proj, pos, qg, kg, sinks, y, dy, bsz, seq):
    t = proj.shape[0]
    nb = seq // SW_BLOCK
    freq_np, sign_np, seg_np = _sw_constants()
    freq, sign = jnp.asarray(freq_np), jnp.asarray(sign_np)
    seg = jnp.asarray(seg_np, _MXU_DTYPE)
    cur, prev = _sw_specs(nb)
    scale = SW_HD ** -0.5

    def body(q_ref, kc_ref, kp_ref, vc_ref, vp_ref, pc_ref, pp_ref, qg_ref, kg_ref, sk_ref, fr_ref, sn_ref, seg_ref,
             y_ref, dy_ref, dp_ref, dqg_ref, dkg_ref, dsk_ref,
             dq_car, dkv_car, dqr_s, dkc_s, dkp_s, dvc_s, dvp_s, gq_acc, gk_acc, sk_acc):
        b, n = pl.program_id(0), pl.program_id(1)
        first = jnp.logical_and(b == 0, n == 0)
        last = jnp.logical_and(b == pl.num_programs(0) - 1, n == nb)

        @pl.when(first)
        def _():
            gq_acc[...] = jnp.zeros_like(gq_acc)
            gk_acc[...] = jnp.zeros_like(gk_acc)
            sk_acc[...] = jnp.zeros_like(sk_acc)

        @pl.when(n < nb)
        def _():
            segv = seg_ref[...]
            cos_c, sin_c = _rope_tables(pc_ref[...], fr_ref[...], sn_ref[...])
            cos_p, sin_p = _rope_tables(pp_ref[...], fr_ref[...], sn_ref[...])
            qv, kcv, kpv = q_ref[...], kc_ref[...], kp_ref[...]
            qr, rq = _sw_norm_rope(qv, qg_ref[...], segv, cos_c, sin_c)
            kcr, rkc = _sw_norm_rope(kcv, kg_ref[...], segv, cos_c, sin_c)
            kpr, rkp = _sw_norm_rope(kpv, kg_ref[...], segv, cos_p, sin_p)
            vc, vp = vc_ref[...], vp_ref[...]
            dkc_s[...] = jnp.zeros_like(dkc_s)
            dkp_s[...] = jnp.zeros_like(dkp_s)
            dvc_s[...] = jnp.zeros_like(dvc_s)
            dvp_s[...] = jnp.zeros_like(dvp_s)
            lane = lax.broadcasted_iota(jnp.int32, (1, 128), 1)
            dsk = jnp.zeros((1, 128), F32)
            for h in range(SW_HEADS):
                kv = h // SW_GROUP
                ks = slice(SW_HD * kv, SW_HD * (kv + 1))
                hs = slice(SW_HD * h, SW_HD * (h + 1))
                qh = qr[:, hs]
                pp, pc, ps = _sw_probs(qh, kpr[:, ks], kcr[:, ks], sk_ref[0, h], n == 0)
                doh = dy_ref[:, hs]
                delta = jnp.sum(doh * y_ref[:, hs], axis=1, keepdims=True)
                dsp = pp * (_dot(doh, vp[:, ks], NT) - delta) * scale
                dsc = pc * (_dot(doh, vc[:, ks], NT) - delta) * scale
                dsk = dsk + jnp.where(lane == h, -jnp.sum(ps * delta), 0.0)
                dvp_s[:, ks] += _dot(pp, doh, TN)
                dvc_s[:, ks] += _dot(pc, doh, TN)
                dqr_s[:, hs] = _dot(dsp, kpr[:, ks]) + _dot(dsc, kcr[:, ks])
                dkp_s[:, ks] += _dot(dsp, qh, TN)
                dkc_s[:, ks] += _dot(dsc, qh, TN)
            dq, gq = _sw_norm_rope_bwd(dqr_s[...], qv, rq, qg_ref[...], segv, cos_c, sin_c)
            dkc, gkc = _sw_norm_rope_bwd(dkc_s[...], kcv, rkc, kg_ref[...], segv, cos_c, sin_c)
            dkp, gkp = _sw_norm_rope_bwd(dkp_s[...], kpv, rkp, kg_ref[...], segv, cos_p, sin_p)
            gq_acc[...] += gq
            gk_acc[...] += gkc + gkp
            sk_acc[...] += dsk

            @pl.when(n > 0)
            def _():
                dp_ref[:, 0:512] = dq_car[...]
                dp_ref[:, 512:640] = dkv_car[:, 0:128] + dkp
                dp_ref[:, 640:768] = dkv_car[:, 128:256] + dvp_s[...]

            dq_car[...] = dq
            dkv_car[:, 0:128] = dkc
            dkv_car[:, 128:256] = dvc_s[...]

        @pl.when(n == nb)
        def _():
            dp_ref[:, 0:512] = dq_car[...]
            dp_ref[:, 512:768] = dkv_car[...]

        @pl.when(last)
        def _():
            gq = gq_acc[...]
            acc = gq[:, 0:SW_HD]
            for h in range(1, SW_HEADS):
                acc = acc + gq[:, SW_HD * h:SW_HD * (h + 1)]
            dqg_ref[...] = acc
            gk = gk_acc[...]
            dkg_ref[...] = gk[:, 0:SW_HD] + gk[:, SW_HD:2 * SW_HD]
            dsk_ref[...] = sk_acc[...]

    rowq = pl.BlockSpec((SW_BLOCK, 512), lambda b, n: (cur(b, n), 0))
    full = lambda a: pl.BlockSpec(a.shape, lambda b, n: (0,) * a.ndim)

    def out_row(b, n):
        return b * nb + jnp.maximum(n - 1, 0)

    return pl.pallas_call(
        body, name="swa_bwd", grid=(bsz, nb + 1),
        in_specs=[rowq,
                  pl.BlockSpec((SW_BLOCK, 128), lambda b, n: (cur(b, n), 4)),
                  pl.BlockSpec((SW_BLOCK, 128), lambda b, n: (prev(b, n), 4)),
                  pl.BlockSpec((SW_BLOCK, 128), lambda b, n: (cur(b, n), 5)),
                  pl.BlockSpec((SW_BLOCK, 128), lambda b, n: (prev(b, n), 5)),
                  pl.BlockSpec((SW_BLOCK, 1), lambda b, n: (cur(b, n), 0)),
                  pl.BlockSpec((SW_BLOCK, 1), lambda b, n: (prev(b, n), 0)),
                  full(qg), full(kg),
                  pl.BlockSpec(memory_space=pltpu.SMEM),
                  full(freq), full(sign), full(seg),
                  pl.BlockSpec((SW_BLOCK, 512), lambda b, n: (cur(b, n), 1)),
                  pl.BlockSpec((SW_BLOCK, 512), lambda b, n: (cur(b, n), 1))],
        out_specs=(pl.BlockSpec((SW_BLOCK, SW_COLS), lambda b, n: (out_row(b, n), 0)),
                   pl.BlockSpec((1, SW_HD), lambda b, n: (0, 0)),
                   pl.BlockSpec((1, SW_HD), lambda b, n: (0, 0)),
                   pl.BlockSpec((1, 128), lambda b, n: (0, 0))),
        out_shape=(jax.ShapeDtypeStruct((t, SW_COLS), F32),
                   jax.ShapeDtypeStruct((1, SW_HD), F32),
                   jax.ShapeDtypeStruct((1, SW_HD), F32),
                   jax.ShapeDtypeStruct((1, 128), F32)),
        scratch_shapes=[pltpu.VMEM((SW_BLOCK, 512), F32), pltpu.VMEM((SW_BLOCK, 256), F32),
                        pltpu.VMEM((SW_BLOCK, 512), F32),
                        pltpu.VMEM((SW_BLOCK, 128), F32), pltpu.VMEM((SW_BLOCK, 128), F32),
                        pltpu.VMEM((SW_BLOCK, 128), F32), pltpu.VMEM((SW_BLOCK, 128), F32),
                        pltpu.VMEM((1, 512), F32), pltpu.VMEM((1, 128), F32), pltpu.VMEM((1, 128), F32)],
        compiler_params=_params(("arbitrary", "arbitrary")),
    )(proj, proj, proj, proj, proj, pos, pos, qg, kg, sinks, freq, sign, seg, y, dy)


def _head_rms(tv, gain):
    r = lax.rsqrt(jnp.mean(tv * tv, axis=1, keepdims=True) + EPS)
    return tv * r * gain, r


def _head_rms_bwd(dtn, tv, r, gain):
    u = dtn * gain
    return r * u - tv * (r * r * r) * jnp.mean(u * tv, axis=1, keepdims=True), jnp.sum(dtn * tv * r, axis=0, keepdims=True)


def _xa_probs(qn, kn):
    s = _dot(qn, kn, NT) * (XA_HD ** -0.5)
    e = jnp.exp(s - jnp.max(s, axis=1, keepdims=True))
    return e / jnp.sum(e, axis=1, keepdims=True)


def _xa_fwd(qx, kvx, qg, kg, bsz, seq, mlen, *, tq=512):
    t = qx.shape[0]
    tq = min(tq, seq)
    nq = seq // tq
    w = XA_HEADS * XA_HD

    def body(q_ref, kv_ref, qg_ref, kg_ref, o_ref):
        for h in range(XA_HEADS):
            hs = slice(XA_HD * h, XA_HD * (h + 1))
            qn, _ = _head_rms(q_ref[:, hs], qg_ref[...])
            kn, _ = _head_rms(kv_ref[:, hs], kg_ref[...])
            o_ref[:, hs] = _dot(_xa_probs(qn, kn), kv_ref[:, w + XA_HD * h:w + XA_HD * (h + 1)])

    vec = pl.BlockSpec((1, XA_HD), lambda b, i: (0, 0))
    return pl.pallas_call(
        body, name="xattn_fwd", grid=(bsz, nq),
        in_specs=[pl.BlockSpec((tq, w), lambda b, i: (b * nq + i, 0)),
                  pl.BlockSpec((mlen, 2 * w), lambda b, i: (b, 0)), vec, vec],
        out_specs=pl.BlockSpec((tq, w), lambda b, i: (b * nq + i, 0)),
        out_shape=jax.ShapeDtypeStruct((t, w), F32),
        compiler_params=_params(("parallel", "parallel")),
    )(qx, kvx, qg, kg)


def _xa_bwd(qx, kvx, qg, kg, do, bsz, seq, mlen, *, tq=512):
    t = qx.shape[0]
    tq = min(tq, seq)
    nq = seq // tq
    w = XA_HEADS * XA_HD
    scale = XA_HD ** -0.5

    def body(q_ref, kv_ref, qg_ref, kg_ref, do_ref, dq_ref, dkv_ref, dqg_ref, dkg_ref):
        b, i = pl.program_id(0), pl.program_id(1)

        @pl.when(jnp.logical_and(b == 0, i == 0))
        def _():
            dqg_ref[...] = jnp.zeros_like(dqg_ref)
            dkg_ref[...] = jnp.zeros_like(dkg_ref)

        @pl.when(i == 0)
        def _():
            dkv_ref[...] = jnp.zeros_like(dkv_ref)

        gq_sum = jnp.zeros((1, XA_HD), F32)
        gk_sum = jnp.zeros((1, XA_HD), F32)
        for h in range(XA_HEADS):
            hs = slice(XA_HD * h, XA_HD * (h + 1))
            vs = slice(w + XA_HD * h, w + XA_HD * (h + 1))
            qv, kv, vv = q_ref[:, hs], kv_ref[:, hs], kv_ref[:, vs]
            qn, rq = _head_rms(qv, qg_ref[...])
            kn, rk = _head_rms(kv, kg_ref[...])
            p = _xa_probs(qn, kn)
            doh = do_ref[:, hs]
            dp = _dot(doh, vv, NT)
            ds = p * (dp - jnp.sum(p * dp, axis=1, keepdims=True)) * scale
            dqv, gq = _head_rms_bwd(_dot(ds, kn), qv, rq, qg_ref[...])
            dkv, gk = _head_rms_bwd(_dot(ds, qn, TN), kv, rk, kg_ref[...])
            dq_ref[:, hs] = dqv
            dkv_ref[:, hs] += dkv
            dkv_ref[:, vs] += _dot(p, doh, TN)
            gq_sum = gq_sum + gq
            gk_sum = gk_sum + gk
        dqg_ref[...] += gq_sum
        dkg_ref[...] += gk_sum

    vec = pl.BlockSpec((1, XA_HD), lambda b, i: (0, 0))
    row = pl.BlockSpec((tq, w), lambda b, i: (b * nq + i, 0))
    mem = pl.BlockSpec((mlen, 2 * w), lambda b, i: (b, 0))
    return pl.pallas_call(
        body, name="xattn_bwd", grid=(bsz, nq),
        in_specs=[row, mem, vec, vec, row],
        out_specs=(row, mem, vec, vec),
        out_shape=(jax.ShapeDtypeStruct((t, w), F32), jax.ShapeDtypeStruct((bsz * mlen, 2 * w), F32),
                   jax.ShapeDtypeStruct((1, XA_HD), F32), jax.ShapeDtypeStruct((1, XA_HD), F32)),
        compiler_params=_params(("arbitrary", "arbitrary")),
    )(qx, kvx, qg, kg, do)


def _loss_sum(dy, d_model, *, tm=512):
    t, d = dy.shape
    tm = min(tm, t)
    steps = t // tm

    def body(dy_ref, o_ref, acc_ref):
        i = pl.program_id(0)

        @pl.when(i == 0)
        def _():
            acc_ref[...] = jnp.zeros_like(acc_ref)

        diff = dy_ref[...] * float(d_model)
        acc_ref[...] += jnp.sum(diff * diff, axis=0, keepdims=True)

        @pl.when(i == steps - 1)
        def _():
            o_ref[...] = jnp.zeros_like(o_ref) + 0.5 * jnp.sum(acc_ref[...]) / float(d_model)

    return pl.pallas_call(
        body, name="loss_sum", grid=(steps,),
        in_specs=[pl.BlockSpec((tm, d), lambda i: (i, 0))],
        out_specs=pl.BlockSpec((1, 128), lambda i: (0, 0)),
        out_shape=jax.ShapeDtypeStruct((1, 128), F32),
        scratch_shapes=[pltpu.VMEM((1, d), F32)],
        compiler_params=_params(("arbitrary",)),
    )(dy)


def _adamw_math(w, g, m, v):
    m = ADAM_B1 * m + (1.0 - ADAM_B1) * g
    v = ADAM_B2 * v + (1.0 - ADAM_B2) * (g * g)
    m_hat = m / (1.0 - ADAM_B1 ** ADAM_STEP)
    v_hat = v / (1.0 - ADAM_B2 ** ADAM_STEP)
    return -ADAM_LR * (m_hat / (jnp.sqrt(v_hat) + ADAM_EPS) + ADAM_WD * w), m, v


def _adamw_big(w, g, m, v, *, name, tr=256):
    r, c = w.shape
    tr = min(tr, r)

    def body(w_ref, g_ref, m_ref, v_ref, d_ref, mo_ref, vo_ref):
        d, mn, vn = _adamw_math(w_ref[...], g_ref[...], m_ref[...], v_ref[...])
        d_ref[...] = d
        mo_ref[...] = mn
        vo_ref[...] = vn

    spec = pl.BlockSpec((tr, c), lambda i: (i, 0))
    shp = jax.ShapeDtypeStruct((r, c), F32)
    return pl.pallas_call(
        body, name=name, grid=(r // tr,), in_specs=[spec] * 4, out_specs=(spec,) * 3, out_shape=(shp,) * 3,
        compiler_params=_params(("parallel",)),
    )(w, g, m, v)


def _adamw_small(ws, gs, ms, vs):
    n = len(ws)

    def body(*refs):
        for i in range(n):
            d, mn, vn = _adamw_math(refs[i][...], refs[n + i][...], refs[2 * n + i][...], refs[3 * n + i][...])
            refs[4 * n + i][...] = d
            refs[5 * n + i][...] = mn
            refs[6 * n + i][...] = vn

    shapes = tuple(jax.ShapeDtypeStruct(w.shape, F32) for w in ws)
    return pl.pallas_call(body, name="adamw_small", out_shape=shapes * 3)(*ws, *gs, *ms, *vs)


def _add_halves(g, recv, c_idx, *, name, tr=256):
    _, r, c = g.shape
    h = r // 2
    tr = min(tr, h)
    nt = h // tr

    def body(c_ref, g_ref, r_ref, o_ref):
        del c_ref
        o_ref[...] = g_ref[...] + r_ref[...]

    return pl.pallas_call(
        body, name=name,
        grid_spec=pltpu.PrefetchScalarGridSpec(
            num_scalar_prefetch=1, grid=(4, nt),
            in_specs=[pl.BlockSpec((None, tr, c), lambda k, i, cr: (k, cr[0] * nt + i, 0)),
                      pl.BlockSpec((None, tr, c), lambda k, i, cr: (k, i, 0))],
            out_specs=pl.BlockSpec((None, tr, c), lambda k, i, cr: (k, i, 0))),
        out_shape=jax.ShapeDtypeStruct((4, h, c), F32),
        compiler_params=_params(("parallel", "parallel")),
    )(c_idx, g, recv)


def _add_chips(p, recv, place_idx, *, name, tr=256):
    _, h, c = p.shape
    tr = min(tr, h)
    nt = h // tr

    def body(pi_ref, p_ref, r_ref, o_ref):
        del pi_ref
        o_ref[...] = ((p_ref[...] + r_ref[0]) + r_ref[1]) + r_ref[2]

    return pl.pallas_call(
        body, name=name,
        grid_spec=pltpu.PrefetchScalarGridSpec(
            num_scalar_prefetch=1, grid=(nt,),
            in_specs=[pl.BlockSpec((None, tr, c), lambda i, pi: (pi[0], i, 0)),
                      pl.BlockSpec((3, tr, c), lambda i, pi: (0, i, 0))],
            out_specs=pl.BlockSpec((tr, c), lambda i, pi: (pi[1] * nt + i, 0))),
        out_shape=jax.ShapeDtypeStruct((2 * h, c), F32),
        compiler_params=_params(("parallel",)),
    )(place_idx, p, recv)


def _place_shard(shard, place_idx, *, name, tr=256):
    r, c = shard.shape
    tr = min(tr, r)

    def body(pi_ref, s_ref, o_ref):
        del pi_ref
        o_ref[...] = s_ref[...]

    return pl.pallas_call(
        body, name=name,
        grid_spec=pltpu.PrefetchScalarGridSpec(
            num_scalar_prefetch=1, grid=(r // tr,),
            in_specs=[pl.BlockSpec((tr, c), lambda i, pi: (i, 0))],
            out_specs=pl.BlockSpec((None, tr, c), lambda i, pi: (pi[0], i, 0))),
        out_shape=jax.ShapeDtypeStruct((4, r, c), shard.dtype),
        compiler_params=_params(("parallel",)),
    )(place_idx, shard)


def _place():
    x, y, c = lax.axis_index("x"), lax.axis_index("y"), lax.axis_index("c")
    chips = [(1 - x, y), (x, 1 - y), (1 - x, 1 - y)]
    return x, y, c, chips


ANY = pl.BlockSpec(memory_space=pl.ANY)


def _all_gather_weights(shards, placed):
    n = len(shards)

    def body(*refs):
        ins, outs = refs[:n], refs[2 * n:3 * n]
        send_sems, recv_sems = refs[3 * n:]
        x, y, c, chips = _place()
        me = 2 * x + y

        def half(a, chip_idx, which):
            h = ins[a].shape[0] // 2
            return outs[a].at[chip_idx, pl.ds(which * h, h), :]

        def copy(a, j, chip_idx, which, to, src=None):
            return pltpu.make_async_remote_copy(
                src_ref=half(a, chip_idx, which) if src is None else src, dst_ref=half(a, chip_idx, which),
                send_sem=send_sems.at[a * 6 + j], recv_sem=recv_sems.at[a * 6 + j], device_id=to, device_id_type=MESH)

        for a in range(n):
            h = ins[a].shape[0] // 2
            for j, (px, py) in enumerate(chips):
                copy(a, j, me, c, (px, py, c), src=ins[a].at[pl.ds(c * h, h), :]).start()
        for a in range(n):
            for j, (px, py) in enumerate(chips):
                copy(a, j, 2 * px + py, c, (x, y, c)).wait_recv()
                copy(a, 3 + j, 2 * px + py, c, (x, y, 1 - c)).start()
        for a in range(n):
            for j, (px, py) in enumerate(chips):
                copy(a, 3 + j, 2 * px + py, 1 - c, (x, y, c)).wait_recv()
        for a in range(n):
            h = ins[a].shape[0] // 2
            for j, (px, py) in enumerate(chips):
                copy(a, j, me, c, (px, py, c), src=ins[a].at[pl.ds(c * h, h), :]).wait_send()
                copy(a, 3 + j, 2 * px + py, c, (x, y, 1 - c)).wait_send()

    return pl.pallas_call(
        body, name="all_gather_weights",
        in_specs=[ANY] * (2 * n), out_specs=tuple([ANY] * n),
        out_shape=tuple(jax.ShapeDtypeStruct(p.shape, p.dtype) for p in placed),
        input_output_aliases={n + i: i for i in range(n)},
        scratch_shapes=[pltpu.SemaphoreType.DMA((6 * n,)), pltpu.SemaphoreType.DMA((6 * n,))],
    )(*shards, *placed)


def _exchange_halves(grads, name):
    n = len(grads)

    def body(*refs):
        ins, outs = refs[:n], refs[n:2 * n]
        send_sems, recv_sems = refs[2 * n:]
        x, y, c, _ = _place()

        def copy(a):
            h = ins[a].shape[1] // 2
            return pltpu.make_async_remote_copy(
                src_ref=ins[a].at[:, pl.ds((1 - c) * h, h), :], dst_ref=outs[a],
                send_sem=send_sems.at[a], recv_sem=recv_sems.at[a], device_id=(x, y, 1 - c), device_id_type=MESH)

        for a in range(n):
            copy(a).start()
        for a in range(n):
            copy(a).wait_recv()
        for a in range(n):
            copy(a).wait_send()

    return pl.pallas_call(
        body, name=name,
        in_specs=[ANY] * n, out_specs=tuple([ANY] * n),
        out_shape=tuple(jax.ShapeDtypeStruct((4, g.shape[1] // 2, g.shape[2]), g.dtype) for g in grads),
        scratch_shapes=[pltpu.SemaphoreType.DMA((n,)), pltpu.SemaphoreType.DMA((n,))],
    )(*grads)


def _scatter_chips(parts, name):
    n = len(parts)

    def body(*refs):
        ins, outs = refs[:n], refs[n:2 * n]
        send_sems, recv_sems = refs[2 * n:]
        x, y, c, chips = _place()

        def copy(a, j, chip_idx, to):
            return pltpu.make_async_remote_copy(
                src_ref=ins[a].at[chip_idx], dst_ref=outs[a].at[j],
                send_sem=send_sems.at[a * 3 + j], recv_sem=recv_sems.at[a * 3 + j], device_id=to, device_id_type=MESH)

        for a in range(n):
            for j, (px, py) in enumerate(chips):
                copy(a, j, 2 * px + py, (px, py, c)).start()
        for a in range(n):
            for j, (px, py) in enumerate(chips):
                copy(a, j, 2 * px + py, (px, py, c)).wait_recv()
        for a in range(n):
            for j, (px, py) in enumerate(chips):
                copy(a, j, 2 * px + py, (px, py, c)).wait_send()

    return pl.pallas_call(
        body, name=name,
        in_specs=[ANY] * n, out_specs=tuple([ANY] * n),
        out_shape=tuple(jax.ShapeDtypeStruct((3,) + p.shape[1:], p.dtype) for p in parts),
        scratch_shapes=[pltpu.SemaphoreType.DMA((3 * n,)), pltpu.SemaphoreType.DMA((3 * n,))],
    )(*parts)


HBM = pl.BlockSpec(memory_space=pltpu.HBM)
SEM = pl.BlockSpec(memory_space=pltpu.SEMAPHORE)
EFFECT = pltpu.SideEffectType.DATAFLOW_SIDE_EFFECTING


def _in_hbm(a):
    return pltpu.with_memory_space_constraint(a, pltpu.HBM)


def _split_copy_calls(name, srcs, lands, n_copies, make_copies):
    ns, nl = len(srcs), len(lands)
    nb = ns + nl

    def start():
        def body(*refs):
            copies = make_copies(refs[:ns], refs[ns:nb], refs[nb], refs[nb + 1])
            for cp in copies:
                cp.start()
            token = refs[-1]
            token[...] = jnp.zeros_like(token)

        bufs = [_in_hbm(a) for a in list(srcs) + list(lands)]
        out = pl.pallas_call(
            body, name=name + "_start",
            out_shape=(pltpu.SemaphoreType.DMA((n_copies,)), pltpu.SemaphoreType.DMA((n_copies,)),
                       *[pltpu.HBM(a.shape, a.dtype) for a in bufs], jax.ShapeDtypeStruct((8, 128), F32)),
            in_specs=[HBM] * nb, out_specs=(SEM, SEM, *[HBM] * nb, pl.BlockSpec(memory_space=pltpu.VMEM)),
            input_output_aliases={i: 2 + i for i in range(nb)},
            compiler_params=pltpu.CompilerParams(has_side_effects=EFFECT),
        )(*bufs)
        return dict(send=out[0], recv=out[1], bufs=list(out[2:2 + nb]), token=out[-1])

    def wait(state, after):
        def body(*refs):
            copies = make_copies(refs[:ns], refs[ns:nb], refs[nb], refs[nb + 1])
            for cp in copies:
                cp.wait_send()
            for cp in copies:
                cp.wait_recv()

        bufs = state["bufs"]
        out = pl.pallas_call(
            body, name=name + "_wait",
            out_shape=tuple(pltpu.HBM(a.shape, a.dtype) for a in bufs),
            in_specs=[HBM] * nb + [SEM, SEM, pl.BlockSpec(memory_space=pl.ANY)], out_specs=tuple([HBM] * nb),
            input_output_aliases={i: i for i in range(nb)},
            compiler_params=pltpu.CompilerParams(has_side_effects=EFFECT),
        )(*bufs, state["send"], state["recv"], after)
        return list(out[:ns]), list(out[ns:])

    return start, wait


def _scatter_chips_split(name, parts):
    n = len(parts)
    lands = [lax.empty((3,) + p.shape[1:], p.dtype) for p in parts]

    def make_copies(srcs, lnds, send_sems, recv_sems):
        _, _, c, chips = _place()
        return [pltpu.make_async_remote_copy(
            src_ref=srcs[a].at[2 * px + py], dst_ref=lnds[a].at[j], send_sem=send_sems.at[a * 3 + j],
            recv_sem=recv_sems.at[a * 3 + j], device_id=(px, py, c), device_id_type=MESH)
            for a in range(n) for j, (px, py) in enumerate(chips)]

    return _split_copy_calls(name, parts, lands, 3 * n, make_copies)


def _gather_chips_split(name, shards, lands):
    n = len(shards)

    def make_copies(srcs, lnds, send_sems, recv_sems):
        x, y, c, chips = _place()
        out = []
        for a in range(n):
            h = srcs[a].shape[0] // 2
            for j, (px, py) in enumerate(chips):
                out.append(pltpu.make_async_remote_copy(
                    src_ref=srcs[a].at[pl.ds(c * h, h), :], dst_ref=lnds[a].at[2 * x + y, pl.ds(c * h, h), :],
                    send_sem=send_sems.at[a * 3 + j], recv_sem=recv_sems.at[a * 3 + j],
                    device_id=(px, py, c), device_id_type=MESH))
        return out

    return _split_copy_calls(name, shards, lands, 3 * n, make_copies)


def _gather_finish(gathered):
    n = len(gathered)

    def body(*refs):
        outs = refs[n:2 * n]
        send_sems, recv_sems = refs[2 * n:]
        x, y, c, chips = _place()

        def copy(a, j, chip_idx, which):
            h = outs[a].shape[1] // 2
            rows = outs[a].at[chip_idx, pl.ds(which * h, h), :]
            return pltpu.make_async_remote_copy(
                src_ref=rows, dst_ref=rows, send_sem=send_sems.at[a * 3 + j], recv_sem=recv_sems.at[a * 3 + j],
                device_id=(x, y, 1 - c), device_id_type=MESH)

        for a in range(n):
            for j, (px, py) in enumerate(chips):
                copy(a, j, 2 * px + py, c).start()
        for a in range(n):
            for j, (px, py) in enumerate(chips):
                copy(a, j, 2 * px + py, 1 - c).wait_recv()
        for a in range(n):
            for j, (px, py) in enumerate(chips):
                copy(a, j, 2 * px + py, c).wait_send()

    return pl.pallas_call(
        body, name="gather_finish",
        in_specs=[ANY] * n, out_specs=tuple([ANY] * n),
        out_shape=tuple(jax.ShapeDtypeStruct(g.shape, g.dtype) for g in gathered),
        input_output_aliases={i: i for i in range(n)},
        scratch_shapes=[pltpu.SemaphoreType.DMA((3 * n,)), pltpu.SemaphoreType.DMA((3 * n,))],
    )(*gathered)


def _join_halves(fulls):
    n = len(fulls)

    def body(*refs):
        outs = refs[n:2 * n]
        send_sems, recv_sems = refs[2 * n:]
        x, y, c, _ = _place()

        def copy(a, which):
            h = outs[a].shape[0] // 2
            rows = outs[a].at[pl.ds(which * h, h), :]
            return pltpu.make_async_remote_copy(
                src_ref=rows, dst_ref=rows, send_sem=send_sems.at[a], recv_sem=recv_sems.at[a],
                device_id=(x, y, 1 - c), device_id_type=MESH)

        for a in range(n):
            copy(a, c).start()
        for a in range(n):
            copy(a, 1 - c).wait_recv()
        for a in range(n):
            copy(a, c).wait_send()

    return pl.pallas_call(
        body, name="rs_join_halves",
        in_specs=[ANY] * n, out_specs=tuple([ANY] * n),
        out_shape=tuple(jax.ShapeDtypeStruct(p.shape, p.dtype) for p in fulls),
        input_output_aliases={i: i for i in range(n)},
        scratch_shapes=[pltpu.SemaphoreType.DMA((n,)), pltpu.SemaphoreType.DMA((n,))],
    )(*fulls)


def _all_reduce_small(sm):
    r, w = sm.shape

    def body(sm_ref, o_ref, buf, send_sems, recv_sems):
        x, y, c, _ = _place()
        me = 4 * x + 2 * y + c
        buf[me] = sm_ref[...]
        rel = [(dx, dy, dc) for dx in (0, 1) for dy in (0, 1) for dc in (0, 1)][1:]

        def copy(k, slot, to):
            return pltpu.make_async_remote_copy(
                src_ref=sm_ref, dst_ref=buf.at[slot], send_sem=send_sems.at[k], recv_sem=recv_sems.at[k],
                device_id=to, device_id_type=MESH)

        peers = []
        for k, (dx, dy, dc) in enumerate(rel):
            px = 1 - x if dx else x
            py = 1 - y if dy else y
            pc = 1 - c if dc else c
            peers.append((px, py, pc))
            copy(k, me, (px, py, pc)).start()
        for k, (px, py, pc) in enumerate(peers):
            copy(k, 4 * px + 2 * py + pc, (px, py, pc)).wait_recv()
        for k, (px, py, pc) in enumerate(peers):
            copy(k, me, (px, py, pc)).wait_send()
        acc = buf[0]
        for d in range(1, 8):
            acc = acc + buf[d]
        o_ref[...] = acc

    vm = pl.BlockSpec(memory_space=pltpu.VMEM)
    return pl.pallas_call(
        body, name="all_reduce_small", in_specs=[vm], out_specs=vm,
        out_shape=jax.ShapeDtypeStruct((r, w), F32),
        scratch_shapes=[pltpu.VMEM((8, r, w), F32), pltpu.SemaphoreType.DMA((7,)), pltpu.SemaphoreType.DMA((7,))],
    )(sm)


class _LocalWeights:
    def __init__(self, w):
        self.w = w
        self.g = {}

    def first(self):
        return self.w

    def rest(self, after):
        del after
        return self.w

    def grads(self, tag, g):
        del tag
        self.g.update(g)
        return None


def _local_step(x3, mem3, pos2, target3, small, comm):
    bsz, seq, d = x3.shape
    mlen = mem3.shape[1]
    t = bsz * seq
    ds = d // 4
    w = comm.first()
    x = x3.reshape(t, d)
    mem = mem3.reshape(bsz * mlen, d)
    target = target3.reshape(t, d)
    pos = pos2.reshape(t, 1)
    qg_t = jnp.tile(small["sw_q_norm_g"], (1, SW_HEADS))
    kg_t = jnp.tile(small["sw_k_norm_g"], (1, SW_KV_HEADS))

    hn1 = _rms_fwd(x, small["norm1_g"], name="rms1_fwd")
    proj_hg = _mm(hn1, w["w_in_hg"], NN, t, HG_COLS, d, name="proj_hg", tk=d, after=(w.get("token"),))[0]
    proj_sw = _mm(hn1, w["w_in_sw"], NN, t, SW_COLS, d, name="proj_sw", tk=d)[0]
    y_mix, o_hg, states = _hg_fwd(proj_hg, small["hg_lower_bounds"], small["hg_norm_g"], bsz, seq, y_width=1024)
    y_mix = _sw_fwd(proj_sw, pos, qg_t, kg_t, small["sw_sinks"], y_mix, bsz, seq)
    w_in_hg, w_in_sw = w["w_in_hg"], w["w_in_sw"]
    w = comm.rest(y_mix)
    ff = w["down"].shape[0]
    ffs = ff // 4
    h1 = _mm(y_mix, w["w_out"], NN, t, d, 1024, name="out_proj", tk=1024, extras=(x,),
             epilogue=lambda acc, res: (acc + res,))[0]
    hn2 = _rms_fwd(h1, small["norm2_g"], name="rms2_fwd")
    mn = _rms_fwd(mem, small["mem_norm_g"], name="rms_mem_fwd")
    qx = _mm(hn2, w["wq"], NN, t, 512, d, name="xa_q", tk=d)[0]
    kvx = _mm(mn, w["wkv"], NN, bsz * mlen, 1024, d, name="xa_kv", tk=d)[0]
    ox = _xa_fwd(qx, kvx, small["xa_q_norm_g"], small["xa_k_norm_g"], bsz, seq, mlen)
    h2 = _mm(ox, w["wo"], NN, t, d, 512, name="xa_o", tn=ds, tk=512, extras=(h1,),
             b_spec=pl.BlockSpec((None, 512, ds), lambda i, j, kk: (j, 0, 0)),
             epilogue=lambda acc, res: (acc + res,))[0]
    hn3 = _rms_fwd(h2, small["norm3_g"], name="rms3_fwd")

    def relu_sq(acc):
        a = jnp.maximum(acc, 0.0)
        return a, a * a

    act, act2 = _mm(hn3, w["up"], NN, t, ff, d, name="mlp_up", tn=ffs, tk=d,
                    b_spec=pl.BlockSpec((None, d, ffs), lambda i, j, kk: (j, 0, 0)),
                    epilogue=relu_sq, out_dtypes=(_MXU_DTYPE, _MXU_DTYPE))
    inv_d = 1.0 / d
    dy = _mm(act2, w["down"], NN, t, d, ff, name="mlp_down", extras=(h2, target),
             epilogue=lambda acc, res, tgt: ((acc + res - tgt) * inv_d,))[0]
    loss_row = _loss_sum(dy, d)

    dz = _mm(dy, w["down"], NT, t, ff, d, name="d_act", tk=d, extras=(act,),
             epilogue=lambda acc, a: (acc * (2.0 * a.astype(F32)),), out_dtypes=(_MXU_DTYPE,))[0]
    g_down = _mm(act2, dy, TN, ff, d, t, name="g_down")[0]
    g_up = _mm(hn3, dz, TN, d, ff, t, name="g_up", tn=ffs,
               out_shape=(jax.ShapeDtypeStruct((4, d, ffs), F32),),
               out_spec=(pl.BlockSpec((None, min(1024, d), ffs), lambda i, j, kk: (j, i, 0)),))[0]
    tok = comm.grads("mlp", dict(up=g_up, down=g_down))
    dhn3 = _mm(dz, w["up"], NT, t, d, ff, name="d_hn3", tk=ffs, after=(tok,),
               b_spec=pl.BlockSpec((None, min(1024, d), ffs), lambda i, j, kk: (kk, j, 0)))[0]
    dh2, g_norm3 = _rms_bwd(h2, small["norm3_g"], dhn3, dy, name="rms3_bwd")
    d_ox = _mm(dh2, w["wo"], NT, t, 512, d, name="d_ox", tk=ds,
               b_spec=pl.BlockSpec((None, 512, ds), lambda i, j, kk: (kk, 0, 0)))[0]
    g_wo = _mm(ox, dh2, TN, 512, d, t, name="g_wo", tn=ds,
               out_shape=(jax.ShapeDtypeStruct((4, 512, ds), F32),),
               out_spec=(pl.BlockSpec((None, 512, ds), lambda i, j, kk: (j, 0, 0)),))[0]
    d_qx, d_kvx, g_xq, g_xk = _xa_bwd(qx, kvx, small["xa_q_norm_g"], small["xa_k_norm_g"], d_ox, bsz, seq, mlen)
    g_wq = _mm(hn2, d_qx, TN, d, 512, t, name="g_wq")[0]
    g_wkv = _mm(mn, d_kvx, TN, d, 1024, bsz * mlen, name="g_wkv")[0]
    dhn2 = _mm(d_qx, w["wq"], NT, t, d, 512, name="d_hn2", tk=512)[0]
    dmn = _mm(d_kvx, w["wkv"], NT, bsz * mlen, d, 1024, name="d_mn", tk=1024)[0]
    dh1, g_norm2 = _rms_bwd(h1, small["norm2_g"], dhn2, dh2, name="rms2_bwd")
    _, g_memn = _rms_bwd(mem, small["mem_norm_g"], dmn, None, name="rms_mem_bwd")
    g_wout = _mm(y_mix, dh1, TN, 1024, d, t, name="g_wout")[0]
    tok = comm.grads("mid", dict(w_out=g_wout, wq=g_wq, wkv=g_wkv, wo=g_wo))
    d_mix = _mm(dh1, w["w_out"], NT, t, 1024, d, name="d_mix", tk=d, after=(tok,))[0]
    dproj_sw, g_swq, g_swk, g_sinks = _sw_bwd(proj_sw, pos, qg_t, kg_t, small["sw_sinks"], y_mix, d_mix, bsz, seq)
    dproj_hg, g_lb, g_hgn = _hg_bwd(proj_hg, small["hg_lower_bounds"], small["hg_norm_g"], o_hg, states, d_mix, bsz, seq)
    g_in_hg = _mm(hn1, dproj_hg, TN, d, HG_COLS, t, name="g_in_hg")[0]
    g_in_sw = _mm(hn1, dproj_sw, TN, d, SW_COLS, t, name="g_in_sw")[0]
    comm.grads("in", dict(w_in_hg=g_in_hg, w_in_sw=g_in_sw))
    dhn1_a = _mm(dproj_hg, w_in_hg, NT, t, d, HG_COLS, name="d_hn1_hg", tk=1024)[0]
    dhn1 = _mm(dproj_sw, w_in_sw, NT, t, d, SW_COLS, name="d_hn1_sw", tk=SW_COLS, extras=(dhn1_a,),
               epilogue=lambda acc, prev: (acc + prev,))[0]
    grad_x, g_norm1 = _rms_bwd(x, small["norm1_g"], dhn1, dh1, name="rms1_bwd")

    g_small = dict(norm1_g=g_norm1, hg_lower_bounds=g_lb, hg_norm_g=g_hgn, sw_q_norm_g=g_swq, sw_k_norm_g=g_swk,
                   sw_sinks=g_sinks[:, 0:SW_HEADS], norm2_g=g_norm2, mem_norm_g=g_memn, xa_q_norm_g=g_xq,
                   xa_k_norm_g=g_xk, norm3_g=g_norm3)
    return loss_row, grad_x.reshape(bsz, seq, d), g_small


SMALL_NAMES = ("norm1_g", "hg_lower_bounds", "hg_norm_g", "sw_q_norm_g", "sw_k_norm_g", "sw_sinks", "norm2_g",
               "mem_norm_g", "xa_q_norm_g", "xa_k_norm_g", "norm3_g")
BIG_NAMES = ("w_in", "w_out", "xa_wq", "xa_wkv", "xa_wo", "mlp_up", "mlp_down")
WEIGHT_ORDER = ("norm1_g", "w_in", "hg_lower_bounds", "hg_norm_g", "sw_q_norm_g", "sw_k_norm_g", "sw_sinks", "w_out",
                "norm2_g", "mem_norm_g", "xa_wq", "xa_wkv", "xa_q_norm_g", "xa_k_norm_g", "xa_wo", "norm3_g",
                "mlp_up", "mlp_down")


def _head_major_blocks():
    return [kind * HG_HEADS + h for h in range(HG_HEADS) for kind in range(4)]


def _permute_col_blocks(a, blocks):
    return jnp.concatenate([a[:, 128 * b:128 * (b + 1)] for b in blocks], axis=1)


def _pack_rows(vals, width):
    starts, at = [], 0
    for v in vals:
        starts.append(at)
        at += v.shape[0]
    total = at + (-at) % 8
    out = None
    for v, s in zip(vals, starts):
        placed = jnp.pad(v, ((s, total - s - v.shape[0]), (0, width - v.shape[1])))
        out = placed if out is None else out + placed
    return out, starts


class _MeshWeights:
    LATE = ("w_out", "xa_wq", "xa_wkv", "xa_wo", "mlp_up", "mlp_down")

    def __init__(self, shards, d, ff):
        self.shards, self.d, self.ff = shards, d, ff
        self.c_idx = lax.axis_index("c").astype(jnp.int32).reshape(1)
        chip = (2 * lax.axis_index("x") + lax.axis_index("y")).astype(jnp.int32)
        self.place_idx = jnp.stack([chip, lax.axis_index("c").astype(jnp.int32)])
        self.pending = []
        self.halves = {}

    def first(self):
        placed = {n: _place_shard(s, self.place_idx, name="place_" + n) for n, s in self.shards.items()}
        (g_in,) = _all_gather_weights([self.shards["w_in"]], [placed["w_in"]])
        start, self.late_wait = _gather_chips_split("gather_late", [self.shards[n] for n in self.LATE],
                                                    [placed[n] for n in self.LATE])
        self.late_state = start()
        full = jnp.concatenate([g_in[k] for k in range(4)], axis=1)
        return dict(w_in_hg=_permute_col_blocks(full, _head_major_blocks()), w_in_sw=full[:, HG_COLS:],
                    token=self.late_state["token"])

    def rest(self, after):
        _, lands = self.late_wait(self.late_state, after)
        g_out, g_q, g_kv, g_o, g_up, g_dn = _gather_finish(lands)
        d = self.d
        return dict(w_out=g_out.reshape(-1, d), wq=g_q.reshape(d, -1), wkv=g_kv.reshape(d, -1), wo=g_o, up=g_up,
                    down=g_dn.reshape(self.ff, d))

    def _chip_partials(self, tag, names, arrays):
        recv = _exchange_halves(arrays, "rs_exchange_" + tag)
        return [_add_halves(g, r, self.c_idx, name="rs_add_halves_" + n) for n, g, r in zip(names, arrays, recv)]

    def grads(self, tag, g):
        d, ff = self.d, self.ff
        if tag == "mlp":
            names, arrays = ("mlp_up", "mlp_down"), [g["up"], g["down"].reshape(4, ff // 4, d)]
        elif tag == "mid":
            names = ("w_out", "xa_wq", "xa_wkv", "xa_wo")
            arrays = [g["w_out"].reshape(4, -1, d), g["wq"].reshape(4, d // 4, -1), g["wkv"].reshape(4, d // 4, -1), g["wo"]]
        else:
            self.g_in = g
            return None
        parts = self._chip_partials(tag, names, arrays)
        start, wait = _scatter_chips_split("rs_scatter_" + tag, parts)
        state = start()
        self.pending.append((names, wait, state))
        return state["token"]

    def finish(self):
        inv = [int(b) for b in np.argsort(_head_major_blocks())]
        full = jnp.concatenate([_permute_col_blocks(self.g_in["w_in_hg"], inv), self.g_in["w_in_sw"]], axis=1)
        ws = full.shape[1] // 4
        parts = self._chip_partials("in", ("w_in",), [jnp.stack([full[:, ws * k:ws * (k + 1)] for k in range(4)])])
        (recv,) = _scatter_chips(parts, "rs_scatter_in")
        self.halves["w_in"] = _add_chips(parts[0], recv, self.place_idx, name="rs_add_chips_w_in")
        for names, wait, state in self.pending:
            srcs, lands = wait(state, self.halves["w_in"])
            for n, p, r in zip(names, srcs, lands):
                self.halves[n] = _add_chips(p, r, self.place_idx, name="rs_add_chips_" + n)
        return dict(zip(BIG_NAMES, _join_halves([self.halves[n] for n in BIG_NAMES])))


def kernel(x, mem, positions, norm1_g, w_in, hg_lower_bounds, hg_norm_g, sw_q_norm_g, sw_k_norm_g, sw_sinks, w_out, norm2_g, mem_norm_g, xa_wq, xa_wkv, xa_q_norm_g, xa_k_norm_g, xa_wo, norm3_g, mlp_up, mlp_down, loss_target, m_norm1_g, m_w_in, m_hg_lower_bounds, m_hg_norm_g, m_sw_q_norm_g, m_sw_k_norm_g, m_sw_sinks, m_w_out, m_norm2_g, m_mem_norm_g, m_xa_wq, m_xa_wkv, m_xa_q_norm_g, m_xa_k_norm_g, m_xa_wo, m_norm3_g, m_mlp_up, m_mlp_down, v_norm1_g, v_w_in, v_hg_lower_bounds, v_hg_norm_g, v_sw_q_norm_g, v_sw_k_norm_g, v_sw_sinks, v_w_out, v_norm2_g, v_mem_norm_g, v_xa_wq, v_xa_wkv, v_xa_q_norm_g, v_xa_k_norm_g, v_xa_wo, v_norm3_g, v_mlp_up, v_mlp_down):
    given = dict(locals())
    weights = {n: given[n] for n in WEIGHT_ORDER}
    moms = {n: given["m_" + n] for n in WEIGHT_ORDER}
    vars_ = {n: given["v_" + n] for n in WEIGHT_ORDER}
    d = x.shape[-1]
    ff = mlp_down.shape[1] * 4
    small = {n: weights[n] for n in SMALL_NAMES}

    comm = _MeshWeights({n: weights[n][0].astype(_MXU_DTYPE) for n in BIG_NAMES}, d, ff)
    loss_row, grad_x, g_small = _local_step(x, mem, positions, loss_target, small, comm)
    big_grads = comm.finish()

    packed, starts = _pack_rows([g_small[n] for n in SMALL_NAMES] + [loss_row], 1024)
    summed = _all_reduce_small(packed)
    small_grads = {}
    for n, s in zip(SMALL_NAMES, starts):
        r, c = weights[n].shape
        small_grads[n] = summed[s:s + r, 0:c]
    loss = summed[starts[-1], 0]

    grads, deltas, new_m, new_v = {}, {}, {}, {}
    for n in BIG_NAMES:
        shp = weights[n].shape
        g2 = big_grads[n]
        dl, mo, vo = _adamw_big(weights[n][0], g2, moms[n][0], vars_[n][0], name="adamw_" + n)
        grads[n], deltas[n], new_m[n], new_v[n] = (a.reshape(shp) for a in (g2, dl, mo, vo))
    sm_out = _adamw_small([weights[n] for n in SMALL_NAMES], [small_grads[n] for n in SMALL_NAMES],
                          [moms[n] for n in SMALL_NAMES], [vars_[n] for n in SMALL_NAMES])
    ns = len(SMALL_NAMES)
    for i, n in enumerate(SMALL_NAMES):
        grads[n], deltas[n], new_m[n], new_v[n] = small_grads[n], sm_out[i], sm_out[ns + i], sm_out[2 * ns + i]

    return (loss, grad_x, *[grads[n] for n in WEIGHT_ORDER], *[deltas[n] for n in WEIGHT_ORDER],
            *[new_m[n] for n in WEIGHT_ORDER], *[new_v[n] for n in WEIGHT_ORDER])
```

```python
import functools

import numpy as np
import jax
import jax.numpy as jnp
from jax import lax
from jax.experimental import pallas as pl
from jax.experimental.pallas import tpu as pltpu

F32 = jnp.float32
_MXU_DTYPE = jnp.bfloat16

EPS = 1e-6
HG_HEADS = 4
HG_D = 128
HG_CHUNK = 64
HG_TILE = 512
HG_LEVELS = (32, 16, 8, 4, 2, 1)
SW_HEADS = 8
SW_KV_HEADS = 2
SW_GROUP = SW_HEADS // SW_KV_HEADS
SW_HD = 64
SW_BLOCK = 128
ROPE_THETA = 500000.0
ROT_DIM = SW_HD // 4
XA_HEADS = 4
XA_HD = 128
HG_COLS = 4 * HG_HEADS * HG_D
SW_COLS = (SW_HEADS + 2 * SW_KV_HEADS) * SW_HD

ADAM_LR = 0.001
ADAM_B1 = 0.9
ADAM_B2 = 0.999
ADAM_EPS = 1e-08
ADAM_WD = 0.01
ADAM_STEP = 10

VMEM_LIMIT = 56 * 1024 * 1024
MESH = pl.DeviceIdType.MESH

NN = ((1,), (0,))
NT = ((1,), (1,))
TN = ((0,), (0,))


def _mx(v):
    return v.astype(_MXU_DTYPE)


def _dot(a, b, dims=NN):
    return lax.dot_general(_mx(a), _mx(b), (dims, ((), ())), preferred_element_type=F32)


def _split_dot(a, v, dims, parts):
    acc = None
    rest = v
    for p in range(parts):
        piece = _mx(rest)
        term = lax.dot_general(a, piece, (dims, ((), ())), preferred_element_type=F32)
        acc = term if acc is None else acc + term
        if p + 1 < parts:
            rest = rest - piece.astype(F32)
    return acc


def _params(sem):
    return pltpu.CompilerParams(dimension_semantics=sem, vmem_limit_bytes=VMEM_LIMIT)


def _mm(a, b, mode, m, n, k, *, name, tm=1024, tn=1024, tk=512, a_spec=None, b_spec=None, extras=(), epilogue=None,
        out_dtypes=(F32,), out_shape=None, out_spec=None, after=()):
    after = tuple(t for t in after if t is not None)
    tm, tn, tk = min(tm, m), min(tn, n), min(tk, k)
    assert m % tm == 0 and n % tn == 0 and k % tk == 0, (name, m, n, k, tm, tn, tk)
    gi, gj, gk = m // tm, n // tn, k // tk
    if a_spec is None:
        a_spec = (pl.BlockSpec((tk, tm), lambda i, j, kk: (kk, i)) if mode == TN
                  else pl.BlockSpec((tm, tk), lambda i, j, kk: (i, kk)))
    if b_spec is None:
        b_spec = (pl.BlockSpec((tn, tk), lambda i, j, kk: (j, kk)) if mode == NT
                  else pl.BlockSpec((tk, tn), lambda i, j, kk: (kk, j)))
    mn_spec = pl.BlockSpec((tm, tn), lambda i, j, kk: (i, j))
    if epilogue is None:
        epilogue = lambda acc: (acc,)
    n_ex, n_out = len(extras), len(out_dtypes)
    if out_shape is None:
        out_shape = tuple(jax.ShapeDtypeStruct((m, n), d) for d in out_dtypes)
        out_spec = tuple(mn_spec for _ in out_dtypes)

    n_after = len(after)

    def body(*refs):
        a_ref, b_ref = refs[0], refs[1]
        ex = refs[2:2 + n_ex]
        outs = refs[2 + n_ex + n_after:2 + n_ex + n_after + n_out]

        def finish(acc):
            res = epilogue(acc, *[e[...] for e in ex])
            for o, r in zip(outs, res):
                o[...] = r.astype(o.dtype)

        if gk == 1:
            finish(_dot(a_ref[...], b_ref[...], mode))
        else:
            acc_ref = refs[-1]
            kk = pl.program_id(2)

            @pl.when(kk == 0)
            def _():
                acc_ref[...] = jnp.zeros_like(acc_ref)

            acc_ref[...] += _dot(a_ref[...], b_ref[...], mode)

            @pl.when(kk == gk - 1)
            def _():
                finish(acc_ref[...])

    return pl.pallas_call(
        body, name=name, grid=(gi, gj, gk),
        in_specs=[a_spec, b_spec] + [mn_spec] * n_ex + [pl.BlockSpec(memory_space=pl.ANY)] * n_after,
        out_specs=out_spec, out_shape=out_shape,
        scratch_shapes=[pltpu.VMEM((tm, tn), F32)] if gk > 1 else [],
        compiler_params=_params(("parallel", "parallel", "arbitrary")),
    )(a, b, *extras, *after)


def _rms_fwd(x, g, *, name, tm=512):
    t, d = x.shape
    tm = min(tm, t)

    def body(x_ref, g_ref, o_ref):
        xv = x_ref[...]
        r = lax.rsqrt(jnp.mean(xv * xv, axis=1, keepdims=True) + EPS)
        o_ref[...] = (xv * r * g_ref[...]).astype(o_ref.dtype)

    return pl.pallas_call(
        body, name=name, grid=(t // tm,),
        in_specs=[pl.BlockSpec((tm, d), lambda i: (i, 0)), pl.BlockSpec((1, d), lambda i: (0, 0))],
        out_specs=pl.BlockSpec((tm, d), lambda i: (i, 0)),
        out_shape=jax.ShapeDtypeStruct((t, d), _MXU_DTYPE),
        compiler_params=_params(("parallel",)),
    )(x, g)


def _rms_bwd(x, g, dy, dres, *, name, tm=512):
    t, d = x.shape
    tm = min(tm, t)
    has_res = dres is not None

    def body(*refs):
        x_ref, g_ref, dy_ref = refs[:3]
        dx_ref, dg_ref = refs[-2:]
        xv, dyv = x_ref[...], dy_ref[...]
        r = lax.rsqrt(jnp.mean(xv * xv, axis=1, keepdims=True) + EPS)
        u = dyv * g_ref[...]
        dx = r * u - xv * (r * r * r) * jnp.mean(u * xv, axis=1, keepdims=True)
        if has_res:
            dx = dx + refs[3][...]
        dx_ref[...] = dx

        @pl.when(pl.program_id(0) == 0)
        def _():
            dg_ref[...] = jnp.zeros_like(dg_ref)

        dg_ref[...] += jnp.sum(dyv * xv * r, axis=0, keepdims=True)

    row = pl.BlockSpec((tm, d), lambda i: (i, 0))
    vec = pl.BlockSpec((1, d), lambda i: (0, 0))
    return pl.pallas_call(
        body, name=name, grid=(t // tm,),
        in_specs=[row, vec, row] + ([row] if has_res else []),
        out_specs=(row, vec),
        out_shape=(jax.ShapeDtypeStruct((t, d), F32), jax.ShapeDtypeStruct((1, d), F32)),
        compiler_params=_params(("arbitrary",)),
    )(*([x, g, dy] + ([dres] if has_res else [])))


def _hg_constants():
    c = HG_CHUNK
    t = np.arange(c)
    sums = [t[None, :] <= t[:, None]]
    masks = []
    for m in HG_LEVELS:
        base = (t // (2 * m)) * (2 * m)
        mid = base + m - 1
        second = (t - base) >= m
        upper = (t[None, :] > mid[:, None]) & (t[None, :] <= t[:, None])
        lower = (t[None, :] > t[:, None]) & (t[None, :] <= mid[:, None])
        sums.append(np.where(second[:, None], upper, lower))
        masks.append(second[:, None] & (~second)[None, :] & (base[:, None] == base[None, :]))
    return (np.concatenate(sums, axis=0).astype(np.float32), np.stack(masks).astype(np.float32))


HG_HEAD_LANES = tuple(slice(HG_D * h, HG_D * (h + 1)) for h in range(HG_HEADS))


def _per_head(fn, slab):
    return jnp.concatenate([jnp.broadcast_to(fn(slab[:, hs]), (slab.shape[0], HG_D)) for hs in HG_HEAD_LANES], axis=1)


def _lane_sum(v):
    return jnp.sum(v, axis=1, keepdims=True)


def _lane_mean(v):
    return jnp.mean(v, axis=1, keepdims=True)


def _hg_gates(blk, lbp):
    w = HG_HEADS * HG_D
    q, x, v, gl = blk[:, 0:w], blk[:, w:2 * w], blk[:, 2 * w:3 * w], blk[:, 3 * w:4 * w]
    mx = jnp.max(lbp, axis=0, keepdims=True)
    e = jnp.exp(lbp - mx)
    lb = e[0:1, :] / jnp.sum(e, axis=0, keepdims=True)
    sig = jax.nn.sigmoid(x)
    f = lb + (1.0 - lb) * sig
    return q, v, gl, lb, sig, f, 1.0 - f, jnp.log(f)


def _hg_fwd(proj, lbp, ng, bsz, seq, *, y_width):
    t = proj.shape[0]
    nc = seq // HG_CHUNK
    a_np, m_np = _hg_constants()
    a_all = jnp.asarray(a_np, _MXU_DTYPE)
    masks = jnp.asarray(m_np, F32)
    nl = len(HG_LEVELS)

    ts = min(HG_TILE, seq)
    ns, nct = seq // ts, ts // HG_CHUNK
    hw = HG_HEADS * HG_D

    def body(p_ref, lb_ref, ng_ref, a_ref, m_ref, y_ref, o_ref, st_ref, carry):
        a_mat = a_ref[...]
        ngv = ng_ref[...]

        @pl.when(pl.program_id(1) == 0)
        def _():
            carry[...] = jnp.zeros_like(carry)

        ng4 = _tile_lanes(ngv, HG_HEADS)
        heads = range(HG_HEADS)
        hl = HG_HEAD_LANES

        def chunk(c, _):
            rows = pl.ds(pl.multiple_of(c * HG_CHUNK, HG_CHUNK), HG_CHUNK)
            q, v, gl, lb, sig, f, k, g = _hg_gates(p_ref[rows, :], lb_ref[...])
            sts = [carry[h] for h in heads]
            e_all = _split_dot(a_mat, g, NN, 3)
            b = e_all[0:HG_CHUNK]
            qb = q * jnp.exp(b)
            o = [_dot(qb[:, hl[h]], sts[h], NT) for h in heads]
            p = [jnp.zeros((HG_CHUNK, HG_CHUNK), F32) for _ in heads]
            for li in range(nl):
                e = jnp.exp(e_all[HG_CHUNK * (li + 1):HG_CHUNK * (li + 2)])
                qm, km, mk = q * e, k * e, m_ref[li]
                p = [p[h] + mk * _dot(qm[:, hl[h]], km[:, hl[h]], NT) for h in heads]
            bl = b[HG_CHUNK - 1:HG_CHUNK, :]
            kd = k * jnp.exp(bl - b)
            ebl = jnp.exp(bl)
            pv = [_dot(p[h], v[:, hl[h]]) for h in heads]
            upd = [_dot(v[:, hl[h]], kd[:, hl[h]], TN) for h in heads]
            o_all = jnp.concatenate([o[h] + pv[h] for h in heads], axis=1) + _per_head(_lane_sum, q * k) * v
            r = lax.rsqrt(_per_head(_lane_mean, o_all * o_all) + EPS)
            for h in heads:
                st_ref[h, c] = sts[h]
                carry[h] = sts[h] * ebl[:, hl[h]] + upd[h]
            o_ref[rows, :] = o_all
            y_ref[rows, :] = (o_all * r * ng4) * (gl * jax.nn.sigmoid(gl))
            return 0

        lax.fori_loop(0, nct, chunk, 0)

    return pl.pallas_call(
        body, name="hgrn2_fwd", grid=(bsz, ns),
        in_specs=[pl.BlockSpec((ts, HG_COLS), lambda b, s: (b * ns + s, 0)),
                  pl.BlockSpec((2, hw), lambda b, s: (0, 0)),
                  pl.BlockSpec((1, HG_D), lambda b, s: (0, 0)),
                  pl.BlockSpec(a_all.shape, lambda b, s: (0, 0)),
                  pl.BlockSpec(masks.shape, lambda b, s: (0, 0, 0))],
        out_specs=(pl.BlockSpec((ts, hw), lambda b, s: (b * ns + s, 0)),
                   pl.BlockSpec((ts, hw), lambda b, s: (b * ns + s, 0)),
                   pl.BlockSpec((None, HG_HEADS, nct, HG_D, HG_D), lambda b, s: (b, 0, s, 0, 0))),
        out_shape=(jax.ShapeDtypeStruct((t, y_width), F32),
                   jax.ShapeDtypeStruct((t, hw), F32),
                   jax.ShapeDtypeStruct((bsz, HG_HEADS, nc, HG_D, HG_D), F32)),
        scratch_shapes=[pltpu.VMEM((HG_HEADS, HG_D, HG_D), F32)],
        compiler_params=_params(("parallel", "arbitrary")),
    )(proj, lbp, ng, a_all, masks)


def _hg_bwd(proj, lbp, ng, o_all, states, dy, bsz, seq):
    t = proj.shape[0]
    nc = seq // HG_CHUNK
    a_np, m_np = _hg_constants()
    a_all = jnp.asarray(a_np, _MXU_DTYPE)
    masks = jnp.asarray(m_np, F32)
    nl = len(HG_LEVELS)
    cs = HG_CHUNK

    ts = min(HG_TILE, seq)
    ns, nct = seq // ts, ts // cs
    hw = HG_HEADS * HG_D

    def body(p_ref, lb_ref, ng_ref, a_ref, m_ref, o_ref, st_ref, dy_ref, dp_ref, dlb_ref, dng_ref, dst_ref):
        a_mat = a_ref[...]
        ngv = ng_ref[...]
        ng4 = _tile_lanes(ngv, HG_HEADS)
        last_row = lax.broadcasted_iota(jnp.int32, (cs, hw), 0) == cs - 1
        si = pl.program_id(1)
        first = jnp.logical_and(pl.program_id(0) == 0, si == 0)
        heads = range(HG_HEADS)
        hl = HG_HEAD_LANES

        @pl.when(si == 0)
        def _():
            dst_ref[...] = jnp.zeros_like(dst_ref)

        def side_by_side(parts):
            return jnp.concatenate(parts, axis=1)

        def chunk(i, carry):
            dlb_acc, dng_acc = carry
            c = nct - 1 - i
            rows = pl.ds(pl.multiple_of(c * cs, cs), cs)
            q, v, gl, lb, sig, f, k, g = _hg_gates(p_ref[rows, :], lb_ref[...])
            o = o_ref[rows, :]
            dyv = dy_ref[rows, :]
            sts = [st_ref[h, c] for h in heads]
            dsts = [dst_ref[h] for h in heads]
            e_all = _split_dot(a_mat, g, NN, 3)
            b = e_all[0:cs]
            eb = jnp.exp(b)
            bl = b[cs - 1:cs, :]
            ebl = jnp.exp(bl)
            ekd = jnp.exp(bl - b)
            qb, kd = q * eb, k * ekd
            sg = jax.nn.sigmoid(gl)
            silu = gl * sg
            r = lax.rsqrt(_per_head(_lane_mean, o * o) + EPS)
            dgl = dyv * (o * r * ng4) * (sg * (1.0 + gl * (1.0 - sg)))
            u = dyv * silu * ng4
            do = r * u - o * (r * r * r) * _per_head(_lane_mean, u * o)
            dng4 = jnp.sum(dyv * silu * o * r, axis=0, keepdims=True)
            dng_acc = dng_acc + ((dng4[:, hl[0]] + dng4[:, hl[1]]) + (dng4[:, hl[2]] + dng4[:, hl[3]]))
            es, qm, km = [], [], []
            p = [jnp.zeros((cs, cs), F32) for _ in heads]
            for li in range(nl):
                e = jnp.exp(e_all[cs * (li + 1):cs * (li + 2)])
                es.append(e)
                qm.append(q * e)
                km.append(k * e)
                mk = m_ref[li]
                p = [p[h] + mk * _dot(qm[li][:, hl[h]], km[li][:, hl[h]], NT) for h in heads]
            dp = [_dot(do[:, hl[h]], v[:, hl[h]], NT) for h in heads]
            dv_p = [_dot(p[h], do[:, hl[h]], TN) for h in heads]
            dv_s = [_dot(kd[:, hl[h]], dsts[h], NT) for h in heads]
            dqb = side_by_side([_dot(do[:, hl[h]], sts[h]) for h in heads])
            dkd = side_by_side([_dot(v[:, hl[h]], dsts[h]) for h in heads])
            new_dst = [_dot(do[:, hl[h]], qb[:, hl[h]], TN) for h in heads]
            dv = side_by_side([dv_p[h] + dv_s[h] for h in heads]) + _per_head(_lane_sum, q * k) * do
            dq = dqb * eb
            dk = dkd * ekd
            db = dqb * qb - dkd * kd
            dbl = (jnp.sum(dkd * kd, axis=0, keepdims=True)
                   + side_by_side([jnp.sum(dsts[h] * sts[h], axis=0, keepdims=True) for h in heads]) * ebl)
            de = [db + jnp.where(last_row, dbl, 0.0)]
            for li in range(nl):
                mk = m_ref[li]
                dpm = [mk * dp[h] for h in heads]
                dqm = side_by_side([_dot(dpm[h], km[li][:, hl[h]]) for h in heads])
                dkm = side_by_side([_dot(dpm[h], qm[li][:, hl[h]], TN) for h in heads])
                dq = dq + dqm * es[li]
                dk = dk + dkm * es[li]
                de.append(dqm * qm[li] + dkm * km[li])
            dpd = _per_head(_lane_sum, do * v)
            dq = dq + dpd * k
            dk = dk + dpd * q
            dg = _split_dot(a_mat, jnp.concatenate(de, axis=0), TN, 2)
            df = dg / f - dk
            dp_ref[rows, 0:hw] = dq
            dp_ref[rows, hw:2 * hw] = df * (1.0 - lb) * sig * (1.0 - sig)
            dp_ref[rows, 2 * hw:3 * hw] = dv
            dp_ref[rows, 3 * hw:4 * hw] = dgl
            for h in heads:
                dst_ref[h] = dsts[h] * ebl[:, hl[h]] + new_dst[h]
            return dlb_acc + jnp.sum(df * (1.0 - sig), axis=0, keepdims=True), dng_acc

        dlb, dng = lax.fori_loop(0, nct, chunk, (jnp.zeros((1, hw), F32), jnp.zeros((1, HG_D), F32)))

        @pl.when(first)
        def _():
            dlb_ref[...] = jnp.zeros_like(dlb_ref)
            dng_ref[...] = jnp.zeros_like(dng_ref)

        lbp_v = lb_ref[...]
        mx = jnp.max(lbp_v, axis=0, keepdims=True)
        e = jnp.exp(lbp_v - mx)
        s0 = e[0:1, :] / jnp.sum(e, axis=0, keepdims=True)
        da0 = dlb * s0 * (1.0 - s0)
        dlb_ref[...] += jnp.concatenate([da0, -da0], axis=0)
        dng_ref[...] += dng

    def tile(b, s):
        return b * ns + (ns - 1 - s)

    return pl.pallas_call(
        body, name="hgrn2_bwd", grid=(bsz, ns),
        in_specs=[pl.BlockSpec((ts, HG_COLS), lambda b, s: (tile(b, s), 0)),
                  pl.BlockSpec((2, hw), lambda b, s: (0, 0)),
                  pl.BlockSpec((1, HG_D), lambda b, s: (0, 0)),
                  pl.BlockSpec(a_all.shape, lambda b, s: (0, 0)),
                  pl.BlockSpec(masks.shape, lambda b, s: (0, 0, 0)),
                  pl.BlockSpec((ts, hw), lambda b, s: (tile(b, s), 0)),
                  pl.BlockSpec((None, HG_HEADS, nct, HG_D, HG_D), lambda b, s: (b, 0, ns - 1 - s, 0, 0)),
                  pl.BlockSpec((ts, hw), lambda b, s: (tile(b, s), 0))],
        out_specs=(pl.BlockSpec((ts, HG_COLS), lambda b, s: (tile(b, s), 0)),
                   pl.BlockSpec((2, hw), lambda b, s: (0, 0)),
                   pl.BlockSpec((1, HG_D), lambda b, s: (0, 0))),
        out_shape=(jax.ShapeDtypeStruct((t, HG_COLS), F32),
                   jax.ShapeDtypeStruct((2, hw), F32),
                   jax.ShapeDtypeStruct((1, HG_D), F32)),
        scratch_shapes=[pltpu.VMEM((HG_HEADS, HG_D, HG_D), F32)],
        compiler_params=_params(("arbitrary", "arbitrary")),
    )(proj, lbp, ng, a_all, masks, o_all, states, dy)


def _sw_constants():
    half = ROT_DIM // 2
    inv = (np.float32(ROPE_THETA) ** (-(np.arange(half, dtype=np.float32) * np.float32(2.0) / np.float32(ROT_DIM)))
           ).astype(np.float32)
    freq = np.zeros((1, 128), np.float32)
    sign = np.zeros((1, 128), np.float32)
    for h in range(2):
        freq[0, 64 * h:64 * h + half] = inv
        freq[0, 64 * h + half:64 * h + 2 * half] = inv
        sign[0, 64 * h:64 * h + half] = -1.0
        sign[0, 64 * h + half:64 * h + 2 * half] = 1.0
    seg = np.kron(np.eye(8, dtype=np.float32), np.full((64, 64), 1.0 / 64.0, np.float32))
    return freq, sign, seg


def _rope_tables(pos, freq, sign):
    ang = pos.astype(F32) * freq
    return jnp.cos(ang), jnp.sin(ang) * sign


def _tile_lanes(v, times):
    return v if times == 1 else jnp.concatenate([v] * times, axis=1)


def _swap_halves(v):
    w = v.shape[1]
    half = ROT_DIM // 2
    lane = lax.broadcasted_iota(jnp.int32, v.shape, 1) % SW_HD
    return jnp.where(lane < half, pltpu.roll(v, w - half, 1), jnp.where(lane < 2 * half, pltpu.roll(v, half, 1), 0.0))


def _sw_norm_rope(tv, gain, seg, cosv, sinv):
    w = tv.shape[1]
    ms = _split_dot_rhs(tv * tv, seg[0:w, 0:w])
    r = lax.rsqrt(ms + EPS)
    tn = tv * r * gain
    reps = w // 128
    return tn * _tile_lanes(cosv, reps) + _swap_halves(tn) * _tile_lanes(sinv, reps), r


def _split_dot_rhs(v, a):
    hi = _mx(v)
    lo = _mx(v - hi.astype(F32))
    return (lax.dot_general(hi, a, (NN, ((), ())), preferred_element_type=F32)
            + lax.dot_general(lo, a, (NN, ((), ())), preferred_element_type=F32))


def _sw_norm_rope_bwd(dt, tv, r, gain, seg, cosv, sinv):
    w = tv.shape[1]
    reps = w // 128
    dtn = dt * _tile_lanes(cosv, reps) + _swap_halves(dt * _tile_lanes(sinv, reps))
    u = dtn * gain
    dtv = r * u - tv * (r * r * r) * _split_dot_rhs(u * tv, seg[0:w, 0:w])
    return dtv, jnp.sum(dtn * tv * r, axis=0, keepdims=True)


def _sw_scores(qh, kp, kc):
    return _dot(qh, kp, NT), _dot(qh, kc, NT)


def _sw_probs(raw, sink, first_block):
    scale = SW_HD ** -0.5
    qi = lax.broadcasted_iota(jnp.int32, (SW_BLOCK, SW_BLOCK), 0)
    kj = lax.broadcasted_iota(jnp.int32, (SW_BLOCK, SW_BLOCK), 1)
    ok_prev = jnp.logical_and(kj > qi, jnp.logical_not(first_block))
    ok_cur = kj <= qi
    sp = jnp.where(ok_prev, raw[0] * scale, -jnp.inf)
    sc = jnp.where(ok_cur, raw[1] * scale, -jnp.inf)
    m = jnp.maximum(jnp.maximum(jnp.max(sp, axis=1, keepdims=True), jnp.max(sc, axis=1, keepdims=True)), sink)
    pp, pc = jnp.exp(sp - m), jnp.exp(sc - m)
    es = jnp.exp(sink - m)
    den = jnp.sum(pp, axis=1, keepdims=True) + jnp.sum(pc, axis=1, keepdims=True) + es
    return pp / den, pc / den, es / den


def _sw_specs(nb):
    def cur(b, n):
        return b * nb + jnp.minimum(n, nb - 1)

    def prev(b, n):
        return b * nb + jnp.maximum(jnp.minimum(n, nb - 1) - 1, 0)

    return cur, prev


def _sw_fwd(proj, pos, qg, kg, sinks, y_in, bsz, seq):
    t = proj.shape[0]
    nb = seq // SW_BLOCK
    freq_np, sign_np, seg_np = _sw_constants()
    freq, sign = jnp.asarray(freq_np), jnp.asarray(sign_np)
    seg = jnp.asarray(seg_np, _MXU_DTYPE)
    cur, prev = _sw_specs(nb)

    def body(q_ref, kc_ref, kp_ref, vc_ref, vp_ref, pc_ref, pp_ref, qg_ref, kg_ref, sk_ref, fr_ref, sn_ref, seg_ref,
             yin_ref, y_ref):
        del yin_ref
        n = pl.program_id(1)
        segv = seg_ref[...]
        cos_c, sin_c = _rope_tables(pc_ref[...], fr_ref[...], sn_ref[...])
        cos_p, sin_p = _rope_tables(pp_ref[...], fr_ref[...], sn_ref[...])
        qr, _ = _sw_norm_rope(q_ref[...], qg_ref[...], segv, cos_c, sin_c)
        kcr, _ = _sw_norm_rope(kc_ref[...], kg_ref[...], segv, cos_c, sin_c)
        kpr, _ = _sw_norm_rope(kp_ref[...], kg_ref[...], segv, cos_p, sin_p)
        vc, vp = vc_ref[...], vp_ref[...]
        ks = [slice(SW_HD * (h // SW_GROUP), SW_HD * (h // SW_GROUP + 1)) for h in range(SW_HEADS)]
        raw = [_sw_scores(qr[:, SW_HD * h:SW_HD * (h + 1)], kpr[:, ks[h]], kcr[:, ks[h]]) for h in range(SW_HEADS)]
        probs = [_sw_probs(raw[h], sk_ref[0, h], n == 0) for h in range(SW_HEADS)]
        for h in range(SW_HEADS):
            y_ref[:, SW_HD * h:SW_HD * (h + 1)] = _dot(probs[h][0], vp[:, ks[h]]) + _dot(probs[h][1], vc[:, ks[h]])

    rowq = pl.BlockSpec((SW_BLOCK, 512), lambda b, n: (cur(b, n), 0))
    full = lambda a: pl.BlockSpec(a.shape, lambda b, n: (0,) * a.ndim)
    yw = y_in.shape[1]
    return pl.pallas_call(
        body, name="swa_fwd", grid=(bsz, nb),
        in_specs=[rowq,
                  pl.BlockSpec((SW_BLOCK, 128), lambda b, n: (cur(b, n), 4)),
                  pl.BlockSpec((SW_BLOCK, 128), lambda b, n: (prev(b, n), 4)),
                  pl.BlockSpec((SW_BLOCK, 128), lambda b, n: (cur(b, n), 5)),
                  pl.BlockSpec((SW_BLOCK, 128), lambda b, n: (prev(b, n), 5)),
                  pl.BlockSpec((SW_BLOCK, 1), lambda b, n: (cur(b, n), 0)),
                  pl.BlockSpec((SW_BLOCK, 1), lambda b, n: (prev(b, n), 0)),
                  full(qg), full(kg),
                  pl.BlockSpec(memory_space=pltpu.SMEM),
                  full(freq), full(sign), full(seg),
                  pl.BlockSpec(memory_space=pl.ANY)],
        out_specs=pl.BlockSpec((SW_BLOCK, 512), lambda b, n: (cur(b, n), 1)),
        out_shape=jax.ShapeDtypeStruct((t, yw), F32),
        input_output_aliases={13: 0},
        compiler_params=_params(("parallel", "parallel")),
    )(proj, proj, proj, proj, proj, pos, pos, qg, kg, sinks, freq, sign, seg, y_in)


def _sw_bwd(proj, pos, qg, kg, sinks, y, dy, bsz, seq):
    t = proj.shape[0]
    nb = seq // SW_BLOCK
    freq_np, sign_np, seg_np = _sw_constants()
    freq, sign = jnp.asarray(freq_np), jnp.asarray(sign_np)
    seg = jnp.asarray(seg_np, _MXU_DTYPE)
    cur, prev = _sw_specs(nb)
    scale = SW_HD ** -0.5

    def body(q_ref, kc_ref, kp_ref, vc_ref, vp_ref, pc_ref, pp_ref, qg_ref, kg_ref, sk_ref, fr_ref, sn_ref, seg_ref,
             y_ref, dy_ref, dp_ref, dqg_ref, dkg_ref, dsk_ref,
             dq_car, dkv_car, dqr_s, dkc_s, dkp_s, dvc_s, dvp_s, gq_acc, gk_acc, sk_acc):
        b, n = pl.program_id(0), pl.program_id(1)
        first = jnp.logical_and(b == 0, n == 0)
        last = jnp.logical_and(b == pl.num_programs(0) - 1, n == nb)

        @pl.when(first)
        def _():
            gq_acc[...] = jnp.zeros_like(gq_acc)
            gk_acc[...] = jnp.zeros_like(gk_acc)
            sk_acc[...] = jnp.zeros_like(sk_acc)

        @pl.when(n < nb)
        def _():
            segv = seg_ref[...]
            cos_c, sin_c = _rope_tables(pc_ref[...], fr_ref[...], sn_ref[...])
            cos_p, sin_p = _rope_tables(pp_ref[...], fr_ref[...], sn_ref[...])
            qv, kcv, kpv = q_ref[...], kc_ref[...], kp_ref[...]
            qr, rq = _sw_norm_rope(qv, qg_ref[...], segv, cos_c, sin_c)
            kcr, rkc = _sw_norm_rope(kcv, kg_ref[...], segv, cos_c, sin_c)
            kpr, rkp = _sw_norm_rope(kpv, kg_ref[...], segv, cos_p, sin_p)
            vc, vp = vc_ref[...], vp_ref[...]
            lane = lax.broadcasted_iota(jnp.int32, (1, 128), 1)
            dsk = jnp.zeros((1, 128), F32)
            heads = range(SW_HEADS)
            ks = [slice(SW_HD * (h // SW_GROUP), SW_HD * (h // SW_GROUP + 1)) for h in heads]
            hs = [slice(SW_HD * h, SW_HD * (h + 1)) for h in heads]
            qh = [qr[:, hs[h]] for h in heads]
            doh = [dy_ref[:, hs[h]] for h in heads]
            raw = [_sw_scores(qh[h], kpr[:, ks[h]], kcr[:, ks[h]]) for h in heads]
            dpp = [_dot(doh[h], vp[:, ks[h]], NT) for h in heads]
            dpc = [_dot(doh[h], vc[:, ks[h]], NT) for h in heads]
            probs = [_sw_probs(raw[h], sk_ref[0, h], n == 0) for h in heads]
            dsp, dsc = [], []
            for h in heads:
                pp, pc, ps = probs[h]
                delta = jnp.sum(doh[h] * y_ref[:, hs[h]], axis=1, keepdims=True)
                dsp.append(pp * (dpp[h] - delta) * scale)
                dsc.append(pc * (dpc[h] - delta) * scale)
                dsk = dsk + jnp.where(lane == h, -jnp.sum(ps * delta), 0.0)
            for h in heads:
                dqr_s[:, hs[h]] = _dot(dsp[h], kpr[:, ks[h]]) + _dot(dsc[h], kcr[:, ks[h]])
            for kv in range(SW_KV_HEADS):
                group = range(SW_GROUP * kv, SW_GROUP * (kv + 1))
                kvs = slice(SW_HD * kv, SW_HD * (kv + 1))
                dvp_s[:, kvs] = sum(_dot(probs[h][0], doh[h], TN) for h in group)
                dvc_s[:, kvs] = sum(_dot(probs[h][1], doh[h], TN) for h in group)
                dkp_s[:, kvs] = sum(_dot(dsp[h], qh[h], TN) for h in group)
                dkc_s[:, kvs] = sum(_dot(dsc[h], qh[h], TN) for h in group)
            dq, gq = _sw_norm_rope_bwd(dqr_s[...], qv, rq, qg_ref[...], segv, cos_c, sin_c)
            dkc, gkc = _sw_norm_rope_bwd(dkc_s[...], kcv, rkc, kg_ref[...], segv, cos_c, sin_c)
            dkp, gkp = _sw_norm_rope_bwd(dkp_s[...], kpv, rkp, kg_ref[...], segv, cos_p, sin_p)
            gq_acc[...] += gq
            gk_acc[...] += gkc + gkp
            sk_acc[...] += dsk

            @pl.when(n > 0)
            def _():
                dp_ref[:, 0:512] = dq_car[...]
                dp_ref[:, 512:640] = dkv_car[:, 0:128] + dkp
                dp_ref[:, 640:768] = dkv_car[:, 128:256] + dvp_s[...]

            dq_car[...] = dq
            dkv_car[:, 0:128] = dkc
            dkv_car[:, 128:256] = dvc_s[...]

        @pl.when(n == nb)
        def _():
            dp_ref[:, 0:512] = dq_car[...]
            dp_ref[:, 512:768] = dkv_car[...]

        @pl.when(last)
        def _():
            gq = gq_acc[...]
            acc = gq[:, 0:SW_HD]
            for h in range(1, SW_HEADS):
                acc = acc + gq[:, SW_HD * h:SW_HD * (h + 1)]
            dqg_ref[...] = acc
            gk = gk_acc[...]
            dkg_ref[...] = gk[:, 0:SW_HD] + gk[:, SW_HD:2 * SW_HD]
            dsk_ref[...] = sk_acc[...]

    rowq = pl.BlockSpec((SW_BLOCK, 512), lambda b, n: (cur(b, n), 0))
    full = lambda a: pl.BlockSpec(a.shape, lambda b, n: (0,) * a.ndim)

    def out_row(b, n):
        return b * nb + jnp.maximum(n - 1, 0)

    return pl.pallas_call(
        body, name="swa_bwd", grid=(bsz, nb + 1),
        in_specs=[rowq,
                  pl.BlockSpec((SW_BLOCK, 128), lambda b, n: (cur(b, n), 4)),
                  pl.BlockSpec((SW_BLOCK, 128), lambda b, n: (prev(b, n), 4)),
                  pl.BlockSpec((SW_BLOCK, 128), lambda b, n: (cur(b, n), 5)),
                  pl.BlockSpec((SW_BLOCK, 128), lambda b, n: (prev(b, n), 5)),
                  pl.BlockSpec((SW_BLOCK, 1), lambda b, n: (cur(b, n), 0)),
                  pl.BlockSpec((SW_BLOCK, 1), lambda b, n: (prev(b, n), 0)),
                  full(qg), full(kg),
                  pl.BlockSpec(memory_space=pltpu.SMEM),
                  full(freq), full(sign), full(seg),
                  pl.BlockSpec((SW_BLOCK, 512), lambda b, n: (cur(b, n), 1)),
                  pl.BlockSpec((SW_BLOCK, 512), lambda b, n: (cur(b, n), 1))],
        out_specs=(pl.BlockSpec((SW_BLOCK, SW_COLS), lambda b, n: (out_row(b, n), 0)),
                   pl.BlockSpec((1, SW_HD), lambda b, n: (0, 0)),
                   pl.BlockSpec((1, SW_HD), lambda b, n: (0, 0)),
                   pl.BlockSpec((1, 128), lambda b, n: (0, 0))),
        out_shape=(jax.ShapeDtypeStruct((t, SW_COLS), F32),
                   jax.ShapeDtypeStruct((1, SW_HD), F32),
                   jax.ShapeDtypeStruct((1, SW_HD), F32),
                   jax.ShapeDtypeStruct((1, 128), F32)),
        scratch_shapes=[pltpu.VMEM((SW_BLOCK, 512), F32), pltpu.VMEM((SW_BLOCK, 256), F32),
                        pltpu.VMEM((SW_BLOCK, 512), F32),
                        pltpu.VMEM((SW_BLOCK, 128), F32), pltpu.VMEM((SW_BLOCK, 128), F32),
                        pltpu.VMEM((SW_BLOCK, 128), F32), pltpu.VMEM((SW_BLOCK, 128), F32),
                        pltpu.VMEM((1, 512), F32), pltpu.VMEM((1, 128), F32), pltpu.VMEM((1, 128), F32)],
        compiler_params=_params(("arbitrary", "arbitrary")),
    )(proj, proj, proj, proj, proj, pos, pos, qg, kg, sinks, freq, sign, seg, y, dy)


def _head_rms(tv, gain):
    r = lax.rsqrt(jnp.mean(tv * tv, axis=1, keepdims=True) + EPS)
    return tv * r * gain, r


def _head_rms_bwd(dtn, tv, r, gain):
    u = dtn * gain
    return r * u - tv * (r * r * r) * jnp.mean(u * tv, axis=1, keepdims=True), jnp.sum(dtn * tv * r, axis=0, keepdims=True)


def _xa_probs(qn, kn):
    s = _dot(qn, kn, NT) * (XA_HD ** -0.5)
    e = jnp.exp(s - jnp.max(s, axis=1, keepdims=True))
    return e / jnp.sum(e, axis=1, keepdims=True)


def _xa_fwd(qx, kvx, qg, kg, bsz, seq, mlen, *, tq=512):
    t = qx.shape[0]
    tq = min(tq, seq)
    nq = seq // tq
    w = XA_HEADS * XA_HD

    def body(q_ref, kv_ref, qg_ref, kg_ref, o_ref):
        for h in range(XA_HEADS):
            hs = slice(XA_HD * h, XA_HD * (h + 1))
            qn, _ = _head_rms(q_ref[:, hs], qg_ref[...])
            kn, _ = _head_rms(kv_ref[:, hs], kg_ref[...])
            o_ref[:, hs] = _dot(_xa_probs(qn, kn), kv_ref[:, w + XA_HD * h:w + XA_HD * (h + 1)])

    vec = pl.BlockSpec((1, XA_HD), lambda b, i: (0, 0))
    return pl.pallas_call(
        body, name="xattn_fwd", grid=(bsz, nq),
        in_specs=[pl.BlockSpec((tq, w), lambda b, i: (b * nq + i, 0)),
                  pl.BlockSpec((mlen, 2 * w), lambda b, i: (b, 0)), vec, vec],
        out_specs=pl.BlockSpec((tq, w), lambda b, i: (b * nq + i, 0)),
        out_shape=jax.ShapeDtypeStruct((t, w), F32),
        compiler_params=_params(("parallel", "parallel")),
    )(qx, kvx, qg, kg)


def _xa_bwd(qx, kvx, qg, kg, do, bsz, seq, mlen, *, tq=512):
    t = qx.shape[0]
    tq = min(tq, seq)
    nq = seq // tq
    w = XA_HEADS * XA_HD
    scale = XA_HD ** -0.5

    def body(q_ref, kv_ref, qg_ref, kg_ref, do_ref, dq_ref, dkv_ref, dqg_ref, dkg_ref):
        b, i = pl.program_id(0), pl.program_id(1)

        @pl.when(jnp.logical_and(b == 0, i == 0))
        def _():
            dqg_ref[...] = jnp.zeros_like(dqg_ref)
            dkg_ref[...] = jnp.zeros_like(dkg_ref)

        @pl.when(i == 0)
        def _():
            dkv_ref[...] = jnp.zeros_like(dkv_ref)

        gq_sum = jnp.zeros((1, XA_HD), F32)
        gk_sum = jnp.zeros((1, XA_HD), F32)
        for h in range(XA_HEADS):
            hs = slice(XA_HD * h, XA_HD * (h + 1))
            vs = slice(w + XA_HD * h, w + XA_HD * (h + 1))
            qv, kv, vv = q_ref[:, hs], kv_ref[:, hs], kv_ref[:, vs]
            qn, rq = _head_rms(qv, qg_ref[...])
            kn, rk = _head_rms(kv, kg_ref[...])
            p = _xa_probs(qn, kn)
            doh = do_ref[:, hs]
            dp = _dot(doh, vv, NT)
            ds = p * (dp - jnp.sum(p * dp, axis=1, keepdims=True)) * scale
            dqv, gq = _head_rms_bwd(_dot(ds, kn), qv, rq, qg_ref[...])
            dkv, gk = _head_rms_bwd(_dot(ds, qn, TN), kv, rk, kg_ref[...])
            dq_ref[:, hs] = dqv
            dkv_ref[:, hs] += dkv
            dkv_ref[:, vs] += _dot(p, doh, TN)
            gq_sum = gq_sum + gq
            gk_sum = gk_sum + gk
        dqg_ref[...] += gq_sum
        dkg_ref[...] += gk_sum

    vec = pl.BlockSpec((1, XA_HD), lambda b, i: (0, 0))
    row = pl.BlockSpec((tq, w), lambda b, i: (b * nq + i, 0))
    mem = pl.BlockSpec((mlen, 2 * w), lambda b, i: (b, 0))
    return pl.pallas_call(
        body, name="xattn_bwd", grid=(bsz, nq),
        in_specs=[row, mem, vec, vec, row],
        out_specs=(row, mem, vec, vec),
        out_shape=(jax.ShapeDtypeStruct((t, w), F32), jax.ShapeDtypeStruct((bsz * mlen, 2 * w), F32),
                   jax.ShapeDtypeStruct((1, XA_HD), F32), jax.ShapeDtypeStruct((1, XA_HD), F32)),
        compiler_params=_params(("arbitrary", "arbitrary")),
    )(qx, kvx, qg, kg, do)


def _loss_sum(dy, d_model, *, tm=512):
    t, d = dy.shape
    tm = min(tm, t)
    steps = t // tm

    def body(dy_ref, o_ref, acc_ref):
        i = pl.program_id(0)

        @pl.when(i == 0)
        def _():
            acc_ref[...] = jnp.zeros_like(acc_ref)

        diff = dy_ref[...] * float(d_model)
        acc_ref[...] += jnp.sum(diff * diff, axis=0, keepdims=True)

        @pl.when(i == steps - 1)
        def _():
            o_ref[...] = jnp.zeros_like(o_ref) + 0.5 * jnp.sum(acc_ref[...]) / float(d_model)

    return pl.pallas_call(
        body, name="loss_sum", grid=(steps,),
        in_specs=[pl.BlockSpec((tm, d), lambda i: (i, 0))],
        out_specs=pl.BlockSpec((1, 128), lambda i: (0, 0)),
        out_shape=jax.ShapeDtypeStruct((1, 128), F32),
        scratch_shapes=[pltpu.VMEM((1, d), F32)],
        compiler_params=_params(("arbitrary",)),
    )(dy)


def _adamw_math(w, g, m, v):
    m = ADAM_B1 * m + (1.0 - ADAM_B1) * g
    v = ADAM_B2 * v + (1.0 - ADAM_B2) * (g * g)
    m_hat = m / (1.0 - ADAM_B1 ** ADAM_STEP)
    v_hat = v / (1.0 - ADAM_B2 ** ADAM_STEP)
    return -ADAM_LR * (m_hat / (jnp.sqrt(v_hat) + ADAM_EPS) + ADAM_WD * w), m, v


def _adamw_big(w, g, m, v, *, name, tr=256):
    r, c = w.shape
    tr = min(tr, r)

    def body(w_ref, g_ref, m_ref, v_ref, d_ref, mo_ref, vo_ref):
        d, mn, vn = _adamw_math(w_ref[...], g_ref[...], m_ref[...], v_ref[...])
        d_ref[...] = d
        mo_ref[...] = mn
        vo_ref[...] = vn

    spec = pl.BlockSpec((tr, c), lambda i: (i, 0))
    shp = jax.ShapeDtypeStruct((r, c), F32)
    return pl.pallas_call(
        body, name=name, grid=(r // tr,), in_specs=[spec] * 4, out_specs=(spec,) * 3, out_shape=(shp,) * 3,
        compiler_params=_params(("parallel",)),
    )(w, g, m, v)


def _adamw_small(ws, gs, ms, vs):
    n = len(ws)

    def body(*refs):
        for i in range(n):
            d, mn, vn = _adamw_math(refs[i][...], refs[n + i][...], refs[2 * n + i][...], refs[3 * n + i][...])
            refs[4 * n + i][...] = d
            refs[5 * n + i][...] = mn
            refs[6 * n + i][...] = vn

    shapes = tuple(jax.ShapeDtypeStruct(w.shape, F32) for w in ws)
    return pl.pallas_call(body, name="adamw_small", out_shape=shapes * 3)(*ws, *gs, *ms, *vs)


def _add_halves(g, recv, c_idx, *, name, tr=256):
    _, r, c = g.shape
    h = r // 2
    tr = min(tr, h)
    nt = h // tr

    def body(c_ref, g_ref, r_ref, o_ref):
        del c_ref
        o_ref[...] = g_ref[...] + r_ref[...]

    return pl.pallas_call(
        body, name=name,
        grid_spec=pltpu.PrefetchScalarGridSpec(
            num_scalar_prefetch=1, grid=(4, nt),
            in_specs=[pl.BlockSpec((None, tr, c), lambda k, i, cr: (k, cr[0] * nt + i, 0)),
                      pl.BlockSpec((None, tr, c), lambda k, i, cr: (k, i, 0))],
            out_specs=pl.BlockSpec((None, tr, c), lambda k, i, cr: (k, i, 0))),
        out_shape=jax.ShapeDtypeStruct((4, h, c), F32),
        compiler_params=_params(("parallel", "parallel")),
    )(c_idx, g, recv)


def _add_chips(p, recv, place_idx, *, name, tr=256):
    _, h, c = p.shape
    tr = min(tr, h)
    nt = h // tr

    def body(pi_ref, p_ref, r_ref, o_ref):
        del pi_ref
        o_ref[...] = ((p_ref[...] + r_ref[0]) + r_ref[1]) + r_ref[2]

    return pl.pallas_call(
        body, name=name,
        grid_spec=pltpu.PrefetchScalarGridSpec(
            num_scalar_prefetch=1, grid=(nt,),
            in_specs=[pl.BlockSpec((None, tr, c), lambda i, pi: (pi[0], i, 0)),
                      pl.BlockSpec((3, tr, c), lambda i, pi: (0, i, 0))],
            out_specs=pl.BlockSpec((tr, c), lambda i, pi: (pi[1] * nt + i, 0))),
        out_shape=jax.ShapeDtypeStruct((2 * h, c), F32),
        compiler_params=_params(("parallel",)),
    )(place_idx, p, recv)


def _place_shard(shard, place_idx, *, name, tr=256):
    r, c = shard.shape
    tr = min(tr, r)

    def body(pi_ref, s_ref, o_ref):
        del pi_ref
        o_ref[...] = s_ref[...]

    return pl.pallas_call(
        body, name=name,
        grid_spec=pltpu.PrefetchScalarGridSpec(
            num_scalar_prefetch=1, grid=(r // tr,),
            in_specs=[pl.BlockSpec((tr, c), lambda i, pi: (i, 0))],
            out_specs=pl.BlockSpec((None, tr, c), lambda i, pi: (pi[0], i, 0))),
        out_shape=jax.ShapeDtypeStruct((4, r, c), shard.dtype),
        compiler_params=_params(("parallel",)),
    )(place_idx, shard)


def _place():
    x, y, c = lax.axis_index("x"), lax.axis_index("y"), lax.axis_index("c")
    chips = [(1 - x, y), (x, 1 - y), (1 - x, 1 - y)]
    return x, y, c, chips


ANY = pl.BlockSpec(memory_space=pl.ANY)


def _all_gather_weights(shards, placed):
    n = len(shards)

    def body(*refs):
        ins, outs = refs[:n], refs[2 * n:3 * n]
        send_sems, recv_sems = refs[3 * n:]
        x, y, c, chips = _place()
        me = 2 * x + y

        def half(a, chip_idx, which):
            h = ins[a].shape[0] // 2
            return outs[a].at[chip_idx, pl.ds(which * h, h), :]

        def copy(a, j, chip_idx, which, to, src=None):
            return pltpu.make_async_remote_copy(
                src_ref=half(a, chip_idx, which) if src is None else src, dst_ref=half(a, chip_idx, which),
                send_sem=send_sems.at[a * 6 + j], recv_sem=recv_sems.at[a * 6 + j], device_id=to, device_id_type=MESH)

        for a in range(n):
            h = ins[a].shape[0] // 2
            for j, (px, py) in enumerate(chips):
                copy(a, j, me, c, (px, py, c), src=ins[a].at[pl.ds(c * h, h), :]).start()
        for a in range(n):
            for j, (px, py) in enumerate(chips):
                copy(a, j, 2 * px + py, c, (x, y, c)).wait_recv()
                copy(a, 3 + j, 2 * px + py, c, (x, y, 1 - c)).start()
        for a in range(n):
            for j, (px, py) in enumerate(chips):
                copy(a, 3 + j, 2 * px + py, 1 - c, (x, y, c)).wait_recv()
        for a in range(n):
            h = ins[a].shape[0] // 2
            for j, (px, py) in enumerate(chips):
                copy(a, j, me, c, (px, py, c), src=ins[a].at[pl.ds(c * h, h), :]).wait_send()
                copy(a, 3 + j, 2 * px + py, c, (x, y, 1 - c)).wait_send()

    return pl.pallas_call(
        body, name="all_gather_weights",
        in_specs=[ANY] * (2 * n), out_specs=tuple([ANY] * n),
        out_shape=tuple(jax.ShapeDtypeStruct(p.shape, p.dtype) for p in placed),
        input_output_aliases={n + i: i for i in range(n)},
        scratch_shapes=[pltpu.SemaphoreType.DMA((6 * n,)), pltpu.SemaphoreType.DMA((6 * n,))],
    )(*shards, *placed)


def _exchange_halves(grads, name):
    n = len(grads)

    def body(*refs):
        ins, outs = refs[:n], refs[n:2 * n]
        send_sems, recv_sems = refs[2 * n:]
        x, y, c, _ = _place()

        def copy(a):
            h = ins[a].shape[1] // 2
            return pltpu.make_async_remote_copy(
                src_ref=ins[a].at[:, pl.ds((1 - c) * h, h), :], dst_ref=outs[a],
                send_sem=send_sems.at[a], recv_sem=recv_sems.at[a], device_id=(x, y, 1 - c), device_id_type=MESH)

        for a in range(n):
            copy(a).start()
        for a in range(n):
            copy(a).wait_recv()
        for a in range(n):
            copy(a).wait_send()

    return pl.pallas_call(
        body, name=name,
        in_specs=[ANY] * n, out_specs=tuple([ANY] * n),
        out_shape=tuple(jax.ShapeDtypeStruct((4, g.shape[1] // 2, g.shape[2]), g.dtype) for g in grads),
        scratch_shapes=[pltpu.SemaphoreType.DMA((n,)), pltpu.SemaphoreType.DMA((n,))],
    )(*grads)


def _scatter_chips(parts, name):
    n = len(parts)

    def body(*refs):
        ins, outs = refs[:n], refs[n:2 * n]
        send_sems, recv_sems = refs[2 * n:]
        x, y, c, chips = _place()

        def copy(a, j, chip_idx, to):
            return pltpu.make_async_remote_copy(
                src_ref=ins[a].at[chip_idx], dst_ref=outs[a].at[j],
                send_sem=send_sems.at[a * 3 + j], recv_sem=recv_sems.at[a * 3 + j], device_id=to, device_id_type=MESH)

        for a in range(n):
            for j, (px, py) in enumerate(chips):
                copy(a, j, 2 * px + py, (px, py, c)).start()
        for a in range(n):
            for j, (px, py) in enumerate(chips):
                copy(a, j, 2 * px + py, (px, py, c)).wait_recv()
        for a in range(n):
            for j, (px, py) in enumerate(chips):
                copy(a, j, 2 * px + py, (px, py, c)).wait_send()

    return pl.pallas_call(
        body, name=name,
        in_specs=[ANY] * n, out_specs=tuple([ANY] * n),
        out_shape=tuple(jax.ShapeDtypeStruct((3,) + p.shape[1:], p.dtype) for p in parts),
        scratch_shapes=[pltpu.SemaphoreType.DMA((3 * n,)), pltpu.SemaphoreType.DMA((3 * n,))],
    )(*parts)


HBM = pl.BlockSpec(memory_space=pltpu.HBM)
SEM = pl.BlockSpec(memory_space=pltpu.SEMAPHORE)
EFFECT = pltpu.SideEffectType.DATAFLOW_SIDE_EFFECTING


def _in_hbm(a):
    return pltpu.with_memory_space_constraint(a, pltpu.HBM)


def _split_copy_calls(name, srcs, lands, n_copies, make_copies):
    ns, nl = len(srcs), len(lands)
    nb = ns + nl

    def start():
        def body(*refs):
            copies = make_copies(refs[:ns], refs[ns:nb], refs[nb], refs[nb + 1])
            for cp in copies:
                cp.start()
            token = refs[-1]
            token[...] = jnp.zeros_like(token)

        bufs = [_in_hbm(a) for a in list(srcs) + list(lands)]
        out = pl.pallas_call(
            body, name=name + "_start",
            out_shape=(pltpu.SemaphoreType.DMA((n_copies,)), pltpu.SemaphoreType.DMA((n_copies,)),
                       *[pltpu.HBM(a.shape, a.dtype) for a in bufs], jax.ShapeDtypeStruct((8, 128), F32)),
            in_specs=[HBM] * nb, out_specs=(SEM, SEM, *[HBM] * nb, pl.BlockSpec(memory_space=pltpu.VMEM)),
            input_output_aliases={i: 2 + i for i in range(nb)},
            compiler_params=pltpu.CompilerParams(has_side_effects=EFFECT),
        )(*bufs)
        return dict(send=out[0], recv=out[1], bufs=list(out[2:2 + nb]), token=out[-1])

    def wait(state, after):
        def body(*refs):
            copies = make_copies(refs[:ns], refs[ns:nb], refs[nb], refs[nb + 1])
            for cp in copies:
                cp.wait_send()
            for cp in copies:
                cp.wait_recv()

        bufs = state["bufs"]
        out = pl.pallas_call(
            body, name=name + "_wait",
            out_shape=tuple(pltpu.HBM(a.shape, a.dtype) for a in bufs),
            in_specs=[HBM] * nb + [SEM, SEM, pl.BlockSpec(memory_space=pl.ANY)], out_specs=tuple([HBM] * nb),
            input_output_aliases={i: i for i in range(nb)},
            compiler_params=pltpu.CompilerParams(has_side_effects=EFFECT),
        )(*bufs, state["send"], state["recv"], after)
        return list(out[:ns]), list(out[ns:])

    return start, wait


def _scatter_chips_split(name, parts):
    n = len(parts)
    lands = [lax.empty((3,) + p.shape[1:], p.dtype) for p in parts]

    def make_copies(srcs, lnds, send_sems, recv_sems):
        _, _, c, chips = _place()
        return [pltpu.make_async_remote_copy(
            src_ref=srcs[a].at[2 * px + py], dst_ref=lnds[a].at[j], send_sem=send_sems.at[a * 3 + j],
            recv_sem=recv_sems.at[a * 3 + j], device_id=(px, py, c), device_id_type=MESH)
            for a in range(n) for j, (px, py) in enumerate(chips)]

    return _split_copy_calls(name, parts, lands, 3 * n, make_copies)


def _gather_chips_split(name, shards, lands):
    n = len(shards)

    def make_copies(srcs, lnds, send_sems, recv_sems):
        x, y, c, chips = _place()
        out = []
        for a in range(n):
            h = srcs[a].shape[0] // 2
            for j, (px, py) in enumerate(chips):
                out.append(pltpu.make_async_remote_copy(
                    src_ref=srcs[a].at[pl.ds(c * h, h), :], dst_ref=lnds[a].at[2 * x + y, pl.ds(c * h, h), :],
                    send_sem=send_sems.at[a * 3 + j], recv_sem=recv_sems.at[a * 3 + j],
                    device_id=(px, py, c), device_id_type=MESH))
        return out

    return _split_copy_calls(name, shards, lands, 3 * n, make_copies)


def _gather_finish(gathered):
    n = len(gathered)

    def body(*refs):
        outs = refs[n:2 * n]
        send_sems, recv_sems = refs[2 * n:]
        x, y, c, chips = _place()

        def copy(a, j, chip_idx, which):
            h = outs[a].shape[1] // 2
            rows = outs[a].at[chip_idx, pl.ds(which * h, h), :]
            return pltpu.make_async_remote_copy(
                src_ref=rows, dst_ref=rows, send_sem=send_sems.at[a * 3 + j], recv_sem=recv_sems.at[a * 3 + j],
                device_id=(x, y, 1 - c), device_id_type=MESH)

        for a in range(n):
            for j, (px, py) in enumerate(chips):
                copy(a, j, 2 * px + py, c).start()
        for a in range(n):
            for j, (px, py) in enumerate(chips):
                copy(a, j, 2 * px + py, 1 - c).wait_recv()
        for a in range(n):
            for j, (px, py) in enumerate(chips):
                copy(a, j, 2 * px + py, c).wait_send()

    return pl.pallas_call(
        body, name="gather_finish",
        in_specs=[ANY] * n, out_specs=tuple([ANY] * n),
        out_shape=tuple(jax.ShapeDtypeStruct(g.shape, g.dtype) for g in gathered),
        input_output_aliases={i: i for i in range(n)},
        scratch_shapes=[pltpu.SemaphoreType.DMA((3 * n,)), pltpu.SemaphoreType.DMA((3 * n,))],
    )(*gathered)


def _join_halves(fulls):
    n = len(fulls)

    def body(*refs):
        outs = refs[n:2 * n]
        send_sems, recv_sems = refs[2 * n:]
        x, y, c, _ = _place()

        def copy(a, which):
            h = outs[a].shape[0] // 2
            rows = outs[a].at[pl.ds(which * h, h), :]
            return pltpu.make_async_remote_copy(
                src_ref=rows, dst_ref=rows, send_sem=send_sems.at[a], recv_sem=recv_sems.at[a],
                device_id=(x, y, 1 - c), device_id_type=MESH)

        for a in range(n):
            copy(a, c).start()
        for a in range(n):
            copy(a, 1 - c).wait_recv()
        for a in range(n):
            copy(a, c).wait_send()

    return pl.pallas_call(
        body, name="rs_join_halves",
        in_specs=[ANY] * n, out_specs=tuple([ANY] * n),
        out_shape=tuple(jax.ShapeDtypeStruct(p.shape, p.dtype) for p in fulls),
        input_output_aliases={i: i for i in range(n)},
        scratch_shapes=[pltpu.SemaphoreType.DMA((n,)), pltpu.SemaphoreType.DMA((n,))],
    )(*fulls)


def _all_reduce_small(sm):
    r, w = sm.shape

    def body(sm_ref, o_ref, buf, send_sems, recv_sems):
        x, y, c, _ = _place()
        me = 4 * x + 2 * y + c
        buf[me] = sm_ref[...]
        rel = [(dx, dy, dc) for dx in (0, 1) for dy in (0, 1) for dc in (0, 1)][1:]

        def copy(k, slot, to):
            return pltpu.make_async_remote_copy(
                src_ref=sm_ref, dst_ref=buf.at[slot], send_sem=send_sems.at[k], recv_sem=recv_sems.at[k],
                device_id=to, device_id_type=MESH)

        peers = []
        for k, (dx, dy, dc) in enumerate(rel):
            px = 1 - x if dx else x
            py = 1 - y if dy else y
            pc = 1 - c if dc else c
            peers.append((px, py, pc))
            copy(k, me, (px, py, pc)).start()
        for k, (px, py, pc) in enumerate(peers):
            copy(k, 4 * px + 2 * py + pc, (px, py, pc)).wait_recv()
        for k, (px, py, pc) in enumerate(peers):
            copy(k, me, (px, py, pc)).wait_send()
        acc = buf[0]
        for d in range(1, 8):
            acc = acc + buf[d]
        o_ref[...] = acc

    vm = pl.BlockSpec(memory_space=pltpu.VMEM)
    return pl.pallas_call(
        body, name="all_reduce_small", in_specs=[vm], out_specs=vm,
        out_shape=jax.ShapeDtypeStruct((r, w), F32),
        scratch_shapes=[pltpu.VMEM((8, r, w), F32), pltpu.SemaphoreType.DMA((7,)), pltpu.SemaphoreType.DMA((7,))],
    )(sm)


class _LocalWeights:
    def __init__(self, w):
        self.w = w
        self.g = {}

    def first(self):
        return self.w

    def rest(self, after):
        del after
        return self.w

    def grads(self, tag, g):
        del tag
        self.g.update(g)
        return None


def _local_step(x3, mem3, pos2, target3, small, comm):
    bsz, seq, d = x3.shape
    mlen = mem3.shape[1]
    t = bsz * seq
    ds = d // 4
    w = comm.first()
    x = x3.reshape(t, d)
    mem = mem3.reshape(bsz * mlen, d)
    target = target3.reshape(t, d)
    pos = pos2.reshape(t, 1)
    qg_t = jnp.tile(small["sw_q_norm_g"], (1, SW_HEADS))
    kg_t = jnp.tile(small["sw_k_norm_g"], (1, SW_KV_HEADS))

    hn1 = _rms_fwd(x, small["norm1_g"], name="rms1_fwd")
    proj_hg = _mm(hn1, w["w_in_hg"], NN, t, HG_COLS, d, name="proj_hg", tk=d, after=(w.get("token"),))[0]
    proj_sw = _mm(hn1, w["w_in_sw"], NN, t, SW_COLS, d, name="proj_sw", tk=d)[0]
    y_mix, o_hg, states = _hg_fwd(proj_hg, small["hg_lower_bounds"], small["hg_norm_g"], bsz, seq, y_width=1024)
    y_mix = _sw_fwd(proj_sw, pos, qg_t, kg_t, small["sw_sinks"], y_mix, bsz, seq)
    w_in_hg, w_in_sw = w["w_in_hg"], w["w_in_sw"]
    w = comm.rest(y_mix)
    ff = w["down"].shape[0]
    ffs = ff // 4
    h1 = _mm(y_mix, w["w_out"], NN, t, d, 1024, name="out_proj", tk=1024, extras=(x,),
             epilogue=lambda acc, res: (acc + res,))[0]
    hn2 = _rms_fwd(h1, small["norm2_g"], name="rms2_fwd")
    mn = _rms_fwd(mem, small["mem_norm_g"], name="rms_mem_fwd")
    qx = _mm(hn2, w["wq"], NN, t, 512, d, name="xa_q", tk=d)[0]
    kvx = _mm(mn, w["wkv"], NN, bsz * mlen, 1024, d, name="xa_kv", tk=d)[0]
    ox = _xa_fwd(qx, kvx, small["xa_q_norm_g"], small["xa_k_norm_g"], bsz, seq, mlen)
    h2 = _mm(ox, w["wo"], NN, t, d, 512, name="xa_o", tn=ds, tk=512, extras=(h1,),
             b_spec=pl.BlockSpec((None, 512, ds), lambda i, j, kk: (j, 0, 0)),
             epilogue=lambda acc, res: (acc + res,))[0]
    hn3 = _rms_fwd(h2, small["norm3_g"], name="rms3_fwd")

    def relu_sq(acc):
        a = jnp.maximum(acc, 0.0)
        return a, a * a

    act, act2 = _mm(hn3, w["up"], NN, t, ff, d, name="mlp_up", tn=ffs, tk=d,
                    b_spec=pl.BlockSpec((None, d, ffs), lambda i, j, kk: (j, 0, 0)),
                    epilogue=relu_sq, out_dtypes=(_MXU_DTYPE, _MXU_DTYPE))
    inv_d = 1.0 / d
    dy = _mm(act2, w["down"], NN, t, d, ff, name="mlp_down", extras=(h2, target),
             epilogue=lambda acc, res, tgt: ((acc + res - tgt) * inv_d,))[0]
    loss_row = _loss_sum(dy, d)

    dz = _mm(dy, w["down"], NT, t, ff, d, name="d_act", tk=d, extras=(act,),
             epilogue=lambda acc, a: (acc * (2.0 * a.astype(F32)),), out_dtypes=(_MXU_DTYPE,))[0]
    g_down = _mm(act2, dy, TN, ff, d, t, name="g_down")[0]
    g_up = _mm(hn3, dz, TN, d, ff, t, name="g_up", tn=ffs,
               out_shape=(jax.ShapeDtypeStruct((4, d, ffs), F32),),
               out_spec=(pl.BlockSpec((None, min(1024, d), ffs), lambda i, j, kk: (j, i, 0)),))[0]
    tok = comm.grads("mlp", dict(up=g_up, down=g_down))
    dhn3 = _mm(dz, w["up"], NT, t, d, ff, name="d_hn3", tk=ffs, after=(tok,),
               b_spec=pl.BlockSpec((None, min(1024, d), ffs), lambda i, j, kk: (kk, j, 0)))[0]
    dh2, g_norm3 = _rms_bwd(h2, small["norm3_g"], dhn3, dy, name="rms3_bwd")
    d_ox = _mm(dh2, w["wo"], NT, t, 512, d, name="d_ox", tk=ds,
               b_spec=pl.BlockSpec((None, 512, ds), lambda i, j, kk: (kk, 0, 0)))[0]
    g_wo = _mm(ox, dh2, TN, 512, d, t, name="g_wo", tn=ds,
               out_shape=(jax.ShapeDtypeStruct((4, 512, ds), F32),),
               out_spec=(pl.BlockSpec((None, 512, ds), lambda i, j, kk: (j, 0, 0)),))[0]
    d_qx, d_kvx, g_xq, g_xk = _xa_bwd(qx, kvx, small["xa_q_norm_g"], small["xa_k_norm_g"], d_ox, bsz, seq, mlen)
    g_wq = _mm(hn2, d_qx, TN, d, 512, t, name="g_wq")[0]
    g_wkv = _mm(mn, d_kvx, TN, d, 1024, bsz * mlen, name="g_wkv")[0]
    dhn2 = _mm(d_qx, w["wq"], NT, t, d, 512, name="d_hn2", tk=512)[0]
    dmn = _mm(d_kvx, w["wkv"], NT, bsz * mlen, d, 1024, name="d_mn", tk=1024)[0]
    dh1, g_norm2 = _rms_bwd(h1, small["norm2_g"], dhn2, dh2, name="rms2_bwd")
    _, g_memn = _rms_bwd(mem, small["mem_norm_g"], dmn, None, name="rms_mem_bwd")
    g_wout = _mm(y_mix, dh1, TN, 1024, d, t, name="g_wout")[0]
    tok = comm.grads("mid", dict(w_out=g_wout, wq=g_wq, wkv=g_wkv, wo=g_wo))
    d_mix = _mm(dh1, w["w_out"], NT, t, 1024, d, name="d_mix", tk=d, after=(tok,))[0]
    dproj_sw, g_swq, g_swk, g_sinks = _sw_bwd(proj_sw, pos, qg_t, kg_t, small["sw_sinks"], y_mix, d_mix, bsz, seq)
    dproj_hg, g_lb, g_hgn = _hg_bwd(proj_hg, small["hg_lower_bounds"], small["hg_norm_g"], o_hg, states, d_mix, bsz, seq)
    g_in_hg = _mm(hn1, dproj_hg, TN, d, HG_COLS, t, name="g_in_hg")[0]
    g_in_sw = _mm(hn1, dproj_sw, TN, d, SW_COLS, t, name="g_in_sw")[0]
    comm.grads("in", dict(w_in_hg=g_in_hg, w_in_sw=g_in_sw))
    dhn1_a = _mm(dproj_hg, w_in_hg, NT, t, d, HG_COLS, name="d_hn1_hg", tk=1024)[0]
    dhn1 = _mm(dproj_sw, w_in_sw, NT, t, d, SW_COLS, name="d_hn1_sw", tk=SW_COLS, extras=(dhn1_a,),
               epilogue=lambda acc, prev: (acc + prev,))[0]
    grad_x, g_norm1 = _rms_bwd(x, small["norm1_g"], dhn1, dh1, name="rms1_bwd")

    g_small = dict(norm1_g=g_norm1, hg_lower_bounds=g_lb, hg_norm_g=g_hgn, sw_q_norm_g=g_swq, sw_k_norm_g=g_swk,
                   sw_sinks=g_sinks[:, 0:SW_HEADS], norm2_g=g_norm2, mem_norm_g=g_memn, xa_q_norm_g=g_xq,
                   xa_k_norm_g=g_xk, norm3_g=g_norm3)
    return loss_row, grad_x.reshape(bsz, seq, d), g_small


SMALL_NAMES = ("norm1_g", "hg_lower_bounds", "hg_norm_g", "sw_q_norm_g", "sw_k_norm_g", "sw_sinks", "norm2_g",
               "mem_norm_g", "xa_q_norm_g", "xa_k_norm_g", "norm3_g")
BIG_NAMES = ("w_in", "w_out", "xa_wq", "xa_wkv", "xa_wo", "mlp_up", "mlp_down")
WEIGHT_ORDER = ("norm1_g", "w_in", "hg_lower_bounds", "hg_norm_g", "sw_q_norm_g", "sw_k_norm_g", "sw_sinks", "w_out",
                "norm2_g", "mem_norm_g", "xa_wq", "xa_wkv", "xa_q_norm_g", "xa_k_norm_g", "xa_wo", "norm3_g",
                "mlp_up", "mlp_down")


def _pack_rows(vals, width):
    starts, at = [], 0
    for v in vals:
        starts.append(at)
        at += v.shape[0]
    total = at + (-at) % 8
    out = None
    for v, s in zip(vals, starts):
        placed = jnp.pad(v, ((s, total - s - v.shape[0]), (0, width - v.shape[1])))
        out = placed if out is None else out + placed
    return out, starts


class _MeshWeights:
    LATE = ("w_out", "xa_wq", "xa_wkv", "xa_wo", "mlp_up", "mlp_down")

    def __init__(self, shards, d, ff):
        self.shards, self.d, self.ff = shards, d, ff
        self.c_idx = lax.axis_index("c").astype(jnp.int32).reshape(1)
        chip = (2 * lax.axis_index("x") + lax.axis_index("y")).astype(jnp.int32)
        self.place_idx = jnp.stack([chip, lax.axis_index("c").astype(jnp.int32)])
        self.pending = []
        self.halves = {}

    def first(self):
        placed = {n: _place_shard(s, self.place_idx, name="place_" + n) for n, s in self.shards.items()}
        (g_in,) = _all_gather_weights([self.shards["w_in"]], [placed["w_in"]])
        start, self.late_wait = _gather_chips_split("gather_late", [self.shards[n] for n in self.LATE],
                                                    [placed[n] for n in self.LATE])
        self.late_state = start()
        full = jnp.concatenate([g_in[k] for k in range(4)], axis=1)
        return dict(w_in_hg=full[:, :HG_COLS], w_in_sw=full[:, HG_COLS:], token=self.late_state["token"])

    def rest(self, after):
        _, lands = self.late_wait(self.late_state, after)
        g_out, g_q, g_kv, g_o, g_up, g_dn = _gather_finish(lands)
        d = self.d
        return dict(w_out=g_out.reshape(-1, d), wq=g_q.reshape(d, -1), wkv=g_kv.reshape(d, -1), wo=g_o, up=g_up,
                    down=g_dn.reshape(self.ff, d))

    def _chip_partials(self, tag, names, arrays):
        recv = _exchange_halves(arrays, "rs_exchange_" + tag)
        return [_add_halves(g, r, self.c_idx, name="rs_add_halves_" + n) for n, g, r in zip(names, arrays, recv)]

    def grads(self, tag, g):
        d, ff = self.d, self.ff
        if tag == "mlp":
            names, arrays = ("mlp_up", "mlp_down"), [g["up"], g["down"].reshape(4, ff // 4, d)]
        elif tag == "mid":
            names = ("w_out", "xa_wq", "xa_wkv", "xa_wo")
            arrays = [g["w_out"].reshape(4, -1, d), g["wq"].reshape(4, d // 4, -1), g["wkv"].reshape(4, d // 4, -1), g["wo"]]
        else:
            self.g_in = g
            return None
        parts = self._chip_partials(tag, names, arrays)
        start, wait = _scatter_chips_split("rs_scatter_" + tag, parts)
        state = start()
        self.pending.append((names, wait, state))
        return state["token"]

    def finish(self):
        full = jnp.concatenate([self.g_in["w_in_hg"], self.g_in["w_in_sw"]], axis=1)
        ws = full.shape[1] // 4
        parts = self._chip_partials("in", ("w_in",), [jnp.stack([full[:, ws * k:ws * (k + 1)] for k in range(4)])])
        (recv,) = _scatter_chips(parts, "rs_scatter_in")
        self.halves["w_in"] = _add_chips(parts[0], recv, self.place_idx, name="rs_add_chips_w_in")
        for names, wait, state in self.pending:
            srcs, lands = wait(state, self.halves["w_in"])
            for n, p, r in zip(names, srcs, lands):
                self.halves[n] = _add_chips(p, r, self.place_idx, name="rs_add_chips_" + n)
        return dict(zip(BIG_NAMES, _join_halves([self.halves[n] for n in BIG_NAMES])))


def kernel(x, mem, positions, norm1_g, w_in, hg_lower_bounds, hg_norm_g, sw_q_norm_g, sw_k_norm_g, sw_sinks, w_out, norm2_g, mem_norm_g, xa_wq, xa_wkv, xa_q_norm_g, xa_k_norm_g, xa_wo, norm3_g, mlp_up, mlp_down, loss_target, m_norm1_g, m_w_in, m_hg_lower_bounds, m_hg_norm_g, m_sw_q_norm_g, m_sw_k_norm_g, m_sw_sinks, m_w_out, m_norm2_g, m_mem_norm_g, m_xa_wq, m_xa_wkv, m_xa_q_norm_g, m_xa_k_norm_g, m_xa_wo, m_norm3_g, m_mlp_up, m_mlp_down, v_norm1_g, v_w_in, v_hg_lower_bounds, v_hg_norm_g, v_sw_q_norm_g, v_sw_k_norm_g, v_sw_sinks, v_w_out, v_norm2_g, v_mem_norm_g, v_xa_wq, v_xa_wkv, v_xa_q_norm_g, v_xa_k_norm_g, v_xa_wo, v_norm3_g, v_mlp_up, v_mlp_down):
    given = dict(locals())
    weights = {n: given[n] for n in WEIGHT_ORDER}
    moms = {n: given["m_" + n] for n in WEIGHT_ORDER}
    vars_ = {n: given["v_" + n] for n in WEIGHT_ORDER}
    d = x.shape[-1]
    ff = mlp_down.shape[1] * 4
    small = {n: weights[n] for n in SMALL_NAMES}

    comm = _MeshWeights({n: weights[n][0].astype(_MXU_DTYPE) for n in BIG_NAMES}, d, ff)
    loss_row, grad_x, g_small = _local_step(x, mem, positions, loss_target, small, comm)
    big_grads = comm.finish()

    packed, starts = _pack_rows([g_small[n] for n in SMALL_NAMES] + [loss_row], 1024)
    summed = _all_reduce_small(packed)
    small_grads = {}
    for n, s in zip(SMALL_NAMES, starts):
        r, c = weights[n].shape
        small_grads[n] = summed[s:s + r, 0:c]
    loss = summed[starts[-1], 0]

    grads, deltas, new_m, new_v = {}, {}, {}, {}
    for n in BIG_NAMES:
        shp = weights[n].shape
        g2 = big_grads[n]
        dl, mo, vo = _adamw_big(weights[n][0], g2, moms[n][0], vars_[n][0], name="adamw_" + n)
        grads[n], deltas[n], new_m[n], new_v[n] = (a.reshape(shp) for a in (g2, dl, mo, vo))
    sm_out = _adamw_small([weights[n] for n in SMALL_NAMES], [small_grads[n] for n in SMALL_NAMES],
                          [moms[n] for n in SMALL_NAMES], [vars_[n] for n in SMALL_NAMES])
    ns = len(SMALL_NAMES)
    for i, n in enumerate(SMALL_NAMES):
        grads[n], deltas[n], new_m[n], new_v[n] = small_grads[n], sm_out[i], sm_out[ns + i], sm_out[2 * ns + i]

    return (loss, grad_x, *[grads[n] for n in WEIGHT_ORDER], *[deltas[n] for n in WEIGHT_ORDER],
            *[new_m[n] for n in WEIGHT_ORDER], *[new_v[n] for n in WEIGHT_ORDER])
```

```python
import numpy as np
import jax
import jax.numpy as jnp
from jax import lax
from jax.experimental import pallas as pl
from jax.experimental.pallas import tpu as pltpu

F32 = jnp.float32
_MXU_DTYPE = jnp.bfloat16

EPS = 1e-6
HG_HEADS = 4
HG_D = 128
HG_CHUNK = 64
HG_TILE = 512
HG_LEVELS = (32, 16, 8, 4, 2, 1)
SW_HEADS = 8
SW_KV_HEADS = 2
SW_GROUP = SW_HEADS // SW_KV_HEADS
SW_HD = 64
SW_BLOCK = 128
ROPE_THETA = 500000.0
ROT_DIM = SW_HD // 4
XA_HEADS = 4
XA_HD = 128
HG_COLS = 4 * HG_HEADS * HG_D
SW_COLS = (SW_HEADS + 2 * SW_KV_HEADS) * SW_HD

ADAM_LR = 0.001
ADAM_B1 = 0.9
ADAM_B2 = 0.999
ADAM_EPS = 1e-08
ADAM_WD = 0.01
ADAM_STEP = 10

VMEM_LIMIT = 56 * 1024 * 1024
MESH = pl.DeviceIdType.MESH

NN = ((1,), (0,))
NT = ((1,), (1,))
TN = ((0,), (0,))


def _mx(v):
    return v.astype(_MXU_DTYPE)


def _dot(a, b, dims=NN):
    return lax.dot_general(_mx(a), _mx(b), (dims, ((), ())), preferred_element_type=F32)


def _split_dot(a, v, dims, parts):
    acc = None
    rest = v
    for p in range(parts):
        piece = _mx(rest)
        term = lax.dot_general(a, piece, (dims, ((), ())), preferred_element_type=F32)
        acc = term if acc is None else acc + term
        if p + 1 < parts:
            rest = rest - piece.astype(F32)
    return acc


def _params(sem):
    return pltpu.CompilerParams(dimension_semantics=sem, vmem_limit_bytes=VMEM_LIMIT)


def _mm(a, b, mode, m, n, k, *, name, tm=1024, tn=1024, tk=512, a_spec=None, b_spec=None, extras=(), epilogue=None,
        out_dtypes=(F32,), out_shape=None, out_spec=None, after=()):
    after = tuple(t for t in after if t is not None)
    tm, tn, tk = min(tm, m), min(tn, n), min(tk, k)
    assert m % tm == 0 and n % tn == 0 and k % tk == 0, (name, m, n, k, tm, tn, tk)
    gi, gj, gk = m // tm, n // tn, k // tk
    if a_spec is None:
        a_spec = (pl.BlockSpec((tk, tm), lambda i, j, kk: (kk, i)) if mode == TN
                  else pl.BlockSpec((tm, tk), lambda i, j, kk: (i, kk)))
    if b_spec is None:
        b_spec = (pl.BlockSpec((tn, tk), lambda i, j, kk: (j, kk)) if mode == NT
                  else pl.BlockSpec((tk, tn), lambda i, j, kk: (kk, j)))
    mn_spec = pl.BlockSpec((tm, tn), lambda i, j, kk: (i, j))
    if epilogue is None:
        epilogue = lambda acc: (acc,)
    n_ex, n_out = len(extras), len(out_dtypes)
    if out_shape is None:
        out_shape = tuple(jax.ShapeDtypeStruct((m, n), d) for d in out_dtypes)
        out_spec = tuple(mn_spec for _ in out_dtypes)

    n_after = len(after)

    def body(*refs):
        a_ref, b_ref = refs[0], refs[1]
        ex = refs[2:2 + n_ex]
        outs = refs[2 + n_ex + n_after:2 + n_ex + n_after + n_out]

        def finish(acc):
            res = epilogue(acc, *[e[...] for e in ex])
            for o, r in zip(outs, res):
                o[...] = r.astype(o.dtype)

        if gk == 1:
            finish(_dot(a_ref[...], b_ref[...], mode))
        else:
            acc_ref = refs[-1]
            kk = pl.program_id(2)

            @pl.when(kk == 0)
            def _():
                acc_ref[...] = jnp.zeros_like(acc_ref)

            acc_ref[...] += _dot(a_ref[...], b_ref[...], mode)

            @pl.when(kk == gk - 1)
            def _():
                finish(acc_ref[...])

    return pl.pallas_call(
        body, name=name, grid=(gi, gj, gk),
        in_specs=[a_spec, b_spec] + [mn_spec] * n_ex + [pl.BlockSpec(memory_space=pl.ANY)] * n_after,
        out_specs=out_spec, out_shape=out_shape,
        scratch_shapes=[pltpu.VMEM((tm, tn), F32)] if gk > 1 else [],
        compiler_params=_params(("parallel", "parallel", "arbitrary")),
    )(a, b, *extras, *after)


def _rms_fwd(x, g, *, name, tm=512):
    t, d = x.shape
    tm = min(tm, t)

    def body(x_ref, g_ref, o_ref):
        xv = x_ref[...]
        r = lax.rsqrt(jnp.mean(xv * xv, axis=1, keepdims=True) + EPS)
        o_ref[...] = (xv * r * g_ref[...]).astype(o_ref.dtype)

    return pl.pallas_call(
        body, name=name, grid=(t // tm,),
        in_specs=[pl.BlockSpec((tm, d), lambda i: (i, 0)), pl.BlockSpec((1, d), lambda i: (0, 0))],
        out_specs=pl.BlockSpec((tm, d), lambda i: (i, 0)),
        out_shape=jax.ShapeDtypeStruct((t, d), _MXU_DTYPE),
        compiler_params=_params(("parallel",)),
    )(x, g)


def _rms_bwd(x, g, dy, dres, *, name, tm=512):
    t, d = x.shape
    tm = min(tm, t)
    has_res = dres is not None

    def body(*refs):
        x_ref, g_ref, dy_ref = refs[:3]
        dx_ref, dg_ref = refs[-2:]
        xv, dyv = x_ref[...], dy_ref[...]
        r = lax.rsqrt(jnp.mean(xv * xv, axis=1, keepdims=True) + EPS)
        u = dyv * g_ref[...]
        dx = r * u - xv * (r * r * r) * jnp.mean(u * xv, axis=1, keepdims=True)
        if has_res:
            dx = dx + refs[3][...]
        dx_ref[...] = dx

        @pl.when(pl.program_id(0) == 0)
        def _():
            dg_ref[...] = jnp.zeros_like(dg_ref)

        dg_ref[...] += jnp.sum(dyv * xv * r, axis=0, keepdims=True)

    row = pl.BlockSpec((tm, d), lambda i: (i, 0))
    vec = pl.BlockSpec((1, d), lambda i: (0, 0))
    return pl.pallas_call(
        body, name=name, grid=(t // tm,),
        in_specs=[row, vec, row] + ([row] if has_res else []),
        out_specs=(row, vec),
        out_shape=(jax.ShapeDtypeStruct((t, d), F32), jax.ShapeDtypeStruct((1, d), F32)),
        compiler_params=_params(("arbitrary",)),
    )(*([x, g, dy] + ([dres] if has_res else [])))


def _hg_constants():
    c = HG_CHUNK
    t = np.arange(c)
    sums = [t[None, :] <= t[:, None]]
    masks = []
    for m in HG_LEVELS:
        base = (t // (2 * m)) * (2 * m)
        mid = base + m - 1
        second = (t - base) >= m
        upper = (t[None, :] > mid[:, None]) & (t[None, :] <= t[:, None])
        lower = (t[None, :] > t[:, None]) & (t[None, :] <= mid[:, None])
        sums.append(np.where(second[:, None], upper, lower))
        masks.append(second[:, None] & (~second)[None, :] & (base[:, None] == base[None, :]))
    return (np.concatenate(sums, axis=0).astype(np.float32), np.stack(masks).astype(np.float32))


HG_HEAD_LANES = tuple(slice(HG_D * h, HG_D * (h + 1)) for h in range(HG_HEADS))


def _per_head(fn, slab):
    return jnp.concatenate([jnp.broadcast_to(fn(slab[:, hs]), (slab.shape[0], HG_D)) for hs in HG_HEAD_LANES], axis=1)


def _lane_sum(v):
    return jnp.sum(v, axis=1, keepdims=True)


def _lane_mean(v):
    return jnp.mean(v, axis=1, keepdims=True)


def _hg_gates(blk, lbp):
    w = HG_HEADS * HG_D
    q, x, v, gl = blk[:, 0:w], blk[:, w:2 * w], blk[:, 2 * w:3 * w], blk[:, 3 * w:4 * w]
    mx = jnp.max(lbp, axis=0, keepdims=True)
    e = jnp.exp(lbp - mx)
    lb = e[0:1, :] / jnp.sum(e, axis=0, keepdims=True)
    sig = jax.nn.sigmoid(x)
    f = lb + (1.0 - lb) * sig
    return q, v, gl, lb, sig, f, 1.0 - f, jnp.log(f)


def _hg_fwd(proj, lbp, ng, bsz, seq, *, y_width):
    t = proj.shape[0]
    nc = seq // HG_CHUNK
    a_np, m_np = _hg_constants()
    a_all = jnp.asarray(a_np, _MXU_DTYPE)
    masks = jnp.asarray(m_np, F32)
    nl = len(HG_LEVELS)

    ts = min(HG_TILE, seq)
    ns, nct = seq // ts, ts // HG_CHUNK
    hw = HG_HEADS * HG_D

    def body(p_ref, lb_ref, ng_ref, a_ref, m_ref, y_ref, o_ref, st_ref, carry):
        a_mat = a_ref[...]
        ngv = ng_ref[...]

        @pl.when(pl.program_id(1) == 0)
        def _():
            carry[...] = jnp.zeros_like(carry)

        ng4 = _tile_lanes(ngv, HG_HEADS)
        heads = range(HG_HEADS)
        hl = HG_HEAD_LANES

        def chunk(c, _):
            rows = pl.ds(pl.multiple_of(c * HG_CHUNK, HG_CHUNK), HG_CHUNK)
            q, v, gl, lb, sig, f, k, g = _hg_gates(p_ref[rows, :], lb_ref[...])
            sts = [carry[h] for h in heads]
            e_all = _split_dot(a_mat, g, NN, 3)
            b = e_all[0:HG_CHUNK]
            qb = q * jnp.exp(b)
            o = [_dot(qb[:, hl[h]], sts[h], NT) for h in heads]
            p = [jnp.zeros((HG_CHUNK, HG_CHUNK), F32) for _ in heads]
            for li in range(nl):
                e = jnp.exp(e_all[HG_CHUNK * (li + 1):HG_CHUNK * (li + 2)])
                qm, km, mk = q * e, k * e, m_ref[li]
                p = [p[h] + mk * _dot(qm[:, hl[h]], km[:, hl[h]], NT) for h in heads]
            bl = b[HG_CHUNK - 1:HG_CHUNK, :]
            kd = k * jnp.exp(bl - b)
            ebl = jnp.exp(bl)
            pv = [_dot(p[h], v[:, hl[h]]) for h in heads]
            upd = [_dot(v[:, hl[h]], kd[:, hl[h]], TN) for h in heads]
            o_all = jnp.concatenate([o[h] + pv[h] for h in heads], axis=1) + _per_head(_lane_sum, q * k) * v
            r = lax.rsqrt(_per_head(_lane_mean, o_all * o_all) + EPS)
            for h in heads:
                st_ref[h, c] = sts[h]
                carry[h] = sts[h] * ebl[:, hl[h]] + upd[h]
            o_ref[rows, :] = o_all
            y_ref[rows, :] = (o_all * r * ng4) * (gl * jax.nn.sigmoid(gl))
            return 0

        lax.fori_loop(0, nct, chunk, 0)

    return pl.pallas_call(
        body, name="hgrn2_fwd", grid=(bsz, ns),
        in_specs=[pl.BlockSpec((ts, HG_COLS), lambda b, s: (b * ns + s, 0)),
                  pl.BlockSpec((2, hw), lambda b, s: (0, 0)),
                  pl.BlockSpec((1, HG_D), lambda b, s: (0, 0)),
                  pl.BlockSpec(a_all.shape, lambda b, s: (0, 0)),
                  pl.BlockSpec(masks.shape, lambda b, s: (0, 0, 0))],
        out_specs=(pl.BlockSpec((ts, hw), lambda b, s: (b * ns + s, 0)),
                   pl.BlockSpec((ts, hw), lambda b, s: (b * ns + s, 0)),
                   pl.BlockSpec((None, HG_HEADS, nct, HG_D, HG_D), lambda b, s: (b, 0, s, 0, 0))),
        out_shape=(jax.ShapeDtypeStruct((t, y_width), F32),
                   jax.ShapeDtypeStruct((t, hw), F32),
                   jax.ShapeDtypeStruct((bsz, HG_HEADS, nc, HG_D, HG_D), F32)),
        scratch_shapes=[pltpu.VMEM((HG_HEADS, HG_D, HG_D), F32)],
        compiler_params=_params(("parallel", "arbitrary")),
    )(proj, lbp, ng, a_all, masks)


def _hg_bwd(proj, lbp, ng, o_all, states, dy, bsz, seq):
    t = proj.shape[0]
    nc = seq // HG_CHUNK
    a_np, m_np = _hg_constants()
    a_all = jnp.asarray(a_np, _MXU_DTYPE)
    masks = jnp.asarray(m_np, F32)
    nl = len(HG_LEVELS)
    cs = HG_CHUNK

    ts = min(HG_TILE, seq)
    ns, nct = seq // ts, ts // cs
    hw = HG_HEADS * HG_D

    def body(p_ref, lb_ref, ng_ref, a_ref, m_ref, o_ref, st_ref, dy_ref, dp_ref, dlb_ref, dng_ref, dst_ref):
        a_mat = a_ref[...]
        ngv = ng_ref[...]
        ng4 = _tile_lanes(ngv, HG_HEADS)
        last_row = lax.broadcasted_iota(jnp.int32, (cs, hw), 0) == cs - 1
        si = pl.program_id(1)
        first = jnp.logical_and(pl.program_id(0) == 0, si == 0)
        heads = range(HG_HEADS)
        hl = HG_HEAD_LANES

        @pl.when(si == 0)
        def _():
            dst_ref[...] = jnp.zeros_like(dst_ref)

        def side_by_side(parts):
            return jnp.concatenate(parts, axis=1)

        def chunk(i, carry):
            dlb_acc, dng_acc = carry
            c = nct - 1 - i
            rows = pl.ds(pl.multiple_of(c * cs, cs), cs)
            q, v, gl, lb, sig, f, k, g = _hg_gates(p_ref[rows, :], lb_ref[...])
            o = o_ref[rows, :]
            dyv = dy_ref[rows, :]
            sts = [st_ref[h, c] for h in heads]
            dsts = [dst_ref[h] for h in heads]
            e_all = _split_dot(a_mat, g, NN, 3)
            b = e_all[0:cs]
            eb = jnp.exp(b)
            bl = b[cs - 1:cs, :]
            ebl = jnp.exp(bl)
            ekd = jnp.exp(bl - b)
            qb, kd = q * eb, k * ekd
            sg = jax.nn.sigmoid(gl)
            silu = gl * sg
            r = lax.rsqrt(_per_head(_lane_mean, o * o) + EPS)
            dgl = dyv * (o * r * ng4) * (sg * (1.0 + gl * (1.0 - sg)))
            u = dyv * silu * ng4
            do = r * u - o * (r * r * r) * _per_head(_lane_mean, u * o)
            dng4 = jnp.sum(dyv * silu * o * r, axis=0, keepdims=True)
            dng_acc = dng_acc + ((dng4[:, hl[0]] + dng4[:, hl[1]]) + (dng4[:, hl[2]] + dng4[:, hl[3]]))
            es, qm, km = [], [], []
            p = [jnp.zeros((cs, cs), F32) for _ in heads]
            for li in range(nl):
                e = jnp.exp(e_all[cs * (li + 1):cs * (li + 2)])
                es.append(e)
                qm.append(q * e)
                km.append(k * e)
                mk = m_ref[li]
                p = [p[h] + mk * _dot(qm[li][:, hl[h]], km[li][:, hl[h]], NT) for h in heads]
            dp = [_dot(do[:, hl[h]], v[:, hl[h]], NT) for h in heads]
            dv_p = [_dot(p[h], do[:, hl[h]], TN) for h in heads]
            dv_s = [_dot(kd[:, hl[h]], dsts[h], NT) for h in heads]
            dqb = side_by_side([_dot(do[:, hl[h]], sts[h]) for h in heads])
            dkd = side_by_side([_dot(v[:, hl[h]], dsts[h]) for h in heads])
            new_dst = [_dot(do[:, hl[h]], qb[:, hl[h]], TN) for h in heads]
            dv = side_by_side([dv_p[h] + dv_s[h] for h in heads]) + _per_head(_lane_sum, q * k) * do
            dq = dqb * eb
            dk = dkd * ekd
            db = dqb * qb - dkd * kd
            dbl = (jnp.sum(dkd * kd, axis=0, keepdims=True)
                   + side_by_side([jnp.sum(dsts[h] * sts[h], axis=0, keepdims=True) for h in heads]) * ebl)
            de = [db + jnp.where(last_row, dbl, 0.0)]
            for li in range(nl):
                mk = m_ref[li]
                dpm = [mk * dp[h] for h in heads]
                dqm = side_by_side([_dot(dpm[h], km[li][:, hl[h]]) for h in heads])
                dkm = side_by_side([_dot(dpm[h], qm[li][:, hl[h]], TN) for h in heads])
                dq = dq + dqm * es[li]
                dk = dk + dkm * es[li]
                de.append(dqm * qm[li] + dkm * km[li])
            dpd = _per_head(_lane_sum, do * v)
            dq = dq + dpd * k
            dk = dk + dpd * q
            dg = _split_dot(a_mat, jnp.concatenate(de, axis=0), TN, 2)
            df = dg / f - dk
            dp_ref[rows, 0:hw] = dq
            dp_ref[rows, hw:2 * hw] = df * (1.0 - lb) * sig * (1.0 - sig)
            dp_ref[rows, 2 * hw:3 * hw] = dv
            dp_ref[rows, 3 * hw:4 * hw] = dgl
            for h in heads:
                dst_ref[h] = dsts[h] * ebl[:, hl[h]] + new_dst[h]
            return dlb_acc + jnp.sum(df * (1.0 - sig), axis=0, keepdims=True), dng_acc

        dlb, dng = lax.fori_loop(0, nct, chunk, (jnp.zeros((1, hw), F32), jnp.zeros((1, HG_D), F32)))

        @pl.when(first)
        def _():
            dlb_ref[...] = jnp.zeros_like(dlb_ref)
            dng_ref[...] = jnp.zeros_like(dng_ref)

        lbp_v = lb_ref[...]
        mx = jnp.max(lbp_v, axis=0, keepdims=True)
        e = jnp.exp(lbp_v - mx)
        s0 = e[0:1, :] / jnp.sum(e, axis=0, keepdims=True)
        da0 = dlb * s0 * (1.0 - s0)
        dlb_ref[...] += jnp.concatenate([da0, -da0], axis=0)
        dng_ref[...] += dng

    def tile(b, s):
        return b * ns + (ns - 1 - s)

    return pl.pallas_call(
        body, name="hgrn2_bwd", grid=(bsz, ns),
        in_specs=[pl.BlockSpec((ts, HG_COLS), lambda b, s: (tile(b, s), 0)),
                  pl.BlockSpec((2, hw), lambda b, s: (0, 0)),
                  pl.BlockSpec((1, HG_D), lambda b, s: (0, 0)),
                  pl.BlockSpec(a_all.shape, lambda b, s: (0, 0)),
                  pl.BlockSpec(masks.shape, lambda b, s: (0, 0, 0)),
                  pl.BlockSpec((ts, hw), lambda b, s: (tile(b, s), 0)),
                  pl.BlockSpec((None, HG_HEADS, nct, HG_D, HG_D), lambda b, s: (b, 0, ns - 1 - s, 0, 0)),
                  pl.BlockSpec((ts, hw), lambda b, s: (tile(b, s), 0))],
        out_specs=(pl.BlockSpec((ts, HG_COLS), lambda b, s: (tile(b, s), 0)),
                   pl.BlockSpec((2, hw), lambda b, s: (0, 0)),
                   pl.BlockSpec((1, HG_D), lambda b, s: (0, 0))),
        out_shape=(jax.ShapeDtypeStruct((t, HG_COLS), F32),
                   jax.ShapeDtypeStruct((2, hw), F32),
                   jax.ShapeDtypeStruct((1, HG_D), F32)),
        scratch_shapes=[pltpu.VMEM((HG_HEADS, HG_D, HG_D), F32)],
        compiler_params=_params(("arbitrary", "arbitrary")),
    )(proj, lbp, ng, a_all, masks, o_all, states, dy)


def _sw_constants():
    half = ROT_DIM // 2
    inv = (np.float32(ROPE_THETA) ** (-(np.arange(half, dtype=np.float32) * np.float32(2.0) / np.float32(ROT_DIM)))
           ).astype(np.float32)
    freq = np.zeros((1, 128), np.float32)
    sign = np.zeros((1, 128), np.float32)
    for h in range(2):
        freq[0, 64 * h:64 * h + half] = inv
        freq[0, 64 * h + half:64 * h + 2 * half] = inv
        sign[0, 64 * h:64 * h + half] = -1.0
        sign[0, 64 * h + half:64 * h + 2 * half] = 1.0
    seg = np.kron(np.eye(8, dtype=np.float32), np.full((64, 64), 1.0 / 64.0, np.float32))
    return freq, sign, seg


def _rope_tables(pos, freq, sign):
    ang = pos.astype(F32) * freq
    return jnp.cos(ang), jnp.sin(ang) * sign


def _tile_lanes(v, times):
    return v if times == 1 else jnp.concatenate([v] * times, axis=1)


def _swap_halves(v):
    w = v.shape[1]
    half = ROT_DIM // 2
    lane = lax.broadcasted_iota(jnp.int32, v.shape, 1) % SW_HD
    return jnp.where(lane < half, pltpu.roll(v, w - half, 1), jnp.where(lane < 2 * half, pltpu.roll(v, half, 1), 0.0))


def _sw_norm_rope(tv, gain, seg, cosv, sinv):
    w = tv.shape[1]
    ms = _split_dot_rhs(tv * tv, seg[0:w, 0:w])
    r = lax.rsqrt(ms + EPS)
    tn = tv * r * gain
    reps = w // 128
    return tn * _tile_lanes(cosv, reps) + _swap_halves(tn) * _tile_lanes(sinv, reps), r


def _split_dot_rhs(v, a):
    hi = _mx(v)
    lo = _mx(v - hi.astype(F32))
    return (lax.dot_general(hi, a, (NN, ((), ())), preferred_element_type=F32)
            + lax.dot_general(lo, a, (NN, ((), ())), preferred_element_type=F32))


def _sw_norm_rope_bwd(dt, tv, r, gain, seg, cosv, sinv):
    w = tv.shape[1]
    reps = w // 128
    dtn = dt * _tile_lanes(cosv, reps) + _swap_halves(dt * _tile_lanes(sinv, reps))
    u = dtn * gain
    dtv = r * u - tv * (r * r * r) * _split_dot_rhs(u * tv, seg[0:w, 0:w])
    return dtv, jnp.sum(dtn * tv * r, axis=0, keepdims=True)


def _sw_scores(qh, kp, kc):
    return _dot(qh, kp, NT), _dot(qh, kc, NT)


def _sw_probs(raw, sink, first_block):
    scale = SW_HD ** -0.5
    qi = lax.broadcasted_iota(jnp.int32, (SW_BLOCK, SW_BLOCK), 0)
    kj = lax.broadcasted_iota(jnp.int32, (SW_BLOCK, SW_BLOCK), 1)
    ok_prev = jnp.logical_and(kj > qi, jnp.logical_not(first_block))
    ok_cur = kj <= qi
    sp = jnp.where(ok_prev, raw[0] * scale, -jnp.inf)
    sc = jnp.where(ok_cur, raw[1] * scale, -jnp.inf)
    m = jnp.maximum(jnp.maximum(jnp.max(sp, axis=1, keepdims=True), jnp.max(sc, axis=1, keepdims=True)), sink)
    pp, pc = jnp.exp(sp - m), jnp.exp(sc - m)
    es = jnp.exp(sink - m)
    den = jnp.sum(pp, axis=1, keepdims=True) + jnp.sum(pc, axis=1, keepdims=True) + es
    return pp / den, pc / den, es / den


def _sw_specs(nb):
    def cur(b, n):
        return b * nb + jnp.minimum(n, nb - 1)

    def prev(b, n):
        return b * nb + jnp.maximum(jnp.minimum(n, nb - 1) - 1, 0)

    return cur, prev


def _sw_fwd(proj, pos, qg, kg, sinks, y_in, bsz, seq):
    t = proj.shape[0]
    nb = seq // SW_BLOCK
    freq_np, sign_np, seg_np = _sw_constants()
    freq, sign = jnp.asarray(freq_np), jnp.asarray(sign_np)
    seg = jnp.asarray(seg_np, _MXU_DTYPE)
    cur, prev = _sw_specs(nb)

    def body(q_ref, kc_ref, kp_ref, vc_ref, vp_ref, pc_ref, pp_ref, qg_ref, kg_ref, sk_ref, fr_ref, sn_ref, seg_ref,
             yin_ref, y_ref):
        del yin_ref
        n = pl.program_id(1)
        segv = seg_ref[...]
        cos_c, sin_c = _rope_tables(pc_ref[...], fr_ref[...], sn_ref[...])
        cos_p, sin_p = _rope_tables(pp_ref[...], fr_ref[...], sn_ref[...])
        qr, _ = _sw_norm_rope(q_ref[...], qg_ref[...], segv, cos_c, sin_c)
        kcr, _ = _sw_norm_rope(kc_ref[...], kg_ref[...], segv, cos_c, sin_c)
        kpr, _ = _sw_norm_rope(kp_ref[...], kg_ref[...], segv, cos_p, sin_p)
        vc, vp = vc_ref[...], vp_ref[...]
        ks = [slice(SW_HD * (h // SW_GROUP), SW_HD * (h // SW_GROUP + 1)) for h in range(SW_HEADS)]
        raw = [_sw_scores(qr[:, SW_HD * h:SW_HD * (h + 1)], kpr[:, ks[h]], kcr[:, ks[h]]) for h in range(SW_HEADS)]
        probs = [_sw_probs(raw[h], sk_ref[0, h], n == 0) for h in range(SW_HEADS)]
        for h in range(SW_HEADS):
            y_ref[:, SW_HD * h:SW_HD * (h + 1)] = _dot(probs[h][0], vp[:, ks[h]]) + _dot(probs[h][1], vc[:, ks[h]])

    rowq = pl.BlockSpec((SW_BLOCK, 512), lambda b, n: (cur(b, n), 0))
    full = lambda a: pl.BlockSpec(a.shape, lambda b, n: (0,) * a.ndim)
    yw = y_in.shape[1]
    return pl.pallas_call(
        body, name="swa_fwd", grid=(bsz, nb),
        in_specs=[rowq,
                  pl.BlockSpec((SW_BLOCK, 128), lambda b, n: (cur(b, n), 4)),
                  pl.BlockSpec((SW_BLOCK, 128), lambda b, n: (prev(b, n), 4)),
                  pl.BlockSpec((SW_BLOCK, 128), lambda b, n: (cur(b, n), 5)),
                  pl.BlockSpec((SW_BLOCK, 128), lambda b, n: (prev(b, n), 5)),
                  pl.BlockSpec((SW_BLOCK, 1), lambda b, n: (cur(b, n), 0)),
                  pl.BlockSpec((SW_BLOCK, 1), lambda b, n: (prev(b, n), 0)),
                  full(qg), full(kg),
                  pl.BlockSpec(memory_space=pltpu.SMEM),
                  full(freq), full(sign), full(seg),
                  pl.BlockSpec(memory_space=pl.ANY)],
        out_specs=pl.BlockSpec((SW_BLOCK, 512), lambda b, n: (cur(b, n), 1)),
        out_shape=jax.ShapeDtypeStruct((t, yw), F32),
        input_output_aliases={13: 0},
        compiler_params=_params(("parallel", "parallel")),
    )(proj, proj, proj, proj, proj, pos, pos, qg, kg, sinks, freq, sign, seg, y_in)


def _sw_bwd(proj, pos, qg, kg, sinks, y, dy, bsz, seq):
    t = proj.shape[0]
    nb = seq // SW_BLOCK
    freq_np, sign_np, seg_np = _sw_constants()
    freq, sign = jnp.asarray(freq_np), jnp.asarray(sign_np)
    seg = jnp.asarray(seg_np, _MXU_DTYPE)
    cur, prev = _sw_specs(nb)
    scale = SW_HD ** -0.5

    def body(q_ref, kc_ref, kp_ref, vc_ref, vp_ref, pc_ref, pp_ref, qg_ref, kg_ref, sk_ref, fr_ref, sn_ref, seg_ref,
             y_ref, dy_ref, dp_ref, dqg_ref, dkg_ref, dsk_ref,
             dq_car, dkv_car, dqr_s, dkc_s, dkp_s, dvc_s, dvp_s, gq_acc, gk_acc, sk_acc):
        b, n = pl.program_id(0), pl.program_id(1)
        first = jnp.logical_and(b == 0, n == 0)
        last = jnp.logical_and(b == pl.num_programs(0) - 1, n == nb)

        @pl.when(first)
        def _():
            gq_acc[...] = jnp.zeros_like(gq_acc)
            gk_acc[...] = jnp.zeros_like(gk_acc)
            sk_acc[...] = jnp.zeros_like(sk_acc)

        @pl.when(n < nb)
        def _():
            segv = seg_ref[...]
            cos_c, sin_c = _rope_tables(pc_ref[...], fr_ref[...], sn_ref[...])
            cos_p, sin_p = _rope_tables(pp_ref[...], fr_ref[...], sn_ref[...])
            qv, kcv, kpv = q_ref[...], kc_ref[...], kp_ref[...]
            qr, rq = _sw_norm_rope(qv, qg_ref[...], segv, cos_c, sin_c)
            kcr, rkc = _sw_norm_rope(kcv, kg_ref[...], segv, cos_c, sin_c)
            kpr, rkp = _sw_norm_rope(kpv, kg_ref[...], segv, cos_p, sin_p)
            vc, vp = vc_ref[...], vp_ref[...]
            lane = lax.broadcasted_iota(jnp.int32, (1, 128), 1)
            dsk = jnp.zeros((1, 128), F32)
            heads = range(SW_HEADS)
            ks = [slice(SW_HD * (h // SW_GROUP), SW_HD * (h // SW_GROUP + 1)) for h in heads]
            hs = [slice(SW_HD * h, SW_HD * (h + 1)) for h in heads]
            qh = [qr[:, hs[h]] for h in heads]
            doh = [dy_ref[:, hs[h]] for h in heads]
            raw = [_sw_scores(qh[h], kpr[:, ks[h]], kcr[:, ks[h]]) for h in heads]
            dpp = [_dot(doh[h], vp[:, ks[h]], NT) for h in heads]
            dpc = [_dot(doh[h], vc[:, ks[h]], NT) for h in heads]
            probs = [_sw_probs(raw[h], sk_ref[0, h], n == 0) for h in heads]
            dsp, dsc = [], []
            for h in heads:
                pp, pc, ps = probs[h]
                delta = jnp.sum(doh[h] * y_ref[:, hs[h]], axis=1, keepdims=True)
                dsp.append(pp * (dpp[h] - delta) * scale)
                dsc.append(pc * (dpc[h] - delta) * scale)
                dsk = dsk + jnp.where(lane == h, -jnp.sum(ps * delta), 0.0)
            for h in heads:
                dqr_s[:, hs[h]] = _dot(dsp[h], kpr[:, ks[h]]) + _dot(dsc[h], kcr[:, ks[h]])
            for kv in range(SW_KV_HEADS):
                group = range(SW_GROUP * kv, SW_GROUP * (kv + 1))
                kvs = slice(SW_HD * kv, SW_HD * (kv + 1))
                dvp_s[:, kvs] = sum(_dot(probs[h][0], doh[h], TN) for h in group)
                dvc_s[:, kvs] = sum(_dot(probs[h][1], doh[h], TN) for h in group)
                dkp_s[:, kvs] = sum(_dot(dsp[h], qh[h], TN) for h in group)
                dkc_s[:, kvs] = sum(_dot(dsc[h], qh[h], TN) for h in group)
            dq, gq = _sw_norm_rope_bwd(dqr_s[...], qv, rq, qg_ref[...], segv, cos_c, sin_c)
            dkc, gkc = _sw_norm_rope_bwd(dkc_s[...], kcv, rkc, kg_ref[...], segv, cos_c, sin_c)
            dkp, gkp = _sw_norm_rope_bwd(dkp_s[...], kpv, rkp, kg_ref[...], segv, cos_p, sin_p)
            gq_acc[...] += gq
            gk_acc[...] += gkc + gkp
            sk_acc[...] += dsk

            @pl.when(n > 0)
            def _():
                dp_ref[:, 0:512] = dq_car[...]
                dp_ref[:, 512:640] = dkv_car[:, 0:128] + dkp
                dp_ref[:, 640:768] = dkv_car[:, 128:256] + dvp_s[...]

            dq_car[...] = dq
            dkv_car[:, 0:128] = dkc
            dkv_car[:, 128:256] = dvc_s[...]

        @pl.when(n == nb)
        def _():
            dp_ref[:, 0:512] = dq_car[...]
            dp_ref[:, 512:768] = dkv_car[...]

        @pl.when(last)
        def _():
            gq = gq_acc[...]
            acc = gq[:, 0:SW_HD]
            for h in range(1, SW_HEADS):
                acc = acc + gq[:, SW_HD * h:SW_HD * (h + 1)]
            dqg_ref[...] = acc
            gk = gk_acc[...]
            dkg_ref[...] = gk[:, 0:SW_HD] + gk[:, SW_HD:2 * SW_HD]
            dsk_ref[...] = sk_acc[...]

    rowq = pl.BlockSpec((SW_BLOCK, 512), lambda b, n: (cur(b, n), 0))
    full = lambda a: pl.BlockSpec(a.shape, lambda b, n: (0,) * a.ndim)

    def out_row(b, n):
        return b * nb + jnp.maximum(n - 1, 0)

    return pl.pallas_call(
        body, name="swa_bwd", grid=(bsz, nb + 1),
        in_specs=[rowq,
                  pl.BlockSpec((SW_BLOCK, 128), lambda b, n: (cur(b, n), 4)),
                  pl.BlockSpec((SW_BLOCK, 128), lambda b, n: (prev(b, n), 4)),
                  pl.BlockSpec((SW_BLOCK, 128), lambda b, n: (cur(b, n), 5)),
                  pl.BlockSpec((SW_BLOCK, 128), lambda b, n: (prev(b, n), 5)),
                  pl.BlockSpec((SW_BLOCK, 1), lambda b, n: (cur(b, n), 0)),
                  pl.BlockSpec((SW_BLOCK, 1), lambda b, n: (prev(b, n), 0)),
                  full(qg), full(kg),
                  pl.BlockSpec(memory_space=pltpu.SMEM),
                  full(freq), full(sign), full(seg),
                  pl.BlockSpec((SW_BLOCK, 512), lambda b, n: (cur(b, n), 1)),
                  pl.BlockSpec((SW_BLOCK, 512), lambda b, n: (cur(b, n), 1))],
        out_specs=(pl.BlockSpec((SW_BLOCK, SW_COLS), lambda b, n: (out_row(b, n), 0)),
                   pl.BlockSpec((1, SW_HD), lambda b, n: (0, 0)),
                   pl.BlockSpec((1, SW_HD), lambda b, n: (0, 0)),
                   pl.BlockSpec((1, 128), lambda b, n: (0, 0))),
        out_shape=(jax.ShapeDtypeStruct((t, SW_COLS), F32),
                   jax.ShapeDtypeStruct((1, SW_HD), F32),
                   jax.ShapeDtypeStruct((1, SW_HD), F32),
                   jax.ShapeDtypeStruct((1, 128), F32)),
        scratch_shapes=[pltpu.VMEM((SW_BLOCK, 512), F32), pltpu.VMEM((SW_BLOCK, 256), F32),
                        pltpu.VMEM((SW_BLOCK, 512), F32),
                        pltpu.VMEM((SW_BLOCK, 128), F32), pltpu.VMEM((SW_BLOCK, 128), F32),
                        pltpu.VMEM((SW_BLOCK, 128), F32), pltpu.VMEM((SW_BLOCK, 128), F32),
                        pltpu.VMEM((1, 512), F32), pltpu.VMEM((1, 128), F32), pltpu.VMEM((1, 128), F32)],
        compiler_params=_params(("arbitrary", "arbitrary")),
    )(proj, proj, proj, proj, proj, pos, pos, qg, kg, sinks, freq, sign, seg, y, dy)


def _head_rms(tv, gain):
    r = lax.rsqrt(jnp.mean(tv * tv, axis=1, keepdims=True) + EPS)
    return tv * r * gain, r


def _head_rms_bwd(dtn, tv, r, gain):
    u = dtn * gain
    return r * u - tv * (r * r * r) * jnp.mean(u * tv, axis=1, keepdims=True), jnp.sum(dtn * tv * r, axis=0, keepdims=True)


def _xa_probs(qn, kn):
    s = _dot(qn, kn, NT) * (XA_HD ** -0.5)
    e = jnp.exp(s - jnp.max(s, axis=1, keepdims=True))
    return e / jnp.sum(e, axis=1, keepdims=True)


def _xa_fwd(qx, kvx, qg, kg, bsz, seq, mlen, *, tq=512):
    t = qx.shape[0]
    tq = min(tq, seq)
    nq = seq // tq
    w = XA_HEADS * XA_HD

    def body(q_ref, kv_ref, qg_ref, kg_ref, o_ref):
        for h in range(XA_HEADS):
            hs = slice(XA_HD * h, XA_HD * (h + 1))
            qn, _ = _head_rms(q_ref[:, hs], qg_ref[...])
            kn, _ = _head_rms(kv_ref[:, hs], kg_ref[...])
            o_ref[:, hs] = _dot(_xa_probs(qn, kn), kv_ref[:, w + XA_HD * h:w + XA_HD * (h + 1)])

    vec = pl.BlockSpec((1, XA_HD), lambda b, i: (0, 0))
    return pl.pallas_call(
        body, name="xattn_fwd", grid=(bsz, nq),
        in_specs=[pl.BlockSpec((tq, w), lambda b, i: (b * nq + i, 0)),
                  pl.BlockSpec((mlen, 2 * w), lambda b, i: (b, 0)), vec, vec],
        out_specs=pl.BlockSpec((tq, w), lambda b, i: (b * nq + i, 0)),
        out_shape=jax.ShapeDtypeStruct((t, w), F32),
        compiler_params=_params(("parallel", "parallel")),
    )(qx, kvx, qg, kg)


def _xa_bwd(qx, kvx, qg, kg, do, bsz, seq, mlen, *, tq=512):
    t = qx.shape[0]
    tq = min(tq, seq)
    nq = seq // tq
    w = XA_HEADS * XA_HD
    scale = XA_HD ** -0.5

    def body(q_ref, kv_ref, qg_ref, kg_ref, do_ref, dq_ref, dkv_ref, dqg_ref, dkg_ref):
        b, i = pl.program_id(0), pl.program_id(1)

        @pl.when(jnp.logical_and(b == 0, i == 0))
        def _():
            dqg_ref[...] = jnp.zeros_like(dqg_ref)
            dkg_ref[...] = jnp.zeros_like(dkg_ref)

        @pl.when(i == 0)
        def _():
            dkv_ref[...] = jnp.zeros_like(dkv_ref)

        gq_sum = jnp.zeros((1, XA_HD), F32)
        gk_sum = jnp.zeros((1, XA_HD), F32)
        for h in range(XA_HEADS):
            hs = slice(XA_HD * h, XA_HD * (h + 1))
            vs = slice(w + XA_HD * h, w + XA_HD * (h + 1))
            qv, kv, vv = q_ref[:, hs], kv_ref[:, hs], kv_ref[:, vs]
            qn, rq = _head_rms(qv, qg_ref[...])
            kn, rk = _head_rms(kv, kg_ref[...])
            p = _xa_probs(qn, kn)
            doh = do_ref[:, hs]
            dp = _dot(doh, vv, NT)
            ds = p * (dp - jnp.sum(p * dp, axis=1, keepdims=True)) * scale
            dqv, gq = _head_rms_bwd(_dot(ds, kn), qv, rq, qg_ref[...])
            dkv, gk = _head_rms_bwd(_dot(ds, qn, TN), kv, rk, kg_ref[...])
            dq_ref[:, hs] = dqv
            dkv_ref[:, hs] += dkv
            dkv_ref[:, vs] += _dot(p, doh, TN)
            gq_sum = gq_sum + gq
            gk_sum = gk_sum + gk
        dqg_ref[...] += gq_sum
        dkg_ref[...] += gk_sum

    vec = pl.BlockSpec((1, XA_HD), lambda b, i: (0, 0))
    row = pl.BlockSpec((tq, w), lambda b, i: (b * nq + i, 0))
    mem = pl.BlockSpec((mlen, 2 * w), lambda b, i: (b, 0))
    return pl.pallas_call(
        body, name="xattn_bwd", grid=(bsz, nq),
        in_specs=[row, mem, vec, vec, row],
        out_specs=(row, mem, vec, vec),
        out_shape=(jax.ShapeDtypeStruct((t, w), F32), jax.ShapeDtypeStruct((bsz * mlen, 2 * w), F32),
                   jax.ShapeDtypeStruct((1, XA_HD), F32), jax.ShapeDtypeStruct((1, XA_HD), F32)),
        compiler_params=_params(("arbitrary", "arbitrary")),
    )(qx, kvx, qg, kg, do)


def _loss_sum(dy, d_model, *, tm=512):
    t, d = dy.shape
    tm = min(tm, t)
    steps = t // tm

    def body(dy_ref, o_ref, acc_ref):
        i = pl.program_id(0)

        @pl.when(i == 0)
        def _():
            acc_ref[...] = jnp.zeros_like(acc_ref)

        diff = dy_ref[...] * float(d_model)
        acc_ref[...] += jnp.sum(diff * diff, axis=0, keepdims=True)

        @pl.when(i == steps - 1)
        def _():
            o_ref[...] = jnp.zeros_like(o_ref) + 0.5 * jnp.sum(acc_ref[...]) / float(d_model)

    return pl.pallas_call(
        body, name="loss_sum", grid=(steps,),
        in_specs=[pl.BlockSpec((tm, d), lambda i: (i, 0))],
        out_specs=pl.BlockSpec((1, 128), lambda i: (0, 0)),
        out_shape=jax.ShapeDtypeStruct((1, 128), F32),
        scratch_shapes=[pltpu.VMEM((1, d), F32)],
        compiler_params=_params(("arbitrary",)),
    )(dy)


def _adamw_math(w, g, m, v):
    m = ADAM_B1 * m + (1.0 - ADAM_B1) * g
    v = ADAM_B2 * v + (1.0 - ADAM_B2) * (g * g)
    m_hat = m / (1.0 - ADAM_B1 ** ADAM_STEP)
    v_hat = v / (1.0 - ADAM_B2 ** ADAM_STEP)
    return -ADAM_LR * (m_hat / (jnp.sqrt(v_hat) + ADAM_EPS) + ADAM_WD * w), m, v


def _adamw_big(w, g, m, v, *, name, tr=256):
    r, c = w.shape
    tr = min(tr, r)

    def body(w_ref, g_ref, m_ref, v_ref, d_ref, mo_ref, vo_ref):
        d, mn, vn = _adamw_math(w_ref[...], g_ref[...], m_ref[...], v_ref[...])
        d_ref[...] = d
        mo_ref[...] = mn
        vo_ref[...] = vn

    spec = pl.BlockSpec((tr, c), lambda i: (i, 0))
    shp = jax.ShapeDtypeStruct((r, c), F32)
    return pl.pallas_call(
        body, name=name, grid=(r // tr,), in_specs=[spec] * 4, out_specs=(spec,) * 3, out_shape=(shp,) * 3,
        compiler_params=_params(("parallel",)),
    )(w, g, m, v)


def _adamw_small(ws, gs, ms, vs):
    n = len(ws)

    def body(*refs):
        for i in range(n):
            d, mn, vn = _adamw_math(refs[i][...], refs[n + i][...], refs[2 * n + i][...], refs[3 * n + i][...])
            refs[4 * n + i][...] = d
            refs[5 * n + i][...] = mn
            refs[6 * n + i][...] = vn

    shapes = tuple(jax.ShapeDtypeStruct(w.shape, F32) for w in ws)
    return pl.pallas_call(body, name="adamw_small", out_shape=shapes * 3)(*ws, *gs, *ms, *vs)


def _add_halves(g, recv, c_idx, *, name, tr=256):
    _, r, c = g.shape
    h = r // 2
    tr = min(tr, h)
    nt = h // tr

    def body(c_ref, g_ref, r_ref, o_ref):
        del c_ref
        o_ref[...] = g_ref[...] + r_ref[...]

    return pl.pallas_call(
        body, name=name,
        grid_spec=pltpu.PrefetchScalarGridSpec(
            num_scalar_prefetch=1, grid=(4, nt),
            in_specs=[pl.BlockSpec((None, tr, c), lambda k, i, cr: (k, cr[0] * nt + i, 0)),
                      pl.BlockSpec((None, tr, c), lambda k, i, cr: (k, i, 0))],
            out_specs=pl.BlockSpec((None, tr, c), lambda k, i, cr: (k, i, 0))),
        out_shape=jax.ShapeDtypeStruct((4, h, c), F32),
        compiler_params=_params(("parallel", "parallel")),
    )(c_idx, g, recv)


def _add_chips(p, recv, place_idx, *, name, tr=256):
    _, h, c = p.shape
    tr = min(tr, h)
    nt = h // tr

    def body(pi_ref, p_ref, r_ref, o_ref):
        del pi_ref
        o_ref[...] = ((p_ref[...] + r_ref[0]) + r_ref[1]) + r_ref[2]

    return pl.pallas_call(
        body, name=name,
        grid_spec=pltpu.PrefetchScalarGridSpec(
            num_scalar_prefetch=1, grid=(nt,),
            in_specs=[pl.BlockSpec((None, tr, c), lambda i, pi: (pi[0], i, 0)),
                      pl.BlockSpec((3, tr, c), lambda i, pi: (0, i, 0))],
            out_specs=pl.BlockSpec((tr, c), lambda i, pi: (pi[1] * nt + i, 0))),
        out_shape=jax.ShapeDtypeStruct((2 * h, c), F32),
        compiler_params=_params(("parallel",)),
    )(place_idx, p, recv)


def _place_shard(shard, place_idx, *, name, tr=256):
    r, c = shard.shape
    tr = min(tr, r)

    def body(pi_ref, s_ref, o_ref):
        del pi_ref
        o_ref[...] = s_ref[...]

    return pl.pallas_call(
        body, name=name,
        grid_spec=pltpu.PrefetchScalarGridSpec(
            num_scalar_prefetch=1, grid=(r // tr,),
            in_specs=[pl.BlockSpec((tr, c), lambda i, pi: (i, 0))],
            out_specs=pl.BlockSpec((None, tr, c), lambda i, pi: (pi[0], i, 0))),
        out_shape=jax.ShapeDtypeStruct((4, r, c), shard.dtype),
        compiler_params=_params(("parallel",)),
    )(place_idx, shard)


def _place():
    x, y, c = lax.axis_index("x"), lax.axis_index("y"), lax.axis_index("c")
    chips = [(1 - x, y), (x, 1 - y), (1 - x, 1 - y)]
    return x, y, c, chips


ANY = pl.BlockSpec(memory_space=pl.ANY)


def _all_gather_weights(shards, placed):
    n = len(shards)

    def body(*refs):
        ins, outs = refs[:n], refs[2 * n:3 * n]
        send_sems, recv_sems = refs[3 * n:]
        x, y, c, chips = _place()
        me = 2 * x + y

        def half(a, chip_idx, which):
            h = ins[a].shape[0] // 2
            return outs[a].at[chip_idx, pl.ds(which * h, h), :]

        def copy(a, j, chip_idx, which, to, src=None):
            return pltpu.make_async_remote_copy(
                src_ref=half(a, chip_idx, which) if src is None else src, dst_ref=half(a, chip_idx, which),
                send_sem=send_sems.at[a * 6 + j], recv_sem=recv_sems.at[a * 6 + j], device_id=to, device_id_type=MESH)

        for a in range(n):
            h = ins[a].shape[0] // 2
            for j, (px, py) in enumerate(chips):
                copy(a, j, me, c, (px, py, c), src=ins[a].at[pl.ds(c * h, h), :]).start()
        for a in range(n):
            for j, (px, py) in enumerate(chips):
                copy(a, j, 2 * px + py, c, (x, y, c)).wait_recv()
                copy(a, 3 + j, 2 * px + py, c, (x, y, 1 - c)).start()
        for a in range(n):
            for j, (px, py) in enumerate(chips):
                copy(a, 3 + j, 2 * px + py, 1 - c, (x, y, c)).wait_recv()
        for a in range(n):
            h = ins[a].shape[0] // 2
            for j, (px, py) in enumerate(chips):
                copy(a, j, me, c, (px, py, c), src=ins[a].at[pl.ds(c * h, h), :]).wait_send()
                copy(a, 3 + j, 2 * px + py, c, (x, y, 1 - c)).wait_send()

    return pl.pallas_call(
        body, name="all_gather_weights",
        in_specs=[ANY] * (2 * n), out_specs=tuple([ANY] * n),
        out_shape=tuple(jax.ShapeDtypeStruct(p.shape, p.dtype) for p in placed),
        input_output_aliases={n + i: i for i in range(n)},
        scratch_shapes=[pltpu.SemaphoreType.DMA((6 * n,)), pltpu.SemaphoreType.DMA((6 * n,))],
    )(*shards, *placed)


def _exchange_halves(grads, name):
    n = len(grads)

    def body(*refs):
        ins, outs = refs[:n], refs[n:2 * n]
        send_sems, recv_sems = refs[2 * n:]
        x, y, c, _ = _place()

        def copy(a):
            h = ins[a].shape[1] // 2
            return pltpu.make_async_remote_copy(
                src_ref=ins[a].at[:, pl.ds((1 - c) * h, h), :], dst_ref=outs[a],
                send_sem=send_sems.at[a], recv_sem=recv_sems.at[a], device_id=(x, y, 1 - c), device_id_type=MESH)

        for a in range(n):
            copy(a).start()
        for a in range(n):
            copy(a).wait_recv()
        for a in range(n):
            copy(a).wait_send()

    return pl.pallas_call(
        body, name=name,
        in_specs=[ANY] * n, out_specs=tuple([ANY] * n),
        out_shape=tuple(jax.ShapeDtypeStruct((4, g.shape[1] // 2, g.shape[2]), g.dtype) for g in grads),
        scratch_shapes=[pltpu.SemaphoreType.DMA((n,)), pltpu.SemaphoreType.DMA((n,))],
    )(*grads)


HBM = pl.BlockSpec(memory_space=pltpu.HBM)
SEM = pl.BlockSpec(memory_space=pltpu.SEMAPHORE)
EFFECT = pltpu.SideEffectType.DATAFLOW_SIDE_EFFECTING


def _in_hbm(a):
    return pltpu.with_memory_space_constraint(a, pltpu.HBM)


def _split_copy_calls(name, srcs, lands, n_copies, make_copies):
    ns, nl = len(srcs), len(lands)
    nb = ns + nl

    def start():
        def body(*refs):
            copies = make_copies(refs[:ns], refs[ns:nb], refs[nb], refs[nb + 1])
            for cp in copies:
                cp.start()
            token = refs[-1]
            token[...] = jnp.zeros_like(token)

        bufs = [_in_hbm(a) for a in list(srcs) + list(lands)]
        out = pl.pallas_call(
            body, name=name + "_start",
            out_shape=(pltpu.SemaphoreType.DMA((n_copies,)), pltpu.SemaphoreType.DMA((n_copies,)),
                       *[pltpu.HBM(a.shape, a.dtype) for a in bufs], jax.ShapeDtypeStruct((8, 128), F32)),
            in_specs=[HBM] * nb, out_specs=(SEM, SEM, *[HBM] * nb, pl.BlockSpec(memory_space=pltpu.VMEM)),
            input_output_aliases={i: 2 + i for i in range(nb)},
            compiler_params=pltpu.CompilerParams(has_side_effects=EFFECT),
        )(*bufs)
        return dict(send=out[0], recv=out[1], bufs=list(out[2:2 + nb]), token=out[-1])

    def wait(state, after):
        def body(*refs):
            copies = make_copies(refs[:ns], refs[ns:nb], refs[nb], refs[nb + 1])
            for cp in copies:
                cp.wait_send()
            for cp in copies:
                cp.wait_recv()

        bufs = state["bufs"]
        out = pl.pallas_call(
            body, name=name + "_wait",
            out_shape=tuple(pltpu.HBM(a.shape, a.dtype) for a in bufs),
            in_specs=[HBM] * nb + [SEM, SEM, pl.BlockSpec(memory_space=pl.ANY)], out_specs=tuple([HBM] * nb),
            input_output_aliases={i: i for i in range(nb)},
            compiler_params=pltpu.CompilerParams(has_side_effects=EFFECT),
        )(*bufs, state["send"], state["recv"], after)
        return list(out[:ns]), list(out[ns:])

    return start, wait


def _scatter_chips_split(name, parts):
    n = len(parts)
    lands = [lax.empty((3,) + p.shape[1:], p.dtype) for p in parts]

    def make_copies(srcs, lnds, send_sems, recv_sems):
        _, _, c, chips = _place()
        return [pltpu.make_async_remote_copy(
            src_ref=srcs[a].at[2 * px + py], dst_ref=lnds[a].at[j], send_sem=send_sems.at[a * 3 + j],
            recv_sem=recv_sems.at[a * 3 + j], device_id=(px, py, c), device_id_type=MESH)
            for a in range(n) for j, (px, py) in enumerate(chips)]

    return _split_copy_calls(name, parts, lands, 3 * n, make_copies)


def _gather_chips_split(name, shards, lands):
    n = len(shards)

    def make_copies(srcs, lnds, send_sems, recv_sems):
        x, y, c, chips = _place()
        out = []
        for a in range(n):
            h = srcs[a].shape[0] // 2
            for j, (px, py) in enumerate(chips):
                out.append(pltpu.make_async_remote_copy(
                    src_ref=srcs[a].at[pl.ds(c * h, h), :], dst_ref=lnds[a].at[2 * x + y, pl.ds(c * h, h), :],
                    send_sem=send_sems.at[a * 3 + j], recv_sem=recv_sems.at[a * 3 + j],
                    device_id=(px, py, c), device_id_type=MESH))
        return out

    return _split_copy_calls(name, shards, lands, 3 * n, make_copies)


def _gather_finish(gathered):
    n = len(gathered)

    def body(*refs):
        outs = refs[n:2 * n]
        send_sems, recv_sems = refs[2 * n:]
        x, y, c, chips = _place()

        def copy(a, j, chip_idx, which):
            h = outs[a].shape[1] // 2
            rows = outs[a].at[chip_idx, pl.ds(which * h, h), :]
            return pltpu.make_async_remote_copy(
                src_ref=rows, dst_ref=rows, send_sem=send_sems.at[a * 3 + j], recv_sem=recv_sems.at[a * 3 + j],
                device_id=(x, y, 1 - c), device_id_type=MESH)

        for a in range(n):
            for j, (px, py) in enumerate(chips):
                copy(a, j, 2 * px + py, c).start()
        for a in range(n):
            for j, (px, py) in enumerate(chips):
                copy(a, j, 2 * px + py, 1 - c).wait_recv()
        for a in range(n):
            for j, (px, py) in enumerate(chips):
                copy(a, j, 2 * px + py, c).wait_send()

    return pl.pallas_call(
        body, name="gather_finish",
        in_specs=[ANY] * n, out_specs=tuple([ANY] * n),
        out_shape=tuple(jax.ShapeDtypeStruct(g.shape, g.dtype) for g in gathered),
        input_output_aliases={i: i for i in range(n)},
        scratch_shapes=[pltpu.SemaphoreType.DMA((3 * n,)), pltpu.SemaphoreType.DMA((3 * n,))],
    )(*gathered)


def _join_halves(fulls):
    n = len(fulls)

    def body(*refs):
        outs = refs[n:2 * n]
        send_sems, recv_sems = refs[2 * n:]
        x, y, c, _ = _place()

        def copy(a, which):
            h = outs[a].shape[0] // 2
            rows = outs[a].at[pl.ds(which * h, h), :]
            return pltpu.make_async_remote_copy(
                src_ref=rows, dst_ref=rows, send_sem=send_sems.at[a], recv_sem=recv_sems.at[a],
                device_id=(x, y, 1 - c), device_id_type=MESH)

        for a in range(n):
            copy(a, c).start()
        for a in range(n):
            copy(a, 1 - c).wait_recv()
        for a in range(n):
            copy(a, c).wait_send()

    return pl.pallas_call(
        body, name="rs_join_halves",
        in_specs=[ANY] * n, out_specs=tuple([ANY] * n),
        out_shape=tuple(jax.ShapeDtypeStruct(p.shape, p.dtype) for p in fulls),
        input_output_aliases={i: i for i in range(n)},
        scratch_shapes=[pltpu.SemaphoreType.DMA((n,)), pltpu.SemaphoreType.DMA((n,))],
    )(*fulls)


def _all_reduce_small(sm):
    r, w = sm.shape

    def body(sm_ref, o_ref, buf, send_sems, recv_sems):
        x, y, c, _ = _place()
        me = 4 * x + 2 * y + c
        buf[me] = sm_ref[...]
        rel = [(dx, dy, dc) for dx in (0, 1) for dy in (0, 1) for dc in (0, 1)][1:]

        def copy(k, slot, to):
            return pltpu.make_async_remote_copy(
                src_ref=sm_ref, dst_ref=buf.at[slot], send_sem=send_sems.at[k], recv_sem=recv_sems.at[k],
                device_id=to, device_id_type=MESH)

        peers = []
        for k, (dx, dy, dc) in enumerate(rel):
            px = 1 - x if dx else x
            py = 1 - y if dy else y
            pc = 1 - c if dc else c
            peers.append((px, py, pc))
            copy(k, me, (px, py, pc)).start()
        for k, (px, py, pc) in enumerate(peers):
            copy(k, 4 * px + 2 * py + pc, (px, py, pc)).wait_recv()
        for k, (px, py, pc) in enumerate(peers):
            copy(k, me, (px, py, pc)).wait_send()
        acc = buf[0]
        for d in range(1, 8):
            acc = acc + buf[d]
        o_ref[...] = acc

    vm = pl.BlockSpec(memory_space=pltpu.VMEM)
    return pl.pallas_call(
        body, name="all_reduce_small", in_specs=[vm], out_specs=vm,
        out_shape=jax.ShapeDtypeStruct((r, w), F32),
        scratch_shapes=[pltpu.VMEM((8, r, w), F32), pltpu.SemaphoreType.DMA((7,)), pltpu.SemaphoreType.DMA((7,))],
    )(sm)


class _LocalWeights:
    def __init__(self, w):
        self.w = w
        self.g = {}

    def first(self):
        return self.w

    def rest(self, after):
        del after
        return self.w

    def grads(self, tag, g):
        del tag
        self.g.update(g)
        return None


def _local_step(x3, mem3, pos2, target3, small, comm):
    bsz, seq, d = x3.shape
    mlen = mem3.shape[1]
    t = bsz * seq
    ds = d // 4
    w = comm.first()
    x = x3.reshape(t, d)
    mem = mem3.reshape(bsz * mlen, d)
    target = target3.reshape(t, d)
    pos = pos2.reshape(t, 1)
    qg_t = jnp.tile(small["sw_q_norm_g"], (1, SW_HEADS))
    kg_t = jnp.tile(small["sw_k_norm_g"], (1, SW_KV_HEADS))

    hn1 = _rms_fwd(x, small["norm1_g"], name="rms1_fwd")
    proj_hg = _mm(hn1, w["w_in_hg"], NN, t, HG_COLS, d, name="proj_hg", tk=d, after=(w.get("token"),))[0]
    proj_sw = _mm(hn1, w["w_in_sw"], NN, t, SW_COLS, d, name="proj_sw", tk=d)[0]
    y_mix, o_hg, states = _hg_fwd(proj_hg, small["hg_lower_bounds"], small["hg_norm_g"], bsz, seq, y_width=1024)
    y_mix = _sw_fwd(proj_sw, pos, qg_t, kg_t, small["sw_sinks"], y_mix, bsz, seq)
    w_in_hg, w_in_sw = w["w_in_hg"], w["w_in_sw"]
    w = comm.rest(y_mix)
    ff = w["down"].shape[0]
    ffs = ff // 4
    h1 = _mm(y_mix, w["w_out"], NN, t, d, 1024, name="out_proj", tk=1024, extras=(x,),
             epilogue=lambda acc, res: (acc + res,))[0]
    hn2 = _rms_fwd(h1, small["norm2_g"], name="rms2_fwd")
    mn = _rms_fwd(mem, small["mem_norm_g"], name="rms_mem_fwd")
    qx = _mm(hn2, w["wq"], NN, t, 512, d, name="xa_q", tk=d)[0]
    kvx = _mm(mn, w["wkv"], NN, bsz * mlen, 1024, d, name="xa_kv", tk=d)[0]
    ox = _xa_fwd(qx, kvx, small["xa_q_norm_g"], small["xa_k_norm_g"], bsz, seq, mlen)
    h2 = _mm(ox, w["wo"], NN, t, d, 512, name="xa_o", tk=512, extras=(h1,), epilogue=lambda acc, res: (acc + res,))[0]
    hn3 = _rms_fwd(h2, small["norm3_g"], name="rms3_fwd")

    def relu_sq(acc):
        a = jnp.maximum(acc, 0.0)
        return a, a * a

    act, act2 = _mm(hn3, w["up"], NN, t, ff, d, name="mlp_up", tn=ffs, tk=d,
                    b_spec=pl.BlockSpec((None, d, ffs), lambda i, j, kk: (j, 0, 0)),
                    epilogue=relu_sq, out_dtypes=(_MXU_DTYPE, _MXU_DTYPE))
    inv_d = 1.0 / d
    dy = _mm(act2, w["down"], NN, t, d, ff, name="mlp_down", extras=(h2, target),
             epilogue=lambda acc, res, tgt: ((acc + res - tgt) * inv_d,))[0]
    loss_row = _loss_sum(dy, d)

    dz = _mm(dy, w["down"], NT, t, ff, d, name="d_act", tk=d, extras=(act,),
             epilogue=lambda acc, a: (acc * (2.0 * a.astype(F32)),), out_dtypes=(_MXU_DTYPE,))[0]
    g_down = _mm(act2, dy, TN, ff, d, t, name="g_down")[0]
    g_up = _mm(hn3, dz, TN, d, ff, t, name="g_up", tn=ffs,
               out_shape=(jax.ShapeDtypeStruct((4, d, ffs), F32),),
               out_spec=(pl.BlockSpec((None, min(1024, d), ffs), lambda i, j, kk: (j, i, 0)),))[0]
    tok = comm.grads("mlp", dict(up=g_up, down=g_down))
    dhn3 = _mm(dz, w["up"], NT, t, d, ff, name="d_hn3", tk=ffs, after=(tok,),
               b_spec=pl.BlockSpec((None, min(1024, d), ffs), lambda i, j, kk: (kk, j, 0)))[0]
    dh2, g_norm3 = _rms_bwd(h2, small["norm3_g"], dhn3, dy, name="rms3_bwd")
    d_ox = _mm(dh2, w["wo"], NT, t, 512, d, name="d_ox", tk=d)[0]
    g_wo = _mm(ox, dh2, TN, 512, d, t, name="g_wo")[0]
    d_qx, d_kvx, g_xq, g_xk = _xa_bwd(qx, kvx, small["xa_q_norm_g"], small["xa_k_norm_g"], d_ox, bsz, seq, mlen)
    g_wq = _mm(hn2, d_qx, TN, d, 512, t, name="g_wq")[0]
    g_wkv = _mm(mn, d_kvx, TN, d, 1024, bsz * mlen, name="g_wkv")[0]
    dhn2 = _mm(d_qx, w["wq"], NT, t, d, 512, name="d_hn2", tk=512)[0]
    dmn = _mm(d_kvx, w["wkv"], NT, bsz * mlen, d, 1024, name="d_mn", tk=1024)[0]
    dh1, g_norm2 = _rms_bwd(h1, small["norm2_g"], dhn2, dh2, name="rms2_bwd")
    _, g_memn = _rms_bwd(mem, small["mem_norm_g"], dmn, None, name="rms_mem_bwd")
    g_wout = _mm(y_mix, dh1, TN, 1024, d, t, name="g_wout")[0]
    tok = comm.grads("mid", dict(w_out=g_wout, wq=g_wq, wkv=g_wkv, wo=g_wo))
    d_mix = _mm(dh1, w["w_out"], NT, t, 1024, d, name="d_mix", tk=d, after=(tok,))[0]
    dproj_sw, g_swq, g_swk, g_sinks = _sw_bwd(proj_sw, pos, qg_t, kg_t, small["sw_sinks"], y_mix, d_mix, bsz, seq)
    dproj_hg, g_lb, g_hgn = _hg_bwd(proj_hg, small["hg_lower_bounds"], small["hg_norm_g"], o_hg, states, d_mix, bsz, seq)
    g_in_hg = _mm(hn1, dproj_hg, TN, d, HG_COLS, t, name="g_in_hg")[0]
    g_in_sw = _mm(hn1, dproj_sw, TN, d, SW_COLS, t, name="g_in_sw")[0]
    tok = comm.grads("in", dict(w_in_hg=g_in_hg, w_in_sw=g_in_sw))
    dhn1_a = _mm(dproj_hg, w_in_hg, NT, t, d, HG_COLS, name="d_hn1_hg", tk=1024, after=(tok,))[0]
    dhn1 = _mm(dproj_sw, w_in_sw, NT, t, d, SW_COLS, name="d_hn1_sw", tk=SW_COLS, extras=(dhn1_a,),
               epilogue=lambda acc, prev: (acc + prev,))[0]
    grad_x, g_norm1 = _rms_bwd(x, small["norm1_g"], dhn1, dh1, name="rms1_bwd")

    g_small = dict(norm1_g=g_norm1, hg_lower_bounds=g_lb, hg_norm_g=g_hgn, sw_q_norm_g=g_swq, sw_k_norm_g=g_swk,
                   sw_sinks=g_sinks[:, 0:SW_HEADS], norm2_g=g_norm2, mem_norm_g=g_memn, xa_q_norm_g=g_xq,
                   xa_k_norm_g=g_xk, norm3_g=g_norm3)
    return loss_row, grad_x.reshape(bsz, seq, d), g_small


SMALL_NAMES = ("norm1_g", "hg_lower_bounds", "hg_norm_g", "sw_q_norm_g", "sw_k_norm_g", "sw_sinks", "norm2_g",
               "mem_norm_g", "xa_q_norm_g", "xa_k_norm_g", "norm3_g")
BIG_NAMES = ("w_in", "w_out", "xa_wq", "xa_wkv", "xa_wo", "mlp_up", "mlp_down")
WEIGHT_ORDER = ("norm1_g", "w_in", "hg_lower_bounds", "hg_norm_g", "sw_q_norm_g", "sw_k_norm_g", "sw_sinks", "w_out",
                "norm2_g", "mem_norm_g", "xa_wq", "xa_wkv", "xa_q_norm_g", "xa_k_norm_g", "xa_wo", "norm3_g",
                "mlp_up", "mlp_down")


def _pack_rows(vals, width):
    starts, at = [], 0
    for v in vals:
        starts.append(at)
        at += v.shape[0]
    total = at + (-at) % 8
    out = None
    for v, s in zip(vals, starts):
        placed = jnp.pad(v, ((s, total - s - v.shape[0]), (0, width - v.shape[1])))
        out = placed if out is None else out + placed
    return out, starts


class _MeshWeights:
    LATE = ("w_out", "xa_wq", "xa_wkv", "xa_wo", "mlp_up", "mlp_down")

    def __init__(self, shards, d, ff):
        self.shards, self.d, self.ff = shards, d, ff
        self.c_idx = lax.axis_index("c").astype(jnp.int32).reshape(1)
        chip = (2 * lax.axis_index("x") + lax.axis_index("y")).astype(jnp.int32)
        self.place_idx = jnp.stack([chip, lax.axis_index("c").astype(jnp.int32)])
        self.pending = []
        self.halves = {}

    def first(self):
        placed = {n: _place_shard(s, self.place_idx, name="place_" + n) for n, s in self.shards.items()}
        (g_in,) = _all_gather_weights([self.shards["w_in"]], [placed["w_in"]])
        start, self.late_wait = _gather_chips_split("gather_late", [self.shards[n] for n in self.LATE],
                                                    [placed[n] for n in self.LATE])
        self.late_state = start()
        full = jnp.concatenate([g_in[k] for k in range(4)], axis=1)
        return dict(w_in_hg=full[:, :HG_COLS], w_in_sw=full[:, HG_COLS:], token=self.late_state["token"])

    def rest(self, after):
        _, lands = self.late_wait(self.late_state, after)
        g_out, g_q, g_kv, g_o, g_up, g_dn = _gather_finish(lands)
        d = self.d
        return dict(w_out=g_out.reshape(-1, d), wq=g_q.reshape(d, -1), wkv=g_kv.reshape(d, -1),
                    wo=jnp.concatenate([g_o[k] for k in range(4)], axis=1), up=g_up, down=g_dn.reshape(self.ff, d))

    def _chip_partials(self, tag, names, arrays):
        recv = _exchange_halves(arrays, "rs_exchange_" + tag)
        return [_add_halves(g, r, self.c_idx, name="rs_add_halves_" + n) for n, g, r in zip(names, arrays, recv)]

    def grads(self, tag, g):
        d, ff = self.d, self.ff
        if tag == "mlp":
            names, arrays = ("mlp_up", "mlp_down"), [g["up"], g["down"].reshape(4, ff // 4, d)]
        elif tag == "mid":
            names = ("w_out", "xa_wq", "xa_wkv", "xa_wo")
            ds = d // 4
            g_wo = jnp.stack([g["wo"][:, ds * k:ds * (k + 1)] for k in range(4)])
            arrays = [g["w_out"].reshape(4, -1, d), g["wq"].reshape(4, d // 4, -1), g["wkv"].reshape(4, d // 4, -1), g_wo]
        else:
            full = jnp.concatenate([g["w_in_hg"], g["w_in_sw"]], axis=1)
            ws = full.shape[1] // 4
            names, arrays = ("w_in",), [jnp.stack([full[:, ws * k:ws * (k + 1)] for k in range(4)])]
        parts = self._chip_partials(tag, names, arrays)
        start, wait = _scatter_chips_split("rs_scatter_" + tag, parts)
        state = start()
        self.pending.append((names, wait, state))
        return state["token"]

    def finish(self, after):
        for names, wait, state in self.pending:
            srcs, lands = wait(state, after)
            for n, p, r in zip(names, srcs, lands):
                self.halves[n] = _add_chips(p, r, self.place_idx, name="rs_add_chips_" + n)
        return dict(zip(BIG_NAMES, _join_halves([self.halves[n] for n in BIG_NAMES])))


def kernel(x, mem, positions, norm1_g, w_in, hg_lower_bounds, hg_norm_g, sw_q_norm_g, sw_k_norm_g, sw_sinks, w_out, norm2_g, mem_norm_g, xa_wq, xa_wkv, xa_q_norm_g, xa_k_norm_g, xa_wo, norm3_g, mlp_up, mlp_down, loss_target, m_norm1_g, m_w_in, m_hg_lower_bounds, m_hg_norm_g, m_sw_q_norm_g, m_sw_k_norm_g, m_sw_sinks, m_w_out, m_norm2_g, m_mem_norm_g, m_xa_wq, m_xa_wkv, m_xa_q_norm_g, m_xa_k_norm_g, m_xa_wo, m_norm3_g, m_mlp_up, m_mlp_down, v_norm1_g, v_w_in, v_hg_lower_bounds, v_hg_norm_g, v_sw_q_norm_g, v_sw_k_norm_g, v_sw_sinks, v_w_out, v_norm2_g, v_mem_norm_g, v_xa_wq, v_xa_wkv, v_xa_q_norm_g, v_xa_k_norm_g, v_xa_wo, v_norm3_g, v_mlp_up, v_mlp_down):
    given = dict(locals())
    weights = {n: given[n] for n in WEIGHT_ORDER}
    moms = {n: given["m_" + n] for n in WEIGHT_ORDER}
    vars_ = {n: given["v_" + n] for n in WEIGHT_ORDER}
    d = x.shape[-1]
    ff = mlp_down.shape[1] * 4
    small = {n: weights[n] for n in SMALL_NAMES}

    comm = _MeshWeights({n: weights[n][0].astype(_MXU_DTYPE) for n in BIG_NAMES}, d, ff)
    loss_row, grad_x, g_small = _local_step(x, mem, positions, loss_target, small, comm)
    big_grads = comm.finish(grad_x)

    packed, starts = _pack_rows([g_small[n] for n in SMALL_NAMES] + [loss_row], 1024)
    summed = _all_reduce_small(packed)
    small_grads = {}
    for n, s in zip(SMALL_NAMES, starts):
        r, c = weights[n].shape
        small_grads[n] = summed[s:s + r, 0:c]
    loss = summed[starts[-1], 0]

    grads, deltas, new_m, new_v = {}, {}, {}, {}
    for n in BIG_NAMES:
        shp = weights[n].shape
        g2 = big_grads[n]
        dl, mo, vo = _adamw_big(weights[n][0], g2, moms[n][0], vars_[n][0], name="adamw_" + n)
        grads[n], deltas[n], new_m[n], new_v[n] = (a.reshape(shp) for a in (g2, dl, mo, vo))
    sm_out = _adamw_small([weights[n] for n in SMALL_NAMES], [small_grads[n] for n in SMALL_NAMES],
                          [moms[n] for n in SMALL_NAMES], [vars_[n] for n in SMALL_NAMES])
    ns = len(SMALL_NAMES)
    for i, n in enumerate(SMALL_NAMES):
        grads[n], deltas[n], new_m[n], new_v[n] = small_grads[n], sm_out[i], sm_out[ns + i], sm_out[2 * ns + i]

    return (loss, grad_x, *[grads[n] for n in WEIGHT_ORDER], *[deltas[n] for n in WEIGHT_ORDER],
            *[new_m[n] for n in WEIGHT_ORDER], *[new_v[n] for n in WEIGHT_ORDER])
```

```python
import numpy as np
import jax
import jax.numpy as jnp
from jax import lax
from jax.experimental import pallas as pl
from jax.experimental.pallas import tpu as pltpu

F32 = jnp.float32
_MXU_DTYPE = jnp.bfloat16

EPS = 1e-6
HG_HEADS = 4
HG_D = 128
HG_CHUNK = 64
HG_TILE = 512
HG_LEVELS = (32, 16, 8, 4, 2, 1)
SW_HEADS = 8
SW_KV_HEADS = 2
SW_GROUP = SW_HEADS // SW_KV_HEADS
SW_HD = 64
SW_BLOCK = 128
ROPE_THETA = 500000.0
ROT_DIM = SW_HD // 4
XA_HEADS = 4
XA_HD = 128
HG_COLS = 4 * HG_HEADS * HG_D
SW_COLS = (SW_HEADS + 2 * SW_KV_HEADS) * SW_HD

ADAM_LR = 0.001
ADAM_B1 = 0.9
ADAM_B2 = 0.999
ADAM_EPS = 1e-08
ADAM_WD = 0.01
ADAM_STEP = 10

VMEM_LIMIT = 56 * 1024 * 1024
MESH = pl.DeviceIdType.MESH

NN = ((1,), (0,))
NT = ((1,), (1,))
TN = ((0,), (0,))


def _mx(v):
    return v.astype(_MXU_DTYPE)


def _dot(a, b, dims=NN):
    return lax.dot_general(_mx(a), _mx(b), (dims, ((), ())), preferred_element_type=F32)


def _split_dot(a, v, dims, parts):
    acc = None
    rest = v
    for p in range(parts):
        piece = _mx(rest)
        term = lax.dot_general(a, piece, (dims, ((), ())), preferred_element_type=F32)
        acc = term if acc is None else acc + term
        if p + 1 < parts:
            rest = rest - piece.astype(F32)
    return acc


def _params(sem):
    return pltpu.CompilerParams(dimension_semantics=sem, vmem_limit_bytes=VMEM_LIMIT)


def _mm(a, b, mode, m, n, k, *, name, tm=1024, tn=1024, tk=512, a_spec=None, b_spec=None, extras=(), epilogue=None,
        out_dtypes=(F32,), out_shape=None, out_spec=None, after=()):
    after = tuple(t for t in after if t is not None)
    tm, tn, tk = min(tm, m), min(tn, n), min(tk, k)
    assert m % tm == 0 and n % tn == 0 and k % tk == 0, (name, m, n, k, tm, tn, tk)
    gi, gj, gk = m // tm, n // tn, k // tk
    if a_spec is None:
        a_spec = (pl.BlockSpec((tk, tm), lambda i, j, kk: (kk, i)) if mode == TN
                  else pl.BlockSpec((tm, tk), lambda i, j, kk: (i, kk)))
    if b_spec is None:
        b_spec = (pl.BlockSpec((tn, tk), lambda i, j, kk: (j, kk)) if mode == NT
                  else pl.BlockSpec((tk, tn), lambda i, j, kk: (kk, j)))
    mn_spec = pl.BlockSpec((tm, tn), lambda i, j, kk: (i, j))
    if epilogue is None:
        epilogue = lambda acc: (acc,)
    n_ex, n_out = len(extras), len(out_dtypes)
    if out_shape is None:
        out_shape = tuple(jax.ShapeDtypeStruct((m, n), d) for d in out_dtypes)
        out_spec = tuple(mn_spec for _ in out_dtypes)

    n_after = len(after)

    def body(*refs):
        a_ref, b_ref = refs[0], refs[1]
        ex = refs[2:2 + n_ex]
        outs = refs[2 + n_ex + n_after:2 + n_ex + n_after + n_out]

        def finish(acc):
            res = epilogue(acc, *[e[...] for e in ex])
            for o, r in zip(outs, res):
                o[...] = r.astype(o.dtype)

        if gk == 1:
            finish(_dot(a_ref[...], b_ref[...], mode))
        else:
            acc_ref = refs[-1]
            kk = pl.program_id(2)

            @pl.when(kk == 0)
            def _():
                acc_ref[...] = jnp.zeros_like(acc_ref)

            acc_ref[...] += _dot(a_ref[...], b_ref[...], mode)

            @pl.when(kk == gk - 1)
            def _():
                finish(acc_ref[...])

    return pl.pallas_call(
        body, name=name, grid=(gi, gj, gk),
        in_specs=[a_spec, b_spec] + [mn_spec] * n_ex + [pl.BlockSpec(memory_space=pl.ANY)] * n_after,
        out_specs=out_spec, out_shape=out_shape,
        scratch_shapes=[pltpu.VMEM((tm, tn), F32)] if gk > 1 else [],
        compiler_params=_params(("parallel", "parallel", "arbitrary")),
    )(a, b, *extras, *after)


def _rms_fwd(x, g, *, name, tm=512):
    t, d = x.shape
    tm = min(tm, t)

    def body(x_ref, g_ref, o_ref):
        xv = x_ref[...]
        r = lax.rsqrt(jnp.mean(xv * xv, axis=1, keepdims=True) + EPS)
        o_ref[...] = (xv * r * g_ref[...]).astype(o_ref.dtype)

    return pl.pallas_call(
        body, name=name, grid=(t // tm,),
        in_specs=[pl.BlockSpec((tm, d), lambda i: (i, 0)), pl.BlockSpec((1, d), lambda i: (0, 0))],
        out_specs=pl.BlockSpec((tm, d), lambda i: (i, 0)),
        out_shape=jax.ShapeDtypeStruct((t, d), _MXU_DTYPE),
        compiler_params=_params(("parallel",)),
    )(x, g)


def _rms_bwd(x, g, dy, dres, *, name, tm=512):
    t, d = x.shape
    tm = min(tm, t)
    has_res = dres is not None

    def body(*refs):
        x_ref, g_ref, dy_ref = refs[:3]
        dx_ref, dg_ref = refs[-2:]
        xv, dyv = x_ref[...], dy_ref[...]
        r = lax.rsqrt(jnp.mean(xv * xv, axis=1, keepdims=True) + EPS)
        u = dyv * g_ref[...]
        dx = r * u - xv * (r * r * r) * jnp.mean(u * xv, axis=1, keepdims=True)
        if has_res:
            dx = dx + refs[3][...]
        dx_ref[...] = dx

        @pl.when(pl.program_id(0) == 0)
        def _():
            dg_ref[...] = jnp.zeros_like(dg_ref)

        dg_ref[...] += jnp.sum(dyv * xv * r, axis=0, keepdims=True)

    row = pl.BlockSpec((tm, d), lambda i: (i, 0))
    vec = pl.BlockSpec((1, d), lambda i: (0, 0))
    return pl.pallas_call(
        body, name=name, grid=(t // tm,),
        in_specs=[row, vec, row] + ([row] if has_res else []),
        out_specs=(row, vec),
        out_shape=(jax.ShapeDtypeStruct((t, d), F32), jax.ShapeDtypeStruct((1, d), F32)),
        compiler_params=_params(("arbitrary",)),
    )(*([x, g, dy] + ([dres] if has_res else [])))


def _hg_constants():
    c = HG_CHUNK
    t = np.arange(c)
    sums = [t[None, :] <= t[:, None]]
    masks = []
    for m in HG_LEVELS:
        base = (t // (2 * m)) * (2 * m)
        mid = base + m - 1
        second = (t - base) >= m
        upper = (t[None, :] > mid[:, None]) & (t[None, :] <= t[:, None])
        lower = (t[None, :] > t[:, None]) & (t[None, :] <= mid[:, None])
        sums.append(np.where(second[:, None], upper, lower))
        masks.append(second[:, None] & (~second)[None, :] & (base[:, None] == base[None, :]))
    return (np.concatenate(sums, axis=0).astype(np.float32), np.stack(masks).astype(np.float32))


HG_HEAD_LANES = tuple(slice(HG_D * h, HG_D * (h + 1)) for h in range(HG_HEADS))


def _per_head(fn, slab):
    return jnp.concatenate([jnp.broadcast_to(fn(slab[:, hs]), (slab.shape[0], HG_D)) for hs in HG_HEAD_LANES], axis=1)


def _lane_sum(v):
    return jnp.sum(v, axis=1, keepdims=True)


def _lane_mean(v):
    return jnp.mean(v, axis=1, keepdims=True)


def _hg_gates(blk, lbp):
    w = HG_HEADS * HG_D
    q, x, v, gl = blk[:, 0:w], blk[:, w:2 * w], blk[:, 2 * w:3 * w], blk[:, 3 * w:4 * w]
    mx = jnp.max(lbp, axis=0, keepdims=True)
    e = jnp.exp(lbp - mx)
    lb = e[0:1, :] / jnp.sum(e, axis=0, keepdims=True)
    sig = jax.nn.sigmoid(x)
    f = lb + (1.0 - lb) * sig
    return q, v, gl, lb, sig, f, 1.0 - f, jnp.log(f)


def _hg_fwd(proj, lbp, ng, bsz, seq, *, y_width):
    t = proj.shape[0]
    nc = seq // HG_CHUNK
    a_np, m_np = _hg_constants()
    a_all = jnp.asarray(a_np, _MXU_DTYPE)
    masks = jnp.asarray(m_np, F32)
    nl = len(HG_LEVELS)

    ts = min(HG_TILE, seq)
    ns, nct = seq // ts, ts // HG_CHUNK
    hw = HG_HEADS * HG_D

    def body(p_ref, lb_ref, ng_ref, a_ref, m_ref, y_ref, o_ref, st_ref, carry):
        a_mat = a_ref[...]
        ngv = ng_ref[...]

        @pl.when(pl.program_id(1) == 0)
        def _():
            carry[...] = jnp.zeros_like(carry)

        ng4 = _tile_lanes(ngv, HG_HEADS)
        heads = range(HG_HEADS)
        hl = HG_HEAD_LANES

        def chunk(c, _):
            rows = pl.ds(pl.multiple_of(c * HG_CHUNK, HG_CHUNK), HG_CHUNK)
            q, v, gl, lb, sig, f, k, g = _hg_gates(p_ref[rows, :], lb_ref[...])
            sts = [carry[h] for h in heads]
            e_all = _split_dot(a_mat, g, NN, 3)
            b = e_all[0:HG_CHUNK]
            qb = q * jnp.exp(b)
            o = [_dot(qb[:, hl[h]], sts[h], NT) for h in heads]
            p = [jnp.zeros((HG_CHUNK, HG_CHUNK), F32) for _ in heads]
            for li in range(nl):
                e = jnp.exp(e_all[HG_CHUNK * (li + 1):HG_CHUNK * (li + 2)])
                qm, km, mk = q * e, k * e, m_ref[li]
                p = [p[h] + mk * _dot(qm[:, hl[h]], km[:, hl[h]], NT) for h in heads]
            bl = b[HG_CHUNK - 1:HG_CHUNK, :]
            kd = k * jnp.exp(bl - b)
            ebl = jnp.exp(bl)
            pv = [_dot(p[h], v[:, hl[h]]) for h in heads]
            upd = [_dot(v[:, hl[h]], kd[:, hl[h]], TN) for h in heads]
            o_all = jnp.concatenate([o[h] + pv[h] for h in heads], axis=1) + _per_head(_lane_sum, q * k) * v
            r = lax.rsqrt(_per_head(_lane_mean, o_all * o_all) + EPS)
            for h in heads:
                st_ref[h, c] = sts[h]
                carry[h] = sts[h] * ebl[:, hl[h]] + upd[h]
            o_ref[rows, :] = o_all
            y_ref[rows, :] = (o_all * r * ng4) * (gl * jax.nn.sigmoid(gl))
            return 0

        lax.fori_loop(0, nct, chunk, 0)

    return pl.pallas_call(
        body, name="hgrn2_fwd", grid=(bsz, ns),
        in_specs=[pl.BlockSpec((ts, HG_COLS), lambda b, s: (b * ns + s, 0)),
                  pl.BlockSpec((2, hw), lambda b, s: (0, 0)),
                  pl.BlockSpec((1, HG_D), lambda b, s: (0, 0)),
                  pl.BlockSpec(a_all.shape, lambda b, s: (0, 0)),
                  pl.BlockSpec(masks.shape, lambda b, s: (0, 0, 0))],
        out_specs=(pl.BlockSpec((ts, hw), lambda b, s: (b * ns + s, 0)),
                   pl.BlockSpec((ts, hw), lambda b, s: (b * ns + s, 0)),
                   pl.BlockSpec((None, HG_HEADS, nct, HG_D, HG_D), lambda b, s: (b, 0, s, 0, 0))),
        out_shape=(jax.ShapeDtypeStruct((t, y_width), F32),
                   jax.ShapeDtypeStruct((t, hw), F32),
                   jax.ShapeDtypeStruct((bsz, HG_HEADS, nc, HG_D, HG_D), F32)),
        scratch_shapes=[pltpu.VMEM((HG_HEADS, HG_D, HG_D), F32)],
        compiler_params=_params(("parallel", "arbitrary")),
    )(proj, lbp, ng, a_all, masks)


def _hg_bwd(proj, lbp, ng, o_all, states, dy, bsz, seq):
    t = proj.shape[0]
    nc = seq // HG_CHUNK
    a_np, m_np = _hg_constants()
    a_all = jnp.asarray(a_np, _MXU_DTYPE)
    masks = jnp.asarray(m_np, F32)
    nl = len(HG_LEVELS)
    cs = HG_CHUNK

    ts = min(HG_TILE, seq)
    ns, nct = seq // ts, ts // cs
    hw = HG_HEADS * HG_D

    def body(p_ref, lb_ref, ng_ref, a_ref, m_ref, o_ref, st_ref, dy_ref, dp_ref, dlb_ref, dng_ref, dst_ref):
        a_mat = a_ref[...]
        ngv = ng_ref[...]
        ng4 = _tile_lanes(ngv, HG_HEADS)
        last_row = lax.broadcasted_iota(jnp.int32, (cs, hw), 0) == cs - 1
        si = pl.program_id(1)
        first = jnp.logical_and(pl.program_id(0) == 0, si == 0)
        heads = range(HG_HEADS)
        hl = HG_HEAD_LANES

        @pl.when(si == 0)
        def _():
            dst_ref[...] = jnp.zeros_like(dst_ref)

        def side_by_side(parts):
            return jnp.concatenate(parts, axis=1)

        def chunk(i, carry):
            dlb_acc, dng_acc = carry
            c = nct - 1 - i
            rows = pl.ds(pl.multiple_of(c * cs, cs), cs)
            q, v, gl, lb, sig, f, k, g = _hg_gates(p_ref[rows, :], lb_ref[...])
            o = o_ref[rows, :]
            dyv = dy_ref[rows, :]
            sts = [st_ref[h, c] for h in heads]
            dsts = [dst_ref[h] for h in heads]
            e_all = _split_dot(a_mat, g, NN, 3)
            b = e_all[0:cs]
            eb = jnp.exp(b)
            bl = b[cs - 1:cs, :]
            ebl = jnp.exp(bl)
            ekd = jnp.exp(bl - b)
            qb, kd = q * eb, k * ekd
            sg = jax.nn.sigmoid(gl)
            silu = gl * sg
            r = lax.rsqrt(_per_head(_lane_mean, o * o) + EPS)
            dgl = dyv * (o * r * ng4) * (sg * (1.0 + gl * (1.0 - sg)))
            u = dyv * silu * ng4
            do = r * u - o * (r * r * r) * _per_head(_lane_mean, u * o)
            dng4 = jnp.sum(dyv * silu * o * r, axis=0, keepdims=True)
            dng_acc = dng_acc + ((dng4[:, hl[0]] + dng4[:, hl[1]]) + (dng4[:, hl[2]] + dng4[:, hl[3]]))
            es, qm, km = [], [], []
            p = [jnp.zeros((cs, cs), F32) for _ in heads]
            for li in range(nl):
                e = jnp.exp(e_all[cs * (li + 1):cs * (li + 2)])
                es.append(e)
                qm.append(q * e)
                km.append(k * e)
                mk = m_ref[li]
                p = [p[h] + mk * _dot(qm[li][:, hl[h]], km[li][:, hl[h]], NT) for h in heads]
            dp = [_dot(do[:, hl[h]], v[:, hl[h]], NT) for h in heads]
            dv_p = [_dot(p[h], do[:, hl[h]], TN) for h in heads]
            dv_s = [_dot(kd[:, hl[h]], dsts[h], NT) for h in heads]
            dqb = side_by_side([_dot(do[:, hl[h]], sts[h]) for h in heads])
            dkd = side_by_side([_dot(v[:, hl[h]], dsts[h]) for h in heads])
            new_dst = [_dot(do[:, hl[h]], qb[:, hl[h]], TN) for h in heads]
            dv = side_by_side([dv_p[h] + dv_s[h] for h in heads]) + _per_head(_lane_sum, q * k) * do
            dq = dqb * eb
            dk = dkd * ekd
            db = dqb * qb - dkd * kd
            dbl = (jnp.sum(dkd * kd, axis=0, keepdims=True)
                   + side_by_side([jnp.sum(dsts[h] * sts[h], axis=0, keepdims=True) for h in heads]) * ebl)
            de = [db + jnp.where(last_row, dbl, 0.0)]
            for li in range(nl):
                mk = m_ref[li]
                dpm = [mk * dp[h] for h in heads]
                dqm = side_by_side([_dot(dpm[h], km[li][:, hl[h]]) for h in heads])
                dkm = side_by_side([_dot(dpm[h], qm[li][:, hl[h]], TN) for h in heads])
                dq = dq + dqm * es[li]
                dk = dk + dkm * es[li]
                de.append(dqm * qm[li] + dkm * km[li])
            dpd = _per_head(_lane_sum, do * v)
            dq = dq + dpd * k
            dk = dk + dpd * q
            dg = _split_dot(a_mat, jnp.concatenate(de, axis=0), TN, 2)
            df = dg / f - dk
            dp_ref[rows, 0:hw] = dq
            dp_ref[rows, hw:2 * hw] = df * (1.0 - lb) * sig * (1.0 - sig)
            dp_ref[rows, 2 * hw:3 * hw] = dv
            dp_ref[rows, 3 * hw:4 * hw] = dgl
            for h in heads:
                dst_ref[h] = dsts[h] * ebl[:, hl[h]] + new_dst[h]
            return dlb_acc + jnp.sum(df * (1.0 - sig), axis=0, keepdims=True), dng_acc

        dlb, dng = lax.fori_loop(0, nct, chunk, (jnp.zeros((1, hw), F32), jnp.zeros((1, HG_D), F32)))

        @pl.when(first)
        def _():
            dlb_ref[...] = jnp.zeros_like(dlb_ref)
            dng_ref[...] = jnp.zeros_like(dng_ref)

        lbp_v = lb_ref[...]
        mx = jnp.max(lbp_v, axis=0, keepdims=True)
        e = jnp.exp(lbp_v - mx)
        s0 = e[0:1, :] / jnp.sum(e, axis=0, keepdims=True)
        da0 = dlb * s0 * (1.0 - s0)
        dlb_ref[...] += jnp.concatenate([da0, -da0], axis=0)
        dng_ref[...] += dng

    def tile(b, s):
        return b * ns + (ns - 1 - s)

    return pl.pallas_call(
        body, name="hgrn2_bwd", grid=(bsz, ns),
        in_specs=[pl.BlockSpec((ts, HG_COLS), lambda b, s: (tile(b, s), 0)),
                  pl.BlockSpec((2, hw), lambda b, s: (0, 0)),
                  pl.BlockSpec((1, HG_D), lambda b, s: (0, 0)),
                  pl.BlockSpec(a_all.shape, lambda b, s: (0, 0)),
                  pl.BlockSpec(masks.shape, lambda b, s: (0, 0, 0)),
                  pl.BlockSpec((ts, hw), lambda b, s: (tile(b, s), 0)),
                  pl.BlockSpec((None, HG_HEADS, nct, HG_D, HG_D), lambda b, s: (b, 0, ns - 1 - s, 0, 0)),
                  pl.BlockSpec((ts, hw), lambda b, s: (tile(b, s), 0))],
        out_specs=(pl.BlockSpec((ts, HG_COLS), lambda b, s: (tile(b, s), 0)),
                   pl.BlockSpec((2, hw), lambda b, s: (0, 0)),
                   pl.BlockSpec((1, HG_D), lambda b, s: (0, 0))),
        out_shape=(jax.ShapeDtypeStruct((t, HG_COLS), F32),
                   jax.ShapeDtypeStruct((2, hw), F32),
                   jax.ShapeDtypeStruct((1, HG_D), F32)),
        scratch_shapes=[pltpu.VMEM((HG_HEADS, HG_D, HG_D), F32)],
        compiler_params=_params(("arbitrary", "arbitrary")),
    )(proj, lbp, ng, a_all, masks, o_all, states, dy)


def _sw_constants():
    half = ROT_DIM // 2
    inv = (np.float32(ROPE_THETA) ** (-(np.arange(half, dtype=np.float32) * np.float32(2.0) / np.float32(ROT_DIM)))
           ).astype(np.float32)
    freq = np.zeros((1, 128), np.float32)
    sign = np.zeros((1, 128), np.float32)
    for h in range(2):
        freq[0, 64 * h:64 * h + half] = inv
        freq[0, 64 * h + half:64 * h + 2 * half] = inv
        sign[0, 64 * h:64 * h + half] = -1.0
        sign[0, 64 * h + half:64 * h + 2 * half] = 1.0
    seg = np.kron(np.eye(8, dtype=np.float32), np.full((64, 64), 1.0 / 64.0, np.float32))
    return freq, sign, seg


def _rope_tables(pos, freq, sign):
    ang = pos.astype(F32) * freq
    return jnp.cos(ang), jnp.sin(ang) * sign


def _tile_lanes(v, times):
    return v if times == 1 else jnp.concatenate([v] * times, axis=1)


def _swap_halves(v):
    w = v.shape[1]
    half = ROT_DIM // 2
    lane = lax.broadcasted_iota(jnp.int32, v.shape, 1) % SW_HD
    return jnp.where(lane < half, pltpu.roll(v, w - half, 1), jnp.where(lane < 2 * half, pltpu.roll(v, half, 1), 0.0))


def _sw_norm_rope(tv, gain, seg, cosv, sinv):
    w = tv.shape[1]
    ms = _split_dot_rhs(tv * tv, seg[0:w, 0:w])
    r = lax.rsqrt(ms + EPS)
    tn = tv * r * gain
    reps = w // 128
    return tn * _tile_lanes(cosv, reps) + _swap_halves(tn) * _tile_lanes(sinv, reps), r


def _split_dot_rhs(v, a):
    hi = _mx(v)
    lo = _mx(v - hi.astype(F32))
    return (lax.dot_general(hi, a, (NN, ((), ())), preferred_element_type=F32)
            + lax.dot_general(lo, a, (NN, ((), ())), preferred_element_type=F32))


def _sw_norm_rope_bwd(dt, tv, r, gain, seg, cosv, sinv):
    w = tv.shape[1]
    reps = w // 128
    dtn = dt * _tile_lanes(cosv, reps) + _swap_halves(dt * _tile_lanes(sinv, reps))
    u = dtn * gain
    dtv = r * u - tv * (r * r * r) * _split_dot_rhs(u * tv, seg[0:w, 0:w])
    return dtv, jnp.sum(dtn * tv * r, axis=0, keepdims=True)


def _sw_scores(qh, kp, kc):
    return _dot(qh, kp, NT), _dot(qh, kc, NT)


def _sw_probs(raw, sink, first_block):
    scale = SW_HD ** -0.5
    qi = lax.broadcasted_iota(jnp.int32, (SW_BLOCK, SW_BLOCK), 0)
    kj = lax.broadcasted_iota(jnp.int32, (SW_BLOCK, SW_BLOCK), 1)
    ok_prev = jnp.logical_and(kj > qi, jnp.logical_not(first_block))
    ok_cur = kj <= qi
    sp = jnp.where(ok_prev, raw[0] * scale, -jnp.inf)
    sc = jnp.where(ok_cur, raw[1] * scale, -jnp.inf)
    m = jnp.maximum(jnp.maximum(jnp.max(sp, axis=1, keepdims=True), jnp.max(sc, axis=1, keepdims=True)), sink)
    pp, pc = jnp.exp(sp - m), jnp.exp(sc - m)
    es = jnp.exp(sink - m)
    den = jnp.sum(pp, axis=1, keepdims=True) + jnp.sum(pc, axis=1, keepdims=True) + es
    return pp / den, pc / den, es / den


def _sw_specs(nb):
    def cur(b, n):
        return b * nb + jnp.minimum(n, nb - 1)

    def prev(b, n):
        return b * nb + jnp.maximum(jnp.minimum(n, nb - 1) - 1, 0)

    return cur, prev


def _sw_fwd(proj, pos, qg, kg, sinks, y_in, bsz, seq):
    t = proj.shape[0]
    nb = seq // SW_BLOCK
    freq_np, sign_np, seg_np = _sw_constants()
    freq, sign = jnp.asarray(freq_np), jnp.asarray(sign_np)
    seg = jnp.asarray(seg_np, _MXU_DTYPE)
    cur, prev = _sw_specs(nb)

    def body(q_ref, kc_ref, kp_ref, vc_ref, vp_ref, pc_ref, pp_ref, qg_ref, kg_ref, sk_ref, fr_ref, sn_ref, seg_ref,
             yin_ref, y_ref):
        del yin_ref
        n = pl.program_id(1)
        segv = seg_ref[...]
        cos_c, sin_c = _rope_tables(pc_ref[...], fr_ref[...], sn_ref[...])
        cos_p, sin_p = _rope_tables(pp_ref[...], fr_ref[...], sn_ref[...])
        qr, _ = _sw_norm_rope(q_ref[...], qg_ref[...], segv, cos_c, sin_c)
        kcr, _ = _sw_norm_rope(kc_ref[...], kg_ref[...], segv, cos_c, sin_c)
        kpr, _ = _sw_norm_rope(kp_ref[...], kg_ref[...], segv, cos_p, sin_p)
        vc, vp = vc_ref[...], vp_ref[...]
        ks = [slice(SW_HD * (h // SW_GROUP), SW_HD * (h // SW_GROUP + 1)) for h in range(SW_HEADS)]
        raw = [_sw_scores(qr[:, SW_HD * h:SW_HD * (h + 1)], kpr[:, ks[h]], kcr[:, ks[h]]) for h in range(SW_HEADS)]
        probs = [_sw_probs(raw[h], sk_ref[0, h], n == 0) for h in range(SW_HEADS)]
        for h in range(SW_HEADS):
            y_ref[:, SW_HD * h:SW_HD * (h + 1)] = _dot(probs[h][0], vp[:, ks[h]]) + _dot(probs[h][1], vc[:, ks[h]])

    rowq = pl.BlockSpec((SW_BLOCK, 512), lambda b, n: (cur(b, n), 0))
    full = lambda a: pl.BlockSpec(a.shape, lambda b, n: (0,) * a.ndim)
    yw = y_in.shape[1]
    return pl.pallas_call(
        body, name="swa_fwd", grid=(bsz, nb),
        in_specs=[rowq,
                  pl.BlockSpec((SW_BLOCK, 128), lambda b, n: (cur(b, n), 4)),
                  pl.BlockSpec((SW_BLOCK, 128), lambda b, n: (prev(b, n), 4)),
                  pl.BlockSpec((SW_BLOCK, 128), lambda b, n: (cur(b, n), 5)),
                  pl.BlockSpec((SW_BLOCK, 128), lambda b, n: (prev(b, n), 5)),
                  pl.BlockSpec((SW_BLOCK, 1), lambda b, n: (cur(b, n), 0)),
                  pl.BlockSpec((SW_BLOCK, 1), lambda b, n: (prev(b, n), 0)),
                  full(qg), full(kg),
                  pl.BlockSpec(memory_space=pltpu.SMEM),
                  full(freq), full(sign), full(seg),
                  pl.BlockSpec(memory_space=pl.ANY)],
        out_specs=pl.BlockSpec((SW_BLOCK, 512), lambda b, n: (cur(b, n), 1)),
        out_shape=jax.ShapeDtypeStruct((t, yw), F32),
        input_output_aliases={13: 0},
        compiler_params=_params(("parallel", "parallel")),
    )(proj, proj, proj, proj, proj, pos, pos, qg, kg, sinks, freq, sign, seg, y_in)


def _sw_bwd(proj, pos, qg, kg, sinks, y, dy, bsz, seq):
    t = proj.shape[0]
    nb = seq // SW_BLOCK
    freq_np, sign_np, seg_np = _sw_constants()
    freq, sign = jnp.asarray(freq_np), jnp.asarray(sign_np)
    seg = jnp.asarray(seg_np, _MXU_DTYPE)
    cur, prev = _sw_specs(nb)
    scale = SW_HD ** -0.5

    def body(q_ref, kc_ref, kp_ref, vc_ref, vp_ref, pc_ref, pp_ref, qg_ref, kg_ref, sk_ref, fr_ref, sn_ref, seg_ref,
             y_ref, dy_ref, dp_ref, dqg_ref, dkg_ref, dsk_ref,
             dq_car, dkv_car, dqr_s, dkc_s, dkp_s, dvc_s, dvp_s, gq_acc, gk_acc, sk_acc):
        b, n = pl.program_id(0), pl.program_id(1)
        first = jnp.logical_and(b == 0, n == 0)
        last = jnp.logical_and(b == pl.num_programs(0) - 1, n == nb)

        @pl.when(first)
        def _():
            gq_acc[...] = jnp.zeros_like(gq_acc)
            gk_acc[...] = jnp.zeros_like(gk_acc)
            sk_acc[...] = jnp.zeros_like(sk_acc)

        @pl.when(n < nb)
        def _():
            segv = seg_ref[...]
            cos_c, sin_c = _rope_tables(pc_ref[...], fr_ref[...], sn_ref[...])
            cos_p, sin_p = _rope_tables(pp_ref[...], fr_ref[...], sn_ref[...])
            qv, kcv, kpv = q_ref[...], kc_ref[...], kp_ref[...]
            qr, rq = _sw_norm_rope(qv, qg_ref[...], segv, cos_c, sin_c)
            kcr, rkc = _sw_norm_rope(kcv, kg_ref[...], segv, cos_c, sin_c)
            kpr, rkp = _sw_norm_rope(kpv, kg_ref[...], segv, cos_p, sin_p)
            vc, vp = vc_ref[...], vp_ref[...]
            lane = lax.broadcasted_iota(jnp.int32, (1, 128), 1)
            dsk = jnp.zeros((1, 128), F32)
            heads = range(SW_HEADS)
            ks = [slice(SW_HD * (h // SW_GROUP), SW_HD * (h // SW_GROUP + 1)) for h in heads]
            hs = [slice(SW_HD * h, SW_HD * (h + 1)) for h in heads]
            qh = [qr[:, hs[h]] for h in heads]
            doh = [dy_ref[:, hs[h]] for h in heads]
            raw = [_sw_scores(qh[h], kpr[:, ks[h]], kcr[:, ks[h]]) for h in heads]
            dpp = [_dot(doh[h], vp[:, ks[h]], NT) for h in heads]
            dpc = [_dot(doh[h], vc[:, ks[h]], NT) for h in heads]
            probs = [_sw_probs(raw[h], sk_ref[0, h], n == 0) for h in heads]
            dsp, dsc = [], []
            for h in heads:
                pp, pc, ps = probs[h]
                delta = jnp.sum(doh[h] * y_ref[:, hs[h]], axis=1, keepdims=True)
                dsp.append(pp * (dpp[h] - delta) * scale)
                dsc.append(pc * (dpc[h] - delta) * scale)
                dsk = dsk + jnp.where(lane == h, -jnp.sum(ps * delta), 0.0)
            for h in heads:
                dqr_s[:, hs[h]] = _dot(dsp[h], kpr[:, ks[h]]) + _dot(dsc[h], kcr[:, ks[h]])
            for kv in range(SW_KV_HEADS):
                group = range(SW_GROUP * kv, SW_GROUP * (kv + 1))
                kvs = slice(SW_HD * kv, SW_HD * (kv + 1))
                dvp_s[:, kvs] = sum(_dot(probs[h][0], doh[h], TN) for h in group)
                dvc_s[:, kvs] = sum(_dot(probs[h][1], doh[h], TN) for h in group)
                dkp_s[:, kvs] = sum(_dot(dsp[h], qh[h], TN) for h in group)
                dkc_s[:, kvs] = sum(_dot(dsc[h], qh[h], TN) for h in group)
            dq, gq = _sw_norm_rope_bwd(dqr_s[...], qv, rq, qg_ref[...], segv, cos_c, sin_c)
            dkc, gkc = _sw_norm_rope_bwd(dkc_s[...], kcv, rkc, kg_ref[...], segv, cos_c, sin_c)
            dkp, gkp = _sw_norm_rope_bwd(dkp_s[...], kpv, rkp, kg_ref[...], segv, cos_p, sin_p)
            gq_acc[...] += gq
            gk_acc[...] += gkc + gkp
            sk_acc[...] += dsk

            @pl.when(n > 0)
            def _():
                dp_ref[:, 0:512] = dq_car[...]
                dp_ref[:, 512:640] = dkv_car[:, 0:128] + dkp
                dp_ref[:, 640:768] = dkv_car[:, 128:256] + dvp_s[...]

            dq_car[...] = dq
            dkv_car[:, 0:128] = dkc
            dkv_car[:, 128:256] = dvc_s[...]

        @pl.when(n == nb)
        def _():
            dp_ref[:, 0:512] = dq_car[...]
            dp_ref[:, 512:768] = dkv_car[...]

        @pl.when(last)
        def _():
            gq = gq_acc[...]
            acc = gq[:, 0:SW_HD]
            for h in range(1, SW_HEADS):
                acc = acc + gq[:, SW_HD * h:SW_HD * (h + 1)]
            dqg_ref[...] = acc
            gk = gk_acc[...]
            dkg_ref[...] = gk[:, 0:SW_HD] + gk[:, SW_HD:2 * SW_HD]
            dsk_ref[...] = sk_acc[...]

    rowq = pl.BlockSpec((SW_BLOCK, 512), lambda b, n: (cur(b, n), 0))
    full = lambda a: pl.BlockSpec(a.shape, lambda b, n: (0,) * a.ndim)

    def out_row(b, n):
        return b * nb + jnp.maximum(n - 1, 0)

    return pl.pallas_call(
        body, name="swa_bwd", grid=(bsz, nb + 1),
        in_specs=[rowq,
                  pl.BlockSpec((SW_BLOCK, 128), lambda b, n: (cur(b, n), 4)),
                  pl.BlockSpec((SW_BLOCK, 128), lambda b, n: (prev(b, n), 4)),
                  pl.BlockSpec((SW_BLOCK, 128), lambda b, n: (cur(b, n), 5)),
                  pl.BlockSpec((SW_BLOCK, 128), lambda b, n: (prev(b, n), 5)),
                  pl.BlockSpec((SW_BLOCK, 1), lambda b, n: (cur(b, n), 0)),
                  pl.BlockSpec((SW_BLOCK, 1), lambda b, n: (prev(b, n), 0)),
                  full(qg), full(kg),
                  pl.BlockSpec(memory_space=pltpu.SMEM),
                  full(freq), full(sign), full(seg),
                  pl.BlockSpec((SW_BLOCK, 512), lambda b, n: (cur(b, n), 1)),
                  pl.BlockSpec((SW_BLOCK, 512), lambda b, n: (cur(b, n), 1))],
        out_specs=(pl.BlockSpec((SW_BLOCK, SW_COLS), lambda b, n: (out_row(b, n), 0)),
                   pl.BlockSpec((1, SW_HD), lambda b, n: (0, 0)),
                   pl.BlockSpec((1, SW_HD), lambda b, n: (0, 0)),
                   pl.BlockSpec((1, 128), lambda b, n: (0, 0))),
        out_shape=(jax.ShapeDtypeStruct((t, SW_COLS), F32),
                   jax.ShapeDtypeStruct((1, SW_HD), F32),
                   jax.ShapeDtypeStruct((1, SW_HD), F32),
                   jax.ShapeDtypeStruct((1, 128), F32)),
        scratch_shapes=[pltpu.VMEM((SW_BLOCK, 512), F32), pltpu.VMEM((SW_BLOCK, 256), F32),
                        pltpu.VMEM((SW_BLOCK, 512), F32),
                        pltpu.VMEM((SW_BLOCK, 128), F32), pltpu.VMEM((SW_BLOCK, 128), F32),
                        pltpu.VMEM((SW_BLOCK, 128), F32), pltpu.VMEM((SW_BLOCK, 128), F32),
                        pltpu.VMEM((1, 512), F32), pltpu.VMEM((1, 128), F32), pltpu.VMEM((1, 128), F32)],
        compiler_params=_params(("arbitrary", "arbitrary")),
    )(proj, proj, proj, proj, proj, pos, pos, qg, kg, sinks, freq, sign, seg, y, dy)


def _head_rms(tv, gain):
    r = lax.rsqrt(jnp.mean(tv * tv, axis=1, keepdims=True) + EPS)
    return tv * r * gain, r


def _head_rms_bwd(dtn, tv, r, gain):
    u = dtn * gain
    return r * u - tv * (r * r * r) * jnp.mean(u * tv, axis=1, keepdims=True), jnp.sum(dtn * tv * r, axis=0, keepdims=True)


def _xa_probs(qn, kn):
    s = _dot(qn, kn, NT) * (XA_HD ** -0.5)
    e = jnp.exp(s - jnp.max(s, axis=1, keepdims=True))
    return e / jnp.sum(e, axis=1, keepdims=True)


def _xa_fwd(qx, kvx, qg, kg, bsz, seq, mlen, *, tq=512):
    t = qx.shape[0]
    tq = min(tq, seq)
    nq = seq // tq
    w = XA_HEADS * XA_HD

    def body(q_ref, kv_ref, qg_ref, kg_ref, o_ref):
        for h in range(XA_HEADS):
            hs = slice(XA_HD * h, XA_HD * (h + 1))
            qn, _ = _head_rms(q_ref[:, hs], qg_ref[...])
            kn, _ = _head_rms(kv_ref[:, hs], kg_ref[...])
            o_ref[:, hs] = _dot(_xa_probs(qn, kn), kv_ref[:, w + XA_HD * h:w + XA_HD * (h + 1)])

    vec = pl.BlockSpec((1, XA_HD), lambda b, i: (0, 0))
    return pl.pallas_call(
        body, name="xattn_fwd", grid=(bsz, nq),
        in_specs=[pl.BlockSpec((tq, w), lambda b, i: (b * nq + i, 0)),
                  pl.BlockSpec((mlen, 2 * w), lambda b, i: (b, 0)), vec, vec],
        out_specs=pl.BlockSpec((tq, w), lambda b, i: (b * nq + i, 0)),
        out_shape=jax.ShapeDtypeStruct((t, w), F32),
        compiler_params=_params(("parallel", "parallel")),
    )(qx, kvx, qg, kg)


def _xa_bwd(qx, kvx, qg, kg, do, bsz, seq, mlen, *, tq=512):
    t = qx.shape[0]
    tq = min(tq, seq)
    nq = seq // tq
    w = XA_HEADS * XA_HD
    scale = XA_HD ** -0.5

    def body(q_ref, kv_ref, qg_ref, kg_ref, do_ref, dq_ref, dkv_ref, dqg_ref, dkg_ref):
        b, i = pl.program_id(0), pl.program_id(1)

        @pl.when(jnp.logical_and(b == 0, i == 0))
        def _():
            dqg_ref[...] = jnp.zeros_like(dqg_ref)
            dkg_ref[...] = jnp.zeros_like(dkg_ref)

        @pl.when(i == 0)
        def _():
            dkv_ref[...] = jnp.zeros_like(dkv_ref)

        gq_sum = jnp.zeros((1, XA_HD), F32)
        gk_sum = jnp.zeros((1, XA_HD), F32)
        for h in range(XA_HEADS):
            hs = slice(XA_HD * h, XA_HD * (h + 1))
            vs = slice(w + XA_HD * h, w + XA_HD * (h + 1))
            qv, kv, vv = q_ref[:, hs], kv_ref[:, hs], kv_ref[:, vs]
            qn, rq = _head_rms(qv, qg_ref[...])
            kn, rk = _head_rms(kv, kg_ref[...])
            p = _xa_probs(qn, kn)
            doh = do_ref[:, hs]
            dp = _dot(doh, vv, NT)
            ds = p * (dp - jnp.sum(p * dp, axis=1, keepdims=True)) * scale
            dqv, gq = _head_rms_bwd(_dot(ds, kn), qv, rq, qg_ref[...])
            dkv, gk = _head_rms_bwd(_dot(ds, qn, TN), kv, rk, kg_ref[...])
            dq_ref[:, hs] = dqv
            dkv_ref[:, hs] += dkv
            dkv_ref[:, vs] += _dot(p, doh, TN)
            gq_sum = gq_sum + gq
            gk_sum = gk_sum + gk
        dqg_ref[...] += gq_sum
        dkg_ref[...] += gk_sum

    vec = pl.BlockSpec((1, XA_HD), lambda b, i: (0, 0))
    row = pl.BlockSpec((tq, w), lambda b, i: (b * nq + i, 0))
    mem = pl.BlockSpec((mlen, 2 * w), lambda b, i: (b, 0))
    return pl.pallas_call(
        body, name="xattn_bwd", grid=(bsz, nq),
        in_specs=[row, mem, vec, vec, row],
        out_specs=(row, mem, vec, vec),
        out_shape=(jax.ShapeDtypeStruct((t, w), F32), jax.ShapeDtypeStruct((bsz * mlen, 2 * w), F32),
                   jax.ShapeDtypeStruct((1, XA_HD), F32), jax.ShapeDtypeStruct((1, XA_HD), F32)),
        compiler_params=_params(("arbitrary", "arbitrary")),
    )(qx, kvx, qg, kg, do)


def _loss_sum(dy, d_model, *, tm=512):
    t, d = dy.shape
    tm = min(tm, t)
    steps = t // tm

    def body(dy_ref, o_ref, acc_ref):
        i = pl.program_id(0)

        @pl.when(i == 0)
        def _():
            acc_ref[...] = jnp.zeros_like(acc_ref)

        diff = dy_ref[...] * float(d_model)
        acc_ref[...] += jnp.sum(diff * diff, axis=0, keepdims=True)

        @pl.when(i == steps - 1)
        def _():
            o_ref[...] = jnp.zeros_like(o_ref) + 0.5 * jnp.sum(acc_ref[...]) / float(d_model)

    return pl.pallas_call(
        body, name="loss_sum", grid=(steps,),
        in_specs=[pl.BlockSpec((tm, d), lambda i: (i, 0))],
        out_specs=pl.BlockSpec((1, 128), lambda i: (0, 0)),
        out_shape=jax.ShapeDtypeStruct((1, 128), F32),
        scratch_shapes=[pltpu.VMEM((1, d), F32)],
        compiler_params=_params(("arbitrary",)),
    )(dy)


def _adamw_math(w, g, m, v):
    m = ADAM_B1 * m + (1.0 - ADAM_B1) * g
    v = ADAM_B2 * v + (1.0 - ADAM_B2) * (g * g)
    m_hat = m / (1.0 - ADAM_B1 ** ADAM_STEP)
    v_hat = v / (1.0 - ADAM_B2 ** ADAM_STEP)
    return -ADAM_LR * (m_hat / (jnp.sqrt(v_hat) + ADAM_EPS) + ADAM_WD * w), m, v


def _adamw_big(w, g, m, v, *, name, tr=256):
    r, c = w.shape
    tr = min(tr, r)

    def body(w_ref, g_ref, m_ref, v_ref, d_ref, mo_ref, vo_ref):
        d, mn, vn = _adamw_math(w_ref[...], g_ref[...], m_ref[...], v_ref[...])
        d_ref[...] = d
        mo_ref[...] = mn
        vo_ref[...] = vn

    spec = pl.BlockSpec((tr, c), lambda i: (i, 0))
    shp = jax.ShapeDtypeStruct((r, c), F32)
    return pl.pallas_call(
        body, name=name, grid=(r // tr,), in_specs=[spec] * 4, out_specs=(spec,) * 3, out_shape=(shp,) * 3,
        compiler_params=_params(("parallel",)),
    )(w, g, m, v)


def _adamw_small(ws, gs, ms, vs):
    n = len(ws)

    def body(*refs):
        for i in range(n):
            d, mn, vn = _adamw_math(refs[i][...], refs[n + i][...], refs[2 * n + i][...], refs[3 * n + i][...])
            refs[4 * n + i][...] = d
            refs[5 * n + i][...] = mn
            refs[6 * n + i][...] = vn

    shapes = tuple(jax.ShapeDtypeStruct(w.shape, F32) for w in ws)
    return pl.pallas_call(body, name="adamw_small", out_shape=shapes * 3)(*ws, *gs, *ms, *vs)


def _add_halves(g, recv, c_idx, *, name, tr=256):
    _, r, c = g.shape
    h = r // 2
    tr = min(tr, h)
    nt = h // tr

    def body(c_ref, g_ref, r_ref, o_ref):
        del c_ref
        o_ref[...] = g_ref[...] + r_ref[...]

    return pl.pallas_call(
        body, name=name,
        grid_spec=pltpu.PrefetchScalarGridSpec(
            num_scalar_prefetch=1, grid=(4, nt),
            in_specs=[pl.BlockSpec((None, tr, c), lambda k, i, cr: (k, cr[0] * nt + i, 0)),
                      pl.BlockSpec((None, tr, c), lambda k, i, cr: (k, i, 0))],
            out_specs=pl.BlockSpec((None, tr, c), lambda k, i, cr: (k, i, 0))),
        out_shape=jax.ShapeDtypeStruct((4, h, c), F32),
        compiler_params=_params(("parallel", "parallel")),
    )(c_idx, g, recv)


def _add_chips(p, recv, place_idx, *, name, tr=256):
    _, h, c = p.shape
    tr = min(tr, h)
    nt = h // tr

    def body(pi_ref, p_ref, r_ref, o_ref):
        del pi_ref
        o_ref[...] = ((p_ref[...] + r_ref[0]) + r_ref[1]) + r_ref[2]

    return pl.pallas_call(
        body, name=name,
        grid_spec=pltpu.PrefetchScalarGridSpec(
            num_scalar_prefetch=1, grid=(nt,),
            in_specs=[pl.BlockSpec((None, tr, c), lambda i, pi: (pi[0], i, 0)),
                      pl.BlockSpec((3, tr, c), lambda i, pi: (0, i, 0))],
            out_specs=pl.BlockSpec((tr, c), lambda i, pi: (pi[1] * nt + i, 0))),
        out_shape=jax.ShapeDtypeStruct((2 * h, c), F32),
        compiler_params=_params(("parallel",)),
    )(place_idx, p, recv)


def _place_shard(shard, place_idx, *, name, tr=256, after=()):
    r, c = shard.shape
    tr = min(tr, r)

    def body(pi_ref, s_ref, *rest):
        del pi_ref
        rest[-1][...] = s_ref[...]

    return pl.pallas_call(
        body, name=name,
        grid_spec=pltpu.PrefetchScalarGridSpec(
            num_scalar_prefetch=1, grid=(r // tr,),
            in_specs=[pl.BlockSpec((tr, c), lambda i, pi: (i, 0))] + [pl.BlockSpec(memory_space=pl.ANY)] * len(after),
            out_specs=pl.BlockSpec((None, tr, c), lambda i, pi: (pi[0], i, 0))),
        out_shape=jax.ShapeDtypeStruct((4, r, c), shard.dtype),
        compiler_params=_params(("parallel",)),
    )(place_idx, shard, *after)


def _place():
    x, y, c = lax.axis_index("x"), lax.axis_index("y"), lax.axis_index("c")
    chips = [(1 - x, y), (x, 1 - y), (1 - x, 1 - y)]
    return x, y, c, chips


ANY = pl.BlockSpec(memory_space=pl.ANY)


def _exchange_halves(grads, name):
    n = len(grads)

    def body(*refs):
        ins, outs = refs[:n], refs[n:2 * n]
        send_sems, recv_sems = refs[2 * n:]
        x, y, c, _ = _place()

        def copy(a):
            h = ins[a].shape[1] // 2
            return pltpu.make_async_remote_copy(
                src_ref=ins[a].at[:, pl.ds((1 - c) * h, h), :], dst_ref=outs[a],
                send_sem=send_sems.at[a], recv_sem=recv_sems.at[a], device_id=(x, y, 1 - c), device_id_type=MESH)

        for a in range(n):
            copy(a).start()
        for a in range(n):
            copy(a).wait_recv()
        for a in range(n):
            copy(a).wait_send()

    return pl.pallas_call(
        body, name=name,
        in_specs=[ANY] * n, out_specs=tuple([ANY] * n),
        out_shape=tuple(jax.ShapeDtypeStruct((4, g.shape[1] // 2, g.shape[2]), g.dtype) for g in grads),
        scratch_shapes=[pltpu.SemaphoreType.DMA((n,)), pltpu.SemaphoreType.DMA((n,))],
    )(*grads)


HBM = pl.BlockSpec(memory_space=pltpu.HBM)
SEM = pl.BlockSpec(memory_space=pltpu.SEMAPHORE)
EFFECT = pltpu.SideEffectType.DATAFLOW_SIDE_EFFECTING


def _in_hbm(a):
    return pltpu.with_memory_space_constraint(a, pltpu.HBM)


def _split_copy_calls(name, srcs, lands, n_copies, make_copies):
    ns, nl = len(srcs), len(lands)
    nb = ns + nl

    def start(after=()):
        n_after = len(after)

        def body(*refs):
            outs = refs[nb + n_after:]
            copies = make_copies(refs[:ns], refs[ns:nb], outs[0], outs[1])
            for cp in copies:
                cp.start()
            token = refs[-1]
            token[...] = jnp.zeros_like(token)

        bufs = [_in_hbm(a) for a in list(srcs) + list(lands)]
        out = pl.pallas_call(
            body, name=name + "_start",
            out_shape=(pltpu.SemaphoreType.DMA((n_copies,)), pltpu.SemaphoreType.DMA((n_copies,)),
                       *[pltpu.HBM(a.shape, a.dtype) for a in bufs], jax.ShapeDtypeStruct((8, 128), F32)),
            in_specs=[HBM] * nb + [pl.BlockSpec(memory_space=pl.ANY)] * n_after,
            out_specs=(SEM, SEM, *[HBM] * nb, pl.BlockSpec(memory_space=pltpu.VMEM)),
            input_output_aliases={i: 2 + i for i in range(nb)},
            compiler_params=pltpu.CompilerParams(has_side_effects=EFFECT),
        )(*bufs, *after)
        return dict(send=out[0], recv=out[1], bufs=list(out[2:2 + nb]), token=out[-1])

    def wait(state, after):
        def body(*refs):
            copies = make_copies(refs[:ns], refs[ns:nb], refs[nb], refs[nb + 1])
            for cp in copies:
                cp.wait_send()
            for cp in copies:
                cp.wait_recv()

        bufs = state["bufs"]
        out = pl.pallas_call(
            body, name=name + "_wait",
            out_shape=tuple(pltpu.HBM(a.shape, a.dtype) for a in bufs),
            in_specs=[HBM] * nb + [SEM, SEM, pl.BlockSpec(memory_space=pl.ANY)], out_specs=tuple([HBM] * nb),
            input_output_aliases={i: i for i in range(nb)},
            compiler_params=pltpu.CompilerParams(has_side_effects=EFFECT),
        )(*bufs, state["send"], state["recv"], after)
        return list(out[:ns]), list(out[ns:])

    return start, wait


def _scatter_chips_split(name, parts):
    n = len(parts)
    lands = [lax.empty((3,) + p.shape[1:], p.dtype) for p in parts]

    def make_copies(srcs, lnds, send_sems, recv_sems):
        _, _, c, chips = _place()
        return [pltpu.make_async_remote_copy(
            src_ref=srcs[a].at[2 * px + py], dst_ref=lnds[a].at[j], send_sem=send_sems.at[a * 3 + j],
            recv_sem=recv_sems.at[a * 3 + j], device_id=(px, py, c), device_id_type=MESH)
            for a in range(n) for j, (px, py) in enumerate(chips)]

    return _split_copy_calls(name, parts, lands, 3 * n, make_copies)


def _gather_chips_split(name, shards, lands):
    n = len(shards)

    def make_copies(srcs, lnds, send_sems, recv_sems):
        x, y, c, chips = _place()
        out = []
        for a in range(n):
            h = srcs[a].shape[0] // 2
            for j, (px, py) in enumerate(chips):
                out.append(pltpu.make_async_remote_copy(
                    src_ref=srcs[a].at[pl.ds(c * h, h), :], dst_ref=lnds[a].at[2 * x + y, pl.ds(c * h, h), :],
                    send_sem=send_sems.at[a * 3 + j], recv_sem=recv_sems.at[a * 3 + j],
                    device_id=(px, py, c), device_id_type=MESH))
        return out

    return _split_copy_calls(name, shards, lands, 3 * n, make_copies)


def _gather_finish(gathered, name):
    n = len(gathered)

    def body(*refs):
        outs = refs[n:2 * n]
        send_sems, recv_sems = refs[2 * n:]
        x, y, c, chips = _place()

        def copy(a, j, chip_idx, which):
            h = outs[a].shape[1] // 2
            rows = outs[a].at[chip_idx, pl.ds(which * h, h), :]
            return pltpu.make_async_remote_copy(
                src_ref=rows, dst_ref=rows, send_sem=send_sems.at[a * 3 + j], recv_sem=recv_sems.at[a * 3 + j],
                device_id=(x, y, 1 - c), device_id_type=MESH)

        for a in range(n):
            for j, (px, py) in enumerate(chips):
                copy(a, j, 2 * px + py, c).start()
        for a in range(n):
            for j, (px, py) in enumerate(chips):
                copy(a, j, 2 * px + py, 1 - c).wait_recv()
        for a in range(n):
            for j, (px, py) in enumerate(chips):
                copy(a, j, 2 * px + py, c).wait_send()

    return pl.pallas_call(
        body, name=name,
        in_specs=[ANY] * n, out_specs=tuple([ANY] * n),
        out_shape=tuple(jax.ShapeDtypeStruct(g.shape, g.dtype) for g in gathered),
        input_output_aliases={i: i for i in range(n)},
        scratch_shapes=[pltpu.SemaphoreType.DMA((3 * n,)), pltpu.SemaphoreType.DMA((3 * n,))],
    )(*gathered)


def _join_halves(fulls):
    n = len(fulls)

    def body(*refs):
        outs = refs[n:2 * n]
        send_sems, recv_sems = refs[2 * n:]
        x, y, c, _ = _place()

        def copy(a, which):
            h = outs[a].shape[0] // 2
            rows = outs[a].at[pl.ds(which * h, h), :]
            return pltpu.make_async_remote_copy(
                src_ref=rows, dst_ref=rows, send_sem=send_sems.at[a], recv_sem=recv_sems.at[a],
                device_id=(x, y, 1 - c), device_id_type=MESH)

        for a in range(n):
            copy(a, c).start()
        for a in range(n):
            copy(a, 1 - c).wait_recv()
        for a in range(n):
            copy(a, c).wait_send()

    return pl.pallas_call(
        body, name="rs_join_halves",
        in_specs=[ANY] * n, out_specs=tuple([ANY] * n),
        out_shape=tuple(jax.ShapeDtypeStruct(p.shape, p.dtype) for p in fulls),
        input_output_aliases={i: i for i in range(n)},
        scratch_shapes=[pltpu.SemaphoreType.DMA((n,)), pltpu.SemaphoreType.DMA((n,))],
    )(*fulls)


def _all_reduce_small(sm):
    r, w = sm.shape

    def body(sm_ref, o_ref, buf, send_sems, recv_sems):
        x, y, c, _ = _place()
        me = 4 * x + 2 * y + c
        buf[me] = sm_ref[...]
        rel = [(dx, dy, dc) for dx in (0, 1) for dy in (0, 1) for dc in (0, 1)][1:]

        def copy(k, slot, to):
            return pltpu.make_async_remote_copy(
                src_ref=sm_ref, dst_ref=buf.at[slot], send_sem=send_sems.at[k], recv_sem=recv_sems.at[k],
                device_id=to, device_id_type=MESH)

        peers = []
        for k, (dx, dy, dc) in enumerate(rel):
            px = 1 - x if dx else x
            py = 1 - y if dy else y
            pc = 1 - c if dc else c
            peers.append((px, py, pc))
            copy(k, me, (px, py, pc)).start()
        for k, (px, py, pc) in enumerate(peers):
            copy(k, 4 * px + 2 * py + pc, (px, py, pc)).wait_recv()
        for k, (px, py, pc) in enumerate(peers):
            copy(k, me, (px, py, pc)).wait_send()
        acc = buf[0]
        for d in range(1, 8):
            acc = acc + buf[d]
        o_ref[...] = acc

    vm = pl.BlockSpec(memory_space=pltpu.VMEM)
    return pl.pallas_call(
        body, name="all_reduce_small", in_specs=[vm], out_specs=vm,
        out_shape=jax.ShapeDtypeStruct((r, w), F32),
        scratch_shapes=[pltpu.VMEM((8, r, w), F32), pltpu.SemaphoreType.DMA((7,)), pltpu.SemaphoreType.DMA((7,))],
    )(sm)


class _LocalWeights:
    def __init__(self, w):
        self.w = w
        self.g = {}

    def begin(self):
        pass

    def first(self, after):
        del after
        return self.w

    def rest(self, after):
        del after
        return self.w

    def grads(self, tag, g):
        del tag
        self.g.update(g)
        return None


def _local_step(x3, mem3, pos2, target3, small, comm):
    bsz, seq, d = x3.shape
    mlen = mem3.shape[1]
    t = bsz * seq
    comm.begin()
    x = x3.reshape(t, d)
    mem = mem3.reshape(bsz * mlen, d)
    target = target3.reshape(t, d)
    pos = pos2.reshape(t, 1)
    qg_t = jnp.tile(small["sw_q_norm_g"], (1, SW_HEADS))
    kg_t = jnp.tile(small["sw_k_norm_g"], (1, SW_KV_HEADS))

    hn1 = _rms_fwd(x, small["norm1_g"], name="rms1_fwd")
    w = comm.first(hn1)
    proj_hg = _mm(hn1, w["w_in_hg"], NN, t, HG_COLS, d, name="proj_hg", tk=d, after=(w.get("token"),))[0]
    proj_sw = _mm(hn1, w["w_in_sw"], NN, t, SW_COLS, d, name="proj_sw", tk=d)[0]
    y_mix, o_hg, states = _hg_fwd(proj_hg, small["hg_lower_bounds"], small["hg_norm_g"], bsz, seq, y_width=1024)
    y_mix = _sw_fwd(proj_sw, pos, qg_t, kg_t, small["sw_sinks"], y_mix, bsz, seq)
    w_in_hg, w_in_sw = w["w_in_hg"], w["w_in_sw"]
    w = comm.rest(y_mix)
    ff = w["down"].shape[0]
    ffs = ff // 4
    h1 = _mm(y_mix, w["w_out"], NN, t, d, 1024, name="out_proj", tk=1024, extras=(x,),
             epilogue=lambda acc, res: (acc + res,))[0]
    hn2 = _rms_fwd(h1, small["norm2_g"], name="rms2_fwd")
    mn = _rms_fwd(mem, small["mem_norm_g"], name="rms_mem_fwd")
    qx = _mm(hn2, w["wq"], NN, t, 512, d, name="xa_q", tk=d)[0]
    kvx = _mm(mn, w["wkv"], NN, bsz * mlen, 1024, d, name="xa_kv", tk=d)[0]
    ox = _xa_fwd(qx, kvx, small["xa_q_norm_g"], small["xa_k_norm_g"], bsz, seq, mlen)
    h2 = _mm(ox, w["wo"], NN, t, d, 512, name="xa_o", tk=512, extras=(h1,), epilogue=lambda acc, res: (acc + res,))[0]
    hn3 = _rms_fwd(h2, small["norm3_g"], name="rms3_fwd")

    def relu_sq(acc):
        a = jnp.maximum(acc, 0.0)
        return a, a * a

    act, act2 = _mm(hn3, w["up"], NN, t, ff, d, name="mlp_up", tn=ffs, tk=d,
                    b_spec=pl.BlockSpec((None, d, ffs), lambda i, j, kk: (j, 0, 0)),
                    epilogue=relu_sq, out_dtypes=(_MXU_DTYPE, _MXU_DTYPE))
    inv_d = 1.0 / d
    dy = _mm(act2, w["down"], NN, t, d, ff, name="mlp_down", extras=(h2, target),
             epilogue=lambda acc, res, tgt: ((acc + res - tgt) * inv_d,))[0]
    loss_row = _loss_sum(dy, d)

    dz = _mm(dy, w["down"], NT, t, ff, d, name="d_act", tk=d, extras=(act,),
             epilogue=lambda acc, a: (acc * (2.0 * a.astype(F32)),), out_dtypes=(_MXU_DTYPE,))[0]
    g_down = _mm(act2, dy, TN, ff, d, t, name="g_down")[0]
    g_up = _mm(hn3, dz, TN, d, ff, t, name="g_up", tn=ffs,
               out_shape=(jax.ShapeDtypeStruct((4, d, ffs), F32),),
               out_spec=(pl.BlockSpec((None, min(1024, d), ffs), lambda i, j, kk: (j, i, 0)),))[0]
    tok = comm.grads("mlp", dict(up=g_up, down=g_down))
    dhn3 = _mm(dz, w["up"], NT, t, d, ff, name="d_hn3", tk=ffs, after=(tok,),
               b_spec=pl.BlockSpec((None, min(1024, d), ffs), lambda i, j, kk: (kk, j, 0)))[0]
    dh2, g_norm3 = _rms_bwd(h2, small["norm3_g"], dhn3, dy, name="rms3_bwd")
    d_ox = _mm(dh2, w["wo"], NT, t, 512, d, name="d_ox", tk=d)[0]
    g_wo = _mm(ox, dh2, TN, 512, d, t, name="g_wo")[0]
    d_qx, d_kvx, g_xq, g_xk = _xa_bwd(qx, kvx, small["xa_q_norm_g"], small["xa_k_norm_g"], d_ox, bsz, seq, mlen)
    g_wq = _mm(hn2, d_qx, TN, d, 512, t, name="g_wq")[0]
    g_wkv = _mm(mn, d_kvx, TN, d, 1024, bsz * mlen, name="g_wkv")[0]
    dhn2 = _mm(d_qx, w["wq"], NT, t, d, 512, name="d_hn2", tk=512)[0]
    dmn = _mm(d_kvx, w["wkv"], NT, bsz * mlen, d, 1024, name="d_mn", tk=1024)[0]
    dh1, g_norm2 = _rms_bwd(h1, small["norm2_g"], dhn2, dh2, name="rms2_bwd")
    _, g_memn = _rms_bwd(mem, small["mem_norm_g"], dmn, None, name="rms_mem_bwd")
    g_wout = _mm(y_mix, dh1, TN, 1024, d, t, name="g_wout")[0]
    tok = comm.grads("mid", dict(w_out=g_wout, wq=g_wq, wkv=g_wkv, wo=g_wo))
    d_mix = _mm(dh1, w["w_out"], NT, t, 1024, d, name="d_mix", tk=d, after=(tok,))[0]
    dproj_sw, g_swq, g_swk, g_sinks = _sw_bwd(proj_sw, pos, qg_t, kg_t, small["sw_sinks"], y_mix, d_mix, bsz, seq)
    dproj_hg, g_lb, g_hgn = _hg_bwd(proj_hg, small["hg_lower_bounds"], small["hg_norm_g"], o_hg, states, d_mix, bsz, seq)
    g_in_hg = _mm(hn1, dproj_hg, TN, d, HG_COLS, t, name="g_in_hg")[0]
    g_in_sw = _mm(hn1, dproj_sw, TN, d, SW_COLS, t, name="g_in_sw")[0]
    tok = comm.grads("in", dict(w_in_hg=g_in_hg, w_in_sw=g_in_sw))
    dhn1_a = _mm(dproj_hg, w_in_hg, NT, t, d, HG_COLS, name="d_hn1_hg", tk=1024, after=(tok,))[0]
    dhn1 = _mm(dproj_sw, w_in_sw, NT, t, d, SW_COLS, name="d_hn1_sw", tk=SW_COLS, extras=(dhn1_a,),
               epilogue=lambda acc, prev: (acc + prev,))[0]
    grad_x, g_norm1 = _rms_bwd(x, small["norm1_g"], dhn1, dh1, name="rms1_bwd")

    g_small = dict(norm1_g=g_norm1, hg_lower_bounds=g_lb, hg_norm_g=g_hgn, sw_q_norm_g=g_swq, sw_k_norm_g=g_swk,
                   sw_sinks=g_sinks[:, 0:SW_HEADS], norm2_g=g_norm2, mem_norm_g=g_memn, xa_q_norm_g=g_xq,
                   xa_k_norm_g=g_xk, norm3_g=g_norm3)
    return loss_row, grad_x.reshape(bsz, seq, d), g_small


SMALL_NAMES = ("norm1_g", "hg_lower_bounds", "hg_norm_g", "sw_q_norm_g", "sw_k_norm_g", "sw_sinks", "norm2_g",
               "mem_norm_g", "xa_q_norm_g", "xa_k_norm_g", "norm3_g")
BIG_NAMES = ("w_in", "w_out", "xa_wq", "xa_wkv", "xa_wo", "mlp_up", "mlp_down")
WEIGHT_ORDER = ("norm1_g", "w_in", "hg_lower_bounds", "hg_norm_g", "sw_q_norm_g", "sw_k_norm_g", "sw_sinks", "w_out",
                "norm2_g", "mem_norm_g", "xa_wq", "xa_wkv", "xa_q_norm_g", "xa_k_norm_g", "xa_wo", "norm3_g",
                "mlp_up", "mlp_down")


def _pack_rows(vals, width):
    starts, at = [], 0
    for v in vals:
        starts.append(at)
        at += v.shape[0]
    total = at + (-at) % 8
    out = None
    for v, s in zip(vals, starts):
        placed = jnp.pad(v, ((s, total - s - v.shape[0]), (0, width - v.shape[1])))
        out = placed if out is None else out + placed
    return out, starts


class _MeshWeights:
    LATE = ("w_out", "xa_wq", "xa_wkv", "xa_wo", "mlp_up", "mlp_down")

    def __init__(self, shards, d, ff):
        self.shards, self.d, self.ff = shards, d, ff
        self.c_idx = lax.axis_index("c").astype(jnp.int32).reshape(1)
        chip = (2 * lax.axis_index("x") + lax.axis_index("y")).astype(jnp.int32)
        self.place_idx = jnp.stack([chip, lax.axis_index("c").astype(jnp.int32)])
        self.pending = []
        self.halves = {}

    def begin(self):
        shard = self.shards["w_in"]
        start, self.in_wait = _gather_chips_split(
            "gather_in", [shard], [_place_shard(shard, self.place_idx, name="place_w_in")])
        self.in_state = start()
        tok = (self.in_state["token"],)
        self.placed = [_place_shard(self.shards[n], self.place_idx, name="place_" + n, after=tok) for n in self.LATE]

    def first(self, after):
        _, lands = self.in_wait(self.in_state, after)
        (g_in,) = _gather_finish(lands, "gather_in_finish")
        start, self.late_wait = _gather_chips_split("gather_late", [self.shards[n] for n in self.LATE], self.placed)
        self.late_state = start(after=(g_in,))
        full = jnp.concatenate([g_in[k] for k in range(4)], axis=1)
        return dict(w_in_hg=full[:, :HG_COLS], w_in_sw=full[:, HG_COLS:], token=self.late_state["token"])

    def rest(self, after):
        _, lands = self.late_wait(self.late_state, after)
        g_out, g_q, g_kv, g_o, g_up, g_dn = _gather_finish(lands, "gather_late_finish")
        d = self.d
        return dict(w_out=g_out.reshape(-1, d), wq=g_q.reshape(d, -1), wkv=g_kv.reshape(d, -1),
                    wo=jnp.concatenate([g_o[k] for k in range(4)], axis=1), up=g_up, down=g_dn.reshape(self.ff, d))

    def _chip_partials(self, tag, names, arrays):
        recv = _exchange_halves(arrays, "rs_exchange_" + tag)
        return [_add_halves(g, r, self.c_idx, name="rs_add_halves_" + n) for n, g, r in zip(names, arrays, recv)]

    def grads(self, tag, g):
        d, ff = self.d, self.ff
        if tag == "mlp":
            names, arrays = ("mlp_up", "mlp_down"), [g["up"], g["down"].reshape(4, ff // 4, d)]
        elif tag == "mid":
            names = ("w_out", "xa_wq", "xa_wkv", "xa_wo")
            ds = d // 4
            g_wo = jnp.stack([g["wo"][:, ds * k:ds * (k + 1)] for k in range(4)])
            arrays = [g["w_out"].reshape(4, -1, d), g["wq"].reshape(4, d // 4, -1), g["wkv"].reshape(4, d // 4, -1), g_wo]
        else:
            full = jnp.concatenate([g["w_in_hg"], g["w_in_sw"]], axis=1)
            ws = full.shape[1] // 4
            names, arrays = ("w_in",), [jnp.stack([full[:, ws * k:ws * (k + 1)] for k in range(4)])]
        parts = self._chip_partials(tag, names, arrays)
        start, wait = _scatter_chips_split("rs_scatter_" + tag, parts)
        state = start()
        self.pending.append((names, wait, state))
        return state["token"]

    def finish(self, after):
        for names, wait, state in self.pending:
            srcs, lands = wait(state, after)
            for n, p, r in zip(names, srcs, lands):
                self.halves[n] = _add_chips(p, r, self.place_idx, name="rs_add_chips_" + n)
        return dict(zip(BIG_NAMES, _join_halves([self.halves[n] for n in BIG_NAMES])))


def kernel(x, mem, positions, norm1_g, w_in, hg_lower_bounds, hg_norm_g, sw_q_norm_g, sw_k_norm_g, sw_sinks, w_out, norm2_g, mem_norm_g, xa_wq, xa_wkv, xa_q_norm_g, xa_k_norm_g, xa_wo, norm3_g, mlp_up, mlp_down, loss_target, m_norm1_g, m_w_in, m_hg_lower_bounds, m_hg_norm_g, m_sw_q_norm_g, m_sw_k_norm_g, m_sw_sinks, m_w_out, m_norm2_g, m_mem_norm_g, m_xa_wq, m_xa_wkv, m_xa_q_norm_g, m_xa_k_norm_g, m_xa_wo, m_norm3_g, m_mlp_up, m_mlp_down, v_norm1_g, v_w_in, v_hg_lower_bounds, v_hg_norm_g, v_sw_q_norm_g, v_sw_k_norm_g, v_sw_sinks, v_w_out, v_norm2_g, v_mem_norm_g, v_xa_wq, v_xa_wkv, v_xa_q_norm_g, v_xa_k_norm_g, v_xa_wo, v_norm3_g, v_mlp_up, v_mlp_down):
    given = dict(locals())
    weights = {n: given[n] for n in WEIGHT_ORDER}
    moms = {n: given["m_" + n] for n in WEIGHT_ORDER}
    vars_ = {n: given["v_" + n] for n in WEIGHT_ORDER}
    d = x.shape[-1]
    ff = mlp_down.shape[1] * 4
    small = {n: weights[n] for n in SMALL_NAMES}

    comm = _MeshWeights({n: weights[n][0].astype(_MXU_DTYPE) for n in BIG_NAMES}, d, ff)
    loss_row, grad_x, g_small = _local_step(x, mem, positions, loss_target, small, comm)
    big_grads = comm.finish(grad_x)

    packed, starts = _pack_rows([g_small[n] for n in SMALL_NAMES] + [loss_row], 1024)
    summed = _all_reduce_small(packed)
    small_grads = {}
    for n, s in zip(SMALL_NAMES, starts):
        r, c = weights[n].shape
        small_grads[n] = summed[s:s + r, 0:c]
    loss = summed[starts[-1], 0]

    grads, deltas, new_m, new_v = {}, {}, {}, {}
    for n in BIG_NAMES:
        shp = weights[n].shape
        g2 = big_grads[n]
        dl, mo, vo = _adamw_big(weights[n][0], g2, moms[n][0], vars_[n][0], name="adamw_" + n)
        grads[n], deltas[n], new_m[n], new_v[n] = (a.reshape(shp) for a in (g2, dl, mo, vo))
    sm_out = _adamw_small([weights[n] for n in SMALL_NAMES], [small_grads[n] for n in SMALL_NAMES],
                          [moms[n] for n in SMALL_NAMES], [vars_[n] for n in SMALL_NAMES])
    ns = len(SMALL_NAMES)
    for i, n in enumerate(SMALL_NAMES):
        grads[n], deltas[n], new_m[n], new_v[n] = small_grads[n], sm_out[i], sm_out[ns + i], sm_out[2 * ns + i]

    return (loss, grad_x, *[grads[n] for n in WEIGHT_ORDER], *[deltas[n] for n in WEIGHT_ORDER],
            *[new_m[n] for n in WEIGHT_ORDER], *[new_v[n] for n in WEIGHT_ORDER])
```

```python
import numpy as np
import jax
import jax.numpy as jnp
from jax import lax
from jax.experimental import pallas as pl
from jax.experimental.pallas import tpu as pltpu

F32 = jnp.float32
_MXU_DTYPE = jnp.bfloat16

EPS = 1e-6
HG_HEADS = 4
HG_D = 128
HG_CHUNK = 64
HG_TILE = 512
HG_LEVELS = (32, 16, 8, 4, 2, 1)
SW_HEADS = 8
SW_KV_HEADS = 2
SW_GROUP = SW_HEADS // SW_KV_HEADS
SW_HD = 64
SW_BLOCK = 128
ROPE_THETA = 500000.0
ROT_DIM = SW_HD // 4
XA_HEADS = 4
XA_HD = 128
HG_COLS = 4 * HG_HEADS * HG_D
SW_COLS = (SW_HEADS + 2 * SW_KV_HEADS) * SW_HD

ADAM_LR = 0.001
ADAM_B1 = 0.9
ADAM_B2 = 0.999
ADAM_EPS = 1e-08
ADAM_WD = 0.01
ADAM_STEP = 10

VMEM_LIMIT = 56 * 1024 * 1024
MESH = pl.DeviceIdType.MESH

NN = ((1,), (0,))
NT = ((1,), (1,))
TN = ((0,), (0,))


def _mx(v):
    return v.astype(_MXU_DTYPE)


def _dot(a, b, dims=NN):
    return lax.dot_general(_mx(a), _mx(b), (dims, ((), ())), preferred_element_type=F32)


def _split_dot(a, v, dims, parts):
    acc = None
    rest = v
    for p in range(parts):
        piece = _mx(rest)
        term = lax.dot_general(a, piece, (dims, ((), ())), preferred_element_type=F32)
        acc = term if acc is None else acc + term
        if p + 1 < parts:
            rest = rest - piece.astype(F32)
    return acc


def _params(sem):
    return pltpu.CompilerParams(dimension_semantics=sem, vmem_limit_bytes=VMEM_LIMIT)


def _mm(a, b, mode, m, n, k, *, name, tm=1024, tn=1024, tk=1024, a_spec=None, b_spec=None, extras=(), epilogue=None,
        out_dtypes=(F32,), out_shape=None, out_spec=None, after=()):
    after = tuple(t for t in after if t is not None)
    tm, tn, tk = min(tm, m), min(tn, n), min(tk, k)
    assert m % tm == 0 and n % tn == 0 and k % tk == 0, (name, m, n, k, tm, tn, tk)
    gi, gj, gk = m // tm, n // tn, k // tk
    if a_spec is None:
        a_spec = (pl.BlockSpec((tk, tm), lambda i, j, kk: (kk, i)) if mode == TN
                  else pl.BlockSpec((tm, tk), lambda i, j, kk: (i, kk)))
    if b_spec is None:
        b_spec = (pl.BlockSpec((tn, tk), lambda i, j, kk: (j, kk)) if mode == NT
                  else pl.BlockSpec((tk, tn), lambda i, j, kk: (kk, j)))
    mn_spec = pl.BlockSpec((tm, tn), lambda i, j, kk: (i, j))
    if epilogue is None:
        epilogue = lambda acc: (acc,)
    n_ex, n_out = len(extras), len(out_dtypes)
    if out_shape is None:
        out_shape = tuple(jax.ShapeDtypeStruct((m, n), d) for d in out_dtypes)
        out_spec = tuple(mn_spec for _ in out_dtypes)

    n_after = len(after)

    def body(*refs):
        a_ref, b_ref = refs[0], refs[1]
        ex = refs[2:2 + n_ex]
        outs = refs[2 + n_ex + n_after:2 + n_ex + n_after + n_out]

        def finish(acc):
            res = epilogue(acc, *[e[...] for e in ex])
            for o, r in zip(outs, res):
                o[...] = r.astype(o.dtype)

        if gk == 1:
            finish(_dot(a_ref[...], b_ref[...], mode))
        else:
            acc_ref = refs[-1]
            kk = pl.program_id(2)

            @pl.when(kk == 0)
            def _():
                acc_ref[...] = jnp.zeros_like(acc_ref)

            acc_ref[...] += _dot(a_ref[...], b_ref[...], mode)

            @pl.when(kk == gk - 1)
            def _():
                finish(acc_ref[...])

    return pl.pallas_call(
        body, name=name, grid=(gi, gj, gk),
        in_specs=[a_spec, b_spec] + [mn_spec] * n_ex + [pl.BlockSpec(memory_space=pl.ANY)] * n_after,
        out_specs=out_spec, out_shape=out_shape,
        scratch_shapes=[pltpu.VMEM((tm, tn), F32)] if gk > 1 else [],
        compiler_params=_params(("parallel", "parallel", "arbitrary")),
    )(a, b, *extras, *after)


def _rms_fwd(x, g, *, name, tm=512):
    t, d = x.shape
    tm = min(tm, t)

    def body(x_ref, g_ref, o_ref):
        xv = x_ref[...]
        r = lax.rsqrt(jnp.mean(xv * xv, axis=1, keepdims=True) + EPS)
        o_ref[...] = (xv * r * g_ref[...]).astype(o_ref.dtype)

    return pl.pallas_call(
        body, name=name, grid=(t // tm,),
        in_specs=[pl.BlockSpec((tm, d), lambda i: (i, 0)), pl.BlockSpec((1, d), lambda i: (0, 0))],
        out_specs=pl.BlockSpec((tm, d), lambda i: (i, 0)),
        out_shape=jax.ShapeDtypeStruct((t, d), _MXU_DTYPE),
        compiler_params=_params(("parallel",)),
    )(x, g)


def _rms_bwd(x, g, dy, dres, *, name, tm=512, operand_copy=False):
    t, d = x.shape
    tm = min(tm, t)
    has_res = dres is not None

    def body(*refs):
        x_ref, g_ref, dy_ref = refs[:3]
        dx_ref, dg_ref = refs[3 + has_res], refs[-1]
        xv, dyv = x_ref[...], dy_ref[...]
        r = lax.rsqrt(jnp.mean(xv * xv, axis=1, keepdims=True) + EPS)
        u = dyv * g_ref[...]
        dx = r * u - xv * (r * r * r) * jnp.mean(u * xv, axis=1, keepdims=True)
        if has_res:
            dx = dx + refs[3][...]
        dx_ref[...] = dx
        if operand_copy:
            refs[-2][...] = _mx(dx)

        @pl.when(pl.program_id(0) == 0)
        def _():
            dg_ref[...] = jnp.zeros_like(dg_ref)

        dg_ref[...] += jnp.sum(dyv * xv * r, axis=0, keepdims=True)

    row = pl.BlockSpec((tm, d), lambda i: (i, 0))
    vec = pl.BlockSpec((1, d), lambda i: (0, 0))
    copy = [jax.ShapeDtypeStruct((t, d), _MXU_DTYPE)] if operand_copy else []
    return pl.pallas_call(
        body, name=name, grid=(t // tm,),
        in_specs=[row, vec, row] + ([row] if has_res else []),
        out_specs=tuple([row] * (1 + len(copy)) + [vec]),
        out_shape=tuple([jax.ShapeDtypeStruct((t, d), F32)] + copy + [jax.ShapeDtypeStruct((1, d), F32)]),
        compiler_params=_params(("arbitrary",)),
    )(*([x, g, dy] + ([dres] if has_res else [])))


def _hg_constants():
    c = HG_CHUNK
    t = np.arange(c)
    sums = [t[None, :] <= t[:, None]]
    masks = []
    for m in HG_LEVELS:
        base = (t // (2 * m)) * (2 * m)
        mid = base + m - 1
        second = (t - base) >= m
        upper = (t[None, :] > mid[:, None]) & (t[None, :] <= t[:, None])
        lower = (t[None, :] > t[:, None]) & (t[None, :] <= mid[:, None])
        sums.append(np.where(second[:, None], upper, lower))
        masks.append(second[:, None] & (~second)[None, :] & (base[:, None] == base[None, :]))
    return (np.concatenate(sums, axis=0).astype(np.float32), np.stack(masks).astype(np.float32))


HG_HEAD_LANES = tuple(slice(HG_D * h, HG_D * (h + 1)) for h in range(HG_HEADS))


def _per_head(fn, slab):
    return jnp.concatenate([jnp.broadcast_to(fn(slab[:, hs]), (slab.shape[0], HG_D)) for hs in HG_HEAD_LANES], axis=1)


def _lane_sum(v):
    return jnp.sum(v, axis=1, keepdims=True)


def _lane_mean(v):
    return jnp.mean(v, axis=1, keepdims=True)


def _hg_gates(blk, lbp):
    w = HG_HEADS * HG_D
    q, x, v, gl = blk[:, 0:w], blk[:, w:2 * w], blk[:, 2 * w:3 * w], blk[:, 3 * w:4 * w]
    mx = jnp.max(lbp, axis=0, keepdims=True)
    e = jnp.exp(lbp - mx)
    lb = e[0:1, :] / jnp.sum(e, axis=0, keepdims=True)
    sig = jax.nn.sigmoid(x)
    f = lb + (1.0 - lb) * sig
    return q, v, gl, lb, sig, f, 1.0 - f, jnp.log(f)


def _hg_fwd(proj, lbp, ng, bsz, seq, *, y_width):
    t = proj.shape[0]
    nc = seq // HG_CHUNK
    a_np, m_np = _hg_constants()
    a_all = jnp.asarray(a_np, _MXU_DTYPE)
    masks = jnp.asarray(m_np, F32)
    nl = len(HG_LEVELS)

    ts = min(HG_TILE, seq)
    ns, nct = seq // ts, ts // HG_CHUNK
    hw = HG_HEADS * HG_D

    def body(p_ref, lb_ref, ng_ref, a_ref, m_ref, y_ref, o_ref, st_ref, carry):
        a_mat = a_ref[...]
        ngv = ng_ref[...]

        @pl.when(pl.program_id(1) == 0)
        def _():
            carry[...] = jnp.zeros_like(carry)

        ng4 = _tile_lanes(ngv, HG_HEADS)
        heads = range(HG_HEADS)
        hl = HG_HEAD_LANES

        def chunk(c, _):
            rows = pl.ds(pl.multiple_of(c * HG_CHUNK, HG_CHUNK), HG_CHUNK)
            q, v, gl, lb, sig, f, k, g = _hg_gates(p_ref[rows, :], lb_ref[...])
            sts = [carry[h] for h in heads]
            e_all = _split_dot(a_mat, g, NN, 3)
            b = e_all[0:HG_CHUNK]
            qb = q * jnp.exp(b)
            o = [_dot(qb[:, hl[h]], sts[h], NT) for h in heads]
            p = [jnp.zeros((HG_CHUNK, HG_CHUNK), F32) for _ in heads]
            for li in range(nl):
                e = jnp.exp(e_all[HG_CHUNK * (li + 1):HG_CHUNK * (li + 2)])
                qm, km, mk = q * e, k * e, m_ref[li]
                p = [p[h] + mk * _dot(qm[:, hl[h]], km[:, hl[h]], NT) for h in heads]
            bl = b[HG_CHUNK - 1:HG_CHUNK, :]
            kd = k * jnp.exp(bl - b)
            ebl = jnp.exp(bl)
            pv = [_dot(p[h], v[:, hl[h]]) for h in heads]
            upd = [_dot(v[:, hl[h]], kd[:, hl[h]], TN) for h in heads]
            o_all = jnp.concatenate([o[h] + pv[h] for h in heads], axis=1) + _per_head(_lane_sum, q * k) * v
            r = lax.rsqrt(_per_head(_lane_mean, o_all * o_all) + EPS)
            for h in heads:
                st_ref[h, c] = sts[h]
                carry[h] = sts[h] * ebl[:, hl[h]] + upd[h]
            o_ref[rows, :] = o_all
            y_ref[rows, :] = (o_all * r * ng4) * (gl * jax.nn.sigmoid(gl))
            return 0

        lax.fori_loop(0, nct, chunk, 0)

    return pl.pallas_call(
        body, name="hgrn2_fwd", grid=(bsz, ns),
        in_specs=[pl.BlockSpec((ts, HG_COLS), lambda b, s: (b * ns + s, 0)),
                  pl.BlockSpec((2, hw), lambda b, s: (0, 0)),
                  pl.BlockSpec((1, HG_D), lambda b, s: (0, 0)),
                  pl.BlockSpec(a_all.shape, lambda b, s: (0, 0)),
                  pl.BlockSpec(masks.shape, lambda b, s: (0, 0, 0))],
        out_specs=(pl.BlockSpec((ts, hw), lambda b, s: (b * ns + s, 0)),
                   pl.BlockSpec((ts, hw), lambda b, s: (b * ns + s, 0)),
                   pl.BlockSpec((None, HG_HEADS, nct, HG_D, HG_D), lambda b, s: (b, 0, s, 0, 0))),
        out_shape=(jax.ShapeDtypeStruct((t, y_width), F32),
                   jax.ShapeDtypeStruct((t, hw), F32),
                   jax.ShapeDtypeStruct((bsz, HG_HEADS, nc, HG_D, HG_D), F32)),
        scratch_shapes=[pltpu.VMEM((HG_HEADS, HG_D, HG_D), F32)],
        compiler_params=_params(("parallel", "arbitrary")),
    )(proj, lbp, ng, a_all, masks)


def _hg_bwd(proj, lbp, ng, o_all, states, dy, bsz, seq):
    t = proj.shape[0]
    nc = seq // HG_CHUNK
    a_np, m_np = _hg_constants()
    a_all = jnp.asarray(a_np, _MXU_DTYPE)
    masks = jnp.asarray(m_np, F32)
    nl = len(HG_LEVELS)
    cs = HG_CHUNK

    ts = min(HG_TILE, seq)
    ns, nct = seq // ts, ts // cs
    hw = HG_HEADS * HG_D

    def body(p_ref, lb_ref, ng_ref, a_ref, m_ref, o_ref, st_ref, dy_ref, dp_ref, dlb_ref, dng_ref, dst_ref):
        a_mat = a_ref[...]
        ngv = ng_ref[...]
        ng4 = _tile_lanes(ngv, HG_HEADS)
        last_row = lax.broadcasted_iota(jnp.int32, (cs, hw), 0) == cs - 1
        si = pl.program_id(1)
        first = jnp.logical_and(pl.program_id(0) == 0, si == 0)
        heads = range(HG_HEADS)
        hl = HG_HEAD_LANES

        @pl.when(si == 0)
        def _():
            dst_ref[...] = jnp.zeros_like(dst_ref)

        def side_by_side(parts):
            return jnp.concatenate(parts, axis=1)

        def chunk(i, carry):
            dlb_acc, dng_acc = carry
            c = nct - 1 - i
            rows = pl.ds(pl.multiple_of(c * cs, cs), cs)
            q, v, gl, lb, sig, f, k, g = _hg_gates(p_ref[rows, :], lb_ref[...])
            o = o_ref[rows, :]
            dyv = dy_ref[rows, :]
            sts = [st_ref[h, c] for h in heads]
            dsts = [dst_ref[h] for h in heads]
            e_all = _split_dot(a_mat, g, NN, 3)
            b = e_all[0:cs]
            eb = jnp.exp(b)
            bl = b[cs - 1:cs, :]
            ebl = jnp.exp(bl)
            ekd = jnp.exp(bl - b)
            qb, kd = q * eb, k * ekd
            sg = jax.nn.sigmoid(gl)
            silu = gl * sg
            r = lax.rsqrt(_per_head(_lane_mean, o * o) + EPS)
            dgl = dyv * (o * r * ng4) * (sg * (1.0 + gl * (1.0 - sg)))
            u = dyv * silu * ng4
            do = r * u - o * (r * r * r) * _per_head(_lane_mean, u * o)
            dng4 = jnp.sum(dyv * silu * o * r, axis=0, keepdims=True)
            dng_acc = dng_acc + ((dng4[:, hl[0]] + dng4[:, hl[1]]) + (dng4[:, hl[2]] + dng4[:, hl[3]]))
            es, qm, km = [], [], []
            p = [jnp.zeros((cs, cs), F32) for _ in heads]
            for li in range(nl):
                e = jnp.exp(e_all[cs * (li + 1):cs * (li + 2)])
                es.append(e)
                qm.append(q * e)
                km.append(k * e)
                mk = m_ref[li]
                p = [p[h] + mk * _dot(qm[li][:, hl[h]], km[li][:, hl[h]], NT) for h in heads]
            dp = [_dot(do[:, hl[h]], v[:, hl[h]], NT) for h in heads]
            dv_p = [_dot(p[h], do[:, hl[h]], TN) for h in heads]
            dv_s = [_dot(kd[:, hl[h]], dsts[h], NT) for h in heads]
            dqb = side_by_side([_dot(do[:, hl[h]], sts[h]) for h in heads])
            dkd = side_by_side([_dot(v[:, hl[h]], dsts[h]) for h in heads])
            new_dst = [_dot(do[:, hl[h]], qb[:, hl[h]], TN) for h in heads]
            dv = side_by_side([dv_p[h] + dv_s[h] for h in heads]) + _per_head(_lane_sum, q * k) * do
            dq = dqb * eb
            dk = dkd * ekd
            db = dqb * qb - dkd * kd
            dbl = (jnp.sum(dkd * kd, axis=0, keepdims=True)
                   + side_by_side([jnp.sum(dsts[h] * sts[h], axis=0, keepdims=True) for h in heads]) * ebl)
            de = [db + jnp.where(last_row, dbl, 0.0)]
            for li in range(nl):
                mk = m_ref[li]
                dpm = [mk * dp[h] for h in heads]
                dqm = side_by_side([_dot(dpm[h], km[li][:, hl[h]]) for h in heads])
                dkm = side_by_side([_dot(dpm[h], qm[li][:, hl[h]], TN) for h in heads])
                dq = dq + dqm * es[li]
                dk = dk + dkm * es[li]
                de.append(dqm * qm[li] + dkm * km[li])
            dpd = _per_head(_lane_sum, do * v)
            dq = dq + dpd * k
            dk = dk + dpd * q
            dg = _split_dot(a_mat, jnp.concatenate(de, axis=0), TN, 2)
            df = dg / f - dk
            dp_ref[rows, 0:hw] = _mx(dq)
            dp_ref[rows, hw:2 * hw] = _mx(df * (1.0 - lb) * sig * (1.0 - sig))
            dp_ref[rows, 2 * hw:3 * hw] = _mx(dv)
            dp_ref[rows, 3 * hw:4 * hw] = _mx(dgl)
            for h in heads:
                dst_ref[h] = dsts[h] * ebl[:, hl[h]] + new_dst[h]
            return dlb_acc + jnp.sum(df * (1.0 - sig), axis=0, keepdims=True), dng_acc

        dlb, dng = lax.fori_loop(0, nct, chunk, (jnp.zeros((1, hw), F32), jnp.zeros((1, HG_D), F32)))

        @pl.when(first)
        def _():
            dlb_ref[...] = jnp.zeros_like(dlb_ref)
            dng_ref[...] = jnp.zeros_like(dng_ref)

        lbp_v = lb_ref[...]
        mx = jnp.max(lbp_v, axis=0, keepdims=True)
        e = jnp.exp(lbp_v - mx)
        s0 = e[0:1, :] / jnp.sum(e, axis=0, keepdims=True)
        da0 = dlb * s0 * (1.0 - s0)
        dlb_ref[...] += jnp.concatenate([da0, -da0], axis=0)
        dng_ref[...] += dng

    def tile(b, s):
        return b * ns + (ns - 1 - s)

    return pl.pallas_call(
        body, name="hgrn2_bwd", grid=(bsz, ns),
        in_specs=[pl.BlockSpec((ts, HG_COLS), lambda b, s: (tile(b, s), 0)),
                  pl.BlockSpec((2, hw), lambda b, s: (0, 0)),
                  pl.BlockSpec((1, HG_D), lambda b, s: (0, 0)),
                  pl.BlockSpec(a_all.shape, lambda b, s: (0, 0)),
                  pl.BlockSpec(masks.shape, lambda b, s: (0, 0, 0)),
                  pl.BlockSpec((ts, hw), lambda b, s: (tile(b, s), 0)),
                  pl.BlockSpec((None, HG_HEADS, nct, HG_D, HG_D), lambda b, s: (b, 0, ns - 1 - s, 0, 0)),
                  pl.BlockSpec((ts, hw), lambda b, s: (tile(b, s), 0))],
        out_specs=(pl.BlockSpec((ts, HG_COLS), lambda b, s: (tile(b, s), 0)),
                   pl.BlockSpec((2, hw), lambda b, s: (0, 0)),
                   pl.BlockSpec((1, HG_D), lambda b, s: (0, 0))),
        out_shape=(jax.ShapeDtypeStruct((t, HG_COLS), _MXU_DTYPE),
                   jax.ShapeDtypeStruct((2, hw), F32),
                   jax.ShapeDtypeStruct((1, HG_D), F32)),
        scratch_shapes=[pltpu.VMEM((HG_HEADS, HG_D, HG_D), F32)],
        compiler_params=_params(("arbitrary", "arbitrary")),
    )(proj, lbp, ng, a_all, masks, o_all, states, dy)


def _sw_constants():
    half = ROT_DIM // 2
    inv = (np.float32(ROPE_THETA) ** (-(np.arange(half, dtype=np.float32) * np.float32(2.0) / np.float32(ROT_DIM)))
           ).astype(np.float32)
    freq = np.zeros((1, 128), np.float32)
    sign = np.zeros((1, 128), np.float32)
    for h in range(2):
        freq[0, 64 * h:64 * h + half] = inv
        freq[0, 64 * h + half:64 * h + 2 * half] = inv
        sign[0, 64 * h:64 * h + half] = -1.0
        sign[0, 64 * h + half:64 * h + 2 * half] = 1.0
    seg = np.kron(np.eye(8, dtype=np.float32), np.full((64, 64), 1.0 / 64.0, np.float32))
    return freq, sign, seg


def _rope_tables(pos, freq, sign):
    ang = pos.astype(F32) * freq
    return jnp.cos(ang), jnp.sin(ang) * sign


def _tile_lanes(v, times):
    return v if times == 1 else jnp.concatenate([v] * times, axis=1)


def _swap_halves(v):
    w = v.shape[1]
    half = ROT_DIM // 2
    lane = lax.broadcasted_iota(jnp.int32, v.shape, 1) % SW_HD
    return jnp.where(lane < half, pltpu.roll(v, w - half, 1), jnp.where(lane < 2 * half, pltpu.roll(v, half, 1), 0.0))


def _sw_norm_rope(tv, gain, seg, cosv, sinv):
    w = tv.shape[1]
    ms = _split_dot_rhs(tv * tv, seg[0:w, 0:w])
    r = lax.rsqrt(ms + EPS)
    tn = tv * r * gain
    reps = w // 128
    return tn * _tile_lanes(cosv, reps) + _swap_halves(tn) * _tile_lanes(sinv, reps), r


def _split_dot_rhs(v, a):
    hi = _mx(v)
    lo = _mx(v - hi.astype(F32))
    return (lax.dot_general(hi, a, (NN, ((), ())), preferred_element_type=F32)
            + lax.dot_general(lo, a, (NN, ((), ())), preferred_element_type=F32))


def _sw_norm_rope_bwd(dt, tv, r, gain, seg, cosv, sinv):
    w = tv.shape[1]
    reps = w // 128
    dtn = dt * _tile_lanes(cosv, reps) + _swap_halves(dt * _tile_lanes(sinv, reps))
    u = dtn * gain
    dtv = r * u - tv * (r * r * r) * _split_dot_rhs(u * tv, seg[0:w, 0:w])
    return dtv, jnp.sum(dtn * tv * r, axis=0, keepdims=True)


def _sw_scores(qh, kp, kc):
    return _dot(qh, kp, NT), _dot(qh, kc, NT)


def _sw_probs(raw, sink, first_block):
    scale = SW_HD ** -0.5
    qi = lax.broadcasted_iota(jnp.int32, (SW_BLOCK, SW_BLOCK), 0)
    kj = lax.broadcasted_iota(jnp.int32, (SW_BLOCK, SW_BLOCK), 1)
    ok_prev = jnp.logical_and(kj > qi, jnp.logical_not(first_block))
    ok_cur = kj <= qi
    sp = jnp.where(ok_prev, raw[0] * scale, -jnp.inf)
    sc = jnp.where(ok_cur, raw[1] * scale, -jnp.inf)
    m = jnp.maximum(jnp.maximum(jnp.max(sp, axis=1, keepdims=True), jnp.max(sc, axis=1, keepdims=True)), sink)
    pp, pc = jnp.exp(sp - m), jnp.exp(sc - m)
    es = jnp.exp(sink - m)
    den = jnp.sum(pp, axis=1, keepdims=True) + jnp.sum(pc, axis=1, keepdims=True) + es
    return pp / den, pc / den, es / den


def _sw_specs(nb):
    def cur(b, n):
        return b * nb + jnp.minimum(n, nb - 1)

    def prev(b, n):
        return b * nb + jnp.maximum(jnp.minimum(n, nb - 1) - 1, 0)

    return cur, prev


def _sw_fwd(proj, pos, qg, kg, sinks, y_in, bsz, seq):
    t = proj.shape[0]
    nb = seq // SW_BLOCK
    freq_np, sign_np, seg_np = _sw_constants()
    freq, sign = jnp.asarray(freq_np), jnp.asarray(sign_np)
    seg = jnp.asarray(seg_np, _MXU_DTYPE)
    cur, prev = _sw_specs(nb)

    def body(q_ref, kc_ref, kp_ref, vc_ref, vp_ref, pc_ref, pp_ref, qg_ref, kg_ref, sk_ref, fr_ref, sn_ref, seg_ref,
             yin_ref, y_ref):
        del yin_ref
        n = pl.program_id(1)
        segv = seg_ref[...]
        cos_c, sin_c = _rope_tables(pc_ref[...], fr_ref[...], sn_ref[...])
        cos_p, sin_p = _rope_tables(pp_ref[...], fr_ref[...], sn_ref[...])
        qr, _ = _sw_norm_rope(q_ref[...], qg_ref[...], segv, cos_c, sin_c)
        kcr, _ = _sw_norm_rope(kc_ref[...], kg_ref[...], segv, cos_c, sin_c)
        kpr, _ = _sw_norm_rope(kp_ref[...], kg_ref[...], segv, cos_p, sin_p)
        vc, vp = vc_ref[...], vp_ref[...]
        ks = [slice(SW_HD * (h // SW_GROUP), SW_HD * (h // SW_GROUP + 1)) for h in range(SW_HEADS)]
        raw = [_sw_scores(qr[:, SW_HD * h:SW_HD * (h + 1)], kpr[:, ks[h]], kcr[:, ks[h]]) for h in range(SW_HEADS)]
        probs = [_sw_probs(raw[h], sk_ref[0, h], n == 0) for h in range(SW_HEADS)]
        for h in range(SW_HEADS):
            y_ref[:, SW_HD * h:SW_HD * (h + 1)] = _dot(probs[h][0], vp[:, ks[h]]) + _dot(probs[h][1], vc[:, ks[h]])

    rowq = pl.BlockSpec((SW_BLOCK, 512), lambda b, n: (cur(b, n), 0))
    full = lambda a: pl.BlockSpec(a.shape, lambda b, n: (0,) * a.ndim)
    yw = y_in.shape[1]
    return pl.pallas_call(
        body, name="swa_fwd", grid=(bsz, nb),
        in_specs=[rowq,
                  pl.BlockSpec((SW_BLOCK, 128), lambda b, n: (cur(b, n), 4)),
                  pl.BlockSpec((SW_BLOCK, 128), lambda b, n: (prev(b, n), 4)),
                  pl.BlockSpec((SW_BLOCK, 128), lambda b, n: (cur(b, n), 5)),
                  pl.BlockSpec((SW_BLOCK, 128), lambda b, n: (prev(b, n), 5)),
                  pl.BlockSpec((SW_BLOCK, 1), lambda b, n: (cur(b, n), 0)),
                  pl.BlockSpec((SW_BLOCK, 1), lambda b, n: (prev(b, n), 0)),
                  full(qg), full(kg),
                  pl.BlockSpec(memory_space=pltpu.SMEM),
                  full(freq), full(sign), full(seg),
                  pl.BlockSpec(memory_space=pl.ANY)],
        out_specs=pl.BlockSpec((SW_BLOCK, 512), lambda b, n: (cur(b, n), 1)),
        out_shape=jax.ShapeDtypeStruct((t, yw), F32),
        input_output_aliases={13: 0},
        compiler_params=_params(("parallel", "parallel")),
    )(proj, proj, proj, proj, proj, pos, pos, qg, kg, sinks, freq, sign, seg, y_in)


def _sw_bwd(proj, pos, qg, kg, sinks, y, dy, bsz, seq):
    t = proj.shape[0]
    nb = seq // SW_BLOCK
    freq_np, sign_np, seg_np = _sw_constants()
    freq, sign = jnp.asarray(freq_np), jnp.asarray(sign_np)
    seg = jnp.asarray(seg_np, _MXU_DTYPE)
    cur, prev = _sw_specs(nb)
    scale = SW_HD ** -0.5

    def body(q_ref, kc_ref, kp_ref, vc_ref, vp_ref, pc_ref, pp_ref, qg_ref, kg_ref, sk_ref, fr_ref, sn_ref, seg_ref,
             y_ref, dy_ref, dp_ref, dqg_ref, dkg_ref, dsk_ref,
             dq_car, dkv_car, dqr_s, dkc_s, dkp_s, dvc_s, dvp_s, gq_acc, gk_acc, sk_acc):
        b, n = pl.program_id(0), pl.program_id(1)
        first = jnp.logical_and(b == 0, n == 0)
        last = jnp.logical_and(b == pl.num_programs(0) - 1, n == nb)

        @pl.when(first)
        def _():
            gq_acc[...] = jnp.zeros_like(gq_acc)
            gk_acc[...] = jnp.zeros_like(gk_acc)
            sk_acc[...] = jnp.zeros_like(sk_acc)

        @pl.when(n < nb)
        def _():
            segv = seg_ref[...]
            cos_c, sin_c = _rope_tables(pc_ref[...], fr_ref[...], sn_ref[...])
            cos_p, sin_p = _rope_tables(pp_ref[...], fr_ref[...], sn_ref[...])
            qv, kcv, kpv = q_ref[...], kc_ref[...], kp_ref[...]
            qr, rq = _sw_norm_rope(qv, qg_ref[...], segv, cos_c, sin_c)
            kcr, rkc = _sw_norm_rope(kcv, kg_ref[...], segv, cos_c, sin_c)
            kpr, rkp = _sw_norm_rope(kpv, kg_ref[...], segv, cos_p, sin_p)
            vc, vp = vc_ref[...], vp_ref[...]
            lane = lax.broadcasted_iota(jnp.int32, (1, 128), 1)
            dsk = jnp.zeros((1, 128), F32)
            heads = range(SW_HEADS)
            ks = [slice(SW_HD * (h // SW_GROUP), SW_HD * (h // SW_GROUP + 1)) for h in heads]
            hs = [slice(SW_HD * h, SW_HD * (h + 1)) for h in heads]
            qh = [qr[:, hs[h]] for h in heads]
            doh = [dy_ref[:, hs[h]] for h in heads]
            raw = [_sw_scores(qh[h], kpr[:, ks[h]], kcr[:, ks[h]]) for h in heads]
            dpp = [_dot(doh[h], vp[:, ks[h]], NT) for h in heads]
            dpc = [_dot(doh[h], vc[:, ks[h]], NT) for h in heads]
            probs = [_sw_probs(raw[h], sk_ref[0, h], n == 0) for h in heads]
            dsp, dsc = [], []
            for h in heads:
                pp, pc, ps = probs[h]
                delta = jnp.sum(doh[h] * y_ref[:, hs[h]], axis=1, keepdims=True)
                dsp.append(pp * (dpp[h] - delta) * scale)
                dsc.append(pc * (dpc[h] - delta) * scale)
                dsk = dsk + jnp.where(lane == h, -jnp.sum(ps * delta), 0.0)
            for h in heads:
                dqr_s[:, hs[h]] = _dot(dsp[h], kpr[:, ks[h]]) + _dot(dsc[h], kcr[:, ks[h]])
            for kv in range(SW_KV_HEADS):
                group = range(SW_GROUP * kv, SW_GROUP * (kv + 1))
                kvs = slice(SW_HD * kv, SW_HD * (kv + 1))
                dvp_s[:, kvs] = sum(_dot(probs[h][0], doh[h], TN) for h in group)
                dvc_s[:, kvs] = sum(_dot(probs[h][1], doh[h], TN) for h in group)
                dkp_s[:, kvs] = sum(_dot(dsp[h], qh[h], TN) for h in group)
                dkc_s[:, kvs] = sum(_dot(dsc[h], qh[h], TN) for h in group)
            dq, gq = _sw_norm_rope_bwd(dqr_s[...], qv, rq, qg_ref[...], segv, cos_c, sin_c)
            dkc, gkc = _sw_norm_rope_bwd(dkc_s[...], kcv, rkc, kg_ref[...], segv, cos_c, sin_c)
            dkp, gkp = _sw_norm_rope_bwd(dkp_s[...], kpv, rkp, kg_ref[...], segv, cos_p, sin_p)
            gq_acc[...] += gq
            gk_acc[...] += gkc + gkp
            sk_acc[...] += dsk

            @pl.when(n > 0)
            def _():
                dp_ref[:, 0:512] = _mx(dq_car[...])
                dp_ref[:, 512:640] = _mx(dkv_car[:, 0:128] + dkp)
                dp_ref[:, 640:768] = _mx(dkv_car[:, 128:256] + dvp_s[...])

            dq_car[...] = dq
            dkv_car[:, 0:128] = dkc
            dkv_car[:, 128:256] = dvc_s[...]

        @pl.when(n == nb)
        def _():
            dp_ref[:, 0:512] = _mx(dq_car[...])
            dp_ref[:, 512:768] = _mx(dkv_car[...])

        @pl.when(last)
        def _():
            gq = gq_acc[...]
            acc = gq[:, 0:SW_HD]
            for h in range(1, SW_HEADS):
                acc = acc + gq[:, SW_HD * h:SW_HD * (h + 1)]
            dqg_ref[...] = acc
            gk = gk_acc[...]
            dkg_ref[...] = gk[:, 0:SW_HD] + gk[:, SW_HD:2 * SW_HD]
            dsk_ref[...] = sk_acc[...]

    rowq = pl.BlockSpec((SW_BLOCK, 512), lambda b, n: (cur(b, n), 0))
    full = lambda a: pl.BlockSpec(a.shape, lambda b, n: (0,) * a.ndim)

    def out_row(b, n):
        return b * nb + jnp.maximum(n - 1, 0)

    return pl.pallas_call(
        body, name="swa_bwd", grid=(bsz, nb + 1),
        in_specs=[rowq,
                  pl.BlockSpec((SW_BLOCK, 128), lambda b, n: (cur(b, n), 4)),
                  pl.BlockSpec((SW_BLOCK, 128), lambda b, n: (prev(b, n), 4)),
                  pl.BlockSpec((SW_BLOCK, 128), lambda b, n: (cur(b, n), 5)),
                  pl.BlockSpec((SW_BLOCK, 128), lambda b, n: (prev(b, n), 5)),
                  pl.BlockSpec((SW_BLOCK, 1), lambda b, n: (cur(b, n), 0)),
                  pl.BlockSpec((SW_BLOCK, 1), lambda b, n: (prev(b, n), 0)),
                  full(qg), full(kg),
                  pl.BlockSpec(memory_space=pltpu.SMEM),
                  full(freq), full(sign), full(seg),
                  pl.BlockSpec((SW_BLOCK, 512), lambda b, n: (cur(b, n), 1)),
                  pl.BlockSpec((SW_BLOCK, 512), lambda b, n: (cur(b, n), 1))],
        out_specs=(pl.BlockSpec((SW_BLOCK, SW_COLS), lambda b, n: (out_row(b, n), 0)),
                   pl.BlockSpec((1, SW_HD), lambda b, n: (0, 0)),
                   pl.BlockSpec((1, SW_HD), lambda b, n: (0, 0)),
                   pl.BlockSpec((1, 128), lambda b, n: (0, 0))),
        out_shape=(jax.ShapeDtypeStruct((t, SW_COLS), _MXU_DTYPE),
                   jax.ShapeDtypeStruct((1, SW_HD), F32),
                   jax.ShapeDtypeStruct((1, SW_HD), F32),
                   jax.ShapeDtypeStruct((1, 128), F32)),
        scratch_shapes=[pltpu.VMEM((SW_BLOCK, 512), F32), pltpu.VMEM((SW_BLOCK, 256), F32),
                        pltpu.VMEM((SW_BLOCK, 512), F32),
                        pltpu.VMEM((SW_BLOCK, 128), F32), pltpu.VMEM((SW_BLOCK, 128), F32),
                        pltpu.VMEM((SW_BLOCK, 128), F32), pltpu.VMEM((SW_BLOCK, 128), F32),
                        pltpu.VMEM((1, 512), F32), pltpu.VMEM((1, 128), F32), pltpu.VMEM((1, 128), F32)],
        compiler_params=_params(("arbitrary", "arbitrary")),
    )(proj, proj, proj, proj, proj, pos, pos, qg, kg, sinks, freq, sign, seg, y, dy)


def _head_rms(tv, gain):
    r = lax.rsqrt(jnp.mean(tv * tv, axis=1, keepdims=True) + EPS)
    return tv * r * gain, r


def _head_rms_bwd(dtn, tv, r, gain):
    u = dtn * gain
    return r * u - tv * (r * r * r) * jnp.mean(u * tv, axis=1, keepdims=True), jnp.sum(dtn * tv * r, axis=0, keepdims=True)


def _xa_softmax(raw):
    s = raw * (XA_HD ** -0.5)
    e = jnp.exp(s - jnp.max(s, axis=1, keepdims=True))
    return e / jnp.sum(e, axis=1, keepdims=True)


def _xa_fwd(qx, kvx, qg, kg, bsz, seq, mlen, *, tq=512):
    t = qx.shape[0]
    tq = min(tq, seq)
    nq = seq // tq
    w = XA_HEADS * XA_HD

    def body(q_ref, kv_ref, qg_ref, kg_ref, o_ref):
        heads = range(XA_HEADS)
        hs = [slice(XA_HD * h, XA_HD * (h + 1)) for h in heads]
        qn = [_head_rms(q_ref[:, hs[h]], qg_ref[...])[0] for h in heads]
        kn = [_head_rms(kv_ref[:, hs[h]], kg_ref[...])[0] for h in heads]
        raw = [_dot(qn[h], kn[h], NT) for h in heads]
        p = [_xa_softmax(raw[h]) for h in heads]
        for h in heads:
            o_ref[:, hs[h]] = _dot(p[h], kv_ref[:, w + XA_HD * h:w + XA_HD * (h + 1)]).astype(o_ref.dtype)

    vec = pl.BlockSpec((1, XA_HD), lambda b, i: (0, 0))
    return pl.pallas_call(
        body, name="xattn_fwd", grid=(bsz, nq),
        in_specs=[pl.BlockSpec((tq, w), lambda b, i: (b * nq + i, 0)),
                  pl.BlockSpec((mlen, 2 * w), lambda b, i: (b, 0)), vec, vec],
        out_specs=pl.BlockSpec((tq, w), lambda b, i: (b * nq + i, 0)),
        out_shape=jax.ShapeDtypeStruct((t, w), _MXU_DTYPE),
        compiler_params=_params(("parallel", "parallel")),
    )(qx, kvx, qg, kg)


def _xa_bwd(qx, kvx, qg, kg, do, bsz, seq, mlen, *, tq=512):
    t = qx.shape[0]
    tq = min(tq, seq)
    nq = seq // tq
    w = XA_HEADS * XA_HD
    scale = XA_HD ** -0.5

    def body(q_ref, kv_ref, qg_ref, kg_ref, do_ref, dq_ref, dkv_ref, dqg_ref, dkg_ref):
        b, i = pl.program_id(0), pl.program_id(1)

        @pl.when(jnp.logical_and(b == 0, i == 0))
        def _():
            dqg_ref[...] = jnp.zeros_like(dqg_ref)
            dkg_ref[...] = jnp.zeros_like(dkg_ref)

        @pl.when(i == 0)
        def _():
            dkv_ref[...] = jnp.zeros_like(dkv_ref)

        heads = range(XA_HEADS)
        hs = [slice(XA_HD * h, XA_HD * (h + 1)) for h in heads]
        vs = [slice(w + XA_HD * h, w + XA_HD * (h + 1)) for h in heads]
        qv = [q_ref[:, hs[h]] for h in heads]
        kv = [kv_ref[:, hs[h]] for h in heads]
        doh = [do_ref[:, hs[h]] for h in heads]
        qn = [_head_rms(qv[h], qg_ref[...]) for h in heads]
        kn = [_head_rms(kv[h], kg_ref[...]) for h in heads]
        raw = [_dot(qn[h][0], kn[h][0], NT) for h in heads]
        dp = [_dot(doh[h], kv_ref[:, vs[h]], NT) for h in heads]
        p = [_xa_softmax(raw[h]) for h in heads]
        ds = [p[h] * (dp[h] - jnp.sum(p[h] * dp[h], axis=1, keepdims=True)) * scale for h in heads]
        dqn = [_dot(ds[h], kn[h][0]) for h in heads]
        dkn = [_dot(ds[h], qn[h][0], TN) for h in heads]
        dvv = [_dot(p[h], doh[h], TN) for h in heads]
        gq_sum = jnp.zeros((1, XA_HD), F32)
        gk_sum = jnp.zeros((1, XA_HD), F32)
        for h in heads:
            dqv, gq = _head_rms_bwd(dqn[h], qv[h], qn[h][1], qg_ref[...])
            dkv, gk = _head_rms_bwd(dkn[h], kv[h], kn[h][1], kg_ref[...])
            dq_ref[:, hs[h]] = dqv.astype(dq_ref.dtype)
            dkv_ref[:, hs[h]] += dkv
            dkv_ref[:, vs[h]] += dvv[h]
            gq_sum = gq_sum + gq
            gk_sum = gk_sum + gk
        dqg_ref[...] += gq_sum
        dkg_ref[...] += gk_sum

    vec = pl.BlockSpec((1, XA_HD), lambda b, i: (0, 0))
    row = pl.BlockSpec((tq, w), lambda b, i: (b * nq + i, 0))
    mem = pl.BlockSpec((mlen, 2 * w), lambda b, i: (b, 0))
    return pl.pallas_call(
        body, name="xattn_bwd", grid=(bsz, nq),
        in_specs=[row, mem, vec, vec, row],
        out_specs=(row, mem, vec, vec),
        out_shape=(jax.ShapeDtypeStruct((t, w), _MXU_DTYPE), jax.ShapeDtypeStruct((bsz * mlen, 2 * w), F32),
                   jax.ShapeDtypeStruct((1, XA_HD), F32), jax.ShapeDtypeStruct((1, XA_HD), F32)),
        compiler_params=_params(("arbitrary", "arbitrary")),
    )(qx, kvx, qg, kg, do)


def _loss_sum(dy, d_model, *, tm=512):
    t, d = dy.shape
    tm = min(tm, t)
    steps = t // tm

    def body(dy_ref, o_ref, acc_ref):
        i = pl.program_id(0)

        @pl.when(i == 0)
        def _():
            acc_ref[...] = jnp.zeros_like(acc_ref)

        diff = dy_ref[...] * float(d_model)
        acc_ref[...] += jnp.sum(diff * diff, axis=0, keepdims=True)

        @pl.when(i == steps - 1)
        def _():
            o_ref[...] = jnp.zeros_like(o_ref) + 0.5 * jnp.sum(acc_ref[...]) / float(d_model)

    return pl.pallas_call(
        body, name="loss_sum", grid=(steps,),
        in_specs=[pl.BlockSpec((tm, d), lambda i: (i, 0))],
        out_specs=pl.BlockSpec((1, 128), lambda i: (0, 0)),
        out_shape=jax.ShapeDtypeStruct((1, 128), F32),
        scratch_shapes=[pltpu.VMEM((1, d), F32)],
        compiler_params=_params(("arbitrary",)),
    )(dy)


def _adamw_math(w, g, m, v):
    m = ADAM_B1 * m + (1.0 - ADAM_B1) * g
    v = ADAM_B2 * v + (1.0 - ADAM_B2) * (g * g)
    m_hat = m / (1.0 - ADAM_B1 ** ADAM_STEP)
    v_hat = v / (1.0 - ADAM_B2 ** ADAM_STEP)
    return -ADAM_LR * (m_hat / (jnp.sqrt(v_hat) + ADAM_EPS) + ADAM_WD * w), m, v


def _adamw_big(w, g, m, v, *, name, tr=256):
    r, c = w.shape
    tr = min(tr, r)

    def body(w_ref, g_ref, m_ref, v_ref, d_ref, mo_ref, vo_ref):
        d, mn, vn = _adamw_math(w_ref[...], g_ref[...], m_ref[...], v_ref[...])
        d_ref[...] = d
        mo_ref[...] = mn
        vo_ref[...] = vn

    spec = pl.BlockSpec((tr, c), lambda i: (i, 0))
    shp = jax.ShapeDtypeStruct((r, c), F32)
    return pl.pallas_call(
        body, name=name, grid=(r // tr,), in_specs=[spec] * 4, out_specs=(spec,) * 3, out_shape=(shp,) * 3,
        compiler_params=_params(("parallel",)),
    )(w, g, m, v)


def _adamw_small(ws, gs, ms, vs):
    n = len(ws)

    def body(*refs):
        for i in range(n):
            d, mn, vn = _adamw_math(refs[i][...], refs[n + i][...], refs[2 * n + i][...], refs[3 * n + i][...])
            refs[4 * n + i][...] = d
            refs[5 * n + i][...] = mn
            refs[6 * n + i][...] = vn

    shapes = tuple(jax.ShapeDtypeStruct(w.shape, F32) for w in ws)
    return pl.pallas_call(body, name="adamw_small", out_shape=shapes * 3)(*ws, *gs, *ms, *vs)


def _add_halves(g, recv, c_idx, *, name, tr=256):
    _, r, c = g.shape
    h = r // 2
    tr = min(tr, h)
    nt = h // tr

    def body(c_ref, g_ref, r_ref, o_ref):
        del c_ref
        o_ref[...] = g_ref[...] + r_ref[...]

    return pl.pallas_call(
        body, name=name,
        grid_spec=pltpu.PrefetchScalarGridSpec(
            num_scalar_prefetch=1, grid=(4, nt),
            in_specs=[pl.BlockSpec((None, tr, c), lambda k, i, cr: (k, cr[0] * nt + i, 0)),
                      pl.BlockSpec((None, tr, c), lambda k, i, cr: (k, i, 0))],
            out_specs=pl.BlockSpec((None, tr, c), lambda k, i, cr: (k, i, 0))),
        out_shape=jax.ShapeDtypeStruct((4, h, c), F32),
        compiler_params=_params(("parallel", "parallel")),
    )(c_idx, g, recv)


def _add_chips(p, recv, place_idx, *, name, tr=256):
    _, h, c = p.shape
    tr = min(tr, h)
    nt = h // tr

    def body(pi_ref, p_ref, r_ref, o_ref):
        del pi_ref
        o_ref[...] = ((p_ref[...] + r_ref[0]) + r_ref[1]) + r_ref[2]

    return pl.pallas_call(
        body, name=name,
        grid_spec=pltpu.PrefetchScalarGridSpec(
            num_scalar_prefetch=1, grid=(nt,),
            in_specs=[pl.BlockSpec((None, tr, c), lambda i, pi: (pi[0], i, 0)),
                      pl.BlockSpec((3, tr, c), lambda i, pi: (0, i, 0))],
            out_specs=pl.BlockSpec((tr, c), lambda i, pi: (pi[1] * nt + i, 0))),
        out_shape=jax.ShapeDtypeStruct((2 * h, c), F32),
        compiler_params=_params(("parallel",)),
    )(place_idx, p, recv)


def _place_shard(shard, place_idx, *, name, tr=256, after=()):
    r, c = shard.shape
    tr = min(tr, r)

    def body(pi_ref, s_ref, *rest):
        del pi_ref
        rest[-1][...] = s_ref[...]

    return pl.pallas_call(
        body, name=name,
        grid_spec=pltpu.PrefetchScalarGridSpec(
            num_scalar_prefetch=1, grid=(r // tr,),
            in_specs=[pl.BlockSpec((tr, c), lambda i, pi: (i, 0))] + [pl.BlockSpec(memory_space=pl.ANY)] * len(after),
            out_specs=pl.BlockSpec((None, tr, c), lambda i, pi: (pi[0], i, 0))),
        out_shape=jax.ShapeDtypeStruct((4, r, c), shard.dtype),
        compiler_params=_params(("parallel",)),
    )(place_idx, shard, *after)


def _place():
    x, y, c = lax.axis_index("x"), lax.axis_index("y"), lax.axis_index("c")
    chips = [(1 - x, y), (x, 1 - y), (1 - x, 1 - y)]
    return x, y, c, chips


ANY = pl.BlockSpec(memory_space=pl.ANY)


def _exchange_halves(grads, name):
    n = len(grads)

    def body(*refs):
        ins, outs = refs[:n], refs[n:2 * n]
        send_sems, recv_sems = refs[2 * n:]
        x, y, c, _ = _place()

        def copy(a):
            h = ins[a].shape[1] // 2
            return pltpu.make_async_remote_copy(
                src_ref=ins[a].at[:, pl.ds((1 - c) * h, h), :], dst_ref=outs[a],
                send_sem=send_sems.at[a], recv_sem=recv_sems.at[a], device_id=(x, y, 1 - c), device_id_type=MESH)

        for a in range(n):
            copy(a).start()
        for a in range(n):
            copy(a).wait_recv()
        for a in range(n):
            copy(a).wait_send()

    return pl.pallas_call(
        body, name=name,
        in_specs=[ANY] * n, out_specs=tuple([ANY] * n),
        out_shape=tuple(jax.ShapeDtypeStruct((4, g.shape[1] // 2, g.shape[2]), g.dtype) for g in grads),
        scratch_shapes=[pltpu.SemaphoreType.DMA((n,)), pltpu.SemaphoreType.DMA((n,))],
    )(*grads)


HBM = pl.BlockSpec(memory_space=pltpu.HBM)
SEM = pl.BlockSpec(memory_space=pltpu.SEMAPHORE)
EFFECT = pltpu.SideEffectType.DATAFLOW_SIDE_EFFECTING


def _in_hbm(a):
    return pltpu.with_memory_space_constraint(a, pltpu.HBM)


def _split_copy_calls(name, srcs, lands, n_copies, make_copies):
    ns, nl = len(srcs), len(lands)
    nb = ns + nl

    def start(after=()):
        n_after = len(after)

        def body(*refs):
            outs = refs[nb + n_after:]
            copies = make_copies(refs[:ns], refs[ns:nb], outs[0], outs[1])
            for cp in copies:
                cp.start()
            token = refs[-1]
            token[...] = jnp.zeros_like(token)

        bufs = [_in_hbm(a) for a in list(srcs) + list(lands)]
        out = pl.pallas_call(
            body, name=name + "_start",
            out_shape=(pltpu.SemaphoreType.DMA((n_copies,)), pltpu.SemaphoreType.DMA((n_copies,)),
                       *[pltpu.HBM(a.shape, a.dtype) for a in bufs], jax.ShapeDtypeStruct((8, 128), F32)),
            in_specs=[HBM] * nb + [pl.BlockSpec(memory_space=pl.ANY)] * n_after,
            out_specs=(SEM, SEM, *[HBM] * nb, pl.BlockSpec(memory_space=pltpu.VMEM)),
            input_output_aliases={i: 2 + i for i in range(nb)},
            compiler_params=pltpu.CompilerParams(has_side_effects=EFFECT),
        )(*bufs, *after)
        return dict(send=out[0], recv=out[1], bufs=list(out[2:2 + nb]), token=out[-1])

    def wait(state, after):
        def body(*refs):
            copies = make_copies(refs[:ns], refs[ns:nb], refs[nb], refs[nb + 1])
            for cp in copies:
                cp.wait_send()
            for cp in copies:
                cp.wait_recv()

        bufs = state["bufs"]
        out = pl.pallas_call(
            body, name=name + "_wait",
            out_shape=tuple(pltpu.HBM(a.shape, a.dtype) for a in bufs),
            in_specs=[HBM] * nb + [SEM, SEM, pl.BlockSpec(memory_space=pl.ANY)], out_specs=tuple([HBM] * nb),
            input_output_aliases={i: i for i in range(nb)},
            compiler_params=pltpu.CompilerParams(has_side_effects=EFFECT),
        )(*bufs, state["send"], state["recv"], after)
        return list(out[:ns]), list(out[ns:])

    return start, wait


def _scatter_chips_split(name, parts):
    n = len(parts)
    lands = [lax.empty((3,) + p.shape[1:], p.dtype) for p in parts]

    def make_copies(srcs, lnds, send_sems, recv_sems):
        _, _, c, chips = _place()
        return [pltpu.make_async_remote_copy(
            src_ref=srcs[a].at[2 * px + py], dst_ref=lnds[a].at[j], send_sem=send_sems.at[a * 3 + j],
            recv_sem=recv_sems.at[a * 3 + j], device_id=(px, py, c), device_id_type=MESH)
            for a in range(n) for j, (px, py) in enumerate(chips)]

    return _split_copy_calls(name, parts, lands, 3 * n, make_copies)


def _gather_chips_split(name, shards, lands):
    n = len(shards)

    def make_copies(srcs, lnds, send_sems, recv_sems):
        x, y, c, chips = _place()
        out = []
        for a in range(n):
            h = srcs[a].shape[0] // 2
            for j, (px, py) in enumerate(chips):
                out.append(pltpu.make_async_remote_copy(
                    src_ref=srcs[a].at[pl.ds(c * h, h), :], dst_ref=lnds[a].at[2 * x + y, pl.ds(c * h, h), :],
                    send_sem=send_sems.at[a * 3 + j], recv_sem=recv_sems.at[a * 3 + j],
                    device_id=(px, py, c), device_id_type=MESH))
        return out

    return _split_copy_calls(name, shards, lands, 3 * n, make_copies)


def _gather_finish(gathered, name):
    n = len(gathered)

    def body(*refs):
        outs = refs[n:2 * n]
        send_sems, recv_sems = refs[2 * n:]
        x, y, c, chips = _place()

        def copy(a, j, chip_idx, which):
            h = outs[a].shape[1] // 2
            rows = outs[a].at[chip_idx, pl.ds(which * h, h), :]
            return pltpu.make_async_remote_copy(
                src_ref=rows, dst_ref=rows, send_sem=send_sems.at[a * 3 + j], recv_sem=recv_sems.at[a * 3 + j],
                device_id=(x, y, 1 - c), device_id_type=MESH)

        for a in range(n):
            for j, (px, py) in enumerate(chips):
                copy(a, j, 2 * px + py, c).start()
        for a in range(n):
            for j, (px, py) in enumerate(chips):
                copy(a, j, 2 * px + py, 1 - c).wait_recv()
        for a in range(n):
            for j, (px, py) in enumerate(chips):
                copy(a, j, 2 * px + py, c).wait_send()

    return pl.pallas_call(
        body, name=name,
        in_specs=[ANY] * n, out_specs=tuple([ANY] * n),
        out_shape=tuple(jax.ShapeDtypeStruct(g.shape, g.dtype) for g in gathered),
        input_output_aliases={i: i for i in range(n)},
        scratch_shapes=[pltpu.SemaphoreType.DMA((3 * n,)), pltpu.SemaphoreType.DMA((3 * n,))],
    )(*gathered)


def _join_halves(fulls):
    n = len(fulls)

    def body(*refs):
        outs = refs[n:2 * n]
        send_sems, recv_sems = refs[2 * n:]
        x, y, c, _ = _place()

        def copy(a, which):
            h = outs[a].shape[0] // 2
            rows = outs[a].at[pl.ds(which * h, h), :]
            return pltpu.make_async_remote_copy(
                src_ref=rows, dst_ref=rows, send_sem=send_sems.at[a], recv_sem=recv_sems.at[a],
                device_id=(x, y, 1 - c), device_id_type=MESH)

        for a in range(n):
            copy(a, c).start()
        for a in range(n):
            copy(a, 1 - c).wait_recv()
        for a in range(n):
            copy(a, c).wait_send()

    return pl.pallas_call(
        body, name="rs_join_halves",
        in_specs=[ANY] * n, out_specs=tuple([ANY] * n),
        out_shape=tuple(jax.ShapeDtypeStruct(p.shape, p.dtype) for p in fulls),
        input_output_aliases={i: i for i in range(n)},
        scratch_shapes=[pltpu.SemaphoreType.DMA((n,)), pltpu.SemaphoreType.DMA((n,))],
    )(*fulls)


def _all_reduce_small(sm):
    r, w = sm.shape

    def body(sm_ref, o_ref, buf, send_sems, recv_sems):
        x, y, c, _ = _place()
        me = 4 * x + 2 * y + c
        buf[me] = sm_ref[...]
        rel = [(dx, dy, dc) for dx in (0, 1) for dy in (0, 1) for dc in (0, 1)][1:]

        def copy(k, slot, to):
            return pltpu.make_async_remote_copy(
                src_ref=sm_ref, dst_ref=buf.at[slot], send_sem=send_sems.at[k], recv_sem=recv_sems.at[k],
                device_id=to, device_id_type=MESH)

        peers = []
        for k, (dx, dy, dc) in enumerate(rel):
            px = 1 - x if dx else x
            py = 1 - y if dy else y
            pc = 1 - c if dc else c
            peers.append((px, py, pc))
            copy(k, me, (px, py, pc)).start()
        for k, (px, py, pc) in enumerate(peers):
            copy(k, 4 * px + 2 * py + pc, (px, py, pc)).wait_recv()
        for k, (px, py, pc) in enumerate(peers):
            copy(k, me, (px, py, pc)).wait_send()
        acc = buf[0]
        for d in range(1, 8):
            acc = acc + buf[d]
        o_ref[...] = acc

    vm = pl.BlockSpec(memory_space=pltpu.VMEM)
    return pl.pallas_call(
        body, name="all_reduce_small", in_specs=[vm], out_specs=vm,
        out_shape=jax.ShapeDtypeStruct((r, w), F32),
        scratch_shapes=[pltpu.VMEM((8, r, w), F32), pltpu.SemaphoreType.DMA((7,)), pltpu.SemaphoreType.DMA((7,))],
    )(sm)


class _LocalWeights:
    def __init__(self, w):
        self.w = w
        self.g = {}

    def begin(self):
        pass

    def first(self, after):
        del after
        return self.w

    def rest(self, after):
        del after
        return self.w

    def grads(self, tag, g):
        del tag
        self.g.update(g)
        return None


def _local_step(x3, mem3, pos2, target3, small, comm):
    bsz, seq, d = x3.shape
    mlen = mem3.shape[1]
    t = bsz * seq
    comm.begin()
    x = x3.reshape(t, d)
    mem = mem3.reshape(bsz * mlen, d)
    target = target3.reshape(t, d)
    pos = pos2.reshape(t, 1)
    qg_t = jnp.tile(small["sw_q_norm_g"], (1, SW_HEADS))
    kg_t = jnp.tile(small["sw_k_norm_g"], (1, SW_KV_HEADS))

    hn1 = _rms_fwd(x, small["norm1_g"], name="rms1_fwd")
    w = comm.first(hn1)
    proj_hg = _mm(hn1, w["w_in_hg"], NN, t, HG_COLS, d, name="proj_hg", tk=d, after=(w.get("token"),))[0]
    proj_sw = _mm(hn1, w["w_in_sw"], NN, t, SW_COLS, d, name="proj_sw", tk=d)[0]
    y_mix, o_hg, states = _hg_fwd(proj_hg, small["hg_lower_bounds"], small["hg_norm_g"], bsz, seq, y_width=1024)
    y_mix = _sw_fwd(proj_sw, pos, qg_t, kg_t, small["sw_sinks"], y_mix, bsz, seq)
    w_in_hg, w_in_sw = w["w_in_hg"], w["w_in_sw"]
    w = comm.rest(y_mix)
    ff = w["down"].shape[0]
    ffs = ff // 4
    h1 = _mm(y_mix, w["w_out"], NN, t, d, 1024, name="out_proj", tk=1024, extras=(x,),
             epilogue=lambda acc, res: (acc + res,))[0]
    hn2 = _rms_fwd(h1, small["norm2_g"], name="rms2_fwd")
    mn = _rms_fwd(mem, small["mem_norm_g"], name="rms_mem_fwd")
    qx = _mm(hn2, w["wq"], NN, t, 512, d, name="xa_q", tk=d)[0]
    kvx = _mm(mn, w["wkv"], NN, bsz * mlen, 1024, d, name="xa_kv", tk=d)[0]
    ox = _xa_fwd(qx, kvx, small["xa_q_norm_g"], small["xa_k_norm_g"], bsz, seq, mlen)
    h2 = _mm(ox, w["wo"], NN, t, d, 512, name="xa_o", tk=512, extras=(h1,), epilogue=lambda acc, res: (acc + res,))[0]
    hn3 = _rms_fwd(h2, small["norm3_g"], name="rms3_fwd")

    def relu_sq(acc):
        a = jnp.maximum(acc, 0.0)
        return a, a * a

    act, act2 = _mm(hn3, w["up"], NN, t, ff, d, name="mlp_up", tm=2048, tn=ffs, tk=d,
                    b_spec=pl.BlockSpec((None, d, ffs), lambda i, j, kk: (j, 0, 0)),
                    epilogue=relu_sq, out_dtypes=(_MXU_DTYPE, _MXU_DTYPE))
    inv_d = 1.0 / d

    def loss_cotangent(acc, res, tgt):
        v = (acc + res - tgt) * inv_d
        return v, v

    dy, dy_mx = _mm(act2, w["down"], NN, t, d, ff, name="mlp_down", extras=(h2, target), epilogue=loss_cotangent,
                    out_dtypes=(F32, _MXU_DTYPE))
    loss_row = _loss_sum(dy, d)

    dz = _mm(dy_mx, w["down"], NT, t, ff, d, name="d_act", tm=2048, tk=d, extras=(act,),
             epilogue=lambda acc, a: (acc * (2.0 * a.astype(F32)),), out_dtypes=(_MXU_DTYPE,))[0]
    g_down = _mm(act2, dy_mx, TN, ff, d, t, name="g_down")[0]
    g_up = _mm(hn3, dz, TN, d, ff, t, name="g_up", tn=ffs,
               out_shape=(jax.ShapeDtypeStruct((4, d, ffs), F32),),
               out_spec=(pl.BlockSpec((None, min(1024, d), ffs), lambda i, j, kk: (j, i, 0)),))[0]
    tok = comm.grads("mlp", dict(up=g_up, down=g_down))
    dhn3 = _mm(dz, w["up"], NT, t, d, ff, name="d_hn3", tm=2048, tk=ffs, after=(tok,),
               b_spec=pl.BlockSpec((None, min(1024, d), ffs), lambda i, j, kk: (kk, j, 0)))[0]
    dh2, dh2_mx, g_norm3 = _rms_bwd(h2, small["norm3_g"], dhn3, dy, name="rms3_bwd", operand_copy=True)
    d_ox = _mm(dh2_mx, w["wo"], NT, t, 512, d, name="d_ox", tk=d)[0]
    g_wo = _mm(ox, dh2_mx, TN, 512, d, t, name="g_wo")[0]
    d_qx, d_kvx, g_xq, g_xk = _xa_bwd(qx, kvx, small["xa_q_norm_g"], small["xa_k_norm_g"], d_ox, bsz, seq, mlen)
    g_wq = _mm(hn2, d_qx, TN, d, 512, t, name="g_wq")[0]
    g_wkv = _mm(mn, d_kvx, TN, d, 1024, bsz * mlen, name="g_wkv")[0]
    dhn2 = _mm(d_qx, w["wq"], NT, t, d, 512, name="d_hn2", tk=512)[0]
    dmn = _mm(d_kvx, w["wkv"], NT, bsz * mlen, d, 1024, name="d_mn", tk=1024)[0]
    dh1, dh1_mx, g_norm2 = _rms_bwd(h1, small["norm2_g"], dhn2, dh2, name="rms2_bwd", operand_copy=True)
    _, g_memn = _rms_bwd(mem, small["mem_norm_g"], dmn, None, name="rms_mem_bwd")
    g_wout = _mm(y_mix, dh1_mx, TN, 1024, d, t, name="g_wout")[0]
    tok = comm.grads("mid", dict(w_out=g_wout, wq=g_wq, wkv=g_wkv, wo=g_wo))
    d_mix = _mm(dh1_mx, w["w_out"], NT, t, 1024, d, name="d_mix", tk=d, after=(tok,))[0]
    dproj_sw, g_swq, g_swk, g_sinks = _sw_bwd(proj_sw, pos, qg_t, kg_t, small["sw_sinks"], y_mix, d_mix, bsz, seq)
    dproj_hg, g_lb, g_hgn = _hg_bwd(proj_hg, small["hg_lower_bounds"], small["hg_norm_g"], o_hg, states, d_mix, bsz, seq)
    g_in_hg = _mm(hn1, dproj_hg, TN, d, HG_COLS, t, name="g_in_hg")[0]
    g_in_sw = _mm(hn1, dproj_sw, TN, d, SW_COLS, t, name="g_in_sw")[0]
    tok = comm.grads("in", dict(w_in_hg=g_in_hg, w_in_sw=g_in_sw))
    dhn1_a = _mm(dproj_hg, w_in_hg, NT, t, d, HG_COLS, name="d_hn1_hg", tk=1024, after=(tok,))[0]
    dhn1 = _mm(dproj_sw, w_in_sw, NT, t, d, SW_COLS, name="d_hn1_sw", tk=SW_COLS, extras=(dhn1_a,),
               epilogue=lambda acc, prev: (acc + prev,))[0]
    grad_x, g_norm1 = _rms_bwd(x, small["norm1_g"], dhn1, dh1, name="rms1_bwd")

    g_small = dict(norm1_g=g_norm1, hg_lower_bounds=g_lb, hg_norm_g=g_hgn, sw_q_norm_g=g_swq, sw_k_norm_g=g_swk,
                   sw_sinks=g_sinks[:, 0:SW_HEADS], norm2_g=g_norm2, mem_norm_g=g_memn, xa_q_norm_g=g_xq,
                   xa_k_norm_g=g_xk, norm3_g=g_norm3)
    return loss_row, grad_x.reshape(bsz, seq, d), g_small


SMALL_NAMES = ("norm1_g", "hg_lower_bounds", "hg_norm_g", "sw_q_norm_g", "sw_k_norm_g", "sw_sinks", "norm2_g",
               "mem_norm_g", "xa_q_norm_g", "xa_k_norm_g", "norm3_g")
BIG_NAMES = ("w_in", "w_out", "xa_wq", "xa_wkv", "xa_wo", "mlp_up", "mlp_down")
WEIGHT_ORDER = ("norm1_g", "w_in", "hg_lower_bounds", "hg_norm_g", "sw_q_norm_g", "sw_k_norm_g", "sw_sinks", "w_out",
                "norm2_g", "mem_norm_g", "xa_wq", "xa_wkv", "xa_q_norm_g", "xa_k_norm_g", "xa_wo", "norm3_g",
                "mlp_up", "mlp_down")


def _pack_rows(vals, width):
    starts, at = [], 0
    for v in vals:
        starts.append(at)
        at += v.shape[0]
    total = at + (-at) % 8
    out = None
    for v, s in zip(vals, starts):
        placed = jnp.pad(v, ((s, total - s - v.shape[0]), (0, width - v.shape[1])))
        out = placed if out is None else out + placed
    return out, starts


class _MeshWeights:
    LATE = ("w_out", "xa_wq", "xa_wkv", "xa_wo", "mlp_up", "mlp_down")

    def __init__(self, shards, d, ff):
        self.shards, self.d, self.ff = shards, d, ff
        self.c_idx = lax.axis_index("c").astype(jnp.int32).reshape(1)
        chip = (2 * lax.axis_index("x") + lax.axis_index("y")).astype(jnp.int32)
        self.place_idx = jnp.stack([chip, lax.axis_index("c").astype(jnp.int32)])
        self.pending = []
        self.halves = {}

    def begin(self):
        shard = self.shards["w_in"]
        start, self.in_wait = _gather_chips_split(
            "gather_in", [shard], [_place_shard(shard, self.place_idx, name="place_w_in")])
        self.in_state = start()
        tok = (self.in_state["token"],)
        self.placed = [_place_shard(self.shards[n], self.place_idx, name="place_" + n, after=tok) for n in self.LATE]

    def first(self, after):
        _, lands = self.in_wait(self.in_state, after)
        (g_in,) = _gather_finish(lands, "gather_in_finish")
        start, self.late_wait = _gather_chips_split("gather_late", [self.shards[n] for n in self.LATE], self.placed)
        self.late_state = start(after=(g_in,))
        full = jnp.concatenate([g_in[k] for k in range(4)], axis=1)
        return dict(w_in_hg=full[:, :HG_COLS], w_in_sw=full[:, HG_COLS:], token=self.late_state["token"])

    def rest(self, after):
        _, lands = self.late_wait(self.late_state, after)
        g_out, g_q, g_kv, g_o, g_up, g_dn = _gather_finish(lands, "gather_late_finish")
        d = self.d
        return dict(w_out=g_out.reshape(-1, d), wq=g_q.reshape(d, -1), wkv=g_kv.reshape(d, -1),
                    wo=jnp.concatenate([g_o[k] for k in range(4)], axis=1), up=g_up, down=g_dn.reshape(self.ff, d))

    def _chip_partials(self, tag, names, arrays):
        recv = _exchange_halves(arrays, "rs_exchange_" + tag)
        return [_add_halves(g, r, self.c_idx, name="rs_add_halves_" + n) for n, g, r in zip(names, arrays, recv)]

    def grads(self, tag, g):
        d, ff = self.d, self.ff
        if tag == "mlp":
            names, arrays = ("mlp_up", "mlp_down"), [g["up"], g["down"].reshape(4, ff // 4, d)]
        elif tag == "mid":
            names = ("w_out", "xa_wq", "xa_wkv", "xa_wo")
            ds = d // 4
            g_wo = jnp.stack([g["wo"][:, ds * k:ds * (k + 1)] for k in range(4)])
            arrays = [g["w_out"].reshape(4, -1, d), g["wq"].reshape(4, d // 4, -1), g["wkv"].reshape(4, d // 4, -1), g_wo]
        else:
            full = jnp.concatenate([g["w_in_hg"], g["w_in_sw"]], axis=1)
            ws = full.shape[1] // 4
            names, arrays = ("w_in",), [jnp.stack([full[:, ws * k:ws * (k + 1)] for k in range(4)])]
        parts = self._chip_partials(tag, names, arrays)
        start, wait = _scatter_chips_split("rs_scatter_" + tag, parts)
        state = start()
        self.pending.append((names, wait, state))
        return state["token"]

    def finish(self, after):
        for names, wait, state in self.pending:
            srcs, lands = wait(state, after)
            for n, p, r in zip(names, srcs, lands):
                self.halves[n] = _add_chips(p, r, self.place_idx, name="rs_add_chips_" + n)
        return dict(zip(BIG_NAMES, _join_halves([self.halves[n] for n in BIG_NAMES])))


def kernel(x, mem, positions, norm1_g, w_in, hg_lower_bounds, hg_norm_g, sw_q_norm_g, sw_k_norm_g, sw_sinks, w_out, norm2_g, mem_norm_g, xa_wq, xa_wkv, xa_q_norm_g, xa_k_norm_g, xa_wo, norm3_g, mlp_up, mlp_down, loss_target, m_norm1_g, m_w_in, m_hg_lower_bounds, m_hg_norm_g, m_sw_q_norm_g, m_sw_k_norm_g, m_sw_sinks, m_w_out, m_norm2_g, m_mem_norm_g, m_xa_wq, m_xa_wkv, m_xa_q_norm_g, m_xa_k_norm_g, m_xa_wo, m_norm3_g, m_mlp_up, m_mlp_down, v_norm1_g, v_w_in, v_hg_lower_bounds, v_hg_norm_g, v_sw_q_norm_g, v_sw_k_norm_g, v_sw_sinks, v_w_out, v_norm2_g, v_mem_norm_g, v_xa_wq, v_xa_wkv, v_xa_q_norm_g, v_xa_k_norm_g, v_xa_wo, v_norm3_g, v_mlp_up, v_mlp_down):
    given = dict(locals())
    weights = {n: given[n] for n in WEIGHT_ORDER}
    moms = {n: given["m_" + n] for n in WEIGHT_ORDER}
    vars_ = {n: given["v_" + n] for n in WEIGHT_ORDER}
    d = x.shape[-1]
    ff = mlp_down.shape[1] * 4
    small = {n: weights[n] for n in SMALL_NAMES}

    comm = _MeshWeights({n: weights[n][0].astype(_MXU_DTYPE) for n in BIG_NAMES}, d, ff)
    loss_row, grad_x, g_small = _local_step(x, mem, positions, loss_target, small, comm)
    big_grads = comm.finish(grad_x)

    packed, starts = _pack_rows([g_small[n] for n in SMALL_NAMES] + [loss_row], 1024)
    summed = _all_reduce_small(packed)
    small_grads = {}
    for n, s in zip(SMALL_NAMES, starts):
        r, c = weights[n].shape
        small_grads[n] = summed[s:s + r, 0:c]
    loss = summed[starts[-1], 0]

    grads, deltas, new_m, new_v = {}, {}, {}, {}
    for n in BIG_NAMES:
        shp = weights[n].shape
        g2 = big_grads[n]
        dl, mo, vo = _adamw_big(weights[n][0], g2, moms[n][0], vars_[n][0], name="adamw_" + n)
        grads[n], deltas[n], new_m[n], new_v[n] = (a.reshape(shp) for a in (g2, dl, mo, vo))
    sm_out = _adamw_small([weights[n] for n in SMALL_NAMES], [small_grads[n] for n in SMALL_NAMES],
                          [moms[n] for n in SMALL_NAMES], [vars_[n] for n in SMALL_NAMES])
    ns = len(SMALL_NAMES)
    for i, n in enumerate(SMALL_NAMES):
        grads[n], deltas[n], new_m[n], new_v[n] = small_grads[n], sm_out[i], sm_out[ns + i], sm_out[2 * ns + i]

    return (loss, grad_x, *[grads[n] for n in WEIGHT_ORDER], *[deltas[n] for n in WEIGHT_ORDER],
            *[new_m[n] for n in WEIGHT_ORDER], *[new_v[n] for n in WEIGHT_ORDER])
```

```python
import numpy as np
import jax
import jax.numpy as jnp
from jax import lax
from jax.experimental import pallas as pl
from jax.experimental.pallas import tpu as pltpu

F32 = jnp.float32
_MXU_DTYPE = jnp.bfloat16

EPS = 1e-6
HG_HEADS = 4
HG_D = 128
HG_CHUNK = 64
HG_TILE = 512
HG_LEVELS = (32, 16, 8, 4, 2, 1)
SW_HEADS = 8
SW_KV_HEADS = 2
SW_GROUP = SW_HEADS // SW_KV_HEADS
SW_HD = 64
SW_BLOCK = 128
ROPE_THETA = 500000.0
ROT_DIM = SW_HD // 4
XA_HEADS = 4
XA_HD = 128
HG_COLS = 4 * HG_HEADS * HG_D
SW_COLS = (SW_HEADS + 2 * SW_KV_HEADS) * SW_HD

ADAM_LR = 0.001
ADAM_B1 = 0.9
ADAM_B2 = 0.999
ADAM_EPS = 1e-08
ADAM_WD = 0.01
ADAM_STEP = 10

VMEM_LIMIT = 56 * 1024 * 1024
MESH = pl.DeviceIdType.MESH

NN = ((1,), (0,))
NT = ((1,), (1,))
TN = ((0,), (0,))


def _mx(v):
    return v.astype(_MXU_DTYPE)


def _dot(a, b, dims=NN):
    return lax.dot_general(_mx(a), _mx(b), (dims, ((), ())), preferred_element_type=F32)


def _split_dot(a, v, dims, parts):
    acc = None
    rest = v
    for p in range(parts):
        piece = _mx(rest)
        term = lax.dot_general(a, piece, (dims, ((), ())), preferred_element_type=F32)
        acc = term if acc is None else acc + term
        if p + 1 < parts:
            rest = rest - piece.astype(F32)
    return acc


def _params(sem):
    return pltpu.CompilerParams(dimension_semantics=sem, vmem_limit_bytes=VMEM_LIMIT)


def _mm(a, b, mode, m, n, k, *, name, tm=1024, tn=1024, tk=1024, a_spec=None, b_spec=None, extras=(), epilogue=None,
        out_dtypes=(F32,), out_shape=None, out_spec=None, after=()):
    after = tuple(t for t in after if t is not None)
    tm, tn, tk = min(tm, m), min(tn, n), min(tk, k)
    assert m % tm == 0 and n % tn == 0 and k % tk == 0, (name, m, n, k, tm, tn, tk)
    gi, gj, gk = m // tm, n // tn, k // tk
    if a_spec is None:
        a_spec = (pl.BlockSpec((tk, tm), lambda i, j, kk: (kk, i)) if mode == TN
                  else pl.BlockSpec((tm, tk), lambda i, j, kk: (i, kk)))
    if b_spec is None:
        b_spec = (pl.BlockSpec((tn, tk), lambda i, j, kk: (j, kk)) if mode == NT
                  else pl.BlockSpec((tk, tn), lambda i, j, kk: (kk, j)))
    mn_spec = pl.BlockSpec((tm, tn), lambda i, j, kk: (i, j))
    if epilogue is None:
        epilogue = lambda acc: (acc,)
    n_ex, n_out = len(extras), len(out_dtypes)
    if out_shape is None:
        out_shape = tuple(jax.ShapeDtypeStruct((m, n), d) for d in out_dtypes)
        out_spec = tuple(mn_spec for _ in out_dtypes)

    n_after = len(after)

    def body(*refs):
        a_ref, b_ref = refs[0], refs[1]
        ex = refs[2:2 + n_ex]
        outs = refs[2 + n_ex + n_after:2 + n_ex + n_after + n_out]

        def finish(acc):
            res = epilogue(acc, *[e[...] for e in ex])
            for o, r in zip(outs, res):
                o[...] = r.astype(o.dtype)

        if gk == 1:
            finish(_dot(a_ref[...], b_ref[...], mode))
        else:
            acc_ref = refs[-1]
            kk = pl.program_id(2)

            @pl.when(kk == 0)
            def _():
                acc_ref[...] = jnp.zeros_like(acc_ref)

            acc_ref[...] += _dot(a_ref[...], b_ref[...], mode)

            @pl.when(kk == gk - 1)
            def _():
                finish(acc_ref[...])

    return pl.pallas_call(
        body, name=name, grid=(gi, gj, gk),
        in_specs=[a_spec, b_spec] + [mn_spec] * n_ex + [pl.BlockSpec(memory_space=pl.ANY)] * n_after,
        out_specs=out_spec, out_shape=out_shape,
        scratch_shapes=[pltpu.VMEM((tm, tn), F32)] if gk > 1 else [],
        compiler_params=_params(("parallel", "parallel", "arbitrary")),
    )(a, b, *extras, *after)


def _rms_fwd(x, g, *, name, tm=512):
    t, d = x.shape
    tm = min(tm, t)

    def body(x_ref, g_ref, o_ref):
        xv = x_ref[...]
        r = lax.rsqrt(jnp.mean(xv * xv, axis=1, keepdims=True) + EPS)
        o_ref[...] = (xv * r * g_ref[...]).astype(o_ref.dtype)

    return pl.pallas_call(
        body, name=name, grid=(t // tm,),
        in_specs=[pl.BlockSpec((tm, d), lambda i: (i, 0)), pl.BlockSpec((1, d), lambda i: (0, 0))],
        out_specs=pl.BlockSpec((tm, d), lambda i: (i, 0)),
        out_shape=jax.ShapeDtypeStruct((t, d), _MXU_DTYPE),
        compiler_params=_params(("parallel",)),
    )(x, g)


def _rms_bwd(x, g, dy, dres, *, name, tm=512, operand_copy=False):
    t, d = x.shape
    tm = min(tm, t)
    has_res = dres is not None

    def body(*refs):
        x_ref, g_ref, dy_ref = refs[:3]
        dx_ref, dg_ref = refs[3 + has_res], refs[-1]
        xv, dyv = x_ref[...], dy_ref[...]
        r = lax.rsqrt(jnp.mean(xv * xv, axis=1, keepdims=True) + EPS)
        u = dyv * g_ref[...]
        dx = r * u - xv * (r * r * r) * jnp.mean(u * xv, axis=1, keepdims=True)
        if has_res:
            dx = dx + refs[3][...]
        dx_ref[...] = dx
        if operand_copy:
            refs[-2][...] = _mx(dx)

        @pl.when(pl.program_id(0) == 0)
        def _():
            dg_ref[...] = jnp.zeros_like(dg_ref)

        dg_ref[...] += jnp.sum(dyv * xv * r, axis=0, keepdims=True)

    row = pl.BlockSpec((tm, d), lambda i: (i, 0))
    vec = pl.BlockSpec((1, d), lambda i: (0, 0))
    copy = [jax.ShapeDtypeStruct((t, d), _MXU_DTYPE)] if operand_copy else []
    return pl.pallas_call(
        body, name=name, grid=(t // tm,),
        in_specs=[row, vec, row] + ([row] if has_res else []),
        out_specs=tuple([row] * (1 + len(copy)) + [vec]),
        out_shape=tuple([jax.ShapeDtypeStruct((t, d), F32)] + copy + [jax.ShapeDtypeStruct((1, d), F32)]),
        compiler_params=_params(("arbitrary",)),
    )(*([x, g, dy] + ([dres] if has_res else [])))


def _hg_constants():
    c = HG_CHUNK
    t = np.arange(c)
    sums = [t[None, :] <= t[:, None]]
    masks = []
    for m in HG_LEVELS:
        base = (t // (2 * m)) * (2 * m)
        mid = base + m - 1
        second = (t - base) >= m
        upper = (t[None, :] > mid[:, None]) & (t[None, :] <= t[:, None])
        lower = (t[None, :] > t[:, None]) & (t[None, :] <= mid[:, None])
        sums.append(np.where(second[:, None], upper, lower))
        masks.append(second[:, None] & (~second)[None, :] & (base[:, None] == base[None, :]))
    return (np.concatenate(sums, axis=0).astype(np.float32), np.stack(masks).astype(np.float32))


HG_HEAD_LANES = tuple(slice(HG_D * h, HG_D * (h + 1)) for h in range(HG_HEADS))


def _per_head(fn, slab):
    return jnp.concatenate([jnp.broadcast_to(fn(slab[:, hs]), (slab.shape[0], HG_D)) for hs in HG_HEAD_LANES], axis=1)


def _lane_sum(v):
    return jnp.sum(v, axis=1, keepdims=True)


def _lane_mean(v):
    return jnp.mean(v, axis=1, keepdims=True)


def _hg_gates(blk, lbp):
    w = HG_HEADS * HG_D
    q, x, v, gl = blk[:, 0:w], blk[:, w:2 * w], blk[:, 2 * w:3 * w], blk[:, 3 * w:4 * w]
    mx = jnp.max(lbp, axis=0, keepdims=True)
    e = jnp.exp(lbp - mx)
    lb = e[0:1, :] / jnp.sum(e, axis=0, keepdims=True)
    sig = jax.nn.sigmoid(x)
    f = lb + (1.0 - lb) * sig
    return q, v, gl, lb, sig, f, 1.0 - f, jnp.log(f)


def _hg_fwd(proj, lbp, ng, bsz, seq, *, y_width):
    t = proj.shape[0]
    nc = seq // HG_CHUNK
    a_np, m_np = _hg_constants()
    a_all = jnp.asarray(a_np, _MXU_DTYPE)
    masks = jnp.asarray(m_np, F32)
    nl = len(HG_LEVELS)

    ts = min(HG_TILE, seq)
    ns, nct = seq // ts, ts // HG_CHUNK
    hw = HG_HEADS * HG_D

    def body(p_ref, lb_ref, ng_ref, a_ref, m_ref, y_ref, o_ref, st_ref, carry):
        a_mat = a_ref[...]
        ngv = ng_ref[...]

        @pl.when(pl.program_id(1) == 0)
        def _():
            carry[...] = jnp.zeros_like(carry)

        ng4 = _tile_lanes(ngv, HG_HEADS)
        heads = range(HG_HEADS)
        hl = HG_HEAD_LANES

        def chunk(c, _):
            rows = pl.ds(pl.multiple_of(c * HG_CHUNK, HG_CHUNK), HG_CHUNK)
            q, v, gl, lb, sig, f, k, g = _hg_gates(p_ref[rows, :], lb_ref[...])
            sts = [carry[h] for h in heads]
            e_all = _split_dot(a_mat, g, NN, 3)
            b = e_all[0:HG_CHUNK]
            qb = q * jnp.exp(b)
            o = [_dot(qb[:, hl[h]], sts[h], NT) for h in heads]
            p = [jnp.zeros((HG_CHUNK, HG_CHUNK), F32) for _ in heads]
            for li in range(nl):
                e = jnp.exp(e_all[HG_CHUNK * (li + 1):HG_CHUNK * (li + 2)])
                qm, km, mk = q * e, k * e, m_ref[li]
                p = [p[h] + mk * _dot(qm[:, hl[h]], km[:, hl[h]], NT) for h in heads]
            bl = b[HG_CHUNK - 1:HG_CHUNK, :]
            kd = k * jnp.exp(bl - b)
            ebl = jnp.exp(bl)
            pv = [_dot(p[h], v[:, hl[h]]) for h in heads]
            upd = [_dot(v[:, hl[h]], kd[:, hl[h]], TN) for h in heads]
            o_all = jnp.concatenate([o[h] + pv[h] for h in heads], axis=1) + _per_head(_lane_sum, q * k) * v
            r = lax.rsqrt(_per_head(_lane_mean, o_all * o_all) + EPS)
            for h in heads:
                st_ref[h, c] = sts[h]
                carry[h] = sts[h] * ebl[:, hl[h]] + upd[h]
            o_ref[rows, :] = o_all
            y_ref[rows, :] = (o_all * r * ng4) * (gl * jax.nn.sigmoid(gl))
            return 0

        lax.fori_loop(0, nct, chunk, 0)

    return pl.pallas_call(
        body, name="hgrn2_fwd", grid=(bsz, ns),
        in_specs=[pl.BlockSpec((ts, HG_COLS), lambda b, s: (b * ns + s, 0)),
                  pl.BlockSpec((2, hw), lambda b, s: (0, 0)),
                  pl.BlockSpec((1, HG_D), lambda b, s: (0, 0)),
                  pl.BlockSpec(a_all.shape, lambda b, s: (0, 0)),
                  pl.BlockSpec(masks.shape, lambda b, s: (0, 0, 0))],
        out_specs=(pl.BlockSpec((ts, hw), lambda b, s: (b * ns + s, 0)),
                   pl.BlockSpec((ts, hw), lambda b, s: (b * ns + s, 0)),
                   pl.BlockSpec((None, HG_HEADS, nct, HG_D, HG_D), lambda b, s: (b, 0, s, 0, 0))),
        out_shape=(jax.ShapeDtypeStruct((t, y_width), F32),
                   jax.ShapeDtypeStruct((t, hw), F32),
                   jax.ShapeDtypeStruct((bsz, HG_HEADS, nc, HG_D, HG_D), F32)),
        scratch_shapes=[pltpu.VMEM((HG_HEADS, HG_D, HG_D), F32)],
        compiler_params=_params(("parallel", "arbitrary")),
    )(proj, lbp, ng, a_all, masks)


def _hg_bwd(proj, lbp, ng, o_all, states, dy, bsz, seq, after=()):
    after = tuple(a for a in after if a is not None)
    t = proj.shape[0]
    nc = seq // HG_CHUNK
    a_np, m_np = _hg_constants()
    a_all = jnp.asarray(a_np, _MXU_DTYPE)
    masks = jnp.asarray(m_np, F32)
    nl = len(HG_LEVELS)
    cs = HG_CHUNK

    ts = min(HG_TILE, seq)
    ns, nct = seq // ts, ts // cs
    hw = HG_HEADS * HG_D

    def body(p_ref, lb_ref, ng_ref, a_ref, m_ref, o_ref, st_ref, dy_ref, *rest):
        dp_ref, dlb_ref, dng_ref, dst_ref = rest[len(after):]
        a_mat = a_ref[...]
        ngv = ng_ref[...]
        ng4 = _tile_lanes(ngv, HG_HEADS)
        last_row = lax.broadcasted_iota(jnp.int32, (cs, hw), 0) == cs - 1
        si = pl.program_id(1)
        first = jnp.logical_and(pl.program_id(0) == 0, si == 0)
        heads = range(HG_HEADS)
        hl = HG_HEAD_LANES

        @pl.when(si == 0)
        def _():
            dst_ref[...] = jnp.zeros_like(dst_ref)

        def side_by_side(parts):
            return jnp.concatenate(parts, axis=1)

        def chunk(i, carry):
            dlb_acc, dng_acc = carry
            c = nct - 1 - i
            rows = pl.ds(pl.multiple_of(c * cs, cs), cs)
            q, v, gl, lb, sig, f, k, g = _hg_gates(p_ref[rows, :], lb_ref[...])
            o = o_ref[rows, :]
            dyv = dy_ref[rows, :]
            sts = [st_ref[h, c] for h in heads]
            dsts = [dst_ref[h] for h in heads]
            e_all = _split_dot(a_mat, g, NN, 3)
            b = e_all[0:cs]
            eb = jnp.exp(b)
            bl = b[cs - 1:cs, :]
            ebl = jnp.exp(bl)
            ekd = jnp.exp(bl - b)
            qb, kd = q * eb, k * ekd
            sg = jax.nn.sigmoid(gl)
            silu = gl * sg
            r = lax.rsqrt(_per_head(_lane_mean, o * o) + EPS)
            dgl = dyv * (o * r * ng4) * (sg * (1.0 + gl * (1.0 - sg)))
            u = dyv * silu * ng4
            do = r * u - o * (r * r * r) * _per_head(_lane_mean, u * o)
            dng4 = jnp.sum(dyv * silu * o * r, axis=0, keepdims=True)
            dng_acc = dng_acc + ((dng4[:, hl[0]] + dng4[:, hl[1]]) + (dng4[:, hl[2]] + dng4[:, hl[3]]))
            es, qm, km = [], [], []
            p = [jnp.zeros((cs, cs), F32) for _ in heads]
            for li in range(nl):
                e = jnp.exp(e_all[cs * (li + 1):cs * (li + 2)])
                es.append(e)
                qm.append(q * e)
                km.append(k * e)
                mk = m_ref[li]
                p = [p[h] + mk * _dot(qm[li][:, hl[h]], km[li][:, hl[h]], NT) for h in heads]
            dp = [_dot(do[:, hl[h]], v[:, hl[h]], NT) for h in heads]
            dv_p = [_dot(p[h], do[:, hl[h]], TN) for h in heads]
            dv_s = [_dot(kd[:, hl[h]], dsts[h], NT) for h in heads]
            dqb = side_by_side([_dot(do[:, hl[h]], sts[h]) for h in heads])
            dkd = side_by_side([_dot(v[:, hl[h]], dsts[h]) for h in heads])
            new_dst = [_dot(do[:, hl[h]], qb[:, hl[h]], TN) for h in heads]
            dv = side_by_side([dv_p[h] + dv_s[h] for h in heads]) + _per_head(_lane_sum, q * k) * do
            dq = dqb * eb
            dk = dkd * ekd
            db = dqb * qb - dkd * kd
            dbl = (jnp.sum(dkd * kd, axis=0, keepdims=True)
                   + side_by_side([jnp.sum(dsts[h] * sts[h], axis=0, keepdims=True) for h in heads]) * ebl)
            de = [db + jnp.where(last_row, dbl, 0.0)]
            for li in range(nl):
                mk = m_ref[li]
                dpm = [mk * dp[h] for h in heads]
                dqm = side_by_side([_dot(dpm[h], km[li][:, hl[h]]) for h in heads])
                dkm = side_by_side([_dot(dpm[h], qm[li][:, hl[h]], TN) for h in heads])
                dq = dq + dqm * es[li]
                dk = dk + dkm * es[li]
                de.append(dqm * qm[li] + dkm * km[li])
            dpd = _per_head(_lane_sum, do * v)
            dq = dq + dpd * k
            dk = dk + dpd * q
            dg = _split_dot(a_mat, jnp.concatenate(de, axis=0), TN, 2)
            df = dg / f - dk
            dp_ref[rows, 0:hw] = _mx(dq)
            dp_ref[rows, hw:2 * hw] = _mx(df * (1.0 - lb) * sig * (1.0 - sig))
            dp_ref[rows, 2 * hw:3 * hw] = _mx(dv)
            dp_ref[rows, 3 * hw:4 * hw] = _mx(dgl)
            for h in heads:
                dst_ref[h] = dsts[h] * ebl[:, hl[h]] + new_dst[h]
            return dlb_acc + jnp.sum(df * (1.0 - sig), axis=0, keepdims=True), dng_acc

        dlb, dng = lax.fori_loop(0, nct, chunk, (jnp.zeros((1, hw), F32), jnp.zeros((1, HG_D), F32)))

        @pl.when(first)
        def _():
            dlb_ref[...] = jnp.zeros_like(dlb_ref)
            dng_ref[...] = jnp.zeros_like(dng_ref)

        lbp_v = lb_ref[...]
        mx = jnp.max(lbp_v, axis=0, keepdims=True)
        e = jnp.exp(lbp_v - mx)
        s0 = e[0:1, :] / jnp.sum(e, axis=0, keepdims=True)
        da0 = dlb * s0 * (1.0 - s0)
        dlb_ref[...] += jnp.concatenate([da0, -da0], axis=0)
        dng_ref[...] += dng

    def tile(b, s):
        return b * ns + (ns - 1 - s)

    return pl.pallas_call(
        body, name="hgrn2_bwd", grid=(bsz, ns),
        in_specs=[pl.BlockSpec((ts, HG_COLS), lambda b, s: (tile(b, s), 0)),
                  pl.BlockSpec((2, hw), lambda b, s: (0, 0)),
                  pl.BlockSpec((1, HG_D), lambda b, s: (0, 0)),
                  pl.BlockSpec(a_all.shape, lambda b, s: (0, 0)),
                  pl.BlockSpec(masks.shape, lambda b, s: (0, 0, 0)),
                  pl.BlockSpec((ts, hw), lambda b, s: (tile(b, s), 0)),
                  pl.BlockSpec((None, HG_HEADS, nct, HG_D, HG_D), lambda b, s: (b, 0, ns - 1 - s, 0, 0)),
                  pl.BlockSpec((ts, hw), lambda b, s: (tile(b, s), 0))] + [pl.BlockSpec(memory_space=pl.ANY)] * len(after),
        out_specs=(pl.BlockSpec((ts, HG_COLS), lambda b, s: (tile(b, s), 0)),
                   pl.BlockSpec((2, hw), lambda b, s: (0, 0)),
                   pl.BlockSpec((1, HG_D), lambda b, s: (0, 0))),
        out_shape=(jax.ShapeDtypeStruct((t, HG_COLS), _MXU_DTYPE),
                   jax.ShapeDtypeStruct((2, hw), F32),
                   jax.ShapeDtypeStruct((1, HG_D), F32)),
        scratch_shapes=[pltpu.VMEM((HG_HEADS, HG_D, HG_D), F32)],
        compiler_params=_params(("arbitrary", "arbitrary")),
    )(proj, lbp, ng, a_all, masks, o_all, states, dy, *after)


def _sw_constants():
    half = ROT_DIM // 2
    inv = (np.float32(ROPE_THETA) ** (-(np.arange(half, dtype=np.float32) * np.float32(2.0) / np.float32(ROT_DIM)))
           ).astype(np.float32)
    freq = np.zeros((1, 128), np.float32)
    sign = np.zeros((1, 128), np.float32)
    for h in range(2):
        freq[0, 64 * h:64 * h + half] = inv
        freq[0, 64 * h + half:64 * h + 2 * half] = inv
        sign[0, 64 * h:64 * h + half] = -1.0
        sign[0, 64 * h + half:64 * h + 2 * half] = 1.0
    seg = np.kron(np.eye(8, dtype=np.float32), np.full((64, 64), 1.0 / 64.0, np.float32))
    return freq, sign, seg


def _rope_tables(pos, freq, sign):
    ang = pos.astype(F32) * freq
    return jnp.cos(ang), jnp.sin(ang) * sign


def _tile_lanes(v, times):
    return v if times == 1 else jnp.concatenate([v] * times, axis=1)


def _swap_halves(v):
    w = v.shape[1]
    half = ROT_DIM // 2
    lane = lax.broadcasted_iota(jnp.int32, v.shape, 1) % SW_HD
    return jnp.where(lane < half, pltpu.roll(v, w - half, 1), jnp.where(lane < 2 * half, pltpu.roll(v, half, 1), 0.0))


def _sw_norm_rope(tv, gain, seg, cosv, sinv):
    w = tv.shape[1]
    ms = _split_dot_rhs(tv * tv, seg[0:w, 0:w])
    r = lax.rsqrt(ms + EPS)
    tn = tv * r * gain
    reps = w // 128
    return tn * _tile_lanes(cosv, reps) + _swap_halves(tn) * _tile_lanes(sinv, reps), r


def _split_dot_rhs(v, a):
    hi = _mx(v)
    lo = _mx(v - hi.astype(F32))
    return (lax.dot_general(hi, a, (NN, ((), ())), preferred_element_type=F32)
            + lax.dot_general(lo, a, (NN, ((), ())), preferred_element_type=F32))


def _sw_norm_rope_bwd(dt, tv, r, gain, seg, cosv, sinv):
    w = tv.shape[1]
    reps = w // 128
    dtn = dt * _tile_lanes(cosv, reps) + _swap_halves(dt * _tile_lanes(sinv, reps))
    u = dtn * gain
    dtv = r * u - tv * (r * r * r) * _split_dot_rhs(u * tv, seg[0:w, 0:w])
    return dtv, jnp.sum(dtn * tv * r, axis=0, keepdims=True)


def _sw_scores(qh, kp, kc):
    return _dot(qh, kp, NT), _dot(qh, kc, NT)


def _sw_probs(raw, sink, first_block):
    scale = SW_HD ** -0.5
    qi = lax.broadcasted_iota(jnp.int32, (SW_BLOCK, SW_BLOCK), 0)
    kj = lax.broadcasted_iota(jnp.int32, (SW_BLOCK, SW_BLOCK), 1)
    ok_prev = jnp.logical_and(kj > qi, jnp.logical_not(first_block))
    ok_cur = kj <= qi
    sp = jnp.where(ok_prev, raw[0] * scale, -jnp.inf)
    sc = jnp.where(ok_cur, raw[1] * scale, -jnp.inf)
    m = jnp.maximum(jnp.maximum(jnp.max(sp, axis=1, keepdims=True), jnp.max(sc, axis=1, keepdims=True)), sink)
    pp, pc = jnp.exp(sp - m), jnp.exp(sc - m)
    es = jnp.exp(sink - m)
    den = jnp.sum(pp, axis=1, keepdims=True) + jnp.sum(pc, axis=1, keepdims=True) + es
    return pp / den, pc / den, es / den


def _sw_specs(nb):
    def cur(b, n):
        return b * nb + jnp.minimum(n, nb - 1)

    def prev(b, n):
        return b * nb + jnp.maximum(jnp.minimum(n, nb - 1) - 1, 0)

    return cur, prev


def _sw_fwd(proj, pos, qg, kg, sinks, y_in, bsz, seq):
    t = proj.shape[0]
    nb = seq // SW_BLOCK
    freq_np, sign_np, seg_np = _sw_constants()
    freq, sign = jnp.asarray(freq_np), jnp.asarray(sign_np)
    seg = jnp.asarray(seg_np, _MXU_DTYPE)
    cur, prev = _sw_specs(nb)

    def body(q_ref, kc_ref, kp_ref, vc_ref, vp_ref, pc_ref, pp_ref, qg_ref, kg_ref, sk_ref, fr_ref, sn_ref, seg_ref,
             yin_ref, y_ref):
        del yin_ref
        n = pl.program_id(1)
        segv = seg_ref[...]
        cos_c, sin_c = _rope_tables(pc_ref[...], fr_ref[...], sn_ref[...])
        cos_p, sin_p = _rope_tables(pp_ref[...], fr_ref[...], sn_ref[...])
        qr, _ = _sw_norm_rope(q_ref[...], qg_ref[...], segv, cos_c, sin_c)
        kcr, _ = _sw_norm_rope(kc_ref[...], kg_ref[...], segv, cos_c, sin_c)
        kpr, _ = _sw_norm_rope(kp_ref[...], kg_ref[...], segv, cos_p, sin_p)
        vc, vp = vc_ref[...], vp_ref[...]
        ks = [slice(SW_HD * (h // SW_GROUP), SW_HD * (h // SW_GROUP + 1)) for h in range(SW_HEADS)]
        raw = [_sw_scores(qr[:, SW_HD * h:SW_HD * (h + 1)], kpr[:, ks[h]], kcr[:, ks[h]]) for h in range(SW_HEADS)]
        probs = [_sw_probs(raw[h], sk_ref[0, h], n == 0) for h in range(SW_HEADS)]
        for h in range(SW_HEADS):
            y_ref[:, SW_HD * h:SW_HD * (h + 1)] = _dot(probs[h][0], vp[:, ks[h]]) + _dot(probs[h][1], vc[:, ks[h]])

    rowq = pl.BlockSpec((SW_BLOCK, 512), lambda b, n: (cur(b, n), 0))
    full = lambda a: pl.BlockSpec(a.shape, lambda b, n: (0,) * a.ndim)
    yw = y_in.shape[1]
    return pl.pallas_call(
        body, name="swa_fwd", grid=(bsz, nb),
        in_specs=[rowq,
                  pl.BlockSpec((SW_BLOCK, 128), lambda b, n: (cur(b, n), 4)),
                  pl.BlockSpec((SW_BLOCK, 128), lambda b, n: (prev(b, n), 4)),
                  pl.BlockSpec((SW_BLOCK, 128), lambda b, n: (cur(b, n), 5)),
                  pl.BlockSpec((SW_BLOCK, 128), lambda b, n: (prev(b, n), 5)),
                  pl.BlockSpec((SW_BLOCK, 1), lambda b, n: (cur(b, n), 0)),
                  pl.BlockSpec((SW_BLOCK, 1), lambda b, n: (prev(b, n), 0)),
                  full(qg), full(kg),
                  pl.BlockSpec(memory_space=pltpu.SMEM),
                  full(freq), full(sign), full(seg),
                  pl.BlockSpec(memory_space=pl.ANY)],
        out_specs=pl.BlockSpec((SW_BLOCK, 512), lambda b, n: (cur(b, n), 1)),
        out_shape=jax.ShapeDtypeStruct((t, yw), F32),
        input_output_aliases={13: 0},
        compiler_params=_params(("parallel", "parallel")),
    )(proj, proj, proj, proj, proj, pos, pos, qg, kg, sinks, freq, sign, seg, y_in)


def _sw_bwd(proj, pos, qg, kg, sinks, y, dy, bsz, seq):
    t = proj.shape[0]
    nb = seq // SW_BLOCK
    freq_np, sign_np, seg_np = _sw_constants()
    freq, sign = jnp.asarray(freq_np), jnp.asarray(sign_np)
    seg = jnp.asarray(seg_np, _MXU_DTYPE)
    cur, prev = _sw_specs(nb)
    scale = SW_HD ** -0.5

    def body(q_ref, kc_ref, kp_ref, vc_ref, vp_ref, pc_ref, pp_ref, qg_ref, kg_ref, sk_ref, fr_ref, sn_ref, seg_ref,
             y_ref, dy_ref, dp_ref, dqg_ref, dkg_ref, dsk_ref,
             dq_car, dkv_car, dqr_s, dkc_s, dkp_s, dvc_s, dvp_s, gq_acc, gk_acc, sk_acc):
        b, n = pl.program_id(0), pl.program_id(1)
        first = jnp.logical_and(b == 0, n == 0)
        last = jnp.logical_and(b == pl.num_programs(0) - 1, n == nb)

        @pl.when(first)
        def _():
            gq_acc[...] = jnp.zeros_like(gq_acc)
            gk_acc[...] = jnp.zeros_like(gk_acc)
            sk_acc[...] = jnp.zeros_like(sk_acc)

        @pl.when(n < nb)
        def _():
            segv = seg_ref[...]
            cos_c, sin_c = _rope_tables(pc_ref[...], fr_ref[...], sn_ref[...])
            cos_p, sin_p = _rope_tables(pp_ref[...], fr_ref[...], sn_ref[...])
            qv, kcv, kpv = q_ref[...], kc_ref[...], kp_ref[...]
            qr, rq = _sw_norm_rope(qv, qg_ref[...], segv, cos_c, sin_c)
            kcr, rkc = _sw_norm_rope(kcv, kg_ref[...], segv, cos_c, sin_c)
            kpr, rkp = _sw_norm_rope(kpv, kg_ref[...], segv, cos_p, sin_p)
            vc, vp = vc_ref[...], vp_ref[...]
            lane = lax.broadcasted_iota(jnp.int32, (1, 128), 1)
            dsk = jnp.zeros((1, 128), F32)
            heads = range(SW_HEADS)
            ks = [slice(SW_HD * (h // SW_GROUP), SW_HD * (h // SW_GROUP + 1)) for h in heads]
            hs = [slice(SW_HD * h, SW_HD * (h + 1)) for h in heads]
            qh = [qr[:, hs[h]] for h in heads]
            doh = [dy_ref[:, hs[h]] for h in heads]
            raw = [_sw_scores(qh[h], kpr[:, ks[h]], kcr[:, ks[h]]) for h in heads]
            dpp = [_dot(doh[h], vp[:, ks[h]], NT) for h in heads]
            dpc = [_dot(doh[h], vc[:, ks[h]], NT) for h in heads]
            probs = [_sw_probs(raw[h], sk_ref[0, h], n == 0) for h in heads]
            dsp, dsc = [], []
            for h in heads:
                pp, pc, ps = probs[h]
                delta = jnp.sum(doh[h] * y_ref[:, hs[h]], axis=1, keepdims=True)
                dsp.append(pp * (dpp[h] - delta) * scale)
                dsc.append(pc * (dpc[h] - delta) * scale)
                dsk = dsk + jnp.where(lane == h, -jnp.sum(ps * delta), 0.0)
            for h in heads:
                dqr_s[:, hs[h]] = _dot(dsp[h], kpr[:, ks[h]]) + _dot(dsc[h], kcr[:, ks[h]])
            for kv in range(SW_KV_HEADS):
                group = range(SW_GROUP * kv, SW_GROUP * (kv + 1))
                kvs = slice(SW_HD * kv, SW_HD * (kv + 1))
                dvp_s[:, kvs] = sum(_dot(probs[h][0], doh[h], TN) for h in group)
                dvc_s[:, kvs] = sum(_dot(probs[h][1], doh[h], TN) for h in group)
                dkp_s[:, kvs] = sum(_dot(dsp[h], qh[h], TN) for h in group)
                dkc_s[:, kvs] = sum(_dot(dsc[h], qh[h], TN) for h in group)
            dq, gq = _sw_norm_rope_bwd(dqr_s[...], qv, rq, qg_ref[...], segv, cos_c, sin_c)
            dkc, gkc = _sw_norm_rope_bwd(dkc_s[...], kcv, rkc, kg_ref[...], segv, cos_c, sin_c)
            dkp, gkp = _sw_norm_rope_bwd(dkp_s[...], kpv, rkp, kg_ref[...], segv, cos_p, sin_p)
            gq_acc[...] += gq
            gk_acc[...] += gkc + gkp
            sk_acc[...] += dsk

            @pl.when(n > 0)
            def _():
                dp_ref[:, 0:512] = _mx(dq_car[...])
                dp_ref[:, 512:640] = _mx(dkv_car[:, 0:128] + dkp)
                dp_ref[:, 640:768] = _mx(dkv_car[:, 128:256] + dvp_s[...])

            dq_car[...] = dq
            dkv_car[:, 0:128] = dkc
            dkv_car[:, 128:256] = dvc_s[...]

        @pl.when(n == nb)
        def _():
            dp_ref[:, 0:512] = _mx(dq_car[...])
            dp_ref[:, 512:768] = _mx(dkv_car[...])

        @pl.when(last)
        def _():
            gq = gq_acc[...]
            acc = gq[:, 0:SW_HD]
            for h in range(1, SW_HEADS):
                acc = acc + gq[:, SW_HD * h:SW_HD * (h + 1)]
            dqg_ref[...] = acc
            gk = gk_acc[...]
            dkg_ref[...] = gk[:, 0:SW_HD] + gk[:, SW_HD:2 * SW_HD]
            dsk_ref[...] = sk_acc[...]

    rowq = pl.BlockSpec((SW_BLOCK, 512), lambda b, n: (cur(b, n), 0))
    full = lambda a: pl.BlockSpec(a.shape, lambda b, n: (0,) * a.ndim)

    def out_row(b, n):
        return b * nb + jnp.maximum(n - 1, 0)

    return pl.pallas_call(
        body, name="swa_bwd", grid=(bsz, nb + 1),
        in_specs=[rowq,
                  pl.BlockSpec((SW_BLOCK, 128), lambda b, n: (cur(b, n), 4)),
                  pl.BlockSpec((SW_BLOCK, 128), lambda b, n: (prev(b, n), 4)),
                  pl.BlockSpec((SW_BLOCK, 128), lambda b, n: (cur(b, n), 5)),
                  pl.BlockSpec((SW_BLOCK, 128), lambda b, n: (prev(b, n), 5)),
                  pl.BlockSpec((SW_BLOCK, 1), lambda b, n: (cur(b, n), 0)),
                  pl.BlockSpec((SW_BLOCK, 1), lambda b, n: (prev(b, n), 0)),
                  full(qg), full(kg),
                  pl.BlockSpec(memory_space=pltpu.SMEM),
                  full(freq), full(sign), full(seg),
                  pl.BlockSpec((SW_BLOCK, 512), lambda b, n: (cur(b, n), 1)),
                  pl.BlockSpec((SW_BLOCK, 512), lambda b, n: (cur(b, n), 1))],
        out_specs=(pl.BlockSpec((SW_BLOCK, SW_COLS), lambda b, n: (out_row(b, n), 0)),
                   pl.BlockSpec((1, SW_HD), lambda b, n: (0, 0)),
                   pl.BlockSpec((1, SW_HD), lambda b, n: (0, 0)),
                   pl.BlockSpec((1, 128), lambda b, n: (0, 0))),
        out_shape=(jax.ShapeDtypeStruct((t, SW_COLS), _MXU_DTYPE),
                   jax.ShapeDtypeStruct((1, SW_HD), F32),
                   jax.ShapeDtypeStruct((1, SW_HD), F32),
                   jax.ShapeDtypeStruct((1, 128), F32)),
        scratch_shapes=[pltpu.VMEM((SW_BLOCK, 512), F32), pltpu.VMEM((SW_BLOCK, 256), F32),
                        pltpu.VMEM((SW_BLOCK, 512), F32),
                        pltpu.VMEM((SW_BLOCK, 128), F32), pltpu.VMEM((SW_BLOCK, 128), F32),
                        pltpu.VMEM((SW_BLOCK, 128), F32), pltpu.VMEM((SW_BLOCK, 128), F32),
                        pltpu.VMEM((1, 512), F32), pltpu.VMEM((1, 128), F32), pltpu.VMEM((1, 128), F32)],
        compiler_params=_params(("arbitrary", "arbitrary")),
    )(proj, proj, proj, proj, proj, pos, pos, qg, kg, sinks, freq, sign, seg, y, dy)


def _head_rms(tv, gain):
    r = lax.rsqrt(jnp.mean(tv * tv, axis=1, keepdims=True) + EPS)
    return tv * r * gain, r


def _head_rms_bwd(dtn, tv, r, gain):
    u = dtn * gain
    return r * u - tv * (r * r * r) * jnp.mean(u * tv, axis=1, keepdims=True), jnp.sum(dtn * tv * r, axis=0, keepdims=True)


def _xa_softmax(raw):
    s = raw * (XA_HD ** -0.5)
    e = jnp.exp(s - jnp.max(s, axis=1, keepdims=True))
    return e / jnp.sum(e, axis=1, keepdims=True)


def _xa_fwd(qx, kvx, qg, kg, bsz, seq, mlen, *, tq=512):
    t = qx.shape[0]
    tq = min(tq, seq)
    nq = seq // tq
    w = XA_HEADS * XA_HD

    def body(q_ref, kv_ref, qg_ref, kg_ref, o_ref):
        heads = range(XA_HEADS)
        hs = [slice(XA_HD * h, XA_HD * (h + 1)) for h in heads]
        qn = [_head_rms(q_ref[:, hs[h]], qg_ref[...])[0] for h in heads]
        kn = [_head_rms(kv_ref[:, hs[h]], kg_ref[...])[0] for h in heads]
        raw = [_dot(qn[h], kn[h], NT) for h in heads]
        p = [_xa_softmax(raw[h]) for h in heads]
        for h in heads:
            o_ref[:, hs[h]] = _dot(p[h], kv_ref[:, w + XA_HD * h:w + XA_HD * (h + 1)]).astype(o_ref.dtype)

    vec = pl.BlockSpec((1, XA_HD), lambda b, i: (0, 0))
    return pl.pallas_call(
        body, name="xattn_fwd", grid=(bsz, nq),
        in_specs=[pl.BlockSpec((tq, w), lambda b, i: (b * nq + i, 0)),
                  pl.BlockSpec((mlen, 2 * w), lambda b, i: (b, 0)), vec, vec],
        out_specs=pl.BlockSpec((tq, w), lambda b, i: (b * nq + i, 0)),
        out_shape=jax.ShapeDtypeStruct((t, w), _MXU_DTYPE),
        compiler_params=_params(("parallel", "parallel")),
    )(qx, kvx, qg, kg)


def _xa_bwd(qx, kvx, qg, kg, do, bsz, seq, mlen, *, tq=512):
    t = qx.shape[0]
    tq = min(tq, seq)
    nq = seq // tq
    w = XA_HEADS * XA_HD
    scale = XA_HD ** -0.5

    def body(q_ref, kv_ref, qg_ref, kg_ref, do_ref, dq_ref, dkv_ref, dqg_ref, dkg_ref):
        b, i = pl.program_id(0), pl.program_id(1)

        @pl.when(jnp.logical_and(b == 0, i == 0))
        def _():
            dqg_ref[...] = jnp.zeros_like(dqg_ref)
            dkg_ref[...] = jnp.zeros_like(dkg_ref)

        @pl.when(i == 0)
        def _():
            dkv_ref[...] = jnp.zeros_like(dkv_ref)

        heads = range(XA_HEADS)
        hs = [slice(XA_HD * h, XA_HD * (h + 1)) for h in heads]
        vs = [slice(w + XA_HD * h, w + XA_HD * (h + 1)) for h in heads]
        qv = [q_ref[:, hs[h]] for h in heads]
        kv = [kv_ref[:, hs[h]] for h in heads]
        doh = [do_ref[:, hs[h]] for h in heads]
        qn = [_head_rms(qv[h], qg_ref[...]) for h in heads]
        kn = [_head_rms(kv[h], kg_ref[...]) for h in heads]
        raw = [_dot(qn[h][0], kn[h][0], NT) for h in heads]
        dp = [_dot(doh[h], kv_ref[:, vs[h]], NT) for h in heads]
        p = [_xa_softmax(raw[h]) for h in heads]
        ds = [p[h] * (dp[h] - jnp.sum(p[h] * dp[h], axis=1, keepdims=True)) * scale for h in heads]
        dqn = [_dot(ds[h], kn[h][0]) for h in heads]
        dkn = [_dot(ds[h], qn[h][0], TN) for h in heads]
        dvv = [_dot(p[h], doh[h], TN) for h in heads]
        gq_sum = jnp.zeros((1, XA_HD), F32)
        gk_sum = jnp.zeros((1, XA_HD), F32)
        for h in heads:
            dqv, gq = _head_rms_bwd(dqn[h], qv[h], qn[h][1], qg_ref[...])
            dkv, gk = _head_rms_bwd(dkn[h], kv[h], kn[h][1], kg_ref[...])
            dq_ref[:, hs[h]] = dqv.astype(dq_ref.dtype)
            dkv_ref[:, hs[h]] += dkv
            dkv_ref[:, vs[h]] += dvv[h]
            gq_sum = gq_sum + gq
            gk_sum = gk_sum + gk
        dqg_ref[...] += gq_sum
        dkg_ref[...] += gk_sum

    vec = pl.BlockSpec((1, XA_HD), lambda b, i: (0, 0))
    row = pl.BlockSpec((tq, w), lambda b, i: (b * nq + i, 0))
    mem = pl.BlockSpec((mlen, 2 * w), lambda b, i: (b, 0))
    return pl.pallas_call(
        body, name="xattn_bwd", grid=(bsz, nq),
        in_specs=[row, mem, vec, vec, row],
        out_specs=(row, mem, vec, vec),
        out_shape=(jax.ShapeDtypeStruct((t, w), _MXU_DTYPE), jax.ShapeDtypeStruct((bsz * mlen, 2 * w), F32),
                   jax.ShapeDtypeStruct((1, XA_HD), F32), jax.ShapeDtypeStruct((1, XA_HD), F32)),
        compiler_params=_params(("arbitrary", "arbitrary")),
    )(qx, kvx, qg, kg, do)


def _loss_sum(dy, d_model, *, tm=512):
    t, d = dy.shape
    tm = min(tm, t)
    steps = t // tm

    def body(dy_ref, o_ref, acc_ref):
        i = pl.program_id(0)

        @pl.when(i == 0)
        def _():
            acc_ref[...] = jnp.zeros_like(acc_ref)

        diff = dy_ref[...] * float(d_model)
        acc_ref[...] += jnp.sum(diff * diff, axis=0, keepdims=True)

        @pl.when(i == steps - 1)
        def _():
            o_ref[...] = jnp.zeros_like(o_ref) + 0.5 * jnp.sum(acc_ref[...]) / float(d_model)

    return pl.pallas_call(
        body, name="loss_sum", grid=(steps,),
        in_specs=[pl.BlockSpec((tm, d), lambda i: (i, 0))],
        out_specs=pl.BlockSpec((1, 128), lambda i: (0, 0)),
        out_shape=jax.ShapeDtypeStruct((1, 128), F32),
        scratch_shapes=[pltpu.VMEM((1, d), F32)],
        compiler_params=_params(("arbitrary",)),
    )(dy)


def _adamw_math(w, g, m, v):
    m = ADAM_B1 * m + (1.0 - ADAM_B1) * g
    v = ADAM_B2 * v + (1.0 - ADAM_B2) * (g * g)
    m_hat = m / (1.0 - ADAM_B1 ** ADAM_STEP)
    v_hat = v / (1.0 - ADAM_B2 ** ADAM_STEP)
    return -ADAM_LR * (m_hat / (jnp.sqrt(v_hat) + ADAM_EPS) + ADAM_WD * w), m, v


def _adamw_big(w, g, m, v, *, name, tr=256):
    r, c = w.shape
    tr = min(tr, r)

    def body(w_ref, g_ref, m_ref, v_ref, d_ref, mo_ref, vo_ref):
        d, mn, vn = _adamw_math(w_ref[...], g_ref[...], m_ref[...], v_ref[...])
        d_ref[...] = d
        mo_ref[...] = mn
        vo_ref[...] = vn

    spec = pl.BlockSpec((tr, c), lambda i: (i, 0))
    shp = jax.ShapeDtypeStruct((r, c), F32)
    return pl.pallas_call(
        body, name=name, grid=(r // tr,), in_specs=[spec] * 4, out_specs=(spec,) * 3, out_shape=(shp,) * 3,
        compiler_params=_params(("parallel",)),
    )(w, g, m, v)


def _adamw_small(ws, gs, ms, vs):
    n = len(ws)

    def body(*refs):
        for i in range(n):
            d, mn, vn = _adamw_math(refs[i][...], refs[n + i][...], refs[2 * n + i][...], refs[3 * n + i][...])
            refs[4 * n + i][...] = d
            refs[5 * n + i][...] = mn
            refs[6 * n + i][...] = vn

    shapes = tuple(jax.ShapeDtypeStruct(w.shape, F32) for w in ws)
    return pl.pallas_call(body, name="adamw_small", out_shape=shapes * 3)(*ws, *gs, *ms, *vs)


def _add_halves(g, recv, c_idx, *, name, tr=256):
    _, r, c = g.shape
    h = r // 2
    tr = min(tr, h)
    nt = h // tr

    def body(c_ref, g_ref, r_ref, o_ref):
        del c_ref
        o_ref[...] = g_ref[...] + r_ref[...]

    return pl.pallas_call(
        body, name=name,
        grid_spec=pltpu.PrefetchScalarGridSpec(
            num_scalar_prefetch=1, grid=(4, nt),
            in_specs=[pl.BlockSpec((None, tr, c), lambda k, i, cr: (k, cr[0] * nt + i, 0)),
                      pl.BlockSpec((None, tr, c), lambda k, i, cr: (k, i, 0))],
            out_specs=pl.BlockSpec((None, tr, c), lambda k, i, cr: (k, i, 0))),
        out_shape=jax.ShapeDtypeStruct((4, h, c), F32),
        compiler_params=_params(("parallel", "parallel")),
    )(c_idx, g, recv)


def _add_chips(p, recv, place_idx, *, name, tr=256):
    _, h, c = p.shape
    tr = min(tr, h)
    nt = h // tr

    def body(pi_ref, p_ref, r_ref, o_ref):
        del pi_ref
        o_ref[...] = ((p_ref[...] + r_ref[0]) + r_ref[1]) + r_ref[2]

    return pl.pallas_call(
        body, name=name,
        grid_spec=pltpu.PrefetchScalarGridSpec(
            num_scalar_prefetch=1, grid=(nt,),
            in_specs=[pl.BlockSpec((None, tr, c), lambda i, pi: (pi[0], i, 0)),
                      pl.BlockSpec((3, tr, c), lambda i, pi: (0, i, 0))],
            out_specs=pl.BlockSpec((tr, c), lambda i, pi: (pi[1] * nt + i, 0))),
        out_shape=jax.ShapeDtypeStruct((2 * h, c), F32),
        compiler_params=_params(("parallel",)),
    )(place_idx, p, recv)


def _place_shard(shard, place_idx, *, name, tr=256, after=()):
    r, c = shard.shape
    tr = min(tr, r)

    def body(pi_ref, s_ref, *rest):
        del pi_ref
        rest[-1][...] = s_ref[...]

    return pl.pallas_call(
        body, name=name,
        grid_spec=pltpu.PrefetchScalarGridSpec(
            num_scalar_prefetch=1, grid=(r // tr,),
            in_specs=[pl.BlockSpec((tr, c), lambda i, pi: (i, 0))] + [pl.BlockSpec(memory_space=pl.ANY)] * len(after),
            out_specs=pl.BlockSpec((None, tr, c), lambda i, pi: (pi[0], i, 0))),
        out_shape=jax.ShapeDtypeStruct((4, r, c), shard.dtype),
        compiler_params=_params(("parallel",)),
    )(place_idx, shard, *after)


def _place():
    x, y, c = lax.axis_index("x"), lax.axis_index("y"), lax.axis_index("c")
    chips = [(1 - x, y), (x, 1 - y), (1 - x, 1 - y)]
    return x, y, c, chips


ANY = pl.BlockSpec(memory_space=pl.ANY)


def _exchange_halves(grads, name):
    n = len(grads)

    def body(*refs):
        ins, outs = refs[:n], refs[n:2 * n]
        send_sems, recv_sems = refs[2 * n:]
        x, y, c, _ = _place()

        def copy(a):
            h = ins[a].shape[1] // 2
            return pltpu.make_async_remote_copy(
                src_ref=ins[a].at[:, pl.ds((1 - c) * h, h), :], dst_ref=outs[a],
                send_sem=send_sems.at[a], recv_sem=recv_sems.at[a], device_id=(x, y, 1 - c), device_id_type=MESH)

        for a in range(n):
            copy(a).start()
        for a in range(n):
            copy(a).wait_recv()
        for a in range(n):
            copy(a).wait_send()

    return pl.pallas_call(
        body, name=name,
        in_specs=[ANY] * n, out_specs=tuple([ANY] * n),
        out_shape=tuple(jax.ShapeDtypeStruct((4, g.shape[1] // 2, g.shape[2]), g.dtype) for g in grads),
        scratch_shapes=[pltpu.SemaphoreType.DMA((n,)), pltpu.SemaphoreType.DMA((n,))],
    )(*grads)


HBM = pl.BlockSpec(memory_space=pltpu.HBM)
SEM = pl.BlockSpec(memory_space=pltpu.SEMAPHORE)
EFFECT = pltpu.SideEffectType.DATAFLOW_SIDE_EFFECTING


def _in_hbm(a):
    return pltpu.with_memory_space_constraint(a, pltpu.HBM)


def _split_copy_calls(name, srcs, lands, n_copies, make_copies):
    ns, nl = len(srcs), len(lands)
    nb = ns + nl

    def start(after=()):
        n_after = len(after)

        def body(*refs):
            outs = refs[nb + n_after:]
            copies = make_copies(refs[:ns], refs[ns:nb], outs[0], outs[1])
            for cp in copies:
                cp.start()
            token = refs[-1]
            token[...] = jnp.zeros_like(token)

        bufs = [_in_hbm(a) for a in list(srcs) + list(lands)]
        out = pl.pallas_call(
            body, name=name + "_start",
            out_shape=(pltpu.SemaphoreType.DMA((n_copies,)), pltpu.SemaphoreType.DMA((n_copies,)),
                       *[pltpu.HBM(a.shape, a.dtype) for a in bufs], jax.ShapeDtypeStruct((8, 128), F32)),
            in_specs=[HBM] * nb + [pl.BlockSpec(memory_space=pl.ANY)] * n_after,
            out_specs=(SEM, SEM, *[HBM] * nb, pl.BlockSpec(memory_space=pltpu.VMEM)),
            input_output_aliases={i: 2 + i for i in range(nb)},
            compiler_params=pltpu.CompilerParams(has_side_effects=EFFECT),
        )(*bufs, *after)
        return dict(send=out[0], recv=out[1], bufs=list(out[2:2 + nb]), token=out[-1])

    def wait(state, after):
        def body(*refs):
            copies = make_copies(refs[:ns], refs[ns:nb], refs[nb], refs[nb + 1])
            for cp in copies:
                cp.wait_send()
            for cp in copies:
                cp.wait_recv()

        bufs = state["bufs"]
        out = pl.pallas_call(
            body, name=name + "_wait",
            out_shape=tuple(pltpu.HBM(a.shape, a.dtype) for a in bufs),
            in_specs=[HBM] * nb + [SEM, SEM, pl.BlockSpec(memory_space=pl.ANY)], out_specs=tuple([HBM] * nb),
            input_output_aliases={i: i for i in range(nb)},
            compiler_params=pltpu.CompilerParams(has_side_effects=EFFECT),
        )(*bufs, state["send"], state["recv"], after)
        return list(out[:ns]), list(out[ns:])

    return start, wait


def _scatter_chips_split(name, parts):
    n = len(parts)
    lands = [lax.empty((3,) + p.shape[1:], p.dtype) for p in parts]

    def make_copies(srcs, lnds, send_sems, recv_sems):
        _, _, c, chips = _place()
        return [pltpu.make_async_remote_copy(
            src_ref=srcs[a].at[2 * px + py], dst_ref=lnds[a].at[j], send_sem=send_sems.at[a * 3 + j],
            recv_sem=recv_sems.at[a * 3 + j], device_id=(px, py, c), device_id_type=MESH)
            for a in range(n) for j, (px, py) in enumerate(chips)]

    return _split_copy_calls(name, parts, lands, 3 * n, make_copies)


def _exchange_halves_split(name, grads):
    n = len(grads)
    lands = [lax.empty((4, g.shape[1] // 2, g.shape[2]), g.dtype) for g in grads]

    def make_copies(srcs, lnds, send_sems, recv_sems):
        x, y, c, _ = _place()
        out = []
        for a in range(n):
            h = srcs[a].shape[1] // 2
            out.append(pltpu.make_async_remote_copy(
                src_ref=srcs[a].at[:, pl.ds((1 - c) * h, h), :], dst_ref=lnds[a], send_sem=send_sems.at[a],
                recv_sem=recv_sems.at[a], device_id=(x, y, 1 - c), device_id_type=MESH))
        return out

    return _split_copy_calls(name, grads, lands, n, make_copies)


def _gather_chips_split(name, shards, lands):
    n = len(shards)

    def make_copies(srcs, lnds, send_sems, recv_sems):
        x, y, c, chips = _place()
        out = []
        for a in range(n):
            h = srcs[a].shape[0] // 2
            for j, (px, py) in enumerate(chips):
                out.append(pltpu.make_async_remote_copy(
                    src_ref=srcs[a].at[pl.ds(c * h, h), :], dst_ref=lnds[a].at[2 * x + y, pl.ds(c * h, h), :],
                    send_sem=send_sems.at[a * 3 + j], recv_sem=recv_sems.at[a * 3 + j],
                    device_id=(px, py, c), device_id_type=MESH))
        return out

    return _split_copy_calls(name, shards, lands, 3 * n, make_copies)


def _gather_finish(gathered, name):
    n = len(gathered)

    def body(*refs):
        outs = refs[n:2 * n]
        send_sems, recv_sems = refs[2 * n:]
        x, y, c, chips = _place()

        def copy(a, j, chip_idx, which):
            h = outs[a].shape[1] // 2
            rows = outs[a].at[chip_idx, pl.ds(which * h, h), :]
            return pltpu.make_async_remote_copy(
                src_ref=rows, dst_ref=rows, send_sem=send_sems.at[a * 3 + j], recv_sem=recv_sems.at[a * 3 + j],
                device_id=(x, y, 1 - c), device_id_type=MESH)

        for a in range(n):
            for j, (px, py) in enumerate(chips):
                copy(a, j, 2 * px + py, c).start()
        for a in range(n):
            for j, (px, py) in enumerate(chips):
                copy(a, j, 2 * px + py, 1 - c).wait_recv()
        for a in range(n):
            for j, (px, py) in enumerate(chips):
                copy(a, j, 2 * px + py, c).wait_send()

    return pl.pallas_call(
        body, name=name,
        in_specs=[ANY] * n, out_specs=tuple([ANY] * n),
        out_shape=tuple(jax.ShapeDtypeStruct(g.shape, g.dtype) for g in gathered),
        input_output_aliases={i: i for i in range(n)},
        scratch_shapes=[pltpu.SemaphoreType.DMA((3 * n,)), pltpu.SemaphoreType.DMA((3 * n,))],
    )(*gathered)


def _join_halves(fulls):
    n = len(fulls)

    def body(*refs):
        outs = refs[n:2 * n]
        send_sems, recv_sems = refs[2 * n:]
        x, y, c, _ = _place()

        def copy(a, which):
            h = outs[a].shape[0] // 2
            rows = outs[a].at[pl.ds(which * h, h), :]
            return pltpu.make_async_remote_copy(
                src_ref=rows, dst_ref=rows, send_sem=send_sems.at[a], recv_sem=recv_sems.at[a],
                device_id=(x, y, 1 - c), device_id_type=MESH)

        for a in range(n):
            copy(a, c).start()
        for a in range(n):
            copy(a, 1 - c).wait_recv()
        for a in range(n):
            copy(a, c).wait_send()

    return pl.pallas_call(
        body, name="rs_join_halves",
        in_specs=[ANY] * n, out_specs=tuple([ANY] * n),
        out_shape=tuple(jax.ShapeDtypeStruct(p.shape, p.dtype) for p in fulls),
        input_output_aliases={i: i for i in range(n)},
        scratch_shapes=[pltpu.SemaphoreType.DMA((n,)), pltpu.SemaphoreType.DMA((n,))],
    )(*fulls)


def _all_reduce_small(sm):
    r, w = sm.shape

    def body(sm_ref, o_ref, buf, send_sems, recv_sems):
        x, y, c, _ = _place()
        me = 4 * x + 2 * y + c
        buf[me] = sm_ref[...]
        rel = [(dx, dy, dc) for dx in (0, 1) for dy in (0, 1) for dc in (0, 1)][1:]

        def copy(k, slot, to):
            return pltpu.make_async_remote_copy(
                src_ref=sm_ref, dst_ref=buf.at[slot], send_sem=send_sems.at[k], recv_sem=recv_sems.at[k],
                device_id=to, device_id_type=MESH)

        peers = []
        for k, (dx, dy, dc) in enumerate(rel):
            px = 1 - x if dx else x
            py = 1 - y if dy else y
            pc = 1 - c if dc else c
            peers.append((px, py, pc))
            copy(k, me, (px, py, pc)).start()
        for k, (px, py, pc) in enumerate(peers):
            copy(k, 4 * px + 2 * py + pc, (px, py, pc)).wait_recv()
        for k, (px, py, pc) in enumerate(peers):
            copy(k, me, (px, py, pc)).wait_send()
        acc = buf[0]
        for d in range(1, 8):
            acc = acc + buf[d]
        o_ref[...] = acc

    vm = pl.BlockSpec(memory_space=pltpu.VMEM)
    return pl.pallas_call(
        body, name="all_reduce_small", in_specs=[vm], out_specs=vm,
        out_shape=jax.ShapeDtypeStruct((r, w), F32),
        scratch_shapes=[pltpu.VMEM((8, r, w), F32), pltpu.SemaphoreType.DMA((7,)), pltpu.SemaphoreType.DMA((7,))],
    )(sm)


class _LocalWeights:
    def __init__(self, w):
        self.w = w
        self.g = {}

    def begin(self):
        pass

    def first(self, after):
        del after
        return self.w

    def rest(self, after):
        del after
        return self.w

    def grads(self, tag, g):
        del tag
        self.g.update(g)
        return None

    def poll(self, after):
        del after
        return None


def _local_step(x3, mem3, pos2, target3, small, comm):
    bsz, seq, d = x3.shape
    mlen = mem3.shape[1]
    t = bsz * seq
    comm.begin()
    x = x3.reshape(t, d)
    mem = mem3.reshape(bsz * mlen, d)
    target = target3.reshape(t, d)
    pos = pos2.reshape(t, 1)
    qg_t = jnp.tile(small["sw_q_norm_g"], (1, SW_HEADS))
    kg_t = jnp.tile(small["sw_k_norm_g"], (1, SW_KV_HEADS))

    hn1 = _rms_fwd(x, small["norm1_g"], name="rms1_fwd")
    w = comm.first(hn1)
    proj_hg = _mm(hn1, w["w_in_hg"], NN, t, HG_COLS, d, name="proj_hg", tk=d, after=(w.get("token"),))[0]
    proj_sw = _mm(hn1, w["w_in_sw"], NN, t, SW_COLS, d, name="proj_sw", tk=d)[0]
    y_mix, o_hg, states = _hg_fwd(proj_hg, small["hg_lower_bounds"], small["hg_norm_g"], bsz, seq, y_width=1024)
    y_mix = _sw_fwd(proj_sw, pos, qg_t, kg_t, small["sw_sinks"], y_mix, bsz, seq)
    w_in_hg, w_in_sw = w["w_in_hg"], w["w_in_sw"]
    w = comm.rest(y_mix)
    ff = w["down"].shape[0]
    ffs = ff // 4
    h1 = _mm(y_mix, w["w_out"], NN, t, d, 1024, name="out_proj", tk=1024, extras=(x,),
             epilogue=lambda acc, res: (acc + res,))[0]
    hn2 = _rms_fwd(h1, small["norm2_g"], name="rms2_fwd")
    mn = _rms_fwd(mem, small["mem_norm_g"], name="rms_mem_fwd")
    qx = _mm(hn2, w["wq"], NN, t, 512, d, name="xa_q", tk=d)[0]
    kvx = _mm(mn, w["wkv"], NN, bsz * mlen, 1024, d, name="xa_kv", tk=d)[0]
    ox = _xa_fwd(qx, kvx, small["xa_q_norm_g"], small["xa_k_norm_g"], bsz, seq, mlen)
    h2 = _mm(ox, w["wo"], NN, t, d, 512, name="xa_o", tk=512, extras=(h1,), epilogue=lambda acc, res: (acc + res,))[0]
    hn3 = _rms_fwd(h2, small["norm3_g"], name="rms3_fwd")

    def relu_sq(acc):
        a = jnp.maximum(acc, 0.0)
        return a, a * a

    act, act2 = _mm(hn3, w["up"], NN, t, ff, d, name="mlp_up", tm=2048, tn=ffs, tk=d,
                    b_spec=pl.BlockSpec((None, d, ffs), lambda i, j, kk: (j, 0, 0)),
                    epilogue=relu_sq, out_dtypes=(_MXU_DTYPE, _MXU_DTYPE))
    inv_d = 1.0 / d

    def loss_cotangent(acc, res, tgt):
        v = (acc + res - tgt) * inv_d
        return v, v

    dy, dy_mx = _mm(act2, w["down"], NN, t, d, ff, name="mlp_down", extras=(h2, target), epilogue=loss_cotangent,
                    out_dtypes=(F32, _MXU_DTYPE))
    loss_row = _loss_sum(dy, d)

    dz = _mm(dy_mx, w["down"], NT, t, ff, d, name="d_act", tm=2048, tk=d, extras=(act,),
             epilogue=lambda acc, a: (acc * (2.0 * a.astype(F32)),), out_dtypes=(_MXU_DTYPE,))[0]
    g_down = _mm(act2, dy_mx, TN, ff, d, t, name="g_down")[0]
    g_up = _mm(hn3, dz, TN, d, ff, t, name="g_up", tn=ffs,
               out_shape=(jax.ShapeDtypeStruct((4, d, ffs), F32),),
               out_spec=(pl.BlockSpec((None, min(1024, d), ffs), lambda i, j, kk: (j, i, 0)),))[0]
    tok = comm.grads("mlp", dict(up=g_up, down=g_down))
    dhn3 = _mm(dz, w["up"], NT, t, d, ff, name="d_hn3", tm=2048, tk=ffs, after=(tok,),
               b_spec=pl.BlockSpec((None, min(1024, d), ffs), lambda i, j, kk: (kk, j, 0)))[0]
    dh2, dh2_mx, g_norm3 = _rms_bwd(h2, small["norm3_g"], dhn3, dy, name="rms3_bwd", operand_copy=True)
    d_ox = _mm(dh2_mx, w["wo"], NT, t, 512, d, name="d_ox", tk=d)[0]
    g_wo = _mm(ox, dh2_mx, TN, 512, d, t, name="g_wo")[0]
    d_qx, d_kvx, g_xq, g_xk = _xa_bwd(qx, kvx, small["xa_q_norm_g"], small["xa_k_norm_g"], d_ox, bsz, seq, mlen)
    g_wq = _mm(hn2, d_qx, TN, d, 512, t, name="g_wq")[0]
    g_wkv = _mm(mn, d_kvx, TN, d, 1024, bsz * mlen, name="g_wkv")[0]
    dhn2 = _mm(d_qx, w["wq"], NT, t, d, 512, name="d_hn2", tk=512)[0]
    dmn = _mm(d_kvx, w["wkv"], NT, bsz * mlen, d, 1024, name="d_mn", tk=1024)[0]
    dh1, dh1_mx, g_norm2 = _rms_bwd(h1, small["norm2_g"], dhn2, dh2, name="rms2_bwd", operand_copy=True)
    _, g_memn = _rms_bwd(mem, small["mem_norm_g"], dmn, None, name="rms_mem_bwd")
    g_wout = _mm(y_mix, dh1_mx, TN, 1024, d, t, name="g_wout")[0]
    tok = comm.grads("mid", dict(w_out=g_wout, wq=g_wq, wkv=g_wkv, wo=g_wo))
    d_mix = _mm(dh1_mx, w["w_out"], NT, t, 1024, d, name="d_mix", tk=d, after=(tok,))[0]
    dproj_sw, g_swq, g_swk, g_sinks = _sw_bwd(proj_sw, pos, qg_t, kg_t, small["sw_sinks"], y_mix, d_mix, bsz, seq)
    tok = comm.poll(dproj_sw)
    dproj_hg, g_lb, g_hgn = _hg_bwd(proj_hg, small["hg_lower_bounds"], small["hg_norm_g"], o_hg, states, d_mix, bsz, seq,
                                    after=(tok,))
    g_in_hg = _mm(hn1, dproj_hg, TN, d, HG_COLS, t, name="g_in_hg")[0]
    g_in_sw = _mm(hn1, dproj_sw, TN, d, SW_COLS, t, name="g_in_sw")[0]
    tok = comm.grads("in", dict(w_in_hg=g_in_hg, w_in_sw=g_in_sw))
    dhn1_a = _mm(dproj_hg, w_in_hg, NT, t, d, HG_COLS, name="d_hn1_hg", tk=1024, after=(tok,))[0]
    dhn1 = _mm(dproj_sw, w_in_sw, NT, t, d, SW_COLS, name="d_hn1_sw", tk=SW_COLS, extras=(dhn1_a,),
               epilogue=lambda acc, prev: (acc + prev,))[0]
    grad_x, g_norm1 = _rms_bwd(x, small["norm1_g"], dhn1, dh1, name="rms1_bwd")

    g_small = dict(norm1_g=g_norm1, hg_lower_bounds=g_lb, hg_norm_g=g_hgn, sw_q_norm_g=g_swq, sw_k_norm_g=g_swk,
                   sw_sinks=g_sinks[:, 0:SW_HEADS], norm2_g=g_norm2, mem_norm_g=g_memn, xa_q_norm_g=g_xq,
                   xa_k_norm_g=g_xk, norm3_g=g_norm3)
    return loss_row, grad_x.reshape(bsz, seq, d), g_small


SMALL_NAMES = ("norm1_g", "hg_lower_bounds", "hg_norm_g", "sw_q_norm_g", "sw_k_norm_g", "sw_sinks", "norm2_g",
               "mem_norm_g", "xa_q_norm_g", "xa_k_norm_g", "norm3_g")
BIG_NAMES = ("w_in", "w_out", "xa_wq", "xa_wkv", "xa_wo", "mlp_up", "mlp_down")
WEIGHT_ORDER = ("norm1_g", "w_in", "hg_lower_bounds", "hg_norm_g", "sw_q_norm_g", "sw_k_norm_g", "sw_sinks", "w_out",
                "norm2_g", "mem_norm_g", "xa_wq", "xa_wkv", "xa_q_norm_g", "xa_k_norm_g", "xa_wo", "norm3_g",
                "mlp_up", "mlp_down")


def _pack_rows(vals, width):
    starts, at = [], 0
    for v in vals:
        starts.append(at)
        at += v.shape[0]
    total = at + (-at) % 8
    out = None
    for v, s in zip(vals, starts):
        placed = jnp.pad(v, ((s, total - s - v.shape[0]), (0, width - v.shape[1])))
        out = placed if out is None else out + placed
    return out, starts


class _MeshWeights:
    LATE = ("w_out", "xa_wq", "xa_wkv", "xa_wo", "mlp_up", "mlp_down")

    def __init__(self, shards, d, ff):
        self.shards, self.d, self.ff = shards, d, ff
        self.c_idx = lax.axis_index("c").astype(jnp.int32).reshape(1)
        chip = (2 * lax.axis_index("x") + lax.axis_index("y")).astype(jnp.int32)
        self.place_idx = jnp.stack([chip, lax.axis_index("c").astype(jnp.int32)])
        self.pending = []
        self.exchanging = None
        self.halves = {}

    def begin(self):
        shard = self.shards["w_in"]
        start, self.in_wait = _gather_chips_split(
            "gather_in", [shard], [_place_shard(shard, self.place_idx, name="place_w_in")])
        self.in_state = start()
        tok = (self.in_state["token"],)
        self.placed = [_place_shard(self.shards[n], self.place_idx, name="place_" + n, after=tok) for n in self.LATE]

    def first(self, after):
        _, lands = self.in_wait(self.in_state, after)
        (g_in,) = _gather_finish(lands, "gather_in_finish")
        start, self.late_wait = _gather_chips_split("gather_late", [self.shards[n] for n in self.LATE], self.placed)
        self.late_state = start(after=(g_in,))
        full = jnp.concatenate([g_in[k] for k in range(4)], axis=1)
        return dict(w_in_hg=full[:, :HG_COLS], w_in_sw=full[:, HG_COLS:], token=self.late_state["token"])

    def rest(self, after):
        _, lands = self.late_wait(self.late_state, after)
        g_out, g_q, g_kv, g_o, g_up, g_dn = _gather_finish(lands, "gather_late_finish")
        d = self.d
        return dict(w_out=g_out.reshape(-1, d), wq=g_q.reshape(d, -1), wkv=g_kv.reshape(d, -1),
                    wo=jnp.concatenate([g_o[k] for k in range(4)], axis=1), up=g_up, down=g_dn.reshape(self.ff, d))

    def _scatter(self, tag, names, arrays, recv):
        parts = [_add_halves(g, r, self.c_idx, name="rs_add_halves_" + n) for n, g, r in zip(names, arrays, recv)]
        start, wait = _scatter_chips_split("rs_scatter_" + tag, parts)
        state = start()
        self.pending.append((names, wait, state))
        return state["token"]

    def _advance(self, after):
        if self.exchanging is not None:
            tag, names, wait, state = self.exchanging
            self.exchanging = None
            arrays, recv = wait(state, after)
            self._scatter(tag, names, arrays, recv)

    def poll(self, after):
        self._advance(after)
        return self.pending[-1][2]["token"]

    def grads(self, tag, g):
        d, ff = self.d, self.ff
        if tag == "mlp":
            names, arrays = ("mlp_up", "mlp_down"), [g["up"], g["down"].reshape(4, ff // 4, d)]
        elif tag == "mid":
            names = ("w_out", "xa_wq", "xa_wkv", "xa_wo")
            ds = d // 4
            g_wo = jnp.stack([g["wo"][:, ds * k:ds * (k + 1)] for k in range(4)])
            arrays = [g["w_out"].reshape(4, -1, d), g["wq"].reshape(4, d // 4, -1), g["wkv"].reshape(4, d // 4, -1), g_wo]
        else:
            full = jnp.concatenate([g["w_in_hg"], g["w_in_sw"]], axis=1)
            ws = full.shape[1] // 4
            names, arrays = ("w_in",), [jnp.stack([full[:, ws * k:ws * (k + 1)] for k in range(4)])]
        self._advance(arrays[0])
        if tag == "in":
            return self._scatter(tag, names, arrays, _exchange_halves(arrays, "rs_exchange_" + tag))
        start, wait = _exchange_halves_split("rs_exchange_" + tag, arrays)
        state = start()
        self.exchanging = (tag, names, wait, state)
        return state["token"]

    def finish(self, after):
        for names, wait, state in self.pending:
            srcs, lands = wait(state, after)
            for n, p, r in zip(names, srcs, lands):
                self.halves[n] = _add_chips(p, r, self.place_idx, name="rs_add_chips_" + n)
        return dict(zip(BIG_NAMES, _join_halves([self.halves[n] for n in BIG_NAMES])))


def kernel(x, mem, positions, norm1_g, w_in, hg_lower_bounds, hg_norm_g, sw_q_norm_g, sw_k_norm_g, sw_sinks, w_out, norm2_g, mem_norm_g, xa_wq, xa_wkv, xa_q_norm_g, xa_k_norm_g, xa_wo, norm3_g, mlp_up, mlp_down, loss_target, m_norm1_g, m_w_in, m_hg_lower_bounds, m_hg_norm_g, m_sw_q_norm_g, m_sw_k_norm_g, m_sw_sinks, m_w_out, m_norm2_g, m_mem_norm_g, m_xa_wq, m_xa_wkv, m_xa_q_norm_g, m_xa_k_norm_g, m_xa_wo, m_norm3_g, m_mlp_up, m_mlp_down, v_norm1_g, v_w_in, v_hg_lower_bounds, v_hg_norm_g, v_sw_q_norm_g, v_sw_k_norm_g, v_sw_sinks, v_w_out, v_norm2_g, v_mem_norm_g, v_xa_wq, v_xa_wkv, v_xa_q_norm_g, v_xa_k_norm_g, v_xa_wo, v_norm3_g, v_mlp_up, v_mlp_down):
    given = dict(locals())
    weights = {n: given[n] for n in WEIGHT_ORDER}
    moms = {n: given["m_" + n] for n in WEIGHT_ORDER}
    vars_ = {n: given["v_" + n] for n in WEIGHT_ORDER}
    d = x.shape[-1]
    ff = mlp_down.shape[1] * 4
    small = {n: weights[n] for n in SMALL_NAMES}

    comm = _MeshWeights({n: weights[n][0].astype(_MXU_DTYPE) for n in BIG_NAMES}, d, ff)
    loss_row, grad_x, g_small = _local_step(x, mem, positions, loss_target, small, comm)
    big_grads = comm.finish(grad_x)

    packed, starts = _pack_rows([g_small[n] for n in SMALL_NAMES] + [loss_row], 1024)
    summed = _all_reduce_small(packed)
    small_grads = {}
    for n, s in zip(SMALL_NAMES, starts):
        r, c = weights[n].shape
        small_grads[n] = summed[s:s + r, 0:c]
    loss = summed[starts[-1], 0]

    grads, deltas, new_m, new_v = {}, {}, {}, {}
    for n in BIG_NAMES:
        shp = weights[n].shape
        g2 = big_grads[n]
        dl, mo, vo = _adamw_big(weights[n][0], g2, moms[n][0], vars_[n][0], name="adamw_" + n)
        grads[n], deltas[n], new_m[n], new_v[n] = (a.reshape(shp) for a in (g2, dl, mo, vo))
    sm_out = _adamw_small([weights[n] for n in SMALL_NAMES], [small_grads[n] for n in SMALL_NAMES],
                          [moms[n] for n in SMALL_NAMES], [vars_[n] for n in SMALL_NAMES])
    ns = len(SMALL_NAMES)
    for i, n in enumerate(SMALL_NAMES):
        grads[n], deltas[n], new_m[n], new_v[n] = small_grads[n], sm_out[i], sm_out[ns + i], sm_out[2 * ns + i]

    return (loss, grad_x, *[grads[n] for n in WEIGHT_ORDER], *[deltas[n] for n in WEIGHT_ORDER],
            *[new_m[n] for n in WEIGHT_ORDER], *[new_v[n] for n in WEIGHT_ORDER])
```

```python
import numpy as np
import jax
import jax.numpy as jnp
from jax import lax
from jax.experimental import pallas as pl
from jax.experimental.pallas import tpu as pltpu

F32 = jnp.float32
_MXU_DTYPE = jnp.bfloat16

EPS = 1e-6
HG_HEADS = 4
HG_D = 128
HG_CHUNK = 64
HG_TILE = 512
HG_LEVELS = (32, 16, 8, 4, 2, 1)
SW_HEADS = 8
SW_KV_HEADS = 2
SW_GROUP = SW_HEADS // SW_KV_HEADS
SW_HD = 64
SW_BLOCK = 128
ROPE_THETA = 500000.0
ROT_DIM = SW_HD // 4
XA_HEADS = 4
XA_HD = 128
HG_COLS = 4 * HG_HEADS * HG_D
SW_COLS = (SW_HEADS + 2 * SW_KV_HEADS) * SW_HD

ADAM_LR = 0.001
ADAM_B1 = 0.9
ADAM_B2 = 0.999
ADAM_EPS = 1e-08
ADAM_WD = 0.01
ADAM_STEP = 10

VMEM_LIMIT = 56 * 1024 * 1024
MESH = pl.DeviceIdType.MESH

NN = ((1,), (0,))
NT = ((1,), (1,))
TN = ((0,), (0,))


def _mx(v):
    return v.astype(_MXU_DTYPE)


def _dot(a, b, dims=NN):
    return lax.dot_general(_mx(a), _mx(b), (dims, ((), ())), preferred_element_type=F32)


def _split_dot(a, v, dims, parts):
    acc = None
    rest = v
    for p in range(parts):
        piece = _mx(rest)
        term = lax.dot_general(a, piece, (dims, ((), ())), preferred_element_type=F32)
        acc = term if acc is None else acc + term
        if p + 1 < parts:
            rest = rest - piece.astype(F32)
    return acc


def _params(sem):
    return pltpu.CompilerParams(dimension_semantics=sem, vmem_limit_bytes=VMEM_LIMIT)


def _mm(a, b, mode, m, n, k, *, name, tm=1024, tn=1024, tk=1024, a_spec=None, b_spec=None, extras=(), rows=(),
        epilogue=None, out_dtypes=(F32,), row_sums=0, out_shape=None, out_spec=None, after=()):
    after = tuple(t for t in after if t is not None)
    tm, tn, tk = min(tm, m), min(tn, n), min(tk, k)
    assert m % tm == 0 and n % tn == 0 and k % tk == 0, (name, m, n, k, tm, tn, tk)
    gi, gj, gk = m // tm, n // tn, k // tk
    assert row_sums == 0 or gj == 1, name
    if a_spec is None:
        a_spec = (pl.BlockSpec((tk, tm), lambda i, j, kk: (kk, i)) if mode == TN
                  else pl.BlockSpec((tm, tk), lambda i, j, kk: (i, kk)))
    if b_spec is None:
        b_spec = (pl.BlockSpec((tn, tk), lambda i, j, kk: (j, kk)) if mode == NT
                  else pl.BlockSpec((tk, tn), lambda i, j, kk: (kk, j)))
    mn_spec = pl.BlockSpec((tm, tn), lambda i, j, kk: (i, j))
    if epilogue is None:
        epilogue = lambda acc: (acc,)
    row_spec = pl.BlockSpec((1, tn), lambda i, j, kk: (0, j))
    n_ex, n_out = len(extras) + len(rows), len(out_dtypes)
    if out_shape is None:
        out_shape = tuple(jax.ShapeDtypeStruct((m, n), d) for d in out_dtypes)
        out_spec = tuple(mn_spec for _ in out_dtypes)
    out_shape = tuple(out_shape) + tuple(jax.ShapeDtypeStruct((1, n), F32) for _ in range(row_sums))
    out_spec = tuple(out_spec) + tuple(row_spec for _ in range(row_sums))

    n_after = len(after)

    def body(*refs):
        a_ref, b_ref = refs[0], refs[1]
        ex = refs[2:2 + n_ex]
        outs = refs[2 + n_ex + n_after:2 + n_ex + n_after + n_out + row_sums]
        first_row_tile = pl.program_id(0) == 0

        def finish(acc):
            res = epilogue(acc, *[e[...] for e in ex])
            for o, r in zip(outs[:n_out], res[:n_out]):
                o[...] = r.astype(o.dtype)
            if row_sums:
                @pl.when(first_row_tile)
                def _():
                    for o in outs[n_out:]:
                        o[...] = jnp.zeros_like(o)

                for o, r in zip(outs[n_out:], res[n_out:]):
                    o[...] += r

        if gk == 1:
            finish(_dot(a_ref[...], b_ref[...], mode))
        else:
            acc_ref = refs[-1]
            kk = pl.program_id(2)

            @pl.when(kk == 0)
            def _():
                acc_ref[...] = jnp.zeros_like(acc_ref)

            acc_ref[...] += _dot(a_ref[...], b_ref[...], mode)

            @pl.when(kk == gk - 1)
            def _():
                finish(acc_ref[...])

    return pl.pallas_call(
        body, name=name, grid=(gi, gj, gk),
        in_specs=([a_spec, b_spec] + [mn_spec] * len(extras) + [row_spec] * len(rows)
                  + [pl.BlockSpec(memory_space=pl.ANY)] * n_after),
        out_specs=out_spec, out_shape=out_shape,
        scratch_shapes=[pltpu.VMEM((tm, tn), F32)] if gk > 1 else [],
        compiler_params=_params(("arbitrary" if row_sums else "parallel", "parallel", "arbitrary")),
    )(a, b, *extras, *rows, *after)


def _rms_rows(xv, g):
    return xv * lax.rsqrt(jnp.mean(xv * xv, axis=1, keepdims=True) + EPS) * g


def _rms_rows_bwd(xv, g, dyv):
    r = lax.rsqrt(jnp.mean(xv * xv, axis=1, keepdims=True) + EPS)
    u = dyv * g
    return (r * u - xv * (r * r * r) * jnp.mean(u * xv, axis=1, keepdims=True),
            jnp.sum(dyv * xv * r, axis=0, keepdims=True))


def _residual_rms(acc, res, g):
    h = acc + res
    return h, _rms_rows(h, g)


def _rms_bwd_residual(dhn, xv, dres, g):
    dx, dg = _rms_rows_bwd(xv, g, dhn)
    dx = dx + dres
    return dx, dx, dg


def _rms_fwd(x, g, *, name, tm=512):
    t, d = x.shape
    tm = min(tm, t)

    def body(x_ref, g_ref, o_ref):
        o_ref[...] = _rms_rows(x_ref[...], g_ref[...]).astype(o_ref.dtype)

    return pl.pallas_call(
        body, name=name, grid=(t // tm,),
        in_specs=[pl.BlockSpec((tm, d), lambda i: (i, 0)), pl.BlockSpec((1, d), lambda i: (0, 0))],
        out_specs=pl.BlockSpec((tm, d), lambda i: (i, 0)),
        out_shape=jax.ShapeDtypeStruct((t, d), _MXU_DTYPE),
        compiler_params=_params(("parallel",)),
    )(x, g)


def _rms_gain_grad(x, g, dy, *, name, tm=512):
    t, d = x.shape
    tm = min(tm, t)

    def body(x_ref, g_ref, dy_ref, dg_ref):
        @pl.when(pl.program_id(0) == 0)
        def _():
            dg_ref[...] = jnp.zeros_like(dg_ref)

        dg_ref[...] += _rms_rows_bwd(x_ref[...], g_ref[...], dy_ref[...])[1]

    row = pl.BlockSpec((tm, d), lambda i: (i, 0))
    vec = pl.BlockSpec((1, d), lambda i: (0, 0))
    return pl.pallas_call(
        body, name=name, grid=(t // tm,), in_specs=[row, vec, row], out_specs=vec,
        out_shape=jax.ShapeDtypeStruct((1, d), F32), compiler_params=_params(("arbitrary",)),
    )(x, g, dy)


def _hg_constants():
    c = HG_CHUNK
    t = np.arange(c)
    sums = [t[None, :] <= t[:, None]]
    masks = []
    for m in HG_LEVELS:
        base = (t // (2 * m)) * (2 * m)
        mid = base + m - 1
        second = (t - base) >= m
        upper = (t[None, :] > mid[:, None]) & (t[None, :] <= t[:, None])
        lower = (t[None, :] > t[:, None]) & (t[None, :] <= mid[:, None])
        sums.append(np.where(second[:, None], upper, lower))
        masks.append(second[:, None] & (~second)[None, :] & (base[:, None] == base[None, :]))
    return (np.concatenate(sums, axis=0).astype(np.float32), np.stack(masks).astype(np.float32))


HG_HEAD_LANES = tuple(slice(HG_D * h, HG_D * (h + 1)) for h in range(HG_HEADS))


def _per_head(fn, slab):
    return jnp.concatenate([jnp.broadcast_to(fn(slab[:, hs]), (slab.shape[0], HG_D)) for hs in HG_HEAD_LANES], axis=1)


def _lane_sum(v):
    return jnp.sum(v, axis=1, keepdims=True)


def _lane_mean(v):
    return jnp.mean(v, axis=1, keepdims=True)


def _hg_gates(blk, lbp):
    w = HG_HEADS * HG_D
    q, x, v, gl = blk[:, 0:w], blk[:, w:2 * w], blk[:, 2 * w:3 * w], blk[:, 3 * w:4 * w]
    mx = jnp.max(lbp, axis=0, keepdims=True)
    e = jnp.exp(lbp - mx)
    lb = e[0:1, :] / jnp.sum(e, axis=0, keepdims=True)
    sig = jax.nn.sigmoid(x)
    f = lb + (1.0 - lb) * sig
    return q, v, gl, lb, sig, f, 1.0 - f, jnp.log(f)


def _hg_fwd(proj, lbp, ng, bsz, seq, *, y_width):
    t = proj.shape[0]
    nc = seq // HG_CHUNK
    a_np, m_np = _hg_constants()
    a_all = jnp.asarray(a_np, _MXU_DTYPE)
    masks = jnp.asarray(m_np, F32)
    nl = len(HG_LEVELS)

    ts = min(HG_TILE, seq)
    ns, nct = seq // ts, ts // HG_CHUNK
    hw = HG_HEADS * HG_D

    def body(p_ref, lb_ref, ng_ref, a_ref, m_ref, y_ref, o_ref, st_ref, carry):
        a_mat = a_ref[...]
        ngv = ng_ref[...]

        @pl.when(pl.program_id(1) == 0)
        def _():
            carry[...] = jnp.zeros_like(carry)

        ng4 = _tile_lanes(ngv, HG_HEADS)
        heads = range(HG_HEADS)
        hl = HG_HEAD_LANES

        def chunk(c, _):
            rows = pl.ds(pl.multiple_of(c * HG_CHUNK, HG_CHUNK), HG_CHUNK)
            q, v, gl, lb, sig, f, k, g = _hg_gates(p_ref[rows, :], lb_ref[...])
            sts = [carry[h] for h in heads]
            e_all = _split_dot(a_mat, g, NN, 3)
            b = e_all[0:HG_CHUNK]
            qb = q * jnp.exp(b)
            o = [_dot(qb[:, hl[h]], sts[h], NT) for h in heads]
            p = [jnp.zeros((HG_CHUNK, HG_CHUNK), F32) for _ in heads]
            for li in range(nl):
                e = jnp.exp(e_all[HG_CHUNK * (li + 1):HG_CHUNK * (li + 2)])
                qm, km, mk = q * e, k * e, m_ref[li]
                p = [p[h] + mk * _dot(qm[:, hl[h]], km[:, hl[h]], NT) for h in heads]
            bl = b[HG_CHUNK - 1:HG_CHUNK, :]
            kd = k * jnp.exp(bl - b)
            ebl = jnp.exp(bl)
            pv = [_dot(p[h], v[:, hl[h]]) for h in heads]
            upd = [_dot(v[:, hl[h]], kd[:, hl[h]], TN) for h in heads]
            o_all = jnp.concatenate([o[h] + pv[h] for h in heads], axis=1) + _per_head(_lane_sum, q * k) * v
            r = lax.rsqrt(_per_head(_lane_mean, o_all * o_all) + EPS)
            for h in heads:
                st_ref[h, c] = sts[h]
                carry[h] = sts[h] * ebl[:, hl[h]] + upd[h]
            o_ref[rows, :] = o_all
            y_ref[rows, :] = (o_all * r * ng4) * (gl * jax.nn.sigmoid(gl))
            return 0

        lax.fori_loop(0, nct, chunk, 0)

    return pl.pallas_call(
        body, name="hgrn2_fwd", grid=(bsz, ns),
        in_specs=[pl.BlockSpec((ts, HG_COLS), lambda b, s: (b * ns + s, 0)),
                  pl.BlockSpec((2, hw), lambda b, s: (0, 0)),
                  pl.BlockSpec((1, HG_D), lambda b, s: (0, 0)),
                  pl.BlockSpec(a_all.shape, lambda b, s: (0, 0)),
                  pl.BlockSpec(masks.shape, lambda b, s: (0, 0, 0))],
        out_specs=(pl.BlockSpec((ts, hw), lambda b, s: (b * ns + s, 0)),
                   pl.BlockSpec((ts, hw), lambda b, s: (b * ns + s, 0)),
                   pl.BlockSpec((None, HG_HEADS, nct, HG_D, HG_D), lambda b, s: (b, 0, s, 0, 0))),
        out_shape=(jax.ShapeDtypeStruct((t, y_width), F32),
                   jax.ShapeDtypeStruct((t, hw), F32),
                   jax.ShapeDtypeStruct((bsz, HG_HEADS, nc, HG_D, HG_D), F32)),
        scratch_shapes=[pltpu.VMEM((HG_HEADS, HG_D, HG_D), F32)],
        compiler_params=_params(("parallel", "arbitrary")),
    )(proj, lbp, ng, a_all, masks)


def _hg_bwd(proj, lbp, ng, o_all, states, dy, bsz, seq, after=()):
    after = tuple(a for a in after if a is not None)
    t = proj.shape[0]
    nc = seq // HG_CHUNK
    a_np, m_np = _hg_constants()
    a_all = jnp.asarray(a_np, _MXU_DTYPE)
    masks = jnp.asarray(m_np, F32)
    nl = len(HG_LEVELS)
    cs = HG_CHUNK

    ts = min(HG_TILE, seq)
    ns, nct = seq // ts, ts // cs
    hw = HG_HEADS * HG_D

    def body(p_ref, lb_ref, ng_ref, a_ref, m_ref, o_ref, st_ref, dy_ref, *rest):
        dp_ref, dlb_ref, dng_ref, dst_ref = rest[len(after):]
        a_mat = a_ref[...]
        ngv = ng_ref[...]
        ng4 = _tile_lanes(ngv, HG_HEADS)
        last_row = lax.broadcasted_iota(jnp.int32, (cs, hw), 0) == cs - 1
        si = pl.program_id(1)
        first = jnp.logical_and(pl.program_id(0) == 0, si == 0)
        heads = range(HG_HEADS)
        hl = HG_HEAD_LANES

        @pl.when(si == 0)
        def _():
            dst_ref[...] = jnp.zeros_like(dst_ref)

        def side_by_side(parts):
            return jnp.concatenate(parts, axis=1)

        def chunk(i, carry):
            dlb_acc, dng_acc = carry
            c = nct - 1 - i
            rows = pl.ds(pl.multiple_of(c * cs, cs), cs)
            q, v, gl, lb, sig, f, k, g = _hg_gates(p_ref[rows, :], lb_ref[...])
            o = o_ref[rows, :]
            dyv = dy_ref[rows, :]
            sts = [st_ref[h, c] for h in heads]
            dsts = [dst_ref[h] for h in heads]
            e_all = _split_dot(a_mat, g, NN, 3)
            b = e_all[0:cs]
            eb = jnp.exp(b)
            bl = b[cs - 1:cs, :]
            ebl = jnp.exp(bl)
            ekd = jnp.exp(bl - b)
            qb, kd = q * eb, k * ekd
            sg = jax.nn.sigmoid(gl)
            silu = gl * sg
            r = lax.rsqrt(_per_head(_lane_mean, o * o) + EPS)
            dgl = dyv * (o * r * ng4) * (sg * (1.0 + gl * (1.0 - sg)))
            u = dyv * silu * ng4
            do = r * u - o * (r * r * r) * _per_head(_lane_mean, u * o)
            dng4 = jnp.sum(dyv * silu * o * r, axis=0, keepdims=True)
            dng_acc = dng_acc + ((dng4[:, hl[0]] + dng4[:, hl[1]]) + (dng4[:, hl[2]] + dng4[:, hl[3]]))
            es, qm, km = [], [], []
            p = [jnp.zeros((cs, cs), F32) for _ in heads]
            for li in range(nl):
                e = jnp.exp(e_all[cs * (li + 1):cs * (li + 2)])
                es.append(e)
                qm.append(q * e)
                km.append(k * e)
                mk = m_ref[li]
                p = [p[h] + mk * _dot(qm[li][:, hl[h]], km[li][:, hl[h]], NT) for h in heads]
            dp = [_dot(do[:, hl[h]], v[:, hl[h]], NT) for h in heads]
            dv_p = [_dot(p[h], do[:, hl[h]], TN) for h in heads]
            dv_s = [_dot(kd[:, hl[h]], dsts[h], NT) for h in heads]
            dqb = side_by_side([_dot(do[:, hl[h]], sts[h]) for h in heads])
            dkd = side_by_side([_dot(v[:, hl[h]], dsts[h]) for h in heads])
            new_dst = [_dot(do[:, hl[h]], qb[:, hl[h]], TN) for h in heads]
            dv = side_by_side([dv_p[h] + dv_s[h] for h in heads]) + _per_head(_lane_sum, q * k) * do
            dq = dqb * eb
            dk = dkd * ekd
            db = dqb * qb - dkd * kd
            dbl = (jnp.sum(dkd * kd, axis=0, keepdims=True)
                   + side_by_side([jnp.sum(dsts[h] * sts[h], axis=0, keepdims=True) for h in heads]) * ebl)
            de = [db + jnp.where(last_row, dbl, 0.0)]
            for li in range(nl):
                mk = m_ref[li]
                dpm = [mk * dp[h] for h in heads]
                dqm = side_by_side([_dot(dpm[h], km[li][:, hl[h]]) for h in heads])
                dkm = side_by_side([_dot(dpm[h], qm[li][:, hl[h]], TN) for h in heads])
                dq = dq + dqm * es[li]
                dk = dk + dkm * es[li]
                de.append(dqm * qm[li] + dkm * km[li])
            dpd = _per_head(_lane_sum, do * v)
            dq = dq + dpd * k
            dk = dk + dpd * q
            dg = _split_dot(a_mat, jnp.concatenate(de, axis=0), TN, 2)
            df = dg / f - dk
            dp_ref[rows, 0:hw] = _mx(dq)
            dp_ref[rows, hw:2 * hw] = _mx(df * (1.0 - lb) * sig * (1.0 - sig))
            dp_ref[rows, 2 * hw:3 * hw] = _mx(dv)
            dp_ref[rows, 3 * hw:4 * hw] = _mx(dgl)
            for h in heads:
                dst_ref[h] = dsts[h] * ebl[:, hl[h]] + new_dst[h]
            return dlb_acc + jnp.sum(df * (1.0 - sig), axis=0, keepdims=True), dng_acc

        dlb, dng = lax.fori_loop(0, nct, chunk, (jnp.zeros((1, hw), F32), jnp.zeros((1, HG_D), F32)))

        @pl.when(first)
        def _():
            dlb_ref[...] = jnp.zeros_like(dlb_ref)
            dng_ref[...] = jnp.zeros_like(dng_ref)

        lbp_v = lb_ref[...]
        mx = jnp.max(lbp_v, axis=0, keepdims=True)
        e = jnp.exp(lbp_v - mx)
        s0 = e[0:1, :] / jnp.sum(e, axis=0, keepdims=True)
        da0 = dlb * s0 * (1.0 - s0)
        dlb_ref[...] += jnp.concatenate([da0, -da0], axis=0)
        dng_ref[...] += dng

    def tile(b, s):
        return b * ns + (ns - 1 - s)

    return pl.pallas_call(
        body, name="hgrn2_bwd", grid=(bsz, ns),
        in_specs=[pl.BlockSpec((ts, HG_COLS), lambda b, s: (tile(b, s), 0)),
                  pl.BlockSpec((2, hw), lambda b, s: (0, 0)),
                  pl.BlockSpec((1, HG_D), lambda b, s: (0, 0)),
                  pl.BlockSpec(a_all.shape, lambda b, s: (0, 0)),
                  pl.BlockSpec(masks.shape, lambda b, s: (0, 0, 0)),
                  pl.BlockSpec((ts, hw), lambda b, s: (tile(b, s), 0)),
                  pl.BlockSpec((None, HG_HEADS, nct, HG_D, HG_D), lambda b, s: (b, 0, ns - 1 - s, 0, 0)),
                  pl.BlockSpec((ts, hw), lambda b, s: (tile(b, s), 0))] + [pl.BlockSpec(memory_space=pl.ANY)] * len(after),
        out_specs=(pl.BlockSpec((ts, HG_COLS), lambda b, s: (tile(b, s), 0)),
                   pl.BlockSpec((2, hw), lambda b, s: (0, 0)),
                   pl.BlockSpec((1, HG_D), lambda b, s: (0, 0))),
        out_shape=(jax.ShapeDtypeStruct((t, HG_COLS), _MXU_DTYPE),
                   jax.ShapeDtypeStruct((2, hw), F32),
                   jax.ShapeDtypeStruct((1, HG_D), F32)),
        scratch_shapes=[pltpu.VMEM((HG_HEADS, HG_D, HG_D), F32)],
        compiler_params=_params(("arbitrary", "arbitrary")),
    )(proj, lbp, ng, a_all, masks, o_all, states, dy, *after)


def _sw_constants():
    half = ROT_DIM // 2
    inv = (np.float32(ROPE_THETA) ** (-(np.arange(half, dtype=np.float32) * np.float32(2.0) / np.float32(ROT_DIM)))
           ).astype(np.float32)
    freq = np.zeros((1, 128), np.float32)
    sign = np.zeros((1, 128), np.float32)
    for h in range(2):
        freq[0, 64 * h:64 * h + half] = inv
        freq[0, 64 * h + half:64 * h + 2 * half] = inv
        sign[0, 64 * h:64 * h + half] = -1.0
        sign[0, 64 * h + half:64 * h + 2 * half] = 1.0
    seg = np.kron(np.eye(8, dtype=np.float32), np.full((64, 64), 1.0 / 64.0, np.float32))
    return freq, sign, seg


def _rope_tables(pos, freq, sign):
    ang = pos.astype(F32) * freq
    return jnp.cos(ang), jnp.sin(ang) * sign


def _tile_lanes(v, times):
    return v if times == 1 else jnp.concatenate([v] * times, axis=1)


def _swap_halves(v):
    w = v.shape[1]
    half = ROT_DIM // 2
    lane = lax.broadcasted_iota(jnp.int32, v.shape, 1) % SW_HD
    return jnp.where(lane < half, pltpu.roll(v, w - half, 1), jnp.where(lane < 2 * half, pltpu.roll(v, half, 1), 0.0))


def _sw_norm_rope(tv, gain, seg, cosv, sinv):
    w = tv.shape[1]
    ms = _split_dot_rhs(tv * tv, seg[0:w, 0:w])
    r = lax.rsqrt(ms + EPS)
    tn = tv * r * gain
    reps = w // 128
    return tn * _tile_lanes(cosv, reps) + _swap_halves(tn) * _tile_lanes(sinv, reps), r


def _split_dot_rhs(v, a):
    hi = _mx(v)
    lo = _mx(v - hi.astype(F32))
    return (lax.dot_general(hi, a, (NN, ((), ())), preferred_element_type=F32)
            + lax.dot_general(lo, a, (NN, ((), ())), preferred_element_type=F32))


def _sw_norm_rope_bwd(dt, tv, r, gain, seg, cosv, sinv):
    w = tv.shape[1]
    reps = w // 128
    dtn = dt * _tile_lanes(cosv, reps) + _swap_halves(dt * _tile_lanes(sinv, reps))
    u = dtn * gain
    dtv = r * u - tv * (r * r * r) * _split_dot_rhs(u * tv, seg[0:w, 0:w])
    return dtv, jnp.sum(dtn * tv * r, axis=0, keepdims=True)


def _sw_scores(qh, kp, kc):
    return _dot(qh, kp, NT), _dot(qh, kc, NT)


def _sw_probs(raw, sink, first_block):
    scale = SW_HD ** -0.5
    qi = lax.broadcasted_iota(jnp.int32, (SW_BLOCK, SW_BLOCK), 0)
    kj = lax.broadcasted_iota(jnp.int32, (SW_BLOCK, SW_BLOCK), 1)
    ok_prev = jnp.logical_and(kj > qi, jnp.logical_not(first_block))
    ok_cur = kj <= qi
    sp = jnp.where(ok_prev, raw[0] * scale, -jnp.inf)
    sc = jnp.where(ok_cur, raw[1] * scale, -jnp.inf)
    m = jnp.maximum(jnp.maximum(jnp.max(sp, axis=1, keepdims=True), jnp.max(sc, axis=1, keepdims=True)), sink)
    pp, pc = jnp.exp(sp - m), jnp.exp(sc - m)
    es = jnp.exp(sink - m)
    den = jnp.sum(pp, axis=1, keepdims=True) + jnp.sum(pc, axis=1, keepdims=True) + es
    return pp / den, pc / den, es / den


def _sw_specs(nb):
    def cur(b, n):
        return b * nb + jnp.minimum(n, nb - 1)

    def prev(b, n):
        return b * nb + jnp.maximum(jnp.minimum(n, nb - 1) - 1, 0)

    return cur, prev


def _sw_fwd(proj, pos, qg, kg, sinks, y_in, bsz, seq):
    t = proj.shape[0]
    nb = seq // SW_BLOCK
    freq_np, sign_np, seg_np = _sw_constants()
    freq, sign = jnp.asarray(freq_np), jnp.asarray(sign_np)
    seg = jnp.asarray(seg_np, _MXU_DTYPE)
    cur, prev = _sw_specs(nb)

    def body(q_ref, kc_ref, kp_ref, vc_ref, vp_ref, pc_ref, pp_ref, qg_ref, kg_ref, sk_ref, fr_ref, sn_ref, seg_ref,
             yin_ref, y_ref):
        del yin_ref
        n = pl.program_id(1)
        segv = seg_ref[...]
        cos_c, sin_c = _rope_tables(pc_ref[...], fr_ref[...], sn_ref[...])
        cos_p, sin_p = _rope_tables(pp_ref[...], fr_ref[...], sn_ref[...])
        qr, _ = _sw_norm_rope(q_ref[...], qg_ref[...], segv, cos_c, sin_c)
        kcr, _ = _sw_norm_rope(kc_ref[...], kg_ref[...], segv, cos_c, sin_c)
        kpr, _ = _sw_norm_rope(kp_ref[...], kg_ref[...], segv, cos_p, sin_p)
        vc, vp = vc_ref[...], vp_ref[...]
        ks = [slice(SW_HD * (h // SW_GROUP), SW_HD * (h // SW_GROUP + 1)) for h in range(SW_HEADS)]
        raw = [_sw_scores(qr[:, SW_HD * h:SW_HD * (h + 1)], kpr[:, ks[h]], kcr[:, ks[h]]) for h in range(SW_HEADS)]
        probs = [_sw_probs(raw[h], sk_ref[0, h], n == 0) for h in range(SW_HEADS)]
        for h in range(SW_HEADS):
            y_ref[:, SW_HD * h:SW_HD * (h + 1)] = _dot(probs[h][0], vp[:, ks[h]]) + _dot(probs[h][1], vc[:, ks[h]])

    rowq = pl.BlockSpec((SW_BLOCK, 512), lambda b, n: (cur(b, n), 0))
    full = lambda a: pl.BlockSpec(a.shape, lambda b, n: (0,) * a.ndim)
    yw = y_in.shape[1]
    return pl.pallas_call(
        body, name="swa_fwd", grid=(bsz, nb),
        in_specs=[rowq,
                  pl.BlockSpec((SW_BLOCK, 128), lambda b, n: (cur(b, n), 4)),
                  pl.BlockSpec((SW_BLOCK, 128), lambda b, n: (prev(b, n), 4)),
                  pl.BlockSpec((SW_BLOCK, 128), lambda b, n: (cur(b, n), 5)),
                  pl.BlockSpec((SW_BLOCK, 128), lambda b, n: (prev(b, n), 5)),
                  pl.BlockSpec((SW_BLOCK, 1), lambda b, n: (cur(b, n), 0)),
                  pl.BlockSpec((SW_BLOCK, 1), lambda b, n: (prev(b, n), 0)),
                  full(qg), full(kg),
                  pl.BlockSpec(memory_space=pltpu.SMEM),
                  full(freq), full(sign), full(seg),
                  pl.BlockSpec(memory_space=pl.ANY)],
        out_specs=pl.BlockSpec((SW_BLOCK, 512), lambda b, n: (cur(b, n), 1)),
        out_shape=jax.ShapeDtypeStruct((t, yw), F32),
        input_output_aliases={13: 0},
        compiler_params=_params(("parallel", "parallel")),
    )(proj, proj, proj, proj, proj, pos, pos, qg, kg, sinks, freq, sign, seg, y_in)


def _sw_bwd(proj, pos, qg, kg, sinks, y, dy, bsz, seq):
    t = proj.shape[0]
    nb = seq // SW_BLOCK
    freq_np, sign_np, seg_np = _sw_constants()
    freq, sign = jnp.asarray(freq_np), jnp.asarray(sign_np)
    seg = jnp.asarray(seg_np, _MXU_DTYPE)
    cur, prev = _sw_specs(nb)
    scale = SW_HD ** -0.5

    def body(q_ref, kc_ref, kp_ref, vc_ref, vp_ref, pc_ref, pp_ref, qg_ref, kg_ref, sk_ref, fr_ref, sn_ref, seg_ref,
             y_ref, dy_ref, dp_ref, dqg_ref, dkg_ref, dsk_ref,
             dq_car, dkv_car, dqr_s, dkc_s, dkp_s, dvc_s, dvp_s, gq_acc, gk_acc, sk_acc):
        b, n = pl.program_id(0), pl.program_id(1)
        first = jnp.logical_and(b == 0, n == 0)
        last = jnp.logical_and(b == pl.num_programs(0) - 1, n == nb)

        @pl.when(first)
        def _():
            gq_acc[...] = jnp.zeros_like(gq_acc)
            gk_acc[...] = jnp.zeros_like(gk_acc)
            sk_acc[...] = jnp.zeros_like(sk_acc)

        @pl.when(n < nb)
        def _():
            segv = seg_ref[...]
            cos_c, sin_c = _rope_tables(pc_ref[...], fr_ref[...], sn_ref[...])
            cos_p, sin_p = _rope_tables(pp_ref[...], fr_ref[...], sn_ref[...])
            qv, kcv, kpv = q_ref[...], kc_ref[...], kp_ref[...]
            qr, rq = _sw_norm_rope(qv, qg_ref[...], segv, cos_c, sin_c)
            kcr, rkc = _sw_norm_rope(kcv, kg_ref[...], segv, cos_c, sin_c)
            kpr, rkp = _sw_norm_rope(kpv, kg_ref[...], segv, cos_p, sin_p)
            vc, vp = vc_ref[...], vp_ref[...]
            lane = lax.broadcasted_iota(jnp.int32, (1, 128), 1)
            dsk = jnp.zeros((1, 128), F32)
            heads = range(SW_HEADS)
            ks = [slice(SW_HD * (h // SW_GROUP), SW_HD * (h // SW_GROUP + 1)) for h in heads]
            hs = [slice(SW_HD * h, SW_HD * (h + 1)) for h in heads]
            qh = [qr[:, hs[h]] for h in heads]
            doh = [dy_ref[:, hs[h]] for h in heads]
            raw = [_sw_scores(qh[h], kpr[:, ks[h]], kcr[:, ks[h]]) for h in heads]
            dpp = [_dot(doh[h], vp[:, ks[h]], NT) for h in heads]
            dpc = [_dot(doh[h], vc[:, ks[h]], NT) for h in heads]
            probs = [_sw_probs(raw[h], sk_ref[0, h], n == 0) for h in heads]
            dsp, dsc = [], []
            for h in heads:
                pp, pc, ps = probs[h]
                delta = jnp.sum(doh[h] * y_ref[:, hs[h]], axis=1, keepdims=True)
                dsp.append(pp * (dpp[h] - delta) * scale)
                dsc.append(pc * (dpc[h] - delta) * scale)
                dsk = dsk + jnp.where(lane == h, -jnp.sum(ps * delta), 0.0)
            for h in heads:
                dqr_s[:, hs[h]] = _dot(dsp[h], kpr[:, ks[h]]) + _dot(dsc[h], kcr[:, ks[h]])
            for kv in range(SW_KV_HEADS):
                group = range(SW_GROUP * kv, SW_GROUP * (kv + 1))
                kvs = slice(SW_HD * kv, SW_HD * (kv + 1))
                dvp_s[:, kvs] = sum(_dot(probs[h][0], doh[h], TN) for h in group)
                dvc_s[:, kvs] = sum(_dot(probs[h][1], doh[h], TN) for h in group)
                dkp_s[:, kvs] = sum(_dot(dsp[h], qh[h], TN) for h in group)
                dkc_s[:, kvs] = sum(_dot(dsc[h], qh[h], TN) for h in group)
            dq, gq = _sw_norm_rope_bwd(dqr_s[...], qv, rq, qg_ref[...], segv, cos_c, sin_c)
            dkc, gkc = _sw_norm_rope_bwd(dkc_s[...], kcv, rkc, kg_ref[...], segv, cos_c, sin_c)
            dkp, gkp = _sw_norm_rope_bwd(dkp_s[...], kpv, rkp, kg_ref[...], segv, cos_p, sin_p)
            gq_acc[...] += gq
            gk_acc[...] += gkc + gkp
            sk_acc[...] += dsk

            @pl.when(n > 0)
            def _():
                dp_ref[:, 0:512] = _mx(dq_car[...])
                dp_ref[:, 512:640] = _mx(dkv_car[:, 0:128] + dkp)
                dp_ref[:, 640:768] = _mx(dkv_car[:, 128:256] + dvp_s[...])

            dq_car[...] = dq
            dkv_car[:, 0:128] = dkc
            dkv_car[:, 128:256] = dvc_s[...]

        @pl.when(n == nb)
        def _():
            dp_ref[:, 0:512] = _mx(dq_car[...])
            dp_ref[:, 512:768] = _mx(dkv_car[...])

        @pl.when(last)
        def _():
            gq = gq_acc[...]
            acc = gq[:, 0:SW_HD]
            for h in range(1, SW_HEADS):
                acc = acc + gq[:, SW_HD * h:SW_HD * (h + 1)]
            dqg_ref[...] = acc
            gk = gk_acc[...]
            dkg_ref[...] = gk[:, 0:SW_HD] + gk[:, SW_HD:2 * SW_HD]
            dsk_ref[...] = sk_acc[...]

    rowq = pl.BlockSpec((SW_BLOCK, 512), lambda b, n: (cur(b, n), 0))
    full = lambda a: pl.BlockSpec(a.shape, lambda b, n: (0,) * a.ndim)

    def out_row(b, n):
        return b * nb + jnp.maximum(n - 1, 0)

    return pl.pallas_call(
        body, name="swa_bwd", grid=(bsz, nb + 1),
        in_specs=[rowq,
                  pl.BlockSpec((SW_BLOCK, 128), lambda b, n: (cur(b, n), 4)),
                  pl.BlockSpec((SW_BLOCK, 128), lambda b, n: (prev(b, n), 4)),
                  pl.BlockSpec((SW_BLOCK, 128), lambda b, n: (cur(b, n), 5)),
                  pl.BlockSpec((SW_BLOCK, 128), lambda b, n: (prev(b, n), 5)),
                  pl.BlockSpec((SW_BLOCK, 1), lambda b, n: (cur(b, n), 0)),
                  pl.BlockSpec((SW_BLOCK, 1), lambda b, n: (prev(b, n), 0)),
                  full(qg), full(kg),
                  pl.BlockSpec(memory_space=pltpu.SMEM),
                  full(freq), full(sign), full(seg),
                  pl.BlockSpec((SW_BLOCK, 512), lambda b, n: (cur(b, n), 1)),
                  pl.BlockSpec((SW_BLOCK, 512), lambda b, n: (cur(b, n), 1))],
        out_specs=(pl.BlockSpec((SW_BLOCK, SW_COLS), lambda b, n: (out_row(b, n), 0)),
                   pl.BlockSpec((1, SW_HD), lambda b, n: (0, 0)),
                   pl.BlockSpec((1, SW_HD), lambda b, n: (0, 0)),
                   pl.BlockSpec((1, 128), lambda b, n: (0, 0))),
        out_shape=(jax.ShapeDtypeStruct((t, SW_COLS), _MXU_DTYPE),
                   jax.ShapeDtypeStruct((1, SW_HD), F32),
                   jax.ShapeDtypeStruct((1, SW_HD), F32),
                   jax.ShapeDtypeStruct((1, 128), F32)),
        scratch_shapes=[pltpu.VMEM((SW_BLOCK, 512), F32), pltpu.VMEM((SW_BLOCK, 256), F32),
                        pltpu.VMEM((SW_BLOCK, 512), F32),
                        pltpu.VMEM((SW_BLOCK, 128), F32), pltpu.VMEM((SW_BLOCK, 128), F32),
                        pltpu.VMEM((SW_BLOCK, 128), F32), pltpu.VMEM((SW_BLOCK, 128), F32),
                        pltpu.VMEM((1, 512), F32), pltpu.VMEM((1, 128), F32), pltpu.VMEM((1, 128), F32)],
        compiler_params=_params(("arbitrary", "arbitrary")),
    )(proj, proj, proj, proj, proj, pos, pos, qg, kg, sinks, freq, sign, seg, y, dy)


def _head_rms(tv, gain):
    r = lax.rsqrt(jnp.mean(tv * tv, axis=1, keepdims=True) + EPS)
    return tv * r * gain, r


def _head_rms_bwd(dtn, tv, r, gain):
    u = dtn * gain
    return r * u - tv * (r * r * r) * jnp.mean(u * tv, axis=1, keepdims=True), jnp.sum(dtn * tv * r, axis=0, keepdims=True)


def _xa_softmax(raw):
    s = raw * (XA_HD ** -0.5)
    e = jnp.exp(s - jnp.max(s, axis=1, keepdims=True))
    return e / jnp.sum(e, axis=1, keepdims=True)


def _xa_fwd(qx, kvx, qg, kg, bsz, seq, mlen, *, tq=512):
    t = qx.shape[0]
    tq = min(tq, seq)
    nq = seq // tq
    w = XA_HEADS * XA_HD

    def body(q_ref, kv_ref, qg_ref, kg_ref, o_ref):
        heads = range(XA_HEADS)
        hs = [slice(XA_HD * h, XA_HD * (h + 1)) for h in heads]
        qn = [_head_rms(q_ref[:, hs[h]], qg_ref[...])[0] for h in heads]
        kn = [_head_rms(kv_ref[:, hs[h]], kg_ref[...])[0] for h in heads]
        raw = [_dot(qn[h], kn[h], NT) for h in heads]
        p = [_xa_softmax(raw[h]) for h in heads]
        for h in heads:
            o_ref[:, hs[h]] = _dot(p[h], kv_ref[:, w + XA_HD * h:w + XA_HD * (h + 1)]).astype(o_ref.dtype)

    vec = pl.BlockSpec((1, XA_HD), lambda b, i: (0, 0))
    return pl.pallas_call(
        body, name="xattn_fwd", grid=(bsz, nq),
        in_specs=[pl.BlockSpec((tq, w), lambda b, i: (b * nq + i, 0)),
                  pl.BlockSpec((mlen, 2 * w), lambda b, i: (b, 0)), vec, vec],
        out_specs=pl.BlockSpec((tq, w), lambda b, i: (b * nq + i, 0)),
        out_shape=jax.ShapeDtypeStruct((t, w), _MXU_DTYPE),
        compiler_params=_params(("parallel", "parallel")),
    )(qx, kvx, qg, kg)


def _xa_bwd(qx, kvx, qg, kg, do, bsz, seq, mlen, *, tq=512):
    t = qx.shape[0]
    tq = min(tq, seq)
    nq = seq // tq
    w = XA_HEADS * XA_HD
    scale = XA_HD ** -0.5

    def body(q_ref, kv_ref, qg_ref, kg_ref, do_ref, dq_ref, dkv_ref, dqg_ref, dkg_ref):
        b, i = pl.program_id(0), pl.program_id(1)

        @pl.when(jnp.logical_and(b == 0, i == 0))
        def _():
            dqg_ref[...] = jnp.zeros_like(dqg_ref)
            dkg_ref[...] = jnp.zeros_like(dkg_ref)

        @pl.when(i == 0)
        def _():
            dkv_ref[...] = jnp.zeros_like(dkv_ref)

        heads = range(XA_HEADS)
        hs = [slice(XA_HD * h, XA_HD * (h + 1)) for h in heads]
        vs = [slice(w + XA_HD * h, w + XA_HD * (h + 1)) for h in heads]
        qv = [q_ref[:, hs[h]] for h in heads]
        kv = [kv_ref[:, hs[h]] for h in heads]
        doh = [do_ref[:, hs[h]] for h in heads]
        qn = [_head_rms(qv[h], qg_ref[...]) for h in heads]
        kn = [_head_rms(kv[h], kg_ref[...]) for h in heads]
        raw = [_dot(qn[h][0], kn[h][0], NT) for h in heads]
        dp = [_dot(doh[h], kv_ref[:, vs[h]], NT) for h in heads]
        p = [_xa_softmax(raw[h]) for h in heads]
        ds = [p[h] * (dp[h] - jnp.sum(p[h] * dp[h], axis=1, keepdims=True)) * scale for h in heads]
        dqn = [_dot(ds[h], kn[h][0]) for h in heads]
        dkn = [_dot(ds[h], qn[h][0], TN) for h in heads]
        dvv = [_dot(p[h], doh[h], TN) for h in heads]
        gq_sum = jnp.zeros((1, XA_HD), F32)
        gk_sum = jnp.zeros((1, XA_HD), F32)
        for h in heads:
            dqv, gq = _head_rms_bwd(dqn[h], qv[h], qn[h][1], qg_ref[...])
            dkv, gk = _head_rms_bwd(dkn[h], kv[h], kn[h][1], kg_ref[...])
            dq_ref[:, hs[h]] = dqv.astype(dq_ref.dtype)
            dkv_ref[:, hs[h]] += dkv
            dkv_ref[:, vs[h]] += dvv[h]
            gq_sum = gq_sum + gq
            gk_sum = gk_sum + gk
        dqg_ref[...] += gq_sum
        dkg_ref[...] += gk_sum

    vec = pl.BlockSpec((1, XA_HD), lambda b, i: (0, 0))
    row = pl.BlockSpec((tq, w), lambda b, i: (b * nq + i, 0))
    mem = pl.BlockSpec((mlen, 2 * w), lambda b, i: (b, 0))
    return pl.pallas_call(
        body, name="xattn_bwd", grid=(bsz, nq),
        in_specs=[row, mem, vec, vec, row],
        out_specs=(row, mem, vec, vec),
        out_shape=(jax.ShapeDtypeStruct((t, w), _MXU_DTYPE), jax.ShapeDtypeStruct((bsz * mlen, 2 * w), F32),
                   jax.ShapeDtypeStruct((1, XA_HD), F32), jax.ShapeDtypeStruct((1, XA_HD), F32)),
        compiler_params=_params(("arbitrary", "arbitrary")),
    )(qx, kvx, qg, kg, do)


def _loss_finish(sq_row, d_model):
    def body(s_ref, o_ref):
        o_ref[...] = jnp.zeros_like(o_ref) + 0.5 * jnp.sum(s_ref[...]) / float(d_model)

    return pl.pallas_call(body, name="loss_finish", out_shape=jax.ShapeDtypeStruct((1, 128), F32))(sq_row)


def _adamw_math(w, g, m, v):
    m = ADAM_B1 * m + (1.0 - ADAM_B1) * g
    v = ADAM_B2 * v + (1.0 - ADAM_B2) * (g * g)
    m_hat = m / (1.0 - ADAM_B1 ** ADAM_STEP)
    v_hat = v / (1.0 - ADAM_B2 ** ADAM_STEP)
    return -ADAM_LR * (m_hat / (jnp.sqrt(v_hat) + ADAM_EPS) + ADAM_WD * w), m, v


def _adamw_big(w, g, m, v, *, name, tr=256):
    r, c = w.shape
    tr = min(tr, r)

    def body(w_ref, g_ref, m_ref, v_ref, d_ref, mo_ref, vo_ref):
        d, mn, vn = _adamw_math(w_ref[...], g_ref[...], m_ref[...], v_ref[...])
        d_ref[...] = d
        mo_ref[...] = mn
        vo_ref[...] = vn

    spec = pl.BlockSpec((tr, c), lambda i: (i, 0))
    shp = jax.ShapeDtypeStruct((r, c), F32)
    return pl.pallas_call(
        body, name=name, grid=(r // tr,), in_specs=[spec] * 4, out_specs=(spec,) * 3, out_shape=(shp,) * 3,
        compiler_params=_params(("parallel",)),
    )(w, g, m, v)


def _adamw_small(ws, gs, ms, vs):
    n = len(ws)

    def body(*refs):
        for i in range(n):
            d, mn, vn = _adamw_math(refs[i][...], refs[n + i][...], refs[2 * n + i][...], refs[3 * n + i][...])
            refs[4 * n + i][...] = d
            refs[5 * n + i][...] = mn
            refs[6 * n + i][...] = vn

    shapes = tuple(jax.ShapeDtypeStruct(w.shape, F32) for w in ws)
    return pl.pallas_call(body, name="adamw_small", out_shape=shapes * 3)(*ws, *gs, *ms, *vs)


def _add_halves(g, recv, c_idx, *, name, tr=256):
    _, r, c = g.shape
    h = r // 2
    tr = min(tr, h)
    nt = h // tr

    def body(c_ref, g_ref, r_ref, o_ref):
        del c_ref
        o_ref[...] = g_ref[...] + r_ref[...]

    return pl.pallas_call(
        body, name=name,
        grid_spec=pltpu.PrefetchScalarGridSpec(
            num_scalar_prefetch=1, grid=(4, nt),
            in_specs=[pl.BlockSpec((None, tr, c), lambda k, i, cr: (k, cr[0] * nt + i, 0)),
                      pl.BlockSpec((None, tr, c), lambda k, i, cr: (k, i, 0))],
            out_specs=pl.BlockSpec((None, tr, c), lambda k, i, cr: (k, i, 0))),
        out_shape=jax.ShapeDtypeStruct((4, h, c), F32),
        compiler_params=_params(("parallel", "parallel")),
    )(c_idx, g, recv)


def _add_chips(p, recv, place_idx, *, name, tr=256):
    _, h, c = p.shape
    tr = min(tr, h)
    nt = h // tr

    def body(pi_ref, p_ref, r_ref, o_ref):
        del pi_ref
        o_ref[...] = ((p_ref[...] + r_ref[0]) + r_ref[1]) + r_ref[2]

    return pl.pallas_call(
        body, name=name,
        grid_spec=pltpu.PrefetchScalarGridSpec(
            num_scalar_prefetch=1, grid=(nt,),
            in_specs=[pl.BlockSpec((None, tr, c), lambda i, pi: (pi[0], i, 0)),
                      pl.BlockSpec((3, tr, c), lambda i, pi: (0, i, 0))],
            out_specs=pl.BlockSpec((tr, c), lambda i, pi: (pi[1] * nt + i, 0))),
        out_shape=jax.ShapeDtypeStruct((2 * h, c), F32),
        compiler_params=_params(("parallel",)),
    )(place_idx, p, recv)


def _place_shard(shard, place_idx, *, name, tr=256, after=()):
    r, c = shard.shape
    tr = min(tr, r)

    def body(pi_ref, s_ref, *rest):
        del pi_ref
        rest[-1][...] = s_ref[...]

    return pl.pallas_call(
        body, name=name,
        grid_spec=pltpu.PrefetchScalarGridSpec(
            num_scalar_prefetch=1, grid=(r // tr,),
            in_specs=[pl.BlockSpec((tr, c), lambda i, pi: (i, 0))] + [pl.BlockSpec(memory_space=pl.ANY)] * len(after),
            out_specs=pl.BlockSpec((None, tr, c), lambda i, pi: (pi[0], i, 0))),
        out_shape=jax.ShapeDtypeStruct((4, r, c), shard.dtype),
        compiler_params=_params(("parallel",)),
    )(place_idx, shard, *after)


def _place():
    x, y, c = lax.axis_index("x"), lax.axis_index("y"), lax.axis_index("c")
    chips = [(1 - x, y), (x, 1 - y), (1 - x, 1 - y)]
    return x, y, c, chips


ANY = pl.BlockSpec(memory_space=pl.ANY)


def _exchange_halves(grads, name):
    n = len(grads)

    def body(*refs):
        ins, outs = refs[:n], refs[n:2 * n]
        send_sems, recv_sems = refs[2 * n:]
        x, y, c, _ = _place()

        def copy(a):
            h = ins[a].shape[1] // 2
            return pltpu.make_async_remote_copy(
                src_ref=ins[a].at[:, pl.ds((1 - c) * h, h), :], dst_ref=outs[a],
                send_sem=send_sems.at[a], recv_sem=recv_sems.at[a], device_id=(x, y, 1 - c), device_id_type=MESH)

        for a in range(n):
            copy(a).start()
        for a in range(n):
            copy(a).wait_recv()
        for a in range(n):
            copy(a).wait_send()

    return pl.pallas_call(
        body, name=name,
        in_specs=[ANY] * n, out_specs=tuple([ANY] * n),
        out_shape=tuple(jax.ShapeDtypeStruct((4, g.shape[1] // 2, g.shape[2]), g.dtype) for g in grads),
        scratch_shapes=[pltpu.SemaphoreType.DMA((n,)), pltpu.SemaphoreType.DMA((n,))],
    )(*grads)


HBM = pl.BlockSpec(memory_space=pltpu.HBM)
SEM = pl.BlockSpec(memory_space=pltpu.SEMAPHORE)
EFFECT = pltpu.SideEffectType.DATAFLOW_SIDE_EFFECTING


def _in_hbm(a):
    return pltpu.with_memory_space_constraint(a, pltpu.HBM)


def _split_copy_calls(name, srcs, lands, n_copies, make_copies):
    ns, nl = len(srcs), len(lands)
    nb = ns + nl

    def start(after=()):
        n_after = len(after)

        def body(*refs):
            outs = refs[nb + n_after:]
            copies = make_copies(refs[:ns], refs[ns:nb], outs[0], outs[1])
            for cp in copies:
                cp.start()
            token = refs[-1]
            token[...] = jnp.zeros_like(token)

        bufs = [_in_hbm(a) for a in list(srcs) + list(lands)]
        out = pl.pallas_call(
            body, name=name + "_start",
            out_shape=(pltpu.SemaphoreType.DMA((n_copies,)), pltpu.SemaphoreType.DMA((n_copies,)),
                       *[pltpu.HBM(a.shape, a.dtype) for a in bufs], jax.ShapeDtypeStruct((8, 128), F32)),
            in_specs=[HBM] * nb + [pl.BlockSpec(memory_space=pl.ANY)] * n_after,
            out_specs=(SEM, SEM, *[HBM] * nb, pl.BlockSpec(memory_space=pltpu.VMEM)),
            input_output_aliases={i: 2 + i for i in range(nb)},
            compiler_params=pltpu.CompilerParams(has_side_effects=EFFECT),
        )(*bufs, *after)
        return dict(send=out[0], recv=out[1], bufs=list(out[2:2 + nb]), token=out[-1])

    def wait(state, after):
        def body(*refs):
            copies = make_copies(refs[:ns], refs[ns:nb], refs[nb], refs[nb + 1])
            for cp in copies:
                cp.wait_send()
            for cp in copies:
                cp.wait_recv()

        bufs = state["bufs"]
        out = pl.pallas_call(
            body, name=name + "_wait",
            out_shape=tuple(pltpu.HBM(a.shape, a.dtype) for a in bufs),
            in_specs=[HBM] * nb + [SEM, SEM, pl.BlockSpec(memory_space=pl.ANY)], out_specs=tuple([HBM] * nb),
            input_output_aliases={i: i for i in range(nb)},
            compiler_params=pltpu.CompilerParams(has_side_effects=EFFECT),
        )(*bufs, state["send"], state["recv"], after)
        return list(out[:ns]), list(out[ns:])

    return start, wait


def _scatter_chips_split(name, parts):
    n = len(parts)
    lands = [lax.empty((3,) + p.shape[1:], p.dtype) for p in parts]

    def make_copies(srcs, lnds, send_sems, recv_sems):
        _, _, c, chips = _place()
        return [pltpu.make_async_remote_copy(
            src_ref=srcs[a].at[2 * px + py], dst_ref=lnds[a].at[j], send_sem=send_sems.at[a * 3 + j],
            recv_sem=recv_sems.at[a * 3 + j], device_id=(px, py, c), device_id_type=MESH)
            for a in range(n) for j, (px, py) in enumerate(chips)]

    return _split_copy_calls(name, parts, lands, 3 * n, make_copies)


def _exchange_halves_split(name, grads):
    n = len(grads)
    lands = [lax.empty((4, g.shape[1] // 2, g.shape[2]), g.dtype) for g in grads]

    def make_copies(srcs, lnds, send_sems, recv_sems):
        x, y, c, _ = _place()
        out = []
        for a in range(n):
            h = srcs[a].shape[1] // 2
            out.append(pltpu.make_async_remote_copy(
                src_ref=srcs[a].at[:, pl.ds((1 - c) * h, h), :], dst_ref=lnds[a], send_sem=send_sems.at[a],
                recv_sem=recv_sems.at[a], device_id=(x, y, 1 - c), device_id_type=MESH))
        return out

    return _split_copy_calls(name, grads, lands, n, make_copies)


def _gather_chips_split(name, shards, lands):
    n = len(shards)

    def make_copies(srcs, lnds, send_sems, recv_sems):
        x, y, c, chips = _place()
        out = []
        for a in range(n):
            h = srcs[a].shape[0] // 2
            for j, (px, py) in enumerate(chips):
                out.append(pltpu.make_async_remote_copy(
                    src_ref=srcs[a].at[pl.ds(c * h, h), :], dst_ref=lnds[a].at[2 * x + y, pl.ds(c * h, h), :],
                    send_sem=send_sems.at[a * 3 + j], recv_sem=recv_sems.at[a * 3 + j],
                    device_id=(px, py, c), device_id_type=MESH))
        return out

    return _split_copy_calls(name, shards, lands, 3 * n, make_copies)


def _gather_finish(gathered, name):
    n = len(gathered)

    def body(*refs):
        outs = refs[n:2 * n]
        send_sems, recv_sems = refs[2 * n:]
        x, y, c, chips = _place()

        def copy(a, j, chip_idx, which):
            h = outs[a].shape[1] // 2
            rows = outs[a].at[chip_idx, pl.ds(which * h, h), :]
            return pltpu.make_async_remote_copy(
                src_ref=rows, dst_ref=rows, send_sem=send_sems.at[a * 3 + j], recv_sem=recv_sems.at[a * 3 + j],
                device_id=(x, y, 1 - c), device_id_type=MESH)

        for a in range(n):
            for j, (px, py) in enumerate(chips):
                copy(a, j, 2 * px + py, c).start()
        for a in range(n):
            for j, (px, py) in enumerate(chips):
                copy(a, j, 2 * px + py, 1 - c).wait_recv()
        for a in range(n):
            for j, (px, py) in enumerate(chips):
                copy(a, j, 2 * px + py, c).wait_send()

    return pl.pallas_call(
        body, name=name,
        in_specs=[ANY] * n, out_specs=tuple([ANY] * n),
        out_shape=tuple(jax.ShapeDtypeStruct(g.shape, g.dtype) for g in gathered),
        input_output_aliases={i: i for i in range(n)},
        scratch_shapes=[pltpu.SemaphoreType.DMA((3 * n,)), pltpu.SemaphoreType.DMA((3 * n,))],
    )(*gathered)


def _join_halves(fulls):
    n = len(fulls)

    def body(*refs):
        outs = refs[n:2 * n]
        send_sems, recv_sems = refs[2 * n:]
        x, y, c, _ = _place()

        def copy(a, which):
            h = outs[a].shape[0] // 2
            rows = outs[a].at[pl.ds(which * h, h), :]
            return pltpu.make_async_remote_copy(
                src_ref=rows, dst_ref=rows, send_sem=send_sems.at[a], recv_sem=recv_sems.at[a],
                device_id=(x, y, 1 - c), device_id_type=MESH)

        for a in range(n):
            copy(a, c).start()
        for a in range(n):
            copy(a, 1 - c).wait_recv()
        for a in range(n):
            copy(a, c).wait_send()

    return pl.pallas_call(
        body, name="rs_join_halves",
        in_specs=[ANY] * n, out_specs=tuple([ANY] * n),
        out_shape=tuple(jax.ShapeDtypeStruct(p.shape, p.dtype) for p in fulls),
        input_output_aliases={i: i for i in range(n)},
        scratch_shapes=[pltpu.SemaphoreType.DMA((n,)), pltpu.SemaphoreType.DMA((n,))],
    )(*fulls)


def _all_reduce_small(sm):
    r, w = sm.shape

    def body(sm_ref, o_ref, buf, send_sems, recv_sems):
        x, y, c, _ = _place()
        me = 4 * x + 2 * y + c
        buf[me] = sm_ref[...]
        rel = [(dx, dy, dc) for dx in (0, 1) for dy in (0, 1) for dc in (0, 1)][1:]

        def copy(k, slot, to):
            return pltpu.make_async_remote_copy(
                src_ref=sm_ref, dst_ref=buf.at[slot], send_sem=send_sems.at[k], recv_sem=recv_sems.at[k],
                device_id=to, device_id_type=MESH)

        peers = []
        for k, (dx, dy, dc) in enumerate(rel):
            px = 1 - x if dx else x
            py = 1 - y if dy else y
            pc = 1 - c if dc else c
            peers.append((px, py, pc))
            copy(k, me, (px, py, pc)).start()
        for k, (px, py, pc) in enumerate(peers):
            copy(k, 4 * px + 2 * py + pc, (px, py, pc)).wait_recv()
        for k, (px, py, pc) in enumerate(peers):
            copy(k, me, (px, py, pc)).wait_send()
        acc = buf[0]
        for d in range(1, 8):
            acc = acc + buf[d]
        o_ref[...] = acc

    vm = pl.BlockSpec(memory_space=pltpu.VMEM)
    return pl.pallas_call(
        body, name="all_reduce_small", in_specs=[vm], out_specs=vm,
        out_shape=jax.ShapeDtypeStruct((r, w), F32),
        scratch_shapes=[pltpu.VMEM((8, r, w), F32), pltpu.SemaphoreType.DMA((7,)), pltpu.SemaphoreType.DMA((7,))],
    )(sm)


class _LocalWeights:
    def __init__(self, w):
        self.w = w
        self.g = {}

    def begin(self):
        pass

    def first(self, after):
        del after
        return self.w

    def rest(self, after):
        del after
        return self.w

    def grads(self, tag, g):
        del tag
        self.g.update(g)
        return None

    def poll(self, after):
        del after
        return None


def _local_step(x3, mem3, pos2, target3, small, comm):
    bsz, seq, d = x3.shape
    mlen = mem3.shape[1]
    t = bsz * seq
    comm.begin()
    x = x3.reshape(t, d)
    mem = mem3.reshape(bsz * mlen, d)
    target = target3.reshape(t, d)
    pos = pos2.reshape(t, 1)
    qg_t = jnp.tile(small["sw_q_norm_g"], (1, SW_HEADS))
    kg_t = jnp.tile(small["sw_k_norm_g"], (1, SW_KV_HEADS))

    hn1 = _rms_fwd(x, small["norm1_g"], name="rms1_fwd")
    w = comm.first(hn1)
    proj_hg = _mm(hn1, w["w_in_hg"], NN, t, HG_COLS, d, name="proj_hg", tk=d, after=(w.get("token"),))[0]
    proj_sw = _mm(hn1, w["w_in_sw"], NN, t, SW_COLS, d, name="proj_sw", tk=d)[0]
    y_mix, o_hg, states = _hg_fwd(proj_hg, small["hg_lower_bounds"], small["hg_norm_g"], bsz, seq, y_width=1024)
    y_mix = _sw_fwd(proj_sw, pos, qg_t, kg_t, small["sw_sinks"], y_mix, bsz, seq)
    w_in_hg, w_in_sw = w["w_in_hg"], w["w_in_sw"]
    w = comm.rest(y_mix)
    ff = w["down"].shape[0]
    ffs = ff // 4
    h1, hn2 = _mm(y_mix, w["w_out"], NN, t, d, 1024, name="out_proj", tk=1024, extras=(x,), rows=(small["norm2_g"],),
                  epilogue=_residual_rms, out_dtypes=(F32, _MXU_DTYPE))
    mn = _rms_fwd(mem, small["mem_norm_g"], name="rms_mem_fwd")
    qx = _mm(hn2, w["wq"], NN, t, 512, d, name="xa_q", tk=d)[0]
    kvx = _mm(mn, w["wkv"], NN, bsz * mlen, 1024, d, name="xa_kv", tk=d)[0]
    ox = _xa_fwd(qx, kvx, small["xa_q_norm_g"], small["xa_k_norm_g"], bsz, seq, mlen)
    h2, hn3 = _mm(ox, w["wo"], NN, t, d, 512, name="xa_o", tk=512, extras=(h1,), rows=(small["norm3_g"],),
                  epilogue=_residual_rms, out_dtypes=(F32, _MXU_DTYPE))

    def relu_sq(acc):
        a = jnp.maximum(acc, 0.0)
        return a, a * a

    act, act2 = _mm(hn3, w["up"], NN, t, ff, d, name="mlp_up", tm=2048, tn=ffs, tk=d,
                    b_spec=pl.BlockSpec((None, d, ffs), lambda i, j, kk: (j, 0, 0)),
                    epilogue=relu_sq, out_dtypes=(_MXU_DTYPE, _MXU_DTYPE))
    inv_d = 1.0 / d

    def loss_cotangent(acc, res, tgt):
        diff = acc + res - tgt
        v = diff * inv_d
        return v, v, jnp.sum(diff * diff, axis=0, keepdims=True)

    dy, dy_mx, sq_row = _mm(act2, w["down"], NN, t, d, ff, name="mlp_down", extras=(h2, target),
                            epilogue=loss_cotangent, out_dtypes=(F32, _MXU_DTYPE), row_sums=1)
    loss_row = _loss_finish(sq_row, d)

    dz = _mm(dy_mx, w["down"], NT, t, ff, d, name="d_act", tm=2048, tk=d, extras=(act,),
             epilogue=lambda acc, a: (acc * (2.0 * a.astype(F32)),), out_dtypes=(_MXU_DTYPE,))[0]
    g_down = _mm(act2, dy_mx, TN, ff, d, t, name="g_down")[0]
    g_up = _mm(hn3, dz, TN, d, ff, t, name="g_up", tn=ffs,
               out_shape=(jax.ShapeDtypeStruct((4, d, ffs), F32),),
               out_spec=(pl.BlockSpec((None, min(1024, d), ffs), lambda i, j, kk: (j, i, 0)),))[0]
    tok = comm.grads("mlp", dict(up=g_up, down=g_down))
    dh2, dh2_mx, g_norm3 = _mm(dz, w["up"], NT, t, d, ff, name="d_hn3", tk=ffs, after=(tok,),
                               b_spec=pl.BlockSpec((None, min(1024, d), ffs), lambda i, j, kk: (kk, j, 0)),
                               extras=(h2, dy), rows=(small["norm3_g"],), epilogue=_rms_bwd_residual,
                               out_dtypes=(F32, _MXU_DTYPE), row_sums=1)
    d_ox = _mm(dh2_mx, w["wo"], NT, t, 512, d, name="d_ox", tk=d)[0]
    g_wo = _mm(ox, dh2_mx, TN, 512, d, t, name="g_wo")[0]
    d_qx, d_kvx, g_xq, g_xk = _xa_bwd(qx, kvx, small["xa_q_norm_g"], small["xa_k_norm_g"], d_ox, bsz, seq, mlen)
    g_wq = _mm(hn2, d_qx, TN, d, 512, t, name="g_wq")[0]
    g_wkv = _mm(mn, d_kvx, TN, d, 1024, bsz * mlen, name="g_wkv")[0]
    dh1, dh1_mx, g_norm2 = _mm(d_qx, w["wq"], NT, t, d, 512, name="d_hn2", tk=512, extras=(h1, dh2),
                               rows=(small["norm2_g"],), epilogue=_rms_bwd_residual, out_dtypes=(F32, _MXU_DTYPE),
                               row_sums=1)
    dmn = _mm(d_kvx, w["wkv"], NT, bsz * mlen, d, 1024, name="d_mn", tk=1024)[0]
    g_memn = _rms_gain_grad(mem, small["mem_norm_g"], dmn, name="rms_mem_bwd")
    g_wout = _mm(y_mix, dh1_mx, TN, 1024, d, t, name="g_wout")[0]
    tok = comm.grads("mid", dict(w_out=g_wout, wq=g_wq, wkv=g_wkv, wo=g_wo))
    d_mix = _mm(dh1_mx, w["w_out"], NT, t, 1024, d, name="d_mix", tk=d, after=(tok,))[0]
    dproj_sw, g_swq, g_swk, g_sinks = _sw_bwd(proj_sw, pos, qg_t, kg_t, small["sw_sinks"], y_mix, d_mix, bsz, seq)
    tok = comm.poll(dproj_sw)
    dproj_hg, g_lb, g_hgn = _hg_bwd(proj_hg, small["hg_lower_bounds"], small["hg_norm_g"], o_hg, states, d_mix, bsz, seq,
                                    after=(tok,))
    g_in_hg = _mm(hn1, dproj_hg, TN, d, HG_COLS, t, name="g_in_hg")[0]
    g_in_sw = _mm(hn1, dproj_sw, TN, d, SW_COLS, t, name="g_in_sw")[0]
    tok = comm.grads("in", dict(w_in_hg=g_in_hg, w_in_sw=g_in_sw))
    dhn1_a = _mm(dproj_hg, w_in_hg, NT, t, d, HG_COLS, name="d_hn1_hg", tk=1024, after=(tok,))[0]
    grad_x, g_norm1 = _mm(dproj_sw, w_in_sw, NT, t, d, SW_COLS, name="d_hn1_sw", tk=SW_COLS, extras=(dhn1_a, x, dh1),
                          rows=(small["norm1_g"],), row_sums=1,
                          epilogue=lambda acc, prev, xv, dres, g: _rms_bwd_residual(acc + prev, xv, dres, g)[1:])

    g_small = dict(norm1_g=g_norm1, hg_lower_bounds=g_lb, hg_norm_g=g_hgn, sw_q_norm_g=g_swq, sw_k_norm_g=g_swk,
                   sw_sinks=g_sinks[:, 0:SW_HEADS], norm2_g=g_norm2, mem_norm_g=g_memn, xa_q_norm_g=g_xq,
                   xa_k_norm_g=g_xk, norm3_g=g_norm3)
    return loss_row, grad_x.reshape(bsz, seq, d), g_small


SMALL_NAMES = ("norm1_g", "hg_lower_bounds", "hg_norm_g", "sw_q_norm_g", "sw_k_norm_g", "sw_sinks", "norm2_g",
               "mem_norm_g", "xa_q_norm_g", "xa_k_norm_g", "norm3_g")
BIG_NAMES = ("w_in", "w_out", "xa_wq", "xa_wkv", "xa_wo", "mlp_up", "mlp_down")
WEIGHT_ORDER = ("norm1_g", "w_in", "hg_lower_bounds", "hg_norm_g", "sw_q_norm_g", "sw_k_norm_g", "sw_sinks", "w_out",
                "norm2_g", "mem_norm_g", "xa_wq", "xa_wkv", "xa_q_norm_g", "xa_k_norm_g", "xa_wo", "norm3_g",
                "mlp_up", "mlp_down")


def _pack_rows(vals, width):
    starts, at = [], 0
    for v in vals:
        starts.append(at)
        at += v.shape[0]
    total = at + (-at) % 8
    out = None
    for v, s in zip(vals, starts):
        placed = jnp.pad(v, ((s, total - s - v.shape[0]), (0, width - v.shape[1])))
        out = placed if out is None else out + placed
    return out, starts


class _MeshWeights:
    LATE = ("w_out", "xa_wq", "xa_wkv", "xa_wo", "mlp_up", "mlp_down")

    def __init__(self, shards, d, ff):
        self.shards, self.d, self.ff = shards, d, ff
        self.c_idx = lax.axis_index("c").astype(jnp.int32).reshape(1)
        chip = (2 * lax.axis_index("x") + lax.axis_index("y")).astype(jnp.int32)
        self.place_idx = jnp.stack([chip, lax.axis_index("c").astype(jnp.int32)])
        self.pending = []
        self.exchanging = None
        self.halves = {}

    def begin(self):
        shard = self.shards["w_in"]
        start, self.in_wait = _gather_chips_split(
            "gather_in", [shard], [_place_shard(shard, self.place_idx, name="place_w_in")])
        self.in_state = start()
        tok = (self.in_state["token"],)
        self.placed = [_place_shard(self.shards[n], self.place_idx, name="place_" + n, after=tok) for n in self.LATE]

    def first(self, after):
        _, lands = self.in_wait(self.in_state, after)
        (g_in,) = _gather_finish(lands, "gather_in_finish")
        start, self.late_wait = _gather_chips_split("gather_late", [self.shards[n] for n in self.LATE], self.placed)
        self.late_state = start(after=(g_in,))
        full = jnp.concatenate([g_in[k] for k in range(4)], axis=1)
        return dict(w_in_hg=full[:, :HG_COLS], w_in_sw=full[:, HG_COLS:], token=self.late_state["token"])

    def rest(self, after):
        _, lands = self.late_wait(self.late_state, after)
        g_out, g_q, g_kv, g_o, g_up, g_dn = _gather_finish(lands, "gather_late_finish")
        d = self.d
        return dict(w_out=g_out.reshape(-1, d), wq=g_q.reshape(d, -1), wkv=g_kv.reshape(d, -1),
                    wo=jnp.concatenate([g_o[k] for k in range(4)], axis=1), up=g_up, down=g_dn.reshape(self.ff, d))

    def _scatter(self, tag, names, arrays, recv):
        parts = [_add_halves(g, r, self.c_idx, name="rs_add_halves_" + n) for n, g, r in zip(names, arrays, recv)]
        start, wait = _scatter_chips_split("rs_scatter_" + tag, parts)
        state = start()
        self.pending.append((names, wait, state))
        return state["token"]

    def _advance(self, after):
        if self.exchanging is not None:
            tag, names, wait, state = self.exchanging
            self.exchanging = None
            arrays, recv = wait(state, after)
            self._scatter(tag, names, arrays, recv)

    def poll(self, after):
        self._advance(after)
        return self.pending[-1][2]["token"]

    def grads(self, tag, g):
        d, ff = self.d, self.ff
        if tag == "mlp":
            names, arrays = ("mlp_up", "mlp_down"), [g["up"], g["down"].reshape(4, ff // 4, d)]
        elif tag == "mid":
            names = ("w_out", "xa_wq", "xa_wkv", "xa_wo")
            ds = d // 4
            g_wo = jnp.stack([g["wo"][:, ds * k:ds * (k + 1)] for k in range(4)])
            arrays = [g["w_out"].reshape(4, -1, d), g["wq"].reshape(4, d // 4, -1), g["wkv"].reshape(4, d // 4, -1), g_wo]
        else:
            full = jnp.concatenate([g["w_in_hg"], g["w_in_sw"]], axis=1)
            ws = full.shape[1] // 4
            names, arrays = ("w_in",), [jnp.stack([full[:, ws * k:ws * (k + 1)] for k in range(4)])]
        self._advance(arrays[0])
        if tag == "in":
            return self._scatter(tag, names, arrays, _exchange_halves(arrays, "rs_exchange_" + tag))
        start, wait = _exchange_halves_split("rs_exchange_" + tag, arrays)
        state = start()
        self.exchanging = (tag, names, wait, state)
        return state["token"]

    def finish(self, after):
        for names, wait, state in self.pending:
            srcs, lands = wait(state, after)
            for n, p, r in zip(names, srcs, lands):
                self.halves[n] = _add_chips(p, r, self.place_idx, name="rs_add_chips_" + n)
        return dict(zip(BIG_NAMES, _join_halves([self.halves[n] for n in BIG_NAMES])))


def kernel(x, mem, positions, norm1_g, w_in, hg_lower_bounds, hg_norm_g, sw_q_norm_g, sw_k_norm_g, sw_sinks, w_out, norm2_g, mem_norm_g, xa_wq, xa_wkv, xa_q_norm_g, xa_k_norm_g, xa_wo, norm3_g, mlp_up, mlp_down, loss_target, m_norm1_g, m_w_in, m_hg_lower_bounds, m_hg_norm_g, m_sw_q_norm_g, m_sw_k_norm_g, m_sw_sinks, m_w_out, m_norm2_g, m_mem_norm_g, m_xa_wq, m_xa_wkv, m_xa_q_norm_g, m_xa_k_norm_g, m_xa_wo, m_norm3_g, m_mlp_up, m_mlp_down, v_norm1_g, v_w_in, v_hg_lower_bounds, v_hg_norm_g, v_sw_q_norm_g, v_sw_k_norm_g, v_sw_sinks, v_w_out, v_norm2_g, v_mem_norm_g, v_xa_wq, v_xa_wkv, v_xa_q_norm_g, v_xa_k_norm_g, v_xa_wo, v_norm3_g, v_mlp_up, v_mlp_down):
    given = dict(locals())
    weights = {n: given[n] for n in WEIGHT_ORDER}
    moms = {n: given["m_" + n] for n in WEIGHT_ORDER}
    vars_ = {n: given["v_" + n] for n in WEIGHT_ORDER}
    d = x.shape[-1]
    ff = mlp_down.shape[1] * 4
    small = {n: weights[n] for n in SMALL_NAMES}

    comm = _MeshWeights({n: weights[n][0].astype(_MXU_DTYPE) for n in BIG_NAMES}, d, ff)
    loss_row, grad_x, g_small = _local_step(x, mem, positions, loss_target, small, comm)
    big_grads = comm.finish(grad_x)

    packed, starts = _pack_rows([g_small[n] for n in SMALL_NAMES] + [loss_row], 1024)
    summed = _all_reduce_small(packed)
    small_grads = {}
    for n, s in zip(SMALL_NAMES, starts):
        r, c = weights[n].shape
        small_grads[n] = summed[s:s + r, 0:c]
    loss = summed[starts[-1], 0]

    grads, deltas, new_m, new_v = {}, {}, {}, {}
    for n in BIG_NAMES:
        shp = weights[n].shape
        g2 = big_grads[n]
        dl, mo, vo = _adamw_big(weights[n][0], g2, moms[n][0], vars_[n][0], name="adamw_" + n)
        grads[n], deltas[n], new_m[n], new_v[n] = (a.reshape(shp) for a in (g2, dl, mo, vo))
    sm_out = _adamw_small([weights[n] for n in SMALL_NAMES], [small_grads[n] for n in SMALL_NAMES],
                          [moms[n] for n in SMALL_NAMES], [vars_[n] for n in SMALL_NAMES])
    ns = len(SMALL_NAMES)
    for i, n in enumerate(SMALL_NAMES):
        grads[n], deltas[n], new_m[n], new_v[n] = small_grads[n], sm_out[i], sm_out[ns + i], sm_out[2 * ns + i]

    return (loss, grad_x, *[grads[n] for n in WEIGHT_ORDER], *[deltas[n] for n in WEIGHT_ORDER],
            *[new_m[n] for n in WEIGHT_ORDER], *[new_v[n] for n in WEIGHT_ORDER])
```

```python
import numpy as np
import jax
import jax.numpy as jnp
from jax import lax
from jax.experimental import pallas as pl
from jax.experimental.pallas import tpu as pltpu

F32 = jnp.float32
_MXU_DTYPE = jnp.bfloat16

EPS = 1e-6
HG_HEADS = 4
HG_D = 128
HG_CHUNK = 64
HG_TILE = 512
HG_LEVELS = (32, 16, 8, 4, 2, 1)
SW_HEADS = 8
SW_KV_HEADS = 2
SW_GROUP = SW_HEADS // SW_KV_HEADS
SW_HD = 64
SW_BLOCK = 128
ROPE_THETA = 500000.0
ROT_DIM = SW_HD // 4
XA_HEADS = 4
XA_HD = 128
HG_COLS = 4 * HG_HEADS * HG_D
SW_COLS = (SW_HEADS + 2 * SW_KV_HEADS) * SW_HD

ADAM_LR = 0.001
ADAM_B1 = 0.9
ADAM_B2 = 0.999
ADAM_EPS = 1e-08
ADAM_WD = 0.01
ADAM_STEP = 10

VMEM_LIMIT = 56 * 1024 * 1024
MESH = pl.DeviceIdType.MESH

NN = ((1,), (0,))
NT = ((1,), (1,))
TN = ((0,), (0,))


def _mx(v):
    return v.astype(_MXU_DTYPE)


def _dot(a, b, dims=NN):
    return lax.dot_general(_mx(a), _mx(b), (dims, ((), ())), preferred_element_type=F32)


def _split_dot(a, v, dims, parts):
    acc = None
    rest = v
    for p in range(parts):
        piece = _mx(rest)
        term = lax.dot_general(a, piece, (dims, ((), ())), preferred_element_type=F32)
        acc = term if acc is None else acc + term
        if p + 1 < parts:
            rest = rest - piece.astype(F32)
    return acc


def _params(sem):
    return pltpu.CompilerParams(dimension_semantics=sem, vmem_limit_bytes=VMEM_LIMIT)


def _mm(a, b, mode, m, n, k, *, name, tm=1024, tn=1024, tk=1024, a_spec=None, b_spec=None, extras=(), rows=(),
        epilogue=None, out_dtypes=(F32,), row_sums=0, out_shape=None, out_spec=None, after=()):
    after = tuple(t for t in after if t is not None)
    tm, tn, tk = min(tm, m), min(tn, n), min(tk, k)
    assert m % tm == 0 and n % tn == 0 and k % tk == 0, (name, m, n, k, tm, tn, tk)
    gi, gj, gk = m // tm, n // tn, k // tk
    assert row_sums == 0 or gj == 1, name
    if a_spec is None:
        a_spec = (pl.BlockSpec((tk, tm), lambda i, j, kk: (kk, i)) if mode == TN
                  else pl.BlockSpec((tm, tk), lambda i, j, kk: (i, kk)))
    if b_spec is None:
        b_spec = (pl.BlockSpec((tn, tk), lambda i, j, kk: (j, kk)) if mode == NT
                  else pl.BlockSpec((tk, tn), lambda i, j, kk: (kk, j)))
    mn_spec = pl.BlockSpec((tm, tn), lambda i, j, kk: (i, j))
    if epilogue is None:
        epilogue = lambda acc: (acc,)
    row_spec = pl.BlockSpec((1, tn), lambda i, j, kk: (0, j))
    n_ex, n_out = len(extras) + len(rows), len(out_dtypes)
    if out_shape is None:
        out_shape = tuple(jax.ShapeDtypeStruct((m, n), d) for d in out_dtypes)
        out_spec = tuple(mn_spec for _ in out_dtypes)
    out_shape = tuple(out_shape) + tuple(jax.ShapeDtypeStruct((1, n), F32) for _ in range(row_sums))
    out_spec = tuple(out_spec) + tuple(row_spec for _ in range(row_sums))

    n_after = len(after)

    def body(*refs):
        a_ref, b_ref = refs[0], refs[1]
        ex = refs[2:2 + n_ex]
        outs = refs[2 + n_ex + n_after:2 + n_ex + n_after + n_out + row_sums]
        first_row_tile = pl.program_id(0) == 0

        def finish(acc):
            res = epilogue(acc, *[e[...] for e in ex])
            for o, r in zip(outs[:n_out], res[:n_out]):
                o[...] = r.astype(o.dtype)
            if row_sums:
                @pl.when(first_row_tile)
                def _():
                    for o in outs[n_out:]:
                        o[...] = jnp.zeros_like(o)

                for o, r in zip(outs[n_out:], res[n_out:]):
                    o[...] += r

        if gk == 1:
            finish(_dot(a_ref[...], b_ref[...], mode))
        else:
            acc_ref = refs[-1]
            kk = pl.program_id(2)

            @pl.when(kk == 0)
            def _():
                acc_ref[...] = jnp.zeros_like(acc_ref)

            acc_ref[...] += _dot(a_ref[...], b_ref[...], mode)

            @pl.when(kk == gk - 1)
            def _():
                finish(acc_ref[...])

    return pl.pallas_call(
        body, name=name, grid=(gi, gj, gk),
        in_specs=([a_spec, b_spec] + [mn_spec] * len(extras) + [row_spec] * len(rows)
                  + [pl.BlockSpec(memory_space=pl.ANY)] * n_after),
        out_specs=out_spec, out_shape=out_shape,
        scratch_shapes=[pltpu.VMEM((tm, tn), F32)] if gk > 1 else [],
        compiler_params=_params(("arbitrary" if row_sums else "parallel", "parallel", "arbitrary")),
    )(a, b, *extras, *rows, *after)


def _rms_rows(xv, g):
    return xv * lax.rsqrt(jnp.mean(xv * xv, axis=1, keepdims=True) + EPS) * g


def _rms_rows_bwd(xv, g, dyv):
    r = lax.rsqrt(jnp.mean(xv * xv, axis=1, keepdims=True) + EPS)
    u = dyv * g
    return (r * u - xv * (r * r * r) * jnp.mean(u * xv, axis=1, keepdims=True),
            jnp.sum(dyv * xv * r, axis=0, keepdims=True))


def _residual_rms(acc, res, g):
    h = acc + res
    return h, _rms_rows(h, g)


def _rms_bwd_residual(dhn, xv, dres, g):
    dx, dg = _rms_rows_bwd(xv, g, dhn)
    dx = dx + dres
    return dx, dx, dg


def _rms_fwd(x, g, *, name, tm=512):
    t, d = x.shape
    tm = min(tm, t)

    def body(x_ref, g_ref, o_ref):
        o_ref[...] = _rms_rows(x_ref[...], g_ref[...]).astype(o_ref.dtype)

    return pl.pallas_call(
        body, name=name, grid=(t // tm,),
        in_specs=[pl.BlockSpec((tm, d), lambda i: (i, 0)), pl.BlockSpec((1, d), lambda i: (0, 0))],
        out_specs=pl.BlockSpec((tm, d), lambda i: (i, 0)),
        out_shape=jax.ShapeDtypeStruct((t, d), _MXU_DTYPE),
        compiler_params=_params(("parallel",)),
    )(x, g)


def _rms_gain_grad(x, g, dy, *, name, tm=512):
    t, d = x.shape
    tm = min(tm, t)

    def body(x_ref, g_ref, dy_ref, dg_ref):
        @pl.when(pl.program_id(0) == 0)
        def _():
            dg_ref[...] = jnp.zeros_like(dg_ref)

        dg_ref[...] += _rms_rows_bwd(x_ref[...], g_ref[...], dy_ref[...])[1]

    row = pl.BlockSpec((tm, d), lambda i: (i, 0))
    vec = pl.BlockSpec((1, d), lambda i: (0, 0))
    return pl.pallas_call(
        body, name=name, grid=(t // tm,), in_specs=[row, vec, row], out_specs=vec,
        out_shape=jax.ShapeDtypeStruct((1, d), F32), compiler_params=_params(("arbitrary",)),
    )(x, g, dy)


def _hg_constants():
    c = HG_CHUNK
    t = np.arange(c)
    sums = [t[None, :] <= t[:, None]]
    masks = []
    for m in HG_LEVELS:
        base = (t // (2 * m)) * (2 * m)
        mid = base + m - 1
        second = (t - base) >= m
        upper = (t[None, :] > mid[:, None]) & (t[None, :] <= t[:, None])
        lower = (t[None, :] > t[:, None]) & (t[None, :] <= mid[:, None])
        sums.append(np.where(second[:, None], upper, lower))
        masks.append(second[:, None] & (~second)[None, :] & (base[:, None] == base[None, :]))
    return (np.concatenate(sums, axis=0).astype(np.float32), np.stack(masks).astype(np.float32))


HG_HEAD_LANES = tuple(slice(HG_D * h, HG_D * (h + 1)) for h in range(HG_HEADS))


def _per_head(fn, slab):
    return jnp.concatenate([jnp.broadcast_to(fn(slab[:, hs]), (slab.shape[0], HG_D)) for hs in HG_HEAD_LANES], axis=1)


def _lane_sum(v):
    return jnp.sum(v, axis=1, keepdims=True)


def _lane_mean(v):
    return jnp.mean(v, axis=1, keepdims=True)


def _hg_gates(blk, lbp):
    w = HG_HEADS * HG_D
    q, x, v, gl = blk[:, 0:w], blk[:, w:2 * w], blk[:, 2 * w:3 * w], blk[:, 3 * w:4 * w]
    mx = jnp.max(lbp, axis=0, keepdims=True)
    e = jnp.exp(lbp - mx)
    lb = e[0:1, :] / jnp.sum(e, axis=0, keepdims=True)
    sig = jax.nn.sigmoid(x)
    f = lb + (1.0 - lb) * sig
    return q, v, gl, lb, sig, f, 1.0 - f, jnp.log(f)


def _hg_fwd(proj, lbp, ng, bsz, seq, *, y_width):
    t = proj.shape[0]
    nc = seq // HG_CHUNK
    a_np, m_np = _hg_constants()
    a_all = jnp.asarray(a_np, _MXU_DTYPE)
    masks = jnp.asarray(m_np, F32)
    nl = len(HG_LEVELS)

    ts = min(HG_TILE, seq)
    ns, nct = seq // ts, ts // HG_CHUNK
    hw = HG_HEADS * HG_D

    def body(p_ref, lb_ref, ng_ref, a_ref, m_ref, y_ref, o_ref, st_ref, carry):
        a_mat = a_ref[...]
        ngv = ng_ref[...]

        @pl.when(pl.program_id(1) == 0)
        def _():
            carry[...] = jnp.zeros_like(carry)

        ng4 = _tile_lanes(ngv, HG_HEADS)
        heads = range(HG_HEADS)
        hl = HG_HEAD_LANES

        def chunk(c, _):
            rows = pl.ds(pl.multiple_of(c * HG_CHUNK, HG_CHUNK), HG_CHUNK)
            q, v, gl, lb, sig, f, k, g = _hg_gates(p_ref[rows, :], lb_ref[...])
            sts = [carry[h] for h in heads]
            e_all = _split_dot(a_mat, g, NN, 3)
            b = e_all[0:HG_CHUNK]
            qb = q * jnp.exp(b)
            o = [_dot(qb[:, hl[h]], sts[h], NT) for h in heads]
            p = [jnp.zeros((HG_CHUNK, HG_CHUNK), F32) for _ in heads]
            for li in range(nl):
                e = jnp.exp(e_all[HG_CHUNK * (li + 1):HG_CHUNK * (li + 2)])
                qm, km, mk = q * e, k * e, m_ref[li]
                p = [p[h] + mk * _dot(qm[:, hl[h]], km[:, hl[h]], NT) for h in heads]
            bl = b[HG_CHUNK - 1:HG_CHUNK, :]
            kd = k * jnp.exp(bl - b)
            ebl = jnp.exp(bl)
            pv = [_dot(p[h], v[:, hl[h]]) for h in heads]
            upd = [_dot(v[:, hl[h]], kd[:, hl[h]], TN) for h in heads]
            o_all = jnp.concatenate([o[h] + pv[h] for h in heads], axis=1) + _per_head(_lane_sum, q * k) * v
            r = lax.rsqrt(_per_head(_lane_mean, o_all * o_all) + EPS)
            for h in heads:
                st_ref[h, c] = sts[h]
                carry[h] = sts[h] * ebl[:, hl[h]] + upd[h]
            o_ref[rows, :] = o_all
            y_ref[rows, :] = (o_all * r * ng4) * (gl * jax.nn.sigmoid(gl))
            return 0

        lax.fori_loop(0, nct, chunk, 0)

    return pl.pallas_call(
        body, name="hgrn2_fwd", grid=(bsz, ns),
        in_specs=[pl.BlockSpec((ts, HG_COLS), lambda b, s: (b * ns + s, 0)),
                  pl.BlockSpec((2, hw), lambda b, s: (0, 0)),
                  pl.BlockSpec((1, HG_D), lambda b, s: (0, 0)),
                  pl.BlockSpec(a_all.shape, lambda b, s: (0, 0)),
                  pl.BlockSpec(masks.shape, lambda b, s: (0, 0, 0))],
        out_specs=(pl.BlockSpec((ts, hw), lambda b, s: (b * ns + s, 0)),
                   pl.BlockSpec((ts, hw), lambda b, s: (b * ns + s, 0)),
                   pl.BlockSpec((None, HG_HEADS, nct, HG_D, HG_D), lambda b, s: (b, 0, s, 0, 0))),
        out_shape=(jax.ShapeDtypeStruct((t, y_width), F32),
                   jax.ShapeDtypeStruct((t, hw), F32),
                   jax.ShapeDtypeStruct((bsz, HG_HEADS, nc, HG_D, HG_D), F32)),
        scratch_shapes=[pltpu.VMEM((HG_HEADS, HG_D, HG_D), F32)],
        compiler_params=_params(("parallel", "arbitrary")),
    )(proj, lbp, ng, a_all, masks)


def _hg_bwd(proj, lbp, ng, o_all, states, dy, bsz, seq, after=()):
    after = tuple(a for a in after if a is not None)
    t = proj.shape[0]
    nc = seq // HG_CHUNK
    a_np, m_np = _hg_constants()
    a_all = jnp.asarray(a_np, _MXU_DTYPE)
    masks = jnp.asarray(m_np, F32)
    nl = len(HG_LEVELS)
    cs = HG_CHUNK

    ts = min(HG_TILE, seq)
    ns, nct = seq // ts, ts // cs
    hw = HG_HEADS * HG_D

    def body(p_ref, lb_ref, ng_ref, a_ref, m_ref, o_ref, st_ref, dy_ref, *rest):
        dp_ref, dlb_ref, dng_ref, dst_ref = rest[len(after):]
        a_mat = a_ref[...]
        ngv = ng_ref[...]
        ng4 = _tile_lanes(ngv, HG_HEADS)
        last_row = lax.broadcasted_iota(jnp.int32, (cs, hw), 0) == cs - 1
        si = pl.program_id(1)
        first = jnp.logical_and(pl.program_id(0) == 0, si == 0)
        heads = range(HG_HEADS)
        hl = HG_HEAD_LANES

        @pl.when(si == 0)
        def _():
            dst_ref[...] = jnp.zeros_like(dst_ref)

        def side_by_side(parts):
            return jnp.concatenate(parts, axis=1)

        def chunk(i, carry):
            dlb_acc, dng_acc = carry
            c = nct - 1 - i
            rows = pl.ds(pl.multiple_of(c * cs, cs), cs)
            q, v, gl, lb, sig, f, k, g = _hg_gates(p_ref[rows, :], lb_ref[...])
            o = o_ref[rows, :]
            dyv = dy_ref[rows, :]
            sts = [st_ref[h, c] for h in heads]
            dsts = [dst_ref[h] for h in heads]
            e_all = _split_dot(a_mat, g, NN, 3)
            b = e_all[0:cs]
            eb = jnp.exp(b)
            bl = b[cs - 1:cs, :]
            ebl = jnp.exp(bl)
            ekd = jnp.exp(bl - b)
            qb, kd = q * eb, k * ekd
            sg = jax.nn.sigmoid(gl)
            silu = gl * sg
            r = lax.rsqrt(_per_head(_lane_mean, o * o) + EPS)
            dgl = dyv * (o * r * ng4) * (sg * (1.0 + gl * (1.0 - sg)))
            u = dyv * silu * ng4
            do = r * u - o * (r * r * r) * _per_head(_lane_mean, u * o)
            dng4 = jnp.sum(dyv * silu * o * r, axis=0, keepdims=True)
            dng_acc = dng_acc + ((dng4[:, hl[0]] + dng4[:, hl[1]]) + (dng4[:, hl[2]] + dng4[:, hl[3]]))
            es, qm, km = [], [], []
            p = [jnp.zeros((cs, cs), F32) for _ in heads]
            for li in range(nl):
                e = jnp.exp(e_all[cs * (li + 1):cs * (li + 2)])
                es.append(e)
                qm.append(q * e)
                km.append(k * e)
                mk = m_ref[li]
                p = [p[h] + mk * _dot(qm[li][:, hl[h]], km[li][:, hl[h]], NT) for h in heads]
            dp = [_dot(do[:, hl[h]], v[:, hl[h]], NT) for h in heads]
            dv_p = [_dot(p[h], do[:, hl[h]], TN) for h in heads]
            dv_s = [_dot(kd[:, hl[h]], dsts[h], NT) for h in heads]
            dqb = side_by_side([_dot(do[:, hl[h]], sts[h]) for h in heads])
            dkd = side_by_side([_dot(v[:, hl[h]], dsts[h]) for h in heads])
            new_dst = [_dot(do[:, hl[h]], qb[:, hl[h]], TN) for h in heads]
            dv = side_by_side([dv_p[h] + dv_s[h] for h in heads]) + _per_head(_lane_sum, q * k) * do
            dq = dqb * eb
            dk = dkd * ekd
            db = dqb * qb - dkd * kd
            dbl = (jnp.sum(dkd * kd, axis=0, keepdims=True)
                   + side_by_side([jnp.sum(dsts[h] * sts[h], axis=0, keepdims=True) for h in heads]) * ebl)
            de = [db + jnp.where(last_row, dbl, 0.0)]
            for li in range(nl):
                mk = m_ref[li]
                dpm = [mk * dp[h] for h in heads]
                dqm = side_by_side([_dot(dpm[h], km[li][:, hl[h]]) for h in heads])
                dkm = side_by_side([_dot(dpm[h], qm[li][:, hl[h]], TN) for h in heads])
                dq = dq + dqm * es[li]
                dk = dk + dkm * es[li]
                de.append(dqm * qm[li] + dkm * km[li])
            dpd = _per_head(_lane_sum, do * v)
            dq = dq + dpd * k
            dk = dk + dpd * q
            dg = _split_dot(a_mat, jnp.concatenate(de, axis=0), TN, 2)
            df = dg / f - dk
            dp_ref[rows, 0:hw] = _mx(dq)
            dp_ref[rows, hw:2 * hw] = _mx(df * (1.0 - lb) * sig * (1.0 - sig))
            dp_ref[rows, 2 * hw:3 * hw] = _mx(dv)
            dp_ref[rows, 3 * hw:4 * hw] = _mx(dgl)
            for h in heads:
                dst_ref[h] = dsts[h] * ebl[:, hl[h]] + new_dst[h]
            return dlb_acc + jnp.sum(df * (1.0 - sig), axis=0, keepdims=True), dng_acc

        dlb, dng = lax.fori_loop(0, nct, chunk, (jnp.zeros((1, hw), F32), jnp.zeros((1, HG_D), F32)))

        @pl.when(first)
        def _():
            dlb_ref[...] = jnp.zeros_like(dlb_ref)
            dng_ref[...] = jnp.zeros_like(dng_ref)

        lbp_v = lb_ref[...]
        mx = jnp.max(lbp_v, axis=0, keepdims=True)
        e = jnp.exp(lbp_v - mx)
        s0 = e[0:1, :] / jnp.sum(e, axis=0, keepdims=True)
        da0 = dlb * s0 * (1.0 - s0)
        dlb_ref[...] += jnp.concatenate([da0, -da0], axis=0)
        dng_ref[...] += dng

    def tile(b, s):
        return b * ns + (ns - 1 - s)

    return pl.pallas_call(
        body, name="hgrn2_bwd", grid=(bsz, ns),
        in_specs=[pl.BlockSpec((ts, HG_COLS), lambda b, s: (tile(b, s), 0)),
                  pl.BlockSpec((2, hw), lambda b, s: (0, 0)),
                  pl.BlockSpec((1, HG_D), lambda b, s: (0, 0)),
                  pl.BlockSpec(a_all.shape, lambda b, s: (0, 0)),
                  pl.BlockSpec(masks.shape, lambda b, s: (0, 0, 0)),
                  pl.BlockSpec((ts, hw), lambda b, s: (tile(b, s), 0)),
                  pl.BlockSpec((None, HG_HEADS, nct, HG_D, HG_D), lambda b, s: (b, 0, ns - 1 - s, 0, 0)),
                  pl.BlockSpec((ts, hw), lambda b, s: (tile(b, s), 0))] + [pl.BlockSpec(memory_space=pl.ANY)] * len(after),
        out_specs=(pl.BlockSpec((ts, HG_COLS), lambda b, s: (tile(b, s), 0)),
                   pl.BlockSpec((2, hw), lambda b, s: (0, 0)),
                   pl.BlockSpec((1, HG_D), lambda b, s: (0, 0))),
        out_shape=(jax.ShapeDtypeStruct((t, HG_COLS), _MXU_DTYPE),
                   jax.ShapeDtypeStruct((2, hw), F32),
                   jax.ShapeDtypeStruct((1, HG_D), F32)),
        scratch_shapes=[pltpu.VMEM((HG_HEADS, HG_D, HG_D), F32)],
        compiler_params=_params(("arbitrary", "arbitrary")),
    )(proj, lbp, ng, a_all, masks, o_all, states, dy, *after)


def _sw_constants():
    half = ROT_DIM // 2
    inv = (np.float32(ROPE_THETA) ** (-(np.arange(half, dtype=np.float32) * np.float32(2.0) / np.float32(ROT_DIM)))
           ).astype(np.float32)
    freq = np.zeros((1, 128), np.float32)
    sign = np.zeros((1, 128), np.float32)
    for h in range(2):
        freq[0, 64 * h:64 * h + half] = inv
        freq[0, 64 * h + half:64 * h + 2 * half] = inv
        sign[0, 64 * h:64 * h + half] = -1.0
        sign[0, 64 * h + half:64 * h + 2 * half] = 1.0
    seg = np.kron(np.eye(8, dtype=np.float32), np.full((64, 64), 1.0 / 64.0, np.float32))
    return freq, sign, seg


def _rope_tables(pos, freq, sign):
    ang = pos.astype(F32) * freq
    return jnp.cos(ang), jnp.sin(ang) * sign


def _tile_lanes(v, times):
    return v if times == 1 else jnp.concatenate([v] * times, axis=1)


def _swap_halves(v):
    w = v.shape[1]
    half = ROT_DIM // 2
    lane = lax.broadcasted_iota(jnp.int32, v.shape, 1) % SW_HD
    return jnp.where(lane < half, pltpu.roll(v, w - half, 1), jnp.where(lane < 2 * half, pltpu.roll(v, half, 1), 0.0))


def _sw_norm_rope(tv, gain, seg, cosv, sinv):
    w = tv.shape[1]
    ms = _split_dot_rhs(tv * tv, seg[0:w, 0:w])
    r = lax.rsqrt(ms + EPS)
    tn = tv * r * gain
    reps = w // 128
    return tn * _tile_lanes(cosv, reps) + _swap_halves(tn) * _tile_lanes(sinv, reps), r


def _split_dot_rhs(v, a):
    hi = _mx(v)
    lo = _mx(v - hi.astype(F32))
    return (lax.dot_general(hi, a, (NN, ((), ())), preferred_element_type=F32)
            + lax.dot_general(lo, a, (NN, ((), ())), preferred_element_type=F32))


def _sw_norm_rope_bwd(dt, tv, r, gain, seg, cosv, sinv):
    w = tv.shape[1]
    reps = w // 128
    dtn = dt * _tile_lanes(cosv, reps) + _swap_halves(dt * _tile_lanes(sinv, reps))
    u = dtn * gain
    dtv = r * u - tv * (r * r * r) * _split_dot_rhs(u * tv, seg[0:w, 0:w])
    return dtv, jnp.sum(dtn * tv * r, axis=0, keepdims=True)


def _sw_scores(qh, kp, kc):
    return _dot(qh, kp, NT), _dot(qh, kc, NT)


def _sw_probs(raw, sink, first_block):
    scale = SW_HD ** -0.5
    qi = lax.broadcasted_iota(jnp.int32, (SW_BLOCK, SW_BLOCK), 0)
    kj = lax.broadcasted_iota(jnp.int32, (SW_BLOCK, SW_BLOCK), 1)
    ok_prev = jnp.logical_and(kj > qi, jnp.logical_not(first_block))
    ok_cur = kj <= qi
    sp = jnp.where(ok_prev, raw[0] * scale, -jnp.inf)
    sc = jnp.where(ok_cur, raw[1] * scale, -jnp.inf)
    m = jnp.maximum(jnp.maximum(jnp.max(sp, axis=1, keepdims=True), jnp.max(sc, axis=1, keepdims=True)), sink)
    pp, pc = jnp.exp(sp - m), jnp.exp(sc - m)
    es = jnp.exp(sink - m)
    den = jnp.sum(pp, axis=1, keepdims=True) + jnp.sum(pc, axis=1, keepdims=True) + es
    return pp / den, pc / den, es / den


def _sw_specs(nb):
    def cur(b, n):
        return b * nb + jnp.minimum(n, nb - 1)

    def prev(b, n):
        return b * nb + jnp.maximum(jnp.minimum(n, nb - 1) - 1, 0)

    return cur, prev


def _sw_fwd(proj, pos, qg, kg, sinks, y_in, bsz, seq):
    t = proj.shape[0]
    nb = seq // SW_BLOCK
    freq_np, sign_np, seg_np = _sw_constants()
    freq, sign = jnp.asarray(freq_np), jnp.asarray(sign_np)
    seg = jnp.asarray(seg_np, _MXU_DTYPE)
    cur, prev = _sw_specs(nb)

    def body(q_ref, kc_ref, kp_ref, vc_ref, vp_ref, pc_ref, pp_ref, qg_ref, kg_ref, sk_ref, fr_ref, sn_ref, seg_ref,
             yin_ref, y_ref):
        del yin_ref
        n = pl.program_id(1)
        segv = seg_ref[...]
        cos_c, sin_c = _rope_tables(pc_ref[...], fr_ref[...], sn_ref[...])
        cos_p, sin_p = _rope_tables(pp_ref[...], fr_ref[...], sn_ref[...])
        qr, _ = _sw_norm_rope(q_ref[...], qg_ref[...], segv, cos_c, sin_c)
        kcr, _ = _sw_norm_rope(kc_ref[...], kg_ref[...], segv, cos_c, sin_c)
        kpr, _ = _sw_norm_rope(kp_ref[...], kg_ref[...], segv, cos_p, sin_p)
        vc, vp = vc_ref[...], vp_ref[...]
        ks = [slice(SW_HD * (h // SW_GROUP), SW_HD * (h // SW_GROUP + 1)) for h in range(SW_HEADS)]
        raw = [_sw_scores(qr[:, SW_HD * h:SW_HD * (h + 1)], kpr[:, ks[h]], kcr[:, ks[h]]) for h in range(SW_HEADS)]
        probs = [_sw_probs(raw[h], sk_ref[0, h], n == 0) for h in range(SW_HEADS)]
        for h in range(SW_HEADS):
            y_ref[:, SW_HD * h:SW_HD * (h + 1)] = _dot(probs[h][0], vp[:, ks[h]]) + _dot(probs[h][1], vc[:, ks[h]])

    rowq = pl.BlockSpec((SW_BLOCK, 512), lambda b, n: (cur(b, n), 0))
    full = lambda a: pl.BlockSpec(a.shape, lambda b, n: (0,) * a.ndim)
    yw = y_in.shape[1]
    return pl.pallas_call(
        body, name="swa_fwd", grid=(bsz, nb),
        in_specs=[rowq,
                  pl.BlockSpec((SW_BLOCK, 128), lambda b, n: (cur(b, n), 4)),
                  pl.BlockSpec((SW_BLOCK, 128), lambda b, n: (prev(b, n), 4)),
                  pl.BlockSpec((SW_BLOCK, 128), lambda b, n: (cur(b, n), 5)),
                  pl.BlockSpec((SW_BLOCK, 128), lambda b, n: (prev(b, n), 5)),
                  pl.BlockSpec((SW_BLOCK, 1), lambda b, n: (cur(b, n), 0)),
                  pl.BlockSpec((SW_BLOCK, 1), lambda b, n: (prev(b, n), 0)),
                  full(qg), full(kg),
                  pl.BlockSpec(memory_space=pltpu.SMEM),
                  full(freq), full(sign), full(seg),
                  pl.BlockSpec(memory_space=pl.ANY)],
        out_specs=pl.BlockSpec((SW_BLOCK, 512), lambda b, n: (cur(b, n), 1)),
        out_shape=jax.ShapeDtypeStruct((t, yw), F32),
        input_output_aliases={13: 0},
        compiler_params=_params(("parallel", "parallel")),
    )(proj, proj, proj, proj, proj, pos, pos, qg, kg, sinks, freq, sign, seg, y_in)


def _sw_bwd(proj, pos, qg, kg, sinks, y, dy, bsz, seq):
    t = proj.shape[0]
    nb = seq // SW_BLOCK
    freq_np, sign_np, seg_np = _sw_constants()
    freq, sign = jnp.asarray(freq_np), jnp.asarray(sign_np)
    seg = jnp.asarray(seg_np, _MXU_DTYPE)
    cur, prev = _sw_specs(nb)
    scale = SW_HD ** -0.5

    def body(q_ref, kc_ref, kp_ref, vc_ref, vp_ref, pc_ref, pp_ref, qg_ref, kg_ref, sk_ref, fr_ref, sn_ref, seg_ref,
             y_ref, dy_ref, dp_ref, dqg_ref, dkg_ref, dsk_ref,
             dq_car, dkv_car, dqr_s, dkc_s, dkp_s, dvc_s, dvp_s, gq_acc, gk_acc, sk_acc):
        b, n = pl.program_id(0), pl.program_id(1)
        first = jnp.logical_and(b == 0, n == 0)
        last = jnp.logical_and(b == pl.num_programs(0) - 1, n == nb)

        @pl.when(first)
        def _():
            gq_acc[...] = jnp.zeros_like(gq_acc)
            gk_acc[...] = jnp.zeros_like(gk_acc)
            sk_acc[...] = jnp.zeros_like(sk_acc)

        @pl.when(n < nb)
        def _():
            segv = seg_ref[...]
            cos_c, sin_c = _rope_tables(pc_ref[...], fr_ref[...], sn_ref[...])
            cos_p, sin_p = _rope_tables(pp_ref[...], fr_ref[...], sn_ref[...])
            qv, kcv, kpv = q_ref[...], kc_ref[...], kp_ref[...]
            qr, rq = _sw_norm_rope(qv, qg_ref[...], segv, cos_c, sin_c)
            kcr, rkc = _sw_norm_rope(kcv, kg_ref[...], segv, cos_c, sin_c)
            kpr, rkp = _sw_norm_rope(kpv, kg_ref[...], segv, cos_p, sin_p)
            vc, vp = vc_ref[...], vp_ref[...]
            lane = lax.broadcasted_iota(jnp.int32, (1, 128), 1)
            dsk = jnp.zeros((1, 128), F32)
            heads = range(SW_HEADS)
            ks = [slice(SW_HD * (h // SW_GROUP), SW_HD * (h // SW_GROUP + 1)) for h in heads]
            hs = [slice(SW_HD * h, SW_HD * (h + 1)) for h in heads]
            qh = [qr[:, hs[h]] for h in heads]
            doh = [dy_ref[:, hs[h]] for h in heads]
            raw = [_sw_scores(qh[h], kpr[:, ks[h]], kcr[:, ks[h]]) for h in heads]
            dpp = [_dot(doh[h], vp[:, ks[h]], NT) for h in heads]
            dpc = [_dot(doh[h], vc[:, ks[h]], NT) for h in heads]
            probs = [_sw_probs(raw[h], sk_ref[0, h], n == 0) for h in heads]
            dsp, dsc = [], []
            for h in heads:
                pp, pc, ps = probs[h]
                delta = jnp.sum(doh[h] * y_ref[:, hs[h]], axis=1, keepdims=True)
                dsp.append(pp * (dpp[h] - delta) * scale)
                dsc.append(pc * (dpc[h] - delta) * scale)
                dsk = dsk + jnp.where(lane == h, -jnp.sum(ps * delta), 0.0)
            for h in heads:
                dqr_s[:, hs[h]] = _dot(dsp[h], kpr[:, ks[h]]) + _dot(dsc[h], kcr[:, ks[h]])
            for kv in range(SW_KV_HEADS):
                group = range(SW_GROUP * kv, SW_GROUP * (kv + 1))
                kvs = slice(SW_HD * kv, SW_HD * (kv + 1))
                dvp_s[:, kvs] = sum(_dot(probs[h][0], doh[h], TN) for h in group)
                dvc_s[:, kvs] = sum(_dot(probs[h][1], doh[h], TN) for h in group)
                dkp_s[:, kvs] = sum(_dot(dsp[h], qh[h], TN) for h in group)
                dkc_s[:, kvs] = sum(_dot(dsc[h], qh[h], TN) for h in group)
            dq, gq = _sw_norm_rope_bwd(dqr_s[...], qv, rq, qg_ref[...], segv, cos_c, sin_c)
            dkc, gkc = _sw_norm_rope_bwd(dkc_s[...], kcv, rkc, kg_ref[...], segv, cos_c, sin_c)
            dkp, gkp = _sw_norm_rope_bwd(dkp_s[...], kpv, rkp, kg_ref[...], segv, cos_p, sin_p)
            gq_acc[...] += gq
            gk_acc[...] += gkc + gkp
            sk_acc[...] += dsk

            @pl.when(n > 0)
            def _():
                dp_ref[:, 0:512] = _mx(dq_car[...])
                dp_ref[:, 512:640] = _mx(dkv_car[:, 0:128] + dkp)
                dp_ref[:, 640:768] = _mx(dkv_car[:, 128:256] + dvp_s[...])

            dq_car[...] = dq
            dkv_car[:, 0:128] = dkc
            dkv_car[:, 128:256] = dvc_s[...]

        @pl.when(n == nb)
        def _():
            dp_ref[:, 0:512] = _mx(dq_car[...])
            dp_ref[:, 512:768] = _mx(dkv_car[...])

        @pl.when(last)
        def _():
            gq = gq_acc[...]
            acc = gq[:, 0:SW_HD]
            for h in range(1, SW_HEADS):
                acc = acc + gq[:, SW_HD * h:SW_HD * (h + 1)]
            dqg_ref[...] = acc
            gk = gk_acc[...]
            dkg_ref[...] = gk[:, 0:SW_HD] + gk[:, SW_HD:2 * SW_HD]
            dsk_ref[...] = sk_acc[...]

    rowq = pl.BlockSpec((SW_BLOCK, 512), lambda b, n: (cur(b, n), 0))
    full = lambda a: pl.BlockSpec(a.shape, lambda b, n: (0,) * a.ndim)

    def out_row(b, n):
        return b * nb + jnp.maximum(n - 1, 0)

    return pl.pallas_call(
        body, name="swa_bwd", grid=(bsz, nb + 1),
        in_specs=[rowq,
                  pl.BlockSpec((SW_BLOCK, 128), lambda b, n: (cur(b, n), 4)),
                  pl.BlockSpec((SW_BLOCK, 128), lambda b, n: (prev(b, n), 4)),
                  pl.BlockSpec((SW_BLOCK, 128), lambda b, n: (cur(b, n), 5)),
                  pl.BlockSpec((SW_BLOCK, 128), lambda b, n: (prev(b, n), 5)),
                  pl.BlockSpec((SW_BLOCK, 1), lambda b, n: (cur(b, n), 0)),
                  pl.BlockSpec((SW_BLOCK, 1), lambda b, n: (prev(b, n), 0)),
                  full(qg), full(kg),
                  pl.BlockSpec(memory_space=pltpu.SMEM),
                  full(freq), full(sign), full(seg),
                  pl.BlockSpec((SW_BLOCK, 512), lambda b, n: (cur(b, n), 1)),
                  pl.BlockSpec((SW_BLOCK, 512), lambda b, n: (cur(b, n), 1))],
        out_specs=(pl.BlockSpec((SW_BLOCK, SW_COLS), lambda b, n: (out_row(b, n), 0)),
                   pl.BlockSpec((1, SW_HD), lambda b, n: (0, 0)),
                   pl.BlockSpec((1, SW_HD), lambda b, n: (0, 0)),
                   pl.BlockSpec((1, 128), lambda b, n: (0, 0))),
        out_shape=(jax.ShapeDtypeStruct((t, SW_COLS), _MXU_DTYPE),
                   jax.ShapeDtypeStruct((1, SW_HD), F32),
                   jax.ShapeDtypeStruct((1, SW_HD), F32),
                   jax.ShapeDtypeStruct((1, 128), F32)),
        scratch_shapes=[pltpu.VMEM((SW_BLOCK, 512), F32), pltpu.VMEM((SW_BLOCK, 256), F32),
                        pltpu.VMEM((SW_BLOCK, 512), F32),
                        pltpu.VMEM((SW_BLOCK, 128), F32), pltpu.VMEM((SW_BLOCK, 128), F32),
                        pltpu.VMEM((SW_BLOCK, 128), F32), pltpu.VMEM((SW_BLOCK, 128), F32),
                        pltpu.VMEM((1, 512), F32), pltpu.VMEM((1, 128), F32), pltpu.VMEM((1, 128), F32)],
        compiler_params=_params(("arbitrary", "arbitrary")),
    )(proj, proj, proj, proj, proj, pos, pos, qg, kg, sinks, freq, sign, seg, y, dy)


def _head_rms(tv, gain):
    r = lax.rsqrt(jnp.mean(tv * tv, axis=1, keepdims=True) + EPS)
    return tv * r * gain, r


def _head_rms_bwd(dtn, tv, r, gain):
    u = dtn * gain
    return r * u - tv * (r * r * r) * jnp.mean(u * tv, axis=1, keepdims=True), jnp.sum(dtn * tv * r, axis=0, keepdims=True)


def _xa_softmax(raw):
    s = raw * (XA_HD ** -0.5)
    e = jnp.exp(s - jnp.max(s, axis=1, keepdims=True))
    return e / jnp.sum(e, axis=1, keepdims=True)


def _xa_fwd(qx, kvx, qg, kg, bsz, seq, mlen, *, tq=512):
    t = qx.shape[0]
    tq = min(tq, seq)
    nq = seq // tq
    w = XA_HEADS * XA_HD

    def body(q_ref, kv_ref, qg_ref, kg_ref, o_ref):
        heads = range(XA_HEADS)
        hs = [slice(XA_HD * h, XA_HD * (h + 1)) for h in heads]
        qn = [_head_rms(q_ref[:, hs[h]], qg_ref[...])[0] for h in heads]
        kn = [_head_rms(kv_ref[:, hs[h]], kg_ref[...])[0] for h in heads]
        raw = [_dot(qn[h], kn[h], NT) for h in heads]
        p = [_xa_softmax(raw[h]) for h in heads]
        for h in heads:
            o_ref[:, hs[h]] = _dot(p[h], kv_ref[:, w + XA_HD * h:w + XA_HD * (h + 1)]).astype(o_ref.dtype)

    vec = pl.BlockSpec((1, XA_HD), lambda b, i: (0, 0))
    return pl.pallas_call(
        body, name="xattn_fwd", grid=(bsz, nq),
        in_specs=[pl.BlockSpec((tq, w), lambda b, i: (b * nq + i, 0)),
                  pl.BlockSpec((mlen, 2 * w), lambda b, i: (b, 0)), vec, vec],
        out_specs=pl.BlockSpec((tq, w), lambda b, i: (b * nq + i, 0)),
        out_shape=jax.ShapeDtypeStruct((t, w), _MXU_DTYPE),
        compiler_params=_params(("parallel", "parallel")),
    )(qx, kvx, qg, kg)


def _xa_bwd(qx, kvx, qg, kg, do, bsz, seq, mlen, *, tq=512):
    t = qx.shape[0]
    tq = min(tq, seq)
    nq = seq // tq
    w = XA_HEADS * XA_HD
    scale = XA_HD ** -0.5

    def body(q_ref, kv_ref, qg_ref, kg_ref, do_ref, dq_ref, dkv_ref, dqg_ref, dkg_ref):
        b, i = pl.program_id(0), pl.program_id(1)

        @pl.when(jnp.logical_and(b == 0, i == 0))
        def _():
            dqg_ref[...] = jnp.zeros_like(dqg_ref)
            dkg_ref[...] = jnp.zeros_like(dkg_ref)

        @pl.when(i == 0)
        def _():
            dkv_ref[...] = jnp.zeros_like(dkv_ref)

        heads = range(XA_HEADS)
        hs = [slice(XA_HD * h, XA_HD * (h + 1)) for h in heads]
        vs = [slice(w + XA_HD * h, w + XA_HD * (h + 1)) for h in heads]
        qv = [q_ref[:, hs[h]] for h in heads]
        kv = [kv_ref[:, hs[h]] for h in heads]
        doh = [do_ref[:, hs[h]] for h in heads]
        qn = [_head_rms(qv[h], qg_ref[...]) for h in heads]
        kn = [_head_rms(kv[h], kg_ref[...]) for h in heads]
        raw = [_dot(qn[h][0], kn[h][0], NT) for h in heads]
        dp = [_dot(doh[h], kv_ref[:, vs[h]], NT) for h in heads]
        p = [_xa_softmax(raw[h]) for h in heads]
        ds = [p[h] * (dp[h] - jnp.sum(p[h] * dp[h], axis=1, keepdims=True)) * scale for h in heads]
        dqn = [_dot(ds[h], kn[h][0]) for h in heads]
        dkn = [_dot(ds[h], qn[h][0], TN) for h in heads]
        dvv = [_dot(p[h], doh[h], TN) for h in heads]
        gq_sum = jnp.zeros((1, XA_HD), F32)
        gk_sum = jnp.zeros((1, XA_HD), F32)
        for h in heads:
            dqv, gq = _head_rms_bwd(dqn[h], qv[h], qn[h][1], qg_ref[...])
            dkv, gk = _head_rms_bwd(dkn[h], kv[h], kn[h][1], kg_ref[...])
            dq_ref[:, hs[h]] = dqv.astype(dq_ref.dtype)
            dkv_ref[:, hs[h]] += dkv
            dkv_ref[:, vs[h]] += dvv[h]
            gq_sum = gq_sum + gq
            gk_sum = gk_sum + gk
        dqg_ref[...] += gq_sum
        dkg_ref[...] += gk_sum

    vec = pl.BlockSpec((1, XA_HD), lambda b, i: (0, 0))
    row = pl.BlockSpec((tq, w), lambda b, i: (b * nq + i, 0))
    mem = pl.BlockSpec((mlen, 2 * w), lambda b, i: (b, 0))
    return pl.pallas_call(
        body, name="xattn_bwd", grid=(bsz, nq),
        in_specs=[row, mem, vec, vec, row],
        out_specs=(row, mem, vec, vec),
        out_shape=(jax.ShapeDtypeStruct((t, w), _MXU_DTYPE), jax.ShapeDtypeStruct((bsz * mlen, 2 * w), F32),
                   jax.ShapeDtypeStruct((1, XA_HD), F32), jax.ShapeDtypeStruct((1, XA_HD), F32)),
        compiler_params=_params(("arbitrary", "arbitrary")),
    )(qx, kvx, qg, kg, do)


def _loss_finish(sq_row, d_model):
    def body(s_ref, o_ref):
        o_ref[...] = jnp.zeros_like(o_ref) + 0.5 * jnp.sum(s_ref[...]) / float(d_model)

    return pl.pallas_call(body, name="loss_finish", out_shape=jax.ShapeDtypeStruct((1, 128), F32))(sq_row)


def _adamw_math(w, g, m, v):
    m = ADAM_B1 * m + (1.0 - ADAM_B1) * g
    v = ADAM_B2 * v + (1.0 - ADAM_B2) * (g * g)
    m_hat = m / (1.0 - ADAM_B1 ** ADAM_STEP)
    v_hat = v / (1.0 - ADAM_B2 ** ADAM_STEP)
    return -ADAM_LR * (m_hat / (jnp.sqrt(v_hat) + ADAM_EPS) + ADAM_WD * w), m, v


def _adamw_big(w, g, m, v, *, name, tr=256):
    r, c = w.shape
    tr = min(tr, r)

    def body(w_ref, g_ref, m_ref, v_ref, d_ref, mo_ref, vo_ref):
        d, mn, vn = _adamw_math(w_ref[...], g_ref[...], m_ref[...], v_ref[...])
        d_ref[...] = d
        mo_ref[...] = mn
        vo_ref[...] = vn

    spec = pl.BlockSpec((tr, c), lambda i: (i, 0))
    shp = jax.ShapeDtypeStruct((r, c), F32)
    return pl.pallas_call(
        body, name=name, grid=(r // tr,), in_specs=[spec] * 4, out_specs=(spec,) * 3, out_shape=(shp,) * 3,
        compiler_params=_params(("parallel",)),
    )(w, g, m, v)


def _adamw_small(ws, gs, ms, vs):
    n = len(ws)

    def body(*refs):
        for i in range(n):
            d, mn, vn = _adamw_math(refs[i][...], refs[n + i][...], refs[2 * n + i][...], refs[3 * n + i][...])
            refs[4 * n + i][...] = d
            refs[5 * n + i][...] = mn
            refs[6 * n + i][...] = vn

    shapes = tuple(jax.ShapeDtypeStruct(w.shape, F32) for w in ws)
    return pl.pallas_call(body, name="adamw_small", out_shape=shapes * 3)(*ws, *gs, *ms, *vs)


def _add_halves(g, recv, c_idx, *, name, tr=256):
    _, r, c = g.shape
    h = r // 2
    tr = min(tr, h)
    nt = h // tr

    def body(c_ref, g_ref, r_ref, o_ref):
        del c_ref
        o_ref[...] = g_ref[...] + r_ref[...]

    return pl.pallas_call(
        body, name=name,
        grid_spec=pltpu.PrefetchScalarGridSpec(
            num_scalar_prefetch=1, grid=(4, nt),
            in_specs=[pl.BlockSpec((None, tr, c), lambda k, i, cr: (k, cr[0] * nt + i, 0)),
                      pl.BlockSpec((None, tr, c), lambda k, i, cr: (k, i, 0))],
            out_specs=pl.BlockSpec((None, tr, c), lambda k, i, cr: (k, i, 0))),
        out_shape=jax.ShapeDtypeStruct((4, h, c), F32),
        compiler_params=_params(("parallel", "parallel")),
    )(c_idx, g, recv)


def _add_chips(p, recv, place_idx, *, name, tr=256):
    _, h, c = p.shape
    tr = min(tr, h)
    nt = h // tr

    def body(pi_ref, p_ref, r_ref, o_ref):
        del pi_ref
        o_ref[...] = ((p_ref[...] + r_ref[0]) + r_ref[1]) + r_ref[2]

    return pl.pallas_call(
        body, name=name,
        grid_spec=pltpu.PrefetchScalarGridSpec(
            num_scalar_prefetch=1, grid=(nt,),
            in_specs=[pl.BlockSpec((None, tr, c), lambda i, pi: (pi[0], i, 0)),
                      pl.BlockSpec((3, tr, c), lambda i, pi: (0, i, 0))],
            out_specs=pl.BlockSpec((tr, c), lambda i, pi: (pi[1] * nt + i, 0))),
        out_shape=jax.ShapeDtypeStruct((2 * h, c), F32),
        compiler_params=_params(("parallel",)),
    )(place_idx, p, recv)


def _place_shard(shard, place_idx, *, name, tr=256, after=()):
    r, c = shard.shape
    tr = min(tr, r)

    def body(pi_ref, s_ref, *rest):
        del pi_ref
        rest[-1][...] = s_ref[...]

    return pl.pallas_call(
        body, name=name,
        grid_spec=pltpu.PrefetchScalarGridSpec(
            num_scalar_prefetch=1, grid=(r // tr,),
            in_specs=[pl.BlockSpec((tr, c), lambda i, pi: (i, 0))] + [pl.BlockSpec(memory_space=pl.ANY)] * len(after),
            out_specs=pl.BlockSpec((None, tr, c), lambda i, pi: (pi[0], i, 0))),
        out_shape=jax.ShapeDtypeStruct((4, r, c), shard.dtype),
        compiler_params=_params(("parallel",)),
    )(place_idx, shard, *after)


def _place():
    x, y, c = lax.axis_index("x"), lax.axis_index("y"), lax.axis_index("c")
    chips = [(1 - x, y), (x, 1 - y), (1 - x, 1 - y)]
    return x, y, c, chips


ANY = pl.BlockSpec(memory_space=pl.ANY)


def _exchange_halves(grads, name):
    n = len(grads)

    def body(*refs):
        ins, outs = refs[:n], refs[n:2 * n]
        send_sems, recv_sems = refs[2 * n:]
        x, y, c, _ = _place()

        def copy(a):
            h = ins[a].shape[1] // 2
            return pltpu.make_async_remote_copy(
                src_ref=ins[a].at[:, pl.ds((1 - c) * h, h), :], dst_ref=outs[a],
                send_sem=send_sems.at[a], recv_sem=recv_sems.at[a], device_id=(x, y, 1 - c), device_id_type=MESH)

        for a in range(n):
            copy(a).start()
        for a in range(n):
            copy(a).wait_recv()
        for a in range(n):
            copy(a).wait_send()

    return pl.pallas_call(
        body, name=name,
        in_specs=[ANY] * n, out_specs=tuple([ANY] * n),
        out_shape=tuple(jax.ShapeDtypeStruct((4, g.shape[1] // 2, g.shape[2]), g.dtype) for g in grads),
        scratch_shapes=[pltpu.SemaphoreType.DMA((n,)), pltpu.SemaphoreType.DMA((n,))],
    )(*grads)


HBM = pl.BlockSpec(memory_space=pltpu.HBM)
SEM = pl.BlockSpec(memory_space=pltpu.SEMAPHORE)
EFFECT = pltpu.SideEffectType.DATAFLOW_SIDE_EFFECTING


def _in_hbm(a):
    return pltpu.with_memory_space_constraint(a, pltpu.HBM)


def _split_copy_calls(name, srcs, lands, n_copies, make_copies):
    ns, nl = len(srcs), len(lands)
    nb = ns + nl

    def start(after=()):
        n_after = len(after)

        def body(*refs):
            outs = refs[nb + n_after:]
            copies = make_copies(refs[:ns], refs[ns:nb], outs[0], outs[1])
            for cp in copies:
                cp.start()
            token = refs[-1]
            token[...] = jnp.zeros_like(token)

        bufs = [_in_hbm(a) for a in list(srcs) + list(lands)]
        out = pl.pallas_call(
            body, name=name + "_start",
            out_shape=(pltpu.SemaphoreType.DMA((n_copies,)), pltpu.SemaphoreType.DMA((n_copies,)),
                       *[pltpu.HBM(a.shape, a.dtype) for a in bufs], jax.ShapeDtypeStruct((8, 128), F32)),
            in_specs=[HBM] * nb + [pl.BlockSpec(memory_space=pl.ANY)] * n_after,
            out_specs=(SEM, SEM, *[HBM] * nb, pl.BlockSpec(memory_space=pltpu.VMEM)),
            input_output_aliases={i: 2 + i for i in range(nb)},
            compiler_params=pltpu.CompilerParams(has_side_effects=EFFECT),
        )(*bufs, *after)
        return dict(send=out[0], recv=out[1], bufs=list(out[2:2 + nb]), token=out[-1])

    def wait(state, after):
        def body(*refs):
            copies = make_copies(refs[:ns], refs[ns:nb], refs[nb], refs[nb + 1])
            for cp in copies:
                cp.wait_send()
            for cp in copies:
                cp.wait_recv()

        bufs = state["bufs"]
        out = pl.pallas_call(
            body, name=name + "_wait",
            out_shape=tuple(pltpu.HBM(a.shape, a.dtype) for a in bufs),
            in_specs=[HBM] * nb + [SEM, SEM] + [pl.BlockSpec(memory_space=pl.ANY)] * len(after),
            out_specs=tuple([HBM] * nb),
            input_output_aliases={i: i for i in range(nb)},
            compiler_params=pltpu.CompilerParams(has_side_effects=EFFECT),
        )(*bufs, state["send"], state["recv"], *after)
        return list(out[:ns]), list(out[ns:])

    return start, wait


def _scatter_chips_split(name, parts):
    n = len(parts)
    lands = [lax.empty((3,) + p.shape[1:], p.dtype) for p in parts]

    def make_copies(srcs, lnds, send_sems, recv_sems):
        _, _, c, chips = _place()
        return [pltpu.make_async_remote_copy(
            src_ref=srcs[a].at[2 * px + py], dst_ref=lnds[a].at[j], send_sem=send_sems.at[a * 3 + j],
            recv_sem=recv_sems.at[a * 3 + j], device_id=(px, py, c), device_id_type=MESH)
            for a in range(n) for j, (px, py) in enumerate(chips)]

    return _split_copy_calls(name, parts, lands, 3 * n, make_copies)


def _exchange_halves_split(name, grads):
    n = len(grads)
    lands = [lax.empty((4, g.shape[1] // 2, g.shape[2]), g.dtype) for g in grads]

    def make_copies(srcs, lnds, send_sems, recv_sems):
        x, y, c, _ = _place()
        out = []
        for a in range(n):
            h = srcs[a].shape[1] // 2
            out.append(pltpu.make_async_remote_copy(
                src_ref=srcs[a].at[:, pl.ds((1 - c) * h, h), :], dst_ref=lnds[a], send_sem=send_sems.at[a],
                recv_sem=recv_sems.at[a], device_id=(x, y, 1 - c), device_id_type=MESH))
        return out

    return _split_copy_calls(name, grads, lands, n, make_copies)


def _gather_chips_split(name, shards, lands):
    n = len(shards)

    def make_copies(srcs, lnds, send_sems, recv_sems):
        x, y, c, chips = _place()
        out = []
        for a in range(n):
            h = srcs[a].shape[0] // 2
            for j, (px, py) in enumerate(chips):
                out.append(pltpu.make_async_remote_copy(
                    src_ref=srcs[a].at[pl.ds(c * h, h), :], dst_ref=lnds[a].at[2 * x + y, pl.ds(c * h, h), :],
                    send_sem=send_sems.at[a * 3 + j], recv_sem=recv_sems.at[a * 3 + j],
                    device_id=(px, py, c), device_id_type=MESH))
        return out

    return _split_copy_calls(name, shards, lands, 3 * n, make_copies)


def _gather_finish(gathered, name):
    n = len(gathered)

    def body(*refs):
        outs = refs[n:2 * n]
        send_sems, recv_sems = refs[2 * n:]
        x, y, c, chips = _place()

        def copy(a, j, chip_idx, which):
            h = outs[a].shape[1] // 2
            rows = outs[a].at[chip_idx, pl.ds(which * h, h), :]
            return pltpu.make_async_remote_copy(
                src_ref=rows, dst_ref=rows, send_sem=send_sems.at[a * 3 + j], recv_sem=recv_sems.at[a * 3 + j],
                device_id=(x, y, 1 - c), device_id_type=MESH)

        for a in range(n):
            for j, (px, py) in enumerate(chips):
                copy(a, j, 2 * px + py, c).start()
        for a in range(n):
            for j, (px, py) in enumerate(chips):
                copy(a, j, 2 * px + py, 1 - c).wait_recv()
        for a in range(n):
            for j, (px, py) in enumerate(chips):
                copy(a, j, 2 * px + py, c).wait_send()

    return pl.pallas_call(
        body, name=name,
        in_specs=[ANY] * n, out_specs=tuple([ANY] * n),
        out_shape=tuple(jax.ShapeDtypeStruct(g.shape, g.dtype) for g in gathered),
        input_output_aliases={i: i for i in range(n)},
        scratch_shapes=[pltpu.SemaphoreType.DMA((3 * n,)), pltpu.SemaphoreType.DMA((3 * n,))],
    )(*gathered)


def _join_halves(fulls):
    n = len(fulls)

    def body(*refs):
        outs = refs[n:2 * n]
        send_sems, recv_sems = refs[2 * n:]
        x, y, c, _ = _place()

        def copy(a, which):
            h = outs[a].shape[0] // 2
            rows = outs[a].at[pl.ds(which * h, h), :]
            return pltpu.make_async_remote_copy(
                src_ref=rows, dst_ref=rows, send_sem=send_sems.at[a], recv_sem=recv_sems.at[a],
                device_id=(x, y, 1 - c), device_id_type=MESH)

        for a in range(n):
            copy(a, c).start()
        for a in range(n):
            copy(a, 1 - c).wait_recv()
        for a in range(n):
            copy(a, c).wait_send()

    return pl.pallas_call(
        body, name="rs_join_halves",
        in_specs=[ANY] * n, out_specs=tuple([ANY] * n),
        out_shape=tuple(jax.ShapeDtypeStruct(p.shape, p.dtype) for p in fulls),
        input_output_aliases={i: i for i in range(n)},
        scratch_shapes=[pltpu.SemaphoreType.DMA((n,)), pltpu.SemaphoreType.DMA((n,))],
    )(*fulls)


def _all_reduce_small(sm):
    r, w = sm.shape

    def body(sm_ref, o_ref, buf, send_sems, recv_sems):
        x, y, c, _ = _place()
        me = 4 * x + 2 * y + c
        buf[me] = sm_ref[...]
        rel = [(dx, dy, dc) for dx in (0, 1) for dy in (0, 1) for dc in (0, 1)][1:]

        def copy(k, slot, to):
            return pltpu.make_async_remote_copy(
                src_ref=sm_ref, dst_ref=buf.at[slot], send_sem=send_sems.at[k], recv_sem=recv_sems.at[k],
                device_id=to, device_id_type=MESH)

        peers = []
        for k, (dx, dy, dc) in enumerate(rel):
            px = 1 - x if dx else x
            py = 1 - y if dy else y
            pc = 1 - c if dc else c
            peers.append((px, py, pc))
            copy(k, me, (px, py, pc)).start()
        for k, (px, py, pc) in enumerate(peers):
            copy(k, 4 * px + 2 * py + pc, (px, py, pc)).wait_recv()
        for k, (px, py, pc) in enumerate(peers):
            copy(k, me, (px, py, pc)).wait_send()
        acc = buf[0]
        for d in range(1, 8):
            acc = acc + buf[d]
        o_ref[...] = acc

    vm = pl.BlockSpec(memory_space=pltpu.VMEM)
    return pl.pallas_call(
        body, name="all_reduce_small", in_specs=[vm], out_specs=vm,
        out_shape=jax.ShapeDtypeStruct((r, w), F32),
        scratch_shapes=[pltpu.VMEM((8, r, w), F32), pltpu.SemaphoreType.DMA((7,)), pltpu.SemaphoreType.DMA((7,))],
    )(sm)


class _LocalWeights:
    def __init__(self, w):
        self.w = w
        self.g = {}

    def begin(self):
        pass

    def first(self, after):
        del after
        return self.w

    def rest(self, after):
        del after
        return self.w

    def grads(self, tag, g):
        del tag
        self.g.update(g)
        return ()

    def poll(self, after):
        del after
        return ()


def _local_step(x3, mem3, pos2, target3, small, comm):
    bsz, seq, d = x3.shape
    mlen = mem3.shape[1]
    t = bsz * seq
    comm.begin()
    x = x3.reshape(t, d)
    mem = mem3.reshape(bsz * mlen, d)
    target = target3.reshape(t, d)
    pos = pos2.reshape(t, 1)
    qg_t = jnp.tile(small["sw_q_norm_g"], (1, SW_HEADS))
    kg_t = jnp.tile(small["sw_k_norm_g"], (1, SW_KV_HEADS))

    hn1 = _rms_fwd(x, small["norm1_g"], name="rms1_fwd")
    w = comm.first(hn1)
    proj_hg = _mm(hn1, w["w_in_hg"], NN, t, HG_COLS, d, name="proj_hg", tk=d, after=(w.get("token"),))[0]
    proj_sw = _mm(hn1, w["w_in_sw"], NN, t, SW_COLS, d, name="proj_sw", tk=d)[0]
    y_mix, o_hg, states = _hg_fwd(proj_hg, small["hg_lower_bounds"], small["hg_norm_g"], bsz, seq, y_width=1024)
    y_mix = _sw_fwd(proj_sw, pos, qg_t, kg_t, small["sw_sinks"], y_mix, bsz, seq)
    w_in_hg, w_in_sw = w["w_in_hg"], w["w_in_sw"]
    w = comm.rest(y_mix)
    ff = w["down"].shape[0]
    ffs = ff // 4
    h1, hn2 = _mm(y_mix, w["w_out"], NN, t, d, 1024, name="out_proj", tk=1024, extras=(x,), rows=(small["norm2_g"],),
                  epilogue=_residual_rms, out_dtypes=(F32, _MXU_DTYPE))
    mn = _rms_fwd(mem, small["mem_norm_g"], name="rms_mem_fwd")
    qx = _mm(hn2, w["wq"], NN, t, 512, d, name="xa_q", tk=d)[0]
    kvx = _mm(mn, w["wkv"], NN, bsz * mlen, 1024, d, name="xa_kv", tk=d)[0]
    ox = _xa_fwd(qx, kvx, small["xa_q_norm_g"], small["xa_k_norm_g"], bsz, seq, mlen)
    h2, hn3 = _mm(ox, w["wo"], NN, t, d, 512, name="xa_o", tk=512, extras=(h1,), rows=(small["norm3_g"],),
                  epilogue=_residual_rms, out_dtypes=(F32, _MXU_DTYPE))

    def relu_sq(acc):
        a = jnp.maximum(acc, 0.0)
        return a, a * a

    act, act2 = _mm(hn3, w["up"], NN, t, ff, d, name="mlp_up", tm=2048, tn=ffs, tk=d,
                    b_spec=pl.BlockSpec((None, d, ffs), lambda i, j, kk: (j, 0, 0)),
                    epilogue=relu_sq, out_dtypes=(_MXU_DTYPE, _MXU_DTYPE))
    inv_d = 1.0 / d

    def loss_cotangent(acc, res, tgt):
        diff = acc + res - tgt
        v = diff * inv_d
        return v, v, jnp.sum(diff * diff, axis=0, keepdims=True)

    dy, dy_mx, sq_row = _mm(act2, w["down"], NN, t, d, ff, name="mlp_down", extras=(h2, target),
                            epilogue=loss_cotangent, out_dtypes=(F32, _MXU_DTYPE), row_sums=1)
    loss_row = _loss_finish(sq_row, d)

    dz = _mm(dy_mx, w["down"], NT, t, ff, d, name="d_act", tm=2048, tk=d, extras=(act,),
             epilogue=lambda acc, a: (acc * (2.0 * a.astype(F32)),), out_dtypes=(_MXU_DTYPE,))[0]
    g_down = _mm(act2, dy_mx, TN, ff, d, t, name="g_down")[0]
    g_up = _mm(hn3, dz, TN, d, ff, t, name="g_up", tn=ffs,
               out_shape=(jax.ShapeDtypeStruct((4, d, ffs), F32),),
               out_spec=(pl.BlockSpec((None, min(1024, d), ffs), lambda i, j, kk: (j, i, 0)),))[0]
    tok = comm.grads("mlp", dict(up=g_up, down=g_down))
    dh2, dh2_mx, g_norm3 = _mm(dz, w["up"], NT, t, d, ff, name="d_hn3", tk=ffs, after=tok,
                               b_spec=pl.BlockSpec((None, min(1024, d), ffs), lambda i, j, kk: (kk, j, 0)),
                               extras=(h2, dy), rows=(small["norm3_g"],), epilogue=_rms_bwd_residual,
                               out_dtypes=(F32, _MXU_DTYPE), row_sums=1)
    d_ox = _mm(dh2_mx, w["wo"], NT, t, 512, d, name="d_ox", tk=d)[0]
    g_wo = _mm(ox, dh2_mx, TN, 512, d, t, name="g_wo")[0]
    d_qx, d_kvx, g_xq, g_xk = _xa_bwd(qx, kvx, small["xa_q_norm_g"], small["xa_k_norm_g"], d_ox, bsz, seq, mlen)
    g_wq = _mm(hn2, d_qx, TN, d, 512, t, name="g_wq")[0]
    g_wkv = _mm(mn, d_kvx, TN, d, 1024, bsz * mlen, name="g_wkv")[0]
    dh1, dh1_mx, g_norm2 = _mm(d_qx, w["wq"], NT, t, d, 512, name="d_hn2", tk=512, extras=(h1, dh2),
                               rows=(small["norm2_g"],), epilogue=_rms_bwd_residual, out_dtypes=(F32, _MXU_DTYPE),
                               row_sums=1)
    dmn = _mm(d_kvx, w["wkv"], NT, bsz * mlen, d, 1024, name="d_mn", tk=1024)[0]
    g_memn = _rms_gain_grad(mem, small["mem_norm_g"], dmn, name="rms_mem_bwd")
    g_wout = _mm(y_mix, dh1_mx, TN, 1024, d, t, name="g_wout")[0]
    tok = comm.grads("mid", dict(w_out=g_wout, wq=g_wq, wkv=g_wkv, wo=g_wo))
    d_mix = _mm(dh1_mx, w["w_out"], NT, t, 1024, d, name="d_mix", tk=d, after=tok)[0]
    dproj_sw, g_swq, g_swk, g_sinks = _sw_bwd(proj_sw, pos, qg_t, kg_t, small["sw_sinks"], y_mix, d_mix, bsz, seq)
    tok = comm.poll(dproj_sw)
    dproj_hg, g_lb, g_hgn = _hg_bwd(proj_hg, small["hg_lower_bounds"], small["hg_norm_g"], o_hg, states, d_mix, bsz, seq,
                                    after=tok)
    g_in_hg = _mm(hn1, dproj_hg, TN, d, HG_COLS, t, name="g_in_hg")[0]
    g_in_sw = _mm(hn1, dproj_sw, TN, d, SW_COLS, t, name="g_in_sw")[0]
    tok = comm.grads("in", dict(w_in_hg=g_in_hg, w_in_sw=g_in_sw))
    dhn1_a = _mm(dproj_hg, w_in_hg, NT, t, d, HG_COLS, name="d_hn1_hg", tk=1024, after=tok)[0]
    grad_x, g_norm1 = _mm(dproj_sw, w_in_sw, NT, t, d, SW_COLS, name="d_hn1_sw", tk=SW_COLS, extras=(dhn1_a, x, dh1),
                          rows=(small["norm1_g"],), row_sums=1,
                          epilogue=lambda acc, prev, xv, dres, g: _rms_bwd_residual(acc + prev, xv, dres, g)[1:])

    g_small = dict(norm1_g=g_norm1, hg_lower_bounds=g_lb, hg_norm_g=g_hgn, sw_q_norm_g=g_swq, sw_k_norm_g=g_swk,
                   sw_sinks=g_sinks[:, 0:SW_HEADS], norm2_g=g_norm2, mem_norm_g=g_memn, xa_q_norm_g=g_xq,
                   xa_k_norm_g=g_xk, norm3_g=g_norm3)
    return loss_row, grad_x.reshape(bsz, seq, d), g_small


SMALL_NAMES = ("norm1_g", "hg_lower_bounds", "hg_norm_g", "sw_q_norm_g", "sw_k_norm_g", "sw_sinks", "norm2_g",
               "mem_norm_g", "xa_q_norm_g", "xa_k_norm_g", "norm3_g")
BIG_NAMES = ("w_in", "w_out", "xa_wq", "xa_wkv", "xa_wo", "mlp_up", "mlp_down")
WEIGHT_ORDER = ("norm1_g", "w_in", "hg_lower_bounds", "hg_norm_g", "sw_q_norm_g", "sw_k_norm_g", "sw_sinks", "w_out",
                "norm2_g", "mem_norm_g", "xa_wq", "xa_wkv", "xa_q_norm_g", "xa_k_norm_g", "xa_wo", "norm3_g",
                "mlp_up", "mlp_down")


def _pack_rows(vals, width):
    starts, at = [], 0
    for v in vals:
        starts.append(at)
        at += v.shape[0]
    total = at + (-at) % 8
    out = None
    for v, s in zip(vals, starts):
        placed = jnp.pad(v, ((s, total - s - v.shape[0]), (0, width - v.shape[1])))
        out = placed if out is None else out + placed
    return out, starts


class _MeshWeights:
    LATE = ("w_out", "xa_wq", "xa_wkv", "xa_wo", "mlp_up", "mlp_down")

    def __init__(self, shards, d, ff):
        self.shards, self.d, self.ff = shards, d, ff
        self.c_idx = lax.axis_index("c").astype(jnp.int32).reshape(1)
        chip = (2 * lax.axis_index("x") + lax.axis_index("y")).astype(jnp.int32)
        self.place_idx = jnp.stack([chip, lax.axis_index("c").astype(jnp.int32)])
        self.pending = []
        self.exchanging = None
        self.halves = {}

    def begin(self):
        shard = self.shards["w_in"]
        start, self.in_wait = _gather_chips_split(
            "gather_in", [shard], [_place_shard(shard, self.place_idx, name="place_w_in")])
        self.in_state = start()
        tok = (self.in_state["token"],)
        self.placed = [_place_shard(self.shards[n], self.place_idx, name="place_" + n, after=tok) for n in self.LATE]

    def first(self, after):
        _, lands = self.in_wait(self.in_state, (after, *self.placed))
        (g_in,) = _gather_finish(lands, "gather_in_finish")
        start, self.late_wait = _gather_chips_split("gather_late", [self.shards[n] for n in self.LATE], self.placed)
        self.late_state = start(after=(g_in,))
        full = jnp.concatenate([g_in[k] for k in range(4)], axis=1)
        return dict(w_in_hg=full[:, :HG_COLS], w_in_sw=full[:, HG_COLS:], token=self.late_state["token"])

    def rest(self, after):
        _, lands = self.late_wait(self.late_state, (after,))
        g_out, g_q, g_kv, g_o, g_up, g_dn = _gather_finish(lands, "gather_late_finish")
        d = self.d
        return dict(w_out=g_out.reshape(-1, d), wq=g_q.reshape(d, -1), wkv=g_kv.reshape(d, -1),
                    wo=jnp.concatenate([g_o[k] for k in range(4)], axis=1), up=g_up, down=g_dn.reshape(self.ff, d))

    def _scatter(self, tag, names, arrays, recv):
        parts = [_add_halves(g, r, self.c_idx, name="rs_add_halves_" + n) for n, g, r in zip(names, arrays, recv)]
        start, wait = _scatter_chips_split("rs_scatter_" + tag, parts)
        state = start()
        self.pending.append((names, wait, state))
        return state["token"]

    def _advance(self, after):
        if self.exchanging is None:
            return ()
        tag, names, wait, state = self.exchanging
        self.exchanging = None
        arrays, recv = wait(state, (after,))
        return (self._scatter(tag, names, arrays, recv),)

    def poll(self, after):
        return self._advance(after)

    def grads(self, tag, g):
        d, ff = self.d, self.ff
        if tag == "mlp":
            names, arrays = ("mlp_up", "mlp_down"), [g["up"], g["down"].reshape(4, ff // 4, d)]
        elif tag == "mid":
            names = ("w_out", "xa_wq", "xa_wkv", "xa_wo")
            ds = d // 4
            g_wo = jnp.stack([g["wo"][:, ds * k:ds * (k + 1)] for k in range(4)])
            arrays = [g["w_out"].reshape(4, -1, d), g["wq"].reshape(4, d // 4, -1), g["wkv"].reshape(4, d // 4, -1), g_wo]
        else:
            full = jnp.concatenate([g["w_in_hg"], g["w_in_sw"]], axis=1)
            ws = full.shape[1] // 4
            names, arrays = ("w_in",), [jnp.stack([full[:, ws * k:ws * (k + 1)] for k in range(4)])]
        toks = self._advance(arrays[0])
        if tag == "in":
            return toks + (self._scatter(tag, names, arrays, _exchange_halves(arrays, "rs_exchange_" + tag)),)
        start, wait = _exchange_halves_split("rs_exchange_" + tag, arrays)
        state = start()
        self.exchanging = (tag, names, wait, state)
        return toks + (state["token"],)

    def finish(self, after):
        for names, wait, state in self.pending:
            srcs, lands = wait(state, (after,))
            for n, p, r in zip(names, srcs, lands):
                self.halves[n] = _add_chips(p, r, self.place_idx, name="rs_add_chips_" + n)
        return dict(zip(BIG_NAMES, _join_halves([self.halves[n] for n in BIG_NAMES])))


def kernel(x, mem, positions, norm1_g, w_in, hg_lower_bounds, hg_norm_g, sw_q_norm_g, sw_k_norm_g, sw_sinks, w_out, norm2_g, mem_norm_g, xa_wq, xa_wkv, xa_q_norm_g, xa_k_norm_g, xa_wo, norm3_g, mlp_up, mlp_down, loss_target, m_norm1_g, m_w_in, m_hg_lower_bounds, m_hg_norm_g, m_sw_q_norm_g, m_sw_k_norm_g, m_sw_sinks, m_w_out, m_norm2_g, m_mem_norm_g, m_xa_wq, m_xa_wkv, m_xa_q_norm_g, m_xa_k_norm_g, m_xa_wo, m_norm3_g, m_mlp_up, m_mlp_down, v_norm1_g, v_w_in, v_hg_lower_bounds, v_hg_norm_g, v_sw_q_norm_g, v_sw_k_norm_g, v_sw_sinks, v_w_out, v_norm2_g, v_mem_norm_g, v_xa_wq, v_xa_wkv, v_xa_q_norm_g, v_xa_k_norm_g, v_xa_wo, v_norm3_g, v_mlp_up, v_mlp_down):
    given = dict(locals())
    weights = {n: given[n] for n in WEIGHT_ORDER}
    moms = {n: given["m_" + n] for n in WEIGHT_ORDER}
    vars_ = {n: given["v_" + n] for n in WEIGHT_ORDER}
    d = x.shape[-1]
    ff = mlp_down.shape[1] * 4
    small = {n: weights[n] for n in SMALL_NAMES}

    comm = _MeshWeights({n: weights[n][0].astype(_MXU_DTYPE) for n in BIG_NAMES}, d, ff)
    loss_row, grad_x, g_small = _local_step(x, mem, positions, loss_target, small, comm)
    big_grads = comm.finish(grad_x)

    packed, starts = _pack_rows([g_small[n] for n in SMALL_NAMES] + [loss_row], 1024)
    summed = _all_reduce_small(packed)
    small_grads = {}
    for n, s in zip(SMALL_NAMES, starts):
        r, c = weights[n].shape
        small_grads[n] = summed[s:s + r, 0:c]
    loss = summed[starts[-1], 0]

    grads, deltas, new_m, new_v = {}, {}, {}, {}
    for n in BIG_NAMES:
        shp = weights[n].shape
        g2 = big_grads[n]
        dl, mo, vo = _adamw_big(weights[n][0], g2, moms[n][0], vars_[n][0], name="adamw_" + n)
        grads[n], deltas[n], new_m[n], new_v[n] = (a.reshape(shp) for a in (g2, dl, mo, vo))
    sm_out = _adamw_small([weights[n] for n in SMALL_NAMES], [small_grads[n] for n in SMALL_NAMES],
                          [moms[n] for n in SMALL_NAMES], [vars_[n] for n in SMALL_NAMES])
    ns = len(SMALL_NAMES)
    for i, n in enumerate(SMALL_NAMES):
        grads[n], deltas[n], new_m[n], new_v[n] = small_grads[n], sm_out[i], sm_out[ns + i], sm_out[2 * ns + i]

    return (loss, grad_x, *[grads[n] for n in WEIGHT_ORDER], *[deltas[n] for n in WEIGHT_ORDER],
            *[new_m[n] for n in WEIGHT_ORDER], *[new_v[n] for n in WEIGHT_ORDER])
```

```python
import numpy as np
import jax
import jax.numpy as jnp
from jax import lax
from jax.experimental import pallas as pl
from jax.experimental.pallas import tpu as pltpu

F32 = jnp.float32
_MXU_DTYPE = jnp.bfloat16

EPS = 1e-6
HG_HEADS = 4
HG_D = 128
HG_CHUNK = 64
HG_TILE = 512
HG_LEVELS = (32, 16, 8, 4, 2, 1)
SW_HEADS = 8
SW_KV_HEADS = 2
SW_GROUP = SW_HEADS // SW_KV_HEADS
SW_HD = 64
SW_BLOCK = 128
ROPE_THETA = 500000.0
ROT_DIM = SW_HD // 4
XA_HEADS = 4
XA_HD = 128
HG_COLS = 4 * HG_HEADS * HG_D
SW_COLS = (SW_HEADS + 2 * SW_KV_HEADS) * SW_HD

ADAM_LR = 0.001
ADAM_B1 = 0.9
ADAM_B2 = 0.999
ADAM_EPS = 1e-08
ADAM_WD = 0.01
ADAM_STEP = 10

VMEM_LIMIT = 56 * 1024 * 1024
MESH = pl.DeviceIdType.MESH

NN = ((1,), (0,))
NT = ((1,), (1,))
TN = ((0,), (0,))


def _mx(v):
    return v.astype(_MXU_DTYPE)


def _dot(a, b, dims=NN):
    return lax.dot_general(_mx(a), _mx(b), (dims, ((), ())), preferred_element_type=F32)


def _split_dot(a, v, dims, parts):
    acc = None
    rest = v
    for p in range(parts):
        piece = _mx(rest)
        term = lax.dot_general(a, piece, (dims, ((), ())), preferred_element_type=F32)
        acc = term if acc is None else acc + term
        if p + 1 < parts:
            rest = rest - piece.astype(F32)
    return acc


def _params(sem):
    return pltpu.CompilerParams(dimension_semantics=sem, vmem_limit_bytes=VMEM_LIMIT)


def _mm(a, b, mode, m, n, k, *, name, tm=1024, tn=1024, tk=1024, a_spec=None, b_spec=None, extras=(), rows=(),
        epilogue=None, out_dtypes=(F32,), row_sums=0, out_shape=None, out_spec=None, after=()):
    after = tuple(t for t in after if t is not None)
    tm, tn, tk = min(tm, m), min(tn, n), min(tk, k)
    assert m % tm == 0 and n % tn == 0 and k % tk == 0, (name, m, n, k, tm, tn, tk)
    gi, gj, gk = m // tm, n // tn, k // tk
    assert row_sums == 0 or gj == 1, name
    if a_spec is None:
        a_spec = (pl.BlockSpec((tk, tm), lambda i, j, kk: (kk, i)) if mode == TN
                  else pl.BlockSpec((tm, tk), lambda i, j, kk: (i, kk)))
    if b_spec is None:
        b_spec = (pl.BlockSpec((tn, tk), lambda i, j, kk: (j, kk)) if mode == NT
                  else pl.BlockSpec((tk, tn), lambda i, j, kk: (kk, j)))
    mn_spec = pl.BlockSpec((tm, tn), lambda i, j, kk: (i, j))
    if epilogue is None:
        epilogue = lambda acc: (acc,)
    row_spec = pl.BlockSpec((1, tn), lambda i, j, kk: (0, j))
    n_ex, n_out = len(extras) + len(rows), len(out_dtypes)
    if out_shape is None:
        out_shape = tuple(jax.ShapeDtypeStruct((m, n), d) for d in out_dtypes)
        out_spec = tuple(mn_spec for _ in out_dtypes)
    out_shape = tuple(out_shape) + tuple(jax.ShapeDtypeStruct((1, n), F32) for _ in range(row_sums))
    out_spec = tuple(out_spec) + tuple(row_spec for _ in range(row_sums))

    n_after = len(after)

    def body(*refs):
        a_ref, b_ref = refs[0], refs[1]
        ex = refs[2:2 + n_ex]
        outs = refs[2 + n_ex + n_after:2 + n_ex + n_after + n_out + row_sums]
        first_row_tile = pl.program_id(0) == 0

        def finish(acc):
            res = epilogue(acc, *[e[...] for e in ex])
            for o, r in zip(outs[:n_out], res[:n_out]):
                o[...] = r.astype(o.dtype)
            if row_sums:
                @pl.when(first_row_tile)
                def _():
                    for o in outs[n_out:]:
                        o[...] = jnp.zeros_like(o)

                for o, r in zip(outs[n_out:], res[n_out:]):
                    o[...] += r

        if gk == 1:
            finish(_dot(a_ref[...], b_ref[...], mode))
        else:
            acc_ref = refs[-1]
            kk = pl.program_id(2)

            @pl.when(kk == 0)
            def _():
                acc_ref[...] = jnp.zeros_like(acc_ref)

            acc_ref[...] += _dot(a_ref[...], b_ref[...], mode)

            @pl.when(kk == gk - 1)
            def _():
                finish(acc_ref[...])

    return pl.pallas_call(
        body, name=name, grid=(gi, gj, gk),
        in_specs=([a_spec, b_spec] + [mn_spec] * len(extras) + [row_spec] * len(rows)
                  + [pl.BlockSpec(memory_space=pl.ANY)] * n_after),
        out_specs=out_spec, out_shape=out_shape,
        scratch_shapes=[pltpu.VMEM((tm, tn), F32)] if gk > 1 else [],
        compiler_params=_params(("arbitrary" if row_sums else "parallel", "parallel", "arbitrary")),
    )(a, b, *extras, *rows, *after)


def _rms_rows(xv, g):
    return xv * lax.rsqrt(jnp.mean(xv * xv, axis=1, keepdims=True) + EPS) * g


def _rms_rows_bwd(xv, g, dyv):
    r = lax.rsqrt(jnp.mean(xv * xv, axis=1, keepdims=True) + EPS)
    u = dyv * g
    return (r * u - xv * (r * r * r) * jnp.mean(u * xv, axis=1, keepdims=True),
            jnp.sum(dyv * xv * r, axis=0, keepdims=True))


def _residual_rms(acc, res, g):
    h = acc + res
    return h, _rms_rows(h, g)


def _rms_bwd_residual(dhn, xv, dres, g):
    dx, dg = _rms_rows_bwd(xv, g, dhn)
    dx = dx + dres
    return dx, dx, dg


def _rms_fwd(x, g, *, name, tm=512):
    t, d = x.shape
    tm = min(tm, t)

    def body(x_ref, g_ref, o_ref):
        o_ref[...] = _rms_rows(x_ref[...], g_ref[...]).astype(o_ref.dtype)

    return pl.pallas_call(
        body, name=name, grid=(t // tm,),
        in_specs=[pl.BlockSpec((tm, d), lambda i: (i, 0)), pl.BlockSpec((1, d), lambda i: (0, 0))],
        out_specs=pl.BlockSpec((tm, d), lambda i: (i, 0)),
        out_shape=jax.ShapeDtypeStruct((t, d), _MXU_DTYPE),
        compiler_params=_params(("parallel",)),
    )(x, g)


def _rms_gain_grad(x, g, dy, *, name, tm=512):
    t, d = x.shape
    tm = min(tm, t)

    def body(x_ref, g_ref, dy_ref, dg_ref):
        @pl.when(pl.program_id(0) == 0)
        def _():
            dg_ref[...] = jnp.zeros_like(dg_ref)

        dg_ref[...] += _rms_rows_bwd(x_ref[...], g_ref[...], dy_ref[...])[1]

    row = pl.BlockSpec((tm, d), lambda i: (i, 0))
    vec = pl.BlockSpec((1, d), lambda i: (0, 0))
    return pl.pallas_call(
        body, name=name, grid=(t // tm,), in_specs=[row, vec, row], out_specs=vec,
        out_shape=jax.ShapeDtypeStruct((1, d), F32), compiler_params=_params(("arbitrary",)),
    )(x, g, dy)


def _hg_constants():
    c = HG_CHUNK
    t = np.arange(c)
    sums = [t[None, :] <= t[:, None]]
    masks = []
    for m in HG_LEVELS:
        base = (t // (2 * m)) * (2 * m)
        mid = base + m - 1
        second = (t - base) >= m
        upper = (t[None, :] > mid[:, None]) & (t[None, :] <= t[:, None])
        lower = (t[None, :] > t[:, None]) & (t[None, :] <= mid[:, None])
        sums.append(np.where(second[:, None], upper, lower))
        masks.append(second[:, None] & (~second)[None, :] & (base[:, None] == base[None, :]))
    return (np.concatenate(sums, axis=0).astype(np.float32), np.stack(masks).astype(np.float32))


HG_HEAD_LANES = tuple(slice(HG_D * h, HG_D * (h + 1)) for h in range(HG_HEADS))


def _per_head(fn, slab):
    return jnp.concatenate([jnp.broadcast_to(fn(slab[:, hs]), (slab.shape[0], HG_D)) for hs in HG_HEAD_LANES], axis=1)


def _lane_sum(v):
    return jnp.sum(v, axis=1, keepdims=True)


def _lane_mean(v):
    return jnp.mean(v, axis=1, keepdims=True)


def _hg_gates(blk, lbp):
    w = HG_HEADS * HG_D
    q, x, v, gl = blk[:, 0:w], blk[:, w:2 * w], blk[:, 2 * w:3 * w], blk[:, 3 * w:4 * w]
    mx = jnp.max(lbp, axis=0, keepdims=True)
    e = jnp.exp(lbp - mx)
    lb = e[0:1, :] / jnp.sum(e, axis=0, keepdims=True)
    sig = jax.nn.sigmoid(x)
    f = lb + (1.0 - lb) * sig
    return q, v, gl, lb, sig, f, 1.0 - f, jnp.log(f)


def _hg_fwd(proj, lbp, ng, bsz, seq, *, y_width):
    t = proj.shape[0]
    nc = seq // HG_CHUNK
    a_np, m_np = _hg_constants()
    a_all = jnp.asarray(a_np, _MXU_DTYPE)
    masks = jnp.asarray(m_np, F32)
    nl = len(HG_LEVELS)

    ts = min(HG_TILE, seq)
    ns, nct = seq // ts, ts // HG_CHUNK
    hw = HG_HEADS * HG_D

    def body(p_ref, lb_ref, ng_ref, a_ref, m_ref, y_ref, o_ref, st_ref, carry):
        a_mat = a_ref[...]
        ngv = ng_ref[...]

        @pl.when(pl.program_id(0) == 0)
        def _():
            carry[...] = jnp.zeros_like(carry)

        ng4 = _tile_lanes(ngv, HG_HEADS)
        heads = range(HG_HEADS)
        exs = range(bsz)
        hl = HG_HEAD_LANES
        lbp_v = lb_ref[...]

        def chunk(c, _):
            rows = pl.ds(pl.multiple_of(c * HG_CHUNK, HG_CHUNK), HG_CHUNK)
            gates = [_hg_gates(p_ref[e, rows, :], lbp_v) for e in exs]
            q, v, gl = [g[0] for g in gates], [g[1] for g in gates], [g[2] for g in gates]
            k = [g[6] for g in gates]
            sts = [[carry[e, h] for h in heads] for e in exs]
            e_all = [_split_dot(a_mat, gates[e][7], NN, 3) for e in exs]
            b = [e_all[e][0:HG_CHUNK] for e in exs]
            qb = [q[e] * jnp.exp(b[e]) for e in exs]
            o = [[_dot(qb[e][:, hl[h]], sts[e][h], NT) for h in heads] for e in exs]
            p = [[jnp.zeros((HG_CHUNK, HG_CHUNK), F32) for _ in heads] for _ in exs]
            for li in range(nl):
                dec = [jnp.exp(e_all[e][HG_CHUNK * (li + 1):HG_CHUNK * (li + 2)]) for e in exs]
                qm, km, mk = [q[e] * dec[e] for e in exs], [k[e] * dec[e] for e in exs], m_ref[li]
                p = [[p[e][h] + mk * _dot(qm[e][:, hl[h]], km[e][:, hl[h]], NT) for h in heads] for e in exs]
            bl = [b[e][HG_CHUNK - 1:HG_CHUNK, :] for e in exs]
            kd = [k[e] * jnp.exp(bl[e] - b[e]) for e in exs]
            pv = [[_dot(p[e][h], v[e][:, hl[h]]) for h in heads] for e in exs]
            upd = [[_dot(v[e][:, hl[h]], kd[e][:, hl[h]], TN) for h in heads] for e in exs]
            for e in exs:
                o_all = (jnp.concatenate([o[e][h] + pv[e][h] for h in heads], axis=1)
                         + _per_head(_lane_sum, q[e] * k[e]) * v[e])
                r = lax.rsqrt(_per_head(_lane_mean, o_all * o_all) + EPS)
                ebl = jnp.exp(bl[e])
                for h in heads:
                    st_ref[e, h, c] = sts[e][h]
                    carry[e, h] = sts[e][h] * ebl[:, hl[h]] + upd[e][h]
                o_ref[e, rows, :] = o_all
                y_ref[e, rows, :] = (o_all * r * ng4) * (gl[e] * jax.nn.sigmoid(gl[e]))
            return 0

        lax.fori_loop(0, nct, chunk, 0)

    y3, o3, states = pl.pallas_call(
        body, name="hgrn2_fwd", grid=(ns,),
        in_specs=[pl.BlockSpec((bsz, ts, HG_COLS), lambda s: (0, s, 0)),
                  pl.BlockSpec((2, hw), lambda s: (0, 0)),
                  pl.BlockSpec((1, HG_D), lambda s: (0, 0)),
                  pl.BlockSpec(a_all.shape, lambda s: (0, 0)),
                  pl.BlockSpec(masks.shape, lambda s: (0, 0, 0))],
        out_specs=(pl.BlockSpec((bsz, ts, hw), lambda s: (0, s, 0)),
                   pl.BlockSpec((bsz, ts, hw), lambda s: (0, s, 0)),
                   pl.BlockSpec((bsz, HG_HEADS, nct, HG_D, HG_D), lambda s: (0, 0, s, 0, 0))),
        out_shape=(jax.ShapeDtypeStruct((bsz, seq, y_width), F32),
                   jax.ShapeDtypeStruct((bsz, seq, hw), F32),
                   jax.ShapeDtypeStruct((bsz, HG_HEADS, nc, HG_D, HG_D), F32)),
        scratch_shapes=[pltpu.VMEM((bsz, HG_HEADS, HG_D, HG_D), F32)],
        compiler_params=_params(("arbitrary",)),
    )(proj.reshape(bsz, seq, HG_COLS), lbp, ng, a_all, masks)
    return y3.reshape(t, y_width), o3.reshape(t, hw), states


def _hg_bwd(proj, lbp, ng, o_all, states, dy, bsz, seq, after=()):
    after = tuple(a for a in after if a is not None)
    t = proj.shape[0]
    nc = seq // HG_CHUNK
    a_np, m_np = _hg_constants()
    a_all = jnp.asarray(a_np, _MXU_DTYPE)
    masks = jnp.asarray(m_np, F32)
    nl = len(HG_LEVELS)
    cs = HG_CHUNK

    ts = min(HG_TILE, seq)
    ns, nct = seq // ts, ts // cs
    hw = HG_HEADS * HG_D

    def body(p_ref, lb_ref, ng_ref, a_ref, m_ref, o_ref, st_ref, dy_ref, *rest):
        dp_ref, dlb_ref, dng_ref, dst_ref = rest[len(after):]
        a_mat = a_ref[...]
        ngv = ng_ref[...]
        ng4 = _tile_lanes(ngv, HG_HEADS)
        last_row = lax.broadcasted_iota(jnp.int32, (cs, hw), 0) == cs - 1
        first = pl.program_id(0) == 0
        heads = range(HG_HEADS)
        exs = range(bsz)
        hl = HG_HEAD_LANES
        lbp_v = lb_ref[...]

        @pl.when(first)
        def _():
            dst_ref[...] = jnp.zeros_like(dst_ref)

        def side_by_side(parts):
            return jnp.concatenate(parts, axis=1)

        def chunk(i, carry):
            dlb_acc, dng_acc = carry
            c = nct - 1 - i
            rows = pl.ds(pl.multiple_of(c * cs, cs), cs)
            gates = [_hg_gates(p_ref[e, rows, :], lbp_v) for e in exs]
            q, v, gl = [g[0] for g in gates], [g[1] for g in gates], [g[2] for g in gates]
            lb, sig, f, k = gates[0][3], [g[4] for g in gates], [g[5] for g in gates], [g[6] for g in gates]
            o = [o_ref[e, rows, :] for e in exs]
            dyv = [dy_ref[e, rows, :] for e in exs]
            sts = [[st_ref[e, h, c] for h in heads] for e in exs]
            dsts = [[dst_ref[e, h] for h in heads] for e in exs]
            e_all = [_split_dot(a_mat, gates[e][7], NN, 3) for e in exs]
            b = [e_all[e][0:cs] for e in exs]
            eb = [jnp.exp(b[e]) for e in exs]
            bl = [b[e][cs - 1:cs, :] for e in exs]
            ebl = [jnp.exp(bl[e]) for e in exs]
            ekd = [jnp.exp(bl[e] - b[e]) for e in exs]
            qb = [q[e] * eb[e] for e in exs]
            kd = [k[e] * ekd[e] for e in exs]
            do, dgl = [], []
            for e in exs:
                sg = jax.nn.sigmoid(gl[e])
                silu = gl[e] * sg
                r = lax.rsqrt(_per_head(_lane_mean, o[e] * o[e]) + EPS)
                dgl.append(dyv[e] * (o[e] * r * ng4) * (sg * (1.0 + gl[e] * (1.0 - sg))))
                u = dyv[e] * silu * ng4
                do.append(r * u - o[e] * (r * r * r) * _per_head(_lane_mean, u * o[e]))
                dng4 = jnp.sum(dyv[e] * silu * o[e] * r, axis=0, keepdims=True)
                dng_acc = dng_acc + ((dng4[:, hl[0]] + dng4[:, hl[1]]) + (dng4[:, hl[2]] + dng4[:, hl[3]]))
            es, qm, km = [], [], []
            p = [[jnp.zeros((cs, cs), F32) for _ in heads] for _ in exs]
            for li in range(nl):
                dec = [jnp.exp(e_all[e][cs * (li + 1):cs * (li + 2)]) for e in exs]
                es.append(dec)
                qm.append([q[e] * dec[e] for e in exs])
                km.append([k[e] * dec[e] for e in exs])
                mk = m_ref[li]
                p = [[p[e][h] + mk * _dot(qm[li][e][:, hl[h]], km[li][e][:, hl[h]], NT) for h in heads] for e in exs]
            dp = [[_dot(do[e][:, hl[h]], v[e][:, hl[h]], NT) for h in heads] for e in exs]
            dv_p = [[_dot(p[e][h], do[e][:, hl[h]], TN) for h in heads] for e in exs]
            dv_s = [[_dot(kd[e][:, hl[h]], dsts[e][h], NT) for h in heads] for e in exs]
            dqb = [side_by_side([_dot(do[e][:, hl[h]], sts[e][h]) for h in heads]) for e in exs]
            dkd = [side_by_side([_dot(v[e][:, hl[h]], dsts[e][h]) for h in heads]) for e in exs]
            new_dst = [[_dot(do[e][:, hl[h]], qb[e][:, hl[h]], TN) for h in heads] for e in exs]
            dv = [side_by_side([dv_p[e][h] + dv_s[e][h] for h in heads]) + _per_head(_lane_sum, q[e] * k[e]) * do[e]
                  for e in exs]
            dq = [dqb[e] * eb[e] for e in exs]
            dk = [dkd[e] * ekd[e] for e in exs]
            de = []
            for e in exs:
                dbl = (jnp.sum(dkd[e] * kd[e], axis=0, keepdims=True)
                       + side_by_side([jnp.sum(dsts[e][h] * sts[e][h], axis=0, keepdims=True) for h in heads]) * ebl[e])
                de.append([dqb[e] * qb[e] - dkd[e] * kd[e] + jnp.where(last_row, dbl, 0.0)])
            for li in range(nl):
                mk = m_ref[li]
                dpm = [[mk * dp[e][h] for h in heads] for e in exs]
                dqm = [side_by_side([_dot(dpm[e][h], km[li][e][:, hl[h]]) for h in heads]) for e in exs]
                dkm = [side_by_side([_dot(dpm[e][h], qm[li][e][:, hl[h]], TN) for h in heads]) for e in exs]
                for e in exs:
                    dq[e] = dq[e] + dqm[e] * es[li][e]
                    dk[e] = dk[e] + dkm[e] * es[li][e]
                    de[e].append(dqm[e] * qm[li][e] + dkm[e] * km[li][e])
            dg = [_split_dot(a_mat, jnp.concatenate(de[e], axis=0), TN, 2) for e in exs]
            for e in exs:
                dpd = _per_head(_lane_sum, do[e] * v[e])
                df = dg[e] / f[e] - (dk[e] + dpd * q[e])
                dp_ref[e, rows, 0:hw] = _mx(dq[e] + dpd * k[e])
                dp_ref[e, rows, hw:2 * hw] = _mx(df * (1.0 - lb) * sig[e] * (1.0 - sig[e]))
                dp_ref[e, rows, 2 * hw:3 * hw] = _mx(dv[e])
                dp_ref[e, rows, 3 * hw:4 * hw] = _mx(dgl[e])
                for h in heads:
                    dst_ref[e, h] = dsts[e][h] * ebl[e][:, hl[h]] + new_dst[e][h]
                dlb_acc = dlb_acc + jnp.sum(df * (1.0 - sig[e]), axis=0, keepdims=True)
            return dlb_acc, dng_acc

        dlb, dng = lax.fori_loop(0, nct, chunk, (jnp.zeros((1, hw), F32), jnp.zeros((1, HG_D), F32)))

        @pl.when(first)
        def _():
            dlb_ref[...] = jnp.zeros_like(dlb_ref)
            dng_ref[...] = jnp.zeros_like(dng_ref)

        mx = jnp.max(lbp_v, axis=0, keepdims=True)
        e = jnp.exp(lbp_v - mx)
        s0 = e[0:1, :] / jnp.sum(e, axis=0, keepdims=True)
        da0 = dlb * s0 * (1.0 - s0)
        dlb_ref[...] += jnp.concatenate([da0, -da0], axis=0)
        dng_ref[...] += dng

    rows3 = lambda w: pl.BlockSpec((bsz, ts, w), lambda s: (0, ns - 1 - s, 0))
    dproj, dlb, dng = pl.pallas_call(
        body, name="hgrn2_bwd", grid=(ns,),
        in_specs=[rows3(HG_COLS),
                  pl.BlockSpec((2, hw), lambda s: (0, 0)),
                  pl.BlockSpec((1, HG_D), lambda s: (0, 0)),
                  pl.BlockSpec(a_all.shape, lambda s: (0, 0)),
                  pl.BlockSpec(masks.shape, lambda s: (0, 0, 0)),
                  rows3(hw),
                  pl.BlockSpec((bsz, HG_HEADS, nct, HG_D, HG_D), lambda s: (0, 0, ns - 1 - s, 0, 0)),
                  rows3(hw)] + [pl.BlockSpec(memory_space=pl.ANY)] * len(after),
        out_specs=(rows3(HG_COLS),
                   pl.BlockSpec((2, hw), lambda s: (0, 0)),
                   pl.BlockSpec((1, HG_D), lambda s: (0, 0))),
        out_shape=(jax.ShapeDtypeStruct((bsz, seq, HG_COLS), _MXU_DTYPE),
                   jax.ShapeDtypeStruct((2, hw), F32),
                   jax.ShapeDtypeStruct((1, HG_D), F32)),
        scratch_shapes=[pltpu.VMEM((bsz, HG_HEADS, HG_D, HG_D), F32)],
        compiler_params=_params(("arbitrary",)),
    )(proj.reshape(bsz, seq, HG_COLS), lbp, ng, a_all, masks, o_all.reshape(bsz, seq, hw), states,
      dy.reshape(bsz, seq, dy.shape[1]), *after)
    return dproj.reshape(t, HG_COLS), dlb, dng


def _sw_constants():
    half = ROT_DIM // 2
    inv = (np.float32(ROPE_THETA) ** (-(np.arange(half, dtype=np.float32) * np.float32(2.0) / np.float32(ROT_DIM)))
           ).astype(np.float32)
    freq = np.zeros((1, 128), np.float32)
    sign = np.zeros((1, 128), np.float32)
    for h in range(2):
        freq[0, 64 * h:64 * h + half] = inv
        freq[0, 64 * h + half:64 * h + 2 * half] = inv
        sign[0, 64 * h:64 * h + half] = -1.0
        sign[0, 64 * h + half:64 * h + 2 * half] = 1.0
    seg = np.kron(np.eye(8, dtype=np.float32), np.full((64, 64), 1.0 / 64.0, np.float32))
    return freq, sign, seg


def _rope_tables(pos, freq, sign):
    ang = pos.astype(F32) * freq
    return jnp.cos(ang), jnp.sin(ang) * sign


def _tile_lanes(v, times):
    return v if times == 1 else jnp.concatenate([v] * times, axis=1)


def _swap_halves(v):
    w = v.shape[1]
    half = ROT_DIM // 2
    lane = lax.broadcasted_iota(jnp.int32, v.shape, 1) % SW_HD
    return jnp.where(lane < half, pltpu.roll(v, w - half, 1), jnp.where(lane < 2 * half, pltpu.roll(v, half, 1), 0.0))


def _sw_norm_rope(tv, gain, seg, cosv, sinv):
    w = tv.shape[1]
    ms = _split_dot_rhs(tv * tv, seg[0:w, 0:w])
    r = lax.rsqrt(ms + EPS)
    tn = tv * r * gain
    reps = w // 128
    return tn * _tile_lanes(cosv, reps) + _swap_halves(tn) * _tile_lanes(sinv, reps), r


def _split_dot_rhs(v, a):
    hi = _mx(v)
    lo = _mx(v - hi.astype(F32))
    return (lax.dot_general(hi, a, (NN, ((), ())), preferred_element_type=F32)
            + lax.dot_general(lo, a, (NN, ((), ())), preferred_element_type=F32))


def _sw_norm_rope_bwd(dt, tv, r, gain, seg, cosv, sinv):
    w = tv.shape[1]
    reps = w // 128
    dtn = dt * _tile_lanes(cosv, reps) + _swap_halves(dt * _tile_lanes(sinv, reps))
    u = dtn * gain
    dtv = r * u - tv * (r * r * r) * _split_dot_rhs(u * tv, seg[0:w, 0:w])
    return dtv, jnp.sum(dtn * tv * r, axis=0, keepdims=True)


def _sw_scores(qh, kp, kc):
    return _dot(qh, kp, NT), _dot(qh, kc, NT)


def _sw_probs(raw, sink, first_block):
    scale = SW_HD ** -0.5
    qi = lax.broadcasted_iota(jnp.int32, (SW_BLOCK, SW_BLOCK), 0)
    kj = lax.broadcasted_iota(jnp.int32, (SW_BLOCK, SW_BLOCK), 1)
    ok_prev = jnp.logical_and(kj > qi, jnp.logical_not(first_block))
    ok_cur = kj <= qi
    sp = jnp.where(ok_prev, raw[0] * scale, -jnp.inf)
    sc = jnp.where(ok_cur, raw[1] * scale, -jnp.inf)
    m = jnp.maximum(jnp.maximum(jnp.max(sp, axis=1, keepdims=True), jnp.max(sc, axis=1, keepdims=True)), sink)
    pp, pc = jnp.exp(sp - m), jnp.exp(sc - m)
    es = jnp.exp(sink - m)
    den = jnp.sum(pp, axis=1, keepdims=True) + jnp.sum(pc, axis=1, keepdims=True) + es
    return pp / den, pc / den, es / den


def _sw_specs(nb):
    def cur(b, n):
        return b * nb + jnp.minimum(n, nb - 1)

    def prev(b, n):
        return b * nb + jnp.maximum(jnp.minimum(n, nb - 1) - 1, 0)

    return cur, prev


def _sw_fwd(proj, pos, qg, kg, sinks, y_in, bsz, seq):
    t = proj.shape[0]
    nb = seq // SW_BLOCK
    freq_np, sign_np, seg_np = _sw_constants()
    freq, sign = jnp.asarray(freq_np), jnp.asarray(sign_np)
    seg = jnp.asarray(seg_np, _MXU_DTYPE)
    cur, prev = _sw_specs(nb)

    def body(q_ref, kc_ref, kp_ref, vc_ref, vp_ref, pc_ref, pp_ref, qg_ref, kg_ref, sk_ref, fr_ref, sn_ref, seg_ref,
             yin_ref, y_ref):
        del yin_ref
        n = pl.program_id(1)
        segv = seg_ref[...]
        cos_c, sin_c = _rope_tables(pc_ref[...], fr_ref[...], sn_ref[...])
        cos_p, sin_p = _rope_tables(pp_ref[...], fr_ref[...], sn_ref[...])
        qr, _ = _sw_norm_rope(q_ref[...], qg_ref[...], segv, cos_c, sin_c)
        kcr, _ = _sw_norm_rope(kc_ref[...], kg_ref[...], segv, cos_c, sin_c)
        kpr, _ = _sw_norm_rope(kp_ref[...], kg_ref[...], segv, cos_p, sin_p)
        vc, vp = vc_ref[...], vp_ref[...]
        ks = [slice(SW_HD * (h // SW_GROUP), SW_HD * (h // SW_GROUP + 1)) for h in range(SW_HEADS)]
        raw = [_sw_scores(qr[:, SW_HD * h:SW_HD * (h + 1)], kpr[:, ks[h]], kcr[:, ks[h]]) for h in range(SW_HEADS)]
        probs = [_sw_probs(raw[h], sk_ref[0, h], n == 0) for h in range(SW_HEADS)]
        for h in range(SW_HEADS):
            y_ref[:, SW_HD * h:SW_HD * (h + 1)] = _dot(probs[h][0], vp[:, ks[h]]) + _dot(probs[h][1], vc[:, ks[h]])

    rowq = pl.BlockSpec((SW_BLOCK, 512), lambda b, n: (cur(b, n), 0))
    full = lambda a: pl.BlockSpec(a.shape, lambda b, n: (0,) * a.ndim)
    yw = y_in.shape[1]
    return pl.pallas_call(
        body, name="swa_fwd", grid=(bsz, nb),
        in_specs=[rowq,
                  pl.BlockSpec((SW_BLOCK, 128), lambda b, n: (cur(b, n), 4)),
                  pl.BlockSpec((SW_BLOCK, 128), lambda b, n: (prev(b, n), 4)),
                  pl.BlockSpec((SW_BLOCK, 128), lambda b, n: (cur(b, n), 5)),
                  pl.BlockSpec((SW_BLOCK, 128), lambda b, n: (prev(b, n), 5)),
                  pl.BlockSpec((SW_BLOCK, 1), lambda b, n: (cur(b, n), 0)),
                  pl.BlockSpec((SW_BLOCK, 1), lambda b, n: (prev(b, n), 0)),
                  full(qg), full(kg),
                  pl.BlockSpec(memory_space=pltpu.SMEM),
                  full(freq), full(sign), full(seg),
                  pl.BlockSpec(memory_space=pl.ANY)],
        out_specs=pl.BlockSpec((SW_BLOCK, 512), lambda b, n: (cur(b, n), 1)),
        out_shape=jax.ShapeDtypeStruct((t, yw), F32),
        input_output_aliases={13: 0},
        compiler_params=_params(("parallel", "parallel")),
    )(proj, proj, proj, proj, proj, pos, pos, qg, kg, sinks, freq, sign, seg, y_in)


def _sw_bwd(proj, pos, qg, kg, sinks, y, dy, bsz, seq):
    t = proj.shape[0]
    nb = seq // SW_BLOCK
    freq_np, sign_np, seg_np = _sw_constants()
    freq, sign = jnp.asarray(freq_np), jnp.asarray(sign_np)
    seg = jnp.asarray(seg_np, _MXU_DTYPE)
    cur, prev = _sw_specs(nb)
    scale = SW_HD ** -0.5

    def body(q_ref, kc_ref, kp_ref, vc_ref, vp_ref, pc_ref, pp_ref, qg_ref, kg_ref, sk_ref, fr_ref, sn_ref, seg_ref,
             y_ref, dy_ref, dp_ref, dqg_ref, dkg_ref, dsk_ref,
             dq_car, dkv_car, dqr_s, dkc_s, dkp_s, dvc_s, dvp_s, gq_acc, gk_acc, sk_acc):
        b, n = pl.program_id(0), pl.program_id(1)
        first = jnp.logical_and(b == 0, n == 0)
        last = jnp.logical_and(b == pl.num_programs(0) - 1, n == nb)

        @pl.when(first)
        def _():
            gq_acc[...] = jnp.zeros_like(gq_acc)
            gk_acc[...] = jnp.zeros_like(gk_acc)
            sk_acc[...] = jnp.zeros_like(sk_acc)

        @pl.when(n < nb)
        def _():
            segv = seg_ref[...]
            cos_c, sin_c = _rope_tables(pc_ref[...], fr_ref[...], sn_ref[...])
            cos_p, sin_p = _rope_tables(pp_ref[...], fr_ref[...], sn_ref[...])
            qv, kcv, kpv = q_ref[...], kc_ref[...], kp_ref[...]
            qr, rq = _sw_norm_rope(qv, qg_ref[...], segv, cos_c, sin_c)
            kcr, rkc = _sw_norm_rope(kcv, kg_ref[...], segv, cos_c, sin_c)
            kpr, rkp = _sw_norm_rope(kpv, kg_ref[...], segv, cos_p, sin_p)
            vc, vp = vc_ref[...], vp_ref[...]
            lane = lax.broadcasted_iota(jnp.int32, (1, 128), 1)
            dsk = jnp.zeros((1, 128), F32)
            heads = range(SW_HEADS)
            ks = [slice(SW_HD * (h // SW_GROUP), SW_HD * (h // SW_GROUP + 1)) for h in heads]
            hs = [slice(SW_HD * h, SW_HD * (h + 1)) for h in heads]
            qh = [qr[:, hs[h]] for h in heads]
            doh = [dy_ref[:, hs[h]] for h in heads]
            raw = [_sw_scores(qh[h], kpr[:, ks[h]], kcr[:, ks[h]]) for h in heads]
            dpp = [_dot(doh[h], vp[:, ks[h]], NT) for h in heads]
            dpc = [_dot(doh[h], vc[:, ks[h]], NT) for h in heads]
            probs = [_sw_probs(raw[h], sk_ref[0, h], n == 0) for h in heads]
            dsp, dsc = [], []
            for h in heads:
                pp, pc, ps = probs[h]
                delta = jnp.sum(doh[h] * y_ref[:, hs[h]], axis=1, keepdims=True)
                dsp.append(pp * (dpp[h] - delta) * scale)
                dsc.append(pc * (dpc[h] - delta) * scale)
                dsk = dsk + jnp.where(lane == h, -jnp.sum(ps * delta), 0.0)
            for h in heads:
                dqr_s[:, hs[h]] = _dot(dsp[h], kpr[:, ks[h]]) + _dot(dsc[h], kcr[:, ks[h]])
            for kv in range(SW_KV_HEADS):
                group = range(SW_GROUP * kv, SW_GROUP * (kv + 1))
                kvs = slice(SW_HD * kv, SW_HD * (kv + 1))
                dvp_s[:, kvs] = sum(_dot(probs[h][0], doh[h], TN) for h in group)
                dvc_s[:, kvs] = sum(_dot(probs[h][1], doh[h], TN) for h in group)
                dkp_s[:, kvs] = sum(_dot(dsp[h], qh[h], TN) for h in group)
                dkc_s[:, kvs] = sum(_dot(dsc[h], qh[h], TN) for h in group)
            dq, gq = _sw_norm_rope_bwd(dqr_s[...], qv, rq, qg_ref[...], segv, cos_c, sin_c)
            dkc, gkc = _sw_norm_rope_bwd(dkc_s[...], kcv, rkc, kg_ref[...], segv, cos_c, sin_c)
            dkp, gkp = _sw_norm_rope_bwd(dkp_s[...], kpv, rkp, kg_ref[...], segv, cos_p, sin_p)
            gq_acc[...] += gq
            gk_acc[...] += gkc + gkp
            sk_acc[...] += dsk

            @pl.when(n > 0)
            def _():
                dp_ref[:, 0:512] = _mx(dq_car[...])
                dp_ref[:, 512:640] = _mx(dkv_car[:, 0:128] + dkp)
                dp_ref[:, 640:768] = _mx(dkv_car[:, 128:256] + dvp_s[...])

            dq_car[...] = dq
            dkv_car[:, 0:128] = dkc
            dkv_car[:, 128:256] = dvc_s[...]

        @pl.when(n == nb)
        def _():
            dp_ref[:, 0:512] = _mx(dq_car[...])
            dp_ref[:, 512:768] = _mx(dkv_car[...])

        @pl.when(last)
        def _():
            gq = gq_acc[...]
            acc = gq[:, 0:SW_HD]
            for h in range(1, SW_HEADS):
                acc = acc + gq[:, SW_HD * h:SW_HD * (h + 1)]
            dqg_ref[...] = acc
            gk = gk_acc[...]
            dkg_ref[...] = gk[:, 0:SW_HD] + gk[:, SW_HD:2 * SW_HD]
            dsk_ref[...] = sk_acc[...]

    rowq = pl.BlockSpec((SW_BLOCK, 512), lambda b, n: (cur(b, n), 0))
    full = lambda a: pl.BlockSpec(a.shape, lambda b, n: (0,) * a.ndim)

    def out_row(b, n):
        return b * nb + jnp.maximum(n - 1, 0)

    return pl.pallas_call(
        body, name="swa_bwd", grid=(bsz, nb + 1),
        in_specs=[rowq,
                  pl.BlockSpec((SW_BLOCK, 128), lambda b, n: (cur(b, n), 4)),
                  pl.BlockSpec((SW_BLOCK, 128), lambda b, n: (prev(b, n), 4)),
                  pl.BlockSpec((SW_BLOCK, 128), lambda b, n: (cur(b, n), 5)),
                  pl.BlockSpec((SW_BLOCK, 128), lambda b, n: (prev(b, n), 5)),
                  pl.BlockSpec((SW_BLOCK, 1), lambda b, n: (cur(b, n), 0)),
                  pl.BlockSpec((SW_BLOCK, 1), lambda b, n: (prev(b, n), 0)),
                  full(qg), full(kg),
                  pl.BlockSpec(memory_space=pltpu.SMEM),
                  full(freq), full(sign), full(seg),
                  pl.BlockSpec((SW_BLOCK, 512), lambda b, n: (cur(b, n), 1)),
                  pl.BlockSpec((SW_BLOCK, 512), lambda b, n: (cur(b, n), 1))],
        out_specs=(pl.BlockSpec((SW_BLOCK, SW_COLS), lambda b, n: (out_row(b, n), 0)),
                   pl.BlockSpec((1, SW_HD), lambda b, n: (0, 0)),
                   pl.BlockSpec((1, SW_HD), lambda b, n: (0, 0)),
                   pl.BlockSpec((1, 128), lambda b, n: (0, 0))),
        out_shape=(jax.ShapeDtypeStruct((t, SW_COLS), _MXU_DTYPE),
                   jax.ShapeDtypeStruct((1, SW_HD), F32),
                   jax.ShapeDtypeStruct((1, SW_HD), F32),
                   jax.ShapeDtypeStruct((1, 128), F32)),
        scratch_shapes=[pltpu.VMEM((SW_BLOCK, 512), F32), pltpu.VMEM((SW_BLOCK, 256), F32),
                        pltpu.VMEM((SW_BLOCK, 512), F32),
                        pltpu.VMEM((SW_BLOCK, 128), F32), pltpu.VMEM((SW_BLOCK, 128), F32),
                        pltpu.VMEM((SW_BLOCK, 128), F32), pltpu.VMEM((SW_BLOCK, 128), F32),
                        pltpu.VMEM((1, 512), F32), pltpu.VMEM((1, 128), F32), pltpu.VMEM((1, 128), F32)],
        compiler_params=_params(("arbitrary", "arbitrary")),
    )(proj, proj, proj, proj, proj, pos, pos, qg, kg, sinks, freq, sign, seg, y, dy)


def _head_rms(tv, gain):
    r = lax.rsqrt(jnp.mean(tv * tv, axis=1, keepdims=True) + EPS)
    return tv * r * gain, r


def _head_rms_bwd(dtn, tv, r, gain):
    u = dtn * gain
    return r * u - tv * (r * r * r) * jnp.mean(u * tv, axis=1, keepdims=True), jnp.sum(dtn * tv * r, axis=0, keepdims=True)


def _xa_softmax(raw):
    s = raw * (XA_HD ** -0.5)
    e = jnp.exp(s - jnp.max(s, axis=1, keepdims=True))
    return e / jnp.sum(e, axis=1, keepdims=True)


def _xa_fwd(qx, kvx, qg, kg, bsz, seq, mlen, *, tq=512):
    t = qx.shape[0]
    tq = min(tq, seq)
    nq = seq // tq
    w = XA_HEADS * XA_HD

    def body(q_ref, kv_ref, qg_ref, kg_ref, o_ref):
        heads = range(XA_HEADS)
        hs = [slice(XA_HD * h, XA_HD * (h + 1)) for h in heads]
        qn = [_head_rms(q_ref[:, hs[h]], qg_ref[...])[0] for h in heads]
        kn = [_head_rms(kv_ref[:, hs[h]], kg_ref[...])[0] for h in heads]
        raw = [_dot(qn[h], kn[h], NT) for h in heads]
        p = [_xa_softmax(raw[h]) for h in heads]
        for h in heads:
            o_ref[:, hs[h]] = _dot(p[h], kv_ref[:, w + XA_HD * h:w + XA_HD * (h + 1)]).astype(o_ref.dtype)

    vec = pl.BlockSpec((1, XA_HD), lambda b, i: (0, 0))
    return pl.pallas_call(
        body, name="xattn_fwd", grid=(bsz, nq),
        in_specs=[pl.BlockSpec((tq, w), lambda b, i: (b * nq + i, 0)),
                  pl.BlockSpec((mlen, 2 * w), lambda b, i: (b, 0)), vec, vec],
        out_specs=pl.BlockSpec((tq, w), lambda b, i: (b * nq + i, 0)),
        out_shape=jax.ShapeDtypeStruct((t, w), _MXU_DTYPE),
        compiler_params=_params(("parallel", "parallel")),
    )(qx, kvx, qg, kg)


def _xa_bwd(qx, kvx, qg, kg, do, bsz, seq, mlen, *, tq=512):
    t = qx.shape[0]
    tq = min(tq, seq)
    nq = seq // tq
    w = XA_HEADS * XA_HD
    scale = XA_HD ** -0.5

    def body(q_ref, kv_ref, qg_ref, kg_ref, do_ref, dq_ref, dkv_ref, dqg_ref, dkg_ref):
        b, i = pl.program_id(0), pl.program_id(1)

        @pl.when(jnp.logical_and(b == 0, i == 0))
        def _():
            dqg_ref[...] = jnp.zeros_like(dqg_ref)
            dkg_ref[...] = jnp.zeros_like(dkg_ref)

        @pl.when(i == 0)
        def _():
            dkv_ref[...] = jnp.zeros_like(dkv_ref)

        heads = range(XA_HEADS)
        hs = [slice(XA_HD * h, XA_HD * (h + 1)) for h in heads]
        vs = [slice(w + XA_HD * h, w + XA_HD * (h + 1)) for h in heads]
        qv = [q_ref[:, hs[h]] for h in heads]
        kv = [kv_ref[:, hs[h]] for h in heads]
        doh = [do_ref[:, hs[h]] for h in heads]
        qn = [_head_rms(qv[h], qg_ref[...]) for h in heads]
        kn = [_head_rms(kv[h], kg_ref[...]) for h in heads]
        raw = [_dot(qn[h][0], kn[h][0], NT) for h in heads]
        dp = [_dot(doh[h], kv_ref[:, vs[h]], NT) for h in heads]
        p = [_xa_softmax(raw[h]) for h in heads]
        ds = [p[h] * (dp[h] - jnp.sum(p[h] * dp[h], axis=1, keepdims=True)) * scale for h in heads]
        dqn = [_dot(ds[h], kn[h][0]) for h in heads]
        dkn = [_dot(ds[h], qn[h][0], TN) for h in heads]
        dvv = [_dot(p[h], doh[h], TN) for h in heads]
        gq_sum = jnp.zeros((1, XA_HD), F32)
        gk_sum = jnp.zeros((1, XA_HD), F32)
        for h in heads:
            dqv, gq = _head_rms_bwd(dqn[h], qv[h], qn[h][1], qg_ref[...])
            dkv, gk = _head_rms_bwd(dkn[h], kv[h], kn[h][1], kg_ref[...])
            dq_ref[:, hs[h]] = dqv.astype(dq_ref.dtype)
            dkv_ref[:, hs[h]] += dkv
            dkv_ref[:, vs[h]] += dvv[h]
            gq_sum = gq_sum + gq
            gk_sum = gk_sum + gk
        dqg_ref[...] += gq_sum
        dkg_ref[...] += gk_sum

    vec = pl.BlockSpec((1, XA_HD), lambda b, i: (0, 0))
    row = pl.BlockSpec((tq, w), lambda b, i: (b * nq + i, 0))
    mem = pl.BlockSpec((mlen, 2 * w), lambda b, i: (b, 0))
    return pl.pallas_call(
        body, name="xattn_bwd", grid=(bsz, nq),
        in_specs=[row, mem, vec, vec, row],
        out_specs=(row, mem, vec, vec),
        out_shape=(jax.ShapeDtypeStruct((t, w), _MXU_DTYPE), jax.ShapeDtypeStruct((bsz * mlen, 2 * w), F32),
                   jax.ShapeDtypeStruct((1, XA_HD), F32), jax.ShapeDtypeStruct((1, XA_HD), F32)),
        compiler_params=_params(("arbitrary", "arbitrary")),
    )(qx, kvx, qg, kg, do)


def _loss_finish(sq_row, d_model):
    def body(s_ref, o_ref):
        o_ref[...] = jnp.zeros_like(o_ref) + 0.5 * jnp.sum(s_ref[...]) / float(d_model)

    return pl.pallas_call(body, name="loss_finish", out_shape=jax.ShapeDtypeStruct((1, 128), F32))(sq_row)


def _adamw_math(w, g, m, v):
    m = ADAM_B1 * m + (1.0 - ADAM_B1) * g
    v = ADAM_B2 * v + (1.0 - ADAM_B2) * (g * g)
    m_hat = m / (1.0 - ADAM_B1 ** ADAM_STEP)
    v_hat = v / (1.0 - ADAM_B2 ** ADAM_STEP)
    return -ADAM_LR * (m_hat / (jnp.sqrt(v_hat) + ADAM_EPS) + ADAM_WD * w), m, v


def _adamw_big(w, g, m, v, *, name, tr=512):
    r, c = w.shape
    tr = min(tr, r)

    def body(w_ref, g_ref, m_ref, v_ref, go_ref, d_ref, mo_ref, vo_ref):
        gv = g_ref[...]
        d, mn, vn = _adamw_math(w_ref[...], gv, m_ref[...], v_ref[...])
        go_ref[...] = gv
        d_ref[...] = d
        mo_ref[...] = mn
        vo_ref[...] = vn

    spec = pl.BlockSpec((tr, c), lambda i: (i, 0))
    shp = jax.ShapeDtypeStruct((r, c), F32)
    return pl.pallas_call(
        body, name=name, grid=(r // tr,), in_specs=[spec] * 4, out_specs=(spec,) * 4, out_shape=(shp,) * 4,
        compiler_params=_params(("parallel",)),
    )(w, g, m, v)


def _adamw_small(ws, gs, ms, vs):
    n = len(ws)

    def body(*refs):
        for i in range(n):
            d, mn, vn = _adamw_math(refs[i][...], refs[n + i][...], refs[2 * n + i][...], refs[3 * n + i][...])
            refs[4 * n + i][...] = d
            refs[5 * n + i][...] = mn
            refs[6 * n + i][...] = vn

    shapes = tuple(jax.ShapeDtypeStruct(w.shape, F32) for w in ws)
    return pl.pallas_call(body, name="adamw_small", out_shape=shapes * 3)(*ws, *gs, *ms, *vs)


def _add_halves(g, recv, c_idx, *, name, tr=512):
    _, r, c = g.shape
    h = r // 2
    tr = min(tr, h)
    nt = h // tr

    def body(c_ref, g_ref, r_ref, o_ref):
        del c_ref
        o_ref[...] = g_ref[...] + r_ref[...]

    return pl.pallas_call(
        body, name=name,
        grid_spec=pltpu.PrefetchScalarGridSpec(
            num_scalar_prefetch=1, grid=(4, nt),
            in_specs=[pl.BlockSpec((None, tr, c), lambda k, i, cr: (k, cr[0] * nt + i, 0)),
                      pl.BlockSpec((None, tr, c), lambda k, i, cr: (k, i, 0))],
            out_specs=pl.BlockSpec((None, tr, c), lambda k, i, cr: (k, i, 0))),
        out_shape=jax.ShapeDtypeStruct((4, h, c), F32),
        compiler_params=_params(("parallel", "parallel")),
    )(c_idx, g, recv)


def _add_chips(p, recv, place_idx, *, name, tr=512):
    _, h, c = p.shape
    tr = min(tr, h)
    nt = h // tr

    def body(pi_ref, p_ref, r_ref, o_ref):
        del pi_ref
        o_ref[...] = ((p_ref[...] + r_ref[0]) + r_ref[1]) + r_ref[2]

    return pl.pallas_call(
        body, name=name,
        grid_spec=pltpu.PrefetchScalarGridSpec(
            num_scalar_prefetch=1, grid=(nt,),
            in_specs=[pl.BlockSpec((None, tr, c), lambda i, pi: (pi[0], i, 0)),
                      pl.BlockSpec((3, tr, c), lambda i, pi: (0, i, 0))],
            out_specs=pl.BlockSpec((tr, c), lambda i, pi: (pi[1] * nt + i, 0))),
        out_shape=jax.ShapeDtypeStruct((2 * h, c), F32),
        compiler_params=_params(("parallel",)),
    )(place_idx, p, recv)


def _place_shard(shard, place_idx, *, name, tr=512, after=()):
    r, c = shard.shape
    tr = min(tr, r)

    def body(pi_ref, s_ref, *rest):
        del pi_ref
        rest[-1][...] = s_ref[...]

    return pl.pallas_call(
        body, name=name,
        grid_spec=pltpu.PrefetchScalarGridSpec(
            num_scalar_prefetch=1, grid=(r // tr,),
            in_specs=[pl.BlockSpec((tr, c), lambda i, pi: (i, 0))] + [pl.BlockSpec(memory_space=pl.ANY)] * len(after),
            out_specs=pl.BlockSpec((None, tr, c), lambda i, pi: (pi[0], i, 0))),
        out_shape=jax.ShapeDtypeStruct((4, r, c), shard.dtype),
        compiler_params=_params(("parallel",)),
    )(place_idx, shard, *after)


def _place():
    x, y, c = lax.axis_index("x"), lax.axis_index("y"), lax.axis_index("c")
    chips = [(1 - x, y), (x, 1 - y), (1 - x, 1 - y)]
    return x, y, c, chips


ANY = pl.BlockSpec(memory_space=pl.ANY)


def _exchange_halves(grads, name):
    n = len(grads)

    def body(*refs):
        ins, outs = refs[:n], refs[n:2 * n]
        send_sems, recv_sems = refs[2 * n:]
        x, y, c, _ = _place()

        def copy(a):
            h = ins[a].shape[1] // 2
            return pltpu.make_async_remote_copy(
                src_ref=ins[a].at[:, pl.ds((1 - c) * h, h), :], dst_ref=outs[a],
                send_sem=send_sems.at[a], recv_sem=recv_sems.at[a], device_id=(x, y, 1 - c), device_id_type=MESH)

        for a in range(n):
            copy(a).start()
        for a in range(n):
            copy(a).wait_recv()
        for a in range(n):
            copy(a).wait_send()

    return pl.pallas_call(
        body, name=name,
        in_specs=[ANY] * n, out_specs=tuple([ANY] * n),
        out_shape=tuple(jax.ShapeDtypeStruct((4, g.shape[1] // 2, g.shape[2]), g.dtype) for g in grads),
        scratch_shapes=[pltpu.SemaphoreType.DMA((n,)), pltpu.SemaphoreType.DMA((n,))],
    )(*grads)


HBM = pl.BlockSpec(memory_space=pltpu.HBM)
SEM = pl.BlockSpec(memory_space=pltpu.SEMAPHORE)
EFFECT = pltpu.SideEffectType.DATAFLOW_SIDE_EFFECTING


def _in_hbm(a):
    return pltpu.with_memory_space_constraint(a, pltpu.HBM)


def _split_copy_calls(name, srcs, lands, n_copies, make_copies):
    ns, nl = len(srcs), len(lands)
    nb = ns + nl

    def start(after=()):
        n_after = len(after)

        def body(*refs):
            outs = refs[nb + n_after:]
            copies = make_copies(refs[:ns], refs[ns:nb], outs[0], outs[1])
            for cp in copies:
                cp.start()
            token = refs[-1]
            token[...] = jnp.zeros_like(token)

        bufs = [_in_hbm(a) for a in list(srcs) + list(lands)]
        out = pl.pallas_call(
            body, name=name + "_start",
            out_shape=(pltpu.SemaphoreType.DMA((n_copies,)), pltpu.SemaphoreType.DMA((n_copies,)),
                       *[pltpu.HBM(a.shape, a.dtype) for a in bufs], jax.ShapeDtypeStruct((8, 128), F32)),
            in_specs=[HBM] * nb + [pl.BlockSpec(memory_space=pl.ANY)] * n_after,
            out_specs=(SEM, SEM, *[HBM] * nb, pl.BlockSpec(memory_space=pltpu.VMEM)),
            input_output_aliases={i: 2 + i for i in range(nb)},
            compiler_params=pltpu.CompilerParams(has_side_effects=EFFECT),
        )(*bufs, *after)
        return dict(send=out[0], recv=out[1], bufs=list(out[2:2 + nb]), token=out[-1])

    def wait(state, after):
        def body(*refs):
            copies = make_copies(refs[:ns], refs[ns:nb], refs[nb], refs[nb + 1])
            for cp in copies:
                cp.wait_send()
            for cp in copies:
                cp.wait_recv()

        bufs = state["bufs"]
        out = pl.pallas_call(
            body, name=name + "_wait",
            out_shape=tuple(pltpu.HBM(a.shape, a.dtype) for a in bufs),
            in_specs=[HBM] * nb + [SEM, SEM] + [pl.BlockSpec(memory_space=pl.ANY)] * len(after),
            out_specs=tuple([HBM] * nb),
            input_output_aliases={i: i for i in range(nb)},
            compiler_params=pltpu.CompilerParams(has_side_effects=EFFECT),
        )(*bufs, state["send"], state["recv"], *after)
        return list(out[:ns]), list(out[ns:])

    return start, wait


def _scatter_chips_split(name, parts):
    n = len(parts)
    lands = [lax.empty((3,) + p.shape[1:], p.dtype) for p in parts]

    def make_copies(srcs, lnds, send_sems, recv_sems):
        _, _, c, chips = _place()
        return [pltpu.make_async_remote_copy(
            src_ref=srcs[a].at[2 * px + py], dst_ref=lnds[a].at[j], send_sem=send_sems.at[a * 3 + j],
            recv_sem=recv_sems.at[a * 3 + j], device_id=(px, py, c), device_id_type=MESH)
            for a in range(n) for j, (px, py) in enumerate(chips)]

    return _split_copy_calls(name, parts, lands, 3 * n, make_copies)


def _exchange_halves_split(name, grads):
    n = len(grads)
    lands = [lax.empty((4, g.shape[1] // 2, g.shape[2]), g.dtype) for g in grads]

    def make_copies(srcs, lnds, send_sems, recv_sems):
        x, y, c, _ = _place()
        out = []
        for a in range(n):
            h = srcs[a].shape[1] // 2
            out.append(pltpu.make_async_remote_copy(
                src_ref=srcs[a].at[:, pl.ds((1 - c) * h, h), :], dst_ref=lnds[a], send_sem=send_sems.at[a],
                recv_sem=recv_sems.at[a], device_id=(x, y, 1 - c), device_id_type=MESH))
        return out

    return _split_copy_calls(name, grads, lands, n, make_copies)


def _gather_chips_split(name, shards, lands):
    n = len(shards)

    def make_copies(srcs, lnds, send_sems, recv_sems):
        x, y, c, chips = _place()
        out = []
        for a in range(n):
            h = srcs[a].shape[0] // 2
            for j, (px, py) in enumerate(chips):
                out.append(pltpu.make_async_remote_copy(
                    src_ref=srcs[a].at[pl.ds(c * h, h), :], dst_ref=lnds[a].at[2 * x + y, pl.ds(c * h, h), :],
                    send_sem=send_sems.at[a * 3 + j], recv_sem=recv_sems.at[a * 3 + j],
                    device_id=(px, py, c), device_id_type=MESH))
        return out

    return _split_copy_calls(name, shards, lands, 3 * n, make_copies)


def _gather_finish(gathered, name):
    n = len(gathered)

    def body(*refs):
        outs = refs[n:2 * n]
        send_sems, recv_sems = refs[2 * n:]
        x, y, c, chips = _place()

        def copy(a, j, chip_idx, which):
            h = outs[a].shape[1] // 2
            rows = outs[a].at[chip_idx, pl.ds(which * h, h), :]
            return pltpu.make_async_remote_copy(
                src_ref=rows, dst_ref=rows, send_sem=send_sems.at[a * 3 + j], recv_sem=recv_sems.at[a * 3 + j],
                device_id=(x, y, 1 - c), device_id_type=MESH)

        for a in range(n):
            for j, (px, py) in enumerate(chips):
                copy(a, j, 2 * px + py, c).start()
        for a in range(n):
            for j, (px, py) in enumerate(chips):
                copy(a, j, 2 * px + py, 1 - c).wait_recv()
        for a in range(n):
            for j, (px, py) in enumerate(chips):
                copy(a, j, 2 * px + py, c).wait_send()

    return pl.pallas_call(
        body, name=name,
        in_specs=[ANY] * n, out_specs=tuple([ANY] * n),
        out_shape=tuple(jax.ShapeDtypeStruct(g.shape, g.dtype) for g in gathered),
        input_output_aliases={i: i for i in range(n)},
        scratch_shapes=[pltpu.SemaphoreType.DMA((3 * n,)), pltpu.SemaphoreType.DMA((3 * n,))],
    )(*gathered)


def _join_halves(fulls):
    n = len(fulls)

    def body(*refs):
        outs = refs[n:2 * n]
        send_sems, recv_sems = refs[2 * n:]
        x, y, c, _ = _place()

        def copy(a, which):
            h = outs[a].shape[0] // 2
            rows = outs[a].at[pl.ds(which * h, h), :]
            return pltpu.make_async_remote_copy(
                src_ref=rows, dst_ref=rows, send_sem=send_sems.at[a], recv_sem=recv_sems.at[a],
                device_id=(x, y, 1 - c), device_id_type=MESH)

        for a in range(n):
            copy(a, c).start()
        for a in range(n):
            copy(a, 1 - c).wait_recv()
        for a in range(n):
            copy(a, c).wait_send()

    return pl.pallas_call(
        body, name="rs_join_halves",
        in_specs=[ANY] * n, out_specs=tuple([ANY] * n),
        out_shape=tuple(jax.ShapeDtypeStruct(p.shape, p.dtype) for p in fulls),
        input_output_aliases={i: i for i in range(n)},
        scratch_shapes=[pltpu.SemaphoreType.DMA((n,)), pltpu.SemaphoreType.DMA((n,))],
    )(*fulls)


def _all_reduce_small(sm):
    r, w = sm.shape

    def body(sm_ref, o_ref, buf, send_sems, recv_sems):
        x, y, c, _ = _place()
        me = 4 * x + 2 * y + c
        buf[me] = sm_ref[...]
        rel = [(dx, dy, dc) for dx in (0, 1) for dy in (0, 1) for dc in (0, 1)][1:]

        def copy(k, slot, to):
            return pltpu.make_async_remote_copy(
                src_ref=sm_ref, dst_ref=buf.at[slot], send_sem=send_sems.at[k], recv_sem=recv_sems.at[k],
                device_id=to, device_id_type=MESH)

        peers = []
        for k, (dx, dy, dc) in enumerate(rel):
            px = 1 - x if dx else x
            py = 1 - y if dy else y
            pc = 1 - c if dc else c
            peers.append((px, py, pc))
            copy(k, me, (px, py, pc)).start()
        for k, (px, py, pc) in enumerate(peers):
            copy(k, 4 * px + 2 * py + pc, (px, py, pc)).wait_recv()
        for k, (px, py, pc) in enumerate(peers):
            copy(k, me, (px, py, pc)).wait_send()
        acc = buf[0]
        for d in range(1, 8):
            acc = acc + buf[d]
        o_ref[...] = acc

    vm = pl.BlockSpec(memory_space=pltpu.VMEM)
    return pl.pallas_call(
        body, name="all_reduce_small", in_specs=[vm], out_specs=vm,
        out_shape=jax.ShapeDtypeStruct((r, w), F32),
        scratch_shapes=[pltpu.VMEM((8, r, w), F32), pltpu.SemaphoreType.DMA((7,)), pltpu.SemaphoreType.DMA((7,))],
    )(sm)


class _LocalWeights:
    def __init__(self, w):
        self.w = w
        self.g = {}

    def begin(self):
        pass

    def first(self, after):
        del after
        return self.w

    def rest(self, after):
        del after
        return self.w

    def grads(self, tag, g):
        del tag
        self.g.update(g)
        return ()

    def poll(self, after):
        del after
        return ()


def _local_step(x3, mem3, pos2, target3, small, comm):
    bsz, seq, d = x3.shape
    mlen = mem3.shape[1]
    t = bsz * seq
    comm.begin()
    x = x3.reshape(t, d)
    mem = mem3.reshape(bsz * mlen, d)
    target = target3.reshape(t, d)
    pos = pos2.reshape(t, 1)
    qg_t = jnp.tile(small["sw_q_norm_g"], (1, SW_HEADS))
    kg_t = jnp.tile(small["sw_k_norm_g"], (1, SW_KV_HEADS))

    hn1 = _rms_fwd(x, small["norm1_g"], name="rms1_fwd")
    w = comm.first(hn1)
    proj_hg = _mm(hn1, w["w_in_hg"], NN, t, HG_COLS, d, name="proj_hg", tk=d, after=(w.get("token"),))[0]
    proj_sw = _mm(hn1, w["w_in_sw"], NN, t, SW_COLS, d, name="proj_sw", tk=d)[0]
    y_mix, o_hg, states = _hg_fwd(proj_hg, small["hg_lower_bounds"], small["hg_norm_g"], bsz, seq, y_width=1024)
    y_mix = _sw_fwd(proj_sw, pos, qg_t, kg_t, small["sw_sinks"], y_mix, bsz, seq)
    w_in_hg, w_in_sw = w["w_in_hg"], w["w_in_sw"]
    w = comm.rest(y_mix)
    ff = w["down"].shape[0]
    ffs = ff // 4
    h1, hn2 = _mm(y_mix, w["w_out"], NN, t, d, 1024, name="out_proj", tk=1024, extras=(x,), rows=(small["norm2_g"],),
                  epilogue=_residual_rms, out_dtypes=(F32, _MXU_DTYPE))
    mn = _rms_fwd(mem, small["mem_norm_g"], name="rms_mem_fwd")
    qx = _mm(hn2, w["wq"], NN, t, 512, d, name="xa_q", tk=d)[0]
    kvx = _mm(mn, w["wkv"], NN, bsz * mlen, 1024, d, name="xa_kv", tk=d)[0]
    ox = _xa_fwd(qx, kvx, small["xa_q_norm_g"], small["xa_k_norm_g"], bsz, seq, mlen)
    h2, hn3 = _mm(ox, w["wo"], NN, t, d, 512, name="xa_o", tk=512, extras=(h1,), rows=(small["norm3_g"],),
                  epilogue=_residual_rms, out_dtypes=(F32, _MXU_DTYPE))

    def relu_sq(acc):
        a = jnp.maximum(acc, 0.0)
        return a, a * a

    act, act2 = _mm(hn3, w["up"], NN, t, ff, d, name="mlp_up", tm=2048, tn=ffs, tk=d,
                    b_spec=pl.BlockSpec((None, d, ffs), lambda i, j, kk: (j, 0, 0)),
                    epilogue=relu_sq, out_dtypes=(_MXU_DTYPE, _MXU_DTYPE))
    inv_d = 1.0 / d

    def loss_cotangent(acc, res, tgt):
        diff = acc + res - tgt
        v = diff * inv_d
        return v, v, jnp.sum(diff * diff, axis=0, keepdims=True)

    dy, dy_mx, sq_row = _mm(act2, w["down"], NN, t, d, ff, name="mlp_down", extras=(h2, target),
                            epilogue=loss_cotangent, out_dtypes=(F32, _MXU_DTYPE), row_sums=1)
    loss_row = _loss_finish(sq_row, d)

    dz = _mm(dy_mx, w["down"], NT, t, ff, d, name="d_act", tm=2048, tk=d, extras=(act,),
             epilogue=lambda acc, a: (acc * (2.0 * a.astype(F32)),), out_dtypes=(_MXU_DTYPE,))[0]
    g_down = _mm(act2, dy_mx, TN, ff, d, t, name="g_down")[0]
    g_up = _mm(hn3, dz, TN, d, ff, t, name="g_up", tn=ffs,
               out_shape=(jax.ShapeDtypeStruct((4, d, ffs), F32),),
               out_spec=(pl.BlockSpec((None, min(1024, d), ffs), lambda i, j, kk: (j, i, 0)),))[0]
    tok = comm.grads("mlp", dict(up=g_up, down=g_down))
    dh2, dh2_mx, g_norm3 = _mm(dz, w["up"], NT, t, d, ff, name="d_hn3", tk=ffs, after=tok,
                               b_spec=pl.BlockSpec((None, min(1024, d), ffs), lambda i, j, kk: (kk, j, 0)),
                               extras=(h2, dy), rows=(small["norm3_g"],), epilogue=_rms_bwd_residual,
                               out_dtypes=(F32, _MXU_DTYPE), row_sums=1)
    d_ox = _mm(dh2_mx, w["wo"], NT, t, 512, d, name="d_ox", tk=d)[0]
    g_wo = _mm(ox, dh2_mx, TN, 512, d, t, name="g_wo")[0]
    d_qx, d_kvx, g_xq, g_xk = _xa_bwd(qx, kvx, small["xa_q_norm_g"], small["xa_k_norm_g"], d_ox, bsz, seq, mlen)
    g_wq = _mm(hn2, d_qx, TN, d, 512, t, name="g_wq")[0]
    g_wkv = _mm(mn, d_kvx, TN, d, 1024, bsz * mlen, name="g_wkv")[0]
    dh1, dh1_mx, g_norm2 = _mm(d_qx, w["wq"], NT, t, d, 512, name="d_hn2", tk=512, extras=(h1, dh2),
                               rows=(small["norm2_g"],), epilogue=_rms_bwd_residual, out_dtypes=(F32, _MXU_DTYPE),
                               row_sums=1)
    dmn = _mm(d_kvx, w["wkv"], NT, bsz * mlen, d, 1024, name="d_mn", tk=1024)[0]
    g_memn = _rms_gain_grad(mem, small["mem_norm_g"], dmn, name="rms_mem_bwd")
    g_wout = _mm(y_mix, dh1_mx, TN, 1024, d, t, name="g_wout")[0]
    tok = comm.grads("mid", dict(w_out=g_wout, wq=g_wq, wkv=g_wkv, wo=g_wo))
    d_mix = _mm(dh1_mx, w["w_out"], NT, t, 1024, d, name="d_mix", tk=d, after=tok)[0]
    dproj_sw, g_swq, g_swk, g_sinks = _sw_bwd(proj_sw, pos, qg_t, kg_t, small["sw_sinks"], y_mix, d_mix, bsz, seq)
    tok = comm.poll(dproj_sw)
    dproj_hg, g_lb, g_hgn = _hg_bwd(proj_hg, small["hg_lower_bounds"], small["hg_norm_g"], o_hg, states, d_mix, bsz, seq,
                                    after=tok)
    g_in_hg = _mm(hn1, dproj_hg, TN, d, HG_COLS, t, name="g_in_hg")[0]
    g_in_sw = _mm(hn1, dproj_sw, TN, d, SW_COLS, t, name="g_in_sw")[0]
    tok = comm.grads("in", dict(w_in_hg=g_in_hg, w_in_sw=g_in_sw))
    dhn1_a = _mm(dproj_hg, w_in_hg, NT, t, d, HG_COLS, name="d_hn1_hg", tk=1024, after=tok)[0]
    grad_x, g_norm1 = _mm(dproj_sw, w_in_sw, NT, t, d, SW_COLS, name="d_hn1_sw", tk=SW_COLS, extras=(dhn1_a, x, dh1),
                          rows=(small["norm1_g"],), row_sums=1,
                          epilogue=lambda acc, prev, xv, dres, g: _rms_bwd_residual(acc + prev, xv, dres, g)[1:])

    g_small = dict(norm1_g=g_norm1, hg_lower_bounds=g_lb, hg_norm_g=g_hgn, sw_q_norm_g=g_swq, sw_k_norm_g=g_swk,
                   sw_sinks=g_sinks[:, 0:SW_HEADS], norm2_g=g_norm2, mem_norm_g=g_memn, xa_q_norm_g=g_xq,
                   xa_k_norm_g=g_xk, norm3_g=g_norm3)
    return loss_row, grad_x.reshape(bsz, seq, d), g_small


SMALL_NAMES = ("norm1_g", "hg_lower_bounds", "hg_norm_g", "sw_q_norm_g", "sw_k_norm_g", "sw_sinks", "norm2_g",
               "mem_norm_g", "xa_q_norm_g", "xa_k_norm_g", "norm3_g")
BIG_NAMES = ("w_in", "w_out", "xa_wq", "xa_wkv", "xa_wo", "mlp_up", "mlp_down")
WEIGHT_ORDER = ("norm1_g", "w_in", "hg_lower_bounds", "hg_norm_g", "sw_q_norm_g", "sw_k_norm_g", "sw_sinks", "w_out",
                "norm2_g", "mem_norm_g", "xa_wq", "xa_wkv", "xa_q_norm_g", "xa_k_norm_g", "xa_wo", "norm3_g",
                "mlp_up", "mlp_down")


def _pack_rows(vals, width):
    starts, at = [], 0
    for v in vals:
        starts.append(at)
        at += v.shape[0]
    total = at + (-at) % 8
    out = None
    for v, s in zip(vals, starts):
        placed = jnp.pad(v, ((s, total - s - v.shape[0]), (0, width - v.shape[1])))
        out = placed if out is None else out + placed
    return out, starts


class _MeshWeights:
    LATE = ("w_out", "xa_wq", "xa_wkv", "xa_wo", "mlp_up", "mlp_down")

    def __init__(self, shards, d, ff):
        self.shards, self.d, self.ff = shards, d, ff
        self.c_idx = lax.axis_index("c").astype(jnp.int32).reshape(1)
        chip = (2 * lax.axis_index("x") + lax.axis_index("y")).astype(jnp.int32)
        self.place_idx = jnp.stack([chip, lax.axis_index("c").astype(jnp.int32)])
        self.pending = []
        self.exchanging = None
        self.halves = {}

    def begin(self):
        shard = self.shards["w_in"]
        start, self.in_wait = _gather_chips_split(
            "gather_in", [shard], [_place_shard(shard, self.place_idx, name="place_w_in")])
        self.in_state = start()
        tok = (self.in_state["token"],)
        self.placed = [_place_shard(self.shards[n], self.place_idx, name="place_" + n, after=tok) for n in self.LATE]

    def first(self, after):
        _, lands = self.in_wait(self.in_state, (after, *self.placed))
        (g_in,) = _gather_finish(lands, "gather_in_finish")
        start, self.late_wait = _gather_chips_split("gather_late", [self.shards[n] for n in self.LATE], self.placed)
        self.late_state = start(after=(g_in,))
        full = jnp.concatenate([g_in[k] for k in range(4)], axis=1)
        return dict(w_in_hg=full[:, :HG_COLS], w_in_sw=full[:, HG_COLS:], token=self.late_state["token"])

    def rest(self, after):
        _, lands = self.late_wait(self.late_state, (after,))
        g_out, g_q, g_kv, g_o, g_up, g_dn = _gather_finish(lands, "gather_late_finish")
        d = self.d
        return dict(w_out=g_out.reshape(-1, d), wq=g_q.reshape(d, -1), wkv=g_kv.reshape(d, -1),
                    wo=jnp.concatenate([g_o[k] for k in range(4)], axis=1), up=g_up, down=g_dn.reshape(self.ff, d))

    def _scatter(self, tag, names, arrays, recv):
        parts = [_add_halves(g, r, self.c_idx, name="rs_add_halves_" + n) for n, g, r in zip(names, arrays, recv)]
        start, wait = _scatter_chips_split("rs_scatter_" + tag, parts)
        state = start()
        self.pending.append((names, wait, state))
        return state["token"]

    def _advance(self, after):
        if self.exchanging is None:
            return ()
        tag, names, wait, state = self.exchanging
        self.exchanging = None
        arrays, recv = wait(state, (after,))
        return (self._scatter(tag, names, arrays, recv),)

    def poll(self, after):
        return self._advance(after)

    def grads(self, tag, g):
        d, ff = self.d, self.ff
        if tag == "mlp":
            names, arrays = ("mlp_up", "mlp_down"), [g["up"], g["down"].reshape(4, ff // 4, d)]
        elif tag == "mid":
            names = ("w_out", "xa_wq", "xa_wkv", "xa_wo")
            ds = d // 4
            g_wo = jnp.stack([g["wo"][:, ds * k:ds * (k + 1)] for k in range(4)])
            arrays = [g["w_out"].reshape(4, -1, d), g["wq"].reshape(4, d // 4, -1), g["wkv"].reshape(4, d // 4, -1), g_wo]
        else:
            full = jnp.concatenate([g["w_in_hg"], g["w_in_sw"]], axis=1)
            ws = full.shape[1] // 4
            names, arrays = ("w_in",), [jnp.stack([full[:, ws * k:ws * (k + 1)] for k in range(4)])]
        toks = self._advance(arrays[0])
        if tag == "in":
            return toks + (self._scatter(tag, names, arrays, _exchange_halves(arrays, "rs_exchange_" + tag)),)
        start, wait = _exchange_halves_split("rs_exchange_" + tag, arrays)
        state = start()
        self.exchanging = (tag, names, wait, state)
        return toks + (state["token"],)

    def finish(self, after):
        for names, wait, state in self.pending:
            srcs, lands = wait(state, (after,))
            for n, p, r in zip(names, srcs, lands):
                self.halves[n] = _add_chips(p, r, self.place_idx, name="rs_add_chips_" + n)
        return dict(zip(BIG_NAMES, _join_halves([self.halves[n] for n in BIG_NAMES])))


def kernel(x, mem, positions, norm1_g, w_in, hg_lower_bounds, hg_norm_g, sw_q_norm_g, sw_k_norm_g, sw_sinks, w_out, norm2_g, mem_norm_g, xa_wq, xa_wkv, xa_q_norm_g, xa_k_norm_g, xa_wo, norm3_g, mlp_up, mlp_down, loss_target, m_norm1_g, m_w_in, m_hg_lower_bounds, m_hg_norm_g, m_sw_q_norm_g, m_sw_k_norm_g, m_sw_sinks, m_w_out, m_norm2_g, m_mem_norm_g, m_xa_wq, m_xa_wkv, m_xa_q_norm_g, m_xa_k_norm_g, m_xa_wo, m_norm3_g, m_mlp_up, m_mlp_down, v_norm1_g, v_w_in, v_hg_lower_bounds, v_hg_norm_g, v_sw_q_norm_g, v_sw_k_norm_g, v_sw_sinks, v_w_out, v_norm2_g, v_mem_norm_g, v_xa_wq, v_xa_wkv, v_xa_q_norm_g, v_xa_k_norm_g, v_xa_wo, v_norm3_g, v_mlp_up, v_mlp_down):
    given = dict(locals())
    weights = {n: given[n] for n in WEIGHT_ORDER}
    moms = {n: given["m_" + n] for n in WEIGHT_ORDER}
    vars_ = {n: given["v_" + n] for n in WEIGHT_ORDER}
    d = x.shape[-1]
    ff = mlp_down.shape[1] * 4
    small = {n: weights[n] for n in SMALL_NAMES}

    comm = _MeshWeights({n: weights[n][0].astype(_MXU_DTYPE) for n in BIG_NAMES}, d, ff)
    loss_row, grad_x, g_small = _local_step(x, mem, positions, loss_target, small, comm)
    big_grads = comm.finish(grad_x)

    packed, starts = _pack_rows([g_small[n] for n in SMALL_NAMES] + [loss_row], 1024)
    summed = _all_reduce_small(packed)
    small_grads = {}
    for n, s in zip(SMALL_NAMES, starts):
        r, c = weights[n].shape
        small_grads[n] = summed[s:s + r, 0:c]
    loss = summed[starts[-1], 0]

    grads, deltas, new_m, new_v = {}, {}, {}, {}
    for n in BIG_NAMES:
        shp = weights[n].shape
        g2, dl, mo, vo = _adamw_big(weights[n][0], big_grads[n], moms[n][0], vars_[n][0], name="adamw_" + n)
        grads[n], deltas[n], new_m[n], new_v[n] = (a.reshape(shp) for a in (g2, dl, mo, vo))
    sm_out = _adamw_small([weights[n] for n in SMALL_NAMES], [small_grads[n] for n in SMALL_NAMES],
                          [moms[n] for n in SMALL_NAMES], [vars_[n] for n in SMALL_NAMES])
    ns = len(SMALL_NAMES)
    for i, n in enumerate(SMALL_NAMES):
        grads[n], deltas[n], new_m[n], new_v[n] = small_grads[n], sm_out[i], sm_out[ns + i], sm_out[2 * ns + i]

    return (loss, grad_x, *[grads[n] for n in WEIGHT_ORDER], *[deltas[n] for n in WEIGHT_ORDER],
            *[new_m[n] for n in WEIGHT_ORDER], *[new_v[n] for n in WEIGHT_ORDER])
```

```python
import numpy as np
import jax
import jax.numpy as jnp
from jax import lax
from jax.experimental import pallas as pl
from jax.experimental.pallas import tpu as pltpu

F32 = jnp.float32
_MXU_DTYPE = jnp.bfloat16

EPS = 1e-6
HG_HEADS = 4
HG_D = 128
HG_CHUNK = 64
HG_TILE = 512
HG_LEVELS = (32, 16, 8, 4, 2, 1)
SW_HEADS = 8
SW_KV_HEADS = 2
SW_GROUP = SW_HEADS // SW_KV_HEADS
SW_HD = 64
SW_BLOCK = 128
ROPE_THETA = 500000.0
ROT_DIM = SW_HD // 4
XA_HEADS = 4
XA_HD = 128
HG_COLS = 4 * HG_HEADS * HG_D
SW_COLS = (SW_HEADS + 2 * SW_KV_HEADS) * SW_HD

ADAM_LR = 0.001
ADAM_B1 = 0.9
ADAM_B2 = 0.999
ADAM_EPS = 1e-08
ADAM_WD = 0.01
ADAM_STEP = 10

VMEM_LIMIT = 56 * 1024 * 1024
MESH = pl.DeviceIdType.MESH

NN = ((1,), (0,))
NT = ((1,), (1,))
TN = ((0,), (0,))


def _mx(v):
    return v.astype(_MXU_DTYPE)


def _dot(a, b, dims=NN):
    return lax.dot_general(_mx(a), _mx(b), (dims, ((), ())), preferred_element_type=F32)


def _split_dot(a, v, dims, parts):
    acc = None
    rest = v
    for p in range(parts):
        piece = _mx(rest)
        term = lax.dot_general(a, piece, (dims, ((), ())), preferred_element_type=F32)
        acc = term if acc is None else acc + term
        if p + 1 < parts:
            rest = rest - piece.astype(F32)
    return acc


def _params(sem):
    return pltpu.CompilerParams(dimension_semantics=sem, vmem_limit_bytes=VMEM_LIMIT)


def _mm(a, b, mode, m, n, k, *, name, tm=1024, tn=1024, tk=1024, a_spec=None, b_spec=None, extras=(), rows=(),
        epilogue=None, out_dtypes=(F32,), row_sums=0, out_shape=None, out_spec=None, after=()):
    after = tuple(t for t in after if t is not None)
    tm, tn, tk = min(tm, m), min(tn, n), min(tk, k)
    assert m % tm == 0 and n % tn == 0 and k % tk == 0, (name, m, n, k, tm, tn, tk)
    gi, gj, gk = m // tm, n // tn, k // tk
    assert row_sums == 0 or gj == 1, name
    if a_spec is None:
        a_spec = (pl.BlockSpec((tk, tm), lambda i, j, kk: (kk, i)) if mode == TN
                  else pl.BlockSpec((tm, tk), lambda i, j, kk: (i, kk)))
    if b_spec is None:
        b_spec = (pl.BlockSpec((tn, tk), lambda i, j, kk: (j, kk)) if mode == NT
                  else pl.BlockSpec((tk, tn), lambda i, j, kk: (kk, j)))
    mn_spec = pl.BlockSpec((tm, tn), lambda i, j, kk: (i, j))
    if epilogue is None:
        epilogue = lambda acc: (acc,)
    row_spec = pl.BlockSpec((1, tn), lambda i, j, kk: (0, j))
    n_ex, n_out = len(extras) + len(rows), len(out_dtypes)
    if out_shape is None:
        out_shape = tuple(jax.ShapeDtypeStruct((m, n), d) for d in out_dtypes)
        out_spec = tuple(mn_spec for _ in out_dtypes)
    out_shape = tuple(out_shape) + tuple(jax.ShapeDtypeStruct((1, n), F32) for _ in range(row_sums))
    out_spec = tuple(out_spec) + tuple(row_spec for _ in range(row_sums))

    n_after = len(after)

    def body(*refs):
        a_ref, b_ref = refs[0], refs[1]
        ex = refs[2:2 + n_ex]
        outs = refs[2 + n_ex + n_after:2 + n_ex + n_after + n_out + row_sums]
        first_row_tile = pl.program_id(0) == 0

        def finish(acc):
            res = epilogue(acc, *[e[...] for e in ex])
            for o, r in zip(outs[:n_out], res[:n_out]):
                o[...] = r.astype(o.dtype)
            if row_sums:
                @pl.when(first_row_tile)
                def _():
                    for o in outs[n_out:]:
                        o[...] = jnp.zeros_like(o)

                for o, r in zip(outs[n_out:], res[n_out:]):
                    o[...] += r

        if gk == 1:
            finish(_dot(a_ref[...], b_ref[...], mode))
        else:
            acc_ref = refs[-1]
            kk = pl.program_id(2)

            @pl.when(kk == 0)
            def _():
                acc_ref[...] = jnp.zeros_like(acc_ref)

            acc_ref[...] += _dot(a_ref[...], b_ref[...], mode)

            @pl.when(kk == gk - 1)
            def _():
                finish(acc_ref[...])

    return pl.pallas_call(
        body, name=name, grid=(gi, gj, gk),
        in_specs=([a_spec, b_spec] + [mn_spec] * len(extras) + [row_spec] * len(rows)
                  + [pl.BlockSpec(memory_space=pl.ANY)] * n_after),
        out_specs=out_spec, out_shape=out_shape,
        scratch_shapes=[pltpu.VMEM((tm, tn), F32)] if gk > 1 else [],
        compiler_params=_params(("arbitrary" if row_sums else "parallel", "parallel", "arbitrary")),
    )(a, b, *extras, *rows, *after)


def _rms_rows(xv, g):
    return xv * lax.rsqrt(jnp.mean(xv * xv, axis=1, keepdims=True) + EPS) * g


def _rms_rows_bwd(xv, g, dyv):
    r = lax.rsqrt(jnp.mean(xv * xv, axis=1, keepdims=True) + EPS)
    u = dyv * g
    return (r * u - xv * (r * r * r) * jnp.mean(u * xv, axis=1, keepdims=True),
            jnp.sum(dyv * xv * r, axis=0, keepdims=True))


def _residual_rms(acc, res, g):
    h = acc + res
    return h, _rms_rows(h, g)


def _rms_bwd_residual(dhn, xv, dres, g):
    dx, dg = _rms_rows_bwd(xv, g, dhn)
    dx = dx + dres
    return dx, dx, dg


def _rms_fwd(x, g, *, name, tm=512):
    t, d = x.shape
    tm = min(tm, t)

    def body(x_ref, g_ref, o_ref):
        o_ref[...] = _rms_rows(x_ref[...], g_ref[...]).astype(o_ref.dtype)

    return pl.pallas_call(
        body, name=name, grid=(t // tm,),
        in_specs=[pl.BlockSpec((tm, d), lambda i: (i, 0)), pl.BlockSpec((1, d), lambda i: (0, 0))],
        out_specs=pl.BlockSpec((tm, d), lambda i: (i, 0)),
        out_shape=jax.ShapeDtypeStruct((t, d), _MXU_DTYPE),
        compiler_params=_params(("parallel",)),
    )(x, g)


def _rms_gain_grad(x, g, dy, *, name, tm=512):
    t, d = x.shape
    tm = min(tm, t)

    def body(x_ref, g_ref, dy_ref, dg_ref):
        @pl.when(pl.program_id(0) == 0)
        def _():
            dg_ref[...] = jnp.zeros_like(dg_ref)

        dg_ref[...] += _rms_rows_bwd(x_ref[...], g_ref[...], dy_ref[...])[1]

    row = pl.BlockSpec((tm, d), lambda i: (i, 0))
    vec = pl.BlockSpec((1, d), lambda i: (0, 0))
    return pl.pallas_call(
        body, name=name, grid=(t // tm,), in_specs=[row, vec, row], out_specs=vec,
        out_shape=jax.ShapeDtypeStruct((1, d), F32), compiler_params=_params(("arbitrary",)),
    )(x, g, dy)


def _hg_constants():
    c = HG_CHUNK
    t = np.arange(c)
    sums = [t[None, :] <= t[:, None]]
    masks = []
    for m in HG_LEVELS:
        base = (t // (2 * m)) * (2 * m)
        mid = base + m - 1
        second = (t - base) >= m
        upper = (t[None, :] > mid[:, None]) & (t[None, :] <= t[:, None])
        lower = (t[None, :] > t[:, None]) & (t[None, :] <= mid[:, None])
        sums.append(np.where(second[:, None], upper, lower))
        masks.append(second[:, None] & (~second)[None, :] & (base[:, None] == base[None, :]))
    return (np.concatenate(sums, axis=0).astype(np.float32), np.stack(masks).astype(np.float32))


HG_HEAD_LANES = tuple(slice(HG_D * h, HG_D * (h + 1)) for h in range(HG_HEADS))


def _per_head(fn, slab):
    return jnp.concatenate([jnp.broadcast_to(fn(slab[:, hs]), (slab.shape[0], HG_D)) for hs in HG_HEAD_LANES], axis=1)


def _lane_sum(v):
    return jnp.sum(v, axis=1, keepdims=True)


def _lane_mean(v):
    return jnp.mean(v, axis=1, keepdims=True)


def _hg_gates(blk, lbp):
    w = HG_HEADS * HG_D
    q, x, v, gl = blk[:, 0:w], blk[:, w:2 * w], blk[:, 2 * w:3 * w], blk[:, 3 * w:4 * w]
    mx = jnp.max(lbp, axis=0, keepdims=True)
    e = jnp.exp(lbp - mx)
    lb = e[0:1, :] / jnp.sum(e, axis=0, keepdims=True)
    sig = jax.nn.sigmoid(x)
    f = lb + (1.0 - lb) * sig
    return q, v, gl, lb, sig, f, 1.0 - f, jnp.log(f)


def _hg_fwd(proj, lbp, ng, bsz, seq, *, y_width):
    t = proj.shape[0]
    nc = seq // HG_CHUNK
    a_np, m_np = _hg_constants()
    a_all = jnp.asarray(a_np, _MXU_DTYPE)
    masks = jnp.asarray(m_np, F32)
    nl = len(HG_LEVELS)

    ts = min(HG_TILE, seq)
    ns, nct = seq // ts, ts // HG_CHUNK
    hw = HG_HEADS * HG_D

    def body(p_ref, lb_ref, ng_ref, a_ref, m_ref, y_ref, o_ref, st_ref, carry):
        a_mat = a_ref[...]
        ngv = ng_ref[...]

        @pl.when(pl.program_id(0) == 0)
        def _():
            carry[...] = jnp.zeros_like(carry)

        ng4 = _tile_lanes(ngv, HG_HEADS)
        heads = range(HG_HEADS)
        exs = range(bsz)
        hl = HG_HEAD_LANES
        lbp_v = lb_ref[...]

        def chunk(c, _):
            rows = pl.ds(pl.multiple_of(c * HG_CHUNK, HG_CHUNK), HG_CHUNK)
            gates = [_hg_gates(p_ref[e, rows, :], lbp_v) for e in exs]
            q, v, gl = [g[0] for g in gates], [g[1] for g in gates], [g[2] for g in gates]
            k = [g[6] for g in gates]
            sts = [[carry[e, h] for h in heads] for e in exs]
            e_all = [_split_dot(a_mat, gates[e][7], NN, 3) for e in exs]
            b = [e_all[e][0:HG_CHUNK] for e in exs]
            qb = [q[e] * jnp.exp(b[e]) for e in exs]
            o = [[_dot(qb[e][:, hl[h]], sts[e][h], NT) for h in heads] for e in exs]
            p = [[jnp.zeros((HG_CHUNK, HG_CHUNK), F32) for _ in heads] for _ in exs]
            for li in range(nl):
                dec = [jnp.exp(e_all[e][HG_CHUNK * (li + 1):HG_CHUNK * (li + 2)]) for e in exs]
                qm, km, mk = [q[e] * dec[e] for e in exs], [k[e] * dec[e] for e in exs], m_ref[li]
                p = [[p[e][h] + mk * _dot(qm[e][:, hl[h]], km[e][:, hl[h]], NT) for h in heads] for e in exs]
            bl = [b[e][HG_CHUNK - 1:HG_CHUNK, :] for e in exs]
            kd = [k[e] * jnp.exp(bl[e] - b[e]) for e in exs]
            pv = [[_dot(p[e][h], v[e][:, hl[h]]) for h in heads] for e in exs]
            upd = [[_dot(v[e][:, hl[h]], kd[e][:, hl[h]], TN) for h in heads] for e in exs]
            for e in exs:
                o_all = (jnp.concatenate([o[e][h] + pv[e][h] for h in heads], axis=1)
                         + _per_head(_lane_sum, q[e] * k[e]) * v[e])
                r = lax.rsqrt(_per_head(_lane_mean, o_all * o_all) + EPS)
                ebl = jnp.exp(bl[e])
                for h in heads:
                    st_ref[e, h, c] = sts[e][h]
                    carry[e, h] = sts[e][h] * ebl[:, hl[h]] + upd[e][h]
                o_ref[e, rows, :] = o_all
                y_ref[e, rows, :] = (o_all * r * ng4) * (gl[e] * jax.nn.sigmoid(gl[e]))
            return 0

        lax.fori_loop(0, nct, chunk, 0)

    y3, o3, states = pl.pallas_call(
        body, name="hgrn2_fwd", grid=(ns,),
        in_specs=[pl.BlockSpec((bsz, ts, HG_COLS), lambda s: (0, s, 0)),
                  pl.BlockSpec((2, hw), lambda s: (0, 0)),
                  pl.BlockSpec((1, HG_D), lambda s: (0, 0)),
                  pl.BlockSpec(a_all.shape, lambda s: (0, 0)),
                  pl.BlockSpec(masks.shape, lambda s: (0, 0, 0))],
        out_specs=(pl.BlockSpec((bsz, ts, hw), lambda s: (0, s, 0)),
                   pl.BlockSpec((bsz, ts, hw), lambda s: (0, s, 0)),
                   pl.BlockSpec((bsz, HG_HEADS, nct, HG_D, HG_D), lambda s: (0, 0, s, 0, 0))),
        out_shape=(jax.ShapeDtypeStruct((bsz, seq, y_width), F32),
                   jax.ShapeDtypeStruct((bsz, seq, hw), F32),
                   jax.ShapeDtypeStruct((bsz, HG_HEADS, nc, HG_D, HG_D), F32)),
        scratch_shapes=[pltpu.VMEM((bsz, HG_HEADS, HG_D, HG_D), F32)],
        compiler_params=_params(("arbitrary",)),
    )(proj.reshape(bsz, seq, HG_COLS), lbp, ng, a_all, masks)
    return y3.reshape(t, y_width), o3.reshape(t, hw), states


def _hg_bwd(proj, lbp, ng, o_all, states, dy, bsz, seq, after=()):
    after = tuple(a for a in after if a is not None)
    t = proj.shape[0]
    nc = seq // HG_CHUNK
    a_np, m_np = _hg_constants()
    a_all = jnp.asarray(a_np, _MXU_DTYPE)
    masks = jnp.asarray(m_np, F32)
    nl = len(HG_LEVELS)
    cs = HG_CHUNK

    ts = min(HG_TILE, seq)
    ns, nct = seq // ts, ts // cs
    hw = HG_HEADS * HG_D

    def body(p_ref, lb_ref, ng_ref, a_ref, m_ref, o_ref, st_ref, dy_ref, *rest):
        dp_ref, dlb_ref, dng_ref, dst_ref = rest[len(after):]
        a_mat = a_ref[...]
        ngv = ng_ref[...]
        ng4 = _tile_lanes(ngv, HG_HEADS)
        last_row = lax.broadcasted_iota(jnp.int32, (cs, hw), 0) == cs - 1
        first = pl.program_id(0) == 0
        heads = range(HG_HEADS)
        exs = range(bsz)
        hl = HG_HEAD_LANES
        lbp_v = lb_ref[...]

        @pl.when(first)
        def _():
            dst_ref[...] = jnp.zeros_like(dst_ref)

        def side_by_side(parts):
            return jnp.concatenate(parts, axis=1)

        def chunk(i, carry):
            dlb_acc, dng_acc = carry
            c = nct - 1 - i
            rows = pl.ds(pl.multiple_of(c * cs, cs), cs)
            gates = [_hg_gates(p_ref[e, rows, :], lbp_v) for e in exs]
            q, v, gl = [g[0] for g in gates], [g[1] for g in gates], [g[2] for g in gates]
            lb, sig, f, k = gates[0][3], [g[4] for g in gates], [g[5] for g in gates], [g[6] for g in gates]
            o = [o_ref[e, rows, :] for e in exs]
            dyv = [dy_ref[e, rows, :] for e in exs]
            sts = [[st_ref[e, h, c] for h in heads] for e in exs]
            dsts = [[dst_ref[e, h] for h in heads] for e in exs]
            e_all = [_split_dot(a_mat, gates[e][7], NN, 3) for e in exs]
            b = [e_all[e][0:cs] for e in exs]
            eb = [jnp.exp(b[e]) for e in exs]
            bl = [b[e][cs - 1:cs, :] for e in exs]
            ebl = [jnp.exp(bl[e]) for e in exs]
            ekd = [jnp.exp(bl[e] - b[e]) for e in exs]
            qb = [q[e] * eb[e] for e in exs]
            kd = [k[e] * ekd[e] for e in exs]
            do, dgl = [], []
            for e in exs:
                sg = jax.nn.sigmoid(gl[e])
                silu = gl[e] * sg
                r = lax.rsqrt(_per_head(_lane_mean, o[e] * o[e]) + EPS)
                dgl.append(dyv[e] * (o[e] * r * ng4) * (sg * (1.0 + gl[e] * (1.0 - sg))))
                u = dyv[e] * silu * ng4
                do.append(r * u - o[e] * (r * r * r) * _per_head(_lane_mean, u * o[e]))
                dng4 = jnp.sum(dyv[e] * silu * o[e] * r, axis=0, keepdims=True)
                dng_acc = dng_acc + ((dng4[:, hl[0]] + dng4[:, hl[1]]) + (dng4[:, hl[2]] + dng4[:, hl[3]]))
            es, qm, km = [], [], []
            p = [[jnp.zeros((cs, cs), F32) for _ in heads] for _ in exs]
            for li in range(nl):
                dec = [jnp.exp(e_all[e][cs * (li + 1):cs * (li + 2)]) for e in exs]
                es.append(dec)
                qm.append([q[e] * dec[e] for e in exs])
                km.append([k[e] * dec[e] for e in exs])
                mk = m_ref[li]
                p = [[p[e][h] + mk * _dot(qm[li][e][:, hl[h]], km[li][e][:, hl[h]], NT) for h in heads] for e in exs]
            dp = [[_dot(do[e][:, hl[h]], v[e][:, hl[h]], NT) for h in heads] for e in exs]
            dv_p = [[_dot(p[e][h], do[e][:, hl[h]], TN) for h in heads] for e in exs]
            dv_s = [[_dot(kd[e][:, hl[h]], dsts[e][h], NT) for h in heads] for e in exs]
            dqb = [side_by_side([_dot(do[e][:, hl[h]], sts[e][h]) for h in heads]) for e in exs]
            dkd = [side_by_side([_dot(v[e][:, hl[h]], dsts[e][h]) for h in heads]) for e in exs]
            new_dst = [[_dot(do[e][:, hl[h]], qb[e][:, hl[h]], TN) for h in heads] for e in exs]
            dv = [side_by_side([dv_p[e][h] + dv_s[e][h] for h in heads]) + _per_head(_lane_sum, q[e] * k[e]) * do[e]
                  for e in exs]
            dq = [dqb[e] * eb[e] for e in exs]
            dk = [dkd[e] * ekd[e] for e in exs]
            de = []
            for e in exs:
                dbl = (jnp.sum(dkd[e] * kd[e], axis=0, keepdims=True)
                       + side_by_side([jnp.sum(dsts[e][h] * sts[e][h], axis=0, keepdims=True) for h in heads]) * ebl[e])
                de.append([dqb[e] * qb[e] - dkd[e] * kd[e] + jnp.where(last_row, dbl, 0.0)])
            for li in range(nl):
                mk = m_ref[li]
                dpm = [[mk * dp[e][h] for h in heads] for e in exs]
                dqm = [side_by_side([_dot(dpm[e][h], km[li][e][:, hl[h]]) for h in heads]) for e in exs]
                dkm = [side_by_side([_dot(dpm[e][h], qm[li][e][:, hl[h]], TN) for h in heads]) for e in exs]
                for e in exs:
                    dq[e] = dq[e] + dqm[e] * es[li][e]
                    dk[e] = dk[e] + dkm[e] * es[li][e]
                    de[e].append(dqm[e] * qm[li][e] + dkm[e] * km[li][e])
            dg = [_split_dot(a_mat, jnp.concatenate(de[e], axis=0), TN, 2) for e in exs]
            for e in exs:
                dpd = _per_head(_lane_sum, do[e] * v[e])
                df = dg[e] / f[e] - (dk[e] + dpd * q[e])
                dp_ref[e, rows, 0:hw] = _mx(dq[e] + dpd * k[e])
                dp_ref[e, rows, hw:2 * hw] = _mx(df * (1.0 - lb) * sig[e] * (1.0 - sig[e]))
                dp_ref[e, rows, 2 * hw:3 * hw] = _mx(dv[e])
                dp_ref[e, rows, 3 * hw:4 * hw] = _mx(dgl[e])
                for h in heads:
                    dst_ref[e, h] = dsts[e][h] * ebl[e][:, hl[h]] + new_dst[e][h]
                dlb_acc = dlb_acc + jnp.sum(df * (1.0 - sig[e]), axis=0, keepdims=True)
            return dlb_acc, dng_acc

        dlb, dng = lax.fori_loop(0, nct, chunk, (jnp.zeros((1, hw), F32), jnp.zeros((1, HG_D), F32)))

        @pl.when(first)
        def _():
            dlb_ref[...] = jnp.zeros_like(dlb_ref)
            dng_ref[...] = jnp.zeros_like(dng_ref)

        mx = jnp.max(lbp_v, axis=0, keepdims=True)
        e = jnp.exp(lbp_v - mx)
        s0 = e[0:1, :] / jnp.sum(e, axis=0, keepdims=True)
        da0 = dlb * s0 * (1.0 - s0)
        dlb_ref[...] += jnp.concatenate([da0, -da0], axis=0)
        dng_ref[...] += dng

    rows3 = lambda w: pl.BlockSpec((bsz, ts, w), lambda s: (0, ns - 1 - s, 0))
    dproj, dlb, dng = pl.pallas_call(
        body, name="hgrn2_bwd", grid=(ns,),
        in_specs=[rows3(HG_COLS),
                  pl.BlockSpec((2, hw), lambda s: (0, 0)),
                  pl.BlockSpec((1, HG_D), lambda s: (0, 0)),
                  pl.BlockSpec(a_all.shape, lambda s: (0, 0)),
                  pl.BlockSpec(masks.shape, lambda s: (0, 0, 0)),
                  rows3(hw),
                  pl.BlockSpec((bsz, HG_HEADS, nct, HG_D, HG_D), lambda s: (0, 0, ns - 1 - s, 0, 0)),
                  rows3(hw)] + [pl.BlockSpec(memory_space=pl.ANY)] * len(after),
        out_specs=(rows3(HG_COLS),
                   pl.BlockSpec((2, hw), lambda s: (0, 0)),
                   pl.BlockSpec((1, HG_D), lambda s: (0, 0))),
        out_shape=(jax.ShapeDtypeStruct((bsz, seq, HG_COLS), _MXU_DTYPE),
                   jax.ShapeDtypeStruct((2, hw), F32),
                   jax.ShapeDtypeStruct((1, HG_D), F32)),
        scratch_shapes=[pltpu.VMEM((bsz, HG_HEADS, HG_D, HG_D), F32)],
        compiler_params=_params(("arbitrary",)),
    )(proj.reshape(bsz, seq, HG_COLS), lbp, ng, a_all, masks, o_all.reshape(bsz, seq, hw), states,
      dy.reshape(bsz, seq, dy.shape[1]), *after)
    return dproj.reshape(t, HG_COLS), dlb, dng


def _sw_constants():
    half = ROT_DIM // 2
    inv = (np.float32(ROPE_THETA) ** (-(np.arange(half, dtype=np.float32) * np.float32(2.0) / np.float32(ROT_DIM)))
           ).astype(np.float32)
    freq = np.zeros((1, 128), np.float32)
    sign = np.zeros((1, 128), np.float32)
    for h in range(2):
        freq[0, 64 * h:64 * h + half] = inv
        freq[0, 64 * h + half:64 * h + 2 * half] = inv
        sign[0, 64 * h:64 * h + half] = -1.0
        sign[0, 64 * h + half:64 * h + 2 * half] = 1.0
    seg = np.kron(np.eye(8, dtype=np.float32), np.full((64, 64), 1.0 / 64.0, np.float32))
    return freq, sign, seg


def _rope_tables(pos, freq, sign):
    ang = pos.astype(F32) * freq
    return jnp.cos(ang), jnp.sin(ang) * sign


def _tile_lanes(v, times):
    return v if times == 1 else jnp.concatenate([v] * times, axis=1)


def _swap_halves(v):
    w = v.shape[1]
    half = ROT_DIM // 2
    lane = lax.broadcasted_iota(jnp.int32, v.shape, 1) % SW_HD
    return jnp.where(lane < half, pltpu.roll(v, w - half, 1), jnp.where(lane < 2 * half, pltpu.roll(v, half, 1), 0.0))


def _sw_norm_rope(tv, gain, seg, cosv, sinv):
    w = tv.shape[1]
    ms = _split_dot_rhs(tv * tv, seg[0:w, 0:w])
    r = lax.rsqrt(ms + EPS)
    tn = tv * r * gain
    reps = w // 128
    return tn * _tile_lanes(cosv, reps) + _swap_halves(tn) * _tile_lanes(sinv, reps), r


def _split_dot_rhs(v, a):
    hi = _mx(v)
    lo = _mx(v - hi.astype(F32))
    return (lax.dot_general(hi, a, (NN, ((), ())), preferred_element_type=F32)
            + lax.dot_general(lo, a, (NN, ((), ())), preferred_element_type=F32))


def _sw_norm_rope_bwd(dt, tv, r, gain, seg, cosv, sinv):
    w = tv.shape[1]
    reps = w // 128
    dtn = dt * _tile_lanes(cosv, reps) + _swap_halves(dt * _tile_lanes(sinv, reps))
    u = dtn * gain
    dtv = r * u - tv * (r * r * r) * _split_dot_rhs(u * tv, seg[0:w, 0:w])
    return dtv, jnp.sum(dtn * tv * r, axis=0, keepdims=True)


def _sw_scores(qh, kp, kc):
    return _dot(qh, kp, NT), _dot(qh, kc, NT)


def _sw_probs(raw, sink, first_block):
    scale = SW_HD ** -0.5
    qi = lax.broadcasted_iota(jnp.int32, (SW_BLOCK, SW_BLOCK), 0)
    kj = lax.broadcasted_iota(jnp.int32, (SW_BLOCK, SW_BLOCK), 1)
    ok_prev = jnp.logical_and(kj > qi, jnp.logical_not(first_block))
    ok_cur = kj <= qi
    sp = jnp.where(ok_prev, raw[0] * scale, -jnp.inf)
    sc = jnp.where(ok_cur, raw[1] * scale, -jnp.inf)
    m = jnp.maximum(jnp.maximum(jnp.max(sp, axis=1, keepdims=True), jnp.max(sc, axis=1, keepdims=True)), sink)
    pp, pc = jnp.exp(sp - m), jnp.exp(sc - m)
    es = jnp.exp(sink - m)
    den = jnp.sum(pp, axis=1, keepdims=True) + jnp.sum(pc, axis=1, keepdims=True) + es
    return pp / den, pc / den, es / den


def _sw_specs(nb):
    def cur(b, n):
        return b * nb + jnp.minimum(n, nb - 1)

    def prev(b, n):
        return b * nb + jnp.maximum(jnp.minimum(n, nb - 1) - 1, 0)

    return cur, prev


def _sw_fwd(proj, pos, qg, kg, sinks, y_in, bsz, seq):
    t = proj.shape[0]
    nb = seq // SW_BLOCK
    freq_np, sign_np, seg_np = _sw_constants()
    freq, sign = jnp.asarray(freq_np), jnp.asarray(sign_np)
    seg = jnp.asarray(seg_np, _MXU_DTYPE)
    cur, prev = _sw_specs(nb)

    def body(q_ref, kc_ref, kp_ref, vc_ref, vp_ref, pc_ref, pp_ref, qg_ref, kg_ref, sk_ref, fr_ref, sn_ref, seg_ref,
             yin_ref, y_ref):
        del yin_ref
        n = pl.program_id(1)
        segv = seg_ref[...]
        cos_c, sin_c = _rope_tables(pc_ref[...], fr_ref[...], sn_ref[...])
        cos_p, sin_p = _rope_tables(pp_ref[...], fr_ref[...], sn_ref[...])
        qr, _ = _sw_norm_rope(q_ref[...], qg_ref[...], segv, cos_c, sin_c)
        kcr, _ = _sw_norm_rope(kc_ref[...], kg_ref[...], segv, cos_c, sin_c)
        kpr, _ = _sw_norm_rope(kp_ref[...], kg_ref[...], segv, cos_p, sin_p)
        vc, vp = vc_ref[...], vp_ref[...]
        ks = [slice(SW_HD * (h // SW_GROUP), SW_HD * (h // SW_GROUP + 1)) for h in range(SW_HEADS)]
        raw = [_sw_scores(qr[:, SW_HD * h:SW_HD * (h + 1)], kpr[:, ks[h]], kcr[:, ks[h]]) for h in range(SW_HEADS)]
        probs = [_sw_probs(raw[h], sk_ref[0, h], n == 0) for h in range(SW_HEADS)]
        for h in range(SW_HEADS):
            y_ref[:, SW_HD * h:SW_HD * (h + 1)] = _dot(probs[h][0], vp[:, ks[h]]) + _dot(probs[h][1], vc[:, ks[h]])

    rowq = pl.BlockSpec((SW_BLOCK, 512), lambda b, n: (cur(b, n), 0))
    full = lambda a: pl.BlockSpec(a.shape, lambda b, n: (0,) * a.ndim)
    yw = y_in.shape[1]
    return pl.pallas_call(
        body, name="swa_fwd", grid=(bsz, nb),
        in_specs=[rowq,
                  pl.BlockSpec((SW_BLOCK, 128), lambda b, n: (cur(b, n), 4)),
                  pl.BlockSpec((SW_BLOCK, 128), lambda b, n: (prev(b, n), 4)),
                  pl.BlockSpec((SW_BLOCK, 128), lambda b, n: (cur(b, n), 5)),
                  pl.BlockSpec((SW_BLOCK, 128), lambda b, n: (prev(b, n), 5)),
                  pl.BlockSpec((SW_BLOCK, 1), lambda b, n: (cur(b, n), 0)),
                  pl.BlockSpec((SW_BLOCK, 1), lambda b, n: (prev(b, n), 0)),
                  full(qg), full(kg),
                  pl.BlockSpec(memory_space=pltpu.SMEM),
                  full(freq), full(sign), full(seg),
                  pl.BlockSpec(memory_space=pl.ANY)],
        out_specs=pl.BlockSpec((SW_BLOCK, 512), lambda b, n: (cur(b, n), 1)),
        out_shape=jax.ShapeDtypeStruct((t, yw), F32),
        input_output_aliases={13: 0},
        compiler_params=_params(("parallel", "parallel")),
    )(proj, proj, proj, proj, proj, pos, pos, qg, kg, sinks, freq, sign, seg, y_in)


def _sw_bwd(proj, pos, qg, kg, sinks, y, dy, bsz, seq):
    t = proj.shape[0]
    nb = seq // SW_BLOCK
    freq_np, sign_np, seg_np = _sw_constants()
    freq, sign = jnp.asarray(freq_np), jnp.asarray(sign_np)
    seg = jnp.asarray(seg_np, _MXU_DTYPE)
    cur, prev = _sw_specs(nb)
    scale = SW_HD ** -0.5

    def body(q_ref, kc_ref, kp_ref, vc_ref, vp_ref, pc_ref, pp_ref, qg_ref, kg_ref, sk_ref, fr_ref, sn_ref, seg_ref,
             y_ref, dy_ref, dp_ref, dqg_ref, dkg_ref, dsk_ref,
             dq_car, dkv_car, dqr_s, dkc_s, dkp_s, dvc_s, dvp_s, gq_acc, gk_acc, sk_acc):
        b, n = pl.program_id(0), pl.program_id(1)
        first = jnp.logical_and(b == 0, n == 0)
        last = jnp.logical_and(b == pl.num_programs(0) - 1, n == nb)

        @pl.when(first)
        def _():
            gq_acc[...] = jnp.zeros_like(gq_acc)
            gk_acc[...] = jnp.zeros_like(gk_acc)
            sk_acc[...] = jnp.zeros_like(sk_acc)

        @pl.when(n < nb)
        def _():
            segv = seg_ref[...]
            cos_c, sin_c = _rope_tables(pc_ref[...], fr_ref[...], sn_ref[...])
            cos_p, sin_p = _rope_tables(pp_ref[...], fr_ref[...], sn_ref[...])
            qv, kcv, kpv = q_ref[...], kc_ref[...], kp_ref[...]
            qr, rq = _sw_norm_rope(qv, qg_ref[...], segv, cos_c, sin_c)
            kcr, rkc = _sw_norm_rope(kcv, kg_ref[...], segv, cos_c, sin_c)
            kpr, rkp = _sw_norm_rope(kpv, kg_ref[...], segv, cos_p, sin_p)
            vc, vp = vc_ref[...], vp_ref[...]
            lane = lax.broadcasted_iota(jnp.int32, (1, 128), 1)
            dsk = jnp.zeros((1, 128), F32)
            heads = range(SW_HEADS)
            ks = [slice(SW_HD * (h // SW_GROUP), SW_HD * (h // SW_GROUP + 1)) for h in heads]
            hs = [slice(SW_HD * h, SW_HD * (h + 1)) for h in heads]
            qh = [qr[:, hs[h]] for h in heads]
            doh = [dy_ref[:, hs[h]] for h in heads]
            raw = [_sw_scores(qh[h], kpr[:, ks[h]], kcr[:, ks[h]]) for h in heads]
            dpp = [_dot(doh[h], vp[:, ks[h]], NT) for h in heads]
            dpc = [_dot(doh[h], vc[:, ks[h]], NT) for h in heads]
            probs = [_sw_probs(raw[h], sk_ref[0, h], n == 0) for h in heads]
            dsp, dsc = [], []
            for h in heads:
                pp, pc, ps = probs[h]
                delta = jnp.sum(doh[h] * y_ref[:, hs[h]], axis=1, keepdims=True)
                dsp.append(pp * (dpp[h] - delta) * scale)
                dsc.append(pc * (dpc[h] - delta) * scale)
                dsk = dsk + jnp.where(lane == h, -jnp.sum(ps * delta), 0.0)
            for h in heads:
                dqr_s[:, hs[h]] = _dot(dsp[h], kpr[:, ks[h]]) + _dot(dsc[h], kcr[:, ks[h]])
            for kv in range(SW_KV_HEADS):
                group = range(SW_GROUP * kv, SW_GROUP * (kv + 1))
                kvs = slice(SW_HD * kv, SW_HD * (kv + 1))
                dvp_s[:, kvs] = sum(_dot(probs[h][0], doh[h], TN) for h in group)
                dvc_s[:, kvs] = sum(_dot(probs[h][1], doh[h], TN) for h in group)
                dkp_s[:, kvs] = sum(_dot(dsp[h], qh[h], TN) for h in group)
                dkc_s[:, kvs] = sum(_dot(dsc[h], qh[h], TN) for h in group)
            dq, gq = _sw_norm_rope_bwd(dqr_s[...], qv, rq, qg_ref[...], segv, cos_c, sin_c)
            dkc, gkc = _sw_norm_rope_bwd(dkc_s[...], kcv, rkc, kg_ref[...], segv, cos_c, sin_c)
            dkp, gkp = _sw_norm_rope_bwd(dkp_s[...], kpv, rkp, kg_ref[...], segv, cos_p, sin_p)
            gq_acc[...] += gq
            gk_acc[...] += gkc + gkp
            sk_acc[...] += dsk

            @pl.when(n > 0)
            def _():
                dp_ref[:, 0:512] = _mx(dq_car[...])
                dp_ref[:, 512:640] = _mx(dkv_car[:, 0:128] + dkp)
                dp_ref[:, 640:768] = _mx(dkv_car[:, 128:256] + dvp_s[...])

            dq_car[...] = dq
            dkv_car[:, 0:128] = dkc
            dkv_car[:, 128:256] = dvc_s[...]

        @pl.when(n == nb)
        def _():
            dp_ref[:, 0:512] = _mx(dq_car[...])
            dp_ref[:, 512:768] = _mx(dkv_car[...])

        @pl.when(last)
        def _():
            gq = gq_acc[...]
            acc = gq[:, 0:SW_HD]
            for h in range(1, SW_HEADS):
                acc = acc + gq[:, SW_HD * h:SW_HD * (h + 1)]
            dqg_ref[...] = acc
            gk = gk_acc[...]
            dkg_ref[...] = gk[:, 0:SW_HD] + gk[:, SW_HD:2 * SW_HD]
            dsk_ref[...] = sk_acc[...]

    rowq = pl.BlockSpec((SW_BLOCK, 512), lambda b, n: (cur(b, n), 0))
    full = lambda a: pl.BlockSpec(a.shape, lambda b, n: (0,) * a.ndim)

    def out_row(b, n):
        return b * nb + jnp.maximum(n - 1, 0)

    return pl.pallas_call(
        body, name="swa_bwd", grid=(bsz, nb + 1),
        in_specs=[rowq,
                  pl.BlockSpec((SW_BLOCK, 128), lambda b, n: (cur(b, n), 4)),
                  pl.BlockSpec((SW_BLOCK, 128), lambda b, n: (prev(b, n), 4)),
                  pl.BlockSpec((SW_BLOCK, 128), lambda b, n: (cur(b, n), 5)),
                  pl.BlockSpec((SW_BLOCK, 128), lambda b, n: (prev(b, n), 5)),
                  pl.BlockSpec((SW_BLOCK, 1), lambda b, n: (cur(b, n), 0)),
                  pl.BlockSpec((SW_BLOCK, 1), lambda b, n: (prev(b, n), 0)),
                  full(qg), full(kg),
                  pl.BlockSpec(memory_space=pltpu.SMEM),
                  full(freq), full(sign), full(seg),
                  pl.BlockSpec((SW_BLOCK, 512), lambda b, n: (cur(b, n), 1)),
                  pl.BlockSpec((SW_BLOCK, 512), lambda b, n: (cur(b, n), 1))],
        out_specs=(pl.BlockSpec((SW_BLOCK, SW_COLS), lambda b, n: (out_row(b, n), 0)),
                   pl.BlockSpec((1, SW_HD), lambda b, n: (0, 0)),
                   pl.BlockSpec((1, SW_HD), lambda b, n: (0, 0)),
                   pl.BlockSpec((1, 128), lambda b, n: (0, 0))),
        out_shape=(jax.ShapeDtypeStruct((t, SW_COLS), _MXU_DTYPE),
                   jax.ShapeDtypeStruct((1, SW_HD), F32),
                   jax.ShapeDtypeStruct((1, SW_HD), F32),
                   jax.ShapeDtypeStruct((1, 128), F32)),
        scratch_shapes=[pltpu.VMEM((SW_BLOCK, 512), F32), pltpu.VMEM((SW_BLOCK, 256), F32),
                        pltpu.VMEM((SW_BLOCK, 512), F32),
                        pltpu.VMEM((SW_BLOCK, 128), F32), pltpu.VMEM((SW_BLOCK, 128), F32),
                        pltpu.VMEM((SW_BLOCK, 128), F32), pltpu.VMEM((SW_BLOCK, 128), F32),
                        pltpu.VMEM((1, 512), F32), pltpu.VMEM((1, 128), F32), pltpu.VMEM((1, 128), F32)],
        compiler_params=_params(("arbitrary", "arbitrary")),
    )(proj, proj, proj, proj, proj, pos, pos, qg, kg, sinks, freq, sign, seg, y, dy)


def _head_rms(tv, gain):
    r = lax.rsqrt(jnp.mean(tv * tv, axis=1, keepdims=True) + EPS)
    return tv * r * gain, r


def _head_rms_bwd(dtn, tv, r, gain):
    u = dtn * gain
    return r * u - tv * (r * r * r) * jnp.mean(u * tv, axis=1, keepdims=True), jnp.sum(dtn * tv * r, axis=0, keepdims=True)


def _xa_softmax(raw):
    s = raw * (XA_HD ** -0.5)
    e = jnp.exp(s - jnp.max(s, axis=1, keepdims=True))
    return e / jnp.sum(e, axis=1, keepdims=True)


def _xa_fwd(qx, kvx, qg, kg, bsz, seq, mlen, *, tq=512):
    t = qx.shape[0]
    tq = min(tq, seq)
    nq = seq // tq
    w = XA_HEADS * XA_HD

    def body(q_ref, kv_ref, qg_ref, kg_ref, o_ref):
        heads = range(XA_HEADS)
        hs = [slice(XA_HD * h, XA_HD * (h + 1)) for h in heads]
        qn = [_head_rms(q_ref[:, hs[h]], qg_ref[...])[0] for h in heads]
        kn = [_head_rms(kv_ref[:, hs[h]], kg_ref[...])[0] for h in heads]
        raw = [_dot(qn[h], kn[h], NT) for h in heads]
        p = [_xa_softmax(raw[h]) for h in heads]
        for h in heads:
            o_ref[:, hs[h]] = _dot(p[h], kv_ref[:, w + XA_HD * h:w + XA_HD * (h + 1)]).astype(o_ref.dtype)

    vec = pl.BlockSpec((1, XA_HD), lambda b, i: (0, 0))
    return pl.pallas_call(
        body, name="xattn_fwd", grid=(bsz, nq),
        in_specs=[pl.BlockSpec((tq, w), lambda b, i: (b * nq + i, 0)),
                  pl.BlockSpec((mlen, 2 * w), lambda b, i: (b, 0)), vec, vec],
        out_specs=pl.BlockSpec((tq, w), lambda b, i: (b * nq + i, 0)),
        out_shape=jax.ShapeDtypeStruct((t, w), _MXU_DTYPE),
        compiler_params=_params(("parallel", "parallel")),
    )(qx, kvx, qg, kg)


def _xa_bwd(qx, kvx, qg, kg, do, bsz, seq, mlen, *, tq=512):
    t = qx.shape[0]
    tq = min(tq, seq)
    nq = seq // tq
    w = XA_HEADS * XA_HD
    scale = XA_HD ** -0.5

    def body(q_ref, kv_ref, qg_ref, kg_ref, do_ref, dq_ref, dkv_ref, dqg_ref, dkg_ref):
        b, i = pl.program_id(0), pl.program_id(1)

        @pl.when(jnp.logical_and(b == 0, i == 0))
        def _():
            dqg_ref[...] = jnp.zeros_like(dqg_ref)
            dkg_ref[...] = jnp.zeros_like(dkg_ref)

        @pl.when(i == 0)
        def _():
            dkv_ref[...] = jnp.zeros_like(dkv_ref)

        heads = range(XA_HEADS)
        hs = [slice(XA_HD * h, XA_HD * (h + 1)) for h in heads]
        vs = [slice(w + XA_HD * h, w + XA_HD * (h + 1)) for h in heads]
        qv = [q_ref[:, hs[h]] for h in heads]
        kv = [kv_ref[:, hs[h]] for h in heads]
        doh = [do_ref[:, hs[h]] for h in heads]
        qn = [_head_rms(qv[h], qg_ref[...]) for h in heads]
        kn = [_head_rms(kv[h], kg_ref[...]) for h in heads]
        raw = [_dot(qn[h][0], kn[h][0], NT) for h in heads]
        dp = [_dot(doh[h], kv_ref[:, vs[h]], NT) for h in heads]
        p = [_xa_softmax(raw[h]) for h in heads]
        ds = [p[h] * (dp[h] - jnp.sum(p[h] * dp[h], axis=1, keepdims=True)) * scale for h in heads]
        dqn = [_dot(ds[h], kn[h][0]) for h in heads]
        dkn = [_dot(ds[h], qn[h][0], TN) for h in heads]
        dvv = [_dot(p[h], doh[h], TN) for h in heads]
        gq_sum = jnp.zeros((1, XA_HD), F32)
        gk_sum = jnp.zeros((1, XA_HD), F32)
        for h in heads:
            dqv, gq = _head_rms_bwd(dqn[h], qv[h], qn[h][1], qg_ref[...])
            dkv, gk = _head_rms_bwd(dkn[h], kv[h], kn[h][1], kg_ref[...])
            dq_ref[:, hs[h]] = dqv.astype(dq_ref.dtype)
            dkv_ref[:, hs[h]] += dkv
            dkv_ref[:, vs[h]] += dvv[h]
            gq_sum = gq_sum + gq
            gk_sum = gk_sum + gk
        dqg_ref[...] += gq_sum
        dkg_ref[...] += gk_sum

    vec = pl.BlockSpec((1, XA_HD), lambda b, i: (0, 0))
    row = pl.BlockSpec((tq, w), lambda b, i: (b * nq + i, 0))
    mem = pl.BlockSpec((mlen, 2 * w), lambda b, i: (b, 0))
    return pl.pallas_call(
        body, name="xattn_bwd", grid=(bsz, nq),
        in_specs=[row, mem, vec, vec, row],
        out_specs=(row, mem, vec, vec),
        out_shape=(jax.ShapeDtypeStruct((t, w), _MXU_DTYPE), jax.ShapeDtypeStruct((bsz * mlen, 2 * w), F32),
                   jax.ShapeDtypeStruct((1, XA_HD), F32), jax.ShapeDtypeStruct((1, XA_HD), F32)),
        compiler_params=_params(("arbitrary", "arbitrary")),
    )(qx, kvx, qg, kg, do)


def _loss_finish(sq_row, d_model):
    def body(s_ref, o_ref):
        o_ref[...] = jnp.zeros_like(o_ref) + 0.5 * jnp.sum(s_ref[...]) / float(d_model)

    return pl.pallas_call(body, name="loss_finish", out_shape=jax.ShapeDtypeStruct((1, 128), F32))(sq_row)


def _adamw_math(w, g, m, v):
    m = ADAM_B1 * m + (1.0 - ADAM_B1) * g
    v = ADAM_B2 * v + (1.0 - ADAM_B2) * (g * g)
    m_hat = m / (1.0 - ADAM_B1 ** ADAM_STEP)
    v_hat = v / (1.0 - ADAM_B2 ** ADAM_STEP)
    return -ADAM_LR * (m_hat / (jnp.sqrt(v_hat) + ADAM_EPS) + ADAM_WD * w), m, v


def _adamw_big(w, g, m, v, *, name, tr=512):
    r, c = w.shape
    tr = min(tr, r)

    def body(w_ref, g_ref, m_ref, v_ref, go_ref, d_ref, mo_ref, vo_ref):
        gv = g_ref[...]
        d, mn, vn = _adamw_math(w_ref[...], gv, m_ref[...], v_ref[...])
        go_ref[...] = gv
        d_ref[...] = d
        mo_ref[...] = mn
        vo_ref[...] = vn

    spec = pl.BlockSpec((tr, c), lambda i: (i, 0))
    shp = jax.ShapeDtypeStruct((r, c), F32)
    return pl.pallas_call(
        body, name=name, grid=(r // tr,), in_specs=[spec] * 4, out_specs=(spec,) * 4, out_shape=(shp,) * 4,
        compiler_params=_params(("parallel",)),
    )(w, g, m, v)


def _adamw_small(ws, gs, ms, vs):
    n = len(ws)

    def body(*refs):
        for i in range(n):
            d, mn, vn = _adamw_math(refs[i][...], refs[n + i][...], refs[2 * n + i][...], refs[3 * n + i][...])
            refs[4 * n + i][...] = d
            refs[5 * n + i][...] = mn
            refs[6 * n + i][...] = vn

    shapes = tuple(jax.ShapeDtypeStruct(w.shape, F32) for w in ws)
    return pl.pallas_call(body, name="adamw_small", out_shape=shapes * 3)(*ws, *gs, *ms, *vs)


def _add_halves(g, recv, c_idx, *, name, tr=512):
    _, r, c = g.shape
    h = r // 2
    tr = min(tr, h)
    nt = h // tr

    def body(c_ref, g_ref, r_ref, o_ref):
        del c_ref
        o_ref[...] = g_ref[...] + r_ref[...]

    return pl.pallas_call(
        body, name=name,
        grid_spec=pltpu.PrefetchScalarGridSpec(
            num_scalar_prefetch=1, grid=(4, nt),
            in_specs=[pl.BlockSpec((None, tr, c), lambda k, i, cr: (k, cr[0] * nt + i, 0)),
                      pl.BlockSpec((None, tr, c), lambda k, i, cr: (k, i, 0))],
            out_specs=pl.BlockSpec((None, tr, c), lambda k, i, cr: (k, i, 0))),
        out_shape=jax.ShapeDtypeStruct((4, h, c), F32),
        compiler_params=_params(("parallel", "parallel")),
    )(c_idx, g, recv)


def _add_chips(p, recv, place_idx, *, name, tr=512):
    _, h, c = p.shape
    tr = min(tr, h)
    nt = h // tr

    def body(pi_ref, p_ref, r_ref, o_ref):
        del pi_ref
        o_ref[...] = ((p_ref[...] + r_ref[0]) + r_ref[1]) + r_ref[2]

    return pl.pallas_call(
        body, name=name,
        grid_spec=pltpu.PrefetchScalarGridSpec(
            num_scalar_prefetch=1, grid=(nt,),
            in_specs=[pl.BlockSpec((None, tr, c), lambda i, pi: (pi[0], i, 0)),
                      pl.BlockSpec((3, tr, c), lambda i, pi: (0, i, 0))],
            out_specs=pl.BlockSpec((tr, c), lambda i, pi: (pi[1] * nt + i, 0))),
        out_shape=jax.ShapeDtypeStruct((2 * h, c), F32),
        compiler_params=_params(("parallel",)),
    )(place_idx, p, recv)


def _place_shard(shard, place_idx, *, name, tr=512, after=()):
    r, c = shard.shape
    tr = min(tr, r)

    def body(pi_ref, s_ref, *rest):
        del pi_ref
        rest[-1][...] = s_ref[...]

    return pl.pallas_call(
        body, name=name,
        grid_spec=pltpu.PrefetchScalarGridSpec(
            num_scalar_prefetch=1, grid=(r // tr,),
            in_specs=[pl.BlockSpec((tr, c), lambda i, pi: (i, 0))] + [pl.BlockSpec(memory_space=pl.ANY)] * len(after),
            out_specs=pl.BlockSpec((None, tr, c), lambda i, pi: (pi[0], i, 0))),
        out_shape=jax.ShapeDtypeStruct((4, r, c), shard.dtype),
        compiler_params=_params(("parallel",)),
    )(place_idx, shard, *after)


def _place():
    x, y, c = lax.axis_index("x"), lax.axis_index("y"), lax.axis_index("c")
    chips = [(1 - x, y), (x, 1 - y), (1 - x, 1 - y)]
    return x, y, c, chips


ANY = pl.BlockSpec(memory_space=pl.ANY)


def _exchange_halves(grads, name):
    n = len(grads)

    def body(*refs):
        ins, outs = refs[:n], refs[n:2 * n]
        send_sems, recv_sems = refs[2 * n:]
        x, y, c, _ = _place()

        def copy(a):
            h = ins[a].shape[1] // 2
            return pltpu.make_async_remote_copy(
                src_ref=ins[a].at[:, pl.ds((1 - c) * h, h), :], dst_ref=outs[a],
                send_sem=send_sems.at[a], recv_sem=recv_sems.at[a], device_id=(x, y, 1 - c), device_id_type=MESH)

        for a in range(n):
            copy(a).start()
        for a in range(n):
            copy(a).wait_recv()
        for a in range(n):
            copy(a).wait_send()

    return pl.pallas_call(
        body, name=name,
        in_specs=[ANY] * n, out_specs=tuple([ANY] * n),
        out_shape=tuple(jax.ShapeDtypeStruct((4, g.shape[1] // 2, g.shape[2]), g.dtype) for g in grads),
        scratch_shapes=[pltpu.SemaphoreType.DMA((n,)), pltpu.SemaphoreType.DMA((n,))],
    )(*grads)


HBM = pl.BlockSpec(memory_space=pltpu.HBM)
SEM = pl.BlockSpec(memory_space=pltpu.SEMAPHORE)
EFFECT = pltpu.SideEffectType.DATAFLOW_SIDE_EFFECTING


def _in_hbm(a):
    return pltpu.with_memory_space_constraint(a, pltpu.HBM)


def _split_copy_calls(name, srcs, lands, n_copies, make_copies):
    ns, nl = len(srcs), len(lands)
    nb = ns + nl

    def start(after=()):
        n_after = len(after)

        def body(*refs):
            outs = refs[nb + n_after:]
            copies = make_copies(refs[:ns], refs[ns:nb], outs[0], outs[1])
            for cp in copies:
                cp.start()
            token = refs[-1]
            token[...] = jnp.zeros_like(token)

        bufs = [_in_hbm(a) for a in list(srcs) + list(lands)]
        out = pl.pallas_call(
            body, name=name + "_start",
            out_shape=(pltpu.SemaphoreType.DMA((n_copies,)), pltpu.SemaphoreType.DMA((n_copies,)),
                       *[pltpu.HBM(a.shape, a.dtype) for a in bufs], jax.ShapeDtypeStruct((8, 128), F32)),
            in_specs=[HBM] * nb + [pl.BlockSpec(memory_space=pl.ANY)] * n_after,
            out_specs=(SEM, SEM, *[HBM] * nb, pl.BlockSpec(memory_space=pltpu.VMEM)),
            input_output_aliases={i: 2 + i for i in range(nb)},
            compiler_params=pltpu.CompilerParams(has_side_effects=EFFECT),
        )(*bufs, *after)
        return dict(send=out[0], recv=out[1], bufs=list(out[2:2 + nb]), token=out[-1])

    def wait(state, after):
        def body(*refs):
            copies = make_copies(refs[:ns], refs[ns:nb], refs[nb], refs[nb + 1])
            for cp in copies:
                cp.wait_send()
            for cp in copies:
                cp.wait_recv()

        bufs = state["bufs"]
        out = pl.pallas_call(
            body, name=name + "_wait",
            out_shape=tuple(pltpu.HBM(a.shape, a.dtype) for a in bufs),
            in_specs=[HBM] * nb + [SEM, SEM] + [pl.BlockSpec(memory_space=pl.ANY)] * len(after),
            out_specs=tuple([HBM] * nb),
            input_output_aliases={i: i for i in range(nb)},
            compiler_params=pltpu.CompilerParams(has_side_effects=EFFECT),
        )(*bufs, state["send"], state["recv"], *after)
        return list(out[:ns]), list(out[ns:])

    return start, wait


def _scatter_chips_split(name, parts):
    n = len(parts)
    lands = [lax.empty((3,) + p.shape[1:], p.dtype) for p in parts]

    def make_copies(srcs, lnds, send_sems, recv_sems):
        _, _, c, chips = _place()
        return [pltpu.make_async_remote_copy(
            src_ref=srcs[a].at[2 * px + py], dst_ref=lnds[a].at[j], send_sem=send_sems.at[a * 3 + j],
            recv_sem=recv_sems.at[a * 3 + j], device_id=(px, py, c), device_id_type=MESH)
            for a in range(n) for j, (px, py) in enumerate(chips)]

    return _split_copy_calls(name, parts, lands, 3 * n, make_copies)


def _exchange_halves_split(name, grads):
    n = len(grads)
    lands = [lax.empty((4, g.shape[1] // 2, g.shape[2]), g.dtype) for g in grads]

    def make_copies(srcs, lnds, send_sems, recv_sems):
        x, y, c, _ = _place()
        out = []
        for a in range(n):
            h = srcs[a].shape[1] // 2
            out.append(pltpu.make_async_remote_copy(
                src_ref=srcs[a].at[:, pl.ds((1 - c) * h, h), :], dst_ref=lnds[a], send_sem=send_sems.at[a],
                recv_sem=recv_sems.at[a], device_id=(x, y, 1 - c), device_id_type=MESH))
        return out

    return _split_copy_calls(name, grads, lands, n, make_copies)


def _gather_chips_split(name, shards, lands):
    n = len(shards)

    def make_copies(srcs, lnds, send_sems, recv_sems):
        x, y, c, chips = _place()
        out = []
        for a in range(n):
            h = srcs[a].shape[0] // 2
            for j, (px, py) in enumerate(chips):
                out.append(pltpu.make_async_remote_copy(
                    src_ref=srcs[a].at[pl.ds(c * h, h), :], dst_ref=lnds[a].at[2 * x + y, pl.ds(c * h, h), :],
                    send_sem=send_sems.at[a * 3 + j], recv_sem=recv_sems.at[a * 3 + j],
                    device_id=(px, py, c), device_id_type=MESH))
        return out

    return _split_copy_calls(name, shards, lands, 3 * n, make_copies)


def _gather_finish(gathered, name):
    n = len(gathered)

    def body(*refs):
        outs = refs[n:2 * n]
        send_sems, recv_sems = refs[2 * n:]
        x, y, c, chips = _place()

        def copy(a, j, chip_idx, which):
            h = outs[a].shape[1] // 2
            rows = outs[a].at[chip_idx, pl.ds(which * h, h), :]
            return pltpu.make_async_remote_copy(
                src_ref=rows, dst_ref=rows, send_sem=send_sems.at[a * 3 + j], recv_sem=recv_sems.at[a * 3 + j],
                device_id=(x, y, 1 - c), device_id_type=MESH)

        for a in range(n):
            for j, (px, py) in enumerate(chips):
                copy(a, j, 2 * px + py, c).start()
        for a in range(n):
            for j, (px, py) in enumerate(chips):
                copy(a, j, 2 * px + py, 1 - c).wait_recv()
        for a in range(n):
            for j, (px, py) in enumerate(chips):
                copy(a, j, 2 * px + py, c).wait_send()

    return pl.pallas_call(
        body, name=name,
        in_specs=[ANY] * n, out_specs=tuple([ANY] * n),
        out_shape=tuple(jax.ShapeDtypeStruct(g.shape, g.dtype) for g in gathered),
        input_output_aliases={i: i for i in range(n)},
        scratch_shapes=[pltpu.SemaphoreType.DMA((3 * n,)), pltpu.SemaphoreType.DMA((3 * n,))],
    )(*gathered)


def _join_halves(fulls):
    n = len(fulls)

    def body(*refs):
        outs = refs[n:2 * n]
        send_sems, recv_sems = refs[2 * n:]
        x, y, c, _ = _place()

        def copy(a, which):
            h = outs[a].shape[0] // 2
            rows = outs[a].at[pl.ds(which * h, h), :]
            return pltpu.make_async_remote_copy(
                src_ref=rows, dst_ref=rows, send_sem=send_sems.at[a], recv_sem=recv_sems.at[a],
                device_id=(x, y, 1 - c), device_id_type=MESH)

        for a in range(n):
            copy(a, c).start()
        for a in range(n):
            copy(a, 1 - c).wait_recv()
        for a in range(n):
            copy(a, c).wait_send()

    return pl.pallas_call(
        body, name="rs_join_halves",
        in_specs=[ANY] * n, out_specs=tuple([ANY] * n),
        out_shape=tuple(jax.ShapeDtypeStruct(p.shape, p.dtype) for p in fulls),
        input_output_aliases={i: i for i in range(n)},
        scratch_shapes=[pltpu.SemaphoreType.DMA((n,)), pltpu.SemaphoreType.DMA((n,))],
    )(*fulls)


def _all_reduce_small(sm):
    r, w = sm.shape

    def body(sm_ref, o_ref, buf, send_sems, recv_sems):
        x, y, c, _ = _place()
        me = 4 * x + 2 * y + c
        buf[me] = sm_ref[...]
        rel = [(dx, dy, dc) for dx in (0, 1) for dy in (0, 1) for dc in (0, 1)][1:]

        def copy(k, slot, to):
            return pltpu.make_async_remote_copy(
                src_ref=sm_ref, dst_ref=buf.at[slot], send_sem=send_sems.at[k], recv_sem=recv_sems.at[k],
                device_id=to, device_id_type=MESH)

        peers = []
        for k, (dx, dy, dc) in enumerate(rel):
            px = 1 - x if dx else x
            py = 1 - y if dy else y
            pc = 1 - c if dc else c
            peers.append((px, py, pc))
            copy(k, me, (px, py, pc)).start()
        for k, (px, py, pc) in enumerate(peers):
            copy(k, 4 * px + 2 * py + pc, (px, py, pc)).wait_recv()
        for k, (px, py, pc) in enumerate(peers):
            copy(k, me, (px, py, pc)).wait_send()
        acc = buf[0]
        for d in range(1, 8):
            acc = acc + buf[d]
        o_ref[...] = acc

    vm = pl.BlockSpec(memory_space=pltpu.VMEM)
    return pl.pallas_call(
        body, name="all_reduce_small", in_specs=[vm], out_specs=vm,
        out_shape=jax.ShapeDtypeStruct((r, w), F32),
        scratch_shapes=[pltpu.VMEM((8, r, w), F32), pltpu.SemaphoreType.DMA((7,)), pltpu.SemaphoreType.DMA((7,))],
    )(sm)


class _LocalWeights:
    def __init__(self, w):
        self.w = w
        self.g = {}

    def begin(self):
        pass

    def first(self, after):
        del after
        return self.w

    def rest(self, after):
        del after
        return self.w

    def grads(self, tag, g):
        del tag
        self.g.update(g)
        return ()

    def poll(self, after):
        del after
        return ()


def _local_step(x3, mem3, pos2, target3, small, comm):
    bsz, seq, d = x3.shape
    mlen = mem3.shape[1]
    t = bsz * seq
    comm.begin()
    x = x3.reshape(t, d)
    mem = mem3.reshape(bsz * mlen, d)
    target = target3.reshape(t, d)
    pos = pos2.reshape(t, 1)
    qg_t = jnp.tile(small["sw_q_norm_g"], (1, SW_HEADS))
    kg_t = jnp.tile(small["sw_k_norm_g"], (1, SW_KV_HEADS))

    hn1 = _rms_fwd(x, small["norm1_g"], name="rms1_fwd")
    w = comm.first(hn1)
    proj_hg = _mm(hn1, w["w_in_hg"], NN, t, HG_COLS, d, name="proj_hg", tk=d, after=(w.get("token"),))[0]
    proj_sw = _mm(hn1, w["w_in_sw"], NN, t, SW_COLS, d, name="proj_sw", tk=d)[0]
    y_mix, o_hg, states = _hg_fwd(proj_hg, small["hg_lower_bounds"], small["hg_norm_g"], bsz, seq, y_width=1024)
    y_mix = _sw_fwd(proj_sw, pos, qg_t, kg_t, small["sw_sinks"], y_mix, bsz, seq)
    w_in_hg, w_in_sw = w["w_in_hg"], w["w_in_sw"]
    w = comm.rest(y_mix)
    ff = w["down"].shape[0]
    ffs = ff // 4
    h1, hn2 = _mm(y_mix, w["w_out"], NN, t, d, 1024, name="out_proj", tk=1024, extras=(x,), rows=(small["norm2_g"],),
                  epilogue=_residual_rms, out_dtypes=(F32, _MXU_DTYPE))
    mn = _rms_fwd(mem, small["mem_norm_g"], name="rms_mem_fwd")
    qx = _mm(hn2, w["wq"], NN, t, 512, d, name="xa_q", tk=d)[0]
    kvx = _mm(mn, w["wkv"], NN, bsz * mlen, 1024, d, name="xa_kv", tk=d)[0]
    ox = _xa_fwd(qx, kvx, small["xa_q_norm_g"], small["xa_k_norm_g"], bsz, seq, mlen)
    h2, hn3 = _mm(ox, w["wo"], NN, t, d, 512, name="xa_o", tk=512, extras=(h1,), rows=(small["norm3_g"],),
                  epilogue=_residual_rms, out_dtypes=(F32, _MXU_DTYPE))

    def relu_sq(acc):
        a = jnp.maximum(acc, 0.0)
        return a, a * a

    act, act2 = _mm(hn3, w["up"], NN, t, ff, d, name="mlp_up", tm=2048, tn=ffs, tk=d,
                    b_spec=pl.BlockSpec((None, d, ffs), lambda i, j, kk: (j, 0, 0)),
                    epilogue=relu_sq, out_dtypes=(_MXU_DTYPE, _MXU_DTYPE))
    inv_d = 1.0 / d

    def loss_cotangent(acc, res, tgt):
        diff = acc + res - tgt
        v = diff * inv_d
        return v, v, jnp.sum(diff * diff, axis=0, keepdims=True)

    dy, dy_mx, sq_row = _mm(act2, w["down"], NN, t, d, ff, name="mlp_down", extras=(h2, target),
                            epilogue=loss_cotangent, out_dtypes=(F32, _MXU_DTYPE), row_sums=1)
    loss_row = _loss_finish(sq_row, d)

    dz = _mm(dy_mx, w["down"], NT, t, ff, d, name="d_act", tm=2048, tk=d, extras=(act,),
             epilogue=lambda acc, a: (acc * (2.0 * a.astype(F32)),), out_dtypes=(_MXU_DTYPE,))[0]
    g_down = _mm(act2, dy_mx, TN, ff, d, t, name="g_down")[0]
    g_up = _mm(hn3, dz, TN, d, ff, t, name="g_up", tn=ffs,
               out_shape=(jax.ShapeDtypeStruct((4, d, ffs), F32),),
               out_spec=(pl.BlockSpec((None, min(1024, d), ffs), lambda i, j, kk: (j, i, 0)),))[0]
    tok = comm.grads("mlp", dict(up=g_up, down=g_down))
    dh2, dh2_mx, g_norm3 = _mm(dz, w["up"], NT, t, d, ff, name="d_hn3", tk=ffs, after=tok,
                               b_spec=pl.BlockSpec((None, min(1024, d), ffs), lambda i, j, kk: (kk, j, 0)),
                               extras=(h2, dy), rows=(small["norm3_g"],), epilogue=_rms_bwd_residual,
                               out_dtypes=(F32, _MXU_DTYPE), row_sums=1)
    d_ox = _mm(dh2_mx, w["wo"], NT, t, 512, d, name="d_ox", tk=d)[0]
    g_wo = _mm(ox, dh2_mx, TN, 512, d, t, name="g_wo")[0]
    d_qx, d_kvx, g_xq, g_xk = _xa_bwd(qx, kvx, small["xa_q_norm_g"], small["xa_k_norm_g"], d_ox, bsz, seq, mlen)
    g_wq = _mm(hn2, d_qx, TN, d, 512, t, name="g_wq")[0]
    g_wkv = _mm(mn, d_kvx, TN, d, 1024, bsz * mlen, name="g_wkv")[0]
    dh1, dh1_mx, g_norm2 = _mm(d_qx, w["wq"], NT, t, d, 512, name="d_hn2", tk=512, extras=(h1, dh2),
                               rows=(small["norm2_g"],), epilogue=_rms_bwd_residual, out_dtypes=(F32, _MXU_DTYPE),
                               row_sums=1)
    dmn = _mm(d_kvx, w["wkv"], NT, bsz * mlen, d, 1024, name="d_mn", tk=1024)[0]
    g_memn = _rms_gain_grad(mem, small["mem_norm_g"], dmn, name="rms_mem_bwd")
    g_wout = _mm(y_mix, dh1_mx, TN, 1024, d, t, name="g_wout")[0]
    tok = comm.grads("mid", dict(w_out=g_wout, wq=g_wq, wkv=g_wkv, wo=g_wo))
    d_mix = _mm(dh1_mx, w["w_out"], NT, t, 1024, d, name="d_mix", tk=d, after=tok)[0]
    dproj_sw, g_swq, g_swk, g_sinks = _sw_bwd(proj_sw, pos, qg_t, kg_t, small["sw_sinks"], y_mix, d_mix, bsz, seq)
    tok = comm.poll(dproj_sw)
    dproj_hg, g_lb, g_hgn = _hg_bwd(proj_hg, small["hg_lower_bounds"], small["hg_norm_g"], o_hg, states, d_mix, bsz, seq,
                                    after=tok)
    g_in_hg = _mm(hn1, dproj_hg, TN, d, HG_COLS, t, name="g_in_hg")[0]
    g_in_sw = _mm(hn1, dproj_sw, TN, d, SW_COLS, t, name="g_in_sw")[0]
    tok = comm.grads("in", dict(w_in_hg=g_in_hg, w_in_sw=g_in_sw))
    dhn1_a = _mm(dproj_hg, w_in_hg, NT, t, d, HG_COLS, name="d_hn1_hg", tk=1024, after=tok)[0]
    grad_x, g_norm1 = _mm(dproj_sw, w_in_sw, NT, t, d, SW_COLS, name="d_hn1_sw", tk=SW_COLS, extras=(dhn1_a, x, dh1),
                          rows=(small["norm1_g"],), row_sums=1,
                          epilogue=lambda acc, prev, xv, dres, g: _rms_bwd_residual(acc + prev, xv, dres, g)[1:])

    g_small = dict(norm1_g=g_norm1, hg_lower_bounds=g_lb, hg_norm_g=g_hgn, sw_q_norm_g=g_swq, sw_k_norm_g=g_swk,
                   sw_sinks=g_sinks[:, 0:SW_HEADS], norm2_g=g_norm2, mem_norm_g=g_memn, xa_q_norm_g=g_xq,
                   xa_k_norm_g=g_xk, norm3_g=g_norm3)
    return loss_row, grad_x.reshape(bsz, seq, d), g_small


SMALL_NAMES = ("norm1_g", "hg_lower_bounds", "hg_norm_g", "sw_q_norm_g", "sw_k_norm_g", "sw_sinks", "norm2_g",
               "mem_norm_g", "xa_q_norm_g", "xa_k_norm_g", "norm3_g")
BIG_NAMES = ("w_in", "w_out", "xa_wq", "xa_wkv", "xa_wo", "mlp_up", "mlp_down")
WEIGHT_ORDER = ("norm1_g", "w_in", "hg_lower_bounds", "hg_norm_g", "sw_q_norm_g", "sw_k_norm_g", "sw_sinks", "w_out",
                "norm2_g", "mem_norm_g", "xa_wq", "xa_wkv", "xa_q_norm_g", "xa_k_norm_g", "xa_wo", "norm3_g",
                "mlp_up", "mlp_down")


def _pack_rows(vals, width):
    starts, at = [], 0
    for v in vals:
        starts.append(at)
        at += v.shape[0]
    total = at + (-at) % 8
    out = None
    for v, s in zip(vals, starts):
        placed = jnp.pad(v, ((s, total - s - v.shape[0]), (0, width - v.shape[1])))
        out = placed if out is None else out + placed
    return out, starts


class _MeshWeights:
    LATE = ("w_out", "xa_wq", "xa_wkv", "xa_wo", "mlp_up", "mlp_down")

    def __init__(self, shards, d, ff):
        self.shards, self.d, self.ff = shards, d, ff
        self.c_idx = lax.axis_index("c").astype(jnp.int32).reshape(1)
        chip = (2 * lax.axis_index("x") + lax.axis_index("y")).astype(jnp.int32)
        self.place_idx = jnp.stack([chip, lax.axis_index("c").astype(jnp.int32)])
        self.pending = []
        self.exchanging = None
        self.halves = {}

    def begin(self):
        shard = self.shards["w_in"]
        start, self.in_wait = _gather_chips_split(
            "gather_in", [shard], [_place_shard(shard, self.place_idx, name="place_w_in")])
        self.in_state = start()
        tok = (self.in_state["token"],)
        self.placed = [_place_shard(self.shards[n], self.place_idx, name="place_" + n, after=tok) for n in self.LATE]

    def first(self, after):
        _, lands = self.in_wait(self.in_state, (after, *self.placed))
        (g_in,) = _gather_finish(lands, "gather_in_finish")
        start, self.late_wait = _gather_chips_split("gather_late", [self.shards[n] for n in self.LATE], self.placed)
        self.late_state = start(after=(g_in,))
        ws = g_in.shape[2]
        cut = HG_COLS - 2 * ws
        return dict(w_in_hg=jnp.concatenate([g_in[0], g_in[1], g_in[2][:, :cut]], axis=1),
                    w_in_sw=jnp.concatenate([g_in[2][:, cut:], g_in[3]], axis=1), token=self.late_state["token"])

    def rest(self, after):
        _, lands = self.late_wait(self.late_state, (after,))
        g_out, g_q, g_kv, g_o, g_up, g_dn = _gather_finish(lands, "gather_late_finish")
        d = self.d
        return dict(w_out=g_out.reshape(-1, d), wq=g_q.reshape(d, -1), wkv=g_kv.reshape(d, -1),
                    wo=jnp.concatenate([g_o[k] for k in range(4)], axis=1), up=g_up, down=g_dn.reshape(self.ff, d))

    def _scatter(self, tag, names, arrays, recv):
        parts = [_add_halves(g, r, self.c_idx, name="rs_add_halves_" + n) for n, g, r in zip(names, arrays, recv)]
        start, wait = _scatter_chips_split("rs_scatter_" + tag, parts)
        state = start()
        self.pending.append((names, wait, state))
        return state["token"]

    def _advance(self, after):
        if self.exchanging is None:
            return ()
        tag, names, wait, state = self.exchanging
        self.exchanging = None
        arrays, recv = wait(state, (after,))
        return (self._scatter(tag, names, arrays, recv),)

    def poll(self, after):
        return self._advance(after)

    def grads(self, tag, g):
        d, ff = self.d, self.ff
        if tag == "mlp":
            names, arrays = ("mlp_up", "mlp_down"), [g["up"], g["down"].reshape(4, ff // 4, d)]
        elif tag == "mid":
            names = ("w_out", "xa_wq", "xa_wkv", "xa_wo")
            ds = d // 4
            g_wo = jnp.stack([g["wo"][:, ds * k:ds * (k + 1)] for k in range(4)])
            arrays = [g["w_out"].reshape(4, -1, d), g["wq"].reshape(4, d // 4, -1), g["wkv"].reshape(4, d // 4, -1), g_wo]
        else:
            hg, sw = g["w_in_hg"], g["w_in_sw"]
            ws = (hg.shape[1] + sw.shape[1]) // 4
            cut = hg.shape[1] - 2 * ws
            names = ("w_in",)
            arrays = [jnp.stack([hg[:, :ws], hg[:, ws:2 * ws], jnp.concatenate([hg[:, 2 * ws:], sw[:, :ws - cut]], axis=1),
                                 sw[:, ws - cut:]])]
        toks = self._advance(arrays[0])
        if tag == "in":
            return toks + (self._scatter(tag, names, arrays, _exchange_halves(arrays, "rs_exchange_" + tag)),)
        start, wait = _exchange_halves_split("rs_exchange_" + tag, arrays)
        state = start()
        self.exchanging = (tag, names, wait, state)
        return toks + (state["token"],)

    def finish(self, after):
        for names, wait, state in self.pending:
            srcs, lands = wait(state, (after,))
            for n, p, r in zip(names, srcs, lands):
                self.halves[n] = _add_chips(p, r, self.place_idx, name="rs_add_chips_" + n)
        return dict(zip(BIG_NAMES, _join_halves([self.halves[n] for n in BIG_NAMES])))


def kernel(x, mem, positions, norm1_g, w_in, hg_lower_bounds, hg_norm_g, sw_q_norm_g, sw_k_norm_g, sw_sinks, w_out, norm2_g, mem_norm_g, xa_wq, xa_wkv, xa_q_norm_g, xa_k_norm_g, xa_wo, norm3_g, mlp_up, mlp_down, loss_target, m_norm1_g, m_w_in, m_hg_lower_bounds, m_hg_norm_g, m_sw_q_norm_g, m_sw_k_norm_g, m_sw_sinks, m_w_out, m_norm2_g, m_mem_norm_g, m_xa_wq, m_xa_wkv, m_xa_q_norm_g, m_xa_k_norm_g, m_xa_wo, m_norm3_g, m_mlp_up, m_mlp_down, v_norm1_g, v_w_in, v_hg_lower_bounds, v_hg_norm_g, v_sw_q_norm_g, v_sw_k_norm_g, v_sw_sinks, v_w_out, v_norm2_g, v_mem_norm_g, v_xa_wq, v_xa_wkv, v_xa_q_norm_g, v_xa_k_norm_g, v_xa_wo, v_norm3_g, v_mlp_up, v_mlp_down):
    given = dict(locals())
    weights = {n: given[n] for n in WEIGHT_ORDER}
    moms = {n: given["m_" + n] for n in WEIGHT_ORDER}
    vars_ = {n: given["v_" + n] for n in WEIGHT_ORDER}
    d = x.shape[-1]
    ff = mlp_down.shape[1] * 4
    small = {n: weights[n] for n in SMALL_NAMES}

    comm = _MeshWeights({n: weights[n][0].astype(_MXU_DTYPE) for n in BIG_NAMES}, d, ff)
    loss_row, grad_x, g_small = _local_step(x, mem, positions, loss_target, small, comm)
    big_grads = comm.finish(grad_x)

    packed, starts = _pack_rows([g_small[n] for n in SMALL_NAMES] + [loss_row], 1024)
    summed = _all_reduce_small(packed)
    small_grads = {}
    for n, s in zip(SMALL_NAMES, starts):
        r, c = weights[n].shape
        small_grads[n] = summed[s:s + r, 0:c]
    loss = summed[starts[-1], 0]

    grads, deltas, new_m, new_v = {}, {}, {}, {}
    for n in BIG_NAMES:
        shp = weights[n].shape
        g2, dl, mo, vo = _adamw_big(weights[n][0], big_grads[n], moms[n][0], vars_[n][0], name="adamw_" + n)
        grads[n], deltas[n], new_m[n], new_v[n] = (a.reshape(shp) for a in (g2, dl, mo, vo))
    sm_out = _adamw_small([weights[n] for n in SMALL_NAMES], [small_grads[n] for n in SMALL_NAMES],
                          [moms[n] for n in SMALL_NAMES], [vars_[n] for n in SMALL_NAMES])
    ns = len(SMALL_NAMES)
    for i, n in enumerate(SMALL_NAMES):
        grads[n], deltas[n], new_m[n], new_v[n] = small_grads[n], sm_out[i], sm_out[ns + i], sm_out[2 * ns + i]

    return (loss, grad_x, *[grads[n] for n in WEIGHT_ORDER], *[deltas[n] for n in WEIGHT_ORDER],
            *[new_m[n] for n in WEIGHT_ORDER], *[new_v[n] for n in WEIGHT_ORDER])
```

```python
import numpy as np
import jax
import jax.numpy as jnp
from jax import lax
from jax.experimental import pallas as pl
from jax.experimental.pallas import tpu as pltpu

F32 = jnp.float32
_MXU_DTYPE = jnp.bfloat16

EPS = 1e-6
HG_HEADS = 4
HG_D = 128
HG_CHUNK = 64
HG_TILE = 512
HG_LEVELS = (32, 16, 8, 4, 2, 1)
SW_HEADS = 8
SW_KV_HEADS = 2
SW_GROUP = SW_HEADS // SW_KV_HEADS
SW_HD = 64
SW_BLOCK = 128
ROPE_THETA = 500000.0
ROT_DIM = SW_HD // 4
XA_HEADS = 4
XA_HD = 128
HG_COLS = 4 * HG_HEADS * HG_D
SW_COLS = (SW_HEADS + 2 * SW_KV_HEADS) * SW_HD

ADAM_LR = 0.001
ADAM_B1 = 0.9
ADAM_B2 = 0.999
ADAM_EPS = 1e-08
ADAM_WD = 0.01
ADAM_STEP = 10

VMEM_LIMIT = 56 * 1024 * 1024
MESH = pl.DeviceIdType.MESH

NN = ((1,), (0,))
NT = ((1,), (1,))
TN = ((0,), (0,))


def _mx(v):
    return v.astype(_MXU_DTYPE)


def _dot(a, b, dims=NN):
    return lax.dot_general(_mx(a), _mx(b), (dims, ((), ())), preferred_element_type=F32)


def _split_dot(a, v, dims, parts):
    acc = None
    rest = v
    for p in range(parts):
        piece = _mx(rest)
        term = lax.dot_general(a, piece, (dims, ((), ())), preferred_element_type=F32)
        acc = term if acc is None else acc + term
        if p + 1 < parts:
            rest = rest - piece.astype(F32)
    return acc


def _params(sem):
    return pltpu.CompilerParams(dimension_semantics=sem, vmem_limit_bytes=VMEM_LIMIT)


def _mm(a, b, mode, m, n, k, *, name, tm=1024, tn=1024, tk=1024, a_spec=None, b_spec=None, extras=(), rows=(),
        epilogue=None, out_dtypes=(F32,), row_sums=0, out_shape=None, out_spec=None, after=()):
    after = tuple(t for t in after if t is not None)
    tm, tn, tk = min(tm, m), min(tn, n), min(tk, k)
    assert m % tm == 0 and n % tn == 0 and k % tk == 0, (name, m, n, k, tm, tn, tk)
    gi, gj, gk = m // tm, n // tn, k // tk
    assert row_sums == 0 or gj == 1, name
    if a_spec is None:
        a_spec = (pl.BlockSpec((tk, tm), lambda i, j, kk: (kk, i)) if mode == TN
                  else pl.BlockSpec((tm, tk), lambda i, j, kk: (i, kk)))
    if b_spec is None:
        b_spec = (pl.BlockSpec((tn, tk), lambda i, j, kk: (j, kk)) if mode == NT
                  else pl.BlockSpec((tk, tn), lambda i, j, kk: (kk, j)))
    mn_spec = pl.BlockSpec((tm, tn), lambda i, j, kk: (i, j))
    if epilogue is None:
        epilogue = lambda acc: (acc,)
    row_spec = pl.BlockSpec((1, tn), lambda i, j, kk: (0, j))
    n_ex, n_out = len(extras) + len(rows), len(out_dtypes)
    if out_shape is None:
        out_shape = tuple(jax.ShapeDtypeStruct((m, n), d) for d in out_dtypes)
        out_spec = tuple(mn_spec for _ in out_dtypes)
    out_shape = tuple(out_shape) + tuple(jax.ShapeDtypeStruct((1, n), F32) for _ in range(row_sums))
    out_spec = tuple(out_spec) + tuple(row_spec for _ in range(row_sums))

    n_after = len(after)

    def body(*refs):
        a_ref, b_ref = refs[0], refs[1]
        ex = refs[2:2 + n_ex]
        outs = refs[2 + n_ex + n_after:2 + n_ex + n_after + n_out + row_sums]
        first_row_tile = pl.program_id(0) == 0

        def finish(acc):
            res = epilogue(acc, *[e[...] for e in ex])
            for o, r in zip(outs[:n_out], res[:n_out]):
                o[...] = r.astype(o.dtype)
            if row_sums:
                @pl.when(first_row_tile)
                def _():
                    for o in outs[n_out:]:
                        o[...] = jnp.zeros_like(o)

                for o, r in zip(outs[n_out:], res[n_out:]):
                    o[...] += r

        if gk == 1:
            finish(_dot(a_ref[...], b_ref[...], mode))
        else:
            acc_ref = refs[-1]
            kk = pl.program_id(2)

            @pl.when(kk == 0)
            def _():
                acc_ref[...] = jnp.zeros_like(acc_ref)

            acc_ref[...] += _dot(a_ref[...], b_ref[...], mode)

            @pl.when(kk == gk - 1)
            def _():
                finish(acc_ref[...])

    return pl.pallas_call(
        body, name=name, grid=(gi, gj, gk),
        in_specs=([a_spec, b_spec] + [mn_spec] * len(extras) + [row_spec] * len(rows)
                  + [pl.BlockSpec(memory_space=pl.ANY)] * n_after),
        out_specs=out_spec, out_shape=out_shape,
        scratch_shapes=[pltpu.VMEM((tm, tn), F32)] if gk > 1 else [],
        compiler_params=_params(("arbitrary" if row_sums else "parallel", "parallel", "arbitrary")),
    )(a, b, *extras, *rows, *after)


def _rms_rows(xv, g):
    return xv * lax.rsqrt(jnp.mean(xv * xv, axis=1, keepdims=True) + EPS) * g


def _rms_rows_bwd(xv, g, dyv):
    r = lax.rsqrt(jnp.mean(xv * xv, axis=1, keepdims=True) + EPS)
    u = dyv * g
    return (r * u - xv * (r * r * r) * jnp.mean(u * xv, axis=1, keepdims=True),
            jnp.sum(dyv * xv * r, axis=0, keepdims=True))


def _residual_rms(acc, res, g):
    h = acc + res
    return h, _rms_rows(h, g)


def _rms_bwd_residual(dhn, xv, dres, g):
    dx, dg = _rms_rows_bwd(xv, g, dhn)
    dx = dx + dres
    return dx, dx, dg


def _rms_fwd(x, g, *, name, tm=512, after=()):
    t, d = x.shape
    tm = min(tm, t)
    after = tuple(a for a in after if a is not None)

    def body(x_ref, g_ref, *rest):
        rest[-1][...] = _rms_rows(x_ref[...], g_ref[...]).astype(rest[-1].dtype)

    return pl.pallas_call(
        body, name=name, grid=(t // tm,),
        in_specs=[pl.BlockSpec((tm, d), lambda i: (i, 0)), pl.BlockSpec((1, d), lambda i: (0, 0))]
        + [pl.BlockSpec(memory_space=pl.ANY)] * len(after),
        out_specs=pl.BlockSpec((tm, d), lambda i: (i, 0)),
        out_shape=jax.ShapeDtypeStruct((t, d), _MXU_DTYPE),
        compiler_params=_params(("parallel",)),
    )(x, g, *after)


def _rms_gain_grad(x, g, dy, *, name, tm=512):
    t, d = x.shape
    tm = min(tm, t)

    def body(x_ref, g_ref, dy_ref, dg_ref):
        @pl.when(pl.program_id(0) == 0)
        def _():
            dg_ref[...] = jnp.zeros_like(dg_ref)

        dg_ref[...] += _rms_rows_bwd(x_ref[...], g_ref[...], dy_ref[...])[1]

    row = pl.BlockSpec((tm, d), lambda i: (i, 0))
    vec = pl.BlockSpec((1, d), lambda i: (0, 0))
    return pl.pallas_call(
        body, name=name, grid=(t // tm,), in_specs=[row, vec, row], out_specs=vec,
        out_shape=jax.ShapeDtypeStruct((1, d), F32), compiler_params=_params(("arbitrary",)),
    )(x, g, dy)


def _hg_constants():
    c = HG_CHUNK
    t = np.arange(c)
    sums = [t[None, :] <= t[:, None]]
    masks = []
    for m in HG_LEVELS:
        base = (t // (2 * m)) * (2 * m)
        mid = base + m - 1
        second = (t - base) >= m
        upper = (t[None, :] > mid[:, None]) & (t[None, :] <= t[:, None])
        lower = (t[None, :] > t[:, None]) & (t[None, :] <= mid[:, None])
        sums.append(np.where(second[:, None], upper, lower))
        masks.append(second[:, None] & (~second)[None, :] & (base[:, None] == base[None, :]))
    return (np.concatenate(sums, axis=0).astype(np.float32), np.stack(masks).astype(np.float32))


HG_HEAD_LANES = tuple(slice(HG_D * h, HG_D * (h + 1)) for h in range(HG_HEADS))


def _per_head(fn, slab):
    return jnp.concatenate([jnp.broadcast_to(fn(slab[:, hs]), (slab.shape[0], HG_D)) for hs in HG_HEAD_LANES], axis=1)


def _lane_sum(v):
    return jnp.sum(v, axis=1, keepdims=True)


def _lane_mean(v):
    return jnp.mean(v, axis=1, keepdims=True)


def _hg_gates(blk, lbp):
    w = HG_HEADS * HG_D
    q, x, v, gl = blk[:, 0:w], blk[:, w:2 * w], blk[:, 2 * w:3 * w], blk[:, 3 * w:4 * w]
    mx = jnp.max(lbp, axis=0, keepdims=True)
    e = jnp.exp(lbp - mx)
    lb = e[0:1, :] / jnp.sum(e, axis=0, keepdims=True)
    sig = jax.nn.sigmoid(x)
    f = lb + (1.0 - lb) * sig
    return q, v, gl, lb, sig, f, 1.0 - f, jnp.log(f)


def _hg_fwd(proj, lbp, ng, bsz, seq, *, y_width):
    t = proj.shape[0]
    nc = seq // HG_CHUNK
    a_np, m_np = _hg_constants()
    a_all = jnp.asarray(a_np, _MXU_DTYPE)
    masks = jnp.asarray(m_np, F32)
    nl = len(HG_LEVELS)

    ts = min(HG_TILE, seq)
    ns, nct = seq // ts, ts // HG_CHUNK
    hw = HG_HEADS * HG_D

    def body(p_ref, lb_ref, ng_ref, a_ref, m_ref, y_ref, o_ref, st_ref, carry):
        a_mat = a_ref[...]
        ngv = ng_ref[...]

        @pl.when(pl.program_id(0) == 0)
        def _():
            carry[...] = jnp.zeros_like(carry)

        ng4 = _tile_lanes(ngv, HG_HEADS)
        heads = range(HG_HEADS)
        exs = range(bsz)
        hl = HG_HEAD_LANES
        lbp_v = lb_ref[...]

        def chunk(c, _):
            rows = pl.ds(pl.multiple_of(c * HG_CHUNK, HG_CHUNK), HG_CHUNK)
            gates = [_hg_gates(p_ref[e, rows, :], lbp_v) for e in exs]
            q, v, gl = [g[0] for g in gates], [g[1] for g in gates], [g[2] for g in gates]
            k = [g[6] for g in gates]
            sts = [[carry[e, h] for h in heads] for e in exs]
            e_all = [_split_dot(a_mat, gates[e][7], NN, 3) for e in exs]
            b = [e_all[e][0:HG_CHUNK] for e in exs]
            qb = [q[e] * jnp.exp(b[e]) for e in exs]
            o = [[_dot(qb[e][:, hl[h]], sts[e][h], NT) for h in heads] for e in exs]
            p = [[jnp.zeros((HG_CHUNK, HG_CHUNK), F32) for _ in heads] for _ in exs]
            for li in range(nl):
                dec = [jnp.exp(e_all[e][HG_CHUNK * (li + 1):HG_CHUNK * (li + 2)]) for e in exs]
                qm, km, mk = [q[e] * dec[e] for e in exs], [k[e] * dec[e] for e in exs], m_ref[li]
                p = [[p[e][h] + mk * _dot(qm[e][:, hl[h]], km[e][:, hl[h]], NT) for h in heads] for e in exs]
            bl = [b[e][HG_CHUNK - 1:HG_CHUNK, :] for e in exs]
            kd = [k[e] * jnp.exp(bl[e] - b[e]) for e in exs]
            pv = [[_dot(p[e][h], v[e][:, hl[h]]) for h in heads] for e in exs]
            upd = [[_dot(v[e][:, hl[h]], kd[e][:, hl[h]], TN) for h in heads] for e in exs]
            for e in exs:
                o_all = (jnp.concatenate([o[e][h] + pv[e][h] for h in heads], axis=1)
                         + _per_head(_lane_sum, q[e] * k[e]) * v[e])
                r = lax.rsqrt(_per_head(_lane_mean, o_all * o_all) + EPS)
                ebl = jnp.exp(bl[e])
                for h in heads:
                    st_ref[e, h, c] = sts[e][h]
                    carry[e, h] = sts[e][h] * ebl[:, hl[h]] + upd[e][h]
                o_ref[e, rows, :] = o_all
                y_ref[e, rows, :] = (o_all * r * ng4) * (gl[e] * jax.nn.sigmoid(gl[e]))
            return 0

        lax.fori_loop(0, nct, chunk, 0)

    y3, o3, states = pl.pallas_call(
        body, name="hgrn2_fwd", grid=(ns,),
        in_specs=[pl.BlockSpec((bsz, ts, HG_COLS), lambda s: (0, s, 0)),
                  pl.BlockSpec((2, hw), lambda s: (0, 0)),
                  pl.BlockSpec((1, HG_D), lambda s: (0, 0)),
                  pl.BlockSpec(a_all.shape, lambda s: (0, 0)),
                  pl.BlockSpec(masks.shape, lambda s: (0, 0, 0))],
        out_specs=(pl.BlockSpec((bsz, ts, hw), lambda s: (0, s, 0)),
                   pl.BlockSpec((bsz, ts, hw), lambda s: (0, s, 0)),
                   pl.BlockSpec((bsz, HG_HEADS, nct, HG_D, HG_D), lambda s: (0, 0, s, 0, 0))),
        out_shape=(jax.ShapeDtypeStruct((bsz, seq, y_width), F32),
                   jax.ShapeDtypeStruct((bsz, seq, hw), F32),
                   jax.ShapeDtypeStruct((bsz, HG_HEADS, nc, HG_D, HG_D), F32)),
        scratch_shapes=[pltpu.VMEM((bsz, HG_HEADS, HG_D, HG_D), F32)],
        compiler_params=_params(("arbitrary",)),
    )(proj.reshape(bsz, seq, HG_COLS), lbp, ng, a_all, masks)
    return y3.reshape(t, y_width), o3.reshape(t, hw), states


def _hg_bwd(proj, lbp, ng, o_all, states, dy, bsz, seq, after=()):
    after = tuple(a for a in after if a is not None)
    t = proj.shape[0]
    nc = seq // HG_CHUNK
    a_np, m_np = _hg_constants()
    a_all = jnp.asarray(a_np, _MXU_DTYPE)
    masks = jnp.asarray(m_np, F32)
    nl = len(HG_LEVELS)
    cs = HG_CHUNK

    ts = min(HG_TILE, seq)
    ns, nct = seq // ts, ts // cs
    hw = HG_HEADS * HG_D

    def body(p_ref, lb_ref, ng_ref, a_ref, m_ref, o_ref, st_ref, dy_ref, *rest):
        dp_ref, dlb_ref, dng_ref, dst_ref = rest[len(after):]
        a_mat = a_ref[...]
        ngv = ng_ref[...]
        ng4 = _tile_lanes(ngv, HG_HEADS)
        last_row = lax.broadcasted_iota(jnp.int32, (cs, hw), 0) == cs - 1
        first = pl.program_id(0) == 0
        heads = range(HG_HEADS)
        exs = range(bsz)
        hl = HG_HEAD_LANES
        lbp_v = lb_ref[...]

        @pl.when(first)
        def _():
            dst_ref[...] = jnp.zeros_like(dst_ref)

        def side_by_side(parts):
            return jnp.concatenate(parts, axis=1)

        def chunk(i, carry):
            dlb_acc, dng_acc = carry
            c = nct - 1 - i
            rows = pl.ds(pl.multiple_of(c * cs, cs), cs)
            gates = [_hg_gates(p_ref[e, rows, :], lbp_v) for e in exs]
            q, v, gl = [g[0] for g in gates], [g[1] for g in gates], [g[2] for g in gates]
            lb, sig, f, k = gates[0][3], [g[4] for g in gates], [g[5] for g in gates], [g[6] for g in gates]
            o = [o_ref[e, rows, :] for e in exs]
            dyv = [dy_ref[e, rows, :] for e in exs]
            sts = [[st_ref[e, h, c] for h in heads] for e in exs]
            dsts = [[dst_ref[e, h] for h in heads] for e in exs]
            e_all = [_split_dot(a_mat, gates[e][7], NN, 3) for e in exs]
            b = [e_all[e][0:cs] for e in exs]
            eb = [jnp.exp(b[e]) for e in exs]
            bl = [b[e][cs - 1:cs, :] for e in exs]
            ebl = [jnp.exp(bl[e]) for e in exs]
            ekd = [jnp.exp(bl[e] - b[e]) for e in exs]
            qb = [q[e] * eb[e] for e in exs]
            kd = [k[e] * ekd[e] for e in exs]
            do, dgl = [], []
            for e in exs:
                sg = jax.nn.sigmoid(gl[e])
                silu = gl[e] * sg
                r = lax.rsqrt(_per_head(_lane_mean, o[e] * o[e]) + EPS)
                dgl.append(dyv[e] * (o[e] * r * ng4) * (sg * (1.0 + gl[e] * (1.0 - sg))))
                u = dyv[e] * silu * ng4
                do.append(r * u - o[e] * (r * r * r) * _per_head(_lane_mean, u * o[e]))
                dng4 = jnp.sum(dyv[e] * silu * o[e] * r, axis=0, keepdims=True)
                dng_acc = dng_acc + ((dng4[:, hl[0]] + dng4[:, hl[1]]) + (dng4[:, hl[2]] + dng4[:, hl[3]]))
            es, qm, km = [], [], []
            p = [[jnp.zeros((cs, cs), F32) for _ in heads] for _ in exs]
            for li in range(nl):
                dec = [jnp.exp(e_all[e][cs * (li + 1):cs * (li + 2)]) for e in exs]
                es.append(dec)
                qm.append([q[e] * dec[e] for e in exs])
                km.append([k[e] * dec[e] for e in exs])
                mk = m_ref[li]
                p = [[p[e][h] + mk * _dot(qm[li][e][:, hl[h]], km[li][e][:, hl[h]], NT) for h in heads] for e in exs]
            dp = [[_dot(do[e][:, hl[h]], v[e][:, hl[h]], NT) for h in heads] for e in exs]
            dv_p = [[_dot(p[e][h], do[e][:, hl[h]], TN) for h in heads] for e in exs]
            dv_s = [[_dot(kd[e][:, hl[h]], dsts[e][h], NT) for h in heads] for e in exs]
            dqb = [side_by_side([_dot(do[e][:, hl[h]], sts[e][h]) for h in heads]) for e in exs]
            dkd = [side_by_side([_dot(v[e][:, hl[h]], dsts[e][h]) for h in heads]) for e in exs]
            new_dst = [[_dot(do[e][:, hl[h]], qb[e][:, hl[h]], TN) for h in heads] for e in exs]
            dv = [side_by_side([dv_p[e][h] + dv_s[e][h] for h in heads]) + _per_head(_lane_sum, q[e] * k[e]) * do[e]
                  for e in exs]
            dq = [dqb[e] * eb[e] for e in exs]
            dk = [dkd[e] * ekd[e] for e in exs]
            de = []
            for e in exs:
                dbl = (jnp.sum(dkd[e] * kd[e], axis=0, keepdims=True)
                       + side_by_side([jnp.sum(dsts[e][h] * sts[e][h], axis=0, keepdims=True) for h in heads]) * ebl[e])
                de.append([dqb[e] * qb[e] - dkd[e] * kd[e] + jnp.where(last_row, dbl, 0.0)])
            for li in range(nl):
                mk = m_ref[li]
                dpm = [[mk * dp[e][h] for h in heads] for e in exs]
                dqm = [side_by_side([_dot(dpm[e][h], km[li][e][:, hl[h]]) for h in heads]) for e in exs]
                dkm = [side_by_side([_dot(dpm[e][h], qm[li][e][:, hl[h]], TN) for h in heads]) for e in exs]
                for e in exs:
                    dq[e] = dq[e] + dqm[e] * es[li][e]
                    dk[e] = dk[e] + dkm[e] * es[li][e]
                    de[e].append(dqm[e] * qm[li][e] + dkm[e] * km[li][e])
            dg = [_split_dot(a_mat, jnp.concatenate(de[e], axis=0), TN, 2) for e in exs]
            for e in exs:
                dpd = _per_head(_lane_sum, do[e] * v[e])
                df = dg[e] / f[e] - (dk[e] + dpd * q[e])
                dp_ref[e, rows, 0:hw] = _mx(dq[e] + dpd * k[e])
                dp_ref[e, rows, hw:2 * hw] = _mx(df * (1.0 - lb) * sig[e] * (1.0 - sig[e]))
                dp_ref[e, rows, 2 * hw:3 * hw] = _mx(dv[e])
                dp_ref[e, rows, 3 * hw:4 * hw] = _mx(dgl[e])
                for h in heads:
                    dst_ref[e, h] = dsts[e][h] * ebl[e][:, hl[h]] + new_dst[e][h]
                dlb_acc = dlb_acc + jnp.sum(df * (1.0 - sig[e]), axis=0, keepdims=True)
            return dlb_acc, dng_acc

        dlb, dng = lax.fori_loop(0, nct, chunk, (jnp.zeros((1, hw), F32), jnp.zeros((1, HG_D), F32)))

        @pl.when(first)
        def _():
            dlb_ref[...] = jnp.zeros_like(dlb_ref)
            dng_ref[...] = jnp.zeros_like(dng_ref)

        mx = jnp.max(lbp_v, axis=0, keepdims=True)
        e = jnp.exp(lbp_v - mx)
        s0 = e[0:1, :] / jnp.sum(e, axis=0, keepdims=True)
        da0 = dlb * s0 * (1.0 - s0)
        dlb_ref[...] += jnp.concatenate([da0, -da0], axis=0)
        dng_ref[...] += dng

    rows3 = lambda w: pl.BlockSpec((bsz, ts, w), lambda s: (0, ns - 1 - s, 0))
    dproj, dlb, dng = pl.pallas_call(
        body, name="hgrn2_bwd", grid=(ns,),
        in_specs=[rows3(HG_COLS),
                  pl.BlockSpec((2, hw), lambda s: (0, 0)),
                  pl.BlockSpec((1, HG_D), lambda s: (0, 0)),
                  pl.BlockSpec(a_all.shape, lambda s: (0, 0)),
                  pl.BlockSpec(masks.shape, lambda s: (0, 0, 0)),
                  rows3(hw),
                  pl.BlockSpec((bsz, HG_HEADS, nct, HG_D, HG_D), lambda s: (0, 0, ns - 1 - s, 0, 0)),
                  rows3(hw)] + [pl.BlockSpec(memory_space=pl.ANY)] * len(after),
        out_specs=(rows3(HG_COLS),
                   pl.BlockSpec((2, hw), lambda s: (0, 0)),
                   pl.BlockSpec((1, HG_D), lambda s: (0, 0))),
        out_shape=(jax.ShapeDtypeStruct((bsz, seq, HG_COLS), _MXU_DTYPE),
                   jax.ShapeDtypeStruct((2, hw), F32),
                   jax.ShapeDtypeStruct((1, HG_D), F32)),
        scratch_shapes=[pltpu.VMEM((bsz, HG_HEADS, HG_D, HG_D), F32)],
        compiler_params=_params(("arbitrary",)),
    )(proj.reshape(bsz, seq, HG_COLS), lbp, ng, a_all, masks, o_all.reshape(bsz, seq, hw), states,
      dy.reshape(bsz, seq, dy.shape[1]), *after)
    return dproj.reshape(t, HG_COLS), dlb, dng


def _sw_constants():
    half = ROT_DIM // 2
    inv = (np.float32(ROPE_THETA) ** (-(np.arange(half, dtype=np.float32) * np.float32(2.0) / np.float32(ROT_DIM)))
           ).astype(np.float32)
    freq = np.zeros((1, 128), np.float32)
    sign = np.zeros((1, 128), np.float32)
    for h in range(2):
        freq[0, 64 * h:64 * h + half] = inv
        freq[0, 64 * h + half:64 * h + 2 * half] = inv
        sign[0, 64 * h:64 * h + half] = -1.0
        sign[0, 64 * h + half:64 * h + 2 * half] = 1.0
    seg = np.kron(np.eye(8, dtype=np.float32), np.full((64, 64), 1.0 / 64.0, np.float32))
    return freq, sign, seg


def _rope_tables(pos, freq, sign):
    ang = pos.astype(F32) * freq
    return jnp.cos(ang), jnp.sin(ang) * sign


def _tile_lanes(v, times):
    return v if times == 1 else jnp.concatenate([v] * times, axis=1)


def _swap_halves(v):
    w = v.shape[1]
    half = ROT_DIM // 2
    lane = lax.broadcasted_iota(jnp.int32, v.shape, 1) % SW_HD
    return jnp.where(lane < half, pltpu.roll(v, w - half, 1), jnp.where(lane < 2 * half, pltpu.roll(v, half, 1), 0.0))


def _sw_norm_rope(tv, gain, seg, cosv, sinv):
    w = tv.shape[1]
    ms = _split_dot_rhs(tv * tv, seg[0:w, 0:w])
    r = lax.rsqrt(ms + EPS)
    tn = tv * r * gain
    reps = w // 128
    return tn * _tile_lanes(cosv, reps) + _swap_halves(tn) * _tile_lanes(sinv, reps), r


def _split_dot_rhs(v, a):
    hi = _mx(v)
    lo = _mx(v - hi.astype(F32))
    return (lax.dot_general(hi, a, (NN, ((), ())), preferred_element_type=F32)
            + lax.dot_general(lo, a, (NN, ((), ())), preferred_element_type=F32))


def _sw_norm_rope_bwd(dt, tv, r, gain, seg, cosv, sinv):
    w = tv.shape[1]
    reps = w // 128
    dtn = dt * _tile_lanes(cosv, reps) + _swap_halves(dt * _tile_lanes(sinv, reps))
    u = dtn * gain
    dtv = r * u - tv * (r * r * r) * _split_dot_rhs(u * tv, seg[0:w, 0:w])
    return dtv, jnp.sum(dtn * tv * r, axis=0, keepdims=True)


def _sw_scores(qh, kp, kc):
    return _dot(qh, kp, NT), _dot(qh, kc, NT)


def _sw_probs(raw, sink, first_block):
    scale = SW_HD ** -0.5
    qi = lax.broadcasted_iota(jnp.int32, (SW_BLOCK, SW_BLOCK), 0)
    kj = lax.broadcasted_iota(jnp.int32, (SW_BLOCK, SW_BLOCK), 1)
    ok_prev = jnp.logical_and(kj > qi, jnp.logical_not(first_block))
    ok_cur = kj <= qi
    sp = jnp.where(ok_prev, raw[0] * scale, -jnp.inf)
    sc = jnp.where(ok_cur, raw[1] * scale, -jnp.inf)
    m = jnp.maximum(jnp.maximum(jnp.max(sp, axis=1, keepdims=True), jnp.max(sc, axis=1, keepdims=True)), sink)
    pp, pc = jnp.exp(sp - m), jnp.exp(sc - m)
    es = jnp.exp(sink - m)
    den = jnp.sum(pp, axis=1, keepdims=True) + jnp.sum(pc, axis=1, keepdims=True) + es
    return pp / den, pc / den, es / den


def _sw_specs(nb):
    def cur(b, n):
        return b * nb + jnp.minimum(n, nb - 1)

    def prev(b, n):
        return b * nb + jnp.maximum(jnp.minimum(n, nb - 1) - 1, 0)

    return cur, prev


def _sw_fwd(proj, pos, qg, kg, sinks, y_in, bsz, seq):
    t = proj.shape[0]
    nb = seq // SW_BLOCK
    freq_np, sign_np, seg_np = _sw_constants()
    freq, sign = jnp.asarray(freq_np), jnp.asarray(sign_np)
    seg = jnp.asarray(seg_np, _MXU_DTYPE)
    cur, prev = _sw_specs(nb)

    def body(q_ref, kc_ref, kp_ref, vc_ref, vp_ref, pc_ref, pp_ref, qg_ref, kg_ref, sk_ref, fr_ref, sn_ref, seg_ref,
             yin_ref, y_ref):
        del yin_ref
        n = pl.program_id(1)
        segv = seg_ref[...]
        cos_c, sin_c = _rope_tables(pc_ref[...], fr_ref[...], sn_ref[...])
        cos_p, sin_p = _rope_tables(pp_ref[...], fr_ref[...], sn_ref[...])
        qr, _ = _sw_norm_rope(q_ref[...], qg_ref[...], segv, cos_c, sin_c)
        kcr, _ = _sw_norm_rope(kc_ref[...], kg_ref[...], segv, cos_c, sin_c)
        kpr, _ = _sw_norm_rope(kp_ref[...], kg_ref[...], segv, cos_p, sin_p)
        vc, vp = vc_ref[...], vp_ref[...]
        ks = [slice(SW_HD * (h // SW_GROUP), SW_HD * (h // SW_GROUP + 1)) for h in range(SW_HEADS)]
        raw = [_sw_scores(qr[:, SW_HD * h:SW_HD * (h + 1)], kpr[:, ks[h]], kcr[:, ks[h]]) for h in range(SW_HEADS)]
        probs = [_sw_probs(raw[h], sk_ref[0, h], n == 0) for h in range(SW_HEADS)]
        for h in range(SW_HEADS):
            y_ref[:, SW_HD * h:SW_HD * (h + 1)] = _dot(probs[h][0], vp[:, ks[h]]) + _dot(probs[h][1], vc[:, ks[h]])

    rowq = pl.BlockSpec((SW_BLOCK, 512), lambda b, n: (cur(b, n), 0))
    full = lambda a: pl.BlockSpec(a.shape, lambda b, n: (0,) * a.ndim)
    yw = y_in.shape[1]
    return pl.pallas_call(
        body, name="swa_fwd", grid=(bsz, nb),
        in_specs=[rowq,
                  pl.BlockSpec((SW_BLOCK, 128), lambda b, n: (cur(b, n), 4)),
                  pl.BlockSpec((SW_BLOCK, 128), lambda b, n: (prev(b, n), 4)),
                  pl.BlockSpec((SW_BLOCK, 128), lambda b, n: (cur(b, n), 5)),
                  pl.BlockSpec((SW_BLOCK, 128), lambda b, n: (prev(b, n), 5)),
                  pl.BlockSpec((SW_BLOCK, 1), lambda b, n: (cur(b, n), 0)),
                  pl.BlockSpec((SW_BLOCK, 1), lambda b, n: (prev(b, n), 0)),
                  full(qg), full(kg),
                  pl.BlockSpec(memory_space=pltpu.SMEM),
                  full(freq), full(sign), full(seg),
                  pl.BlockSpec(memory_space=pl.ANY)],
        out_specs=pl.BlockSpec((SW_BLOCK, 512), lambda b, n: (cur(b, n), 1)),
        out_shape=jax.ShapeDtypeStruct((t, yw), F32),
        input_output_aliases={13: 0},
        compiler_params=_params(("parallel", "parallel")),
    )(proj, proj, proj, proj, proj, pos, pos, qg, kg, sinks, freq, sign, seg, y_in)


def _sw_bwd(proj, pos, qg, kg, sinks, y, dy, bsz, seq):
    t = proj.shape[0]
    nb = seq // SW_BLOCK
    freq_np, sign_np, seg_np = _sw_constants()
    freq, sign = jnp.asarray(freq_np), jnp.asarray(sign_np)
    seg = jnp.asarray(seg_np, _MXU_DTYPE)
    cur, prev = _sw_specs(nb)
    scale = SW_HD ** -0.5

    def body(q_ref, kc_ref, kp_ref, vc_ref, vp_ref, pc_ref, pp_ref, qg_ref, kg_ref, sk_ref, fr_ref, sn_ref, seg_ref,
             y_ref, dy_ref, dp_ref, dqg_ref, dkg_ref, dsk_ref,
             dq_car, dkv_car, dqr_s, dkc_s, dkp_s, dvc_s, dvp_s, gq_acc, gk_acc, sk_acc):
        b, n = pl.program_id(0), pl.program_id(1)
        first = jnp.logical_and(b == 0, n == 0)
        last = jnp.logical_and(b == pl.num_programs(0) - 1, n == nb)

        @pl.when(first)
        def _():
            gq_acc[...] = jnp.zeros_like(gq_acc)
            gk_acc[...] = jnp.zeros_like(gk_acc)
            sk_acc[...] = jnp.zeros_like(sk_acc)

        @pl.when(n < nb)
        def _():
            segv = seg_ref[...]
            cos_c, sin_c = _rope_tables(pc_ref[...], fr_ref[...], sn_ref[...])
            cos_p, sin_p = _rope_tables(pp_ref[...], fr_ref[...], sn_ref[...])
            qv, kcv, kpv = q_ref[...], kc_ref[...], kp_ref[...]
            qr, rq = _sw_norm_rope(qv, qg_ref[...], segv, cos_c, sin_c)
            kcr, rkc = _sw_norm_rope(kcv, kg_ref[...], segv, cos_c, sin_c)
            kpr, rkp = _sw_norm_rope(kpv, kg_ref[...], segv, cos_p, sin_p)
            vc, vp = vc_ref[...], vp_ref[...]
            lane = lax.broadcasted_iota(jnp.int32, (1, 128), 1)
            dsk = jnp.zeros((1, 128), F32)
            heads = range(SW_HEADS)
            ks = [slice(SW_HD * (h // SW_GROUP), SW_HD * (h // SW_GROUP + 1)) for h in heads]
            hs = [slice(SW_HD * h, SW_HD * (h + 1)) for h in heads]
            qh = [qr[:, hs[h]] for h in heads]
            doh = [dy_ref[:, hs[h]] for h in heads]
            raw = [_sw_scores(qh[h], kpr[:, ks[h]], kcr[:, ks[h]]) for h in heads]
            dpp = [_dot(doh[h], vp[:, ks[h]], NT) for h in heads]
            dpc = [_dot(doh[h], vc[:, ks[h]], NT) for h in heads]
            probs = [_sw_probs(raw[h], sk_ref[0, h], n == 0) for h in heads]
            dsp, dsc = [], []
            for h in heads:
                pp, pc, ps = probs[h]
                delta = jnp.sum(doh[h] * y_ref[:, hs[h]], axis=1, keepdims=True)
                dsp.append(pp * (dpp[h] - delta) * scale)
                dsc.append(pc * (dpc[h] - delta) * scale)
                dsk = dsk + jnp.where(lane == h, -jnp.sum(ps * delta), 0.0)
            for h in heads:
                dqr_s[:, hs[h]] = _dot(dsp[h], kpr[:, ks[h]]) + _dot(dsc[h], kcr[:, ks[h]])
            for kv in range(SW_KV_HEADS):
                group = range(SW_GROUP * kv, SW_GROUP * (kv + 1))
                kvs = slice(SW_HD * kv, SW_HD * (kv + 1))
                dvp_s[:, kvs] = sum(_dot(probs[h][0], doh[h], TN) for h in group)
                dvc_s[:, kvs] = sum(_dot(probs[h][1], doh[h], TN) for h in group)
                dkp_s[:, kvs] = sum(_dot(dsp[h], qh[h], TN) for h in group)
                dkc_s[:, kvs] = sum(_dot(dsc[h], qh[h], TN) for h in group)
            dq, gq = _sw_norm_rope_bwd(dqr_s[...], qv, rq, qg_ref[...], segv, cos_c, sin_c)
            dkc, gkc = _sw_norm_rope_bwd(dkc_s[...], kcv, rkc, kg_ref[...], segv, cos_c, sin_c)
            dkp, gkp = _sw_norm_rope_bwd(dkp_s[...], kpv, rkp, kg_ref[...], segv, cos_p, sin_p)
            gq_acc[...] += gq
            gk_acc[...] += gkc + gkp
            sk_acc[...] += dsk

            @pl.when(n > 0)
            def _():
                dp_ref[:, 0:512] = _mx(dq_car[...])
                dp_ref[:, 512:640] = _mx(dkv_car[:, 0:128] + dkp)
                dp_ref[:, 640:768] = _mx(dkv_car[:, 128:256] + dvp_s[...])

            dq_car[...] = dq
            dkv_car[:, 0:128] = dkc
            dkv_car[:, 128:256] = dvc_s[...]

        @pl.when(n == nb)
        def _():
            dp_ref[:, 0:512] = _mx(dq_car[...])
            dp_ref[:, 512:768] = _mx(dkv_car[...])

        @pl.when(last)
        def _():
            gq = gq_acc[...]
            acc = gq[:, 0:SW_HD]
            for h in range(1, SW_HEADS):
                acc = acc + gq[:, SW_HD * h:SW_HD * (h + 1)]
            dqg_ref[...] = acc
            gk = gk_acc[...]
            dkg_ref[...] = gk[:, 0:SW_HD] + gk[:, SW_HD:2 * SW_HD]
            dsk_ref[...] = sk_acc[...]

    rowq = pl.BlockSpec((SW_BLOCK, 512), lambda b, n: (cur(b, n), 0))
    full = lambda a: pl.BlockSpec(a.shape, lambda b, n: (0,) * a.ndim)

    def out_row(b, n):
        return b * nb + jnp.maximum(n - 1, 0)

    return pl.pallas_call(
        body, name="swa_bwd", grid=(bsz, nb + 1),
        in_specs=[rowq,
                  pl.BlockSpec((SW_BLOCK, 128), lambda b, n: (cur(b, n), 4)),
                  pl.BlockSpec((SW_BLOCK, 128), lambda b, n: (prev(b, n), 4)),
                  pl.BlockSpec((SW_BLOCK, 128), lambda b, n: (cur(b, n), 5)),
                  pl.BlockSpec((SW_BLOCK, 128), lambda b, n: (prev(b, n), 5)),
                  pl.BlockSpec((SW_BLOCK, 1), lambda b, n: (cur(b, n), 0)),
                  pl.BlockSpec((SW_BLOCK, 1), lambda b, n: (prev(b, n), 0)),
                  full(qg), full(kg),
                  pl.BlockSpec(memory_space=pltpu.SMEM),
                  full(freq), full(sign), full(seg),
                  pl.BlockSpec((SW_BLOCK, 512), lambda b, n: (cur(b, n), 1)),
                  pl.BlockSpec((SW_BLOCK, 512), lambda b, n: (cur(b, n), 1))],
        out_specs=(pl.BlockSpec((SW_BLOCK, SW_COLS), lambda b, n: (out_row(b, n), 0)),
                   pl.BlockSpec((1, SW_HD), lambda b, n: (0, 0)),
                   pl.BlockSpec((1, SW_HD), lambda b, n: (0, 0)),
                   pl.BlockSpec((1, 128), lambda b, n: (0, 0))),
        out_shape=(jax.ShapeDtypeStruct((t, SW_COLS), _MXU_DTYPE),
                   jax.ShapeDtypeStruct((1, SW_HD), F32),
                   jax.ShapeDtypeStruct((1, SW_HD), F32),
                   jax.ShapeDtypeStruct((1, 128), F32)),
        scratch_shapes=[pltpu.VMEM((SW_BLOCK, 512), F32), pltpu.VMEM((SW_BLOCK, 256), F32),
                        pltpu.VMEM((SW_BLOCK, 512), F32),
                        pltpu.VMEM((SW_BLOCK, 128), F32), pltpu.VMEM((SW_BLOCK, 128), F32),
                        pltpu.VMEM((SW_BLOCK, 128), F32), pltpu.VMEM((SW_BLOCK, 128), F32),
                        pltpu.VMEM((1, 512), F32), pltpu.VMEM((1, 128), F32), pltpu.VMEM((1, 128), F32)],
        compiler_params=_params(("arbitrary", "arbitrary")),
    )(proj, proj, proj, proj, proj, pos, pos, qg, kg, sinks, freq, sign, seg, y, dy)


def _head_rms(tv, gain):
    r = lax.rsqrt(jnp.mean(tv * tv, axis=1, keepdims=True) + EPS)
    return tv * r * gain, r


def _head_rms_bwd(dtn, tv, r, gain):
    u = dtn * gain
    return r * u - tv * (r * r * r) * jnp.mean(u * tv, axis=1, keepdims=True), jnp.sum(dtn * tv * r, axis=0, keepdims=True)


def _xa_softmax(raw):
    s = raw * (XA_HD ** -0.5)
    e = jnp.exp(s - jnp.max(s, axis=1, keepdims=True))
    return e / jnp.sum(e, axis=1, keepdims=True)


def _xa_fwd(qx, kvx, qg, kg, bsz, seq, mlen, *, tq=512):
    t = qx.shape[0]
    tq = min(tq, seq)
    nq = seq // tq
    w = XA_HEADS * XA_HD

    def body(q_ref, kv_ref, qg_ref, kg_ref, o_ref):
        heads = range(XA_HEADS)
        hs = [slice(XA_HD * h, XA_HD * (h + 1)) for h in heads]
        qn = [_head_rms(q_ref[:, hs[h]], qg_ref[...])[0] for h in heads]
        kn = [_head_rms(kv_ref[:, hs[h]], kg_ref[...])[0] for h in heads]
        raw = [_dot(qn[h], kn[h], NT) for h in heads]
        p = [_xa_softmax(raw[h]) for h in heads]
        for h in heads:
            o_ref[:, hs[h]] = _dot(p[h], kv_ref[:, w + XA_HD * h:w + XA_HD * (h + 1)]).astype(o_ref.dtype)

    vec = pl.BlockSpec((1, XA_HD), lambda b, i: (0, 0))
    return pl.pallas_call(
        body, name="xattn_fwd", grid=(bsz, nq),
        in_specs=[pl.BlockSpec((tq, w), lambda b, i: (b * nq + i, 0)),
                  pl.BlockSpec((mlen, 2 * w), lambda b, i: (b, 0)), vec, vec],
        out_specs=pl.BlockSpec((tq, w), lambda b, i: (b * nq + i, 0)),
        out_shape=jax.ShapeDtypeStruct((t, w), _MXU_DTYPE),
        compiler_params=_params(("parallel", "parallel")),
    )(qx, kvx, qg, kg)


def _xa_bwd(qx, kvx, qg, kg, do, bsz, seq, mlen, *, tq=512):
    t = qx.shape[0]
    tq = min(tq, seq)
    nq = seq // tq
    w = XA_HEADS * XA_HD
    scale = XA_HD ** -0.5

    def body(q_ref, kv_ref, qg_ref, kg_ref, do_ref, dq_ref, dkv_ref, dqg_ref, dkg_ref):
        b, i = pl.program_id(0), pl.program_id(1)

        @pl.when(jnp.logical_and(b == 0, i == 0))
        def _():
            dqg_ref[...] = jnp.zeros_like(dqg_ref)
            dkg_ref[...] = jnp.zeros_like(dkg_ref)

        @pl.when(i == 0)
        def _():
            dkv_ref[...] = jnp.zeros_like(dkv_ref)

        heads = range(XA_HEADS)
        hs = [slice(XA_HD * h, XA_HD * (h + 1)) for h in heads]
        vs = [slice(w + XA_HD * h, w + XA_HD * (h + 1)) for h in heads]
        qv = [q_ref[:, hs[h]] for h in heads]
        kv = [kv_ref[:, hs[h]] for h in heads]
        doh = [do_ref[:, hs[h]] for h in heads]
        qn = [_head_rms(qv[h], qg_ref[...]) for h in heads]
        kn = [_head_rms(kv[h], kg_ref[...]) for h in heads]
        raw = [_dot(qn[h][0], kn[h][0], NT) for h in heads]
        dp = [_dot(doh[h], kv_ref[:, vs[h]], NT) for h in heads]
        p = [_xa_softmax(raw[h]) for h in heads]
        ds = [p[h] * (dp[h] - jnp.sum(p[h] * dp[h], axis=1, keepdims=True)) * scale for h in heads]
        dqn = [_dot(ds[h], kn[h][0]) for h in heads]
        dkn = [_dot(ds[h], qn[h][0], TN) for h in heads]
        dvv = [_dot(p[h], doh[h], TN) for h in heads]
        gq_sum = jnp.zeros((1, XA_HD), F32)
        gk_sum = jnp.zeros((1, XA_HD), F32)
        for h in heads:
            dqv, gq = _head_rms_bwd(dqn[h], qv[h], qn[h][1], qg_ref[...])
            dkv, gk = _head_rms_bwd(dkn[h], kv[h], kn[h][1], kg_ref[...])
            dq_ref[:, hs[h]] = dqv.astype(dq_ref.dtype)
            dkv_ref[:, hs[h]] += dkv
            dkv_ref[:, vs[h]] += dvv[h]
            gq_sum = gq_sum + gq
            gk_sum = gk_sum + gk
        dqg_ref[...] += gq_sum
        dkg_ref[...] += gk_sum

    vec = pl.BlockSpec((1, XA_HD), lambda b, i: (0, 0))
    row = pl.BlockSpec((tq, w), lambda b, i: (b * nq + i, 0))
    mem = pl.BlockSpec((mlen, 2 * w), lambda b, i: (b, 0))
    return pl.pallas_call(
        body, name="xattn_bwd", grid=(bsz, nq),
        in_specs=[row, mem, vec, vec, row],
        out_specs=(row, mem, vec, vec),
        out_shape=(jax.ShapeDtypeStruct((t, w), _MXU_DTYPE), jax.ShapeDtypeStruct((bsz * mlen, 2 * w), F32),
                   jax.ShapeDtypeStruct((1, XA_HD), F32), jax.ShapeDtypeStruct((1, XA_HD), F32)),
        compiler_params=_params(("arbitrary", "arbitrary")),
    )(qx, kvx, qg, kg, do)


def _loss_finish(sq_row, d_model):
    def body(s_ref, o_ref):
        o_ref[...] = jnp.zeros_like(o_ref) + 0.5 * jnp.sum(s_ref[...]) / float(d_model)

    return pl.pallas_call(body, name="loss_finish", out_shape=jax.ShapeDtypeStruct((1, 128), F32))(sq_row)


def _adamw_math(w, g, m, v):
    m = ADAM_B1 * m + (1.0 - ADAM_B1) * g
    v = ADAM_B2 * v + (1.0 - ADAM_B2) * (g * g)
    m_hat = m / (1.0 - ADAM_B1 ** ADAM_STEP)
    v_hat = v / (1.0 - ADAM_B2 ** ADAM_STEP)
    return -ADAM_LR * (m_hat / (jnp.sqrt(v_hat) + ADAM_EPS) + ADAM_WD * w), m, v


def _adamw_big(w, g, m, v, *, name, tr=512):
    r, c = w.shape
    tr = min(tr, r)

    def body(w_ref, g_ref, m_ref, v_ref, go_ref, d_ref, mo_ref, vo_ref):
        gv = g_ref[...]
        d, mn, vn = _adamw_math(w_ref[...], gv, m_ref[...], v_ref[...])
        go_ref[...] = gv
        d_ref[...] = d
        mo_ref[...] = mn
        vo_ref[...] = vn

    spec = pl.BlockSpec((tr, c), lambda i: (i, 0))
    shp = jax.ShapeDtypeStruct((r, c), F32)
    return pl.pallas_call(
        body, name=name, grid=(r // tr,), in_specs=[spec] * 4, out_specs=(spec,) * 4, out_shape=(shp,) * 4,
        compiler_params=_params(("parallel",)),
    )(w, g, m, v)


def _adamw_small(ws, gs, ms, vs):
    n = len(ws)

    def body(*refs):
        for i in range(n):
            d, mn, vn = _adamw_math(refs[i][...], refs[n + i][...], refs[2 * n + i][...], refs[3 * n + i][...])
            refs[4 * n + i][...] = d
            refs[5 * n + i][...] = mn
            refs[6 * n + i][...] = vn

    shapes = tuple(jax.ShapeDtypeStruct(w.shape, F32) for w in ws)
    return pl.pallas_call(body, name="adamw_small", out_shape=shapes * 3)(*ws, *gs, *ms, *vs)


def _add_halves(g, recv, c_idx, *, name, tr=512):
    _, r, c = g.shape
    h = r // 2
    tr = min(tr, h)
    nt = h // tr

    def body(c_ref, g_ref, r_ref, o_ref):
        del c_ref
        o_ref[...] = g_ref[...] + r_ref[...]

    return pl.pallas_call(
        body, name=name,
        grid_spec=pltpu.PrefetchScalarGridSpec(
            num_scalar_prefetch=1, grid=(4, nt),
            in_specs=[pl.BlockSpec((None, tr, c), lambda k, i, cr: (k, cr[0] * nt + i, 0)),
                      pl.BlockSpec((None, tr, c), lambda k, i, cr: (k, i, 0))],
            out_specs=pl.BlockSpec((None, tr, c), lambda k, i, cr: (k, i, 0))),
        out_shape=jax.ShapeDtypeStruct((4, h, c), F32),
        compiler_params=_params(("parallel", "parallel")),
    )(c_idx, g, recv)


def _add_chips(p, recv, place_idx, *, name, tr=512):
    _, h, c = p.shape
    tr = min(tr, h)
    nt = h // tr

    def body(pi_ref, p_ref, r_ref, o_ref):
        del pi_ref
        o_ref[...] = ((p_ref[...] + r_ref[0]) + r_ref[1]) + r_ref[2]

    return pl.pallas_call(
        body, name=name,
        grid_spec=pltpu.PrefetchScalarGridSpec(
            num_scalar_prefetch=1, grid=(nt,),
            in_specs=[pl.BlockSpec((None, tr, c), lambda i, pi: (pi[0], i, 0)),
                      pl.BlockSpec((3, tr, c), lambda i, pi: (0, i, 0))],
            out_specs=pl.BlockSpec((tr, c), lambda i, pi: (pi[1] * nt + i, 0))),
        out_shape=jax.ShapeDtypeStruct((2 * h, c), F32),
        compiler_params=_params(("parallel",)),
    )(place_idx, p, recv)


def _place_shard(shard, place_idx, *, name, tr=512, after=()):
    r, c = shard.shape
    tr = min(tr, r)

    def body(pi_ref, s_ref, *rest):
        del pi_ref
        rest[-1][...] = s_ref[...]

    return pl.pallas_call(
        body, name=name,
        grid_spec=pltpu.PrefetchScalarGridSpec(
            num_scalar_prefetch=1, grid=(r // tr,),
            in_specs=[pl.BlockSpec((tr, c), lambda i, pi: (i, 0))] + [pl.BlockSpec(memory_space=pl.ANY)] * len(after),
            out_specs=pl.BlockSpec((None, tr, c), lambda i, pi: (pi[0], i, 0))),
        out_shape=jax.ShapeDtypeStruct((4, r, c), shard.dtype),
        compiler_params=_params(("parallel",)),
    )(place_idx, shard, *after)


def _place():
    x, y, c = lax.axis_index("x"), lax.axis_index("y"), lax.axis_index("c")
    chips = [(1 - x, y), (x, 1 - y), (1 - x, 1 - y)]
    return x, y, c, chips


ANY = pl.BlockSpec(memory_space=pl.ANY)


def _exchange_halves(grads, name):
    n = len(grads)

    def body(*refs):
        ins, outs = refs[:n], refs[n:2 * n]
        send_sems, recv_sems = refs[2 * n:]
        x, y, c, _ = _place()

        def copy(a):
            h = ins[a].shape[1] // 2
            return pltpu.make_async_remote_copy(
                src_ref=ins[a].at[:, pl.ds((1 - c) * h, h), :], dst_ref=outs[a],
                send_sem=send_sems.at[a], recv_sem=recv_sems.at[a], device_id=(x, y, 1 - c), device_id_type=MESH)

        for a in range(n):
            copy(a).start()
        for a in range(n):
            copy(a).wait_recv()
        for a in range(n):
            copy(a).wait_send()

    return pl.pallas_call(
        body, name=name,
        in_specs=[ANY] * n, out_specs=tuple([ANY] * n),
        out_shape=tuple(jax.ShapeDtypeStruct((4, g.shape[1] // 2, g.shape[2]), g.dtype) for g in grads),
        scratch_shapes=[pltpu.SemaphoreType.DMA((n,)), pltpu.SemaphoreType.DMA((n,))],
    )(*grads)


HBM = pl.BlockSpec(memory_space=pltpu.HBM)
SEM = pl.BlockSpec(memory_space=pltpu.SEMAPHORE)
EFFECT = pltpu.SideEffectType.DATAFLOW_SIDE_EFFECTING


def _in_hbm(a):
    return pltpu.with_memory_space_constraint(a, pltpu.HBM)


def _split_copy_calls(name, srcs, lands, n_copies, make_copies):
    ns, nl = len(srcs), len(lands)
    nb = ns + nl

    def start(after=()):
        n_after = len(after)

        def body(*refs):
            outs = refs[nb + n_after:]
            copies = make_copies(refs[:ns], refs[ns:nb], outs[0], outs[1])
            for cp in copies:
                cp.start()
            token = refs[-1]
            token[...] = jnp.zeros_like(token)

        bufs = [_in_hbm(a) for a in list(srcs) + list(lands)]
        out = pl.pallas_call(
            body, name=name + "_start",
            out_shape=(pltpu.SemaphoreType.DMA((n_copies,)), pltpu.SemaphoreType.DMA((n_copies,)),
                       *[pltpu.HBM(a.shape, a.dtype) for a in bufs], jax.ShapeDtypeStruct((8, 128), F32)),
            in_specs=[HBM] * nb + [pl.BlockSpec(memory_space=pl.ANY)] * n_after,
            out_specs=(SEM, SEM, *[HBM] * nb, pl.BlockSpec(memory_space=pltpu.VMEM)),
            input_output_aliases={i: 2 + i for i in range(nb)},
            compiler_params=pltpu.CompilerParams(has_side_effects=EFFECT),
        )(*bufs, *after)
        return dict(send=out[0], recv=out[1], bufs=list(out[2:2 + nb]), token=out[-1])

    def wait(state, after):
        def body(*refs):
            copies = make_copies(refs[:ns], refs[ns:nb], refs[nb], refs[nb + 1])
            for cp in copies:
                cp.wait_send()
            for cp in copies:
                cp.wait_recv()

        bufs = state["bufs"]
        out = pl.pallas_call(
            body, name=name + "_wait",
            out_shape=tuple(pltpu.HBM(a.shape, a.dtype) for a in bufs),
            in_specs=[HBM] * nb + [SEM, SEM] + [pl.BlockSpec(memory_space=pl.ANY)] * len(after),
            out_specs=tuple([HBM] * nb),
            input_output_aliases={i: i for i in range(nb)},
            compiler_params=pltpu.CompilerParams(has_side_effects=EFFECT),
        )(*bufs, state["send"], state["recv"], *after)
        return list(out[:ns]), list(out[ns:])

    return start, wait


def _scatter_chips_split(name, parts):
    n = len(parts)
    lands = [lax.empty((3,) + p.shape[1:], p.dtype) for p in parts]

    def make_copies(srcs, lnds, send_sems, recv_sems):
        _, _, c, chips = _place()
        return [pltpu.make_async_remote_copy(
            src_ref=srcs[a].at[2 * px + py], dst_ref=lnds[a].at[j], send_sem=send_sems.at[a * 3 + j],
            recv_sem=recv_sems.at[a * 3 + j], device_id=(px, py, c), device_id_type=MESH)
            for a in range(n) for j, (px, py) in enumerate(chips)]

    return _split_copy_calls(name, parts, lands, 3 * n, make_copies)


def _exchange_halves_split(name, grads):
    n = len(grads)
    lands = [lax.empty((4, g.shape[1] // 2, g.shape[2]), g.dtype) for g in grads]

    def make_copies(srcs, lnds, send_sems, recv_sems):
        x, y, c, _ = _place()
        out = []
        for a in range(n):
            h = srcs[a].shape[1] // 2
            out.append(pltpu.make_async_remote_copy(
                src_ref=srcs[a].at[:, pl.ds((1 - c) * h, h), :], dst_ref=lnds[a], send_sem=send_sems.at[a],
                recv_sem=recv_sems.at[a], device_id=(x, y, 1 - c), device_id_type=MESH))
        return out

    return _split_copy_calls(name, grads, lands, n, make_copies)


def _gather_chips_split(name, shards, lands):
    n = len(shards)

    def make_copies(srcs, lnds, send_sems, recv_sems):
        x, y, c, chips = _place()
        out = []
        for a in range(n):
            h = srcs[a].shape[0] // 2
            for j, (px, py) in enumerate(chips):
                out.append(pltpu.make_async_remote_copy(
                    src_ref=srcs[a].at[pl.ds(c * h, h), :], dst_ref=lnds[a].at[2 * x + y, pl.ds(c * h, h), :],
                    send_sem=send_sems.at[a * 3 + j], recv_sem=recv_sems.at[a * 3 + j],
                    device_id=(px, py, c), device_id_type=MESH))
        return out

    return _split_copy_calls(name, shards, lands, 3 * n, make_copies)


def _gather_finish(gathered, name):
    n = len(gathered)

    def body(*refs):
        outs = refs[n:2 * n]
        send_sems, recv_sems = refs[2 * n:]
        x, y, c, chips = _place()

        def copy(a, j, chip_idx, which):
            h = outs[a].shape[1] // 2
            rows = outs[a].at[chip_idx, pl.ds(which * h, h), :]
            return pltpu.make_async_remote_copy(
                src_ref=rows, dst_ref=rows, send_sem=send_sems.at[a * 3 + j], recv_sem=recv_sems.at[a * 3 + j],
                device_id=(x, y, 1 - c), device_id_type=MESH)

        for a in range(n):
            for j, (px, py) in enumerate(chips):
                copy(a, j, 2 * px + py, c).start()
        for a in range(n):
            for j, (px, py) in enumerate(chips):
                copy(a, j, 2 * px + py, 1 - c).wait_recv()
        for a in range(n):
            for j, (px, py) in enumerate(chips):
                copy(a, j, 2 * px + py, c).wait_send()

    return pl.pallas_call(
        body, name=name,
        in_specs=[ANY] * n, out_specs=tuple([ANY] * n),
        out_shape=tuple(jax.ShapeDtypeStruct(g.shape, g.dtype) for g in gathered),
        input_output_aliases={i: i for i in range(n)},
        scratch_shapes=[pltpu.SemaphoreType.DMA((3 * n,)), pltpu.SemaphoreType.DMA((3 * n,))],
    )(*gathered)


def _join_halves(fulls):
    n = len(fulls)

    def body(*refs):
        outs = refs[n:2 * n]
        send_sems, recv_sems = refs[2 * n:]
        x, y, c, _ = _place()

        def copy(a, which):
            h = outs[a].shape[0] // 2
            rows = outs[a].at[pl.ds(which * h, h), :]
            return pltpu.make_async_remote_copy(
                src_ref=rows, dst_ref=rows, send_sem=send_sems.at[a], recv_sem=recv_sems.at[a],
                device_id=(x, y, 1 - c), device_id_type=MESH)

        for a in range(n):
            copy(a, c).start()
        for a in range(n):
            copy(a, 1 - c).wait_recv()
        for a in range(n):
            copy(a, c).wait_send()

    return pl.pallas_call(
        body, name="rs_join_halves",
        in_specs=[ANY] * n, out_specs=tuple([ANY] * n),
        out_shape=tuple(jax.ShapeDtypeStruct(p.shape, p.dtype) for p in fulls),
        input_output_aliases={i: i for i in range(n)},
        scratch_shapes=[pltpu.SemaphoreType.DMA((n,)), pltpu.SemaphoreType.DMA((n,))],
    )(*fulls)


def _all_gather_small_split(sm):
    r, w = sm.shape

    def make_copies(srcs, lnds, send_sems, recv_sems):
        x, y, c, _ = _place()
        me = 4 * x + 2 * y + c
        rel = [(dx, dy, dc) for dx in (0, 1) for dy in (0, 1) for dc in (0, 1)][1:]
        return [pltpu.make_async_remote_copy(
            src_ref=srcs[0], dst_ref=lnds[0].at[me], send_sem=send_sems.at[k], recv_sem=recv_sems.at[k],
            device_id=(1 - x if dx else x, 1 - y if dy else y, 1 - c if dc else c), device_id_type=MESH)
            for k, (dx, dy, dc) in enumerate(rel)]

    return _split_copy_calls("all_gather_small", [sm], [lax.empty((8, r, w), sm.dtype)], 7, make_copies)


def _sum_devices(sm, gathered, me_idx):
    def body(me_ref, sm_ref, g_ref, o_ref):
        own = sm_ref[...]
        acc = jnp.where(me_ref[0] == 0, own, g_ref[0])
        for d in range(1, 8):
            acc = acc + jnp.where(me_ref[0] == d, own, g_ref[d])
        o_ref[...] = acc

    vm = pl.BlockSpec(memory_space=pltpu.VMEM)
    return pl.pallas_call(
        body, name="sum_devices", in_specs=[pl.BlockSpec(memory_space=pltpu.SMEM), vm, vm], out_specs=vm,
        out_shape=jax.ShapeDtypeStruct(sm.shape, F32),
    )(me_idx, sm, gathered)


class _LocalWeights:
    def __init__(self, w):
        self.w = w
        self.g = {}

    def begin(self):
        return ()

    def first(self, after):
        del after
        return self.w

    def rest(self, after):
        del after
        return self.w

    def grads(self, tag, g):
        del tag
        self.g.update(g)
        return ()

    def poll(self, after):
        del after
        return ()


def _local_step(x3, mem3, pos2, target3, small, comm):
    bsz, seq, d = x3.shape
    mlen = mem3.shape[1]
    t = bsz * seq
    tok = comm.begin()
    x = x3.reshape(t, d)
    mem = mem3.reshape(bsz * mlen, d)
    target = target3.reshape(t, d)
    pos = pos2.reshape(t, 1)
    qg_t = jnp.tile(small["sw_q_norm_g"], (1, SW_HEADS))
    kg_t = jnp.tile(small["sw_k_norm_g"], (1, SW_KV_HEADS))

    hn1 = _rms_fwd(x, small["norm1_g"], name="rms1_fwd", after=tok)
    w = comm.first(hn1)
    proj_hg = _mm(hn1, w["w_in_hg"], NN, t, HG_COLS, d, name="proj_hg", tk=d, after=(w.get("token"),))[0]
    proj_sw = _mm(hn1, w["w_in_sw"], NN, t, SW_COLS, d, name="proj_sw", tk=d)[0]
    y_mix, o_hg, states = _hg_fwd(proj_hg, small["hg_lower_bounds"], small["hg_norm_g"], bsz, seq, y_width=1024)
    y_mix = _sw_fwd(proj_sw, pos, qg_t, kg_t, small["sw_sinks"], y_mix, bsz, seq)
    w_in_hg, w_in_sw = w["w_in_hg"], w["w_in_sw"]
    w = comm.rest(y_mix)
    ff = w["down"].shape[0]
    ffs = ff // 4
    h1, hn2 = _mm(y_mix, w["w_out"], NN, t, d, 1024, name="out_proj", tk=1024, extras=(x,), rows=(small["norm2_g"],),
                  epilogue=_residual_rms, out_dtypes=(F32, _MXU_DTYPE))
    mn = _rms_fwd(mem, small["mem_norm_g"], name="rms_mem_fwd")
    qx = _mm(hn2, w["wq"], NN, t, 512, d, name="xa_q", tk=d)[0]
    kvx = _mm(mn, w["wkv"], NN, bsz * mlen, 1024, d, name="xa_kv", tk=d)[0]
    ox = _xa_fwd(qx, kvx, small["xa_q_norm_g"], small["xa_k_norm_g"], bsz, seq, mlen)
    h2, hn3 = _mm(ox, w["wo"], NN, t, d, 512, name="xa_o", tk=512, extras=(h1,), rows=(small["norm3_g"],),
                  epilogue=_residual_rms, out_dtypes=(F32, _MXU_DTYPE))

    def relu_sq(acc):
        a = jnp.maximum(acc, 0.0)
        return a, a * a

    act, act2 = _mm(hn3, w["up"], NN, t, ff, d, name="mlp_up", tm=2048, tn=ffs, tk=d,
                    b_spec=pl.BlockSpec((None, d, ffs), lambda i, j, kk: (j, 0, 0)),
                    epilogue=relu_sq, out_dtypes=(_MXU_DTYPE, _MXU_DTYPE))
    inv_d = 1.0 / d

    def loss_cotangent(acc, res, tgt):
        diff = acc + res - tgt
        v = diff * inv_d
        return v, v, jnp.sum(diff * diff, axis=0, keepdims=True)

    dy, dy_mx, sq_row = _mm(act2, w["down"], NN, t, d, ff, name="mlp_down", extras=(h2, target),
                            epilogue=loss_cotangent, out_dtypes=(F32, _MXU_DTYPE), row_sums=1)
    loss_row = _loss_finish(sq_row, d)

    dz = _mm(dy_mx, w["down"], NT, t, ff, d, name="d_act", tm=2048, tk=d, extras=(act,),
             epilogue=lambda acc, a: (acc * (2.0 * a.astype(F32)),), out_dtypes=(_MXU_DTYPE,))[0]
    g_down = _mm(act2, dy_mx, TN, ff, d, t, name="g_down")[0]
    g_up = _mm(hn3, dz, TN, d, ff, t, name="g_up", tn=ffs,
               out_shape=(jax.ShapeDtypeStruct((4, d, ffs), F32),),
               out_spec=(pl.BlockSpec((None, min(1024, d), ffs), lambda i, j, kk: (j, i, 0)),))[0]
    tok = comm.grads("mlp", dict(up=g_up, down=g_down))
    dh2, dh2_mx, g_norm3 = _mm(dz, w["up"], NT, t, d, ff, name="d_hn3", tk=ffs, after=tok,
                               b_spec=pl.BlockSpec((None, min(1024, d), ffs), lambda i, j, kk: (kk, j, 0)),
                               extras=(h2, dy), rows=(small["norm3_g"],), epilogue=_rms_bwd_residual,
                               out_dtypes=(F32, _MXU_DTYPE), row_sums=1)
    d_ox = _mm(dh2_mx, w["wo"], NT, t, 512, d, name="d_ox", tk=d)[0]
    g_wo = _mm(ox, dh2_mx, TN, 512, d, t, name="g_wo")[0]
    d_qx, d_kvx, g_xq, g_xk = _xa_bwd(qx, kvx, small["xa_q_norm_g"], small["xa_k_norm_g"], d_ox, bsz, seq, mlen)
    g_wq = _mm(hn2, d_qx, TN, d, 512, t, name="g_wq")[0]
    g_wkv = _mm(mn, d_kvx, TN, d, 1024, bsz * mlen, name="g_wkv")[0]
    dh1, dh1_mx, g_norm2 = _mm(d_qx, w["wq"], NT, t, d, 512, name="d_hn2", tk=512, extras=(h1, dh2),
                               rows=(small["norm2_g"],), epilogue=_rms_bwd_residual, out_dtypes=(F32, _MXU_DTYPE),
                               row_sums=1)
    dmn = _mm(d_kvx, w["wkv"], NT, bsz * mlen, d, 1024, name="d_mn", tk=1024)[0]
    g_memn = _rms_gain_grad(mem, small["mem_norm_g"], dmn, name="rms_mem_bwd")
    g_wout = _mm(y_mix, dh1_mx, TN, 1024, d, t, name="g_wout")[0]
    tok = comm.grads("mid", dict(w_out=g_wout, wq=g_wq, wkv=g_wkv, wo=g_wo))
    d_mix = _mm(dh1_mx, w["w_out"], NT, t, 1024, d, name="d_mix", tk=d, after=tok)[0]
    dproj_sw, g_swq, g_swk, g_sinks = _sw_bwd(proj_sw, pos, qg_t, kg_t, small["sw_sinks"], y_mix, d_mix, bsz, seq)
    tok = comm.poll(dproj_sw)
    dproj_hg, g_lb, g_hgn = _hg_bwd(proj_hg, small["hg_lower_bounds"], small["hg_norm_g"], o_hg, states, d_mix, bsz, seq,
                                    after=tok)
    g_in_hg = _mm(hn1, dproj_hg, TN, d, HG_COLS, t, name="g_in_hg")[0]
    g_in_sw = _mm(hn1, dproj_sw, TN, d, SW_COLS, t, name="g_in_sw")[0]
    tok = comm.grads("in", dict(w_in_hg=g_in_hg, w_in_sw=g_in_sw))
    dhn1_a = _mm(dproj_hg, w_in_hg, NT, t, d, HG_COLS, name="d_hn1_hg", tk=1024, after=tok)[0]
    grad_x, g_norm1 = _mm(dproj_sw, w_in_sw, NT, t, d, SW_COLS, name="d_hn1_sw", tk=SW_COLS, extras=(dhn1_a, x, dh1),
                          rows=(small["norm1_g"],), row_sums=1,
                          epilogue=lambda acc, prev, xv, dres, g: _rms_bwd_residual(acc + prev, xv, dres, g)[1:])

    g_small = dict(norm1_g=g_norm1, hg_lower_bounds=g_lb, hg_norm_g=g_hgn, sw_q_norm_g=g_swq, sw_k_norm_g=g_swk,
                   sw_sinks=g_sinks[:, 0:SW_HEADS], norm2_g=g_norm2, mem_norm_g=g_memn, xa_q_norm_g=g_xq,
                   xa_k_norm_g=g_xk, norm3_g=g_norm3)
    return loss_row, grad_x.reshape(bsz, seq, d), g_small


SMALL_NAMES = ("norm1_g", "hg_lower_bounds", "hg_norm_g", "sw_q_norm_g", "sw_k_norm_g", "sw_sinks", "norm2_g",
               "mem_norm_g", "xa_q_norm_g", "xa_k_norm_g", "norm3_g")
BIG_NAMES = ("w_in", "w_out", "xa_wq", "xa_wkv", "xa_wo", "mlp_up", "mlp_down")
WEIGHT_ORDER = ("norm1_g", "w_in", "hg_lower_bounds", "hg_norm_g", "sw_q_norm_g", "sw_k_norm_g", "sw_sinks", "w_out",
                "norm2_g", "mem_norm_g", "xa_wq", "xa_wkv", "xa_q_norm_g", "xa_k_norm_g", "xa_wo", "norm3_g",
                "mlp_up", "mlp_down")


def _pack_rows(vals, width):
    starts, at = [], 0
    for v in vals:
        starts.append(at)
        at += v.shape[0]
    total = at + (-at) % 8
    out = None
    for v, s in zip(vals, starts):
        placed = jnp.pad(v, ((s, total - s - v.shape[0]), (0, width - v.shape[1])))
        out = placed if out is None else out + placed
    return out, starts


class _MeshWeights:
    LATE = ("w_out", "xa_wq", "xa_wkv", "xa_wo", "mlp_up", "mlp_down")

    def __init__(self, shards, d, ff):
        self.shards, self.d, self.ff = shards, d, ff
        self.c_idx = lax.axis_index("c").astype(jnp.int32).reshape(1)
        chip = (2 * lax.axis_index("x") + lax.axis_index("y")).astype(jnp.int32)
        self.place_idx = jnp.stack([chip, lax.axis_index("c").astype(jnp.int32)])
        self.pending = []
        self.exchanging = None
        self.halves = {}

    def begin(self):
        shard = self.shards["w_in"]
        start, self.in_wait = _gather_chips_split(
            "gather_in", [shard], [_place_shard(shard, self.place_idx, name="place_w_in")])
        self.in_state = start()
        tok = (self.in_state["token"],)
        self.placed = [_place_shard(self.shards[n], self.place_idx, name="place_" + n, after=tok) for n in self.LATE]
        return tok

    def first(self, after):
        _, lands = self.in_wait(self.in_state, (after, *self.placed))
        (g_in,) = _gather_finish(lands, "gather_in_finish")
        start, self.late_wait = _gather_chips_split("gather_late", [self.shards[n] for n in self.LATE], self.placed)
        self.late_state = start(after=(g_in,))
        ws = g_in.shape[2]
        cut = HG_COLS - 2 * ws
        return dict(w_in_hg=jnp.concatenate([g_in[0], g_in[1], g_in[2][:, :cut]], axis=1),
                    w_in_sw=jnp.concatenate([g_in[2][:, cut:], g_in[3]], axis=1), token=self.late_state["token"])

    def rest(self, after):
        _, lands = self.late_wait(self.late_state, (after,))
        g_out, g_q, g_kv, g_o, g_up, g_dn = _gather_finish(lands, "gather_late_finish")
        d = self.d
        return dict(w_out=g_out.reshape(-1, d), wq=g_q.reshape(d, -1), wkv=g_kv.reshape(d, -1),
                    wo=jnp.concatenate([g_o[k] for k in range(4)], axis=1), up=g_up, down=g_dn.reshape(self.ff, d))

    def _scatter(self, tag, names, arrays, recv):
        parts = [_add_halves(g, r, self.c_idx, name="rs_add_halves_" + n) for n, g, r in zip(names, arrays, recv)]
        start, wait = _scatter_chips_split("rs_scatter_" + tag, parts)
        state = start()
        self.pending.append((names, wait, state))
        return state["token"]

    def _advance(self, after):
        if self.exchanging is None:
            return ()
        tag, names, wait, state = self.exchanging
        self.exchanging = None
        arrays, recv = wait(state, (after,))
        return (self._scatter(tag, names, arrays, recv),)

    def poll(self, after):
        return self._advance(after)

    def grads(self, tag, g):
        d, ff = self.d, self.ff
        if tag == "mlp":
            names, arrays = ("mlp_up", "mlp_down"), [g["up"], g["down"].reshape(4, ff // 4, d)]
        elif tag == "mid":
            names = ("w_out", "xa_wq", "xa_wkv", "xa_wo")
            ds = d // 4
            g_wo = jnp.stack([g["wo"][:, ds * k:ds * (k + 1)] for k in range(4)])
            arrays = [g["w_out"].reshape(4, -1, d), g["wq"].reshape(4, d // 4, -1), g["wkv"].reshape(4, d // 4, -1), g_wo]
        else:
            hg, sw = g["w_in_hg"], g["w_in_sw"]
            ws = (hg.shape[1] + sw.shape[1]) // 4
            cut = hg.shape[1] - 2 * ws
            names = ("w_in",)
            arrays = [jnp.stack([hg[:, :ws], hg[:, ws:2 * ws], jnp.concatenate([hg[:, 2 * ws:], sw[:, :ws - cut]], axis=1),
                                 sw[:, ws - cut:]])]
        toks = self._advance(arrays[0])
        if tag == "in":
            return toks + (self._scatter(tag, names, arrays, _exchange_halves(arrays, "rs_exchange_" + tag)),)
        start, wait = _exchange_halves_split("rs_exchange_" + tag, arrays)
        state = start()
        self.exchanging = (tag, names, wait, state)
        return toks + (state["token"],)

    def finish(self, after):
        for names, wait, state in self.pending:
            srcs, lands = wait(state, after)
            for n, p, r in zip(names, srcs, lands):
                self.halves[n] = _add_chips(p, r, self.place_idx, name="rs_add_chips_" + n)
        return dict(zip(BIG_NAMES, _join_halves([self.halves[n] for n in BIG_NAMES])))


def kernel(x, mem, positions, norm1_g, w_in, hg_lower_bounds, hg_norm_g, sw_q_norm_g, sw_k_norm_g, sw_sinks, w_out, norm2_g, mem_norm_g, xa_wq, xa_wkv, xa_q_norm_g, xa_k_norm_g, xa_wo, norm3_g, mlp_up, mlp_down, loss_target, m_norm1_g, m_w_in, m_hg_lower_bounds, m_hg_norm_g, m_sw_q_norm_g, m_sw_k_norm_g, m_sw_sinks, m_w_out, m_norm2_g, m_mem_norm_g, m_xa_wq, m_xa_wkv, m_xa_q_norm_g, m_xa_k_norm_g, m_xa_wo, m_norm3_g, m_mlp_up, m_mlp_down, v_norm1_g, v_w_in, v_hg_lower_bounds, v_hg_norm_g, v_sw_q_norm_g, v_sw_k_norm_g, v_sw_sinks, v_w_out, v_norm2_g, v_mem_norm_g, v_xa_wq, v_xa_wkv, v_xa_q_norm_g, v_xa_k_norm_g, v_xa_wo, v_norm3_g, v_mlp_up, v_mlp_down):
    given = dict(locals())
    weights = {n: given[n] for n in WEIGHT_ORDER}
    moms = {n: given["m_" + n] for n in WEIGHT_ORDER}
    vars_ = {n: given["v_" + n] for n in WEIGHT_ORDER}
    d = x.shape[-1]
    ff = mlp_down.shape[1] * 4
    small = {n: weights[n] for n in SMALL_NAMES}

    comm = _MeshWeights({n: weights[n][0].astype(_MXU_DTYPE) for n in BIG_NAMES}, d, ff)
    loss_row, grad_x, g_small = _local_step(x, mem, positions, loss_target, small, comm)
    packed, starts = _pack_rows([g_small[n] for n in SMALL_NAMES] + [loss_row], 1024)
    start, wait = _all_gather_small_split(packed)
    state = start()
    big_grads = comm.finish((grad_x, state["token"]))
    (own,), (gathered,) = wait(state, (big_grads[BIG_NAMES[0]],))
    device = (4 * lax.axis_index("x") + 2 * lax.axis_index("y") + lax.axis_index("c")).astype(jnp.int32).reshape(1)
    summed = _sum_devices(own, gathered, device)
    small_grads = {}
    for n, s in zip(SMALL_NAMES, starts):
        r, c = weights[n].shape
        small_grads[n] = summed[s:s + r, 0:c]
    loss = summed[starts[-1], 0]

    grads, deltas, new_m, new_v = {}, {}, {}, {}
    for n in BIG_NAMES:
        shp = weights[n].shape
        g2, dl, mo, vo = _adamw_big(weights[n][0], big_grads[n], moms[n][0], vars_[n][0], name="adamw_" + n)
        grads[n], deltas[n], new_m[n], new_v[n] = (a.reshape(shp) for a in (g2, dl, mo, vo))
    sm_out = _adamw_small([weights[n] for n in SMALL_NAMES], [small_grads[n] for n in SMALL_NAMES],
                          [moms[n] for n in SMALL_NAMES], [vars_[n] for n in SMALL_NAMES])
    ns = len(SMALL_NAMES)
    for i, n in enumerate(SMALL_NAMES):
        grads[n], deltas[n], new_m[n], new_v[n] = small_grads[n], sm_out[i], sm_out[ns + i], sm_out[2 * ns + i]

    return (loss, grad_x, *[grads[n] for n in WEIGHT_ORDER], *[deltas[n] for n in WEIGHT_ORDER],
            *[new_m[n] for n in WEIGHT_ORDER], *[new_v[n] for n in WEIGHT_ORDER])
```

```python
import numpy as np
import jax
import jax.numpy as jnp
from jax import lax
from jax.experimental import pallas as pl
from jax.experimental.pallas import tpu as pltpu

F32 = jnp.float32
_MXU_DTYPE = jnp.bfloat16

EPS = 1e-6
HG_HEADS = 4
HG_D = 128
HG_CHUNK = 64
HG_TILE = 512
HG_LEVELS = (32, 16, 8, 4, 2, 1)
SW_HEADS = 8
SW_KV_HEADS = 2
SW_GROUP = SW_HEADS // SW_KV_HEADS
SW_HD = 64
SW_BLOCK = 128
ROPE_THETA = 500000.0
ROT_DIM = SW_HD // 4
XA_HEADS = 4
XA_HD = 128
HG_COLS = 4 * HG_HEADS * HG_D
SW_COLS = (SW_HEADS + 2 * SW_KV_HEADS) * SW_HD

ADAM_LR = 0.001
ADAM_B1 = 0.9
ADAM_B2 = 0.999
ADAM_EPS = 1e-08
ADAM_WD = 0.01
ADAM_STEP = 10

VMEM_LIMIT = 56 * 1024 * 1024
MESH = pl.DeviceIdType.MESH

NN = ((1,), (0,))
NT = ((1,), (1,))
TN = ((0,), (0,))


def _mx(v):
    return v.astype(_MXU_DTYPE)


def _dot(a, b, dims=NN):
    return lax.dot_general(_mx(a), _mx(b), (dims, ((), ())), preferred_element_type=F32)


def _split_dot(a, v, dims, parts):
    acc = None
    rest = v
    for p in range(parts):
        piece = _mx(rest)
        term = lax.dot_general(a, piece, (dims, ((), ())), preferred_element_type=F32)
        acc = term if acc is None else acc + term
        if p + 1 < parts:
            rest = rest - piece.astype(F32)
    return acc


def _params(sem):
    return pltpu.CompilerParams(dimension_semantics=sem, vmem_limit_bytes=VMEM_LIMIT)


def _mm(a, b, mode, m, n, k, *, name, tm=1024, tn=1024, tk=1024, a_spec=None, b_spec=None, extras=(), rows=(),
        epilogue=None, out_dtypes=(F32,), row_sums=0, out_shape=None, out_spec=None, after=()):
    after = tuple(t for t in after if t is not None)
    tm, tn, tk = min(tm, m), min(tn, n), min(tk, k)
    assert m % tm == 0 and n % tn == 0 and k % tk == 0, (name, m, n, k, tm, tn, tk)
    gi, gj, gk = m // tm, n // tn, k // tk
    assert row_sums == 0 or gj == 1, name
    if a_spec is None:
        a_spec = (pl.BlockSpec((tk, tm), lambda i, j, kk: (kk, i)) if mode == TN
                  else pl.BlockSpec((tm, tk), lambda i, j, kk: (i, kk)))
    if b_spec is None:
        b_spec = (pl.BlockSpec((tn, tk), lambda i, j, kk: (j, kk)) if mode == NT
                  else pl.BlockSpec((tk, tn), lambda i, j, kk: (kk, j)))
    mn_spec = pl.BlockSpec((tm, tn), lambda i, j, kk: (i, j))
    if epilogue is None:
        epilogue = lambda acc: (acc,)
    row_spec = pl.BlockSpec((1, tn), lambda i, j, kk: (0, j))
    n_ex, n_out = len(extras) + len(rows), len(out_dtypes)
    if out_shape is None:
        out_shape = tuple(jax.ShapeDtypeStruct((m, n), d) for d in out_dtypes)
        out_spec = tuple(mn_spec for _ in out_dtypes)
    out_shape = tuple(out_shape) + tuple(jax.ShapeDtypeStruct((1, n), F32) for _ in range(row_sums))
    out_spec = tuple(out_spec) + tuple(row_spec for _ in range(row_sums))

    n_after = len(after)

    def body(*refs):
        a_ref, b_ref = refs[0], refs[1]
        ex = refs[2:2 + n_ex]
        outs = refs[2 + n_ex + n_after:2 + n_ex + n_after + n_out + row_sums]
        first_row_tile = pl.program_id(0) == 0

        def finish(acc):
            res = epilogue(acc, *[e[...] for e in ex])
            for o, r in zip(outs[:n_out], res[:n_out]):
                o[...] = r.astype(o.dtype)
            if row_sums:
                @pl.when(first_row_tile)
                def _():
                    for o in outs[n_out:]:
                        o[...] = jnp.zeros_like(o)

                for o, r in zip(outs[n_out:], res[n_out:]):
                    o[...] += r

        if gk == 1:
            finish(_dot(a_ref[...], b_ref[...], mode))
        else:
            acc_ref = refs[-1]
            kk = pl.program_id(2)

            @pl.when(kk == 0)
            def _():
                acc_ref[...] = jnp.zeros_like(acc_ref)

            acc_ref[...] += _dot(a_ref[...], b_ref[...], mode)

            @pl.when(kk == gk - 1)
            def _():
                finish(acc_ref[...])

    return pl.pallas_call(
        body, name=name, grid=(gi, gj, gk),
        in_specs=([a_spec, b_spec] + [mn_spec] * len(extras) + [row_spec] * len(rows)
                  + [pl.BlockSpec(memory_space=pl.ANY)] * n_after),
        out_specs=out_spec, out_shape=out_shape,
        scratch_shapes=[pltpu.VMEM((tm, tn), F32)] if gk > 1 else [],
        compiler_params=_params(("arbitrary" if row_sums else "parallel", "parallel", "arbitrary")),
    )(a, b, *extras, *rows, *after)


def _rms_rows(xv, g):
    return xv * lax.rsqrt(jnp.mean(xv * xv, axis=1, keepdims=True) + EPS) * g


def _rms_rows_bwd(xv, g, dyv):
    r = lax.rsqrt(jnp.mean(xv * xv, axis=1, keepdims=True) + EPS)
    u = dyv * g
    return (r * u - xv * (r * r * r) * jnp.mean(u * xv, axis=1, keepdims=True),
            jnp.sum(dyv * xv * r, axis=0, keepdims=True))


def _residual_rms(acc, res, g):
    h = acc + res
    return h, _rms_rows(h, g)


def _rms_bwd_residual(dhn, xv, dres, g):
    dx, dg = _rms_rows_bwd(xv, g, dhn)
    dx = dx + dres
    return dx, dx, dg


def _rms_fwd(x, g, *, name, tm=512, after=()):
    t, d = x.shape
    tm = min(tm, t)
    after = tuple(a for a in after if a is not None)

    def body(x_ref, g_ref, *rest):
        rest[-1][...] = _rms_rows(x_ref[...], g_ref[...]).astype(rest[-1].dtype)

    return pl.pallas_call(
        body, name=name, grid=(t // tm,),
        in_specs=[pl.BlockSpec((tm, d), lambda i: (i, 0)), pl.BlockSpec((1, d), lambda i: (0, 0))]
        + [pl.BlockSpec(memory_space=pl.ANY)] * len(after),
        out_specs=pl.BlockSpec((tm, d), lambda i: (i, 0)),
        out_shape=jax.ShapeDtypeStruct((t, d), _MXU_DTYPE),
        compiler_params=_params(("parallel",)),
    )(x, g, *after)


def _rms_gain_grad(x, g, dy, *, name, tm=512):
    t, d = x.shape
    tm = min(tm, t)

    def body(x_ref, g_ref, dy_ref, dg_ref):
        @pl.when(pl.program_id(0) == 0)
        def _():
            dg_ref[...] = jnp.zeros_like(dg_ref)

        dg_ref[...] += _rms_rows_bwd(x_ref[...], g_ref[...], dy_ref[...])[1]

    row = pl.BlockSpec((tm, d), lambda i: (i, 0))
    vec = pl.BlockSpec((1, d), lambda i: (0, 0))
    return pl.pallas_call(
        body, name=name, grid=(t // tm,), in_specs=[row, vec, row], out_specs=vec,
        out_shape=jax.ShapeDtypeStruct((1, d), F32), compiler_params=_params(("arbitrary",)),
    )(x, g, dy)


def _hg_constants():
    c = HG_CHUNK
    t = np.arange(c)
    sums = [t[None, :] <= t[:, None]]
    masks = []
    for m in HG_LEVELS:
        base = (t // (2 * m)) * (2 * m)
        mid = base + m - 1
        second = (t - base) >= m
        upper = (t[None, :] > mid[:, None]) & (t[None, :] <= t[:, None])
        lower = (t[None, :] > t[:, None]) & (t[None, :] <= mid[:, None])
        sums.append(np.where(second[:, None], upper, lower))
        masks.append(second[:, None] & (~second)[None, :] & (base[:, None] == base[None, :]))
    return (np.concatenate(sums, axis=0).astype(np.float32), np.stack(masks).astype(np.float32))


HG_HEAD_LANES = tuple(slice(HG_D * h, HG_D * (h + 1)) for h in range(HG_HEADS))


def _per_head(fn, slab):
    return jnp.concatenate([jnp.broadcast_to(fn(slab[:, hs]), (slab.shape[0], HG_D)) for hs in HG_HEAD_LANES], axis=1)


def _lane_sum(v):
    return jnp.sum(v, axis=1, keepdims=True)


def _lane_mean(v):
    return jnp.mean(v, axis=1, keepdims=True)


def _hg_gates(blk, lbp):
    w = HG_HEADS * HG_D
    q, x, v, gl = blk[:, 0:w], blk[:, w:2 * w], blk[:, 2 * w:3 * w], blk[:, 3 * w:4 * w]
    mx = jnp.max(lbp, axis=0, keepdims=True)
    e = jnp.exp(lbp - mx)
    lb = e[0:1, :] / jnp.sum(e, axis=0, keepdims=True)
    sig = jax.nn.sigmoid(x)
    f = lb + (1.0 - lb) * sig
    return q, v, gl, lb, sig, f, 1.0 - f, jnp.log(f)


def _hg_fwd(proj, lbp, ng, bsz, seq, *, y_width):
    t = proj.shape[0]
    nc = seq // HG_CHUNK
    a_np, m_np = _hg_constants()
    a_all = jnp.asarray(a_np, _MXU_DTYPE)
    masks = jnp.asarray(m_np, F32)
    nl = len(HG_LEVELS)

    ts = min(HG_TILE, seq)
    ns, nct = seq // ts, ts // HG_CHUNK
    hw = HG_HEADS * HG_D

    def body(p_ref, lb_ref, ng_ref, a_ref, m_ref, y_ref, o_ref, st_ref, carry):
        a_mat = a_ref[...]
        ngv = ng_ref[...]

        @pl.when(pl.program_id(0) == 0)
        def _():
            carry[...] = jnp.zeros_like(carry)

        ng4 = _tile_lanes(ngv, HG_HEADS)
        heads = range(HG_HEADS)
        exs = range(bsz)
        hl = HG_HEAD_LANES
        lbp_v = lb_ref[...]

        def chunk(c, _):
            rows = pl.ds(pl.multiple_of(c * HG_CHUNK, HG_CHUNK), HG_CHUNK)
            gates = [_hg_gates(p_ref[e, rows, :], lbp_v) for e in exs]
            q, v, gl = [g[0] for g in gates], [g[1] for g in gates], [g[2] for g in gates]
            k = [g[6] for g in gates]
            sts = [[carry[e, h] for h in heads] for e in exs]
            e_all = [_split_dot(a_mat, gates[e][7], NN, 3) for e in exs]
            b = [e_all[e][0:HG_CHUNK] for e in exs]
            qb = [q[e] * jnp.exp(b[e]) for e in exs]
            o = [[_dot(qb[e][:, hl[h]], sts[e][h], NT) for h in heads] for e in exs]
            p = [[jnp.zeros((HG_CHUNK, HG_CHUNK), F32) for _ in heads] for _ in exs]
            for li in range(nl):
                dec = [jnp.exp(e_all[e][HG_CHUNK * (li + 1):HG_CHUNK * (li + 2)]) for e in exs]
                qm, km, mk = [q[e] * dec[e] for e in exs], [k[e] * dec[e] for e in exs], m_ref[li]
                p = [[p[e][h] + mk * _dot(qm[e][:, hl[h]], km[e][:, hl[h]], NT) for h in heads] for e in exs]
            bl = [b[e][HG_CHUNK - 1:HG_CHUNK, :] for e in exs]
            kd = [k[e] * jnp.exp(bl[e] - b[e]) for e in exs]
            pv = [[_dot(p[e][h], v[e][:, hl[h]]) for h in heads] for e in exs]
            upd = [[_dot(v[e][:, hl[h]], kd[e][:, hl[h]], TN) for h in heads] for e in exs]
            for e in exs:
                o_all = (jnp.concatenate([o[e][h] + pv[e][h] for h in heads], axis=1)
                         + _per_head(_lane_sum, q[e] * k[e]) * v[e])
                r = lax.rsqrt(_per_head(_lane_mean, o_all * o_all) + EPS)
                ebl = jnp.exp(bl[e])
                for h in heads:
                    st_ref[e, h, c] = sts[e][h]
                    carry[e, h] = sts[e][h] * ebl[:, hl[h]] + upd[e][h]
                o_ref[e, rows, :] = o_all
                y_ref[e, rows, :] = (o_all * r * ng4) * (gl[e] * jax.nn.sigmoid(gl[e]))
            return 0

        lax.fori_loop(0, nct, chunk, 0)

    y3, o3, states = pl.pallas_call(
        body, name="hgrn2_fwd", grid=(ns,),
        in_specs=[pl.BlockSpec((bsz, ts, HG_COLS), lambda s: (0, s, 0)),
                  pl.BlockSpec((2, hw), lambda s: (0, 0)),
                  pl.BlockSpec((1, HG_D), lambda s: (0, 0)),
                  pl.BlockSpec(a_all.shape, lambda s: (0, 0)),
                  pl.BlockSpec(masks.shape, lambda s: (0, 0, 0))],
        out_specs=(pl.BlockSpec((bsz, ts, hw), lambda s: (0, s, 0)),
                   pl.BlockSpec((bsz, ts, hw), lambda s: (0, s, 0)),
                   pl.BlockSpec((bsz, HG_HEADS, nct, HG_D, HG_D), lambda s: (0, 0, s, 0, 0))),
        out_shape=(jax.ShapeDtypeStruct((bsz, seq, y_width), F32),
                   jax.ShapeDtypeStruct((bsz, seq, hw), F32),
                   jax.ShapeDtypeStruct((bsz, HG_HEADS, nc, HG_D, HG_D), F32)),
        scratch_shapes=[pltpu.VMEM((bsz, HG_HEADS, HG_D, HG_D), F32)],
        compiler_params=_params(("arbitrary",)),
    )(proj.reshape(bsz, seq, HG_COLS), lbp, ng, a_all, masks)
    return y3.reshape(t, y_width), o3.reshape(t, hw), states


def _hg_bwd(proj, lbp, ng, o_all, states, dy, bsz, seq, after=()):
    after = tuple(a for a in after if a is not None)
    t = proj.shape[0]
    nc = seq // HG_CHUNK
    a_np, m_np = _hg_constants()
    a_all = jnp.asarray(a_np, _MXU_DTYPE)
    masks = jnp.asarray(m_np, F32)
    nl = len(HG_LEVELS)
    cs = HG_CHUNK

    ts = min(HG_TILE, seq)
    ns, nct = seq // ts, ts // cs
    hw = HG_HEADS * HG_D

    def body(p_ref, lb_ref, ng_ref, a_ref, m_ref, o_ref, st_ref, dy_ref, *rest):
        dp_ref, dlb_ref, dng_ref, dst_ref = rest[len(after):]
        a_mat = a_ref[...]
        ngv = ng_ref[...]
        ng4 = _tile_lanes(ngv, HG_HEADS)
        last_row = lax.broadcasted_iota(jnp.int32, (cs, hw), 0) == cs - 1
        first = pl.program_id(0) == 0
        heads = range(HG_HEADS)
        exs = range(bsz)
        hl = HG_HEAD_LANES
        lbp_v = lb_ref[...]

        @pl.when(first)
        def _():
            dst_ref[...] = jnp.zeros_like(dst_ref)

        def side_by_side(parts):
            return jnp.concatenate(parts, axis=1)

        def chunk(i, carry):
            dlb_acc, dng_acc = carry
            c = nct - 1 - i
            rows = pl.ds(pl.multiple_of(c * cs, cs), cs)
            gates = [_hg_gates(p_ref[e, rows, :], lbp_v) for e in exs]
            q, v, gl = [g[0] for g in gates], [g[1] for g in gates], [g[2] for g in gates]
            lb, sig, f, k = gates[0][3], [g[4] for g in gates], [g[5] for g in gates], [g[6] for g in gates]
            o = [o_ref[e, rows, :] for e in exs]
            dyv = [dy_ref[e, rows, :] for e in exs]
            sts = [[st_ref[e, h, c] for h in heads] for e in exs]
            dsts = [[dst_ref[e, h] for h in heads] for e in exs]
            e_all = [_split_dot(a_mat, gates[e][7], NN, 3) for e in exs]
            b = [e_all[e][0:cs] for e in exs]
            eb = [jnp.exp(b[e]) for e in exs]
            bl = [b[e][cs - 1:cs, :] for e in exs]
            ebl = [jnp.exp(bl[e]) for e in exs]
            ekd = [jnp.exp(bl[e] - b[e]) for e in exs]
            qb = [q[e] * eb[e] for e in exs]
            kd = [k[e] * ekd[e] for e in exs]
            do, dgl = [], []
            for e in exs:
                sg = jax.nn.sigmoid(gl[e])
                silu = gl[e] * sg
                r = lax.rsqrt(_per_head(_lane_mean, o[e] * o[e]) + EPS)
                dgl.append(dyv[e] * (o[e] * r * ng4) * (sg * (1.0 + gl[e] * (1.0 - sg))))
                u = dyv[e] * silu * ng4
                do.append(r * u - o[e] * (r * r * r) * _per_head(_lane_mean, u * o[e]))
                dng4 = jnp.sum(dyv[e] * silu * o[e] * r, axis=0, keepdims=True)
                dng_acc = dng_acc + ((dng4[:, hl[0]] + dng4[:, hl[1]]) + (dng4[:, hl[2]] + dng4[:, hl[3]]))
            es, qm, km = [], [], []
            p = [[jnp.zeros((cs, cs), F32) for _ in heads] for _ in exs]
            for li in range(nl):
                dec = [jnp.exp(e_all[e][cs * (li + 1):cs * (li + 2)]) for e in exs]
                es.append(dec)
                qm.append([q[e] * dec[e] for e in exs])
                km.append([k[e] * dec[e] for e in exs])
                mk = m_ref[li]
                p = [[p[e][h] + mk * _dot(qm[li][e][:, hl[h]], km[li][e][:, hl[h]], NT) for h in heads] for e in exs]
            dp = [[_dot(do[e][:, hl[h]], v[e][:, hl[h]], NT) for h in heads] for e in exs]
            dv_p = [[_dot(p[e][h], do[e][:, hl[h]], TN) for h in heads] for e in exs]
            dv_s = [[_dot(kd[e][:, hl[h]], dsts[e][h], NT) for h in heads] for e in exs]
            dqb = [side_by_side([_dot(do[e][:, hl[h]], sts[e][h]) for h in heads]) for e in exs]
            dkd = [side_by_side([_dot(v[e][:, hl[h]], dsts[e][h]) for h in heads]) for e in exs]
            new_dst = [[_dot(do[e][:, hl[h]], qb[e][:, hl[h]], TN) for h in heads] for e in exs]
            dv = [side_by_side([dv_p[e][h] + dv_s[e][h] for h in heads]) + _per_head(_lane_sum, q[e] * k[e]) * do[e]
                  for e in exs]
            dq = [dqb[e] * eb[e] for e in exs]
            dk = [dkd[e] * ekd[e] for e in exs]
            de = []
            for e in exs:
                dbl = (jnp.sum(dkd[e] * kd[e], axis=0, keepdims=True)
                       + side_by_side([jnp.sum(dsts[e][h] * sts[e][h], axis=0, keepdims=True) for h in heads]) * ebl[e])
                de.append([dqb[e] * qb[e] - dkd[e] * kd[e] + jnp.where(last_row, dbl, 0.0)])
            for li in range(nl):
                mk = m_ref[li]
                dpm = [[mk * dp[e][h] for h in heads] for e in exs]
                dqm = [side_by_side([_dot(dpm[e][h], km[li][e][:, hl[h]]) for h in heads]) for e in exs]
                dkm = [side_by_side([_dot(dpm[e][h], qm[li][e][:, hl[h]], TN) for h in heads]) for e in exs]
                for e in exs:
                    dq[e] = dq[e] + dqm[e] * es[li][e]
                    dk[e] = dk[e] + dkm[e] * es[li][e]
                    de[e].append(dqm[e] * qm[li][e] + dkm[e] * km[li][e])
            dg = [_split_dot(a_mat, jnp.concatenate(de[e], axis=0), TN, 2) for e in exs]
            for e in exs:
                dpd = _per_head(_lane_sum, do[e] * v[e])
                df = dg[e] / f[e] - (dk[e] + dpd * q[e])
                dp_ref[e, rows, 0:hw] = _mx(dq[e] + dpd * k[e])
                dp_ref[e, rows, hw:2 * hw] = _mx(df * (1.0 - lb) * sig[e] * (1.0 - sig[e]))
                dp_ref[e, rows, 2 * hw:3 * hw] = _mx(dv[e])
                dp_ref[e, rows, 3 * hw:4 * hw] = _mx(dgl[e])
                for h in heads:
                    dst_ref[e, h] = dsts[e][h] * ebl[e][:, hl[h]] + new_dst[e][h]
                dlb_acc = dlb_acc + jnp.sum(df * (1.0 - sig[e]), axis=0, keepdims=True)
            return dlb_acc, dng_acc

        dlb, dng = lax.fori_loop(0, nct, chunk, (jnp.zeros((1, hw), F32), jnp.zeros((1, HG_D), F32)))

        @pl.when(first)
        def _():
            dlb_ref[...] = jnp.zeros_like(dlb_ref)
            dng_ref[...] = jnp.zeros_like(dng_ref)

        mx = jnp.max(lbp_v, axis=0, keepdims=True)
        e = jnp.exp(lbp_v - mx)
        s0 = e[0:1, :] / jnp.sum(e, axis=0, keepdims=True)
        da0 = dlb * s0 * (1.0 - s0)
        dlb_ref[...] += jnp.concatenate([da0, -da0], axis=0)
        dng_ref[...] += dng

    rows3 = lambda w: pl.BlockSpec((bsz, ts, w), lambda s: (0, ns - 1 - s, 0))
    dproj, dlb, dng = pl.pallas_call(
        body, name="hgrn2_bwd", grid=(ns,),
        in_specs=[rows3(HG_COLS),
                  pl.BlockSpec((2, hw), lambda s: (0, 0)),
                  pl.BlockSpec((1, HG_D), lambda s: (0, 0)),
                  pl.BlockSpec(a_all.shape, lambda s: (0, 0)),
                  pl.BlockSpec(masks.shape, lambda s: (0, 0, 0)),
                  rows3(hw),
                  pl.BlockSpec((bsz, HG_HEADS, nct, HG_D, HG_D), lambda s: (0, 0, ns - 1 - s, 0, 0)),
                  rows3(hw)] + [pl.BlockSpec(memory_space=pl.ANY)] * len(after),
        out_specs=(rows3(HG_COLS),
                   pl.BlockSpec((2, hw), lambda s: (0, 0)),
                   pl.BlockSpec((1, HG_D), lambda s: (0, 0))),
        out_shape=(jax.ShapeDtypeStruct((bsz, seq, HG_COLS), _MXU_DTYPE),
                   jax.ShapeDtypeStruct((2, hw), F32),
                   jax.ShapeDtypeStruct((1, HG_D), F32)),
        scratch_shapes=[pltpu.VMEM((bsz, HG_HEADS, HG_D, HG_D), F32)],
        compiler_params=_params(("arbitrary",)),
    )(proj.reshape(bsz, seq, HG_COLS), lbp, ng, a_all, masks, o_all.reshape(bsz, seq, hw), states,
      dy.reshape(bsz, seq, dy.shape[1]), *after)
    return dproj.reshape(t, HG_COLS), dlb, dng


def _sw_constants():
    half = ROT_DIM // 2
    inv = (np.float32(ROPE_THETA) ** (-(np.arange(half, dtype=np.float32) * np.float32(2.0) / np.float32(ROT_DIM)))
           ).astype(np.float32)
    freq = np.zeros((1, 128), np.float32)
    sign = np.zeros((1, 128), np.float32)
    for h in range(2):
        freq[0, 64 * h:64 * h + half] = inv
        freq[0, 64 * h + half:64 * h + 2 * half] = inv
        sign[0, 64 * h:64 * h + half] = -1.0
        sign[0, 64 * h + half:64 * h + 2 * half] = 1.0
    seg = np.kron(np.eye(8, dtype=np.float32), np.full((64, 64), 1.0 / 64.0, np.float32))
    return freq, sign, seg


def _rope_tables(pos, freq, sign):
    ang = pos.astype(F32) * freq
    return jnp.cos(ang), jnp.sin(ang) * sign


def _tile_lanes(v, times):
    return v if times == 1 else jnp.concatenate([v] * times, axis=1)


def _swap_halves(v):
    w = v.shape[1]
    half = ROT_DIM // 2
    lane = lax.broadcasted_iota(jnp.int32, v.shape, 1) % SW_HD
    return jnp.where(lane < half, pltpu.roll(v, w - half, 1), jnp.where(lane < 2 * half, pltpu.roll(v, half, 1), 0.0))


def _sw_norm_rope(tv, gain, seg, cosv, sinv):
    w = tv.shape[1]
    ms = _split_dot_rhs(tv * tv, seg[0:w, 0:w])
    r = lax.rsqrt(ms + EPS)
    tn = tv * r * gain
    reps = w // 128
    return tn * _tile_lanes(cosv, reps) + _swap_halves(tn) * _tile_lanes(sinv, reps), r


def _split_dot_rhs(v, a):
    hi = _mx(v)
    lo = _mx(v - hi.astype(F32))
    return (lax.dot_general(hi, a, (NN, ((), ())), preferred_element_type=F32)
            + lax.dot_general(lo, a, (NN, ((), ())), preferred_element_type=F32))


def _sw_norm_rope_bwd(dt, tv, r, gain, seg, cosv, sinv):
    w = tv.shape[1]
    reps = w // 128
    dtn = dt * _tile_lanes(cosv, reps) + _swap_halves(dt * _tile_lanes(sinv, reps))
    u = dtn * gain
    dtv = r * u - tv * (r * r * r) * _split_dot_rhs(u * tv, seg[0:w, 0:w])
    return dtv, jnp.sum(dtn * tv * r, axis=0, keepdims=True)


def _sw_scores(qh, kp, kc):
    return _dot(qh, kp, NT), _dot(qh, kc, NT)


def _sw_probs(raw, sink, first_block):
    scale = SW_HD ** -0.5
    qi = lax.broadcasted_iota(jnp.int32, (SW_BLOCK, SW_BLOCK), 0)
    kj = lax.broadcasted_iota(jnp.int32, (SW_BLOCK, SW_BLOCK), 1)
    ok_prev = jnp.logical_and(kj > qi, jnp.logical_not(first_block))
    ok_cur = kj <= qi
    sp = jnp.where(ok_prev, raw[0] * scale, -jnp.inf)
    sc = jnp.where(ok_cur, raw[1] * scale, -jnp.inf)
    m = jnp.maximum(jnp.maximum(jnp.max(sp, axis=1, keepdims=True), jnp.max(sc, axis=1, keepdims=True)), sink)
    pp, pc = jnp.exp(sp - m), jnp.exp(sc - m)
    es = jnp.exp(sink - m)
    den = jnp.sum(pp, axis=1, keepdims=True) + jnp.sum(pc, axis=1, keepdims=True) + es
    return pp / den, pc / den, es / den


def _sw_specs(nb):
    def cur(b, n):
        return b * nb + jnp.minimum(n, nb - 1)

    def prev(b, n):
        return b * nb + jnp.maximum(jnp.minimum(n, nb - 1) - 1, 0)

    return cur, prev


def _sw_fwd(proj, pos, qg, kg, sinks, y_in, bsz, seq):
    t = proj.shape[0]
    nb = seq // SW_BLOCK
    freq_np, sign_np, seg_np = _sw_constants()
    freq, sign = jnp.asarray(freq_np), jnp.asarray(sign_np)
    seg = jnp.asarray(seg_np, _MXU_DTYPE)
    cur, prev = _sw_specs(nb)

    def body(q_ref, kc_ref, kp_ref, vc_ref, vp_ref, pc_ref, pp_ref, qg_ref, kg_ref, sk_ref, fr_ref, sn_ref, seg_ref,
             yin_ref, y_ref):
        del yin_ref
        n = pl.program_id(1)
        segv = seg_ref[...]
        cos_c, sin_c = _rope_tables(pc_ref[...], fr_ref[...], sn_ref[...])
        cos_p, sin_p = _rope_tables(pp_ref[...], fr_ref[...], sn_ref[...])
        qr, _ = _sw_norm_rope(q_ref[...], qg_ref[...], segv, cos_c, sin_c)
        kcr, _ = _sw_norm_rope(kc_ref[...], kg_ref[...], segv, cos_c, sin_c)
        kpr, _ = _sw_norm_rope(kp_ref[...], kg_ref[...], segv, cos_p, sin_p)
        vc, vp = vc_ref[...], vp_ref[...]
        ks = [slice(SW_HD * (h // SW_GROUP), SW_HD * (h // SW_GROUP + 1)) for h in range(SW_HEADS)]
        raw = [_sw_scores(qr[:, SW_HD * h:SW_HD * (h + 1)], kpr[:, ks[h]], kcr[:, ks[h]]) for h in range(SW_HEADS)]
        probs = [_sw_probs(raw[h], sk_ref[0, h], n == 0) for h in range(SW_HEADS)]
        for h in range(SW_HEADS):
            y_ref[:, SW_HD * h:SW_HD * (h + 1)] = _dot(probs[h][0], vp[:, ks[h]]) + _dot(probs[h][1], vc[:, ks[h]])

    rowq = pl.BlockSpec((SW_BLOCK, 512), lambda b, n: (cur(b, n), 0))
    full = lambda a: pl.BlockSpec(a.shape, lambda b, n: (0,) * a.ndim)
    yw = y_in.shape[1]
    return pl.pallas_call(
        body, name="swa_fwd", grid=(bsz, nb),
        in_specs=[rowq,
                  pl.BlockSpec((SW_BLOCK, 128), lambda b, n: (cur(b, n), 4)),
                  pl.BlockSpec((SW_BLOCK, 128), lambda b, n: (prev(b, n), 4)),
                  pl.BlockSpec((SW_BLOCK, 128), lambda b, n: (cur(b, n), 5)),
                  pl.BlockSpec((SW_BLOCK, 128), lambda b, n: (prev(b, n), 5)),
                  pl.BlockSpec((SW_BLOCK, 1), lambda b, n: (cur(b, n), 0)),
                  pl.BlockSpec((SW_BLOCK, 1), lambda b, n: (prev(b, n), 0)),
                  full(qg), full(kg),
                  pl.BlockSpec(memory_space=pltpu.SMEM),
                  full(freq), full(sign), full(seg),
                  pl.BlockSpec(memory_space=pl.ANY)],
        out_specs=pl.BlockSpec((SW_BLOCK, 512), lambda b, n: (cur(b, n), 1)),
        out_shape=jax.ShapeDtypeStruct((t, yw), F32),
        input_output_aliases={13: 0},
        compiler_params=_params(("parallel", "parallel")),
    )(proj, proj, proj, proj, proj, pos, pos, qg, kg, sinks, freq, sign, seg, y_in)


def _sw_bwd(proj, pos, qg, kg, sinks, y, dy, bsz, seq):
    t = proj.shape[0]
    nb = seq // SW_BLOCK
    freq_np, sign_np, seg_np = _sw_constants()
    freq, sign = jnp.asarray(freq_np), jnp.asarray(sign_np)
    seg = jnp.asarray(seg_np, _MXU_DTYPE)
    cur, prev = _sw_specs(nb)
    scale = SW_HD ** -0.5

    def body(q_ref, kc_ref, kp_ref, vc_ref, vp_ref, pc_ref, pp_ref, qg_ref, kg_ref, sk_ref, fr_ref, sn_ref, seg_ref,
             y_ref, dy_ref, dp_ref, dqg_ref, dkg_ref, dsk_ref,
             dq_car, dkv_car, dqr_s, dkc_s, dkp_s, dvc_s, dvp_s, gq_acc, gk_acc, sk_acc):
        b, n = pl.program_id(0), pl.program_id(1)
        first = jnp.logical_and(b == 0, n == 0)
        last = jnp.logical_and(b == pl.num_programs(0) - 1, n == nb)

        @pl.when(first)
        def _():
            gq_acc[...] = jnp.zeros_like(gq_acc)
            gk_acc[...] = jnp.zeros_like(gk_acc)
            sk_acc[...] = jnp.zeros_like(sk_acc)

        @pl.when(n < nb)
        def _():
            segv = seg_ref[...]
            cos_c, sin_c = _rope_tables(pc_ref[...], fr_ref[...], sn_ref[...])
            cos_p, sin_p = _rope_tables(pp_ref[...], fr_ref[...], sn_ref[...])
            qv, kcv, kpv = q_ref[...], kc_ref[...], kp_ref[...]
            qr, rq = _sw_norm_rope(qv, qg_ref[...], segv, cos_c, sin_c)
            kcr, rkc = _sw_norm_rope(kcv, kg_ref[...], segv, cos_c, sin_c)
            kpr, rkp = _sw_norm_rope(kpv, kg_ref[...], segv, cos_p, sin_p)
            vc, vp = vc_ref[...], vp_ref[...]
            lane = lax.broadcasted_iota(jnp.int32, (1, 128), 1)
            dsk = jnp.zeros((1, 128), F32)
            heads = range(SW_HEADS)
            ks = [slice(SW_HD * (h // SW_GROUP), SW_HD * (h // SW_GROUP + 1)) for h in heads]
            hs = [slice(SW_HD * h, SW_HD * (h + 1)) for h in heads]
            qh = [qr[:, hs[h]] for h in heads]
            doh = [dy_ref[:, hs[h]] for h in heads]
            raw = [_sw_scores(qh[h], kpr[:, ks[h]], kcr[:, ks[h]]) for h in heads]
            dpp = [_dot(doh[h], vp[:, ks[h]], NT) for h in heads]
            dpc = [_dot(doh[h], vc[:, ks[h]], NT) for h in heads]
            probs = [_sw_probs(raw[h], sk_ref[0, h], n == 0) for h in heads]
            dsp, dsc = [], []
            for h in heads:
                pp, pc, ps = probs[h]
                delta = jnp.sum(doh[h] * y_ref[:, hs[h]], axis=1, keepdims=True)
                dsp.append(pp * (dpp[h] - delta) * scale)
                dsc.append(pc * (dpc[h] - delta) * scale)
                dsk = dsk + jnp.where(lane == h, -jnp.sum(ps * delta), 0.0)
            for h in heads:
                dqr_s[:, hs[h]] = _dot(dsp[h], kpr[:, ks[h]]) + _dot(dsc[h], kcr[:, ks[h]])
            for kv in range(SW_KV_HEADS):
                group = range(SW_GROUP * kv, SW_GROUP * (kv + 1))
                kvs = slice(SW_HD * kv, SW_HD * (kv + 1))
                dvp_s[:, kvs] = sum(_dot(probs[h][0], doh[h], TN) for h in group)
                dvc_s[:, kvs] = sum(_dot(probs[h][1], doh[h], TN) for h in group)
                dkp_s[:, kvs] = sum(_dot(dsp[h], qh[h], TN) for h in group)
                dkc_s[:, kvs] = sum(_dot(dsc[h], qh[h], TN) for h in group)
            dq, gq = _sw_norm_rope_bwd(dqr_s[...], qv, rq, qg_ref[...], segv, cos_c, sin_c)
            dkc, gkc = _sw_norm_rope_bwd(dkc_s[...], kcv, rkc, kg_ref[...], segv, cos_c, sin_c)
            dkp, gkp = _sw_norm_rope_bwd(dkp_s[...], kpv, rkp, kg_ref[...], segv, cos_p, sin_p)
            gq_acc[...] += gq
            gk_acc[...] += gkc + gkp
            sk_acc[...] += dsk

            @pl.when(n > 0)
            def _():
                dp_ref[:, 0:512] = _mx(dq_car[...])
                dp_ref[:, 512:640] = _mx(dkv_car[:, 0:128] + dkp)
                dp_ref[:, 640:768] = _mx(dkv_car[:, 128:256] + dvp_s[...])

            dq_car[...] = dq
            dkv_car[:, 0:128] = dkc
            dkv_car[:, 128:256] = dvc_s[...]

        @pl.when(n == nb)
        def _():
            dp_ref[:, 0:512] = _mx(dq_car[...])
            dp_ref[:, 512:768] = _mx(dkv_car[...])

        @pl.when(last)
        def _():
            gq = gq_acc[...]
            acc = gq[:, 0:SW_HD]
            for h in range(1, SW_HEADS):
                acc = acc + gq[:, SW_HD * h:SW_HD * (h + 1)]
            dqg_ref[...] = acc
            gk = gk_acc[...]
            dkg_ref[...] = gk[:, 0:SW_HD] + gk[:, SW_HD:2 * SW_HD]
            dsk_ref[...] = sk_acc[...]

    rowq = pl.BlockSpec((SW_BLOCK, 512), lambda b, n: (cur(b, n), 0))
    full = lambda a: pl.BlockSpec(a.shape, lambda b, n: (0,) * a.ndim)

    def out_row(b, n):
        return b * nb + jnp.maximum(n - 1, 0)

    return pl.pallas_call(
        body, name="swa_bwd", grid=(bsz, nb + 1),
        in_specs=[rowq,
                  pl.BlockSpec((SW_BLOCK, 128), lambda b, n: (cur(b, n), 4)),
                  pl.BlockSpec((SW_BLOCK, 128), lambda b, n: (prev(b, n), 4)),
                  pl.BlockSpec((SW_BLOCK, 128), lambda b, n: (cur(b, n), 5)),
                  pl.BlockSpec((SW_BLOCK, 128), lambda b, n: (prev(b, n), 5)),
                  pl.BlockSpec((SW_BLOCK, 1), lambda b, n: (cur(b, n), 0)),
                  pl.BlockSpec((SW_BLOCK, 1), lambda b, n: (prev(b, n), 0)),
                  full(qg), full(kg),
                  pl.BlockSpec(memory_space=pltpu.SMEM),
                  full(freq), full(sign), full(seg),
                  pl.BlockSpec((SW_BLOCK, 512), lambda b, n: (cur(b, n), 1)),
                  pl.BlockSpec((SW_BLOCK, 512), lambda b, n: (cur(b, n), 1))],
        out_specs=(pl.BlockSpec((SW_BLOCK, SW_COLS), lambda b, n: (out_row(b, n), 0)),
                   pl.BlockSpec((1, SW_HD), lambda b, n: (0, 0)),
                   pl.BlockSpec((1, SW_HD), lambda b, n: (0, 0)),
                   pl.BlockSpec((1, 128), lambda b, n: (0, 0))),
        out_shape=(jax.ShapeDtypeStruct((t, SW_COLS), _MXU_DTYPE),
                   jax.ShapeDtypeStruct((1, SW_HD), F32),
                   jax.ShapeDtypeStruct((1, SW_HD), F32),
                   jax.ShapeDtypeStruct((1, 128), F32)),
        scratch_shapes=[pltpu.VMEM((SW_BLOCK, 512), F32), pltpu.VMEM((SW_BLOCK, 256), F32),
                        pltpu.VMEM((SW_BLOCK, 512), F32),
                        pltpu.VMEM((SW_BLOCK, 128), F32), pltpu.VMEM((SW_BLOCK, 128), F32),
                        pltpu.VMEM((SW_BLOCK, 128), F32), pltpu.VMEM((SW_BLOCK, 128), F32),
                        pltpu.VMEM((1, 512), F32), pltpu.VMEM((1, 128), F32), pltpu.VMEM((1, 128), F32)],
        compiler_params=_params(("arbitrary", "arbitrary")),
    )(proj, proj, proj, proj, proj, pos, pos, qg, kg, sinks, freq, sign, seg, y, dy)


def _head_rms(tv, gain):
    r = lax.rsqrt(jnp.mean(tv * tv, axis=1, keepdims=True) + EPS)
    return tv * r * gain, r


def _head_rms_bwd(dtn, tv, r, gain):
    u = dtn * gain
    return r * u - tv * (r * r * r) * jnp.mean(u * tv, axis=1, keepdims=True), jnp.sum(dtn * tv * r, axis=0, keepdims=True)


def _xa_softmax(raw):
    s = raw * (XA_HD ** -0.5)
    e = jnp.exp(s - jnp.max(s, axis=1, keepdims=True))
    return e / jnp.sum(e, axis=1, keepdims=True)


def _xa_fwd(qx, kvx, qg, kg, bsz, seq, mlen, *, tq=512):
    t = qx.shape[0]
    tq = min(tq, seq)
    nq = seq // tq
    w = XA_HEADS * XA_HD

    def body(q_ref, kv_ref, qg_ref, kg_ref, o_ref):
        heads = range(XA_HEADS)
        hs = [slice(XA_HD * h, XA_HD * (h + 1)) for h in heads]
        qn = [_head_rms(q_ref[:, hs[h]], qg_ref[...])[0] for h in heads]
        kn = [_head_rms(kv_ref[:, hs[h]], kg_ref[...])[0] for h in heads]
        raw = [_dot(qn[h], kn[h], NT) for h in heads]
        p = [_xa_softmax(raw[h]) for h in heads]
        for h in heads:
            o_ref[:, hs[h]] = _dot(p[h], kv_ref[:, w + XA_HD * h:w + XA_HD * (h + 1)]).astype(o_ref.dtype)

    vec = pl.BlockSpec((1, XA_HD), lambda b, i: (0, 0))
    return pl.pallas_call(
        body, name="xattn_fwd", grid=(bsz, nq),
        in_specs=[pl.BlockSpec((tq, w), lambda b, i: (b * nq + i, 0)),
                  pl.BlockSpec((mlen, 2 * w), lambda b, i: (b, 0)), vec, vec],
        out_specs=pl.BlockSpec((tq, w), lambda b, i: (b * nq + i, 0)),
        out_shape=jax.ShapeDtypeStruct((t, w), _MXU_DTYPE),
        compiler_params=_params(("parallel", "parallel")),
    )(qx, kvx, qg, kg)


def _xa_bwd(qx, kvx, qg, kg, do, bsz, seq, mlen, *, tq=512):
    t = qx.shape[0]
    tq = min(tq, seq)
    nq = seq // tq
    w = XA_HEADS * XA_HD
    scale = XA_HD ** -0.5

    def body(q_ref, kv_ref, qg_ref, kg_ref, do_ref, dq_ref, dkv_ref, dqg_ref, dkg_ref):
        b, i = pl.program_id(0), pl.program_id(1)

        @pl.when(jnp.logical_and(b == 0, i == 0))
        def _():
            dqg_ref[...] = jnp.zeros_like(dqg_ref)
            dkg_ref[...] = jnp.zeros_like(dkg_ref)

        @pl.when(i == 0)
        def _():
            dkv_ref[...] = jnp.zeros_like(dkv_ref)

        heads = range(XA_HEADS)
        hs = [slice(XA_HD * h, XA_HD * (h + 1)) for h in heads]
        vs = [slice(w + XA_HD * h, w + XA_HD * (h + 1)) for h in heads]
        qv = [q_ref[:, hs[h]] for h in heads]
        kv = [kv_ref[:, hs[h]] for h in heads]
        doh = [do_ref[:, hs[h]] for h in heads]
        qn = [_head_rms(qv[h], qg_ref[...]) for h in heads]
        kn = [_head_rms(kv[h], kg_ref[...]) for h in heads]
        raw = [_dot(qn[h][0], kn[h][0], NT) for h in heads]
        dp = [_dot(doh[h], kv_ref[:, vs[h]], NT) for h in heads]
        p = [_xa_softmax(raw[h]) for h in heads]
        ds = [p[h] * (dp[h] - jnp.sum(p[h] * dp[h], axis=1, keepdims=True)) * scale for h in heads]
        dqn = [_dot(ds[h], kn[h][0]) for h in heads]
        dkn = [_dot(ds[h], qn[h][0], TN) for h in heads]
        dvv = [_dot(p[h], doh[h], TN) for h in heads]
        gq_sum = jnp.zeros((1, XA_HD), F32)
        gk_sum = jnp.zeros((1, XA_HD), F32)
        for h in heads:
            dqv, gq = _head_rms_bwd(dqn[h], qv[h], qn[h][1], qg_ref[...])
            dkv, gk = _head_rms_bwd(dkn[h], kv[h], kn[h][1], kg_ref[...])
            dq_ref[:, hs[h]] = dqv.astype(dq_ref.dtype)
            dkv_ref[:, hs[h]] += dkv
            dkv_ref[:, vs[h]] += dvv[h]
            gq_sum = gq_sum + gq
            gk_sum = gk_sum + gk
        dqg_ref[...] += gq_sum
        dkg_ref[...] += gk_sum

    vec = pl.BlockSpec((1, XA_HD), lambda b, i: (0, 0))
    row = pl.BlockSpec((tq, w), lambda b, i: (b * nq + i, 0))
    mem = pl.BlockSpec((mlen, 2 * w), lambda b, i: (b, 0))
    return pl.pallas_call(
        body, name="xattn_bwd", grid=(bsz, nq),
        in_specs=[row, mem, vec, vec, row],
        out_specs=(row, mem, vec, vec),
        out_shape=(jax.ShapeDtypeStruct((t, w), _MXU_DTYPE), jax.ShapeDtypeStruct((bsz * mlen, 2 * w), F32),
                   jax.ShapeDtypeStruct((1, XA_HD), F32), jax.ShapeDtypeStruct((1, XA_HD), F32)),
        compiler_params=_params(("arbitrary", "arbitrary")),
    )(qx, kvx, qg, kg, do)


def _loss_finish(sq_row, d_model):
    def body(s_ref, o_ref):
        o_ref[...] = jnp.zeros_like(o_ref) + 0.5 * jnp.sum(s_ref[...]) / float(d_model)

    return pl.pallas_call(body, name="loss_finish", out_shape=jax.ShapeDtypeStruct((1, 128), F32))(sq_row)


def _adamw_math(w, g, m, v):
    m = ADAM_B1 * m + (1.0 - ADAM_B1) * g
    v = ADAM_B2 * v + (1.0 - ADAM_B2) * (g * g)
    m_hat = m / (1.0 - ADAM_B1 ** ADAM_STEP)
    v_hat = v / (1.0 - ADAM_B2 ** ADAM_STEP)
    return -ADAM_LR * (m_hat / (jnp.sqrt(v_hat) + ADAM_EPS) + ADAM_WD * w), m, v


def _adamw_big(w, g, m, v, *, name, tr=512):
    r, c = w.shape
    tr = min(tr, r)

    def body(w_ref, g_ref, m_ref, v_ref, go_ref, d_ref, mo_ref, vo_ref):
        gv = g_ref[...]
        d, mn, vn = _adamw_math(w_ref[...], gv, m_ref[...], v_ref[...])
        go_ref[...] = gv
        d_ref[...] = d
        mo_ref[...] = mn
        vo_ref[...] = vn

    spec = pl.BlockSpec((tr, c), lambda i: (i, 0))
    shp = jax.ShapeDtypeStruct((r, c), F32)
    return pl.pallas_call(
        body, name=name, grid=(r // tr,), in_specs=[spec] * 4, out_specs=(spec,) * 4, out_shape=(shp,) * 4,
        compiler_params=_params(("parallel",)),
    )(w, g, m, v)


def _adamw_small(ws, gs, ms, vs):
    n = len(ws)

    def body(*refs):
        for i in range(n):
            d, mn, vn = _adamw_math(refs[i][...], refs[n + i][...], refs[2 * n + i][...], refs[3 * n + i][...])
            refs[4 * n + i][...] = d
            refs[5 * n + i][...] = mn
            refs[6 * n + i][...] = vn

    shapes = tuple(jax.ShapeDtypeStruct(w.shape, F32) for w in ws)
    return pl.pallas_call(body, name="adamw_small", out_shape=shapes * 3)(*ws, *gs, *ms, *vs)


def _add_halves(g, recv, c_idx, *, name, tr=512):
    _, r, c = g.shape
    h = r // 2
    tr = min(tr, h)
    nt = h // tr

    def body(c_ref, g_ref, r_ref, o_ref):
        del c_ref
        o_ref[...] = g_ref[...] + r_ref[...]

    return pl.pallas_call(
        body, name=name,
        grid_spec=pltpu.PrefetchScalarGridSpec(
            num_scalar_prefetch=1, grid=(4, nt),
            in_specs=[pl.BlockSpec((None, tr, c), lambda k, i, cr: (k, cr[0] * nt + i, 0)),
                      pl.BlockSpec((None, tr, c), lambda k, i, cr: (k, i, 0))],
            out_specs=pl.BlockSpec((None, tr, c), lambda k, i, cr: (k, i, 0))),
        out_shape=jax.ShapeDtypeStruct((4, h, c), F32),
        compiler_params=_params(("parallel", "parallel")),
    )(c_idx, g, recv)


def _add_chips(p, recv, place_idx, *, name, tr=512, after=()):
    _, h, c = p.shape
    tr = min(tr, h)
    nt = h // tr

    def body(pi_ref, p_ref, r_ref, *rest):
        del pi_ref
        rest[-1][...] = ((p_ref[...] + r_ref[0]) + r_ref[1]) + r_ref[2]

    return pl.pallas_call(
        body, name=name,
        grid_spec=pltpu.PrefetchScalarGridSpec(
            num_scalar_prefetch=1, grid=(nt,),
            in_specs=[pl.BlockSpec((None, tr, c), lambda i, pi: (pi[0], i, 0)),
                      pl.BlockSpec((3, tr, c), lambda i, pi: (0, i, 0))] + [pl.BlockSpec(memory_space=pl.ANY)] * len(after),
            out_specs=pl.BlockSpec((tr, c), lambda i, pi: (pi[1] * nt + i, 0))),
        out_shape=jax.ShapeDtypeStruct((2 * h, c), F32),
        compiler_params=_params(("parallel",)),
    )(place_idx, p, recv, *after)


def _place_shard(shard, place_idx, *, name, tr=512, after=()):
    r, c = shard.shape
    tr = min(tr, r)

    def body(pi_ref, s_ref, *rest):
        del pi_ref
        rest[-1][...] = s_ref[...]

    return pl.pallas_call(
        body, name=name,
        grid_spec=pltpu.PrefetchScalarGridSpec(
            num_scalar_prefetch=1, grid=(r // tr,),
            in_specs=[pl.BlockSpec((tr, c), lambda i, pi: (i, 0))] + [pl.BlockSpec(memory_space=pl.ANY)] * len(after),
            out_specs=pl.BlockSpec((None, tr, c), lambda i, pi: (pi[0], i, 0))),
        out_shape=jax.ShapeDtypeStruct((4, r, c), shard.dtype),
        compiler_params=_params(("parallel",)),
    )(place_idx, shard, *after)


def _place():
    x, y, c = lax.axis_index("x"), lax.axis_index("y"), lax.axis_index("c")
    chips = [(1 - x, y), (x, 1 - y), (1 - x, 1 - y)]
    return x, y, c, chips


ANY = pl.BlockSpec(memory_space=pl.ANY)


def _exchange_halves(grads, name):
    n = len(grads)

    def body(*refs):
        ins, outs = refs[:n], refs[n:2 * n]
        send_sems, recv_sems = refs[2 * n:]
        x, y, c, _ = _place()

        def copy(a):
            h = ins[a].shape[1] // 2
            return pltpu.make_async_remote_copy(
                src_ref=ins[a].at[:, pl.ds((1 - c) * h, h), :], dst_ref=outs[a],
                send_sem=send_sems.at[a], recv_sem=recv_sems.at[a], device_id=(x, y, 1 - c), device_id_type=MESH)

        for a in range(n):
            copy(a).start()
        for a in range(n):
            copy(a).wait_recv()
        for a in range(n):
            copy(a).wait_send()

    return pl.pallas_call(
        body, name=name,
        in_specs=[ANY] * n, out_specs=tuple([ANY] * n),
        out_shape=tuple(jax.ShapeDtypeStruct((4, g.shape[1] // 2, g.shape[2]), g.dtype) for g in grads),
        scratch_shapes=[pltpu.SemaphoreType.DMA((n,)), pltpu.SemaphoreType.DMA((n,))],
    )(*grads)


HBM = pl.BlockSpec(memory_space=pltpu.HBM)
SEM = pl.BlockSpec(memory_space=pltpu.SEMAPHORE)
EFFECT = pltpu.SideEffectType.DATAFLOW_SIDE_EFFECTING


def _in_hbm(a):
    return pltpu.with_memory_space_constraint(a, pltpu.HBM)


def _split_copy_calls(name, srcs, lands, n_copies, make_copies):
    ns, nl = len(srcs), len(lands)
    nb = ns + nl

    def start(after=()):
        n_after = len(after)

        def body(*refs):
            outs = refs[nb + n_after:]
            copies = make_copies(refs[:ns], refs[ns:nb], outs[0], outs[1])
            for cp in copies:
                cp.start()
            token = refs[-1]
            token[...] = jnp.zeros_like(token)

        bufs = [_in_hbm(a) for a in list(srcs) + list(lands)]
        out = pl.pallas_call(
            body, name=name + "_start",
            out_shape=(pltpu.SemaphoreType.DMA((n_copies,)), pltpu.SemaphoreType.DMA((n_copies,)),
                       *[pltpu.HBM(a.shape, a.dtype) for a in bufs], jax.ShapeDtypeStruct((8, 128), F32)),
            in_specs=[HBM] * nb + [pl.BlockSpec(memory_space=pl.ANY)] * n_after,
            out_specs=(SEM, SEM, *[HBM] * nb, pl.BlockSpec(memory_space=pltpu.VMEM)),
            input_output_aliases={i: 2 + i for i in range(nb)},
            compiler_params=pltpu.CompilerParams(has_side_effects=EFFECT),
        )(*bufs, *after)
        return dict(send=out[0], recv=out[1], bufs=list(out[2:2 + nb]), token=out[-1])

    def wait(state, after):
        def body(*refs):
            copies = make_copies(refs[:ns], refs[ns:nb], refs[nb], refs[nb + 1])
            for cp in copies:
                cp.wait_send()
            for cp in copies:
                cp.wait_recv()

        bufs = state["bufs"]
        out = pl.pallas_call(
            body, name=name + "_wait",
            out_shape=tuple(pltpu.HBM(a.shape, a.dtype) for a in bufs),
            in_specs=[HBM] * nb + [SEM, SEM] + [pl.BlockSpec(memory_space=pl.ANY)] * len(after),
            out_specs=tuple([HBM] * nb),
            input_output_aliases={i: i for i in range(nb)},
            compiler_params=pltpu.CompilerParams(has_side_effects=EFFECT),
        )(*bufs, state["send"], state["recv"], *after)
        return list(out[:ns]), list(out[ns:])

    return start, wait


def _scatter_chips_split(name, parts):
    n = len(parts)
    lands = [lax.empty((3,) + p.shape[1:], p.dtype) for p in parts]

    def make_copies(srcs, lnds, send_sems, recv_sems):
        _, _, c, chips = _place()
        return [pltpu.make_async_remote_copy(
            src_ref=srcs[a].at[2 * px + py], dst_ref=lnds[a].at[j], send_sem=send_sems.at[a * 3 + j],
            recv_sem=recv_sems.at[a * 3 + j], device_id=(px, py, c), device_id_type=MESH)
            for a in range(n) for j, (px, py) in enumerate(chips)]

    return _split_copy_calls(name, parts, lands, 3 * n, make_copies)


def _exchange_halves_split(name, grads):
    n = len(grads)
    lands = [lax.empty((4, g.shape[1] // 2, g.shape[2]), g.dtype) for g in grads]

    def make_copies(srcs, lnds, send_sems, recv_sems):
        x, y, c, _ = _place()
        out = []
        for a in range(n):
            h = srcs[a].shape[1] // 2
            out.append(pltpu.make_async_remote_copy(
                src_ref=srcs[a].at[:, pl.ds((1 - c) * h, h), :], dst_ref=lnds[a], send_sem=send_sems.at[a],
                recv_sem=recv_sems.at[a], device_id=(x, y, 1 - c), device_id_type=MESH))
        return out

    return _split_copy_calls(name, grads, lands, n, make_copies)


def _gather_chips_split(name, shards, lands):
    n = len(shards)

    def make_copies(srcs, lnds, send_sems, recv_sems):
        x, y, c, chips = _place()
        out = []
        for a in range(n):
            h = srcs[a].shape[0] // 2
            for j, (px, py) in enumerate(chips):
                out.append(pltpu.make_async_remote_copy(
                    src_ref=srcs[a].at[pl.ds(c * h, h), :], dst_ref=lnds[a].at[2 * x + y, pl.ds(c * h, h), :],
                    send_sem=send_sems.at[a * 3 + j], recv_sem=recv_sems.at[a * 3 + j],
                    device_id=(px, py, c), device_id_type=MESH))
        return out

    return _split_copy_calls(name, shards, lands, 3 * n, make_copies)


def _gather_finish(gathered, name):
    n = len(gathered)

    def body(*refs):
        outs = refs[n:2 * n]
        send_sems, recv_sems = refs[2 * n:]
        x, y, c, chips = _place()

        def copy(a, j, chip_idx, which):
            h = outs[a].shape[1] // 2
            rows = outs[a].at[chip_idx, pl.ds(which * h, h), :]
            return pltpu.make_async_remote_copy(
                src_ref=rows, dst_ref=rows, send_sem=send_sems.at[a * 3 + j], recv_sem=recv_sems.at[a * 3 + j],
                device_id=(x, y, 1 - c), device_id_type=MESH)

        for a in range(n):
            for j, (px, py) in enumerate(chips):
                copy(a, j, 2 * px + py, c).start()
        for a in range(n):
            for j, (px, py) in enumerate(chips):
                copy(a, j, 2 * px + py, 1 - c).wait_recv()
        for a in range(n):
            for j, (px, py) in enumerate(chips):
                copy(a, j, 2 * px + py, c).wait_send()

    return pl.pallas_call(
        body, name=name,
        in_specs=[ANY] * n, out_specs=tuple([ANY] * n),
        out_shape=tuple(jax.ShapeDtypeStruct(g.shape, g.dtype) for g in gathered),
        input_output_aliases={i: i for i in range(n)},
        scratch_shapes=[pltpu.SemaphoreType.DMA((3 * n,)), pltpu.SemaphoreType.DMA((3 * n,))],
    )(*gathered)


def _gather_forward_split(name, gathered):
    n = len(gathered)

    def make_copies(srcs, lnds, send_sems, recv_sems):
        x, y, c, chips = _place()
        out = []
        for a in range(n):
            h = lnds[a].shape[1] // 2
            for j, (px, py) in enumerate(chips):
                rows = lnds[a].at[2 * px + py, pl.ds(c * h, h), :]
                out.append(pltpu.make_async_remote_copy(
                    src_ref=rows, dst_ref=rows, send_sem=send_sems.at[a * 3 + j], recv_sem=recv_sems.at[a * 3 + j],
                    device_id=(x, y, 1 - c), device_id_type=MESH))
        return out

    return _split_copy_calls(name, [], gathered, 3 * n, make_copies)


def _join_halves_split(name, fulls):
    n = len(fulls)

    def make_copies(srcs, lnds, send_sems, recv_sems):
        x, y, c, _ = _place()
        out = []
        for a in range(n):
            h = lnds[a].shape[0] // 2
            rows = lnds[a].at[pl.ds(c * h, h), :]
            out.append(pltpu.make_async_remote_copy(
                src_ref=rows, dst_ref=rows, send_sem=send_sems.at[a], recv_sem=recv_sems.at[a],
                device_id=(x, y, 1 - c), device_id_type=MESH))
        return out

    return _split_copy_calls(name, [], fulls, n, make_copies)


def _all_gather_small_split(sm):
    r, w = sm.shape

    def make_copies(srcs, lnds, send_sems, recv_sems):
        x, y, c, _ = _place()
        me = 4 * x + 2 * y + c
        rel = [(dx, dy, dc) for dx in (0, 1) for dy in (0, 1) for dc in (0, 1)][1:]
        return [pltpu.make_async_remote_copy(
            src_ref=srcs[0], dst_ref=lnds[0].at[me], send_sem=send_sems.at[k], recv_sem=recv_sems.at[k],
            device_id=(1 - x if dx else x, 1 - y if dy else y, 1 - c if dc else c), device_id_type=MESH)
            for k, (dx, dy, dc) in enumerate(rel)]

    return _split_copy_calls("all_gather_small", [sm], [lax.empty((8, r, w), sm.dtype)], 7, make_copies)


def _sum_devices(sm, gathered, me_idx):
    def body(me_ref, sm_ref, g_ref, o_ref):
        own = sm_ref[...]
        acc = jnp.where(me_ref[0] == 0, own, g_ref[0])
        for d in range(1, 8):
            acc = acc + jnp.where(me_ref[0] == d, own, g_ref[d])
        o_ref[...] = acc

    vm = pl.BlockSpec(memory_space=pltpu.VMEM)
    return pl.pallas_call(
        body, name="sum_devices", in_specs=[pl.BlockSpec(memory_space=pltpu.SMEM), vm, vm], out_specs=vm,
        out_shape=jax.ShapeDtypeStruct(sm.shape, F32),
    )(me_idx, sm, gathered)


class _LocalWeights:
    def __init__(self, w):
        self.w = w
        self.g = {}

    def begin(self):
        return ()

    def first(self, after):
        del after
        return self.w

    def rest(self, after):
        del after
        return self.w

    def mlp(self, after):
        del after
        return self.w

    def grads(self, tag, g):
        del tag
        self.g.update(g)
        return ()

    def poll(self, after):
        del after
        return ()


def _local_step(x3, mem3, pos2, target3, small, comm):
    bsz, seq, d = x3.shape
    mlen = mem3.shape[1]
    t = bsz * seq
    tok = comm.begin()
    x = x3.reshape(t, d)
    mem = mem3.reshape(bsz * mlen, d)
    target = target3.reshape(t, d)
    pos = pos2.reshape(t, 1)
    qg_t = jnp.tile(small["sw_q_norm_g"], (1, SW_HEADS))
    kg_t = jnp.tile(small["sw_k_norm_g"], (1, SW_KV_HEADS))

    hn1 = _rms_fwd(x, small["norm1_g"], name="rms1_fwd", after=tok)
    w = comm.first(hn1)
    proj_hg = _mm(hn1, w["w_in_hg"], NN, t, HG_COLS, d, name="proj_hg", tk=d, after=(w.get("token"),))[0]
    proj_sw = _mm(hn1, w["w_in_sw"], NN, t, SW_COLS, d, name="proj_sw", tk=d)[0]
    y_mix, o_hg, states = _hg_fwd(proj_hg, small["hg_lower_bounds"], small["hg_norm_g"], bsz, seq, y_width=1024)
    y_mix = _sw_fwd(proj_sw, pos, qg_t, kg_t, small["sw_sinks"], y_mix, bsz, seq)
    w_in_hg, w_in_sw = w["w_in_hg"], w["w_in_sw"]
    w = comm.rest(y_mix)
    h1, hn2 = _mm(y_mix, w["w_out"], NN, t, d, 1024, name="out_proj", tk=1024, extras=(x,), rows=(small["norm2_g"],),
                  epilogue=_residual_rms, out_dtypes=(F32, _MXU_DTYPE), after=(w.get("token"),))
    mn = _rms_fwd(mem, small["mem_norm_g"], name="rms_mem_fwd")
    qx = _mm(hn2, w["wq"], NN, t, 512, d, name="xa_q", tk=d)[0]
    kvx = _mm(mn, w["wkv"], NN, bsz * mlen, 1024, d, name="xa_kv", tk=d)[0]
    ox = _xa_fwd(qx, kvx, small["xa_q_norm_g"], small["xa_k_norm_g"], bsz, seq, mlen)
    h2, hn3 = _mm(ox, w["wo"], NN, t, d, 512, name="xa_o", tk=512, extras=(h1,), rows=(small["norm3_g"],),
                  epilogue=_residual_rms, out_dtypes=(F32, _MXU_DTYPE))
    w = {**w, **comm.mlp(hn3)}
    ff = w["down"].shape[0]
    ffs = ff // 4

    def relu_sq(acc):
        a = jnp.maximum(acc, 0.0)
        return a, a * a

    act, act2 = _mm(hn3, w["up"], NN, t, ff, d, name="mlp_up", tm=2048, tn=ffs, tk=d,
                    b_spec=pl.BlockSpec((None, d, ffs), lambda i, j, kk: (j, 0, 0)),
                    epilogue=relu_sq, out_dtypes=(_MXU_DTYPE, _MXU_DTYPE))
    inv_d = 1.0 / d

    def loss_cotangent(acc, res, tgt):
        diff = acc + res - tgt
        v = diff * inv_d
        return v, v, jnp.sum(diff * diff, axis=0, keepdims=True)

    dy, dy_mx, sq_row = _mm(act2, w["down"], NN, t, d, ff, name="mlp_down", extras=(h2, target),
                            epilogue=loss_cotangent, out_dtypes=(F32, _MXU_DTYPE), row_sums=1)
    loss_row = _loss_finish(sq_row, d)

    dz = _mm(dy_mx, w["down"], NT, t, ff, d, name="d_act", tm=2048, tk=d, extras=(act,),
             epilogue=lambda acc, a: (acc * (2.0 * a.astype(F32)),), out_dtypes=(_MXU_DTYPE,))[0]
    g_down = _mm(act2, dy_mx, TN, ff, d, t, name="g_down")[0]
    g_up = _mm(hn3, dz, TN, d, ff, t, name="g_up", tn=ffs,
               out_shape=(jax.ShapeDtypeStruct((4, d, ffs), F32),),
               out_spec=(pl.BlockSpec((None, min(1024, d), ffs), lambda i, j, kk: (j, i, 0)),))[0]
    tok = comm.grads("mlp", dict(up=g_up, down=g_down))
    dh2, dh2_mx, g_norm3 = _mm(dz, w["up"], NT, t, d, ff, name="d_hn3", tk=ffs, after=tok,
                               b_spec=pl.BlockSpec((None, min(1024, d), ffs), lambda i, j, kk: (kk, j, 0)),
                               extras=(h2, dy), rows=(small["norm3_g"],), epilogue=_rms_bwd_residual,
                               out_dtypes=(F32, _MXU_DTYPE), row_sums=1)
    d_ox = _mm(dh2_mx, w["wo"], NT, t, 512, d, name="d_ox", tk=d)[0]
    g_wo = _mm(ox, dh2_mx, TN, 512, d, t, name="g_wo")[0]
    d_qx, d_kvx, g_xq, g_xk = _xa_bwd(qx, kvx, small["xa_q_norm_g"], small["xa_k_norm_g"], d_ox, bsz, seq, mlen)
    g_wq = _mm(hn2, d_qx, TN, d, 512, t, name="g_wq")[0]
    g_wkv = _mm(mn, d_kvx, TN, d, 1024, bsz * mlen, name="g_wkv")[0]
    dh1, dh1_mx, g_norm2 = _mm(d_qx, w["wq"], NT, t, d, 512, name="d_hn2", tk=512, extras=(h1, dh2),
                               rows=(small["norm2_g"],), epilogue=_rms_bwd_residual, out_dtypes=(F32, _MXU_DTYPE),
                               row_sums=1)
    dmn = _mm(d_kvx, w["wkv"], NT, bsz * mlen, d, 1024, name="d_mn", tk=1024)[0]
    g_memn = _rms_gain_grad(mem, small["mem_norm_g"], dmn, name="rms_mem_bwd")
    g_wout = _mm(y_mix, dh1_mx, TN, 1024, d, t, name="g_wout")[0]
    tok = comm.grads("mid", dict(w_out=g_wout, wq=g_wq, wkv=g_wkv, wo=g_wo))
    d_mix = _mm(dh1_mx, w["w_out"], NT, t, 1024, d, name="d_mix", tk=d, after=tok)[0]
    dproj_sw, g_swq, g_swk, g_sinks = _sw_bwd(proj_sw, pos, qg_t, kg_t, small["sw_sinks"], y_mix, d_mix, bsz, seq)
    tok = comm.poll(dproj_sw)
    dproj_hg, g_lb, g_hgn = _hg_bwd(proj_hg, small["hg_lower_bounds"], small["hg_norm_g"], o_hg, states, d_mix, bsz, seq,
                                    after=tok)
    g_in_hg = _mm(hn1, dproj_hg, TN, d, HG_COLS, t, name="g_in_hg")[0]
    g_in_sw = _mm(hn1, dproj_sw, TN, d, SW_COLS, t, name="g_in_sw")[0]
    tok = comm.grads("in", dict(w_in_hg=g_in_hg, w_in_sw=g_in_sw))
    dhn1_a = _mm(dproj_hg, w_in_hg, NT, t, d, HG_COLS, name="d_hn1_hg", tk=1024, after=tok)[0]
    grad_x, g_norm1 = _mm(dproj_sw, w_in_sw, NT, t, d, SW_COLS, name="d_hn1_sw", tk=SW_COLS, extras=(dhn1_a, x, dh1),
                          rows=(small["norm1_g"],), row_sums=1,
                          epilogue=lambda acc, prev, xv, dres, g: _rms_bwd_residual(acc + prev, xv, dres, g)[1:])

    g_small = dict(norm1_g=g_norm1, hg_lower_bounds=g_lb, hg_norm_g=g_hgn, sw_q_norm_g=g_swq, sw_k_norm_g=g_swk,
                   sw_sinks=g_sinks[:, 0:SW_HEADS], norm2_g=g_norm2, mem_norm_g=g_memn, xa_q_norm_g=g_xq,
                   xa_k_norm_g=g_xk, norm3_g=g_norm3)
    return loss_row, grad_x.reshape(bsz, seq, d), g_small


SMALL_NAMES = ("norm1_g", "hg_lower_bounds", "hg_norm_g", "sw_q_norm_g", "sw_k_norm_g", "sw_sinks", "norm2_g",
               "mem_norm_g", "xa_q_norm_g", "xa_k_norm_g", "norm3_g")
BIG_NAMES = ("w_in", "w_out", "xa_wq", "xa_wkv", "xa_wo", "mlp_up", "mlp_down")
WEIGHT_ORDER = ("norm1_g", "w_in", "hg_lower_bounds", "hg_norm_g", "sw_q_norm_g", "sw_k_norm_g", "sw_sinks", "w_out",
                "norm2_g", "mem_norm_g", "xa_wq", "xa_wkv", "xa_q_norm_g", "xa_k_norm_g", "xa_wo", "norm3_g",
                "mlp_up", "mlp_down")


def _pack_rows(vals, width):
    starts, at = [], 0
    for v in vals:
        starts.append(at)
        at += v.shape[0]
    total = at + (-at) % 8
    out = None
    for v, s in zip(vals, starts):
        placed = jnp.pad(v, ((s, total - s - v.shape[0]), (0, width - v.shape[1])))
        out = placed if out is None else out + placed
    return out, starts


class _MeshWeights:
    LATE = ("w_out", "xa_wq", "xa_wkv", "xa_wo", "mlp_up", "mlp_down")

    def __init__(self, shards, d, ff):
        self.shards, self.d, self.ff = shards, d, ff
        self.c_idx = lax.axis_index("c").astype(jnp.int32).reshape(1)
        chip = (2 * lax.axis_index("x") + lax.axis_index("y")).astype(jnp.int32)
        self.place_idx = jnp.stack([chip, lax.axis_index("c").astype(jnp.int32)])
        self.pending = []
        self.exchanging = None

    def begin(self):
        shard = self.shards["w_in"]
        start, self.in_wait = _gather_chips_split(
            "gather_in", [shard], [_place_shard(shard, self.place_idx, name="place_w_in")])
        self.in_state = start()
        tok = (self.in_state["token"],)
        self.placed = [_place_shard(self.shards[n], self.place_idx, name="place_" + n, after=tok) for n in self.LATE]
        return tok

    def first(self, after):
        _, lands = self.in_wait(self.in_state, (after, *self.placed))
        (g_in,) = _gather_finish(lands, "gather_in_finish")
        start, self.late_wait = _gather_chips_split("gather_late", [self.shards[n] for n in self.LATE], self.placed)
        self.late_state = start(after=(g_in,))
        ws = g_in.shape[2]
        cut = HG_COLS - 2 * ws
        return dict(w_in_hg=jnp.concatenate([g_in[0], g_in[1], g_in[2][:, :cut]], axis=1),
                    w_in_sw=jnp.concatenate([g_in[2][:, cut:], g_in[3]], axis=1), token=self.late_state["token"])

    def rest(self, after):
        _, lands = self.late_wait(self.late_state, (after,))
        g_out, g_q, g_kv, g_o = _gather_finish(lands[:4], "gather_late_finish")
        start, self.mlp_wait = _gather_forward_split("gather_mlp_forward", lands[4:])
        self.mlp_state = start(after=(g_out,))
        d = self.d
        return dict(w_out=g_out.reshape(-1, d), wq=g_q.reshape(d, -1), wkv=g_kv.reshape(d, -1),
                    wo=jnp.concatenate([g_o[k] for k in range(4)], axis=1), token=self.mlp_state["token"])

    def mlp(self, after):
        _, (g_up, g_dn) = self.mlp_wait(self.mlp_state, (after,))
        return dict(up=g_up, down=g_dn.reshape(self.ff, self.d))

    def _scatter(self, tag, names, arrays, recv):
        parts = [_add_halves(g, r, self.c_idx, name="rs_add_halves_" + n) for n, g, r in zip(names, arrays, recv)]
        start, wait = _scatter_chips_split("rs_scatter_" + tag, parts)
        state = start()
        self.pending.append((names, wait, state))
        return state["token"]

    def _advance(self, after):
        if self.exchanging is None:
            return ()
        tag, names, wait, state = self.exchanging
        self.exchanging = None
        arrays, recv = wait(state, (after,))
        return (self._scatter(tag, names, arrays, recv),)

    def poll(self, after):
        return self._advance(after)

    def grads(self, tag, g):
        d, ff = self.d, self.ff
        if tag == "mlp":
            names, arrays = ("mlp_up", "mlp_down"), [g["up"], g["down"].reshape(4, ff // 4, d)]
        elif tag == "mid":
            names = ("w_out", "xa_wq", "xa_wkv", "xa_wo")
            ds = d // 4
            g_wo = jnp.stack([g["wo"][:, ds * k:ds * (k + 1)] for k in range(4)])
            arrays = [g["w_out"].reshape(4, -1, d), g["wq"].reshape(4, d // 4, -1), g["wkv"].reshape(4, d // 4, -1), g_wo]
        else:
            hg, sw = g["w_in_hg"], g["w_in_sw"]
            ws = (hg.shape[1] + sw.shape[1]) // 4
            cut = hg.shape[1] - 2 * ws
            names = ("w_in",)
            arrays = [jnp.stack([hg[:, :ws], hg[:, ws:2 * ws], jnp.concatenate([hg[:, 2 * ws:], sw[:, :ws - cut]], axis=1),
                                 sw[:, ws - cut:]])]
        toks = self._advance(arrays[0])
        if tag == "in":
            return toks + (self._scatter(tag, names, arrays, _exchange_halves(arrays, "rs_exchange_" + tag)),)
        start, wait = _exchange_halves_split("rs_exchange_" + tag, arrays)
        state = start()
        self.exchanging = (tag, names, wait, state)
        return toks + (state["token"],)

    def finish(self, after):
        joins, tok = [], ()
        for names, wait, state in self.pending:
            srcs, lands = wait(state, after)
            fulls = [_add_chips(p, r, self.place_idx, name="rs_add_chips_" + n, after=tok)
                     for n, p, r in zip(names, srcs, lands)]
            start, jwait = _join_halves_split("rs_join_" + names[0], fulls)
            jstate = start()
            tok = (jstate["token"],)
            joins.append((names, jwait, jstate))
        out = {}
        for names, jwait, jstate in joins:
            _, fulls = jwait(jstate, tok)
            out.update(zip(names, fulls))
        return out


def kernel(x, mem, positions, norm1_g, w_in, hg_lower_bounds, hg_norm_g, sw_q_norm_g, sw_k_norm_g, sw_sinks, w_out, norm2_g, mem_norm_g, xa_wq, xa_wkv, xa_q_norm_g, xa_k_norm_g, xa_wo, norm3_g, mlp_up, mlp_down, loss_target, m_norm1_g, m_w_in, m_hg_lower_bounds, m_hg_norm_g, m_sw_q_norm_g, m_sw_k_norm_g, m_sw_sinks, m_w_out, m_norm2_g, m_mem_norm_g, m_xa_wq, m_xa_wkv, m_xa_q_norm_g, m_xa_k_norm_g, m_xa_wo, m_norm3_g, m_mlp_up, m_mlp_down, v_norm1_g, v_w_in, v_hg_lower_bounds, v_hg_norm_g, v_sw_q_norm_g, v_sw_k_norm_g, v_sw_sinks, v_w_out, v_norm2_g, v_mem_norm_g, v_xa_wq, v_xa_wkv, v_xa_q_norm_g, v_xa_k_norm_g, v_xa_wo, v_norm3_g, v_mlp_up, v_mlp_down):
    given = dict(locals())
    weights = {n: given[n] for n in WEIGHT_ORDER}
    moms = {n: given["m_" + n] for n in WEIGHT_ORDER}
    vars_ = {n: given["v_" + n] for n in WEIGHT_ORDER}
    d = x.shape[-1]
    ff = mlp_down.shape[1] * 4
    small = {n: weights[n] for n in SMALL_NAMES}

    comm = _MeshWeights({n: weights[n][0].astype(_MXU_DTYPE) for n in BIG_NAMES}, d, ff)
    loss_row, grad_x, g_small = _local_step(x, mem, positions, loss_target, small, comm)
    packed, starts = _pack_rows([g_small[n] for n in SMALL_NAMES] + [loss_row], 1024)
    start, wait = _all_gather_small_split(packed)
    state = start()
    big_grads = comm.finish((grad_x, state["token"]))
    (own,), (gathered,) = wait(state, (big_grads[BIG_NAMES[0]],))
    device = (4 * lax.axis_index("x") + 2 * lax.axis_index("y") + lax.axis_index("c")).astype(jnp.int32).reshape(1)
    summed = _sum_devices(own, gathered, device)
    small_grads = {}
    for n, s in zip(SMALL_NAMES, starts):
        r, c = weights[n].shape
        small_grads[n] = summed[s:s + r, 0:c]
    loss = summed[starts[-1], 0]

    grads, deltas, new_m, new_v = {}, {}, {}, {}
    for n in BIG_NAMES:
        shp = weights[n].shape
        g2, dl, mo, vo = _adamw_big(weights[n][0], big_grads[n], moms[n][0], vars_[n][0], name="adamw_" + n)
        grads[n], deltas[n], new_m[n], new_v[n] = (a.reshape(shp) for a in (g2, dl, mo, vo))
    sm_out = _adamw_small([weights[n] for n in SMALL_NAMES], [small_grads[n] for n in SMALL_NAMES],
                          [moms[n] for n in SMALL_NAMES], [vars_[n] for n in SMALL_NAMES])
    ns = len(SMALL_NAMES)
    for i, n in enumerate(SMALL_NAMES):
        grads[n], deltas[n], new_m[n], new_v[n] = small_grads[n], sm_out[i], sm_out[ns + i], sm_out[2 * ns + i]

    return (loss, grad_x, *[grads[n] for n in WEIGHT_ORDER], *[deltas[n] for n in WEIGHT_ORDER],
            *[new_m[n] for n in WEIGHT_ORDER], *[new_v[n] for n in WEIGHT_ORDER])
```

```python
import numpy as np
import jax
import jax.numpy as jnp
from jax import lax
from jax.experimental import pallas as pl
from jax.experimental.pallas import tpu as pltpu

F32 = jnp.float32
_MXU_DTYPE = jnp.bfloat16

EPS = 1e-6
HG_HEADS = 4
HG_D = 128
HG_CHUNK = 64
HG_TILE = 512
HG_LEVELS = (32, 16, 8, 4, 2, 1)
SW_HEADS = 8
SW_KV_HEADS = 2
SW_GROUP = SW_HEADS // SW_KV_HEADS
SW_HD = 64
SW_BLOCK = 128
ROPE_THETA = 500000.0
ROT_DIM = SW_HD // 4
XA_HEADS = 4
XA_HD = 128
HG_COLS = 4 * HG_HEADS * HG_D
SW_COLS = (SW_HEADS + 2 * SW_KV_HEADS) * SW_HD

ADAM_LR = 0.001
ADAM_B1 = 0.9
ADAM_B2 = 0.999
ADAM_EPS = 1e-08
ADAM_WD = 0.01
ADAM_STEP = 10

VMEM_LIMIT = 56 * 1024 * 1024
MESH = pl.DeviceIdType.MESH

NN = ((1,), (0,))
NT = ((1,), (1,))
TN = ((0,), (0,))


def _mx(v):
    return v.astype(_MXU_DTYPE)


def _dot(a, b, dims=NN):
    return lax.dot_general(_mx(a), _mx(b), (dims, ((), ())), preferred_element_type=F32)


def _split_dot(a, v, dims, parts):
    acc = None
    rest = v
    for p in range(parts):
        piece = _mx(rest)
        term = lax.dot_general(a, piece, (dims, ((), ())), preferred_element_type=F32)
        acc = term if acc is None else acc + term
        if p + 1 < parts:
            rest = rest - piece.astype(F32)
    return acc


def _params(sem):
    return pltpu.CompilerParams(dimension_semantics=sem, vmem_limit_bytes=VMEM_LIMIT)


def _mm(a, b, mode, m, n, k, *, name, tm=1024, tn=1024, tk=1024, a_spec=None, b_spec=None, extras=(), rows=(),
        epilogue=None, out_dtypes=(F32,), row_sums=0, out_shape=None, out_spec=None, after=()):
    after = tuple(t for t in after if t is not None)
    tm, tn, tk = min(tm, m), min(tn, n), min(tk, k)
    assert m % tm == 0 and n % tn == 0 and k % tk == 0, (name, m, n, k, tm, tn, tk)
    gi, gj, gk = m // tm, n // tn, k // tk
    assert row_sums == 0 or gj == 1, name
    if a_spec is None:
        a_spec = (pl.BlockSpec((tk, tm), lambda i, j, kk: (kk, i)) if mode == TN
                  else pl.BlockSpec((tm, tk), lambda i, j, kk: (i, kk)))
    if b_spec is None:
        b_spec = (pl.BlockSpec((tn, tk), lambda i, j, kk: (j, kk)) if mode == NT
                  else pl.BlockSpec((tk, tn), lambda i, j, kk: (kk, j)))
    mn_spec = pl.BlockSpec((tm, tn), lambda i, j, kk: (i, j))
    if epilogue is None:
        epilogue = lambda acc: (acc,)
    row_spec = pl.BlockSpec((1, tn), lambda i, j, kk: (0, j))
    n_ex, n_out = len(extras) + len(rows), len(out_dtypes)
    if out_shape is None:
        out_shape = tuple(jax.ShapeDtypeStruct((m, n), d) for d in out_dtypes)
        out_spec = tuple(mn_spec for _ in out_dtypes)
    out_shape = tuple(out_shape) + tuple(jax.ShapeDtypeStruct((1, n), F32) for _ in range(row_sums))
    out_spec = tuple(out_spec) + tuple(row_spec for _ in range(row_sums))

    n_after = len(after)

    def body(*refs):
        a_ref, b_ref = refs[0], refs[1]
        ex = refs[2:2 + n_ex]
        outs = refs[2 + n_ex + n_after:2 + n_ex + n_after + n_out + row_sums]
        first_row_tile = pl.program_id(0) == 0

        def finish(acc):
            res = epilogue(acc, *[e[...] for e in ex])
            for o, r in zip(outs[:n_out], res[:n_out]):
                o[...] = r.astype(o.dtype)
            if row_sums:
                @pl.when(first_row_tile)
                def _():
                    for o in outs[n_out:]:
                        o[...] = jnp.zeros_like(o)

                for o, r in zip(outs[n_out:], res[n_out:]):
                    o[...] += r

        if gk == 1:
            finish(_dot(a_ref[...], b_ref[...], mode))
        else:
            acc_ref = refs[-1]
            kk = pl.program_id(2)

            @pl.when(kk == 0)
            def _():
                acc_ref[...] = jnp.zeros_like(acc_ref)

            acc_ref[...] += _dot(a_ref[...], b_ref[...], mode)

            @pl.when(kk == gk - 1)
            def _():
                finish(acc_ref[...])

    return pl.pallas_call(
        body, name=name, grid=(gi, gj, gk),
        in_specs=([a_spec, b_spec] + [mn_spec] * len(extras) + [row_spec] * len(rows)
                  + [pl.BlockSpec(memory_space=pl.ANY)] * n_after),
        out_specs=out_spec, out_shape=out_shape,
        scratch_shapes=[pltpu.VMEM((tm, tn), F32)] if gk > 1 else [],
        compiler_params=_params(("arbitrary" if row_sums else "parallel", "parallel", "arbitrary")),
    )(a, b, *extras, *rows, *after)


def _rms_rows(xv, g):
    return xv * lax.rsqrt(jnp.mean(xv * xv, axis=1, keepdims=True) + EPS) * g


def _rms_rows_bwd(xv, g, dyv):
    r = lax.rsqrt(jnp.mean(xv * xv, axis=1, keepdims=True) + EPS)
    u = dyv * g
    return (r * u - xv * (r * r * r) * jnp.mean(u * xv, axis=1, keepdims=True),
            jnp.sum(dyv * xv * r, axis=0, keepdims=True))


def _residual_rms(acc, res, g):
    h = acc + res
    return h, _rms_rows(h, g)


def _rms_bwd_residual(dhn, xv, dres, g):
    dx, dg = _rms_rows_bwd(xv, g, dhn)
    dx = dx + dres
    return dx, dx, dg


def _rms_fwd(x, g, *, name, tm=512, after=()):
    t, d = x.shape
    tm = min(tm, t)
    after = tuple(a for a in after if a is not None)

    def body(x_ref, g_ref, *rest):
        rest[-1][...] = _rms_rows(x_ref[...], g_ref[...]).astype(rest[-1].dtype)

    return pl.pallas_call(
        body, name=name, grid=(t // tm,),
        in_specs=[pl.BlockSpec((tm, d), lambda i: (i, 0)), pl.BlockSpec((1, d), lambda i: (0, 0))]
        + [pl.BlockSpec(memory_space=pl.ANY)] * len(after),
        out_specs=pl.BlockSpec((tm, d), lambda i: (i, 0)),
        out_shape=jax.ShapeDtypeStruct((t, d), _MXU_DTYPE),
        compiler_params=_params(("parallel",)),
    )(x, g, *after)


def _rms_gain_grad(x, g, dy, *, name, tm=512):
    t, d = x.shape
    tm = min(tm, t)

    def body(x_ref, g_ref, dy_ref, dg_ref):
        @pl.when(pl.program_id(0) == 0)
        def _():
            dg_ref[...] = jnp.zeros_like(dg_ref)

        dg_ref[...] += _rms_rows_bwd(x_ref[...], g_ref[...], dy_ref[...])[1]

    row = pl.BlockSpec((tm, d), lambda i: (i, 0))
    vec = pl.BlockSpec((1, d), lambda i: (0, 0))
    return pl.pallas_call(
        body, name=name, grid=(t // tm,), in_specs=[row, vec, row], out_specs=vec,
        out_shape=jax.ShapeDtypeStruct((1, d), F32), compiler_params=_params(("arbitrary",)),
    )(x, g, dy)


def _hg_constants():
    c = HG_CHUNK
    t = np.arange(c)
    sums = [t[None, :] <= t[:, None]]
    masks = []
    for m in HG_LEVELS:
        base = (t // (2 * m)) * (2 * m)
        mid = base + m - 1
        second = (t - base) >= m
        upper = (t[None, :] > mid[:, None]) & (t[None, :] <= t[:, None])
        lower = (t[None, :] > t[:, None]) & (t[None, :] <= mid[:, None])
        sums.append(np.where(second[:, None], upper, lower))
        masks.append(second[:, None] & (~second)[None, :] & (base[:, None] == base[None, :]))
    return (np.concatenate(sums, axis=0).astype(np.float32), np.stack(masks).astype(np.float32))


HG_HEAD_LANES = tuple(slice(HG_D * h, HG_D * (h + 1)) for h in range(HG_HEADS))


def _per_head(fn, slab):
    return jnp.concatenate([jnp.broadcast_to(fn(slab[:, hs]), (slab.shape[0], HG_D)) for hs in HG_HEAD_LANES], axis=1)


def _lane_sum(v):
    return jnp.sum(v, axis=1, keepdims=True)


def _lane_mean(v):
    return jnp.mean(v, axis=1, keepdims=True)


def _hg_gates(blk, lbp):
    w = HG_HEADS * HG_D
    q, x, v, gl = blk[:, 0:w], blk[:, w:2 * w], blk[:, 2 * w:3 * w], blk[:, 3 * w:4 * w]
    mx = jnp.max(lbp, axis=0, keepdims=True)
    e = jnp.exp(lbp - mx)
    lb = e[0:1, :] / jnp.sum(e, axis=0, keepdims=True)
    sig = jax.nn.sigmoid(x)
    f = lb + (1.0 - lb) * sig
    return q, v, gl, lb, sig, f, 1.0 - f, jnp.log(f)


def _hg_fwd(proj, lbp, ng, bsz, seq, *, y_width):
    t = proj.shape[0]
    nc = seq // HG_CHUNK
    a_np, m_np = _hg_constants()
    a_all = jnp.asarray(a_np, _MXU_DTYPE)
    masks = jnp.asarray(m_np, F32)
    nl = len(HG_LEVELS)

    ts = min(HG_TILE, seq)
    ns, nct = seq // ts, ts // HG_CHUNK
    hw = HG_HEADS * HG_D

    def body(p_ref, lb_ref, ng_ref, a_ref, m_ref, y_ref, o_ref, st_ref, carry):
        a_mat = a_ref[...]
        ngv = ng_ref[...]

        @pl.when(pl.program_id(0) == 0)
        def _():
            carry[...] = jnp.zeros_like(carry)

        ng4 = _tile_lanes(ngv, HG_HEADS)
        heads = range(HG_HEADS)
        exs = range(bsz)
        hl = HG_HEAD_LANES
        lbp_v = lb_ref[...]

        def chunk(c, _):
            rows = pl.ds(pl.multiple_of(c * HG_CHUNK, HG_CHUNK), HG_CHUNK)
            gates = [_hg_gates(p_ref[e, rows, :], lbp_v) for e in exs]
            q, v, gl = [g[0] for g in gates], [g[1] for g in gates], [g[2] for g in gates]
            k = [g[6] for g in gates]
            sts = [[carry[e, h] for h in heads] for e in exs]
            e_all = [_split_dot(a_mat, gates[e][7], NN, 3) for e in exs]
            b = [e_all[e][0:HG_CHUNK] for e in exs]
            qb = [q[e] * jnp.exp(b[e]) for e in exs]
            o = [[_dot(qb[e][:, hl[h]], sts[e][h], NT) for h in heads] for e in exs]
            p = [[jnp.zeros((HG_CHUNK, HG_CHUNK), F32) for _ in heads] for _ in exs]
            for li in range(nl):
                dec = [jnp.exp(e_all[e][HG_CHUNK * (li + 1):HG_CHUNK * (li + 2)]) for e in exs]
                qm, km, mk = [q[e] * dec[e] for e in exs], [k[e] * dec[e] for e in exs], m_ref[li]
                p = [[p[e][h] + mk * _dot(qm[e][:, hl[h]], km[e][:, hl[h]], NT) for h in heads] for e in exs]
            bl = [b[e][HG_CHUNK - 1:HG_CHUNK, :] for e in exs]
            kd = [k[e] * jnp.exp(bl[e] - b[e]) for e in exs]
            pv = [[_dot(p[e][h], v[e][:, hl[h]]) for h in heads] for e in exs]
            upd = [[_dot(v[e][:, hl[h]], kd[e][:, hl[h]], TN) for h in heads] for e in exs]
            for e in exs:
                o_all = (jnp.concatenate([o[e][h] + pv[e][h] for h in heads], axis=1)
                         + _per_head(_lane_sum, q[e] * k[e]) * v[e])
                r = lax.rsqrt(_per_head(_lane_mean, o_all * o_all) + EPS)
                ebl = jnp.exp(bl[e])
                for h in heads:
                    st_ref[e, h, c] = sts[e][h]
                    carry[e, h] = sts[e][h] * ebl[:, hl[h]] + upd[e][h]
                o_ref[e, rows, :] = o_all
                y_ref[e, rows, :] = (o_all * r * ng4) * (gl[e] * jax.nn.sigmoid(gl[e]))
            return 0

        lax.fori_loop(0, nct, chunk, 0)

    y3, o3, states = pl.pallas_call(
        body, name="hgrn2_fwd", grid=(ns,),
        in_specs=[pl.BlockSpec((bsz, ts, HG_COLS), lambda s: (0, s, 0)),
                  pl.BlockSpec((2, hw), lambda s: (0, 0)),
                  pl.BlockSpec((1, HG_D), lambda s: (0, 0)),
                  pl.BlockSpec(a_all.shape, lambda s: (0, 0)),
                  pl.BlockSpec(masks.shape, lambda s: (0, 0, 0))],
        out_specs=(pl.BlockSpec((bsz, ts, hw), lambda s: (0, s, 0)),
                   pl.BlockSpec((bsz, ts, hw), lambda s: (0, s, 0)),
                   pl.BlockSpec((bsz, HG_HEADS, nct, HG_D, HG_D), lambda s: (0, 0, s, 0, 0))),
        out_shape=(jax.ShapeDtypeStruct((bsz, seq, y_width), F32),
                   jax.ShapeDtypeStruct((bsz, seq, hw), F32),
                   jax.ShapeDtypeStruct((bsz, HG_HEADS, nc, HG_D, HG_D), F32)),
        scratch_shapes=[pltpu.VMEM((bsz, HG_HEADS, HG_D, HG_D), F32)],
        compiler_params=_params(("arbitrary",)),
    )(proj.reshape(bsz, seq, HG_COLS), lbp, ng, a_all, masks)
    return y3.reshape(t, y_width), o3.reshape(t, hw), states


def _hg_bwd(proj, lbp, ng, o_all, states, dy, bsz, seq, after=()):
    after = tuple(a for a in after if a is not None)
    t = proj.shape[0]
    nc = seq // HG_CHUNK
    a_np, m_np = _hg_constants()
    a_all = jnp.asarray(a_np, _MXU_DTYPE)
    masks = jnp.asarray(m_np, F32)
    nl = len(HG_LEVELS)
    cs = HG_CHUNK

    ts = min(HG_TILE, seq)
    ns, nct = seq // ts, ts // cs
    hw = HG_HEADS * HG_D

    def body(p_ref, lb_ref, ng_ref, a_ref, m_ref, o_ref, st_ref, dy_ref, *rest):
        dp_ref, dlb_ref, dng_ref, dst_ref = rest[len(after):]
        a_mat = a_ref[...]
        ngv = ng_ref[...]
        ng4 = _tile_lanes(ngv, HG_HEADS)
        last_row = lax.broadcasted_iota(jnp.int32, (cs, hw), 0) == cs - 1
        first = pl.program_id(0) == 0
        heads = range(HG_HEADS)
        exs = range(bsz)
        hl = HG_HEAD_LANES
        lbp_v = lb_ref[...]

        @pl.when(first)
        def _():
            dst_ref[...] = jnp.zeros_like(dst_ref)

        def side_by_side(parts):
            return jnp.concatenate(parts, axis=1)

        def chunk(i, carry):
            dlb_acc, dng_acc = carry
            c = nct - 1 - i
            rows = pl.ds(pl.multiple_of(c * cs, cs), cs)
            gates = [_hg_gates(p_ref[e, rows, :], lbp_v) for e in exs]
            q, v, gl = [g[0] for g in gates], [g[1] for g in gates], [g[2] for g in gates]
            lb, sig, f, k = gates[0][3], [g[4] for g in gates], [g[5] for g in gates], [g[6] for g in gates]
            o = [o_ref[e, rows, :] for e in exs]
            dyv = [dy_ref[e, rows, :] for e in exs]
            sts = [[st_ref[e, h, c] for h in heads] for e in exs]
            dsts = [[dst_ref[e, h] for h in heads] for e in exs]
            e_all = [_split_dot(a_mat, gates[e][7], NN, 3) for e in exs]
            b = [e_all[e][0:cs] for e in exs]
            eb = [jnp.exp(b[e]) for e in exs]
            bl = [b[e][cs - 1:cs, :] for e in exs]
            ebl = [jnp.exp(bl[e]) for e in exs]
            ekd = [jnp.exp(bl[e] - b[e]) for e in exs]
            qb = [q[e] * eb[e] for e in exs]
            kd = [k[e] * ekd[e] for e in exs]
            do, dgl = [], []
            for e in exs:
                sg = jax.nn.sigmoid(gl[e])
                silu = gl[e] * sg
                r = lax.rsqrt(_per_head(_lane_mean, o[e] * o[e]) + EPS)
                dgl.append(dyv[e] * (o[e] * r * ng4) * (sg * (1.0 + gl[e] * (1.0 - sg))))
                u = dyv[e] * silu * ng4
                do.append(r * u - o[e] * (r * r * r) * _per_head(_lane_mean, u * o[e]))
                dng4 = jnp.sum(dyv[e] * silu * o[e] * r, axis=0, keepdims=True)
                dng_acc = dng_acc + ((dng4[:, hl[0]] + dng4[:, hl[1]]) + (dng4[:, hl[2]] + dng4[:, hl[3]]))
            es, qm, km = [], [], []
            p = [[jnp.zeros((cs, cs), F32) for _ in heads] for _ in exs]
            for li in range(nl):
                dec = [jnp.exp(e_all[e][cs * (li + 1):cs * (li + 2)]) for e in exs]
                es.append(dec)
                qm.append([q[e] * dec[e] for e in exs])
                km.append([k[e] * dec[e] for e in exs])
                mk = m_ref[li]
                p = [[p[e][h] + mk * _dot(qm[li][e][:, hl[h]], km[li][e][:, hl[h]], NT) for h in heads] for e in exs]
            dp = [[_dot(do[e][:, hl[h]], v[e][:, hl[h]], NT) for h in heads] for e in exs]
            dv_p = [[_dot(p[e][h], do[e][:, hl[h]], TN) for h in heads] for e in exs]
            dv_s = [[_dot(kd[e][:, hl[h]], dsts[e][h], NT) for h in heads] for e in exs]
            dqb = [side_by_side([_dot(do[e][:, hl[h]], sts[e][h]) for h in heads]) for e in exs]
            dkd = [side_by_side([_dot(v[e][:, hl[h]], dsts[e][h]) for h in heads]) for e in exs]
            new_dst = [[_dot(do[e][:, hl[h]], qb[e][:, hl[h]], TN) for h in heads] for e in exs]
            dv = [side_by_side([dv_p[e][h] + dv_s[e][h] for h in heads]) + _per_head(_lane_sum, q[e] * k[e]) * do[e]
                  for e in exs]
            dq = [dqb[e] * eb[e] for e in exs]
            dk = [dkd[e] * ekd[e] for e in exs]
            de = []
            for e in exs:
                dbl = (jnp.sum(dkd[e] * kd[e], axis=0, keepdims=True)
                       + side_by_side([jnp.sum(dsts[e][h] * sts[e][h], axis=0, keepdims=True) for h in heads]) * ebl[e])
                de.append([dqb[e] * qb[e] - dkd[e] * kd[e] + jnp.where(last_row, dbl, 0.0)])
            for li in range(nl):
                mk = m_ref[li]
                dpm = [[mk * dp[e][h] for h in heads] for e in exs]
                dqm = [side_by_side([_dot(dpm[e][h], km[li][e][:, hl[h]]) for h in heads]) for e in exs]
                dkm = [side_by_side([_dot(dpm[e][h], qm[li][e][:, hl[h]], TN) for h in heads]) for e in exs]
                for e in exs:
                    dq[e] = dq[e] + dqm[e] * es[li][e]
                    dk[e] = dk[e] + dkm[e] * es[li][e]
                    de[e].append(dqm[e] * qm[li][e] + dkm[e] * km[li][e])
            dg = [_split_dot(a_mat, jnp.concatenate(de[e], axis=0), TN, 2) for e in exs]
            for e in exs:
                dpd = _per_head(_lane_sum, do[e] * v[e])
                df = dg[e] / f[e] - (dk[e] + dpd * q[e])
                dp_ref[e, rows, 0:hw] = _mx(dq[e] + dpd * k[e])
                dp_ref[e, rows, hw:2 * hw] = _mx(df * (1.0 - lb) * sig[e] * (1.0 - sig[e]))
                dp_ref[e, rows, 2 * hw:3 * hw] = _mx(dv[e])
                dp_ref[e, rows, 3 * hw:4 * hw] = _mx(dgl[e])
                for h in heads:
                    dst_ref[e, h] = dsts[e][h] * ebl[e][:, hl[h]] + new_dst[e][h]
                dlb_acc = dlb_acc + jnp.sum(df * (1.0 - sig[e]), axis=0, keepdims=True)
            return dlb_acc, dng_acc

        dlb, dng = lax.fori_loop(0, nct, chunk, (jnp.zeros((1, hw), F32), jnp.zeros((1, HG_D), F32)))

        @pl.when(first)
        def _():
            dlb_ref[...] = jnp.zeros_like(dlb_ref)
            dng_ref[...] = jnp.zeros_like(dng_ref)

        mx = jnp.max(lbp_v, axis=0, keepdims=True)
        e = jnp.exp(lbp_v - mx)
        s0 = e[0:1, :] / jnp.sum(e, axis=0, keepdims=True)
        da0 = dlb * s0 * (1.0 - s0)
        dlb_ref[...] += jnp.concatenate([da0, -da0], axis=0)
        dng_ref[...] += dng

    rows3 = lambda w: pl.BlockSpec((bsz, ts, w), lambda s: (0, ns - 1 - s, 0))
    dproj, dlb, dng = pl.pallas_call(
        body, name="hgrn2_bwd", grid=(ns,),
        in_specs=[rows3(HG_COLS),
                  pl.BlockSpec((2, hw), lambda s: (0, 0)),
                  pl.BlockSpec((1, HG_D), lambda s: (0, 0)),
                  pl.BlockSpec(a_all.shape, lambda s: (0, 0)),
                  pl.BlockSpec(masks.shape, lambda s: (0, 0, 0)),
                  rows3(hw),
                  pl.BlockSpec((bsz, HG_HEADS, nct, HG_D, HG_D), lambda s: (0, 0, ns - 1 - s, 0, 0)),
                  rows3(hw)] + [pl.BlockSpec(memory_space=pl.ANY)] * len(after),
        out_specs=(rows3(HG_COLS),
                   pl.BlockSpec((2, hw), lambda s: (0, 0)),
                   pl.BlockSpec((1, HG_D), lambda s: (0, 0))),
        out_shape=(jax.ShapeDtypeStruct((bsz, seq, HG_COLS), _MXU_DTYPE),
                   jax.ShapeDtypeStruct((2, hw), F32),
                   jax.ShapeDtypeStruct((1, HG_D), F32)),
        scratch_shapes=[pltpu.VMEM((bsz, HG_HEADS, HG_D, HG_D), F32)],
        compiler_params=_params(("arbitrary",)),
    )(proj.reshape(bsz, seq, HG_COLS), lbp, ng, a_all, masks, o_all.reshape(bsz, seq, hw), states,
      dy.reshape(bsz, seq, dy.shape[1]), *after)
    return dproj.reshape(t, HG_COLS), dlb, dng


def _sw_constants():
    half = ROT_DIM // 2
    inv = (np.float32(ROPE_THETA) ** (-(np.arange(half, dtype=np.float32) * np.float32(2.0) / np.float32(ROT_DIM)))
           ).astype(np.float32)
    freq = np.zeros((1, 128), np.float32)
    sign = np.zeros((1, 128), np.float32)
    for h in range(2):
        freq[0, 64 * h:64 * h + half] = inv
        freq[0, 64 * h + half:64 * h + 2 * half] = inv
        sign[0, 64 * h:64 * h + half] = -1.0
        sign[0, 64 * h + half:64 * h + 2 * half] = 1.0
    seg = np.kron(np.eye(8, dtype=np.float32), np.full((64, 64), 1.0 / 64.0, np.float32))
    return freq, sign, seg


def _rope_tables(pos, freq, sign):
    ang = pos.astype(F32) * freq
    return jnp.cos(ang), jnp.sin(ang) * sign


def _tile_lanes(v, times):
    return v if times == 1 else jnp.concatenate([v] * times, axis=1)


def _swap_halves(v):
    w = v.shape[1]
    half = ROT_DIM // 2
    lane = lax.broadcasted_iota(jnp.int32, v.shape, 1) % SW_HD
    return jnp.where(lane < half, pltpu.roll(v, w - half, 1), jnp.where(lane < 2 * half, pltpu.roll(v, half, 1), 0.0))


def _sw_norm_rope(tv, gain, seg, cosv, sinv):
    w = tv.shape[1]
    ms = _split_dot_rhs(tv * tv, seg[0:w, 0:w])
    r = lax.rsqrt(ms + EPS)
    tn = tv * r * gain
    reps = w // 128
    return tn * _tile_lanes(cosv, reps) + _swap_halves(tn) * _tile_lanes(sinv, reps), r


def _split_dot_rhs(v, a):
    hi = _mx(v)
    lo = _mx(v - hi.astype(F32))
    return (lax.dot_general(hi, a, (NN, ((), ())), preferred_element_type=F32)
            + lax.dot_general(lo, a, (NN, ((), ())), preferred_element_type=F32))


def _sw_norm_rope_bwd(dt, tv, r, gain, seg, cosv, sinv):
    w = tv.shape[1]
    reps = w // 128
    dtn = dt * _tile_lanes(cosv, reps) + _swap_halves(dt * _tile_lanes(sinv, reps))
    u = dtn * gain
    dtv = r * u - tv * (r * r * r) * _split_dot_rhs(u * tv, seg[0:w, 0:w])
    return dtv, jnp.sum(dtn * tv * r, axis=0, keepdims=True)


def _sw_scores(qh, kp, kc):
    return _dot(qh, kp, NT), _dot(qh, kc, NT)


def _sw_probs(raw, sink, first_block):
    scale = SW_HD ** -0.5
    qi = lax.broadcasted_iota(jnp.int32, (SW_BLOCK, SW_BLOCK), 0)
    kj = lax.broadcasted_iota(jnp.int32, (SW_BLOCK, SW_BLOCK), 1)
    ok_prev = jnp.logical_and(kj > qi, jnp.logical_not(first_block))
    ok_cur = kj <= qi
    sp = jnp.where(ok_prev, raw[0] * scale, -jnp.inf)
    sc = jnp.where(ok_cur, raw[1] * scale, -jnp.inf)
    m = jnp.maximum(jnp.maximum(jnp.max(sp, axis=1, keepdims=True), jnp.max(sc, axis=1, keepdims=True)), sink)
    pp, pc = jnp.exp(sp - m), jnp.exp(sc - m)
    es = jnp.exp(sink - m)
    den = jnp.sum(pp, axis=1, keepdims=True) + jnp.sum(pc, axis=1, keepdims=True) + es
    return pp / den, pc / den, es / den


def _sw_specs(nb):
    def cur(b, n):
        return b * nb + jnp.minimum(n, nb - 1)

    def prev(b, n):
        return b * nb + jnp.maximum(jnp.minimum(n, nb - 1) - 1, 0)

    return cur, prev


def _sw_fwd(proj, pos, qg, kg, sinks, y_in, bsz, seq):
    t = proj.shape[0]
    nb = seq // SW_BLOCK
    freq_np, sign_np, seg_np = _sw_constants()
    freq, sign = jnp.asarray(freq_np), jnp.asarray(sign_np)
    seg = jnp.asarray(seg_np, _MXU_DTYPE)
    cur, prev = _sw_specs(nb)

    def body(q_ref, kc_ref, kp_ref, vc_ref, vp_ref, pc_ref, pp_ref, qg_ref, kg_ref, sk_ref, fr_ref, sn_ref, seg_ref,
             yin_ref, y_ref):
        del yin_ref
        n = pl.program_id(1)
        segv = seg_ref[...]
        cos_c, sin_c = _rope_tables(pc_ref[...], fr_ref[...], sn_ref[...])
        cos_p, sin_p = _rope_tables(pp_ref[...], fr_ref[...], sn_ref[...])
        qr, _ = _sw_norm_rope(q_ref[...], qg_ref[...], segv, cos_c, sin_c)
        kcr, _ = _sw_norm_rope(kc_ref[...], kg_ref[...], segv, cos_c, sin_c)
        kpr, _ = _sw_norm_rope(kp_ref[...], kg_ref[...], segv, cos_p, sin_p)
        vc, vp = vc_ref[...], vp_ref[...]
        ks = [slice(SW_HD * (h // SW_GROUP), SW_HD * (h // SW_GROUP + 1)) for h in range(SW_HEADS)]
        raw = [_sw_scores(qr[:, SW_HD * h:SW_HD * (h + 1)], kpr[:, ks[h]], kcr[:, ks[h]]) for h in range(SW_HEADS)]
        probs = [_sw_probs(raw[h], sk_ref[0, h], n == 0) for h in range(SW_HEADS)]
        for h in range(SW_HEADS):
            y_ref[:, SW_HD * h:SW_HD * (h + 1)] = _dot(probs[h][0], vp[:, ks[h]]) + _dot(probs[h][1], vc[:, ks[h]])

    rowq = pl.BlockSpec((SW_BLOCK, 512), lambda b, n: (cur(b, n), 0))
    full = lambda a: pl.BlockSpec(a.shape, lambda b, n: (0,) * a.ndim)
    yw = y_in.shape[1]
    return pl.pallas_call(
        body, name="swa_fwd", grid=(bsz, nb),
        in_specs=[rowq,
                  pl.BlockSpec((SW_BLOCK, 128), lambda b, n: (cur(b, n), 4)),
                  pl.BlockSpec((SW_BLOCK, 128), lambda b, n: (prev(b, n), 4)),
                  pl.BlockSpec((SW_BLOCK, 128), lambda b, n: (cur(b, n), 5)),
                  pl.BlockSpec((SW_BLOCK, 128), lambda b, n: (prev(b, n), 5)),
                  pl.BlockSpec((SW_BLOCK, 1), lambda b, n: (cur(b, n), 0)),
                  pl.BlockSpec((SW_BLOCK, 1), lambda b, n: (prev(b, n), 0)),
                  full(qg), full(kg),
                  pl.BlockSpec(memory_space=pltpu.SMEM),
                  full(freq), full(sign), full(seg),
                  pl.BlockSpec(memory_space=pl.ANY)],
        out_specs=pl.BlockSpec((SW_BLOCK, 512), lambda b, n: (cur(b, n), 1)),
        out_shape=jax.ShapeDtypeStruct((t, yw), F32),
        input_output_aliases={13: 0},
        compiler_params=_params(("parallel", "parallel")),
    )(proj, proj, proj, proj, proj, pos, pos, qg, kg, sinks, freq, sign, seg, y_in)


def _sw_bwd(proj, pos, qg, kg, sinks, y, dy, bsz, seq):
    t = proj.shape[0]
    nb = seq // SW_BLOCK
    freq_np, sign_np, seg_np = _sw_constants()
    freq, sign = jnp.asarray(freq_np), jnp.asarray(sign_np)
    seg = jnp.asarray(seg_np, _MXU_DTYPE)
    cur, prev = _sw_specs(nb)
    scale = SW_HD ** -0.5

    def body(q_ref, kc_ref, kp_ref, vc_ref, vp_ref, pc_ref, pp_ref, qg_ref, kg_ref, sk_ref, fr_ref, sn_ref, seg_ref,
             y_ref, dy_ref, dp_ref, dqg_ref, dkg_ref, dsk_ref,
             dq_car, dkv_car, dqr_s, dkc_s, dkp_s, dvc_s, dvp_s, gq_acc, gk_acc, sk_acc):
        b, n = pl.program_id(0), pl.program_id(1)
        first = jnp.logical_and(b == 0, n == 0)
        last = jnp.logical_and(b == pl.num_programs(0) - 1, n == nb)

        @pl.when(first)
        def _():
            gq_acc[...] = jnp.zeros_like(gq_acc)
            gk_acc[...] = jnp.zeros_like(gk_acc)
            sk_acc[...] = jnp.zeros_like(sk_acc)

        @pl.when(n < nb)
        def _():
            segv = seg_ref[...]
            cos_c, sin_c = _rope_tables(pc_ref[...], fr_ref[...], sn_ref[...])
            cos_p, sin_p = _rope_tables(pp_ref[...], fr_ref[...], sn_ref[...])
            qv, kcv, kpv = q_ref[...], kc_ref[...], kp_ref[...]
            qr, rq = _sw_norm_rope(qv, qg_ref[...], segv, cos_c, sin_c)
            kcr, rkc = _sw_norm_rope(kcv, kg_ref[...], segv, cos_c, sin_c)
            kpr, rkp = _sw_norm_rope(kpv, kg_ref[...], segv, cos_p, sin_p)
            vc, vp = vc_ref[...], vp_ref[...]
            lane = lax.broadcasted_iota(jnp.int32, (1, 128), 1)
            dsk = jnp.zeros((1, 128), F32)
            heads = range(SW_HEADS)
            ks = [slice(SW_HD * (h // SW_GROUP), SW_HD * (h // SW_GROUP + 1)) for h in heads]
            hs = [slice(SW_HD * h, SW_HD * (h + 1)) for h in heads]
            qh = [qr[:, hs[h]] for h in heads]
            doh = [dy_ref[:, hs[h]] for h in heads]
            raw = [_sw_scores(qh[h], kpr[:, ks[h]], kcr[:, ks[h]]) for h in heads]
            dpp = [_dot(doh[h], vp[:, ks[h]], NT) for h in heads]
            dpc = [_dot(doh[h], vc[:, ks[h]], NT) for h in heads]
            probs = [_sw_probs(raw[h], sk_ref[0, h], n == 0) for h in heads]
            dsp, dsc = [], []
            for h in heads:
                pp, pc, ps = probs[h]
                delta = jnp.sum(doh[h] * y_ref[:, hs[h]], axis=1, keepdims=True)
                dsp.append(pp * (dpp[h] - delta) * scale)
                dsc.append(pc * (dpc[h] - delta) * scale)
                dsk = dsk + jnp.where(lane == h, -jnp.sum(ps * delta), 0.0)
            for h in heads:
                dqr_s[:, hs[h]] = _dot(dsp[h], kpr[:, ks[h]]) + _dot(dsc[h], kcr[:, ks[h]])
            for kv in range(SW_KV_HEADS):
                group = range(SW_GROUP * kv, SW_GROUP * (kv + 1))
                kvs = slice(SW_HD * kv, SW_HD * (kv + 1))
                dvp_s[:, kvs] = sum(_dot(probs[h][0], doh[h], TN) for h in group)
                dvc_s[:, kvs] = sum(_dot(probs[h][1], doh[h], TN) for h in group)
                dkp_s[:, kvs] = sum(_dot(dsp[h], qh[h], TN) for h in group)
                dkc_s[:, kvs] = sum(_dot(dsc[h], qh[h], TN) for h in group)
            dq, gq = _sw_norm_rope_bwd(dqr_s[...], qv, rq, qg_ref[...], segv, cos_c, sin_c)
            dkc, gkc = _sw_norm_rope_bwd(dkc_s[...], kcv, rkc, kg_ref[...], segv, cos_c, sin_c)
            dkp, gkp = _sw_norm_rope_bwd(dkp_s[...], kpv, rkp, kg_ref[...], segv, cos_p, sin_p)
            gq_acc[...] += gq
            gk_acc[...] += gkc + gkp
            sk_acc[...] += dsk

            @pl.when(n > 0)
            def _():
                dp_ref[:, 0:512] = _mx(dq_car[...])
                dp_ref[:, 512:640] = _mx(dkv_car[:, 0:128] + dkp)
                dp_ref[:, 640:768] = _mx(dkv_car[:, 128:256] + dvp_s[...])

            dq_car[...] = dq
            dkv_car[:, 0:128] = dkc
            dkv_car[:, 128:256] = dvc_s[...]

        @pl.when(n == nb)
        def _():
            dp_ref[:, 0:512] = _mx(dq_car[...])
            dp_ref[:, 512:768] = _mx(dkv_car[...])

        @pl.when(last)
        def _():
            gq = gq_acc[...]
            acc = gq[:, 0:SW_HD]
            for h in range(1, SW_HEADS):
                acc = acc + gq[:, SW_HD * h:SW_HD * (h + 1)]
            dqg_ref[...] = acc
            gk = gk_acc[...]
            dkg_ref[...] = gk[:, 0:SW_HD] + gk[:, SW_HD:2 * SW_HD]
            dsk_ref[...] = sk_acc[...]

    rowq = pl.BlockSpec((SW_BLOCK, 512), lambda b, n: (cur(b, n), 0))
    full = lambda a: pl.BlockSpec(a.shape, lambda b, n: (0,) * a.ndim)

    def out_row(b, n):
        return b * nb + jnp.maximum(n - 1, 0)

    return pl.pallas_call(
        body, name="swa_bwd", grid=(bsz, nb + 1),
        in_specs=[rowq,
                  pl.BlockSpec((SW_BLOCK, 128), lambda b, n: (cur(b, n), 4)),
                  pl.BlockSpec((SW_BLOCK, 128), lambda b, n: (prev(b, n), 4)),
                  pl.BlockSpec((SW_BLOCK, 128), lambda b, n: (cur(b, n), 5)),
                  pl.BlockSpec((SW_BLOCK, 128), lambda b, n: (prev(b, n), 5)),
                  pl.BlockSpec((SW_BLOCK, 1), lambda b, n: (cur(b, n), 0)),
                  pl.BlockSpec((SW_BLOCK, 1), lambda b, n: (prev(b, n), 0)),
                  full(qg), full(kg),
                  pl.BlockSpec(memory_space=pltpu.SMEM),
                  full(freq), full(sign), full(seg),
                  pl.BlockSpec((SW_BLOCK, 512), lambda b, n: (cur(b, n), 1)),
                  pl.BlockSpec((SW_BLOCK, 512), lambda b, n: (cur(b, n), 1))],
        out_specs=(pl.BlockSpec((SW_BLOCK, SW_COLS), lambda b, n: (out_row(b, n), 0)),
                   pl.BlockSpec((1, SW_HD), lambda b, n: (0, 0)),
                   pl.BlockSpec((1, SW_HD), lambda b, n: (0, 0)),
                   pl.BlockSpec((1, 128), lambda b, n: (0, 0))),
        out_shape=(jax.ShapeDtypeStruct((t, SW_COLS), _MXU_DTYPE),
                   jax.ShapeDtypeStruct((1, SW_HD), F32),
                   jax.ShapeDtypeStruct((1, SW_HD), F32),
                   jax.ShapeDtypeStruct((1, 128), F32)),
        scratch_shapes=[pltpu.VMEM((SW_BLOCK, 512), F32), pltpu.VMEM((SW_BLOCK, 256), F32),
                        pltpu.VMEM((SW_BLOCK, 512), F32),
                        pltpu.VMEM((SW_BLOCK, 128), F32), pltpu.VMEM((SW_BLOCK, 128), F32),
                        pltpu.VMEM((SW_BLOCK, 128), F32), pltpu.VMEM((SW_BLOCK, 128), F32),
                        pltpu.VMEM((1, 512), F32), pltpu.VMEM((1, 128), F32), pltpu.VMEM((1, 128), F32)],
        compiler_params=_params(("arbitrary", "arbitrary")),
    )(proj, proj, proj, proj, proj, pos, pos, qg, kg, sinks, freq, sign, seg, y, dy)


def _head_rms(tv, gain):
    r = lax.rsqrt(jnp.mean(tv * tv, axis=1, keepdims=True) + EPS)
    return tv * r * gain, r


def _head_rms_bwd(dtn, tv, r, gain):
    u = dtn * gain
    return r * u - tv * (r * r * r) * jnp.mean(u * tv, axis=1, keepdims=True), jnp.sum(dtn * tv * r, axis=0, keepdims=True)


def _xa_softmax(raw):
    s = raw * (XA_HD ** -0.5)
    e = jnp.exp(s - jnp.max(s, axis=1, keepdims=True))
    return e / jnp.sum(e, axis=1, keepdims=True)


def _xa_fwd(qx, kvx, qg, kg, bsz, seq, mlen, *, tq=512):
    t = qx.shape[0]
    tq = min(tq, seq)
    nq = seq // tq
    w = XA_HEADS * XA_HD

    def body(q_ref, kv_ref, qg_ref, kg_ref, o_ref):
        heads = range(XA_HEADS)
        hs = [slice(XA_HD * h, XA_HD * (h + 1)) for h in heads]
        qn = [_head_rms(q_ref[:, hs[h]], qg_ref[...])[0] for h in heads]
        kn = [_head_rms(kv_ref[:, hs[h]], kg_ref[...])[0] for h in heads]
        raw = [_dot(qn[h], kn[h], NT) for h in heads]
        p = [_xa_softmax(raw[h]) for h in heads]
        for h in heads:
            o_ref[:, hs[h]] = _dot(p[h], kv_ref[:, w + XA_HD * h:w + XA_HD * (h + 1)]).astype(o_ref.dtype)

    vec = pl.BlockSpec((1, XA_HD), lambda b, i: (0, 0))
    return pl.pallas_call(
        body, name="xattn_fwd", grid=(bsz, nq),
        in_specs=[pl.BlockSpec((tq, w), lambda b, i: (b * nq + i, 0)),
                  pl.BlockSpec((mlen, 2 * w), lambda b, i: (b, 0)), vec, vec],
        out_specs=pl.BlockSpec((tq, w), lambda b, i: (b * nq + i, 0)),
        out_shape=jax.ShapeDtypeStruct((t, w), _MXU_DTYPE),
        compiler_params=_params(("parallel", "parallel")),
    )(qx, kvx, qg, kg)


def _xa_bwd(qx, kvx, qg, kg, do, bsz, seq, mlen, *, tq=512):
    t = qx.shape[0]
    tq = min(tq, seq)
    nq = seq // tq
    w = XA_HEADS * XA_HD
    scale = XA_HD ** -0.5

    def body(q_ref, kv_ref, qg_ref, kg_ref, do_ref, dq_ref, dkv_ref, dqg_ref, dkg_ref):
        b, i = pl.program_id(0), pl.program_id(1)

        @pl.when(jnp.logical_and(b == 0, i == 0))
        def _():
            dqg_ref[...] = jnp.zeros_like(dqg_ref)
            dkg_ref[...] = jnp.zeros_like(dkg_ref)

        @pl.when(i == 0)
        def _():
            dkv_ref[...] = jnp.zeros_like(dkv_ref)

        heads = range(XA_HEADS)
        hs = [slice(XA_HD * h, XA_HD * (h + 1)) for h in heads]
        vs = [slice(w + XA_HD * h, w + XA_HD * (h + 1)) for h in heads]
        qv = [q_ref[:, hs[h]] for h in heads]
        kv = [kv_ref[:, hs[h]] for h in heads]
        doh = [do_ref[:, hs[h]] for h in heads]
        qn = [_head_rms(qv[h], qg_ref[...]) for h in heads]
        kn = [_head_rms(kv[h], kg_ref[...]) for h in heads]
        raw = [_dot(qn[h][0], kn[h][0], NT) for h in heads]
        dp = [_dot(doh[h], kv_ref[:, vs[h]], NT) for h in heads]
        p = [_xa_softmax(raw[h]) for h in heads]
        ds = [p[h] * (dp[h] - jnp.sum(p[h] * dp[h], axis=1, keepdims=True)) * scale for h in heads]
        dqn = [_dot(ds[h], kn[h][0]) for h in heads]
        dkn = [_dot(ds[h], qn[h][0], TN) for h in heads]
        dvv = [_dot(p[h], doh[h], TN) for h in heads]
        gq_sum = jnp.zeros((1, XA_HD), F32)
        gk_sum = jnp.zeros((1, XA_HD), F32)
        for h in heads:
            dqv, gq = _head_rms_bwd(dqn[h], qv[h], qn[h][1], qg_ref[...])
            dkv, gk = _head_rms_bwd(dkn[h], kv[h], kn[h][1], kg_ref[...])
            dq_ref[:, hs[h]] = dqv.astype(dq_ref.dtype)
            dkv_ref[:, hs[h]] += dkv
            dkv_ref[:, vs[h]] += dvv[h]
            gq_sum = gq_sum + gq
            gk_sum = gk_sum + gk
        dqg_ref[...] += gq_sum
        dkg_ref[...] += gk_sum

    vec = pl.BlockSpec((1, XA_HD), lambda b, i: (0, 0))
    row = pl.BlockSpec((tq, w), lambda b, i: (b * nq + i, 0))
    mem = pl.BlockSpec((mlen, 2 * w), lambda b, i: (b, 0))
    return pl.pallas_call(
        body, name="xattn_bwd", grid=(bsz, nq),
        in_specs=[row, mem, vec, vec, row],
        out_specs=(row, mem, vec, vec),
        out_shape=(jax.ShapeDtypeStruct((t, w), _MXU_DTYPE), jax.ShapeDtypeStruct((bsz * mlen, 2 * w), F32),
                   jax.ShapeDtypeStruct((1, XA_HD), F32), jax.ShapeDtypeStruct((1, XA_HD), F32)),
        compiler_params=_params(("arbitrary", "arbitrary")),
    )(qx, kvx, qg, kg, do)


def _loss_finish(sq_row, d_model):
    def body(s_ref, o_ref):
        o_ref[...] = jnp.zeros_like(o_ref) + 0.5 * jnp.sum(s_ref[...]) / float(d_model)

    return pl.pallas_call(body, name="loss_finish", out_shape=jax.ShapeDtypeStruct((1, 128), F32))(sq_row)


def _adamw_math(w, g, m, v):
    m = ADAM_B1 * m + (1.0 - ADAM_B1) * g
    v = ADAM_B2 * v + (1.0 - ADAM_B2) * (g * g)
    m_hat = m / (1.0 - ADAM_B1 ** ADAM_STEP)
    v_hat = v / (1.0 - ADAM_B2 ** ADAM_STEP)
    return -ADAM_LR * (m_hat / (jnp.sqrt(v_hat) + ADAM_EPS) + ADAM_WD * w), m, v


def _adamw_big(w, g, m, v, *, name, tr=512):
    r, c = w.shape
    tr = min(tr, r)

    def body(w_ref, g_ref, m_ref, v_ref, go_ref, d_ref, mo_ref, vo_ref):
        gv = g_ref[...]
        d, mn, vn = _adamw_math(w_ref[...], gv, m_ref[...], v_ref[...])
        go_ref[...] = gv
        d_ref[...] = d
        mo_ref[...] = mn
        vo_ref[...] = vn

    spec = pl.BlockSpec((tr, c), lambda i: (i, 0))
    shp = jax.ShapeDtypeStruct((r, c), F32)
    return pl.pallas_call(
        body, name=name, grid=(r // tr,), in_specs=[spec] * 4, out_specs=(spec,) * 4, out_shape=(shp,) * 4,
        compiler_params=_params(("parallel",)),
    )(w, g, m, v)


def _adamw_small(ws, gs, ms, vs):
    n = len(ws)

    def body(*refs):
        for i in range(n):
            d, mn, vn = _adamw_math(refs[i][...], refs[n + i][...], refs[2 * n + i][...], refs[3 * n + i][...])
            refs[4 * n + i][...] = d
            refs[5 * n + i][...] = mn
            refs[6 * n + i][...] = vn

    shapes = tuple(jax.ShapeDtypeStruct(w.shape, F32) for w in ws)
    return pl.pallas_call(body, name="adamw_small", out_shape=shapes * 3)(*ws, *gs, *ms, *vs)


def _add_halves(g, recv, c_idx, *, name, tr=512):
    _, r, c = g.shape
    h = r // 2
    tr = min(tr, h)
    nt = h // tr

    def body(c_ref, g_ref, r_ref, o_ref):
        del c_ref
        o_ref[...] = g_ref[...] + r_ref[...]

    return pl.pallas_call(
        body, name=name,
        grid_spec=pltpu.PrefetchScalarGridSpec(
            num_scalar_prefetch=1, grid=(4, nt),
            in_specs=[pl.BlockSpec((None, tr, c), lambda k, i, cr: (k, cr[0] * nt + i, 0)),
                      pl.BlockSpec((None, tr, c), lambda k, i, cr: (k, i, 0))],
            out_specs=pl.BlockSpec((None, tr, c), lambda k, i, cr: (k, i, 0))),
        out_shape=jax.ShapeDtypeStruct((4, h, c), F32),
        compiler_params=_params(("parallel", "parallel")),
    )(c_idx, g, recv)


def _add_chips(p, recv, place_idx, *, name, tr=512, after=()):
    _, h, c = p.shape
    tr = min(tr, h)
    nt = h // tr

    def body(pi_ref, p_ref, r_ref, *rest):
        del pi_ref
        rest[-1][...] = ((p_ref[...] + r_ref[0]) + r_ref[1]) + r_ref[2]

    return pl.pallas_call(
        body, name=name,
        grid_spec=pltpu.PrefetchScalarGridSpec(
            num_scalar_prefetch=1, grid=(nt,),
            in_specs=[pl.BlockSpec((None, tr, c), lambda i, pi: (pi[0], i, 0)),
                      pl.BlockSpec((3, tr, c), lambda i, pi: (0, i, 0))] + [pl.BlockSpec(memory_space=pl.ANY)] * len(after),
            out_specs=pl.BlockSpec((tr, c), lambda i, pi: (pi[1] * nt + i, 0))),
        out_shape=jax.ShapeDtypeStruct((2 * h, c), F32),
        compiler_params=_params(("parallel",)),
    )(place_idx, p, recv, *after)


def _place_shard(shard, place_idx, *, name, tr=512, after=()):
    r, c = shard.shape
    tr = min(tr, r)

    def body(pi_ref, s_ref, *rest):
        del pi_ref
        rest[-1][...] = s_ref[...]

    return pl.pallas_call(
        body, name=name,
        grid_spec=pltpu.PrefetchScalarGridSpec(
            num_scalar_prefetch=1, grid=(r // tr,),
            in_specs=[pl.BlockSpec((tr, c), lambda i, pi: (i, 0))] + [pl.BlockSpec(memory_space=pl.ANY)] * len(after),
            out_specs=pl.BlockSpec((None, tr, c), lambda i, pi: (pi[0], i, 0))),
        out_shape=jax.ShapeDtypeStruct((4, r, c), shard.dtype),
        compiler_params=_params(("parallel",)),
    )(place_idx, shard, *after)


def _place():
    x, y, c = lax.axis_index("x"), lax.axis_index("y"), lax.axis_index("c")
    chips = [(1 - x, y), (x, 1 - y), (1 - x, 1 - y)]
    return x, y, c, chips


ANY = pl.BlockSpec(memory_space=pl.ANY)


def _exchange_halves(grads, name):
    n = len(grads)

    def body(*refs):
        ins, outs = refs[:n], refs[n:2 * n]
        send_sems, recv_sems = refs[2 * n:]
        x, y, c, _ = _place()

        def copy(a):
            h = ins[a].shape[1] // 2
            return pltpu.make_async_remote_copy(
                src_ref=ins[a].at[:, pl.ds((1 - c) * h, h), :], dst_ref=outs[a],
                send_sem=send_sems.at[a], recv_sem=recv_sems.at[a], device_id=(x, y, 1 - c), device_id_type=MESH)

        for a in range(n):
            copy(a).start()
        for a in range(n):
            copy(a).wait_recv()
        for a in range(n):
            copy(a).wait_send()

    return pl.pallas_call(
        body, name=name,
        in_specs=[ANY] * n, out_specs=tuple([ANY] * n),
        out_shape=tuple(jax.ShapeDtypeStruct((4, g.shape[1] // 2, g.shape[2]), g.dtype) for g in grads),
        scratch_shapes=[pltpu.SemaphoreType.DMA((n,)), pltpu.SemaphoreType.DMA((n,))],
    )(*grads)


HBM = pl.BlockSpec(memory_space=pltpu.HBM)
SEM = pl.BlockSpec(memory_space=pltpu.SEMAPHORE)
EFFECT = pltpu.SideEffectType.DATAFLOW_SIDE_EFFECTING


def _in_hbm(a):
    return pltpu.with_memory_space_constraint(a, pltpu.HBM)


def _split_copy_calls(name, srcs, lands, n_copies, make_copies):
    ns, nl = len(srcs), len(lands)
    nb = ns + nl

    def start(after=()):
        n_after = len(after)

        def body(*refs):
            outs = refs[nb + n_after:]
            copies = make_copies(refs[:ns], refs[ns:nb], outs[0], outs[1])
            for cp in copies:
                cp.start()
            token = refs[-1]
            token[...] = jnp.zeros_like(token)

        bufs = [_in_hbm(a) for a in list(srcs) + list(lands)]
        out = pl.pallas_call(
            body, name=name + "_start",
            out_shape=(pltpu.SemaphoreType.DMA((n_copies,)), pltpu.SemaphoreType.DMA((n_copies,)),
                       *[pltpu.HBM(a.shape, a.dtype) for a in bufs], jax.ShapeDtypeStruct((8, 128), F32)),
            in_specs=[HBM] * nb + [pl.BlockSpec(memory_space=pl.ANY)] * n_after,
            out_specs=(SEM, SEM, *[HBM] * nb, pl.BlockSpec(memory_space=pltpu.VMEM)),
            input_output_aliases={i: 2 + i for i in range(nb)},
            compiler_params=pltpu.CompilerParams(has_side_effects=EFFECT),
        )(*bufs, *after)
        return dict(send=out[0], recv=out[1], bufs=list(out[2:2 + nb]), token=out[-1])

    def wait(state, after):
        def body(*refs):
            copies = make_copies(refs[:ns], refs[ns:nb], refs[nb], refs[nb + 1])
            for cp in copies:
                cp.wait_send()
            for cp in copies:
                cp.wait_recv()

        bufs = state["bufs"]
        out = pl.pallas_call(
            body, name=name + "_wait",
            out_shape=tuple(pltpu.HBM(a.shape, a.dtype) for a in bufs),
            in_specs=[HBM] * nb + [SEM, SEM] + [pl.BlockSpec(memory_space=pl.ANY)] * len(after),
            out_specs=tuple([HBM] * nb),
            input_output_aliases={i: i for i in range(nb)},
            compiler_params=pltpu.CompilerParams(has_side_effects=EFFECT),
        )(*bufs, state["send"], state["recv"], *after)
        return list(out[:ns]), list(out[ns:])

    return start, wait


def _scatter_chips_split(name, parts):
    n = len(parts)
    lands = [lax.empty((3,) + p.shape[1:], p.dtype) for p in parts]

    def make_copies(srcs, lnds, send_sems, recv_sems):
        _, _, c, chips = _place()
        return [pltpu.make_async_remote_copy(
            src_ref=srcs[a].at[2 * px + py], dst_ref=lnds[a].at[j], send_sem=send_sems.at[a * 3 + j],
            recv_sem=recv_sems.at[a * 3 + j], device_id=(px, py, c), device_id_type=MESH)
            for a in range(n) for j, (px, py) in enumerate(chips)]

    return _split_copy_calls(name, parts, lands, 3 * n, make_copies)


def _exchange_halves_split(name, grads):
    n = len(grads)
    lands = [lax.empty((4, g.shape[1] // 2, g.shape[2]), g.dtype) for g in grads]

    def make_copies(srcs, lnds, send_sems, recv_sems):
        x, y, c, _ = _place()
        out = []
        for a in range(n):
            h = srcs[a].shape[1] // 2
            out.append(pltpu.make_async_remote_copy(
                src_ref=srcs[a].at[:, pl.ds((1 - c) * h, h), :], dst_ref=lnds[a], send_sem=send_sems.at[a],
                recv_sem=recv_sems.at[a], device_id=(x, y, 1 - c), device_id_type=MESH))
        return out

    return _split_copy_calls(name, grads, lands, n, make_copies)


def _gather_chips_split(name, shards, lands):
    n = len(shards)

    def make_copies(srcs, lnds, send_sems, recv_sems):
        x, y, c, chips = _place()
        out = []
        for a in range(n):
            h = srcs[a].shape[0] // 2
            for j, (px, py) in enumerate(chips):
                out.append(pltpu.make_async_remote_copy(
                    src_ref=srcs[a].at[pl.ds(c * h, h), :], dst_ref=lnds[a].at[2 * x + y, pl.ds(c * h, h), :],
                    send_sem=send_sems.at[a * 3 + j], recv_sem=recv_sems.at[a * 3 + j],
                    device_id=(px, py, c), device_id_type=MESH))
        return out

    return _split_copy_calls(name, shards, lands, 3 * n, make_copies)


def _gather_finish(gathered, name):
    n = len(gathered)

    def body(*refs):
        outs = refs[n:2 * n]
        send_sems, recv_sems = refs[2 * n:]
        x, y, c, chips = _place()

        def copy(a, j, chip_idx, which):
            h = outs[a].shape[1] // 2
            rows = outs[a].at[chip_idx, pl.ds(which * h, h), :]
            return pltpu.make_async_remote_copy(
                src_ref=rows, dst_ref=rows, send_sem=send_sems.at[a * 3 + j], recv_sem=recv_sems.at[a * 3 + j],
                device_id=(x, y, 1 - c), device_id_type=MESH)

        for a in range(n):
            for j, (px, py) in enumerate(chips):
                copy(a, j, 2 * px + py, c).start()
        for a in range(n):
            for j, (px, py) in enumerate(chips):
                copy(a, j, 2 * px + py, 1 - c).wait_recv()
        for a in range(n):
            for j, (px, py) in enumerate(chips):
                copy(a, j, 2 * px + py, c).wait_send()

    return pl.pallas_call(
        body, name=name,
        in_specs=[ANY] * n, out_specs=tuple([ANY] * n),
        out_shape=tuple(jax.ShapeDtypeStruct(g.shape, g.dtype) for g in gathered),
        input_output_aliases={i: i for i in range(n)},
        scratch_shapes=[pltpu.SemaphoreType.DMA((3 * n,)), pltpu.SemaphoreType.DMA((3 * n,))],
    )(*gathered)


def _gather_forward_split(name, gathered):
    n = len(gathered)

    def make_copies(srcs, lnds, send_sems, recv_sems):
        x, y, c, chips = _place()
        out = []
        for a in range(n):
            h = lnds[a].shape[1] // 2
            for j, (px, py) in enumerate(chips):
                rows = lnds[a].at[2 * px + py, pl.ds(c * h, h), :]
                out.append(pltpu.make_async_remote_copy(
                    src_ref=rows, dst_ref=rows, send_sem=send_sems.at[a * 3 + j], recv_sem=recv_sems.at[a * 3 + j],
                    device_id=(x, y, 1 - c), device_id_type=MESH))
        return out

    return _split_copy_calls(name, [], gathered, 3 * n, make_copies)


def _join_halves_split(name, fulls):
    n = len(fulls)

    def make_copies(srcs, lnds, send_sems, recv_sems):
        x, y, c, _ = _place()
        out = []
        for a in range(n):
            h = lnds[a].shape[0] // 2
            rows = lnds[a].at[pl.ds(c * h, h), :]
            out.append(pltpu.make_async_remote_copy(
                src_ref=rows, dst_ref=rows, send_sem=send_sems.at[a], recv_sem=recv_sems.at[a],
                device_id=(x, y, 1 - c), device_id_type=MESH))
        return out

    return _split_copy_calls(name, [], fulls, n, make_copies)


def _all_gather_small_split(sm):
    r, w = sm.shape

    def make_copies(srcs, lnds, send_sems, recv_sems):
        x, y, c, _ = _place()
        me = 4 * x + 2 * y + c
        rel = [(dx, dy, dc) for dx in (0, 1) for dy in (0, 1) for dc in (0, 1)][1:]
        return [pltpu.make_async_remote_copy(
            src_ref=srcs[0], dst_ref=lnds[0].at[me], send_sem=send_sems.at[k], recv_sem=recv_sems.at[k],
            device_id=(1 - x if dx else x, 1 - y if dy else y, 1 - c if dc else c), device_id_type=MESH)
            for k, (dx, dy, dc) in enumerate(rel)]

    return _split_copy_calls("all_gather_small", [sm], [lax.empty((8, r, w), sm.dtype)], 7, make_copies)


def _sum_devices(sm, gathered, me_idx):
    def body(me_ref, sm_ref, g_ref, o_ref):
        own = sm_ref[...]
        acc = jnp.where(me_ref[0] == 0, own, g_ref[0])
        for d in range(1, 8):
            acc = acc + jnp.where(me_ref[0] == d, own, g_ref[d])
        o_ref[...] = acc

    vm = pl.BlockSpec(memory_space=pltpu.VMEM)
    return pl.pallas_call(
        body, name="sum_devices", in_specs=[pl.BlockSpec(memory_space=pltpu.SMEM), vm, vm], out_specs=vm,
        out_shape=jax.ShapeDtypeStruct(sm.shape, F32),
    )(me_idx, sm, gathered)


class _LocalWeights:
    def __init__(self, w):
        self.w = w
        self.g = {}

    def begin(self):
        return ()

    def first(self, after):
        del after
        return self.w

    def rest(self, after):
        del after
        return self.w

    def mlp(self, after):
        del after
        return self.w

    def grads(self, tag, g):
        del tag
        self.g.update(g)
        return ()

    def poll(self, after):
        del after
        return ()


def _local_step(x3, mem3, pos2, target3, small, comm):
    bsz, seq, d = x3.shape
    mlen = mem3.shape[1]
    t = bsz * seq
    tok = comm.begin()
    x = x3.reshape(t, d)
    mem = mem3.reshape(bsz * mlen, d)
    target = target3.reshape(t, d)
    pos = pos2.reshape(t, 1)
    qg_t = jnp.tile(small["sw_q_norm_g"], (1, SW_HEADS))
    kg_t = jnp.tile(small["sw_k_norm_g"], (1, SW_KV_HEADS))

    hn1 = _rms_fwd(x, small["norm1_g"], name="rms1_fwd", after=tok)
    w = comm.first(hn1)
    proj_hg = _mm(hn1, w["w_in_hg"], NN, t, HG_COLS, d, name="proj_hg", tk=d, after=(w.get("token"),))[0]
    proj_sw = _mm(hn1, w["w_in_sw"], NN, t, SW_COLS, d, name="proj_sw", tk=d)[0]
    y_mix, o_hg, states = _hg_fwd(proj_hg, small["hg_lower_bounds"], small["hg_norm_g"], bsz, seq, y_width=1024)
    y_mix = _sw_fwd(proj_sw, pos, qg_t, kg_t, small["sw_sinks"], y_mix, bsz, seq)
    w_in_hg, w_in_sw = w["w_in_hg"], w["w_in_sw"]
    w = comm.rest(y_mix)
    h1, hn2 = _mm(y_mix, w["w_out"], NN, t, d, 1024, name="out_proj", tk=1024, extras=(x,), rows=(small["norm2_g"],),
                  epilogue=_residual_rms, out_dtypes=(F32, _MXU_DTYPE), after=(w.get("token"),))
    mn = _rms_fwd(mem, small["mem_norm_g"], name="rms_mem_fwd")
    qx = _mm(hn2, w["wq"], NN, t, 512, d, name="xa_q", tk=d)[0]
    kvx = _mm(mn, w["wkv"], NN, bsz * mlen, 1024, d, name="xa_kv", tk=d)[0]
    ox = _xa_fwd(qx, kvx, small["xa_q_norm_g"], small["xa_k_norm_g"], bsz, seq, mlen)
    h2, hn3 = _mm(ox, w["wo"], NN, t, d, 512, name="xa_o", tk=512, extras=(h1,), rows=(small["norm3_g"],),
                  epilogue=_residual_rms, out_dtypes=(F32, _MXU_DTYPE))
    w = {**w, **comm.mlp(hn3)}
    ff = w["down"].shape[0]
    ffs = ff // 4

    def relu_sq(acc):
        a = jnp.maximum(acc, 0.0)
        return a, a * a

    act, act2 = _mm(hn3, w["up"], NN, t, ff, d, name="mlp_up", tm=2048, tn=ffs, tk=d,
                    b_spec=pl.BlockSpec((None, d, ffs), lambda i, j, kk: (j, 0, 0)),
                    epilogue=relu_sq, out_dtypes=(_MXU_DTYPE, _MXU_DTYPE))
    inv_d = 1.0 / d

    def loss_cotangent(acc, res, tgt):
        diff = acc + res - tgt
        v = diff * inv_d
        return v, v, jnp.sum(diff * diff, axis=0, keepdims=True)

    dy, dy_mx, sq_row = _mm(act2, w["down"], NN, t, d, ff, name="mlp_down", extras=(h2, target),
                            epilogue=loss_cotangent, out_dtypes=(F32, _MXU_DTYPE), row_sums=1)
    loss_row = _loss_finish(sq_row, d)

    dz = _mm(dy_mx, w["down"], NT, t, ff, d, name="d_act", tm=2048, tk=d, extras=(act,),
             epilogue=lambda acc, a: (acc * (2.0 * a.astype(F32)),), out_dtypes=(_MXU_DTYPE,))[0]
    g_down = _mm(act2, dy_mx, TN, ff, d, t, name="g_down", tk=2048)[0]
    g_up = _mm(hn3, dz, TN, d, ff, t, name="g_up", tn=ffs, tk=2048,
               out_shape=(jax.ShapeDtypeStruct((4, d, ffs), F32),),
               out_spec=(pl.BlockSpec((None, min(1024, d), ffs), lambda i, j, kk: (j, i, 0)),))[0]
    tok = comm.grads("mlp", dict(up=g_up, down=g_down))
    dh2, dh2_mx, g_norm3 = _mm(dz, w["up"], NT, t, d, ff, name="d_hn3", tk=ffs, after=tok,
                               b_spec=pl.BlockSpec((None, min(1024, d), ffs), lambda i, j, kk: (kk, j, 0)),
                               extras=(h2, dy), rows=(small["norm3_g"],), epilogue=_rms_bwd_residual,
                               out_dtypes=(F32, _MXU_DTYPE), row_sums=1)
    d_ox = _mm(dh2_mx, w["wo"], NT, t, 512, d, name="d_ox", tk=d)[0]
    g_wo = _mm(ox, dh2_mx, TN, 512, d, t, name="g_wo")[0]
    d_qx, d_kvx, g_xq, g_xk = _xa_bwd(qx, kvx, small["xa_q_norm_g"], small["xa_k_norm_g"], d_ox, bsz, seq, mlen)
    g_wq = _mm(hn2, d_qx, TN, d, 512, t, name="g_wq")[0]
    g_wkv = _mm(mn, d_kvx, TN, d, 1024, bsz * mlen, name="g_wkv")[0]
    dh1, dh1_mx, g_norm2 = _mm(d_qx, w["wq"], NT, t, d, 512, name="d_hn2", tk=512, extras=(h1, dh2),
                               rows=(small["norm2_g"],), epilogue=_rms_bwd_residual, out_dtypes=(F32, _MXU_DTYPE),
                               row_sums=1)
    dmn = _mm(d_kvx, w["wkv"], NT, bsz * mlen, d, 1024, name="d_mn", tk=1024)[0]
    g_memn = _rms_gain_grad(mem, small["mem_norm_g"], dmn, name="rms_mem_bwd")
    g_wout = _mm(y_mix, dh1_mx, TN, 1024, d, t, name="g_wout")[0]
    tok = comm.grads("mid", dict(w_out=g_wout, wq=g_wq, wkv=g_wkv, wo=g_wo))
    d_mix = _mm(dh1_mx, w["w_out"], NT, t, 1024, d, name="d_mix", tk=d, after=tok)[0]
    dproj_sw, g_swq, g_swk, g_sinks = _sw_bwd(proj_sw, pos, qg_t, kg_t, small["sw_sinks"], y_mix, d_mix, bsz, seq)
    tok = comm.poll(dproj_sw)
    dproj_hg, g_lb, g_hgn = _hg_bwd(proj_hg, small["hg_lower_bounds"], small["hg_norm_g"], o_hg, states, d_mix, bsz, seq,
                                    after=tok)
    g_in_hg = _mm(hn1, dproj_hg, TN, d, HG_COLS, t, name="g_in_hg", tk=2048)[0]
    g_in_sw = _mm(hn1, dproj_sw, TN, d, SW_COLS, t, name="g_in_sw")[0]
    tok = comm.grads("in", dict(w_in_hg=g_in_hg, w_in_sw=g_in_sw))
    dhn1_a = _mm(dproj_hg, w_in_hg, NT, t, d, HG_COLS, name="d_hn1_hg", tk=1024, after=tok)[0]
    grad_x, g_norm1 = _mm(dproj_sw, w_in_sw, NT, t, d, SW_COLS, name="d_hn1_sw", tk=SW_COLS, extras=(dhn1_a, x, dh1),
                          rows=(small["norm1_g"],), row_sums=1,
                          epilogue=lambda acc, prev, xv, dres, g: _rms_bwd_residual(acc + prev, xv, dres, g)[1:])

    g_small = dict(norm1_g=g_norm1, hg_lower_bounds=g_lb, hg_norm_g=g_hgn, sw_q_norm_g=g_swq, sw_k_norm_g=g_swk,
                   sw_sinks=g_sinks[:, 0:SW_HEADS], norm2_g=g_norm2, mem_norm_g=g_memn, xa_q_norm_g=g_xq,
                   xa_k_norm_g=g_xk, norm3_g=g_norm3)
    return loss_row, grad_x.reshape(bsz, seq, d), g_small


SMALL_NAMES = ("norm1_g", "hg_lower_bounds", "hg_norm_g", "sw_q_norm_g", "sw_k_norm_g", "sw_sinks", "norm2_g",
               "mem_norm_g", "xa_q_norm_g", "xa_k_norm_g", "norm3_g")
BIG_NAMES = ("w_in", "w_out", "xa_wq", "xa_wkv", "xa_wo", "mlp_up", "mlp_down")
WEIGHT_ORDER = ("norm1_g", "w_in", "hg_lower_bounds", "hg_norm_g", "sw_q_norm_g", "sw_k_norm_g", "sw_sinks", "w_out",
                "norm2_g", "mem_norm_g", "xa_wq", "xa_wkv", "xa_q_norm_g", "xa_k_norm_g", "xa_wo", "norm3_g",
                "mlp_up", "mlp_down")


def _pack_rows(vals, width):
    starts, at = [], 0
    for v in vals:
        starts.append(at)
        at += v.shape[0]
    total = at + (-at) % 8
    out = None
    for v, s in zip(vals, starts):
        placed = jnp.pad(v, ((s, total - s - v.shape[0]), (0, width - v.shape[1])))
        out = placed if out is None else out + placed
    return out, starts


class _MeshWeights:
    LATE = ("w_out", "xa_wq", "xa_wkv", "xa_wo", "mlp_up", "mlp_down")

    def __init__(self, shards, d, ff):
        self.shards, self.d, self.ff = shards, d, ff
        self.c_idx = lax.axis_index("c").astype(jnp.int32).reshape(1)
        chip = (2 * lax.axis_index("x") + lax.axis_index("y")).astype(jnp.int32)
        self.place_idx = jnp.stack([chip, lax.axis_index("c").astype(jnp.int32)])
        self.pending = []
        self.exchanging = None

    def begin(self):
        shard = self.shards["w_in"]
        start, self.in_wait = _gather_chips_split(
            "gather_in", [shard], [_place_shard(shard, self.place_idx, name="place_w_in")])
        self.in_state = start()
        tok = (self.in_state["token"],)
        self.placed = [_place_shard(self.shards[n], self.place_idx, name="place_" + n, after=tok) for n in self.LATE]
        return tok

    def first(self, after):
        _, lands = self.in_wait(self.in_state, (after, *self.placed))
        (g_in,) = _gather_finish(lands, "gather_in_finish")
        start, self.late_wait = _gather_chips_split("gather_late", [self.shards[n] for n in self.LATE], self.placed)
        self.late_state = start(after=(g_in,))
        ws = g_in.shape[2]
        cut = HG_COLS - 2 * ws
        return dict(w_in_hg=jnp.concatenate([g_in[0], g_in[1], g_in[2][:, :cut]], axis=1),
                    w_in_sw=jnp.concatenate([g_in[2][:, cut:], g_in[3]], axis=1), token=self.late_state["token"])

    def rest(self, after):
        _, lands = self.late_wait(self.late_state, (after,))
        g_out, g_q, g_kv, g_o = _gather_finish(lands[:4], "gather_late_finish")
        start, self.mlp_wait = _gather_forward_split("gather_mlp_forward", lands[4:])
        self.mlp_state = start(after=(g_out,))
        d = self.d
        return dict(w_out=g_out.reshape(-1, d), wq=g_q.reshape(d, -1), wkv=g_kv.reshape(d, -1),
                    wo=jnp.concatenate([g_o[k] for k in range(4)], axis=1), token=self.mlp_state["token"])

    def mlp(self, after):
        _, (g_up, g_dn) = self.mlp_wait(self.mlp_state, (after,))
        return dict(up=g_up, down=g_dn.reshape(self.ff, self.d))

    def _scatter(self, tag, names, arrays, recv):
        parts = [_add_halves(g, r, self.c_idx, name="rs_add_halves_" + n) for n, g, r in zip(names, arrays, recv)]
        start, wait = _scatter_chips_split("rs_scatter_" + tag, parts)
        state = start()
        self.pending.append((names, wait, state))
        return state["token"]

    def _advance(self, after):
        if self.exchanging is None:
            return ()
        tag, names, wait, state = self.exchanging
        self.exchanging = None
        arrays, recv = wait(state, (after,))
        return (self._scatter(tag, names, arrays, recv),)

    def poll(self, after):
        return self._advance(after)

    def grads(self, tag, g):
        d, ff = self.d, self.ff
        if tag == "mlp":
            names, arrays = ("mlp_up", "mlp_down"), [g["up"], g["down"].reshape(4, ff // 4, d)]
        elif tag == "mid":
            names = ("w_out", "xa_wq", "xa_wkv", "xa_wo")
            ds = d // 4
            g_wo = jnp.stack([g["wo"][:, ds * k:ds * (k + 1)] for k in range(4)])
            arrays = [g["w_out"].reshape(4, -1, d), g["wq"].reshape(4, d // 4, -1), g["wkv"].reshape(4, d // 4, -1), g_wo]
        else:
            hg, sw = g["w_in_hg"], g["w_in_sw"]
            ws = (hg.shape[1] + sw.shape[1]) // 4
            cut = hg.shape[1] - 2 * ws
            names = ("w_in",)
            arrays = [jnp.stack([hg[:, :ws], hg[:, ws:2 * ws], jnp.concatenate([hg[:, 2 * ws:], sw[:, :ws - cut]], axis=1),
                                 sw[:, ws - cut:]])]
        toks = self._advance(arrays[0])
        if tag == "in":
            return toks + (self._scatter(tag, names, arrays, _exchange_halves(arrays, "rs_exchange_" + tag)),)
        start, wait = _exchange_halves_split("rs_exchange_" + tag, arrays)
        state = start()
        self.exchanging = (tag, names, wait, state)
        return toks + (state["token"],)

    def finish(self, after):
        joins, tok = [], ()
        for names, wait, state in self.pending:
            srcs, lands = wait(state, after)
            fulls = [_add_chips(p, r, self.place_idx, name="rs_add_chips_" + n, after=tok)
                     for n, p, r in zip(names, srcs, lands)]
            start, jwait = _join_halves_split("rs_join_" + names[0], fulls)
            jstate = start()
            tok = (jstate["token"],)
            joins.append((names, jwait, jstate))
        out = {}
        for names, jwait, jstate in joins:
            _, fulls = jwait(jstate, tok)
            out.update(zip(names, fulls))
        return out


def kernel(x, mem, positions, norm1_g, w_in, hg_lower_bounds, hg_norm_g, sw_q_norm_g, sw_k_norm_g, sw_sinks, w_out, norm2_g, mem_norm_g, xa_wq, xa_wkv, xa_q_norm_g, xa_k_norm_g, xa_wo, norm3_g, mlp_up, mlp_down, loss_target, m_norm1_g, m_w_in, m_hg_lower_bounds, m_hg_norm_g, m_sw_q_norm_g, m_sw_k_norm_g, m_sw_sinks, m_w_out, m_norm2_g, m_mem_norm_g, m_xa_wq, m_xa_wkv, m_xa_q_norm_g, m_xa_k_norm_g, m_xa_wo, m_norm3_g, m_mlp_up, m_mlp_down, v_norm1_g, v_w_in, v_hg_lower_bounds, v_hg_norm_g, v_sw_q_norm_g, v_sw_k_norm_g, v_sw_sinks, v_w_out, v_norm2_g, v_mem_norm_g, v_xa_wq, v_xa_wkv, v_xa_q_norm_g, v_xa_k_norm_g, v_xa_wo, v_norm3_g, v_mlp_up, v_mlp_down):
    given = dict(locals())
    weights = {n: given[n] for n in WEIGHT_ORDER}
    moms = {n: given["m_" + n] for n in WEIGHT_ORDER}
    vars_ = {n: given["v_" + n] for n in WEIGHT_ORDER}
    d = x.shape[-1]
    ff = mlp_down.shape[1] * 4
    small = {n: weights[n] for n in SMALL_NAMES}

    comm = _MeshWeights({n: weights[n][0].astype(_MXU_DTYPE) for n in BIG_NAMES}, d, ff)
    loss_row, grad_x, g_small = _local_step(x, mem, positions, loss_target, small, comm)
    packed, starts = _pack_rows([g_small[n] for n in SMALL_NAMES] + [loss_row], 1024)
    start, wait = _all_gather_small_split(packed)
    state = start()
    big_grads = comm.finish((grad_x, state["token"]))
    (own,), (gathered,) = wait(state, (big_grads[BIG_NAMES[0]],))
    device = (4 * lax.axis_index("x") + 2 * lax.axis_index("y") + lax.axis_index("c")).astype(jnp.int32).reshape(1)
    summed = _sum_devices(own, gathered, device)
    small_grads = {}
    for n, s in zip(SMALL_NAMES, starts):
        r, c = weights[n].shape
        small_grads[n] = summed[s:s + r, 0:c]
    loss = summed[starts[-1], 0]

    grads, deltas, new_m, new_v = {}, {}, {}, {}
    for n in BIG_NAMES:
        shp = weights[n].shape
        g2, dl, mo, vo = _adamw_big(weights[n][0], big_grads[n], moms[n][0], vars_[n][0], name="adamw_" + n)
        grads[n], deltas[n], new_m[n], new_v[n] = (a.reshape(shp) for a in (g2, dl, mo, vo))
    sm_out = _adamw_small([weights[n] for n in SMALL_NAMES], [small_grads[n] for n in SMALL_NAMES],
                          [moms[n] for n in SMALL_NAMES], [vars_[n] for n in SMALL_NAMES])
    ns = len(SMALL_NAMES)
    for i, n in enumerate(SMALL_NAMES):
        grads[n], deltas[n], new_m[n], new_v[n] = small_grads[n], sm_out[i], sm_out[ns + i], sm_out[2 * ns + i]

    return (loss, grad_x, *[grads[n] for n in WEIGHT_ORDER], *[deltas[n] for n in WEIGHT_ORDER],
            *[new_m[n] for n in WEIGHT_ORDER], *[new_v[n] for n in WEIGHT_ORDER])
```

```python
import numpy as np
import jax
import jax.numpy as jnp
from jax import lax
from jax.experimental import pallas as pl
from jax.experimental.pallas import tpu as pltpu

F32 = jnp.float32
_MXU_DTYPE = jnp.bfloat16

EPS = 1e-6
HG_HEADS = 4
HG_D = 128
HG_CHUNK = 64
HG_TILE = 512
HG_LEVELS = (32, 16, 8, 4, 2, 1)
SW_HEADS = 8
SW_KV_HEADS = 2
SW_GROUP = SW_HEADS // SW_KV_HEADS
SW_HD = 64
SW_BLOCK = 128
ROPE_THETA = 500000.0
ROT_DIM = SW_HD // 4
XA_HEADS = 4
XA_HD = 128
HG_COLS = 4 * HG_HEADS * HG_D
SW_COLS = (SW_HEADS + 2 * SW_KV_HEADS) * SW_HD

ADAM_LR = 0.001
ADAM_B1 = 0.9
ADAM_B2 = 0.999
ADAM_EPS = 1e-08
ADAM_WD = 0.01
ADAM_STEP = 10

VMEM_LIMIT = 56 * 1024 * 1024
MESH = pl.DeviceIdType.MESH

NN = ((1,), (0,))
NT = ((1,), (1,))
TN = ((0,), (0,))


def _mx(v):
    return v.astype(_MXU_DTYPE)


def _dot(a, b, dims=NN):
    return lax.dot_general(_mx(a), _mx(b), (dims, ((), ())), preferred_element_type=F32)


def _split_dot(a, v, dims, parts):
    acc = None
    rest = v
    for p in range(parts):
        piece = _mx(rest)
        term = lax.dot_general(a, piece, (dims, ((), ())), preferred_element_type=F32)
        acc = term if acc is None else acc + term
        if p + 1 < parts:
            rest = rest - piece.astype(F32)
    return acc


def _params(sem):
    return pltpu.CompilerParams(dimension_semantics=sem, vmem_limit_bytes=VMEM_LIMIT)


def _mm(a, b, mode, m, n, k, *, name, tm=1024, tn=1024, tk=1024, a_spec=None, b_spec=None, extras=(), rows=(),
        epilogue=None, out_dtypes=(F32,), row_sums=0, out_shape=None, out_spec=None, after=()):
    after = tuple(t for t in after if t is not None)
    tm, tn, tk = min(tm, m), min(tn, n), min(tk, k)
    assert m % tm == 0 and n % tn == 0 and k % tk == 0, (name, m, n, k, tm, tn, tk)
    gi, gj, gk = m // tm, n // tn, k // tk
    assert row_sums == 0 or gj == 1, name
    if a_spec is None:
        a_spec = (pl.BlockSpec((tk, tm), lambda i, j, kk: (kk, i)) if mode == TN
                  else pl.BlockSpec((tm, tk), lambda i, j, kk: (i, kk)))
    if b_spec is None:
        b_spec = (pl.BlockSpec((tn, tk), lambda i, j, kk: (j, kk)) if mode == NT
                  else pl.BlockSpec((tk, tn), lambda i, j, kk: (kk, j)))
    mn_spec = pl.BlockSpec((tm, tn), lambda i, j, kk: (i, j))
    if epilogue is None:
        epilogue = lambda acc: (acc,)
    row_spec = pl.BlockSpec((1, tn), lambda i, j, kk: (0, j))
    n_ex, n_out = len(extras) + len(rows), len(out_dtypes)
    if out_shape is None:
        out_shape = tuple(jax.ShapeDtypeStruct((m, n), d) for d in out_dtypes)
        out_spec = tuple(mn_spec for _ in out_dtypes)
    out_shape = tuple(out_shape) + tuple(jax.ShapeDtypeStruct((1, n), F32) for _ in range(row_sums))
    out_spec = tuple(out_spec) + tuple(row_spec for _ in range(row_sums))

    n_after = len(after)

    def body(*refs):
        a_ref, b_ref = refs[0], refs[1]
        ex = refs[2:2 + n_ex]
        outs = refs[2 + n_ex + n_after:2 + n_ex + n_after + n_out + row_sums]
        first_row_tile = pl.program_id(0) == 0

        def finish(acc):
            res = epilogue(acc, *[e[...] for e in ex])
            for o, r in zip(outs[:n_out], res[:n_out]):
                o[...] = r.astype(o.dtype)
            if row_sums:
                @pl.when(first_row_tile)
                def _():
                    for o in outs[n_out:]:
                        o[...] = jnp.zeros_like(o)

                for o, r in zip(outs[n_out:], res[n_out:]):
                    o[...] += r

        if gk == 1:
            finish(_dot(a_ref[...], b_ref[...], mode))
        else:
            acc_ref = refs[-1]
            kk = pl.program_id(2)

            @pl.when(kk == 0)
            def _():
                acc_ref[...] = jnp.zeros_like(acc_ref)

            acc_ref[...] += _dot(a_ref[...], b_ref[...], mode)

            @pl.when(kk == gk - 1)
            def _():
                finish(acc_ref[...])

    return pl.pallas_call(
        body, name=name, grid=(gi, gj, gk),
        in_specs=([a_spec, b_spec] + [mn_spec] * len(extras) + [row_spec] * len(rows)
                  + [pl.BlockSpec(memory_space=pl.ANY)] * n_after),
        out_specs=out_spec, out_shape=out_shape,
        scratch_shapes=[pltpu.VMEM((tm, tn), F32)] if gk > 1 else [],
        compiler_params=_params(("arbitrary" if row_sums else "parallel", "parallel", "arbitrary")),
    )(a, b, *extras, *rows, *after)


def _rms_rows(xv, g):
    return xv * lax.rsqrt(jnp.mean(xv * xv, axis=1, keepdims=True) + EPS) * g


def _rms_rows_bwd(xv, g, dyv):
    r = lax.rsqrt(jnp.mean(xv * xv, axis=1, keepdims=True) + EPS)
    u = dyv * g
    return (r * u - xv * (r * r * r) * jnp.mean(u * xv, axis=1, keepdims=True),
            jnp.sum(dyv * xv * r, axis=0, keepdims=True))


def _residual_rms(acc, res, g):
    h = acc + res
    return h, _rms_rows(h, g)


def _rms_bwd_residual(dhn, xv, dres, g):
    dx, dg = _rms_rows_bwd(xv, g, dhn)
    dx = dx + dres
    return dx, dx, dg


def _rms_fwd(x, g, *, name, tm=512, after=()):
    t, d = x.shape
    tm = min(tm, t)
    after = tuple(a for a in after if a is not None)

    def body(x_ref, g_ref, *rest):
        rest[-1][...] = _rms_rows(x_ref[...], g_ref[...]).astype(rest[-1].dtype)

    return pl.pallas_call(
        body, name=name, grid=(t // tm,),
        in_specs=[pl.BlockSpec((tm, d), lambda i: (i, 0)), pl.BlockSpec((1, d), lambda i: (0, 0))]
        + [pl.BlockSpec(memory_space=pl.ANY)] * len(after),
        out_specs=pl.BlockSpec((tm, d), lambda i: (i, 0)),
        out_shape=jax.ShapeDtypeStruct((t, d), _MXU_DTYPE),
        compiler_params=_params(("parallel",)),
    )(x, g, *after)


def _rms_gain_grad(x, g, dy, *, name, tm=512):
    t, d = x.shape
    tm = min(tm, t)

    def body(x_ref, g_ref, dy_ref, dg_ref):
        @pl.when(pl.program_id(0) == 0)
        def _():
            dg_ref[...] = jnp.zeros_like(dg_ref)

        dg_ref[...] += _rms_rows_bwd(x_ref[...], g_ref[...], dy_ref[...])[1]

    row = pl.BlockSpec((tm, d), lambda i: (i, 0))
    vec = pl.BlockSpec((1, d), lambda i: (0, 0))
    return pl.pallas_call(
        body, name=name, grid=(t // tm,), in_specs=[row, vec, row], out_specs=vec,
        out_shape=jax.ShapeDtypeStruct((1, d), F32), compiler_params=_params(("arbitrary",)),
    )(x, g, dy)


def _hg_constants():
    c = HG_CHUNK
    t = np.arange(c)
    sums = [t[None, :] <= t[:, None]]
    masks = []
    for m in HG_LEVELS:
        base = (t // (2 * m)) * (2 * m)
        mid = base + m - 1
        second = (t - base) >= m
        upper = (t[None, :] > mid[:, None]) & (t[None, :] <= t[:, None])
        lower = (t[None, :] > t[:, None]) & (t[None, :] <= mid[:, None])
        sums.append(np.where(second[:, None], upper, lower))
        masks.append(second[:, None] & (~second)[None, :] & (base[:, None] == base[None, :]))
    return (np.concatenate(sums, axis=0).astype(np.float32), np.stack(masks).astype(np.float32))


HG_HEAD_LANES = tuple(slice(HG_D * h, HG_D * (h + 1)) for h in range(HG_HEADS))


def _per_head(fn, slab):
    return jnp.concatenate([jnp.broadcast_to(fn(slab[:, hs]), (slab.shape[0], HG_D)) for hs in HG_HEAD_LANES], axis=1)


def _lane_sum(v):
    return jnp.sum(v, axis=1, keepdims=True)


def _lane_mean(v):
    return jnp.mean(v, axis=1, keepdims=True)


def _hg_gates(blk, lbp):
    w = HG_HEADS * HG_D
    q, x, v, gl = blk[:, 0:w], blk[:, w:2 * w], blk[:, 2 * w:3 * w], blk[:, 3 * w:4 * w]
    mx = jnp.max(lbp, axis=0, keepdims=True)
    e = jnp.exp(lbp - mx)
    lb = e[0:1, :] / jnp.sum(e, axis=0, keepdims=True)
    sig = jax.nn.sigmoid(x)
    f = lb + (1.0 - lb) * sig
    return q, v, gl, lb, sig, f, 1.0 - f, jnp.log(f)


def _hg_fwd(proj, lbp, ng, bsz, seq, *, y_width):
    t = proj.shape[0]
    nc = seq // HG_CHUNK
    a_np, m_np = _hg_constants()
    a_all = jnp.asarray(a_np, _MXU_DTYPE)
    masks = jnp.asarray(m_np, F32)
    nl = len(HG_LEVELS)

    ts = min(HG_TILE, seq)
    ns, nct = seq // ts, ts // HG_CHUNK
    hw = HG_HEADS * HG_D

    def body(p_ref, lb_ref, ng_ref, a_ref, m_ref, y_ref, o_ref, st_ref, carry):
        a_mat = a_ref[...]
        ngv = ng_ref[...]

        @pl.when(pl.program_id(0) == 0)
        def _():
            carry[...] = jnp.zeros_like(carry)

        ng4 = _tile_lanes(ngv, HG_HEADS)
        heads = range(HG_HEADS)
        exs = range(bsz)
        hl = HG_HEAD_LANES
        lbp_v = lb_ref[...]

        def chunk(c, _):
            rows = pl.ds(pl.multiple_of(c * HG_CHUNK, HG_CHUNK), HG_CHUNK)
            gates = [_hg_gates(p_ref[e, rows, :], lbp_v) for e in exs]
            q, v, gl = [g[0] for g in gates], [g[1] for g in gates], [g[2] for g in gates]
            k = [g[6] for g in gates]
            sts = [[carry[e, h] for h in heads] for e in exs]
            e_all = [_split_dot(a_mat, gates[e][7], NN, 3) for e in exs]
            b = [e_all[e][0:HG_CHUNK] for e in exs]
            qb = [q[e] * jnp.exp(b[e]) for e in exs]
            o = [[_dot(qb[e][:, hl[h]], sts[e][h], NT) for h in heads] for e in exs]
            p = [[jnp.zeros((HG_CHUNK, HG_CHUNK), F32) for _ in heads] for _ in exs]
            for li in range(nl):
                dec = [jnp.exp(e_all[e][HG_CHUNK * (li + 1):HG_CHUNK * (li + 2)]) for e in exs]
                qm, km, mk = [q[e] * dec[e] for e in exs], [k[e] * dec[e] for e in exs], m_ref[li]
                p = [[p[e][h] + mk * _dot(qm[e][:, hl[h]], km[e][:, hl[h]], NT) for h in heads] for e in exs]
            bl = [b[e][HG_CHUNK - 1:HG_CHUNK, :] for e in exs]
            kd = [k[e] * jnp.exp(bl[e] - b[e]) for e in exs]
            pv = [[_dot(p[e][h], v[e][:, hl[h]]) for h in heads] for e in exs]
            upd = [[_dot(v[e][:, hl[h]], kd[e][:, hl[h]], TN) for h in heads] for e in exs]
            for e in exs:
                o_all = (jnp.concatenate([o[e][h] + pv[e][h] for h in heads], axis=1)
                         + _per_head(_lane_sum, q[e] * k[e]) * v[e])
                r = lax.rsqrt(_per_head(_lane_mean, o_all * o_all) + EPS)
                ebl = jnp.exp(bl[e])
                for h in heads:
                    st_ref[e, h, c] = sts[e][h]
                    carry[e, h] = sts[e][h] * ebl[:, hl[h]] + upd[e][h]
                o_ref[e, rows, :] = o_all
                y_ref[e, rows, :] = (o_all * r * ng4) * (gl[e] * jax.nn.sigmoid(gl[e]))
            return 0

        lax.fori_loop(0, nct, chunk, 0)

    y3, o3, states = pl.pallas_call(
        body, name="hgrn2_fwd", grid=(ns,),
        in_specs=[pl.BlockSpec((bsz, ts, HG_COLS), lambda s: (0, s, 0)),
                  pl.BlockSpec((2, hw), lambda s: (0, 0)),
                  pl.BlockSpec((1, HG_D), lambda s: (0, 0)),
                  pl.BlockSpec(a_all.shape, lambda s: (0, 0)),
                  pl.BlockSpec(masks.shape, lambda s: (0, 0, 0))],
        out_specs=(pl.BlockSpec((bsz, ts, hw), lambda s: (0, s, 0)),
                   pl.BlockSpec((bsz, ts, hw), lambda s: (0, s, 0)),
                   pl.BlockSpec((bsz, HG_HEADS, nct, HG_D, HG_D), lambda s: (0, 0, s, 0, 0))),
        out_shape=(jax.ShapeDtypeStruct((bsz, seq, y_width), F32),
                   jax.ShapeDtypeStruct((bsz, seq, hw), F32),
                   jax.ShapeDtypeStruct((bsz, HG_HEADS, nc, HG_D, HG_D), F32)),
        scratch_shapes=[pltpu.VMEM((bsz, HG_HEADS, HG_D, HG_D), F32)],
        compiler_params=_params(("arbitrary",)),
    )(proj.reshape(bsz, seq, HG_COLS), lbp, ng, a_all, masks)
    return y3.reshape(t, y_width), o3.reshape(t, hw), states


def _hg_bwd(proj, lbp, ng, o_all, states, dy, bsz, seq, after=()):
    after = tuple(a for a in after if a is not None)
    t = proj.shape[0]
    nc = seq // HG_CHUNK
    a_np, m_np = _hg_constants()
    a_all = jnp.asarray(a_np, _MXU_DTYPE)
    masks = jnp.asarray(m_np, F32)
    nl = len(HG_LEVELS)
    cs = HG_CHUNK

    ts = min(HG_TILE, seq)
    ns, nct = seq // ts, ts // cs
    hw = HG_HEADS * HG_D

    def body(p_ref, lb_ref, ng_ref, a_ref, m_ref, o_ref, st_ref, dy_ref, *rest):
        dp_ref, dlb_ref, dng_ref, dst_ref = rest[len(after):]
        a_mat = a_ref[...]
        ngv = ng_ref[...]
        ng4 = _tile_lanes(ngv, HG_HEADS)
        last_row = lax.broadcasted_iota(jnp.int32, (cs, hw), 0) == cs - 1
        first = pl.program_id(0) == 0
        heads = range(HG_HEADS)
        exs = range(bsz)
        hl = HG_HEAD_LANES
        lbp_v = lb_ref[...]

        @pl.when(first)
        def _():
            dst_ref[...] = jnp.zeros_like(dst_ref)

        def side_by_side(parts):
            return jnp.concatenate(parts, axis=1)

        def chunk(i, carry):
            dlb_acc, dng_acc = carry
            c = nct - 1 - i
            rows = pl.ds(pl.multiple_of(c * cs, cs), cs)
            gates = [_hg_gates(p_ref[e, rows, :], lbp_v) for e in exs]
            q, v, gl = [g[0] for g in gates], [g[1] for g in gates], [g[2] for g in gates]
            lb, sig, f, k = gates[0][3], [g[4] for g in gates], [g[5] for g in gates], [g[6] for g in gates]
            o = [o_ref[e, rows, :] for e in exs]
            dyv = [dy_ref[e, rows, :] for e in exs]
            sts = [[st_ref[e, h, c] for h in heads] for e in exs]
            dsts = [[dst_ref[e, h] for h in heads] for e in exs]
            e_all = [_split_dot(a_mat, gates[e][7], NN, 3) for e in exs]
            b = [e_all[e][0:cs] for e in exs]
            eb = [jnp.exp(b[e]) for e in exs]
            bl = [b[e][cs - 1:cs, :] for e in exs]
            ebl = [jnp.exp(bl[e]) for e in exs]
            ekd = [jnp.exp(bl[e] - b[e]) for e in exs]
            qb = [q[e] * eb[e] for e in exs]
            kd = [k[e] * ekd[e] for e in exs]
            do, dgl = [], []
            for e in exs:
                sg = jax.nn.sigmoid(gl[e])
                silu = gl[e] * sg
                r = lax.rsqrt(_per_head(_lane_mean, o[e] * o[e]) + EPS)
                dgl.append(dyv[e] * (o[e] * r * ng4) * (sg * (1.0 + gl[e] * (1.0 - sg))))
                u = dyv[e] * silu * ng4
                do.append(r * u - o[e] * (r * r * r) * _per_head(_lane_mean, u * o[e]))
                dng4 = jnp.sum(dyv[e] * silu * o[e] * r, axis=0, keepdims=True)
                dng_acc = dng_acc + ((dng4[:, hl[0]] + dng4[:, hl[1]]) + (dng4[:, hl[2]] + dng4[:, hl[3]]))
            es, qm, km = [], [], []
            p = [[jnp.zeros((cs, cs), F32) for _ in heads] for _ in exs]
            for li in range(nl):
                dec = [jnp.exp(e_all[e][cs * (li + 1):cs * (li + 2)]) for e in exs]
                es.append(dec)
                qm.append([q[e] * dec[e] for e in exs])
                km.append([k[e] * dec[e] for e in exs])
                mk = m_ref[li]
                p = [[p[e][h] + mk * _dot(qm[li][e][:, hl[h]], km[li][e][:, hl[h]], NT) for h in heads] for e in exs]
            dp = [[_dot(do[e][:, hl[h]], v[e][:, hl[h]], NT) for h in heads] for e in exs]
            dv_p = [[_dot(p[e][h], do[e][:, hl[h]], TN) for h in heads] for e in exs]
            dv_s = [[_dot(kd[e][:, hl[h]], dsts[e][h], NT) for h in heads] for e in exs]
            dqb = [side_by_side([_dot(do[e][:, hl[h]], sts[e][h]) for h in heads]) for e in exs]
            dkd = [side_by_side([_dot(v[e][:, hl[h]], dsts[e][h]) for h in heads]) for e in exs]
            new_dst = [[_dot(do[e][:, hl[h]], qb[e][:, hl[h]], TN) for h in heads] for e in exs]
            dv = [side_by_side([dv_p[e][h] + dv_s[e][h] for h in heads]) + _per_head(_lane_sum, q[e] * k[e]) * do[e]
                  for e in exs]
            dq = [dqb[e] * eb[e] for e in exs]
            dk = [dkd[e] * ekd[e] for e in exs]
            de = []
            for e in exs:
                dbl = (jnp.sum(dkd[e] * kd[e], axis=0, keepdims=True)
                       + side_by_side([jnp.sum(dsts[e][h] * sts[e][h], axis=0, keepdims=True) for h in heads]) * ebl[e])
                de.append([dqb[e] * qb[e] - dkd[e] * kd[e] + jnp.where(last_row, dbl, 0.0)])
            for li in range(nl):
                mk = m_ref[li]
                dpm = [[mk * dp[e][h] for h in heads] for e in exs]
                dqm = [side_by_side([_dot(dpm[e][h], km[li][e][:, hl[h]]) for h in heads]) for e in exs]
                dkm = [side_by_side([_dot(dpm[e][h], qm[li][e][:, hl[h]], TN) for h in heads]) for e in exs]
                for e in exs:
                    dq[e] = dq[e] + dqm[e] * es[li][e]
                    dk[e] = dk[e] + dkm[e] * es[li][e]
                    de[e].append(dqm[e] * qm[li][e] + dkm[e] * km[li][e])
            dg = [_split_dot(a_mat, jnp.concatenate(de[e], axis=0), TN, 2) for e in exs]
            for e in exs:
                dpd = _per_head(_lane_sum, do[e] * v[e])
                df = dg[e] / f[e] - (dk[e] + dpd * q[e])
                dp_ref[e, rows, 0:hw] = _mx(dq[e] + dpd * k[e])
                dp_ref[e, rows, hw:2 * hw] = _mx(df * (1.0 - lb) * sig[e] * (1.0 - sig[e]))
                dp_ref[e, rows, 2 * hw:3 * hw] = _mx(dv[e])
                dp_ref[e, rows, 3 * hw:4 * hw] = _mx(dgl[e])
                for h in heads:
                    dst_ref[e, h] = dsts[e][h] * ebl[e][:, hl[h]] + new_dst[e][h]
                dlb_acc = dlb_acc + jnp.sum(df * (1.0 - sig[e]), axis=0, keepdims=True)
            return dlb_acc, dng_acc

        dlb, dng = lax.fori_loop(0, nct, chunk, (jnp.zeros((1, hw), F32), jnp.zeros((1, HG_D), F32)))

        @pl.when(first)
        def _():
            dlb_ref[...] = jnp.zeros_like(dlb_ref)
            dng_ref[...] = jnp.zeros_like(dng_ref)

        mx = jnp.max(lbp_v, axis=0, keepdims=True)
        e = jnp.exp(lbp_v - mx)
        s0 = e[0:1, :] / jnp.sum(e, axis=0, keepdims=True)
        da0 = dlb * s0 * (1.0 - s0)
        dlb_ref[...] += jnp.concatenate([da0, -da0], axis=0)
        dng_ref[...] += dng

    rows3 = lambda w: pl.BlockSpec((bsz, ts, w), lambda s: (0, ns - 1 - s, 0))
    dproj, dlb, dng = pl.pallas_call(
        body, name="hgrn2_bwd", grid=(ns,),
        in_specs=[rows3(HG_COLS),
                  pl.BlockSpec((2, hw), lambda s: (0, 0)),
                  pl.BlockSpec((1, HG_D), lambda s: (0, 0)),
                  pl.BlockSpec(a_all.shape, lambda s: (0, 0)),
                  pl.BlockSpec(masks.shape, lambda s: (0, 0, 0)),
                  rows3(hw),
                  pl.BlockSpec((bsz, HG_HEADS, nct, HG_D, HG_D), lambda s: (0, 0, ns - 1 - s, 0, 0)),
                  rows3(hw)] + [pl.BlockSpec(memory_space=pl.ANY)] * len(after),
        out_specs=(rows3(HG_COLS),
                   pl.BlockSpec((2, hw), lambda s: (0, 0)),
                   pl.BlockSpec((1, HG_D), lambda s: (0, 0))),
        out_shape=(jax.ShapeDtypeStruct((bsz, seq, HG_COLS), _MXU_DTYPE),
                   jax.ShapeDtypeStruct((2, hw), F32),
                   jax.ShapeDtypeStruct((1, HG_D), F32)),
        scratch_shapes=[pltpu.VMEM((bsz, HG_HEADS, HG_D, HG_D), F32)],
        compiler_params=_params(("arbitrary",)),
    )(proj.reshape(bsz, seq, HG_COLS), lbp, ng, a_all, masks, o_all.reshape(bsz, seq, hw), states,
      dy.reshape(bsz, seq, dy.shape[1]), *after)
    return dproj.reshape(t, HG_COLS), dlb, dng


def _sw_constants():
    half = ROT_DIM // 2
    inv = (np.float32(ROPE_THETA) ** (-(np.arange(half, dtype=np.float32) * np.float32(2.0) / np.float32(ROT_DIM)))
           ).astype(np.float32)
    freq = np.zeros((1, 128), np.float32)
    sign = np.zeros((1, 128), np.float32)
    for h in range(2):
        freq[0, 64 * h:64 * h + half] = inv
        freq[0, 64 * h + half:64 * h + 2 * half] = inv
        sign[0, 64 * h:64 * h + half] = -1.0
        sign[0, 64 * h + half:64 * h + 2 * half] = 1.0
    seg = np.kron(np.eye(8, dtype=np.float32), np.full((64, 64), 1.0 / 64.0, np.float32))
    return freq, sign, seg


def _rope_tables(pos, freq, sign):
    ang = pos.astype(F32) * freq
    return jnp.cos(ang), jnp.sin(ang) * sign


def _tile_lanes(v, times):
    return v if times == 1 else jnp.concatenate([v] * times, axis=1)


def _swap_halves(v):
    w = v.shape[1]
    half = ROT_DIM // 2
    lane = lax.broadcasted_iota(jnp.int32, v.shape, 1) % SW_HD
    return jnp.where(lane < half, pltpu.roll(v, w - half, 1), jnp.where(lane < 2 * half, pltpu.roll(v, half, 1), 0.0))


def _sw_norm_rope(tv, gain, seg, cosv, sinv):
    w = tv.shape[1]
    ms = _split_dot_rhs(tv * tv, seg[0:w, 0:w])
    r = lax.rsqrt(ms + EPS)
    tn = tv * r * gain
    reps = w // 128
    return tn * _tile_lanes(cosv, reps) + _swap_halves(tn) * _tile_lanes(sinv, reps), r


def _split_dot_rhs(v, a):
    hi = _mx(v)
    lo = _mx(v - hi.astype(F32))
    return (lax.dot_general(hi, a, (NN, ((), ())), preferred_element_type=F32)
            + lax.dot_general(lo, a, (NN, ((), ())), preferred_element_type=F32))


def _sw_norm_rope_bwd(dt, tv, r, gain, seg, cosv, sinv):
    w = tv.shape[1]
    reps = w // 128
    dtn = dt * _tile_lanes(cosv, reps) + _swap_halves(dt * _tile_lanes(sinv, reps))
    u = dtn * gain
    dtv = r * u - tv * (r * r * r) * _split_dot_rhs(u * tv, seg[0:w, 0:w])
    return dtv, jnp.sum(dtn * tv * r, axis=0, keepdims=True)


def _sw_scores(qh, kp, kc):
    return _dot(qh, kp, NT), _dot(qh, kc, NT)


def _sw_probs(raw, sink, first_block):
    scale = SW_HD ** -0.5
    qi = lax.broadcasted_iota(jnp.int32, (SW_BLOCK, SW_BLOCK), 0)
    kj = lax.broadcasted_iota(jnp.int32, (SW_BLOCK, SW_BLOCK), 1)
    ok_prev = jnp.logical_and(kj > qi, jnp.logical_not(first_block))
    ok_cur = kj <= qi
    sp = jnp.where(ok_prev, raw[0] * scale, -jnp.inf)
    sc = jnp.where(ok_cur, raw[1] * scale, -jnp.inf)
    m = jnp.maximum(jnp.maximum(jnp.max(sp, axis=1, keepdims=True), jnp.max(sc, axis=1, keepdims=True)), sink)
    pp, pc = jnp.exp(sp - m), jnp.exp(sc - m)
    es = jnp.exp(sink - m)
    den = jnp.sum(pp, axis=1, keepdims=True) + jnp.sum(pc, axis=1, keepdims=True) + es
    return pp / den, pc / den, es / den


def _sw_specs(nb):
    def cur(b, n):
        return b * nb + jnp.minimum(n, nb - 1)

    def prev(b, n):
        return b * nb + jnp.maximum(jnp.minimum(n, nb - 1) - 1, 0)

    return cur, prev


def _sw_fwd(proj, pos, qg, kg, sinks, y_in, bsz, seq):
    t = proj.shape[0]
    nb = seq // SW_BLOCK
    freq_np, sign_np, seg_np = _sw_constants()
    freq, sign = jnp.asarray(freq_np), jnp.asarray(sign_np)
    seg = jnp.asarray(seg_np, _MXU_DTYPE)
    cur, prev = _sw_specs(nb)

    def body(q_ref, kc_ref, kp_ref, vc_ref, vp_ref, pc_ref, pp_ref, qg_ref, kg_ref, sk_ref, fr_ref, sn_ref, seg_ref,
             yin_ref, y_ref):
        del yin_ref
        n = pl.program_id(1)
        segv = seg_ref[...]
        cos_c, sin_c = _rope_tables(pc_ref[...], fr_ref[...], sn_ref[...])
        cos_p, sin_p = _rope_tables(pp_ref[...], fr_ref[...], sn_ref[...])
        qr, _ = _sw_norm_rope(q_ref[...], qg_ref[...], segv, cos_c, sin_c)
        kcr, _ = _sw_norm_rope(kc_ref[...], kg_ref[...], segv, cos_c, sin_c)
        kpr, _ = _sw_norm_rope(kp_ref[...], kg_ref[...], segv, cos_p, sin_p)
        vc, vp = vc_ref[...], vp_ref[...]
        ks = [slice(SW_HD * (h // SW_GROUP), SW_HD * (h // SW_GROUP + 1)) for h in range(SW_HEADS)]
        raw = [_sw_scores(qr[:, SW_HD * h:SW_HD * (h + 1)], kpr[:, ks[h]], kcr[:, ks[h]]) for h in range(SW_HEADS)]
        probs = [_sw_probs(raw[h], sk_ref[0, h], n == 0) for h in range(SW_HEADS)]
        for h in range(SW_HEADS):
            y_ref[:, SW_HD * h:SW_HD * (h + 1)] = _dot(probs[h][0], vp[:, ks[h]]) + _dot(probs[h][1], vc[:, ks[h]])

    rowq = pl.BlockSpec((SW_BLOCK, 512), lambda b, n: (cur(b, n), 0))
    full = lambda a: pl.BlockSpec(a.shape, lambda b, n: (0,) * a.ndim)
    yw = y_in.shape[1]
    return pl.pallas_call(
        body, name="swa_fwd", grid=(bsz, nb),
        in_specs=[rowq,
                  pl.BlockSpec((SW_BLOCK, 128), lambda b, n: (cur(b, n), 4)),
                  pl.BlockSpec((SW_BLOCK, 128), lambda b, n: (prev(b, n), 4)),
                  pl.BlockSpec((SW_BLOCK, 128), lambda b, n: (cur(b, n), 5)),
                  pl.BlockSpec((SW_BLOCK, 128), lambda b, n: (prev(b, n), 5)),
                  pl.BlockSpec((SW_BLOCK, 1), lambda b, n: (cur(b, n), 0)),
                  pl.BlockSpec((SW_BLOCK, 1), lambda b, n: (prev(b, n), 0)),
                  full(qg), full(kg),
                  pl.BlockSpec(memory_space=pltpu.SMEM),
                  full(freq), full(sign), full(seg),
                  pl.BlockSpec(memory_space=pl.ANY)],
        out_specs=pl.BlockSpec((SW_BLOCK, 512), lambda b, n: (cur(b, n), 1)),
        out_shape=jax.ShapeDtypeStruct((t, yw), F32),
        input_output_aliases={13: 0},
        compiler_params=_params(("parallel", "parallel")),
    )(proj, proj, proj, proj, proj, pos, pos, qg, kg, sinks, freq, sign, seg, y_in)


def _sw_bwd(proj, pos, qg, kg, sinks, y, dy, bsz, seq):
    t = proj.shape[0]
    nb = seq // SW_BLOCK
    freq_np, sign_np, seg_np = _sw_constants()
    freq, sign = jnp.asarray(freq_np), jnp.asarray(sign_np)
    seg = jnp.asarray(seg_np, _MXU_DTYPE)
    cur, prev = _sw_specs(nb)
    scale = SW_HD ** -0.5

    def body(q_ref, kc_ref, kp_ref, vc_ref, vp_ref, pc_ref, pp_ref, qg_ref, kg_ref, sk_ref, fr_ref, sn_ref, seg_ref,
             y_ref, dy_ref, dp_ref, dqg_ref, dkg_ref, dsk_ref,
             dq_car, dkv_car, dqr_s, dkc_s, dkp_s, dvc_s, dvp_s, gq_acc, gk_acc, sk_acc):
        b, n = pl.program_id(0), pl.program_id(1)
        first = jnp.logical_and(b == 0, n == 0)
        last = jnp.logical_and(b == pl.num_programs(0) - 1, n == nb)

        @pl.when(first)
        def _():
            gq_acc[...] = jnp.zeros_like(gq_acc)
            gk_acc[...] = jnp.zeros_like(gk_acc)
            sk_acc[...] = jnp.zeros_like(sk_acc)

        @pl.when(n < nb)
        def _():
            segv = seg_ref[...]
            cos_c, sin_c = _rope_tables(pc_ref[...], fr_ref[...], sn_ref[...])
            cos_p, sin_p = _rope_tables(pp_ref[...], fr_ref[...], sn_ref[...])
            qv, kcv, kpv = q_ref[...], kc_ref[...], kp_ref[...]
            qr, rq = _sw_norm_rope(qv, qg_ref[...], segv, cos_c, sin_c)
            kcr, rkc = _sw_norm_rope(kcv, kg_ref[...], segv, cos_c, sin_c)
            kpr, rkp = _sw_norm_rope(kpv, kg_ref[...], segv, cos_p, sin_p)
            vc, vp = vc_ref[...], vp_ref[...]
            lane = lax.broadcasted_iota(jnp.int32, (1, 128), 1)
            dsk = jnp.zeros((1, 128), F32)
            heads = range(SW_HEADS)
            ks = [slice(SW_HD * (h // SW_GROUP), SW_HD * (h // SW_GROUP + 1)) for h in heads]
            hs = [slice(SW_HD * h, SW_HD * (h + 1)) for h in heads]
            qh = [qr[:, hs[h]] for h in heads]
            doh = [dy_ref[:, hs[h]] for h in heads]
            raw = [_sw_scores(qh[h], kpr[:, ks[h]], kcr[:, ks[h]]) for h in heads]
            dpp = [_dot(doh[h], vp[:, ks[h]], NT) for h in heads]
            dpc = [_dot(doh[h], vc[:, ks[h]], NT) for h in heads]
            probs = [_sw_probs(raw[h], sk_ref[0, h], n == 0) for h in heads]
            dsp, dsc = [], []
            for h in heads:
                pp, pc, ps = probs[h]
                delta = jnp.sum(doh[h] * y_ref[:, hs[h]], axis=1, keepdims=True)
                dsp.append(pp * (dpp[h] - delta) * scale)
                dsc.append(pc * (dpc[h] - delta) * scale)
                dsk = dsk + jnp.where(lane == h, -jnp.sum(ps * delta), 0.0)
            for h in heads:
                dqr_s[:, hs[h]] = _dot(dsp[h], kpr[:, ks[h]]) + _dot(dsc[h], kcr[:, ks[h]])
            for kv in range(SW_KV_HEADS):
                group = range(SW_GROUP * kv, SW_GROUP * (kv + 1))
                kvs = slice(SW_HD * kv, SW_HD * (kv + 1))
                dvp_s[:, kvs] = sum(_dot(probs[h][0], doh[h], TN) for h in group)
                dvc_s[:, kvs] = sum(_dot(probs[h][1], doh[h], TN) for h in group)
                dkp_s[:, kvs] = sum(_dot(dsp[h], qh[h], TN) for h in group)
                dkc_s[:, kvs] = sum(_dot(dsc[h], qh[h], TN) for h in group)
            dq, gq = _sw_norm_rope_bwd(dqr_s[...], qv, rq, qg_ref[...], segv, cos_c, sin_c)
            dkc, gkc = _sw_norm_rope_bwd(dkc_s[...], kcv, rkc, kg_ref[...], segv, cos_c, sin_c)
            dkp, gkp = _sw_norm_rope_bwd(dkp_s[...], kpv, rkp, kg_ref[...], segv, cos_p, sin_p)
            gq_acc[...] += gq
            gk_acc[...] += gkc + gkp
            sk_acc[...] += dsk

            @pl.when(n > 0)
            def _():
                dp_ref[:, 0:512] = _mx(dq_car[...])
                dp_ref[:, 512:640] = _mx(dkv_car[:, 0:128] + dkp)
                dp_ref[:, 640:768] = _mx(dkv_car[:, 128:256] + dvp_s[...])

            dq_car[...] = dq
            dkv_car[:, 0:128] = dkc
            dkv_car[:, 128:256] = dvc_s[...]

        @pl.when(n == nb)
        def _():
            dp_ref[:, 0:512] = _mx(dq_car[...])
            dp_ref[:, 512:768] = _mx(dkv_car[...])

        @pl.when(last)
        def _():
            gq = gq_acc[...]
            acc = gq[:, 0:SW_HD]
            for h in range(1, SW_HEADS):
                acc = acc + gq[:, SW_HD * h:SW_HD * (h + 1)]
            dqg_ref[...] = acc
            gk = gk_acc[...]
            dkg_ref[...] = gk[:, 0:SW_HD] + gk[:, SW_HD:2 * SW_HD]
            dsk_ref[...] = sk_acc[...]

    rowq = pl.BlockSpec((SW_BLOCK, 512), lambda b, n: (cur(b, n), 0))
    full = lambda a: pl.BlockSpec(a.shape, lambda b, n: (0,) * a.ndim)

    def out_row(b, n):
        return b * nb + jnp.maximum(n - 1, 0)

    return pl.pallas_call(
        body, name="swa_bwd", grid=(bsz, nb + 1),
        in_specs=[rowq,
                  pl.BlockSpec((SW_BLOCK, 128), lambda b, n: (cur(b, n), 4)),
                  pl.BlockSpec((SW_BLOCK, 128), lambda b, n: (prev(b, n), 4)),
                  pl.BlockSpec((SW_BLOCK, 128), lambda b, n: (cur(b, n), 5)),
                  pl.BlockSpec((SW_BLOCK, 128), lambda b, n: (prev(b, n), 5)),
                  pl.BlockSpec((SW_BLOCK, 1), lambda b, n: (cur(b, n), 0)),
                  pl.BlockSpec((SW_BLOCK, 1), lambda b, n: (prev(b, n), 0)),
                  full(qg), full(kg),
                  pl.BlockSpec(memory_space=pltpu.SMEM),
                  full(freq), full(sign), full(seg),
                  pl.BlockSpec((SW_BLOCK, 512), lambda b, n: (cur(b, n), 1)),
                  pl.BlockSpec((SW_BLOCK, 512), lambda b, n: (cur(b, n), 1))],
        out_specs=(pl.BlockSpec((SW_BLOCK, SW_COLS), lambda b, n: (out_row(b, n), 0)),
                   pl.BlockSpec((1, SW_HD), lambda b, n: (0, 0)),
                   pl.BlockSpec((1, SW_HD), lambda b, n: (0, 0)),
                   pl.BlockSpec((1, 128), lambda b, n: (0, 0))),
        out_shape=(jax.ShapeDtypeStruct((t, SW_COLS), _MXU_DTYPE),
                   jax.ShapeDtypeStruct((1, SW_HD), F32),
                   jax.ShapeDtypeStruct((1, SW_HD), F32),
                   jax.ShapeDtypeStruct((1, 128), F32)),
        scratch_shapes=[pltpu.VMEM((SW_BLOCK, 512), F32), pltpu.VMEM((SW_BLOCK, 256), F32),
                        pltpu.VMEM((SW_BLOCK, 512), F32),
                        pltpu.VMEM((SW_BLOCK, 128), F32), pltpu.VMEM((SW_BLOCK, 128), F32),
                        pltpu.VMEM((SW_BLOCK, 128), F32), pltpu.VMEM((SW_BLOCK, 128), F32),
                        pltpu.VMEM((1, 512), F32), pltpu.VMEM((1, 128), F32), pltpu.VMEM((1, 128), F32)],
        compiler_params=_params(("arbitrary", "arbitrary")),
    )(proj, proj, proj, proj, proj, pos, pos, qg, kg, sinks, freq, sign, seg, y, dy)


def _head_rms(tv, gain):
    r = lax.rsqrt(jnp.mean(tv * tv, axis=1, keepdims=True) + EPS)
    return tv * r * gain, r


def _head_rms_bwd(dtn, tv, r, gain):
    u = dtn * gain
    return r * u - tv * (r * r * r) * jnp.mean(u * tv, axis=1, keepdims=True), jnp.sum(dtn * tv * r, axis=0, keepdims=True)


def _xa_softmax(raw):
    s = raw * (XA_HD ** -0.5)
    e = jnp.exp(s - jnp.max(s, axis=1, keepdims=True))
    return e / jnp.sum(e, axis=1, keepdims=True)


def _xa_fwd(qx, kvx, qg, kg, bsz, seq, mlen, *, tq=512):
    t = qx.shape[0]
    tq = min(tq, seq)
    nq = seq // tq
    w = XA_HEADS * XA_HD

    def body(q_ref, kv_ref, qg_ref, kg_ref, o_ref):
        heads = range(XA_HEADS)
        hs = [slice(XA_HD * h, XA_HD * (h + 1)) for h in heads]
        qn = [_head_rms(q_ref[:, hs[h]], qg_ref[...])[0] for h in heads]
        kn = [_head_rms(kv_ref[:, hs[h]], kg_ref[...])[0] for h in heads]
        raw = [_dot(qn[h], kn[h], NT) for h in heads]
        p = [_xa_softmax(raw[h]) for h in heads]
        for h in heads:
            o_ref[:, hs[h]] = _dot(p[h], kv_ref[:, w + XA_HD * h:w + XA_HD * (h + 1)]).astype(o_ref.dtype)

    vec = pl.BlockSpec((1, XA_HD), lambda b, i: (0, 0))
    return pl.pallas_call(
        body, name="xattn_fwd", grid=(bsz, nq),
        in_specs=[pl.BlockSpec((tq, w), lambda b, i: (b * nq + i, 0)),
                  pl.BlockSpec((mlen, 2 * w), lambda b, i: (b, 0)), vec, vec],
        out_specs=pl.BlockSpec((tq, w), lambda b, i: (b * nq + i, 0)),
        out_shape=jax.ShapeDtypeStruct((t, w), _MXU_DTYPE),
        compiler_params=_params(("parallel", "parallel")),
    )(qx, kvx, qg, kg)


def _xa_bwd(qx, kvx, qg, kg, do, bsz, seq, mlen, *, tq=512):
    t = qx.shape[0]
    tq = min(tq, seq)
    nq = seq // tq
    w = XA_HEADS * XA_HD
    scale = XA_HD ** -0.5

    def body(q_ref, kv_ref, qg_ref, kg_ref, do_ref, dq_ref, dkv_ref, dqg_ref, dkg_ref):
        b, i = pl.program_id(0), pl.program_id(1)

        @pl.when(jnp.logical_and(b == 0, i == 0))
        def _():
            dqg_ref[...] = jnp.zeros_like(dqg_ref)
            dkg_ref[...] = jnp.zeros_like(dkg_ref)

        @pl.when(i == 0)
        def _():
            dkv_ref[...] = jnp.zeros_like(dkv_ref)

        heads = range(XA_HEADS)
        hs = [slice(XA_HD * h, XA_HD * (h + 1)) for h in heads]
        vs = [slice(w + XA_HD * h, w + XA_HD * (h + 1)) for h in heads]
        qv = [q_ref[:, hs[h]] for h in heads]
        kv = [kv_ref[:, hs[h]] for h in heads]
        doh = [do_ref[:, hs[h]] for h in heads]
        qn = [_head_rms(qv[h], qg_ref[...]) for h in heads]
        kn = [_head_rms(kv[h], kg_ref[...]) for h in heads]
        raw = [_dot(qn[h][0], kn[h][0], NT) for h in heads]
        dp = [_dot(doh[h], kv_ref[:, vs[h]], NT) for h in heads]
        p = [_xa_softmax(raw[h]) for h in heads]
        ds = [p[h] * (dp[h] - jnp.sum(p[h] * dp[h], axis=1, keepdims=True)) * scale for h in heads]
        dqn = [_dot(ds[h], kn[h][0]) for h in heads]
        dkn = [_dot(ds[h], qn[h][0], TN) for h in heads]
        dvv = [_dot(p[h], doh[h], TN) for h in heads]
        gq_sum = jnp.zeros((1, XA_HD), F32)
        gk_sum = jnp.zeros((1, XA_HD), F32)
        for h in heads:
            dqv, gq = _head_rms_bwd(dqn[h], qv[h], qn[h][1], qg_ref[...])
            dkv, gk = _head_rms_bwd(dkn[h], kv[h], kn[h][1], kg_ref[...])
            dq_ref[:, hs[h]] = dqv.astype(dq_ref.dtype)
            dkv_ref[:, hs[h]] += dkv
            dkv_ref[:, vs[h]] += dvv[h]
            gq_sum = gq_sum + gq
            gk_sum = gk_sum + gk
        dqg_ref[...] += gq_sum
        dkg_ref[...] += gk_sum

    vec = pl.BlockSpec((1, XA_HD), lambda b, i: (0, 0))
    row = pl.BlockSpec((tq, w), lambda b, i: (b * nq + i, 0))
    mem = pl.BlockSpec((mlen, 2 * w), lambda b, i: (b, 0))
    return pl.pallas_call(
        body, name="xattn_bwd", grid=(bsz, nq),
        in_specs=[row, mem, vec, vec, row],
        out_specs=(row, mem, vec, vec),
        out_shape=(jax.ShapeDtypeStruct((t, w), _MXU_DTYPE), jax.ShapeDtypeStruct((bsz * mlen, 2 * w), F32),
                   jax.ShapeDtypeStruct((1, XA_HD), F32), jax.ShapeDtypeStruct((1, XA_HD), F32)),
        compiler_params=_params(("arbitrary", "arbitrary")),
    )(qx, kvx, qg, kg, do)


def _loss_finish(sq_row, d_model):
    def body(s_ref, o_ref):
        o_ref[...] = jnp.zeros_like(o_ref) + 0.5 * jnp.sum(s_ref[...]) / float(d_model)

    return pl.pallas_call(body, name="loss_finish", out_shape=jax.ShapeDtypeStruct((1, 128), F32))(sq_row)


def _adamw_math(w, g, m, v):
    m = ADAM_B1 * m + (1.0 - ADAM_B1) * g
    v = ADAM_B2 * v + (1.0 - ADAM_B2) * (g * g)
    m_hat = m / (1.0 - ADAM_B1 ** ADAM_STEP)
    v_hat = v / (1.0 - ADAM_B2 ** ADAM_STEP)
    return -ADAM_LR * (m_hat / (jnp.sqrt(v_hat) + ADAM_EPS) + ADAM_WD * w), m, v


def _adamw_big(w, g, m, v, *, name, tr=512):
    r, c = w.shape
    tr = min(tr, r)

    def body(w_ref, g_ref, m_ref, v_ref, go_ref, d_ref, mo_ref, vo_ref):
        gv = g_ref[...]
        d, mn, vn = _adamw_math(w_ref[...], gv, m_ref[...], v_ref[...])
        go_ref[...] = gv
        d_ref[...] = d
        mo_ref[...] = mn
        vo_ref[...] = vn

    spec = pl.BlockSpec((tr, c), lambda i: (i, 0))
    shp = jax.ShapeDtypeStruct((r, c), F32)
    return pl.pallas_call(
        body, name=name, grid=(r // tr,), in_specs=[spec] * 4, out_specs=(spec,) * 4, out_shape=(shp,) * 4,
        compiler_params=_params(("parallel",)),
    )(w, g, m, v)


def _adamw_small(ws, gs, ms, vs):
    n = len(ws)

    def body(*refs):
        for i in range(n):
            d, mn, vn = _adamw_math(refs[i][...], refs[n + i][...], refs[2 * n + i][...], refs[3 * n + i][...])
            refs[4 * n + i][...] = d
            refs[5 * n + i][...] = mn
            refs[6 * n + i][...] = vn

    shapes = tuple(jax.ShapeDtypeStruct(w.shape, F32) for w in ws)
    return pl.pallas_call(body, name="adamw_small", out_shape=shapes * 3)(*ws, *gs, *ms, *vs)


def _add_halves(g, recv, c_idx, *, name, tr=512):
    _, r, c = g.shape
    h = r // 2
    tr = min(tr, h)
    nt = h // tr

    def body(c_ref, g_ref, r_ref, o_ref):
        del c_ref
        o_ref[...] = g_ref[...] + r_ref[...]

    return pl.pallas_call(
        body, name=name,
        grid_spec=pltpu.PrefetchScalarGridSpec(
            num_scalar_prefetch=1, grid=(4, nt),
            in_specs=[pl.BlockSpec((None, tr, c), lambda k, i, cr: (k, cr[0] * nt + i, 0)),
                      pl.BlockSpec((None, tr, c), lambda k, i, cr: (k, i, 0))],
            out_specs=pl.BlockSpec((None, tr, c), lambda k, i, cr: (k, i, 0))),
        out_shape=jax.ShapeDtypeStruct((4, h, c), F32),
        compiler_params=_params(("parallel", "parallel")),
    )(c_idx, g, recv)


def _add_chips(p, recv, place_idx, *, name, tr=512, after=()):
    _, h, c = p.shape
    tr = min(tr, h)
    nt = h // tr

    def body(pi_ref, p_ref, r_ref, *rest):
        del pi_ref
        rest[-1][...] = ((p_ref[...] + r_ref[0]) + r_ref[1]) + r_ref[2]

    return pl.pallas_call(
        body, name=name,
        grid_spec=pltpu.PrefetchScalarGridSpec(
            num_scalar_prefetch=1, grid=(nt,),
            in_specs=[pl.BlockSpec((None, tr, c), lambda i, pi: (pi[0], i, 0)),
                      pl.BlockSpec((3, tr, c), lambda i, pi: (0, i, 0))] + [pl.BlockSpec(memory_space=pl.ANY)] * len(after),
            out_specs=pl.BlockSpec((tr, c), lambda i, pi: (pi[1] * nt + i, 0))),
        out_shape=jax.ShapeDtypeStruct((2 * h, c), F32),
        compiler_params=_params(("parallel",)),
    )(place_idx, p, recv, *after)


def _place_shard(shard, place_idx, *, name, tr=512, after=()):
    r, c = shard.shape
    tr = min(tr, r)

    def body(pi_ref, s_ref, *rest):
        del pi_ref
        rest[-1][...] = s_ref[...]

    return pl.pallas_call(
        body, name=name,
        grid_spec=pltpu.PrefetchScalarGridSpec(
            num_scalar_prefetch=1, grid=(r // tr,),
            in_specs=[pl.BlockSpec((tr, c), lambda i, pi: (i, 0))] + [pl.BlockSpec(memory_space=pl.ANY)] * len(after),
            out_specs=pl.BlockSpec((None, tr, c), lambda i, pi: (pi[0], i, 0))),
        out_shape=jax.ShapeDtypeStruct((4, r, c), shard.dtype),
        compiler_params=_params(("parallel",)),
    )(place_idx, shard, *after)


def _place():
    x, y, c = lax.axis_index("x"), lax.axis_index("y"), lax.axis_index("c")
    chips = [(1 - x, y), (x, 1 - y), (1 - x, 1 - y)]
    return x, y, c, chips


ANY = pl.BlockSpec(memory_space=pl.ANY)


def _exchange_halves(grads, name):
    n = len(grads)

    def body(*refs):
        ins, outs = refs[:n], refs[n:2 * n]
        send_sems, recv_sems = refs[2 * n:]
        x, y, c, _ = _place()

        def copy(a):
            h = ins[a].shape[1] // 2
            return pltpu.make_async_remote_copy(
                src_ref=ins[a].at[:, pl.ds((1 - c) * h, h), :], dst_ref=outs[a],
                send_sem=send_sems.at[a], recv_sem=recv_sems.at[a], device_id=(x, y, 1 - c), device_id_type=MESH)

        for a in range(n):
            copy(a).start()
        for a in range(n):
            copy(a).wait_recv()
        for a in range(n):
            copy(a).wait_send()

    return pl.pallas_call(
        body, name=name,
        in_specs=[ANY] * n, out_specs=tuple([ANY] * n),
        out_shape=tuple(jax.ShapeDtypeStruct((4, g.shape[1] // 2, g.shape[2]), g.dtype) for g in grads),
        scratch_shapes=[pltpu.SemaphoreType.DMA((n,)), pltpu.SemaphoreType.DMA((n,))],
    )(*grads)


HBM = pl.BlockSpec(memory_space=pltpu.HBM)
SEM = pl.BlockSpec(memory_space=pltpu.SEMAPHORE)
EFFECT = pltpu.SideEffectType.DATAFLOW_SIDE_EFFECTING


def _in_hbm(a):
    return pltpu.with_memory_space_constraint(a, pltpu.HBM)


def _split_copy_calls(name, srcs, lands, n_copies, make_copies):
    ns, nl = len(srcs), len(lands)
    nb = ns + nl

    def start(after=()):
        n_after = len(after)

        def body(*refs):
            outs = refs[nb + n_after:]
            copies = make_copies(refs[:ns], refs[ns:nb], outs[0], outs[1])
            for cp in copies:
                cp.start()
            token = refs[-1]
            token[...] = jnp.zeros_like(token)

        bufs = [_in_hbm(a) for a in list(srcs) + list(lands)]
        out = pl.pallas_call(
            body, name=name + "_start",
            out_shape=(pltpu.SemaphoreType.DMA((n_copies,)), pltpu.SemaphoreType.DMA((n_copies,)),
                       *[pltpu.HBM(a.shape, a.dtype) for a in bufs], jax.ShapeDtypeStruct((8, 128), F32)),
            in_specs=[HBM] * nb + [pl.BlockSpec(memory_space=pl.ANY)] * n_after,
            out_specs=(SEM, SEM, *[HBM] * nb, pl.BlockSpec(memory_space=pltpu.VMEM)),
            input_output_aliases={i: 2 + i for i in range(nb)},
            compiler_params=pltpu.CompilerParams(has_side_effects=EFFECT),
        )(*bufs, *after)
        return dict(send=out[0], recv=out[1], bufs=list(out[2:2 + nb]), token=out[-1])

    def wait(state, after):
        def body(*refs):
            copies = make_copies(refs[:ns], refs[ns:nb], refs[nb], refs[nb + 1])
            for cp in copies:
                cp.wait_send()
            for cp in copies:
                cp.wait_recv()

        bufs = state["bufs"]
        out = pl.pallas_call(
            body, name=name + "_wait",
            out_shape=tuple(pltpu.HBM(a.shape, a.dtype) for a in bufs),
            in_specs=[HBM] * nb + [SEM, SEM] + [pl.BlockSpec(memory_space=pl.ANY)] * len(after),
            out_specs=tuple([HBM] * nb),
            input_output_aliases={i: i for i in range(nb)},
            compiler_params=pltpu.CompilerParams(has_side_effects=EFFECT),
        )(*bufs, state["send"], state["recv"], *after)
        return list(out[:ns]), list(out[ns:])

    return start, wait


def _scatter_chips_split(name, parts):
    n = len(parts)
    lands = [lax.empty((3,) + p.shape[1:], p.dtype) for p in parts]

    def make_copies(srcs, lnds, send_sems, recv_sems):
        _, _, c, chips = _place()
        return [pltpu.make_async_remote_copy(
            src_ref=srcs[a].at[2 * px + py], dst_ref=lnds[a].at[j], send_sem=send_sems.at[a * 3 + j],
            recv_sem=recv_sems.at[a * 3 + j], device_id=(px, py, c), device_id_type=MESH)
            for a in range(n) for j, (px, py) in enumerate(chips)]

    return _split_copy_calls(name, parts, lands, 3 * n, make_copies)


def _exchange_halves_split(name, grads):
    n = len(grads)
    lands = [lax.empty((4, g.shape[1] // 2, g.shape[2]), g.dtype) for g in grads]

    def make_copies(srcs, lnds, send_sems, recv_sems):
        x, y, c, _ = _place()
        out = []
        for a in range(n):
            h = srcs[a].shape[1] // 2
            out.append(pltpu.make_async_remote_copy(
                src_ref=srcs[a].at[:, pl.ds((1 - c) * h, h), :], dst_ref=lnds[a], send_sem=send_sems.at[a],
                recv_sem=recv_sems.at[a], device_id=(x, y, 1 - c), device_id_type=MESH))
        return out

    return _split_copy_calls(name, grads, lands, n, make_copies)


def _gather_chips_split(name, shards, lands):
    n = len(shards)

    def make_copies(srcs, lnds, send_sems, recv_sems):
        x, y, c, chips = _place()
        out = []
        for a in range(n):
            h = srcs[a].shape[0] // 2
            for j, (px, py) in enumerate(chips):
                out.append(pltpu.make_async_remote_copy(
                    src_ref=srcs[a].at[pl.ds(c * h, h), :], dst_ref=lnds[a].at[2 * x + y, pl.ds(c * h, h), :],
                    send_sem=send_sems.at[a * 3 + j], recv_sem=recv_sems.at[a * 3 + j],
                    device_id=(px, py, c), device_id_type=MESH))
        return out

    return _split_copy_calls(name, shards, lands, 3 * n, make_copies)


def _gather_finish(gathered, name):
    n = len(gathered)

    def body(*refs):
        outs = refs[n:2 * n]
        send_sems, recv_sems = refs[2 * n:]
        x, y, c, chips = _place()

        def copy(a, j, chip_idx, which):
            h = outs[a].shape[1] // 2
            rows = outs[a].at[chip_idx, pl.ds(which * h, h), :]
            return pltpu.make_async_remote_copy(
                src_ref=rows, dst_ref=rows, send_sem=send_sems.at[a * 3 + j], recv_sem=recv_sems.at[a * 3 + j],
                device_id=(x, y, 1 - c), device_id_type=MESH)

        for a in range(n):
            for j, (px, py) in enumerate(chips):
                copy(a, j, 2 * px + py, c).start()
        for a in range(n):
            for j, (px, py) in enumerate(chips):
                copy(a, j, 2 * px + py, 1 - c).wait_recv()
        for a in range(n):
            for j, (px, py) in enumerate(chips):
                copy(a, j, 2 * px + py, c).wait_send()

    return pl.pallas_call(
        body, name=name,
        in_specs=[ANY] * n, out_specs=tuple([ANY] * n),
        out_shape=tuple(jax.ShapeDtypeStruct(g.shape, g.dtype) for g in gathered),
        input_output_aliases={i: i for i in range(n)},
        scratch_shapes=[pltpu.SemaphoreType.DMA((3 * n,)), pltpu.SemaphoreType.DMA((3 * n,))],
    )(*gathered)


def _gather_forward_split(name, gathered):
    n = len(gathered)

    def make_copies(srcs, lnds, send_sems, recv_sems):
        x, y, c, chips = _place()
        out = []
        for a in range(n):
            h = lnds[a].shape[1] // 2
            for j, (px, py) in enumerate(chips):
                rows = lnds[a].at[2 * px + py, pl.ds(c * h, h), :]
                out.append(pltpu.make_async_remote_copy(
                    src_ref=rows, dst_ref=rows, send_sem=send_sems.at[a * 3 + j], recv_sem=recv_sems.at[a * 3 + j],
                    device_id=(x, y, 1 - c), device_id_type=MESH))
        return out

    return _split_copy_calls(name, [], gathered, 3 * n, make_copies)


def _join_halves_split(name, fulls):
    n = len(fulls)

    def make_copies(srcs, lnds, send_sems, recv_sems):
        x, y, c, _ = _place()
        out = []
        for a in range(n):
            h = lnds[a].shape[0] // 2
            rows = lnds[a].at[pl.ds(c * h, h), :]
            out.append(pltpu.make_async_remote_copy(
                src_ref=rows, dst_ref=rows, send_sem=send_sems.at[a], recv_sem=recv_sems.at[a],
                device_id=(x, y, 1 - c), device_id_type=MESH))
        return out

    return _split_copy_calls(name, [], fulls, n, make_copies)


def _all_gather_small_split(sm):
    r, w = sm.shape

    def make_copies(srcs, lnds, send_sems, recv_sems):
        x, y, c, _ = _place()
        me = 4 * x + 2 * y + c
        rel = [(dx, dy, dc) for dx in (0, 1) for dy in (0, 1) for dc in (0, 1)][1:]
        return [pltpu.make_async_remote_copy(
            src_ref=srcs[0], dst_ref=lnds[0].at[me], send_sem=send_sems.at[k], recv_sem=recv_sems.at[k],
            device_id=(1 - x if dx else x, 1 - y if dy else y, 1 - c if dc else c), device_id_type=MESH)
            for k, (dx, dy, dc) in enumerate(rel)]

    return _split_copy_calls("all_gather_small", [sm], [lax.empty((8, r, w), sm.dtype)], 7, make_copies)


def _sum_devices(sm, gathered, me_idx):
    def body(me_ref, sm_ref, g_ref, o_ref):
        own = sm_ref[...]
        acc = jnp.where(me_ref[0] == 0, own, g_ref[0])
        for d in range(1, 8):
            acc = acc + jnp.where(me_ref[0] == d, own, g_ref[d])
        o_ref[...] = acc

    vm = pl.BlockSpec(memory_space=pltpu.VMEM)
    return pl.pallas_call(
        body, name="sum_devices", in_specs=[pl.BlockSpec(memory_space=pltpu.SMEM), vm, vm], out_specs=vm,
        out_shape=jax.ShapeDtypeStruct(sm.shape, F32),
    )(me_idx, sm, gathered)


class _LocalWeights:
    def __init__(self, w):
        self.w = w
        self.g = {}

    def begin(self):
        return ()

    def first(self, after):
        del after
        return self.w

    def rest(self, after):
        del after
        return self.w

    def mlp(self, after):
        del after
        return self.w

    def grads(self, tag, g):
        del tag
        self.g.update(g)
        return ()

    def poll(self, after):
        del after
        return ()


def _local_step(x3, mem3, pos2, target3, small, comm):
    bsz, seq, d = x3.shape
    mlen = mem3.shape[1]
    t = bsz * seq
    tok = comm.begin()
    x = x3.reshape(t, d)
    mem = mem3.reshape(bsz * mlen, d)
    target = target3.reshape(t, d)
    pos = pos2.reshape(t, 1)
    qg_t = jnp.tile(small["sw_q_norm_g"], (1, SW_HEADS))
    kg_t = jnp.tile(small["sw_k_norm_g"], (1, SW_KV_HEADS))

    hn1 = _rms_fwd(x, small["norm1_g"], name="rms1_fwd", after=tok)
    w = comm.first(hn1)
    proj_hg = _mm(hn1, w["w_in_hg"], NN, t, HG_COLS, d, name="proj_hg", tk=d, after=(w.get("token"),))[0]
    proj_sw = _mm(hn1, w["w_in_sw"], NN, t, SW_COLS, d, name="proj_sw", tk=d)[0]
    y_mix, o_hg, states = _hg_fwd(proj_hg, small["hg_lower_bounds"], small["hg_norm_g"], bsz, seq, y_width=1024)
    y_mix = _sw_fwd(proj_sw, pos, qg_t, kg_t, small["sw_sinks"], y_mix, bsz, seq)
    w_in_hg, w_in_sw = w["w_in_hg"], w["w_in_sw"]
    w = comm.rest(y_mix)
    h1, hn2 = _mm(y_mix, w["w_out"], NN, t, d, 1024, name="out_proj", tk=1024, extras=(x,), rows=(small["norm2_g"],),
                  epilogue=_residual_rms, out_dtypes=(F32, _MXU_DTYPE), after=(w.get("token"),))
    mn = _rms_fwd(mem, small["mem_norm_g"], name="rms_mem_fwd")
    qx = _mm(hn2, w["wq"], NN, t, 512, d, name="xa_q", tk=d)[0]
    kvx = _mm(mn, w["wkv"], NN, bsz * mlen, 1024, d, name="xa_kv", tk=d)[0]
    ox = _xa_fwd(qx, kvx, small["xa_q_norm_g"], small["xa_k_norm_g"], bsz, seq, mlen)
    h2, hn3 = _mm(ox, w["wo"], NN, t, d, 512, name="xa_o", tk=512, extras=(h1,), rows=(small["norm3_g"],),
                  epilogue=_residual_rms, out_dtypes=(F32, _MXU_DTYPE))
    w = {**w, **comm.mlp(hn3)}
    ff = w["down"].shape[0]
    ffs = ff // 4

    def relu_sq(acc):
        a = jnp.maximum(acc, 0.0)
        return a, a * a

    act, act2 = _mm(hn3, w["up"], NN, t, ff, d, name="mlp_up", tm=2048, tn=ffs, tk=d,
                    b_spec=pl.BlockSpec((None, d, ffs), lambda i, j, kk: (j, 0, 0)),
                    epilogue=relu_sq, out_dtypes=(_MXU_DTYPE, _MXU_DTYPE))
    inv_d = 1.0 / d

    def loss_cotangent(acc, res, tgt):
        diff = acc + res - tgt
        v = diff * inv_d
        return v, v, jnp.sum(diff * diff, axis=0, keepdims=True)

    dy, dy_mx, sq_row = _mm(act2, w["down"], NN, t, d, ff, name="mlp_down", tk=2048, extras=(h2, target),
                            epilogue=loss_cotangent, out_dtypes=(F32, _MXU_DTYPE), row_sums=1)
    loss_row = _loss_finish(sq_row, d)

    dz = _mm(dy_mx, w["down"], NT, t, ff, d, name="d_act", tm=2048, tk=d, extras=(act,),
             epilogue=lambda acc, a: (acc * (2.0 * a.astype(F32)),), out_dtypes=(_MXU_DTYPE,))[0]
    g_down = _mm(act2, dy_mx, TN, ff, d, t, name="g_down", tk=t)[0]
    g_up = _mm(hn3, dz, TN, d, ff, t, name="g_up", tn=ffs, tk=t,
               out_shape=(jax.ShapeDtypeStruct((4, d, ffs), F32),),
               out_spec=(pl.BlockSpec((None, min(1024, d), ffs), lambda i, j, kk: (j, i, 0)),))[0]
    tok = comm.grads("mlp", dict(up=g_up, down=g_down))
    dh2, dh2_mx, g_norm3 = _mm(dz, w["up"], NT, t, d, ff, name="d_hn3", tk=ffs, after=tok,
                               b_spec=pl.BlockSpec((None, min(1024, d), ffs), lambda i, j, kk: (kk, j, 0)),
                               extras=(h2, dy), rows=(small["norm3_g"],), epilogue=_rms_bwd_residual,
                               out_dtypes=(F32, _MXU_DTYPE), row_sums=1)
    d_ox = _mm(dh2_mx, w["wo"], NT, t, 512, d, name="d_ox", tk=d)[0]
    g_wo = _mm(ox, dh2_mx, TN, 512, d, t, name="g_wo", tk=t)[0]
    d_qx, d_kvx, g_xq, g_xk = _xa_bwd(qx, kvx, small["xa_q_norm_g"], small["xa_k_norm_g"], d_ox, bsz, seq, mlen)
    g_wq = _mm(hn2, d_qx, TN, d, 512, t, name="g_wq", tk=t)[0]
    g_wkv = _mm(mn, d_kvx, TN, d, 1024, bsz * mlen, name="g_wkv")[0]
    dh1, dh1_mx, g_norm2 = _mm(d_qx, w["wq"], NT, t, d, 512, name="d_hn2", tk=512, extras=(h1, dh2),
                               rows=(small["norm2_g"],), epilogue=_rms_bwd_residual, out_dtypes=(F32, _MXU_DTYPE),
                               row_sums=1)
    dmn = _mm(d_kvx, w["wkv"], NT, bsz * mlen, d, 1024, name="d_mn", tk=1024)[0]
    g_memn = _rms_gain_grad(mem, small["mem_norm_g"], dmn, name="rms_mem_bwd")
    g_wout = _mm(y_mix, dh1_mx, TN, 1024, d, t, name="g_wout", tk=2048)[0]
    tok = comm.grads("mid", dict(w_out=g_wout, wq=g_wq, wkv=g_wkv, wo=g_wo))
    d_mix = _mm(dh1_mx, w["w_out"], NT, t, 1024, d, name="d_mix", tk=d, after=tok)[0]
    dproj_sw, g_swq, g_swk, g_sinks = _sw_bwd(proj_sw, pos, qg_t, kg_t, small["sw_sinks"], y_mix, d_mix, bsz, seq)
    tok = comm.poll(dproj_sw)
    dproj_hg, g_lb, g_hgn = _hg_bwd(proj_hg, small["hg_lower_bounds"], small["hg_norm_g"], o_hg, states, d_mix, bsz, seq,
                                    after=tok)
    g_in_hg = _mm(hn1, dproj_hg, TN, d, HG_COLS, t, name="g_in_hg", tk=t)[0]
    g_in_sw = _mm(hn1, dproj_sw, TN, d, SW_COLS, t, name="g_in_sw", tk=t)[0]
    tok = comm.grads("in", dict(w_in_hg=g_in_hg, w_in_sw=g_in_sw))
    dhn1_a = _mm(dproj_hg, w_in_hg, NT, t, d, HG_COLS, name="d_hn1_hg", tk=HG_COLS, after=tok)[0]
    grad_x, g_norm1 = _mm(dproj_sw, w_in_sw, NT, t, d, SW_COLS, name="d_hn1_sw", tk=SW_COLS, extras=(dhn1_a, x, dh1),
                          rows=(small["norm1_g"],), row_sums=1,
                          epilogue=lambda acc, prev, xv, dres, g: _rms_bwd_residual(acc + prev, xv, dres, g)[1:])

    g_small = dict(norm1_g=g_norm1, hg_lower_bounds=g_lb, hg_norm_g=g_hgn, sw_q_norm_g=g_swq, sw_k_norm_g=g_swk,
                   sw_sinks=g_sinks[:, 0:SW_HEADS], norm2_g=g_norm2, mem_norm_g=g_memn, xa_q_norm_g=g_xq,
                   xa_k_norm_g=g_xk, norm3_g=g_norm3)
    return loss_row, grad_x.reshape(bsz, seq, d), g_small


SMALL_NAMES = ("norm1_g", "hg_lower_bounds", "hg_norm_g", "sw_q_norm_g", "sw_k_norm_g", "sw_sinks", "norm2_g",
               "mem_norm_g", "xa_q_norm_g", "xa_k_norm_g", "norm3_g")
BIG_NAMES = ("w_in", "w_out", "xa_wq", "xa_wkv", "xa_wo", "mlp_up", "mlp_down")
WEIGHT_ORDER = ("norm1_g", "w_in", "hg_lower_bounds", "hg_norm_g", "sw_q_norm_g", "sw_k_norm_g", "sw_sinks", "w_out",
                "norm2_g", "mem_norm_g", "xa_wq", "xa_wkv", "xa_q_norm_g", "xa_k_norm_g", "xa_wo", "norm3_g",
                "mlp_up", "mlp_down")


def _pack_rows(vals, width):
    starts, at = [], 0
    for v in vals:
        starts.append(at)
        at += v.shape[0]
    total = at + (-at) % 8
    out = None
    for v, s in zip(vals, starts):
        placed = jnp.pad(v, ((s, total - s - v.shape[0]), (0, width - v.shape[1])))
        out = placed if out is None else out + placed
    return out, starts


class _MeshWeights:
    LATE = ("w_out", "xa_wq", "xa_wkv", "xa_wo", "mlp_up", "mlp_down")

    def __init__(self, shards, d, ff):
        self.shards, self.d, self.ff = shards, d, ff
        self.c_idx = lax.axis_index("c").astype(jnp.int32).reshape(1)
        chip = (2 * lax.axis_index("x") + lax.axis_index("y")).astype(jnp.int32)
        self.place_idx = jnp.stack([chip, lax.axis_index("c").astype(jnp.int32)])
        self.pending = []
        self.exchanging = None

    def begin(self):
        shard = self.shards["w_in"]
        start, self.in_wait = _gather_chips_split(
            "gather_in", [shard], [_place_shard(shard, self.place_idx, name="place_w_in")])
        self.in_state = start()
        tok = (self.in_state["token"],)
        self.placed = [_place_shard(self.shards[n], self.place_idx, name="place_" + n, after=tok) for n in self.LATE]
        return tok

    def first(self, after):
        _, lands = self.in_wait(self.in_state, (after, *self.placed))
        (g_in,) = _gather_finish(lands, "gather_in_finish")
        start, self.late_wait = _gather_chips_split("gather_late", [self.shards[n] for n in self.LATE], self.placed)
        self.late_state = start(after=(g_in,))
        ws = g_in.shape[2]
        cut = HG_COLS - 2 * ws
        return dict(w_in_hg=jnp.concatenate([g_in[0], g_in[1], g_in[2][:, :cut]], axis=1),
                    w_in_sw=jnp.concatenate([g_in[2][:, cut:], g_in[3]], axis=1), token=self.late_state["token"])

    def rest(self, after):
        _, lands = self.late_wait(self.late_state, (after,))
        g_out, g_q, g_kv, g_o = _gather_finish(lands[:4], "gather_late_finish")
        start, self.mlp_wait = _gather_forward_split("gather_mlp_forward", lands[4:])
        self.mlp_state = start(after=(g_out,))
        d = self.d
        return dict(w_out=g_out.reshape(-1, d), wq=g_q.reshape(d, -1), wkv=g_kv.reshape(d, -1),
                    wo=jnp.concatenate([g_o[k] for k in range(4)], axis=1), token=self.mlp_state["token"])

    def mlp(self, after):
        _, (g_up, g_dn) = self.mlp_wait(self.mlp_state, (after,))
        return dict(up=g_up, down=g_dn.reshape(self.ff, self.d))

    def _scatter(self, tag, names, arrays, recv):
        parts = [_add_halves(g, r, self.c_idx, name="rs_add_halves_" + n) for n, g, r in zip(names, arrays, recv)]
        start, wait = _scatter_chips_split("rs_scatter_" + tag, parts)
        state = start()
        self.pending.append((names, wait, state))
        return state["token"]

    def _advance(self, after):
        if self.exchanging is None:
            return ()
        tag, names, wait, state = self.exchanging
        self.exchanging = None
        arrays, recv = wait(state, (after,))
        return (self._scatter(tag, names, arrays, recv),)

    def poll(self, after):
        return self._advance(after)

    def grads(self, tag, g):
        d, ff = self.d, self.ff
        if tag == "mlp":
            names, arrays = ("mlp_up", "mlp_down"), [g["up"], g["down"].reshape(4, ff // 4, d)]
        elif tag == "mid":
            names = ("w_out", "xa_wq", "xa_wkv", "xa_wo")
            ds = d // 4
            g_wo = jnp.stack([g["wo"][:, ds * k:ds * (k + 1)] for k in range(4)])
            arrays = [g["w_out"].reshape(4, -1, d), g["wq"].reshape(4, d // 4, -1), g["wkv"].reshape(4, d // 4, -1), g_wo]
        else:
            hg, sw = g["w_in_hg"], g["w_in_sw"]
            ws = (hg.shape[1] + sw.shape[1]) // 4
            cut = hg.shape[1] - 2 * ws
            names = ("w_in",)
            arrays = [jnp.stack([hg[:, :ws], hg[:, ws:2 * ws], jnp.concatenate([hg[:, 2 * ws:], sw[:, :ws - cut]], axis=1),
                                 sw[:, ws - cut:]])]
        toks = self._advance(arrays[0])
        if tag == "in":
            return toks + (self._scatter(tag, names, arrays, _exchange_halves(arrays, "rs_exchange_" + tag)),)
        start, wait = _exchange_halves_split("rs_exchange_" + tag, arrays)
        state = start()
        self.exchanging = (tag, names, wait, state)
        return toks + (state["token"],)

    def finish(self, after):
        joins, tok = [], ()
        for names, wait, state in self.pending:
            srcs, lands = wait(state, after)
            fulls = [_add_chips(p, r, self.place_idx, name="rs_add_chips_" + n, after=tok)
                     for n, p, r in zip(names, srcs, lands)]
            start, jwait = _join_halves_split("rs_join_" + names[0], fulls)
            jstate = start()
            tok = (jstate["token"],)
            joins.append((names, jwait, jstate))
        out = {}
        for names, jwait, jstate in joins:
            _, fulls = jwait(jstate, tok)
            out.update(zip(names, fulls))
        return out


def kernel(x, mem, positions, norm1_g, w_in, hg_lower_bounds, hg_norm_g, sw_q_norm_g, sw_k_norm_g, sw_sinks, w_out, norm2_g, mem_norm_g, xa_wq, xa_wkv, xa_q_norm_g, xa_k_norm_g, xa_wo, norm3_g, mlp_up, mlp_down, loss_target, m_norm1_g, m_w_in, m_hg_lower_bounds, m_hg_norm_g, m_sw_q_norm_g, m_sw_k_norm_g, m_sw_sinks, m_w_out, m_norm2_g, m_mem_norm_g, m_xa_wq, m_xa_wkv, m_xa_q_norm_g, m_xa_k_norm_g, m_xa_wo, m_norm3_g, m_mlp_up, m_mlp_down, v_norm1_g, v_w_in, v_hg_lower_bounds, v_hg_norm_g, v_sw_q_norm_g, v_sw_k_norm_g, v_sw_sinks, v_w_out, v_norm2_g, v_mem_norm_g, v_xa_wq, v_xa_wkv, v_xa_q_norm_g, v_xa_k_norm_g, v_xa_wo, v_norm3_g, v_mlp_up, v_mlp_down):
    given = dict(locals())
    weights = {n: given[n] for n in WEIGHT_ORDER}
    moms = {n: given["m_" + n] for n in WEIGHT_ORDER}
    vars_ = {n: given["v_" + n] for n in WEIGHT_ORDER}
    d = x.shape[-1]
    ff = mlp_down.shape[1] * 4
    small = {n: weights[n] for n in SMALL_NAMES}

    comm = _MeshWeights({n: weights[n][0].astype(_MXU_DTYPE) for n in BIG_NAMES}, d, ff)
    loss_row, grad_x, g_small = _local_step(x, mem, positions, loss_target, small, comm)
    packed, starts = _pack_rows([g_small[n] for n in SMALL_NAMES] + [loss_row], 1024)
    start, wait = _all_gather_small_split(packed)
    state = start()
    big_grads = comm.finish((grad_x, state["token"]))
    (own,), (gathered,) = wait(state, (big_grads[BIG_NAMES[0]],))
    device = (4 * lax.axis_index("x") + 2 * lax.axis_index("y") + lax.axis_index("c")).astype(jnp.int32).reshape(1)
    summed = _sum_devices(own, gathered, device)
    small_grads = {}
    for n, s in zip(SMALL_NAMES, starts):
        r, c = weights[n].shape
        small_grads[n] = summed[s:s + r, 0:c]
    loss = summed[starts[-1], 0]

    grads, deltas, new_m, new_v = {}, {}, {}, {}
    for n in BIG_NAMES:
        shp = weights[n].shape
        g2, dl, mo, vo = _adamw_big(weights[n][0], big_grads[n], moms[n][0], vars_[n][0], name="adamw_" + n)
        grads[n], deltas[n], new_m[n], new_v[n] = (a.reshape(shp) for a in (g2, dl, mo, vo))
    sm_out = _adamw_small([weights[n] for n in SMALL_NAMES], [small_grads[n] for n in SMALL_NAMES],
                          [moms[n] for n in SMALL_NAMES], [vars_[n] for n in SMALL_NAMES])
    ns = len(SMALL_NAMES)
    for i, n in enumerate(SMALL_NAMES):
        grads[n], deltas[n], new_m[n], new_v[n] = small_grads[n], sm_out[i], sm_out[ns + i], sm_out[2 * ns + i]

    return (loss, grad_x, *[grads[n] for n in WEIGHT_ORDER], *[deltas[n] for n in WEIGHT_ORDER],
            *[new_m[n] for n in WEIGHT_ORDER], *[new_v[n] for n in WEIGHT_ORDER])
```

```python
import numpy as np
import jax
import jax.numpy as jnp
from jax import lax
from jax.experimental import pallas as pl
from jax.experimental.pallas import tpu as pltpu

F32 = jnp.float32
_MXU_DTYPE = jnp.bfloat16

EPS = 1e-6
HG_HEADS = 4
HG_D = 128
HG_CHUNK = 64
HG_TILE = 512
HG_LEVELS = (32, 16, 8, 4, 2, 1)
SW_HEADS = 8
SW_KV_HEADS = 2
SW_GROUP = SW_HEADS // SW_KV_HEADS
SW_HD = 64
SW_BLOCK = 128
ROPE_THETA = 500000.0
ROT_DIM = SW_HD // 4
XA_HEADS = 4
XA_HD = 128
HG_COLS = 4 * HG_HEADS * HG_D
SW_COLS = (SW_HEADS + 2 * SW_KV_HEADS) * SW_HD

ADAM_LR = 0.001
ADAM_B1 = 0.9
ADAM_B2 = 0.999
ADAM_EPS = 1e-08
ADAM_WD = 0.01
ADAM_STEP = 10

VMEM_LIMIT = 56 * 1024 * 1024
MESH = pl.DeviceIdType.MESH

NN = ((1,), (0,))
NT = ((1,), (1,))
TN = ((0,), (0,))


def _mx(v):
    return v.astype(_MXU_DTYPE)


def _dot(a, b, dims=NN):
    return lax.dot_general(_mx(a), _mx(b), (dims, ((), ())), preferred_element_type=F32)


def _split_dot(a, v, dims, parts):
    acc = None
    rest = v
    for p in range(parts):
        piece = _mx(rest)
        term = lax.dot_general(a, piece, (dims, ((), ())), preferred_element_type=F32)
        acc = term if acc is None else acc + term
        if p + 1 < parts:
            rest = rest - piece.astype(F32)
    return acc


def _params(sem):
    return pltpu.CompilerParams(dimension_semantics=sem, vmem_limit_bytes=VMEM_LIMIT)


def _mm(a, b, mode, m, n, k, *, name, tm=1024, tn=1024, tk=1024, a_spec=None, b_spec=None, extras=(), rows=(),
        epilogue=None, out_dtypes=(F32,), row_sums=0, out_shape=None, out_spec=None, after=()):
    after = tuple(t for t in after if t is not None)
    tm, tn, tk = min(tm, m), min(tn, n), min(tk, k)
    assert m % tm == 0 and n % tn == 0 and k % tk == 0, (name, m, n, k, tm, tn, tk)
    gi, gj, gk = m // tm, n // tn, k // tk
    assert row_sums == 0 or gj == 1, name
    if a_spec is None:
        a_spec = (pl.BlockSpec((tk, tm), lambda i, j, kk: (kk, i)) if mode == TN
                  else pl.BlockSpec((tm, tk), lambda i, j, kk: (i, kk)))
    if b_spec is None:
        b_spec = (pl.BlockSpec((tn, tk), lambda i, j, kk: (j, kk)) if mode == NT
                  else pl.BlockSpec((tk, tn), lambda i, j, kk: (kk, j)))
    mn_spec = pl.BlockSpec((tm, tn), lambda i, j, kk: (i, j))
    if epilogue is None:
        epilogue = lambda acc: (acc,)
    row_spec = pl.BlockSpec((1, tn), lambda i, j, kk: (0, j))
    n_ex, n_out = len(extras) + len(rows), len(out_dtypes)
    if out_shape is None:
        out_shape = tuple(jax.ShapeDtypeStruct((m, n), d) for d in out_dtypes)
        out_spec = tuple(mn_spec for _ in out_dtypes)
    out_shape = tuple(out_shape) + tuple(jax.ShapeDtypeStruct((1, n), F32) for _ in range(row_sums))
    out_spec = tuple(out_spec) + tuple(row_spec for _ in range(row_sums))

    n_after = len(after)

    def body(*refs):
        a_ref, b_ref = refs[0], refs[1]
        ex = refs[2:2 + n_ex]
        outs = refs[2 + n_ex + n_after:2 + n_ex + n_after + n_out + row_sums]
        first_row_tile = pl.program_id(0) == 0

        def finish(acc):
            res = epilogue(acc, *[e[...] for e in ex])
            for o, r in zip(outs[:n_out], res[:n_out]):
                o[...] = r.astype(o.dtype)
            if row_sums:
                @pl.when(first_row_tile)
                def _():
                    for o in outs[n_out:]:
                        o[...] = jnp.zeros_like(o)

                for o, r in zip(outs[n_out:], res[n_out:]):
                    o[...] += r

        if gk == 1:
            finish(_dot(a_ref[...], b_ref[...], mode))
        else:
            acc_ref = refs[-1]
            kk = pl.program_id(2)

            @pl.when(kk == 0)
            def _():
                acc_ref[...] = jnp.zeros_like(acc_ref)

            acc_ref[...] += _dot(a_ref[...], b_ref[...], mode)

            @pl.when(kk == gk - 1)
            def _():
                finish(acc_ref[...])

    return pl.pallas_call(
        body, name=name, grid=(gi, gj, gk),
        in_specs=([a_spec, b_spec] + [mn_spec] * len(extras) + [row_spec] * len(rows)
                  + [pl.BlockSpec(memory_space=pl.ANY)] * n_after),
        out_specs=out_spec, out_shape=out_shape,
        scratch_shapes=[pltpu.VMEM((tm, tn), F32)] if gk > 1 else [],
        compiler_params=_params(("arbitrary" if row_sums else "parallel", "parallel", "arbitrary")),
    )(a, b, *extras, *rows, *after)


def _rms_rows(xv, g):
    return xv * lax.rsqrt(jnp.mean(xv * xv, axis=1, keepdims=True) + EPS) * g


def _rms_rows_bwd(xv, g, dyv):
    r = lax.rsqrt(jnp.mean(xv * xv, axis=1, keepdims=True) + EPS)
    u = dyv * g
    return (r * u - xv * (r * r * r) * jnp.mean(u * xv, axis=1, keepdims=True),
            jnp.sum(dyv * xv * r, axis=0, keepdims=True))


def _residual_rms(acc, res, g):
    h = acc + res
    return h, _rms_rows(h, g)


def _rms_bwd_residual(dhn, xv, dres, g):
    dx, dg = _rms_rows_bwd(xv, g, dhn)
    dx = dx + dres
    return dx, dx, dg


def _rms_fwd(x, g, *, name, tm=512, after=()):
    t, d = x.shape
    tm = min(tm, t)
    after = tuple(a for a in after if a is not None)

    def body(x_ref, g_ref, *rest):
        rest[-1][...] = _rms_rows(x_ref[...], g_ref[...]).astype(rest[-1].dtype)

    return pl.pallas_call(
        body, name=name, grid=(t // tm,),
        in_specs=[pl.BlockSpec((tm, d), lambda i: (i, 0)), pl.BlockSpec((1, d), lambda i: (0, 0))]
        + [pl.BlockSpec(memory_space=pl.ANY)] * len(after),
        out_specs=pl.BlockSpec((tm, d), lambda i: (i, 0)),
        out_shape=jax.ShapeDtypeStruct((t, d), _MXU_DTYPE),
        compiler_params=_params(("parallel",)),
    )(x, g, *after)


def _rms_gain_grad(x, g, dy, *, name, tm=512):
    t, d = x.shape
    tm = min(tm, t)

    def body(x_ref, g_ref, dy_ref, dg_ref):
        @pl.when(pl.program_id(0) == 0)
        def _():
            dg_ref[...] = jnp.zeros_like(dg_ref)

        dg_ref[...] += _rms_rows_bwd(x_ref[...], g_ref[...], dy_ref[...])[1]

    row = pl.BlockSpec((tm, d), lambda i: (i, 0))
    vec = pl.BlockSpec((1, d), lambda i: (0, 0))
    return pl.pallas_call(
        body, name=name, grid=(t // tm,), in_specs=[row, vec, row], out_specs=vec,
        out_shape=jax.ShapeDtypeStruct((1, d), F32), compiler_params=_params(("arbitrary",)),
    )(x, g, dy)


def _hg_constants():
    c = HG_CHUNK
    t = np.arange(c)
    sums = [t[None, :] <= t[:, None]]
    masks = []
    for m in HG_LEVELS:
        base = (t // (2 * m)) * (2 * m)
        mid = base + m - 1
        second = (t - base) >= m
        upper = (t[None, :] > mid[:, None]) & (t[None, :] <= t[:, None])
        lower = (t[None, :] > t[:, None]) & (t[None, :] <= mid[:, None])
        sums.append(np.where(second[:, None], upper, lower))
        masks.append(second[:, None] & (~second)[None, :] & (base[:, None] == base[None, :]))
    return (np.concatenate(sums, axis=0).astype(np.float32), np.stack(masks).astype(np.float32))


HG_HEAD_LANES = tuple(slice(HG_D * h, HG_D * (h + 1)) for h in range(HG_HEADS))


def _per_head(fn, slab):
    return jnp.concatenate([jnp.broadcast_to(fn(slab[:, hs]), (slab.shape[0], HG_D)) for hs in HG_HEAD_LANES], axis=1)


def _lane_sum(v):
    return jnp.sum(v, axis=1, keepdims=True)


def _lane_mean(v):
    return jnp.mean(v, axis=1, keepdims=True)


def _hg_gates(blk, lbp):
    w = HG_HEADS * HG_D
    q, x, v, gl = blk[:, 0:w], blk[:, w:2 * w], blk[:, 2 * w:3 * w], blk[:, 3 * w:4 * w]
    mx = jnp.max(lbp, axis=0, keepdims=True)
    e = jnp.exp(lbp - mx)
    lb = e[0:1, :] / jnp.sum(e, axis=0, keepdims=True)
    sig = jax.nn.sigmoid(x)
    f = lb + (1.0 - lb) * sig
    return q, v, gl, lb, sig, f, 1.0 - f, jnp.log(f)


def _hg_fwd(proj, lbp, ng, bsz, seq, *, y_width):
    t = proj.shape[0]
    nc = seq // HG_CHUNK
    a_np, m_np = _hg_constants()
    a_all = jnp.asarray(a_np, _MXU_DTYPE)
    masks = jnp.asarray(m_np, F32)
    nl = len(HG_LEVELS)

    ts = min(HG_TILE, seq)
    ns, nct = seq // ts, ts // HG_CHUNK
    hw = HG_HEADS * HG_D

    def body(p_ref, lb_ref, ng_ref, a_ref, m_ref, y_ref, o_ref, st_ref, carry):
        a_mat = a_ref[...]
        ngv = ng_ref[...]

        @pl.when(pl.program_id(0) == 0)
        def _():
            carry[...] = jnp.zeros_like(carry)

        ng4 = _tile_lanes(ngv, HG_HEADS)
        heads = range(HG_HEADS)
        exs = range(bsz)
        hl = HG_HEAD_LANES
        lbp_v = lb_ref[...]

        def chunk(c, _):
            rows = pl.ds(pl.multiple_of(c * HG_CHUNK, HG_CHUNK), HG_CHUNK)
            gates = [_hg_gates(p_ref[e, rows, :], lbp_v) for e in exs]
            q, v, gl = [g[0] for g in gates], [g[1] for g in gates], [g[2] for g in gates]
            k = [g[6] for g in gates]
            sts = [[carry[e, h] for h in heads] for e in exs]
            e_all = [_split_dot(a_mat, gates[e][7], NN, 3) for e in exs]
            b = [e_all[e][0:HG_CHUNK] for e in exs]
            qb = [q[e] * jnp.exp(b[e]) for e in exs]
            o = [[_dot(qb[e][:, hl[h]], sts[e][h], NT) for h in heads] for e in exs]
            p = [[jnp.zeros((HG_CHUNK, HG_CHUNK), F32) for _ in heads] for _ in exs]
            for li in range(nl):
                dec = [jnp.exp(e_all[e][HG_CHUNK * (li + 1):HG_CHUNK * (li + 2)]) for e in exs]
                qm, km, mk = [q[e] * dec[e] for e in exs], [k[e] * dec[e] for e in exs], m_ref[li]
                p = [[p[e][h] + mk * _dot(qm[e][:, hl[h]], km[e][:, hl[h]], NT) for h in heads] for e in exs]
            bl = [b[e][HG_CHUNK - 1:HG_CHUNK, :] for e in exs]
            kd = [k[e] * jnp.exp(bl[e] - b[e]) for e in exs]
            pv = [[_dot(p[e][h], v[e][:, hl[h]]) for h in heads] for e in exs]
            upd = [[_dot(v[e][:, hl[h]], kd[e][:, hl[h]], TN) for h in heads] for e in exs]
            for e in exs:
                o_all = (jnp.concatenate([o[e][h] + pv[e][h] for h in heads], axis=1)
                         + _per_head(_lane_sum, q[e] * k[e]) * v[e])
                r = lax.rsqrt(_per_head(_lane_mean, o_all * o_all) + EPS)
                ebl = jnp.exp(bl[e])
                for h in heads:
                    st_ref[e, h, c] = sts[e][h]
                    carry[e, h] = sts[e][h] * ebl[:, hl[h]] + upd[e][h]
                o_ref[e, rows, :] = o_all
                y_ref[e, rows, :] = (o_all * r * ng4) * (gl[e] * jax.nn.sigmoid(gl[e]))
            return 0

        lax.fori_loop(0, nct, chunk, 0)

    y3, o3, states = pl.pallas_call(
        body, name="hgrn2_fwd", grid=(ns,),
        in_specs=[pl.BlockSpec((bsz, ts, HG_COLS), lambda s: (0, s, 0)),
                  pl.BlockSpec((2, hw), lambda s: (0, 0)),
                  pl.BlockSpec((1, HG_D), lambda s: (0, 0)),
                  pl.BlockSpec(a_all.shape, lambda s: (0, 0)),
                  pl.BlockSpec(masks.shape, lambda s: (0, 0, 0))],
        out_specs=(pl.BlockSpec((bsz, ts, hw), lambda s: (0, s, 0)),
                   pl.BlockSpec((bsz, ts, hw), lambda s: (0, s, 0)),
                   pl.BlockSpec((bsz, HG_HEADS, nct, HG_D, HG_D), lambda s: (0, 0, s, 0, 0))),
        out_shape=(jax.ShapeDtypeStruct((bsz, seq, y_width), F32),
                   jax.ShapeDtypeStruct((bsz, seq, hw), F32),
                   jax.ShapeDtypeStruct((bsz, HG_HEADS, nc, HG_D, HG_D), F32)),
        scratch_shapes=[pltpu.VMEM((bsz, HG_HEADS, HG_D, HG_D), F32)],
        compiler_params=_params(("arbitrary",)),
    )(proj.reshape(bsz, seq, HG_COLS), lbp, ng, a_all, masks)
    return y3.reshape(t, y_width), o3.reshape(t, hw), states


def _hg_bwd(proj, lbp, ng, o_all, states, dy, bsz, seq, after=()):
    after = tuple(a for a in after if a is not None)
    t = proj.shape[0]
    nc = seq // HG_CHUNK
    a_np, m_np = _hg_constants()
    a_all = jnp.asarray(a_np, _MXU_DTYPE)
    masks = jnp.asarray(m_np, F32)
    nl = len(HG_LEVELS)
    cs = HG_CHUNK

    ts = min(HG_TILE, seq)
    ns, nct = seq // ts, ts // cs
    hw = HG_HEADS * HG_D

    def body(p_ref, lb_ref, ng_ref, a_ref, m_ref, o_ref, st_ref, dy_ref, *rest):
        dp_ref, dlb_ref, dng_ref, dst_ref = rest[len(after):]
        a_mat = a_ref[...]
        ngv = ng_ref[...]
        ng4 = _tile_lanes(ngv, HG_HEADS)
        last_row = lax.broadcasted_iota(jnp.int32, (cs, hw), 0) == cs - 1
        first = pl.program_id(0) == 0
        heads = range(HG_HEADS)
        exs = range(bsz)
        hl = HG_HEAD_LANES
        lbp_v = lb_ref[...]

        @pl.when(first)
        def _():
            dst_ref[...] = jnp.zeros_like(dst_ref)

        def side_by_side(parts):
            return jnp.concatenate(parts, axis=1)

        def chunk(i, carry):
            dlb_acc, dng_acc = carry
            c = nct - 1 - i
            rows = pl.ds(pl.multiple_of(c * cs, cs), cs)
            gates = [_hg_gates(p_ref[e, rows, :], lbp_v) for e in exs]
            q, v, gl = [g[0] for g in gates], [g[1] for g in gates], [g[2] for g in gates]
            lb, sig, f, k = gates[0][3], [g[4] for g in gates], [g[5] for g in gates], [g[6] for g in gates]
            o = [o_ref[e, rows, :] for e in exs]
            dyv = [dy_ref[e, rows, :] for e in exs]
            sts = [[st_ref[e, h, c] for h in heads] for e in exs]
            dsts = [[dst_ref[e, h] for h in heads] for e in exs]
            e_all = [_split_dot(a_mat, gates[e][7], NN, 3) for e in exs]
            b = [e_all[e][0:cs] for e in exs]
            eb = [jnp.exp(b[e]) for e in exs]
            bl = [b[e][cs - 1:cs, :] for e in exs]
            ebl = [jnp.exp(bl[e]) for e in exs]
            ekd = [jnp.exp(bl[e] - b[e]) for e in exs]
            qb = [q[e] * eb[e] for e in exs]
            kd = [k[e] * ekd[e] for e in exs]
            do, dgl = [], []
            for e in exs:
                sg = jax.nn.sigmoid(gl[e])
                silu = gl[e] * sg
                r = lax.rsqrt(_per_head(_lane_mean, o[e] * o[e]) + EPS)
                dgl.append(dyv[e] * (o[e] * r * ng4) * (sg * (1.0 + gl[e] * (1.0 - sg))))
                u = dyv[e] * silu * ng4
                do.append(r * u - o[e] * (r * r * r) * _per_head(_lane_mean, u * o[e]))
                dng4 = jnp.sum(dyv[e] * silu * o[e] * r, axis=0, keepdims=True)
                dng_acc = dng_acc + ((dng4[:, hl[0]] + dng4[:, hl[1]]) + (dng4[:, hl[2]] + dng4[:, hl[3]]))
            es, qm, km = [], [], []
            p = [[jnp.zeros((cs, cs), F32) for _ in heads] for _ in exs]
            for li in range(nl):
                dec = [jnp.exp(e_all[e][cs * (li + 1):cs * (li + 2)]) for e in exs]
                es.append(dec)
                qm.append([q[e] * dec[e] for e in exs])
                km.append([k[e] * dec[e] for e in exs])
                mk = m_ref[li]
                p = [[p[e][h] + mk * _dot(qm[li][e][:, hl[h]], km[li][e][:, hl[h]], NT) for h in heads] for e in exs]
            dp = [[_dot(do[e][:, hl[h]], v[e][:, hl[h]], NT) for h in heads] for e in exs]
            dv_p = [[_dot(p[e][h], do[e][:, hl[h]], TN) for h in heads] for e in exs]
            dv_s = [[_dot(kd[e][:, hl[h]], dsts[e][h], NT) for h in heads] for e in exs]
            dqb = [side_by_side([_dot(do[e][:, hl[h]], sts[e][h]) for h in heads]) for e in exs]
            dkd = [side_by_side([_dot(v[e][:, hl[h]], dsts[e][h]) for h in heads]) for e in exs]
            new_dst = [[_dot(do[e][:, hl[h]], qb[e][:, hl[h]], TN) for h in heads] for e in exs]
            dv = [side_by_side([dv_p[e][h] + dv_s[e][h] for h in heads]) + _per_head(_lane_sum, q[e] * k[e]) * do[e]
                  for e in exs]
            dq = [dqb[e] * eb[e] for e in exs]
            dk = [dkd[e] * ekd[e] for e in exs]
            de = []
            for e in exs:
                dbl = (jnp.sum(dkd[e] * kd[e], axis=0, keepdims=True)
                       + side_by_side([jnp.sum(dsts[e][h] * sts[e][h], axis=0, keepdims=True) for h in heads]) * ebl[e])
                de.append([dqb[e] * qb[e] - dkd[e] * kd[e] + jnp.where(last_row, dbl, 0.0)])
            for li in range(nl):
                mk = m_ref[li]
                dpm = [[mk * dp[e][h] for h in heads] for e in exs]
                dqm = [side_by_side([_dot(dpm[e][h], km[li][e][:, hl[h]]) for h in heads]) for e in exs]
                dkm = [side_by_side([_dot(dpm[e][h], qm[li][e][:, hl[h]], TN) for h in heads]) for e in exs]
                for e in exs:
                    dq[e] = dq[e] + dqm[e] * es[li][e]
                    dk[e] = dk[e] + dkm[e] * es[li][e]
                    de[e].append(dqm[e] * qm[li][e] + dkm[e] * km[li][e])
            dg = [_split_dot(a_mat, jnp.concatenate(de[e], axis=0), TN, 2) for e in exs]
            for e in exs:
                dpd = _per_head(_lane_sum, do[e] * v[e])
                df = dg[e] / f[e] - (dk[e] + dpd * q[e])
                dp_ref[e, rows, 0:hw] = _mx(dq[e] + dpd * k[e])
                dp_ref[e, rows, hw:2 * hw] = _mx(df * (1.0 - lb) * sig[e] * (1.0 - sig[e]))
                dp_ref[e, rows, 2 * hw:3 * hw] = _mx(dv[e])
                dp_ref[e, rows, 3 * hw:4 * hw] = _mx(dgl[e])
                for h in heads:
                    dst_ref[e, h] = dsts[e][h] * ebl[e][:, hl[h]] + new_dst[e][h]
                dlb_acc = dlb_acc + jnp.sum(df * (1.0 - sig[e]), axis=0, keepdims=True)
            return dlb_acc, dng_acc

        dlb, dng = lax.fori_loop(0, nct, chunk, (jnp.zeros((1, hw), F32), jnp.zeros((1, HG_D), F32)))

        @pl.when(first)
        def _():
            dlb_ref[...] = jnp.zeros_like(dlb_ref)
            dng_ref[...] = jnp.zeros_like(dng_ref)

        mx = jnp.max(lbp_v, axis=0, keepdims=True)
        e = jnp.exp(lbp_v - mx)
        s0 = e[0:1, :] / jnp.sum(e, axis=0, keepdims=True)
        da0 = dlb * s0 * (1.0 - s0)
        dlb_ref[...] += jnp.concatenate([da0, -da0], axis=0)
        dng_ref[...] += dng

    rows3 = lambda w: pl.BlockSpec((bsz, ts, w), lambda s: (0, ns - 1 - s, 0))
    dproj, dlb, dng = pl.pallas_call(
        body, name="hgrn2_bwd", grid=(ns,),
        in_specs=[rows3(HG_COLS),
                  pl.BlockSpec((2, hw), lambda s: (0, 0)),
                  pl.BlockSpec((1, HG_D), lambda s: (0, 0)),
                  pl.BlockSpec(a_all.shape, lambda s: (0, 0)),
                  pl.BlockSpec(masks.shape, lambda s: (0, 0, 0)),
                  rows3(hw),
                  pl.BlockSpec((bsz, HG_HEADS, nct, HG_D, HG_D), lambda s: (0, 0, ns - 1 - s, 0, 0)),
                  rows3(hw)] + [pl.BlockSpec(memory_space=pl.ANY)] * len(after),
        out_specs=(rows3(HG_COLS),
                   pl.BlockSpec((2, hw), lambda s: (0, 0)),
                   pl.BlockSpec((1, HG_D), lambda s: (0, 0))),
        out_shape=(jax.ShapeDtypeStruct((bsz, seq, HG_COLS), _MXU_DTYPE),
                   jax.ShapeDtypeStruct((2, hw), F32),
                   jax.ShapeDtypeStruct((1, HG_D), F32)),
        scratch_shapes=[pltpu.VMEM((bsz, HG_HEADS, HG_D, HG_D), F32)],
        compiler_params=_params(("arbitrary",)),
    )(proj.reshape(bsz, seq, HG_COLS), lbp, ng, a_all, masks, o_all.reshape(bsz, seq, hw), states,
      dy.reshape(bsz, seq, dy.shape[1]), *after)
    return dproj.reshape(t, HG_COLS), dlb, dng


def _sw_constants():
    half = ROT_DIM // 2
    inv = (np.float32(ROPE_THETA) ** (-(np.arange(half, dtype=np.float32) * np.float32(2.0) / np.float32(ROT_DIM)))
           ).astype(np.float32)
    freq = np.zeros((1, 128), np.float32)
    sign = np.zeros((1, 128), np.float32)
    for h in range(2):
        freq[0, 64 * h:64 * h + half] = inv
        freq[0, 64 * h + half:64 * h + 2 * half] = inv
        sign[0, 64 * h:64 * h + half] = -1.0
        sign[0, 64 * h + half:64 * h + 2 * half] = 1.0
    seg = np.kron(np.eye(8, dtype=np.float32), np.full((64, 64), 1.0 / 64.0, np.float32))
    return freq, sign, seg


def _rope_tables(pos, freq, sign):
    ang = pos.astype(F32) * freq
    return jnp.cos(ang), jnp.sin(ang) * sign


def _tile_lanes(v, times):
    return v if times == 1 else jnp.concatenate([v] * times, axis=1)


def _swap_halves(v):
    w = v.shape[1]
    half = ROT_DIM // 2
    lane = lax.broadcasted_iota(jnp.int32, v.shape, 1) % SW_HD
    return jnp.where(lane < half, pltpu.roll(v, w - half, 1), jnp.where(lane < 2 * half, pltpu.roll(v, half, 1), 0.0))


def _sw_norm_rope(tv, gain, seg, cosv, sinv):
    w = tv.shape[1]
    ms = _split_dot_rhs(tv * tv, seg[0:w, 0:w])
    r = lax.rsqrt(ms + EPS)
    tn = tv * r * gain
    reps = w // 128
    return tn * _tile_lanes(cosv, reps) + _swap_halves(tn) * _tile_lanes(sinv, reps), r


def _split_dot_rhs(v, a):
    hi = _mx(v)
    lo = _mx(v - hi.astype(F32))
    return (lax.dot_general(hi, a, (NN, ((), ())), preferred_element_type=F32)
            + lax.dot_general(lo, a, (NN, ((), ())), preferred_element_type=F32))


def _sw_norm_rope_bwd(dt, tv, r, gain, seg, cosv, sinv):
    w = tv.shape[1]
    reps = w // 128
    dtn = dt * _tile_lanes(cosv, reps) + _swap_halves(dt * _tile_lanes(sinv, reps))
    u = dtn * gain
    dtv = r * u - tv * (r * r * r) * _split_dot_rhs(u * tv, seg[0:w, 0:w])
    return dtv, jnp.sum(dtn * tv * r, axis=0, keepdims=True)


def _sw_scores(qh, kp, kc):
    return _dot(qh, kp, NT), _dot(qh, kc, NT)


def _sw_probs(raw, sink, first_block):
    scale = SW_HD ** -0.5
    qi = lax.broadcasted_iota(jnp.int32, (SW_BLOCK, SW_BLOCK), 0)
    kj = lax.broadcasted_iota(jnp.int32, (SW_BLOCK, SW_BLOCK), 1)
    ok_prev = jnp.logical_and(kj > qi, jnp.logical_not(first_block))
    ok_cur = kj <= qi
    sp = jnp.where(ok_prev, raw[0] * scale, -jnp.inf)
    sc = jnp.where(ok_cur, raw[1] * scale, -jnp.inf)
    m = jnp.maximum(jnp.maximum(jnp.max(sp, axis=1, keepdims=True), jnp.max(sc, axis=1, keepdims=True)), sink)
    pp, pc = jnp.exp(sp - m), jnp.exp(sc - m)
    es = jnp.exp(sink - m)
    den = jnp.sum(pp, axis=1, keepdims=True) + jnp.sum(pc, axis=1, keepdims=True) + es
    return pp / den, pc / den, es / den


def _sw_specs(nb):
    def cur(b, n):
        return b * nb + jnp.minimum(n, nb - 1)

    def prev(b, n):
        return b * nb + jnp.maximum(jnp.minimum(n, nb - 1) - 1, 0)

    return cur, prev


def _sw_fwd(proj, pos, qg, kg, sinks, y_in, bsz, seq):
    t = proj.shape[0]
    nb = seq // SW_BLOCK
    freq_np, sign_np, seg_np = _sw_constants()
    freq, sign = jnp.asarray(freq_np), jnp.asarray(sign_np)
    seg = jnp.asarray(seg_np, _MXU_DTYPE)
    cur, prev = _sw_specs(nb)

    def body(q_ref, kc_ref, kp_ref, vc_ref, vp_ref, pc_ref, pp_ref, qg_ref, kg_ref, sk_ref, fr_ref, sn_ref, seg_ref,
             yin_ref, y_ref):
        del yin_ref
        n = pl.program_id(1)
        segv = seg_ref[...]
        cos_c, sin_c = _rope_tables(pc_ref[...], fr_ref[...], sn_ref[...])
        cos_p, sin_p = _rope_tables(pp_ref[...], fr_ref[...], sn_ref[...])
        qr, _ = _sw_norm_rope(q_ref[...], qg_ref[...], segv, cos_c, sin_c)
        kcr, _ = _sw_norm_rope(kc_ref[...], kg_ref[...], segv, cos_c, sin_c)
        kpr, _ = _sw_norm_rope(kp_ref[...], kg_ref[...], segv, cos_p, sin_p)
        vc, vp = vc_ref[...], vp_ref[...]
        ks = [slice(SW_HD * (h // SW_GROUP), SW_HD * (h // SW_GROUP + 1)) for h in range(SW_HEADS)]
        raw = [_sw_scores(qr[:, SW_HD * h:SW_HD * (h + 1)], kpr[:, ks[h]], kcr[:, ks[h]]) for h in range(SW_HEADS)]
        probs = [_sw_probs(raw[h], sk_ref[0, h], n == 0) for h in range(SW_HEADS)]
        for h in range(SW_HEADS):
            y_ref[:, SW_HD * h:SW_HD * (h + 1)] = _dot(probs[h][0], vp[:, ks[h]]) + _dot(probs[h][1], vc[:, ks[h]])

    rowq = pl.BlockSpec((SW_BLOCK, 512), lambda b, n: (cur(b, n), 0))
    full = lambda a: pl.BlockSpec(a.shape, lambda b, n: (0,) * a.ndim)
    yw = y_in.shape[1]
    return pl.pallas_call(
        body, name="swa_fwd", grid=(bsz, nb),
        in_specs=[rowq,
                  pl.BlockSpec((SW_BLOCK, 128), lambda b, n: (cur(b, n), 4)),
                  pl.BlockSpec((SW_BLOCK, 128), lambda b, n: (prev(b, n), 4)),
                  pl.BlockSpec((SW_BLOCK, 128), lambda b, n: (cur(b, n), 5)),
                  pl.BlockSpec((SW_BLOCK, 128), lambda b, n: (prev(b, n), 5)),
                  pl.BlockSpec((SW_BLOCK, 1), lambda b, n: (cur(b, n), 0)),
                  pl.BlockSpec((SW_BLOCK, 1), lambda b, n: (prev(b, n), 0)),
                  full(qg), full(kg),
                  pl.BlockSpec(memory_space=pltpu.SMEM),
                  full(freq), full(sign), full(seg),
                  pl.BlockSpec(memory_space=pl.ANY)],
        out_specs=pl.BlockSpec((SW_BLOCK, 512), lambda b, n: (cur(b, n), 1)),
        out_shape=jax.ShapeDtypeStruct((t, yw), F32),
        input_output_aliases={13: 0},
        compiler_params=_params(("parallel", "parallel")),
    )(proj, proj, proj, proj, proj, pos, pos, qg, kg, sinks, freq, sign, seg, y_in)


def _sw_bwd(proj, pos, qg, kg, sinks, y, dy, bsz, seq):
    t = proj.shape[0]
    nb = seq // SW_BLOCK
    freq_np, sign_np, seg_np = _sw_constants()
    freq, sign = jnp.asarray(freq_np), jnp.asarray(sign_np)
    seg = jnp.asarray(seg_np, _MXU_DTYPE)
    cur, prev = _sw_specs(nb)
    scale = SW_HD ** -0.5

    def body(q_ref, kc_ref, kp_ref, vc_ref, vp_ref, pc_ref, pp_ref, qg_ref, kg_ref, sk_ref, fr_ref, sn_ref, seg_ref,
             y_ref, dy_ref, dp_ref, dqg_ref, dkg_ref, dsk_ref,
             dq_car, dkv_car, dqr_s, dkc_s, dkp_s, dvc_s, dvp_s, gq_acc, gk_acc, sk_acc):
        b, n = pl.program_id(0), pl.program_id(1)
        first = jnp.logical_and(b == 0, n == 0)
        last = jnp.logical_and(b == pl.num_programs(0) - 1, n == nb)

        @pl.when(first)
        def _():
            gq_acc[...] = jnp.zeros_like(gq_acc)
            gk_acc[...] = jnp.zeros_like(gk_acc)
            sk_acc[...] = jnp.zeros_like(sk_acc)

        @pl.when(n < nb)
        def _():
            segv = seg_ref[...]
            cos_c, sin_c = _rope_tables(pc_ref[...], fr_ref[...], sn_ref[...])
            cos_p, sin_p = _rope_tables(pp_ref[...], fr_ref[...], sn_ref[...])
            qv, kcv, kpv = q_ref[...], kc_ref[...], kp_ref[...]
            qr, rq = _sw_norm_rope(qv, qg_ref[...], segv, cos_c, sin_c)
            kcr, rkc = _sw_norm_rope(kcv, kg_ref[...], segv, cos_c, sin_c)
            kpr, rkp = _sw_norm_rope(kpv, kg_ref[...], segv, cos_p, sin_p)
            vc, vp = vc_ref[...], vp_ref[...]
            lane = lax.broadcasted_iota(jnp.int32, (1, 128), 1)
            dsk = jnp.zeros((1, 128), F32)
            heads = range(SW_HEADS)
            ks = [slice(SW_HD * (h // SW_GROUP), SW_HD * (h // SW_GROUP + 1)) for h in heads]
            hs = [slice(SW_HD * h, SW_HD * (h + 1)) for h in heads]
            qh = [qr[:, hs[h]] for h in heads]
            doh = [dy_ref[:, hs[h]] for h in heads]
            raw = [_sw_scores(qh[h], kpr[:, ks[h]], kcr[:, ks[h]]) for h in heads]
            dpp = [_dot(doh[h], vp[:, ks[h]], NT) for h in heads]
            dpc = [_dot(doh[h], vc[:, ks[h]], NT) for h in heads]
            probs = [_sw_probs(raw[h], sk_ref[0, h], n == 0) for h in heads]
            dsp, dsc = [], []
            for h in heads:
                pp, pc, ps = probs[h]
                delta = jnp.sum(doh[h] * y_ref[:, hs[h]], axis=1, keepdims=True)
                dsp.append(pp * (dpp[h] - delta) * scale)
                dsc.append(pc * (dpc[h] - delta) * scale)
                dsk = dsk + jnp.where(lane == h, -jnp.sum(ps * delta), 0.0)
            for h in heads:
                dqr_s[:, hs[h]] = _dot(dsp[h], kpr[:, ks[h]]) + _dot(dsc[h], kcr[:, ks[h]])
            for kv in range(SW_KV_HEADS):
                group = range(SW_GROUP * kv, SW_GROUP * (kv + 1))
                kvs = slice(SW_HD * kv, SW_HD * (kv + 1))
                dvp_s[:, kvs] = sum(_dot(probs[h][0], doh[h], TN) for h in group)
                dvc_s[:, kvs] = sum(_dot(probs[h][1], doh[h], TN) for h in group)
                dkp_s[:, kvs] = sum(_dot(dsp[h], qh[h], TN) for h in group)
                dkc_s[:, kvs] = sum(_dot(dsc[h], qh[h], TN) for h in group)
            dq, gq = _sw_norm_rope_bwd(dqr_s[...], qv, rq, qg_ref[...], segv, cos_c, sin_c)
            dkc, gkc = _sw_norm_rope_bwd(dkc_s[...], kcv, rkc, kg_ref[...], segv, cos_c, sin_c)
            dkp, gkp = _sw_norm_rope_bwd(dkp_s[...], kpv, rkp, kg_ref[...], segv, cos_p, sin_p)
            gq_acc[...] += gq
            gk_acc[...] += gkc + gkp
            sk_acc[...] += dsk

            @pl.when(n > 0)
            def _():
                dp_ref[:, 0:512] = _mx(dq_car[...])
                dp_ref[:, 512:640] = _mx(dkv_car[:, 0:128] + dkp)
                dp_ref[:, 640:768] = _mx(dkv_car[:, 128:256] + dvp_s[...])

            dq_car[...] = dq
            dkv_car[:, 0:128] = dkc
            dkv_car[:, 128:256] = dvc_s[...]

        @pl.when(n == nb)
        def _():
            dp_ref[:, 0:512] = _mx(dq_car[...])
            dp_ref[:, 512:768] = _mx(dkv_car[...])

        @pl.when(last)
        def _():
            gq = gq_acc[...]
            acc = gq[:, 0:SW_HD]
            for h in range(1, SW_HEADS):
                acc = acc + gq[:, SW_HD * h:SW_HD * (h + 1)]
            dqg_ref[...] = acc
            gk = gk_acc[...]
            dkg_ref[...] = gk[:, 0:SW_HD] + gk[:, SW_HD:2 * SW_HD]
            dsk_ref[...] = sk_acc[...]

    rowq = pl.BlockSpec((SW_BLOCK, 512), lambda b, n: (cur(b, n), 0))
    full = lambda a: pl.BlockSpec(a.shape, lambda b, n: (0,) * a.ndim)

    def out_row(b, n):
        return b * nb + jnp.maximum(n - 1, 0)

    return pl.pallas_call(
        body, name="swa_bwd", grid=(bsz, nb + 1),
        in_specs=[rowq,
                  pl.BlockSpec((SW_BLOCK, 128), lambda b, n: (cur(b, n), 4)),
                  pl.BlockSpec((SW_BLOCK, 128), lambda b, n: (prev(b, n), 4)),
                  pl.BlockSpec((SW_BLOCK, 128), lambda b, n: (cur(b, n), 5)),
                  pl.BlockSpec((SW_BLOCK, 128), lambda b, n: (prev(b, n), 5)),
                  pl.BlockSpec((SW_BLOCK, 1), lambda b, n: (cur(b, n), 0)),
                  pl.BlockSpec((SW_BLOCK, 1), lambda b, n: (prev(b, n), 0)),
                  full(qg), full(kg),
                  pl.BlockSpec(memory_space=pltpu.SMEM),
                  full(freq), full(sign), full(seg),
                  pl.BlockSpec((SW_BLOCK, 512), lambda b, n: (cur(b, n), 1)),
                  pl.BlockSpec((SW_BLOCK, 512), lambda b, n: (cur(b, n), 1))],
        out_specs=(pl.BlockSpec((SW_BLOCK, SW_COLS), lambda b, n: (out_row(b, n), 0)),
                   pl.BlockSpec((1, SW_HD), lambda b, n: (0, 0)),
                   pl.BlockSpec((1, SW_HD), lambda b, n: (0, 0)),
                   pl.BlockSpec((1, 128), lambda b, n: (0, 0))),
        out_shape=(jax.ShapeDtypeStruct((t, SW_COLS), _MXU_DTYPE),
                   jax.ShapeDtypeStruct((1, SW_HD), F32),
                   jax.ShapeDtypeStruct((1, SW_HD), F32),
                   jax.ShapeDtypeStruct((1, 128), F32)),
        scratch_shapes=[pltpu.VMEM((SW_BLOCK, 512), F32), pltpu.VMEM((SW_BLOCK, 256), F32),
                        pltpu.VMEM((SW_BLOCK, 512), F32),
                        pltpu.VMEM((SW_BLOCK, 128), F32), pltpu.VMEM((SW_BLOCK, 128), F32),
                        pltpu.VMEM((SW_BLOCK, 128), F32), pltpu.VMEM((SW_BLOCK, 128), F32),
                        pltpu.VMEM((1, 512), F32), pltpu.VMEM((1, 128), F32), pltpu.VMEM((1, 128), F32)],
        compiler_params=_params(("arbitrary", "arbitrary")),
    )(proj, proj, proj, proj, proj, pos, pos, qg, kg, sinks, freq, sign, seg, y, dy)


def _head_rms(tv, gain):
    r = lax.rsqrt(jnp.mean(tv * tv, axis=1, keepdims=True) + EPS)
    return tv * r * gain, r


def _head_rms_bwd(dtn, tv, r, gain):
    u = dtn * gain
    return r * u - tv * (r * r * r) * jnp.mean(u * tv, axis=1, keepdims=True), jnp.sum(dtn * tv * r, axis=0, keepdims=True)


def _xa_softmax(raw):
    s = raw * (XA_HD ** -0.5)
    e = jnp.exp(s - jnp.max(s, axis=1, keepdims=True))
    return e / jnp.sum(e, axis=1, keepdims=True)


def _xa_fwd(qx, kvx, qg, kg, bsz, seq, mlen, *, tq=512):
    t = qx.shape[0]
    tq = min(tq, seq)
    nq = seq // tq
    w = XA_HEADS * XA_HD

    def body(q_ref, kv_ref, qg_ref, kg_ref, o_ref):
        heads = range(XA_HEADS)
        hs = [slice(XA_HD * h, XA_HD * (h + 1)) for h in heads]
        qn = [_head_rms(q_ref[:, hs[h]], qg_ref[...])[0] for h in heads]
        kn = [_head_rms(kv_ref[:, hs[h]], kg_ref[...])[0] for h in heads]
        raw = [_dot(qn[h], kn[h], NT) for h in heads]
        p = [_xa_softmax(raw[h]) for h in heads]
        for h in heads:
            o_ref[:, hs[h]] = _dot(p[h], kv_ref[:, w + XA_HD * h:w + XA_HD * (h + 1)]).astype(o_ref.dtype)

    vec = pl.BlockSpec((1, XA_HD), lambda b, i: (0, 0))
    return pl.pallas_call(
        body, name="xattn_fwd", grid=(bsz, nq),
        in_specs=[pl.BlockSpec((tq, w), lambda b, i: (b * nq + i, 0)),
                  pl.BlockSpec((mlen, 2 * w), lambda b, i: (b, 0)), vec, vec],
        out_specs=pl.BlockSpec((tq, w), lambda b, i: (b * nq + i, 0)),
        out_shape=jax.ShapeDtypeStruct((t, w), _MXU_DTYPE),
        compiler_params=_params(("parallel", "parallel")),
    )(qx, kvx, qg, kg)


def _xa_bwd(qx, kvx, qg, kg, do, bsz, seq, mlen, *, tq=512):
    t = qx.shape[0]
    tq = min(tq, seq)
    nq = seq // tq
    w = XA_HEADS * XA_HD
    scale = XA_HD ** -0.5

    def body(q_ref, kv_ref, qg_ref, kg_ref, do_ref, dq_ref, dkv_ref, dqg_ref, dkg_ref):
        b, i = pl.program_id(0), pl.program_id(1)

        @pl.when(jnp.logical_and(b == 0, i == 0))
        def _():
            dqg_ref[...] = jnp.zeros_like(dqg_ref)
            dkg_ref[...] = jnp.zeros_like(dkg_ref)

        @pl.when(i == 0)
        def _():
            dkv_ref[...] = jnp.zeros_like(dkv_ref)

        heads = range(XA_HEADS)
        hs = [slice(XA_HD * h, XA_HD * (h + 1)) for h in heads]
        vs = [slice(w + XA_HD * h, w + XA_HD * (h + 1)) for h in heads]
        qv = [q_ref[:, hs[h]] for h in heads]
        kv = [kv_ref[:, hs[h]] for h in heads]
        doh = [do_ref[:, hs[h]] for h in heads]
        qn = [_head_rms(qv[h], qg_ref[...]) for h in heads]
        kn = [_head_rms(kv[h], kg_ref[...]) for h in heads]
        raw = [_dot(qn[h][0], kn[h][0], NT) for h in heads]
        dp = [_dot(doh[h], kv_ref[:, vs[h]], NT) for h in heads]
        p = [_xa_softmax(raw[h]) for h in heads]
        ds = [p[h] * (dp[h] - jnp.sum(p[h] * dp[h], axis=1, keepdims=True)) * scale for h in heads]
        dqn = [_dot(ds[h], kn[h][0]) for h in heads]
        dkn = [_dot(ds[h], qn[h][0], TN) for h in heads]
        dvv = [_dot(p[h], doh[h], TN) for h in heads]
        gq_sum = jnp.zeros((1, XA_HD), F32)
        gk_sum = jnp.zeros((1, XA_HD), F32)
        for h in heads:
            dqv, gq = _head_rms_bwd(dqn[h], qv[h], qn[h][1], qg_ref[...])
            dkv, gk = _head_rms_bwd(dkn[h], kv[h], kn[h][1], kg_ref[...])
            dq_ref[:, hs[h]] = dqv.astype(dq_ref.dtype)
            dkv_ref[:, hs[h]] += dkv
            dkv_ref[:, vs[h]] += dvv[h]
            gq_sum = gq_sum + gq
            gk_sum = gk_sum + gk
        dqg_ref[...] += gq_sum
        dkg_ref[...] += gk_sum

    vec = pl.BlockSpec((1, XA_HD), lambda b, i: (0, 0))
    row = pl.BlockSpec((tq, w), lambda b, i: (b * nq + i, 0))
    mem = pl.BlockSpec((mlen, 2 * w), lambda b, i: (b, 0))
    return pl.pallas_call(
        body, name="xattn_bwd", grid=(bsz, nq),
        in_specs=[row, mem, vec, vec, row],
        out_specs=(row, mem, vec, vec),
        out_shape=(jax.ShapeDtypeStruct((t, w), _MXU_DTYPE), jax.ShapeDtypeStruct((bsz * mlen, 2 * w), F32),
                   jax.ShapeDtypeStruct((1, XA_HD), F32), jax.ShapeDtypeStruct((1, XA_HD), F32)),
        compiler_params=_params(("arbitrary", "arbitrary")),
    )(qx, kvx, qg, kg, do)


def _loss_finish(sq_row, d_model):
    def body(s_ref, o_ref):
        o_ref[...] = jnp.zeros_like(o_ref) + 0.5 * jnp.sum(s_ref[...]) / float(d_model)

    return pl.pallas_call(body, name="loss_finish", out_shape=jax.ShapeDtypeStruct((1, 128), F32))(sq_row)


def _adamw_math(w, g, m, v):
    m = ADAM_B1 * m + (1.0 - ADAM_B1) * g
    v = ADAM_B2 * v + (1.0 - ADAM_B2) * (g * g)
    m_hat = m / (1.0 - ADAM_B1 ** ADAM_STEP)
    v_hat = v / (1.0 - ADAM_B2 ** ADAM_STEP)
    return -ADAM_LR * (m_hat / (jnp.sqrt(v_hat) + ADAM_EPS) + ADAM_WD * w), m, v


def _adamw_big(w, g, m, v, *, name, tr=512):
    r, c = w.shape
    tr = min(tr, r)

    def body(w_ref, g_ref, m_ref, v_ref, go_ref, d_ref, mo_ref, vo_ref):
        gv = g_ref[...]
        d, mn, vn = _adamw_math(w_ref[...], gv, m_ref[...], v_ref[...])
        go_ref[...] = gv
        d_ref[...] = d
        mo_ref[...] = mn
        vo_ref[...] = vn

    spec = pl.BlockSpec((tr, c), lambda i: (i, 0))
    shp = jax.ShapeDtypeStruct((r, c), F32)
    return pl.pallas_call(
        body, name=name, grid=(r // tr,), in_specs=[spec] * 4, out_specs=(spec,) * 4, out_shape=(shp,) * 4,
        compiler_params=_params(("parallel",)),
    )(w, g, m, v)


def _adamw_small(ws, gs, ms, vs):
    n = len(ws)

    def body(*refs):
        for i in range(n):
            d, mn, vn = _adamw_math(refs[i][...], refs[n + i][...], refs[2 * n + i][...], refs[3 * n + i][...])
            refs[4 * n + i][...] = d
            refs[5 * n + i][...] = mn
            refs[6 * n + i][...] = vn

    shapes = tuple(jax.ShapeDtypeStruct(w.shape, F32) for w in ws)
    return pl.pallas_call(body, name="adamw_small", out_shape=shapes * 3)(*ws, *gs, *ms, *vs)


def _add_halves(g, recv, c_idx, *, name, tr=512):
    _, r, c = g.shape
    h = r // 2
    tr = min(tr, h)
    nt = h // tr

    def body(c_ref, g_ref, r_ref, o_ref):
        del c_ref
        o_ref[...] = g_ref[...] + r_ref[...]

    return pl.pallas_call(
        body, name=name,
        grid_spec=pltpu.PrefetchScalarGridSpec(
            num_scalar_prefetch=1, grid=(4, nt),
            in_specs=[pl.BlockSpec((None, tr, c), lambda k, i, cr: (k, cr[0] * nt + i, 0)),
                      pl.BlockSpec((None, tr, c), lambda k, i, cr: (k, i, 0))],
            out_specs=pl.BlockSpec((None, tr, c), lambda k, i, cr: (k, i, 0))),
        out_shape=jax.ShapeDtypeStruct((4, h, c), F32),
        compiler_params=_params(("parallel", "parallel")),
    )(c_idx, g, recv)


def _add_chips(p, recv, place_idx, *, name, tr=512, after=()):
    _, h, c = p.shape
    tr = min(tr, h)
    nt = h // tr

    def body(pi_ref, p_ref, r_ref, *rest):
        del pi_ref
        rest[-1][...] = ((p_ref[...] + r_ref[0]) + r_ref[1]) + r_ref[2]

    return pl.pallas_call(
        body, name=name,
        grid_spec=pltpu.PrefetchScalarGridSpec(
            num_scalar_prefetch=1, grid=(nt,),
            in_specs=[pl.BlockSpec((None, tr, c), lambda i, pi: (pi[0], i, 0)),
                      pl.BlockSpec((3, tr, c), lambda i, pi: (0, i, 0))] + [pl.BlockSpec(memory_space=pl.ANY)] * len(after),
            out_specs=pl.BlockSpec((tr, c), lambda i, pi: (pi[1] * nt + i, 0))),
        out_shape=jax.ShapeDtypeStruct((2 * h, c), F32),
        compiler_params=_params(("parallel",)),
    )(place_idx, p, recv, *after)


def _place_shard(shard, place_idx, *, name, tr=512, after=()):
    r, c = shard.shape
    tr = min(tr, r)

    def body(pi_ref, s_ref, *rest):
        del pi_ref
        rest[-1][...] = s_ref[...]

    return pl.pallas_call(
        body, name=name,
        grid_spec=pltpu.PrefetchScalarGridSpec(
            num_scalar_prefetch=1, grid=(r // tr,),
            in_specs=[pl.BlockSpec((tr, c), lambda i, pi: (i, 0))] + [pl.BlockSpec(memory_space=pl.ANY)] * len(after),
            out_specs=pl.BlockSpec((None, tr, c), lambda i, pi: (pi[0], i, 0))),
        out_shape=jax.ShapeDtypeStruct((4, r, c), shard.dtype),
        compiler_params=_params(("parallel",)),
    )(place_idx, shard, *after)


def _place():
    x, y, c = lax.axis_index("x"), lax.axis_index("y"), lax.axis_index("c")
    chips = [(1 - x, y), (x, 1 - y), (1 - x, 1 - y)]
    return x, y, c, chips


ANY = pl.BlockSpec(memory_space=pl.ANY)


def _exchange_halves(grads, name):
    n = len(grads)

    def body(*refs):
        ins, outs = refs[:n], refs[n:2 * n]
        send_sems, recv_sems = refs[2 * n:]
        x, y, c, _ = _place()

        def copy(a):
            h = ins[a].shape[1] // 2
            return pltpu.make_async_remote_copy(
                src_ref=ins[a].at[:, pl.ds((1 - c) * h, h), :], dst_ref=outs[a],
                send_sem=send_sems.at[a], recv_sem=recv_sems.at[a], device_id=(x, y, 1 - c), device_id_type=MESH)

        for a in range(n):
            copy(a).start()
        for a in range(n):
            copy(a).wait_recv()
        for a in range(n):
            copy(a).wait_send()

    return pl.pallas_call(
        body, name=name,
        in_specs=[ANY] * n, out_specs=tuple([ANY] * n),
        out_shape=tuple(jax.ShapeDtypeStruct((4, g.shape[1] // 2, g.shape[2]), g.dtype) for g in grads),
        scratch_shapes=[pltpu.SemaphoreType.DMA((n,)), pltpu.SemaphoreType.DMA((n,))],
    )(*grads)


HBM = pl.BlockSpec(memory_space=pltpu.HBM)
SEM = pl.BlockSpec(memory_space=pltpu.SEMAPHORE)
EFFECT = pltpu.SideEffectType.DATAFLOW_SIDE_EFFECTING


def _in_hbm(a):
    return pltpu.with_memory_space_constraint(a, pltpu.HBM)


def _split_copy_calls(name, srcs, lands, n_copies, make_copies):
    ns, nl = len(srcs), len(lands)
    nb = ns + nl

    def start(after=()):
        n_after = len(after)

        def body(*refs):
            outs = refs[nb + n_after:]
            copies = make_copies(refs[:ns], refs[ns:nb], outs[0], outs[1])
            for cp in copies:
                cp.start()
            token = refs[-1]
            token[...] = jnp.zeros_like(token)

        bufs = [_in_hbm(a) for a in list(srcs) + list(lands)]
        out = pl.pallas_call(
            body, name=name + "_start",
            out_shape=(pltpu.SemaphoreType.DMA((n_copies,)), pltpu.SemaphoreType.DMA((n_copies,)),
                       *[pltpu.HBM(a.shape, a.dtype) for a in bufs], jax.ShapeDtypeStruct((8, 128), F32)),
            in_specs=[HBM] * nb + [pl.BlockSpec(memory_space=pl.ANY)] * n_after,
            out_specs=(SEM, SEM, *[HBM] * nb, pl.BlockSpec(memory_space=pltpu.VMEM)),
            input_output_aliases={i: 2 + i for i in range(nb)},
            compiler_params=pltpu.CompilerParams(has_side_effects=EFFECT),
        )(*bufs, *after)
        return dict(send=out[0], recv=out[1], bufs=list(out[2:2 + nb]), token=out[-1])

    def wait(state, after):
        def body(*refs):
            copies = make_copies(refs[:ns], refs[ns:nb], refs[nb], refs[nb + 1])
            for cp in copies:
                cp.wait_send()
            for cp in copies:
                cp.wait_recv()

        bufs = state["bufs"]
        out = pl.pallas_call(
            body, name=name + "_wait",
            out_shape=tuple(pltpu.HBM(a.shape, a.dtype) for a in bufs),
            in_specs=[HBM] * nb + [SEM, SEM] + [pl.BlockSpec(memory_space=pl.ANY)] * len(after),
            out_specs=tuple([HBM] * nb),
            input_output_aliases={i: i for i in range(nb)},
            compiler_params=pltpu.CompilerParams(has_side_effects=EFFECT),
        )(*bufs, state["send"], state["recv"], *after)
        return list(out[:ns]), list(out[ns:])

    return start, wait


def _scatter_chips_split(name, parts):
    n = len(parts)
    lands = [lax.empty((3,) + p.shape[1:], p.dtype) for p in parts]

    def make_copies(srcs, lnds, send_sems, recv_sems):
        _, _, c, chips = _place()
        return [pltpu.make_async_remote_copy(
            src_ref=srcs[a].at[2 * px + py], dst_ref=lnds[a].at[j], send_sem=send_sems.at[a * 3 + j],
            recv_sem=recv_sems.at[a * 3 + j], device_id=(px, py, c), device_id_type=MESH)
            for a in range(n) for j, (px, py) in enumerate(chips)]

    return _split_copy_calls(name, parts, lands, 3 * n, make_copies)


def _exchange_halves_split(name, grads):
    n = len(grads)
    lands = [lax.empty((4, g.shape[1] // 2, g.shape[2]), g.dtype) for g in grads]

    def make_copies(srcs, lnds, send_sems, recv_sems):
        x, y, c, _ = _place()
        out = []
        for a in range(n):
            h = srcs[a].shape[1] // 2
            out.append(pltpu.make_async_remote_copy(
                src_ref=srcs[a].at[:, pl.ds((1 - c) * h, h), :], dst_ref=lnds[a], send_sem=send_sems.at[a],
                recv_sem=recv_sems.at[a], device_id=(x, y, 1 - c), device_id_type=MESH))
        return out

    return _split_copy_calls(name, grads, lands, n, make_copies)


def _gather_chips_split(name, shards, lands):
    n = len(shards)

    def make_copies(srcs, lnds, send_sems, recv_sems):
        x, y, c, chips = _place()
        out = []
        for a in range(n):
            h = srcs[a].shape[0] // 2
            for j, (px, py) in enumerate(chips):
                out.append(pltpu.make_async_remote_copy(
                    src_ref=srcs[a].at[pl.ds(c * h, h), :], dst_ref=lnds[a].at[2 * x + y, pl.ds(c * h, h), :],
                    send_sem=send_sems.at[a * 3 + j], recv_sem=recv_sems.at[a * 3 + j],
                    device_id=(px, py, c), device_id_type=MESH))
        return out

    return _split_copy_calls(name, shards, lands, 3 * n, make_copies)


def _gather_finish(gathered, name):
    n = len(gathered)

    def body(*refs):
        outs = refs[n:2 * n]
        send_sems, recv_sems = refs[2 * n:]
        x, y, c, chips = _place()

        def copy(a, j, chip_idx, which):
            h = outs[a].shape[1] // 2
            rows = outs[a].at[chip_idx, pl.ds(which * h, h), :]
            return pltpu.make_async_remote_copy(
                src_ref=rows, dst_ref=rows, send_sem=send_sems.at[a * 3 + j], recv_sem=recv_sems.at[a * 3 + j],
                device_id=(x, y, 1 - c), device_id_type=MESH)

        for a in range(n):
            for j, (px, py) in enumerate(chips):
                copy(a, j, 2 * px + py, c).start()
        for a in range(n):
            for j, (px, py) in enumerate(chips):
                copy(a, j, 2 * px + py, 1 - c).wait_recv()
        for a in range(n):
            for j, (px, py) in enumerate(chips):
                copy(a, j, 2 * px + py, c).wait_send()

    return pl.pallas_call(
        body, name=name,
        in_specs=[ANY] * n, out_specs=tuple([ANY] * n),
        out_shape=tuple(jax.ShapeDtypeStruct(g.shape, g.dtype) for g in gathered),
        input_output_aliases={i: i for i in range(n)},
        scratch_shapes=[pltpu.SemaphoreType.DMA((3 * n,)), pltpu.SemaphoreType.DMA((3 * n,))],
    )(*gathered)


def _gather_forward_split(name, gathered):
    n = len(gathered)

    def make_copies(srcs, lnds, send_sems, recv_sems):
        x, y, c, chips = _place()
        out = []
        for a in range(n):
            h = lnds[a].shape[1] // 2
            for j, (px, py) in enumerate(chips):
                rows = lnds[a].at[2 * px + py, pl.ds(c * h, h), :]
                out.append(pltpu.make_async_remote_copy(
                    src_ref=rows, dst_ref=rows, send_sem=send_sems.at[a * 3 + j], recv_sem=recv_sems.at[a * 3 + j],
                    device_id=(x, y, 1 - c), device_id_type=MESH))
        return out

    return _split_copy_calls(name, [], gathered, 3 * n, make_copies)


def _join_halves_split(name, fulls):
    n = len(fulls)

    def make_copies(srcs, lnds, send_sems, recv_sems):
        x, y, c, _ = _place()
        out = []
        for a in range(n):
            h = lnds[a].shape[0] // 2
            rows = lnds[a].at[pl.ds(c * h, h), :]
            out.append(pltpu.make_async_remote_copy(
                src_ref=rows, dst_ref=rows, send_sem=send_sems.at[a], recv_sem=recv_sems.at[a],
                device_id=(x, y, 1 - c), device_id_type=MESH))
        return out

    return _split_copy_calls(name, [], fulls, n, make_copies)


def _all_gather_small_split(sm):
    r, w = sm.shape

    def make_copies(srcs, lnds, send_sems, recv_sems):
        x, y, c, _ = _place()
        me = 4 * x + 2 * y + c
        rel = [(dx, dy, dc) for dx in (0, 1) for dy in (0, 1) for dc in (0, 1)][1:]
        return [pltpu.make_async_remote_copy(
            src_ref=srcs[0], dst_ref=lnds[0].at[me], send_sem=send_sems.at[k], recv_sem=recv_sems.at[k],
            device_id=(1 - x if dx else x, 1 - y if dy else y, 1 - c if dc else c), device_id_type=MESH)
            for k, (dx, dy, dc) in enumerate(rel)]

    return _split_copy_calls("all_gather_small", [sm], [lax.empty((8, r, w), sm.dtype)], 7, make_copies)


def _sum_devices(sm, gathered, me_idx):
    def body(me_ref, sm_ref, g_ref, o_ref):
        own = sm_ref[...]
        acc = jnp.where(me_ref[0] == 0, own, g_ref[0])
        for d in range(1, 8):
            acc = acc + jnp.where(me_ref[0] == d, own, g_ref[d])
        o_ref[...] = acc

    vm = pl.BlockSpec(memory_space=pltpu.VMEM)
    return pl.pallas_call(
        body, name="sum_devices", in_specs=[pl.BlockSpec(memory_space=pltpu.SMEM), vm, vm], out_specs=vm,
        out_shape=jax.ShapeDtypeStruct(sm.shape, F32),
    )(me_idx, sm, gathered)


class _LocalWeights:
    def __init__(self, w):
        self.w = w
        self.g = {}

    def begin(self):
        return ()

    def first(self, after):
        del after
        return self.w

    def rest(self, after):
        del after
        return self.w

    def mlp(self, after):
        del after
        return self.w

    def grads(self, tag, g):
        del tag
        self.g.update(g)
        return ()

    def poll(self, after):
        del after
        return ()


def _local_step(x3, mem3, pos2, target3, small, comm):
    bsz, seq, d = x3.shape
    mlen = mem3.shape[1]
    t = bsz * seq
    tok = comm.begin()
    x = x3.reshape(t, d)
    mem = mem3.reshape(bsz * mlen, d)
    target = target3.reshape(t, d)
    pos = pos2.reshape(t, 1)
    qg_t = jnp.tile(small["sw_q_norm_g"], (1, SW_HEADS))
    kg_t = jnp.tile(small["sw_k_norm_g"], (1, SW_KV_HEADS))

    hn1 = _rms_fwd(x, small["norm1_g"], name="rms1_fwd", after=tok)
    w = comm.first(hn1)
    proj_hg = _mm(hn1, w["w_in_hg"], NN, t, HG_COLS, d, name="proj_hg", tk=d, after=(w.get("token"),))[0]
    proj_sw = _mm(hn1, w["w_in_sw"], NN, t, SW_COLS, d, name="proj_sw", tk=d)[0]
    y_mix, o_hg, states = _hg_fwd(proj_hg, small["hg_lower_bounds"], small["hg_norm_g"], bsz, seq, y_width=1024)
    y_mix = _sw_fwd(proj_sw, pos, qg_t, kg_t, small["sw_sinks"], y_mix, bsz, seq)
    w_in_hg, w_in_sw = w["w_in_hg"], w["w_in_sw"]
    w = comm.rest(y_mix)
    h1, hn2 = _mm(y_mix, w["w_out"], NN, t, d, 1024, name="out_proj", tk=1024, extras=(x,), rows=(small["norm2_g"],),
                  epilogue=_residual_rms, out_dtypes=(F32, _MXU_DTYPE), after=(w.get("token"),))
    mn = _rms_fwd(mem, small["mem_norm_g"], name="rms_mem_fwd")
    qx = _mm(hn2, w["wq"], NN, t, 512, d, name="xa_q", tk=d)[0]
    kvx = _mm(mn, w["wkv"], NN, bsz * mlen, 1024, d, name="xa_kv", tk=d)[0]
    ox = _xa_fwd(qx, kvx, small["xa_q_norm_g"], small["xa_k_norm_g"], bsz, seq, mlen)
    h2, hn3 = _mm(ox, w["wo"], NN, t, d, 512, name="xa_o", tk=512, extras=(h1,), rows=(small["norm3_g"],),
                  epilogue=_residual_rms, out_dtypes=(F32, _MXU_DTYPE))
    w = {**w, **comm.mlp(hn3)}
    ff = w["down"].shape[0]
    ffs = ff // 4

    def relu_sq(acc):
        a = jnp.maximum(acc, 0.0)
        return a, a * a

    act, act2 = _mm(hn3, w["up"], NN, t, ff, d, name="mlp_up", tm=2048, tn=ffs, tk=d,
                    b_spec=pl.BlockSpec((None, d, ffs), lambda i, j, kk: (j, 0, 0)),
                    epilogue=relu_sq, out_dtypes=(_MXU_DTYPE, _MXU_DTYPE))
    inv_d = 1.0 / d

    def loss_cotangent(acc, res, tgt):
        diff = acc + res - tgt
        v = diff * inv_d
        return v, v, jnp.sum(diff * diff, axis=0, keepdims=True)

    dy, dy_mx, sq_row = _mm(act2, w["down"], NN, t, d, ff, name="mlp_down", tk=2048, extras=(h2, target),
                            epilogue=loss_cotangent, out_dtypes=(F32, _MXU_DTYPE), row_sums=1)
    loss_row = _loss_finish(sq_row, d)

    dz = _mm(dy_mx, w["down"], NT, t, ff, d, name="d_act", tm=2048, tk=d, extras=(act,),
             epilogue=lambda acc, a: (acc * (2.0 * a.astype(F32)),), out_dtypes=(_MXU_DTYPE,))[0]
    g_down = _mm(act2, dy_mx, TN, ff, d, t, name="g_down", tk=t)[0]
    g_up = _mm(hn3, dz, TN, d, ff, t, name="g_up", tn=ffs, tk=t,
               out_shape=(jax.ShapeDtypeStruct((4, d, ffs), F32),),
               out_spec=(pl.BlockSpec((None, min(1024, d), ffs), lambda i, j, kk: (j, i, 0)),))[0]
    tok = comm.grads("mlp", dict(up=g_up, down=g_down))
    dh2, dh2_mx, g_norm3 = _mm(dz, w["up"], NT, t, d, ff, name="d_hn3", tk=ffs, after=tok,
                               b_spec=pl.BlockSpec((None, min(1024, d), ffs), lambda i, j, kk: (kk, j, 0)),
                               extras=(h2, dy), rows=(small["norm3_g"],), epilogue=_rms_bwd_residual,
                               out_dtypes=(F32, _MXU_DTYPE), row_sums=1)
    d_ox = _mm(dh2_mx, w["wo"], NT, t, 512, d, name="d_ox", tk=d)[0]
    g_wo = _mm(ox, dh2_mx, TN, 512, d, t, name="g_wo", tk=t)[0]
    d_qx, d_kvx, g_xq, g_xk = _xa_bwd(qx, kvx, small["xa_q_norm_g"], small["xa_k_norm_g"], d_ox, bsz, seq, mlen)
    g_wq = _mm(hn2, d_qx, TN, d, 512, t, name="g_wq")[0]
    g_wkv = _mm(mn, d_kvx, TN, d, 1024, bsz * mlen, name="g_wkv")[0]
    dh1, dh1_mx, g_norm2 = _mm(d_qx, w["wq"], NT, t, d, 512, name="d_hn2", tk=512, extras=(h1, dh2),
                               rows=(small["norm2_g"],), epilogue=_rms_bwd_residual, out_dtypes=(F32, _MXU_DTYPE),
                               row_sums=1)
    dmn = _mm(d_kvx, w["wkv"], NT, bsz * mlen, d, 1024, name="d_mn", tk=1024)[0]
    g_memn = _rms_gain_grad(mem, small["mem_norm_g"], dmn, name="rms_mem_bwd")
    g_wout = _mm(y_mix, dh1_mx, TN, 1024, d, t, name="g_wout", tk=2048)[0]
    tok = comm.grads("mid", dict(w_out=g_wout, wq=g_wq, wkv=g_wkv, wo=g_wo))
    d_mix = _mm(dh1_mx, w["w_out"], NT, t, 1024, d, name="d_mix", tk=d, after=tok)[0]
    dproj_sw, g_swq, g_swk, g_sinks = _sw_bwd(proj_sw, pos, qg_t, kg_t, small["sw_sinks"], y_mix, d_mix, bsz, seq)
    tok = comm.poll(dproj_sw)
    dproj_hg, g_lb, g_hgn = _hg_bwd(proj_hg, small["hg_lower_bounds"], small["hg_norm_g"], o_hg, states, d_mix, bsz, seq,
                                    after=tok)
    g_in_hg = _mm(hn1, dproj_hg, TN, d, HG_COLS, t, name="g_in_hg", tk=t)[0]
    g_in_sw = _mm(hn1, dproj_sw, TN, d, SW_COLS, t, name="g_in_sw")[0]
    tok = comm.grads("in", dict(w_in_hg=g_in_hg, w_in_sw=g_in_sw))
    dhn1_a = _mm(dproj_hg, w_in_hg, NT, t, d, HG_COLS, name="d_hn1_hg", tk=HG_COLS, after=tok)[0]
    grad_x, g_norm1 = _mm(dproj_sw, w_in_sw, NT, t, d, SW_COLS, name="d_hn1_sw", tk=SW_COLS, extras=(dhn1_a, x, dh1),
                          rows=(small["norm1_g"],), row_sums=1,
                          epilogue=lambda acc, prev, xv, dres, g: _rms_bwd_residual(acc + prev, xv, dres, g)[1:])

    g_small = dict(norm1_g=g_norm1, hg_lower_bounds=g_lb, hg_norm_g=g_hgn, sw_q_norm_g=g_swq, sw_k_norm_g=g_swk,
                   sw_sinks=g_sinks[:, 0:SW_HEADS], norm2_g=g_norm2, mem_norm_g=g_memn, xa_q_norm_g=g_xq,
                   xa_k_norm_g=g_xk, norm3_g=g_norm3)
    return loss_row, grad_x.reshape(bsz, seq, d), g_small


SMALL_NAMES = ("norm1_g", "hg_lower_bounds", "hg_norm_g", "sw_q_norm_g", "sw_k_norm_g", "sw_sinks", "norm2_g",
               "mem_norm_g", "xa_q_norm_g", "xa_k_norm_g", "norm3_g")
BIG_NAMES = ("w_in", "w_out", "xa_wq", "xa_wkv", "xa_wo", "mlp_up", "mlp_down")
WEIGHT_ORDER = ("norm1_g", "w_in", "hg_lower_bounds", "hg_norm_g", "sw_q_norm_g", "sw_k_norm_g", "sw_sinks", "w_out",
                "norm2_g", "mem_norm_g", "xa_wq", "xa_wkv", "xa_q_norm_g", "xa_k_norm_g", "xa_wo", "norm3_g",
                "mlp_up", "mlp_down")


def _pack_rows(vals, width):
    starts, at = [], 0
    for v in vals:
        starts.append(at)
        at += v.shape[0]
    total = at + (-at) % 8
    out = None
    for v, s in zip(vals, starts):
        placed = jnp.pad(v, ((s, total - s - v.shape[0]), (0, width - v.shape[1])))
        out = placed if out is None else out + placed
    return out, starts


class _MeshWeights:
    LATE = ("w_out", "xa_wq", "xa_wkv", "xa_wo", "mlp_up", "mlp_down")

    def __init__(self, shards, d, ff):
        self.shards, self.d, self.ff = shards, d, ff
        self.c_idx = lax.axis_index("c").astype(jnp.int32).reshape(1)
        chip = (2 * lax.axis_index("x") + lax.axis_index("y")).astype(jnp.int32)
        self.place_idx = jnp.stack([chip, lax.axis_index("c").astype(jnp.int32)])
        self.pending = []
        self.exchanging = None

    def begin(self):
        shard = self.shards["w_in"]
        start, self.in_wait = _gather_chips_split(
            "gather_in", [shard], [_place_shard(shard, self.place_idx, name="place_w_in")])
        self.in_state = start()
        tok = (self.in_state["token"],)
        self.placed = [_place_shard(self.shards[n], self.place_idx, name="place_" + n, after=tok) for n in self.LATE]
        return tok

    def first(self, after):
        _, lands = self.in_wait(self.in_state, (after, *self.placed))
        (g_in,) = _gather_finish(lands, "gather_in_finish")
        start, self.late_wait = _gather_chips_split("gather_late", [self.shards[n] for n in self.LATE], self.placed)
        self.late_state = start(after=(g_in,))
        ws = g_in.shape[2]
        cut = HG_COLS - 2 * ws
        return dict(w_in_hg=jnp.concatenate([g_in[0], g_in[1], g_in[2][:, :cut]], axis=1),
                    w_in_sw=jnp.concatenate([g_in[2][:, cut:], g_in[3]], axis=1), token=self.late_state["token"])

    def rest(self, after):
        _, lands = self.late_wait(self.late_state, (after,))
        g_out, g_q, g_kv, g_o = _gather_finish(lands[:4], "gather_late_finish")
        start, self.mlp_wait = _gather_forward_split("gather_mlp_forward", lands[4:])
        self.mlp_state = start(after=(g_out,))
        d = self.d
        return dict(w_out=g_out.reshape(-1, d), wq=g_q.reshape(d, -1), wkv=g_kv.reshape(d, -1),
                    wo=jnp.concatenate([g_o[k] for k in range(4)], axis=1), token=self.mlp_state["token"])

    def mlp(self, after):
        _, (g_up, g_dn) = self.mlp_wait(self.mlp_state, (after,))
        return dict(up=g_up, down=g_dn.reshape(self.ff, self.d))

    def _scatter(self, tag, names, arrays, recv):
        parts = [_add_halves(g, r, self.c_idx, name="rs_add_halves_" + n) for n, g, r in zip(names, arrays, recv)]
        start, wait = _scatter_chips_split("rs_scatter_" + tag, parts)
        state = start()
        self.pending.append((names, wait, state))
        return state["token"]

    def _advance(self, after):
        if self.exchanging is None:
            return ()
        tag, names, wait, state = self.exchanging
        self.exchanging = None
        arrays, recv = wait(state, (after,))
        return (self._scatter(tag, names, arrays, recv),)

    def poll(self, after):
        return self._advance(after)

    def grads(self, tag, g):
        d, ff = self.d, self.ff
        if tag == "mlp":
            names, arrays = ("mlp_up", "mlp_down"), [g["up"], g["down"].reshape(4, ff // 4, d)]
        elif tag == "mid":
            names = ("w_out", "xa_wq", "xa_wkv", "xa_wo")
            ds = d // 4
            g_wo = jnp.stack([g["wo"][:, ds * k:ds * (k + 1)] for k in range(4)])
            arrays = [g["w_out"].reshape(4, -1, d), g["wq"].reshape(4, d // 4, -1), g["wkv"].reshape(4, d // 4, -1), g_wo]
        else:
            hg, sw = g["w_in_hg"], g["w_in_sw"]
            ws = (hg.shape[1] + sw.shape[1]) // 4
            cut = hg.shape[1] - 2 * ws
            names = ("w_in",)
            arrays = [jnp.stack([hg[:, :ws], hg[:, ws:2 * ws], jnp.concatenate([hg[:, 2 * ws:], sw[:, :ws - cut]], axis=1),
                                 sw[:, ws - cut:]])]
        toks = self._advance(arrays[0])
        if tag == "in":
            return toks + (self._scatter(tag, names, arrays, _exchange_halves(arrays, "rs_exchange_" + tag)),)
        start, wait = _exchange_halves_split("rs_exchange_" + tag, arrays)
        state = start()
        self.exchanging = (tag, names, wait, state)
        return toks + (state["token"],)

    def finish(self, after):
        joins, tok = [], ()
        for names, wait, state in self.pending:
            srcs, lands = wait(state, tuple(after) + tok)
            fulls = [_add_chips(p, r, self.place_idx, name="rs_add_chips_" + n, after=tok)
                     for n, p, r in zip(names, srcs, lands)]
            start, jwait = _join_halves_split("rs_join_" + names[0], fulls)
            jstate = start()
            tok = (jstate["token"],)
            joins.append((names, jwait, jstate))
        out = {}
        for names, jwait, jstate in joins:
            _, fulls = jwait(jstate, tok)
            out.update(zip(names, fulls))
        return out


def kernel(x, mem, positions, norm1_g, w_in, hg_lower_bounds, hg_norm_g, sw_q_norm_g, sw_k_norm_g, sw_sinks, w_out, norm2_g, mem_norm_g, xa_wq, xa_wkv, xa_q_norm_g, xa_k_norm_g, xa_wo, norm3_g, mlp_up, mlp_down, loss_target, m_norm1_g, m_w_in, m_hg_lower_bounds, m_hg_norm_g, m_sw_q_norm_g, m_sw_k_norm_g, m_sw_sinks, m_w_out, m_norm2_g, m_mem_norm_g, m_xa_wq, m_xa_wkv, m_xa_q_norm_g, m_xa_k_norm_g, m_xa_wo, m_norm3_g, m_mlp_up, m_mlp_down, v_norm1_g, v_w_in, v_hg_lower_bounds, v_hg_norm_g, v_sw_q_norm_g, v_sw_k_norm_g, v_sw_sinks, v_w_out, v_norm2_g, v_mem_norm_g, v_xa_wq, v_xa_wkv, v_xa_q_norm_g, v_xa_k_norm_g, v_xa_wo, v_norm3_g, v_mlp_up, v_mlp_down):
    given = dict(locals())
    weights = {n: given[n] for n in WEIGHT_ORDER}
    moms = {n: given["m_" + n] for n in WEIGHT_ORDER}
    vars_ = {n: given["v_" + n] for n in WEIGHT_ORDER}
    d = x.shape[-1]
    ff = mlp_down.shape[1] * 4
    small = {n: weights[n] for n in SMALL_NAMES}

    comm = _MeshWeights({n: weights[n][0].astype(_MXU_DTYPE) for n in BIG_NAMES}, d, ff)
    loss_row, grad_x, g_small = _local_step(x, mem, positions, loss_target, small, comm)
    packed, starts = _pack_rows([g_small[n] for n in SMALL_NAMES] + [loss_row], 1024)
    start, wait = _all_gather_small_split(packed)
    state = start()
    big_grads = comm.finish((grad_x, state["token"]))
    (own,), (gathered,) = wait(state, (big_grads[BIG_NAMES[0]],))
    device = (4 * lax.axis_index("x") + 2 * lax.axis_index("y") + lax.axis_index("c")).astype(jnp.int32).reshape(1)
    summed = _sum_devices(own, gathered, device)
    small_grads = {}
    for n, s in zip(SMALL_NAMES, starts):
        r, c = weights[n].shape
        small_grads[n] = summed[s:s + r, 0:c]
    loss = summed[starts[-1], 0]

    grads, deltas, new_m, new_v = {}, {}, {}, {}
    for n in BIG_NAMES:
        shp = weights[n].shape
        g2, dl, mo, vo = _adamw_big(weights[n][0], big_grads[n], moms[n][0], vars_[n][0], name="adamw_" + n)
        grads[n], deltas[n], new_m[n], new_v[n] = (a.reshape(shp) for a in (g2, dl, mo, vo))
    sm_out = _adamw_small([weights[n] for n in SMALL_NAMES], [small_grads[n] for n in SMALL_NAMES],
                          [moms[n] for n in SMALL_NAMES], [vars_[n] for n in SMALL_NAMES])
    ns = len(SMALL_NAMES)
    for i, n in enumerate(SMALL_NAMES):
        grads[n], deltas[n], new_m[n], new_v[n] = small_grads[n], sm_out[i], sm_out[ns + i], sm_out[2 * ns + i]

    return (loss, grad_x, *[grads[n] for n in WEIGHT_ORDER], *[deltas[n] for n in WEIGHT_ORDER],
            *[new_m[n] for n in WEIGHT_ORDER], *[new_v[n] for n in WEIGHT_ORDER])
```

```python
import numpy as np
import jax
import jax.numpy as jnp
from jax import lax
from jax.experimental import pallas as pl
from jax.experimental.pallas import tpu as pltpu

F32 = jnp.float32
_MXU_DTYPE = jnp.bfloat16

EPS = 1e-6
HG_HEADS = 4
HG_D = 128
HG_CHUNK = 64
HG_TILE = 512
HG_LEVELS = (32, 16, 8, 4, 2, 1)
SW_HEADS = 8
SW_KV_HEADS = 2
SW_GROUP = SW_HEADS // SW_KV_HEADS
SW_HD = 64
SW_BLOCK = 128
ROPE_THETA = 500000.0
ROT_DIM = SW_HD // 4
XA_HEADS = 4
XA_HD = 128
HG_COLS = 4 * HG_HEADS * HG_D
SW_COLS = (SW_HEADS + 2 * SW_KV_HEADS) * SW_HD

ADAM_LR = 0.001
ADAM_B1 = 0.9
ADAM_B2 = 0.999
ADAM_EPS = 1e-08
ADAM_WD = 0.01
ADAM_STEP = 10

VMEM_LIMIT = 56 * 1024 * 1024
MESH = pl.DeviceIdType.MESH

NN = ((1,), (0,))
NT = ((1,), (1,))
TN = ((0,), (0,))


def _mx(v):
    return v.astype(_MXU_DTYPE)


def _dot(a, b, dims=NN):
    return lax.dot_general(_mx(a), _mx(b), (dims, ((), ())), preferred_element_type=F32)


def _split_dot(a, v, dims, parts):
    acc = None
    rest = v
    for p in range(parts):
        piece = _mx(rest)
        term = lax.dot_general(a, piece, (dims, ((), ())), preferred_element_type=F32)
        acc = term if acc is None else acc + term
        if p + 1 < parts:
            rest = rest - piece.astype(F32)
    return acc


def _params(sem):
    return pltpu.CompilerParams(dimension_semantics=sem, vmem_limit_bytes=VMEM_LIMIT)


def _mm(a, b, mode, m, n, k, *, name, tm=1024, tn=1024, tk=1024, a_spec=None, b_spec=None, extras=(), rows=(),
        epilogue=None, out_dtypes=(F32,), row_sums=0, out_shape=None, out_spec=None, after=()):
    after = tuple(t for t in after if t is not None)
    tm, tn, tk = min(tm, m), min(tn, n), min(tk, k)
    assert m % tm == 0 and n % tn == 0 and k % tk == 0, (name, m, n, k, tm, tn, tk)
    gi, gj, gk = m // tm, n // tn, k // tk
    assert row_sums == 0 or gj == 1, name
    if a_spec is None:
        a_spec = (pl.BlockSpec((tk, tm), lambda i, j, kk: (kk, i)) if mode == TN
                  else pl.BlockSpec((tm, tk), lambda i, j, kk: (i, kk)))
    if b_spec is None:
        b_spec = (pl.BlockSpec((tn, tk), lambda i, j, kk: (j, kk)) if mode == NT
                  else pl.BlockSpec((tk, tn), lambda i, j, kk: (kk, j)))
    mn_spec = pl.BlockSpec((tm, tn), lambda i, j, kk: (i, j))
    if epilogue is None:
        epilogue = lambda acc: (acc,)
    row_spec = pl.BlockSpec((1, tn), lambda i, j, kk: (0, j))
    n_ex, n_out = len(extras) + len(rows), len(out_dtypes)
    if out_shape is None:
        out_shape = tuple(jax.ShapeDtypeStruct((m, n), d) for d in out_dtypes)
        out_spec = tuple(mn_spec for _ in out_dtypes)
    out_shape = tuple(out_shape) + tuple(jax.ShapeDtypeStruct((1, n), F32) for _ in range(row_sums))
    out_spec = tuple(out_spec) + tuple(row_spec for _ in range(row_sums))

    n_after = len(after)

    def body(*refs):
        a_ref, b_ref = refs[0], refs[1]
        ex = refs[2:2 + n_ex]
        outs = refs[2 + n_ex + n_after:2 + n_ex + n_after + n_out + row_sums]
        first_row_tile = pl.program_id(0) == 0

        def finish(acc):
            res = epilogue(acc, *[e[...] for e in ex])
            for o, r in zip(outs[:n_out], res[:n_out]):
                o[...] = r.astype(o.dtype)
            if row_sums:
                @pl.when(first_row_tile)
                def _():
                    for o in outs[n_out:]:
                        o[...] = jnp.zeros_like(o)

                for o, r in zip(outs[n_out:], res[n_out:]):
                    o[...] += r

        if gk == 1:
            finish(_dot(a_ref[...], b_ref[...], mode))
        else:
            acc_ref = refs[-1]
            kk = pl.program_id(2)

            @pl.when(kk == 0)
            def _():
                acc_ref[...] = jnp.zeros_like(acc_ref)

            acc_ref[...] += _dot(a_ref[...], b_ref[...], mode)

            @pl.when(kk == gk - 1)
            def _():
                finish(acc_ref[...])

    return pl.pallas_call(
        body, name=name, grid=(gi, gj, gk),
        in_specs=([a_spec, b_spec] + [mn_spec] * len(extras) + [row_spec] * len(rows)
                  + [pl.BlockSpec(memory_space=pl.ANY)] * n_after),
        out_specs=out_spec, out_shape=out_shape,
        scratch_shapes=[pltpu.VMEM((tm, tn), F32)] if gk > 1 else [],
        compiler_params=_params(("arbitrary" if row_sums else "parallel", "parallel", "arbitrary")),
    )(a, b, *extras, *rows, *after)


def _rms_rows(xv, g):
    return xv * lax.rsqrt(jnp.mean(xv * xv, axis=1, keepdims=True) + EPS) * g


def _rms_rows_bwd(xv, g, dyv):
    r = lax.rsqrt(jnp.mean(xv * xv, axis=1, keepdims=True) + EPS)
    u = dyv * g
    return (r * u - xv * (r * r * r) * jnp.mean(u * xv, axis=1, keepdims=True),
            jnp.sum(dyv * xv * r, axis=0, keepdims=True))


def _residual_rms(acc, res, g):
    h = acc + res
    return h, _rms_rows(h, g)


def _rms_bwd_residual(dhn, xv, dres, g):
    dx, dg = _rms_rows_bwd(xv, g, dhn)
    dx = dx + dres
    return dx, dx, dg


def _rms_fwd(x, g, *, name, tm=512, after=()):
    t, d = x.shape
    tm = min(tm, t)
    after = tuple(a for a in after if a is not None)

    def body(x_ref, g_ref, *rest):
        rest[-1][...] = _rms_rows(x_ref[...], g_ref[...]).astype(rest[-1].dtype)

    return pl.pallas_call(
        body, name=name, grid=(t // tm,),
        in_specs=[pl.BlockSpec((tm, d), lambda i: (i, 0)), pl.BlockSpec((1, d), lambda i: (0, 0))]
        + [pl.BlockSpec(memory_space=pl.ANY)] * len(after),
        out_specs=pl.BlockSpec((tm, d), lambda i: (i, 0)),
        out_shape=jax.ShapeDtypeStruct((t, d), _MXU_DTYPE),
        compiler_params=_params(("parallel",)),
    )(x, g, *after)


def _rms_gain_grad(x, g, dy, *, name, tm=512):
    t, d = x.shape
    tm = min(tm, t)

    def body(x_ref, g_ref, dy_ref, dg_ref):
        @pl.when(pl.program_id(0) == 0)
        def _():
            dg_ref[...] = jnp.zeros_like(dg_ref)

        dg_ref[...] += _rms_rows_bwd(x_ref[...], g_ref[...], dy_ref[...])[1]

    row = pl.BlockSpec((tm, d), lambda i: (i, 0))
    vec = pl.BlockSpec((1, d), lambda i: (0, 0))
    return pl.pallas_call(
        body, name=name, grid=(t // tm,), in_specs=[row, vec, row], out_specs=vec,
        out_shape=jax.ShapeDtypeStruct((1, d), F32), compiler_params=_params(("arbitrary",)),
    )(x, g, dy)


def _hg_constants():
    c = HG_CHUNK
    t = np.arange(c)
    sums = [t[None, :] <= t[:, None]]
    masks = []
    for m in HG_LEVELS:
        base = (t // (2 * m)) * (2 * m)
        mid = base + m - 1
        second = (t - base) >= m
        upper = (t[None, :] > mid[:, None]) & (t[None, :] <= t[:, None])
        lower = (t[None, :] > t[:, None]) & (t[None, :] <= mid[:, None])
        sums.append(np.where(second[:, None], upper, lower))
        masks.append(second[:, None] & (~second)[None, :] & (base[:, None] == base[None, :]))
    return (np.concatenate(sums, axis=0).astype(np.float32), np.stack(masks).astype(np.float32))


HG_HEAD_LANES = tuple(slice(HG_D * h, HG_D * (h + 1)) for h in range(HG_HEADS))


def _per_head(fn, slab):
    return jnp.concatenate([jnp.broadcast_to(fn(slab[:, hs]), (slab.shape[0], HG_D)) for hs in HG_HEAD_LANES], axis=1)


def _lane_sum(v):
    return jnp.sum(v, axis=1, keepdims=True)


def _lane_mean(v):
    return jnp.mean(v, axis=1, keepdims=True)


def _hg_gates(blk, lbp):
    w = HG_HEADS * HG_D
    q, x, v, gl = blk[:, 0:w], blk[:, w:2 * w], blk[:, 2 * w:3 * w], blk[:, 3 * w:4 * w]
    mx = jnp.max(lbp, axis=0, keepdims=True)
    e = jnp.exp(lbp - mx)
    lb = e[0:1, :] / jnp.sum(e, axis=0, keepdims=True)
    sig = jax.nn.sigmoid(x)
    f = lb + (1.0 - lb) * sig
    return q, v, gl, lb, sig, f, 1.0 - f, jnp.log(f)


def _hg_fwd(proj, lbp, ng, bsz, seq, *, y_width):
    t = proj.shape[0]
    nc = seq // HG_CHUNK
    a_np, m_np = _hg_constants()
    a_all = jnp.asarray(a_np, _MXU_DTYPE)
    masks = jnp.asarray(m_np, F32)
    nl = len(HG_LEVELS)

    ts = min(HG_TILE, seq)
    ns, nct = seq // ts, ts // HG_CHUNK
    hw = HG_HEADS * HG_D

    def body(p_ref, lb_ref, ng_ref, a_ref, m_ref, y_ref, o_ref, st_ref, carry):
        a_mat = a_ref[...]
        ngv = ng_ref[...]

        @pl.when(pl.program_id(0) == 0)
        def _():
            carry[...] = jnp.zeros_like(carry)

        ng4 = _tile_lanes(ngv, HG_HEADS)
        heads = range(HG_HEADS)
        exs = range(bsz)
        hl = HG_HEAD_LANES
        lbp_v = lb_ref[...]

        def chunk(c, _):
            rows = pl.ds(pl.multiple_of(c * HG_CHUNK, HG_CHUNK), HG_CHUNK)
            gates = [_hg_gates(p_ref[e, rows, :], lbp_v) for e in exs]
            q, v, gl = [g[0] for g in gates], [g[1] for g in gates], [g[2] for g in gates]
            k = [g[6] for g in gates]
            sts = [[carry[e, h] for h in heads] for e in exs]
            e_all = [_split_dot(a_mat, gates[e][7], NN, 3) for e in exs]
            b = [e_all[e][0:HG_CHUNK] for e in exs]
            qb = [q[e] * jnp.exp(b[e]) for e in exs]
            o = [[_dot(qb[e][:, hl[h]], sts[e][h], NT) for h in heads] for e in exs]
            p = [[jnp.zeros((HG_CHUNK, HG_CHUNK), F32) for _ in heads] for _ in exs]
            for li in range(nl):
                dec = [jnp.exp(e_all[e][HG_CHUNK * (li + 1):HG_CHUNK * (li + 2)]) for e in exs]
                qm, km, mk = [q[e] * dec[e] for e in exs], [k[e] * dec[e] for e in exs], m_ref[li]
                p = [[p[e][h] + mk * _dot(qm[e][:, hl[h]], km[e][:, hl[h]], NT) for h in heads] for e in exs]
            bl = [b[e][HG_CHUNK - 1:HG_CHUNK, :] for e in exs]
            kd = [k[e] * jnp.exp(bl[e] - b[e]) for e in exs]
            pv = [[_dot(p[e][h], v[e][:, hl[h]]) for h in heads] for e in exs]
            upd = [[_dot(v[e][:, hl[h]], kd[e][:, hl[h]], TN) for h in heads] for e in exs]
            for e in exs:
                o_all = (jnp.concatenate([o[e][h] + pv[e][h] for h in heads], axis=1)
                         + _per_head(_lane_sum, q[e] * k[e]) * v[e])
                r = lax.rsqrt(_per_head(_lane_mean, o_all * o_all) + EPS)
                ebl = jnp.exp(bl[e])
                for h in heads:
                    st_ref[e, h, c] = sts[e][h]
                    carry[e, h] = sts[e][h] * ebl[:, hl[h]] + upd[e][h]
                o_ref[e, rows, :] = o_all
                y_ref[e, rows, :] = (o_all * r * ng4) * (gl[e] * jax.nn.sigmoid(gl[e]))
            return 0

        lax.fori_loop(0, nct, chunk, 0)

    y3, o3, states = pl.pallas_call(
        body, name="hgrn2_fwd", grid=(ns,),
        in_specs=[pl.BlockSpec((bsz, ts, HG_COLS), lambda s: (0, s, 0)),
                  pl.BlockSpec((2, hw), lambda s: (0, 0)),
                  pl.BlockSpec((1, HG_D), lambda s: (0, 0)),
                  pl.BlockSpec(a_all.shape, lambda s: (0, 0)),
                  pl.BlockSpec(masks.shape, lambda s: (0, 0, 0))],
        out_specs=(pl.BlockSpec((bsz, ts, hw), lambda s: (0, s, 0)),
                   pl.BlockSpec((bsz, ts, hw), lambda s: (0, s, 0)),
                   pl.BlockSpec((bsz, HG_HEADS, nct, HG_D, HG_D), lambda s: (0, 0, s, 0, 0))),
        out_shape=(jax.ShapeDtypeStruct((bsz, seq, y_width), F32),
                   jax.ShapeDtypeStruct((bsz, seq, hw), F32),
                   jax.ShapeDtypeStruct((bsz, HG_HEADS, nc, HG_D, HG_D), F32)),
        scratch_shapes=[pltpu.VMEM((bsz, HG_HEADS, HG_D, HG_D), F32)],
        compiler_params=_params(("arbitrary",)),
    )(proj.reshape(bsz, seq, HG_COLS), lbp, ng, a_all, masks)
    return y3.reshape(t, y_width), o3.reshape(t, hw), states


def _hg_bwd(proj, lbp, ng, o_all, states, dy, bsz, seq, after=()):
    after = tuple(a for a in after if a is not None)
    t = proj.shape[0]
    nc = seq // HG_CHUNK
    a_np, m_np = _hg_constants()
    a_all = jnp.asarray(a_np, _MXU_DTYPE)
    masks = jnp.asarray(m_np, F32)
    nl = len(HG_LEVELS)
    cs = HG_CHUNK

    ts = min(HG_TILE, seq)
    ns, nct = seq // ts, ts // cs
    hw = HG_HEADS * HG_D

    def body(p_ref, lb_ref, ng_ref, a_ref, m_ref, o_ref, st_ref, dy_ref, *rest):
        dp_ref, dlb_ref, dng_ref, dst_ref = rest[len(after):]
        a_mat = a_ref[...]
        ngv = ng_ref[...]
        ng4 = _tile_lanes(ngv, HG_HEADS)
        last_row = lax.broadcasted_iota(jnp.int32, (cs, hw), 0) == cs - 1
        first = pl.program_id(0) == 0
        heads = range(HG_HEADS)
        exs = range(bsz)
        hl = HG_HEAD_LANES
        lbp_v = lb_ref[...]

        @pl.when(first)
        def _():
            dst_ref[...] = jnp.zeros_like(dst_ref)

        def side_by_side(parts):
            return jnp.concatenate(parts, axis=1)

        def chunk(i, carry):
            dlb_acc, dng_acc = carry
            c = nct - 1 - i
            rows = pl.ds(pl.multiple_of(c * cs, cs), cs)
            gates = [_hg_gates(p_ref[e, rows, :], lbp_v) for e in exs]
            q, v, gl = [g[0] for g in gates], [g[1] for g in gates], [g[2] for g in gates]
            lb, sig, f, k = gates[0][3], [g[4] for g in gates], [g[5] for g in gates], [g[6] for g in gates]
            o = [o_ref[e, rows, :] for e in exs]
            dyv = [dy_ref[e, rows, :] for e in exs]
            sts = [[st_ref[e, h, c] for h in heads] for e in exs]
            dsts = [[dst_ref[e, h] for h in heads] for e in exs]
            e_all = [_split_dot(a_mat, gates[e][7], NN, 3) for e in exs]
            b = [e_all[e][0:cs] for e in exs]
            eb = [jnp.exp(b[e]) for e in exs]
            bl = [b[e][cs - 1:cs, :] for e in exs]
            ebl = [jnp.exp(bl[e]) for e in exs]
            ekd = [jnp.exp(bl[e] - b[e]) for e in exs]
            qb = [q[e] * eb[e] for e in exs]
            kd = [k[e] * ekd[e] for e in exs]
            do, dgl = [], []
            for e in exs:
                sg = jax.nn.sigmoid(gl[e])
                silu = gl[e] * sg
                r = lax.rsqrt(_per_head(_lane_mean, o[e] * o[e]) + EPS)
                dgl.append(dyv[e] * (o[e] * r * ng4) * (sg * (1.0 + gl[e] * (1.0 - sg))))
                u = dyv[e] * silu * ng4
                do.append(r * u - o[e] * (r * r * r) * _per_head(_lane_mean, u * o[e]))
                dng4 = jnp.sum(dyv[e] * silu * o[e] * r, axis=0, keepdims=True)
                dng_acc = dng_acc + ((dng4[:, hl[0]] + dng4[:, hl[1]]) + (dng4[:, hl[2]] + dng4[:, hl[3]]))
            es, qm, km = [], [], []
            p = [[jnp.zeros((cs, cs), F32) for _ in heads] for _ in exs]
            for li in range(nl):
                dec = [jnp.exp(e_all[e][cs * (li + 1):cs * (li + 2)]) for e in exs]
                es.append(dec)
                qm.append([q[e] * dec[e] for e in exs])
                km.append([k[e] * dec[e] for e in exs])
                mk = m_ref[li]
                p = [[p[e][h] + mk * _dot(qm[li][e][:, hl[h]], km[li][e][:, hl[h]], NT) for h in heads] for e in exs]
            dp = [[_dot(do[e][:, hl[h]], v[e][:, hl[h]], NT) for h in heads] for e in exs]
            dv_p = [[_dot(p[e][h], do[e][:, hl[h]], TN) for h in heads] for e in exs]
            dv_s = [[_dot(kd[e][:, hl[h]], dsts[e][h], NT) for h in heads] for e in exs]
            dqb = [side_by_side([_dot(do[e][:, hl[h]], sts[e][h]) for h in heads]) for e in exs]
            dkd = [side_by_side([_dot(v[e][:, hl[h]], dsts[e][h]) for h in heads]) for e in exs]
            new_dst = [[_dot(do[e][:, hl[h]], qb[e][:, hl[h]], TN) for h in heads] for e in exs]
            dv = [side_by_side([dv_p[e][h] + dv_s[e][h] for h in heads]) + _per_head(_lane_sum, q[e] * k[e]) * do[e]
                  for e in exs]
            dq = [dqb[e] * eb[e] for e in exs]
            dk = [dkd[e] * ekd[e] for e in exs]
            de = []
            for e in exs:
                dbl = (jnp.sum(dkd[e] * kd[e], axis=0, keepdims=True)
                       + side_by_side([jnp.sum(dsts[e][h] * sts[e][h], axis=0, keepdims=True) for h in heads]) * ebl[e])
                de.append([dqb[e] * qb[e] - dkd[e] * kd[e] + jnp.where(last_row, dbl, 0.0)])
            for li in range(nl):
                mk = m_ref[li]
                dpm = [[mk * dp[e][h] for h in heads] for e in exs]
                dqm = [side_by_side([_dot(dpm[e][h], km[li][e][:, hl[h]]) for h in heads]) for e in exs]
                dkm = [side_by_side([_dot(dpm[e][h], qm[li][e][:, hl[h]], TN) for h in heads]) for e in exs]
                for e in exs:
                    dq[e] = dq[e] + dqm[e] * es[li][e]
                    dk[e] = dk[e] + dkm[e] * es[li][e]
                    de[e].append(dqm[e] * qm[li][e] + dkm[e] * km[li][e])
            dg = [_split_dot(a_mat, jnp.concatenate(de[e], axis=0), TN, 2) for e in exs]
            for e in exs:
                dpd = _per_head(_lane_sum, do[e] * v[e])
                df = dg[e] / f[e] - (dk[e] + dpd * q[e])
                dp_ref[e, rows, 0:hw] = _mx(dq[e] + dpd * k[e])
                dp_ref[e, rows, hw:2 * hw] = _mx(df * (1.0 - lb) * sig[e] * (1.0 - sig[e]))
                dp_ref[e, rows, 2 * hw:3 * hw] = _mx(dv[e])
                dp_ref[e, rows, 3 * hw:4 * hw] = _mx(dgl[e])
                for h in heads:
                    dst_ref[e, h] = dsts[e][h] * ebl[e][:, hl[h]] + new_dst[e][h]
                dlb_acc = dlb_acc + jnp.sum(df * (1.0 - sig[e]), axis=0, keepdims=True)
            return dlb_acc, dng_acc

        dlb, dng = lax.fori_loop(0, nct, chunk, (jnp.zeros((1, hw), F32), jnp.zeros((1, HG_D), F32)))

        @pl.when(first)
        def _():
            dlb_ref[...] = jnp.zeros_like(dlb_ref)
            dng_ref[...] = jnp.zeros_like(dng_ref)

        mx = jnp.max(lbp_v, axis=0, keepdims=True)
        e = jnp.exp(lbp_v - mx)
        s0 = e[0:1, :] / jnp.sum(e, axis=0, keepdims=True)
        da0 = dlb * s0 * (1.0 - s0)
        dlb_ref[...] += jnp.concatenate([da0, -da0], axis=0)
        dng_ref[...] += dng

    rows3 = lambda w: pl.BlockSpec((bsz, ts, w), lambda s: (0, ns - 1 - s, 0))
    dproj, dlb, dng = pl.pallas_call(
        body, name="hgrn2_bwd", grid=(ns,),
        in_specs=[rows3(HG_COLS),
                  pl.BlockSpec((2, hw), lambda s: (0, 0)),
                  pl.BlockSpec((1, HG_D), lambda s: (0, 0)),
                  pl.BlockSpec(a_all.shape, lambda s: (0, 0)),
                  pl.BlockSpec(masks.shape, lambda s: (0, 0, 0)),
                  rows3(hw),
                  pl.BlockSpec((bsz, HG_HEADS, nct, HG_D, HG_D), lambda s: (0, 0, ns - 1 - s, 0, 0)),
                  rows3(hw)] + [pl.BlockSpec(memory_space=pl.ANY)] * len(after),
        out_specs=(rows3(HG_COLS),
                   pl.BlockSpec((2, hw), lambda s: (0, 0)),
                   pl.BlockSpec((1, HG_D), lambda s: (0, 0))),
        out_shape=(jax.ShapeDtypeStruct((bsz, seq, HG_COLS), _MXU_DTYPE),
                   jax.ShapeDtypeStruct((2, hw), F32),
                   jax.ShapeDtypeStruct((1, HG_D), F32)),
        scratch_shapes=[pltpu.VMEM((bsz, HG_HEADS, HG_D, HG_D), F32)],
        compiler_params=_params(("arbitrary",)),
    )(proj.reshape(bsz, seq, HG_COLS), lbp, ng, a_all, masks, o_all.reshape(bsz, seq, hw), states,
      dy.reshape(bsz, seq, dy.shape[1]), *after)
    return dproj.reshape(t, HG_COLS), dlb, dng


def _sw_constants():
    half = ROT_DIM // 2
    inv = (np.float32(ROPE_THETA) ** (-(np.arange(half, dtype=np.float32) * np.float32(2.0) / np.float32(ROT_DIM)))
           ).astype(np.float32)
    freq = np.zeros((1, 128), np.float32)
    sign = np.zeros((1, 128), np.float32)
    for h in range(2):
        freq[0, 64 * h:64 * h + half] = inv
        freq[0, 64 * h + half:64 * h + 2 * half] = inv
        sign[0, 64 * h:64 * h + half] = -1.0
        sign[0, 64 * h + half:64 * h + 2 * half] = 1.0
    seg = np.kron(np.eye(8, dtype=np.float32), np.full((64, 64), 1.0 / 64.0, np.float32))
    return freq, sign, seg


def _rope_table(pos, *, tm=512):
    t = pos.shape[0]
    tm = min(tm, t)
    freq_np, sign_np, _ = _sw_constants()

    def body(p_ref, f_ref, s_ref, o_ref):
        ang = p_ref[...].astype(F32) * f_ref[...]
        o_ref[:, 0:128] = jnp.cos(ang)
        o_ref[:, 128:256] = jnp.sin(ang) * s_ref[...]

    vec = pl.BlockSpec((1, 128), lambda i: (0, 0))
    return pl.pallas_call(
        body, name="rope_table", grid=(t // tm,),
        in_specs=[pl.BlockSpec((tm, 1), lambda i: (i, 0)), vec, vec],
        out_specs=pl.BlockSpec((tm, 256), lambda i: (i, 0)),
        out_shape=jax.ShapeDtypeStruct((t, 256), F32),
        compiler_params=_params(("parallel",)),
    )(pos, jnp.asarray(freq_np), jnp.asarray(sign_np))


def _tile_lanes(v, times):
    return v if times == 1 else jnp.concatenate([v] * times, axis=1)


def _swap_halves(v):
    w = v.shape[1]
    half = ROT_DIM // 2
    lane = lax.broadcasted_iota(jnp.int32, v.shape, 1) % SW_HD
    return jnp.where(lane < half, pltpu.roll(v, w - half, 1), jnp.where(lane < 2 * half, pltpu.roll(v, half, 1), 0.0))


def _sw_norm_rope(tv, gain, seg, cosv, sinv):
    w = tv.shape[1]
    ms = _split_dot_rhs(tv * tv, seg[0:w, 0:w])
    r = lax.rsqrt(ms + EPS)
    tn = tv * r * gain
    reps = w // 128
    return tn * _tile_lanes(cosv, reps) + _swap_halves(tn) * _tile_lanes(sinv, reps), r


def _split_dot_rhs(v, a):
    hi = _mx(v)
    lo = _mx(v - hi.astype(F32))
    return (lax.dot_general(hi, a, (NN, ((), ())), preferred_element_type=F32)
            + lax.dot_general(lo, a, (NN, ((), ())), preferred_element_type=F32))


def _sw_norm_rope_bwd(dt, tv, r, gain, seg, cosv, sinv):
    w = tv.shape[1]
    reps = w // 128
    dtn = dt * _tile_lanes(cosv, reps) + _swap_halves(dt * _tile_lanes(sinv, reps))
    u = dtn * gain
    dtv = r * u - tv * (r * r * r) * _split_dot_rhs(u * tv, seg[0:w, 0:w])
    return dtv, jnp.sum(dtn * tv * r, axis=0, keepdims=True)


def _sw_scores(qh, kp, kc):
    return _dot(qh, kp, NT), _dot(qh, kc, NT)


def _sw_probs(raw, sink, first_block):
    scale = SW_HD ** -0.5
    qi = lax.broadcasted_iota(jnp.int32, (SW_BLOCK, SW_BLOCK), 0)
    kj = lax.broadcasted_iota(jnp.int32, (SW_BLOCK, SW_BLOCK), 1)
    ok_prev = jnp.logical_and(kj > qi, jnp.logical_not(first_block))
    ok_cur = kj <= qi
    sp = jnp.where(ok_prev, raw[0] * scale, -jnp.inf)
    sc = jnp.where(ok_cur, raw[1] * scale, -jnp.inf)
    m = jnp.maximum(jnp.maximum(jnp.max(sp, axis=1, keepdims=True), jnp.max(sc, axis=1, keepdims=True)), sink)
    pp, pc = jnp.exp(sp - m), jnp.exp(sc - m)
    es = jnp.exp(sink - m)
    den = jnp.sum(pp, axis=1, keepdims=True) + jnp.sum(pc, axis=1, keepdims=True) + es
    return pp / den, pc / den, es / den


def _sw_specs(nb):
    def cur(b, n):
        return b * nb + jnp.minimum(n, nb - 1)

    def prev(b, n):
        return b * nb + jnp.maximum(jnp.minimum(n, nb - 1) - 1, 0)

    return cur, prev


def _sw_fwd(proj, rope, qg, kg, sinks, y_in, bsz, seq):
    t = proj.shape[0]
    nb = seq // SW_BLOCK
    seg = jnp.asarray(_sw_constants()[2], _MXU_DTYPE)
    cur, prev = _sw_specs(nb)

    def body(q_ref, kc_ref, kp_ref, vc_ref, vp_ref, rc_ref, rp_ref, qg_ref, kg_ref, sk_ref, seg_ref, yin_ref, y_ref):
        del yin_ref
        n = pl.program_id(1)
        segv = seg_ref[...]
        cos_c, sin_c = rc_ref[:, 0:128], rc_ref[:, 128:256]
        cos_p, sin_p = rp_ref[:, 0:128], rp_ref[:, 128:256]
        qr, _ = _sw_norm_rope(q_ref[...], qg_ref[...], segv, cos_c, sin_c)
        kcr, _ = _sw_norm_rope(kc_ref[...], kg_ref[...], segv, cos_c, sin_c)
        kpr, _ = _sw_norm_rope(kp_ref[...], kg_ref[...], segv, cos_p, sin_p)
        vc, vp = vc_ref[...], vp_ref[...]
        ks = [slice(SW_HD * (h // SW_GROUP), SW_HD * (h // SW_GROUP + 1)) for h in range(SW_HEADS)]
        raw = [_sw_scores(qr[:, SW_HD * h:SW_HD * (h + 1)], kpr[:, ks[h]], kcr[:, ks[h]]) for h in range(SW_HEADS)]
        probs = [_sw_probs(raw[h], sk_ref[0, h], n == 0) for h in range(SW_HEADS)]
        for h in range(SW_HEADS):
            y_ref[:, SW_HD * h:SW_HD * (h + 1)] = _dot(probs[h][0], vp[:, ks[h]]) + _dot(probs[h][1], vc[:, ks[h]])

    rowq = pl.BlockSpec((SW_BLOCK, 512), lambda b, n: (cur(b, n), 0))
    full = lambda a: pl.BlockSpec(a.shape, lambda b, n: (0,) * a.ndim)
    yw = y_in.shape[1]
    return pl.pallas_call(
        body, name="swa_fwd", grid=(bsz, nb),
        in_specs=[rowq,
                  pl.BlockSpec((SW_BLOCK, 128), lambda b, n: (cur(b, n), 4)),
                  pl.BlockSpec((SW_BLOCK, 128), lambda b, n: (prev(b, n), 4)),
                  pl.BlockSpec((SW_BLOCK, 128), lambda b, n: (cur(b, n), 5)),
                  pl.BlockSpec((SW_BLOCK, 128), lambda b, n: (prev(b, n), 5)),
                  pl.BlockSpec((SW_BLOCK, 256), lambda b, n: (cur(b, n), 0)),
                  pl.BlockSpec((SW_BLOCK, 256), lambda b, n: (prev(b, n), 0)),
                  full(qg), full(kg),
                  pl.BlockSpec(memory_space=pltpu.SMEM),
                  full(seg),
                  pl.BlockSpec(memory_space=pl.ANY)],
        out_specs=pl.BlockSpec((SW_BLOCK, 512), lambda b, n: (cur(b, n), 1)),
        out_shape=jax.ShapeDtypeStruct((t, yw), F32),
        input_output_aliases={11: 0},
        compiler_params=_params(("parallel", "parallel")),
    )(proj, proj, proj, proj, proj, rope, rope, qg, kg, sinks, seg, y_in)


def _sw_bwd(proj, rope, qg, kg, sinks, y, dy, bsz, seq):
    t = proj.shape[0]
    nb = seq // SW_BLOCK
    seg = jnp.asarray(_sw_constants()[2], _MXU_DTYPE)
    cur, prev = _sw_specs(nb)
    scale = SW_HD ** -0.5

    def body(q_ref, kc_ref, kp_ref, vc_ref, vp_ref, rc_ref, rp_ref, qg_ref, kg_ref, sk_ref, seg_ref,
             y_ref, dy_ref, dp_ref, dqg_ref, dkg_ref, dsk_ref,
             dq_car, dkv_car, dqr_s, dkc_s, dkp_s, dvc_s, dvp_s, gq_acc, gk_acc, sk_acc):
        b, n = pl.program_id(0), pl.program_id(1)
        first = jnp.logical_and(b == 0, n == 0)
        last = jnp.logical_and(b == pl.num_programs(0) - 1, n == nb)

        @pl.when(first)
        def _():
            gq_acc[...] = jnp.zeros_like(gq_acc)
            gk_acc[...] = jnp.zeros_like(gk_acc)
            sk_acc[...] = jnp.zeros_like(sk_acc)

        @pl.when(n < nb)
        def _():
            segv = seg_ref[...]
            cos_c, sin_c = rc_ref[:, 0:128], rc_ref[:, 128:256]
            cos_p, sin_p = rp_ref[:, 0:128], rp_ref[:, 128:256]
            qv, kcv, kpv = q_ref[...], kc_ref[...], kp_ref[...]
            qr, rq = _sw_norm_rope(qv, qg_ref[...], segv, cos_c, sin_c)
            kcr, rkc = _sw_norm_rope(kcv, kg_ref[...], segv, cos_c, sin_c)
            kpr, rkp = _sw_norm_rope(kpv, kg_ref[...], segv, cos_p, sin_p)
            vc, vp = vc_ref[...], vp_ref[...]
            lane = lax.broadcasted_iota(jnp.int32, (1, 128), 1)
            dsk = jnp.zeros((1, 128), F32)
            heads = range(SW_HEADS)
            ks = [slice(SW_HD * (h // SW_GROUP), SW_HD * (h // SW_GROUP + 1)) for h in heads]
            hs = [slice(SW_HD * h, SW_HD * (h + 1)) for h in heads]
            qh = [qr[:, hs[h]] for h in heads]
            doh = [dy_ref[:, hs[h]] for h in heads]
            raw = [_sw_scores(qh[h], kpr[:, ks[h]], kcr[:, ks[h]]) for h in heads]
            dpp = [_dot(doh[h], vp[:, ks[h]], NT) for h in heads]
            dpc = [_dot(doh[h], vc[:, ks[h]], NT) for h in heads]
            probs = [_sw_probs(raw[h], sk_ref[0, h], n == 0) for h in heads]
            dsp, dsc = [], []
            for h in heads:
                pp, pc, ps = probs[h]
                delta = jnp.sum(doh[h] * y_ref[:, hs[h]], axis=1, keepdims=True)
                dsp.append(pp * (dpp[h] - delta) * scale)
                dsc.append(pc * (dpc[h] - delta) * scale)
                dsk = dsk + jnp.where(lane == h, -jnp.sum(ps * delta), 0.0)
            for h in heads:
                dqr_s[:, hs[h]] = _dot(dsp[h], kpr[:, ks[h]]) + _dot(dsc[h], kcr[:, ks[h]])
            for kv in range(SW_KV_HEADS):
                group = range(SW_GROUP * kv, SW_GROUP * (kv + 1))
                kvs = slice(SW_HD * kv, SW_HD * (kv + 1))
                dvp_s[:, kvs] = sum(_dot(probs[h][0], doh[h], TN) for h in group)
                dvc_s[:, kvs] = sum(_dot(probs[h][1], doh[h], TN) for h in group)
                dkp_s[:, kvs] = sum(_dot(dsp[h], qh[h], TN) for h in group)
                dkc_s[:, kvs] = sum(_dot(dsc[h], qh[h], TN) for h in group)
            dq, gq = _sw_norm_rope_bwd(dqr_s[...], qv, rq, qg_ref[...], segv, cos_c, sin_c)
            dkc, gkc = _sw_norm_rope_bwd(dkc_s[...], kcv, rkc, kg_ref[...], segv, cos_c, sin_c)
            dkp, gkp = _sw_norm_rope_bwd(dkp_s[...], kpv, rkp, kg_ref[...], segv, cos_p, sin_p)
            gq_acc[...] += gq
            gk_acc[...] += gkc + gkp
            sk_acc[...] += dsk

            @pl.when(n > 0)
            def _():
                dp_ref[:, 0:512] = _mx(dq_car[...])
                dp_ref[:, 512:640] = _mx(dkv_car[:, 0:128] + dkp)
                dp_ref[:, 640:768] = _mx(dkv_car[:, 128:256] + dvp_s[...])

            dq_car[...] = dq
            dkv_car[:, 0:128] = dkc
            dkv_car[:, 128:256] = dvc_s[...]

        @pl.when(n == nb)
        def _():
            dp_ref[:, 0:512] = _mx(dq_car[...])
            dp_ref[:, 512:768] = _mx(dkv_car[...])

        @pl.when(last)
        def _():
            gq = gq_acc[...]
            acc = gq[:, 0:SW_HD]
            for h in range(1, SW_HEADS):
                acc = acc + gq[:, SW_HD * h:SW_HD * (h + 1)]
            dqg_ref[...] = acc
            gk = gk_acc[...]
            dkg_ref[...] = gk[:, 0:SW_HD] + gk[:, SW_HD:2 * SW_HD]
            dsk_ref[...] = sk_acc[...]

    rowq = pl.BlockSpec((SW_BLOCK, 512), lambda b, n: (cur(b, n), 0))
    full = lambda a: pl.BlockSpec(a.shape, lambda b, n: (0,) * a.ndim)

    def out_row(b, n):
        return b * nb + jnp.maximum(n - 1, 0)

    return pl.pallas_call(
        body, name="swa_bwd", grid=(bsz, nb + 1),
        in_specs=[rowq,
                  pl.BlockSpec((SW_BLOCK, 128), lambda b, n: (cur(b, n), 4)),
                  pl.BlockSpec((SW_BLOCK, 128), lambda b, n: (prev(b, n), 4)),
                  pl.BlockSpec((SW_BLOCK, 128), lambda b, n: (cur(b, n), 5)),
                  pl.BlockSpec((SW_BLOCK, 128), lambda b, n: (prev(b, n), 5)),
                  pl.BlockSpec((SW_BLOCK, 256), lambda b, n: (cur(b, n), 0)),
                  pl.BlockSpec((SW_BLOCK, 256), lambda b, n: (prev(b, n), 0)),
                  full(qg), full(kg),
                  pl.BlockSpec(memory_space=pltpu.SMEM),
                  full(seg),
                  pl.BlockSpec((SW_BLOCK, 512), lambda b, n: (cur(b, n), 1)),
                  pl.BlockSpec((SW_BLOCK, 512), lambda b, n: (cur(b, n), 1))],
        out_specs=(pl.BlockSpec((SW_BLOCK, SW_COLS), lambda b, n: (out_row(b, n), 0)),
                   pl.BlockSpec((1, SW_HD), lambda b, n: (0, 0)),
                   pl.BlockSpec((1, SW_HD), lambda b, n: (0, 0)),
                   pl.BlockSpec((1, 128), lambda b, n: (0, 0))),
        out_shape=(jax.ShapeDtypeStruct((t, SW_COLS), _MXU_DTYPE),
                   jax.ShapeDtypeStruct((1, SW_HD), F32),
                   jax.ShapeDtypeStruct((1, SW_HD), F32),
                   jax.ShapeDtypeStruct((1, 128), F32)),
        scratch_shapes=[pltpu.VMEM((SW_BLOCK, 512), F32), pltpu.VMEM((SW_BLOCK, 256), F32),
                        pltpu.VMEM((SW_BLOCK, 512), F32),
                        pltpu.VMEM((SW_BLOCK, 128), F32), pltpu.VMEM((SW_BLOCK, 128), F32),
                        pltpu.VMEM((SW_BLOCK, 128), F32), pltpu.VMEM((SW_BLOCK, 128), F32),
                        pltpu.VMEM((1, 512), F32), pltpu.VMEM((1, 128), F32), pltpu.VMEM((1, 128), F32)],
        compiler_params=_params(("arbitrary", "arbitrary")),
    )(proj, proj, proj, proj, proj, rope, rope, qg, kg, sinks, seg, y, dy)


def _head_rms(tv, gain):
    r = lax.rsqrt(jnp.mean(tv * tv, axis=1, keepdims=True) + EPS)
    return tv * r * gain, r


def _head_rms_bwd(dtn, tv, r, gain):
    u = dtn * gain
    return r * u - tv * (r * r * r) * jnp.mean(u * tv, axis=1, keepdims=True), jnp.sum(dtn * tv * r, axis=0, keepdims=True)


def _xa_softmax(raw):
    s = raw * (XA_HD ** -0.5)
    e = jnp.exp(s - jnp.max(s, axis=1, keepdims=True))
    return e / jnp.sum(e, axis=1, keepdims=True)


def _xa_fwd(qx, kvx, qg, kg, bsz, seq, mlen, *, tq=512):
    t = qx.shape[0]
    tq = min(tq, seq)
    nq = seq // tq
    w = XA_HEADS * XA_HD

    def body(q_ref, kv_ref, qg_ref, kg_ref, o_ref):
        heads = range(XA_HEADS)
        hs = [slice(XA_HD * h, XA_HD * (h + 1)) for h in heads]
        qn = [_head_rms(q_ref[:, hs[h]], qg_ref[...])[0] for h in heads]
        kn = [_head_rms(kv_ref[:, hs[h]], kg_ref[...])[0] for h in heads]
        raw = [_dot(qn[h], kn[h], NT) for h in heads]
        p = [_xa_softmax(raw[h]) for h in heads]
        for h in heads:
            o_ref[:, hs[h]] = _dot(p[h], kv_ref[:, w + XA_HD * h:w + XA_HD * (h + 1)]).astype(o_ref.dtype)

    vec = pl.BlockSpec((1, XA_HD), lambda b, i: (0, 0))
    return pl.pallas_call(
        body, name="xattn_fwd", grid=(bsz, nq),
        in_specs=[pl.BlockSpec((tq, w), lambda b, i: (b * nq + i, 0)),
                  pl.BlockSpec((mlen, 2 * w), lambda b, i: (b, 0)), vec, vec],
        out_specs=pl.BlockSpec((tq, w), lambda b, i: (b * nq + i, 0)),
        out_shape=jax.ShapeDtypeStruct((t, w), _MXU_DTYPE),
        compiler_params=_params(("parallel", "parallel")),
    )(qx, kvx, qg, kg)


def _xa_bwd(qx, kvx, qg, kg, do, bsz, seq, mlen, *, tq=512):
    t = qx.shape[0]
    tq = min(tq, seq)
    nq = seq // tq
    w = XA_HEADS * XA_HD
    scale = XA_HD ** -0.5

    def body(q_ref, kv_ref, qg_ref, kg_ref, do_ref, dq_ref, dkv_ref, dqg_ref, dkg_ref):
        b, i = pl.program_id(0), pl.program_id(1)

        @pl.when(jnp.logical_and(b == 0, i == 0))
        def _():
            dqg_ref[...] = jnp.zeros_like(dqg_ref)
            dkg_ref[...] = jnp.zeros_like(dkg_ref)

        @pl.when(i == 0)
        def _():
            dkv_ref[...] = jnp.zeros_like(dkv_ref)

        heads = range(XA_HEADS)
        hs = [slice(XA_HD * h, XA_HD * (h + 1)) for h in heads]
        vs = [slice(w + XA_HD * h, w + XA_HD * (h + 1)) for h in heads]
        qv = [q_ref[:, hs[h]] for h in heads]
        kv = [kv_ref[:, hs[h]] for h in heads]
        doh = [do_ref[:, hs[h]] for h in heads]
        qn = [_head_rms(qv[h], qg_ref[...]) for h in heads]
        kn = [_head_rms(kv[h], kg_ref[...]) for h in heads]
        raw = [_dot(qn[h][0], kn[h][0], NT) for h in heads]
        dp = [_dot(doh[h], kv_ref[:, vs[h]], NT) for h in heads]
        p = [_xa_softmax(raw[h]) for h in heads]
        ds = [p[h] * (dp[h] - jnp.sum(p[h] * dp[h], axis=1, keepdims=True)) * scale for h in heads]
        dqn = [_dot(ds[h], kn[h][0]) for h in heads]
        dkn = [_dot(ds[h], qn[h][0], TN) for h in heads]
        dvv = [_dot(p[h], doh[h], TN) for h in heads]
        gq_sum = jnp.zeros((1, XA_HD), F32)
        gk_sum = jnp.zeros((1, XA_HD), F32)
        for h in heads:
            dqv, gq = _head_rms_bwd(dqn[h], qv[h], qn[h][1], qg_ref[...])
            dkv, gk = _head_rms_bwd(dkn[h], kv[h], kn[h][1], kg_ref[...])
            dq_ref[:, hs[h]] = dqv.astype(dq_ref.dtype)
            dkv_ref[:, hs[h]] += dkv
            dkv_ref[:, vs[h]] += dvv[h]
            gq_sum = gq_sum + gq
            gk_sum = gk_sum + gk
        dqg_ref[...] += gq_sum
        dkg_ref[...] += gk_sum

    vec = pl.BlockSpec((1, XA_HD), lambda b, i: (0, 0))
    row = pl.BlockSpec((tq, w), lambda b, i: (b * nq + i, 0))
    mem = pl.BlockSpec((mlen, 2 * w), lambda b, i: (b, 0))
    return pl.pallas_call(
        body, name="xattn_bwd", grid=(bsz, nq),
        in_specs=[row, mem, vec, vec, row],
        out_specs=(row, mem, vec, vec),
        out_shape=(jax.ShapeDtypeStruct((t, w), _MXU_DTYPE), jax.ShapeDtypeStruct((bsz * mlen, 2 * w), F32),
                   jax.ShapeDtypeStruct((1, XA_HD), F32), jax.ShapeDtypeStruct((1, XA_HD), F32)),
        compiler_params=_params(("arbitrary", "arbitrary")),
    )(qx, kvx, qg, kg, do)


def _loss_finish(sq_row, d_model):
    def body(s_ref, o_ref):
        o_ref[...] = jnp.zeros_like(o_ref) + 0.5 * jnp.sum(s_ref[...]) / float(d_model)

    return pl.pallas_call(body, name="loss_finish", out_shape=jax.ShapeDtypeStruct((1, 128), F32))(sq_row)


def _adamw_math(w, g, m, v):
    m = ADAM_B1 * m + (1.0 - ADAM_B1) * g
    v = ADAM_B2 * v + (1.0 - ADAM_B2) * (g * g)
    m_hat = m / (1.0 - ADAM_B1 ** ADAM_STEP)
    v_hat = v / (1.0 - ADAM_B2 ** ADAM_STEP)
    return -ADAM_LR * (m_hat / (jnp.sqrt(v_hat) + ADAM_EPS) + ADAM_WD * w), m, v


def _adamw_big(w, g, m, v, *, name, tr=512):
    r, c = w.shape
    tr = min(tr, r)

    def body(w_ref, g_ref, m_ref, v_ref, go_ref, d_ref, mo_ref, vo_ref):
        gv = g_ref[...]
        d, mn, vn = _adamw_math(w_ref[...], gv, m_ref[...], v_ref[...])
        go_ref[...] = gv
        d_ref[...] = d
        mo_ref[...] = mn
        vo_ref[...] = vn

    spec = pl.BlockSpec((tr, c), lambda i: (i, 0))
    shp = jax.ShapeDtypeStruct((r, c), F32)
    return pl.pallas_call(
        body, name=name, grid=(r // tr,), in_specs=[spec] * 4, out_specs=(spec,) * 4, out_shape=(shp,) * 4,
        compiler_params=_params(("parallel",)),
    )(w, g, m, v)


def _adamw_small(ws, gs, ms, vs):
    n = len(ws)

    def body(*refs):
        for i in range(n):
            d, mn, vn = _adamw_math(refs[i][...], refs[n + i][...], refs[2 * n + i][...], refs[3 * n + i][...])
            refs[4 * n + i][...] = d
            refs[5 * n + i][...] = mn
            refs[6 * n + i][...] = vn

    shapes = tuple(jax.ShapeDtypeStruct(w.shape, F32) for w in ws)
    return pl.pallas_call(body, name="adamw_small", out_shape=shapes * 3)(*ws, *gs, *ms, *vs)


def _add_halves(g, recv, c_idx, *, name, tr=512):
    _, r, c = g.shape
    h = r // 2
    tr = min(tr, h)
    nt = h // tr

    def body(c_ref, g_ref, r_ref, o_ref):
        del c_ref
        o_ref[...] = g_ref[...] + r_ref[...]

    return pl.pallas_call(
        body, name=name,
        grid_spec=pltpu.PrefetchScalarGridSpec(
            num_scalar_prefetch=1, grid=(4, nt),
            in_specs=[pl.BlockSpec((None, tr, c), lambda k, i, cr: (k, cr[0] * nt + i, 0)),
                      pl.BlockSpec((None, tr, c), lambda k, i, cr: (k, i, 0))],
            out_specs=pl.BlockSpec((None, tr, c), lambda k, i, cr: (k, i, 0))),
        out_shape=jax.ShapeDtypeStruct((4, h, c), F32),
        compiler_params=_params(("parallel", "parallel")),
    )(c_idx, g, recv)


def _add_chips(p, recv, place_idx, *, name, tr=512, after=()):
    _, h, c = p.shape
    tr = min(tr, h)
    nt = h // tr

    def body(pi_ref, p_ref, r_ref, *rest):
        del pi_ref
        rest[-1][...] = ((p_ref[...] + r_ref[0]) + r_ref[1]) + r_ref[2]

    return pl.pallas_call(
        body, name=name,
        grid_spec=pltpu.PrefetchScalarGridSpec(
            num_scalar_prefetch=1, grid=(nt,),
            in_specs=[pl.BlockSpec((None, tr, c), lambda i, pi: (pi[0], i, 0)),
                      pl.BlockSpec((3, tr, c), lambda i, pi: (0, i, 0))] + [pl.BlockSpec(memory_space=pl.ANY)] * len(after),
            out_specs=pl.BlockSpec((tr, c), lambda i, pi: (pi[1] * nt + i, 0))),
        out_shape=jax.ShapeDtypeStruct((2 * h, c), F32),
        compiler_params=_params(("parallel",)),
    )(place_idx, p, recv, *after)


def _place_shard(shard, place_idx, *, name, tr=512, after=()):
    r, c = shard.shape
    tr = min(tr, r)

    def body(pi_ref, s_ref, *rest):
        del pi_ref
        rest[-1][...] = s_ref[...]

    return pl.pallas_call(
        body, name=name,
        grid_spec=pltpu.PrefetchScalarGridSpec(
            num_scalar_prefetch=1, grid=(r // tr,),
            in_specs=[pl.BlockSpec((tr, c), lambda i, pi: (i, 0))] + [pl.BlockSpec(memory_space=pl.ANY)] * len(after),
            out_specs=pl.BlockSpec((None, tr, c), lambda i, pi: (pi[0], i, 0))),
        out_shape=jax.ShapeDtypeStruct((4, r, c), shard.dtype),
        compiler_params=_params(("parallel",)),
    )(place_idx, shard, *after)


def _place():
    x, y, c = lax.axis_index("x"), lax.axis_index("y"), lax.axis_index("c")
    chips = [(1 - x, y), (x, 1 - y), (1 - x, 1 - y)]
    return x, y, c, chips


ANY = pl.BlockSpec(memory_space=pl.ANY)


def _exchange_halves(grads, name):
    n = len(grads)

    def body(*refs):
        ins, outs = refs[:n], refs[n:2 * n]
        send_sems, recv_sems = refs[2 * n:]
        x, y, c, _ = _place()

        def copy(a):
            h = ins[a].shape[1] // 2
            return pltpu.make_async_remote_copy(
                src_ref=ins[a].at[:, pl.ds((1 - c) * h, h), :], dst_ref=outs[a],
                send_sem=send_sems.at[a], recv_sem=recv_sems.at[a], device_id=(x, y, 1 - c), device_id_type=MESH)

        for a in range(n):
            copy(a).start()
        for a in range(n):
            copy(a).wait_recv()
        for a in range(n):
            copy(a).wait_send()

    return pl.pallas_call(
        body, name=name,
        in_specs=[ANY] * n, out_specs=tuple([ANY] * n),
        out_shape=tuple(jax.ShapeDtypeStruct((4, g.shape[1] // 2, g.shape[2]), g.dtype) for g in grads),
        scratch_shapes=[pltpu.SemaphoreType.DMA((n,)), pltpu.SemaphoreType.DMA((n,))],
    )(*grads)


HBM = pl.BlockSpec(memory_space=pltpu.HBM)
SEM = pl.BlockSpec(memory_space=pltpu.SEMAPHORE)
EFFECT = pltpu.SideEffectType.DATAFLOW_SIDE_EFFECTING


def _in_hbm(a):
    return pltpu.with_memory_space_constraint(a, pltpu.HBM)


def _split_copy_calls(name, srcs, lands, n_copies, make_copies):
    ns, nl = len(srcs), len(lands)
    nb = ns + nl

    def start(after=()):
        n_after = len(after)

        def body(*refs):
            outs = refs[nb + n_after:]
            copies = make_copies(refs[:ns], refs[ns:nb], outs[0], outs[1])
            for cp in copies:
                cp.start()
            token = refs[-1]
            token[...] = jnp.zeros_like(token)

        bufs = [_in_hbm(a) for a in list(srcs) + list(lands)]
        out = pl.pallas_call(
            body, name=name + "_start",
            out_shape=(pltpu.SemaphoreType.DMA((n_copies,)), pltpu.SemaphoreType.DMA((n_copies,)),
                       *[pltpu.HBM(a.shape, a.dtype) for a in bufs], jax.ShapeDtypeStruct((8, 128), F32)),
            in_specs=[HBM] * nb + [pl.BlockSpec(memory_space=pl.ANY)] * n_after,
            out_specs=(SEM, SEM, *[HBM] * nb, pl.BlockSpec(memory_space=pltpu.VMEM)),
            input_output_aliases={i: 2 + i for i in range(nb)},
            compiler_params=pltpu.CompilerParams(has_side_effects=EFFECT),
        )(*bufs, *after)
        return dict(send=out[0], recv=out[1], bufs=list(out[2:2 + nb]), token=out[-1])

    def wait(state, after):
        def body(*refs):
            copies = make_copies(refs[:ns], refs[ns:nb], refs[nb], refs[nb + 1])
            for cp in copies:
                cp.wait_send()
            for cp in copies:
                cp.wait_recv()

        bufs = state["bufs"]
        out = pl.pallas_call(
            body, name=name + "_wait",
            out_shape=tuple(pltpu.HBM(a.shape, a.dtype) for a in bufs),
            in_specs=[HBM] * nb + [SEM, SEM] + [pl.BlockSpec(memory_space=pl.ANY)] * len(after),
            out_specs=tuple([HBM] * nb),
            input_output_aliases={i: i for i in range(nb)},
            compiler_params=pltpu.CompilerParams(has_side_effects=EFFECT),
        )(*bufs, state["send"], state["recv"], *after)
        return list(out[:ns]), list(out[ns:])

    return start, wait


def _scatter_chips_split(name, parts):
    n = len(parts)
    lands = [lax.empty((3,) + p.shape[1:], p.dtype) for p in parts]

    def make_copies(srcs, lnds, send_sems, recv_sems):
        _, _, c, chips = _place()
        return [pltpu.make_async_remote_copy(
            src_ref=srcs[a].at[2 * px + py], dst_ref=lnds[a].at[j], send_sem=send_sems.at[a * 3 + j],
            recv_sem=recv_sems.at[a * 3 + j], device_id=(px, py, c), device_id_type=MESH)
            for a in range(n) for j, (px, py) in enumerate(chips)]

    return _split_copy_calls(name, parts, lands, 3 * n, make_copies)


def _exchange_halves_split(name, grads):
    n = len(grads)
    lands = [lax.empty((4, g.shape[1] // 2, g.shape[2]), g.dtype) for g in grads]

    def make_copies(srcs, lnds, send_sems, recv_sems):
        x, y, c, _ = _place()
        out = []
        for a in range(n):
            h = srcs[a].shape[1] // 2
            out.append(pltpu.make_async_remote_copy(
                src_ref=srcs[a].at[:, pl.ds((1 - c) * h, h), :], dst_ref=lnds[a], send_sem=send_sems.at[a],
                recv_sem=recv_sems.at[a], device_id=(x, y, 1 - c), device_id_type=MESH))
        return out

    return _split_copy_calls(name, grads, lands, n, make_copies)


def _gather_chips_split(name, shards, lands):
    n = len(shards)

    def make_copies(srcs, lnds, send_sems, recv_sems):
        x, y, c, chips = _place()
        out = []
        for a in range(n):
            h = srcs[a].shape[0] // 2
            for j, (px, py) in enumerate(chips):
                out.append(pltpu.make_async_remote_copy(
                    src_ref=srcs[a].at[pl.ds(c * h, h), :], dst_ref=lnds[a].at[2 * x + y, pl.ds(c * h, h), :],
                    send_sem=send_sems.at[a * 3 + j], recv_sem=recv_sems.at[a * 3 + j],
                    device_id=(px, py, c), device_id_type=MESH))
        return out

    return _split_copy_calls(name, shards, lands, 3 * n, make_copies)


def _gather_finish(gathered, name):
    n = len(gathered)

    def body(*refs):
        outs = refs[n:2 * n]
        send_sems, recv_sems = refs[2 * n:]
        x, y, c, chips = _place()

        def copy(a, j, chip_idx, which):
            h = outs[a].shape[1] // 2
            rows = outs[a].at[chip_idx, pl.ds(which * h, h), :]
            return pltpu.make_async_remote_copy(
                src_ref=rows, dst_ref=rows, send_sem=send_sems.at[a * 3 + j], recv_sem=recv_sems.at[a * 3 + j],
                device_id=(x, y, 1 - c), device_id_type=MESH)

        for a in range(n):
            for j, (px, py) in enumerate(chips):
                copy(a, j, 2 * px + py, c).start()
        for a in range(n):
            for j, (px, py) in enumerate(chips):
                copy(a, j, 2 * px + py, 1 - c).wait_recv()
        for a in range(n):
            for j, (px, py) in enumerate(chips):
                copy(a, j, 2 * px + py, c).wait_send()

    return pl.pallas_call(
        body, name=name,
        in_specs=[ANY] * n, out_specs=tuple([ANY] * n),
        out_shape=tuple(jax.ShapeDtypeStruct(g.shape, g.dtype) for g in gathered),
        input_output_aliases={i: i for i in range(n)},
        scratch_shapes=[pltpu.SemaphoreType.DMA((3 * n,)), pltpu.SemaphoreType.DMA((3 * n,))],
    )(*gathered)


def _gather_forward_split(name, gathered):
    n = len(gathered)

    def make_copies(srcs, lnds, send_sems, recv_sems):
        x, y, c, chips = _place()
        out = []
        for a in range(n):
            h = lnds[a].shape[1] // 2
            for j, (px, py) in enumerate(chips):
                rows = lnds[a].at[2 * px + py, pl.ds(c * h, h), :]
                out.append(pltpu.make_async_remote_copy(
                    src_ref=rows, dst_ref=rows, send_sem=send_sems.at[a * 3 + j], recv_sem=recv_sems.at[a * 3 + j],
                    device_id=(x, y, 1 - c), device_id_type=MESH))
        return out

    return _split_copy_calls(name, [], gathered, 3 * n, make_copies)


def _join_halves_split(name, fulls):
    n = len(fulls)

    def make_copies(srcs, lnds, send_sems, recv_sems):
        x, y, c, _ = _place()
        out = []
        for a in range(n):
            h = lnds[a].shape[0] // 2
            rows = lnds[a].at[pl.ds(c * h, h), :]
            out.append(pltpu.make_async_remote_copy(
                src_ref=rows, dst_ref=rows, send_sem=send_sems.at[a], recv_sem=recv_sems.at[a],
                device_id=(x, y, 1 - c), device_id_type=MESH))
        return out

    return _split_copy_calls(name, [], fulls, n, make_copies)


def _all_gather_small_split(sm):
    r, w = sm.shape

    def make_copies(srcs, lnds, send_sems, recv_sems):
        x, y, c, _ = _place()
        me = 4 * x + 2 * y + c
        rel = [(dx, dy, dc) for dx in (0, 1) for dy in (0, 1) for dc in (0, 1)][1:]
        return [pltpu.make_async_remote_copy(
            src_ref=srcs[0], dst_ref=lnds[0].at[me], send_sem=send_sems.at[k], recv_sem=recv_sems.at[k],
            device_id=(1 - x if dx else x, 1 - y if dy else y, 1 - c if dc else c), device_id_type=MESH)
            for k, (dx, dy, dc) in enumerate(rel)]

    return _split_copy_calls("all_gather_small", [sm], [lax.empty((8, r, w), sm.dtype)], 7, make_copies)


def _sum_devices(sm, gathered, me_idx):
    def body(me_ref, sm_ref, g_ref, o_ref):
        own = sm_ref[...]
        acc = jnp.where(me_ref[0] == 0, own, g_ref[0])
        for d in range(1, 8):
            acc = acc + jnp.where(me_ref[0] == d, own, g_ref[d])
        o_ref[...] = acc

    vm = pl.BlockSpec(memory_space=pltpu.VMEM)
    return pl.pallas_call(
        body, name="sum_devices", in_specs=[pl.BlockSpec(memory_space=pltpu.SMEM), vm, vm], out_specs=vm,
        out_shape=jax.ShapeDtypeStruct(sm.shape, F32),
    )(me_idx, sm, gathered)


def _local_step(x3, mem3, pos2, target3, small, comm):
    bsz, seq, d = x3.shape
    mlen = mem3.shape[1]
    t = bsz * seq
    tok = comm.begin()
    x = x3.reshape(t, d)
    mem = mem3.reshape(bsz * mlen, d)
    target = target3.reshape(t, d)
    rope = _rope_table(pos2.reshape(t, 1))
    qg_t = jnp.tile(small["sw_q_norm_g"], (1, SW_HEADS))
    kg_t = jnp.tile(small["sw_k_norm_g"], (1, SW_KV_HEADS))

    hn1 = _rms_fwd(x, small["norm1_g"], name="rms1_fwd", after=tok)
    w = comm.first(hn1)
    proj_hg = _mm(hn1, w["w_in_hg"], NN, t, HG_COLS, d, name="proj_hg", tk=d, after=(w.get("token"),))[0]
    proj_sw = _mm(hn1, w["w_in_sw"], NN, t, SW_COLS, d, name="proj_sw", tk=d)[0]
    y_mix, o_hg, states = _hg_fwd(proj_hg, small["hg_lower_bounds"], small["hg_norm_g"], bsz, seq, y_width=1024)
    y_mix = _sw_fwd(proj_sw, rope, qg_t, kg_t, small["sw_sinks"], y_mix, bsz, seq)
    w_in_hg, w_in_sw = w["w_in_hg"], w["w_in_sw"]
    w = comm.rest(y_mix)
    h1, hn2 = _mm(y_mix, w["w_out"], NN, t, d, 1024, name="out_proj", tk=1024, extras=(x,), rows=(small["norm2_g"],),
                  epilogue=_residual_rms, out_dtypes=(F32, _MXU_DTYPE), after=(w.get("token"),))
    mn = _rms_fwd(mem, small["mem_norm_g"], name="rms_mem_fwd")
    qx = _mm(hn2, w["wq"], NN, t, 512, d, name="xa_q", tk=d)[0]
    kvx = _mm(mn, w["wkv"], NN, bsz * mlen, 1024, d, name="xa_kv", tk=d)[0]
    ox = _xa_fwd(qx, kvx, small["xa_q_norm_g"], small["xa_k_norm_g"], bsz, seq, mlen)
    h2, hn3 = _mm(ox, w["wo"], NN, t, d, 512, name="xa_o", tk=512, extras=(h1,), rows=(small["norm3_g"],),
                  epilogue=_residual_rms, out_dtypes=(F32, _MXU_DTYPE))
    w = {**w, **comm.mlp(hn3)}
    ff = w["down"].shape[0]
    ffs = ff // 4

    def relu_sq(acc):
        a = jnp.maximum(acc, 0.0)
        return a, a * a

    act, act2 = _mm(hn3, w["up"], NN, t, ff, d, name="mlp_up", tm=2048, tn=ffs, tk=d,
                    b_spec=pl.BlockSpec((None, d, ffs), lambda i, j, kk: (j, 0, 0)),
                    epilogue=relu_sq, out_dtypes=(_MXU_DTYPE, _MXU_DTYPE))
    inv_d = 1.0 / d

    def loss_cotangent(acc, res, tgt):
        diff = acc + res - tgt
        v = diff * inv_d
        return v, v, jnp.sum(diff * diff, axis=0, keepdims=True)

    dy, dy_mx, sq_row = _mm(act2, w["down"], NN, t, d, ff, name="mlp_down", tk=2048, extras=(h2, target),
                            epilogue=loss_cotangent, out_dtypes=(F32, _MXU_DTYPE), row_sums=1)
    loss_row = _loss_finish(sq_row, d)

    dz = _mm(dy_mx, w["down"], NT, t, ff, d, name="d_act", tm=2048, tk=d, extras=(act,),
             epilogue=lambda acc, a: (acc * (2.0 * a.astype(F32)),), out_dtypes=(_MXU_DTYPE,))[0]
    g_down = _mm(act2, dy_mx, TN, ff, d, t, name="g_down", tk=t)[0]
    g_up = _mm(hn3, dz, TN, d, ff, t, name="g_up", tn=ffs, tk=t,
               out_shape=(jax.ShapeDtypeStruct((4, d, ffs), F32),),
               out_spec=(pl.BlockSpec((None, min(1024, d), ffs), lambda i, j, kk: (j, i, 0)),))[0]
    tok = comm.grads("mlp", dict(up=g_up, down=g_down))
    dh2, dh2_mx, g_norm3 = _mm(dz, w["up"], NT, t, d, ff, name="d_hn3", tk=ffs, after=tok,
                               b_spec=pl.BlockSpec((None, min(1024, d), ffs), lambda i, j, kk: (kk, j, 0)),
                               extras=(h2, dy), rows=(small["norm3_g"],), epilogue=_rms_bwd_residual,
                               out_dtypes=(F32, _MXU_DTYPE), row_sums=1)
    d_ox = _mm(dh2_mx, w["wo"], NT, t, 512, d, name="d_ox", tk=d)[0]
    g_wo = _mm(ox, dh2_mx, TN, 512, d, t, name="g_wo", tk=t)[0]
    d_qx, d_kvx, g_xq, g_xk = _xa_bwd(qx, kvx, small["xa_q_norm_g"], small["xa_k_norm_g"], d_ox, bsz, seq, mlen)
    g_wq = _mm(hn2, d_qx, TN, d, 512, t, name="g_wq")[0]
    g_wkv = _mm(mn, d_kvx, TN, d, 1024, bsz * mlen, name="g_wkv")[0]
    dh1, dh1_mx, g_norm2 = _mm(d_qx, w["wq"], NT, t, d, 512, name="d_hn2", tk=512, extras=(h1, dh2),
                               rows=(small["norm2_g"],), epilogue=_rms_bwd_residual, out_dtypes=(F32, _MXU_DTYPE),
                               row_sums=1)
    dmn = _mm(d_kvx, w["wkv"], NT, bsz * mlen, d, 1024, name="d_mn", tk=1024)[0]
    g_memn = _rms_gain_grad(mem, small["mem_norm_g"], dmn, name="rms_mem_bwd")
    g_wout = _mm(y_mix, dh1_mx, TN, 1024, d, t, name="g_wout", tk=2048)[0]
    tok = comm.grads("mid", dict(w_out=g_wout, wq=g_wq, wkv=g_wkv, wo=g_wo))
    d_mix = _mm(dh1_mx, w["w_out"], NT, t, 1024, d, name="d_mix", tk=d, after=tok)[0]
    dproj_sw, g_swq, g_swk, g_sinks = _sw_bwd(proj_sw, rope, qg_t, kg_t, small["sw_sinks"], y_mix, d_mix, bsz, seq)
    tok = comm.poll(dproj_sw)
    dproj_hg, g_lb, g_hgn = _hg_bwd(proj_hg, small["hg_lower_bounds"], small["hg_norm_g"], o_hg, states, d_mix, bsz, seq,
                                    after=tok)
    g_in_hg = _mm(hn1, dproj_hg, TN, d, HG_COLS, t, name="g_in_hg", tk=t)[0]
    g_in_sw = _mm(hn1, dproj_sw, TN, d, SW_COLS, t, name="g_in_sw")[0]
    tok = comm.grads("in", dict(w_in_hg=g_in_hg, w_in_sw=g_in_sw))
    dhn1_a = _mm(dproj_hg, w_in_hg, NT, t, d, HG_COLS, name="d_hn1_hg", tk=HG_COLS, after=tok)[0]
    grad_x, g_norm1 = _mm(dproj_sw, w_in_sw, NT, t, d, SW_COLS, name="d_hn1_sw", tk=SW_COLS, extras=(dhn1_a, x, dh1),
                          rows=(small["norm1_g"],), row_sums=1,
                          epilogue=lambda acc, prev, xv, dres, g: _rms_bwd_residual(acc + prev, xv, dres, g)[1:])

    g_small = dict(norm1_g=g_norm1, hg_lower_bounds=g_lb, hg_norm_g=g_hgn, sw_q_norm_g=g_swq, sw_k_norm_g=g_swk,
                   sw_sinks=g_sinks[:, 0:SW_HEADS], norm2_g=g_norm2, mem_norm_g=g_memn, xa_q_norm_g=g_xq,
                   xa_k_norm_g=g_xk, norm3_g=g_norm3)
    return loss_row, grad_x.reshape(bsz, seq, d), g_small


SMALL_NAMES = ("norm1_g", "hg_lower_bounds", "hg_norm_g", "sw_q_norm_g", "sw_k_norm_g", "sw_sinks", "norm2_g",
               "mem_norm_g", "xa_q_norm_g", "xa_k_norm_g", "norm3_g")
BIG_NAMES = ("w_in", "w_out", "xa_wq", "xa_wkv", "xa_wo", "mlp_up", "mlp_down")
WEIGHT_ORDER = ("norm1_g", "w_in", "hg_lower_bounds", "hg_norm_g", "sw_q_norm_g", "sw_k_norm_g", "sw_sinks", "w_out",
                "norm2_g", "mem_norm_g", "xa_wq", "xa_wkv", "xa_q_norm_g", "xa_k_norm_g", "xa_wo", "norm3_g",
                "mlp_up", "mlp_down")


def _pack_rows(vals, width):
    starts, at = [], 0
    for v in vals:
        starts.append(at)
        at += v.shape[0]
    total = at + (-at) % 8
    out = None
    for v, s in zip(vals, starts):
        placed = jnp.pad(v, ((s, total - s - v.shape[0]), (0, width - v.shape[1])))
        out = placed if out is None else out + placed
    return out, starts


class _MeshWeights:
    LATE = ("w_out", "xa_wq", "xa_wkv", "xa_wo", "mlp_up", "mlp_down")

    def __init__(self, shards, d, ff):
        self.shards, self.d, self.ff = shards, d, ff
        self.c_idx = lax.axis_index("c").astype(jnp.int32).reshape(1)
        chip = (2 * lax.axis_index("x") + lax.axis_index("y")).astype(jnp.int32)
        self.place_idx = jnp.stack([chip, lax.axis_index("c").astype(jnp.int32)])
        self.pending = []
        self.exchanging = None

    def begin(self):
        shard = self.shards["w_in"]
        start, self.in_wait = _gather_chips_split(
            "gather_in", [shard], [_place_shard(shard, self.place_idx, name="place_w_in")])
        self.in_state = start()
        tok = (self.in_state["token"],)
        self.placed = [_place_shard(self.shards[n], self.place_idx, name="place_" + n, after=tok) for n in self.LATE]
        return tok

    def first(self, after):
        _, lands = self.in_wait(self.in_state, (after, *self.placed))
        (g_in,) = _gather_finish(lands, "gather_in_finish")
        start, self.late_wait = _gather_chips_split("gather_late", [self.shards[n] for n in self.LATE], self.placed)
        self.late_state = start(after=(g_in,))
        ws = g_in.shape[2]
        cut = HG_COLS - 2 * ws
        return dict(w_in_hg=jnp.concatenate([g_in[0], g_in[1], g_in[2][:, :cut]], axis=1),
                    w_in_sw=jnp.concatenate([g_in[2][:, cut:], g_in[3]], axis=1), token=self.late_state["token"])

    def rest(self, after):
        _, lands = self.late_wait(self.late_state, (after,))
        g_out, g_q, g_kv, g_o = _gather_finish(lands[:4], "gather_late_finish")
        start, self.mlp_wait = _gather_forward_split("gather_mlp_forward", lands[4:])
        self.mlp_state = start(after=(g_out,))
        d = self.d
        return dict(w_out=g_out.reshape(-1, d), wq=g_q.reshape(d, -1), wkv=g_kv.reshape(d, -1),
                    wo=jnp.concatenate([g_o[k] for k in range(4)], axis=1), token=self.mlp_state["token"])

    def mlp(self, after):
        _, (g_up, g_dn) = self.mlp_wait(self.mlp_state, (after,))
        return dict(up=g_up, down=g_dn.reshape(self.ff, self.d))

    def _scatter(self, tag, names, arrays, recv):
        parts = [_add_halves(g, r, self.c_idx, name="rs_add_halves_" + n) for n, g, r in zip(names, arrays, recv)]
        start, wait = _scatter_chips_split("rs_scatter_" + tag, parts)
        state = start()
        self.pending.append((names, wait, state))
        return state["token"]

    def _advance(self, after):
        if self.exchanging is None:
            return ()
        tag, names, wait, state = self.exchanging
        self.exchanging = None
        arrays, recv = wait(state, (after,))
        return (self._scatter(tag, names, arrays, recv),)

    def poll(self, after):
        return self._advance(after)

    def grads(self, tag, g):
        d, ff = self.d, self.ff
        if tag == "mlp":
            names, arrays = ("mlp_up", "mlp_down"), [g["up"], g["down"].reshape(4, ff // 4, d)]
        elif tag == "mid":
            names = ("w_out", "xa_wq", "xa_wkv", "xa_wo")
            ds = d // 4
            g_wo = jnp.stack([g["wo"][:, ds * k:ds * (k + 1)] for k in range(4)])
            arrays = [g["w_out"].reshape(4, -1, d), g["wq"].reshape(4, d // 4, -1), g["wkv"].reshape(4, d // 4, -1), g_wo]
        else:
            hg, sw = g["w_in_hg"], g["w_in_sw"]
            ws = (hg.shape[1] + sw.shape[1]) // 4
            cut = hg.shape[1] - 2 * ws
            names = ("w_in",)
            arrays = [jnp.stack([hg[:, :ws], hg[:, ws:2 * ws], jnp.concatenate([hg[:, 2 * ws:], sw[:, :ws - cut]], axis=1),
                                 sw[:, ws - cut:]])]
        toks = self._advance(arrays[0])
        if tag == "in":
            return toks + (self._scatter(tag, names, arrays, _exchange_halves(arrays, "rs_exchange_" + tag)),)
        start, wait = _exchange_halves_split("rs_exchange_" + tag, arrays)
        state = start()
        self.exchanging = (tag, names, wait, state)
        return toks + (state["token"],)

    def finish(self, after):
        joins, tok = [], ()
        for names, wait, state in self.pending:
            srcs, lands = wait(state, tuple(after) + tok)
            fulls = [_add_chips(p, r, self.place_idx, name="rs_add_chips_" + n, after=tok)
                     for n, p, r in zip(names, srcs, lands)]
            start, jwait = _join_halves_split("rs_join_" + names[0], fulls)
            jstate = start()
            tok = (jstate["token"],)
            joins.append((names, jwait, jstate))
        out = {}
        for names, jwait, jstate in joins:
            _, fulls = jwait(jstate, tok)
            out.update(zip(names, fulls))
        return out


def kernel(x, mem, positions, norm1_g, w_in, hg_lower_bounds, hg_norm_g, sw_q_norm_g, sw_k_norm_g, sw_sinks, w_out, norm2_g, mem_norm_g, xa_wq, xa_wkv, xa_q_norm_g, xa_k_norm_g, xa_wo, norm3_g, mlp_up, mlp_down, loss_target, m_norm1_g, m_w_in, m_hg_lower_bounds, m_hg_norm_g, m_sw_q_norm_g, m_sw_k_norm_g, m_sw_sinks, m_w_out, m_norm2_g, m_mem_norm_g, m_xa_wq, m_xa_wkv, m_xa_q_norm_g, m_xa_k_norm_g, m_xa_wo, m_norm3_g, m_mlp_up, m_mlp_down, v_norm1_g, v_w_in, v_hg_lower_bounds, v_hg_norm_g, v_sw_q_norm_g, v_sw_k_norm_g, v_sw_sinks, v_w_out, v_norm2_g, v_mem_norm_g, v_xa_wq, v_xa_wkv, v_xa_q_norm_g, v_xa_k_norm_g, v_xa_wo, v_norm3_g, v_mlp_up, v_mlp_down):
    given = dict(locals())
    weights = {n: given[n] for n in WEIGHT_ORDER}
    moms = {n: given["m_" + n] for n in WEIGHT_ORDER}
    vars_ = {n: given["v_" + n] for n in WEIGHT_ORDER}
    d = x.shape[-1]
    ff = mlp_down.shape[1] * 4
    small = {n: weights[n] for n in SMALL_NAMES}

    comm = _MeshWeights({n: weights[n][0].astype(_MXU_DTYPE) for n in BIG_NAMES}, d, ff)
    loss_row, grad_x, g_small = _local_step(x, mem, positions, loss_target, small, comm)
    packed, starts = _pack_rows([g_small[n] for n in SMALL_NAMES] + [loss_row], 1024)
    start, wait = _all_gather_small_split(packed)
    state = start()
    big_grads = comm.finish((grad_x, state["token"]))
    (own,), (gathered,) = wait(state, (big_grads[BIG_NAMES[0]],))
    device = (4 * lax.axis_index("x") + 2 * lax.axis_index("y") + lax.axis_index("c")).astype(jnp.int32).reshape(1)
    summed = _sum_devices(own, gathered, device)
    small_grads = {}
    for n, s in zip(SMALL_NAMES, starts):
        r, c = weights[n].shape
        small_grads[n] = summed[s:s + r, 0:c]
    loss = summed[starts[-1], 0]

    grads, deltas, new_m, new_v = {}, {}, {}, {}
    for n in BIG_NAMES:
        shp = weights[n].shape
        g2, dl, mo, vo = _adamw_big(weights[n][0], big_grads[n], moms[n][0], vars_[n][0], name="adamw_" + n)
        grads[n], deltas[n], new_m[n], new_v[n] = (a.reshape(shp) for a in (g2, dl, mo, vo))
    sm_out = _adamw_small([weights[n] for n in SMALL_NAMES], [small_grads[n] for n in SMALL_NAMES],
                          [moms[n] for n in SMALL_NAMES], [vars_[n] for n in SMALL_NAMES])
    ns = len(SMALL_NAMES)
    for i, n in enumerate(SMALL_NAMES):
        grads[n], deltas[n], new_m[n], new_v[n] = small_grads[n], sm_out[i], sm_out[ns + i], sm_out[2 * ns + i]

    return (loss, grad_x, *[grads[n] for n in WEIGHT_ORDER], *[deltas[n] for n in WEIGHT_ORDER],
            *[new_m[n] for n in WEIGHT_ORDER], *[new_v[n] for n in WEIGHT_ORDER])
```

```python
import numpy as np
import jax
import jax.numpy as jnp
from jax import lax
from jax.experimental import pallas as pl
from jax.experimental.pallas import tpu as pltpu

F32 = jnp.float32
_MXU_DTYPE = jnp.bfloat16

EPS = 1e-6
HG_HEADS = 4
HG_D = 128
HG_CHUNK = 64
HG_TILE = 512
HG_LEVELS = (32, 16, 8, 4, 2, 1)
SW_HEADS = 8
SW_KV_HEADS = 2
SW_GROUP = SW_HEADS // SW_KV_HEADS
SW_HD = 64
SW_BLOCK = 128
ROPE_THETA = 500000.0
ROT_DIM = SW_HD // 4
XA_HEADS = 4
XA_HD = 128
HG_COLS = 4 * HG_HEADS * HG_D
SW_COLS = (SW_HEADS + 2 * SW_KV_HEADS) * SW_HD

ADAM_LR = 0.001
ADAM_B1 = 0.9
ADAM_B2 = 0.999
ADAM_EPS = 1e-08
ADAM_WD = 0.01
ADAM_STEP = 10

VMEM_LIMIT = 56 * 1024 * 1024
MESH = pl.DeviceIdType.MESH

NN = ((1,), (0,))
NT = ((1,), (1,))
TN = ((0,), (0,))


def _mx(v):
    return v.astype(_MXU_DTYPE)


def _dot(a, b, dims=NN):
    return lax.dot_general(_mx(a), _mx(b), (dims, ((), ())), preferred_element_type=F32)


def _split_dot(a, v, dims, parts):
    acc = None
    rest = v
    for p in range(parts):
        piece = _mx(rest)
        term = lax.dot_general(a, piece, (dims, ((), ())), preferred_element_type=F32)
        acc = term if acc is None else acc + term
        if p + 1 < parts:
            rest = rest - piece.astype(F32)
    return acc


def _params(sem):
    return pltpu.CompilerParams(dimension_semantics=sem, vmem_limit_bytes=VMEM_LIMIT)


def _mm(a, b, mode, m, n, k, *, name, tm=1024, tn=1024, tk=1024, a_spec=None, b_spec=None, extras=(), rows=(),
        epilogue=None, out_dtypes=(F32,), row_sums=0, out_shape=None, out_spec=None, after=()):
    after = tuple(t for t in after if t is not None)
    tm, tn, tk = min(tm, m), min(tn, n), min(tk, k)
    assert m % tm == 0 and n % tn == 0 and k % tk == 0, (name, m, n, k, tm, tn, tk)
    gi, gj, gk = m // tm, n // tn, k // tk
    assert row_sums == 0 or gj == 1, name
    if a_spec is None:
        a_spec = (pl.BlockSpec((tk, tm), lambda i, j, kk: (kk, i)) if mode == TN
                  else pl.BlockSpec((tm, tk), lambda i, j, kk: (i, kk)))
    if b_spec is None:
        b_spec = (pl.BlockSpec((tn, tk), lambda i, j, kk: (j, kk)) if mode == NT
                  else pl.BlockSpec((tk, tn), lambda i, j, kk: (kk, j)))
    mn_spec = pl.BlockSpec((tm, tn), lambda i, j, kk: (i, j))
    if epilogue is None:
        epilogue = lambda acc: (acc,)
    row_spec = pl.BlockSpec((1, tn), lambda i, j, kk: (0, j))
    n_ex, n_out = len(extras) + len(rows), len(out_dtypes)
    if out_shape is None:
        out_shape = tuple(jax.ShapeDtypeStruct((m, n), d) for d in out_dtypes)
        out_spec = tuple(mn_spec for _ in out_dtypes)
    out_shape = tuple(out_shape) + tuple(jax.ShapeDtypeStruct((1, n), F32) for _ in range(row_sums))
    out_spec = tuple(out_spec) + tuple(row_spec for _ in range(row_sums))

    n_after = len(after)

    def body(*refs):
        a_ref, b_ref = refs[0], refs[1]
        ex = refs[2:2 + n_ex]
        outs = refs[2 + n_ex + n_after:2 + n_ex + n_after + n_out + row_sums]
        first_row_tile = pl.program_id(0) == 0

        def finish(acc):
            res = epilogue(acc, *[e[...] for e in ex])
            for o, r in zip(outs[:n_out], res[:n_out]):
                o[...] = r.astype(o.dtype)
            if row_sums:
                @pl.when(first_row_tile)
                def _():
                    for o in outs[n_out:]:
                        o[...] = jnp.zeros_like(o)

                for o, r in zip(outs[n_out:], res[n_out:]):
                    o[...] += r

        if gk == 1:
            finish(_dot(a_ref[...], b_ref[...], mode))
        else:
            acc_ref = refs[-1]
            kk = pl.program_id(2)

            @pl.when(kk == 0)
            def _():
                acc_ref[...] = jnp.zeros_like(acc_ref)

            acc_ref[...] += _dot(a_ref[...], b_ref[...], mode)

            @pl.when(kk == gk - 1)
            def _():
                finish(acc_ref[...])

    return pl.pallas_call(
        body, name=name, grid=(gi, gj, gk),
        in_specs=([a_spec, b_spec] + [mn_spec] * len(extras) + [row_spec] * len(rows)
                  + [pl.BlockSpec(memory_space=pl.ANY)] * n_after),
        out_specs=out_spec, out_shape=out_shape,
        scratch_shapes=[pltpu.VMEM((tm, tn), F32)] if gk > 1 else [],
        compiler_params=_params(("arbitrary" if row_sums else "parallel", "parallel", "arbitrary")),
    )(a, b, *extras, *rows, *after)


def _rms_rows(xv, g):
    return xv * lax.rsqrt(jnp.mean(xv * xv, axis=1, keepdims=True) + EPS) * g


def _rms_rows_bwd(xv, g, dyv):
    r = lax.rsqrt(jnp.mean(xv * xv, axis=1, keepdims=True) + EPS)
    u = dyv * g
    return (r * u - xv * (r * r * r) * jnp.mean(u * xv, axis=1, keepdims=True),
            jnp.sum(dyv * xv * r, axis=0, keepdims=True))


def _residual_rms(acc, res, g):
    h = acc + res
    return h, _rms_rows(h, g)


def _rms_bwd_residual(dhn, xv, dres, g):
    dx, dg = _rms_rows_bwd(xv, g, dhn)
    dx = dx + dres
    return dx, dx, dg


def _rms_fwd(x, g, *, name, tm=512, after=()):
    t, d = x.shape
    tm = min(tm, t)
    after = tuple(a for a in after if a is not None)

    def body(x_ref, g_ref, *rest):
        rest[-1][...] = _rms_rows(x_ref[...], g_ref[...]).astype(rest[-1].dtype)

    return pl.pallas_call(
        body, name=name, grid=(t // tm,),
        in_specs=[pl.BlockSpec((tm, d), lambda i: (i, 0)), pl.BlockSpec((1, d), lambda i: (0, 0))]
        + [pl.BlockSpec(memory_space=pl.ANY)] * len(after),
        out_specs=pl.BlockSpec((tm, d), lambda i: (i, 0)),
        out_shape=jax.ShapeDtypeStruct((t, d), _MXU_DTYPE),
        compiler_params=_params(("parallel",)),
    )(x, g, *after)


def _rms_gain_grad(x, g, dy, *, name, tm=512):
    t, d = x.shape
    tm = min(tm, t)

    def body(x_ref, g_ref, dy_ref, dg_ref):
        @pl.when(pl.program_id(0) == 0)
        def _():
            dg_ref[...] = jnp.zeros_like(dg_ref)

        dg_ref[...] += _rms_rows_bwd(x_ref[...], g_ref[...], dy_ref[...])[1]

    row = pl.BlockSpec((tm, d), lambda i: (i, 0))
    vec = pl.BlockSpec((1, d), lambda i: (0, 0))
    return pl.pallas_call(
        body, name=name, grid=(t // tm,), in_specs=[row, vec, row], out_specs=vec,
        out_shape=jax.ShapeDtypeStruct((1, d), F32), compiler_params=_params(("arbitrary",)),
    )(x, g, dy)


def _hg_constants():
    c = HG_CHUNK
    t = np.arange(c)
    sums = [t[None, :] <= t[:, None]]
    masks = []
    for m in HG_LEVELS:
        base = (t // (2 * m)) * (2 * m)
        mid = base + m - 1
        second = (t - base) >= m
        upper = (t[None, :] > mid[:, None]) & (t[None, :] <= t[:, None])
        lower = (t[None, :] > t[:, None]) & (t[None, :] <= mid[:, None])
        sums.append(np.where(second[:, None], upper, lower))
        masks.append(second[:, None] & (~second)[None, :] & (base[:, None] == base[None, :]))
    return (np.concatenate(sums, axis=0).astype(np.float32), np.stack(masks).astype(np.float32))


HG_HEAD_LANES = tuple(slice(HG_D * h, HG_D * (h + 1)) for h in range(HG_HEADS))


def _per_head(fn, slab):
    return jnp.concatenate([jnp.broadcast_to(fn(slab[:, hs]), (slab.shape[0], HG_D)) for hs in HG_HEAD_LANES], axis=1)


def _lane_sum(v):
    return jnp.sum(v, axis=1, keepdims=True)


def _lane_mean(v):
    return jnp.mean(v, axis=1, keepdims=True)


def _hg_gates(blk, lbp):
    w = HG_HEADS * HG_D
    q, x, v, gl = blk[:, 0:w], blk[:, w:2 * w], blk[:, 2 * w:3 * w], blk[:, 3 * w:4 * w]
    mx = jnp.max(lbp, axis=0, keepdims=True)
    e = jnp.exp(lbp - mx)
    lb = e[0:1, :] / jnp.sum(e, axis=0, keepdims=True)
    sig = jax.nn.sigmoid(x)
    f = lb + (1.0 - lb) * sig
    return q, v, gl, lb, sig, f, 1.0 - f, jnp.log(f)


def _hg_fwd(proj, lbp, ng, bsz, seq, *, y_width):
    t = proj.shape[0]
    nc = seq // HG_CHUNK
    a_np, m_np = _hg_constants()
    a_all = jnp.asarray(a_np, _MXU_DTYPE)
    masks = jnp.asarray(m_np, F32)
    nl = len(HG_LEVELS)

    ts = min(HG_TILE, seq)
    ns, nct = seq // ts, ts // HG_CHUNK
    hw = HG_HEADS * HG_D

    def body(p_ref, lb_ref, ng_ref, a_ref, m_ref, y_ref, o_ref, st_ref, carry):
        a_mat = a_ref[...]
        ngv = ng_ref[...]

        @pl.when(pl.program_id(0) == 0)
        def _():
            carry[...] = jnp.zeros_like(carry)

        ng4 = _tile_lanes(ngv, HG_HEADS)
        heads = range(HG_HEADS)
        exs = range(bsz)
        hl = HG_HEAD_LANES
        lbp_v = lb_ref[...]

        def chunk(c, _):
            rows = pl.ds(pl.multiple_of(c * HG_CHUNK, HG_CHUNK), HG_CHUNK)
            gates = [_hg_gates(p_ref[e, rows, :], lbp_v) for e in exs]
            q, v, gl = [g[0] for g in gates], [g[1] for g in gates], [g[2] for g in gates]
            k = [g[6] for g in gates]
            sts = [[carry[e, h] for h in heads] for e in exs]
            e_all = [_split_dot(a_mat, gates[e][7], NN, 3) for e in exs]
            b = [e_all[e][0:HG_CHUNK] for e in exs]
            qb = [q[e] * jnp.exp(b[e]) for e in exs]
            o = [[_dot(qb[e][:, hl[h]], sts[e][h], NT) for h in heads] for e in exs]
            p = [[jnp.zeros((HG_CHUNK, HG_CHUNK), F32) for _ in heads] for _ in exs]
            for li in range(nl):
                dec = [jnp.exp(e_all[e][HG_CHUNK * (li + 1):HG_CHUNK * (li + 2)]) for e in exs]
                qm, km, mk = [q[e] * dec[e] for e in exs], [k[e] * dec[e] for e in exs], m_ref[li]
                p = [[p[e][h] + mk * _dot(qm[e][:, hl[h]], km[e][:, hl[h]], NT) for h in heads] for e in exs]
            bl = [b[e][HG_CHUNK - 1:HG_CHUNK, :] for e in exs]
            kd = [k[e] * jnp.exp(bl[e] - b[e]) for e in exs]
            pv = [[_dot(p[e][h], v[e][:, hl[h]]) for h in heads] for e in exs]
            upd = [[_dot(v[e][:, hl[h]], kd[e][:, hl[h]], TN) for h in heads] for e in exs]
            for e in exs:
                o_all = (jnp.concatenate([o[e][h] + pv[e][h] for h in heads], axis=1)
                         + _per_head(_lane_sum, q[e] * k[e]) * v[e])
                r = lax.rsqrt(_per_head(_lane_mean, o_all * o_all) + EPS)
                ebl = jnp.exp(bl[e])
                for h in heads:
                    st_ref[e, h, c] = sts[e][h]
                    carry[e, h] = sts[e][h] * ebl[:, hl[h]] + upd[e][h]
                o_ref[e, rows, :] = o_all
                y_ref[e, rows, :] = (o_all * r * ng4) * (gl[e] * jax.nn.sigmoid(gl[e]))
            return 0

        lax.fori_loop(0, nct, chunk, 0)

    y3, o3, states = pl.pallas_call(
        body, name="hgrn2_fwd", grid=(ns,),
        in_specs=[pl.BlockSpec((bsz, ts, HG_COLS), lambda s: (0, s, 0)),
                  pl.BlockSpec((2, hw), lambda s: (0, 0)),
                  pl.BlockSpec((1, HG_D), lambda s: (0, 0)),
                  pl.BlockSpec(a_all.shape, lambda s: (0, 0)),
                  pl.BlockSpec(masks.shape, lambda s: (0, 0, 0))],
        out_specs=(pl.BlockSpec((bsz, ts, hw), lambda s: (0, s, 0)),
                   pl.BlockSpec((bsz, ts, hw), lambda s: (0, s, 0)),
                   pl.BlockSpec((bsz, HG_HEADS, nct, HG_D, HG_D), lambda s: (0, 0, s, 0, 0))),
        out_shape=(jax.ShapeDtypeStruct((bsz, seq, y_width), F32),
                   jax.ShapeDtypeStruct((bsz, seq, hw), F32),
                   jax.ShapeDtypeStruct((bsz, HG_HEADS, nc, HG_D, HG_D), F32)),
        scratch_shapes=[pltpu.VMEM((bsz, HG_HEADS, HG_D, HG_D), F32)],
        compiler_params=_params(("arbitrary",)),
    )(proj.reshape(bsz, seq, HG_COLS), lbp, ng, a_all, masks)
    return y3.reshape(t, y_width), o3.reshape(t, hw), states


def _hg_bwd(proj, lbp, ng, o_all, states, dy, bsz, seq, after=()):
    after = tuple(a for a in after if a is not None)
    t = proj.shape[0]
    nc = seq // HG_CHUNK
    a_np, m_np = _hg_constants()
    a_all = jnp.asarray(a_np, _MXU_DTYPE)
    masks = jnp.asarray(m_np, F32)
    nl = len(HG_LEVELS)
    cs = HG_CHUNK

    ts = min(HG_TILE, seq)
    ns, nct = seq // ts, ts // cs
    hw = HG_HEADS * HG_D

    def body(p_ref, lb_ref, ng_ref, a_ref, m_ref, o_ref, st_ref, dy_ref, *rest):
        dp_ref, dlb_ref, dng_ref, dst_ref = rest[len(after):]
        a_mat = a_ref[...]
        ngv = ng_ref[...]
        ng4 = _tile_lanes(ngv, HG_HEADS)
        last_row = lax.broadcasted_iota(jnp.int32, (cs, hw), 0) == cs - 1
        first = pl.program_id(0) == 0
        heads = range(HG_HEADS)
        exs = range(bsz)
        hl = HG_HEAD_LANES
        lbp_v = lb_ref[...]

        @pl.when(first)
        def _():
            dst_ref[...] = jnp.zeros_like(dst_ref)

        def side_by_side(parts):
            return jnp.concatenate(parts, axis=1)

        def chunk(i, carry):
            dlb_acc, dng_acc = carry
            c = nct - 1 - i
            rows = pl.ds(pl.multiple_of(c * cs, cs), cs)
            gates = [_hg_gates(p_ref[e, rows, :], lbp_v) for e in exs]
            q, v, gl = [g[0] for g in gates], [g[1] for g in gates], [g[2] for g in gates]
            lb, sig, f, k = gates[0][3], [g[4] for g in gates], [g[5] for g in gates], [g[6] for g in gates]
            o = [o_ref[e, rows, :] for e in exs]
            dyv = [dy_ref[e, rows, :] for e in exs]
            sts = [[st_ref[e, h, c] for h in heads] for e in exs]
            dsts = [[dst_ref[e, h] for h in heads] for e in exs]
            e_all = [_split_dot(a_mat, gates[e][7], NN, 3) for e in exs]
            b = [e_all[e][0:cs] for e in exs]
            eb = [jnp.exp(b[e]) for e in exs]
            bl = [b[e][cs - 1:cs, :] for e in exs]
            ebl = [jnp.exp(bl[e]) for e in exs]
            ekd = [jnp.exp(bl[e] - b[e]) for e in exs]
            qb = [q[e] * eb[e] for e in exs]
            kd = [k[e] * ekd[e] for e in exs]
            do, dgl = [], []
            for e in exs:
                sg = jax.nn.sigmoid(gl[e])
                silu = gl[e] * sg
                r = lax.rsqrt(_per_head(_lane_mean, o[e] * o[e]) + EPS)
                dgl.append(dyv[e] * (o[e] * r * ng4) * (sg * (1.0 + gl[e] * (1.0 - sg))))
                u = dyv[e] * silu * ng4
                do.append(r * u - o[e] * (r * r * r) * _per_head(_lane_mean, u * o[e]))
                dng4 = jnp.sum(dyv[e] * silu * o[e] * r, axis=0, keepdims=True)
                dng_acc = dng_acc + ((dng4[:, hl[0]] + dng4[:, hl[1]]) + (dng4[:, hl[2]] + dng4[:, hl[3]]))
            es, qm, km = [], [], []
            p = [[jnp.zeros((cs, cs), F32) for _ in heads] for _ in exs]
            for li in range(nl):
                dec = [jnp.exp(e_all[e][cs * (li + 1):cs * (li + 2)]) for e in exs]
                es.append(dec)
                qm.append([q[e] * dec[e] for e in exs])
                km.append([k[e] * dec[e] for e in exs])
                mk = m_ref[li]
                p = [[p[e][h] + mk * _dot(qm[li][e][:, hl[h]], km[li][e][:, hl[h]], NT) for h in heads] for e in exs]
            dp = [[_dot(do[e][:, hl[h]], v[e][:, hl[h]], NT) for h in heads] for e in exs]
            dv_p = [[_dot(p[e][h], do[e][:, hl[h]], TN) for h in heads] for e in exs]
            dv_s = [[_dot(kd[e][:, hl[h]], dsts[e][h], NT) for h in heads] for e in exs]
            dqb = [side_by_side([_dot(do[e][:, hl[h]], sts[e][h]) for h in heads]) for e in exs]
            dkd = [side_by_side([_dot(v[e][:, hl[h]], dsts[e][h]) for h in heads]) for e in exs]
            new_dst = [[_dot(do[e][:, hl[h]], qb[e][:, hl[h]], TN) for h in heads] for e in exs]
            dv = [side_by_side([dv_p[e][h] + dv_s[e][h] for h in heads]) + _per_head(_lane_sum, q[e] * k[e]) * do[e]
                  for e in exs]
            dq = [dqb[e] * eb[e] for e in exs]
            dk = [dkd[e] * ekd[e] for e in exs]
            de = []
            for e in exs:
                dbl = (jnp.sum(dkd[e] * kd[e], axis=0, keepdims=True)
                       + side_by_side([jnp.sum(dsts[e][h] * sts[e][h], axis=0, keepdims=True) for h in heads]) * ebl[e])
                de.append([dqb[e] * qb[e] - dkd[e] * kd[e] + jnp.where(last_row, dbl, 0.0)])
            for li in range(nl):
                mk = m_ref[li]
                dpm = [[mk * dp[e][h] for h in heads] for e in exs]
                dqm = [side_by_side([_dot(dpm[e][h], km[li][e][:, hl[h]]) for h in heads]) for e in exs]
                dkm = [side_by_side([_dot(dpm[e][h], qm[li][e][:, hl[h]], TN) for h in heads]) for e in exs]
                for e in exs:
                    dq[e] = dq[e] + dqm[e] * es[li][e]
                    dk[e] = dk[e] + dkm[e] * es[li][e]
                    de[e].append(dqm[e] * qm[li][e] + dkm[e] * km[li][e])
            dg = [_split_dot(a_mat, jnp.concatenate(de[e], axis=0), TN, 2) for e in exs]
            for e in exs:
                dpd = _per_head(_lane_sum, do[e] * v[e])
                df = dg[e] / f[e] - (dk[e] + dpd * q[e])
                dp_ref[e, rows, 0:hw] = _mx(dq[e] + dpd * k[e])
                dp_ref[e, rows, hw:2 * hw] = _mx(df * (1.0 - lb) * sig[e] * (1.0 - sig[e]))
                dp_ref[e, rows, 2 * hw:3 * hw] = _mx(dv[e])
                dp_ref[e, rows, 3 * hw:4 * hw] = _mx(dgl[e])
                for h in heads:
                    dst_ref[e, h] = dsts[e][h] * ebl[e][:, hl[h]] + new_dst[e][h]
                dlb_acc = dlb_acc + jnp.sum(df * (1.0 - sig[e]), axis=0, keepdims=True)
            return dlb_acc, dng_acc

        dlb, dng = lax.fori_loop(0, nct, chunk, (jnp.zeros((1, hw), F32), jnp.zeros((1, HG_D), F32)))

        @pl.when(first)
        def _():
            dlb_ref[...] = jnp.zeros_like(dlb_ref)
            dng_ref[...] = jnp.zeros_like(dng_ref)

        mx = jnp.max(lbp_v, axis=0, keepdims=True)
        e = jnp.exp(lbp_v - mx)
        s0 = e[0:1, :] / jnp.sum(e, axis=0, keepdims=True)
        da0 = dlb * s0 * (1.0 - s0)
        dlb_ref[...] += jnp.concatenate([da0, -da0], axis=0)
        dng_ref[...] += dng

    rows3 = lambda w: pl.BlockSpec((bsz, ts, w), lambda s: (0, ns - 1 - s, 0))
    dproj, dlb, dng = pl.pallas_call(
        body, name="hgrn2_bwd", grid=(ns,),
        in_specs=[rows3(HG_COLS),
                  pl.BlockSpec((2, hw), lambda s: (0, 0)),
                  pl.BlockSpec((1, HG_D), lambda s: (0, 0)),
                  pl.BlockSpec(a_all.shape, lambda s: (0, 0)),
                  pl.BlockSpec(masks.shape, lambda s: (0, 0, 0)),
                  rows3(hw),
                  pl.BlockSpec((bsz, HG_HEADS, nct, HG_D, HG_D), lambda s: (0, 0, ns - 1 - s, 0, 0)),
                  rows3(hw)] + [pl.BlockSpec(memory_space=pl.ANY)] * len(after),
        out_specs=(rows3(HG_COLS),
                   pl.BlockSpec((2, hw), lambda s: (0, 0)),
                   pl.BlockSpec((1, HG_D), lambda s: (0, 0))),
        out_shape=(jax.ShapeDtypeStruct((bsz, seq, HG_COLS), _MXU_DTYPE),
                   jax.ShapeDtypeStruct((2, hw), F32),
                   jax.ShapeDtypeStruct((1, HG_D), F32)),
        scratch_shapes=[pltpu.VMEM((bsz, HG_HEADS, HG_D, HG_D), F32)],
        compiler_params=_params(("arbitrary",)),
    )(proj.reshape(bsz, seq, HG_COLS), lbp, ng, a_all, masks, o_all.reshape(bsz, seq, hw), states,
      dy.reshape(bsz, seq, dy.shape[1]), *after)
    return dproj.reshape(t, HG_COLS), dlb, dng


def _sw_constants():
    half = ROT_DIM // 2
    inv = (np.float32(ROPE_THETA) ** (-(np.arange(half, dtype=np.float32) * np.float32(2.0) / np.float32(ROT_DIM)))
           ).astype(np.float32)
    freq = np.zeros((1, 128), np.float32)
    sign = np.zeros((1, 128), np.float32)
    for h in range(2):
        freq[0, 64 * h:64 * h + half] = inv
        freq[0, 64 * h + half:64 * h + 2 * half] = inv
        sign[0, 64 * h:64 * h + half] = -1.0
        sign[0, 64 * h + half:64 * h + 2 * half] = 1.0
    seg = np.kron(np.eye(8, dtype=np.float32), np.full((64, 64), 1.0 / 64.0, np.float32))
    return freq, sign, seg


def _rope_table(pos, *, tm=512):
    t = pos.shape[0]
    tm = min(tm, t)
    freq_np, sign_np, _ = _sw_constants()

    def body(p_ref, f_ref, s_ref, o_ref):
        ang = p_ref[...].astype(F32) * f_ref[...]
        o_ref[:, 0:128] = jnp.cos(ang)
        o_ref[:, 128:256] = jnp.sin(ang) * s_ref[...]

    vec = pl.BlockSpec((1, 128), lambda i: (0, 0))
    return pl.pallas_call(
        body, name="rope_table", grid=(t // tm,),
        in_specs=[pl.BlockSpec((tm, 1), lambda i: (i, 0)), vec, vec],
        out_specs=pl.BlockSpec((tm, 256), lambda i: (i, 0)),
        out_shape=jax.ShapeDtypeStruct((t, 256), F32),
        compiler_params=_params(("parallel",)),
    )(pos, jnp.asarray(freq_np), jnp.asarray(sign_np))


def _tile_lanes(v, times):
    return v if times == 1 else jnp.concatenate([v] * times, axis=1)


def _swap_halves(v):
    w = v.shape[1]
    half = ROT_DIM // 2
    lane = lax.broadcasted_iota(jnp.int32, v.shape, 1) % SW_HD
    return jnp.where(lane < half, pltpu.roll(v, w - half, 1), jnp.where(lane < 2 * half, pltpu.roll(v, half, 1), 0.0))


def _sw_norm_rope(tv, gain, seg, cosv, sinv):
    w = tv.shape[1]
    ms = _split_dot_rhs(tv * tv, seg[0:w, 0:w])
    r = lax.rsqrt(ms + EPS)
    tn = tv * r * gain
    reps = w // 128
    return tn * _tile_lanes(cosv, reps) + _swap_halves(tn) * _tile_lanes(sinv, reps), r


def _split_dot_rhs(v, a):
    hi = _mx(v)
    lo = _mx(v - hi.astype(F32))
    return (lax.dot_general(hi, a, (NN, ((), ())), preferred_element_type=F32)
            + lax.dot_general(lo, a, (NN, ((), ())), preferred_element_type=F32))


def _sw_norm_rope_bwd(dt, tv, r, gain, seg, cosv, sinv):
    w = tv.shape[1]
    reps = w // 128
    dtn = dt * _tile_lanes(cosv, reps) + _swap_halves(dt * _tile_lanes(sinv, reps))
    u = dtn * gain
    dtv = r * u - tv * (r * r * r) * _split_dot_rhs(u * tv, seg[0:w, 0:w])
    return dtv, jnp.sum(dtn * tv * r, axis=0, keepdims=True)


def _sw_scores(qh, kp, kc):
    return _dot(qh, kp, NT), _dot(qh, kc, NT)


SW_SCALE = SW_HD ** -0.5


def _sw_probs(raw, sink, first_block):
    qi = lax.broadcasted_iota(jnp.int32, (SW_BLOCK, SW_BLOCK), 0)
    kj = lax.broadcasted_iota(jnp.int32, (SW_BLOCK, SW_BLOCK), 1)
    ok_prev = jnp.logical_and(kj > qi, jnp.logical_not(first_block))
    ok_cur = kj <= qi
    sp = jnp.where(ok_prev, raw[0], -jnp.inf)
    sc = jnp.where(ok_cur, raw[1], -jnp.inf)
    m = jnp.maximum(jnp.maximum(jnp.max(sp, axis=1, keepdims=True), jnp.max(sc, axis=1, keepdims=True)), sink)
    pp, pc = jnp.exp(sp - m), jnp.exp(sc - m)
    es = jnp.exp(sink - m)
    inv = 1.0 / (jnp.sum(pp, axis=1, keepdims=True) + jnp.sum(pc, axis=1, keepdims=True) + es)
    return pp * inv, pc * inv, es * inv


def _sw_specs(nb):
    def cur(b, n):
        return b * nb + jnp.minimum(n, nb - 1)

    def prev(b, n):
        return b * nb + jnp.maximum(jnp.minimum(n, nb - 1) - 1, 0)

    return cur, prev


def _sw_fwd(proj, rope, qg, kg, sinks, y_in, bsz, seq):
    t = proj.shape[0]
    nb = seq // SW_BLOCK
    seg = jnp.asarray(_sw_constants()[2], _MXU_DTYPE)
    cur, prev = _sw_specs(nb)

    def body(q_ref, kc_ref, kp_ref, vc_ref, vp_ref, rc_ref, rp_ref, qg_ref, kg_ref, sk_ref, seg_ref, yin_ref, y_ref):
        del yin_ref
        n = pl.program_id(1)
        segv = seg_ref[...]
        cos_c, sin_c = rc_ref[:, 0:128], rc_ref[:, 128:256]
        cos_p, sin_p = rp_ref[:, 0:128], rp_ref[:, 128:256]
        qr, _ = _sw_norm_rope(q_ref[...], qg_ref[...] * SW_SCALE, segv, cos_c, sin_c)
        kcr, _ = _sw_norm_rope(kc_ref[...], kg_ref[...], segv, cos_c, sin_c)
        kpr, _ = _sw_norm_rope(kp_ref[...], kg_ref[...], segv, cos_p, sin_p)
        vc, vp = vc_ref[...], vp_ref[...]
        ks = [slice(SW_HD * (h // SW_GROUP), SW_HD * (h // SW_GROUP + 1)) for h in range(SW_HEADS)]
        raw = [_sw_scores(qr[:, SW_HD * h:SW_HD * (h + 1)], kpr[:, ks[h]], kcr[:, ks[h]]) for h in range(SW_HEADS)]
        probs = [_sw_probs(raw[h], sk_ref[0, h], n == 0) for h in range(SW_HEADS)]
        for h in range(SW_HEADS):
            y_ref[:, SW_HD * h:SW_HD * (h + 1)] = _dot(probs[h][0], vp[:, ks[h]]) + _dot(probs[h][1], vc[:, ks[h]])

    rowq = pl.BlockSpec((SW_BLOCK, 512), lambda b, n: (cur(b, n), 0))
    full = lambda a: pl.BlockSpec(a.shape, lambda b, n: (0,) * a.ndim)
    yw = y_in.shape[1]
    return pl.pallas_call(
        body, name="swa_fwd", grid=(bsz, nb),
        in_specs=[rowq,
                  pl.BlockSpec((SW_BLOCK, 128), lambda b, n: (cur(b, n), 4)),
                  pl.BlockSpec((SW_BLOCK, 128), lambda b, n: (prev(b, n), 4)),
                  pl.BlockSpec((SW_BLOCK, 128), lambda b, n: (cur(b, n), 5)),
                  pl.BlockSpec((SW_BLOCK, 128), lambda b, n: (prev(b, n), 5)),
                  pl.BlockSpec((SW_BLOCK, 256), lambda b, n: (cur(b, n), 0)),
                  pl.BlockSpec((SW_BLOCK, 256), lambda b, n: (prev(b, n), 0)),
                  full(qg), full(kg),
                  pl.BlockSpec(memory_space=pltpu.SMEM),
                  full(seg),
                  pl.BlockSpec(memory_space=pl.ANY)],
        out_specs=pl.BlockSpec((SW_BLOCK, 512), lambda b, n: (cur(b, n), 1)),
        out_shape=jax.ShapeDtypeStruct((t, yw), F32),
        input_output_aliases={11: 0},
        compiler_params=_params(("parallel", "parallel")),
    )(proj, proj, proj, proj, proj, rope, rope, qg, kg, sinks, seg, y_in)


def _sw_bwd(proj, rope, qg, kg, sinks, y, dy, bsz, seq):
    t = proj.shape[0]
    nb = seq // SW_BLOCK
    seg = jnp.asarray(_sw_constants()[2], _MXU_DTYPE)
    cur, prev = _sw_specs(nb)

    def body(q_ref, kc_ref, kp_ref, vc_ref, vp_ref, rc_ref, rp_ref, qg_ref, kg_ref, sk_ref, seg_ref,
             y_ref, dy_ref, dp_ref, dqg_ref, dkg_ref, dsk_ref,
             dq_car, dkv_car, dqr_s, dkc_s, dkp_s, dvc_s, dvp_s, gq_acc, gk_acc, sk_acc):
        b, n = pl.program_id(0), pl.program_id(1)
        first = jnp.logical_and(b == 0, n == 0)
        last = jnp.logical_and(b == pl.num_programs(0) - 1, n == nb)

        @pl.when(first)
        def _():
            gq_acc[...] = jnp.zeros_like(gq_acc)
            gk_acc[...] = jnp.zeros_like(gk_acc)
            sk_acc[...] = jnp.zeros_like(sk_acc)

        @pl.when(n < nb)
        def _():
            segv = seg_ref[...]
            cos_c, sin_c = rc_ref[:, 0:128], rc_ref[:, 128:256]
            cos_p, sin_p = rp_ref[:, 0:128], rp_ref[:, 128:256]
            qv, kcv, kpv = q_ref[...], kc_ref[...], kp_ref[...]
            qgain = qg_ref[...] * SW_SCALE
            qr, rq = _sw_norm_rope(qv, qgain, segv, cos_c, sin_c)
            kcr, rkc = _sw_norm_rope(kcv, kg_ref[...], segv, cos_c, sin_c)
            kpr, rkp = _sw_norm_rope(kpv, kg_ref[...], segv, cos_p, sin_p)
            vc, vp = vc_ref[...], vp_ref[...]
            lane = lax.broadcasted_iota(jnp.int32, (1, 128), 1)
            dsk = jnp.zeros((1, 128), F32)
            heads = range(SW_HEADS)
            ks = [slice(SW_HD * (h // SW_GROUP), SW_HD * (h // SW_GROUP + 1)) for h in heads]
            hs = [slice(SW_HD * h, SW_HD * (h + 1)) for h in heads]
            qh = [qr[:, hs[h]] for h in heads]
            doh = [dy_ref[:, hs[h]] for h in heads]
            raw = [_sw_scores(qh[h], kpr[:, ks[h]], kcr[:, ks[h]]) for h in heads]
            dpp = [_dot(doh[h], vp[:, ks[h]], NT) for h in heads]
            dpc = [_dot(doh[h], vc[:, ks[h]], NT) for h in heads]
            probs = [_sw_probs(raw[h], sk_ref[0, h], n == 0) for h in heads]
            dsp, dsc = [], []
            for h in heads:
                pp, pc, ps = probs[h]
                delta = jnp.sum(doh[h] * y_ref[:, hs[h]], axis=1, keepdims=True)
                dsp.append(pp * (dpp[h] - delta))
                dsc.append(pc * (dpc[h] - delta))
                dsk = dsk + jnp.where(lane == h, -jnp.sum(ps * delta), 0.0)
            for h in heads:
                dqr_s[:, hs[h]] = _dot(dsp[h], kpr[:, ks[h]]) + _dot(dsc[h], kcr[:, ks[h]])
            for kv in range(SW_KV_HEADS):
                group = range(SW_GROUP * kv, SW_GROUP * (kv + 1))
                kvs = slice(SW_HD * kv, SW_HD * (kv + 1))
                dvp_s[:, kvs] = sum(_dot(probs[h][0], doh[h], TN) for h in group)
                dvc_s[:, kvs] = sum(_dot(probs[h][1], doh[h], TN) for h in group)
                dkp_s[:, kvs] = sum(_dot(dsp[h], qh[h], TN) for h in group)
                dkc_s[:, kvs] = sum(_dot(dsc[h], qh[h], TN) for h in group)
            dq, gq = _sw_norm_rope_bwd(dqr_s[...], qv, rq, qgain, segv, cos_c, sin_c)
            dkc, gkc = _sw_norm_rope_bwd(dkc_s[...], kcv, rkc, kg_ref[...], segv, cos_c, sin_c)
            dkp, gkp = _sw_norm_rope_bwd(dkp_s[...], kpv, rkp, kg_ref[...], segv, cos_p, sin_p)
            gq_acc[...] += gq
            gk_acc[...] += gkc + gkp
            sk_acc[...] += dsk

            @pl.when(n > 0)
            def _():
                dp_ref[:, 0:512] = _mx(dq_car[...])
                dp_ref[:, 512:640] = _mx(dkv_car[:, 0:128] + dkp)
                dp_ref[:, 640:768] = _mx(dkv_car[:, 128:256] + dvp_s[...])

            dq_car[...] = dq
            dkv_car[:, 0:128] = dkc
            dkv_car[:, 128:256] = dvc_s[...]

        @pl.when(n == nb)
        def _():
            dp_ref[:, 0:512] = _mx(dq_car[...])
            dp_ref[:, 512:768] = _mx(dkv_car[...])

        @pl.when(last)
        def _():
            gq = gq_acc[...] * SW_SCALE
            acc = gq[:, 0:SW_HD]
            for h in range(1, SW_HEADS):
                acc = acc + gq[:, SW_HD * h:SW_HD * (h + 1)]
            dqg_ref[...] = acc
            gk = gk_acc[...]
            dkg_ref[...] = gk[:, 0:SW_HD] + gk[:, SW_HD:2 * SW_HD]
            dsk_ref[...] = sk_acc[...]

    rowq = pl.BlockSpec((SW_BLOCK, 512), lambda b, n: (cur(b, n), 0))
    full = lambda a: pl.BlockSpec(a.shape, lambda b, n: (0,) * a.ndim)

    def out_row(b, n):
        return b * nb + jnp.maximum(n - 1, 0)

    return pl.pallas_call(
        body, name="swa_bwd", grid=(bsz, nb + 1),
        in_specs=[rowq,
                  pl.BlockSpec((SW_BLOCK, 128), lambda b, n: (cur(b, n), 4)),
                  pl.BlockSpec((SW_BLOCK, 128), lambda b, n: (prev(b, n), 4)),
                  pl.BlockSpec((SW_BLOCK, 128), lambda b, n: (cur(b, n), 5)),
                  pl.BlockSpec((SW_BLOCK, 128), lambda b, n: (prev(b, n), 5)),
                  pl.BlockSpec((SW_BLOCK, 256), lambda b, n: (cur(b, n), 0)),
                  pl.BlockSpec((SW_BLOCK, 256), lambda b, n: (prev(b, n), 0)),
                  full(qg), full(kg),
                  pl.BlockSpec(memory_space=pltpu.SMEM),
                  full(seg),
                  pl.BlockSpec((SW_BLOCK, 512), lambda b, n: (cur(b, n), 1)),
                  pl.BlockSpec((SW_BLOCK, 512), lambda b, n: (cur(b, n), 1))],
        out_specs=(pl.BlockSpec((SW_BLOCK, SW_COLS), lambda b, n: (out_row(b, n), 0)),
                   pl.BlockSpec((1, SW_HD), lambda b, n: (0, 0)),
                   pl.BlockSpec((1, SW_HD), lambda b, n: (0, 0)),
                   pl.BlockSpec((1, 128), lambda b, n: (0, 0))),
        out_shape=(jax.ShapeDtypeStruct((t, SW_COLS), _MXU_DTYPE),
                   jax.ShapeDtypeStruct((1, SW_HD), F32),
                   jax.ShapeDtypeStruct((1, SW_HD), F32),
                   jax.ShapeDtypeStruct((1, 128), F32)),
        scratch_shapes=[pltpu.VMEM((SW_BLOCK, 512), F32), pltpu.VMEM((SW_BLOCK, 256), F32),
                        pltpu.VMEM((SW_BLOCK, 512), F32),
                        pltpu.VMEM((SW_BLOCK, 128), F32), pltpu.VMEM((SW_BLOCK, 128), F32),
                        pltpu.VMEM((SW_BLOCK, 128), F32), pltpu.VMEM((SW_BLOCK, 128), F32),
                        pltpu.VMEM((1, 512), F32), pltpu.VMEM((1, 128), F32), pltpu.VMEM((1, 128), F32)],
        compiler_params=_params(("arbitrary", "arbitrary")),
    )(proj, proj, proj, proj, proj, rope, rope, qg, kg, sinks, seg, y, dy)


def _head_rms(tv, gain):
    r = lax.rsqrt(jnp.mean(tv * tv, axis=1, keepdims=True) + EPS)
    return tv * r * gain, r


def _head_rms_bwd(dtn, tv, r, gain):
    u = dtn * gain
    return r * u - tv * (r * r * r) * jnp.mean(u * tv, axis=1, keepdims=True), jnp.sum(dtn * tv * r, axis=0, keepdims=True)


def _xa_softmax(raw):
    s = raw * (XA_HD ** -0.5)
    e = jnp.exp(s - jnp.max(s, axis=1, keepdims=True))
    return e * (1.0 / jnp.sum(e, axis=1, keepdims=True))


def _xa_fwd(qx, kvx, qg, kg, bsz, seq, mlen, *, tq=512):
    t = qx.shape[0]
    tq = min(tq, seq)
    nq = seq // tq
    w = XA_HEADS * XA_HD

    def body(q_ref, kv_ref, qg_ref, kg_ref, o_ref):
        heads = range(XA_HEADS)
        hs = [slice(XA_HD * h, XA_HD * (h + 1)) for h in heads]
        qn = [_head_rms(q_ref[:, hs[h]], qg_ref[...])[0] for h in heads]
        kn = [_head_rms(kv_ref[:, hs[h]], kg_ref[...])[0] for h in heads]
        raw = [_dot(qn[h], kn[h], NT) for h in heads]
        p = [_xa_softmax(raw[h]) for h in heads]
        for h in heads:
            o_ref[:, hs[h]] = _dot(p[h], kv_ref[:, w + XA_HD * h:w + XA_HD * (h + 1)]).astype(o_ref.dtype)

    vec = pl.BlockSpec((1, XA_HD), lambda b, i: (0, 0))
    return pl.pallas_call(
        body, name="xattn_fwd", grid=(bsz, nq),
        in_specs=[pl.BlockSpec((tq, w), lambda b, i: (b * nq + i, 0)),
                  pl.BlockSpec((mlen, 2 * w), lambda b, i: (b, 0)), vec, vec],
        out_specs=pl.BlockSpec((tq, w), lambda b, i: (b * nq + i, 0)),
        out_shape=jax.ShapeDtypeStruct((t, w), _MXU_DTYPE),
        compiler_params=_params(("parallel", "parallel")),
    )(qx, kvx, qg, kg)


def _xa_bwd(qx, kvx, qg, kg, do, bsz, seq, mlen, *, tq=512):
    t = qx.shape[0]
    tq = min(tq, seq)
    nq = seq // tq
    w = XA_HEADS * XA_HD
    scale = XA_HD ** -0.5

    def body(q_ref, kv_ref, qg_ref, kg_ref, do_ref, dq_ref, dkv_ref, dqg_ref, dkg_ref):
        b, i = pl.program_id(0), pl.program_id(1)

        @pl.when(jnp.logical_and(b == 0, i == 0))
        def _():
            dqg_ref[...] = jnp.zeros_like(dqg_ref)
            dkg_ref[...] = jnp.zeros_like(dkg_ref)

        @pl.when(i == 0)
        def _():
            dkv_ref[...] = jnp.zeros_like(dkv_ref)

        heads = range(XA_HEADS)
        hs = [slice(XA_HD * h, XA_HD * (h + 1)) for h in heads]
        vs = [slice(w + XA_HD * h, w + XA_HD * (h + 1)) for h in heads]
        qv = [q_ref[:, hs[h]] for h in heads]
        kv = [kv_ref[:, hs[h]] for h in heads]
        doh = [do_ref[:, hs[h]] for h in heads]
        qn = [_head_rms(qv[h], qg_ref[...]) for h in heads]
        kn = [_head_rms(kv[h], kg_ref[...]) for h in heads]
        raw = [_dot(qn[h][0], kn[h][0], NT) for h in heads]
        dp = [_dot(doh[h], kv_ref[:, vs[h]], NT) for h in heads]
        p = [_xa_softmax(raw[h]) for h in heads]
        ds = [p[h] * (dp[h] - jnp.sum(p[h] * dp[h], axis=1, keepdims=True)) * scale for h in heads]
        dqn = [_dot(ds[h], kn[h][0]) for h in heads]
        dkn = [_dot(ds[h], qn[h][0], TN) for h in heads]
        dvv = [_dot(p[h], doh[h], TN) for h in heads]
        gq_sum = jnp.zeros((1, XA_HD), F32)
        gk_sum = jnp.zeros((1, XA_HD), F32)
        for h in heads:
            dqv, gq = _head_rms_bwd(dqn[h], qv[h], qn[h][1], qg_ref[...])
            dkv, gk = _head_rms_bwd(dkn[h], kv[h], kn[h][1], kg_ref[...])
            dq_ref[:, hs[h]] = dqv.astype(dq_ref.dtype)
            dkv_ref[:, hs[h]] += dkv
            dkv_ref[:, vs[h]] += dvv[h]
            gq_sum = gq_sum + gq
            gk_sum = gk_sum + gk
        dqg_ref[...] += gq_sum
        dkg_ref[...] += gk_sum

    vec = pl.BlockSpec((1, XA_HD), lambda b, i: (0, 0))
    row = pl.BlockSpec((tq, w), lambda b, i: (b * nq + i, 0))
    mem = pl.BlockSpec((mlen, 2 * w), lambda b, i: (b, 0))
    return pl.pallas_call(
        body, name="xattn_bwd", grid=(bsz, nq),
        in_specs=[row, mem, vec, vec, row],
        out_specs=(row, mem, vec, vec),
        out_shape=(jax.ShapeDtypeStruct((t, w), _MXU_DTYPE), jax.ShapeDtypeStruct((bsz * mlen, 2 * w), F32),
                   jax.ShapeDtypeStruct((1, XA_HD), F32), jax.ShapeDtypeStruct((1, XA_HD), F32)),
        compiler_params=_params(("arbitrary", "arbitrary")),
    )(qx, kvx, qg, kg, do)


def _loss_finish(sq_row, d_model):
    def body(s_ref, o_ref):
        o_ref[...] = jnp.zeros_like(o_ref) + 0.5 * jnp.sum(s_ref[...]) / float(d_model)

    return pl.pallas_call(body, name="loss_finish", out_shape=jax.ShapeDtypeStruct((1, 128), F32))(sq_row)


def _adamw_math(w, g, m, v):
    m = ADAM_B1 * m + (1.0 - ADAM_B1) * g
    v = ADAM_B2 * v + (1.0 - ADAM_B2) * (g * g)
    m_hat = m / (1.0 - ADAM_B1 ** ADAM_STEP)
    v_hat = v / (1.0 - ADAM_B2 ** ADAM_STEP)
    return -ADAM_LR * (m_hat / (jnp.sqrt(v_hat) + ADAM_EPS) + ADAM_WD * w), m, v


def _adamw_big(w, g, m, v, *, name, tr=512):
    r, c = w.shape
    tr = min(tr, r)

    def body(w_ref, g_ref, m_ref, v_ref, go_ref, d_ref, mo_ref, vo_ref):
        gv = g_ref[...]
        d, mn, vn = _adamw_math(w_ref[...], gv, m_ref[...], v_ref[...])
        go_ref[...] = gv
        d_ref[...] = d
        mo_ref[...] = mn
        vo_ref[...] = vn

    spec = pl.BlockSpec((tr, c), lambda i: (i, 0))
    shp = jax.ShapeDtypeStruct((r, c), F32)
    return pl.pallas_call(
        body, name=name, grid=(r // tr,), in_specs=[spec] * 4, out_specs=(spec,) * 4, out_shape=(shp,) * 4,
        compiler_params=_params(("parallel",)),
    )(w, g, m, v)


def _adamw_small(ws, gs, ms, vs):
    n = len(ws)

    def body(*refs):
        for i in range(n):
            d, mn, vn = _adamw_math(refs[i][...], refs[n + i][...], refs[2 * n + i][...], refs[3 * n + i][...])
            refs[4 * n + i][...] = d
            refs[5 * n + i][...] = mn
            refs[6 * n + i][...] = vn

    shapes = tuple(jax.ShapeDtypeStruct(w.shape, F32) for w in ws)
    return pl.pallas_call(body, name="adamw_small", out_shape=shapes * 3)(*ws, *gs, *ms, *vs)


def _add_halves(g, recv, c_idx, *, name, tr=512):
    _, r, c = g.shape
    h = r // 2
    tr = min(tr, h)
    nt = h // tr

    def body(c_ref, g_ref, r_ref, o_ref):
        del c_ref
        o_ref[...] = g_ref[...] + r_ref[...]

    return pl.pallas_call(
        body, name=name,
        grid_spec=pltpu.PrefetchScalarGridSpec(
            num_scalar_prefetch=1, grid=(4, nt),
            in_specs=[pl.BlockSpec((None, tr, c), lambda k, i, cr: (k, cr[0] * nt + i, 0)),
                      pl.BlockSpec((None, tr, c), lambda k, i, cr: (k, i, 0))],
            out_specs=pl.BlockSpec((None, tr, c), lambda k, i, cr: (k, i, 0))),
        out_shape=jax.ShapeDtypeStruct((4, h, c), F32),
        compiler_params=_params(("parallel", "parallel")),
    )(c_idx, g, recv)


def _add_chips(p, recv, place_idx, *, name, tr=512, after=()):
    _, h, c = p.shape
    tr = min(tr, h)
    nt = h // tr

    def body(pi_ref, p_ref, r_ref, *rest):
        del pi_ref
        rest[-1][...] = ((p_ref[...] + r_ref[0]) + r_ref[1]) + r_ref[2]

    return pl.pallas_call(
        body, name=name,
        grid_spec=pltpu.PrefetchScalarGridSpec(
            num_scalar_prefetch=1, grid=(nt,),
            in_specs=[pl.BlockSpec((None, tr, c), lambda i, pi: (pi[0], i, 0)),
                      pl.BlockSpec((3, tr, c), lambda i, pi: (0, i, 0))] + [pl.BlockSpec(memory_space=pl.ANY)] * len(after),
            out_specs=pl.BlockSpec((tr, c), lambda i, pi: (pi[1] * nt + i, 0))),
        out_shape=jax.ShapeDtypeStruct((2 * h, c), F32),
        compiler_params=_params(("parallel",)),
    )(place_idx, p, recv, *after)


def _place_shard(shard, place_idx, *, name, tr=512, after=()):
    r, c = shard.shape
    tr = min(tr, r)

    def body(pi_ref, s_ref, *rest):
        del pi_ref
        rest[-1][...] = s_ref[...]

    return pl.pallas_call(
        body, name=name,
        grid_spec=pltpu.PrefetchScalarGridSpec(
            num_scalar_prefetch=1, grid=(r // tr,),
            in_specs=[pl.BlockSpec((tr, c), lambda i, pi: (i, 0))] + [pl.BlockSpec(memory_space=pl.ANY)] * len(after),
            out_specs=pl.BlockSpec((None, tr, c), lambda i, pi: (pi[0], i, 0))),
        out_shape=jax.ShapeDtypeStruct((4, r, c), shard.dtype),
        compiler_params=_params(("parallel",)),
    )(place_idx, shard, *after)


def _place():
    x, y, c = lax.axis_index("x"), lax.axis_index("y"), lax.axis_index("c")
    chips = [(1 - x, y), (x, 1 - y), (1 - x, 1 - y)]
    return x, y, c, chips


ANY = pl.BlockSpec(memory_space=pl.ANY)


def _exchange_halves(grads, name):
    n = len(grads)

    def body(*refs):
        ins, outs = refs[:n], refs[n:2 * n]
        send_sems, recv_sems = refs[2 * n:]
        x, y, c, _ = _place()

        def copy(a):
            h = ins[a].shape[1] // 2
            return pltpu.make_async_remote_copy(
                src_ref=ins[a].at[:, pl.ds((1 - c) * h, h), :], dst_ref=outs[a],
                send_sem=send_sems.at[a], recv_sem=recv_sems.at[a], device_id=(x, y, 1 - c), device_id_type=MESH)

        for a in range(n):
            copy(a).start()
        for a in range(n):
            copy(a).wait_recv()
        for a in range(n):
            copy(a).wait_send()

    return pl.pallas_call(
        body, name=name,
        in_specs=[ANY] * n, out_specs=tuple([ANY] * n),
        out_shape=tuple(jax.ShapeDtypeStruct((4, g.shape[1] // 2, g.shape[2]), g.dtype) for g in grads),
        scratch_shapes=[pltpu.SemaphoreType.DMA((n,)), pltpu.SemaphoreType.DMA((n,))],
    )(*grads)


HBM = pl.BlockSpec(memory_space=pltpu.HBM)
SEM = pl.BlockSpec(memory_space=pltpu.SEMAPHORE)
EFFECT = pltpu.SideEffectType.DATAFLOW_SIDE_EFFECTING


def _in_hbm(a):
    return pltpu.with_memory_space_constraint(a, pltpu.HBM)


def _split_copy_calls(name, srcs, lands, n_copies, make_copies):
    ns, nl = len(srcs), len(lands)
    nb = ns + nl

    def start(after=()):
        n_after = len(after)

        def body(*refs):
            outs = refs[nb + n_after:]
            copies = make_copies(refs[:ns], refs[ns:nb], outs[0], outs[1])
            for cp in copies:
                cp.start()
            token = refs[-1]
            token[...] = jnp.zeros_like(token)

        bufs = [_in_hbm(a) for a in list(srcs) + list(lands)]
        out = pl.pallas_call(
            body, name=name + "_start",
            out_shape=(pltpu.SemaphoreType.DMA((n_copies,)), pltpu.SemaphoreType.DMA((n_copies,)),
                       *[pltpu.HBM(a.shape, a.dtype) for a in bufs], jax.ShapeDtypeStruct((8, 128), F32)),
            in_specs=[HBM] * nb + [pl.BlockSpec(memory_space=pl.ANY)] * n_after,
            out_specs=(SEM, SEM, *[HBM] * nb, pl.BlockSpec(memory_space=pltpu.VMEM)),
            input_output_aliases={i: 2 + i for i in range(nb)},
            compiler_params=pltpu.CompilerParams(has_side_effects=EFFECT),
        )(*bufs, *after)
        return dict(send=out[0], recv=out[1], bufs=list(out[2:2 + nb]), token=out[-1])

    def wait(state, after):
        def body(*refs):
            copies = make_copies(refs[:ns], refs[ns:nb], refs[nb], refs[nb + 1])
            for cp in copies:
                cp.wait_send()
            for cp in copies:
                cp.wait_recv()

        bufs = state["bufs"]
        out = pl.pallas_call(
            body, name=name + "_wait",
            out_shape=tuple(pltpu.HBM(a.shape, a.dtype) for a in bufs),
            in_specs=[HBM] * nb + [SEM, SEM] + [pl.BlockSpec(memory_space=pl.ANY)] * len(after),
            out_specs=tuple([HBM] * nb),
            input_output_aliases={i: i for i in range(nb)},
            compiler_params=pltpu.CompilerParams(has_side_effects=EFFECT),
        )(*bufs, state["send"], state["recv"], *after)
        return list(out[:ns]), list(out[ns:])

    return start, wait


def _scatter_chips_split(name, parts):
    n = len(parts)
    lands = [lax.empty((3,) + p.shape[1:], p.dtype) for p in parts]

    def make_copies(srcs, lnds, send_sems, recv_sems):
        _, _, c, chips = _place()
        return [pltpu.make_async_remote_copy(
            src_ref=srcs[a].at[2 * px + py], dst_ref=lnds[a].at[j], send_sem=send_sems.at[a * 3 + j],
            recv_sem=recv_sems.at[a * 3 + j], device_id=(px, py, c), device_id_type=MESH)
            for a in range(n) for j, (px, py) in enumerate(chips)]

    return _split_copy_calls(name, parts, lands, 3 * n, make_copies)


def _exchange_halves_split(name, grads):
    n = len(grads)
    lands = [lax.empty((4, g.shape[1] // 2, g.shape[2]), g.dtype) for g in grads]

    def make_copies(srcs, lnds, send_sems, recv_sems):
        x, y, c, _ = _place()
        out = []
        for a in range(n):
            h = srcs[a].shape[1] // 2
            out.append(pltpu.make_async_remote_copy(
                src_ref=srcs[a].at[:, pl.ds((1 - c) * h, h), :], dst_ref=lnds[a], send_sem=send_sems.at[a],
                recv_sem=recv_sems.at[a], device_id=(x, y, 1 - c), device_id_type=MESH))
        return out

    return _split_copy_calls(name, grads, lands, n, make_copies)


def _gather_chips_split(name, shards, lands):
    n = len(shards)

    def make_copies(srcs, lnds, send_sems, recv_sems):
        x, y, c, chips = _place()
        out = []
        for a in range(n):
            h = srcs[a].shape[0] // 2
            for j, (px, py) in enumerate(chips):
                out.append(pltpu.make_async_remote_copy(
                    src_ref=srcs[a].at[pl.ds(c * h, h), :], dst_ref=lnds[a].at[2 * x + y, pl.ds(c * h, h), :],
                    send_sem=send_sems.at[a * 3 + j], recv_sem=recv_sems.at[a * 3 + j],
                    device_id=(px, py, c), device_id_type=MESH))
        return out

    return _split_copy_calls(name, shards, lands, 3 * n, make_copies)


def _gather_finish(gathered, name):
    n = len(gathered)

    def body(*refs):
        outs = refs[n:2 * n]
        send_sems, recv_sems = refs[2 * n:]
        x, y, c, chips = _place()

        def copy(a, j, chip_idx, which):
            h = outs[a].shape[1] // 2
            rows = outs[a].at[chip_idx, pl.ds(which * h, h), :]
            return pltpu.make_async_remote_copy(
                src_ref=rows, dst_ref=rows, send_sem=send_sems.at[a * 3 + j], recv_sem=recv_sems.at[a * 3 + j],
                device_id=(x, y, 1 - c), device_id_type=MESH)

        for a in range(n):
            for j, (px, py) in enumerate(chips):
                copy(a, j, 2 * px + py, c).start()
        for a in range(n):
            for j, (px, py) in enumerate(chips):
                copy(a, j, 2 * px + py, 1 - c).wait_recv()
        for a in range(n):
            for j, (px, py) in enumerate(chips):
                copy(a, j, 2 * px + py, c).wait_send()

    return pl.pallas_call(
        body, name=name,
        in_specs=[ANY] * n, out_specs=tuple([ANY] * n),
        out_shape=tuple(jax.ShapeDtypeStruct(g.shape, g.dtype) for g in gathered),
        input_output_aliases={i: i for i in range(n)},
        scratch_shapes=[pltpu.SemaphoreType.DMA((3 * n,)), pltpu.SemaphoreType.DMA((3 * n,))],
    )(*gathered)


def _gather_forward_split(name, gathered):
    n = len(gathered)

    def make_copies(srcs, lnds, send_sems, recv_sems):
        x, y, c, chips = _place()
        out = []
        for a in range(n):
            h = lnds[a].shape[1] // 2
            for j, (px, py) in enumerate(chips):
                rows = lnds[a].at[2 * px + py, pl.ds(c * h, h), :]
                out.append(pltpu.make_async_remote_copy(
                    src_ref=rows, dst_ref=rows, send_sem=send_sems.at[a * 3 + j], recv_sem=recv_sems.at[a * 3 + j],
                    device_id=(x, y, 1 - c), device_id_type=MESH))
        return out

    return _split_copy_calls(name, [], gathered, 3 * n, make_copies)


def _join_halves_split(name, fulls):
    n = len(fulls)

    def make_copies(srcs, lnds, send_sems, recv_sems):
        x, y, c, _ = _place()
        out = []
        for a in range(n):
            h = lnds[a].shape[0] // 2
            rows = lnds[a].at[pl.ds(c * h, h), :]
            out.append(pltpu.make_async_remote_copy(
                src_ref=rows, dst_ref=rows, send_sem=send_sems.at[a], recv_sem=recv_sems.at[a],
                device_id=(x, y, 1 - c), device_id_type=MESH))
        return out

    return _split_copy_calls(name, [], fulls, n, make_copies)


def _all_gather_small_split(sm):
    r, w = sm.shape

    def make_copies(srcs, lnds, send_sems, recv_sems):
        x, y, c, _ = _place()
        me = 4 * x + 2 * y + c
        rel = [(dx, dy, dc) for dx in (0, 1) for dy in (0, 1) for dc in (0, 1)][1:]
        return [pltpu.make_async_remote_copy(
            src_ref=srcs[0], dst_ref=lnds[0].at[me], send_sem=send_sems.at[k], recv_sem=recv_sems.at[k],
            device_id=(1 - x if dx else x, 1 - y if dy else y, 1 - c if dc else c), device_id_type=MESH)
            for k, (dx, dy, dc) in enumerate(rel)]

    return _split_copy_calls("all_gather_small", [sm], [lax.empty((8, r, w), sm.dtype)], 7, make_copies)


def _sum_devices(sm, gathered, me_idx):
    def body(me_ref, sm_ref, g_ref, o_ref):
        own = sm_ref[...]
        acc = jnp.where(me_ref[0] == 0, own, g_ref[0])
        for d in range(1, 8):
            acc = acc + jnp.where(me_ref[0] == d, own, g_ref[d])
        o_ref[...] = acc

    vm = pl.BlockSpec(memory_space=pltpu.VMEM)
    return pl.pallas_call(
        body, name="sum_devices", in_specs=[pl.BlockSpec(memory_space=pltpu.SMEM), vm, vm], out_specs=vm,
        out_shape=jax.ShapeDtypeStruct(sm.shape, F32),
    )(me_idx, sm, gathered)


def _local_step(x3, mem3, pos2, target3, small, comm):
    bsz, seq, d = x3.shape
    mlen = mem3.shape[1]
    t = bsz * seq
    tok = comm.begin()
    x = x3.reshape(t, d)
    mem = mem3.reshape(bsz * mlen, d)
    target = target3.reshape(t, d)
    rope = _rope_table(pos2.reshape(t, 1))
    qg_t = jnp.tile(small["sw_q_norm_g"], (1, SW_HEADS))
    kg_t = jnp.tile(small["sw_k_norm_g"], (1, SW_KV_HEADS))

    hn1 = _rms_fwd(x, small["norm1_g"], name="rms1_fwd", after=tok)
    w = comm.first(hn1)
    proj_hg = _mm(hn1, w["w_in_hg"], NN, t, HG_COLS, d, name="proj_hg", tk=d, after=(w.get("token"),))[0]
    proj_sw = _mm(hn1, w["w_in_sw"], NN, t, SW_COLS, d, name="proj_sw", tk=d)[0]
    y_mix, o_hg, states = _hg_fwd(proj_hg, small["hg_lower_bounds"], small["hg_norm_g"], bsz, seq, y_width=1024)
    y_mix = _sw_fwd(proj_sw, rope, qg_t, kg_t, small["sw_sinks"], y_mix, bsz, seq)
    w_in_hg, w_in_sw = w["w_in_hg"], w["w_in_sw"]
    w = comm.rest(y_mix)
    h1, hn2 = _mm(y_mix, w["w_out"], NN, t, d, 1024, name="out_proj", tk=1024, extras=(x,), rows=(small["norm2_g"],),
                  epilogue=_residual_rms, out_dtypes=(F32, _MXU_DTYPE), after=(w.get("token"),))
    mn = _rms_fwd(mem, small["mem_norm_g"], name="rms_mem_fwd")
    qx = _mm(hn2, w["wq"], NN, t, 512, d, name="xa_q", tk=d)[0]
    kvx = _mm(mn, w["wkv"], NN, bsz * mlen, 1024, d, name="xa_kv", tk=d)[0]
    ox = _xa_fwd(qx, kvx, small["xa_q_norm_g"], small["xa_k_norm_g"], bsz, seq, mlen)
    h2, hn3 = _mm(ox, w["wo"], NN, t, d, 512, name="xa_o", tk=512, extras=(h1,), rows=(small["norm3_g"],),
                  epilogue=_residual_rms, out_dtypes=(F32, _MXU_DTYPE))
    w = {**w, **comm.mlp(hn3)}
    ff = w["down"].shape[0]
    ffs = ff // 4

    def relu_sq(acc):
        a = jnp.maximum(acc, 0.0)
        return a, a * a

    act, act2 = _mm(hn3, w["up"], NN, t, ff, d, name="mlp_up", tm=2048, tn=ffs, tk=d,
                    b_spec=pl.BlockSpec((None, d, ffs), lambda i, j, kk: (j, 0, 0)),
                    epilogue=relu_sq, out_dtypes=(_MXU_DTYPE, _MXU_DTYPE))
    inv_d = 1.0 / d

    def loss_cotangent(acc, res, tgt):
        diff = acc + res - tgt
        v = diff * inv_d
        return v, v, jnp.sum(diff * diff, axis=0, keepdims=True)

    dy, dy_mx, sq_row = _mm(act2, w["down"], NN, t, d, ff, name="mlp_down", tk=2048, extras=(h2, target),
                            epilogue=loss_cotangent, out_dtypes=(F32, _MXU_DTYPE), row_sums=1)
    loss_row = _loss_finish(sq_row, d)

    dz = _mm(dy_mx, w["down"], NT, t, ff, d, name="d_act", tm=2048, tk=d, extras=(act,),
             epilogue=lambda acc, a: (acc * (2.0 * a.astype(F32)),), out_dtypes=(_MXU_DTYPE,))[0]
    g_down = _mm(act2, dy_mx, TN, ff, d, t, name="g_down", tk=t)[0]
    g_up = _mm(hn3, dz, TN, d, ff, t, name="g_up", tn=ffs, tk=t,
               out_shape=(jax.ShapeDtypeStruct((4, d, ffs), F32),),
               out_spec=(pl.BlockSpec((None, min(1024, d), ffs), lambda i, j, kk: (j, i, 0)),))[0]
    tok = comm.grads("mlp", dict(up=g_up, down=g_down))
    dh2, dh2_mx, g_norm3 = _mm(dz, w["up"], NT, t, d, ff, name="d_hn3", tk=ffs, after=tok,
                               b_spec=pl.BlockSpec((None, min(1024, d), ffs), lambda i, j, kk: (kk, j, 0)),
                               extras=(h2, dy), rows=(small["norm3_g"],), epilogue=_rms_bwd_residual,
                               out_dtypes=(F32, _MXU_DTYPE), row_sums=1)
    d_ox = _mm(dh2_mx, w["wo"], NT, t, 512, d, name="d_ox", tk=d)[0]
    g_wo = _mm(ox, dh2_mx, TN, 512, d, t, name="g_wo", tk=t)[0]
    d_qx, d_kvx, g_xq, g_xk = _xa_bwd(qx, kvx, small["xa_q_norm_g"], small["xa_k_norm_g"], d_ox, bsz, seq, mlen)
    g_wq = _mm(hn2, d_qx, TN, d, 512, t, name="g_wq")[0]
    g_wkv = _mm(mn, d_kvx, TN, d, 1024, bsz * mlen, name="g_wkv")[0]
    dh1, dh1_mx, g_norm2 = _mm(d_qx, w["wq"], NT, t, d, 512, name="d_hn2", tk=512, extras=(h1, dh2),
                               rows=(small["norm2_g"],), epilogue=_rms_bwd_residual, out_dtypes=(F32, _MXU_DTYPE),
                               row_sums=1)
    dmn = _mm(d_kvx, w["wkv"], NT, bsz * mlen, d, 1024, name="d_mn", tk=1024)[0]
    g_memn = _rms_gain_grad(mem, small["mem_norm_g"], dmn, name="rms_mem_bwd")
    g_wout = _mm(y_mix, dh1_mx, TN, 1024, d, t, name="g_wout", tk=2048)[0]
    tok = comm.grads("mid", dict(w_out=g_wout, wq=g_wq, wkv=g_wkv, wo=g_wo))
    d_mix = _mm(dh1_mx, w["w_out"], NT, t, 1024, d, name="d_mix", tk=d, after=tok)[0]
    dproj_sw, g_swq, g_swk, g_sinks = _sw_bwd(proj_sw, rope, qg_t, kg_t, small["sw_sinks"], y_mix, d_mix, bsz, seq)
    tok = comm.poll(dproj_sw)
    dproj_hg, g_lb, g_hgn = _hg_bwd(proj_hg, small["hg_lower_bounds"], small["hg_norm_g"], o_hg, states, d_mix, bsz, seq,
                                    after=tok)
    g_in_hg = _mm(hn1, dproj_hg, TN, d, HG_COLS, t, name="g_in_hg", tk=t)[0]
    g_in_sw = _mm(hn1, dproj_sw, TN, d, SW_COLS, t, name="g_in_sw")[0]
    tok = comm.grads("in", dict(w_in_hg=g_in_hg, w_in_sw=g_in_sw))
    dhn1_a = _mm(dproj_hg, w_in_hg, NT, t, d, HG_COLS, name="d_hn1_hg", tk=HG_COLS, after=tok)[0]
    grad_x, g_norm1 = _mm(dproj_sw, w_in_sw, NT, t, d, SW_COLS, name="d_hn1_sw", tk=SW_COLS, extras=(dhn1_a, x, dh1),
                          rows=(small["norm1_g"],), row_sums=1,
                          epilogue=lambda acc, prev, xv, dres, g: _rms_bwd_residual(acc + prev, xv, dres, g)[1:])

    g_small = dict(norm1_g=g_norm1, hg_lower_bounds=g_lb, hg_norm_g=g_hgn, sw_q_norm_g=g_swq, sw_k_norm_g=g_swk,
                   sw_sinks=g_sinks[:, 0:SW_HEADS], norm2_g=g_norm2, mem_norm_g=g_memn, xa_q_norm_g=g_xq,
                   xa_k_norm_g=g_xk, norm3_g=g_norm3)
    return loss_row, grad_x.reshape(bsz, seq, d), g_small


SMALL_NAMES = ("norm1_g", "hg_lower_bounds", "hg_norm_g", "sw_q_norm_g", "sw_k_norm_g", "sw_sinks", "norm2_g",
               "mem_norm_g", "xa_q_norm_g", "xa_k_norm_g", "norm3_g")
BIG_NAMES = ("w_in", "w_out", "xa_wq", "xa_wkv", "xa_wo", "mlp_up", "mlp_down")
WEIGHT_ORDER = ("norm1_g", "w_in", "hg_lower_bounds", "hg_norm_g", "sw_q_norm_g", "sw_k_norm_g", "sw_sinks", "w_out",
                "norm2_g", "mem_norm_g", "xa_wq", "xa_wkv", "xa_q_norm_g", "xa_k_norm_g", "xa_wo", "norm3_g",
                "mlp_up", "mlp_down")


def _pack_rows(vals, width):
    starts, at = [], 0
    for v in vals:
        starts.append(at)
        at += v.shape[0]
    total = at + (-at) % 8
    out = None
    for v, s in zip(vals, starts):
        placed = jnp.pad(v, ((s, total - s - v.shape[0]), (0, width - v.shape[1])))
        out = placed if out is None else out + placed
    return out, starts


class _MeshWeights:
    LATE = ("w_out", "xa_wq", "xa_wkv", "xa_wo", "mlp_up", "mlp_down")

    def __init__(self, shards, d, ff):
        self.shards, self.d, self.ff = shards, d, ff
        self.c_idx = lax.axis_index("c").astype(jnp.int32).reshape(1)
        chip = (2 * lax.axis_index("x") + lax.axis_index("y")).astype(jnp.int32)
        self.place_idx = jnp.stack([chip, lax.axis_index("c").astype(jnp.int32)])
        self.pending = []
        self.exchanging = None

    def begin(self):
        shard = self.shards["w_in"]
        start, self.in_wait = _gather_chips_split(
            "gather_in", [shard], [_place_shard(shard, self.place_idx, name="place_w_in")])
        self.in_state = start()
        tok = (self.in_state["token"],)
        self.placed = [_place_shard(self.shards[n], self.place_idx, name="place_" + n, after=tok) for n in self.LATE]
        return tok

    def first(self, after):
        _, lands = self.in_wait(self.in_state, (after, *self.placed))
        (g_in,) = _gather_finish(lands, "gather_in_finish")
        start, self.late_wait = _gather_chips_split("gather_late", [self.shards[n] for n in self.LATE], self.placed)
        self.late_state = start(after=(g_in,))
        ws = g_in.shape[2]
        cut = HG_COLS - 2 * ws
        return dict(w_in_hg=jnp.concatenate([g_in[0], g_in[1], g_in[2][:, :cut]], axis=1),
                    w_in_sw=jnp.concatenate([g_in[2][:, cut:], g_in[3]], axis=1), token=self.late_state["token"])

    def rest(self, after):
        _, lands = self.late_wait(self.late_state, (after,))
        g_out, g_q, g_kv, g_o = _gather_finish(lands[:4], "gather_late_finish")
        start, self.mlp_wait = _gather_forward_split("gather_mlp_forward", lands[4:])
        self.mlp_state = start(after=(g_out,))
        d = self.d
        return dict(w_out=g_out.reshape(-1, d), wq=g_q.reshape(d, -1), wkv=g_kv.reshape(d, -1),
                    wo=jnp.concatenate([g_o[k] for k in range(4)], axis=1), token=self.mlp_state["token"])

    def mlp(self, after):
        _, (g_up, g_dn) = self.mlp_wait(self.mlp_state, (after,))
        return dict(up=g_up, down=g_dn.reshape(self.ff, self.d))

    def _scatter(self, tag, names, arrays, recv):
        parts = [_add_halves(g, r, self.c_idx, name="rs_add_halves_" + n) for n, g, r in zip(names, arrays, recv)]
        start, wait = _scatter_chips_split("rs_scatter_" + tag, parts)
        state = start()
        self.pending.append((names, wait, state))
        return state["token"]

    def _advance(self, after):
        if self.exchanging is None:
            return ()
        tag, names, wait, state = self.exchanging
        self.exchanging = None
        arrays, recv = wait(state, (after,))
        return (self._scatter(tag, names, arrays, recv),)

    def poll(self, after):
        return self._advance(after)

    def grads(self, tag, g):
        d, ff = self.d, self.ff
        if tag == "mlp":
            names, arrays = ("mlp_up", "mlp_down"), [g["up"], g["down"].reshape(4, ff // 4, d)]
        elif tag == "mid":
            names = ("w_out", "xa_wq", "xa_wkv", "xa_wo")
            ds = d // 4
            g_wo = jnp.stack([g["wo"][:, ds * k:ds * (k + 1)] for k in range(4)])
            arrays = [g["w_out"].reshape(4, -1, d), g["wq"].reshape(4, d // 4, -1), g["wkv"].reshape(4, d // 4, -1), g_wo]
        else:
            hg, sw = g["w_in_hg"], g["w_in_sw"]
            ws = (hg.shape[1] + sw.shape[1]) // 4
            cut = hg.shape[1] - 2 * ws
            names = ("w_in",)
            arrays = [jnp.stack([hg[:, :ws], hg[:, ws:2 * ws], jnp.concatenate([hg[:, 2 * ws:], sw[:, :ws - cut]], axis=1),
                                 sw[:, ws - cut:]])]
        toks = self._advance(arrays[0])
        if tag == "in":
            return toks + (self._scatter(tag, names, arrays, _exchange_halves(arrays, "rs_exchange_" + tag)),)
        start, wait = _exchange_halves_split("rs_exchange_" + tag, arrays)
        state = start()
        self.exchanging = (tag, names, wait, state)
        return toks + (state["token"],)

    def finish(self, after):
        joins, tok = [], ()
        for names, wait, state in self.pending:
            srcs, lands = wait(state, tuple(after) + tok)
            fulls = [_add_chips(p, r, self.place_idx, name="rs_add_chips_" + n, after=tok)
                     for n, p, r in zip(names, srcs, lands)]
            start, jwait = _join_halves_split("rs_join_" + names[0], fulls)
            jstate = start()
            tok = (jstate["token"],)
            joins.append((names, jwait, jstate))
        out = {}
        for names, jwait, jstate in joins:
            _, fulls = jwait(jstate, tok)
            out.update(zip(names, fulls))
        return out


def kernel(x, mem, positions, norm1_g, w_in, hg_lower_bounds, hg_norm_g, sw_q_norm_g, sw_k_norm_g, sw_sinks, w_out, norm2_g, mem_norm_g, xa_wq, xa_wkv, xa_q_norm_g, xa_k_norm_g, xa_wo, norm3_g, mlp_up, mlp_down, loss_target, m_norm1_g, m_w_in, m_hg_lower_bounds, m_hg_norm_g, m_sw_q_norm_g, m_sw_k_norm_g, m_sw_sinks, m_w_out, m_norm2_g, m_mem_norm_g, m_xa_wq, m_xa_wkv, m_xa_q_norm_g, m_xa_k_norm_g, m_xa_wo, m_norm3_g, m_mlp_up, m_mlp_down, v_norm1_g, v_w_in, v_hg_lower_bounds, v_hg_norm_g, v_sw_q_norm_g, v_sw_k_norm_g, v_sw_sinks, v_w_out, v_norm2_g, v_mem_norm_g, v_xa_wq, v_xa_wkv, v_xa_q_norm_g, v_xa_k_norm_g, v_xa_wo, v_norm3_g, v_mlp_up, v_mlp_down):
    given = dict(locals())
    weights = {n: given[n] for n in WEIGHT_ORDER}
    moms = {n: given["m_" + n] for n in WEIGHT_ORDER}
    vars_ = {n: given["v_" + n] for n in WEIGHT_ORDER}
    d = x.shape[-1]
    ff = mlp_down.shape[1] * 4
    small = {n: weights[n] for n in SMALL_NAMES}

    comm = _MeshWeights({n: weights[n][0].astype(_MXU_DTYPE) for n in BIG_NAMES}, d, ff)
    loss_row, grad_x, g_small = _local_step(x, mem, positions, loss_target, small, comm)
    packed, starts = _pack_rows([g_small[n] for n in SMALL_NAMES] + [loss_row], 1024)
    start, wait = _all_gather_small_split(packed)
    state = start()
    big_grads = comm.finish((grad_x, state["token"]))
    (own,), (gathered,) = wait(state, (big_grads[BIG_NAMES[0]],))
    device = (4 * lax.axis_index("x") + 2 * lax.axis_index("y") + lax.axis_index("c")).astype(jnp.int32).reshape(1)
    summed = _sum_devices(own, gathered, device)
    small_grads = {}
    for n, s in zip(SMALL_NAMES, starts):
        r, c = weights[n].shape
        small_grads[n] = summed[s:s + r, 0:c]
    loss = summed[starts[-1], 0]

    grads, deltas, new_m, new_v = {}, {}, {}, {}
    for n in BIG_NAMES:
        shp = weights[n].shape
        g2, dl, mo, vo = _adamw_big(weights[n][0], big_grads[n], moms[n][0], vars_[n][0], name="adamw_" + n)
        grads[n], deltas[n], new_m[n], new_v[n] = (a.reshape(shp) for a in (g2, dl, mo, vo))
    sm_out = _adamw_small([weights[n] for n in SMALL_NAMES], [small_grads[n] for n in SMALL_NAMES],
                          [moms[n] for n in SMALL_NAMES], [vars_[n] for n in SMALL_NAMES])
    ns = len(SMALL_NAMES)
    for i, n in enumerate(SMALL_NAMES):
        grads[n], deltas[n], new_m[n], new_v[n] = small_grads[n], sm_out[i], sm_out[ns + i], sm_out[2 * ns + i]

    return (loss, grad_x, *[grads[n] for n in WEIGHT_ORDER], *[deltas[n] for n in WEIGHT_ORDER],
            *[new_m[n] for n in WEIGHT_ORDER], *[new_v[n] for n in WEIGHT_ORDER])
```

```python
import numpy as np
import jax
import jax.numpy as jnp
from jax import lax
from jax.experimental import pallas as pl
from jax.experimental.pallas import tpu as pltpu

F32 = jnp.float32
_MXU_DTYPE = jnp.bfloat16

EPS = 1e-6
HG_HEADS = 4
HG_D = 128
HG_CHUNK = 64
HG_TILE = 512
HG_LEVELS = (32, 16, 8, 4, 2, 1)
SW_HEADS = 8
SW_KV_HEADS = 2
SW_GROUP = SW_HEADS // SW_KV_HEADS
SW_HD = 64
SW_BLOCK = 128
ROPE_THETA = 500000.0
ROT_DIM = SW_HD // 4
XA_HEADS = 4
XA_HD = 128
HG_COLS = 4 * HG_HEADS * HG_D
SW_COLS = (SW_HEADS + 2 * SW_KV_HEADS) * SW_HD

ADAM_LR = 0.001
ADAM_B1 = 0.9
ADAM_B2 = 0.999
ADAM_EPS = 1e-08
ADAM_WD = 0.01
ADAM_STEP = 10

VMEM_LIMIT = 56 * 1024 * 1024
MESH = pl.DeviceIdType.MESH

NN = ((1,), (0,))
NT = ((1,), (1,))
TN = ((0,), (0,))


def _mx(v):
    return v.astype(_MXU_DTYPE)


def _dot(a, b, dims=NN):
    return lax.dot_general(_mx(a), _mx(b), (dims, ((), ())), preferred_element_type=F32)


def _split_dot(a, v, dims, parts):
    acc = None
    rest = v
    for p in range(parts):
        piece = _mx(rest)
        term = lax.dot_general(a, piece, (dims, ((), ())), preferred_element_type=F32)
        acc = term if acc is None else acc + term
        if p + 1 < parts:
            rest = rest - piece.astype(F32)
    return acc


def _params(sem):
    return pltpu.CompilerParams(dimension_semantics=sem, vmem_limit_bytes=VMEM_LIMIT)


def _mm(a, b, mode, m, n, k, *, name, tm=1024, tn=1024, tk=1024, a_spec=None, b_spec=None, extras=(), rows=(),
        epilogue=None, out_dtypes=(F32,), row_sums=0, out_shape=None, out_spec=None, after=(), into=None):
    after = tuple(t for t in after if t is not None) + (() if into is None else (into,))
    tm, tn, tk = min(tm, m), min(tn, n), min(tk, k)
    assert m % tm == 0 and n % tn == 0 and k % tk == 0, (name, m, n, k, tm, tn, tk)
    gi, gj, gk = m // tm, n // tn, k // tk
    assert row_sums == 0 or gj == 1, name
    if a_spec is None:
        a_spec = (pl.BlockSpec((tk, tm), lambda i, j, kk: (kk, i)) if mode == TN
                  else pl.BlockSpec((tm, tk), lambda i, j, kk: (i, kk)))
    if b_spec is None:
        b_spec = (pl.BlockSpec((tn, tk), lambda i, j, kk: (j, kk)) if mode == NT
                  else pl.BlockSpec((tk, tn), lambda i, j, kk: (kk, j)))
    mn_spec = pl.BlockSpec((tm, tn), lambda i, j, kk: (i, j))
    if epilogue is None:
        epilogue = lambda acc: (acc,)
    row_spec = pl.BlockSpec((1, tn), lambda i, j, kk: (0, j))
    n_ex, n_out = len(extras) + len(rows), len(out_dtypes)
    if out_shape is None:
        out_shape = tuple(jax.ShapeDtypeStruct((m, n), d) for d in out_dtypes)
        out_spec = tuple(mn_spec for _ in out_dtypes)
    out_shape = tuple(out_shape) + tuple(jax.ShapeDtypeStruct((1, n), F32) for _ in range(row_sums))
    out_spec = tuple(out_spec) + tuple(row_spec for _ in range(row_sums))

    n_after = len(after)

    def body(*refs):
        a_ref, b_ref = refs[0], refs[1]
        ex = refs[2:2 + n_ex]
        outs = refs[2 + n_ex + n_after:2 + n_ex + n_after + n_out + row_sums]
        first_row_tile = pl.program_id(0) == 0

        def finish(acc):
            res = epilogue(acc, *[e[...] for e in ex])
            for o, r in zip(outs[:n_out], res[:n_out]):
                o[...] = r.astype(o.dtype)
            if row_sums:
                @pl.when(first_row_tile)
                def _():
                    for o in outs[n_out:]:
                        o[...] = jnp.zeros_like(o)

                for o, r in zip(outs[n_out:], res[n_out:]):
                    o[...] += r

        if gk == 1:
            finish(_dot(a_ref[...], b_ref[...], mode))
        else:
            acc_ref = refs[-1]
            kk = pl.program_id(2)

            @pl.when(kk == 0)
            def _():
                acc_ref[...] = jnp.zeros_like(acc_ref)

            acc_ref[...] += _dot(a_ref[...], b_ref[...], mode)

            @pl.when(kk == gk - 1)
            def _():
                finish(acc_ref[...])

    return pl.pallas_call(
        body, name=name, grid=(gi, gj, gk),
        in_specs=([a_spec, b_spec] + [mn_spec] * len(extras) + [row_spec] * len(rows)
                  + [pl.BlockSpec(memory_space=pl.ANY)] * n_after),
        out_specs=out_spec, out_shape=out_shape,
        input_output_aliases={} if into is None else {2 + n_ex + n_after - 1: 0},
        scratch_shapes=[pltpu.VMEM((tm, tn), F32)] if gk > 1 else [],
        compiler_params=_params(("arbitrary" if row_sums else "parallel", "parallel", "arbitrary")),
    )(a, b, *extras, *rows, *after)


def _rms_rows(xv, g):
    return xv * lax.rsqrt(jnp.mean(xv * xv, axis=1, keepdims=True) + EPS) * g


def _rms_rows_bwd(xv, g, dyv):
    r = lax.rsqrt(jnp.mean(xv * xv, axis=1, keepdims=True) + EPS)
    u = dyv * g
    return (r * u - xv * (r * r * r) * jnp.mean(u * xv, axis=1, keepdims=True),
            jnp.sum(dyv * xv * r, axis=0, keepdims=True))


def _residual_rms(acc, res, g):
    h = acc + res
    return h, _rms_rows(h, g)


def _rms_bwd_residual(dhn, xv, dres, g):
    dx, dg = _rms_rows_bwd(xv, g, dhn)
    dx = dx + dres
    return dx, dx, dg


def _rms_fwd(x, g, *, name, tm=512, after=()):
    t, d = x.shape
    tm = min(tm, t)
    after = tuple(a for a in after if a is not None)

    def body(x_ref, g_ref, *rest):
        rest[-1][...] = _rms_rows(x_ref[...], g_ref[...]).astype(rest[-1].dtype)

    return pl.pallas_call(
        body, name=name, grid=(t // tm,),
        in_specs=[pl.BlockSpec((tm, d), lambda i: (i, 0)), pl.BlockSpec((1, d), lambda i: (0, 0))]
        + [pl.BlockSpec(memory_space=pl.ANY)] * len(after),
        out_specs=pl.BlockSpec((tm, d), lambda i: (i, 0)),
        out_shape=jax.ShapeDtypeStruct((t, d), _MXU_DTYPE),
        compiler_params=_params(("parallel",)),
    )(x, g, *after)


def _rms_gain_grad(x, g, dy, *, name, tm=512):
    t, d = x.shape
    tm = min(tm, t)

    def body(x_ref, g_ref, dy_ref, dg_ref):
        @pl.when(pl.program_id(0) == 0)
        def _():
            dg_ref[...] = jnp.zeros_like(dg_ref)

        dg_ref[...] += _rms_rows_bwd(x_ref[...], g_ref[...], dy_ref[...])[1]

    row = pl.BlockSpec((tm, d), lambda i: (i, 0))
    vec = pl.BlockSpec((1, d), lambda i: (0, 0))
    return pl.pallas_call(
        body, name=name, grid=(t // tm,), in_specs=[row, vec, row], out_specs=vec,
        out_shape=jax.ShapeDtypeStruct((1, d), F32), compiler_params=_params(("arbitrary",)),
    )(x, g, dy)


def _hg_constants():
    c = HG_CHUNK
    t = np.arange(c)
    sums = [t[None, :] <= t[:, None]]
    masks = []
    for m in HG_LEVELS:
        base = (t // (2 * m)) * (2 * m)
        mid = base + m - 1
        second = (t - base) >= m
        upper = (t[None, :] > mid[:, None]) & (t[None, :] <= t[:, None])
        lower = (t[None, :] > t[:, None]) & (t[None, :] <= mid[:, None])
        sums.append(np.where(second[:, None], upper, lower))
        masks.append(second[:, None] & (~second)[None, :] & (base[:, None] == base[None, :]))
    return (np.concatenate(sums, axis=0).astype(np.float32), np.stack(masks).astype(np.float32))


HG_HEAD_LANES = tuple(slice(HG_D * h, HG_D * (h + 1)) for h in range(HG_HEADS))


def _per_head(fn, slab):
    return jnp.concatenate([jnp.broadcast_to(fn(slab[:, hs]), (slab.shape[0], HG_D)) for hs in HG_HEAD_LANES], axis=1)


def _lane_sum(v):
    return jnp.sum(v, axis=1, keepdims=True)


def _lane_mean(v):
    return jnp.mean(v, axis=1, keepdims=True)


def _hg_gates(blk, lbp):
    w = HG_HEADS * HG_D
    q, x, v, gl = blk[:, 0:w], blk[:, w:2 * w], blk[:, 2 * w:3 * w], blk[:, 3 * w:4 * w]
    mx = jnp.max(lbp, axis=0, keepdims=True)
    e = jnp.exp(lbp - mx)
    lb = e[0:1, :] / jnp.sum(e, axis=0, keepdims=True)
    sig = jax.nn.sigmoid(x)
    f = lb + (1.0 - lb) * sig
    return q, v, gl, lb, sig, f, 1.0 - f, jnp.log(f)


def _hg_fwd(proj, lbp, ng, bsz, seq, *, y_width):
    t = proj.shape[0]
    nc = seq // HG_CHUNK
    a_np, m_np = _hg_constants()
    a_all = jnp.asarray(a_np, _MXU_DTYPE)
    masks = jnp.asarray(m_np, F32)
    nl = len(HG_LEVELS)

    ts = min(HG_TILE, seq)
    ns, nct = seq // ts, ts // HG_CHUNK
    hw = HG_HEADS * HG_D

    def body(p_ref, lb_ref, ng_ref, a_ref, m_ref, y_ref, o_ref, st_ref, carry):
        a_mat = a_ref[...]
        ngv = ng_ref[...]

        @pl.when(pl.program_id(0) == 0)
        def _():
            carry[...] = jnp.zeros_like(carry)

        ng4 = _tile_lanes(ngv, HG_HEADS)
        heads = range(HG_HEADS)
        exs = range(bsz)
        hl = HG_HEAD_LANES
        lbp_v = lb_ref[...]

        def chunk(c, _):
            rows = pl.ds(pl.multiple_of(c * HG_CHUNK, HG_CHUNK), HG_CHUNK)
            gates = [_hg_gates(p_ref[e, rows, :], lbp_v) for e in exs]
            q, v, gl = [g[0] for g in gates], [g[1] for g in gates], [g[2] for g in gates]
            k = [g[6] for g in gates]
            sts = [[carry[e, h] for h in heads] for e in exs]
            e_all = [_split_dot(a_mat, gates[e][7], NN, 3) for e in exs]
            b = [e_all[e][0:HG_CHUNK] for e in exs]
            qb = [q[e] * jnp.exp(b[e]) for e in exs]
            o = [[_dot(qb[e][:, hl[h]], sts[e][h], NT) for h in heads] for e in exs]
            p = [[jnp.zeros((HG_CHUNK, HG_CHUNK), F32) for _ in heads] for _ in exs]
            for li in range(nl):
                dec = [jnp.exp(e_all[e][HG_CHUNK * (li + 1):HG_CHUNK * (li + 2)]) for e in exs]
                qm, km, mk = [q[e] * dec[e] for e in exs], [k[e] * dec[e] for e in exs], m_ref[li]
                p = [[p[e][h] + mk * _dot(qm[e][:, hl[h]], km[e][:, hl[h]], NT) for h in heads] for e in exs]
            bl = [b[e][HG_CHUNK - 1:HG_CHUNK, :] for e in exs]
            kd = [k[e] * jnp.exp(bl[e] - b[e]) for e in exs]
            pv = [[_dot(p[e][h], v[e][:, hl[h]]) for h in heads] for e in exs]
            upd = [[_dot(v[e][:, hl[h]], kd[e][:, hl[h]], TN) for h in heads] for e in exs]
            for e in exs:
                o_all = (jnp.concatenate([o[e][h] + pv[e][h] for h in heads], axis=1)
                         + _per_head(_lane_sum, q[e] * k[e]) * v[e])
                r = lax.rsqrt(_per_head(_lane_mean, o_all * o_all) + EPS)
                ebl = jnp.exp(bl[e])
                for h in heads:
                    st_ref[e, h, c] = sts[e][h]
                    carry[e, h] = sts[e][h] * ebl[:, hl[h]] + upd[e][h]
                o_ref[e, rows, :] = o_all
                y_ref[e, rows, :] = (o_all * r * ng4) * (gl[e] * jax.nn.sigmoid(gl[e]))
            return 0

        lax.fori_loop(0, nct, chunk, 0)

    y3, o3, states = pl.pallas_call(
        body, name="hgrn2_fwd", grid=(ns,),
        in_specs=[pl.BlockSpec((bsz, ts, HG_COLS), lambda s: (0, s, 0)),
                  pl.BlockSpec((2, hw), lambda s: (0, 0)),
                  pl.BlockSpec((1, HG_D), lambda s: (0, 0)),
                  pl.BlockSpec(a_all.shape, lambda s: (0, 0)),
                  pl.BlockSpec(masks.shape, lambda s: (0, 0, 0))],
        out_specs=(pl.BlockSpec((bsz, ts, hw), lambda s: (0, s, 0)),
                   pl.BlockSpec((bsz, ts, hw), lambda s: (0, s, 0)),
                   pl.BlockSpec((bsz, HG_HEADS, nct, HG_D, HG_D), lambda s: (0, 0, s, 0, 0))),
        out_shape=(jax.ShapeDtypeStruct((bsz, seq, y_width), F32),
                   jax.ShapeDtypeStruct((bsz, seq, hw), F32),
                   jax.ShapeDtypeStruct((bsz, HG_HEADS, nc, HG_D, HG_D), F32)),
        scratch_shapes=[pltpu.VMEM((bsz, HG_HEADS, HG_D, HG_D), F32)],
        compiler_params=_params(("arbitrary",)),
    )(proj.reshape(bsz, seq, HG_COLS), lbp, ng, a_all, masks)
    return y3.reshape(t, y_width), o3.reshape(t, hw), states


def _hg_bwd(proj, lbp, ng, o_all, states, dy, bsz, seq, after=()):
    after = tuple(a for a in after if a is not None)
    t = proj.shape[0]
    nc = seq // HG_CHUNK
    a_np, m_np = _hg_constants()
    a_all = jnp.asarray(a_np, _MXU_DTYPE)
    masks = jnp.asarray(m_np, F32)
    nl = len(HG_LEVELS)
    cs = HG_CHUNK

    ts = min(HG_TILE, seq)
    ns, nct = seq // ts, ts // cs
    hw = HG_HEADS * HG_D

    def body(p_ref, lb_ref, ng_ref, a_ref, m_ref, o_ref, st_ref, dy_ref, *rest):
        dp_ref, dlb_ref, dng_ref, dst_ref = rest[len(after):]
        a_mat = a_ref[...]
        ngv = ng_ref[...]
        ng4 = _tile_lanes(ngv, HG_HEADS)
        last_row = lax.broadcasted_iota(jnp.int32, (cs, hw), 0) == cs - 1
        first = pl.program_id(0) == 0
        heads = range(HG_HEADS)
        exs = range(bsz)
        hl = HG_HEAD_LANES
        lbp_v = lb_ref[...]

        @pl.when(first)
        def _():
            dst_ref[...] = jnp.zeros_like(dst_ref)

        def side_by_side(parts):
            return jnp.concatenate(parts, axis=1)

        def chunk(i, carry):
            dlb_acc, dng_acc = carry
            c = nct - 1 - i
            rows = pl.ds(pl.multiple_of(c * cs, cs), cs)
            gates = [_hg_gates(p_ref[e, rows, :], lbp_v) for e in exs]
            q, v, gl = [g[0] for g in gates], [g[1] for g in gates], [g[2] for g in gates]
            lb, sig, f, k = gates[0][3], [g[4] for g in gates], [g[5] for g in gates], [g[6] for g in gates]
            o = [o_ref[e, rows, :] for e in exs]
            dyv = [dy_ref[e, rows, :] for e in exs]
            sts = [[st_ref[e, h, c] for h in heads] for e in exs]
            dsts = [[dst_ref[e, h] for h in heads] for e in exs]
            e_all = [_split_dot(a_mat, gates[e][7], NN, 3) for e in exs]
            b = [e_all[e][0:cs] for e in exs]
            eb = [jnp.exp(b[e]) for e in exs]
            bl = [b[e][cs - 1:cs, :] for e in exs]
            ebl = [jnp.exp(bl[e]) for e in exs]
            ekd = [jnp.exp(bl[e] - b[e]) for e in exs]
            qb = [q[e] * eb[e] for e in exs]
            kd = [k[e] * ekd[e] for e in exs]
            do, dgl = [], []
            for e in exs:
                sg = jax.nn.sigmoid(gl[e])
                silu = gl[e] * sg
                r = lax.rsqrt(_per_head(_lane_mean, o[e] * o[e]) + EPS)
                dgl.append(dyv[e] * (o[e] * r * ng4) * (sg * (1.0 + gl[e] * (1.0 - sg))))
                u = dyv[e] * silu * ng4
                do.append(r * u - o[e] * (r * r * r) * _per_head(_lane_mean, u * o[e]))
                dng4 = jnp.sum(dyv[e] * silu * o[e] * r, axis=0, keepdims=True)
                dng_acc = dng_acc + ((dng4[:, hl[0]] + dng4[:, hl[1]]) + (dng4[:, hl[2]] + dng4[:, hl[3]]))
            es, qm, km = [], [], []
            p = [[jnp.zeros((cs, cs), F32) for _ in heads] for _ in exs]
            for li in range(nl):
                dec = [jnp.exp(e_all[e][cs * (li + 1):cs * (li + 2)]) for e in exs]
                es.append(dec)
                qm.append([q[e] * dec[e] for e in exs])
                km.append([k[e] * dec[e] for e in exs])
                mk = m_ref[li]
                p = [[p[e][h] + mk * _dot(qm[li][e][:, hl[h]], km[li][e][:, hl[h]], NT) for h in heads] for e in exs]
            dp = [[_dot(do[e][:, hl[h]], v[e][:, hl[h]], NT) for h in heads] for e in exs]
            dv_p = [[_dot(p[e][h], do[e][:, hl[h]], TN) for h in heads] for e in exs]
            dv_s = [[_dot(kd[e][:, hl[h]], dsts[e][h], NT) for h in heads] for e in exs]
            dqb = [side_by_side([_dot(do[e][:, hl[h]], sts[e][h]) for h in heads]) for e in exs]
            dkd = [side_by_side([_dot(v[e][:, hl[h]], dsts[e][h]) for h in heads]) for e in exs]
            new_dst = [[_dot(do[e][:, hl[h]], qb[e][:, hl[h]], TN) for h in heads] for e in exs]
            dv = [side_by_side([dv_p[e][h] + dv_s[e][h] for h in heads]) + _per_head(_lane_sum, q[e] * k[e]) * do[e]
                  for e in exs]
            dq = [dqb[e] * eb[e] for e in exs]
            dk = [dkd[e] * ekd[e] for e in exs]
            de = []
            for e in exs:
                dbl = (jnp.sum(dkd[e] * kd[e], axis=0, keepdims=True)
                       + side_by_side([jnp.sum(dsts[e][h] * sts[e][h], axis=0, keepdims=True) for h in heads]) * ebl[e])
                de.append([dqb[e] * qb[e] - dkd[e] * kd[e] + jnp.where(last_row, dbl, 0.0)])
            for li in range(nl):
                mk = m_ref[li]
                dpm = [[mk * dp[e][h] for h in heads] for e in exs]
                dqm = [side_by_side([_dot(dpm[e][h], km[li][e][:, hl[h]]) for h in heads]) for e in exs]
                dkm = [side_by_side([_dot(dpm[e][h], qm[li][e][:, hl[h]], TN) for h in heads]) for e in exs]
                for e in exs:
                    dq[e] = dq[e] + dqm[e] * es[li][e]
                    dk[e] = dk[e] + dkm[e] * es[li][e]
                    de[e].append(dqm[e] * qm[li][e] + dkm[e] * km[li][e])
            dg = [_split_dot(a_mat, jnp.concatenate(de[e], axis=0), TN, 2) for e in exs]
            for e in exs:
                dpd = _per_head(_lane_sum, do[e] * v[e])
                df = dg[e] / f[e] - (dk[e] + dpd * q[e])
                dp_ref[e, rows, 0:hw] = _mx(dq[e] + dpd * k[e])
                dp_ref[e, rows, hw:2 * hw] = _mx(df * (1.0 - lb) * sig[e] * (1.0 - sig[e]))
                dp_ref[e, rows, 2 * hw:3 * hw] = _mx(dv[e])
                dp_ref[e, rows, 3 * hw:4 * hw] = _mx(dgl[e])
                for h in heads:
                    dst_ref[e, h] = dsts[e][h] * ebl[e][:, hl[h]] + new_dst[e][h]
                dlb_acc = dlb_acc + jnp.sum(df * (1.0 - sig[e]), axis=0, keepdims=True)
            return dlb_acc, dng_acc

        dlb, dng = lax.fori_loop(0, nct, chunk, (jnp.zeros((1, hw), F32), jnp.zeros((1, HG_D), F32)))

        @pl.when(first)
        def _():
            dlb_ref[...] = jnp.zeros_like(dlb_ref)
            dng_ref[...] = jnp.zeros_like(dng_ref)

        mx = jnp.max(lbp_v, axis=0, keepdims=True)
        e = jnp.exp(lbp_v - mx)
        s0 = e[0:1, :] / jnp.sum(e, axis=0, keepdims=True)
        da0 = dlb * s0 * (1.0 - s0)
        dlb_ref[...] += jnp.concatenate([da0, -da0], axis=0)
        dng_ref[...] += dng

    rows3 = lambda w: pl.BlockSpec((bsz, ts, w), lambda s: (0, ns - 1 - s, 0))
    dproj, dlb, dng = pl.pallas_call(
        body, name="hgrn2_bwd", grid=(ns,),
        in_specs=[rows3(HG_COLS),
                  pl.BlockSpec((2, hw), lambda s: (0, 0)),
                  pl.BlockSpec((1, HG_D), lambda s: (0, 0)),
                  pl.BlockSpec(a_all.shape, lambda s: (0, 0)),
                  pl.BlockSpec(masks.shape, lambda s: (0, 0, 0)),
                  rows3(hw),
                  pl.BlockSpec((bsz, HG_HEADS, nct, HG_D, HG_D), lambda s: (0, 0, ns - 1 - s, 0, 0)),
                  rows3(hw)] + [pl.BlockSpec(memory_space=pl.ANY)] * len(after),
        out_specs=(rows3(HG_COLS),
                   pl.BlockSpec((2, hw), lambda s: (0, 0)),
                   pl.BlockSpec((1, HG_D), lambda s: (0, 0))),
        out_shape=(jax.ShapeDtypeStruct((bsz, seq, HG_COLS), _MXU_DTYPE),
                   jax.ShapeDtypeStruct((2, hw), F32),
                   jax.ShapeDtypeStruct((1, HG_D), F32)),
        scratch_shapes=[pltpu.VMEM((bsz, HG_HEADS, HG_D, HG_D), F32)],
        compiler_params=_params(("arbitrary",)),
    )(proj.reshape(bsz, seq, HG_COLS), lbp, ng, a_all, masks, o_all.reshape(bsz, seq, hw), states,
      dy.reshape(bsz, seq, dy.shape[1]), *after)
    return dproj.reshape(t, HG_COLS), dlb, dng


def _sw_constants():
    half = ROT_DIM // 2
    inv = (np.float32(ROPE_THETA) ** (-(np.arange(half, dtype=np.float32) * np.float32(2.0) / np.float32(ROT_DIM)))
           ).astype(np.float32)
    freq = np.zeros((1, 128), np.float32)
    sign = np.zeros((1, 128), np.float32)
    for h in range(2):
        freq[0, 64 * h:64 * h + half] = inv
        freq[0, 64 * h + half:64 * h + 2 * half] = inv
        sign[0, 64 * h:64 * h + half] = -1.0
        sign[0, 64 * h + half:64 * h + 2 * half] = 1.0
    seg = np.kron(np.eye(8, dtype=np.float32), np.full((64, 64), 1.0 / 64.0, np.float32))
    return freq, sign, seg


def _rope_table(pos, *, tm=512):
    t = pos.shape[0]
    tm = min(tm, t)
    freq_np, sign_np, _ = _sw_constants()

    def body(p_ref, f_ref, s_ref, o_ref):
        ang = p_ref[...].astype(F32) * f_ref[...]
        o_ref[:, 0:128] = jnp.cos(ang)
        o_ref[:, 128:256] = jnp.sin(ang) * s_ref[...]

    vec = pl.BlockSpec((1, 128), lambda i: (0, 0))
    return pl.pallas_call(
        body, name="rope_table", grid=(t // tm,),
        in_specs=[pl.BlockSpec((tm, 1), lambda i: (i, 0)), vec, vec],
        out_specs=pl.BlockSpec((tm, 256), lambda i: (i, 0)),
        out_shape=jax.ShapeDtypeStruct((t, 256), F32),
        compiler_params=_params(("parallel",)),
    )(pos, jnp.asarray(freq_np), jnp.asarray(sign_np))


def _tile_lanes(v, times):
    return v if times == 1 else jnp.concatenate([v] * times, axis=1)


def _swap_halves(v):
    w = v.shape[1]
    half = ROT_DIM // 2
    lane = lax.broadcasted_iota(jnp.int32, v.shape, 1) % SW_HD
    return jnp.where(lane < half, pltpu.roll(v, w - half, 1), jnp.where(lane < 2 * half, pltpu.roll(v, half, 1), 0.0))


def _sw_norm_rope(tv, gain, seg, cosv, sinv):
    w = tv.shape[1]
    ms = _split_dot_rhs(tv * tv, seg[0:w, 0:w])
    r = lax.rsqrt(ms + EPS)
    tn = tv * r * gain
    reps = w // 128
    return tn * _tile_lanes(cosv, reps) + _swap_halves(tn) * _tile_lanes(sinv, reps), r


def _split_dot_rhs(v, a):
    hi = _mx(v)
    lo = _mx(v - hi.astype(F32))
    return (lax.dot_general(hi, a, (NN, ((), ())), preferred_element_type=F32)
            + lax.dot_general(lo, a, (NN, ((), ())), preferred_element_type=F32))


def _sw_norm_rope_bwd(dt, tv, r, gain, seg, cosv, sinv):
    w = tv.shape[1]
    reps = w // 128
    dtn = dt * _tile_lanes(cosv, reps) + _swap_halves(dt * _tile_lanes(sinv, reps))
    u = dtn * gain
    dtv = r * u - tv * (r * r * r) * _split_dot_rhs(u * tv, seg[0:w, 0:w])
    return dtv, jnp.sum(dtn * tv * r, axis=0, keepdims=True)


def _sw_scores(qh, kp, kc):
    return _dot(qh, kp, NT), _dot(qh, kc, NT)


SW_SCALE = SW_HD ** -0.5


def _sw_probs(raw, sink, first_block):
    qi = lax.broadcasted_iota(jnp.int32, (SW_BLOCK, SW_BLOCK), 0)
    kj = lax.broadcasted_iota(jnp.int32, (SW_BLOCK, SW_BLOCK), 1)
    ok_prev = jnp.logical_and(kj > qi, jnp.logical_not(first_block))
    ok_cur = kj <= qi
    sp = jnp.where(ok_prev, raw[0], -jnp.inf)
    sc = jnp.where(ok_cur, raw[1], -jnp.inf)
    m = jnp.maximum(jnp.maximum(jnp.max(sp, axis=1, keepdims=True), jnp.max(sc, axis=1, keepdims=True)), sink)
    pp, pc = jnp.exp(sp - m), jnp.exp(sc - m)
    es = jnp.exp(sink - m)
    inv = 1.0 / (jnp.sum(pp, axis=1, keepdims=True) + jnp.sum(pc, axis=1, keepdims=True) + es)
    return pp * inv, pc * inv, es * inv


def _sw_specs(nb):
    def cur(b, n):
        return b * nb + jnp.minimum(n, nb - 1)

    def prev(b, n):
        return b * nb + jnp.maximum(jnp.minimum(n, nb - 1) - 1, 0)

    return cur, prev


def _sw_fwd(proj, rope, qg, kg, sinks, y_in, bsz, seq):
    t = proj.shape[0]
    nb = seq // SW_BLOCK
    seg = jnp.asarray(_sw_constants()[2], _MXU_DTYPE)
    cur, prev = _sw_specs(nb)

    def body(q_ref, kc_ref, kp_ref, vc_ref, vp_ref, rc_ref, rp_ref, qg_ref, kg_ref, sk_ref, seg_ref, yin_ref, y_ref):
        del yin_ref
        n = pl.program_id(1)
        segv = seg_ref[...]
        cos_c, sin_c = rc_ref[:, 0:128], rc_ref[:, 128:256]
        cos_p, sin_p = rp_ref[:, 0:128], rp_ref[:, 128:256]
        qr, _ = _sw_norm_rope(q_ref[...], qg_ref[...] * SW_SCALE, segv, cos_c, sin_c)
        kcr, _ = _sw_norm_rope(kc_ref[...], kg_ref[...], segv, cos_c, sin_c)
        kpr, _ = _sw_norm_rope(kp_ref[...], kg_ref[...], segv, cos_p, sin_p)
        vc, vp = vc_ref[...], vp_ref[...]
        ks = [slice(SW_HD * (h // SW_GROUP), SW_HD * (h // SW_GROUP + 1)) for h in range(SW_HEADS)]
        raw = [_sw_scores(qr[:, SW_HD * h:SW_HD * (h + 1)], kpr[:, ks[h]], kcr[:, ks[h]]) for h in range(SW_HEADS)]
        probs = [_sw_probs(raw[h], sk_ref[0, h], n == 0) for h in range(SW_HEADS)]
        for h in range(SW_HEADS):
            y_ref[:, SW_HD * h:SW_HD * (h + 1)] = _dot(probs[h][0], vp[:, ks[h]]) + _dot(probs[h][1], vc[:, ks[h]])

    rowq = pl.BlockSpec((SW_BLOCK, 512), lambda b, n: (cur(b, n), 0))
    full = lambda a: pl.BlockSpec(a.shape, lambda b, n: (0,) * a.ndim)
    yw = y_in.shape[1]
    return pl.pallas_call(
        body, name="swa_fwd", grid=(bsz, nb),
        in_specs=[rowq,
                  pl.BlockSpec((SW_BLOCK, 128), lambda b, n: (cur(b, n), 4)),
                  pl.BlockSpec((SW_BLOCK, 128), lambda b, n: (prev(b, n), 4)),
                  pl.BlockSpec((SW_BLOCK, 128), lambda b, n: (cur(b, n), 5)),
                  pl.BlockSpec((SW_BLOCK, 128), lambda b, n: (prev(b, n), 5)),
                  pl.BlockSpec((SW_BLOCK, 256), lambda b, n: (cur(b, n), 0)),
                  pl.BlockSpec((SW_BLOCK, 256), lambda b, n: (prev(b, n), 0)),
                  full(qg), full(kg),
                  pl.BlockSpec(memory_space=pltpu.SMEM),
                  full(seg),
                  pl.BlockSpec(memory_space=pl.ANY)],
        out_specs=pl.BlockSpec((SW_BLOCK, 512), lambda b, n: (cur(b, n), 1)),
        out_shape=jax.ShapeDtypeStruct((t, yw), F32),
        input_output_aliases={11: 0},
        compiler_params=_params(("parallel", "parallel")),
    )(proj, proj, proj, proj, proj, rope, rope, qg, kg, sinks, seg, y_in)


def _sw_bwd(proj, rope, qg, kg, sinks, y, dy, bsz, seq):
    t = proj.shape[0]
    nb = seq // SW_BLOCK
    seg = jnp.asarray(_sw_constants()[2], _MXU_DTYPE)
    cur, prev = _sw_specs(nb)

    def body(q_ref, kc_ref, kp_ref, vc_ref, vp_ref, rc_ref, rp_ref, qg_ref, kg_ref, sk_ref, seg_ref,
             y_ref, dy_ref, dp_ref, dqg_ref, dkg_ref, dsk_ref,
             dq_car, dkv_car, dqr_s, dkc_s, dkp_s, dvc_s, dvp_s, gq_acc, gk_acc, sk_acc):
        b, n = pl.program_id(0), pl.program_id(1)
        first = jnp.logical_and(b == 0, n == 0)
        last = jnp.logical_and(b == pl.num_programs(0) - 1, n == nb)

        @pl.when(first)
        def _():
            gq_acc[...] = jnp.zeros_like(gq_acc)
            gk_acc[...] = jnp.zeros_like(gk_acc)
            sk_acc[...] = jnp.zeros_like(sk_acc)

        @pl.when(n < nb)
        def _():
            segv = seg_ref[...]
            cos_c, sin_c = rc_ref[:, 0:128], rc_ref[:, 128:256]
            cos_p, sin_p = rp_ref[:, 0:128], rp_ref[:, 128:256]
            qv, kcv, kpv = q_ref[...], kc_ref[...], kp_ref[...]
            qgain = qg_ref[...] * SW_SCALE
            qr, rq = _sw_norm_rope(qv, qgain, segv, cos_c, sin_c)
            kcr, rkc = _sw_norm_rope(kcv, kg_ref[...], segv, cos_c, sin_c)
            kpr, rkp = _sw_norm_rope(kpv, kg_ref[...], segv, cos_p, sin_p)
            vc, vp = vc_ref[...], vp_ref[...]
            lane = lax.broadcasted_iota(jnp.int32, (1, 128), 1)
            dsk = jnp.zeros((1, 128), F32)
            heads = range(SW_HEADS)
            ks = [slice(SW_HD * (h // SW_GROUP), SW_HD * (h // SW_GROUP + 1)) for h in heads]
            hs = [slice(SW_HD * h, SW_HD * (h + 1)) for h in heads]
            qh = [qr[:, hs[h]] for h in heads]
            doh = [dy_ref[:, hs[h]] for h in heads]
            raw = [_sw_scores(qh[h], kpr[:, ks[h]], kcr[:, ks[h]]) for h in heads]
            dpp = [_dot(doh[h], vp[:, ks[h]], NT) for h in heads]
            dpc = [_dot(doh[h], vc[:, ks[h]], NT) for h in heads]
            probs = [_sw_probs(raw[h], sk_ref[0, h], n == 0) for h in heads]
            dsp, dsc = [], []
            for h in heads:
                pp, pc, ps = probs[h]
                delta = jnp.sum(doh[h] * y_ref[:, hs[h]], axis=1, keepdims=True)
                dsp.append(pp * (dpp[h] - delta))
                dsc.append(pc * (dpc[h] - delta))
                dsk = dsk + jnp.where(lane == h, -jnp.sum(ps * delta), 0.0)
            for h in heads:
                dqr_s[:, hs[h]] = _dot(dsp[h], kpr[:, ks[h]]) + _dot(dsc[h], kcr[:, ks[h]])
            for kv in range(SW_KV_HEADS):
                group = range(SW_GROUP * kv, SW_GROUP * (kv + 1))
                kvs = slice(SW_HD * kv, SW_HD * (kv + 1))
                dvp_s[:, kvs] = sum(_dot(probs[h][0], doh[h], TN) for h in group)
                dvc_s[:, kvs] = sum(_dot(probs[h][1], doh[h], TN) for h in group)
                dkp_s[:, kvs] = sum(_dot(dsp[h], qh[h], TN) for h in group)
                dkc_s[:, kvs] = sum(_dot(dsc[h], qh[h], TN) for h in group)
            dq, gq = _sw_norm_rope_bwd(dqr_s[...], qv, rq, qgain, segv, cos_c, sin_c)
            dkc, gkc = _sw_norm_rope_bwd(dkc_s[...], kcv, rkc, kg_ref[...], segv, cos_c, sin_c)
            dkp, gkp = _sw_norm_rope_bwd(dkp_s[...], kpv, rkp, kg_ref[...], segv, cos_p, sin_p)
            gq_acc[...] += gq
            gk_acc[...] += gkc + gkp
            sk_acc[...] += dsk

            @pl.when(n > 0)
            def _():
                dp_ref[:, 0:512] = _mx(dq_car[...])
                dp_ref[:, 512:640] = _mx(dkv_car[:, 0:128] + dkp)
                dp_ref[:, 640:768] = _mx(dkv_car[:, 128:256] + dvp_s[...])

            dq_car[...] = dq
            dkv_car[:, 0:128] = dkc
            dkv_car[:, 128:256] = dvc_s[...]

        @pl.when(n == nb)
        def _():
            dp_ref[:, 0:512] = _mx(dq_car[...])
            dp_ref[:, 512:768] = _mx(dkv_car[...])

        @pl.when(last)
        def _():
            gq = gq_acc[...] * SW_SCALE
            acc = gq[:, 0:SW_HD]
            for h in range(1, SW_HEADS):
                acc = acc + gq[:, SW_HD * h:SW_HD * (h + 1)]
            dqg_ref[...] = acc
            gk = gk_acc[...]
            dkg_ref[...] = gk[:, 0:SW_HD] + gk[:, SW_HD:2 * SW_HD]
            dsk_ref[...] = sk_acc[...]

    rowq = pl.BlockSpec((SW_BLOCK, 512), lambda b, n: (cur(b, n), 0))
    full = lambda a: pl.BlockSpec(a.shape, lambda b, n: (0,) * a.ndim)

    def out_row(b, n):
        return b * nb + jnp.maximum(n - 1, 0)

    return pl.pallas_call(
        body, name="swa_bwd", grid=(bsz, nb + 1),
        in_specs=[rowq,
                  pl.BlockSpec((SW_BLOCK, 128), lambda b, n: (cur(b, n), 4)),
                  pl.BlockSpec((SW_BLOCK, 128), lambda b, n: (prev(b, n), 4)),
                  pl.BlockSpec((SW_BLOCK, 128), lambda b, n: (cur(b, n), 5)),
                  pl.BlockSpec((SW_BLOCK, 128), lambda b, n: (prev(b, n), 5)),
                  pl.BlockSpec((SW_BLOCK, 256), lambda b, n: (cur(b, n), 0)),
                  pl.BlockSpec((SW_BLOCK, 256), lambda b, n: (prev(b, n), 0)),
                  full(qg), full(kg),
                  pl.BlockSpec(memory_space=pltpu.SMEM),
                  full(seg),
                  pl.BlockSpec((SW_BLOCK, 512), lambda b, n: (cur(b, n), 1)),
                  pl.BlockSpec((SW_BLOCK, 512), lambda b, n: (cur(b, n), 1))],
        out_specs=(pl.BlockSpec((SW_BLOCK, SW_COLS), lambda b, n: (out_row(b, n), 0)),
                   pl.BlockSpec((1, SW_HD), lambda b, n: (0, 0)),
                   pl.BlockSpec((1, SW_HD), lambda b, n: (0, 0)),
                   pl.BlockSpec((1, 128), lambda b, n: (0, 0))),
        out_shape=(jax.ShapeDtypeStruct((t, SW_COLS), _MXU_DTYPE),
                   jax.ShapeDtypeStruct((1, SW_HD), F32),
                   jax.ShapeDtypeStruct((1, SW_HD), F32),
                   jax.ShapeDtypeStruct((1, 128), F32)),
        scratch_shapes=[pltpu.VMEM((SW_BLOCK, 512), F32), pltpu.VMEM((SW_BLOCK, 256), F32),
                        pltpu.VMEM((SW_BLOCK, 512), F32),
                        pltpu.VMEM((SW_BLOCK, 128), F32), pltpu.VMEM((SW_BLOCK, 128), F32),
                        pltpu.VMEM((SW_BLOCK, 128), F32), pltpu.VMEM((SW_BLOCK, 128), F32),
                        pltpu.VMEM((1, 512), F32), pltpu.VMEM((1, 128), F32), pltpu.VMEM((1, 128), F32)],
        compiler_params=_params(("arbitrary", "arbitrary")),
    )(proj, proj, proj, proj, proj, rope, rope, qg, kg, sinks, seg, y, dy)


def _head_rms(tv, gain):
    r = lax.rsqrt(jnp.mean(tv * tv, axis=1, keepdims=True) + EPS)
    return tv * r * gain, r


def _head_rms_bwd(dtn, tv, r, gain):
    u = dtn * gain
    return r * u - tv * (r * r * r) * jnp.mean(u * tv, axis=1, keepdims=True), jnp.sum(dtn * tv * r, axis=0, keepdims=True)


def _xa_softmax(raw):
    s = raw * (XA_HD ** -0.5)
    e = jnp.exp(s - jnp.max(s, axis=1, keepdims=True))
    return e * (1.0 / jnp.sum(e, axis=1, keepdims=True))


def _xa_fwd(qx, kvx, qg, kg, bsz, seq, mlen, *, tq=512):
    t = qx.shape[0]
    tq = min(tq, seq)
    nq = seq // tq
    w = XA_HEADS * XA_HD

    def body(q_ref, kv_ref, qg_ref, kg_ref, o_ref):
        heads = range(XA_HEADS)
        hs = [slice(XA_HD * h, XA_HD * (h + 1)) for h in heads]
        qn = [_head_rms(q_ref[:, hs[h]], qg_ref[...])[0] for h in heads]
        kn = [_head_rms(kv_ref[:, hs[h]], kg_ref[...])[0] for h in heads]
        raw = [_dot(qn[h], kn[h], NT) for h in heads]
        p = [_xa_softmax(raw[h]) for h in heads]
        for h in heads:
            o_ref[:, hs[h]] = _dot(p[h], kv_ref[:, w + XA_HD * h:w + XA_HD * (h + 1)]).astype(o_ref.dtype)

    vec = pl.BlockSpec((1, XA_HD), lambda b, i: (0, 0))
    return pl.pallas_call(
        body, name="xattn_fwd", grid=(bsz, nq),
        in_specs=[pl.BlockSpec((tq, w), lambda b, i: (b * nq + i, 0)),
                  pl.BlockSpec((mlen, 2 * w), lambda b, i: (b, 0)), vec, vec],
        out_specs=pl.BlockSpec((tq, w), lambda b, i: (b * nq + i, 0)),
        out_shape=jax.ShapeDtypeStruct((t, w), _MXU_DTYPE),
        compiler_params=_params(("parallel", "parallel")),
    )(qx, kvx, qg, kg)


def _xa_bwd(qx, kvx, qg, kg, do, bsz, seq, mlen, *, tq=512):
    t = qx.shape[0]
    tq = min(tq, seq)
    nq = seq // tq
    w = XA_HEADS * XA_HD
    scale = XA_HD ** -0.5

    def body(q_ref, kv_ref, qg_ref, kg_ref, do_ref, dq_ref, dkv_ref, dqg_ref, dkg_ref):
        b, i = pl.program_id(0), pl.program_id(1)

        @pl.when(jnp.logical_and(b == 0, i == 0))
        def _():
            dqg_ref[...] = jnp.zeros_like(dqg_ref)
            dkg_ref[...] = jnp.zeros_like(dkg_ref)

        @pl.when(i == 0)
        def _():
            dkv_ref[...] = jnp.zeros_like(dkv_ref)

        heads = range(XA_HEADS)
        hs = [slice(XA_HD * h, XA_HD * (h + 1)) for h in heads]
        vs = [slice(w + XA_HD * h, w + XA_HD * (h + 1)) for h in heads]
        qv = [q_ref[:, hs[h]] for h in heads]
        kv = [kv_ref[:, hs[h]] for h in heads]
        doh = [do_ref[:, hs[h]] for h in heads]
        qn = [_head_rms(qv[h], qg_ref[...]) for h in heads]
        kn = [_head_rms(kv[h], kg_ref[...]) for h in heads]
        raw = [_dot(qn[h][0], kn[h][0], NT) for h in heads]
        dp = [_dot(doh[h], kv_ref[:, vs[h]], NT) for h in heads]
        p = [_xa_softmax(raw[h]) for h in heads]
        ds = [p[h] * (dp[h] - jnp.sum(p[h] * dp[h], axis=1, keepdims=True)) * scale for h in heads]
        dqn = [_dot(ds[h], kn[h][0]) for h in heads]
        dkn = [_dot(ds[h], qn[h][0], TN) for h in heads]
        dvv = [_dot(p[h], doh[h], TN) for h in heads]
        gq_sum = jnp.zeros((1, XA_HD), F32)
        gk_sum = jnp.zeros((1, XA_HD), F32)
        for h in heads:
            dqv, gq = _head_rms_bwd(dqn[h], qv[h], qn[h][1], qg_ref[...])
            dkv, gk = _head_rms_bwd(dkn[h], kv[h], kn[h][1], kg_ref[...])
            dq_ref[:, hs[h]] = dqv.astype(dq_ref.dtype)
            dkv_ref[:, hs[h]] += dkv
            dkv_ref[:, vs[h]] += dvv[h]
            gq_sum = gq_sum + gq
            gk_sum = gk_sum + gk
        dqg_ref[...] += gq_sum
        dkg_ref[...] += gk_sum

    vec = pl.BlockSpec((1, XA_HD), lambda b, i: (0, 0))
    row = pl.BlockSpec((tq, w), lambda b, i: (b * nq + i, 0))
    mem = pl.BlockSpec((mlen, 2 * w), lambda b, i: (b, 0))
    return pl.pallas_call(
        body, name="xattn_bwd", grid=(bsz, nq),
        in_specs=[row, mem, vec, vec, row],
        out_specs=(row, mem, vec, vec),
        out_shape=(jax.ShapeDtypeStruct((t, w), _MXU_DTYPE), jax.ShapeDtypeStruct((bsz * mlen, 2 * w), F32),
                   jax.ShapeDtypeStruct((1, XA_HD), F32), jax.ShapeDtypeStruct((1, XA_HD), F32)),
        compiler_params=_params(("arbitrary", "arbitrary")),
    )(qx, kvx, qg, kg, do)


def _loss_finish(sq_row, d_model):
    def body(s_ref, o_ref):
        o_ref[...] = jnp.zeros_like(o_ref) + 0.5 * jnp.sum(s_ref[...]) / float(d_model)

    return pl.pallas_call(body, name="loss_finish", out_shape=jax.ShapeDtypeStruct((1, 128), F32))(sq_row)


def _adamw_math(w, g, m, v):
    m = ADAM_B1 * m + (1.0 - ADAM_B1) * g
    v = ADAM_B2 * v + (1.0 - ADAM_B2) * (g * g)
    m_hat = m / (1.0 - ADAM_B1 ** ADAM_STEP)
    v_hat = v / (1.0 - ADAM_B2 ** ADAM_STEP)
    return -ADAM_LR * (m_hat / (jnp.sqrt(v_hat) + ADAM_EPS) + ADAM_WD * w), m, v


def _adamw_big(w, g, m, v, *, name, tr=512):
    r, c = w.shape
    tr = min(tr, r)
    if r % tr:
        tr = r // 2
    assert r % tr == 0 and tr % 8 == 0, (name, r, tr)

    def body(w_ref, g_ref, m_ref, v_ref, go_ref, d_ref, mo_ref, vo_ref):
        gv = g_ref[...]
        d, mn, vn = _adamw_math(w_ref[...], gv, m_ref[...], v_ref[...])
        go_ref[...] = gv
        d_ref[...] = d
        mo_ref[...] = mn
        vo_ref[...] = vn

    spec = pl.BlockSpec((tr, c), lambda i: (i, 0))
    shp = jax.ShapeDtypeStruct((r, c), F32)
    return pl.pallas_call(
        body, name=name, grid=(r // tr,), in_specs=[spec] * 4, out_specs=(spec,) * 4, out_shape=(shp,) * 4,
        compiler_params=_params(("parallel",)),
    )(w, g, m, v)


def _adamw_small(ws, gs, ms, vs):
    n = len(ws)

    def body(*refs):
        for i in range(n):
            d, mn, vn = _adamw_math(refs[i][...], refs[n + i][...], refs[2 * n + i][...], refs[3 * n + i][...])
            refs[4 * n + i][...] = d
            refs[5 * n + i][...] = mn
            refs[6 * n + i][...] = vn

    shapes = tuple(jax.ShapeDtypeStruct(w.shape, F32) for w in ws)
    return pl.pallas_call(body, name="adamw_small", out_shape=shapes * 3)(*ws, *gs, *ms, *vs)


def _add_halves(g, recv, c_idx, *, name, tr=512):
    _, r, c = g.shape
    h = r // 2
    tr = min(tr, h)
    nt = h // tr

    def body(c_ref, g_ref, r_ref, o_ref):
        del c_ref
        o_ref[...] = g_ref[...] + r_ref[...]

    return pl.pallas_call(
        body, name=name,
        grid_spec=pltpu.PrefetchScalarGridSpec(
            num_scalar_prefetch=1, grid=(4, nt),
            in_specs=[pl.BlockSpec((None, tr, c), lambda k, i, cr: (k, cr[0] * nt + i, 0)),
                      pl.BlockSpec((None, tr, c), lambda k, i, cr: (k, i, 0))],
            out_specs=pl.BlockSpec((None, tr, c), lambda k, i, cr: (k, i, 0))),
        out_shape=jax.ShapeDtypeStruct((4, h, c), F32),
        compiler_params=_params(("parallel", "parallel")),
    )(c_idx, g, recv)


def _add_chips(p, recv, place_idx, *, name, tr=512, after=()):
    _, h, c = p.shape
    tr = min(tr, h)
    nt = h // tr

    def body(pi_ref, p_ref, r_ref, *rest):
        del pi_ref
        rest[-1][...] = ((p_ref[...] + r_ref[0]) + r_ref[1]) + r_ref[2]

    return pl.pallas_call(
        body, name=name,
        grid_spec=pltpu.PrefetchScalarGridSpec(
            num_scalar_prefetch=1, grid=(nt,),
            in_specs=[pl.BlockSpec((None, tr, c), lambda i, pi: (pi[0], i, 0)),
                      pl.BlockSpec((3, tr, c), lambda i, pi: (0, i, 0))] + [pl.BlockSpec(memory_space=pl.ANY)] * len(after),
            out_specs=pl.BlockSpec((tr, c), lambda i, pi: (pi[1] * nt + i, 0))),
        out_shape=jax.ShapeDtypeStruct((2 * h, c), F32),
        compiler_params=_params(("parallel",)),
    )(place_idx, p, recv, *after)


def _place_shard(shard, place_idx, *, name, tr=512, after=()):
    r, c = shard.shape
    tr = min(tr, r)
    if r % tr:
        tr = r // 2
    assert r % tr == 0 and tr % 16 == 0, (name, r, tr)

    def body(pi_ref, s_ref, *rest):
        del pi_ref
        rest[-1][...] = s_ref[...]

    return pl.pallas_call(
        body, name=name,
        grid_spec=pltpu.PrefetchScalarGridSpec(
            num_scalar_prefetch=1, grid=(r // tr,),
            in_specs=[pl.BlockSpec((tr, c), lambda i, pi: (i, 0))] + [pl.BlockSpec(memory_space=pl.ANY)] * len(after),
            out_specs=pl.BlockSpec((None, tr, c), lambda i, pi: (pi[0], i, 0))),
        out_shape=jax.ShapeDtypeStruct((4, r, c), shard.dtype),
        compiler_params=_params(("parallel",)),
    )(place_idx, shard, *after)


def _place():
    x, y, c = lax.axis_index("x"), lax.axis_index("y"), lax.axis_index("c")
    chips = [(1 - x, y), (x, 1 - y), (1 - x, 1 - y)]
    return x, y, c, chips


ANY = pl.BlockSpec(memory_space=pl.ANY)


def _exchange_halves(grads, name):
    n = len(grads)

    def body(*refs):
        ins, outs = refs[:n], refs[n:2 * n]
        send_sems, recv_sems = refs[2 * n:]
        x, y, c, _ = _place()

        def copy(a):
            h = ins[a].shape[1] // 2
            return pltpu.make_async_remote_copy(
                src_ref=ins[a].at[:, pl.ds((1 - c) * h, h), :], dst_ref=outs[a],
                send_sem=send_sems.at[a], recv_sem=recv_sems.at[a], device_id=(x, y, 1 - c), device_id_type=MESH)

        for a in range(n):
            copy(a).start()
        for a in range(n):
            copy(a).wait_recv()
        for a in range(n):
            copy(a).wait_send()

    return pl.pallas_call(
        body, name=name,
        in_specs=[ANY] * n, out_specs=tuple([ANY] * n),
        out_shape=tuple(jax.ShapeDtypeStruct((4, g.shape[1] // 2, g.shape[2]), g.dtype) for g in grads),
        scratch_shapes=[pltpu.SemaphoreType.DMA((n,)), pltpu.SemaphoreType.DMA((n,))],
    )(*grads)


HBM = pl.BlockSpec(memory_space=pltpu.HBM)
SEM = pl.BlockSpec(memory_space=pltpu.SEMAPHORE)
EFFECT = pltpu.SideEffectType.DATAFLOW_SIDE_EFFECTING


def _in_hbm(a):
    return pltpu.with_memory_space_constraint(a, pltpu.HBM)


def _split_copy_calls(name, srcs, lands, n_copies, make_copies):
    ns, nl = len(srcs), len(lands)
    nb = ns + nl

    def start(after=()):
        n_after = len(after)

        def body(*refs):
            outs = refs[nb + n_after:]
            copies = make_copies(refs[:ns], refs[ns:nb], outs[0], outs[1])
            for cp in copies:
                cp.start()
            token = refs[-1]
            token[...] = jnp.zeros_like(token)

        bufs = [_in_hbm(a) for a in list(srcs) + list(lands)]
        out = pl.pallas_call(
            body, name=name + "_start",
            out_shape=(pltpu.SemaphoreType.DMA((n_copies,)), pltpu.SemaphoreType.DMA((n_copies,)),
                       *[pltpu.HBM(a.shape, a.dtype) for a in bufs], jax.ShapeDtypeStruct((8, 128), F32)),
            in_specs=[HBM] * nb + [pl.BlockSpec(memory_space=pl.ANY)] * n_after,
            out_specs=(SEM, SEM, *[HBM] * nb, pl.BlockSpec(memory_space=pltpu.VMEM)),
            input_output_aliases={i: 2 + i for i in range(nb)},
            compiler_params=pltpu.CompilerParams(has_side_effects=EFFECT),
        )(*bufs, *after)
        return dict(send=out[0], recv=out[1], bufs=list(out[2:2 + nb]), token=out[-1])

    def wait(state, after):
        def body(*refs):
            copies = make_copies(refs[:ns], refs[ns:nb], refs[nb], refs[nb + 1])
            for cp in copies:
                cp.wait_send()
            for cp in copies:
                cp.wait_recv()

        bufs = state["bufs"]
        out = pl.pallas_call(
            body, name=name + "_wait",
            out_shape=tuple(pltpu.HBM(a.shape, a.dtype) for a in bufs),
            in_specs=[HBM] * nb + [SEM, SEM] + [pl.BlockSpec(memory_space=pl.ANY)] * len(after),
            out_specs=tuple([HBM] * nb),
            input_output_aliases={i: i for i in range(nb)},
            compiler_params=pltpu.CompilerParams(has_side_effects=EFFECT),
        )(*bufs, state["send"], state["recv"], *after)
        return list(out[:ns]), list(out[ns:])

    return start, wait


def _scatter_chips_split(name, parts):
    n = len(parts)
    lands = [lax.empty((3,) + p.shape[1:], p.dtype) for p in parts]

    def make_copies(srcs, lnds, send_sems, recv_sems):
        _, _, c, chips = _place()
        return [pltpu.make_async_remote_copy(
            src_ref=srcs[a].at[2 * px + py], dst_ref=lnds[a].at[j], send_sem=send_sems.at[a * 3 + j],
            recv_sem=recv_sems.at[a * 3 + j], device_id=(px, py, c), device_id_type=MESH)
            for a in range(n) for j, (px, py) in enumerate(chips)]

    return _split_copy_calls(name, parts, lands, 3 * n, make_copies)


def _exchange_halves_split(name, grads):
    n = len(grads)
    lands = [lax.empty((4, g.shape[1] // 2, g.shape[2]), g.dtype) for g in grads]

    def make_copies(srcs, lnds, send_sems, recv_sems):
        x, y, c, _ = _place()
        out = []
        for a in range(n):
            h = srcs[a].shape[1] // 2
            out.append(pltpu.make_async_remote_copy(
                src_ref=srcs[a].at[:, pl.ds((1 - c) * h, h), :], dst_ref=lnds[a], send_sem=send_sems.at[a],
                recv_sem=recv_sems.at[a], device_id=(x, y, 1 - c), device_id_type=MESH))
        return out

    return _split_copy_calls(name, grads, lands, n, make_copies)


def _gather_chips_split(name, shards, lands):
    n = len(shards)

    def make_copies(srcs, lnds, send_sems, recv_sems):
        x, y, c, chips = _place()
        out = []
        for a in range(n):
            h = srcs[a].shape[0] // 2
            for j, (px, py) in enumerate(chips):
                out.append(pltpu.make_async_remote_copy(
                    src_ref=srcs[a].at[pl.ds(c * h, h), :], dst_ref=lnds[a].at[2 * x + y, pl.ds(c * h, h), :],
                    send_sem=send_sems.at[a * 3 + j], recv_sem=recv_sems.at[a * 3 + j],
                    device_id=(px, py, c), device_id_type=MESH))
        return out

    return _split_copy_calls(name, shards, lands, 3 * n, make_copies)


def _gather_finish(gathered, name):
    n = len(gathered)

    def body(*refs):
        outs = refs[n:2 * n]
        send_sems, recv_sems = refs[2 * n:]
        x, y, c, chips = _place()

        def copy(a, j, chip_idx, which):
            h = outs[a].shape[1] // 2
            rows = outs[a].at[chip_idx, pl.ds(which * h, h), :]
            return pltpu.make_async_remote_copy(
                src_ref=rows, dst_ref=rows, send_sem=send_sems.at[a * 3 + j], recv_sem=recv_sems.at[a * 3 + j],
                device_id=(x, y, 1 - c), device_id_type=MESH)

        for a in range(n):
            for j, (px, py) in enumerate(chips):
                copy(a, j, 2 * px + py, c).start()
        for a in range(n):
            for j, (px, py) in enumerate(chips):
                copy(a, j, 2 * px + py, 1 - c).wait_recv()
        for a in range(n):
            for j, (px, py) in enumerate(chips):
                copy(a, j, 2 * px + py, c).wait_send()

    return pl.pallas_call(
        body, name=name,
        in_specs=[ANY] * n, out_specs=tuple([ANY] * n),
        out_shape=tuple(jax.ShapeDtypeStruct(g.shape, g.dtype) for g in gathered),
        input_output_aliases={i: i for i in range(n)},
        scratch_shapes=[pltpu.SemaphoreType.DMA((3 * n,)), pltpu.SemaphoreType.DMA((3 * n,))],
    )(*gathered)


def _gather_forward_split(name, gathered):
    n = len(gathered)

    def make_copies(srcs, lnds, send_sems, recv_sems):
        x, y, c, chips = _place()
        out = []
        for a in range(n):
            h = lnds[a].shape[1] // 2
            for j, (px, py) in enumerate(chips):
                rows = lnds[a].at[2 * px + py, pl.ds(c * h, h), :]
                out.append(pltpu.make_async_remote_copy(
                    src_ref=rows, dst_ref=rows, send_sem=send_sems.at[a * 3 + j], recv_sem=recv_sems.at[a * 3 + j],
                    device_id=(x, y, 1 - c), device_id_type=MESH))
        return out

    return _split_copy_calls(name, [], gathered, 3 * n, make_copies)


def _join_halves_split(name, fulls):
    n = len(fulls)

    def make_copies(srcs, lnds, send_sems, recv_sems):
        x, y, c, _ = _place()
        out = []
        for a in range(n):
            h = lnds[a].shape[0] // 2
            rows = lnds[a].at[pl.ds(c * h, h), :]
            out.append(pltpu.make_async_remote_copy(
                src_ref=rows, dst_ref=rows, send_sem=send_sems.at[a], recv_sem=recv_sems.at[a],
                device_id=(x, y, 1 - c), device_id_type=MESH))
        return out

    return _split_copy_calls(name, [], fulls, n, make_copies)


def _all_gather_small_split(sm):
    r, w = sm.shape

    def make_copies(srcs, lnds, send_sems, recv_sems):
        x, y, c, _ = _place()
        me = 4 * x + 2 * y + c
        rel = [(dx, dy, dc) for dx in (0, 1) for dy in (0, 1) for dc in (0, 1)][1:]
        return [pltpu.make_async_remote_copy(
            src_ref=srcs[0], dst_ref=lnds[0].at[me], send_sem=send_sems.at[k], recv_sem=recv_sems.at[k],
            device_id=(1 - x if dx else x, 1 - y if dy else y, 1 - c if dc else c), device_id_type=MESH)
            for k, (dx, dy, dc) in enumerate(rel)]

    return _split_copy_calls("all_gather_small", [sm], [lax.empty((8, r, w), sm.dtype)], 7, make_copies)


def _sum_devices(sm, gathered, me_idx):
    def body(me_ref, sm_ref, g_ref, o_ref):
        own = sm_ref[...]
        acc = jnp.where(me_ref[0] == 0, own, g_ref[0])
        for d in range(1, 8):
            acc = acc + jnp.where(me_ref[0] == d, own, g_ref[d])
        o_ref[...] = acc

    vm = pl.BlockSpec(memory_space=pltpu.VMEM)
    return pl.pallas_call(
        body, name="sum_devices", in_specs=[pl.BlockSpec(memory_space=pltpu.SMEM), vm, vm], out_specs=vm,
        out_shape=jax.ShapeDtypeStruct(sm.shape, F32),
    )(me_idx, sm, gathered)


def _local_step(x3, mem3, pos2, target3, small, comm):
    bsz, seq, d = x3.shape
    mlen = mem3.shape[1]
    t = bsz * seq
    tok = comm.begin()
    x = x3.reshape(t, d)
    mem = mem3.reshape(bsz * mlen, d)
    target = target3.reshape(t, d)
    rope = _rope_table(pos2.reshape(t, 1))
    qg_t = jnp.tile(small["sw_q_norm_g"], (1, SW_HEADS))
    kg_t = jnp.tile(small["sw_k_norm_g"], (1, SW_KV_HEADS))

    hn1 = _rms_fwd(x, small["norm1_g"], name="rms1_fwd", after=tok)
    w = comm.first(hn1)
    w_in_t = w["w_in_t"]
    w_sw_t = w_in_t[HG_COLS:]
    proj_hg = _mm(hn1, w_in_t, NT, t, HG_COLS, d, name="proj_hg", tk=d, after=(w.get("token"),))[0]
    proj_sw = _mm(hn1, w_sw_t, NT, t, SW_COLS, d, name="proj_sw", tk=d)[0]
    y_mix, o_hg, states = _hg_fwd(proj_hg, small["hg_lower_bounds"], small["hg_norm_g"], bsz, seq, y_width=1024)
    y_mix = _sw_fwd(proj_sw, rope, qg_t, kg_t, small["sw_sinks"], y_mix, bsz, seq)
    w = comm.rest(y_mix)
    h1, hn2 = _mm(y_mix, w["w_out"], NN, t, d, 1024, name="out_proj", tk=1024, extras=(x,), rows=(small["norm2_g"],),
                  epilogue=_residual_rms, out_dtypes=(F32, _MXU_DTYPE), after=(w.get("token"),))
    mn = _rms_fwd(mem, small["mem_norm_g"], name="rms_mem_fwd")
    qx = _mm(hn2, w["wq"], NN, t, 512, d, name="xa_q", tk=d)[0]
    kvx = _mm(mn, w["wkv"], NN, bsz * mlen, 1024, d, name="xa_kv", tk=d)[0]
    ox = _xa_fwd(qx, kvx, small["xa_q_norm_g"], small["xa_k_norm_g"], bsz, seq, mlen)
    h2, hn3 = _mm(ox, w["wo"], NN, t, d, 512, name="xa_o", tk=512, extras=(h1,), rows=(small["norm3_g"],),
                  epilogue=_residual_rms, out_dtypes=(F32, _MXU_DTYPE))
    w = {**w, **comm.mlp(hn3)}
    ff = w["down"].shape[0]
    ffs = ff // 4

    def relu_sq(acc):
        a = jnp.maximum(acc, 0.0)
        return a, a * a

    act, act2 = _mm(hn3, w["up"], NN, t, ff, d, name="mlp_up", tm=2048, tn=ffs, tk=d,
                    b_spec=pl.BlockSpec((None, d, ffs), lambda i, j, kk: (j, 0, 0)),
                    epilogue=relu_sq, out_dtypes=(_MXU_DTYPE, _MXU_DTYPE))
    inv_d = 1.0 / d

    def loss_cotangent(acc, res, tgt):
        diff = acc + res - tgt
        v = diff * inv_d
        return v, v, jnp.sum(diff * diff, axis=0, keepdims=True)

    dy, dy_mx, sq_row = _mm(act2, w["down"], NN, t, d, ff, name="mlp_down", tk=2048, extras=(h2, target),
                            epilogue=loss_cotangent, out_dtypes=(F32, _MXU_DTYPE), row_sums=1)
    loss_row = _loss_finish(sq_row, d)

    dz = _mm(dy_mx, w["down"], NT, t, ff, d, name="d_act", tm=2048, tk=d, extras=(act,),
             epilogue=lambda acc, a: (acc * (2.0 * a.astype(F32)),), out_dtypes=(_MXU_DTYPE,))[0]
    g_down = _mm(act2, dy_mx, TN, ff, d, t, name="g_down", tk=t)[0]
    g_up = _mm(hn3, dz, TN, d, ff, t, name="g_up", tn=ffs, tk=t,
               out_shape=(jax.ShapeDtypeStruct((4, d, ffs), F32),),
               out_spec=(pl.BlockSpec((None, min(1024, d), ffs), lambda i, j, kk: (j, i, 0)),))[0]
    tok = comm.grads("mlp", dict(up=g_up, down=g_down))
    dh2, dh2_mx, g_norm3 = _mm(dz, w["up"], NT, t, d, ff, name="d_hn3", tk=ffs, after=tok,
                               b_spec=pl.BlockSpec((None, min(1024, d), ffs), lambda i, j, kk: (kk, j, 0)),
                               extras=(h2, dy), rows=(small["norm3_g"],), epilogue=_rms_bwd_residual,
                               out_dtypes=(F32, _MXU_DTYPE), row_sums=1)
    d_ox = _mm(dh2_mx, w["wo"], NT, t, 512, d, name="d_ox", tk=d)[0]
    g_wo = _mm(ox, dh2_mx, TN, 512, d, t, name="g_wo", tk=t)[0]
    d_qx, d_kvx, g_xq, g_xk = _xa_bwd(qx, kvx, small["xa_q_norm_g"], small["xa_k_norm_g"], d_ox, bsz, seq, mlen)
    g_wq = _mm(hn2, d_qx, TN, d, 512, t, name="g_wq")[0]
    g_wkv = _mm(mn, d_kvx, TN, d, 1024, bsz * mlen, name="g_wkv")[0]
    dh1, dh1_mx, g_norm2 = _mm(d_qx, w["wq"], NT, t, d, 512, name="d_hn2", tk=512, extras=(h1, dh2),
                               rows=(small["norm2_g"],), epilogue=_rms_bwd_residual, out_dtypes=(F32, _MXU_DTYPE),
                               row_sums=1)
    dmn = _mm(d_kvx, w["wkv"], NT, bsz * mlen, d, 1024, name="d_mn", tk=1024)[0]
    g_memn = _rms_gain_grad(mem, small["mem_norm_g"], dmn, name="rms_mem_bwd")
    g_wout = _mm(y_mix, dh1_mx, TN, 1024, d, t, name="g_wout", tk=2048)[0]
    tok = comm.grads("mid", dict(w_out=g_wout, wq=g_wq, wkv=g_wkv, wo=g_wo))
    d_mix = _mm(dh1_mx, w["w_out"], NT, t, 1024, d, name="d_mix", tk=d, after=tok)[0]
    dproj_sw, g_swq, g_swk, g_sinks = _sw_bwd(proj_sw, rope, qg_t, kg_t, small["sw_sinks"], y_mix, d_mix, bsz, seq)
    tok = comm.poll(dproj_sw)
    dproj_hg, g_lb, g_hgn = _hg_bwd(proj_hg, small["hg_lower_bounds"], small["hg_norm_g"], o_hg, states, d_mix, bsz, seq,
                                    after=tok)
    in_rows = HG_COLS + SW_COLS
    sw_tile = 256
    g_in_t = _mm(dproj_hg, hn1, TN, HG_COLS, d, t, name="g_in_hg", tk=t,
                 out_shape=(jax.ShapeDtypeStruct((in_rows, d), F32),),
                 out_spec=(pl.BlockSpec((1024, min(1024, d)), lambda i, j, kk: (i, j)),))[0]
    g_in_t = _mm(dproj_sw, hn1, TN, SW_COLS, d, t, name="g_in_sw", tm=sw_tile, into=g_in_t,
                 out_shape=(jax.ShapeDtypeStruct((in_rows, d), F32),),
                 out_spec=(pl.BlockSpec((sw_tile, min(1024, d)), lambda i, j, kk: (HG_COLS // sw_tile + i, j)),))[0]
    tok = comm.grads("in", dict(w_in_t=g_in_t))
    dhn1_a = _mm(dproj_hg, w_in_t, NN, t, d, HG_COLS, name="d_hn1_hg", tk=HG_COLS, after=tok)[0]
    grad_x, g_norm1 = _mm(dproj_sw, w_sw_t, NN, t, d, SW_COLS, name="d_hn1_sw", tk=SW_COLS, extras=(dhn1_a, x, dh1),
                          rows=(small["norm1_g"],), row_sums=1,
                          epilogue=lambda acc, prev, xv, dres, g: _rms_bwd_residual(acc + prev, xv, dres, g)[1:])

    g_small = dict(norm1_g=g_norm1, hg_lower_bounds=g_lb, hg_norm_g=g_hgn, sw_q_norm_g=g_swq, sw_k_norm_g=g_swk,
                   sw_sinks=g_sinks[:, 0:SW_HEADS], norm2_g=g_norm2, mem_norm_g=g_memn, xa_q_norm_g=g_xq,
                   xa_k_norm_g=g_xk, norm3_g=g_norm3)
    return loss_row, grad_x.reshape(bsz, seq, d), g_small


SMALL_NAMES = ("norm1_g", "hg_lower_bounds", "hg_norm_g", "sw_q_norm_g", "sw_k_norm_g", "sw_sinks", "norm2_g",
               "mem_norm_g", "xa_q_norm_g", "xa_k_norm_g", "norm3_g")
BIG_NAMES = ("w_in", "w_out", "xa_wq", "xa_wkv", "xa_wo", "mlp_up", "mlp_down")
WEIGHT_ORDER = ("norm1_g", "w_in", "hg_lower_bounds", "hg_norm_g", "sw_q_norm_g", "sw_k_norm_g", "sw_sinks", "w_out",
                "norm2_g", "mem_norm_g", "xa_wq", "xa_wkv", "xa_q_norm_g", "xa_k_norm_g", "xa_wo", "norm3_g",
                "mlp_up", "mlp_down")


def _pack_rows(vals, width):
    starts, at = [], 0
    for v in vals:
        starts.append(at)
        at += v.shape[0]
    total = at + (-at) % 8
    out = None
    for v, s in zip(vals, starts):
        placed = jnp.pad(v, ((s, total - s - v.shape[0]), (0, width - v.shape[1])))
        out = placed if out is None else out + placed
    return out, starts


class _MeshWeights:
    LATE = ("w_out", "xa_wq", "xa_wkv", "xa_wo", "mlp_up", "mlp_down")

    def __init__(self, shards, d, ff):
        self.shards, self.d, self.ff = shards, d, ff
        self.c_idx = lax.axis_index("c").astype(jnp.int32).reshape(1)
        chip = (2 * lax.axis_index("x") + lax.axis_index("y")).astype(jnp.int32)
        self.place_idx = jnp.stack([chip, lax.axis_index("c").astype(jnp.int32)])
        self.pending = []
        self.exchanging = None

    def begin(self):
        shard = self.shards["w_in"]
        start, self.in_wait = _gather_chips_split(
            "gather_in", [shard], [_place_shard(shard, self.place_idx, name="place_w_in")])
        self.in_state = start()
        tok = (self.in_state["token"],)
        self.placed = [_place_shard(self.shards[n], self.place_idx, name="place_" + n, after=tok) for n in self.LATE]
        return tok

    def first(self, after):
        _, lands = self.in_wait(self.in_state, (after, *self.placed))
        (g_in,) = _gather_finish(lands, "gather_in_finish")
        start, self.late_wait = _gather_chips_split("gather_late", [self.shards[n] for n in self.LATE], self.placed)
        self.late_state = start(after=(g_in,))
        return dict(w_in_t=g_in.reshape(-1, self.d), token=self.late_state["token"])

    def rest(self, after):
        _, lands = self.late_wait(self.late_state, (after,))
        g_out, g_q, g_kv, g_o = _gather_finish(lands[:4], "gather_late_finish")
        start, self.mlp_wait = _gather_forward_split("gather_mlp_forward", lands[4:])
        self.mlp_state = start(after=(g_out,))
        d = self.d
        return dict(w_out=g_out.reshape(-1, d), wq=g_q.reshape(d, -1), wkv=g_kv.reshape(d, -1),
                    wo=jnp.concatenate([g_o[k] for k in range(4)], axis=1), token=self.mlp_state["token"])

    def mlp(self, after):
        _, (g_up, g_dn) = self.mlp_wait(self.mlp_state, (after,))
        return dict(up=g_up, down=g_dn.reshape(self.ff, self.d))

    def _scatter(self, tag, names, arrays, recv):
        parts = [_add_halves(g, r, self.c_idx, name="rs_add_halves_" + n) for n, g, r in zip(names, arrays, recv)]
        start, wait = _scatter_chips_split("rs_scatter_" + tag, parts)
        state = start()
        self.pending.append((names, wait, state))
        return state["token"]

    def _advance(self, after):
        if self.exchanging is None:
            return ()
        tag, names, wait, state = self.exchanging
        self.exchanging = None
        arrays, recv = wait(state, (after,))
        return (self._scatter(tag, names, arrays, recv),)

    def poll(self, after):
        return self._advance(after)

    def grads(self, tag, g):
        d, ff = self.d, self.ff
        if tag == "mlp":
            names, arrays = ("mlp_up", "mlp_down"), [g["up"], g["down"].reshape(4, ff // 4, d)]
        elif tag == "mid":
            names = ("w_out", "xa_wq", "xa_wkv", "xa_wo")
            ds = d // 4
            g_wo = jnp.stack([g["wo"][:, ds * k:ds * (k + 1)] for k in range(4)])
            arrays = [g["w_out"].reshape(4, -1, d), g["wq"].reshape(4, d // 4, -1), g["wkv"].reshape(4, d // 4, -1), g_wo]
        else:
            names, arrays = ("w_in",), [g["w_in_t"].reshape(4, -1, d)]
        toks = self._advance(arrays[0])
        if tag == "in":
            return toks + (self._scatter(tag, names, arrays, _exchange_halves(arrays, "rs_exchange_" + tag)),)
        start, wait = _exchange_halves_split("rs_exchange_" + tag, arrays)
        state = start()
        self.exchanging = (tag, names, wait, state)
        return toks + (state["token"],)

    def finish(self, after):
        joins, tok = [], ()
        for names, wait, state in self.pending:
            srcs, lands = wait(state, tuple(after) + tok)
            fulls = [_add_chips(p, r, self.place_idx, name="rs_add_chips_" + n, after=tok)
                     for n, p, r in zip(names, srcs, lands)]
            start, jwait = _join_halves_split("rs_join_" + names[0], fulls)
            jstate = start()
            tok = (jstate["token"],)
            joins.append((names, jwait, jstate))
        out = {}
        for names, jwait, jstate in joins:
            _, fulls = jwait(jstate, tok)
            out.update(zip(names, fulls))
        return out


def kernel(x, mem, positions, norm1_g, w_in, hg_lower_bounds, hg_norm_g, sw_q_norm_g, sw_k_norm_g, sw_sinks, w_out, norm2_g, mem_norm_g, xa_wq, xa_wkv, xa_q_norm_g, xa_k_norm_g, xa_wo, norm3_g, mlp_up, mlp_down, loss_target, m_norm1_g, m_w_in, m_hg_lower_bounds, m_hg_norm_g, m_sw_q_norm_g, m_sw_k_norm_g, m_sw_sinks, m_w_out, m_norm2_g, m_mem_norm_g, m_xa_wq, m_xa_wkv, m_xa_q_norm_g, m_xa_k_norm_g, m_xa_wo, m_norm3_g, m_mlp_up, m_mlp_down, v_norm1_g, v_w_in, v_hg_lower_bounds, v_hg_norm_g, v_sw_q_norm_g, v_sw_k_norm_g, v_sw_sinks, v_w_out, v_norm2_g, v_mem_norm_g, v_xa_wq, v_xa_wkv, v_xa_q_norm_g, v_xa_k_norm_g, v_xa_wo, v_norm3_g, v_mlp_up, v_mlp_down):
    given = dict(locals())
    weights = {n: given[n] for n in WEIGHT_ORDER}
    moms = {n: given["m_" + n] for n in WEIGHT_ORDER}
    vars_ = {n: given["v_" + n] for n in WEIGHT_ORDER}
    d = x.shape[-1]
    ff = mlp_down.shape[1] * 4
    small = {n: weights[n] for n in SMALL_NAMES}

    def plain(n, a):
        return jnp.swapaxes(a[0], 0, 1) if n == "w_in" else a[0]

    comm = _MeshWeights({n: plain(n, weights[n]).astype(_MXU_DTYPE) for n in BIG_NAMES}, d, ff)
    loss_row, grad_x, g_small = _local_step(x, mem, positions, loss_target, small, comm)
    packed, starts = _pack_rows([g_small[n] for n in SMALL_NAMES] + [loss_row], 1024)
    start, wait = _all_gather_small_split(packed)
    state = start()
    big_grads = comm.finish((grad_x, state["token"]))
    (own,), (gathered,) = wait(state, (big_grads[BIG_NAMES[0]],))
    device = (4 * lax.axis_index("x") + 2 * lax.axis_index("y") + lax.axis_index("c")).astype(jnp.int32).reshape(1)
    summed = _sum_devices(own, gathered, device)
    small_grads = {}
    for n, s in zip(SMALL_NAMES, starts):
        r, c = weights[n].shape
        small_grads[n] = summed[s:s + r, 0:c]
    loss = summed[starts[-1], 0]

    grads, deltas, new_m, new_v = {}, {}, {}, {}
    for n in BIG_NAMES:
        outs = _adamw_big(plain(n, weights[n]), big_grads[n], plain(n, moms[n]), plain(n, vars_[n]), name="adamw_" + n)
        grads[n], deltas[n], new_m[n], new_v[n] = ((jnp.swapaxes(a, 0, 1) if n == "w_in" else a)[None] for a in outs)
    sm_out = _adamw_small([weights[n] for n in SMALL_NAMES], [small_grads[n] for n in SMALL_NAMES],
                          [moms[n] for n in SMALL_NAMES], [vars_[n] for n in SMALL_NAMES])
    ns = len(SMALL_NAMES)
    for i, n in enumerate(SMALL_NAMES):
        grads[n], deltas[n], new_m[n], new_v[n] = small_grads[n], sm_out[i], sm_out[ns + i], sm_out[2 * ns + i]

    return (loss, grad_x, *[grads[n] for n in WEIGHT_ORDER], *[deltas[n] for n in WEIGHT_ORDER],
            *[new_m[n] for n in WEIGHT_ORDER], *[new_v[n] for n in WEIGHT_ORDER])
```

```python
import numpy as np
import jax
import jax.numpy as jnp
from jax import lax
from jax.experimental import pallas as pl
from jax.experimental.pallas import tpu as pltpu

F32 = jnp.float32
_MXU_DTYPE = jnp.bfloat16

EPS = 1e-6
HG_HEADS = 4
HG_D = 128
HG_CHUNK = 64
HG_TILE = 512
HG_LEVELS = (32, 16, 8, 4, 2, 1)
SW_HEADS = 8
SW_KV_HEADS = 2
SW_GROUP = SW_HEADS // SW_KV_HEADS
SW_HD = 64
SW_BLOCK = 128
ROPE_THETA = 500000.0
ROT_DIM = SW_HD // 4
XA_HEADS = 4
XA_HD = 128
HG_COLS = 4 * HG_HEADS * HG_D
SW_COLS = (SW_HEADS + 2 * SW_KV_HEADS) * SW_HD

ADAM_LR = 0.001
ADAM_B1 = 0.9
ADAM_B2 = 0.999
ADAM_EPS = 1e-08
ADAM_WD = 0.01
ADAM_STEP = 10

VMEM_LIMIT = 56 * 1024 * 1024
MESH = pl.DeviceIdType.MESH

NN = ((1,), (0,))
NT = ((1,), (1,))
TN = ((0,), (0,))


def _mx(v):
    return v.astype(_MXU_DTYPE)


def _dot(a, b, dims=NN):
    return lax.dot_general(_mx(a), _mx(b), (dims, ((), ())), preferred_element_type=F32)


def _split_dot(a, v, dims, parts):
    acc = None
    rest = v
    for p in range(parts):
        piece = _mx(rest)
        term = lax.dot_general(a, piece, (dims, ((), ())), preferred_element_type=F32)
        acc = term if acc is None else acc + term
        if p + 1 < parts:
            rest = rest - piece.astype(F32)
    return acc


def _params(sem):
    return pltpu.CompilerParams(dimension_semantics=sem, vmem_limit_bytes=VMEM_LIMIT)


def _mm(a, b, mode, m, n, k, *, name, tm=1024, tn=1024, tk=1024, a_spec=None, b_spec=None, extras=(), rows=(),
        epilogue=None, out_dtypes=(F32,), row_sums=0, out_shape=None, out_spec=None, after=(), into=None):
    after = tuple(t for t in after if t is not None) + (() if into is None else (into,))
    tm, tn, tk = min(tm, m), min(tn, n), min(tk, k)
    assert m % tm == 0 and n % tn == 0 and k % tk == 0, (name, m, n, k, tm, tn, tk)
    gi, gj, gk = m // tm, n // tn, k // tk
    assert row_sums == 0 or gj == 1, name
    if a_spec is None:
        a_spec = (pl.BlockSpec((tk, tm), lambda i, j, kk: (kk, i)) if mode == TN
                  else pl.BlockSpec((tm, tk), lambda i, j, kk: (i, kk)))
    if b_spec is None:
        b_spec = (pl.BlockSpec((tn, tk), lambda i, j, kk: (j, kk)) if mode == NT
                  else pl.BlockSpec((tk, tn), lambda i, j, kk: (kk, j)))
    mn_spec = pl.BlockSpec((tm, tn), lambda i, j, kk: (i, j))
    if epilogue is None:
        epilogue = lambda acc: (acc,)
    row_spec = pl.BlockSpec((1, tn), lambda i, j, kk: (0, j))
    n_ex, n_out = len(extras) + len(rows), len(out_dtypes)
    if out_shape is None:
        out_shape = tuple(jax.ShapeDtypeStruct((m, n), d) for d in out_dtypes)
        out_spec = tuple(mn_spec for _ in out_dtypes)
    out_shape = tuple(out_shape) + tuple(jax.ShapeDtypeStruct((1, n), F32) for _ in range(row_sums))
    out_spec = tuple(out_spec) + tuple(row_spec for _ in range(row_sums))

    n_after = len(after)

    def body(*refs):
        a_ref, b_ref = refs[0], refs[1]
        ex = refs[2:2 + n_ex]
        outs = refs[2 + n_ex + n_after:2 + n_ex + n_after + n_out + row_sums]
        first_row_tile = pl.program_id(0) == 0

        def finish(acc):
            res = epilogue(acc, *[e[...] for e in ex])
            for o, r in zip(outs[:n_out], res[:n_out]):
                o[...] = r.astype(o.dtype)
            if row_sums:
                @pl.when(first_row_tile)
                def _():
                    for o in outs[n_out:]:
                        o[...] = jnp.zeros_like(o)

                for o, r in zip(outs[n_out:], res[n_out:]):
                    o[...] += r

        if gk == 1:
            finish(_dot(a_ref[...], b_ref[...], mode))
        else:
            acc_ref = refs[-1]
            kk = pl.program_id(2)

            @pl.when(kk == 0)
            def _():
                acc_ref[...] = jnp.zeros_like(acc_ref)

            acc_ref[...] += _dot(a_ref[...], b_ref[...], mode)

            @pl.when(kk == gk - 1)
            def _():
                finish(acc_ref[...])

    return pl.pallas_call(
        body, name=name, grid=(gi, gj, gk),
        in_specs=([a_spec, b_spec] + [mn_spec] * len(extras) + [row_spec] * len(rows)
                  + [pl.BlockSpec(memory_space=pl.ANY)] * n_after),
        out_specs=out_spec, out_shape=out_shape,
        input_output_aliases={} if into is None else {2 + n_ex + n_after - 1: 0},
        scratch_shapes=[pltpu.VMEM((tm, tn), F32)] if gk > 1 else [],
        compiler_params=_params(("arbitrary" if row_sums else "parallel", "parallel", "arbitrary")),
    )(a, b, *extras, *rows, *after)


def _rms_rows(xv, g):
    return xv * lax.rsqrt(jnp.mean(xv * xv, axis=1, keepdims=True) + EPS) * g


def _rms_rows_bwd(xv, g, dyv):
    r = lax.rsqrt(jnp.mean(xv * xv, axis=1, keepdims=True) + EPS)
    u = dyv * g
    return (r * u - xv * (r * r * r) * jnp.mean(u * xv, axis=1, keepdims=True),
            jnp.sum(dyv * xv * r, axis=0, keepdims=True))


def _residual_rms(acc, res, g):
    h = acc + res
    return h, _rms_rows(h, g)


def _rms_bwd_residual(dhn, xv, dres, g):
    dx, dg = _rms_rows_bwd(xv, g, dhn)
    dx = dx + dres
    return dx, dx, dg


def _rms_fwd(x, g, *, name, tm=512, after=()):
    t, d = x.shape
    tm = min(tm, t)
    after = tuple(a for a in after if a is not None)

    def body(x_ref, g_ref, *rest):
        rest[-1][...] = _rms_rows(x_ref[...], g_ref[...]).astype(rest[-1].dtype)

    return pl.pallas_call(
        body, name=name, grid=(t // tm,),
        in_specs=[pl.BlockSpec((tm, d), lambda i: (i, 0)), pl.BlockSpec((1, d), lambda i: (0, 0))]
        + [pl.BlockSpec(memory_space=pl.ANY)] * len(after),
        out_specs=pl.BlockSpec((tm, d), lambda i: (i, 0)),
        out_shape=jax.ShapeDtypeStruct((t, d), _MXU_DTYPE),
        compiler_params=_params(("parallel",)),
    )(x, g, *after)


def _rms_gain_grad(x, g, dy, *, name, tm=512):
    t, d = x.shape
    tm = min(tm, t)

    def body(x_ref, g_ref, dy_ref, dg_ref):
        @pl.when(pl.program_id(0) == 0)
        def _():
            dg_ref[...] = jnp.zeros_like(dg_ref)

        dg_ref[...] += _rms_rows_bwd(x_ref[...], g_ref[...], dy_ref[...])[1]

    row = pl.BlockSpec((tm, d), lambda i: (i, 0))
    vec = pl.BlockSpec((1, d), lambda i: (0, 0))
    return pl.pallas_call(
        body, name=name, grid=(t // tm,), in_specs=[row, vec, row], out_specs=vec,
        out_shape=jax.ShapeDtypeStruct((1, d), F32), compiler_params=_params(("arbitrary",)),
    )(x, g, dy)


def _hg_constants():
    c = HG_CHUNK
    t = np.arange(c)
    sums = [t[None, :] <= t[:, None]]
    masks = []
    for m in HG_LEVELS:
        base = (t // (2 * m)) * (2 * m)
        mid = base + m - 1
        second = (t - base) >= m
        upper = (t[None, :] > mid[:, None]) & (t[None, :] <= t[:, None])
        lower = (t[None, :] > t[:, None]) & (t[None, :] <= mid[:, None])
        sums.append(np.where(second[:, None], upper, lower))
        masks.append(second[:, None] & (~second)[None, :] & (base[:, None] == base[None, :]))
    return (np.concatenate(sums, axis=0).astype(np.float32), np.stack(masks).astype(np.float32))


HG_HEAD_LANES = tuple(slice(HG_D * h, HG_D * (h + 1)) for h in range(HG_HEADS))


def _per_head(fn, slab):
    return jnp.concatenate([jnp.broadcast_to(fn(slab[:, hs]), (slab.shape[0], HG_D)) for hs in HG_HEAD_LANES], axis=1)


def _lane_sum(v):
    return jnp.sum(v, axis=1, keepdims=True)


def _lane_mean(v):
    return jnp.mean(v, axis=1, keepdims=True)


def _hg_gates(blk, lbp):
    w = HG_HEADS * HG_D
    q, x, v, gl = blk[:, 0:w], blk[:, w:2 * w], blk[:, 2 * w:3 * w], blk[:, 3 * w:4 * w]
    mx = jnp.max(lbp, axis=0, keepdims=True)
    e = jnp.exp(lbp - mx)
    lb = e[0:1, :] / jnp.sum(e, axis=0, keepdims=True)
    sig = jax.nn.sigmoid(x)
    f = lb + (1.0 - lb) * sig
    return q, v, gl, lb, sig, f, 1.0 - f, jnp.log(f)


def _hg_fwd(proj, lbp, ng, bsz, seq, *, y_width):
    t = proj.shape[0]
    nc = seq // HG_CHUNK
    a_np, m_np = _hg_constants()
    a_all = jnp.asarray(a_np, _MXU_DTYPE)
    masks = jnp.asarray(m_np, F32)
    nl = len(HG_LEVELS)

    ts = min(HG_TILE, seq)
    ns, nct = seq // ts, ts // HG_CHUNK
    hw = HG_HEADS * HG_D

    def body(p_ref, lb_ref, ng_ref, a_ref, m_ref, y_ref, o_ref, st_ref, carry):
        a_mat = a_ref[...]
        ngv = ng_ref[...]

        @pl.when(pl.program_id(0) == 0)
        def _():
            carry[...] = jnp.zeros_like(carry)

        ng4 = _tile_lanes(ngv, HG_HEADS)
        heads = range(HG_HEADS)
        exs = range(bsz)
        hl = HG_HEAD_LANES
        lbp_v = lb_ref[...]

        def chunk(c, _):
            rows = pl.ds(pl.multiple_of(c * HG_CHUNK, HG_CHUNK), HG_CHUNK)
            gates = [_hg_gates(p_ref[e, rows, :], lbp_v) for e in exs]
            q, v, gl = [g[0] for g in gates], [g[1] for g in gates], [g[2] for g in gates]
            k = [g[6] for g in gates]
            sts = [[carry[e, h] for h in heads] for e in exs]
            e_all = [_split_dot(a_mat, gates[e][7], NN, 3) for e in exs]
            b = [e_all[e][0:HG_CHUNK] for e in exs]
            qb = [q[e] * jnp.exp(b[e]) for e in exs]
            o = [[_dot(qb[e][:, hl[h]], sts[e][h], NT) for h in heads] for e in exs]
            p = [[jnp.zeros((HG_CHUNK, HG_CHUNK), F32) for _ in heads] for _ in exs]
            for li in range(nl):
                dec = [jnp.exp(e_all[e][HG_CHUNK * (li + 1):HG_CHUNK * (li + 2)]) for e in exs]
                qm, km, mk = [q[e] * dec[e] for e in exs], [k[e] * dec[e] for e in exs], m_ref[li]
                p = [[p[e][h] + mk * _dot(qm[e][:, hl[h]], km[e][:, hl[h]], NT) for h in heads] for e in exs]
            bl = [b[e][HG_CHUNK - 1:HG_CHUNK, :] for e in exs]
            kd = [k[e] * jnp.exp(bl[e] - b[e]) for e in exs]
            pv = [[_dot(p[e][h], v[e][:, hl[h]]) for h in heads] for e in exs]
            upd = [[_dot(v[e][:, hl[h]], kd[e][:, hl[h]], TN) for h in heads] for e in exs]
            for e in exs:
                o_all = (jnp.concatenate([o[e][h] + pv[e][h] for h in heads], axis=1)
                         + _per_head(_lane_sum, q[e] * k[e]) * v[e])
                r = lax.rsqrt(_per_head(_lane_mean, o_all * o_all) + EPS)
                ebl = jnp.exp(bl[e])
                for h in heads:
                    st_ref[e, h, c] = sts[e][h]
                    carry[e, h] = sts[e][h] * ebl[:, hl[h]] + upd[e][h]
                o_ref[e, rows, :] = o_all
                y_ref[e, rows, :] = (o_all * r * ng4) * (gl[e] * jax.nn.sigmoid(gl[e]))
            return 0

        lax.fori_loop(0, nct, chunk, 0)

    y3, o3, states = pl.pallas_call(
        body, name="hgrn2_fwd", grid=(ns,),
        in_specs=[pl.BlockSpec((bsz, ts, HG_COLS), lambda s: (0, s, 0)),
                  pl.BlockSpec((2, hw), lambda s: (0, 0)),
                  pl.BlockSpec((1, HG_D), lambda s: (0, 0)),
                  pl.BlockSpec(a_all.shape, lambda s: (0, 0)),
                  pl.BlockSpec(masks.shape, lambda s: (0, 0, 0))],
        out_specs=(pl.BlockSpec((bsz, ts, hw), lambda s: (0, s, 0)),
                   pl.BlockSpec((bsz, ts, hw), lambda s: (0, s, 0)),
                   pl.BlockSpec((bsz, HG_HEADS, nct, HG_D, HG_D), lambda s: (0, 0, s, 0, 0))),
        out_shape=(jax.ShapeDtypeStruct((bsz, seq, y_width), F32),
                   jax.ShapeDtypeStruct((bsz, seq, hw), F32),
                   jax.ShapeDtypeStruct((bsz, HG_HEADS, nc, HG_D, HG_D), F32)),
        scratch_shapes=[pltpu.VMEM((bsz, HG_HEADS, HG_D, HG_D), F32)],
        compiler_params=_params(("arbitrary",)),
    )(proj.reshape(bsz, seq, HG_COLS), lbp, ng, a_all, masks)
    return y3.reshape(t, y_width), o3.reshape(t, hw), states


def _hg_bwd(proj, lbp, ng, o_all, states, dy, bsz, seq, after=()):
    after = tuple(a for a in after if a is not None)
    t = proj.shape[0]
    nc = seq // HG_CHUNK
    a_np, m_np = _hg_constants()
    a_all = jnp.asarray(a_np, _MXU_DTYPE)
    masks = jnp.asarray(m_np, F32)
    nl = len(HG_LEVELS)
    cs = HG_CHUNK

    ts = min(HG_TILE, seq)
    ns, nct = seq // ts, ts // cs
    hw = HG_HEADS * HG_D

    def body(p_ref, lb_ref, ng_ref, a_ref, m_ref, o_ref, st_ref, dy_ref, *rest):
        dp_ref, dlb_ref, dng_ref, dst_ref = rest[len(after):]
        a_mat = a_ref[...]
        ngv = ng_ref[...]
        ng4 = _tile_lanes(ngv, HG_HEADS)
        last_row = lax.broadcasted_iota(jnp.int32, (cs, hw), 0) == cs - 1
        first = pl.program_id(0) == 0
        heads = range(HG_HEADS)
        exs = range(bsz)
        hl = HG_HEAD_LANES
        lbp_v = lb_ref[...]

        @pl.when(first)
        def _():
            dst_ref[...] = jnp.zeros_like(dst_ref)

        def side_by_side(parts):
            return jnp.concatenate(parts, axis=1)

        def chunk(i, carry):
            dlb_acc, dng_acc = carry
            c = nct - 1 - i
            rows = pl.ds(pl.multiple_of(c * cs, cs), cs)
            gates = [_hg_gates(p_ref[e, rows, :], lbp_v) for e in exs]
            q, v, gl = [g[0] for g in gates], [g[1] for g in gates], [g[2] for g in gates]
            lb, sig, f, k = gates[0][3], [g[4] for g in gates], [g[5] for g in gates], [g[6] for g in gates]
            o = [o_ref[e, rows, :] for e in exs]
            dyv = [dy_ref[e, rows, :] for e in exs]
            sts = [[st_ref[e, h, c] for h in heads] for e in exs]
            dsts = [[dst_ref[e, h] for h in heads] for e in exs]
            e_all = [_split_dot(a_mat, gates[e][7], NN, 3) for e in exs]
            b = [e_all[e][0:cs] for e in exs]
            eb = [jnp.exp(b[e]) for e in exs]
            bl = [b[e][cs - 1:cs, :] for e in exs]
            ebl = [jnp.exp(bl[e]) for e in exs]
            ekd = [jnp.exp(bl[e] - b[e]) for e in exs]
            qb = [q[e] * eb[e] for e in exs]
            kd = [k[e] * ekd[e] for e in exs]
            do, dgl = [], []
            for e in exs:
                sg = jax.nn.sigmoid(gl[e])
                silu = gl[e] * sg
                r = lax.rsqrt(_per_head(_lane_mean, o[e] * o[e]) + EPS)
                dgl.append(dyv[e] * (o[e] * r * ng4) * (sg * (1.0 + gl[e] * (1.0 - sg))))
                u = dyv[e] * silu * ng4
                do.append(r * u - o[e] * (r * r * r) * _per_head(_lane_mean, u * o[e]))
                dng4 = jnp.sum(dyv[e] * silu * o[e] * r, axis=0, keepdims=True)
                dng_acc = dng_acc + ((dng4[:, hl[0]] + dng4[:, hl[1]]) + (dng4[:, hl[2]] + dng4[:, hl[3]]))
            es, qm, km = [], [], []
            p = [[jnp.zeros((cs, cs), F32) for _ in heads] for _ in exs]
            for li in range(nl):
                dec = [jnp.exp(e_all[e][cs * (li + 1):cs * (li + 2)]) for e in exs]
                es.append(dec)
                qm.append([q[e] * dec[e] for e in exs])
                km.append([k[e] * dec[e] for e in exs])
                mk = m_ref[li]
                p = [[p[e][h] + mk * _dot(qm[li][e][:, hl[h]], km[li][e][:, hl[h]], NT) for h in heads] for e in exs]
            dp = [[_dot(do[e][:, hl[h]], v[e][:, hl[h]], NT) for h in heads] for e in exs]
            dv_p = [[_dot(p[e][h], do[e][:, hl[h]], TN) for h in heads] for e in exs]
            dv_s = [[_dot(kd[e][:, hl[h]], dsts[e][h], NT) for h in heads] for e in exs]
            dqb = [side_by_side([_dot(do[e][:, hl[h]], sts[e][h]) for h in heads]) for e in exs]
            dkd = [side_by_side([_dot(v[e][:, hl[h]], dsts[e][h]) for h in heads]) for e in exs]
            new_dst = [[_dot(do[e][:, hl[h]], qb[e][:, hl[h]], TN) for h in heads] for e in exs]
            dv = [side_by_side([dv_p[e][h] + dv_s[e][h] for h in heads]) + _per_head(_lane_sum, q[e] * k[e]) * do[e]
                  for e in exs]
            dq = [dqb[e] * eb[e] for e in exs]
            dk = [dkd[e] * ekd[e] for e in exs]
            de = []
            for e in exs:
                dbl = (jnp.sum(dkd[e] * kd[e], axis=0, keepdims=True)
                       + side_by_side([jnp.sum(dsts[e][h] * sts[e][h], axis=0, keepdims=True) for h in heads]) * ebl[e])
                de.append([dqb[e] * qb[e] - dkd[e] * kd[e] + jnp.where(last_row, dbl, 0.0)])
            for li in range(nl):
                mk = m_ref[li]
                dpm = [[mk * dp[e][h] for h in heads] for e in exs]
                dqm = [side_by_side([_dot(dpm[e][h], km[li][e][:, hl[h]]) for h in heads]) for e in exs]
                dkm = [side_by_side([_dot(dpm[e][h], qm[li][e][:, hl[h]], TN) for h in heads]) for e in exs]
                for e in exs:
                    dq[e] = dq[e] + dqm[e] * es[li][e]
                    dk[e] = dk[e] + dkm[e] * es[li][e]
                    de[e].append(dqm[e] * qm[li][e] + dkm[e] * km[li][e])
            dg = [_split_dot(a_mat, jnp.concatenate(de[e], axis=0), TN, 2) for e in exs]
            for e in exs:
                dpd = _per_head(_lane_sum, do[e] * v[e])
                df = dg[e] / f[e] - (dk[e] + dpd * q[e])
                dp_ref[e, rows, 0:hw] = _mx(dq[e] + dpd * k[e])
                dp_ref[e, rows, hw:2 * hw] = _mx(df * (1.0 - lb) * sig[e] * (1.0 - sig[e]))
                dp_ref[e, rows, 2 * hw:3 * hw] = _mx(dv[e])
                dp_ref[e, rows, 3 * hw:4 * hw] = _mx(dgl[e])
                for h in heads:
                    dst_ref[e, h] = dsts[e][h] * ebl[e][:, hl[h]] + new_dst[e][h]
                dlb_acc = dlb_acc + jnp.sum(df * (1.0 - sig[e]), axis=0, keepdims=True)
            return dlb_acc, dng_acc

        dlb, dng = lax.fori_loop(0, nct, chunk, (jnp.zeros((1, hw), F32), jnp.zeros((1, HG_D), F32)))

        @pl.when(first)
        def _():
            dlb_ref[...] = jnp.zeros_like(dlb_ref)
            dng_ref[...] = jnp.zeros_like(dng_ref)

        mx = jnp.max(lbp_v, axis=0, keepdims=True)
        e = jnp.exp(lbp_v - mx)
        s0 = e[0:1, :] / jnp.sum(e, axis=0, keepdims=True)
        da0 = dlb * s0 * (1.0 - s0)
        dlb_ref[...] += jnp.concatenate([da0, -da0], axis=0)
        dng_ref[...] += dng

    rows3 = lambda w: pl.BlockSpec((bsz, ts, w), lambda s: (0, ns - 1 - s, 0))
    dproj, dlb, dng = pl.pallas_call(
        body, name="hgrn2_bwd", grid=(ns,),
        in_specs=[rows3(HG_COLS),
                  pl.BlockSpec((2, hw), lambda s: (0, 0)),
                  pl.BlockSpec((1, HG_D), lambda s: (0, 0)),
                  pl.BlockSpec(a_all.shape, lambda s: (0, 0)),
                  pl.BlockSpec(masks.shape, lambda s: (0, 0, 0)),
                  rows3(hw),
                  pl.BlockSpec((bsz, HG_HEADS, nct, HG_D, HG_D), lambda s: (0, 0, ns - 1 - s, 0, 0)),
                  rows3(hw)] + [pl.BlockSpec(memory_space=pl.ANY)] * len(after),
        out_specs=(rows3(HG_COLS),
                   pl.BlockSpec((2, hw), lambda s: (0, 0)),
                   pl.BlockSpec((1, HG_D), lambda s: (0, 0))),
        out_shape=(jax.ShapeDtypeStruct((bsz, seq, HG_COLS), _MXU_DTYPE),
                   jax.ShapeDtypeStruct((2, hw), F32),
                   jax.ShapeDtypeStruct((1, HG_D), F32)),
        scratch_shapes=[pltpu.VMEM((bsz, HG_HEADS, HG_D, HG_D), F32)],
        compiler_params=_params(("arbitrary",)),
    )(proj.reshape(bsz, seq, HG_COLS), lbp, ng, a_all, masks, o_all.reshape(bsz, seq, hw), states,
      dy.reshape(bsz, seq, dy.shape[1]), *after)
    return dproj.reshape(t, HG_COLS), dlb, dng


def _sw_constants():
    half = ROT_DIM // 2
    inv = (np.float32(ROPE_THETA) ** (-(np.arange(half, dtype=np.float32) * np.float32(2.0) / np.float32(ROT_DIM)))
           ).astype(np.float32)
    freq = np.zeros((1, 128), np.float32)
    sign = np.zeros((1, 128), np.float32)
    for h in range(2):
        freq[0, 64 * h:64 * h + half] = inv
        freq[0, 64 * h + half:64 * h + 2 * half] = inv
        sign[0, 64 * h:64 * h + half] = -1.0
        sign[0, 64 * h + half:64 * h + 2 * half] = 1.0
    seg = np.kron(np.eye(8, dtype=np.float32), np.full((64, 64), 1.0 / 64.0, np.float32))
    return freq, sign, seg


def _rope_table(pos, *, tm=512, after=()):
    t = pos.shape[0]
    tm = min(tm, t)
    freq_np, sign_np, _ = _sw_constants()
    after = tuple(a for a in after if a is not None)

    def body(p_ref, f_ref, s_ref, *rest):
        o_ref = rest[-1]
        ang = p_ref[...].astype(F32) * f_ref[...]
        o_ref[:, 0:128] = jnp.cos(ang)
        o_ref[:, 128:256] = jnp.sin(ang) * s_ref[...]

    vec = pl.BlockSpec((1, 128), lambda i: (0, 0))
    return pl.pallas_call(
        body, name="rope_table", grid=(t // tm,),
        in_specs=[pl.BlockSpec((tm, 1), lambda i: (i, 0)), vec, vec] + [pl.BlockSpec(memory_space=pl.ANY)] * len(after),
        out_specs=pl.BlockSpec((tm, 256), lambda i: (i, 0)),
        out_shape=jax.ShapeDtypeStruct((t, 256), F32),
        compiler_params=_params(("parallel",)),
    )(pos, jnp.asarray(freq_np), jnp.asarray(sign_np), *after)


def _tile_lanes(v, times):
    return v if times == 1 else jnp.concatenate([v] * times, axis=1)


def _swap_halves(v):
    w = v.shape[1]
    half = ROT_DIM // 2
    lane = lax.broadcasted_iota(jnp.int32, v.shape, 1) % SW_HD
    return jnp.where(lane < half, pltpu.roll(v, w - half, 1), jnp.where(lane < 2 * half, pltpu.roll(v, half, 1), 0.0))


def _sw_norm_rope(tv, gain, seg, cosv, sinv):
    w = tv.shape[1]
    ms = _split_dot_rhs(tv * tv, seg[0:w, 0:w])
    r = lax.rsqrt(ms + EPS)
    tn = tv * r * gain
    reps = w // 128
    return tn * _tile_lanes(cosv, reps) + _swap_halves(tn) * _tile_lanes(sinv, reps), r


def _split_dot_rhs(v, a):
    hi = _mx(v)
    lo = _mx(v - hi.astype(F32))
    return (lax.dot_general(hi, a, (NN, ((), ())), preferred_element_type=F32)
            + lax.dot_general(lo, a, (NN, ((), ())), preferred_element_type=F32))


def _sw_norm_rope_bwd(dt, tv, r, gain, seg, cosv, sinv):
    w = tv.shape[1]
    reps = w // 128
    dtn = dt * _tile_lanes(cosv, reps) + _swap_halves(dt * _tile_lanes(sinv, reps))
    u = dtn * gain
    dtv = r * u - tv * (r * r * r) * _split_dot_rhs(u * tv, seg[0:w, 0:w])
    return dtv, jnp.sum(dtn * tv * r, axis=0, keepdims=True)


def _sw_scores(qh, kp, kc):
    return _dot(qh, kp, NT), _dot(qh, kc, NT)


SW_SCALE = SW_HD ** -0.5


def _sw_probs(raw, sink, first_block):
    qi = lax.broadcasted_iota(jnp.int32, (SW_BLOCK, SW_BLOCK), 0)
    kj = lax.broadcasted_iota(jnp.int32, (SW_BLOCK, SW_BLOCK), 1)
    ok_prev = jnp.logical_and(kj > qi, jnp.logical_not(first_block))
    ok_cur = kj <= qi
    sp = jnp.where(ok_prev, raw[0], -jnp.inf)
    sc = jnp.where(ok_cur, raw[1], -jnp.inf)
    m = jnp.maximum(jnp.maximum(jnp.max(sp, axis=1, keepdims=True), jnp.max(sc, axis=1, keepdims=True)), sink)
    pp, pc = jnp.exp(sp - m), jnp.exp(sc - m)
    es = jnp.exp(sink - m)
    inv = 1.0 / (jnp.sum(pp, axis=1, keepdims=True) + jnp.sum(pc, axis=1, keepdims=True) + es)
    return pp * inv, pc * inv, es * inv


def _sw_specs(nb):
    def cur(b, n):
        return b * nb + jnp.minimum(n, nb - 1)

    def prev(b, n):
        return b * nb + jnp.maximum(jnp.minimum(n, nb - 1) - 1, 0)

    return cur, prev


def _sw_fwd(proj, rope, qg, kg, sinks, y_in, bsz, seq):
    t = proj.shape[0]
    nb = seq // SW_BLOCK
    seg = jnp.asarray(_sw_constants()[2], _MXU_DTYPE)
    cur, prev = _sw_specs(nb)

    def body(q_ref, kc_ref, kp_ref, vc_ref, vp_ref, rc_ref, rp_ref, qg_ref, kg_ref, sk_ref, seg_ref, yin_ref, y_ref):
        del yin_ref
        n = pl.program_id(1)
        segv = seg_ref[...]
        cos_c, sin_c = rc_ref[:, 0:128], rc_ref[:, 128:256]
        cos_p, sin_p = rp_ref[:, 0:128], rp_ref[:, 128:256]
        qr, _ = _sw_norm_rope(q_ref[...], qg_ref[...] * SW_SCALE, segv, cos_c, sin_c)
        kcr, _ = _sw_norm_rope(kc_ref[...], kg_ref[...], segv, cos_c, sin_c)
        kpr, _ = _sw_norm_rope(kp_ref[...], kg_ref[...], segv, cos_p, sin_p)
        vc, vp = vc_ref[...], vp_ref[...]
        ks = [slice(SW_HD * (h // SW_GROUP), SW_HD * (h // SW_GROUP + 1)) for h in range(SW_HEADS)]
        raw = [_sw_scores(qr[:, SW_HD * h:SW_HD * (h + 1)], kpr[:, ks[h]], kcr[:, ks[h]]) for h in range(SW_HEADS)]
        probs = [_sw_probs(raw[h], sk_ref[0, h], n == 0) for h in range(SW_HEADS)]
        for h in range(SW_HEADS):
            y_ref[:, SW_HD * h:SW_HD * (h + 1)] = _dot(probs[h][0], vp[:, ks[h]]) + _dot(probs[h][1], vc[:, ks[h]])

    rowq = pl.BlockSpec((SW_BLOCK, 512), lambda b, n: (cur(b, n), 0))
    full = lambda a: pl.BlockSpec(a.shape, lambda b, n: (0,) * a.ndim)
    yw = y_in.shape[1]
    return pl.pallas_call(
        body, name="swa_fwd", grid=(bsz, nb),
        in_specs=[rowq,
                  pl.BlockSpec((SW_BLOCK, 128), lambda b, n: (cur(b, n), 4)),
                  pl.BlockSpec((SW_BLOCK, 128), lambda b, n: (prev(b, n), 4)),
                  pl.BlockSpec((SW_BLOCK, 128), lambda b, n: (cur(b, n), 5)),
                  pl.BlockSpec((SW_BLOCK, 128), lambda b, n: (prev(b, n), 5)),
                  pl.BlockSpec((SW_BLOCK, 256), lambda b, n: (cur(b, n), 0)),
                  pl.BlockSpec((SW_BLOCK, 256), lambda b, n: (prev(b, n), 0)),
                  full(qg), full(kg),
                  pl.BlockSpec(memory_space=pltpu.SMEM),
                  full(seg),
                  pl.BlockSpec(memory_space=pl.ANY)],
        out_specs=pl.BlockSpec((SW_BLOCK, 512), lambda b, n: (cur(b, n), 1)),
        out_shape=jax.ShapeDtypeStruct((t, yw), F32),
        input_output_aliases={11: 0},
        compiler_params=_params(("parallel", "parallel")),
    )(proj, proj, proj, proj, proj, rope, rope, qg, kg, sinks, seg, y_in)


def _sw_bwd(proj, rope, qg, kg, sinks, y, dy, bsz, seq):
    t = proj.shape[0]
    nb = seq // SW_BLOCK
    seg = jnp.asarray(_sw_constants()[2], _MXU_DTYPE)
    cur, prev = _sw_specs(nb)

    def body(q_ref, kc_ref, kp_ref, vc_ref, vp_ref, rc_ref, rp_ref, qg_ref, kg_ref, sk_ref, seg_ref,
             y_ref, dy_ref, dp_ref, dqg_ref, dkg_ref, dsk_ref,
             dq_car, dkv_car, dqr_s, dkc_s, dkp_s, dvc_s, dvp_s, gq_acc, gk_acc, sk_acc):
        b, n = pl.program_id(0), pl.program_id(1)
        first = jnp.logical_and(b == 0, n == 0)
        last = jnp.logical_and(b == pl.num_programs(0) - 1, n == nb)

        @pl.when(first)
        def _():
            gq_acc[...] = jnp.zeros_like(gq_acc)
            gk_acc[...] = jnp.zeros_like(gk_acc)
            sk_acc[...] = jnp.zeros_like(sk_acc)

        @pl.when(n < nb)
        def _():
            segv = seg_ref[...]
            cos_c, sin_c = rc_ref[:, 0:128], rc_ref[:, 128:256]
            cos_p, sin_p = rp_ref[:, 0:128], rp_ref[:, 128:256]
            qv, kcv, kpv = q_ref[...], kc_ref[...], kp_ref[...]
            qgain = qg_ref[...] * SW_SCALE
            qr, rq = _sw_norm_rope(qv, qgain, segv, cos_c, sin_c)
            kcr, rkc = _sw_norm_rope(kcv, kg_ref[...], segv, cos_c, sin_c)
            kpr, rkp = _sw_norm_rope(kpv, kg_ref[...], segv, cos_p, sin_p)
            vc, vp = vc_ref[...], vp_ref[...]
            lane = lax.broadcasted_iota(jnp.int32, (1, 128), 1)
            dsk = jnp.zeros((1, 128), F32)
            heads = range(SW_HEADS)
            ks = [slice(SW_HD * (h // SW_GROUP), SW_HD * (h // SW_GROUP + 1)) for h in heads]
            hs = [slice(SW_HD * h, SW_HD * (h + 1)) for h in heads]
            qh = [qr[:, hs[h]] for h in heads]
            doh = [dy_ref[:, hs[h]] for h in heads]
            raw = [_sw_scores(qh[h], kpr[:, ks[h]], kcr[:, ks[h]]) for h in heads]
            dpp = [_dot(doh[h], vp[:, ks[h]], NT) for h in heads]
            dpc = [_dot(doh[h], vc[:, ks[h]], NT) for h in heads]
            probs = [_sw_probs(raw[h], sk_ref[0, h], n == 0) for h in heads]
            dsp, dsc = [], []
            for h in heads:
                pp, pc, ps = probs[h]
                delta = jnp.sum(doh[h] * y_ref[:, hs[h]], axis=1, keepdims=True)
                dsp.append(pp * (dpp[h] - delta))
                dsc.append(pc * (dpc[h] - delta))
                dsk = dsk + jnp.where(lane == h, -jnp.sum(ps * delta), 0.0)
            for h in heads:
                dqr_s[:, hs[h]] = _dot(dsp[h], kpr[:, ks[h]]) + _dot(dsc[h], kcr[:, ks[h]])
            for kv in range(SW_KV_HEADS):
                group = range(SW_GROUP * kv, SW_GROUP * (kv + 1))
                kvs = slice(SW_HD * kv, SW_HD * (kv + 1))
                dvp_s[:, kvs] = sum(_dot(probs[h][0], doh[h], TN) for h in group)
                dvc_s[:, kvs] = sum(_dot(probs[h][1], doh[h], TN) for h in group)
                dkp_s[:, kvs] = sum(_dot(dsp[h], qh[h], TN) for h in group)
                dkc_s[:, kvs] = sum(_dot(dsc[h], qh[h], TN) for h in group)
            dq, gq = _sw_norm_rope_bwd(dqr_s[...], qv, rq, qgain, segv, cos_c, sin_c)
            dkc, gkc = _sw_norm_rope_bwd(dkc_s[...], kcv, rkc, kg_ref[...], segv, cos_c, sin_c)
            dkp, gkp = _sw_norm_rope_bwd(dkp_s[...], kpv, rkp, kg_ref[...], segv, cos_p, sin_p)
            gq_acc[...] += gq
            gk_acc[...] += gkc + gkp
            sk_acc[...] += dsk

            @pl.when(n > 0)
            def _():
                dp_ref[:, 0:512] = _mx(dq_car[...])
                dp_ref[:, 512:640] = _mx(dkv_car[:, 0:128] + dkp)
                dp_ref[:, 640:768] = _mx(dkv_car[:, 128:256] + dvp_s[...])

            dq_car[...] = dq
            dkv_car[:, 0:128] = dkc
            dkv_car[:, 128:256] = dvc_s[...]

        @pl.when(n == nb)
        def _():
            dp_ref[:, 0:512] = _mx(dq_car[...])
            dp_ref[:, 512:768] = _mx(dkv_car[...])

        @pl.when(last)
        def _():
            gq = gq_acc[...] * SW_SCALE
            acc = gq[:, 0:SW_HD]
            for h in range(1, SW_HEADS):
                acc = acc + gq[:, SW_HD * h:SW_HD * (h + 1)]
            dqg_ref[...] = acc
            gk = gk_acc[...]
            dkg_ref[...] = gk[:, 0:SW_HD] + gk[:, SW_HD:2 * SW_HD]
            dsk_ref[...] = sk_acc[...]

    rowq = pl.BlockSpec((SW_BLOCK, 512), lambda b, n: (cur(b, n), 0))
    full = lambda a: pl.BlockSpec(a.shape, lambda b, n: (0,) * a.ndim)

    def out_row(b, n):
        return b * nb + jnp.maximum(n - 1, 0)

    return pl.pallas_call(
        body, name="swa_bwd", grid=(bsz, nb + 1),
        in_specs=[rowq,
                  pl.BlockSpec((SW_BLOCK, 128), lambda b, n: (cur(b, n), 4)),
                  pl.BlockSpec((SW_BLOCK, 128), lambda b, n: (prev(b, n), 4)),
                  pl.BlockSpec((SW_BLOCK, 128), lambda b, n: (cur(b, n), 5)),
                  pl.BlockSpec((SW_BLOCK, 128), lambda b, n: (prev(b, n), 5)),
                  pl.BlockSpec((SW_BLOCK, 256), lambda b, n: (cur(b, n), 0)),
                  pl.BlockSpec((SW_BLOCK, 256), lambda b, n: (prev(b, n), 0)),
                  full(qg), full(kg),
                  pl.BlockSpec(memory_space=pltpu.SMEM),
                  full(seg),
                  pl.BlockSpec((SW_BLOCK, 512), lambda b, n: (cur(b, n), 1)),
                  pl.BlockSpec((SW_BLOCK, 512), lambda b, n: (cur(b, n), 1))],
        out_specs=(pl.BlockSpec((SW_BLOCK, SW_COLS), lambda b, n: (out_row(b, n), 0)),
                   pl.BlockSpec((1, SW_HD), lambda b, n: (0, 0)),
                   pl.BlockSpec((1, SW_HD), lambda b, n: (0, 0)),
                   pl.BlockSpec((1, 128), lambda b, n: (0, 0))),
        out_shape=(jax.ShapeDtypeStruct((t, SW_COLS), _MXU_DTYPE),
                   jax.ShapeDtypeStruct((1, SW_HD), F32),
                   jax.ShapeDtypeStruct((1, SW_HD), F32),
                   jax.ShapeDtypeStruct((1, 128), F32)),
        scratch_shapes=[pltpu.VMEM((SW_BLOCK, 512), F32), pltpu.VMEM((SW_BLOCK, 256), F32),
                        pltpu.VMEM((SW_BLOCK, 512), F32),
                        pltpu.VMEM((SW_BLOCK, 128), F32), pltpu.VMEM((SW_BLOCK, 128), F32),
                        pltpu.VMEM((SW_BLOCK, 128), F32), pltpu.VMEM((SW_BLOCK, 128), F32),
                        pltpu.VMEM((1, 512), F32), pltpu.VMEM((1, 128), F32), pltpu.VMEM((1, 128), F32)],
        compiler_params=_params(("arbitrary", "arbitrary")),
    )(proj, proj, proj, proj, proj, rope, rope, qg, kg, sinks, seg, y, dy)


def _head_rms(tv, gain):
    r = lax.rsqrt(jnp.mean(tv * tv, axis=1, keepdims=True) + EPS)
    return tv * r * gain, r


def _head_rms_bwd(dtn, tv, r, gain):
    u = dtn * gain
    return r * u - tv * (r * r * r) * jnp.mean(u * tv, axis=1, keepdims=True), jnp.sum(dtn * tv * r, axis=0, keepdims=True)


def _xa_softmax(raw):
    s = raw * (XA_HD ** -0.5)
    e = jnp.exp(s - jnp.max(s, axis=1, keepdims=True))
    return e * (1.0 / jnp.sum(e, axis=1, keepdims=True))


def _xa_fwd(qx, kvx, qg, kg, bsz, seq, mlen, *, tq=512):
    t = qx.shape[0]
    tq = min(tq, seq)
    nq = seq // tq
    w = XA_HEADS * XA_HD

    def body(q_ref, kv_ref, qg_ref, kg_ref, o_ref):
        heads = range(XA_HEADS)
        hs = [slice(XA_HD * h, XA_HD * (h + 1)) for h in heads]
        qn = [_head_rms(q_ref[:, hs[h]], qg_ref[...])[0] for h in heads]
        kn = [_head_rms(kv_ref[:, hs[h]], kg_ref[...])[0] for h in heads]
        raw = [_dot(qn[h], kn[h], NT) for h in heads]
        p = [_xa_softmax(raw[h]) for h in heads]
        for h in heads:
            o_ref[:, hs[h]] = _dot(p[h], kv_ref[:, w + XA_HD * h:w + XA_HD * (h + 1)]).astype(o_ref.dtype)

    vec = pl.BlockSpec((1, XA_HD), lambda b, i: (0, 0))
    return pl.pallas_call(
        body, name="xattn_fwd", grid=(bsz, nq),
        in_specs=[pl.BlockSpec((tq, w), lambda b, i: (b * nq + i, 0)),
                  pl.BlockSpec((mlen, 2 * w), lambda b, i: (b, 0)), vec, vec],
        out_specs=pl.BlockSpec((tq, w), lambda b, i: (b * nq + i, 0)),
        out_shape=jax.ShapeDtypeStruct((t, w), _MXU_DTYPE),
        compiler_params=_params(("parallel", "parallel")),
    )(qx, kvx, qg, kg)


def _xa_bwd(qx, kvx, qg, kg, do, bsz, seq, mlen, *, tq=512):
    t = qx.shape[0]
    tq = min(tq, seq)
    nq = seq // tq
    w = XA_HEADS * XA_HD
    scale = XA_HD ** -0.5

    def body(q_ref, kv_ref, qg_ref, kg_ref, do_ref, dq_ref, dkv_ref, dqg_ref, dkg_ref):
        b, i = pl.program_id(0), pl.program_id(1)

        @pl.when(jnp.logical_and(b == 0, i == 0))
        def _():
            dqg_ref[...] = jnp.zeros_like(dqg_ref)
            dkg_ref[...] = jnp.zeros_like(dkg_ref)

        @pl.when(i == 0)
        def _():
            dkv_ref[...] = jnp.zeros_like(dkv_ref)

        heads = range(XA_HEADS)
        hs = [slice(XA_HD * h, XA_HD * (h + 1)) for h in heads]
        vs = [slice(w + XA_HD * h, w + XA_HD * (h + 1)) for h in heads]
        qv = [q_ref[:, hs[h]] for h in heads]
        kv = [kv_ref[:, hs[h]] for h in heads]
        doh = [do_ref[:, hs[h]] for h in heads]
        qn = [_head_rms(qv[h], qg_ref[...]) for h in heads]
        kn = [_head_rms(kv[h], kg_ref[...]) for h in heads]
        raw = [_dot(qn[h][0], kn[h][0], NT) for h in heads]
        dp = [_dot(doh[h], kv_ref[:, vs[h]], NT) for h in heads]
        p = [_xa_softmax(raw[h]) for h in heads]
        ds = [p[h] * (dp[h] - jnp.sum(p[h] * dp[h], axis=1, keepdims=True)) * scale for h in heads]
        dqn = [_dot(ds[h], kn[h][0]) for h in heads]
        dkn = [_dot(ds[h], qn[h][0], TN) for h in heads]
        dvv = [_dot(p[h], doh[h], TN) for h in heads]
        gq_sum = jnp.zeros((1, XA_HD), F32)
        gk_sum = jnp.zeros((1, XA_HD), F32)
        for h in heads:
            dqv, gq = _head_rms_bwd(dqn[h], qv[h], qn[h][1], qg_ref[...])
            dkv, gk = _head_rms_bwd(dkn[h], kv[h], kn[h][1], kg_ref[...])
            dq_ref[:, hs[h]] = dqv.astype(dq_ref.dtype)
            dkv_ref[:, hs[h]] += dkv
            dkv_ref[:, vs[h]] += dvv[h]
            gq_sum = gq_sum + gq
            gk_sum = gk_sum + gk
        dqg_ref[...] += gq_sum
        dkg_ref[...] += gk_sum

    vec = pl.BlockSpec((1, XA_HD), lambda b, i: (0, 0))
    row = pl.BlockSpec((tq, w), lambda b, i: (b * nq + i, 0))
    mem = pl.BlockSpec((mlen, 2 * w), lambda b, i: (b, 0))
    return pl.pallas_call(
        body, name="xattn_bwd", grid=(bsz, nq),
        in_specs=[row, mem, vec, vec, row],
        out_specs=(row, mem, vec, vec),
        out_shape=(jax.ShapeDtypeStruct((t, w), _MXU_DTYPE), jax.ShapeDtypeStruct((bsz * mlen, 2 * w), F32),
                   jax.ShapeDtypeStruct((1, XA_HD), F32), jax.ShapeDtypeStruct((1, XA_HD), F32)),
        compiler_params=_params(("arbitrary", "arbitrary")),
    )(qx, kvx, qg, kg, do)


def _loss_finish(sq_row, d_model):
    def body(s_ref, o_ref):
        o_ref[...] = jnp.zeros_like(o_ref) + 0.5 * jnp.sum(s_ref[...]) / float(d_model)

    return pl.pallas_call(body, name="loss_finish", out_shape=jax.ShapeDtypeStruct((1, 128), F32))(sq_row)


def _adamw_math(w, g, m, v):
    m = ADAM_B1 * m + (1.0 - ADAM_B1) * g
    v = ADAM_B2 * v + (1.0 - ADAM_B2) * (g * g)
    m_hat = m / (1.0 - ADAM_B1 ** ADAM_STEP)
    v_hat = v / (1.0 - ADAM_B2 ** ADAM_STEP)
    return -ADAM_LR * (m_hat / (jnp.sqrt(v_hat) + ADAM_EPS) + ADAM_WD * w), m, v


def _adamw_big(w, g, m, v, *, name, tr=512):
    r, c = w.shape
    tr = min(tr, r)
    if r % tr:
        tr = r // 2
    assert r % tr == 0 and tr % 8 == 0, (name, r, tr)

    def body(w_ref, g_ref, m_ref, v_ref, go_ref, d_ref, mo_ref, vo_ref):
        gv = g_ref[...]
        d, mn, vn = _adamw_math(w_ref[...], gv, m_ref[...], v_ref[...])
        go_ref[...] = gv
        d_ref[...] = d
        mo_ref[...] = mn
        vo_ref[...] = vn

    spec = pl.BlockSpec((tr, c), lambda i: (i, 0))
    shp = jax.ShapeDtypeStruct((r, c), F32)
    return pl.pallas_call(
        body, name=name, grid=(r // tr,), in_specs=[spec] * 4, out_specs=(spec,) * 4, out_shape=(shp,) * 4,
        compiler_params=_params(("parallel",)),
    )(w, g, m, v)


def _adamw_small(ws, gs, ms, vs):
    n = len(ws)

    def body(*refs):
        for i in range(n):
            d, mn, vn = _adamw_math(refs[i][...], refs[n + i][...], refs[2 * n + i][...], refs[3 * n + i][...])
            refs[4 * n + i][...] = d
            refs[5 * n + i][...] = mn
            refs[6 * n + i][...] = vn

    shapes = tuple(jax.ShapeDtypeStruct(w.shape, F32) for w in ws)
    return pl.pallas_call(body, name="adamw_small", out_shape=shapes * 3)(*ws, *gs, *ms, *vs)


def _add_halves(g, recv, c_idx, *, name, tr=512):
    _, r, c = g.shape
    h = r // 2
    tr = min(tr, h)
    nt = h // tr

    def body(c_ref, g_ref, r_ref, o_ref):
        del c_ref
        o_ref[...] = g_ref[...] + r_ref[...]

    return pl.pallas_call(
        body, name=name,
        grid_spec=pltpu.PrefetchScalarGridSpec(
            num_scalar_prefetch=1, grid=(4, nt),
            in_specs=[pl.BlockSpec((None, tr, c), lambda k, i, cr: (k, cr[0] * nt + i, 0)),
                      pl.BlockSpec((None, tr, c), lambda k, i, cr: (k, i, 0))],
            out_specs=pl.BlockSpec((None, tr, c), lambda k, i, cr: (k, i, 0))),
        out_shape=jax.ShapeDtypeStruct((4, h, c), F32),
        compiler_params=_params(("parallel", "parallel")),
    )(c_idx, g, recv)


def _add_chips(p, recv, place_idx, *, name, tr=512, after=()):
    _, h, c = p.shape
    tr = min(tr, h)
    nt = h // tr

    def body(pi_ref, p_ref, r_ref, *rest):
        del pi_ref
        rest[-1][...] = ((p_ref[...] + r_ref[0]) + r_ref[1]) + r_ref[2]

    return pl.pallas_call(
        body, name=name,
        grid_spec=pltpu.PrefetchScalarGridSpec(
            num_scalar_prefetch=1, grid=(nt,),
            in_specs=[pl.BlockSpec((None, tr, c), lambda i, pi: (pi[0], i, 0)),
                      pl.BlockSpec((3, tr, c), lambda i, pi: (0, i, 0))] + [pl.BlockSpec(memory_space=pl.ANY)] * len(after),
            out_specs=pl.BlockSpec((tr, c), lambda i, pi: (pi[1] * nt + i, 0))),
        out_shape=jax.ShapeDtypeStruct((2 * h, c), F32),
        compiler_params=_params(("parallel",)),
    )(place_idx, p, recv, *after)


def _place_shards(shards, place_idx, *, name, after=()):
    n = len(shards)

    def body(pi_ref, *refs):
        del pi_ref
        for i in range(n):
            refs[n + len(after) + i][...] = refs[i][...]

    return pl.pallas_call(
        body, name=name,
        grid_spec=pltpu.PrefetchScalarGridSpec(
            num_scalar_prefetch=1, grid=(1,),
            in_specs=[pl.BlockSpec(s.shape, lambda i, pi: (0, 0)) for s in shards]
            + [pl.BlockSpec(memory_space=pl.ANY)] * len(after),
            out_specs=tuple(pl.BlockSpec((None,) + s.shape, lambda i, pi: (pi[0], 0, 0)) for s in shards)),
        out_shape=tuple(jax.ShapeDtypeStruct((4,) + s.shape, s.dtype) for s in shards),
        compiler_params=_params(("arbitrary",)),
    )(place_idx, *shards, *after)


def _place_shard(shard, place_idx, *, name, tr=512, after=()):
    r, c = shard.shape
    tr = min(tr, r)
    if r % tr:
        tr = r // 2
    assert r % tr == 0 and tr % 16 == 0, (name, r, tr)

    def body(pi_ref, s_ref, *rest):
        del pi_ref
        rest[-1][...] = s_ref[...]

    return pl.pallas_call(
        body, name=name,
        grid_spec=pltpu.PrefetchScalarGridSpec(
            num_scalar_prefetch=1, grid=(r // tr,),
            in_specs=[pl.BlockSpec((tr, c), lambda i, pi: (i, 0))] + [pl.BlockSpec(memory_space=pl.ANY)] * len(after),
            out_specs=pl.BlockSpec((None, tr, c), lambda i, pi: (pi[0], i, 0))),
        out_shape=jax.ShapeDtypeStruct((4, r, c), shard.dtype),
        compiler_params=_params(("parallel",)),
    )(place_idx, shard, *after)


def _place():
    x, y, c = lax.axis_index("x"), lax.axis_index("y"), lax.axis_index("c")
    chips = [(1 - x, y), (x, 1 - y), (1 - x, 1 - y)]
    return x, y, c, chips


ANY = pl.BlockSpec(memory_space=pl.ANY)


def _exchange_halves(grads, name):
    n = len(grads)

    def body(*refs):
        ins, outs = refs[:n], refs[n:2 * n]
        send_sems, recv_sems = refs[2 * n:]
        x, y, c, _ = _place()

        def copy(a):
            h = ins[a].shape[1] // 2
            return pltpu.make_async_remote_copy(
                src_ref=ins[a].at[:, pl.ds((1 - c) * h, h), :], dst_ref=outs[a],
                send_sem=send_sems.at[a], recv_sem=recv_sems.at[a], device_id=(x, y, 1 - c), device_id_type=MESH)

        for a in range(n):
            copy(a).start()
        for a in range(n):
            copy(a).wait_recv()
        for a in range(n):
            copy(a).wait_send()

    return pl.pallas_call(
        body, name=name,
        in_specs=[ANY] * n, out_specs=tuple([ANY] * n),
        out_shape=tuple(jax.ShapeDtypeStruct((4, g.shape[1] // 2, g.shape[2]), g.dtype) for g in grads),
        scratch_shapes=[pltpu.SemaphoreType.DMA((n,)), pltpu.SemaphoreType.DMA((n,))],
    )(*grads)


HBM = pl.BlockSpec(memory_space=pltpu.HBM)
SEM = pl.BlockSpec(memory_space=pltpu.SEMAPHORE)
EFFECT = pltpu.SideEffectType.DATAFLOW_SIDE_EFFECTING


def _in_hbm(a):
    return pltpu.with_memory_space_constraint(a, pltpu.HBM)


def _split_copy_calls(name, srcs, lands, n_copies, make_copies):
    ns, nl = len(srcs), len(lands)
    nb = ns + nl

    def start(after=()):
        n_after = len(after)

        def body(*refs):
            outs = refs[nb + n_after:]
            copies = make_copies(refs[:ns], refs[ns:nb], outs[0], outs[1])
            for cp in copies:
                cp.start()
            token = refs[-1]
            token[...] = jnp.zeros_like(token)

        bufs = [_in_hbm(a) for a in list(srcs) + list(lands)]
        out = pl.pallas_call(
            body, name=name + "_start",
            out_shape=(pltpu.SemaphoreType.DMA((n_copies,)), pltpu.SemaphoreType.DMA((n_copies,)),
                       *[pltpu.HBM(a.shape, a.dtype) for a in bufs], jax.ShapeDtypeStruct((8, 128), F32)),
            in_specs=[HBM] * nb + [pl.BlockSpec(memory_space=pl.ANY)] * n_after,
            out_specs=(SEM, SEM, *[HBM] * nb, pl.BlockSpec(memory_space=pltpu.VMEM)),
            input_output_aliases={i: 2 + i for i in range(nb)},
            compiler_params=pltpu.CompilerParams(has_side_effects=EFFECT),
        )(*bufs, *after)
        return dict(send=out[0], recv=out[1], bufs=list(out[2:2 + nb]), token=out[-1])

    def wait(state, after):
        def body(*refs):
            copies = make_copies(refs[:ns], refs[ns:nb], refs[nb], refs[nb + 1])
            for cp in copies:
                cp.wait_send()
            for cp in copies:
                cp.wait_recv()

        bufs = state["bufs"]
        out = pl.pallas_call(
            body, name=name + "_wait",
            out_shape=tuple(pltpu.HBM(a.shape, a.dtype) for a in bufs),
            in_specs=[HBM] * nb + [SEM, SEM] + [pl.BlockSpec(memory_space=pl.ANY)] * len(after),
            out_specs=tuple([HBM] * nb),
            input_output_aliases={i: i for i in range(nb)},
            compiler_params=pltpu.CompilerParams(has_side_effects=EFFECT),
        )(*bufs, state["send"], state["recv"], *after)
        return list(out[:ns]), list(out[ns:])

    return start, wait


def _scatter_chips_split(name, parts):
    n = len(parts)
    lands = [lax.empty((3,) + p.shape[1:], p.dtype) for p in parts]

    def make_copies(srcs, lnds, send_sems, recv_sems):
        _, _, c, chips = _place()
        return [pltpu.make_async_remote_copy(
            src_ref=srcs[a].at[2 * px + py], dst_ref=lnds[a].at[j], send_sem=send_sems.at[a * 3 + j],
            recv_sem=recv_sems.at[a * 3 + j], device_id=(px, py, c), device_id_type=MESH)
            for a in range(n) for j, (px, py) in enumerate(chips)]

    return _split_copy_calls(name, parts, lands, 3 * n, make_copies)


def _exchange_halves_split(name, grads):
    n = len(grads)
    lands = [lax.empty((4, g.shape[1] // 2, g.shape[2]), g.dtype) for g in grads]

    def make_copies(srcs, lnds, send_sems, recv_sems):
        x, y, c, _ = _place()
        out = []
        for a in range(n):
            h = srcs[a].shape[1] // 2
            out.append(pltpu.make_async_remote_copy(
                src_ref=srcs[a].at[:, pl.ds((1 - c) * h, h), :], dst_ref=lnds[a], send_sem=send_sems.at[a],
                recv_sem=recv_sems.at[a], device_id=(x, y, 1 - c), device_id_type=MESH))
        return out

    return _split_copy_calls(name, grads, lands, n, make_copies)


def _gather_chips_split(name, shards, lands):
    n = len(shards)

    def make_copies(srcs, lnds, send_sems, recv_sems):
        x, y, c, chips = _place()
        out = []
        for a in range(n):
            h = srcs[a].shape[0] // 2
            for j, (px, py) in enumerate(chips):
                out.append(pltpu.make_async_remote_copy(
                    src_ref=srcs[a].at[pl.ds(c * h, h), :], dst_ref=lnds[a].at[2 * x + y, pl.ds(c * h, h), :],
                    send_sem=send_sems.at[a * 3 + j], recv_sem=recv_sems.at[a * 3 + j],
                    device_id=(px, py, c), device_id_type=MESH))
        return out

    return _split_copy_calls(name, shards, lands, 3 * n, make_copies)


def _gather_finish(gathered, name):
    n = len(gathered)

    def body(*refs):
        outs = refs[n:2 * n]
        send_sems, recv_sems = refs[2 * n:]
        x, y, c, chips = _place()

        def copy(a, j, chip_idx, which):
            h = outs[a].shape[1] // 2
            rows = outs[a].at[chip_idx, pl.ds(which * h, h), :]
            return pltpu.make_async_remote_copy(
                src_ref=rows, dst_ref=rows, send_sem=send_sems.at[a * 3 + j], recv_sem=recv_sems.at[a * 3 + j],
                device_id=(x, y, 1 - c), device_id_type=MESH)

        for a in range(n):
            for j, (px, py) in enumerate(chips):
                copy(a, j, 2 * px + py, c).start()
        for a in range(n):
            for j, (px, py) in enumerate(chips):
                copy(a, j, 2 * px + py, 1 - c).wait_recv()
        for a in range(n):
            for j, (px, py) in enumerate(chips):
                copy(a, j, 2 * px + py, c).wait_send()

    return pl.pallas_call(
        body, name=name,
        in_specs=[ANY] * n, out_specs=tuple([ANY] * n),
        out_shape=tuple(jax.ShapeDtypeStruct(g.shape, g.dtype) for g in gathered),
        input_output_aliases={i: i for i in range(n)},
        scratch_shapes=[pltpu.SemaphoreType.DMA((3 * n,)), pltpu.SemaphoreType.DMA((3 * n,))],
    )(*gathered)


def _gather_forward_split(name, gathered):
    n = len(gathered)

    def make_copies(srcs, lnds, send_sems, recv_sems):
        x, y, c, chips = _place()
        out = []
        for a in range(n):
            h = lnds[a].shape[1] // 2
            for j, (px, py) in enumerate(chips):
                rows = lnds[a].at[2 * px + py, pl.ds(c * h, h), :]
                out.append(pltpu.make_async_remote_copy(
                    src_ref=rows, dst_ref=rows, send_sem=send_sems.at[a * 3 + j], recv_sem=recv_sems.at[a * 3 + j],
                    device_id=(x, y, 1 - c), device_id_type=MESH))
        return out

    return _split_copy_calls(name, [], gathered, 3 * n, make_copies)


def _join_halves_split(name, fulls):
    n = len(fulls)

    def make_copies(srcs, lnds, send_sems, recv_sems):
        x, y, c, _ = _place()
        out = []
        for a in range(n):
            h = lnds[a].shape[0] // 2
            rows = lnds[a].at[pl.ds(c * h, h), :]
            out.append(pltpu.make_async_remote_copy(
                src_ref=rows, dst_ref=rows, send_sem=send_sems.at[a], recv_sem=recv_sems.at[a],
                device_id=(x, y, 1 - c), device_id_type=MESH))
        return out

    return _split_copy_calls(name, [], fulls, n, make_copies)


def _all_gather_small_split(sm):
    r, w = sm.shape

    def make_copies(srcs, lnds, send_sems, recv_sems):
        x, y, c, _ = _place()
        me = 4 * x + 2 * y + c
        rel = [(dx, dy, dc) for dx in (0, 1) for dy in (0, 1) for dc in (0, 1)][1:]
        return [pltpu.make_async_remote_copy(
            src_ref=srcs[0], dst_ref=lnds[0].at[me], send_sem=send_sems.at[k], recv_sem=recv_sems.at[k],
            device_id=(1 - x if dx else x, 1 - y if dy else y, 1 - c if dc else c), device_id_type=MESH)
            for k, (dx, dy, dc) in enumerate(rel)]

    return _split_copy_calls("all_gather_small", [sm], [lax.empty((8, r, w), sm.dtype)], 7, make_copies)


def _sum_devices(sm, gathered, me_idx):
    def body(me_ref, sm_ref, g_ref, o_ref):
        own = sm_ref[...]
        acc = jnp.where(me_ref[0] == 0, own, g_ref[0])
        for d in range(1, 8):
            acc = acc + jnp.where(me_ref[0] == d, own, g_ref[d])
        o_ref[...] = acc

    vm = pl.BlockSpec(memory_space=pltpu.VMEM)
    return pl.pallas_call(
        body, name="sum_devices", in_specs=[pl.BlockSpec(memory_space=pltpu.SMEM), vm, vm], out_specs=vm,
        out_shape=jax.ShapeDtypeStruct(sm.shape, F32),
    )(me_idx, sm, gathered)


def _local_step(x3, mem3, pos2, target3, small, comm):
    bsz, seq, d = x3.shape
    mlen = mem3.shape[1]
    t = bsz * seq
    tok = comm.begin()
    x = x3.reshape(t, d)
    mem = mem3.reshape(bsz * mlen, d)
    target = target3.reshape(t, d)
    rope = _rope_table(pos2.reshape(t, 1), after=tok)
    qg_t = jnp.tile(small["sw_q_norm_g"], (1, SW_HEADS))
    kg_t = jnp.tile(small["sw_k_norm_g"], (1, SW_KV_HEADS))

    hn1 = _rms_fwd(x, small["norm1_g"], name="rms1_fwd", after=tok)
    w = comm.first((hn1, rope))
    w_in_t = w["w_in_t"]
    w_sw_t = w_in_t[HG_COLS:]
    proj_hg = _mm(hn1, w_in_t, NT, t, HG_COLS, d, name="proj_hg", tk=d, after=(w.get("token"),))[0]
    proj_sw = _mm(hn1, w_sw_t, NT, t, SW_COLS, d, name="proj_sw", tk=d)[0]
    y_mix, o_hg, states = _hg_fwd(proj_hg, small["hg_lower_bounds"], small["hg_norm_g"], bsz, seq, y_width=1024)
    y_mix = _sw_fwd(proj_sw, rope, qg_t, kg_t, small["sw_sinks"], y_mix, bsz, seq)
    w = comm.rest(y_mix)
    h1, hn2 = _mm(y_mix, w["w_out"], NN, t, d, 1024, name="out_proj", tk=1024, extras=(x,), rows=(small["norm2_g"],),
                  epilogue=_residual_rms, out_dtypes=(F32, _MXU_DTYPE), after=(w.get("token"),))
    mn = _rms_fwd(mem, small["mem_norm_g"], name="rms_mem_fwd")
    qx = _mm(hn2, w["wq"], NN, t, 512, d, name="xa_q", tk=d)[0]
    kvx = _mm(mn, w["wkv"], NN, bsz * mlen, 1024, d, name="xa_kv", tk=d)[0]
    ox = _xa_fwd(qx, kvx, small["xa_q_norm_g"], small["xa_k_norm_g"], bsz, seq, mlen)
    h2, hn3 = _mm(ox, w["wo"], NN, t, d, 512, name="xa_o", tk=512, extras=(h1,), rows=(small["norm3_g"],),
                  epilogue=_residual_rms, out_dtypes=(F32, _MXU_DTYPE))
    w = {**w, **comm.mlp(hn3)}
    ff = w["down"].shape[0]
    ffs = ff // 4

    def relu_sq(acc):
        a = jnp.maximum(acc, 0.0)
        return a, a * a

    act, act2 = _mm(hn3, w["up"], NN, t, ff, d, name="mlp_up", tm=2048, tn=ffs, tk=d,
                    b_spec=pl.BlockSpec((None, d, ffs), lambda i, j, kk: (j, 0, 0)),
                    epilogue=relu_sq, out_dtypes=(_MXU_DTYPE, _MXU_DTYPE))
    inv_d = 1.0 / d

    def loss_cotangent(acc, res, tgt):
        diff = acc + res - tgt
        v = diff * inv_d
        return v, v, jnp.sum(diff * diff, axis=0, keepdims=True)

    dy, dy_mx, sq_row = _mm(act2, w["down"], NN, t, d, ff, name="mlp_down", tk=2048, extras=(h2, target),
                            epilogue=loss_cotangent, out_dtypes=(F32, _MXU_DTYPE), row_sums=1)
    loss_row = _loss_finish(sq_row, d)

    dz = _mm(dy_mx, w["down"], NT, t, ff, d, name="d_act", tm=2048, tk=d, extras=(act,),
             epilogue=lambda acc, a: (acc * (2.0 * a.astype(F32)),), out_dtypes=(_MXU_DTYPE,))[0]
    g_down = _mm(act2, dy_mx, TN, ff, d, t, name="g_down", tk=t)[0]
    g_up = _mm(hn3, dz, TN, d, ff, t, name="g_up", tn=ffs, tk=t,
               out_shape=(jax.ShapeDtypeStruct((4, d, ffs), F32),),
               out_spec=(pl.BlockSpec((None, min(1024, d), ffs), lambda i, j, kk: (j, i, 0)),))[0]
    tok = comm.grads("mlp", dict(up=g_up, down=g_down))
    dh2, dh2_mx, g_norm3 = _mm(dz, w["up"], NT, t, d, ff, name="d_hn3", tk=ffs, after=tok,
                               b_spec=pl.BlockSpec((None, min(1024, d), ffs), lambda i, j, kk: (kk, j, 0)),
                               extras=(h2, dy), rows=(small["norm3_g"],), epilogue=_rms_bwd_residual,
                               out_dtypes=(F32, _MXU_DTYPE), row_sums=1)
    d_ox = _mm(dh2_mx, w["wo"], NT, t, 512, d, name="d_ox", tk=d)[0]
    g_wo = _mm(ox, dh2_mx, TN, 512, d, t, name="g_wo", tk=t)[0]
    d_qx, d_kvx, g_xq, g_xk = _xa_bwd(qx, kvx, small["xa_q_norm_g"], small["xa_k_norm_g"], d_ox, bsz, seq, mlen)
    g_wq = _mm(hn2, d_qx, TN, d, 512, t, name="g_wq")[0]
    g_wkv = _mm(mn, d_kvx, TN, d, 1024, bsz * mlen, name="g_wkv")[0]
    dh1, dh1_mx, g_norm2 = _mm(d_qx, w["wq"], NT, t, d, 512, name="d_hn2", tk=512, extras=(h1, dh2),
                               rows=(small["norm2_g"],), epilogue=_rms_bwd_residual, out_dtypes=(F32, _MXU_DTYPE),
                               row_sums=1)
    dmn = _mm(d_kvx, w["wkv"], NT, bsz * mlen, d, 1024, name="d_mn", tk=1024)[0]
    g_memn = _rms_gain_grad(mem, small["mem_norm_g"], dmn, name="rms_mem_bwd")
    g_wout = _mm(y_mix, dh1_mx, TN, 1024, d, t, name="g_wout", tk=2048)[0]
    tok = comm.grads("mid", dict(w_out=g_wout, wq=g_wq, wkv=g_wkv, wo=g_wo))
    d_mix = _mm(dh1_mx, w["w_out"], NT, t, 1024, d, name="d_mix", tk=d, after=tok)[0]
    dproj_sw, g_swq, g_swk, g_sinks = _sw_bwd(proj_sw, rope, qg_t, kg_t, small["sw_sinks"], y_mix, d_mix, bsz, seq)
    tok = comm.poll(dproj_sw)
    dproj_hg, g_lb, g_hgn = _hg_bwd(proj_hg, small["hg_lower_bounds"], small["hg_norm_g"], o_hg, states, d_mix, bsz, seq,
                                    after=tok)
    in_rows = HG_COLS + SW_COLS
    sw_tile = 256
    g_in_t = _mm(dproj_hg, hn1, TN, HG_COLS, d, t, name="g_in_hg", tk=t,
                 out_shape=(jax.ShapeDtypeStruct((in_rows, d), F32),),
                 out_spec=(pl.BlockSpec((1024, min(1024, d)), lambda i, j, kk: (i, j)),))[0]
    g_in_t = _mm(dproj_sw, hn1, TN, SW_COLS, d, t, name="g_in_sw", tm=sw_tile, into=g_in_t,
                 out_shape=(jax.ShapeDtypeStruct((in_rows, d), F32),),
                 out_spec=(pl.BlockSpec((sw_tile, min(1024, d)), lambda i, j, kk: (HG_COLS // sw_tile + i, j)),))[0]
    tok = comm.grads("in", dict(w_in_t=g_in_t))
    dhn1_a = _mm(dproj_hg, w_in_t, NN, t, d, HG_COLS, name="d_hn1_hg", tk=HG_COLS, after=tok)[0]
    grad_x, g_norm1 = _mm(dproj_sw, w_sw_t, NN, t, d, SW_COLS, name="d_hn1_sw", tk=SW_COLS, extras=(dhn1_a, x, dh1),
                          rows=(small["norm1_g"],), row_sums=1,
                          epilogue=lambda acc, prev, xv, dres, g: _rms_bwd_residual(acc + prev, xv, dres, g)[1:])

    g_small = dict(norm1_g=g_norm1, hg_lower_bounds=g_lb, hg_norm_g=g_hgn, sw_q_norm_g=g_swq, sw_k_norm_g=g_swk,
                   sw_sinks=g_sinks[:, 0:SW_HEADS], norm2_g=g_norm2, mem_norm_g=g_memn, xa_q_norm_g=g_xq,
                   xa_k_norm_g=g_xk, norm3_g=g_norm3)
    return loss_row, grad_x.reshape(bsz, seq, d), g_small


SMALL_NAMES = ("norm1_g", "hg_lower_bounds", "hg_norm_g", "sw_q_norm_g", "sw_k_norm_g", "sw_sinks", "norm2_g",
               "mem_norm_g", "xa_q_norm_g", "xa_k_norm_g", "norm3_g")
BIG_NAMES = ("w_in", "w_out", "xa_wq", "xa_wkv", "xa_wo", "mlp_up", "mlp_down")
WEIGHT_ORDER = ("norm1_g", "w_in", "hg_lower_bounds", "hg_norm_g", "sw_q_norm_g", "sw_k_norm_g", "sw_sinks", "w_out",
                "norm2_g", "mem_norm_g", "xa_wq", "xa_wkv", "xa_q_norm_g", "xa_k_norm_g", "xa_wo", "norm3_g",
                "mlp_up", "mlp_down")


def _pack_rows(vals, width):
    starts, at = [], 0
    for v in vals:
        starts.append(at)
        at += v.shape[0]
    total = at + (-at) % 8
    out = None
    for v, s in zip(vals, starts):
        placed = jnp.pad(v, ((s, total - s - v.shape[0]), (0, width - v.shape[1])))
        out = placed if out is None else out + placed
    return out, starts


class _MeshWeights:
    LATE = ("w_out", "xa_wq", "xa_wkv", "xa_wo", "mlp_up", "mlp_down")

    def __init__(self, shards, d, ff):
        self.shards, self.d, self.ff = shards, d, ff
        self.c_idx = lax.axis_index("c").astype(jnp.int32).reshape(1)
        chip = (2 * lax.axis_index("x") + lax.axis_index("y")).astype(jnp.int32)
        self.place_idx = jnp.stack([chip, lax.axis_index("c").astype(jnp.int32)])
        self.pending = []
        self.exchanging = None

    def begin(self):
        shard = self.shards["w_in"]
        start, self.in_wait = _gather_chips_split(
            "gather_in", [shard], [_place_shard(shard, self.place_idx, name="place_w_in")])
        self.in_state = start()
        tok = (self.in_state["token"],)
        self.placed = list(_place_shards([self.shards[n] for n in self.LATE], self.place_idx, name="place_late",
                                         after=tok))
        return tok

    def first(self, after):
        _, lands = self.in_wait(self.in_state, (*after, *self.placed))
        (g_in,) = _gather_finish(lands, "gather_in_finish")
        start, self.late_wait = _gather_chips_split("gather_late", [self.shards[n] for n in self.LATE], self.placed)
        self.late_state = start(after=(g_in,))
        return dict(w_in_t=g_in.reshape(-1, self.d), token=self.late_state["token"])

    def rest(self, after):
        _, lands = self.late_wait(self.late_state, (after,))
        g_out, g_q, g_kv, g_o = _gather_finish(lands[:4], "gather_late_finish")
        start, self.mlp_wait = _gather_forward_split("gather_mlp_forward", lands[4:])
        self.mlp_state = start(after=(g_out,))
        d = self.d
        return dict(w_out=g_out.reshape(-1, d), wq=g_q.reshape(d, -1), wkv=g_kv.reshape(d, -1),
                    wo=jnp.concatenate([g_o[k] for k in range(4)], axis=1), token=self.mlp_state["token"])

    def mlp(self, after):
        _, (g_up, g_dn) = self.mlp_wait(self.mlp_state, (after,))
        return dict(up=g_up, down=g_dn.reshape(self.ff, self.d))

    def _scatter(self, tag, names, arrays, recv):
        parts = [_add_halves(g, r, self.c_idx, name="rs_add_halves_" + n) for n, g, r in zip(names, arrays, recv)]
        start, wait = _scatter_chips_split("rs_scatter_" + tag, parts)
        state = start()
        self.pending.append((names, wait, state))
        return state["token"]

    def _advance(self, after):
        if self.exchanging is None:
            return ()
        tag, names, wait, state = self.exchanging
        self.exchanging = None
        arrays, recv = wait(state, (after,))
        return (self._scatter(tag, names, arrays, recv),)

    def poll(self, after):
        return self._advance(after)

    def grads(self, tag, g):
        d, ff = self.d, self.ff
        if tag == "mlp":
            names, arrays = ("mlp_up", "mlp_down"), [g["up"], g["down"].reshape(4, ff // 4, d)]
        elif tag == "mid":
            names = ("w_out", "xa_wq", "xa_wkv", "xa_wo")
            ds = d // 4
            g_wo = jnp.stack([g["wo"][:, ds * k:ds * (k + 1)] for k in range(4)])
            arrays = [g["w_out"].reshape(4, -1, d), g["wq"].reshape(4, d // 4, -1), g["wkv"].reshape(4, d // 4, -1), g_wo]
        else:
            names, arrays = ("w_in",), [g["w_in_t"].reshape(4, -1, d)]
        toks = self._advance(arrays[0])
        if tag == "in":
            return toks + (self._scatter(tag, names, arrays, _exchange_halves(arrays, "rs_exchange_" + tag)),)
        start, wait = _exchange_halves_split("rs_exchange_" + tag, arrays)
        state = start()
        self.exchanging = (tag, names, wait, state)
        return toks + (state["token"],)

    def finish(self, after):
        joins, tok = [], ()
        for names, wait, state in self.pending:
            srcs, lands = wait(state, tuple(after) + tok)
            fulls = [_add_chips(p, r, self.place_idx, name="rs_add_chips_" + n, after=tok)
                     for n, p, r in zip(names, srcs, lands)]
            start, jwait = _join_halves_split("rs_join_" + names[0], fulls)
            jstate = start()
            tok = (jstate["token"],)
            joins.append((names, jwait, jstate))
        out = {}
        for names, jwait, jstate in joins:
            _, fulls = jwait(jstate, tok)
            out.update(zip(names, fulls))
        return out


def kernel(x, mem, positions, norm1_g, w_in, hg_lower_bounds, hg_norm_g, sw_q_norm_g, sw_k_norm_g, sw_sinks, w_out, norm2_g, mem_norm_g, xa_wq, xa_wkv, xa_q_norm_g, xa_k_norm_g, xa_wo, norm3_g, mlp_up, mlp_down, loss_target, m_norm1_g, m_w_in, m_hg_lower_bounds, m_hg_norm_g, m_sw_q_norm_g, m_sw_k_norm_g, m_sw_sinks, m_w_out, m_norm2_g, m_mem_norm_g, m_xa_wq, m_xa_wkv, m_xa_q_norm_g, m_xa_k_norm_g, m_xa_wo, m_norm3_g, m_mlp_up, m_mlp_down, v_norm1_g, v_w_in, v_hg_lower_bounds, v_hg_norm_g, v_sw_q_norm_g, v_sw_k_norm_g, v_sw_sinks, v_w_out, v_norm2_g, v_mem_norm_g, v_xa_wq, v_xa_wkv, v_xa_q_norm_g, v_xa_k_norm_g, v_xa_wo, v_norm3_g, v_mlp_up, v_mlp_down):
    given = dict(locals())
    weights = {n: given[n] for n in WEIGHT_ORDER}
    moms = {n: given["m_" + n] for n in WEIGHT_ORDER}
    vars_ = {n: given["v_" + n] for n in WEIGHT_ORDER}
    d = x.shape[-1]
    ff = mlp_down.shape[1] * 4
    small = {n: weights[n] for n in SMALL_NAMES}

    def plain(n, a):
        return jnp.swapaxes(a[0], 0, 1) if n == "w_in" else a[0]

    comm = _MeshWeights({n: plain(n, weights[n]).astype(_MXU_DTYPE) for n in BIG_NAMES}, d, ff)
    loss_row, grad_x, g_small = _local_step(x, mem, positions, loss_target, small, comm)
    packed, starts = _pack_rows([g_small[n] for n in SMALL_NAMES] + [loss_row], 1024)
    start, wait = _all_gather_small_split(packed)
    state = start()
    big_grads = comm.finish((grad_x, state["token"]))
    (own,), (gathered,) = wait(state, (big_grads[BIG_NAMES[0]],))
    device = (4 * lax.axis_index("x") + 2 * lax.axis_index("y") + lax.axis_index("c")).astype(jnp.int32).reshape(1)
    summed = _sum_devices(own, gathered, device)
    small_grads = {}
    for n, s in zip(SMALL_NAMES, starts):
        r, c = weights[n].shape
        small_grads[n] = summed[s:s + r, 0:c]
    loss = summed[starts[-1], 0]

    grads, deltas, new_m, new_v = {}, {}, {}, {}
    for n in BIG_NAMES:
        outs = _adamw_big(plain(n, weights[n]), big_grads[n], plain(n, moms[n]), plain(n, vars_[n]), name="adamw_" + n)
        grads[n], deltas[n], new_m[n], new_v[n] = ((jnp.swapaxes(a, 0, 1) if n == "w_in" else a)[None] for a in outs)
    sm_out = _adamw_small([weights[n] for n in SMALL_NAMES], [small_grads[n] for n in SMALL_NAMES],
                          [moms[n] for n in SMALL_NAMES], [vars_[n] for n in SMALL_NAMES])
    ns = len(SMALL_NAMES)
    for i, n in enumerate(SMALL_NAMES):
        grads[n], deltas[n], new_m[n], new_v[n] = small_grads[n], sm_out[i], sm_out[ns + i], sm_out[2 * ns + i]

    return (loss, grad_x, *[grads[n] for n in WEIGHT_ORDER], *[deltas[n] for n in WEIGHT_ORDER],
            *[new_m[n] for n in WEIGHT_ORDER], *[new_v[n] for n in WEIGHT_ORDER])
```

```python
import numpy as np
import jax
import jax.numpy as jnp
from jax import lax
from jax.experimental import pallas as pl
from jax.experimental.pallas import tpu as pltpu

F32 = jnp.float32
_MXU_DTYPE = jnp.bfloat16

EPS = 1e-6
HG_HEADS = 4
HG_D = 128
HG_CHUNK = 64
HG_TILE = 512
HG_LEVELS = (32, 16, 8, 4, 2, 1)
SW_HEADS = 8
SW_KV_HEADS = 2
SW_GROUP = SW_HEADS // SW_KV_HEADS
SW_HD = 64
SW_BLOCK = 128
ROPE_THETA = 500000.0
ROT_DIM = SW_HD // 4
XA_HEADS = 4
XA_HD = 128
HG_COLS = 4 * HG_HEADS * HG_D
SW_COLS = (SW_HEADS + 2 * SW_KV_HEADS) * SW_HD

ADAM_LR = 0.001
ADAM_B1 = 0.9
ADAM_B2 = 0.999
ADAM_EPS = 1e-08
ADAM_WD = 0.01
ADAM_STEP = 10

VMEM_LIMIT = 56 * 1024 * 1024
MESH = pl.DeviceIdType.MESH

NN = ((1,), (0,))
NT = ((1,), (1,))
TN = ((0,), (0,))


def _mx(v):
    return v.astype(_MXU_DTYPE)


def _dot(a, b, dims=NN):
    return lax.dot_general(_mx(a), _mx(b), (dims, ((), ())), preferred_element_type=F32)


def _split_dot(a, v, dims, parts):
    acc = None
    rest = v
    for p in range(parts):
        piece = _mx(rest)
        term = lax.dot_general(a, piece, (dims, ((), ())), preferred_element_type=F32)
        acc = term if acc is None else acc + term
        if p + 1 < parts:
            rest = rest - piece.astype(F32)
    return acc


def _params(sem):
    return pltpu.CompilerParams(dimension_semantics=sem, vmem_limit_bytes=VMEM_LIMIT)


def _mm(a, b, mode, m, n, k, *, name, tm=1024, tn=1024, tk=1024, a_spec=None, b_spec=None, extras=(), rows=(),
        epilogue=None, out_dtypes=(F32,), row_sums=0, out_shape=None, out_spec=None, after=(), into=None):
    after = tuple(t for t in after if t is not None) + (() if into is None else (into,))
    tm, tn, tk = min(tm, m), min(tn, n), min(tk, k)
    assert m % tm == 0 and n % tn == 0 and k % tk == 0, (name, m, n, k, tm, tn, tk)
    gi, gj, gk = m // tm, n // tn, k // tk
    assert row_sums == 0 or gj == 1, name
    if a_spec is None:
        a_spec = (pl.BlockSpec((tk, tm), lambda i, j, kk: (kk, i)) if mode == TN
                  else pl.BlockSpec((tm, tk), lambda i, j, kk: (i, kk)))
    if b_spec is None:
        b_spec = (pl.BlockSpec((tn, tk), lambda i, j, kk: (j, kk)) if mode == NT
                  else pl.BlockSpec((tk, tn), lambda i, j, kk: (kk, j)))
    mn_spec = pl.BlockSpec((tm, tn), lambda i, j, kk: (i, j))
    if epilogue is None:
        epilogue = lambda acc: (acc,)
    row_spec = pl.BlockSpec((1, tn), lambda i, j, kk: (0, j))
    n_ex, n_out = len(extras) + len(rows), len(out_dtypes)
    if out_shape is None:
        out_shape = tuple(jax.ShapeDtypeStruct((m, n), d) for d in out_dtypes)
        out_spec = tuple(mn_spec for _ in out_dtypes)
    out_shape = tuple(out_shape) + tuple(jax.ShapeDtypeStruct((1, n), F32) for _ in range(row_sums))
    out_spec = tuple(out_spec) + tuple(row_spec for _ in range(row_sums))

    n_after = len(after)

    def body(*refs):
        a_ref, b_ref = refs[0], refs[1]
        ex = refs[2:2 + n_ex]
        outs = refs[2 + n_ex + n_after:2 + n_ex + n_after + n_out + row_sums]
        first_row_tile = pl.program_id(0) == 0

        def finish(acc):
            res = epilogue(acc, *[e[...] for e in ex])
            for o, r in zip(outs[:n_out], res[:n_out]):
                o[...] = r.astype(o.dtype)
            if row_sums:
                @pl.when(first_row_tile)
                def _():
                    for o in outs[n_out:]:
                        o[...] = jnp.zeros_like(o)

                for o, r in zip(outs[n_out:], res[n_out:]):
                    o[...] += r

        if gk == 1:
            finish(_dot(a_ref[...], b_ref[...], mode))
        else:
            acc_ref = refs[-1]
            kk = pl.program_id(2)

            @pl.when(kk == 0)
            def _():
                acc_ref[...] = jnp.zeros_like(acc_ref)

            acc_ref[...] += _dot(a_ref[...], b_ref[...], mode)

            @pl.when(kk == gk - 1)
            def _():
                finish(acc_ref[...])

    return pl.pallas_call(
        body, name=name, grid=(gi, gj, gk),
        in_specs=([a_spec, b_spec] + [mn_spec] * len(extras) + [row_spec] * len(rows)
                  + [pl.BlockSpec(memory_space=pl.ANY)] * n_after),
        out_specs=out_spec, out_shape=out_shape,
        input_output_aliases={} if into is None else {2 + n_ex + n_after - 1: 0},
        scratch_shapes=[pltpu.VMEM((tm, tn), F32)] if gk > 1 else [],
        compiler_params=_params(("arbitrary" if row_sums else "parallel", "parallel", "arbitrary")),
    )(a, b, *extras, *rows, *after)


def _rms_rows(xv, g):
    return xv * lax.rsqrt(jnp.mean(xv * xv, axis=1, keepdims=True) + EPS) * g


def _rms_rows_bwd(xv, g, dyv):
    r = lax.rsqrt(jnp.mean(xv * xv, axis=1, keepdims=True) + EPS)
    u = dyv * g
    return (r * u - xv * (r * r * r) * jnp.mean(u * xv, axis=1, keepdims=True),
            jnp.sum(dyv * xv * r, axis=0, keepdims=True))


def _residual_rms(acc, res, g):
    h = acc + res
    return h, _rms_rows(h, g)


def _rms_bwd_residual(dhn, xv, dres, g):
    dx, dg = _rms_rows_bwd(xv, g, dhn)
    dx = dx + dres
    return dx, dx, dg


def _rms_fwd(x, g, *, name, tm=512, after=()):
    t, d = x.shape
    tm = min(tm, t)
    after = tuple(a for a in after if a is not None)

    def body(x_ref, g_ref, *rest):
        rest[-1][...] = _rms_rows(x_ref[...], g_ref[...]).astype(rest[-1].dtype)

    return pl.pallas_call(
        body, name=name, grid=(t // tm,),
        in_specs=[pl.BlockSpec((tm, d), lambda i: (i, 0)), pl.BlockSpec((1, d), lambda i: (0, 0))]
        + [pl.BlockSpec(memory_space=pl.ANY)] * len(after),
        out_specs=pl.BlockSpec((tm, d), lambda i: (i, 0)),
        out_shape=jax.ShapeDtypeStruct((t, d), _MXU_DTYPE),
        compiler_params=_params(("parallel",)),
    )(x, g, *after)


def _rms_gain_grad(x, g, dy, *, name, tm=512):
    t, d = x.shape
    tm = min(tm, t)

    def body(x_ref, g_ref, dy_ref, dg_ref):
        @pl.when(pl.program_id(0) == 0)
        def _():
            dg_ref[...] = jnp.zeros_like(dg_ref)

        dg_ref[...] += _rms_rows_bwd(x_ref[...], g_ref[...], dy_ref[...])[1]

    row = pl.BlockSpec((tm, d), lambda i: (i, 0))
    vec = pl.BlockSpec((1, d), lambda i: (0, 0))
    return pl.pallas_call(
        body, name=name, grid=(t // tm,), in_specs=[row, vec, row], out_specs=vec,
        out_shape=jax.ShapeDtypeStruct((1, d), F32), compiler_params=_params(("arbitrary",)),
    )(x, g, dy)


def _hg_constants():
    c = HG_CHUNK
    t = np.arange(c)
    sums = [t[None, :] <= t[:, None]]
    masks = []
    for m in HG_LEVELS:
        base = (t // (2 * m)) * (2 * m)
        mid = base + m - 1
        second = (t - base) >= m
        upper = (t[None, :] > mid[:, None]) & (t[None, :] <= t[:, None])
        lower = (t[None, :] > t[:, None]) & (t[None, :] <= mid[:, None])
        sums.append(np.where(second[:, None], upper, lower))
        masks.append(second[:, None] & (~second)[None, :] & (base[:, None] == base[None, :]))
    return (np.concatenate(sums, axis=0).astype(np.float32), np.stack(masks).astype(np.float32))


HG_HEAD_LANES = tuple(slice(HG_D * h, HG_D * (h + 1)) for h in range(HG_HEADS))


def _per_head(fn, slab):
    return jnp.concatenate([jnp.broadcast_to(fn(slab[:, hs]), (slab.shape[0], HG_D)) for hs in HG_HEAD_LANES], axis=1)


def _lane_sum(v):
    return jnp.sum(v, axis=1, keepdims=True)


def _lane_mean(v):
    return jnp.mean(v, axis=1, keepdims=True)


def _hg_gates(blk, lbp):
    w = HG_HEADS * HG_D
    q, x, v, gl = blk[:, 0:w], blk[:, w:2 * w], blk[:, 2 * w:3 * w], blk[:, 3 * w:4 * w]
    mx = jnp.max(lbp, axis=0, keepdims=True)
    e = jnp.exp(lbp - mx)
    lb = e[0:1, :] / jnp.sum(e, axis=0, keepdims=True)
    sig = jax.nn.sigmoid(x)
    f = lb + (1.0 - lb) * sig
    return q, v, gl, lb, sig, f, 1.0 - f, jnp.log(f)


def _hg_fwd(proj, lbp, ng, bsz, seq, *, y_width):
    t = proj.shape[0]
    nc = seq // HG_CHUNK
    a_np, m_np = _hg_constants()
    a_all = jnp.asarray(a_np, _MXU_DTYPE)
    masks = jnp.asarray(m_np, F32)
    nl = len(HG_LEVELS)

    ts = min(HG_TILE, seq)
    ns, nct = seq // ts, ts // HG_CHUNK
    hw = HG_HEADS * HG_D

    def body(p_ref, lb_ref, ng_ref, a_ref, m_ref, y_ref, o_ref, st_ref, carry):
        a_mat = a_ref[...]
        ngv = ng_ref[...]

        @pl.when(pl.program_id(0) == 0)
        def _():
            carry[...] = jnp.zeros_like(carry)

        ng4 = _tile_lanes(ngv, HG_HEADS)
        heads = range(HG_HEADS)
        exs = range(bsz)
        hl = HG_HEAD_LANES
        lbp_v = lb_ref[...]

        def chunk(c, _):
            rows = pl.ds(pl.multiple_of(c * HG_CHUNK, HG_CHUNK), HG_CHUNK)
            gates = [_hg_gates(p_ref[e, rows, :], lbp_v) for e in exs]
            q, v, gl = [g[0] for g in gates], [g[1] for g in gates], [g[2] for g in gates]
            k = [g[6] for g in gates]
            sts = [[carry[e, h] for h in heads] for e in exs]
            e_all = [_split_dot(a_mat, gates[e][7], NN, 3) for e in exs]
            b = [e_all[e][0:HG_CHUNK] for e in exs]
            qb = [q[e] * jnp.exp(b[e]) for e in exs]
            o = [[_dot(qb[e][:, hl[h]], sts[e][h], NT) for h in heads] for e in exs]
            p = [[jnp.zeros((HG_CHUNK, HG_CHUNK), F32) for _ in heads] for _ in exs]
            for li in range(nl):
                dec = [jnp.exp(e_all[e][HG_CHUNK * (li + 1):HG_CHUNK * (li + 2)]) for e in exs]
                qm, km, mk = [q[e] * dec[e] for e in exs], [k[e] * dec[e] for e in exs], m_ref[li]
                p = [[p[e][h] + mk * _dot(qm[e][:, hl[h]], km[e][:, hl[h]], NT) for h in heads] for e in exs]
            bl = [b[e][HG_CHUNK - 1:HG_CHUNK, :] for e in exs]
            kd = [k[e] * jnp.exp(bl[e] - b[e]) for e in exs]
            pv = [[_dot(p[e][h], v[e][:, hl[h]]) for h in heads] for e in exs]
            upd = [[_dot(v[e][:, hl[h]], kd[e][:, hl[h]], TN) for h in heads] for e in exs]
            for e in exs:
                o_all = (jnp.concatenate([o[e][h] + pv[e][h] for h in heads], axis=1)
                         + _per_head(_lane_sum, q[e] * k[e]) * v[e])
                r = lax.rsqrt(_per_head(_lane_mean, o_all * o_all) + EPS)
                ebl = jnp.exp(bl[e])
                for h in heads:
                    st_ref[e, h, c] = sts[e][h]
                    carry[e, h] = sts[e][h] * ebl[:, hl[h]] + upd[e][h]
                o_ref[e, rows, :] = o_all
                y_ref[e, rows, :] = (o_all * r * ng4) * (gl[e] * jax.nn.sigmoid(gl[e]))
            return 0

        lax.fori_loop(0, nct, chunk, 0)

    y3, o3, states = pl.pallas_call(
        body, name="hgrn2_fwd", grid=(ns,),
        in_specs=[pl.BlockSpec((bsz, ts, HG_COLS), lambda s: (0, s, 0)),
                  pl.BlockSpec((2, hw), lambda s: (0, 0)),
                  pl.BlockSpec((1, HG_D), lambda s: (0, 0)),
                  pl.BlockSpec(a_all.shape, lambda s: (0, 0)),
                  pl.BlockSpec(masks.shape, lambda s: (0, 0, 0))],
        out_specs=(pl.BlockSpec((bsz, ts, hw), lambda s: (0, s, 0)),
                   pl.BlockSpec((bsz, ts, hw), lambda s: (0, s, 0)),
                   pl.BlockSpec((bsz, HG_HEADS, nct, HG_D, HG_D), lambda s: (0, 0, s, 0, 0))),
        out_shape=(jax.ShapeDtypeStruct((bsz, seq, y_width), F32),
                   jax.ShapeDtypeStruct((bsz, seq, hw), F32),
                   jax.ShapeDtypeStruct((bsz, HG_HEADS, nc, HG_D, HG_D), F32)),
        scratch_shapes=[pltpu.VMEM((bsz, HG_HEADS, HG_D, HG_D), F32)],
        compiler_params=_params(("arbitrary",)),
    )(proj.reshape(bsz, seq, HG_COLS), lbp, ng, a_all, masks)
    return y3.reshape(t, y_width), o3.reshape(t, hw), states


def _hg_bwd(proj, lbp, ng, o_all, states, dy, bsz, seq, after=()):
    after = tuple(a for a in after if a is not None)
    t = proj.shape[0]
    nc = seq // HG_CHUNK
    a_np, m_np = _hg_constants()
    a_all = jnp.asarray(a_np, _MXU_DTYPE)
    masks = jnp.asarray(m_np, F32)
    nl = len(HG_LEVELS)
    cs = HG_CHUNK

    ts = min(HG_TILE, seq)
    ns, nct = seq // ts, ts // cs
    hw = HG_HEADS * HG_D

    def body(p_ref, lb_ref, ng_ref, a_ref, m_ref, o_ref, st_ref, dy_ref, *rest):
        dp_ref, dlb_ref, dng_ref, dst_ref = rest[len(after):]
        a_mat = a_ref[...]
        ngv = ng_ref[...]
        ng4 = _tile_lanes(ngv, HG_HEADS)
        last_row = lax.broadcasted_iota(jnp.int32, (cs, hw), 0) == cs - 1
        first = pl.program_id(0) == 0
        heads = range(HG_HEADS)
        exs = range(bsz)
        hl = HG_HEAD_LANES
        lbp_v = lb_ref[...]

        @pl.when(first)
        def _():
            dst_ref[...] = jnp.zeros_like(dst_ref)

        def side_by_side(parts):
            return jnp.concatenate(parts, axis=1)

        def chunk(i, carry):
            dlb_acc, dng_acc = carry
            c = nct - 1 - i
            rows = pl.ds(pl.multiple_of(c * cs, cs), cs)
            gates = [_hg_gates(p_ref[e, rows, :], lbp_v) for e in exs]
            q, v, gl = [g[0] for g in gates], [g[1] for g in gates], [g[2] for g in gates]
            lb, sig, f, k = gates[0][3], [g[4] for g in gates], [g[5] for g in gates], [g[6] for g in gates]
            o = [o_ref[e, rows, :] for e in exs]
            dyv = [dy_ref[e, rows, :] for e in exs]
            sts = [[st_ref[e, h, c] for h in heads] for e in exs]
            dsts = [[dst_ref[e, h] for h in heads] for e in exs]
            e_all = [_split_dot(a_mat, gates[e][7], NN, 3) for e in exs]
            b = [e_all[e][0:cs] for e in exs]
            eb = [jnp.exp(b[e]) for e in exs]
            bl = [b[e][cs - 1:cs, :] for e in exs]
            ebl = [jnp.exp(bl[e]) for e in exs]
            ekd = [jnp.exp(bl[e] - b[e]) for e in exs]
            qb = [q[e] * eb[e] for e in exs]
            kd = [k[e] * ekd[e] for e in exs]
            do, dgl = [], []
            for e in exs:
                sg = jax.nn.sigmoid(gl[e])
                silu = gl[e] * sg
                r = lax.rsqrt(_per_head(_lane_mean, o[e] * o[e]) + EPS)
                dgl.append(dyv[e] * (o[e] * r * ng4) * (sg * (1.0 + gl[e] * (1.0 - sg))))
                u = dyv[e] * silu * ng4
                do.append(r * u - o[e] * (r * r * r) * _per_head(_lane_mean, u * o[e]))
                dng4 = jnp.sum(dyv[e] * silu * o[e] * r, axis=0, keepdims=True)
                dng_acc = dng_acc + ((dng4[:, hl[0]] + dng4[:, hl[1]]) + (dng4[:, hl[2]] + dng4[:, hl[3]]))
            es, qm, km = [], [], []
            p = [[jnp.zeros((cs, cs), F32) for _ in heads] for _ in exs]
            for li in range(nl):
                dec = [jnp.exp(e_all[e][cs * (li + 1):cs * (li + 2)]) for e in exs]
                es.append(dec)
                qm.append([q[e] * dec[e] for e in exs])
                km.append([k[e] * dec[e] for e in exs])
                mk = m_ref[li]
                p = [[p[e][h] + mk * _dot(qm[li][e][:, hl[h]], km[li][e][:, hl[h]], NT) for h in heads] for e in exs]
            dp = [[_dot(do[e][:, hl[h]], v[e][:, hl[h]], NT) for h in heads] for e in exs]
            dv_p = [[_dot(p[e][h], do[e][:, hl[h]], TN) for h in heads] for e in exs]
            dv_s = [[_dot(kd[e][:, hl[h]], dsts[e][h], NT) for h in heads] for e in exs]
            dqb = [side_by_side([_dot(do[e][:, hl[h]], sts[e][h]) for h in heads]) for e in exs]
            dkd = [side_by_side([_dot(v[e][:, hl[h]], dsts[e][h]) for h in heads]) for e in exs]
            new_dst = [[_dot(do[e][:, hl[h]], qb[e][:, hl[h]], TN) for h in heads] for e in exs]
            dv = [side_by_side([dv_p[e][h] + dv_s[e][h] for h in heads]) + _per_head(_lane_sum, q[e] * k[e]) * do[e]
                  for e in exs]
            dq = [dqb[e] * eb[e] for e in exs]
            dk = [dkd[e] * ekd[e] for e in exs]
            de = []
            for e in exs:
                dbl = (jnp.sum(dkd[e] * kd[e], axis=0, keepdims=True)
                       + side_by_side([jnp.sum(dsts[e][h] * sts[e][h], axis=0, keepdims=True) for h in heads]) * ebl[e])
                de.append([dqb[e] * qb[e] - dkd[e] * kd[e] + jnp.where(last_row, dbl, 0.0)])
            for li in range(nl):
                mk = m_ref[li]
                dpm = [[mk * dp[e][h] for h in heads] for e in exs]
                dqm = [side_by_side([_dot(dpm[e][h], km[li][e][:, hl[h]]) for h in heads]) for e in exs]
                dkm = [side_by_side([_dot(dpm[e][h], qm[li][e][:, hl[h]], TN) for h in heads]) for e in exs]
                for e in exs:
                    dq[e] = dq[e] + dqm[e] * es[li][e]
                    dk[e] = dk[e] + dkm[e] * es[li][e]
                    de[e].append(dqm[e] * qm[li][e] + dkm[e] * km[li][e])
            dg = [_split_dot(a_mat, jnp.concatenate(de[e], axis=0), TN, 2) for e in exs]
            for e in exs:
                dpd = _per_head(_lane_sum, do[e] * v[e])
                df = dg[e] / f[e] - (dk[e] + dpd * q[e])
                dp_ref[e, rows, 0:hw] = _mx(dq[e] + dpd * k[e])
                dp_ref[e, rows, hw:2 * hw] = _mx(df * (1.0 - lb) * sig[e] * (1.0 - sig[e]))
                dp_ref[e, rows, 2 * hw:3 * hw] = _mx(dv[e])
                dp_ref[e, rows, 3 * hw:4 * hw] = _mx(dgl[e])
                for h in heads:
                    dst_ref[e, h] = dsts[e][h] * ebl[e][:, hl[h]] + new_dst[e][h]
                dlb_acc = dlb_acc + jnp.sum(df * (1.0 - sig[e]), axis=0, keepdims=True)
            return dlb_acc, dng_acc

        dlb, dng = lax.fori_loop(0, nct, chunk, (jnp.zeros((1, hw), F32), jnp.zeros((1, HG_D), F32)))

        @pl.when(first)
        def _():
            dlb_ref[...] = jnp.zeros_like(dlb_ref)
            dng_ref[...] = jnp.zeros_like(dng_ref)

        mx = jnp.max(lbp_v, axis=0, keepdims=True)
        e = jnp.exp(lbp_v - mx)
        s0 = e[0:1, :] / jnp.sum(e, axis=0, keepdims=True)
        da0 = dlb * s0 * (1.0 - s0)
        dlb_ref[...] += jnp.concatenate([da0, -da0], axis=0)
        dng_ref[...] += dng

    rows3 = lambda w: pl.BlockSpec((bsz, ts, w), lambda s: (0, ns - 1 - s, 0))
    dproj, dlb, dng = pl.pallas_call(
        body, name="hgrn2_bwd", grid=(ns,),
        in_specs=[rows3(HG_COLS),
                  pl.BlockSpec((2, hw), lambda s: (0, 0)),
                  pl.BlockSpec((1, HG_D), lambda s: (0, 0)),
                  pl.BlockSpec(a_all.shape, lambda s: (0, 0)),
                  pl.BlockSpec(masks.shape, lambda s: (0, 0, 0)),
                  rows3(hw),
                  pl.BlockSpec((bsz, HG_HEADS, nct, HG_D, HG_D), lambda s: (0, 0, ns - 1 - s, 0, 0)),
                  rows3(hw)] + [pl.BlockSpec(memory_space=pl.ANY)] * len(after),
        out_specs=(rows3(HG_COLS),
                   pl.BlockSpec((2, hw), lambda s: (0, 0)),
                   pl.BlockSpec((1, HG_D), lambda s: (0, 0))),
        out_shape=(jax.ShapeDtypeStruct((bsz, seq, HG_COLS), _MXU_DTYPE),
                   jax.ShapeDtypeStruct((2, hw), F32),
                   jax.ShapeDtypeStruct((1, HG_D), F32)),
        scratch_shapes=[pltpu.VMEM((bsz, HG_HEADS, HG_D, HG_D), F32)],
        compiler_params=_params(("arbitrary",)),
    )(proj.reshape(bsz, seq, HG_COLS), lbp, ng, a_all, masks, o_all.reshape(bsz, seq, hw), states,
      dy.reshape(bsz, seq, dy.shape[1]), *after)
    return dproj.reshape(t, HG_COLS), dlb, dng


def _sw_constants():
    half = ROT_DIM // 2
    inv = (np.float32(ROPE_THETA) ** (-(np.arange(half, dtype=np.float32) * np.float32(2.0) / np.float32(ROT_DIM)))
           ).astype(np.float32)
    freq = np.zeros((1, 128), np.float32)
    sign = np.zeros((1, 128), np.float32)
    for h in range(2):
        freq[0, 64 * h:64 * h + half] = inv
        freq[0, 64 * h + half:64 * h + 2 * half] = inv
        sign[0, 64 * h:64 * h + half] = -1.0
        sign[0, 64 * h + half:64 * h + 2 * half] = 1.0
    seg = np.kron(np.eye(8, dtype=np.float32), np.full((64, 64), 1.0 / 64.0, np.float32))
    return freq, sign, seg


def _rope_table(pos, *, tm=512, after=()):
    t = pos.shape[0]
    tm = min(tm, t)
    freq_np, sign_np, _ = _sw_constants()
    after = tuple(a for a in after if a is not None)

    def body(p_ref, f_ref, s_ref, *rest):
        o_ref = rest[-1]
        ang = p_ref[...].astype(F32) * f_ref[...]
        o_ref[:, 0:128] = jnp.cos(ang)
        o_ref[:, 128:256] = jnp.sin(ang) * s_ref[...]

    vec = pl.BlockSpec((1, 128), lambda i: (0, 0))
    return pl.pallas_call(
        body, name="rope_table", grid=(t // tm,),
        in_specs=[pl.BlockSpec((tm, 1), lambda i: (i, 0)), vec, vec] + [pl.BlockSpec(memory_space=pl.ANY)] * len(after),
        out_specs=pl.BlockSpec((tm, 256), lambda i: (i, 0)),
        out_shape=jax.ShapeDtypeStruct((t, 256), F32),
        compiler_params=_params(("parallel",)),
    )(pos, jnp.asarray(freq_np), jnp.asarray(sign_np), *after)


def _tile_lanes(v, times):
    return v if times == 1 else jnp.concatenate([v] * times, axis=1)


def _swap_halves(v):
    w = v.shape[1]
    half = ROT_DIM // 2
    lane = lax.broadcasted_iota(jnp.int32, v.shape, 1) % SW_HD
    return jnp.where(lane < half, pltpu.roll(v, w - half, 1), jnp.where(lane < 2 * half, pltpu.roll(v, half, 1), 0.0))


def _sw_norm_rope(tv, gain, seg, cosv, sinv):
    w = tv.shape[1]
    ms = _split_dot_rhs(tv * tv, seg[0:w, 0:w])
    r = lax.rsqrt(ms + EPS)
    tn = tv * r * gain
    reps = w // 128
    return tn * _tile_lanes(cosv, reps) + _swap_halves(tn) * _tile_lanes(sinv, reps), r


def _split_dot_rhs(v, a):
    hi = _mx(v)
    lo = _mx(v - hi.astype(F32))
    return (lax.dot_general(hi, a, (NN, ((), ())), preferred_element_type=F32)
            + lax.dot_general(lo, a, (NN, ((), ())), preferred_element_type=F32))


def _sw_norm_rope_bwd(dt, tv, r, gain, seg, cosv, sinv):
    w = tv.shape[1]
    reps = w // 128
    dtn = dt * _tile_lanes(cosv, reps) + _swap_halves(dt * _tile_lanes(sinv, reps))
    u = dtn * gain
    dtv = r * u - tv * (r * r * r) * _split_dot_rhs(u * tv, seg[0:w, 0:w])
    return dtv, jnp.sum(dtn * tv * r, axis=0, keepdims=True)


def _sw_scores(qh, kp, kc):
    return _dot(qh, kp, NT), _dot(qh, kc, NT)


SW_SCALE = SW_HD ** -0.5


def _sw_probs(raw, sink, first_block):
    qi = lax.broadcasted_iota(jnp.int32, (SW_BLOCK, SW_BLOCK), 0)
    kj = lax.broadcasted_iota(jnp.int32, (SW_BLOCK, SW_BLOCK), 1)
    ok_prev = jnp.logical_and(kj > qi, jnp.logical_not(first_block))
    ok_cur = kj <= qi
    sp = jnp.where(ok_prev, raw[0], -jnp.inf)
    sc = jnp.where(ok_cur, raw[1], -jnp.inf)
    m = jnp.maximum(jnp.maximum(jnp.max(sp, axis=1, keepdims=True), jnp.max(sc, axis=1, keepdims=True)), sink)
    pp, pc = jnp.exp(sp - m), jnp.exp(sc - m)
    es = jnp.exp(sink - m)
    inv = 1.0 / (jnp.sum(pp, axis=1, keepdims=True) + jnp.sum(pc, axis=1, keepdims=True) + es)
    return pp * inv, pc * inv, es * inv


def _sw_specs(nb):
    def cur(b, n):
        return b * nb + jnp.minimum(n, nb - 1)

    def prev(b, n):
        return b * nb + jnp.maximum(jnp.minimum(n, nb - 1) - 1, 0)

    return cur, prev


def _sw_fwd(proj, rope, qg, kg, sinks, y_in, bsz, seq):
    t = proj.shape[0]
    nb = seq // SW_BLOCK
    seg = jnp.asarray(_sw_constants()[2], _MXU_DTYPE)
    cur, prev = _sw_specs(nb)

    def body(q_ref, kc_ref, kp_ref, vc_ref, vp_ref, rc_ref, rp_ref, qg_ref, kg_ref, sk_ref, seg_ref, yin_ref, y_ref):
        del yin_ref
        n = pl.program_id(1)
        segv = seg_ref[...]
        cos_c, sin_c = rc_ref[:, 0:128], rc_ref[:, 128:256]
        cos_p, sin_p = rp_ref[:, 0:128], rp_ref[:, 128:256]
        qr, _ = _sw_norm_rope(q_ref[...], qg_ref[...] * SW_SCALE, segv, cos_c, sin_c)
        kcr, _ = _sw_norm_rope(kc_ref[...], kg_ref[...], segv, cos_c, sin_c)
        kpr, _ = _sw_norm_rope(kp_ref[...], kg_ref[...], segv, cos_p, sin_p)
        vc, vp = vc_ref[...], vp_ref[...]
        ks = [slice(SW_HD * (h // SW_GROUP), SW_HD * (h // SW_GROUP + 1)) for h in range(SW_HEADS)]
        raw = [_sw_scores(qr[:, SW_HD * h:SW_HD * (h + 1)], kpr[:, ks[h]], kcr[:, ks[h]]) for h in range(SW_HEADS)]
        probs = [_sw_probs(raw[h], sk_ref[0, h], n == 0) for h in range(SW_HEADS)]
        for h in range(SW_HEADS):
            y_ref[:, SW_HD * h:SW_HD * (h + 1)] = _dot(probs[h][0], vp[:, ks[h]]) + _dot(probs[h][1], vc[:, ks[h]])

    rowq = pl.BlockSpec((SW_BLOCK, 512), lambda b, n: (cur(b, n), 0))
    full = lambda a: pl.BlockSpec(a.shape, lambda b, n: (0,) * a.ndim)
    yw = y_in.shape[1]
    return pl.pallas_call(
        body, name="swa_fwd", grid=(bsz, nb),
        in_specs=[rowq,
                  pl.BlockSpec((SW_BLOCK, 128), lambda b, n: (cur(b, n), 4)),
                  pl.BlockSpec((SW_BLOCK, 128), lambda b, n: (prev(b, n), 4)),
                  pl.BlockSpec((SW_BLOCK, 128), lambda b, n: (cur(b, n), 5)),
                  pl.BlockSpec((SW_BLOCK, 128), lambda b, n: (prev(b, n), 5)),
                  pl.BlockSpec((SW_BLOCK, 256), lambda b, n: (cur(b, n), 0)),
                  pl.BlockSpec((SW_BLOCK, 256), lambda b, n: (prev(b, n), 0)),
                  full(qg), full(kg),
                  pl.BlockSpec(memory_space=pltpu.SMEM),
                  full(seg),
                  pl.BlockSpec(memory_space=pl.ANY)],
        out_specs=pl.BlockSpec((SW_BLOCK, 512), lambda b, n: (cur(b, n), 1)),
        out_shape=jax.ShapeDtypeStruct((t, yw), F32),
        input_output_aliases={11: 0},
        compiler_params=_params(("parallel", "parallel")),
    )(proj, proj, proj, proj, proj, rope, rope, qg, kg, sinks, seg, y_in)


def _sw_bwd(proj, rope, qg, kg, sinks, y, dy, bsz, seq):
    t = proj.shape[0]
    nb = seq // SW_BLOCK
    seg = jnp.asarray(_sw_constants()[2], _MXU_DTYPE)
    cur, prev = _sw_specs(nb)

    def body(q_ref, kc_ref, kp_ref, vc_ref, vp_ref, rc_ref, rp_ref, qg_ref, kg_ref, sk_ref, seg_ref,
             y_ref, dy_ref, dp_ref, dqg_ref, dkg_ref, dsk_ref,
             dq_car, dkv_car, dqr_s, dkc_s, dkp_s, dvc_s, dvp_s, gq_acc, gk_acc, sk_acc):
        b, n = pl.program_id(0), pl.program_id(1)
        first = jnp.logical_and(b == 0, n == 0)
        last = jnp.logical_and(b == pl.num_programs(0) - 1, n == nb)

        @pl.when(first)
        def _():
            gq_acc[...] = jnp.zeros_like(gq_acc)
            gk_acc[...] = jnp.zeros_like(gk_acc)
            sk_acc[...] = jnp.zeros_like(sk_acc)

        @pl.when(n < nb)
        def _():
            segv = seg_ref[...]
            cos_c, sin_c = rc_ref[:, 0:128], rc_ref[:, 128:256]
            cos_p, sin_p = rp_ref[:, 0:128], rp_ref[:, 128:256]
            qv, kcv, kpv = q_ref[...], kc_ref[...], kp_ref[...]
            qgain = qg_ref[...] * SW_SCALE
            qr, rq = _sw_norm_rope(qv, qgain, segv, cos_c, sin_c)
            kcr, rkc = _sw_norm_rope(kcv, kg_ref[...], segv, cos_c, sin_c)
            kpr, rkp = _sw_norm_rope(kpv, kg_ref[...], segv, cos_p, sin_p)
            vc, vp = vc_ref[...], vp_ref[...]
            lane = lax.broadcasted_iota(jnp.int32, (1, 128), 1)
            dsk = jnp.zeros((1, 128), F32)
            heads = range(SW_HEADS)
            ks = [slice(SW_HD * (h // SW_GROUP), SW_HD * (h // SW_GROUP + 1)) for h in heads]
            hs = [slice(SW_HD * h, SW_HD * (h + 1)) for h in heads]
            qh = [qr[:, hs[h]] for h in heads]
            doh = [dy_ref[:, hs[h]] for h in heads]
            raw = [_sw_scores(qh[h], kpr[:, ks[h]], kcr[:, ks[h]]) for h in heads]
            dpp = [_dot(doh[h], vp[:, ks[h]], NT) for h in heads]
            dpc = [_dot(doh[h], vc[:, ks[h]], NT) for h in heads]
            probs = [_sw_probs(raw[h], sk_ref[0, h], n == 0) for h in heads]
            dsp, dsc = [], []
            for h in heads:
                pp, pc, ps = probs[h]
                delta = jnp.sum(doh[h] * y_ref[:, hs[h]], axis=1, keepdims=True)
                dsp.append(pp * (dpp[h] - delta))
                dsc.append(pc * (dpc[h] - delta))
                dsk = dsk + jnp.where(lane == h, -jnp.sum(ps * delta), 0.0)
            for h in heads:
                dqr_s[:, hs[h]] = _dot(dsp[h], kpr[:, ks[h]]) + _dot(dsc[h], kcr[:, ks[h]])
            for kv in range(SW_KV_HEADS):
                group = range(SW_GROUP * kv, SW_GROUP * (kv + 1))
                kvs = slice(SW_HD * kv, SW_HD * (kv + 1))
                dvp_s[:, kvs] = sum(_dot(probs[h][0], doh[h], TN) for h in group)
                dvc_s[:, kvs] = sum(_dot(probs[h][1], doh[h], TN) for h in group)
                dkp_s[:, kvs] = sum(_dot(dsp[h], qh[h], TN) for h in group)
                dkc_s[:, kvs] = sum(_dot(dsc[h], qh[h], TN) for h in group)
            dq, gq = _sw_norm_rope_bwd(dqr_s[...], qv, rq, qgain, segv, cos_c, sin_c)
            dkc, gkc = _sw_norm_rope_bwd(dkc_s[...], kcv, rkc, kg_ref[...], segv, cos_c, sin_c)
            dkp, gkp = _sw_norm_rope_bwd(dkp_s[...], kpv, rkp, kg_ref[...], segv, cos_p, sin_p)
            gq_acc[...] += gq
            gk_acc[...] += gkc + gkp
            sk_acc[...] += dsk

            @pl.when(n > 0)
            def _():
                dp_ref[:, 0:512] = _mx(dq_car[...])
                dp_ref[:, 512:640] = _mx(dkv_car[:, 0:128] + dkp)
                dp_ref[:, 640:768] = _mx(dkv_car[:, 128:256] + dvp_s[...])

            dq_car[...] = dq
            dkv_car[:, 0:128] = dkc
            dkv_car[:, 128:256] = dvc_s[...]

        @pl.when(n == nb)
        def _():
            dp_ref[:, 0:512] = _mx(dq_car[...])
            dp_ref[:, 512:768] = _mx(dkv_car[...])

        @pl.when(last)
        def _():
            gq = gq_acc[...] * SW_SCALE
            acc = gq[:, 0:SW_HD]
            for h in range(1, SW_HEADS):
                acc = acc + gq[:, SW_HD * h:SW_HD * (h + 1)]
            dqg_ref[...] = acc
            gk = gk_acc[...]
            dkg_ref[...] = gk[:, 0:SW_HD] + gk[:, SW_HD:2 * SW_HD]
            dsk_ref[...] = sk_acc[...]

    rowq = pl.BlockSpec((SW_BLOCK, 512), lambda b, n: (cur(b, n), 0))
    full = lambda a: pl.BlockSpec(a.shape, lambda b, n: (0,) * a.ndim)

    def out_row(b, n):
        return b * nb + jnp.maximum(n - 1, 0)

    return pl.pallas_call(
        body, name="swa_bwd", grid=(bsz, nb + 1),
        in_specs=[rowq,
                  pl.BlockSpec((SW_BLOCK, 128), lambda b, n: (cur(b, n), 4)),
                  pl.BlockSpec((SW_BLOCK, 128), lambda b, n: (prev(b, n), 4)),
                  pl.BlockSpec((SW_BLOCK, 128), lambda b, n: (cur(b, n), 5)),
                  pl.BlockSpec((SW_BLOCK, 128), lambda b, n: (prev(b, n), 5)),
                  pl.BlockSpec((SW_BLOCK, 256), lambda b, n: (cur(b, n), 0)),
                  pl.BlockSpec((SW_BLOCK, 256), lambda b, n: (prev(b, n), 0)),
                  full(qg), full(kg),
                  pl.BlockSpec(memory_space=pltpu.SMEM),
                  full(seg),
                  pl.BlockSpec((SW_BLOCK, 512), lambda b, n: (cur(b, n), 1)),
                  pl.BlockSpec((SW_BLOCK, 512), lambda b, n: (cur(b, n), 1))],
        out_specs=(pl.BlockSpec((SW_BLOCK, SW_COLS), lambda b, n: (out_row(b, n), 0)),
                   pl.BlockSpec((1, SW_HD), lambda b, n: (0, 0)),
                   pl.BlockSpec((1, SW_HD), lambda b, n: (0, 0)),
                   pl.BlockSpec((1, 128), lambda b, n: (0, 0))),
        out_shape=(jax.ShapeDtypeStruct((t, SW_COLS), _MXU_DTYPE),
                   jax.ShapeDtypeStruct((1, SW_HD), F32),
                   jax.ShapeDtypeStruct((1, SW_HD), F32),
                   jax.ShapeDtypeStruct((1, 128), F32)),
        scratch_shapes=[pltpu.VMEM((SW_BLOCK, 512), F32), pltpu.VMEM((SW_BLOCK, 256), F32),
                        pltpu.VMEM((SW_BLOCK, 512), F32),
                        pltpu.VMEM((SW_BLOCK, 128), F32), pltpu.VMEM((SW_BLOCK, 128), F32),
                        pltpu.VMEM((SW_BLOCK, 128), F32), pltpu.VMEM((SW_BLOCK, 128), F32),
                        pltpu.VMEM((1, 512), F32), pltpu.VMEM((1, 128), F32), pltpu.VMEM((1, 128), F32)],
        compiler_params=_params(("arbitrary", "arbitrary")),
    )(proj, proj, proj, proj, proj, rope, rope, qg, kg, sinks, seg, y, dy)


def _head_rms(tv, gain):
    r = lax.rsqrt(jnp.mean(tv * tv, axis=1, keepdims=True) + EPS)
    return tv * r * gain, r


def _head_rms_bwd(dtn, tv, r, gain):
    u = dtn * gain
    return r * u - tv * (r * r * r) * jnp.mean(u * tv, axis=1, keepdims=True), jnp.sum(dtn * tv * r, axis=0, keepdims=True)


def _xa_softmax(raw):
    s = raw * (XA_HD ** -0.5)
    e = jnp.exp(s - jnp.max(s, axis=1, keepdims=True))
    return e * (1.0 / jnp.sum(e, axis=1, keepdims=True))


def _xa_fwd(qx, kvx, qg, kg, bsz, seq, mlen, *, tq=512):
    t = qx.shape[0]
    tq = min(tq, seq)
    nq = seq // tq
    w = XA_HEADS * XA_HD

    def body(q_ref, kv_ref, qg_ref, kg_ref, o_ref):
        heads = range(XA_HEADS)
        hs = [slice(XA_HD * h, XA_HD * (h + 1)) for h in heads]
        qn = [_head_rms(q_ref[:, hs[h]], qg_ref[...])[0] for h in heads]
        kn = [_head_rms(kv_ref[:, hs[h]], kg_ref[...])[0] for h in heads]
        raw = [_dot(qn[h], kn[h], NT) for h in heads]
        p = [_xa_softmax(raw[h]) for h in heads]
        for h in heads:
            o_ref[:, hs[h]] = _dot(p[h], kv_ref[:, w + XA_HD * h:w + XA_HD * (h + 1)]).astype(o_ref.dtype)

    vec = pl.BlockSpec((1, XA_HD), lambda b, i: (0, 0))
    return pl.pallas_call(
        body, name="xattn_fwd", grid=(bsz, nq),
        in_specs=[pl.BlockSpec((tq, w), lambda b, i: (b * nq + i, 0)),
                  pl.BlockSpec((mlen, 2 * w), lambda b, i: (b, 0)), vec, vec],
        out_specs=pl.BlockSpec((tq, w), lambda b, i: (b * nq + i, 0)),
        out_shape=jax.ShapeDtypeStruct((t, w), _MXU_DTYPE),
        compiler_params=_params(("parallel", "parallel")),
    )(qx, kvx, qg, kg)


def _xa_bwd(qx, kvx, qg, kg, do, bsz, seq, mlen, *, tq=512):
    t = qx.shape[0]
    tq = min(tq, seq)
    nq = seq // tq
    w = XA_HEADS * XA_HD
    scale = XA_HD ** -0.5

    def body(q_ref, kv_ref, qg_ref, kg_ref, do_ref, dq_ref, dkv_ref, dqg_ref, dkg_ref):
        b, i = pl.program_id(0), pl.program_id(1)

        @pl.when(jnp.logical_and(b == 0, i == 0))
        def _():
            dqg_ref[...] = jnp.zeros_like(dqg_ref)
            dkg_ref[...] = jnp.zeros_like(dkg_ref)

        @pl.when(i == 0)
        def _():
            dkv_ref[...] = jnp.zeros_like(dkv_ref)

        heads = range(XA_HEADS)
        hs = [slice(XA_HD * h, XA_HD * (h + 1)) for h in heads]
        vs = [slice(w + XA_HD * h, w + XA_HD * (h + 1)) for h in heads]
        qv = [q_ref[:, hs[h]] for h in heads]
        kv = [kv_ref[:, hs[h]] for h in heads]
        doh = [do_ref[:, hs[h]] for h in heads]
        qn = [_head_rms(qv[h], qg_ref[...]) for h in heads]
        kn = [_head_rms(kv[h], kg_ref[...]) for h in heads]
        raw = [_dot(qn[h][0], kn[h][0], NT) for h in heads]
        dp = [_dot(doh[h], kv_ref[:, vs[h]], NT) for h in heads]
        p = [_xa_softmax(raw[h]) for h in heads]
        ds = [p[h] * (dp[h] - jnp.sum(p[h] * dp[h], axis=1, keepdims=True)) * scale for h in heads]
        dqn = [_dot(ds[h], kn[h][0]) for h in heads]
        dkn = [_dot(ds[h], qn[h][0], TN) for h in heads]
        dvv = [_dot(p[h], doh[h], TN) for h in heads]
        gq_sum = jnp.zeros((1, XA_HD), F32)
        gk_sum = jnp.zeros((1, XA_HD), F32)
        for h in heads:
            dqv, gq = _head_rms_bwd(dqn[h], qv[h], qn[h][1], qg_ref[...])
            dkv, gk = _head_rms_bwd(dkn[h], kv[h], kn[h][1], kg_ref[...])
            dq_ref[:, hs[h]] = dqv.astype(dq_ref.dtype)
            dkv_ref[:, hs[h]] += dkv
            dkv_ref[:, vs[h]] += dvv[h]
            gq_sum = gq_sum + gq
            gk_sum = gk_sum + gk
        dqg_ref[...] += gq_sum
        dkg_ref[...] += gk_sum

    vec = pl.BlockSpec((1, XA_HD), lambda b, i: (0, 0))
    row = pl.BlockSpec((tq, w), lambda b, i: (b * nq + i, 0))
    mem = pl.BlockSpec((mlen, 2 * w), lambda b, i: (b, 0))
    return pl.pallas_call(
        body, name="xattn_bwd", grid=(bsz, nq),
        in_specs=[row, mem, vec, vec, row],
        out_specs=(row, mem, vec, vec),
        out_shape=(jax.ShapeDtypeStruct((t, w), _MXU_DTYPE), jax.ShapeDtypeStruct((bsz * mlen, 2 * w), F32),
                   jax.ShapeDtypeStruct((1, XA_HD), F32), jax.ShapeDtypeStruct((1, XA_HD), F32)),
        compiler_params=_params(("arbitrary", "arbitrary")),
    )(qx, kvx, qg, kg, do)


def _loss_finish(sq_row, d_model):
    def body(s_ref, o_ref):
        o_ref[...] = jnp.zeros_like(o_ref) + 0.5 * jnp.sum(s_ref[...]) / float(d_model)

    return pl.pallas_call(body, name="loss_finish", out_shape=jax.ShapeDtypeStruct((1, 128), F32))(sq_row)


def _adamw_math(w, g, m, v):
    m = ADAM_B1 * m + (1.0 - ADAM_B1) * g
    v = ADAM_B2 * v + (1.0 - ADAM_B2) * (g * g)
    m_hat = m / (1.0 - ADAM_B1 ** ADAM_STEP)
    v_hat = v / (1.0 - ADAM_B2 ** ADAM_STEP)
    return -ADAM_LR * (m_hat / (jnp.sqrt(v_hat) + ADAM_EPS) + ADAM_WD * w), m, v


def _adamw_big(w, g, m, v, *, name, tr=512):
    r, c = w.shape
    tr = min(tr, r)
    if r % tr:
        tr = r // 2
    assert r % tr == 0 and tr % 8 == 0, (name, r, tr)

    def body(w_ref, g_ref, m_ref, v_ref, go_ref, d_ref, mo_ref, vo_ref):
        gv = g_ref[...]
        d, mn, vn = _adamw_math(w_ref[...], gv, m_ref[...], v_ref[...])
        go_ref[...] = gv
        d_ref[...] = d
        mo_ref[...] = mn
        vo_ref[...] = vn

    spec = pl.BlockSpec((tr, c), lambda i: (i, 0))
    shp = jax.ShapeDtypeStruct((r, c), F32)
    return pl.pallas_call(
        body, name=name, grid=(r // tr,), in_specs=[spec] * 4, out_specs=(spec,) * 4, out_shape=(shp,) * 4,
        compiler_params=_params(("parallel",)),
    )(w, g, m, v)


def _adamw_small(ws, gs, ms, vs):
    n = len(ws)

    def body(*refs):
        for i in range(n):
            d, mn, vn = _adamw_math(refs[i][...], refs[n + i][...], refs[2 * n + i][...], refs[3 * n + i][...])
            refs[4 * n + i][...] = d
            refs[5 * n + i][...] = mn
            refs[6 * n + i][...] = vn

    shapes = tuple(jax.ShapeDtypeStruct(w.shape, F32) for w in ws)
    return pl.pallas_call(body, name="adamw_small", out_shape=shapes * 3)(*ws, *gs, *ms, *vs)


def _add_halves(gs, recvs, c_idx, *, name):
    n = len(gs)

    def body(c_ref, *refs):
        del c_ref
        for a in range(n):
            refs[2 * n + a][...] = refs[a][...] + refs[n + a][...]

    def half(g):
        return pl.BlockSpec((None, g.shape[1] // 2, g.shape[2]), lambda k, cr: (k, cr[0], 0))

    def whole(g):
        return pl.BlockSpec((None, g.shape[1] // 2, g.shape[2]), lambda k, cr: (k, 0, 0))

    return pl.pallas_call(
        body, name=name,
        grid_spec=pltpu.PrefetchScalarGridSpec(
            num_scalar_prefetch=1, grid=(4,),
            in_specs=[half(g) for g in gs] + [whole(g) for g in gs],
            out_specs=tuple(whole(g) for g in gs)),
        out_shape=tuple(jax.ShapeDtypeStruct((4, g.shape[1] // 2, g.shape[2]), F32) for g in gs),
        compiler_params=_params(("parallel",)),
    )(c_idx, *gs, *recvs)


def _add_chips(ps, recvs, place_idx, *, name, steps=2, after=()):
    n = len(ps)

    def body(pi_ref, *refs):
        del pi_ref
        outs = refs[2 * n + len(after):]
        for a in range(n):
            r_ref = refs[n + a]
            outs[a][...] = ((refs[a][...] + r_ref[0]) + r_ref[1]) + r_ref[2]

    def tile(p):
        assert p.shape[1] % (8 * steps) == 0, (name, p.shape)
        return p.shape[1] // steps

    return pl.pallas_call(
        body, name=name,
        grid_spec=pltpu.PrefetchScalarGridSpec(
            num_scalar_prefetch=1, grid=(steps,),
            in_specs=[pl.BlockSpec((None, tile(p), p.shape[2]), lambda i, pi: (pi[0], i, 0)) for p in ps]
            + [pl.BlockSpec((3, tile(p), p.shape[2]), lambda i, pi: (0, i, 0)) for p in ps]
            + [pl.BlockSpec(memory_space=pl.ANY)] * len(after),
            out_specs=tuple(pl.BlockSpec((tile(p), p.shape[2]), lambda i, pi: (pi[1] * steps + i, 0)) for p in ps)),
        out_shape=tuple(jax.ShapeDtypeStruct((2 * p.shape[1], p.shape[2]), F32) for p in ps),
        compiler_params=_params(("parallel",)),
    )(place_idx, *ps, *recvs, *after)


def _place_shards(shards, place_idx, *, name, after=()):
    n = len(shards)

    def body(pi_ref, *refs):
        del pi_ref
        for i in range(n):
            refs[n + len(after) + i][...] = refs[i][...]

    return pl.pallas_call(
        body, name=name,
        grid_spec=pltpu.PrefetchScalarGridSpec(
            num_scalar_prefetch=1, grid=(1,),
            in_specs=[pl.BlockSpec(s.shape, lambda i, pi: (0, 0)) for s in shards]
            + [pl.BlockSpec(memory_space=pl.ANY)] * len(after),
            out_specs=tuple(pl.BlockSpec((None,) + s.shape, lambda i, pi: (pi[0], 0, 0)) for s in shards)),
        out_shape=tuple(jax.ShapeDtypeStruct((4,) + s.shape, s.dtype) for s in shards),
        compiler_params=_params(("arbitrary",)),
    )(place_idx, *shards, *after)


def _place_shard(shard, place_idx, *, name, tr=512, after=()):
    r, c = shard.shape
    tr = min(tr, r)
    if r % tr:
        tr = r // 2
    assert r % tr == 0 and tr % 16 == 0, (name, r, tr)

    def body(pi_ref, s_ref, *rest):
        del pi_ref
        rest[-1][...] = s_ref[...]

    return pl.pallas_call(
        body, name=name,
        grid_spec=pltpu.PrefetchScalarGridSpec(
            num_scalar_prefetch=1, grid=(r // tr,),
            in_specs=[pl.BlockSpec((tr, c), lambda i, pi: (i, 0))] + [pl.BlockSpec(memory_space=pl.ANY)] * len(after),
            out_specs=pl.BlockSpec((None, tr, c), lambda i, pi: (pi[0], i, 0))),
        out_shape=jax.ShapeDtypeStruct((4, r, c), shard.dtype),
        compiler_params=_params(("parallel",)),
    )(place_idx, shard, *after)


def _place():
    x, y, c = lax.axis_index("x"), lax.axis_index("y"), lax.axis_index("c")
    chips = [(1 - x, y), (x, 1 - y), (1 - x, 1 - y)]
    return x, y, c, chips


ANY = pl.BlockSpec(memory_space=pl.ANY)


def _exchange_halves(grads, name):
    n = len(grads)

    def body(*refs):
        ins, outs = refs[:n], refs[n:2 * n]
        send_sems, recv_sems = refs[2 * n:]
        x, y, c, _ = _place()

        def copy(a):
            h = ins[a].shape[1] // 2
            return pltpu.make_async_remote_copy(
                src_ref=ins[a].at[:, pl.ds((1 - c) * h, h), :], dst_ref=outs[a],
                send_sem=send_sems.at[a], recv_sem=recv_sems.at[a], device_id=(x, y, 1 - c), device_id_type=MESH)

        for a in range(n):
            copy(a).start()
        for a in range(n):
            copy(a).wait_recv()
        for a in range(n):
            copy(a).wait_send()

    return pl.pallas_call(
        body, name=name,
        in_specs=[ANY] * n, out_specs=tuple([ANY] * n),
        out_shape=tuple(jax.ShapeDtypeStruct((4, g.shape[1] // 2, g.shape[2]), g.dtype) for g in grads),
        scratch_shapes=[pltpu.SemaphoreType.DMA((n,)), pltpu.SemaphoreType.DMA((n,))],
    )(*grads)


HBM = pl.BlockSpec(memory_space=pltpu.HBM)
SEM = pl.BlockSpec(memory_space=pltpu.SEMAPHORE)
EFFECT = pltpu.SideEffectType.DATAFLOW_SIDE_EFFECTING


def _in_hbm(a):
    return pltpu.with_memory_space_constraint(a, pltpu.HBM)


def _split_copy_calls(name, srcs, lands, n_copies, make_copies):
    ns, nl = len(srcs), len(lands)
    nb = ns + nl

    def start(after=()):
        n_after = len(after)

        def body(*refs):
            outs = refs[nb + n_after:]
            copies = make_copies(refs[:ns], refs[ns:nb], outs[0], outs[1])
            for cp in copies:
                cp.start()
            token = refs[-1]
            token[...] = jnp.zeros_like(token)

        bufs = [_in_hbm(a) for a in list(srcs) + list(lands)]
        out = pl.pallas_call(
            body, name=name + "_start",
            out_shape=(pltpu.SemaphoreType.DMA((n_copies,)), pltpu.SemaphoreType.DMA((n_copies,)),
                       *[pltpu.HBM(a.shape, a.dtype) for a in bufs], jax.ShapeDtypeStruct((8, 128), F32)),
            in_specs=[HBM] * nb + [pl.BlockSpec(memory_space=pl.ANY)] * n_after,
            out_specs=(SEM, SEM, *[HBM] * nb, pl.BlockSpec(memory_space=pltpu.VMEM)),
            input_output_aliases={i: 2 + i for i in range(nb)},
            compiler_params=pltpu.CompilerParams(has_side_effects=EFFECT),
        )(*bufs, *after)
        return dict(send=out[0], recv=out[1], bufs=list(out[2:2 + nb]), token=out[-1])

    def wait(state, after):
        def body(*refs):
            copies = make_copies(refs[:ns], refs[ns:nb], refs[nb], refs[nb + 1])
            for cp in copies:
                cp.wait_send()
            for cp in copies:
                cp.wait_recv()

        bufs = state["bufs"]
        out = pl.pallas_call(
            body, name=name + "_wait",
            out_shape=tuple(pltpu.HBM(a.shape, a.dtype) for a in bufs),
            in_specs=[HBM] * nb + [SEM, SEM] + [pl.BlockSpec(memory_space=pl.ANY)] * len(after),
            out_specs=tuple([HBM] * nb),
            input_output_aliases={i: i for i in range(nb)},
            compiler_params=pltpu.CompilerParams(has_side_effects=EFFECT),
        )(*bufs, state["send"], state["recv"], *after)
        return list(out[:ns]), list(out[ns:])

    return start, wait


def _scatter_chips_split(name, parts):
    n = len(parts)
    lands = [lax.empty((3,) + p.shape[1:], p.dtype) for p in parts]

    def make_copies(srcs, lnds, send_sems, recv_sems):
        _, _, c, chips = _place()
        return [pltpu.make_async_remote_copy(
            src_ref=srcs[a].at[2 * px + py], dst_ref=lnds[a].at[j], send_sem=send_sems.at[a * 3 + j],
            recv_sem=recv_sems.at[a * 3 + j], device_id=(px, py, c), device_id_type=MESH)
            for a in range(n) for j, (px, py) in enumerate(chips)]

    return _split_copy_calls(name, parts, lands, 3 * n, make_copies)


def _exchange_halves_split(name, grads):
    n = len(grads)
    lands = [lax.empty((4, g.shape[1] // 2, g.shape[2]), g.dtype) for g in grads]

    def make_copies(srcs, lnds, send_sems, recv_sems):
        x, y, c, _ = _place()
        out = []
        for a in range(n):
            h = srcs[a].shape[1] // 2
            out.append(pltpu.make_async_remote_copy(
                src_ref=srcs[a].at[:, pl.ds((1 - c) * h, h), :], dst_ref=lnds[a], send_sem=send_sems.at[a],
                recv_sem=recv_sems.at[a], device_id=(x, y, 1 - c), device_id_type=MESH))
        return out

    return _split_copy_calls(name, grads, lands, n, make_copies)


def _gather_chips_split(name, shards, lands):
    n = len(shards)

    def make_copies(srcs, lnds, send_sems, recv_sems):
        x, y, c, chips = _place()
        out = []
        for a in range(n):
            h = srcs[a].shape[0] // 2
            for j, (px, py) in enumerate(chips):
                out.append(pltpu.make_async_remote_copy(
                    src_ref=srcs[a].at[pl.ds(c * h, h), :], dst_ref=lnds[a].at[2 * x + y, pl.ds(c * h, h), :],
                    send_sem=send_sems.at[a * 3 + j], recv_sem=recv_sems.at[a * 3 + j],
                    device_id=(px, py, c), device_id_type=MESH))
        return out

    return _split_copy_calls(name, shards, lands, 3 * n, make_copies)


def _gather_finish(gathered, name):
    n = len(gathered)

    def body(*refs):
        outs = refs[n:2 * n]
        send_sems, recv_sems = refs[2 * n:]
        x, y, c, chips = _place()

        def copy(a, j, chip_idx, which):
            h = outs[a].shape[1] // 2
            rows = outs[a].at[chip_idx, pl.ds(which * h, h), :]
            return pltpu.make_async_remote_copy(
                src_ref=rows, dst_ref=rows, send_sem=send_sems.at[a * 3 + j], recv_sem=recv_sems.at[a * 3 + j],
                device_id=(x, y, 1 - c), device_id_type=MESH)

        for a in range(n):
            for j, (px, py) in enumerate(chips):
                copy(a, j, 2 * px + py, c).start()
        for a in range(n):
            for j, (px, py) in enumerate(chips):
                copy(a, j, 2 * px + py, 1 - c).wait_recv()
        for a in range(n):
            for j, (px, py) in enumerate(chips):
                copy(a, j, 2 * px + py, c).wait_send()

    return pl.pallas_call(
        body, name=name,
        in_specs=[ANY] * n, out_specs=tuple([ANY] * n),
        out_shape=tuple(jax.ShapeDtypeStruct(g.shape, g.dtype) for g in gathered),
        input_output_aliases={i: i for i in range(n)},
        scratch_shapes=[pltpu.SemaphoreType.DMA((3 * n,)), pltpu.SemaphoreType.DMA((3 * n,))],
    )(*gathered)


def _gather_forward_split(name, gathered):
    n = len(gathered)

    def make_copies(srcs, lnds, send_sems, recv_sems):
        x, y, c, chips = _place()
        out = []
        for a in range(n):
            h = lnds[a].shape[1] // 2
            for j, (px, py) in enumerate(chips):
                rows = lnds[a].at[2 * px + py, pl.ds(c * h, h), :]
                out.append(pltpu.make_async_remote_copy(
                    src_ref=rows, dst_ref=rows, send_sem=send_sems.at[a * 3 + j], recv_sem=recv_sems.at[a * 3 + j],
                    device_id=(x, y, 1 - c), device_id_type=MESH))
        return out

    return _split_copy_calls(name, [], gathered, 3 * n, make_copies)


def _join_halves_split(name, fulls):
    n = len(fulls)

    def make_copies(srcs, lnds, send_sems, recv_sems):
        x, y, c, _ = _place()
        out = []
        for a in range(n):
            h = lnds[a].shape[0] // 2
            rows = lnds[a].at[pl.ds(c * h, h), :]
            out.append(pltpu.make_async_remote_copy(
                src_ref=rows, dst_ref=rows, send_sem=send_sems.at[a], recv_sem=recv_sems.at[a],
                device_id=(x, y, 1 - c), device_id_type=MESH))
        return out

    return _split_copy_calls(name, [], fulls, n, make_copies)


def _all_gather_small_split(sm):
    r, w = sm.shape

    def make_copies(srcs, lnds, send_sems, recv_sems):
        x, y, c, _ = _place()
        me = 4 * x + 2 * y + c
        rel = [(dx, dy, dc) for dx in (0, 1) for dy in (0, 1) for dc in (0, 1)][1:]
        return [pltpu.make_async_remote_copy(
            src_ref=srcs[0], dst_ref=lnds[0].at[me], send_sem=send_sems.at[k], recv_sem=recv_sems.at[k],
            device_id=(1 - x if dx else x, 1 - y if dy else y, 1 - c if dc else c), device_id_type=MESH)
            for k, (dx, dy, dc) in enumerate(rel)]

    return _split_copy_calls("all_gather_small", [sm], [lax.empty((8, r, w), sm.dtype)], 7, make_copies)


def _sum_devices(sm, gathered, me_idx):
    def body(me_ref, sm_ref, g_ref, o_ref):
        own = sm_ref[...]
        acc = jnp.where(me_ref[0] == 0, own, g_ref[0])
        for d in range(1, 8):
            acc = acc + jnp.where(me_ref[0] == d, own, g_ref[d])
        o_ref[...] = acc

    vm = pl.BlockSpec(memory_space=pltpu.VMEM)
    return pl.pallas_call(
        body, name="sum_devices", in_specs=[pl.BlockSpec(memory_space=pltpu.SMEM), vm, vm], out_specs=vm,
        out_shape=jax.ShapeDtypeStruct(sm.shape, F32),
    )(me_idx, sm, gathered)


def _local_step(x3, mem3, pos2, target3, small, comm):
    bsz, seq, d = x3.shape
    mlen = mem3.shape[1]
    t = bsz * seq
    tok = comm.begin()
    x = x3.reshape(t, d)
    mem = mem3.reshape(bsz * mlen, d)
    target = target3.reshape(t, d)
    rope = _rope_table(pos2.reshape(t, 1), after=tok)
    qg_t = jnp.tile(small["sw_q_norm_g"], (1, SW_HEADS))
    kg_t = jnp.tile(small["sw_k_norm_g"], (1, SW_KV_HEADS))

    hn1 = _rms_fwd(x, small["norm1_g"], name="rms1_fwd", after=tok)
    w = comm.first((hn1, rope))
    w_in_t = w["w_in_t"]
    w_sw_t = w_in_t[HG_COLS:]
    proj_hg = _mm(hn1, w_in_t, NT, t, HG_COLS, d, name="proj_hg", tk=d, after=(w.get("token"),))[0]
    proj_sw = _mm(hn1, w_sw_t, NT, t, SW_COLS, d, name="proj_sw", tk=d)[0]
    y_mix, o_hg, states = _hg_fwd(proj_hg, small["hg_lower_bounds"], small["hg_norm_g"], bsz, seq, y_width=1024)
    y_mix = _sw_fwd(proj_sw, rope, qg_t, kg_t, small["sw_sinks"], y_mix, bsz, seq)
    w = comm.rest(y_mix)
    h1, hn2 = _mm(y_mix, w["w_out"], NN, t, d, 1024, name="out_proj", tk=1024, extras=(x,), rows=(small["norm2_g"],),
                  epilogue=_residual_rms, out_dtypes=(F32, _MXU_DTYPE), after=(w.get("token"),))
    mn = _rms_fwd(mem, small["mem_norm_g"], name="rms_mem_fwd")
    qx = _mm(hn2, w["wq"], NN, t, 512, d, name="xa_q", tk=d)[0]
    kvx = _mm(mn, w["wkv"], NN, bsz * mlen, 1024, d, name="xa_kv", tk=d)[0]
    ox = _xa_fwd(qx, kvx, small["xa_q_norm_g"], small["xa_k_norm_g"], bsz, seq, mlen)
    h2, hn3 = _mm(ox, w["wo"], NN, t, d, 512, name="xa_o", tk=512, extras=(h1,), rows=(small["norm3_g"],),
                  epilogue=_residual_rms, out_dtypes=(F32, _MXU_DTYPE))
    w = {**w, **comm.mlp(hn3)}
    ff = w["down"].shape[0]
    ffs = ff // 4

    def relu_sq(acc):
        a = jnp.maximum(acc, 0.0)
        return a, a * a

    act, act2 = _mm(hn3, w["up"], NN, t, ff, d, name="mlp_up", tm=2048, tn=ffs, tk=d,
                    b_spec=pl.BlockSpec((None, d, ffs), lambda i, j, kk: (j, 0, 0)),
                    epilogue=relu_sq, out_dtypes=(_MXU_DTYPE, _MXU_DTYPE))
    inv_d = 1.0 / d

    def loss_cotangent(acc, res, tgt):
        diff = acc + res - tgt
        v = diff * inv_d
        return v, v, jnp.sum(diff * diff, axis=0, keepdims=True)

    dy, dy_mx, sq_row = _mm(act2, w["down"], NN, t, d, ff, name="mlp_down", tk=2048, extras=(h2, target),
                            epilogue=loss_cotangent, out_dtypes=(F32, _MXU_DTYPE), row_sums=1)
    loss_row = _loss_finish(sq_row, d)

    dz = _mm(dy_mx, w["down"], NT, t, ff, d, name="d_act", tm=2048, tk=d, extras=(act,),
             epilogue=lambda acc, a: (acc * (2.0 * a.astype(F32)),), out_dtypes=(_MXU_DTYPE,))[0]
    g_down = _mm(act2, dy_mx, TN, ff, d, t, name="g_down", tk=t)[0]
    g_up = _mm(hn3, dz, TN, d, ff, t, name="g_up", tn=ffs, tk=t,
               out_shape=(jax.ShapeDtypeStruct((4, d, ffs), F32),),
               out_spec=(pl.BlockSpec((None, min(1024, d), ffs), lambda i, j, kk: (j, i, 0)),))[0]
    tok = comm.grads("mlp", dict(up=g_up, down=g_down))
    dh2, dh2_mx, g_norm3 = _mm(dz, w["up"], NT, t, d, ff, name="d_hn3", tk=ffs, after=tok,
                               b_spec=pl.BlockSpec((None, min(1024, d), ffs), lambda i, j, kk: (kk, j, 0)),
                               extras=(h2, dy), rows=(small["norm3_g"],), epilogue=_rms_bwd_residual,
                               out_dtypes=(F32, _MXU_DTYPE), row_sums=1)
    d_ox = _mm(dh2_mx, w["wo"], NT, t, 512, d, name="d_ox", tk=d)[0]
    g_wo = _mm(ox, dh2_mx, TN, 512, d, t, name="g_wo", tk=t)[0]
    d_qx, d_kvx, g_xq, g_xk = _xa_bwd(qx, kvx, small["xa_q_norm_g"], small["xa_k_norm_g"], d_ox, bsz, seq, mlen)
    g_wq = _mm(hn2, d_qx, TN, d, 512, t, name="g_wq")[0]
    g_wkv = _mm(mn, d_kvx, TN, d, 1024, bsz * mlen, name="g_wkv")[0]
    dh1, dh1_mx, g_norm2 = _mm(d_qx, w["wq"], NT, t, d, 512, name="d_hn2", tk=512, extras=(h1, dh2),
                               rows=(small["norm2_g"],), epilogue=_rms_bwd_residual, out_dtypes=(F32, _MXU_DTYPE),
                               row_sums=1)
    dmn = _mm(d_kvx, w["wkv"], NT, bsz * mlen, d, 1024, name="d_mn", tk=1024)[0]
    g_memn = _rms_gain_grad(mem, small["mem_norm_g"], dmn, name="rms_mem_bwd")
    g_wout = _mm(y_mix, dh1_mx, TN, 1024, d, t, name="g_wout", tk=2048)[0]
    tok = comm.grads("mid", dict(w_out=g_wout, wq=g_wq, wkv=g_wkv, wo=g_wo))
    d_mix = _mm(dh1_mx, w["w_out"], NT, t, 1024, d, name="d_mix", tk=d, after=tok)[0]
    dproj_sw, g_swq, g_swk, g_sinks = _sw_bwd(proj_sw, rope, qg_t, kg_t, small["sw_sinks"], y_mix, d_mix, bsz, seq)
    tok = comm.poll(dproj_sw)
    dproj_hg, g_lb, g_hgn = _hg_bwd(proj_hg, small["hg_lower_bounds"], small["hg_norm_g"], o_hg, states, d_mix, bsz, seq,
                                    after=tok)
    in_rows = HG_COLS + SW_COLS
    sw_tile = 256
    g_in_t = _mm(dproj_hg, hn1, TN, HG_COLS, d, t, name="g_in_hg", tk=t,
                 out_shape=(jax.ShapeDtypeStruct((in_rows, d), F32),),
                 out_spec=(pl.BlockSpec((1024, min(1024, d)), lambda i, j, kk: (i, j)),))[0]
    g_in_t = _mm(dproj_sw, hn1, TN, SW_COLS, d, t, name="g_in_sw", tm=sw_tile, into=g_in_t,
                 out_shape=(jax.ShapeDtypeStruct((in_rows, d), F32),),
                 out_spec=(pl.BlockSpec((sw_tile, min(1024, d)), lambda i, j, kk: (HG_COLS // sw_tile + i, j)),))[0]
    tok = comm.grads("in", dict(w_in_t=g_in_t))
    dhn1_a = _mm(dproj_hg, w_in_t, NN, t, d, HG_COLS, name="d_hn1_hg", tk=HG_COLS, after=tok)[0]
    grad_x, g_norm1 = _mm(dproj_sw, w_sw_t, NN, t, d, SW_COLS, name="d_hn1_sw", tk=SW_COLS, extras=(dhn1_a, x, dh1),
                          rows=(small["norm1_g"],), row_sums=1,
                          epilogue=lambda acc, prev, xv, dres, g: _rms_bwd_residual(acc + prev, xv, dres, g)[1:])

    g_small = dict(norm1_g=g_norm1, hg_lower_bounds=g_lb, hg_norm_g=g_hgn, sw_q_norm_g=g_swq, sw_k_norm_g=g_swk,
                   sw_sinks=g_sinks[:, 0:SW_HEADS], norm2_g=g_norm2, mem_norm_g=g_memn, xa_q_norm_g=g_xq,
                   xa_k_norm_g=g_xk, norm3_g=g_norm3)
    return loss_row, grad_x.reshape(bsz, seq, d), g_small


SMALL_NAMES = ("norm1_g", "hg_lower_bounds", "hg_norm_g", "sw_q_norm_g", "sw_k_norm_g", "sw_sinks", "norm2_g",
               "mem_norm_g", "xa_q_norm_g", "xa_k_norm_g", "norm3_g")
BIG_NAMES = ("w_in", "w_out", "xa_wq", "xa_wkv", "xa_wo", "mlp_up", "mlp_down")
WEIGHT_ORDER = ("norm1_g", "w_in", "hg_lower_bounds", "hg_norm_g", "sw_q_norm_g", "sw_k_norm_g", "sw_sinks", "w_out",
                "norm2_g", "mem_norm_g", "xa_wq", "xa_wkv", "xa_q_norm_g", "xa_k_norm_g", "xa_wo", "norm3_g",
                "mlp_up", "mlp_down")


def _pack_rows(vals, width):
    starts, at = [], 0
    for v in vals:
        starts.append(at)
        at += v.shape[0]
    total = at + (-at) % 8
    out = None
    for v, s in zip(vals, starts):
        placed = jnp.pad(v, ((s, total - s - v.shape[0]), (0, width - v.shape[1])))
        out = placed if out is None else out + placed
    return out, starts


class _MeshWeights:
    LATE = ("w_out", "xa_wq", "xa_wkv", "xa_wo", "mlp_up", "mlp_down")

    def __init__(self, shards, d, ff):
        self.shards, self.d, self.ff = shards, d, ff
        self.c_idx = lax.axis_index("c").astype(jnp.int32).reshape(1)
        chip = (2 * lax.axis_index("x") + lax.axis_index("y")).astype(jnp.int32)
        self.place_idx = jnp.stack([chip, lax.axis_index("c").astype(jnp.int32)])
        self.pending = []
        self.exchanging = None

    def begin(self):
        shard = self.shards["w_in"]
        start, self.in_wait = _gather_chips_split(
            "gather_in", [shard], [_place_shard(shard, self.place_idx, name="place_w_in")])
        self.in_state = start()
        tok = (self.in_state["token"],)
        self.placed = list(_place_shards([self.shards[n] for n in self.LATE], self.place_idx, name="place_late",
                                         after=tok))
        return tok

    def first(self, after):
        _, lands = self.in_wait(self.in_state, (*after, *self.placed))
        (g_in,) = _gather_finish(lands, "gather_in_finish")
        start, self.late_wait = _gather_chips_split("gather_late", [self.shards[n] for n in self.LATE], self.placed)
        self.late_state = start(after=(g_in,))
        return dict(w_in_t=g_in.reshape(-1, self.d), token=self.late_state["token"])

    def rest(self, after):
        _, lands = self.late_wait(self.late_state, (after,))
        g_out, g_q, g_kv, g_o = _gather_finish(lands[:4], "gather_late_finish")
        start, self.mlp_wait = _gather_forward_split("gather_mlp_forward", lands[4:])
        self.mlp_state = start(after=(g_out,))
        d = self.d
        return dict(w_out=g_out.reshape(-1, d), wq=g_q.reshape(d, -1), wkv=g_kv.reshape(d, -1),
                    wo=jnp.concatenate([g_o[k] for k in range(4)], axis=1), token=self.mlp_state["token"])

    def mlp(self, after):
        _, (g_up, g_dn) = self.mlp_wait(self.mlp_state, (after,))
        return dict(up=g_up, down=g_dn.reshape(self.ff, self.d))

    def _scatter(self, tag, names, arrays, recv):
        parts = list(_add_halves(arrays, recv, self.c_idx, name="rs_add_halves_" + tag))
        start, wait = _scatter_chips_split("rs_scatter_" + tag, parts)
        state = start()
        self.pending.append((names, wait, state))
        return state["token"]

    def _advance(self, after):
        if self.exchanging is None:
            return ()
        tag, names, wait, state = self.exchanging
        self.exchanging = None
        arrays, recv = wait(state, (after,))
        return (self._scatter(tag, names, arrays, recv),)

    def poll(self, after):
        return self._advance(after)

    def grads(self, tag, g):
        d, ff = self.d, self.ff
        if tag == "mlp":
            names, arrays = ("mlp_up", "mlp_down"), [g["up"], g["down"].reshape(4, ff // 4, d)]
        elif tag == "mid":
            names = ("w_out", "xa_wq", "xa_wkv", "xa_wo")
            ds = d // 4
            g_wo = jnp.stack([g["wo"][:, ds * k:ds * (k + 1)] for k in range(4)])
            arrays = [g["w_out"].reshape(4, -1, d), g["wq"].reshape(4, d // 4, -1), g["wkv"].reshape(4, d // 4, -1), g_wo]
        else:
            names, arrays = ("w_in",), [g["w_in_t"].reshape(4, -1, d)]
        toks = self._advance(arrays[0])
        if tag == "in":
            return toks + (self._scatter(tag, names, arrays, _exchange_halves(arrays, "rs_exchange_" + tag)),)
        start, wait = _exchange_halves_split("rs_exchange_" + tag, arrays)
        state = start()
        self.exchanging = (tag, names, wait, state)
        return toks + (state["token"],)

    def finish(self, after):
        joins, tok = [], ()
        for names, wait, state in self.pending:
            srcs, lands = wait(state, tuple(after) + tok)
            fulls = list(_add_chips(srcs, lands, self.place_idx, name="rs_add_chips_" + names[0], after=tok))
            start, jwait = _join_halves_split("rs_join_" + names[0], fulls)
            jstate = start()
            tok = (jstate["token"],)
            joins.append((names, jwait, jstate))
        out = {}
        for names, jwait, jstate in joins:
            _, fulls = jwait(jstate, tok)
            out.update(zip(names, fulls))
        return out


def kernel(x, mem, positions, norm1_g, w_in, hg_lower_bounds, hg_norm_g, sw_q_norm_g, sw_k_norm_g, sw_sinks, w_out, norm2_g, mem_norm_g, xa_wq, xa_wkv, xa_q_norm_g, xa_k_norm_g, xa_wo, norm3_g, mlp_up, mlp_down, loss_target, m_norm1_g, m_w_in, m_hg_lower_bounds, m_hg_norm_g, m_sw_q_norm_g, m_sw_k_norm_g, m_sw_sinks, m_w_out, m_norm2_g, m_mem_norm_g, m_xa_wq, m_xa_wkv, m_xa_q_norm_g, m_xa_k_norm_g, m_xa_wo, m_norm3_g, m_mlp_up, m_mlp_down, v_norm1_g, v_w_in, v_hg_lower_bounds, v_hg_norm_g, v_sw_q_norm_g, v_sw_k_norm_g, v_sw_sinks, v_w_out, v_norm2_g, v_mem_norm_g, v_xa_wq, v_xa_wkv, v_xa_q_norm_g, v_xa_k_norm_g, v_xa_wo, v_norm3_g, v_mlp_up, v_mlp_down):
    given = dict(locals())
    weights = {n: given[n] for n in WEIGHT_ORDER}
    moms = {n: given["m_" + n] for n in WEIGHT_ORDER}
    vars_ = {n: given["v_" + n] for n in WEIGHT_ORDER}
    d = x.shape[-1]
    ff = mlp_down.shape[1] * 4
    small = {n: weights[n] for n in SMALL_NAMES}

    def plain(n, a):
        return jnp.swapaxes(a[0], 0, 1) if n == "w_in" else a[0]

    comm = _MeshWeights({n: plain(n, weights[n]).astype(_MXU_DTYPE) for n in BIG_NAMES}, d, ff)
    loss_row, grad_x, g_small = _local_step(x, mem, positions, loss_target, small, comm)
    packed, starts = _pack_rows([g_small[n] for n in SMALL_NAMES] + [loss_row], 1024)
    start, wait = _all_gather_small_split(packed)
    state = start()
    big_grads = comm.finish((grad_x, state["token"]))
    (own,), (gathered,) = wait(state, (big_grads[BIG_NAMES[0]],))
    device = (4 * lax.axis_index("x") + 2 * lax.axis_index("y") + lax.axis_index("c")).astype(jnp.int32).reshape(1)
    summed = _sum_devices(own, gathered, device)
    small_grads = {}
    for n, s in zip(SMALL_NAMES, starts):
        r, c = weights[n].shape
        small_grads[n] = summed[s:s + r, 0:c]
    loss = summed[starts[-1], 0]

    grads, deltas, new_m, new_v = {}, {}, {}, {}
    for n in BIG_NAMES:
        outs = _adamw_big(plain(n, weights[n]), big_grads[n], plain(n, moms[n]), plain(n, vars_[n]), name="adamw_" + n)
        grads[n], deltas[n], new_m[n], new_v[n] = ((jnp.swapaxes(a, 0, 1) if n == "w_in" else a)[None] for a in outs)
    sm_out = _adamw_small([weights[n] for n in SMALL_NAMES], [small_grads[n] for n in SMALL_NAMES],
                          [moms[n] for n in SMALL_NAMES], [vars_[n] for n in SMALL_NAMES])
    ns = len(SMALL_NAMES)
    for i, n in enumerate(SMALL_NAMES):
        grads[n], deltas[n], new_m[n], new_v[n] = small_grads[n], sm_out[i], sm_out[ns + i], sm_out[2 * ns + i]

    return (loss, grad_x, *[grads[n] for n in WEIGHT_ORDER], *[deltas[n] for n in WEIGHT_ORDER],
            *[new_m[n] for n in WEIGHT_ORDER], *[new_v[n] for n in WEIGHT_ORDER])
```

```python
import numpy as np
import jax
import jax.numpy as jnp
from jax import lax
from jax.experimental import pallas as pl
from jax.experimental.pallas import tpu as pltpu

F32 = jnp.float32
_MXU_DTYPE = jnp.bfloat16

EPS = 1e-6
HG_HEADS = 4
HG_D = 128
HG_CHUNK = 64
HG_TILE = 512
HG_LEVELS = (32, 16, 8, 4, 2, 1)
SW_HEADS = 8
SW_KV_HEADS = 2
SW_GROUP = SW_HEADS // SW_KV_HEADS
SW_HD = 64
SW_BLOCK = 128
ROPE_THETA = 500000.0
ROT_DIM = SW_HD // 4
XA_HEADS = 4
XA_HD = 128
HG_COLS = 4 * HG_HEADS * HG_D
SW_COLS = (SW_HEADS + 2 * SW_KV_HEADS) * SW_HD

ADAM_LR = 0.001
ADAM_B1 = 0.9
ADAM_B2 = 0.999
ADAM_EPS = 1e-08
ADAM_WD = 0.01
ADAM_STEP = 10

VMEM_LIMIT = 56 * 1024 * 1024
MESH = pl.DeviceIdType.MESH

NN = ((1,), (0,))
NT = ((1,), (1,))
TN = ((0,), (0,))


def _mx(v):
    return v.astype(_MXU_DTYPE)


def _dot(a, b, dims=NN):
    return lax.dot_general(_mx(a), _mx(b), (dims, ((), ())), preferred_element_type=F32)


def _split_dot(a, v, dims, parts):
    acc = None
    rest = v
    for p in range(parts):
        piece = _mx(rest)
        term = lax.dot_general(a, piece, (dims, ((), ())), preferred_element_type=F32)
        acc = term if acc is None else acc + term
        if p + 1 < parts:
            rest = rest - piece.astype(F32)
    return acc


def _params(sem):
    return pltpu.CompilerParams(dimension_semantics=sem, vmem_limit_bytes=VMEM_LIMIT)


def _mm(a, b, mode, m, n, k, *, name, tm=1024, tn=1024, tk=1024, a_spec=None, b_spec=None, extras=(), rows=(),
        epilogue=None, out_dtypes=(F32,), row_sums=0, out_shape=None, out_spec=None, after=(), into=None):
    after = tuple(t for t in after if t is not None) + (() if into is None else (into,))
    tm, tn, tk = min(tm, m), min(tn, n), min(tk, k)
    assert m % tm == 0 and n % tn == 0 and k % tk == 0, (name, m, n, k, tm, tn, tk)
    gi, gj, gk = m // tm, n // tn, k // tk
    assert row_sums == 0 or gj == 1, name
    if a_spec is None:
        a_spec = (pl.BlockSpec((tk, tm), lambda i, j, kk: (kk, i)) if mode == TN
                  else pl.BlockSpec((tm, tk), lambda i, j, kk: (i, kk)))
    if b_spec is None:
        b_spec = (pl.BlockSpec((tn, tk), lambda i, j, kk: (j, kk)) if mode == NT
                  else pl.BlockSpec((tk, tn), lambda i, j, kk: (kk, j)))
    mn_spec = pl.BlockSpec((tm, tn), lambda i, j, kk: (i, j))
    if epilogue is None:
        epilogue = lambda acc: (acc,)
    row_spec = pl.BlockSpec((1, tn), lambda i, j, kk: (0, j))
    n_ex, n_out = len(extras) + len(rows), len(out_dtypes)
    if out_shape is None:
        out_shape = tuple(jax.ShapeDtypeStruct((m, n), d) for d in out_dtypes)
        out_spec = tuple(mn_spec for _ in out_dtypes)
    out_shape = tuple(out_shape) + tuple(jax.ShapeDtypeStruct((1, n), F32) for _ in range(row_sums))
    out_spec = tuple(out_spec) + tuple(row_spec for _ in range(row_sums))

    n_after = len(after)

    def body(*refs):
        a_ref, b_ref = refs[0], refs[1]
        ex = refs[2:2 + n_ex]
        outs = refs[2 + n_ex + n_after:2 + n_ex + n_after + n_out + row_sums]
        first_row_tile = pl.program_id(0) == 0

        def finish(acc):
            res = epilogue(acc, *[e[...] for e in ex])
            for o, r in zip(outs[:n_out], res[:n_out]):
                o[...] = r.astype(o.dtype)
            if row_sums:
                @pl.when(first_row_tile)
                def _():
                    for o in outs[n_out:]:
                        o[...] = jnp.zeros_like(o)

                for o, r in zip(outs[n_out:], res[n_out:]):
                    o[...] += r

        if gk == 1:
            finish(_dot(a_ref[...], b_ref[...], mode))
        else:
            acc_ref = refs[-1]
            kk = pl.program_id(2)

            @pl.when(kk == 0)
            def _():
                acc_ref[...] = jnp.zeros_like(acc_ref)

            acc_ref[...] += _dot(a_ref[...], b_ref[...], mode)

            @pl.when(kk == gk - 1)
            def _():
                finish(acc_ref[...])

    return pl.pallas_call(
        body, name=name, grid=(gi, gj, gk),
        in_specs=([a_spec, b_spec] + [mn_spec] * len(extras) + [row_spec] * len(rows)
                  + [pl.BlockSpec(memory_space=pl.ANY)] * n_after),
        out_specs=out_spec, out_shape=out_shape,
        input_output_aliases={} if into is None else {2 + n_ex + n_after - 1: 0},
        scratch_shapes=[pltpu.VMEM((tm, tn), F32)] if gk > 1 else [],
        compiler_params=_params(("arbitrary" if row_sums else "parallel", "parallel", "arbitrary")),
    )(a, b, *extras, *rows, *after)


def _rms_rows(xv, g):
    return xv * lax.rsqrt(jnp.mean(xv * xv, axis=1, keepdims=True) + EPS) * g


def _rms_rows_bwd(xv, g, dyv):
    r = lax.rsqrt(jnp.mean(xv * xv, axis=1, keepdims=True) + EPS)
    u = dyv * g
    return (r * u - xv * (r * r * r) * jnp.mean(u * xv, axis=1, keepdims=True),
            jnp.sum(dyv * xv * r, axis=0, keepdims=True))


def _residual_rms(acc, res, g):
    h = acc + res
    return h, _rms_rows(h, g)


def _rms_bwd_residual(dhn, xv, dres, g):
    dx, dg = _rms_rows_bwd(xv, g, dhn)
    dx = dx + dres
    return dx, dx, dg


def _rms_fwd(x, g, *, name, tm=512, after=()):
    t, d = x.shape
    tm = min(tm, t)
    after = tuple(a for a in after if a is not None)

    def body(x_ref, g_ref, *rest):
        rest[-1][...] = _rms_rows(x_ref[...], g_ref[...]).astype(rest[-1].dtype)

    return pl.pallas_call(
        body, name=name, grid=(t // tm,),
        in_specs=[pl.BlockSpec((tm, d), lambda i: (i, 0)), pl.BlockSpec((1, d), lambda i: (0, 0))]
        + [pl.BlockSpec(memory_space=pl.ANY)] * len(after),
        out_specs=pl.BlockSpec((tm, d), lambda i: (i, 0)),
        out_shape=jax.ShapeDtypeStruct((t, d), _MXU_DTYPE),
        compiler_params=_params(("parallel",)),
    )(x, g, *after)


def _rms_gain_grad(x, g, dy, *, name, tm=512):
    t, d = x.shape
    tm = min(tm, t)

    def body(x_ref, g_ref, dy_ref, dg_ref):
        @pl.when(pl.program_id(0) == 0)
        def _():
            dg_ref[...] = jnp.zeros_like(dg_ref)

        dg_ref[...] += _rms_rows_bwd(x_ref[...], g_ref[...], dy_ref[...])[1]

    row = pl.BlockSpec((tm, d), lambda i: (i, 0))
    vec = pl.BlockSpec((1, d), lambda i: (0, 0))
    return pl.pallas_call(
        body, name=name, grid=(t // tm,), in_specs=[row, vec, row], out_specs=vec,
        out_shape=jax.ShapeDtypeStruct((1, d), F32), compiler_params=_params(("arbitrary",)),
    )(x, g, dy)


def _hg_constants():
    c = HG_CHUNK
    t = np.arange(c)
    sums = [t[None, :] <= t[:, None]]
    masks = []
    for m in HG_LEVELS:
        base = (t // (2 * m)) * (2 * m)
        mid = base + m - 1
        second = (t - base) >= m
        upper = (t[None, :] > mid[:, None]) & (t[None, :] <= t[:, None])
        lower = (t[None, :] > t[:, None]) & (t[None, :] <= mid[:, None])
        sums.append(np.where(second[:, None], upper, lower))
        masks.append(second[:, None] & (~second)[None, :] & (base[:, None] == base[None, :]))
    return (np.concatenate(sums, axis=0).astype(np.float32), np.stack(masks).astype(np.float32))


HG_HEAD_LANES = tuple(slice(HG_D * h, HG_D * (h + 1)) for h in range(HG_HEADS))


def _per_head(fn, slab):
    return jnp.concatenate([jnp.broadcast_to(fn(slab[:, hs]), (slab.shape[0], HG_D)) for hs in HG_HEAD_LANES], axis=1)


def _lane_sum(v):
    return jnp.sum(v, axis=1, keepdims=True)


def _lane_mean(v):
    return jnp.mean(v, axis=1, keepdims=True)


def _hg_gates(blk, lbp):
    w = HG_HEADS * HG_D
    q, x, v, gl = blk[:, 0:w], blk[:, w:2 * w], blk[:, 2 * w:3 * w], blk[:, 3 * w:4 * w]
    mx = jnp.max(lbp, axis=0, keepdims=True)
    e = jnp.exp(lbp - mx)
    lb = e[0:1, :] / jnp.sum(e, axis=0, keepdims=True)
    sig = jax.nn.sigmoid(x)
    f = lb + (1.0 - lb) * sig
    return q, v, gl, lb, sig, f, 1.0 - f, jnp.log(f)


def _hg_fwd(proj, lbp, ng, bsz, seq, *, y_width):
    t = proj.shape[0]
    nc = seq // HG_CHUNK
    a_np, m_np = _hg_constants()
    a_all = jnp.asarray(a_np, _MXU_DTYPE)
    masks = jnp.asarray(m_np, F32)
    nl = len(HG_LEVELS)

    ts = min(HG_TILE, seq)
    ns, nct = seq // ts, ts // HG_CHUNK
    hw = HG_HEADS * HG_D

    def body(p_ref, lb_ref, ng_ref, a_ref, m_ref, y_ref, o_ref, st_ref, carry):
        a_mat = a_ref[...]
        ngv = ng_ref[...]

        @pl.when(pl.program_id(0) == 0)
        def _():
            carry[...] = jnp.zeros_like(carry)

        ng4 = _tile_lanes(ngv, HG_HEADS)
        heads = range(HG_HEADS)
        exs = range(bsz)
        hl = HG_HEAD_LANES
        lbp_v = lb_ref[...]

        def chunk(c, _):
            rows = pl.ds(pl.multiple_of(c * HG_CHUNK, HG_CHUNK), HG_CHUNK)
            gates = [_hg_gates(p_ref[e, rows, :], lbp_v) for e in exs]
            q, v, gl = [g[0] for g in gates], [g[1] for g in gates], [g[2] for g in gates]
            k = [g[6] for g in gates]
            sts = [[carry[e, h] for h in heads] for e in exs]
            e_all = [_split_dot(a_mat, gates[e][7], NN, 3) for e in exs]
            b = [e_all[e][0:HG_CHUNK] for e in exs]
            qb = [q[e] * jnp.exp(b[e]) for e in exs]
            o = [[_dot(qb[e][:, hl[h]], sts[e][h], NT) for h in heads] for e in exs]
            p = [[jnp.zeros((HG_CHUNK, HG_CHUNK), F32) for _ in heads] for _ in exs]
            for li in range(nl):
                dec = [jnp.exp(e_all[e][HG_CHUNK * (li + 1):HG_CHUNK * (li + 2)]) for e in exs]
                qm, km, mk = [q[e] * dec[e] for e in exs], [k[e] * dec[e] for e in exs], m_ref[li]
                p = [[p[e][h] + mk * _dot(qm[e][:, hl[h]], km[e][:, hl[h]], NT) for h in heads] for e in exs]
            bl = [b[e][HG_CHUNK - 1:HG_CHUNK, :] for e in exs]
            kd = [k[e] * jnp.exp(bl[e] - b[e]) for e in exs]
            pv = [[_dot(p[e][h], v[e][:, hl[h]]) for h in heads] for e in exs]
            upd = [[_dot(v[e][:, hl[h]], kd[e][:, hl[h]], TN) for h in heads] for e in exs]
            for e in exs:
                o_all = (jnp.concatenate([o[e][h] + pv[e][h] for h in heads], axis=1)
                         + _per_head(_lane_sum, q[e] * k[e]) * v[e])
                r = lax.rsqrt(_per_head(_lane_mean, o_all * o_all) + EPS)
                ebl = jnp.exp(bl[e])
                for h in heads:
                    st_ref[e, h, c] = sts[e][h]
                    carry[e, h] = sts[e][h] * ebl[:, hl[h]] + upd[e][h]
                o_ref[e, rows, :] = o_all
                y_ref[e, rows, :] = (o_all * r * ng4) * (gl[e] * jax.nn.sigmoid(gl[e]))
            return 0

        lax.fori_loop(0, nct, chunk, 0)

    y3, o3, states = pl.pallas_call(
        body, name="hgrn2_fwd", grid=(ns,),
        in_specs=[pl.BlockSpec((bsz, ts, HG_COLS), lambda s: (0, s, 0)),
                  pl.BlockSpec((2, hw), lambda s: (0, 0)),
                  pl.BlockSpec((1, HG_D), lambda s: (0, 0)),
                  pl.BlockSpec(a_all.shape, lambda s: (0, 0)),
                  pl.BlockSpec(masks.shape, lambda s: (0, 0, 0))],
        out_specs=(pl.BlockSpec((bsz, ts, hw), lambda s: (0, s, 0)),
                   pl.BlockSpec((bsz, ts, hw), lambda s: (0, s, 0)),
                   pl.BlockSpec((bsz, HG_HEADS, nct, HG_D, HG_D), lambda s: (0, 0, s, 0, 0))),
        out_shape=(jax.ShapeDtypeStruct((bsz, seq, y_width), F32),
                   jax.ShapeDtypeStruct((bsz, seq, hw), F32),
                   jax.ShapeDtypeStruct((bsz, HG_HEADS, nc, HG_D, HG_D), F32)),
        scratch_shapes=[pltpu.VMEM((bsz, HG_HEADS, HG_D, HG_D), F32)],
        compiler_params=_params(("arbitrary",)),
    )(proj.reshape(bsz, seq, HG_COLS), lbp, ng, a_all, masks)
    return y3.reshape(t, y_width), o3.reshape(t, hw), states


def _hg_bwd(proj, lbp, ng, o_all, states, dy, bsz, seq, after=()):
    after = tuple(a for a in after if a is not None)
    t = proj.shape[0]
    nc = seq // HG_CHUNK
    a_np, m_np = _hg_constants()
    a_all = jnp.asarray(a_np, _MXU_DTYPE)
    masks = jnp.asarray(m_np, F32)
    nl = len(HG_LEVELS)
    cs = HG_CHUNK

    ts = min(HG_TILE, seq)
    ns, nct = seq // ts, ts // cs
    hw = HG_HEADS * HG_D

    def body(p_ref, lb_ref, ng_ref, a_ref, m_ref, o_ref, st_ref, dy_ref, *rest):
        dp_ref, dlb_ref, dng_ref, dst_ref = rest[len(after):]
        a_mat = a_ref[...]
        ngv = ng_ref[...]
        ng4 = _tile_lanes(ngv, HG_HEADS)
        last_row = lax.broadcasted_iota(jnp.int32, (cs, hw), 0) == cs - 1
        first = pl.program_id(0) == 0
        heads = range(HG_HEADS)
        exs = range(bsz)
        hl = HG_HEAD_LANES
        lbp_v = lb_ref[...]

        @pl.when(first)
        def _():
            dst_ref[...] = jnp.zeros_like(dst_ref)

        def side_by_side(parts):
            return jnp.concatenate(parts, axis=1)

        def chunk(i, carry):
            dlb_acc, dng_acc = carry
            c = nct - 1 - i
            rows = pl.ds(pl.multiple_of(c * cs, cs), cs)
            gates = [_hg_gates(p_ref[e, rows, :], lbp_v) for e in exs]
            q, v, gl = [g[0] for g in gates], [g[1] for g in gates], [g[2] for g in gates]
            lb, sig, f, k = gates[0][3], [g[4] for g in gates], [g[5] for g in gates], [g[6] for g in gates]
            o = [o_ref[e, rows, :] for e in exs]
            dyv = [dy_ref[e, rows, :] for e in exs]
            sts = [[st_ref[e, h, c] for h in heads] for e in exs]
            dsts = [[dst_ref[e, h] for h in heads] for e in exs]
            e_all = [_split_dot(a_mat, gates[e][7], NN, 3) for e in exs]
            b = [e_all[e][0:cs] for e in exs]
            eb = [jnp.exp(b[e]) for e in exs]
            bl = [b[e][cs - 1:cs, :] for e in exs]
            ebl = [jnp.exp(bl[e]) for e in exs]
            ekd = [jnp.exp(bl[e] - b[e]) for e in exs]
            qb = [q[e] * eb[e] for e in exs]
            kd = [k[e] * ekd[e] for e in exs]
            do, dgl = [], []
            for e in exs:
                sg = jax.nn.sigmoid(gl[e])
                silu = gl[e] * sg
                r = lax.rsqrt(_per_head(_lane_mean, o[e] * o[e]) + EPS)
                dgl.append(dyv[e] * (o[e] * r * ng4) * (sg * (1.0 + gl[e] * (1.0 - sg))))
                u = dyv[e] * silu * ng4
                do.append(r * u - o[e] * (r * r * r) * _per_head(_lane_mean, u * o[e]))
                dng4 = jnp.sum(dyv[e] * silu * o[e] * r, axis=0, keepdims=True)
                dng_acc = dng_acc + ((dng4[:, hl[0]] + dng4[:, hl[1]]) + (dng4[:, hl[2]] + dng4[:, hl[3]]))
            es, qm, km = [], [], []
            p = [[jnp.zeros((cs, cs), F32) for _ in heads] for _ in exs]
            for li in range(nl):
                dec = [jnp.exp(e_all[e][cs * (li + 1):cs * (li + 2)]) for e in exs]
                es.append(dec)
                qm.append([q[e] * dec[e] for e in exs])
                km.append([k[e] * dec[e] for e in exs])
                mk = m_ref[li]
                p = [[p[e][h] + mk * _dot(qm[li][e][:, hl[h]], km[li][e][:, hl[h]], NT) for h in heads] for e in exs]
            dp = [[_dot(do[e][:, hl[h]], v[e][:, hl[h]], NT) for h in heads] for e in exs]
            dv_p = [[_dot(p[e][h], do[e][:, hl[h]], TN) for h in heads] for e in exs]
            dv_s = [[_dot(kd[e][:, hl[h]], dsts[e][h], NT) for h in heads] for e in exs]
            dqb = [side_by_side([_dot(do[e][:, hl[h]], sts[e][h]) for h in heads]) for e in exs]
            dkd = [side_by_side([_dot(v[e][:, hl[h]], dsts[e][h]) for h in heads]) for e in exs]
            new_dst = [[_dot(do[e][:, hl[h]], qb[e][:, hl[h]], TN) for h in heads] for e in exs]
            dv = [side_by_side([dv_p[e][h] + dv_s[e][h] for h in heads]) + _per_head(_lane_sum, q[e] * k[e]) * do[e]
                  for e in exs]
            dq = [dqb[e] * eb[e] for e in exs]
            dk = [dkd[e] * ekd[e] for e in exs]
            de = []
            for e in exs:
                dbl = (jnp.sum(dkd[e] * kd[e], axis=0, keepdims=True)
                       + side_by_side([jnp.sum(dsts[e][h] * sts[e][h], axis=0, keepdims=True) for h in heads]) * ebl[e])
                de.append([dqb[e] * qb[e] - dkd[e] * kd[e] + jnp.where(last_row, dbl, 0.0)])
            for li in range(nl):
                mk = m_ref[li]
                dpm = [[mk * dp[e][h] for h in heads] for e in exs]
                dqm = [side_by_side([_dot(dpm[e][h], km[li][e][:, hl[h]]) for h in heads]) for e in exs]
                dkm = [side_by_side([_dot(dpm[e][h], qm[li][e][:, hl[h]], TN) for h in heads]) for e in exs]
                for e in exs:
                    dq[e] = dq[e] + dqm[e] * es[li][e]
                    dk[e] = dk[e] + dkm[e] * es[li][e]
                    de[e].append(dqm[e] * qm[li][e] + dkm[e] * km[li][e])
            dg = [_split_dot(a_mat, jnp.concatenate(de[e], axis=0), TN, 2) for e in exs]
            for e in exs:
                dpd = _per_head(_lane_sum, do[e] * v[e])
                df = dg[e] / f[e] - (dk[e] + dpd * q[e])
                dp_ref[e, rows, 0:hw] = _mx(dq[e] + dpd * k[e])
                dp_ref[e, rows, hw:2 * hw] = _mx(df * (1.0 - lb) * sig[e] * (1.0 - sig[e]))
                dp_ref[e, rows, 2 * hw:3 * hw] = _mx(dv[e])
                dp_ref[e, rows, 3 * hw:4 * hw] = _mx(dgl[e])
                for h in heads:
                    dst_ref[e, h] = dsts[e][h] * ebl[e][:, hl[h]] + new_dst[e][h]
                dlb_acc = dlb_acc + jnp.sum(df * (1.0 - sig[e]), axis=0, keepdims=True)
            return dlb_acc, dng_acc

        dlb, dng = lax.fori_loop(0, nct, chunk, (jnp.zeros((1, hw), F32), jnp.zeros((1, HG_D), F32)))

        @pl.when(first)
        def _():
            dlb_ref[...] = jnp.zeros_like(dlb_ref)
            dng_ref[...] = jnp.zeros_like(dng_ref)

        mx = jnp.max(lbp_v, axis=0, keepdims=True)
        e = jnp.exp(lbp_v - mx)
        s0 = e[0:1, :] / jnp.sum(e, axis=0, keepdims=True)
        da0 = dlb * s0 * (1.0 - s0)
        dlb_ref[...] += jnp.concatenate([da0, -da0], axis=0)
        dng_ref[...] += dng

    rows3 = lambda w: pl.BlockSpec((bsz, ts, w), lambda s: (0, ns - 1 - s, 0))
    dproj, dlb, dng = pl.pallas_call(
        body, name="hgrn2_bwd", grid=(ns,),
        in_specs=[rows3(HG_COLS),
                  pl.BlockSpec((2, hw), lambda s: (0, 0)),
                  pl.BlockSpec((1, HG_D), lambda s: (0, 0)),
                  pl.BlockSpec(a_all.shape, lambda s: (0, 0)),
                  pl.BlockSpec(masks.shape, lambda s: (0, 0, 0)),
                  rows3(hw),
                  pl.BlockSpec((bsz, HG_HEADS, nct, HG_D, HG_D), lambda s: (0, 0, ns - 1 - s, 0, 0)),
                  rows3(hw)] + [pl.BlockSpec(memory_space=pl.ANY)] * len(after),
        out_specs=(rows3(HG_COLS),
                   pl.BlockSpec((2, hw), lambda s: (0, 0)),
                   pl.BlockSpec((1, HG_D), lambda s: (0, 0))),
        out_shape=(jax.ShapeDtypeStruct((bsz, seq, HG_COLS), _MXU_DTYPE),
                   jax.ShapeDtypeStruct((2, hw), F32),
                   jax.ShapeDtypeStruct((1, HG_D), F32)),
        scratch_shapes=[pltpu.VMEM((bsz, HG_HEADS, HG_D, HG_D), F32)],
        compiler_params=_params(("arbitrary",)),
    )(proj.reshape(bsz, seq, HG_COLS), lbp, ng, a_all, masks, o_all.reshape(bsz, seq, hw), states,
      dy.reshape(bsz, seq, dy.shape[1]), *after)
    return dproj.reshape(t, HG_COLS), dlb, dng


def _sw_constants():
    half = ROT_DIM // 2
    inv = (np.float32(ROPE_THETA) ** (-(np.arange(half, dtype=np.float32) * np.float32(2.0) / np.float32(ROT_DIM)))
           ).astype(np.float32)
    freq = np.zeros((1, 128), np.float32)
    sign = np.zeros((1, 128), np.float32)
    for h in range(2):
        freq[0, 64 * h:64 * h + half] = inv
        freq[0, 64 * h + half:64 * h + 2 * half] = inv
        sign[0, 64 * h:64 * h + half] = -1.0
        sign[0, 64 * h + half:64 * h + 2 * half] = 1.0
    seg = np.kron(np.eye(8, dtype=np.float32), np.full((64, 64), 1.0 / 64.0, np.float32))
    return freq, sign, seg


def _rope_table(pos, *, tm=512, after=()):
    t = pos.shape[0]
    tm = min(tm, t)
    freq_np, sign_np, _ = _sw_constants()
    after = tuple(a for a in after if a is not None)

    def body(p_ref, f_ref, s_ref, *rest):
        o_ref = rest[-1]
        ang = p_ref[...].astype(F32) * f_ref[...]
        o_ref[:, 0:128] = jnp.cos(ang)
        o_ref[:, 128:256] = jnp.sin(ang) * s_ref[...]

    vec = pl.BlockSpec((1, 128), lambda i: (0, 0))
    return pl.pallas_call(
        body, name="rope_table", grid=(t // tm,),
        in_specs=[pl.BlockSpec((tm, 1), lambda i: (i, 0)), vec, vec] + [pl.BlockSpec(memory_space=pl.ANY)] * len(after),
        out_specs=pl.BlockSpec((tm, 256), lambda i: (i, 0)),
        out_shape=jax.ShapeDtypeStruct((t, 256), F32),
        compiler_params=_params(("parallel",)),
    )(pos, jnp.asarray(freq_np), jnp.asarray(sign_np), *after)


def _tile_lanes(v, times):
    return v if times == 1 else jnp.concatenate([v] * times, axis=1)


def _swap_halves(v):
    w = v.shape[1]
    half = ROT_DIM // 2
    lane = lax.broadcasted_iota(jnp.int32, v.shape, 1) % SW_HD
    return jnp.where(lane < half, pltpu.roll(v, w - half, 1), jnp.where(lane < 2 * half, pltpu.roll(v, half, 1), 0.0))


def _sw_norm_rope(tv, gain, seg, cosv, sinv):
    w = tv.shape[1]
    ms = _split_dot_rhs(tv * tv, seg[0:w, 0:w])
    r = lax.rsqrt(ms + EPS)
    tn = tv * r * gain
    reps = w // 128
    return tn * _tile_lanes(cosv, reps) + _swap_halves(tn) * _tile_lanes(sinv, reps), r


def _split_dot_rhs(v, a):
    hi = _mx(v)
    lo = _mx(v - hi.astype(F32))
    return (lax.dot_general(hi, a, (NN, ((), ())), preferred_element_type=F32)
            + lax.dot_general(lo, a, (NN, ((), ())), preferred_element_type=F32))


def _sw_norm_rope_bwd(dt, tv, r, gain, seg, cosv, sinv):
    w = tv.shape[1]
    reps = w // 128
    dtn = dt * _tile_lanes(cosv, reps) + _swap_halves(dt * _tile_lanes(sinv, reps))
    u = dtn * gain
    dtv = r * u - tv * (r * r * r) * _split_dot_rhs(u * tv, seg[0:w, 0:w])
    return dtv, jnp.sum(dtn * tv * r, axis=0, keepdims=True)


def _sw_scores(qh, kp, kc):
    return _dot(qh, kp, NT), _dot(qh, kc, NT)


SW_SCALE = SW_HD ** -0.5


def _sw_probs(raw, sink, first_block):
    qi = lax.broadcasted_iota(jnp.int32, (SW_BLOCK, SW_BLOCK), 0)
    kj = lax.broadcasted_iota(jnp.int32, (SW_BLOCK, SW_BLOCK), 1)
    ok_prev = jnp.logical_and(kj > qi, jnp.logical_not(first_block))
    ok_cur = kj <= qi
    sp = jnp.where(ok_prev, raw[0], -jnp.inf)
    sc = jnp.where(ok_cur, raw[1], -jnp.inf)
    m = jnp.maximum(jnp.maximum(jnp.max(sp, axis=1, keepdims=True), jnp.max(sc, axis=1, keepdims=True)), sink)
    pp, pc = jnp.exp(sp - m), jnp.exp(sc - m)
    es = jnp.exp(sink - m)
    inv = 1.0 / (jnp.sum(pp, axis=1, keepdims=True) + jnp.sum(pc, axis=1, keepdims=True) + es)
    return pp * inv, pc * inv, es * inv


def _sw_specs(nb):
    def cur(b, n):
        return b * nb + jnp.minimum(n, nb - 1)

    def prev(b, n):
        return b * nb + jnp.maximum(jnp.minimum(n, nb - 1) - 1, 0)

    return cur, prev


def _sw_fwd(proj, rope, qg, kg, sinks, y_in, bsz, seq):
    t = proj.shape[0]
    nb = seq // SW_BLOCK
    seg = jnp.asarray(_sw_constants()[2], _MXU_DTYPE)
    cur, prev = _sw_specs(nb)

    def body(q_ref, kc_ref, kp_ref, vc_ref, vp_ref, rc_ref, rp_ref, qg_ref, kg_ref, sk_ref, seg_ref, yin_ref, y_ref):
        del yin_ref
        n = pl.program_id(1)
        segv = seg_ref[...]
        cos_c, sin_c = rc_ref[:, 0:128], rc_ref[:, 128:256]
        cos_p, sin_p = rp_ref[:, 0:128], rp_ref[:, 128:256]
        qr, _ = _sw_norm_rope(q_ref[...], qg_ref[...] * SW_SCALE, segv, cos_c, sin_c)
        kcr, _ = _sw_norm_rope(kc_ref[...], kg_ref[...], segv, cos_c, sin_c)
        kpr, _ = _sw_norm_rope(kp_ref[...], kg_ref[...], segv, cos_p, sin_p)
        vc, vp = vc_ref[...], vp_ref[...]
        ks = [slice(SW_HD * (h // SW_GROUP), SW_HD * (h // SW_GROUP + 1)) for h in range(SW_HEADS)]
        raw = [_sw_scores(qr[:, SW_HD * h:SW_HD * (h + 1)], kpr[:, ks[h]], kcr[:, ks[h]]) for h in range(SW_HEADS)]
        probs = [_sw_probs(raw[h], sk_ref[0, h], n == 0) for h in range(SW_HEADS)]
        for h in range(SW_HEADS):
            y_ref[:, SW_HD * h:SW_HD * (h + 1)] = _dot(probs[h][0], vp[:, ks[h]]) + _dot(probs[h][1], vc[:, ks[h]])

    rowq = pl.BlockSpec((SW_BLOCK, 512), lambda b, n: (cur(b, n), 0))
    full = lambda a: pl.BlockSpec(a.shape, lambda b, n: (0,) * a.ndim)
    yw = y_in.shape[1]
    return pl.pallas_call(
        body, name="swa_fwd", grid=(bsz, nb),
        in_specs=[rowq,
                  pl.BlockSpec((SW_BLOCK, 128), lambda b, n: (cur(b, n), 4)),
                  pl.BlockSpec((SW_BLOCK, 128), lambda b, n: (prev(b, n), 4)),
                  pl.BlockSpec((SW_BLOCK, 128), lambda b, n: (cur(b, n), 5)),
                  pl.BlockSpec((SW_BLOCK, 128), lambda b, n: (prev(b, n), 5)),
                  pl.BlockSpec((SW_BLOCK, 256), lambda b, n: (cur(b, n), 0)),
                  pl.BlockSpec((SW_BLOCK, 256), lambda b, n: (prev(b, n), 0)),
                  full(qg), full(kg),
                  pl.BlockSpec(memory_space=pltpu.SMEM),
                  full(seg),
                  pl.BlockSpec(memory_space=pl.ANY)],
        out_specs=pl.BlockSpec((SW_BLOCK, 512), lambda b, n: (cur(b, n), 1)),
        out_shape=jax.ShapeDtypeStruct((t, yw), F32),
        input_output_aliases={11: 0},
        compiler_params=_params(("parallel", "parallel")),
    )(proj, proj, proj, proj, proj, rope, rope, qg, kg, sinks, seg, y_in)


def _sw_bwd(proj, rope, qg, kg, sinks, y, dy, bsz, seq):
    t = proj.shape[0]
    nb = seq // SW_BLOCK
    seg = jnp.asarray(_sw_constants()[2], _MXU_DTYPE)
    cur, prev = _sw_specs(nb)

    def body(q_ref, kc_ref, kp_ref, vc_ref, vp_ref, rc_ref, rp_ref, qg_ref, kg_ref, sk_ref, seg_ref,
             y_ref, dy_ref, dp_ref, dqg_ref, dkg_ref, dsk_ref,
             dq_car, dkv_car, dqr_s, dkc_s, dkp_s, dvc_s, dvp_s, gq_acc, gk_acc, sk_acc):
        b, n = pl.program_id(0), pl.program_id(1)
        first = jnp.logical_and(b == 0, n == 0)
        last = jnp.logical_and(b == pl.num_programs(0) - 1, n == nb)

        @pl.when(first)
        def _():
            gq_acc[...] = jnp.zeros_like(gq_acc)
            gk_acc[...] = jnp.zeros_like(gk_acc)
            sk_acc[...] = jnp.zeros_like(sk_acc)

        @pl.when(n < nb)
        def _():
            segv = seg_ref[...]
            cos_c, sin_c = rc_ref[:, 0:128], rc_ref[:, 128:256]
            cos_p, sin_p = rp_ref[:, 0:128], rp_ref[:, 128:256]
            qv, kcv, kpv = q_ref[...], kc_ref[...], kp_ref[...]
            qgain = qg_ref[...] * SW_SCALE
            qr, rq = _sw_norm_rope(qv, qgain, segv, cos_c, sin_c)
            kcr, rkc = _sw_norm_rope(kcv, kg_ref[...], segv, cos_c, sin_c)
            kpr, rkp = _sw_norm_rope(kpv, kg_ref[...], segv, cos_p, sin_p)
            vc, vp = vc_ref[...], vp_ref[...]
            lane = lax.broadcasted_iota(jnp.int32, (1, 128), 1)
            dsk = jnp.zeros((1, 128), F32)
            heads = range(SW_HEADS)
            ks = [slice(SW_HD * (h // SW_GROUP), SW_HD * (h // SW_GROUP + 1)) for h in heads]
            hs = [slice(SW_HD * h, SW_HD * (h + 1)) for h in heads]
            qh = [qr[:, hs[h]] for h in heads]
            doh = [dy_ref[:, hs[h]] for h in heads]
            raw = [_sw_scores(qh[h], kpr[:, ks[h]], kcr[:, ks[h]]) for h in heads]
            dpp = [_dot(doh[h], vp[:, ks[h]], NT) for h in heads]
            dpc = [_dot(doh[h], vc[:, ks[h]], NT) for h in heads]
            probs = [_sw_probs(raw[h], sk_ref[0, h], n == 0) for h in heads]
            dsp, dsc = [], []
            for h in heads:
                pp, pc, ps = probs[h]
                delta = jnp.sum(doh[h] * y_ref[:, hs[h]], axis=1, keepdims=True)
                dsp.append(pp * (dpp[h] - delta))
                dsc.append(pc * (dpc[h] - delta))
                dsk = dsk + jnp.where(lane == h, -jnp.sum(ps * delta), 0.0)
            for h in heads:
                dqr_s[:, hs[h]] = _dot(dsp[h], kpr[:, ks[h]]) + _dot(dsc[h], kcr[:, ks[h]])
            for kv in range(SW_KV_HEADS):
                group = range(SW_GROUP * kv, SW_GROUP * (kv + 1))
                kvs = slice(SW_HD * kv, SW_HD * (kv + 1))
                dvp_s[:, kvs] = sum(_dot(probs[h][0], doh[h], TN) for h in group)
                dvc_s[:, kvs] = sum(_dot(probs[h][1], doh[h], TN) for h in group)
                dkp_s[:, kvs] = sum(_dot(dsp[h], qh[h], TN) for h in group)
                dkc_s[:, kvs] = sum(_dot(dsc[h], qh[h], TN) for h in group)
            dq, gq = _sw_norm_rope_bwd(dqr_s[...], qv, rq, qgain, segv, cos_c, sin_c)
            dkc, gkc = _sw_norm_rope_bwd(dkc_s[...], kcv, rkc, kg_ref[...], segv, cos_c, sin_c)
            dkp, gkp = _sw_norm_rope_bwd(dkp_s[...], kpv, rkp, kg_ref[...], segv, cos_p, sin_p)
            gq_acc[...] += gq
            gk_acc[...] += gkc + gkp
            sk_acc[...] += dsk

            @pl.when(n > 0)
            def _():
                dp_ref[:, 0:512] = _mx(dq_car[...])
                dp_ref[:, 512:640] = _mx(dkv_car[:, 0:128] + dkp)
                dp_ref[:, 640:768] = _mx(dkv_car[:, 128:256] + dvp_s[...])

            dq_car[...] = dq
            dkv_car[:, 0:128] = dkc
            dkv_car[:, 128:256] = dvc_s[...]

        @pl.when(n == nb)
        def _():
            dp_ref[:, 0:512] = _mx(dq_car[...])
            dp_ref[:, 512:768] = _mx(dkv_car[...])

        @pl.when(last)
        def _():
            gq = gq_acc[...] * SW_SCALE
            acc = gq[:, 0:SW_HD]
            for h in range(1, SW_HEADS):
                acc = acc + gq[:, SW_HD * h:SW_HD * (h + 1)]
            dqg_ref[...] = acc
            gk = gk_acc[...]
            dkg_ref[...] = gk[:, 0:SW_HD] + gk[:, SW_HD:2 * SW_HD]
            dsk_ref[...] = sk_acc[...]

    rowq = pl.BlockSpec((SW_BLOCK, 512), lambda b, n: (cur(b, n), 0))
    full = lambda a: pl.BlockSpec(a.shape, lambda b, n: (0,) * a.ndim)

    def out_row(b, n):
        return b * nb + jnp.maximum(n - 1, 0)

    return pl.pallas_call(
        body, name="swa_bwd", grid=(bsz, nb + 1),
        in_specs=[rowq,
                  pl.BlockSpec((SW_BLOCK, 128), lambda b, n: (cur(b, n), 4)),
                  pl.BlockSpec((SW_BLOCK, 128), lambda b, n: (prev(b, n), 4)),
                  pl.BlockSpec((SW_BLOCK, 128), lambda b, n: (cur(b, n), 5)),
                  pl.BlockSpec((SW_BLOCK, 128), lambda b, n: (prev(b, n), 5)),
                  pl.BlockSpec((SW_BLOCK, 256), lambda b, n: (cur(b, n), 0)),
                  pl.BlockSpec((SW_BLOCK, 256), lambda b, n: (prev(b, n), 0)),
                  full(qg), full(kg),
                  pl.BlockSpec(memory_space=pltpu.SMEM),
                  full(seg),
                  pl.BlockSpec((SW_BLOCK, 512), lambda b, n: (cur(b, n), 1)),
                  pl.BlockSpec((SW_BLOCK, 512), lambda b, n: (cur(b, n), 1))],
        out_specs=(pl.BlockSpec((SW_BLOCK, SW_COLS), lambda b, n: (out_row(b, n), 0)),
                   pl.BlockSpec((1, SW_HD), lambda b, n: (0, 0)),
                   pl.BlockSpec((1, SW_HD), lambda b, n: (0, 0)),
                   pl.BlockSpec((1, 128), lambda b, n: (0, 0))),
        out_shape=(jax.ShapeDtypeStruct((t, SW_COLS), _MXU_DTYPE),
                   jax.ShapeDtypeStruct((1, SW_HD), F32),
                   jax.ShapeDtypeStruct((1, SW_HD), F32),
                   jax.ShapeDtypeStruct((1, 128), F32)),
        scratch_shapes=[pltpu.VMEM((SW_BLOCK, 512), F32), pltpu.VMEM((SW_BLOCK, 256), F32),
                        pltpu.VMEM((SW_BLOCK, 512), F32),
                        pltpu.VMEM((SW_BLOCK, 128), F32), pltpu.VMEM((SW_BLOCK, 128), F32),
                        pltpu.VMEM((SW_BLOCK, 128), F32), pltpu.VMEM((SW_BLOCK, 128), F32),
                        pltpu.VMEM((1, 512), F32), pltpu.VMEM((1, 128), F32), pltpu.VMEM((1, 128), F32)],
        compiler_params=_params(("arbitrary", "arbitrary")),
    )(proj, proj, proj, proj, proj, rope, rope, qg, kg, sinks, seg, y, dy)


def _head_rms(tv, gain):
    r = lax.rsqrt(jnp.mean(tv * tv, axis=1, keepdims=True) + EPS)
    return tv * r * gain, r


def _head_rms_bwd(dtn, tv, r, gain):
    u = dtn * gain
    return r * u - tv * (r * r * r) * jnp.mean(u * tv, axis=1, keepdims=True), jnp.sum(dtn * tv * r, axis=0, keepdims=True)


def _xa_softmax(raw):
    s = raw * (XA_HD ** -0.5)
    e = jnp.exp(s - jnp.max(s, axis=1, keepdims=True))
    return e * (1.0 / jnp.sum(e, axis=1, keepdims=True))


def _xa_fwd(qx, kvx, qg, kg, bsz, seq, mlen, *, tq=512):
    t = qx.shape[0]
    tq = min(tq, seq)
    nq = seq // tq
    w = XA_HEADS * XA_HD

    def body(q_ref, kv_ref, qg_ref, kg_ref, o_ref):
        heads = range(XA_HEADS)
        hs = [slice(XA_HD * h, XA_HD * (h + 1)) for h in heads]
        qn = [_head_rms(q_ref[:, hs[h]], qg_ref[...])[0] for h in heads]
        kn = [_head_rms(kv_ref[:, hs[h]], kg_ref[...])[0] for h in heads]
        raw = [_dot(qn[h], kn[h], NT) for h in heads]
        p = [_xa_softmax(raw[h]) for h in heads]
        for h in heads:
            o_ref[:, hs[h]] = _dot(p[h], kv_ref[:, w + XA_HD * h:w + XA_HD * (h + 1)]).astype(o_ref.dtype)

    vec = pl.BlockSpec((1, XA_HD), lambda b, i: (0, 0))
    return pl.pallas_call(
        body, name="xattn_fwd", grid=(bsz, nq),
        in_specs=[pl.BlockSpec((tq, w), lambda b, i: (b * nq + i, 0)),
                  pl.BlockSpec((mlen, 2 * w), lambda b, i: (b, 0)), vec, vec],
        out_specs=pl.BlockSpec((tq, w), lambda b, i: (b * nq + i, 0)),
        out_shape=jax.ShapeDtypeStruct((t, w), _MXU_DTYPE),
        compiler_params=_params(("parallel", "parallel")),
    )(qx, kvx, qg, kg)


def _xa_bwd(qx, kvx, qg, kg, do, bsz, seq, mlen, *, tq=512):
    t = qx.shape[0]
    tq = min(tq, seq)
    nq = seq // tq
    w = XA_HEADS * XA_HD
    scale = XA_HD ** -0.5

    def body(q_ref, kv_ref, qg_ref, kg_ref, do_ref, dq_ref, dkv_ref, dqg_ref, dkg_ref):
        b, i = pl.program_id(0), pl.program_id(1)

        @pl.when(jnp.logical_and(b == 0, i == 0))
        def _():
            dqg_ref[...] = jnp.zeros_like(dqg_ref)
            dkg_ref[...] = jnp.zeros_like(dkg_ref)

        @pl.when(i == 0)
        def _():
            dkv_ref[...] = jnp.zeros_like(dkv_ref)

        heads = range(XA_HEADS)
        hs = [slice(XA_HD * h, XA_HD * (h + 1)) for h in heads]
        vs = [slice(w + XA_HD * h, w + XA_HD * (h + 1)) for h in heads]
        qv = [q_ref[:, hs[h]] for h in heads]
        kv = [kv_ref[:, hs[h]] for h in heads]
        doh = [do_ref[:, hs[h]] for h in heads]
        qn = [_head_rms(qv[h], qg_ref[...]) for h in heads]
        kn = [_head_rms(kv[h], kg_ref[...]) for h in heads]
        raw = [_dot(qn[h][0], kn[h][0], NT) for h in heads]
        dp = [_dot(doh[h], kv_ref[:, vs[h]], NT) for h in heads]
        p = [_xa_softmax(raw[h]) for h in heads]
        ds = [p[h] * (dp[h] - jnp.sum(p[h] * dp[h], axis=1, keepdims=True)) * scale for h in heads]
        dqn = [_dot(ds[h], kn[h][0]) for h in heads]
        dkn = [_dot(ds[h], qn[h][0], TN) for h in heads]
        dvv = [_dot(p[h], doh[h], TN) for h in heads]
        gq_sum = jnp.zeros((1, XA_HD), F32)
        gk_sum = jnp.zeros((1, XA_HD), F32)
        for h in heads:
            dqv, gq = _head_rms_bwd(dqn[h], qv[h], qn[h][1], qg_ref[...])
            dkv, gk = _head_rms_bwd(dkn[h], kv[h], kn[h][1], kg_ref[...])
            dq_ref[:, hs[h]] = dqv.astype(dq_ref.dtype)
            dkv_ref[:, hs[h]] += dkv
            dkv_ref[:, vs[h]] += dvv[h]
            gq_sum = gq_sum + gq
            gk_sum = gk_sum + gk
        dqg_ref[...] += gq_sum
        dkg_ref[...] += gk_sum

    vec = pl.BlockSpec((1, XA_HD), lambda b, i: (0, 0))
    row = pl.BlockSpec((tq, w), lambda b, i: (b * nq + i, 0))
    mem = pl.BlockSpec((mlen, 2 * w), lambda b, i: (b, 0))
    return pl.pallas_call(
        body, name="xattn_bwd", grid=(bsz, nq),
        in_specs=[row, mem, vec, vec, row],
        out_specs=(row, mem, vec, vec),
        out_shape=(jax.ShapeDtypeStruct((t, w), _MXU_DTYPE), jax.ShapeDtypeStruct((bsz * mlen, 2 * w), F32),
                   jax.ShapeDtypeStruct((1, XA_HD), F32), jax.ShapeDtypeStruct((1, XA_HD), F32)),
        compiler_params=_params(("arbitrary", "arbitrary")),
    )(qx, kvx, qg, kg, do)


def _loss_finish(sq_row, d_model):
    def body(s_ref, o_ref):
        o_ref[...] = jnp.zeros_like(o_ref) + 0.5 * jnp.sum(s_ref[...]) / float(d_model)

    return pl.pallas_call(body, name="loss_finish", out_shape=jax.ShapeDtypeStruct((1, 128), F32))(sq_row)


def _adamw_math(w, g, m, v):
    m = ADAM_B1 * m + (1.0 - ADAM_B1) * g
    v = ADAM_B2 * v + (1.0 - ADAM_B2) * (g * g)
    m_hat = m / (1.0 - ADAM_B1 ** ADAM_STEP)
    v_hat = v / (1.0 - ADAM_B2 ** ADAM_STEP)
    return -ADAM_LR * (m_hat / (jnp.sqrt(v_hat) + ADAM_EPS) + ADAM_WD * w), m, v


def _adamw_big(ws, gs, ms, vs, *, steps=8):
    n = len(ws)

    def body(*refs):
        for a in range(n):
            gv = refs[n + a][...]
            d, mn, vn = _adamw_math(refs[a][...], gv, refs[2 * n + a][...], refs[3 * n + a][...])
            refs[4 * n + 4 * a][...] = gv
            refs[4 * n + 4 * a + 1][...] = d
            refs[4 * n + 4 * a + 2][...] = mn
            refs[4 * n + 4 * a + 3][...] = vn

    def spec(w):
        assert w.shape[0] % (8 * steps) == 0, w.shape
        return pl.BlockSpec((w.shape[0] // steps, w.shape[1]), lambda i: (i, 0))

    specs = [spec(w) for w in ws]
    out = pl.pallas_call(
        body, name="adamw_big", grid=(steps,), in_specs=specs * 4,
        out_specs=tuple(s for s in specs for _ in range(4)),
        out_shape=tuple(jax.ShapeDtypeStruct(w.shape, F32) for w in ws for _ in range(4)),
        compiler_params=_params(("parallel",)),
    )(*ws, *gs, *ms, *vs)
    return [out[4 * a:4 * a + 4] for a in range(n)]


def _adamw_small(ws, gs, ms, vs):
    n = len(ws)

    def body(*refs):
        for i in range(n):
            d, mn, vn = _adamw_math(refs[i][...], refs[n + i][...], refs[2 * n + i][...], refs[3 * n + i][...])
            refs[4 * n + i][...] = d
            refs[5 * n + i][...] = mn
            refs[6 * n + i][...] = vn

    shapes = tuple(jax.ShapeDtypeStruct(w.shape, F32) for w in ws)
    return pl.pallas_call(body, name="adamw_small", out_shape=shapes * 3)(*ws, *gs, *ms, *vs)


def _add_halves(gs, recvs, c_idx, *, name):
    n = len(gs)

    def body(c_ref, *refs):
        del c_ref
        for a in range(n):
            refs[2 * n + a][...] = refs[a][...] + refs[n + a][...]

    def half(g):
        return pl.BlockSpec((None, g.shape[1] // 2, g.shape[2]), lambda k, cr: (k, cr[0], 0))

    def whole(g):
        return pl.BlockSpec((None, g.shape[1] // 2, g.shape[2]), lambda k, cr: (k, 0, 0))

    return pl.pallas_call(
        body, name=name,
        grid_spec=pltpu.PrefetchScalarGridSpec(
            num_scalar_prefetch=1, grid=(4,),
            in_specs=[half(g) for g in gs] + [whole(g) for g in gs],
            out_specs=tuple(whole(g) for g in gs)),
        out_shape=tuple(jax.ShapeDtypeStruct((4, g.shape[1] // 2, g.shape[2]), F32) for g in gs),
        compiler_params=_params(("parallel",)),
    )(c_idx, *gs, *recvs)


def _add_chips(ps, recvs, place_idx, *, name, steps=2, after=()):
    n = len(ps)

    def body(pi_ref, *refs):
        del pi_ref
        outs = refs[2 * n + len(after):]
        for a in range(n):
            r_ref = refs[n + a]
            outs[a][...] = ((refs[a][...] + r_ref[0]) + r_ref[1]) + r_ref[2]

    def tile(p):
        assert p.shape[1] % (8 * steps) == 0, (name, p.shape)
        return p.shape[1] // steps

    return pl.pallas_call(
        body, name=name,
        grid_spec=pltpu.PrefetchScalarGridSpec(
            num_scalar_prefetch=1, grid=(steps,),
            in_specs=[pl.BlockSpec((None, tile(p), p.shape[2]), lambda i, pi: (pi[0], i, 0)) for p in ps]
            + [pl.BlockSpec((3, tile(p), p.shape[2]), lambda i, pi: (0, i, 0)) for p in ps]
            + [pl.BlockSpec(memory_space=pl.ANY)] * len(after),
            out_specs=tuple(pl.BlockSpec((tile(p), p.shape[2]), lambda i, pi: (pi[1] * steps + i, 0)) for p in ps)),
        out_shape=tuple(jax.ShapeDtypeStruct((2 * p.shape[1], p.shape[2]), F32) for p in ps),
        compiler_params=_params(("parallel",)),
    )(place_idx, *ps, *recvs, *after)


def _place_shards(shards, place_idx, *, name, after=()):
    n = len(shards)

    def body(pi_ref, *refs):
        del pi_ref
        for i in range(n):
            refs[n + len(after) + i][...] = refs[i][...]

    return pl.pallas_call(
        body, name=name,
        grid_spec=pltpu.PrefetchScalarGridSpec(
            num_scalar_prefetch=1, grid=(1,),
            in_specs=[pl.BlockSpec(s.shape, lambda i, pi: (0, 0)) for s in shards]
            + [pl.BlockSpec(memory_space=pl.ANY)] * len(after),
            out_specs=tuple(pl.BlockSpec((None,) + s.shape, lambda i, pi: (pi[0], 0, 0)) for s in shards)),
        out_shape=tuple(jax.ShapeDtypeStruct((4,) + s.shape, s.dtype) for s in shards),
        compiler_params=_params(("arbitrary",)),
    )(place_idx, *shards, *after)


def _place_shard(shard, place_idx, *, name, tr=512, after=()):
    r, c = shard.shape
    tr = min(tr, r)
    if r % tr:
        tr = r // 2
    assert r % tr == 0 and tr % 16 == 0, (name, r, tr)

    def body(pi_ref, s_ref, *rest):
        del pi_ref
        rest[-1][...] = s_ref[...]

    return pl.pallas_call(
        body, name=name,
        grid_spec=pltpu.PrefetchScalarGridSpec(
            num_scalar_prefetch=1, grid=(r // tr,),
            in_specs=[pl.BlockSpec((tr, c), lambda i, pi: (i, 0))] + [pl.BlockSpec(memory_space=pl.ANY)] * len(after),
            out_specs=pl.BlockSpec((None, tr, c), lambda i, pi: (pi[0], i, 0))),
        out_shape=jax.ShapeDtypeStruct((4, r, c), shard.dtype),
        compiler_params=_params(("parallel",)),
    )(place_idx, shard, *after)


def _place():
    x, y, c = lax.axis_index("x"), lax.axis_index("y"), lax.axis_index("c")
    chips = [(1 - x, y), (x, 1 - y), (1 - x, 1 - y)]
    return x, y, c, chips


ANY = pl.BlockSpec(memory_space=pl.ANY)


def _exchange_halves(grads, name):
    n = len(grads)

    def body(*refs):
        ins, outs = refs[:n], refs[n:2 * n]
        send_sems, recv_sems = refs[2 * n:]
        x, y, c, _ = _place()

        def copy(a):
            h = ins[a].shape[1] // 2
            return pltpu.make_async_remote_copy(
                src_ref=ins[a].at[:, pl.ds((1 - c) * h, h), :], dst_ref=outs[a],
                send_sem=send_sems.at[a], recv_sem=recv_sems.at[a], device_id=(x, y, 1 - c), device_id_type=MESH)

        for a in range(n):
            copy(a).start()
        for a in range(n):
            copy(a).wait_recv()
        for a in range(n):
            copy(a).wait_send()

    return pl.pallas_call(
        body, name=name,
        in_specs=[ANY] * n, out_specs=tuple([ANY] * n),
        out_shape=tuple(jax.ShapeDtypeStruct((4, g.shape[1] // 2, g.shape[2]), g.dtype) for g in grads),
        scratch_shapes=[pltpu.SemaphoreType.DMA((n,)), pltpu.SemaphoreType.DMA((n,))],
    )(*grads)


HBM = pl.BlockSpec(memory_space=pltpu.HBM)
SEM = pl.BlockSpec(memory_space=pltpu.SEMAPHORE)
EFFECT = pltpu.SideEffectType.DATAFLOW_SIDE_EFFECTING


def _in_hbm(a):
    return pltpu.with_memory_space_constraint(a, pltpu.HBM)


def _split_copy_calls(name, srcs, lands, n_copies, make_copies):
    ns, nl = len(srcs), len(lands)
    nb = ns + nl

    def start(after=()):
        n_after = len(after)

        def body(*refs):
            outs = refs[nb + n_after:]
            copies = make_copies(refs[:ns], refs[ns:nb], outs[0], outs[1])
            for cp in copies:
                cp.start()
            token = refs[-1]
            token[...] = jnp.zeros_like(token)

        bufs = [_in_hbm(a) for a in list(srcs) + list(lands)]
        out = pl.pallas_call(
            body, name=name + "_start",
            out_shape=(pltpu.SemaphoreType.DMA((n_copies,)), pltpu.SemaphoreType.DMA((n_copies,)),
                       *[pltpu.HBM(a.shape, a.dtype) for a in bufs], jax.ShapeDtypeStruct((8, 128), F32)),
            in_specs=[HBM] * nb + [pl.BlockSpec(memory_space=pl.ANY)] * n_after,
            out_specs=(SEM, SEM, *[HBM] * nb, pl.BlockSpec(memory_space=pltpu.VMEM)),
            input_output_aliases={i: 2 + i for i in range(nb)},
            compiler_params=pltpu.CompilerParams(has_side_effects=EFFECT),
        )(*bufs, *after)
        return dict(send=out[0], recv=out[1], bufs=list(out[2:2 + nb]), token=out[-1])

    def wait(state, after):
        def body(*refs):
            copies = make_copies(refs[:ns], refs[ns:nb], refs[nb], refs[nb + 1])
            for cp in copies:
                cp.wait_send()
            for cp in copies:
                cp.wait_recv()

        bufs = state["bufs"]
        out = pl.pallas_call(
            body, name=name + "_wait",
            out_shape=tuple(pltpu.HBM(a.shape, a.dtype) for a in bufs),
            in_specs=[HBM] * nb + [SEM, SEM] + [pl.BlockSpec(memory_space=pl.ANY)] * len(after),
            out_specs=tuple([HBM] * nb),
            input_output_aliases={i: i for i in range(nb)},
            compiler_params=pltpu.CompilerParams(has_side_effects=EFFECT),
        )(*bufs, state["send"], state["recv"], *after)
        return list(out[:ns]), list(out[ns:])

    return start, wait


def _scatter_chips_split(name, parts):
    n = len(parts)
    lands = [lax.empty((3,) + p.shape[1:], p.dtype) for p in parts]

    def make_copies(srcs, lnds, send_sems, recv_sems):
        _, _, c, chips = _place()
        return [pltpu.make_async_remote_copy(
            src_ref=srcs[a].at[2 * px + py], dst_ref=lnds[a].at[j], send_sem=send_sems.at[a * 3 + j],
            recv_sem=recv_sems.at[a * 3 + j], device_id=(px, py, c), device_id_type=MESH)
            for a in range(n) for j, (px, py) in enumerate(chips)]

    return _split_copy_calls(name, parts, lands, 3 * n, make_copies)


def _exchange_halves_split(name, grads):
    n = len(grads)
    lands = [lax.empty((4, g.shape[1] // 2, g.shape[2]), g.dtype) for g in grads]

    def make_copies(srcs, lnds, send_sems, recv_sems):
        x, y, c, _ = _place()
        out = []
        for a in range(n):
            h = srcs[a].shape[1] // 2
            out.append(pltpu.make_async_remote_copy(
                src_ref=srcs[a].at[:, pl.ds((1 - c) * h, h), :], dst_ref=lnds[a], send_sem=send_sems.at[a],
                recv_sem=recv_sems.at[a], device_id=(x, y, 1 - c), device_id_type=MESH))
        return out

    return _split_copy_calls(name, grads, lands, n, make_copies)


def _gather_chips_split(name, shards, lands):
    n = len(shards)

    def make_copies(srcs, lnds, send_sems, recv_sems):
        x, y, c, chips = _place()
        out = []
        for a in range(n):
            h = srcs[a].shape[0] // 2
            for j, (px, py) in enumerate(chips):
                out.append(pltpu.make_async_remote_copy(
                    src_ref=srcs[a].at[pl.ds(c * h, h), :], dst_ref=lnds[a].at[2 * x + y, pl.ds(c * h, h), :],
                    send_sem=send_sems.at[a * 3 + j], recv_sem=recv_sems.at[a * 3 + j],
                    device_id=(px, py, c), device_id_type=MESH))
        return out

    return _split_copy_calls(name, shards, lands, 3 * n, make_copies)


def _gather_finish(gathered, name):
    n = len(gathered)

    def body(*refs):
        outs = refs[n:2 * n]
        send_sems, recv_sems = refs[2 * n:]
        x, y, c, chips = _place()

        def copy(a, j, chip_idx, which):
            h = outs[a].shape[1] // 2
            rows = outs[a].at[chip_idx, pl.ds(which * h, h), :]
            return pltpu.make_async_remote_copy(
                src_ref=rows, dst_ref=rows, send_sem=send_sems.at[a * 3 + j], recv_sem=recv_sems.at[a * 3 + j],
                device_id=(x, y, 1 - c), device_id_type=MESH)

        for a in range(n):
            for j, (px, py) in enumerate(chips):
                copy(a, j, 2 * px + py, c).start()
        for a in range(n):
            for j, (px, py) in enumerate(chips):
                copy(a, j, 2 * px + py, 1 - c).wait_recv()
        for a in range(n):
            for j, (px, py) in enumerate(chips):
                copy(a, j, 2 * px + py, c).wait_send()

    return pl.pallas_call(
        body, name=name,
        in_specs=[ANY] * n, out_specs=tuple([ANY] * n),
        out_shape=tuple(jax.ShapeDtypeStruct(g.shape, g.dtype) for g in gathered),
        input_output_aliases={i: i for i in range(n)},
        scratch_shapes=[pltpu.SemaphoreType.DMA((3 * n,)), pltpu.SemaphoreType.DMA((3 * n,))],
    )(*gathered)


def _gather_forward_split(name, gathered):
    n = len(gathered)

    def make_copies(srcs, lnds, send_sems, recv_sems):
        x, y, c, chips = _place()
        out = []
        for a in range(n):
            h = lnds[a].shape[1] // 2
            for j, (px, py) in enumerate(chips):
                rows = lnds[a].at[2 * px + py, pl.ds(c * h, h), :]
                out.append(pltpu.make_async_remote_copy(
                    src_ref=rows, dst_ref=rows, send_sem=send_sems.at[a * 3 + j], recv_sem=recv_sems.at[a * 3 + j],
                    device_id=(x, y, 1 - c), device_id_type=MESH))
        return out

    return _split_copy_calls(name, [], gathered, 3 * n, make_copies)


def _join_halves(fulls):
    n = len(fulls)

    def body(*refs):
        outs = refs[n:2 * n]
        send_sems, recv_sems = refs[2 * n:]
        x, y, c, _ = _place()

        def copy(a, which):
            h = outs[a].shape[0] // 2
            rows = outs[a].at[pl.ds(which * h, h), :]
            return pltpu.make_async_remote_copy(
                src_ref=rows, dst_ref=rows, send_sem=send_sems.at[a], recv_sem=recv_sems.at[a],
                device_id=(x, y, 1 - c), device_id_type=MESH)

        for a in range(n):
            copy(a, c).start()
        for a in range(n):
            copy(a, 1 - c).wait_recv()
        for a in range(n):
            copy(a, c).wait_send()

    return pl.pallas_call(
        body, name="rs_join_halves",
        in_specs=[ANY] * n, out_specs=tuple([ANY] * n),
        out_shape=tuple(jax.ShapeDtypeStruct(p.shape, p.dtype) for p in fulls),
        input_output_aliases={i: i for i in range(n)},
        scratch_shapes=[pltpu.SemaphoreType.DMA((n,)), pltpu.SemaphoreType.DMA((n,))],
    )(*fulls)


def _all_gather_small_split(sm):
    r, w = sm.shape

    def make_copies(srcs, lnds, send_sems, recv_sems):
        x, y, c, _ = _place()
        me = 4 * x + 2 * y + c
        rel = [(dx, dy, dc) for dx in (0, 1) for dy in (0, 1) for dc in (0, 1)][1:]
        return [pltpu.make_async_remote_copy(
            src_ref=srcs[0], dst_ref=lnds[0].at[me], send_sem=send_sems.at[k], recv_sem=recv_sems.at[k],
            device_id=(1 - x if dx else x, 1 - y if dy else y, 1 - c if dc else c), device_id_type=MESH)
            for k, (dx, dy, dc) in enumerate(rel)]

    return _split_copy_calls("all_gather_small", [sm], [lax.empty((8, r, w), sm.dtype)], 7, make_copies)


def _sum_devices(sm, gathered, me_idx):
    def body(me_ref, sm_ref, g_ref, o_ref):
        own = sm_ref[...]
        acc = jnp.where(me_ref[0] == 0, own, g_ref[0])
        for d in range(1, 8):
            acc = acc + jnp.where(me_ref[0] == d, own, g_ref[d])
        o_ref[...] = acc

    vm = pl.BlockSpec(memory_space=pltpu.VMEM)
    return pl.pallas_call(
        body, name="sum_devices", in_specs=[pl.BlockSpec(memory_space=pltpu.SMEM), vm, vm], out_specs=vm,
        out_shape=jax.ShapeDtypeStruct(sm.shape, F32),
    )(me_idx, sm, gathered)


def _local_step(x3, mem3, pos2, target3, small, comm):
    bsz, seq, d = x3.shape
    mlen = mem3.shape[1]
    t = bsz * seq
    tok = comm.begin()
    x = x3.reshape(t, d)
    mem = mem3.reshape(bsz * mlen, d)
    target = target3.reshape(t, d)
    rope = _rope_table(pos2.reshape(t, 1), after=tok)
    qg_t = jnp.tile(small["sw_q_norm_g"], (1, SW_HEADS))
    kg_t = jnp.tile(small["sw_k_norm_g"], (1, SW_KV_HEADS))

    hn1 = _rms_fwd(x, small["norm1_g"], name="rms1_fwd", after=tok)
    w = comm.first((hn1, rope))
    w_in_t = w["w_in_t"]
    w_sw_t = w_in_t[HG_COLS:]
    proj_hg = _mm(hn1, w_in_t, NT, t, HG_COLS, d, name="proj_hg", tk=d, after=(w.get("token"),))[0]
    proj_sw = _mm(hn1, w_sw_t, NT, t, SW_COLS, d, name="proj_sw", tk=d)[0]
    y_mix, o_hg, states = _hg_fwd(proj_hg, small["hg_lower_bounds"], small["hg_norm_g"], bsz, seq, y_width=1024)
    y_mix = _sw_fwd(proj_sw, rope, qg_t, kg_t, small["sw_sinks"], y_mix, bsz, seq)
    w = comm.rest(y_mix)
    h1, hn2 = _mm(y_mix, w["w_out"], NN, t, d, 1024, name="out_proj", tk=1024, extras=(x,), rows=(small["norm2_g"],),
                  epilogue=_residual_rms, out_dtypes=(F32, _MXU_DTYPE), after=(w.get("token"),))
    mn = _rms_fwd(mem, small["mem_norm_g"], name="rms_mem_fwd")
    qx = _mm(hn2, w["wq"], NN, t, 512, d, name="xa_q", tk=d)[0]
    kvx = _mm(mn, w["wkv"], NN, bsz * mlen, 1024, d, name="xa_kv", tk=d)[0]
    ox = _xa_fwd(qx, kvx, small["xa_q_norm_g"], small["xa_k_norm_g"], bsz, seq, mlen)
    h2, hn3 = _mm(ox, w["wo"], NN, t, d, 512, name="xa_o", tk=512, extras=(h1,), rows=(small["norm3_g"],),
                  epilogue=_residual_rms, out_dtypes=(F32, _MXU_DTYPE))
    w = {**w, **comm.mlp(hn3)}
    ff = w["down"].shape[0]
    ffs = ff // 4

    def relu_sq(acc):
        a = jnp.maximum(acc, 0.0)
        return a, a * a

    act, act2 = _mm(hn3, w["up"], NN, t, ff, d, name="mlp_up", tm=2048, tn=ffs, tk=d,
                    b_spec=pl.BlockSpec((None, d, ffs), lambda i, j, kk: (j, 0, 0)),
                    epilogue=relu_sq, out_dtypes=(_MXU_DTYPE, _MXU_DTYPE))
    inv_d = 1.0 / d

    def loss_cotangent(acc, res, tgt):
        diff = acc + res - tgt
        v = diff * inv_d
        return v, v, jnp.sum(diff * diff, axis=0, keepdims=True)

    dy, dy_mx, sq_row = _mm(act2, w["down"], NN, t, d, ff, name="mlp_down", tk=2048, extras=(h2, target),
                            epilogue=loss_cotangent, out_dtypes=(F32, _MXU_DTYPE), row_sums=1)
    loss_row = _loss_finish(sq_row, d)

    dz = _mm(dy_mx, w["down"], NT, t, ff, d, name="d_act", tm=2048, tk=d, extras=(act,),
             epilogue=lambda acc, a: (acc * (2.0 * a.astype(F32)),), out_dtypes=(_MXU_DTYPE,))[0]
    g_down = _mm(act2, dy_mx, TN, ff, d, t, name="g_down", tk=t)[0]
    g_up = _mm(hn3, dz, TN, d, ff, t, name="g_up", tn=ffs, tk=t,
               out_shape=(jax.ShapeDtypeStruct((4, d, ffs), F32),),
               out_spec=(pl.BlockSpec((None, min(1024, d), ffs), lambda i, j, kk: (j, i, 0)),))[0]
    tok = comm.grads("mlp", dict(up=g_up, down=g_down))
    dh2, dh2_mx, g_norm3 = _mm(dz, w["up"], NT, t, d, ff, name="d_hn3", tk=ffs, after=tok,
                               b_spec=pl.BlockSpec((None, min(1024, d), ffs), lambda i, j, kk: (kk, j, 0)),
                               extras=(h2, dy), rows=(small["norm3_g"],), epilogue=_rms_bwd_residual,
                               out_dtypes=(F32, _MXU_DTYPE), row_sums=1)
    d_ox = _mm(dh2_mx, w["wo"], NT, t, 512, d, name="d_ox", tk=d)[0]
    g_wo = _mm(ox, dh2_mx, TN, 512, d, t, name="g_wo", tk=t)[0]
    d_qx, d_kvx, g_xq, g_xk = _xa_bwd(qx, kvx, small["xa_q_norm_g"], small["xa_k_norm_g"], d_ox, bsz, seq, mlen)
    g_wq = _mm(hn2, d_qx, TN, d, 512, t, name="g_wq")[0]
    g_wkv = _mm(mn, d_kvx, TN, d, 1024, bsz * mlen, name="g_wkv")[0]
    dh1, dh1_mx, g_norm2 = _mm(d_qx, w["wq"], NT, t, d, 512, name="d_hn2", tk=512, extras=(h1, dh2),
                               rows=(small["norm2_g"],), epilogue=_rms_bwd_residual, out_dtypes=(F32, _MXU_DTYPE),
                               row_sums=1)
    dmn = _mm(d_kvx, w["wkv"], NT, bsz * mlen, d, 1024, name="d_mn", tk=1024)[0]
    g_memn = _rms_gain_grad(mem, small["mem_norm_g"], dmn, name="rms_mem_bwd")
    g_wout = _mm(y_mix, dh1_mx, TN, 1024, d, t, name="g_wout", tk=2048)[0]
    tok = comm.grads("mid", dict(w_out=g_wout, wq=g_wq, wkv=g_wkv, wo=g_wo))
    d_mix = _mm(dh1_mx, w["w_out"], NT, t, 1024, d, name="d_mix", tk=d, after=tok)[0]
    dproj_sw, g_swq, g_swk, g_sinks = _sw_bwd(proj_sw, rope, qg_t, kg_t, small["sw_sinks"], y_mix, d_mix, bsz, seq)
    tok = comm.poll(dproj_sw)
    dproj_hg, g_lb, g_hgn = _hg_bwd(proj_hg, small["hg_lower_bounds"], small["hg_norm_g"], o_hg, states, d_mix, bsz, seq,
                                    after=tok)
    in_rows = HG_COLS + SW_COLS
    sw_tile = 256
    g_in_t = _mm(dproj_hg, hn1, TN, HG_COLS, d, t, name="g_in_hg", tk=t,
                 out_shape=(jax.ShapeDtypeStruct((in_rows, d), F32),),
                 out_spec=(pl.BlockSpec((1024, min(1024, d)), lambda i, j, kk: (i, j)),))[0]
    g_in_t = _mm(dproj_sw, hn1, TN, SW_COLS, d, t, name="g_in_sw", tm=sw_tile, into=g_in_t,
                 out_shape=(jax.ShapeDtypeStruct((in_rows, d), F32),),
                 out_spec=(pl.BlockSpec((sw_tile, min(1024, d)), lambda i, j, kk: (HG_COLS // sw_tile + i, j)),))[0]
    tok = comm.grads("in", dict(w_in_t=g_in_t))
    dhn1_a = _mm(dproj_hg, w_in_t, NN, t, d, HG_COLS, name="d_hn1_hg", tk=HG_COLS, after=tok)[0]
    grad_x, g_norm1 = _mm(dproj_sw, w_sw_t, NN, t, d, SW_COLS, name="d_hn1_sw", tk=SW_COLS, extras=(dhn1_a, x, dh1),
                          rows=(small["norm1_g"],), row_sums=1,
                          epilogue=lambda acc, prev, xv, dres, g: _rms_bwd_residual(acc + prev, xv, dres, g)[1:])

    g_small = dict(norm1_g=g_norm1, hg_lower_bounds=g_lb, hg_norm_g=g_hgn, sw_q_norm_g=g_swq, sw_k_norm_g=g_swk,
                   sw_sinks=g_sinks[:, 0:SW_HEADS], norm2_g=g_norm2, mem_norm_g=g_memn, xa_q_norm_g=g_xq,
                   xa_k_norm_g=g_xk, norm3_g=g_norm3)
    return loss_row, grad_x.reshape(bsz, seq, d), g_small


SMALL_NAMES = ("norm1_g", "hg_lower_bounds", "hg_norm_g", "sw_q_norm_g", "sw_k_norm_g", "sw_sinks", "norm2_g",
               "mem_norm_g", "xa_q_norm_g", "xa_k_norm_g", "norm3_g")
BIG_NAMES = ("w_in", "w_out", "xa_wq", "xa_wkv", "xa_wo", "mlp_up", "mlp_down")
WEIGHT_ORDER = ("norm1_g", "w_in", "hg_lower_bounds", "hg_norm_g", "sw_q_norm_g", "sw_k_norm_g", "sw_sinks", "w_out",
                "norm2_g", "mem_norm_g", "xa_wq", "xa_wkv", "xa_q_norm_g", "xa_k_norm_g", "xa_wo", "norm3_g",
                "mlp_up", "mlp_down")


def _pack_rows(vals, width):
    starts, at = [], 0
    for v in vals:
        starts.append(at)
        at += v.shape[0]
    total = at + (-at) % 8
    out = None
    for v, s in zip(vals, starts):
        placed = jnp.pad(v, ((s, total - s - v.shape[0]), (0, width - v.shape[1])))
        out = placed if out is None else out + placed
    return out, starts


class _MeshWeights:
    LATE = ("w_out", "xa_wq", "xa_wkv", "xa_wo", "mlp_up", "mlp_down")

    def __init__(self, shards, d, ff):
        self.shards, self.d, self.ff = shards, d, ff
        self.c_idx = lax.axis_index("c").astype(jnp.int32).reshape(1)
        chip = (2 * lax.axis_index("x") + lax.axis_index("y")).astype(jnp.int32)
        self.place_idx = jnp.stack([chip, lax.axis_index("c").astype(jnp.int32)])
        self.pending = []
        self.exchanging = None

    def begin(self):
        shard = self.shards["w_in"]
        start, self.in_wait = _gather_chips_split(
            "gather_in", [shard], [_place_shard(shard, self.place_idx, name="place_w_in")])
        self.in_state = start()
        tok = (self.in_state["token"],)
        self.placed = list(_place_shards([self.shards[n] for n in self.LATE], self.place_idx, name="place_late",
                                         after=tok))
        return tok

    def first(self, after):
        _, lands = self.in_wait(self.in_state, (*after, *self.placed))
        (g_in,) = _gather_finish(lands, "gather_in_finish")
        start, self.late_wait = _gather_chips_split("gather_late", [self.shards[n] for n in self.LATE], self.placed)
        self.late_state = start(after=(g_in,))
        return dict(w_in_t=g_in.reshape(-1, self.d), token=self.late_state["token"])

    def rest(self, after):
        _, lands = self.late_wait(self.late_state, (after,))
        g_out, g_q, g_kv, g_o = _gather_finish(lands[:4], "gather_late_finish")
        start, self.mlp_wait = _gather_forward_split("gather_mlp_forward", lands[4:])
        self.mlp_state = start(after=(g_out,))
        d = self.d
        return dict(w_out=g_out.reshape(-1, d), wq=g_q.reshape(d, -1), wkv=g_kv.reshape(d, -1),
                    wo=jnp.concatenate([g_o[k] for k in range(4)], axis=1), token=self.mlp_state["token"])

    def mlp(self, after):
        _, (g_up, g_dn) = self.mlp_wait(self.mlp_state, (after,))
        return dict(up=g_up, down=g_dn.reshape(self.ff, self.d))

    def _scatter(self, tag, names, arrays, recv):
        parts = list(_add_halves(arrays, recv, self.c_idx, name="rs_add_halves_" + tag))
        start, wait = _scatter_chips_split("rs_scatter_" + tag, parts)
        state = start()
        self.pending.append((names, wait, state))
        return state["token"]

    def _advance(self, after):
        if self.exchanging is None:
            return ()
        tag, names, wait, state = self.exchanging
        self.exchanging = None
        arrays, recv = wait(state, (after,))
        return (self._scatter(tag, names, arrays, recv),)

    def poll(self, after):
        return self._advance(after)

    def grads(self, tag, g):
        d, ff = self.d, self.ff
        if tag == "mlp":
            names, arrays = ("mlp_up", "mlp_down"), [g["up"], g["down"].reshape(4, ff // 4, d)]
        elif tag == "mid":
            names = ("w_out", "xa_wq", "xa_wkv", "xa_wo")
            ds = d // 4
            g_wo = jnp.stack([g["wo"][:, ds * k:ds * (k + 1)] for k in range(4)])
            arrays = [g["w_out"].reshape(4, -1, d), g["wq"].reshape(4, d // 4, -1), g["wkv"].reshape(4, d // 4, -1), g_wo]
        else:
            names, arrays = ("w_in",), [g["w_in_t"].reshape(4, -1, d)]
        toks = self._advance(arrays[0])
        if tag == "in":
            return toks + (self._scatter(tag, names, arrays, _exchange_halves(arrays, "rs_exchange_" + tag)),)
        start, wait = _exchange_halves_split("rs_exchange_" + tag, arrays)
        state = start()
        self.exchanging = (tag, names, wait, state)
        return toks + (state["token"],)

    def finish(self, after):
        halves, tok = {}, ()
        for names, wait, state in self.pending:
            srcs, lands = wait(state, tuple(after) + tok)
            fulls = _add_chips(srcs, lands, self.place_idx, name="rs_add_chips_" + names[0], after=tok)
            tok = (fulls[0],)
            halves.update(zip(names, fulls))
        return dict(zip(BIG_NAMES, _join_halves([halves[n] for n in BIG_NAMES])))


def kernel(x, mem, positions, norm1_g, w_in, hg_lower_bounds, hg_norm_g, sw_q_norm_g, sw_k_norm_g, sw_sinks, w_out, norm2_g, mem_norm_g, xa_wq, xa_wkv, xa_q_norm_g, xa_k_norm_g, xa_wo, norm3_g, mlp_up, mlp_down, loss_target, m_norm1_g, m_w_in, m_hg_lower_bounds, m_hg_norm_g, m_sw_q_norm_g, m_sw_k_norm_g, m_sw_sinks, m_w_out, m_norm2_g, m_mem_norm_g, m_xa_wq, m_xa_wkv, m_xa_q_norm_g, m_xa_k_norm_g, m_xa_wo, m_norm3_g, m_mlp_up, m_mlp_down, v_norm1_g, v_w_in, v_hg_lower_bounds, v_hg_norm_g, v_sw_q_norm_g, v_sw_k_norm_g, v_sw_sinks, v_w_out, v_norm2_g, v_mem_norm_g, v_xa_wq, v_xa_wkv, v_xa_q_norm_g, v_xa_k_norm_g, v_xa_wo, v_norm3_g, v_mlp_up, v_mlp_down):
    given = dict(locals())
    weights = {n: given[n] for n in WEIGHT_ORDER}
    moms = {n: given["m_" + n] for n in WEIGHT_ORDER}
    vars_ = {n: given["v_" + n] for n in WEIGHT_ORDER}
    d = x.shape[-1]
    ff = mlp_down.shape[1] * 4
    small = {n: weights[n] for n in SMALL_NAMES}

    def plain(n, a):
        return jnp.swapaxes(a[0], 0, 1) if n == "w_in" else a[0]

    comm = _MeshWeights({n: plain(n, weights[n]).astype(_MXU_DTYPE) for n in BIG_NAMES}, d, ff)
    loss_row, grad_x, g_small = _local_step(x, mem, positions, loss_target, small, comm)
    packed, starts = _pack_rows([g_small[n] for n in SMALL_NAMES] + [loss_row], 1024)
    start, wait = _all_gather_small_split(packed)
    state = start()
    big_grads = comm.finish((grad_x, state["token"]))
    (own,), (gathered,) = wait(state, (big_grads[BIG_NAMES[0]],))
    device = (4 * lax.axis_index("x") + 2 * lax.axis_index("y") + lax.axis_index("c")).astype(jnp.int32).reshape(1)
    summed = _sum_devices(own, gathered, device)
    small_grads = {}
    for n, s in zip(SMALL_NAMES, starts):
        r, c = weights[n].shape
        small_grads[n] = summed[s:s + r, 0:c]
    loss = summed[starts[-1], 0]

    grads, deltas, new_m, new_v = {}, {}, {}, {}
    big_out = _adamw_big([plain(n, weights[n]) for n in BIG_NAMES], [big_grads[n] for n in BIG_NAMES],
                         [plain(n, moms[n]) for n in BIG_NAMES], [plain(n, vars_[n]) for n in BIG_NAMES])
    for n, outs in zip(BIG_NAMES, big_out):
        grads[n], deltas[n], new_m[n], new_v[n] = ((jnp.swapaxes(a, 0, 1) if n == "w_in" else a)[None] for a in outs)
    sm_out = _adamw_small([weights[n] for n in SMALL_NAMES], [small_grads[n] for n in SMALL_NAMES],
                          [moms[n] for n in SMALL_NAMES], [vars_[n] for n in SMALL_NAMES])
    ns = len(SMALL_NAMES)
    for i, n in enumerate(SMALL_NAMES):
        grads[n], deltas[n], new_m[n], new_v[n] = small_grads[n], sm_out[i], sm_out[ns + i], sm_out[2 * ns + i]

    return (loss, grad_x, *[grads[n] for n in WEIGHT_ORDER], *[deltas[n] for n in WEIGHT_ORDER],
            *[new_m[n] for n in WEIGHT_ORDER], *[new_v[n] for n in WEIGHT_ORDER])
```

```python
import numpy as np
import jax
import jax.numpy as jnp
from jax import lax
from jax.experimental import pallas as pl
from jax.experimental.pallas import tpu as pltpu

F32 = jnp.float32
_MXU_DTYPE = jnp.bfloat16

EPS = 1e-6
HG_HEADS = 4
HG_D = 128
HG_CHUNK = 64
HG_TILE = 512
HG_LEVELS = (32, 16, 8, 4, 2, 1)
SW_HEADS = 8
SW_KV_HEADS = 2
SW_GROUP = SW_HEADS // SW_KV_HEADS
SW_HD = 64
SW_BLOCK = 128
ROPE_THETA = 500000.0
ROT_DIM = SW_HD // 4
XA_HEADS = 4
XA_HD = 128
HG_COLS = 4 * HG_HEADS * HG_D
SW_COLS = (SW_HEADS + 2 * SW_KV_HEADS) * SW_HD

ADAM_LR = 0.001
ADAM_B1 = 0.9
ADAM_B2 = 0.999
ADAM_EPS = 1e-08
ADAM_WD = 0.01
ADAM_STEP = 10

VMEM_LIMIT = 56 * 1024 * 1024
MESH = pl.DeviceIdType.MESH

NN = ((1,), (0,))
NT = ((1,), (1,))
TN = ((0,), (0,))


def _mx(v):
    return v.astype(_MXU_DTYPE)


def _dot(a, b, dims=NN):
    return lax.dot_general(_mx(a), _mx(b), (dims, ((), ())), preferred_element_type=F32)


def _split_dot(a, v, dims, parts):
    acc = None
    rest = v
    for p in range(parts):
        piece = _mx(rest)
        term = lax.dot_general(a, piece, (dims, ((), ())), preferred_element_type=F32)
        acc = term if acc is None else acc + term
        if p + 1 < parts:
            rest = rest - piece.astype(F32)
    return acc


def _params(sem):
    return pltpu.CompilerParams(dimension_semantics=sem, vmem_limit_bytes=VMEM_LIMIT)


def _mm(a, b, mode, m, n, k, *, name, tm=1024, tn=1024, tk=1024, a_spec=None, b_spec=None, extras=(), rows=(),
        epilogue=None, out_dtypes=(F32,), row_sums=0, out_shape=None, out_spec=None, after=(), into=None,
        second=None):
    after = tuple(t for t in after if t is not None) + (() if into is None else (into,))
    tm, tn, tk = min(tm, m), min(tn, n), min(tk, k)
    assert m % tm == 0 and n % tn == 0 and k % tk == 0, (name, m, n, k, tm, tn, tk)
    gi, gj, gk = m // tm, n // tn, k // tk
    assert row_sums == 0 or gj == 1, name
    if a_spec is None:
        a_spec = (pl.BlockSpec((tk, tm), lambda i, j, kk: (kk, i)) if mode == TN
                  else pl.BlockSpec((tm, tk), lambda i, j, kk: (i, kk)))
    if b_spec is None:
        b_spec = (pl.BlockSpec((tn, tk), lambda i, j, kk: (j, kk)) if mode == NT
                  else pl.BlockSpec((tk, tn), lambda i, j, kk: (kk, j)))
    mn_spec = pl.BlockSpec((tm, tn), lambda i, j, kk: (i, j))
    if epilogue is None:
        epilogue = lambda acc: (acc,)
    row_spec = pl.BlockSpec((1, tn), lambda i, j, kk: (0, j))
    n_ex, n_out = len(extras) + len(rows), len(out_dtypes)
    if out_shape is None:
        out_shape = tuple(jax.ShapeDtypeStruct((m, n), d) for d in out_dtypes)
        out_spec = tuple(mn_spec for _ in out_dtypes)
    out_shape = tuple(out_shape) + tuple(jax.ShapeDtypeStruct((1, n), F32) for _ in range(row_sums))
    out_spec = tuple(out_spec) + tuple(row_spec for _ in range(row_sums))

    n_after = len(after)
    lead = 2 if second is None else 4
    assert second is None or (mode == NN and gk == 1), name
    second_specs = [] if second is None else [pl.BlockSpec((tm, second[0].shape[1]), lambda i, j, kk: (i, 0)),
                                                pl.BlockSpec((second[1].shape[0], tn), lambda i, j, kk: (0, j))]

    def body(*refs):
        a_ref, b_ref = refs[0], refs[1]
        ex = refs[lead:lead + n_ex]
        outs = refs[lead + n_ex + n_after:lead + n_ex + n_after + n_out + row_sums]
        first_row_tile = pl.program_id(0) == 0

        def finish(acc):
            res = epilogue(acc, *[e[...] for e in ex])
            for o, r in zip(outs[:n_out], res[:n_out]):
                o[...] = r.astype(o.dtype)
            if row_sums:
                @pl.when(first_row_tile)
                def _():
                    for o in outs[n_out:]:
                        o[...] = jnp.zeros_like(o)

                for o, r in zip(outs[n_out:], res[n_out:]):
                    o[...] += r

        if gk == 1 and second is not None:
            finish(_dot(a_ref[...], b_ref[...], mode) + _dot(refs[2][...], refs[3][...], NN))
        elif gk == 1:
            finish(_dot(a_ref[...], b_ref[...], mode))
        else:
            acc_ref = refs[-1]
            kk = pl.program_id(2)

            @pl.when(kk == 0)
            def _():
                acc_ref[...] = jnp.zeros_like(acc_ref)

            acc_ref[...] += _dot(a_ref[...], b_ref[...], mode)

            @pl.when(kk == gk - 1)
            def _():
                finish(acc_ref[...])

    return pl.pallas_call(
        body, name=name, grid=(gi, gj, gk),
        in_specs=([a_spec, b_spec] + second_specs + [mn_spec] * len(extras) + [row_spec] * len(rows)
                  + [pl.BlockSpec(memory_space=pl.ANY)] * n_after),
        out_specs=out_spec, out_shape=out_shape,
        input_output_aliases={} if into is None else {lead + n_ex + n_after - 1: 0},
        scratch_shapes=[pltpu.VMEM((tm, tn), F32)] if gk > 1 else [],
        compiler_params=_params(("arbitrary" if row_sums else "parallel", "parallel", "arbitrary")),
    )(a, b, *(second or ()), *extras, *rows, *after)


def _rms_rows(xv, g):
    return xv * lax.rsqrt(jnp.mean(xv * xv, axis=1, keepdims=True) + EPS) * g


def _rms_rows_bwd(xv, g, dyv):
    r = lax.rsqrt(jnp.mean(xv * xv, axis=1, keepdims=True) + EPS)
    u = dyv * g
    return (r * u - xv * (r * r * r) * jnp.mean(u * xv, axis=1, keepdims=True),
            jnp.sum(dyv * xv * r, axis=0, keepdims=True))


def _residual_rms(acc, res, g):
    h = acc + res
    return h, _rms_rows(h, g)


def _rms_bwd_residual(dhn, xv, dres, g):
    dx, dg = _rms_rows_bwd(xv, g, dhn)
    dx = dx + dres
    return dx, dx, dg


def _rms_fwd(x, g, *, name, tm=512, after=()):
    t, d = x.shape
    tm = min(tm, t)
    after = tuple(a for a in after if a is not None)

    def body(x_ref, g_ref, *rest):
        rest[-1][...] = _rms_rows(x_ref[...], g_ref[...]).astype(rest[-1].dtype)

    return pl.pallas_call(
        body, name=name, grid=(t // tm,),
        in_specs=[pl.BlockSpec((tm, d), lambda i: (i, 0)), pl.BlockSpec((1, d), lambda i: (0, 0))]
        + [pl.BlockSpec(memory_space=pl.ANY)] * len(after),
        out_specs=pl.BlockSpec((tm, d), lambda i: (i, 0)),
        out_shape=jax.ShapeDtypeStruct((t, d), _MXU_DTYPE),
        compiler_params=_params(("parallel",)),
    )(x, g, *after)


def _rms_gain_grad(x, g, dy, *, name, tm=512):
    t, d = x.shape
    tm = min(tm, t)

    def body(x_ref, g_ref, dy_ref, dg_ref):
        @pl.when(pl.program_id(0) == 0)
        def _():
            dg_ref[...] = jnp.zeros_like(dg_ref)

        dg_ref[...] += _rms_rows_bwd(x_ref[...], g_ref[...], dy_ref[...])[1]

    row = pl.BlockSpec((tm, d), lambda i: (i, 0))
    vec = pl.BlockSpec((1, d), lambda i: (0, 0))
    return pl.pallas_call(
        body, name=name, grid=(t // tm,), in_specs=[row, vec, row], out_specs=vec,
        out_shape=jax.ShapeDtypeStruct((1, d), F32), compiler_params=_params(("arbitrary",)),
    )(x, g, dy)


def _hg_constants():
    c = HG_CHUNK
    t = np.arange(c)
    sums = [t[None, :] <= t[:, None]]
    masks = []
    for m in HG_LEVELS:
        base = (t // (2 * m)) * (2 * m)
        mid = base + m - 1
        second = (t - base) >= m
        upper = (t[None, :] > mid[:, None]) & (t[None, :] <= t[:, None])
        lower = (t[None, :] > t[:, None]) & (t[None, :] <= mid[:, None])
        sums.append(np.where(second[:, None], upper, lower))
        masks.append(second[:, None] & (~second)[None, :] & (base[:, None] == base[None, :]))
    return (np.concatenate(sums, axis=0).astype(np.float32), np.stack(masks).astype(np.float32))


HG_HEAD_LANES = tuple(slice(HG_D * h, HG_D * (h + 1)) for h in range(HG_HEADS))


def _per_head(fn, slab):
    return jnp.concatenate([jnp.broadcast_to(fn(slab[:, hs]), (slab.shape[0], HG_D)) for hs in HG_HEAD_LANES], axis=1)


def _lane_sum(v):
    return jnp.sum(v, axis=1, keepdims=True)


def _lane_mean(v):
    return jnp.mean(v, axis=1, keepdims=True)


def _hg_gates(blk, lbp):
    w = HG_HEADS * HG_D
    q, x, v, gl = blk[:, 0:w], blk[:, w:2 * w], blk[:, 2 * w:3 * w], blk[:, 3 * w:4 * w]
    mx = jnp.max(lbp, axis=0, keepdims=True)
    e = jnp.exp(lbp - mx)
    lb = e[0:1, :] / jnp.sum(e, axis=0, keepdims=True)
    sig = jax.nn.sigmoid(x)
    f = lb + (1.0 - lb) * sig
    return q, v, gl, lb, sig, f, 1.0 - f, jnp.log(f)


def _hg_fwd(proj, lbp, ng, bsz, seq, *, y_width):
    t = proj.shape[0]
    nc = seq // HG_CHUNK
    a_np, m_np = _hg_constants()
    a_all = jnp.asarray(a_np, _MXU_DTYPE)
    masks = jnp.asarray(m_np, F32)
    nl = len(HG_LEVELS)

    ts = min(HG_TILE, seq)
    ns, nct = seq // ts, ts // HG_CHUNK
    hw = HG_HEADS * HG_D

    def body(p_ref, lb_ref, ng_ref, a_ref, m_ref, y_ref, o_ref, st_ref, carry):
        a_mat = a_ref[...]
        ngv = ng_ref[...]

        @pl.when(pl.program_id(0) == 0)
        def _():
            carry[...] = jnp.zeros_like(carry)

        ng4 = _tile_lanes(ngv, HG_HEADS)
        heads = range(HG_HEADS)
        exs = range(bsz)
        hl = HG_HEAD_LANES
        lbp_v = lb_ref[...]

        def chunk(c, _):
            rows = pl.ds(pl.multiple_of(c * HG_CHUNK, HG_CHUNK), HG_CHUNK)
            gates = [_hg_gates(p_ref[e, rows, :], lbp_v) for e in exs]
            q, v, gl = [g[0] for g in gates], [g[1] for g in gates], [g[2] for g in gates]
            k = [g[6] for g in gates]
            sts = [[carry[e, h] for h in heads] for e in exs]
            e_all = [_split_dot(a_mat, gates[e][7], NN, 3) for e in exs]
            b = [e_all[e][0:HG_CHUNK] for e in exs]
            qb = [q[e] * jnp.exp(b[e]) for e in exs]
            o = [[_dot(qb[e][:, hl[h]], sts[e][h], NT) for h in heads] for e in exs]
            p = [[jnp.zeros((HG_CHUNK, HG_CHUNK), F32) for _ in heads] for _ in exs]
            for li in range(nl):
                dec = [jnp.exp(e_all[e][HG_CHUNK * (li + 1):HG_CHUNK * (li + 2)]) for e in exs]
                qm, km, mk = [q[e] * dec[e] for e in exs], [k[e] * dec[e] for e in exs], m_ref[li]
                p = [[p[e][h] + mk * _dot(qm[e][:, hl[h]], km[e][:, hl[h]], NT) for h in heads] for e in exs]
            bl = [b[e][HG_CHUNK - 1:HG_CHUNK, :] for e in exs]
            kd = [k[e] * jnp.exp(bl[e] - b[e]) for e in exs]
            pv = [[_dot(p[e][h], v[e][:, hl[h]]) for h in heads] for e in exs]
            upd = [[_dot(v[e][:, hl[h]], kd[e][:, hl[h]], TN) for h in heads] for e in exs]
            for e in exs:
                o_all = (jnp.concatenate([o[e][h] + pv[e][h] for h in heads], axis=1)
                         + _per_head(_lane_sum, q[e] * k[e]) * v[e])
                r = lax.rsqrt(_per_head(_lane_mean, o_all * o_all) + EPS)
                ebl = jnp.exp(bl[e])
                for h in heads:
                    st_ref[e, h, c] = sts[e][h]
                    carry[e, h] = sts[e][h] * ebl[:, hl[h]] + upd[e][h]
                o_ref[e, rows, :] = o_all
                y_ref[e, rows, :] = (o_all * r * ng4) * (gl[e] * jax.nn.sigmoid(gl[e]))
            return 0

        lax.fori_loop(0, nct, chunk, 0)

    y3, o3, states = pl.pallas_call(
        body, name="hgrn2_fwd", grid=(ns,),
        in_specs=[pl.BlockSpec((bsz, ts, HG_COLS), lambda s: (0, s, 0)),
                  pl.BlockSpec((2, hw), lambda s: (0, 0)),
                  pl.BlockSpec((1, HG_D), lambda s: (0, 0)),
                  pl.BlockSpec(a_all.shape, lambda s: (0, 0)),
                  pl.BlockSpec(masks.shape, lambda s: (0, 0, 0))],
        out_specs=(pl.BlockSpec((bsz, ts, hw), lambda s: (0, s, 0)),
                   pl.BlockSpec((bsz, ts, hw), lambda s: (0, s, 0)),
                   pl.BlockSpec((bsz, HG_HEADS, nct, HG_D, HG_D), lambda s: (0, 0, s, 0, 0))),
        out_shape=(jax.ShapeDtypeStruct((bsz, seq, y_width), F32),
                   jax.ShapeDtypeStruct((bsz, seq, hw), F32),
                   jax.ShapeDtypeStruct((bsz, HG_HEADS, nc, HG_D, HG_D), F32)),
        scratch_shapes=[pltpu.VMEM((bsz, HG_HEADS, HG_D, HG_D), F32)],
        compiler_params=_params(("arbitrary",)),
    )(proj.reshape(bsz, seq, HG_COLS), lbp, ng, a_all, masks)
    return y3.reshape(t, y_width), o3.reshape(t, hw), states


def _hg_bwd(proj, lbp, ng, o_all, states, dy, bsz, seq, after=()):
    after = tuple(a for a in after if a is not None)
    t = proj.shape[0]
    nc = seq // HG_CHUNK
    a_np, m_np = _hg_constants()
    a_all = jnp.asarray(a_np, _MXU_DTYPE)
    masks = jnp.asarray(m_np, F32)
    nl = len(HG_LEVELS)
    cs = HG_CHUNK

    ts = min(HG_TILE, seq)
    ns, nct = seq // ts, ts // cs
    hw = HG_HEADS * HG_D

    def body(p_ref, lb_ref, ng_ref, a_ref, m_ref, o_ref, st_ref, dy_ref, *rest):
        dp_ref, dlb_ref, dng_ref, dst_ref = rest[len(after):]
        a_mat = a_ref[...]
        ngv = ng_ref[...]
        ng4 = _tile_lanes(ngv, HG_HEADS)
        last_row = lax.broadcasted_iota(jnp.int32, (cs, hw), 0) == cs - 1
        first = pl.program_id(0) == 0
        heads = range(HG_HEADS)
        exs = range(bsz)
        hl = HG_HEAD_LANES
        lbp_v = lb_ref[...]

        @pl.when(first)
        def _():
            dst_ref[...] = jnp.zeros_like(dst_ref)

        def side_by_side(parts):
            return jnp.concatenate(parts, axis=1)

        def chunk(i, carry):
            dlb_acc, dng_acc = carry
            c = nct - 1 - i
            rows = pl.ds(pl.multiple_of(c * cs, cs), cs)
            gates = [_hg_gates(p_ref[e, rows, :], lbp_v) for e in exs]
            q, v, gl = [g[0] for g in gates], [g[1] for g in gates], [g[2] for g in gates]
            lb, sig, f, k = gates[0][3], [g[4] for g in gates], [g[5] for g in gates], [g[6] for g in gates]
            o = [o_ref[e, rows, :] for e in exs]
            dyv = [dy_ref[e, rows, :] for e in exs]
            sts = [[st_ref[e, h, c] for h in heads] for e in exs]
            dsts = [[dst_ref[e, h] for h in heads] for e in exs]
            e_all = [_split_dot(a_mat, gates[e][7], NN, 3) for e in exs]
            b = [e_all[e][0:cs] for e in exs]
            eb = [jnp.exp(b[e]) for e in exs]
            bl = [b[e][cs - 1:cs, :] for e in exs]
            ebl = [jnp.exp(bl[e]) for e in exs]
            ekd = [jnp.exp(bl[e] - b[e]) for e in exs]
            qb = [q[e] * eb[e] for e in exs]
            kd = [k[e] * ekd[e] for e in exs]
            do, dgl = [], []
            for e in exs:
                sg = jax.nn.sigmoid(gl[e])
                silu = gl[e] * sg
                r = lax.rsqrt(_per_head(_lane_mean, o[e] * o[e]) + EPS)
                dgl.append(dyv[e] * (o[e] * r * ng4) * (sg * (1.0 + gl[e] * (1.0 - sg))))
                u = dyv[e] * silu * ng4
                do.append(r * u - o[e] * (r * r * r) * _per_head(_lane_mean, u * o[e]))
                dng4 = jnp.sum(dyv[e] * silu * o[e] * r, axis=0, keepdims=True)
                dng_acc = dng_acc + ((dng4[:, hl[0]] + dng4[:, hl[1]]) + (dng4[:, hl[2]] + dng4[:, hl[3]]))
            es, qm, km = [], [], []
            p = [[jnp.zeros((cs, cs), F32) for _ in heads] for _ in exs]
            for li in range(nl):
                dec = [jnp.exp(e_all[e][cs * (li + 1):cs * (li + 2)]) for e in exs]
                es.append(dec)
                qm.append([q[e] * dec[e] for e in exs])
                km.append([k[e] * dec[e] for e in exs])
                mk = m_ref[li]
                p = [[p[e][h] + mk * _dot(qm[li][e][:, hl[h]], km[li][e][:, hl[h]], NT) for h in heads] for e in exs]
            dp = [[_dot(do[e][:, hl[h]], v[e][:, hl[h]], NT) for h in heads] for e in exs]
            dv_p = [[_dot(p[e][h], do[e][:, hl[h]], TN) for h in heads] for e in exs]
            dv_s = [[_dot(kd[e][:, hl[h]], dsts[e][h], NT) for h in heads] for e in exs]
            dqb = [side_by_side([_dot(do[e][:, hl[h]], sts[e][h]) for h in heads]) for e in exs]
            dkd = [side_by_side([_dot(v[e][:, hl[h]], dsts[e][h]) for h in heads]) for e in exs]
            new_dst = [[_dot(do[e][:, hl[h]], qb[e][:, hl[h]], TN) for h in heads] for e in exs]
            dv = [side_by_side([dv_p[e][h] + dv_s[e][h] for h in heads]) + _per_head(_lane_sum, q[e] * k[e]) * do[e]
                  for e in exs]
            dq = [dqb[e] * eb[e] for e in exs]
            dk = [dkd[e] * ekd[e] for e in exs]
            de = []
            for e in exs:
                dbl = (jnp.sum(dkd[e] * kd[e], axis=0, keepdims=True)
                       + side_by_side([jnp.sum(dsts[e][h] * sts[e][h], axis=0, keepdims=True) for h in heads]) * ebl[e])
                de.append([dqb[e] * qb[e] - dkd[e] * kd[e] + jnp.where(last_row, dbl, 0.0)])
            for li in range(nl):
                mk = m_ref[li]
                dpm = [[mk * dp[e][h] for h in heads] for e in exs]
                dqm = [side_by_side([_dot(dpm[e][h], km[li][e][:, hl[h]]) for h in heads]) for e in exs]
                dkm = [side_by_side([_dot(dpm[e][h], qm[li][e][:, hl[h]], TN) for h in heads]) for e in exs]
                for e in exs:
                    dq[e] = dq[e] + dqm[e] * es[li][e]
                    dk[e] = dk[e] + dkm[e] * es[li][e]
                    de[e].append(dqm[e] * qm[li][e] + dkm[e] * km[li][e])
            dg = [_split_dot(a_mat, jnp.concatenate(de[e], axis=0), TN, 2) for e in exs]
            for e in exs:
                dpd = _per_head(_lane_sum, do[e] * v[e])
                df = dg[e] / f[e] - (dk[e] + dpd * q[e])
                dp_ref[e, rows, 0:hw] = _mx(dq[e] + dpd * k[e])
                dp_ref[e, rows, hw:2 * hw] = _mx(df * (1.0 - lb) * sig[e] * (1.0 - sig[e]))
                dp_ref[e, rows, 2 * hw:3 * hw] = _mx(dv[e])
                dp_ref[e, rows, 3 * hw:4 * hw] = _mx(dgl[e])
                for h in heads:
                    dst_ref[e, h] = dsts[e][h] * ebl[e][:, hl[h]] + new_dst[e][h]
                dlb_acc = dlb_acc + jnp.sum(df * (1.0 - sig[e]), axis=0, keepdims=True)
            return dlb_acc, dng_acc

        dlb, dng = lax.fori_loop(0, nct, chunk, (jnp.zeros((1, hw), F32), jnp.zeros((1, HG_D), F32)))

        @pl.when(first)
        def _():
            dlb_ref[...] = jnp.zeros_like(dlb_ref)
            dng_ref[...] = jnp.zeros_like(dng_ref)

        mx = jnp.max(lbp_v, axis=0, keepdims=True)
        e = jnp.exp(lbp_v - mx)
        s0 = e[0:1, :] / jnp.sum(e, axis=0, keepdims=True)
        da0 = dlb * s0 * (1.0 - s0)
        dlb_ref[...] += jnp.concatenate([da0, -da0], axis=0)
        dng_ref[...] += dng

    rows3 = lambda w: pl.BlockSpec((bsz, ts, w), lambda s: (0, ns - 1 - s, 0))
    dproj, dlb, dng = pl.pallas_call(
        body, name="hgrn2_bwd", grid=(ns,),
        in_specs=[rows3(HG_COLS),
                  pl.BlockSpec((2, hw), lambda s: (0, 0)),
                  pl.BlockSpec((1, HG_D), lambda s: (0, 0)),
                  pl.BlockSpec(a_all.shape, lambda s: (0, 0)),
                  pl.BlockSpec(masks.shape, lambda s: (0, 0, 0)),
                  rows3(hw),
                  pl.BlockSpec((bsz, HG_HEADS, nct, HG_D, HG_D), lambda s: (0, 0, ns - 1 - s, 0, 0)),
                  rows3(hw)] + [pl.BlockSpec(memory_space=pl.ANY)] * len(after),
        out_specs=(rows3(HG_COLS),
                   pl.BlockSpec((2, hw), lambda s: (0, 0)),
                   pl.BlockSpec((1, HG_D), lambda s: (0, 0))),
        out_shape=(jax.ShapeDtypeStruct((bsz, seq, HG_COLS), _MXU_DTYPE),
                   jax.ShapeDtypeStruct((2, hw), F32),
                   jax.ShapeDtypeStruct((1, HG_D), F32)),
        scratch_shapes=[pltpu.VMEM((bsz, HG_HEADS, HG_D, HG_D), F32)],
        compiler_params=_params(("arbitrary",)),
    )(proj.reshape(bsz, seq, HG_COLS), lbp, ng, a_all, masks, o_all.reshape(bsz, seq, hw), states,
      dy.reshape(bsz, seq, dy.shape[1]), *after)
    return dproj.reshape(t, HG_COLS), dlb, dng


def _sw_constants():
    half = ROT_DIM // 2
    inv = (np.float32(ROPE_THETA) ** (-(np.arange(half, dtype=np.float32) * np.float32(2.0) / np.float32(ROT_DIM)))
           ).astype(np.float32)
    freq = np.zeros((1, 128), np.float32)
    sign = np.zeros((1, 128), np.float32)
    for h in range(2):
        freq[0, 64 * h:64 * h + half] = inv
        freq[0, 64 * h + half:64 * h + 2 * half] = inv
        sign[0, 64 * h:64 * h + half] = -1.0
        sign[0, 64 * h + half:64 * h + 2 * half] = 1.0
    seg = np.kron(np.eye(8, dtype=np.float32), np.full((64, 64), 1.0 / 64.0, np.float32))
    return freq, sign, seg


def _rope_table(pos, *, tm=512, after=()):
    t = pos.shape[0]
    tm = min(tm, t)
    freq_np, sign_np, _ = _sw_constants()
    after = tuple(a for a in after if a is not None)

    def body(p_ref, f_ref, s_ref, *rest):
        o_ref = rest[-1]
        ang = p_ref[...].astype(F32) * f_ref[...]
        o_ref[:, 0:128] = jnp.cos(ang)
        o_ref[:, 128:256] = jnp.sin(ang) * s_ref[...]

    vec = pl.BlockSpec((1, 128), lambda i: (0, 0))
    return pl.pallas_call(
        body, name="rope_table", grid=(t // tm,),
        in_specs=[pl.BlockSpec((tm, 1), lambda i: (i, 0)), vec, vec] + [pl.BlockSpec(memory_space=pl.ANY)] * len(after),
        out_specs=pl.BlockSpec((tm, 256), lambda i: (i, 0)),
        out_shape=jax.ShapeDtypeStruct((t, 256), F32),
        compiler_params=_params(("parallel",)),
    )(pos, jnp.asarray(freq_np), jnp.asarray(sign_np), *after)


def _tile_lanes(v, times):
    return v if times == 1 else jnp.concatenate([v] * times, axis=1)


def _swap_halves(v):
    w = v.shape[1]
    half = ROT_DIM // 2
    lane = lax.broadcasted_iota(jnp.int32, v.shape, 1) % SW_HD
    return jnp.where(lane < half, pltpu.roll(v, w - half, 1), jnp.where(lane < 2 * half, pltpu.roll(v, half, 1), 0.0))


def _sw_norm_rope(tv, gain, seg, cosv, sinv):
    w = tv.shape[1]
    ms = _split_dot_rhs(tv * tv, seg[0:w, 0:w])
    r = lax.rsqrt(ms + EPS)
    tn = tv * r * gain
    reps = w // 128
    return tn * _tile_lanes(cosv, reps) + _swap_halves(tn) * _tile_lanes(sinv, reps), r


def _split_dot_rhs(v, a):
    hi = _mx(v)
    lo = _mx(v - hi.astype(F32))
    return (lax.dot_general(hi, a, (NN, ((), ())), preferred_element_type=F32)
            + lax.dot_general(lo, a, (NN, ((), ())), preferred_element_type=F32))


def _sw_norm_rope_bwd(dt, tv, r, gain, seg, cosv, sinv):
    w = tv.shape[1]
    reps = w // 128
    dtn = dt * _tile_lanes(cosv, reps) + _swap_halves(dt * _tile_lanes(sinv, reps))
    u = dtn * gain
    dtv = r * u - tv * (r * r * r) * _split_dot_rhs(u * tv, seg[0:w, 0:w])
    return dtv, jnp.sum(dtn * tv * r, axis=0, keepdims=True)


def _sw_scores(qh, kp, kc):
    return _dot(qh, kp, NT), _dot(qh, kc, NT)


SW_SCALE = SW_HD ** -0.5


def _sw_probs(raw, sink, first_block):
    qi = lax.broadcasted_iota(jnp.int32, (SW_BLOCK, SW_BLOCK), 0)
    kj = lax.broadcasted_iota(jnp.int32, (SW_BLOCK, SW_BLOCK), 1)
    ok_prev = jnp.logical_and(kj > qi, jnp.logical_not(first_block))
    ok_cur = kj <= qi
    sp = jnp.where(ok_prev, raw[0], -jnp.inf)
    sc = jnp.where(ok_cur, raw[1], -jnp.inf)
    m = jnp.maximum(jnp.maximum(jnp.max(sp, axis=1, keepdims=True), jnp.max(sc, axis=1, keepdims=True)), sink)
    pp, pc = jnp.exp(sp - m), jnp.exp(sc - m)
    es = jnp.exp(sink - m)
    inv = 1.0 / (jnp.sum(pp, axis=1, keepdims=True) + jnp.sum(pc, axis=1, keepdims=True) + es)
    return pp * inv, pc * inv, es * inv


def _sw_specs(nb):
    def cur(b, n):
        return b * nb + jnp.minimum(n, nb - 1)

    def prev(b, n):
        return b * nb + jnp.maximum(jnp.minimum(n, nb - 1) - 1, 0)

    return cur, prev


def _sw_fwd(proj, rope, qg, kg, sinks, y_in, bsz, seq):
    t = proj.shape[0]
    nb = seq // SW_BLOCK
    seg = jnp.asarray(_sw_constants()[2], _MXU_DTYPE)
    cur, prev = _sw_specs(nb)

    def body(q_ref, kc_ref, kp_ref, vc_ref, vp_ref, rc_ref, rp_ref, qg_ref, kg_ref, sk_ref, seg_ref, yin_ref, y_ref):
        del yin_ref
        n = pl.program_id(1)
        segv = seg_ref[...]
        cos_c, sin_c = rc_ref[:, 0:128], rc_ref[:, 128:256]
        cos_p, sin_p = rp_ref[:, 0:128], rp_ref[:, 128:256]
        qr, _ = _sw_norm_rope(q_ref[...], qg_ref[...] * SW_SCALE, segv, cos_c, sin_c)
        kcr, _ = _sw_norm_rope(kc_ref[...], kg_ref[...], segv, cos_c, sin_c)
        kpr, _ = _sw_norm_rope(kp_ref[...], kg_ref[...], segv, cos_p, sin_p)
        vc, vp = vc_ref[...], vp_ref[...]
        ks = [slice(SW_HD * (h // SW_GROUP), SW_HD * (h // SW_GROUP + 1)) for h in range(SW_HEADS)]
        raw = [_sw_scores(qr[:, SW_HD * h:SW_HD * (h + 1)], kpr[:, ks[h]], kcr[:, ks[h]]) for h in range(SW_HEADS)]
        probs = [_sw_probs(raw[h], sk_ref[0, h], n == 0) for h in range(SW_HEADS)]
        for h in range(SW_HEADS):
            y_ref[:, SW_HD * h:SW_HD * (h + 1)] = _dot(probs[h][0], vp[:, ks[h]]) + _dot(probs[h][1], vc[:, ks[h]])

    rowq = pl.BlockSpec((SW_BLOCK, 512), lambda b, n: (cur(b, n), 0))
    full = lambda a: pl.BlockSpec(a.shape, lambda b, n: (0,) * a.ndim)
    yw = y_in.shape[1]
    return pl.pallas_call(
        body, name="swa_fwd", grid=(bsz, nb),
        in_specs=[rowq,
                  pl.BlockSpec((SW_BLOCK, 128), lambda b, n: (cur(b, n), 4)),
                  pl.BlockSpec((SW_BLOCK, 128), lambda b, n: (prev(b, n), 4)),
                  pl.BlockSpec((SW_BLOCK, 128), lambda b, n: (cur(b, n), 5)),
                  pl.BlockSpec((SW_BLOCK, 128), lambda b, n: (prev(b, n), 5)),
                  pl.BlockSpec((SW_BLOCK, 256), lambda b, n: (cur(b, n), 0)),
                  pl.BlockSpec((SW_BLOCK, 256), lambda b, n: (prev(b, n), 0)),
                  full(qg), full(kg),
                  pl.BlockSpec(memory_space=pltpu.SMEM),
                  full(seg),
                  pl.BlockSpec(memory_space=pl.ANY)],
        out_specs=pl.BlockSpec((SW_BLOCK, 512), lambda b, n: (cur(b, n), 1)),
        out_shape=jax.ShapeDtypeStruct((t, yw), F32),
        input_output_aliases={11: 0},
        compiler_params=_params(("parallel", "parallel")),
    )(proj, proj, proj, proj, proj, rope, rope, qg, kg, sinks, seg, y_in)


def _sw_bwd(proj, rope, qg, kg, sinks, y, dy, bsz, seq):
    t = proj.shape[0]
    nb = seq // SW_BLOCK
    seg = jnp.asarray(_sw_constants()[2], _MXU_DTYPE)
    cur, prev = _sw_specs(nb)

    def body(q_ref, kc_ref, kp_ref, vc_ref, vp_ref, rc_ref, rp_ref, qg_ref, kg_ref, sk_ref, seg_ref,
             y_ref, dy_ref, dp_ref, dqg_ref, dkg_ref, dsk_ref,
             dq_car, dkv_car, dqr_s, dkc_s, dkp_s, dvc_s, dvp_s, gq_acc, gk_acc, sk_acc):
        b, n = pl.program_id(0), pl.program_id(1)
        first = jnp.logical_and(b == 0, n == 0)
        last = jnp.logical_and(b == pl.num_programs(0) - 1, n == nb)

        @pl.when(first)
        def _():
            gq_acc[...] = jnp.zeros_like(gq_acc)
            gk_acc[...] = jnp.zeros_like(gk_acc)
            sk_acc[...] = jnp.zeros_like(sk_acc)

        @pl.when(n < nb)
        def _():
            segv = seg_ref[...]
            cos_c, sin_c = rc_ref[:, 0:128], rc_ref[:, 128:256]
            cos_p, sin_p = rp_ref[:, 0:128], rp_ref[:, 128:256]
            qv, kcv, kpv = q_ref[...], kc_ref[...], kp_ref[...]
            qgain = qg_ref[...] * SW_SCALE
            qr, rq = _sw_norm_rope(qv, qgain, segv, cos_c, sin_c)
            kcr, rkc = _sw_norm_rope(kcv, kg_ref[...], segv, cos_c, sin_c)
            kpr, rkp = _sw_norm_rope(kpv, kg_ref[...], segv, cos_p, sin_p)
            vc, vp = vc_ref[...], vp_ref[...]
            lane = lax.broadcasted_iota(jnp.int32, (1, 128), 1)
            dsk = jnp.zeros((1, 128), F32)
            heads = range(SW_HEADS)
            ks = [slice(SW_HD * (h // SW_GROUP), SW_HD * (h // SW_GROUP + 1)) for h in heads]
            hs = [slice(SW_HD * h, SW_HD * (h + 1)) for h in heads]
            qh = [qr[:, hs[h]] for h in heads]
            doh = [dy_ref[:, hs[h]] for h in heads]
            raw = [_sw_scores(qh[h], kpr[:, ks[h]], kcr[:, ks[h]]) for h in heads]
            dpp = [_dot(doh[h], vp[:, ks[h]], NT) for h in heads]
            dpc = [_dot(doh[h], vc[:, ks[h]], NT) for h in heads]
            probs = [_sw_probs(raw[h], sk_ref[0, h], n == 0) for h in heads]
            dsp, dsc = [], []
            for h in heads:
                pp, pc, ps = probs[h]
                delta = jnp.sum(doh[h] * y_ref[:, hs[h]], axis=1, keepdims=True)
                dsp.append(pp * (dpp[h] - delta))
                dsc.append(pc * (dpc[h] - delta))
                dsk = dsk + jnp.where(lane == h, -jnp.sum(ps * delta), 0.0)
            for h in heads:
                dqr_s[:, hs[h]] = _dot(dsp[h], kpr[:, ks[h]]) + _dot(dsc[h], kcr[:, ks[h]])
            for kv in range(SW_KV_HEADS):
                group = range(SW_GROUP * kv, SW_GROUP * (kv + 1))
                kvs = slice(SW_HD * kv, SW_HD * (kv + 1))
                dvp_s[:, kvs] = sum(_dot(probs[h][0], doh[h], TN) for h in group)
                dvc_s[:, kvs] = sum(_dot(probs[h][1], doh[h], TN) for h in group)
                dkp_s[:, kvs] = sum(_dot(dsp[h], qh[h], TN) for h in group)
                dkc_s[:, kvs] = sum(_dot(dsc[h], qh[h], TN) for h in group)
            dq, gq = _sw_norm_rope_bwd(dqr_s[...], qv, rq, qgain, segv, cos_c, sin_c)
            dkc, gkc = _sw_norm_rope_bwd(dkc_s[...], kcv, rkc, kg_ref[...], segv, cos_c, sin_c)
            dkp, gkp = _sw_norm_rope_bwd(dkp_s[...], kpv, rkp, kg_ref[...], segv, cos_p, sin_p)
            gq_acc[...] += gq
            gk_acc[...] += gkc + gkp
            sk_acc[...] += dsk

            @pl.when(n > 0)
            def _():
                dp_ref[:, 0:512] = _mx(dq_car[...])
                dp_ref[:, 512:640] = _mx(dkv_car[:, 0:128] + dkp)
                dp_ref[:, 640:768] = _mx(dkv_car[:, 128:256] + dvp_s[...])

            dq_car[...] = dq
            dkv_car[:, 0:128] = dkc
            dkv_car[:, 128:256] = dvc_s[...]

        @pl.when(n == nb)
        def _():
            dp_ref[:, 0:512] = _mx(dq_car[...])
            dp_ref[:, 512:768] = _mx(dkv_car[...])

        @pl.when(last)
        def _():
            gq = gq_acc[...] * SW_SCALE
            acc = gq[:, 0:SW_HD]
            for h in range(1, SW_HEADS):
                acc = acc + gq[:, SW_HD * h:SW_HD * (h + 1)]
            dqg_ref[...] = acc
            gk = gk_acc[...]
            dkg_ref[...] = gk[:, 0:SW_HD] + gk[:, SW_HD:2 * SW_HD]
            dsk_ref[...] = sk_acc[...]

    rowq = pl.BlockSpec((SW_BLOCK, 512), lambda b, n: (cur(b, n), 0))
    full = lambda a: pl.BlockSpec(a.shape, lambda b, n: (0,) * a.ndim)

    def out_row(b, n):
        return b * nb + jnp.maximum(n - 1, 0)

    return pl.pallas_call(
        body, name="swa_bwd", grid=(bsz, nb + 1),
        in_specs=[rowq,
                  pl.BlockSpec((SW_BLOCK, 128), lambda b, n: (cur(b, n), 4)),
                  pl.BlockSpec((SW_BLOCK, 128), lambda b, n: (prev(b, n), 4)),
                  pl.BlockSpec((SW_BLOCK, 128), lambda b, n: (cur(b, n), 5)),
                  pl.BlockSpec((SW_BLOCK, 128), lambda b, n: (prev(b, n), 5)),
                  pl.BlockSpec((SW_BLOCK, 256), lambda b, n: (cur(b, n), 0)),
                  pl.BlockSpec((SW_BLOCK, 256), lambda b, n: (prev(b, n), 0)),
                  full(qg), full(kg),
                  pl.BlockSpec(memory_space=pltpu.SMEM),
                  full(seg),
                  pl.BlockSpec((SW_BLOCK, 512), lambda b, n: (cur(b, n), 1)),
                  pl.BlockSpec((SW_BLOCK, 512), lambda b, n: (cur(b, n), 1))],
        out_specs=(pl.BlockSpec((SW_BLOCK, SW_COLS), lambda b, n: (out_row(b, n), 0)),
                   pl.BlockSpec((1, SW_HD), lambda b, n: (0, 0)),
                   pl.BlockSpec((1, SW_HD), lambda b, n: (0, 0)),
                   pl.BlockSpec((1, 128), lambda b, n: (0, 0))),
        out_shape=(jax.ShapeDtypeStruct((t, SW_COLS), _MXU_DTYPE),
                   jax.ShapeDtypeStruct((1, SW_HD), F32),
                   jax.ShapeDtypeStruct((1, SW_HD), F32),
                   jax.ShapeDtypeStruct((1, 128), F32)),
        scratch_shapes=[pltpu.VMEM((SW_BLOCK, 512), F32), pltpu.VMEM((SW_BLOCK, 256), F32),
                        pltpu.VMEM((SW_BLOCK, 512), F32),
                        pltpu.VMEM((SW_BLOCK, 128), F32), pltpu.VMEM((SW_BLOCK, 128), F32),
                        pltpu.VMEM((SW_BLOCK, 128), F32), pltpu.VMEM((SW_BLOCK, 128), F32),
                        pltpu.VMEM((1, 512), F32), pltpu.VMEM((1, 128), F32), pltpu.VMEM((1, 128), F32)],
        compiler_params=_params(("arbitrary", "arbitrary")),
    )(proj, proj, proj, proj, proj, rope, rope, qg, kg, sinks, seg, y, dy)


def _head_rms(tv, gain):
    r = lax.rsqrt(jnp.mean(tv * tv, axis=1, keepdims=True) + EPS)
    return tv * r * gain, r


def _head_rms_bwd(dtn, tv, r, gain):
    u = dtn * gain
    return r * u - tv * (r * r * r) * jnp.mean(u * tv, axis=1, keepdims=True), jnp.sum(dtn * tv * r, axis=0, keepdims=True)


def _xa_softmax(raw):
    s = raw * (XA_HD ** -0.5)
    e = jnp.exp(s - jnp.max(s, axis=1, keepdims=True))
    return e * (1.0 / jnp.sum(e, axis=1, keepdims=True))


def _xa_fwd(qx, kvx, qg, kg, bsz, seq, mlen, *, tq=512):
    t = qx.shape[0]
    tq = min(tq, seq)
    nq = seq // tq
    w = XA_HEADS * XA_HD

    def body(q_ref, kv_ref, qg_ref, kg_ref, o_ref):
        heads = range(XA_HEADS)
        hs = [slice(XA_HD * h, XA_HD * (h + 1)) for h in heads]
        qn = [_head_rms(q_ref[:, hs[h]], qg_ref[...])[0] for h in heads]
        kn = [_head_rms(kv_ref[:, hs[h]], kg_ref[...])[0] for h in heads]
        raw = [_dot(qn[h], kn[h], NT) for h in heads]
        p = [_xa_softmax(raw[h]) for h in heads]
        for h in heads:
            o_ref[:, hs[h]] = _dot(p[h], kv_ref[:, w + XA_HD * h:w + XA_HD * (h + 1)]).astype(o_ref.dtype)

    vec = pl.BlockSpec((1, XA_HD), lambda b, i: (0, 0))
    return pl.pallas_call(
        body, name="xattn_fwd", grid=(bsz, nq),
        in_specs=[pl.BlockSpec((tq, w), lambda b, i: (b * nq + i, 0)),
                  pl.BlockSpec((mlen, 2 * w), lambda b, i: (b, 0)), vec, vec],
        out_specs=pl.BlockSpec((tq, w), lambda b, i: (b * nq + i, 0)),
        out_shape=jax.ShapeDtypeStruct((t, w), _MXU_DTYPE),
        compiler_params=_params(("parallel", "parallel")),
    )(qx, kvx, qg, kg)


def _xa_bwd(qx, kvx, qg, kg, do, bsz, seq, mlen, *, tq=512):
    t = qx.shape[0]
    tq = min(tq, seq)
    nq = seq // tq
    w = XA_HEADS * XA_HD
    scale = XA_HD ** -0.5

    def body(q_ref, kv_ref, qg_ref, kg_ref, do_ref, dq_ref, dkv_ref, dqg_ref, dkg_ref):
        b, i = pl.program_id(0), pl.program_id(1)

        @pl.when(jnp.logical_and(b == 0, i == 0))
        def _():
            dqg_ref[...] = jnp.zeros_like(dqg_ref)
            dkg_ref[...] = jnp.zeros_like(dkg_ref)

        @pl.when(i == 0)
        def _():
            dkv_ref[...] = jnp.zeros_like(dkv_ref)

        heads = range(XA_HEADS)
        hs = [slice(XA_HD * h, XA_HD * (h + 1)) for h in heads]
        vs = [slice(w + XA_HD * h, w + XA_HD * (h + 1)) for h in heads]
        qv = [q_ref[:, hs[h]] for h in heads]
        kv = [kv_ref[:, hs[h]] for h in heads]
        doh = [do_ref[:, hs[h]] for h in heads]
        qn = [_head_rms(qv[h], qg_ref[...]) for h in heads]
        kn = [_head_rms(kv[h], kg_ref[...]) for h in heads]
        raw = [_dot(qn[h][0], kn[h][0], NT) for h in heads]
        dp = [_dot(doh[h], kv_ref[:, vs[h]], NT) for h in heads]
        p = [_xa_softmax(raw[h]) for h in heads]
        ds = [p[h] * (dp[h] - jnp.sum(p[h] * dp[h], axis=1, keepdims=True)) * scale for h in heads]
        dqn = [_dot(ds[h], kn[h][0]) for h in heads]
        dkn = [_dot(ds[h], qn[h][0], TN) for h in heads]
        dvv = [_dot(p[h], doh[h], TN) for h in heads]
        gq_sum = jnp.zeros((1, XA_HD), F32)
        gk_sum = jnp.zeros((1, XA_HD), F32)
        for h in heads:
            dqv, gq = _head_rms_bwd(dqn[h], qv[h], qn[h][1], qg_ref[...])
            dkv, gk = _head_rms_bwd(dkn[h], kv[h], kn[h][1], kg_ref[...])
            dq_ref[:, hs[h]] = dqv.astype(dq_ref.dtype)
            dkv_ref[:, hs[h]] += dkv
            dkv_ref[:, vs[h]] += dvv[h]
            gq_sum = gq_sum + gq
            gk_sum = gk_sum + gk
        dqg_ref[...] += gq_sum
        dkg_ref[...] += gk_sum

    vec = pl.BlockSpec((1, XA_HD), lambda b, i: (0, 0))
    row = pl.BlockSpec((tq, w), lambda b, i: (b * nq + i, 0))
    mem = pl.BlockSpec((mlen, 2 * w), lambda b, i: (b, 0))
    return pl.pallas_call(
        body, name="xattn_bwd", grid=(bsz, nq),
        in_specs=[row, mem, vec, vec, row],
        out_specs=(row, mem, vec, vec),
        out_shape=(jax.ShapeDtypeStruct((t, w), _MXU_DTYPE), jax.ShapeDtypeStruct((bsz * mlen, 2 * w), F32),
                   jax.ShapeDtypeStruct((1, XA_HD), F32), jax.ShapeDtypeStruct((1, XA_HD), F32)),
        compiler_params=_params(("arbitrary", "arbitrary")),
    )(qx, kvx, qg, kg, do)


def _loss_finish(sq_row, d_model):
    def body(s_ref, o_ref):
        o_ref[...] = jnp.zeros_like(o_ref) + 0.5 * jnp.sum(s_ref[...]) / float(d_model)

    return pl.pallas_call(body, name="loss_finish", out_shape=jax.ShapeDtypeStruct((1, 128), F32))(sq_row)


def _adamw_math(w, g, m, v):
    m = ADAM_B1 * m + (1.0 - ADAM_B1) * g
    v = ADAM_B2 * v + (1.0 - ADAM_B2) * (g * g)
    m_hat = m / (1.0 - ADAM_B1 ** ADAM_STEP)
    v_hat = v / (1.0 - ADAM_B2 ** ADAM_STEP)
    return -ADAM_LR * (m_hat / (jnp.sqrt(v_hat) + ADAM_EPS) + ADAM_WD * w), m, v


def _adamw_big(ws, gs, ms, vs, *, steps=8):
    n = len(ws)

    def body(*refs):
        for a in range(n):
            gv = refs[n + a][...]
            d, mn, vn = _adamw_math(refs[a][...], gv, refs[2 * n + a][...], refs[3 * n + a][...])
            refs[4 * n + 4 * a][...] = gv
            refs[4 * n + 4 * a + 1][...] = d
            refs[4 * n + 4 * a + 2][...] = mn
            refs[4 * n + 4 * a + 3][...] = vn

    def spec(w):
        assert w.shape[0] % (8 * steps) == 0, w.shape
        return pl.BlockSpec((w.shape[0] // steps, w.shape[1]), lambda i: (i, 0))

    specs = [spec(w) for w in ws]
    out = pl.pallas_call(
        body, name="adamw_big", grid=(steps,), in_specs=specs * 4,
        out_specs=tuple(s for s in specs for _ in range(4)),
        out_shape=tuple(jax.ShapeDtypeStruct(w.shape, F32) for w in ws for _ in range(4)),
        compiler_params=_params(("parallel",)),
    )(*ws, *gs, *ms, *vs)
    return [out[4 * a:4 * a + 4] for a in range(n)]


def _adamw_small(ws, gs, ms, vs):
    n = len(ws)

    def body(*refs):
        for i in range(n):
            d, mn, vn = _adamw_math(refs[i][...], refs[n + i][...], refs[2 * n + i][...], refs[3 * n + i][...])
            refs[4 * n + i][...] = d
            refs[5 * n + i][...] = mn
            refs[6 * n + i][...] = vn

    shapes = tuple(jax.ShapeDtypeStruct(w.shape, F32) for w in ws)
    return pl.pallas_call(body, name="adamw_small", out_shape=shapes * 3)(*ws, *gs, *ms, *vs)


def _add_halves(gs, recvs, c_idx, *, name):
    n = len(gs)

    def body(c_ref, *refs):
        del c_ref
        for a in range(n):
            refs[2 * n + a][...] = refs[a][...] + refs[n + a][...]

    def half(g):
        return pl.BlockSpec((None, g.shape[1] // 2, g.shape[2]), lambda k, cr: (k, cr[0], 0))

    def whole(g):
        return pl.BlockSpec((None, g.shape[1] // 2, g.shape[2]), lambda k, cr: (k, 0, 0))

    return pl.pallas_call(
        body, name=name,
        grid_spec=pltpu.PrefetchScalarGridSpec(
            num_scalar_prefetch=1, grid=(4,),
            in_specs=[half(g) for g in gs] + [whole(g) for g in gs],
            out_specs=tuple(whole(g) for g in gs)),
        out_shape=tuple(jax.ShapeDtypeStruct((4, g.shape[1] // 2, g.shape[2]), F32) for g in gs),
        compiler_params=_params(("parallel",)),
    )(c_idx, *gs, *recvs)


def _add_chips(ps, recvs, place_idx, *, name, steps=2, after=()):
    n = len(ps)

    def body(pi_ref, *refs):
        del pi_ref
        outs = refs[2 * n + len(after):]
        for a in range(n):
            r_ref = refs[n + a]
            outs[a][...] = ((refs[a][...] + r_ref[0]) + r_ref[1]) + r_ref[2]

    def tile(p):
        assert p.shape[1] % (8 * steps) == 0, (name, p.shape)
        return p.shape[1] // steps

    return pl.pallas_call(
        body, name=name,
        grid_spec=pltpu.PrefetchScalarGridSpec(
            num_scalar_prefetch=1, grid=(steps,),
            in_specs=[pl.BlockSpec((None, tile(p), p.shape[2]), lambda i, pi: (pi[0], i, 0)) for p in ps]
            + [pl.BlockSpec((3, tile(p), p.shape[2]), lambda i, pi: (0, i, 0)) for p in ps]
            + [pl.BlockSpec(memory_space=pl.ANY)] * len(after),
            out_specs=tuple(pl.BlockSpec((tile(p), p.shape[2]), lambda i, pi: (pi[1] * steps + i, 0)) for p in ps)),
        out_shape=tuple(jax.ShapeDtypeStruct((2 * p.shape[1], p.shape[2]), F32) for p in ps),
        compiler_params=_params(("parallel",)),
    )(place_idx, *ps, *recvs, *after)


def _place_shards(shards, place_idx, *, name, after=()):
    n = len(shards)

    def body(pi_ref, *refs):
        del pi_ref
        for i in range(n):
            refs[n + len(after) + i][...] = refs[i][...]

    return pl.pallas_call(
        body, name=name,
        grid_spec=pltpu.PrefetchScalarGridSpec(
            num_scalar_prefetch=1, grid=(1,),
            in_specs=[pl.BlockSpec(s.shape, lambda i, pi: (0, 0)) for s in shards]
            + [pl.BlockSpec(memory_space=pl.ANY)] * len(after),
            out_specs=tuple(pl.BlockSpec((None,) + s.shape, lambda i, pi: (pi[0], 0, 0)) for s in shards)),
        out_shape=tuple(jax.ShapeDtypeStruct((4,) + s.shape, s.dtype) for s in shards),
        compiler_params=_params(("arbitrary",)),
    )(place_idx, *shards, *after)


def _place_shard(shard, place_idx, *, name, tr=512, after=()):
    r, c = shard.shape
    tr = min(tr, r)
    if r % tr:
        tr = r // 2
    assert r % tr == 0 and tr % 16 == 0, (name, r, tr)

    def body(pi_ref, s_ref, *rest):
        del pi_ref
        rest[-1][...] = s_ref[...]

    return pl.pallas_call(
        body, name=name,
        grid_spec=pltpu.PrefetchScalarGridSpec(
            num_scalar_prefetch=1, grid=(r // tr,),
            in_specs=[pl.BlockSpec((tr, c), lambda i, pi: (i, 0))] + [pl.BlockSpec(memory_space=pl.ANY)] * len(after),
            out_specs=pl.BlockSpec((None, tr, c), lambda i, pi: (pi[0], i, 0))),
        out_shape=jax.ShapeDtypeStruct((4, r, c), shard.dtype),
        compiler_params=_params(("parallel",)),
    )(place_idx, shard, *after)


def _place():
    x, y, c = lax.axis_index("x"), lax.axis_index("y"), lax.axis_index("c")
    chips = [(1 - x, y), (x, 1 - y), (1 - x, 1 - y)]
    return x, y, c, chips


ANY = pl.BlockSpec(memory_space=pl.ANY)


def _exchange_halves(grads, name):
    n = len(grads)

    def body(*refs):
        ins, outs = refs[:n], refs[n:2 * n]
        send_sems, recv_sems = refs[2 * n:]
        x, y, c, _ = _place()

        def copy(a):
            h = ins[a].shape[1] // 2
            return pltpu.make_async_remote_copy(
                src_ref=ins[a].at[:, pl.ds((1 - c) * h, h), :], dst_ref=outs[a],
                send_sem=send_sems.at[a], recv_sem=recv_sems.at[a], device_id=(x, y, 1 - c), device_id_type=MESH)

        for a in range(n):
            copy(a).start()
        for a in range(n):
            copy(a).wait_recv()
        for a in range(n):
            copy(a).wait_send()

    return pl.pallas_call(
        body, name=name,
        in_specs=[ANY] * n, out_specs=tuple([ANY] * n),
        out_shape=tuple(jax.ShapeDtypeStruct((4, g.shape[1] // 2, g.shape[2]), g.dtype) for g in grads),
        scratch_shapes=[pltpu.SemaphoreType.DMA((n,)), pltpu.SemaphoreType.DMA((n,))],
    )(*grads)


HBM = pl.BlockSpec(memory_space=pltpu.HBM)
SEM = pl.BlockSpec(memory_space=pltpu.SEMAPHORE)
EFFECT = pltpu.SideEffectType.DATAFLOW_SIDE_EFFECTING


def _in_hbm(a):
    return pltpu.with_memory_space_constraint(a, pltpu.HBM)


def _split_copy_calls(name, srcs, lands, n_copies, make_copies):
    ns, nl = len(srcs), len(lands)
    nb = ns + nl

    def start(after=()):
        n_after = len(after)

        def body(*refs):
            outs = refs[nb + n_after:]
            copies = make_copies(refs[:ns], refs[ns:nb], outs[0], outs[1])
            for cp in copies:
                cp.start()
            token = refs[-1]
            token[...] = jnp.zeros_like(token)

        bufs = [_in_hbm(a) for a in list(srcs) + list(lands)]
        out = pl.pallas_call(
            body, name=name + "_start",
            out_shape=(pltpu.SemaphoreType.DMA((n_copies,)), pltpu.SemaphoreType.DMA((n_copies,)),
                       *[pltpu.HBM(a.shape, a.dtype) for a in bufs], jax.ShapeDtypeStruct((8, 128), F32)),
            in_specs=[HBM] * nb + [pl.BlockSpec(memory_space=pl.ANY)] * n_after,
            out_specs=(SEM, SEM, *[HBM] * nb, pl.BlockSpec(memory_space=pltpu.VMEM)),
            input_output_aliases={i: 2 + i for i in range(nb)},
            compiler_params=pltpu.CompilerParams(has_side_effects=EFFECT),
        )(*bufs, *after)
        return dict(send=out[0], recv=out[1], bufs=list(out[2:2 + nb]), token=out[-1])

    def wait(state, after):
        def body(*refs):
            copies = make_copies(refs[:ns], refs[ns:nb], refs[nb], refs[nb + 1])
            for cp in copies:
                cp.wait_send()
            for cp in copies:
                cp.wait_recv()

        bufs = state["bufs"]
        out = pl.pallas_call(
            body, name=name + "_wait",
            out_shape=tuple(pltpu.HBM(a.shape, a.dtype) for a in bufs),
            in_specs=[HBM] * nb + [SEM, SEM] + [pl.BlockSpec(memory_space=pl.ANY)] * len(after),
            out_specs=tuple([HBM] * nb),
            input_output_aliases={i: i for i in range(nb)},
            compiler_params=pltpu.CompilerParams(has_side_effects=EFFECT),
        )(*bufs, state["send"], state["recv"], *after)
        return list(out[:ns]), list(out[ns:])

    return start, wait


def _scatter_chips_split(name, parts):
    n = len(parts)
    lands = [lax.empty((3,) + p.shape[1:], p.dtype) for p in parts]

    def make_copies(srcs, lnds, send_sems, recv_sems):
        _, _, c, chips = _place()
        return [pltpu.make_async_remote_copy(
            src_ref=srcs[a].at[2 * px + py], dst_ref=lnds[a].at[j], send_sem=send_sems.at[a * 3 + j],
            recv_sem=recv_sems.at[a * 3 + j], device_id=(px, py, c), device_id_type=MESH)
            for a in range(n) for j, (px, py) in enumerate(chips)]

    return _split_copy_calls(name, parts, lands, 3 * n, make_copies)


def _exchange_halves_split(name, grads):
    n = len(grads)
    lands = [lax.empty((4, g.shape[1] // 2, g.shape[2]), g.dtype) for g in grads]

    def make_copies(srcs, lnds, send_sems, recv_sems):
        x, y, c, _ = _place()
        out = []
        for a in range(n):
            h = srcs[a].shape[1] // 2
            out.append(pltpu.make_async_remote_copy(
                src_ref=srcs[a].at[:, pl.ds((1 - c) * h, h), :], dst_ref=lnds[a], send_sem=send_sems.at[a],
                recv_sem=recv_sems.at[a], device_id=(x, y, 1 - c), device_id_type=MESH))
        return out

    return _split_copy_calls(name, grads, lands, n, make_copies)


def _gather_chips_split(name, shards, lands):
    n = len(shards)

    def make_copies(srcs, lnds, send_sems, recv_sems):
        x, y, c, chips = _place()
        out = []
        for a in range(n):
            h = srcs[a].shape[0] // 2
            for j, (px, py) in enumerate(chips):
                out.append(pltpu.make_async_remote_copy(
                    src_ref=srcs[a].at[pl.ds(c * h, h), :], dst_ref=lnds[a].at[2 * x + y, pl.ds(c * h, h), :],
                    send_sem=send_sems.at[a * 3 + j], recv_sem=recv_sems.at[a * 3 + j],
                    device_id=(px, py, c), device_id_type=MESH))
        return out

    return _split_copy_calls(name, shards, lands, 3 * n, make_copies)


def _gather_finish(gathered, name):
    n = len(gathered)

    def body(*refs):
        outs = refs[n:2 * n]
        send_sems, recv_sems = refs[2 * n:]
        x, y, c, chips = _place()

        def copy(a, j, chip_idx, which):
            h = outs[a].shape[1] // 2
            rows = outs[a].at[chip_idx, pl.ds(which * h, h), :]
            return pltpu.make_async_remote_copy(
                src_ref=rows, dst_ref=rows, send_sem=send_sems.at[a * 3 + j], recv_sem=recv_sems.at[a * 3 + j],
                device_id=(x, y, 1 - c), device_id_type=MESH)

        for a in range(n):
            for j, (px, py) in enumerate(chips):
                copy(a, j, 2 * px + py, c).start()
        for a in range(n):
            for j, (px, py) in enumerate(chips):
                copy(a, j, 2 * px + py, 1 - c).wait_recv()
        for a in range(n):
            for j, (px, py) in enumerate(chips):
                copy(a, j, 2 * px + py, c).wait_send()

    return pl.pallas_call(
        body, name=name,
        in_specs=[ANY] * n, out_specs=tuple([ANY] * n),
        out_shape=tuple(jax.ShapeDtypeStruct(g.shape, g.dtype) for g in gathered),
        input_output_aliases={i: i for i in range(n)},
        scratch_shapes=[pltpu.SemaphoreType.DMA((3 * n,)), pltpu.SemaphoreType.DMA((3 * n,))],
    )(*gathered)


def _gather_forward_split(name, gathered):
    n = len(gathered)

    def make_copies(srcs, lnds, send_sems, recv_sems):
        x, y, c, chips = _place()
        out = []
        for a in range(n):
            h = lnds[a].shape[1] // 2
            for j, (px, py) in enumerate(chips):
                rows = lnds[a].at[2 * px + py, pl.ds(c * h, h), :]
                out.append(pltpu.make_async_remote_copy(
                    src_ref=rows, dst_ref=rows, send_sem=send_sems.at[a * 3 + j], recv_sem=recv_sems.at[a * 3 + j],
                    device_id=(x, y, 1 - c), device_id_type=MESH))
        return out

    return _split_copy_calls(name, [], gathered, 3 * n, make_copies)


def _join_halves(fulls):
    n = len(fulls)

    def body(*refs):
        outs = refs[n:2 * n]
        send_sems, recv_sems = refs[2 * n:]
        x, y, c, _ = _place()

        def copy(a, which):
            h = outs[a].shape[0] // 2
            rows = outs[a].at[pl.ds(which * h, h), :]
            return pltpu.make_async_remote_copy(
                src_ref=rows, dst_ref=rows, send_sem=send_sems.at[a], recv_sem=recv_sems.at[a],
                device_id=(x, y, 1 - c), device_id_type=MESH)

        for a in range(n):
            copy(a, c).start()
        for a in range(n):
            copy(a, 1 - c).wait_recv()
        for a in range(n):
            copy(a, c).wait_send()

    return pl.pallas_call(
        body, name="rs_join_halves",
        in_specs=[ANY] * n, out_specs=tuple([ANY] * n),
        out_shape=tuple(jax.ShapeDtypeStruct(p.shape, p.dtype) for p in fulls),
        input_output_aliases={i: i for i in range(n)},
        scratch_shapes=[pltpu.SemaphoreType.DMA((n,)), pltpu.SemaphoreType.DMA((n,))],
    )(*fulls)


def _all_gather_small_split(sm):
    r, w = sm.shape

    def make_copies(srcs, lnds, send_sems, recv_sems):
        x, y, c, _ = _place()
        me = 4 * x + 2 * y + c
        rel = [(dx, dy, dc) for dx in (0, 1) for dy in (0, 1) for dc in (0, 1)][1:]
        return [pltpu.make_async_remote_copy(
            src_ref=srcs[0], dst_ref=lnds[0].at[me], send_sem=send_sems.at[k], recv_sem=recv_sems.at[k],
            device_id=(1 - x if dx else x, 1 - y if dy else y, 1 - c if dc else c), device_id_type=MESH)
            for k, (dx, dy, dc) in enumerate(rel)]

    return _split_copy_calls("all_gather_small", [sm], [lax.empty((8, r, w), sm.dtype)], 7, make_copies)


def _sum_devices(sm, gathered, me_idx):
    def body(me_ref, sm_ref, g_ref, o_ref):
        own = sm_ref[...]
        acc = jnp.where(me_ref[0] == 0, own, g_ref[0])
        for d in range(1, 8):
            acc = acc + jnp.where(me_ref[0] == d, own, g_ref[d])
        o_ref[...] = acc

    vm = pl.BlockSpec(memory_space=pltpu.VMEM)
    return pl.pallas_call(
        body, name="sum_devices", in_specs=[pl.BlockSpec(memory_space=pltpu.SMEM), vm, vm], out_specs=vm,
        out_shape=jax.ShapeDtypeStruct(sm.shape, F32),
    )(me_idx, sm, gathered)


def _local_step(x3, mem3, pos2, target3, small, comm):
    bsz, seq, d = x3.shape
    mlen = mem3.shape[1]
    t = bsz * seq
    tok = comm.begin()
    x = x3.reshape(t, d)
    mem = mem3.reshape(bsz * mlen, d)
    target = target3.reshape(t, d)
    rope = _rope_table(pos2.reshape(t, 1), after=tok)
    qg_t = jnp.tile(small["sw_q_norm_g"], (1, SW_HEADS))
    kg_t = jnp.tile(small["sw_k_norm_g"], (1, SW_KV_HEADS))

    hn1 = _rms_fwd(x, small["norm1_g"], name="rms1_fwd", after=tok)
    w = comm.first((hn1, rope))
    w_in_t = w["w_in_t"]
    w_sw_t = w_in_t[HG_COLS:]
    proj_hg = _mm(hn1, w_in_t, NT, t, HG_COLS, d, name="proj_hg", tk=d, after=(w.get("token"),))[0]
    proj_sw = _mm(hn1, w_sw_t, NT, t, SW_COLS, d, name="proj_sw", tk=d)[0]
    y_mix, o_hg, states = _hg_fwd(proj_hg, small["hg_lower_bounds"], small["hg_norm_g"], bsz, seq, y_width=1024)
    y_mix = _sw_fwd(proj_sw, rope, qg_t, kg_t, small["sw_sinks"], y_mix, bsz, seq)
    w = comm.rest(y_mix)
    h1, hn2 = _mm(y_mix, w["w_out"], NN, t, d, 1024, name="out_proj", tk=1024, extras=(x,), rows=(small["norm2_g"],),
                  epilogue=_residual_rms, out_dtypes=(F32, _MXU_DTYPE), after=(w.get("token"),))
    mn = _rms_fwd(mem, small["mem_norm_g"], name="rms_mem_fwd")
    qx = _mm(hn2, w["wq"], NN, t, 512, d, name="xa_q", tk=d)[0]
    kvx = _mm(mn, w["wkv"], NN, bsz * mlen, 1024, d, name="xa_kv", tk=d)[0]
    ox = _xa_fwd(qx, kvx, small["xa_q_norm_g"], small["xa_k_norm_g"], bsz, seq, mlen)
    h2, hn3 = _mm(ox, w["wo"], NN, t, d, 512, name="xa_o", tk=512, extras=(h1,), rows=(small["norm3_g"],),
                  epilogue=_residual_rms, out_dtypes=(F32, _MXU_DTYPE))
    w = {**w, **comm.mlp(hn3)}
    ff = w["down"].shape[0]
    ffs = ff // 4

    def relu_sq(acc):
        a = jnp.maximum(acc, 0.0)
        return a, a * a

    act, act2 = _mm(hn3, w["up"], NN, t, ff, d, name="mlp_up", tm=2048, tn=ffs, tk=d,
                    b_spec=pl.BlockSpec((None, d, ffs), lambda i, j, kk: (j, 0, 0)),
                    epilogue=relu_sq, out_dtypes=(_MXU_DTYPE, _MXU_DTYPE))
    inv_d = 1.0 / d

    def loss_cotangent(acc, res, tgt):
        diff = acc + res - tgt
        v = diff * inv_d
        return v, v, jnp.sum(diff * diff, axis=0, keepdims=True)

    dy, dy_mx, sq_row = _mm(act2, w["down"], NN, t, d, ff, name="mlp_down", tk=2048, extras=(h2, target),
                            epilogue=loss_cotangent, out_dtypes=(F32, _MXU_DTYPE), row_sums=1)
    loss_row = _loss_finish(sq_row, d)

    dz = _mm(dy_mx, w["down"], NT, t, ff, d, name="d_act", tm=2048, tk=d, extras=(act,),
             epilogue=lambda acc, a: (acc * (2.0 * a.astype(F32)),), out_dtypes=(_MXU_DTYPE,))[0]
    g_down = _mm(act2, dy_mx, TN, ff, d, t, name="g_down", tk=t)[0]
    g_up = _mm(hn3, dz, TN, d, ff, t, name="g_up", tn=ffs, tk=t,
               out_shape=(jax.ShapeDtypeStruct((4, d, ffs), F32),),
               out_spec=(pl.BlockSpec((None, min(1024, d), ffs), lambda i, j, kk: (j, i, 0)),))[0]
    tok = comm.grads("mlp", dict(up=g_up, down=g_down))
    dh2, dh2_mx, g_norm3 = _mm(dz, w["up"], NT, t, d, ff, name="d_hn3", tk=ffs, after=tok,
                               b_spec=pl.BlockSpec((None, min(1024, d), ffs), lambda i, j, kk: (kk, j, 0)),
                               extras=(h2, dy), rows=(small["norm3_g"],), epilogue=_rms_bwd_residual,
                               out_dtypes=(F32, _MXU_DTYPE), row_sums=1)
    d_ox = _mm(dh2_mx, w["wo"], NT, t, 512, d, name="d_ox", tk=d)[0]
    g_wo = _mm(ox, dh2_mx, TN, 512, d, t, name="g_wo", tk=t)[0]
    d_qx, d_kvx, g_xq, g_xk = _xa_bwd(qx, kvx, small["xa_q_norm_g"], small["xa_k_norm_g"], d_ox, bsz, seq, mlen)
    g_wq = _mm(hn2, d_qx, TN, d, 512, t, name="g_wq")[0]
    g_wkv = _mm(mn, d_kvx, TN, d, 1024, bsz * mlen, name="g_wkv")[0]
    dh1, dh1_mx, g_norm2 = _mm(d_qx, w["wq"], NT, t, d, 512, name="d_hn2", tk=512, extras=(h1, dh2),
                               rows=(small["norm2_g"],), epilogue=_rms_bwd_residual, out_dtypes=(F32, _MXU_DTYPE),
                               row_sums=1)
    dmn = _mm(d_kvx, w["wkv"], NT, bsz * mlen, d, 1024, name="d_mn", tk=1024)[0]
    g_memn = _rms_gain_grad(mem, small["mem_norm_g"], dmn, name="rms_mem_bwd")
    g_wout = _mm(y_mix, dh1_mx, TN, 1024, d, t, name="g_wout", tk=2048)[0]
    tok = comm.grads("mid", dict(w_out=g_wout, wq=g_wq, wkv=g_wkv, wo=g_wo))
    d_mix = _mm(dh1_mx, w["w_out"], NT, t, 1024, d, name="d_mix", tk=d, after=tok)[0]
    dproj_sw, g_swq, g_swk, g_sinks = _sw_bwd(proj_sw, rope, qg_t, kg_t, small["sw_sinks"], y_mix, d_mix, bsz, seq)
    tok = comm.poll(dproj_sw)
    dproj_hg, g_lb, g_hgn = _hg_bwd(proj_hg, small["hg_lower_bounds"], small["hg_norm_g"], o_hg, states, d_mix, bsz, seq,
                                    after=tok)
    in_rows = HG_COLS + SW_COLS
    sw_tile = 256
    g_in_t = _mm(dproj_hg, hn1, TN, HG_COLS, d, t, name="g_in_hg", tk=t,
                 out_shape=(jax.ShapeDtypeStruct((in_rows, d), F32),),
                 out_spec=(pl.BlockSpec((1024, min(1024, d)), lambda i, j, kk: (i, j)),))[0]
    g_in_t = _mm(dproj_sw, hn1, TN, SW_COLS, d, t, name="g_in_sw", tm=sw_tile, into=g_in_t,
                 out_shape=(jax.ShapeDtypeStruct((in_rows, d), F32),),
                 out_spec=(pl.BlockSpec((sw_tile, min(1024, d)), lambda i, j, kk: (HG_COLS // sw_tile + i, j)),))[0]
    tok = comm.grads("in", dict(w_in_t=g_in_t))
    grad_x, g_norm1 = _mm(dproj_hg, w_in_t, NN, t, d, HG_COLS, name="d_hn1", tk=HG_COLS, second=(dproj_sw, w_sw_t),
                          extras=(x, dh1), rows=(small["norm1_g"],), row_sums=1, after=tok,
                          epilogue=lambda acc, xv, dres, g: _rms_bwd_residual(acc, xv, dres, g)[1:])

    g_small = dict(norm1_g=g_norm1, hg_lower_bounds=g_lb, hg_norm_g=g_hgn, sw_q_norm_g=g_swq, sw_k_norm_g=g_swk,
                   sw_sinks=g_sinks[:, 0:SW_HEADS], norm2_g=g_norm2, mem_norm_g=g_memn, xa_q_norm_g=g_xq,
                   xa_k_norm_g=g_xk, norm3_g=g_norm3)
    return loss_row, grad_x.reshape(bsz, seq, d), g_small


SMALL_NAMES = ("norm1_g", "hg_lower_bounds", "hg_norm_g", "sw_q_norm_g", "sw_k_norm_g", "sw_sinks", "norm2_g",
               "mem_norm_g", "xa_q_norm_g", "xa_k_norm_g", "norm3_g")
BIG_NAMES = ("w_in", "w_out", "xa_wq", "xa_wkv", "xa_wo", "mlp_up", "mlp_down")
WEIGHT_ORDER = ("norm1_g", "w_in", "hg_lower_bounds", "hg_norm_g", "sw_q_norm_g", "sw_k_norm_g", "sw_sinks", "w_out",
                "norm2_g", "mem_norm_g", "xa_wq", "xa_wkv", "xa_q_norm_g", "xa_k_norm_g", "xa_wo", "norm3_g",
                "mlp_up", "mlp_down")


def _pack_rows(vals, width):
    starts, at = [], 0
    for v in vals:
        starts.append(at)
        at += v.shape[0]
    total = at + (-at) % 8
    out = None
    for v, s in zip(vals, starts):
        placed = jnp.pad(v, ((s, total - s - v.shape[0]), (0, width - v.shape[1])))
        out = placed if out is None else out + placed
    return out, starts


class _MeshWeights:
    LATE = ("w_out", "xa_wq", "xa_wkv", "xa_wo", "mlp_up", "mlp_down")

    def __init__(self, shards, d, ff):
        self.shards, self.d, self.ff = shards, d, ff
        self.c_idx = lax.axis_index("c").astype(jnp.int32).reshape(1)
        chip = (2 * lax.axis_index("x") + lax.axis_index("y")).astype(jnp.int32)
        self.place_idx = jnp.stack([chip, lax.axis_index("c").astype(jnp.int32)])
        self.pending = []
        self.exchanging = None

    def begin(self):
        shard = self.shards["w_in"]
        start, self.in_wait = _gather_chips_split(
            "gather_in", [shard], [_place_shard(shard, self.place_idx, name="place_w_in")])
        self.in_state = start()
        tok = (self.in_state["token"],)
        self.placed = list(_place_shards([self.shards[n] for n in self.LATE], self.place_idx, name="place_late",
                                         after=tok))
        return tok

    def first(self, after):
        _, lands = self.in_wait(self.in_state, (*after, *self.placed))
        (g_in,) = _gather_finish(lands, "gather_in_finish")
        start, self.late_wait = _gather_chips_split("gather_late", [self.shards[n] for n in self.LATE], self.placed)
        self.late_state = start(after=(g_in,))
        return dict(w_in_t=g_in.reshape(-1, self.d), token=self.late_state["token"])

    def rest(self, after):
        _, lands = self.late_wait(self.late_state, (after,))
        g_out, g_q, g_kv, g_o = _gather_finish(lands[:4], "gather_late_finish")
        start, self.mlp_wait = _gather_forward_split("gather_mlp_forward", lands[4:])
        self.mlp_state = start(after=(g_out,))
        d = self.d
        return dict(w_out=g_out.reshape(-1, d), wq=g_q.reshape(d, -1), wkv=g_kv.reshape(d, -1),
                    wo=jnp.concatenate([g_o[k] for k in range(4)], axis=1), token=self.mlp_state["token"])

    def mlp(self, after):
        _, (g_up, g_dn) = self.mlp_wait(self.mlp_state, (after,))
        return dict(up=g_up, down=g_dn.reshape(self.ff, self.d))

    def _scatter(self, tag, names, arrays, recv):
        parts = list(_add_halves(arrays, recv, self.c_idx, name="rs_add_halves_" + tag))
        start, wait = _scatter_chips_split("rs_scatter_" + tag, parts)
        state = start()
        self.pending.append((names, wait, state))
        return state["token"]

    def _advance(self, after):
        if self.exchanging is None:
            return ()
        tag, names, wait, state = self.exchanging
        self.exchanging = None
        arrays, recv = wait(state, (after,))
        return (self._scatter(tag, names, arrays, recv),)

    def poll(self, after):
        return self._advance(after)

    def grads(self, tag, g):
        d, ff = self.d, self.ff
        if tag == "mlp":
            names, arrays = ("mlp_up", "mlp_down"), [g["up"], g["down"].reshape(4, ff // 4, d)]
        elif tag == "mid":
            names = ("w_out", "xa_wq", "xa_wkv", "xa_wo")
            ds = d // 4
            g_wo = jnp.stack([g["wo"][:, ds * k:ds * (k + 1)] for k in range(4)])
            arrays = [g["w_out"].reshape(4, -1, d), g["wq"].reshape(4, d // 4, -1), g["wkv"].reshape(4, d // 4, -1), g_wo]
        else:
            names, arrays = ("w_in",), [g["w_in_t"].reshape(4, -1, d)]
        toks = self._advance(arrays[0])
        if tag == "in":
            return toks + (self._scatter(tag, names, arrays, _exchange_halves(arrays, "rs_exchange_" + tag)),)
        start, wait = _exchange_halves_split("rs_exchange_" + tag, arrays)
        state = start()
        self.exchanging = (tag, names, wait, state)
        return toks + (state["token"],)

    def finish(self, after):
        halves, tok = {}, ()
        for names, wait, state in self.pending:
            srcs, lands = wait(state, tuple(after) + tok)
            fulls = _add_chips(srcs, lands, self.place_idx, name="rs_add_chips_" + names[0], after=tok)
            tok = (fulls[0],)
            halves.update(zip(names, fulls))
        return dict(zip(BIG_NAMES, _join_halves([halves[n] for n in BIG_NAMES])))


def kernel(x, mem, positions, norm1_g, w_in, hg_lower_bounds, hg_norm_g, sw_q_norm_g, sw_k_norm_g, sw_sinks, w_out, norm2_g, mem_norm_g, xa_wq, xa_wkv, xa_q_norm_g, xa_k_norm_g, xa_wo, norm3_g, mlp_up, mlp_down, loss_target, m_norm1_g, m_w_in, m_hg_lower_bounds, m_hg_norm_g, m_sw_q_norm_g, m_sw_k_norm_g, m_sw_sinks, m_w_out, m_norm2_g, m_mem_norm_g, m_xa_wq, m_xa_wkv, m_xa_q_norm_g, m_xa_k_norm_g, m_xa_wo, m_norm3_g, m_mlp_up, m_mlp_down, v_norm1_g, v_w_in, v_hg_lower_bounds, v_hg_norm_g, v_sw_q_norm_g, v_sw_k_norm_g, v_sw_sinks, v_w_out, v_norm2_g, v_mem_norm_g, v_xa_wq, v_xa_wkv, v_xa_q_norm_g, v_xa_k_norm_g, v_xa_wo, v_norm3_g, v_mlp_up, v_mlp_down):
    given = dict(locals())
    weights = {n: given[n] for n in WEIGHT_ORDER}
    moms = {n: given["m_" + n] for n in WEIGHT_ORDER}
    vars_ = {n: given["v_" + n] for n in WEIGHT_ORDER}
    d = x.shape[-1]
    ff = mlp_down.shape[1] * 4
    small = {n: weights[n] for n in SMALL_NAMES}

    def plain(n, a):
        return jnp.swapaxes(a[0], 0, 1) if n == "w_in" else a[0]

    comm = _MeshWeights({n: plain(n, weights[n]).astype(_MXU_DTYPE) for n in BIG_NAMES}, d, ff)
    loss_row, grad_x, g_small = _local_step(x, mem, positions, loss_target, small, comm)
    packed, starts = _pack_rows([g_small[n] for n in SMALL_NAMES] + [loss_row], 1024)
    start, wait = _all_gather_small_split(packed)
    state = start()
    big_grads = comm.finish((grad_x, state["token"]))
    (own,), (gathered,) = wait(state, (big_grads[BIG_NAMES[0]],))
    device = (4 * lax.axis_index("x") + 2 * lax.axis_index("y") + lax.axis_index("c")).astype(jnp.int32).reshape(1)
    summed = _sum_devices(own, gathered, device)
    small_grads = {}
    for n, s in zip(SMALL_NAMES, starts):
        r, c = weights[n].shape
        small_grads[n] = summed[s:s + r, 0:c]
    loss = summed[starts[-1], 0]

    grads, deltas, new_m, new_v = {}, {}, {}, {}
    big_out = _adamw_big([plain(n, weights[n]) for n in BIG_NAMES], [big_grads[n] for n in BIG_NAMES],
                         [plain(n, moms[n]) for n in BIG_NAMES], [plain(n, vars_[n]) for n in BIG_NAMES])
    for n, outs in zip(BIG_NAMES, big_out):
        grads[n], deltas[n], new_m[n], new_v[n] = ((jnp.swapaxes(a, 0, 1) if n == "w_in" else a)[None] for a in outs)
    sm_out = _adamw_small([weights[n] for n in SMALL_NAMES], [small_grads[n] for n in SMALL_NAMES],
                          [moms[n] for n in SMALL_NAMES], [vars_[n] for n in SMALL_NAMES])
    ns = len(SMALL_NAMES)
    for i, n in enumerate(SMALL_NAMES):
        grads[n], deltas[n], new_m[n], new_v[n] = small_grads[n], sm_out[i], sm_out[ns + i], sm_out[2 * ns + i]

    return (loss, grad_x, *[grads[n] for n in WEIGHT_ORDER], *[deltas[n] for n in WEIGHT_ORDER],
            *[new_m[n] for n in WEIGHT_ORDER], *[new_v[n] for n in WEIGHT_ORDER])
```

```python
import numpy as np
import jax
import jax.numpy as jnp
from jax import lax
from jax.experimental import pallas as pl
from jax.experimental.pallas import tpu as pltpu

F32 = jnp.float32
_MXU_DTYPE = jnp.bfloat16

EPS = 1e-6
HG_HEADS = 4
HG_D = 128
HG_CHUNK = 64
HG_TILE = 512
HG_LEVELS = (32, 16, 8, 4, 2, 1)
SW_HEADS = 8
SW_KV_HEADS = 2
SW_GROUP = SW_HEADS // SW_KV_HEADS
SW_HD = 64
SW_BLOCK = 128
ROPE_THETA = 500000.0
ROT_DIM = SW_HD // 4
XA_HEADS = 4
XA_HD = 128
HG_COLS = 4 * HG_HEADS * HG_D
SW_COLS = (SW_HEADS + 2 * SW_KV_HEADS) * SW_HD

ADAM_LR = 0.001
ADAM_B1 = 0.9
ADAM_B2 = 0.999
ADAM_EPS = 1e-08
ADAM_WD = 0.01
ADAM_STEP = 10

VMEM_LIMIT = 56 * 1024 * 1024
MESH = pl.DeviceIdType.MESH

NN = ((1,), (0,))
NT = ((1,), (1,))
TN = ((0,), (0,))


def _mx(v):
    return v.astype(_MXU_DTYPE)


def _dot(a, b, dims=NN):
    return lax.dot_general(_mx(a), _mx(b), (dims, ((), ())), preferred_element_type=F32)


def _split_dot(a, v, dims, parts):
    acc = None
    rest = v
    for p in range(parts):
        piece = _mx(rest)
        term = lax.dot_general(a, piece, (dims, ((), ())), preferred_element_type=F32)
        acc = term if acc is None else acc + term
        if p + 1 < parts:
            rest = rest - piece.astype(F32)
    return acc


def _params(sem):
    return pltpu.CompilerParams(dimension_semantics=sem, vmem_limit_bytes=VMEM_LIMIT)


def _mm(a, b, mode, m, n, k, *, name, tm=1024, tn=1024, tk=1024, a_spec=None, b_spec=None, extras=(), rows=(),
        epilogue=None, out_dtypes=(F32,), row_sums=0, out_shape=None, out_spec=None, after=(), into=None,
        second=None):
    after = tuple(t for t in after if t is not None) + (() if into is None else (into,))
    tm, tn, tk = min(tm, m), min(tn, n), min(tk, k)
    assert m % tm == 0 and n % tn == 0 and k % tk == 0, (name, m, n, k, tm, tn, tk)
    gi, gj, gk = m // tm, n // tn, k // tk
    assert row_sums == 0 or gj == 1, name
    if a_spec is None:
        a_spec = (pl.BlockSpec((tk, tm), lambda i, j, kk: (kk, i)) if mode == TN
                  else pl.BlockSpec((tm, tk), lambda i, j, kk: (i, kk)))
    if b_spec is None:
        b_spec = (pl.BlockSpec((tn, tk), lambda i, j, kk: (j, kk)) if mode == NT
                  else pl.BlockSpec((tk, tn), lambda i, j, kk: (kk, j)))
    mn_spec = pl.BlockSpec((tm, tn), lambda i, j, kk: (i, j))
    if epilogue is None:
        epilogue = lambda acc: (acc,)
    row_spec = pl.BlockSpec((1, tn), lambda i, j, kk: (0, j))
    n_ex, n_out = len(extras) + len(rows), len(out_dtypes)
    if out_shape is None:
        out_shape = tuple(jax.ShapeDtypeStruct((m, n), d) for d in out_dtypes)
        out_spec = tuple(mn_spec for _ in out_dtypes)
    out_shape = tuple(out_shape) + tuple(jax.ShapeDtypeStruct((1, n), F32) for _ in range(row_sums))
    out_spec = tuple(out_spec) + tuple(row_spec for _ in range(row_sums))

    n_after = len(after)
    lead = 2 if second is None else 4
    assert second is None or (mode == NN and gk == 1), name
    second_specs = [] if second is None else [pl.BlockSpec((tm, second[0].shape[1]), lambda i, j, kk: (i, 0)),
                                                pl.BlockSpec((second[1].shape[0], tn), lambda i, j, kk: (0, j))]

    def body(*refs):
        a_ref, b_ref = refs[0], refs[1]
        ex = refs[lead:lead + n_ex]
        outs = refs[lead + n_ex + n_after:lead + n_ex + n_after + n_out + row_sums]
        first_row_tile = pl.program_id(0) == 0

        def finish(acc):
            res = epilogue(acc, *[e[...] for e in ex])
            for o, r in zip(outs[:n_out], res[:n_out]):
                o[...] = r.astype(o.dtype)
            if row_sums:
                @pl.when(first_row_tile)
                def _():
                    for o in outs[n_out:]:
                        o[...] = jnp.zeros_like(o)

                for o, r in zip(outs[n_out:], res[n_out:]):
                    o[...] += r

        if gk == 1 and second is not None:
            finish(_dot(a_ref[...], b_ref[...], mode) + _dot(refs[2][...], refs[3][...], NN))
        elif gk == 1:
            finish(_dot(a_ref[...], b_ref[...], mode))
        else:
            acc_ref = refs[-1]
            kk = pl.program_id(2)

            @pl.when(kk == 0)
            def _():
                acc_ref[...] = jnp.zeros_like(acc_ref)

            acc_ref[...] += _dot(a_ref[...], b_ref[...], mode)

            @pl.when(kk == gk - 1)
            def _():
                finish(acc_ref[...])

    return pl.pallas_call(
        body, name=name, grid=(gi, gj, gk),
        in_specs=([a_spec, b_spec] + second_specs + [mn_spec] * len(extras) + [row_spec] * len(rows)
                  + [pl.BlockSpec(memory_space=pl.ANY)] * n_after),
        out_specs=out_spec, out_shape=out_shape,
        input_output_aliases={} if into is None else {lead + n_ex + n_after - 1: 0},
        scratch_shapes=[pltpu.VMEM((tm, tn), F32)] if gk > 1 else [],
        compiler_params=_params(("arbitrary" if row_sums else "parallel", "parallel", "arbitrary")),
    )(a, b, *(second or ()), *extras, *rows, *after)


def _rms_rows(xv, g):
    return xv * lax.rsqrt(jnp.mean(xv * xv, axis=1, keepdims=True) + EPS) * g


def _rms_rows_bwd(xv, g, dyv):
    r = lax.rsqrt(jnp.mean(xv * xv, axis=1, keepdims=True) + EPS)
    u = dyv * g
    return (r * u - xv * (r * r * r) * jnp.mean(u * xv, axis=1, keepdims=True),
            jnp.sum(dyv * xv * r, axis=0, keepdims=True))


def _residual_rms(acc, res, g):
    h = acc + res
    return h, _rms_rows(h, g)


def _rms_bwd_residual(dhn, xv, dres, g):
    dx, dg = _rms_rows_bwd(xv, g, dhn)
    dx = dx + dres
    return dx, dx, dg


def _rms_fwd(x, g, *, name, tm=512, after=()):
    t, d = x.shape
    tm = min(tm, t)
    after = tuple(a for a in after if a is not None)

    def body(x_ref, g_ref, *rest):
        rest[-1][...] = _rms_rows(x_ref[...], g_ref[...]).astype(rest[-1].dtype)

    return pl.pallas_call(
        body, name=name, grid=(t // tm,),
        in_specs=[pl.BlockSpec((tm, d), lambda i: (i, 0)), pl.BlockSpec((1, d), lambda i: (0, 0))]
        + [pl.BlockSpec(memory_space=pl.ANY)] * len(after),
        out_specs=pl.BlockSpec((tm, d), lambda i: (i, 0)),
        out_shape=jax.ShapeDtypeStruct((t, d), _MXU_DTYPE),
        compiler_params=_params(("parallel",)),
    )(x, g, *after)


def _rms_gain_grad(x, g, dy, *, name, tm=512):
    t, d = x.shape
    tm = min(tm, t)

    def body(x_ref, g_ref, dy_ref, dg_ref):
        @pl.when(pl.program_id(0) == 0)
        def _():
            dg_ref[...] = jnp.zeros_like(dg_ref)

        dg_ref[...] += _rms_rows_bwd(x_ref[...], g_ref[...], dy_ref[...])[1]

    row = pl.BlockSpec((tm, d), lambda i: (i, 0))
    vec = pl.BlockSpec((1, d), lambda i: (0, 0))
    return pl.pallas_call(
        body, name=name, grid=(t // tm,), in_specs=[row, vec, row], out_specs=vec,
        out_shape=jax.ShapeDtypeStruct((1, d), F32), compiler_params=_params(("arbitrary",)),
    )(x, g, dy)


def _hg_constants():
    c = HG_CHUNK
    t = np.arange(c)
    sums = [t[None, :] <= t[:, None]]
    masks = []
    for m in HG_LEVELS:
        base = (t // (2 * m)) * (2 * m)
        mid = base + m - 1
        second = (t - base) >= m
        upper = (t[None, :] > mid[:, None]) & (t[None, :] <= t[:, None])
        lower = (t[None, :] > t[:, None]) & (t[None, :] <= mid[:, None])
        sums.append(np.where(second[:, None], upper, lower))
        masks.append(second[:, None] & (~second)[None, :] & (base[:, None] == base[None, :]))
    return (np.concatenate(sums, axis=0).astype(np.float32), np.stack(masks).astype(np.float32))


HG_HEAD_LANES = tuple(slice(HG_D * h, HG_D * (h + 1)) for h in range(HG_HEADS))


def _per_head(fn, slab):
    return jnp.concatenate([jnp.broadcast_to(fn(slab[:, hs]), (slab.shape[0], HG_D)) for hs in HG_HEAD_LANES], axis=1)


def _lane_sum(v):
    return jnp.sum(v, axis=1, keepdims=True)


def _lane_mean(v):
    return jnp.mean(v, axis=1, keepdims=True)


def _hg_gates(blk, lbp):
    w = HG_HEADS * HG_D
    q, x, v, gl = blk[:, 0:w], blk[:, w:2 * w], blk[:, 2 * w:3 * w], blk[:, 3 * w:4 * w]
    mx = jnp.max(lbp, axis=0, keepdims=True)
    e = jnp.exp(lbp - mx)
    lb = e[0:1, :] / jnp.sum(e, axis=0, keepdims=True)
    sig = jax.nn.sigmoid(x)
    f = lb + (1.0 - lb) * sig
    return q, v, gl, lb, sig, f, 1.0 - f, jnp.log(f)


def _hg_fwd(proj, lbp, ng, bsz, seq, *, y_width):
    t = proj.shape[0]
    nc = seq // HG_CHUNK
    a_np, m_np = _hg_constants()
    a_all = jnp.asarray(a_np, _MXU_DTYPE)
    masks = jnp.asarray(m_np, F32)
    nl = len(HG_LEVELS)

    ts = min(HG_TILE, seq)
    ns, nct = seq // ts, ts // HG_CHUNK
    hw = HG_HEADS * HG_D

    def body(p_ref, lb_ref, ng_ref, a_ref, m_ref, y_ref, o_ref, st_ref, carry):
        a_mat = a_ref[...]
        ngv = ng_ref[...]

        @pl.when(pl.program_id(0) == 0)
        def _():
            carry[...] = jnp.zeros_like(carry)

        ng4 = _tile_lanes(ngv, HG_HEADS)
        heads = range(HG_HEADS)
        exs = range(bsz)
        hl = HG_HEAD_LANES
        lbp_v = lb_ref[...]

        def chunk(c, _):
            rows = pl.ds(pl.multiple_of(c * HG_CHUNK, HG_CHUNK), HG_CHUNK)
            gates = [_hg_gates(p_ref[e, rows, :], lbp_v) for e in exs]
            q, v, gl = [g[0] for g in gates], [g[1] for g in gates], [g[2] for g in gates]
            k = [g[6] for g in gates]
            sts = [[carry[e, h] for h in heads] for e in exs]
            e_all = [_split_dot(a_mat, gates[e][7], NN, 3) for e in exs]
            b = [e_all[e][0:HG_CHUNK] for e in exs]
            qb = [q[e] * jnp.exp(b[e]) for e in exs]
            o = [[_dot(qb[e][:, hl[h]], sts[e][h], NT) for h in heads] for e in exs]
            p = [[jnp.zeros((HG_CHUNK, HG_CHUNK), F32) for _ in heads] for _ in exs]
            for li in range(nl):
                dec = [jnp.exp(e_all[e][HG_CHUNK * (li + 1):HG_CHUNK * (li + 2)]) for e in exs]
                qm, km, mk = [q[e] * dec[e] for e in exs], [k[e] * dec[e] for e in exs], m_ref[li]
                p = [[p[e][h] + mk * _dot(qm[e][:, hl[h]], km[e][:, hl[h]], NT) for h in heads] for e in exs]
            bl = [b[e][HG_CHUNK - 1:HG_CHUNK, :] for e in exs]
            kd = [k[e] * jnp.exp(bl[e] - b[e]) for e in exs]
            pv = [[_dot(p[e][h], v[e][:, hl[h]]) for h in heads] for e in exs]
            upd = [[_dot(v[e][:, hl[h]], kd[e][:, hl[h]], TN) for h in heads] for e in exs]
            for e in exs:
                o_all = (jnp.concatenate([o[e][h] + pv[e][h] for h in heads], axis=1)
                         + _per_head(_lane_sum, q[e] * k[e]) * v[e])
                r = lax.rsqrt(_per_head(_lane_mean, o_all * o_all) + EPS)
                ebl = jnp.exp(bl[e])
                for h in heads:
                    st_ref[e, h, c] = sts[e][h]
                    carry[e, h] = sts[e][h] * ebl[:, hl[h]] + upd[e][h]
                o_ref[e, rows, :] = o_all
                y_ref[e, rows, :] = (o_all * r * ng4) * (gl[e] * jax.nn.sigmoid(gl[e]))
            return 0

        lax.fori_loop(0, nct, chunk, 0)

    y3, o3, states = pl.pallas_call(
        body, name="hgrn2_fwd", grid=(ns,),
        in_specs=[pl.BlockSpec((bsz, ts, HG_COLS), lambda s: (0, s, 0)),
                  pl.BlockSpec((2, hw), lambda s: (0, 0)),
                  pl.BlockSpec((1, HG_D), lambda s: (0, 0)),
                  pl.BlockSpec(a_all.shape, lambda s: (0, 0)),
                  pl.BlockSpec(masks.shape, lambda s: (0, 0, 0))],
        out_specs=(pl.BlockSpec((bsz, ts, hw), lambda s: (0, s, 0)),
                   pl.BlockSpec((bsz, ts, hw), lambda s: (0, s, 0)),
                   pl.BlockSpec((bsz, HG_HEADS, nct, HG_D, HG_D), lambda s: (0, 0, s, 0, 0))),
        out_shape=(jax.ShapeDtypeStruct((bsz, seq, y_width), F32),
                   jax.ShapeDtypeStruct((bsz, seq, hw), F32),
                   jax.ShapeDtypeStruct((bsz, HG_HEADS, nc, HG_D, HG_D), F32)),
        scratch_shapes=[pltpu.VMEM((bsz, HG_HEADS, HG_D, HG_D), F32)],
        compiler_params=_params(("arbitrary",)),
    )(proj.reshape(bsz, seq, HG_COLS), lbp, ng, a_all, masks)
    return y3.reshape(t, y_width), o3.reshape(t, hw), states


def _hg_bwd(proj, lbp, ng, o_all, states, dy, bsz, seq, after=()):
    after = tuple(a for a in after if a is not None)
    t = proj.shape[0]
    nc = seq // HG_CHUNK
    a_np, m_np = _hg_constants()
    a_all = jnp.asarray(a_np, _MXU_DTYPE)
    masks = jnp.asarray(m_np, F32)
    nl = len(HG_LEVELS)
    cs = HG_CHUNK

    ts = min(HG_TILE, seq)
    ns, nct = seq // ts, ts // cs
    hw = HG_HEADS * HG_D

    def body(p_ref, lb_ref, ng_ref, a_ref, m_ref, o_ref, st_ref, dy_ref, *rest):
        dp_ref, dlb_ref, dng_ref, dst_ref = rest[len(after):]
        a_mat = a_ref[...]
        ngv = ng_ref[...]
        ng4 = _tile_lanes(ngv, HG_HEADS)
        last_row = lax.broadcasted_iota(jnp.int32, (cs, hw), 0) == cs - 1
        first = pl.program_id(0) == 0
        heads = range(HG_HEADS)
        exs = range(bsz)
        hl = HG_HEAD_LANES
        lbp_v = lb_ref[...]

        @pl.when(first)
        def _():
            dst_ref[...] = jnp.zeros_like(dst_ref)

        def side_by_side(parts):
            return jnp.concatenate(parts, axis=1)

        def chunk(i, carry):
            dlb_acc, dng_acc = carry
            c = nct - 1 - i
            rows = pl.ds(pl.multiple_of(c * cs, cs), cs)
            gates = [_hg_gates(p_ref[e, rows, :], lbp_v) for e in exs]
            q, v, gl = [g[0] for g in gates], [g[1] for g in gates], [g[2] for g in gates]
            lb, sig, f, k = gates[0][3], [g[4] for g in gates], [g[5] for g in gates], [g[6] for g in gates]
            o = [o_ref[e, rows, :] for e in exs]
            dyv = [dy_ref[e, rows, :] for e in exs]
            sts = [[st_ref[e, h, c] for h in heads] for e in exs]
            dsts = [[dst_ref[e, h] for h in heads] for e in exs]
            e_all = [_split_dot(a_mat, gates[e][7], NN, 3) for e in exs]
            b = [e_all[e][0:cs] for e in exs]
            eb = [jnp.exp(b[e]) for e in exs]
            bl = [b[e][cs - 1:cs, :] for e in exs]
            ebl = [jnp.exp(bl[e]) for e in exs]
            ekd = [jnp.exp(bl[e] - b[e]) for e in exs]
            qb = [q[e] * eb[e] for e in exs]
            kd = [k[e] * ekd[e] for e in exs]
            do, dgl = [], []
            for e in exs:
                sg = jax.nn.sigmoid(gl[e])
                silu = gl[e] * sg
                r = lax.rsqrt(_per_head(_lane_mean, o[e] * o[e]) + EPS)
                dgl.append(dyv[e] * (o[e] * r * ng4) * (sg * (1.0 + gl[e] * (1.0 - sg))))
                u = dyv[e] * silu * ng4
                do.append(r * u - o[e] * (r * r * r) * _per_head(_lane_mean, u * o[e]))
                dng4 = jnp.sum(dyv[e] * silu * o[e] * r, axis=0, keepdims=True)
                dng_acc = dng_acc + ((dng4[:, hl[0]] + dng4[:, hl[1]]) + (dng4[:, hl[2]] + dng4[:, hl[3]]))
            es, qm, km = [], [], []
            p = [[jnp.zeros((cs, cs), F32) for _ in heads] for _ in exs]
            for li in range(nl):
                dec = [jnp.exp(e_all[e][cs * (li + 1):cs * (li + 2)]) for e in exs]
                es.append(dec)
                qm.append([q[e] * dec[e] for e in exs])
                km.append([k[e] * dec[e] for e in exs])
                mk = m_ref[li]
                p = [[p[e][h] + mk * _dot(qm[li][e][:, hl[h]], km[li][e][:, hl[h]], NT) for h in heads] for e in exs]
            dp = [[_dot(do[e][:, hl[h]], v[e][:, hl[h]], NT) for h in heads] for e in exs]
            dv_p = [[_dot(p[e][h], do[e][:, hl[h]], TN) for h in heads] for e in exs]
            dv_s = [[_dot(kd[e][:, hl[h]], dsts[e][h], NT) for h in heads] for e in exs]
            dqb = [side_by_side([_dot(do[e][:, hl[h]], sts[e][h]) for h in heads]) for e in exs]
            dkd = [side_by_side([_dot(v[e][:, hl[h]], dsts[e][h]) for h in heads]) for e in exs]
            new_dst = [[_dot(do[e][:, hl[h]], qb[e][:, hl[h]], TN) for h in heads] for e in exs]
            dv = [side_by_side([dv_p[e][h] + dv_s[e][h] for h in heads]) + _per_head(_lane_sum, q[e] * k[e]) * do[e]
                  for e in exs]
            dq = [dqb[e] * eb[e] for e in exs]
            dk = [dkd[e] * ekd[e] for e in exs]
            de = []
            for e in exs:
                dbl = (jnp.sum(dkd[e] * kd[e], axis=0, keepdims=True)
                       + side_by_side([jnp.sum(dsts[e][h] * sts[e][h], axis=0, keepdims=True) for h in heads]) * ebl[e])
                de.append([dqb[e] * qb[e] - dkd[e] * kd[e] + jnp.where(last_row, dbl, 0.0)])
            for li in range(nl):
                mk = m_ref[li]
                dpm = [[mk * dp[e][h] for h in heads] for e in exs]
                dqm = [side_by_side([_dot(dpm[e][h], km[li][e][:, hl[h]]) for h in heads]) for e in exs]
                dkm = [side_by_side([_dot(dpm[e][h], qm[li][e][:, hl[h]], TN) for h in heads]) for e in exs]
                for e in exs:
                    dq[e] = dq[e] + dqm[e] * es[li][e]
                    dk[e] = dk[e] + dkm[e] * es[li][e]
                    de[e].append(dqm[e] * qm[li][e] + dkm[e] * km[li][e])
            dg = [_split_dot(a_mat, jnp.concatenate(de[e], axis=0), TN, 2) for e in exs]
            for e in exs:
                dpd = _per_head(_lane_sum, do[e] * v[e])
                df = dg[e] / f[e] - (dk[e] + dpd * q[e])
                dp_ref[e, rows, 0:hw] = _mx(dq[e] + dpd * k[e])
                dp_ref[e, rows, hw:2 * hw] = _mx(df * (1.0 - lb) * sig[e] * (1.0 - sig[e]))
                dp_ref[e, rows, 2 * hw:3 * hw] = _mx(dv[e])
                dp_ref[e, rows, 3 * hw:4 * hw] = _mx(dgl[e])
                for h in heads:
                    dst_ref[e, h] = dsts[e][h] * ebl[e][:, hl[h]] + new_dst[e][h]
                dlb_acc = dlb_acc + jnp.sum(df * (1.0 - sig[e]), axis=0, keepdims=True)
            return dlb_acc, dng_acc

        dlb, dng = lax.fori_loop(0, nct, chunk, (jnp.zeros((1, hw), F32), jnp.zeros((1, HG_D), F32)))

        @pl.when(first)
        def _():
            dlb_ref[...] = jnp.zeros_like(dlb_ref)
            dng_ref[...] = jnp.zeros_like(dng_ref)

        mx = jnp.max(lbp_v, axis=0, keepdims=True)
        e = jnp.exp(lbp_v - mx)
        s0 = e[0:1, :] / jnp.sum(e, axis=0, keepdims=True)
        da0 = dlb * s0 * (1.0 - s0)
        dlb_ref[...] += jnp.concatenate([da0, -da0], axis=0)
        dng_ref[...] += dng

    rows3 = lambda w: pl.BlockSpec((bsz, ts, w), lambda s: (0, ns - 1 - s, 0))
    dproj, dlb, dng = pl.pallas_call(
        body, name="hgrn2_bwd", grid=(ns,),
        in_specs=[rows3(HG_COLS),
                  pl.BlockSpec((2, hw), lambda s: (0, 0)),
                  pl.BlockSpec((1, HG_D), lambda s: (0, 0)),
                  pl.BlockSpec(a_all.shape, lambda s: (0, 0)),
                  pl.BlockSpec(masks.shape, lambda s: (0, 0, 0)),
                  rows3(hw),
                  pl.BlockSpec((bsz, HG_HEADS, nct, HG_D, HG_D), lambda s: (0, 0, ns - 1 - s, 0, 0)),
                  rows3(hw)] + [pl.BlockSpec(memory_space=pl.ANY)] * len(after),
        out_specs=(rows3(HG_COLS),
                   pl.BlockSpec((2, hw), lambda s: (0, 0)),
                   pl.BlockSpec((1, HG_D), lambda s: (0, 0))),
        out_shape=(jax.ShapeDtypeStruct((bsz, seq, HG_COLS), _MXU_DTYPE),
                   jax.ShapeDtypeStruct((2, hw), F32),
                   jax.ShapeDtypeStruct((1, HG_D), F32)),
        scratch_shapes=[pltpu.VMEM((bsz, HG_HEADS, HG_D, HG_D), F32)],
        compiler_params=_params(("arbitrary",)),
    )(proj.reshape(bsz, seq, HG_COLS), lbp, ng, a_all, masks, o_all.reshape(bsz, seq, hw), states,
      dy.reshape(bsz, seq, dy.shape[1]), *after)
    return dproj.reshape(t, HG_COLS), dlb, dng


def _sw_constants():
    half = ROT_DIM // 2
    inv = (np.float32(ROPE_THETA) ** (-(np.arange(half, dtype=np.float32) * np.float32(2.0) / np.float32(ROT_DIM)))
           ).astype(np.float32)
    freq = np.zeros((1, 128), np.float32)
    sign = np.zeros((1, 128), np.float32)
    for h in range(2):
        freq[0, 64 * h:64 * h + half] = inv
        freq[0, 64 * h + half:64 * h + 2 * half] = inv
        sign[0, 64 * h:64 * h + half] = -1.0
        sign[0, 64 * h + half:64 * h + 2 * half] = 1.0
    seg = np.kron(np.eye(8, dtype=np.float32), np.full((64, 64), 1.0 / 64.0, np.float32))
    return freq, sign, seg


def _rope_table(pos, *, tm=512, after=()):
    t = pos.shape[0]
    tm = min(tm, t)
    freq_np, sign_np, _ = _sw_constants()
    after = tuple(a for a in after if a is not None)

    def body(p_ref, f_ref, s_ref, *rest):
        o_ref = rest[-1]
        ang = p_ref[...].astype(F32) * f_ref[...]
        o_ref[:, 0:128] = jnp.cos(ang)
        o_ref[:, 128:256] = jnp.sin(ang) * s_ref[...]

    vec = pl.BlockSpec((1, 128), lambda i: (0, 0))
    return pl.pallas_call(
        body, name="rope_table", grid=(t // tm,),
        in_specs=[pl.BlockSpec((tm, 1), lambda i: (i, 0)), vec, vec] + [pl.BlockSpec(memory_space=pl.ANY)] * len(after),
        out_specs=pl.BlockSpec((tm, 256), lambda i: (i, 0)),
        out_shape=jax.ShapeDtypeStruct((t, 256), F32),
        compiler_params=_params(("parallel",)),
    )(pos, jnp.asarray(freq_np), jnp.asarray(sign_np), *after)


def _tile_lanes(v, times):
    return v if times == 1 else jnp.concatenate([v] * times, axis=1)


def _swap_halves(v):
    w = v.shape[1]
    half = ROT_DIM // 2
    lane = lax.broadcasted_iota(jnp.int32, v.shape, 1) % SW_HD
    return jnp.where(lane < half, pltpu.roll(v, w - half, 1), jnp.where(lane < 2 * half, pltpu.roll(v, half, 1), 0.0))


def _sw_norm_rope(tv, gain, seg, cosv, sinv):
    w = tv.shape[1]
    ms = _split_dot_rhs(tv * tv, seg[0:w, 0:w])
    r = lax.rsqrt(ms + EPS)
    tn = tv * r * gain
    reps = w // 128
    return tn * _tile_lanes(cosv, reps) + _swap_halves(tn) * _tile_lanes(sinv, reps), r


def _split_dot_rhs(v, a):
    hi = _mx(v)
    lo = _mx(v - hi.astype(F32))
    return (lax.dot_general(hi, a, (NN, ((), ())), preferred_element_type=F32)
            + lax.dot_general(lo, a, (NN, ((), ())), preferred_element_type=F32))


def _sw_norm_rope_bwd(dt, tv, r, gain, seg, cosv, sinv):
    w = tv.shape[1]
    reps = w // 128
    dtn = dt * _tile_lanes(cosv, reps) + _swap_halves(dt * _tile_lanes(sinv, reps))
    u = dtn * gain
    dtv = r * u - tv * (r * r * r) * _split_dot_rhs(u * tv, seg[0:w, 0:w])
    return dtv, jnp.sum(dtn * tv * r, axis=0, keepdims=True)


def _sw_scores(qh, kp, kc):
    return _dot(qh, kp, NT), _dot(qh, kc, NT)


SW_SCALE = SW_HD ** -0.5


def _sw_probs(raw, sink, first_block):
    qi = lax.broadcasted_iota(jnp.int32, (SW_BLOCK, SW_BLOCK), 0)
    kj = lax.broadcasted_iota(jnp.int32, (SW_BLOCK, SW_BLOCK), 1)
    ok_prev = jnp.logical_and(kj > qi, jnp.logical_not(first_block))
    ok_cur = kj <= qi
    sp = jnp.where(ok_prev, raw[0], -jnp.inf)
    sc = jnp.where(ok_cur, raw[1], -jnp.inf)
    m = jnp.maximum(jnp.maximum(jnp.max(sp, axis=1, keepdims=True), jnp.max(sc, axis=1, keepdims=True)), sink)
    pp, pc = jnp.exp(sp - m), jnp.exp(sc - m)
    es = jnp.exp(sink - m)
    inv = 1.0 / (jnp.sum(pp, axis=1, keepdims=True) + jnp.sum(pc, axis=1, keepdims=True) + es)
    return pp * inv, pc * inv, es * inv


def _sw_specs(nb):
    def cur(b, n):
        return b * nb + jnp.minimum(n, nb - 1)

    def prev(b, n):
        return b * nb + jnp.maximum(jnp.minimum(n, nb - 1) - 1, 0)

    return cur, prev


def _sw_fwd(proj, rope, qg, kg, sinks, y_in, bsz, seq):
    t = proj.shape[0]
    nb = seq // SW_BLOCK
    seg = jnp.asarray(_sw_constants()[2], _MXU_DTYPE)
    cur, prev = _sw_specs(nb)

    def body(q_ref, kc_ref, kp_ref, vc_ref, vp_ref, rc_ref, rp_ref, qg_ref, kg_ref, sk_ref, seg_ref, yin_ref, y_ref):
        del yin_ref
        n = pl.program_id(1)
        segv = seg_ref[...]
        cos_c, sin_c = rc_ref[:, 0:128], rc_ref[:, 128:256]
        cos_p, sin_p = rp_ref[:, 0:128], rp_ref[:, 128:256]
        qr, _ = _sw_norm_rope(q_ref[...], qg_ref[...] * SW_SCALE, segv, cos_c, sin_c)
        kcr, _ = _sw_norm_rope(kc_ref[...], kg_ref[...], segv, cos_c, sin_c)
        kpr, _ = _sw_norm_rope(kp_ref[...], kg_ref[...], segv, cos_p, sin_p)
        vc, vp = vc_ref[...], vp_ref[...]
        ks = [slice(SW_HD * (h // SW_GROUP), SW_HD * (h // SW_GROUP + 1)) for h in range(SW_HEADS)]
        raw = [_sw_scores(qr[:, SW_HD * h:SW_HD * (h + 1)], kpr[:, ks[h]], kcr[:, ks[h]]) for h in range(SW_HEADS)]
        probs = [_sw_probs(raw[h], sk_ref[0, h], n == 0) for h in range(SW_HEADS)]
        for h in range(SW_HEADS):
            y_ref[:, SW_HD * h:SW_HD * (h + 1)] = _dot(probs[h][0], vp[:, ks[h]]) + _dot(probs[h][1], vc[:, ks[h]])

    rowq = pl.BlockSpec((SW_BLOCK, 512), lambda b, n: (cur(b, n), 0))
    full = lambda a: pl.BlockSpec(a.shape, lambda b, n: (0,) * a.ndim)
    yw = y_in.shape[1]
    return pl.pallas_call(
        body, name="swa_fwd", grid=(bsz, nb),
        in_specs=[rowq,
                  pl.BlockSpec((SW_BLOCK, 128), lambda b, n: (cur(b, n), 4)),
                  pl.BlockSpec((SW_BLOCK, 128), lambda b, n: (prev(b, n), 4)),
                  pl.BlockSpec((SW_BLOCK, 128), lambda b, n: (cur(b, n), 5)),
                  pl.BlockSpec((SW_BLOCK, 128), lambda b, n: (prev(b, n), 5)),
                  pl.BlockSpec((SW_BLOCK, 256), lambda b, n: (cur(b, n), 0)),
                  pl.BlockSpec((SW_BLOCK, 256), lambda b, n: (prev(b, n), 0)),
                  full(qg), full(kg),
                  pl.BlockSpec(memory_space=pltpu.SMEM),
                  full(seg),
                  pl.BlockSpec(memory_space=pl.ANY)],
        out_specs=pl.BlockSpec((SW_BLOCK, 512), lambda b, n: (cur(b, n), 1)),
        out_shape=jax.ShapeDtypeStruct((t, yw), F32),
        input_output_aliases={11: 0},
        compiler_params=_params(("parallel", "parallel")),
    )(proj, proj, proj, proj, proj, rope, rope, qg, kg, sinks, seg, y_in)


def _sw_bwd(proj, rope, qg, kg, sinks, y, dy, bsz, seq):
    t = proj.shape[0]
    nb = seq // SW_BLOCK
    seg = jnp.asarray(_sw_constants()[2], _MXU_DTYPE)
    cur, prev = _sw_specs(nb)

    def body(q_ref, kc_ref, kp_ref, vc_ref, vp_ref, rc_ref, rp_ref, qg_ref, kg_ref, sk_ref, seg_ref,
             y_ref, dy_ref, dp_ref, dqg_ref, dkg_ref, dsk_ref,
             dq_car, dkv_car, dqr_s, dkc_s, dkp_s, dvc_s, dvp_s, gq_acc, gk_acc, sk_acc):
        b, n = pl.program_id(0), pl.program_id(1)
        first = jnp.logical_and(b == 0, n == 0)
        last = jnp.logical_and(b == pl.num_programs(0) - 1, n == nb)

        @pl.when(first)
        def _():
            gq_acc[...] = jnp.zeros_like(gq_acc)
            gk_acc[...] = jnp.zeros_like(gk_acc)
            sk_acc[...] = jnp.zeros_like(sk_acc)

        @pl.when(n < nb)
        def _():
            segv = seg_ref[...]
            cos_c, sin_c = rc_ref[:, 0:128], rc_ref[:, 128:256]
            cos_p, sin_p = rp_ref[:, 0:128], rp_ref[:, 128:256]
            qv, kcv, kpv = q_ref[...], kc_ref[...], kp_ref[...]
            qgain = qg_ref[...] * SW_SCALE
            qr, rq = _sw_norm_rope(qv, qgain, segv, cos_c, sin_c)
            kcr, rkc = _sw_norm_rope(kcv, kg_ref[...], segv, cos_c, sin_c)
            kpr, rkp = _sw_norm_rope(kpv, kg_ref[...], segv, cos_p, sin_p)
            vc, vp = vc_ref[...], vp_ref[...]
            lane = lax.broadcasted_iota(jnp.int32, (1, 128), 1)
            dsk = jnp.zeros((1, 128), F32)
            heads = range(SW_HEADS)
            ks = [slice(SW_HD * (h // SW_GROUP), SW_HD * (h // SW_GROUP + 1)) for h in heads]
            hs = [slice(SW_HD * h, SW_HD * (h + 1)) for h in heads]
            qh = [qr[:, hs[h]] for h in heads]
            doh = [dy_ref[:, hs[h]] for h in heads]
            raw = [_sw_scores(qh[h], kpr[:, ks[h]], kcr[:, ks[h]]) for h in heads]
            dpp = [_dot(doh[h], vp[:, ks[h]], NT) for h in heads]
            dpc = [_dot(doh[h], vc[:, ks[h]], NT) for h in heads]
            probs = [_sw_probs(raw[h], sk_ref[0, h], n == 0) for h in heads]
            dsp, dsc = [], []
            for h in heads:
                pp, pc, ps = probs[h]
                delta = jnp.sum(doh[h] * y_ref[:, hs[h]], axis=1, keepdims=True)
                dsp.append(pp * (dpp[h] - delta))
                dsc.append(pc * (dpc[h] - delta))
                dsk = dsk + jnp.where(lane == h, -jnp.sum(ps * delta), 0.0)
            for h in heads:
                dqr_s[:, hs[h]] = _dot(dsp[h], kpr[:, ks[h]]) + _dot(dsc[h], kcr[:, ks[h]])
            for kv in range(SW_KV_HEADS):
                group = range(SW_GROUP * kv, SW_GROUP * (kv + 1))
                kvs = slice(SW_HD * kv, SW_HD * (kv + 1))
                dvp_s[:, kvs] = sum(_dot(probs[h][0], doh[h], TN) for h in group)
                dvc_s[:, kvs] = sum(_dot(probs[h][1], doh[h], TN) for h in group)
                dkp_s[:, kvs] = sum(_dot(dsp[h], qh[h], TN) for h in group)
                dkc_s[:, kvs] = sum(_dot(dsc[h], qh[h], TN) for h in group)
            dq, gq = _sw_norm_rope_bwd(dqr_s[...], qv, rq, qgain, segv, cos_c, sin_c)
            dkc, gkc = _sw_norm_rope_bwd(dkc_s[...], kcv, rkc, kg_ref[...], segv, cos_c, sin_c)
            dkp, gkp = _sw_norm_rope_bwd(dkp_s[...], kpv, rkp, kg_ref[...], segv, cos_p, sin_p)
            gq_acc[...] += gq
            gk_acc[...] += gkc + gkp
            sk_acc[...] += dsk

            @pl.when(n > 0)
            def _():
                dp_ref[:, 0:512] = _mx(dq_car[...])
                dp_ref[:, 512:640] = _mx(dkv_car[:, 0:128] + dkp)
                dp_ref[:, 640:768] = _mx(dkv_car[:, 128:256] + dvp_s[...])

            dq_car[...] = dq
            dkv_car[:, 0:128] = dkc
            dkv_car[:, 128:256] = dvc_s[...]

        @pl.when(n == nb)
        def _():
            dp_ref[:, 0:512] = _mx(dq_car[...])
            dp_ref[:, 512:768] = _mx(dkv_car[...])

        @pl.when(last)
        def _():
            gq = gq_acc[...] * SW_SCALE
            acc = gq[:, 0:SW_HD]
            for h in range(1, SW_HEADS):
                acc = acc + gq[:, SW_HD * h:SW_HD * (h + 1)]
            dqg_ref[...] = acc
            gk = gk_acc[...]
            dkg_ref[...] = gk[:, 0:SW_HD] + gk[:, SW_HD:2 * SW_HD]
            dsk_ref[...] = sk_acc[...]

    rowq = pl.BlockSpec((SW_BLOCK, 512), lambda b, n: (cur(b, n), 0))
    full = lambda a: pl.BlockSpec(a.shape, lambda b, n: (0,) * a.ndim)

    def out_row(b, n):
        return b * nb + jnp.maximum(n - 1, 0)

    return pl.pallas_call(
        body, name="swa_bwd", grid=(bsz, nb + 1),
        in_specs=[rowq,
                  pl.BlockSpec((SW_BLOCK, 128), lambda b, n: (cur(b, n), 4)),
                  pl.BlockSpec((SW_BLOCK, 128), lambda b, n: (prev(b, n), 4)),
                  pl.BlockSpec((SW_BLOCK, 128), lambda b, n: (cur(b, n), 5)),
                  pl.BlockSpec((SW_BLOCK, 128), lambda b, n: (prev(b, n), 5)),
                  pl.BlockSpec((SW_BLOCK, 256), lambda b, n: (cur(b, n), 0)),
                  pl.BlockSpec((SW_BLOCK, 256), lambda b, n: (prev(b, n), 0)),
                  full(qg), full(kg),
                  pl.BlockSpec(memory_space=pltpu.SMEM),
                  full(seg),
                  pl.BlockSpec((SW_BLOCK, 512), lambda b, n: (cur(b, n), 1)),
                  pl.BlockSpec((SW_BLOCK, 512), lambda b, n: (cur(b, n), 1))],
        out_specs=(pl.BlockSpec((SW_BLOCK, SW_COLS), lambda b, n: (out_row(b, n), 0)),
                   pl.BlockSpec((1, SW_HD), lambda b, n: (0, 0)),
                   pl.BlockSpec((1, SW_HD), lambda b, n: (0, 0)),
                   pl.BlockSpec((1, 128), lambda b, n: (0, 0))),
        out_shape=(jax.ShapeDtypeStruct((t, SW_COLS), _MXU_DTYPE),
                   jax.ShapeDtypeStruct((1, SW_HD), F32),
                   jax.ShapeDtypeStruct((1, SW_HD), F32),
                   jax.ShapeDtypeStruct((1, 128), F32)),
        scratch_shapes=[pltpu.VMEM((SW_BLOCK, 512), F32), pltpu.VMEM((SW_BLOCK, 256), F32),
                        pltpu.VMEM((SW_BLOCK, 512), F32),
                        pltpu.VMEM((SW_BLOCK, 128), F32), pltpu.VMEM((SW_BLOCK, 128), F32),
                        pltpu.VMEM((SW_BLOCK, 128), F32), pltpu.VMEM((SW_BLOCK, 128), F32),
                        pltpu.VMEM((1, 512), F32), pltpu.VMEM((1, 128), F32), pltpu.VMEM((1, 128), F32)],
        compiler_params=_params(("arbitrary", "arbitrary")),
    )(proj, proj, proj, proj, proj, rope, rope, qg, kg, sinks, seg, y, dy)


def _head_rms(tv, gain):
    r = lax.rsqrt(jnp.mean(tv * tv, axis=1, keepdims=True) + EPS)
    return tv * r * gain, r


def _head_rms_bwd(dtn, tv, r, gain):
    u = dtn * gain
    return r * u - tv * (r * r * r) * jnp.mean(u * tv, axis=1, keepdims=True), jnp.sum(dtn * tv * r, axis=0, keepdims=True)


def _xa_softmax(raw):
    s = raw * (XA_HD ** -0.5)
    e = jnp.exp(s - jnp.max(s, axis=1, keepdims=True))
    return e * (1.0 / jnp.sum(e, axis=1, keepdims=True))


def _xa_fwd(qx, kvx, qg, kg, bsz, seq, mlen, *, tq=512):
    t = qx.shape[0]
    tq = min(tq, seq)
    nq = seq // tq
    w = XA_HEADS * XA_HD

    def body(q_ref, kv_ref, qg_ref, kg_ref, o_ref):
        heads = range(XA_HEADS)
        hs = [slice(XA_HD * h, XA_HD * (h + 1)) for h in heads]
        qn = [_head_rms(q_ref[:, hs[h]], qg_ref[...])[0] for h in heads]
        kn = [_head_rms(kv_ref[:, hs[h]], kg_ref[...])[0] for h in heads]
        raw = [_dot(qn[h], kn[h], NT) for h in heads]
        p = [_xa_softmax(raw[h]) for h in heads]
        for h in heads:
            o_ref[:, hs[h]] = _dot(p[h], kv_ref[:, w + XA_HD * h:w + XA_HD * (h + 1)]).astype(o_ref.dtype)

    vec = pl.BlockSpec((1, XA_HD), lambda b, i: (0, 0))
    return pl.pallas_call(
        body, name="xattn_fwd", grid=(bsz, nq),
        in_specs=[pl.BlockSpec((tq, w), lambda b, i: (b * nq + i, 0)),
                  pl.BlockSpec((mlen, 2 * w), lambda b, i: (b, 0)), vec, vec],
        out_specs=pl.BlockSpec((tq, w), lambda b, i: (b * nq + i, 0)),
        out_shape=jax.ShapeDtypeStruct((t, w), _MXU_DTYPE),
        compiler_params=_params(("parallel", "parallel")),
    )(qx, kvx, qg, kg)


def _xa_bwd(qx, kvx, qg, kg, do, bsz, seq, mlen, *, tq=1024):
    t = qx.shape[0]
    tq = min(tq, seq)
    nq = seq // tq
    w = XA_HEADS * XA_HD
    scale = XA_HD ** -0.5

    def body(q_ref, kv_ref, qg_ref, kg_ref, do_ref, dq_ref, dkv_ref, dqg_ref, dkg_ref):
        b, i = pl.program_id(0), pl.program_id(1)

        @pl.when(jnp.logical_and(b == 0, i == 0))
        def _():
            dqg_ref[...] = jnp.zeros_like(dqg_ref)
            dkg_ref[...] = jnp.zeros_like(dkg_ref)

        @pl.when(i == 0)
        def _():
            dkv_ref[...] = jnp.zeros_like(dkv_ref)

        heads = range(XA_HEADS)
        hs = [slice(XA_HD * h, XA_HD * (h + 1)) for h in heads]
        vs = [slice(w + XA_HD * h, w + XA_HD * (h + 1)) for h in heads]
        qv = [q_ref[:, hs[h]] for h in heads]
        kv = [kv_ref[:, hs[h]] for h in heads]
        doh = [do_ref[:, hs[h]] for h in heads]
        qn = [_head_rms(qv[h], qg_ref[...]) for h in heads]
        kn = [_head_rms(kv[h], kg_ref[...]) for h in heads]
        raw = [_dot(qn[h][0], kn[h][0], NT) for h in heads]
        dp = [_dot(doh[h], kv_ref[:, vs[h]], NT) for h in heads]
        p = [_xa_softmax(raw[h]) for h in heads]
        ds = [p[h] * (dp[h] - jnp.sum(p[h] * dp[h], axis=1, keepdims=True)) * scale for h in heads]
        dqn = [_dot(ds[h], kn[h][0]) for h in heads]
        dkn = [_dot(ds[h], qn[h][0], TN) for h in heads]
        dvv = [_dot(p[h], doh[h], TN) for h in heads]
        gq_sum = jnp.zeros((1, XA_HD), F32)
        gk_sum = jnp.zeros((1, XA_HD), F32)
        for h in heads:
            dqv, gq = _head_rms_bwd(dqn[h], qv[h], qn[h][1], qg_ref[...])
            dkv, gk = _head_rms_bwd(dkn[h], kv[h], kn[h][1], kg_ref[...])
            dq_ref[:, hs[h]] = dqv.astype(dq_ref.dtype)
            dkv_ref[:, hs[h]] += dkv
            dkv_ref[:, vs[h]] += dvv[h]
            gq_sum = gq_sum + gq
            gk_sum = gk_sum + gk
        dqg_ref[...] += gq_sum
        dkg_ref[...] += gk_sum

    vec = pl.BlockSpec((1, XA_HD), lambda b, i: (0, 0))
    row = pl.BlockSpec((tq, w), lambda b, i: (b * nq + i, 0))
    mem = pl.BlockSpec((mlen, 2 * w), lambda b, i: (b, 0))
    return pl.pallas_call(
        body, name="xattn_bwd", grid=(bsz, nq),
        in_specs=[row, mem, vec, vec, row],
        out_specs=(row, mem, vec, vec),
        out_shape=(jax.ShapeDtypeStruct((t, w), _MXU_DTYPE), jax.ShapeDtypeStruct((bsz * mlen, 2 * w), F32),
                   jax.ShapeDtypeStruct((1, XA_HD), F32), jax.ShapeDtypeStruct((1, XA_HD), F32)),
        compiler_params=_params(("arbitrary", "arbitrary")),
    )(qx, kvx, qg, kg, do)


def _loss_finish(sq_row, d_model):
    def body(s_ref, o_ref):
        o_ref[...] = jnp.zeros_like(o_ref) + 0.5 * jnp.sum(s_ref[...]) / float(d_model)

    return pl.pallas_call(body, name="loss_finish", out_shape=jax.ShapeDtypeStruct((1, 128), F32))(sq_row)


def _adamw_math(w, g, m, v):
    m = ADAM_B1 * m + (1.0 - ADAM_B1) * g
    v = ADAM_B2 * v + (1.0 - ADAM_B2) * (g * g)
    m_hat = m / (1.0 - ADAM_B1 ** ADAM_STEP)
    v_hat = v / (1.0 - ADAM_B2 ** ADAM_STEP)
    return -ADAM_LR * (m_hat / (jnp.sqrt(v_hat) + ADAM_EPS) + ADAM_WD * w), m, v


def _adamw_big(ws, gs, ms, vs, *, steps=8):
    n = len(ws)

    def body(*refs):
        for a in range(n):
            gv = refs[n + a][...]
            d, mn, vn = _adamw_math(refs[a][...], gv, refs[2 * n + a][...], refs[3 * n + a][...])
            refs[4 * n + 4 * a][...] = gv
            refs[4 * n + 4 * a + 1][...] = d
            refs[4 * n + 4 * a + 2][...] = mn
            refs[4 * n + 4 * a + 3][...] = vn

    def spec(w):
        assert w.shape[0] % (8 * steps) == 0, w.shape
        return pl.BlockSpec((w.shape[0] // steps, w.shape[1]), lambda i: (i, 0))

    specs = [spec(w) for w in ws]
    out = pl.pallas_call(
        body, name="adamw_big", grid=(steps,), in_specs=specs * 4,
        out_specs=tuple(s for s in specs for _ in range(4)),
        out_shape=tuple(jax.ShapeDtypeStruct(w.shape, F32) for w in ws for _ in range(4)),
        compiler_params=_params(("parallel",)),
    )(*ws, *gs, *ms, *vs)
    return [out[4 * a:4 * a + 4] for a in range(n)]


def _adamw_small(ws, gs, ms, vs):
    n = len(ws)

    def body(*refs):
        for i in range(n):
            d, mn, vn = _adamw_math(refs[i][...], refs[n + i][...], refs[2 * n + i][...], refs[3 * n + i][...])
            refs[4 * n + i][...] = d
            refs[5 * n + i][...] = mn
            refs[6 * n + i][...] = vn

    shapes = tuple(jax.ShapeDtypeStruct(w.shape, F32) for w in ws)
    return pl.pallas_call(body, name="adamw_small", out_shape=shapes * 3)(*ws, *gs, *ms, *vs)


def _add_halves(gs, recvs, c_idx, *, name):
    n = len(gs)

    def body(c_ref, *refs):
        del c_ref
        for a in range(n):
            refs[2 * n + a][...] = refs[a][...] + refs[n + a][...]

    def half(g):
        return pl.BlockSpec((None, g.shape[1] // 2, g.shape[2]), lambda k, cr: (k, cr[0], 0))

    def whole(g):
        return pl.BlockSpec((None, g.shape[1] // 2, g.shape[2]), lambda k, cr: (k, 0, 0))

    return pl.pallas_call(
        body, name=name,
        grid_spec=pltpu.PrefetchScalarGridSpec(
            num_scalar_prefetch=1, grid=(4,),
            in_specs=[half(g) for g in gs] + [whole(g) for g in gs],
            out_specs=tuple(whole(g) for g in gs)),
        out_shape=tuple(jax.ShapeDtypeStruct((4, g.shape[1] // 2, g.shape[2]), F32) for g in gs),
        compiler_params=_params(("parallel",)),
    )(c_idx, *gs, *recvs)


def _add_chips(ps, recvs, place_idx, *, name, steps=4, after=()):
    n = len(ps)

    def body(pi_ref, *refs):
        del pi_ref
        outs = refs[2 * n + len(after):]
        for a in range(n):
            r_ref = refs[n + a]
            outs[a][...] = ((refs[a][...] + r_ref[0]) + r_ref[1]) + r_ref[2]

    def tile(p):
        assert p.shape[1] % (8 * steps) == 0, (name, p.shape)
        return p.shape[1] // steps

    return pl.pallas_call(
        body, name=name,
        grid_spec=pltpu.PrefetchScalarGridSpec(
            num_scalar_prefetch=1, grid=(steps,),
            in_specs=[pl.BlockSpec((None, tile(p), p.shape[2]), lambda i, pi: (pi[0], i, 0)) for p in ps]
            + [pl.BlockSpec((3, tile(p), p.shape[2]), lambda i, pi: (0, i, 0)) for p in ps]
            + [pl.BlockSpec(memory_space=pl.ANY)] * len(after),
            out_specs=tuple(pl.BlockSpec((tile(p), p.shape[2]), lambda i, pi: (pi[1] * steps + i, 0)) for p in ps)),
        out_shape=tuple(jax.ShapeDtypeStruct((2 * p.shape[1], p.shape[2]), F32) for p in ps),
        compiler_params=_params(("parallel",)),
    )(place_idx, *ps, *recvs, *after)


def _place_shards(shards, place_idx, *, name, after=()):
    n = len(shards)

    def body(pi_ref, *refs):
        del pi_ref
        for i in range(n):
            refs[n + len(after) + i][...] = refs[i][...]

    return pl.pallas_call(
        body, name=name,
        grid_spec=pltpu.PrefetchScalarGridSpec(
            num_scalar_prefetch=1, grid=(1,),
            in_specs=[pl.BlockSpec(s.shape, lambda i, pi: (0, 0)) for s in shards]
            + [pl.BlockSpec(memory_space=pl.ANY)] * len(after),
            out_specs=tuple(pl.BlockSpec((None,) + s.shape, lambda i, pi: (pi[0], 0, 0)) for s in shards)),
        out_shape=tuple(jax.ShapeDtypeStruct((4,) + s.shape, s.dtype) for s in shards),
        compiler_params=_params(("arbitrary",)),
    )(place_idx, *shards, *after)


def _place_shard(shard, place_idx, *, name, tr=512, after=()):
    r, c = shard.shape
    tr = min(tr, r)
    if r % tr:
        tr = r // 2
    assert r % tr == 0 and tr % 16 == 0, (name, r, tr)

    def body(pi_ref, s_ref, *rest):
        del pi_ref
        rest[-1][...] = s_ref[...]

    return pl.pallas_call(
        body, name=name,
        grid_spec=pltpu.PrefetchScalarGridSpec(
            num_scalar_prefetch=1, grid=(r // tr,),
            in_specs=[pl.BlockSpec((tr, c), lambda i, pi: (i, 0))] + [pl.BlockSpec(memory_space=pl.ANY)] * len(after),
            out_specs=pl.BlockSpec((None, tr, c), lambda i, pi: (pi[0], i, 0))),
        out_shape=jax.ShapeDtypeStruct((4, r, c), shard.dtype),
        compiler_params=_params(("parallel",)),
    )(place_idx, shard, *after)


def _place():
    x, y, c = lax.axis_index("x"), lax.axis_index("y"), lax.axis_index("c")
    chips = [(1 - x, y), (x, 1 - y), (1 - x, 1 - y)]
    return x, y, c, chips


ANY = pl.BlockSpec(memory_space=pl.ANY)


def _exchange_halves(grads, name):
    n = len(grads)

    def body(*refs):
        ins, outs = refs[:n], refs[n:2 * n]
        send_sems, recv_sems = refs[2 * n:]
        x, y, c, _ = _place()

        def copy(a):
            h = ins[a].shape[1] // 2
            return pltpu.make_async_remote_copy(
                src_ref=ins[a].at[:, pl.ds((1 - c) * h, h), :], dst_ref=outs[a],
                send_sem=send_sems.at[a], recv_sem=recv_sems.at[a], device_id=(x, y, 1 - c), device_id_type=MESH)

        for a in range(n):
            copy(a).start()
        for a in range(n):
            copy(a).wait_recv()
        for a in range(n):
            copy(a).wait_send()

    return pl.pallas_call(
        body, name=name,
        in_specs=[ANY] * n, out_specs=tuple([ANY] * n),
        out_shape=tuple(jax.ShapeDtypeStruct((4, g.shape[1] // 2, g.shape[2]), g.dtype) for g in grads),
        scratch_shapes=[pltpu.SemaphoreType.DMA((n,)), pltpu.SemaphoreType.DMA((n,))],
    )(*grads)


HBM = pl.BlockSpec(memory_space=pltpu.HBM)
SEM = pl.BlockSpec(memory_space=pltpu.SEMAPHORE)
EFFECT = pltpu.SideEffectType.DATAFLOW_SIDE_EFFECTING


def _in_hbm(a):
    return pltpu.with_memory_space_constraint(a, pltpu.HBM)


def _split_copy_calls(name, srcs, lands, n_copies, make_copies):
    ns, nl = len(srcs), len(lands)
    nb = ns + nl

    def start(after=()):
        n_after = len(after)

        def body(*refs):
            outs = refs[nb + n_after:]
            copies = make_copies(refs[:ns], refs[ns:nb], outs[0], outs[1])
            for cp in copies:
                cp.start()
            token = refs[-1]
            token[...] = jnp.zeros_like(token)

        bufs = [_in_hbm(a) for a in list(srcs) + list(lands)]
        out = pl.pallas_call(
            body, name=name + "_start",
            out_shape=(pltpu.SemaphoreType.DMA((n_copies,)), pltpu.SemaphoreType.DMA((n_copies,)),
                       *[pltpu.HBM(a.shape, a.dtype) for a in bufs], jax.ShapeDtypeStruct((8, 128), F32)),
            in_specs=[HBM] * nb + [pl.BlockSpec(memory_space=pl.ANY)] * n_after,
            out_specs=(SEM, SEM, *[HBM] * nb, pl.BlockSpec(memory_space=pltpu.VMEM)),
            input_output_aliases={i: 2 + i for i in range(nb)},
            compiler_params=pltpu.CompilerParams(has_side_effects=EFFECT),
        )(*bufs, *after)
        return dict(send=out[0], recv=out[1], bufs=list(out[2:2 + nb]), token=out[-1])

    def wait(state, after):
        def body(*refs):
            copies = make_copies(refs[:ns], refs[ns:nb], refs[nb], refs[nb + 1])
            for cp in copies:
                cp.wait_send()
            for cp in copies:
                cp.wait_recv()

        bufs = state["bufs"]
        out = pl.pallas_call(
            body, name=name + "_wait",
            out_shape=tuple(pltpu.HBM(a.shape, a.dtype) for a in bufs),
            in_specs=[HBM] * nb + [SEM, SEM] + [pl.BlockSpec(memory_space=pl.ANY)] * len(after),
            out_specs=tuple([HBM] * nb),
            input_output_aliases={i: i for i in range(nb)},
            compiler_params=pltpu.CompilerParams(has_side_effects=EFFECT),
        )(*bufs, state["send"], state["recv"], *after)
        return list(out[:ns]), list(out[ns:])

    return start, wait


def _scatter_chips_split(name, parts):
    n = len(parts)
    lands = [lax.empty((3,) + p.shape[1:], p.dtype) for p in parts]

    def make_copies(srcs, lnds, send_sems, recv_sems):
        _, _, c, chips = _place()
        return [pltpu.make_async_remote_copy(
            src_ref=srcs[a].at[2 * px + py], dst_ref=lnds[a].at[j], send_sem=send_sems.at[a * 3 + j],
            recv_sem=recv_sems.at[a * 3 + j], device_id=(px, py, c), device_id_type=MESH)
            for a in range(n) for j, (px, py) in enumerate(chips)]

    return _split_copy_calls(name, parts, lands, 3 * n, make_copies)


def _exchange_halves_split(name, grads):
    n = len(grads)
    lands = [lax.empty((4, g.shape[1] // 2, g.shape[2]), g.dtype) for g in grads]

    def make_copies(srcs, lnds, send_sems, recv_sems):
        x, y, c, _ = _place()
        out = []
        for a in range(n):
            h = srcs[a].shape[1] // 2
            out.append(pltpu.make_async_remote_copy(
                src_ref=srcs[a].at[:, pl.ds((1 - c) * h, h), :], dst_ref=lnds[a], send_sem=send_sems.at[a],
                recv_sem=recv_sems.at[a], device_id=(x, y, 1 - c), device_id_type=MESH))
        return out

    return _split_copy_calls(name, grads, lands, n, make_copies)


def _gather_chips_split(name, shards, lands):
    n = len(shards)

    def make_copies(srcs, lnds, send_sems, recv_sems):
        x, y, c, chips = _place()
        out = []
        for a in range(n):
            h = srcs[a].shape[0] // 2
            for j, (px, py) in enumerate(chips):
                out.append(pltpu.make_async_remote_copy(
                    src_ref=srcs[a].at[pl.ds(c * h, h), :], dst_ref=lnds[a].at[2 * x + y, pl.ds(c * h, h), :],
                    send_sem=send_sems.at[a * 3 + j], recv_sem=recv_sems.at[a * 3 + j],
                    device_id=(px, py, c), device_id_type=MESH))
        return out

    return _split_copy_calls(name, shards, lands, 3 * n, make_copies)


def _gather_finish(gathered, name):
    n = len(gathered)

    def body(*refs):
        outs = refs[n:2 * n]
        send_sems, recv_sems = refs[2 * n:]
        x, y, c, chips = _place()

        def copy(a, j, chip_idx, which):
            h = outs[a].shape[1] // 2
            rows = outs[a].at[chip_idx, pl.ds(which * h, h), :]
            return pltpu.make_async_remote_copy(
                src_ref=rows, dst_ref=rows, send_sem=send_sems.at[a * 3 + j], recv_sem=recv_sems.at[a * 3 + j],
                device_id=(x, y, 1 - c), device_id_type=MESH)

        for a in range(n):
            for j, (px, py) in enumerate(chips):
                copy(a, j, 2 * px + py, c).start()
        for a in range(n):
            for j, (px, py) in enumerate(chips):
                copy(a, j, 2 * px + py, 1 - c).wait_recv()
        for a in range(n):
            for j, (px, py) in enumerate(chips):
                copy(a, j, 2 * px + py, c).wait_send()

    return pl.pallas_call(
        body, name=name,
        in_specs=[ANY] * n, out_specs=tuple([ANY] * n),
        out_shape=tuple(jax.ShapeDtypeStruct(g.shape, g.dtype) for g in gathered),
        input_output_aliases={i: i for i in range(n)},
        scratch_shapes=[pltpu.SemaphoreType.DMA((3 * n,)), pltpu.SemaphoreType.DMA((3 * n,))],
    )(*gathered)


def _gather_forward_split(name, gathered):
    n = len(gathered)

    def make_copies(srcs, lnds, send_sems, recv_sems):
        x, y, c, chips = _place()
        out = []
        for a in range(n):
            h = lnds[a].shape[1] // 2
            for j, (px, py) in enumerate(chips):
                rows = lnds[a].at[2 * px + py, pl.ds(c * h, h), :]
                out.append(pltpu.make_async_remote_copy(
                    src_ref=rows, dst_ref=rows, send_sem=send_sems.at[a * 3 + j], recv_sem=recv_sems.at[a * 3 + j],
                    device_id=(x, y, 1 - c), device_id_type=MESH))
        return out

    return _split_copy_calls(name, [], gathered, 3 * n, make_copies)


def _join_halves(fulls):
    n = len(fulls)

    def body(*refs):
        outs = refs[n:2 * n]
        send_sems, recv_sems = refs[2 * n:]
        x, y, c, _ = _place()

        def copy(a, which):
            h = outs[a].shape[0] // 2
            rows = outs[a].at[pl.ds(which * h, h), :]
            return pltpu.make_async_remote_copy(
                src_ref=rows, dst_ref=rows, send_sem=send_sems.at[a], recv_sem=recv_sems.at[a],
                device_id=(x, y, 1 - c), device_id_type=MESH)

        for a in range(n):
            copy(a, c).start()
        for a in range(n):
            copy(a, 1 - c).wait_recv()
        for a in range(n):
            copy(a, c).wait_send()

    return pl.pallas_call(
        body, name="rs_join_halves",
        in_specs=[ANY] * n, out_specs=tuple([ANY] * n),
        out_shape=tuple(jax.ShapeDtypeStruct(p.shape, p.dtype) for p in fulls),
        input_output_aliases={i: i for i in range(n)},
        scratch_shapes=[pltpu.SemaphoreType.DMA((n,)), pltpu.SemaphoreType.DMA((n,))],
    )(*fulls)


def _all_gather_small_split(sm):
    r, w = sm.shape

    def make_copies(srcs, lnds, send_sems, recv_sems):
        x, y, c, _ = _place()
        me = 4 * x + 2 * y + c
        rel = [(dx, dy, dc) for dx in (0, 1) for dy in (0, 1) for dc in (0, 1)][1:]
        return [pltpu.make_async_remote_copy(
            src_ref=srcs[0], dst_ref=lnds[0].at[me], send_sem=send_sems.at[k], recv_sem=recv_sems.at[k],
            device_id=(1 - x if dx else x, 1 - y if dy else y, 1 - c if dc else c), device_id_type=MESH)
            for k, (dx, dy, dc) in enumerate(rel)]

    return _split_copy_calls("all_gather_small", [sm], [lax.empty((8, r, w), sm.dtype)], 7, make_copies)


def _sum_devices(sm, gathered, me_idx):
    def body(me_ref, sm_ref, g_ref, o_ref):
        own = sm_ref[...]
        acc = jnp.where(me_ref[0] == 0, own, g_ref[0])
        for d in range(1, 8):
            acc = acc + jnp.where(me_ref[0] == d, own, g_ref[d])
        o_ref[...] = acc

    vm = pl.BlockSpec(memory_space=pltpu.VMEM)
    return pl.pallas_call(
        body, name="sum_devices", in_specs=[pl.BlockSpec(memory_space=pltpu.SMEM), vm, vm], out_specs=vm,
        out_shape=jax.ShapeDtypeStruct(sm.shape, F32),
    )(me_idx, sm, gathered)


def _local_step(x3, mem3, pos2, target3, small, comm):
    bsz, seq, d = x3.shape
    mlen = mem3.shape[1]
    t = bsz * seq
    tok = comm.begin()
    x = x3.reshape(t, d)
    mem = mem3.reshape(bsz * mlen, d)
    target = target3.reshape(t, d)
    rope = _rope_table(pos2.reshape(t, 1), after=tok)
    qg_t = jnp.tile(small["sw_q_norm_g"], (1, SW_HEADS))
    kg_t = jnp.tile(small["sw_k_norm_g"], (1, SW_KV_HEADS))

    hn1 = _rms_fwd(x, small["norm1_g"], name="rms1_fwd", after=tok)
    w = comm.first((hn1, rope))
    w_in_t = w["w_in_t"]
    w_sw_t = w_in_t[HG_COLS:]
    proj_hg = _mm(hn1, w_in_t, NT, t, HG_COLS, d, name="proj_hg", tk=d, after=(w.get("token"),))[0]
    proj_sw = _mm(hn1, w_sw_t, NT, t, SW_COLS, d, name="proj_sw", tk=d)[0]
    y_mix, o_hg, states = _hg_fwd(proj_hg, small["hg_lower_bounds"], small["hg_norm_g"], bsz, seq, y_width=1024)
    y_mix = _sw_fwd(proj_sw, rope, qg_t, kg_t, small["sw_sinks"], y_mix, bsz, seq)
    w = comm.rest(y_mix)
    h1, hn2 = _mm(y_mix, w["w_out"], NN, t, d, 1024, name="out_proj", tk=1024, extras=(x,), rows=(small["norm2_g"],),
                  epilogue=_residual_rms, out_dtypes=(F32, _MXU_DTYPE), after=(w.get("token"),))
    mn = _rms_fwd(mem, small["mem_norm_g"], name="rms_mem_fwd")
    qx = _mm(hn2, w["wq"], NN, t, 512, d, name="xa_q", tk=d)[0]
    kvx = _mm(mn, w["wkv"], NN, bsz * mlen, 1024, d, name="xa_kv", tk=d)[0]
    ox = _xa_fwd(qx, kvx, small["xa_q_norm_g"], small["xa_k_norm_g"], bsz, seq, mlen)
    h2, hn3 = _mm(ox, w["wo"], NN, t, d, 512, name="xa_o", tk=512, extras=(h1,), rows=(small["norm3_g"],),
                  epilogue=_residual_rms, out_dtypes=(F32, _MXU_DTYPE))
    w = {**w, **comm.mlp(hn3)}
    ff = w["down"].shape[0]
    ffs = ff // 4

    def relu_sq(acc):
        a = jnp.maximum(acc, 0.0)
        return a, a * a

    act, act2 = _mm(hn3, w["up"], NN, t, ff, d, name="mlp_up", tm=2048, tn=ffs, tk=d,
                    b_spec=pl.BlockSpec((None, d, ffs), lambda i, j, kk: (j, 0, 0)),
                    epilogue=relu_sq, out_dtypes=(_MXU_DTYPE, _MXU_DTYPE))
    inv_d = 1.0 / d

    def loss_cotangent(acc, res, tgt):
        diff = acc + res - tgt
        v = diff * inv_d
        return v, v, jnp.sum(diff * diff, axis=0, keepdims=True)

    dy, dy_mx, sq_row = _mm(act2, w["down"], NN, t, d, ff, name="mlp_down", tk=2048, extras=(h2, target),
                            epilogue=loss_cotangent, out_dtypes=(F32, _MXU_DTYPE), row_sums=1)
    loss_row = _loss_finish(sq_row, d)

    dz = _mm(dy_mx, w["down"], NT, t, ff, d, name="d_act", tm=2048, tk=d, extras=(act,),
             epilogue=lambda acc, a: (acc * (2.0 * a.astype(F32)),), out_dtypes=(_MXU_DTYPE,))[0]
    g_down = _mm(act2, dy_mx, TN, ff, d, t, name="g_down", tk=t)[0]
    g_up = _mm(hn3, dz, TN, d, ff, t, name="g_up", tn=ffs, tk=t,
               out_shape=(jax.ShapeDtypeStruct((4, d, ffs), F32),),
               out_spec=(pl.BlockSpec((None, min(1024, d), ffs), lambda i, j, kk: (j, i, 0)),))[0]
    tok = comm.grads("mlp", dict(up=g_up, down=g_down))
    dh2, dh2_mx, g_norm3 = _mm(dz, w["up"], NT, t, d, ff, name="d_hn3", tk=ffs, after=tok,
                               b_spec=pl.BlockSpec((None, min(1024, d), ffs), lambda i, j, kk: (kk, j, 0)),
                               extras=(h2, dy), rows=(small["norm3_g"],), epilogue=_rms_bwd_residual,
                               out_dtypes=(F32, _MXU_DTYPE), row_sums=1)
    d_ox = _mm(dh2_mx, w["wo"], NT, t, 512, d, name="d_ox", tk=d)[0]
    g_wo = _mm(ox, dh2_mx, TN, 512, d, t, name="g_wo", tk=t)[0]
    d_qx, d_kvx, g_xq, g_xk = _xa_bwd(qx, kvx, small["xa_q_norm_g"], small["xa_k_norm_g"], d_ox, bsz, seq, mlen)
    g_wq = _mm(hn2, d_qx, TN, d, 512, t, name="g_wq")[0]
    g_wkv = _mm(mn, d_kvx, TN, d, 1024, bsz * mlen, name="g_wkv")[0]
    dh1, dh1_mx, g_norm2 = _mm(d_qx, w["wq"], NT, t, d, 512, name="d_hn2", tk=512, extras=(h1, dh2),
                               rows=(small["norm2_g"],), epilogue=_rms_bwd_residual, out_dtypes=(F32, _MXU_DTYPE),
                               row_sums=1)
    dmn = _mm(d_kvx, w["wkv"], NT, bsz * mlen, d, 1024, name="d_mn", tk=1024)[0]
    g_memn = _rms_gain_grad(mem, small["mem_norm_g"], dmn, name="rms_mem_bwd")
    g_wout = _mm(y_mix, dh1_mx, TN, 1024, d, t, name="g_wout", tk=2048)[0]
    tok = comm.grads("mid", dict(w_out=g_wout, wq=g_wq, wkv=g_wkv, wo=g_wo))
    d_mix = _mm(dh1_mx, w["w_out"], NT, t, 1024, d, name="d_mix", tk=d, after=tok)[0]
    dproj_sw, g_swq, g_swk, g_sinks = _sw_bwd(proj_sw, rope, qg_t, kg_t, small["sw_sinks"], y_mix, d_mix, bsz, seq)
    tok = comm.poll(dproj_sw)
    dproj_hg, g_lb, g_hgn = _hg_bwd(proj_hg, small["hg_lower_bounds"], small["hg_norm_g"], o_hg, states, d_mix, bsz, seq,
                                    after=tok)
    in_rows = HG_COLS + SW_COLS
    sw_tile = 256
    g_in_t = _mm(dproj_hg, hn1, TN, HG_COLS, d, t, name="g_in_hg", tk=t,
                 out_shape=(jax.ShapeDtypeStruct((in_rows, d), F32),),
                 out_spec=(pl.BlockSpec((1024, min(1024, d)), lambda i, j, kk: (i, j)),))[0]
    g_in_t = _mm(dproj_sw, hn1, TN, SW_COLS, d, t, name="g_in_sw", tm=sw_tile, into=g_in_t,
                 out_shape=(jax.ShapeDtypeStruct((in_rows, d), F32),),
                 out_spec=(pl.BlockSpec((sw_tile, min(1024, d)), lambda i, j, kk: (HG_COLS // sw_tile + i, j)),))[0]
    tok = comm.grads("in", dict(w_in_t=g_in_t))
    grad_x, g_norm1 = _mm(dproj_hg, w_in_t, NN, t, d, HG_COLS, name="d_hn1", tk=HG_COLS, second=(dproj_sw, w_sw_t),
                          extras=(x, dh1), rows=(small["norm1_g"],), row_sums=1, after=tok,
                          epilogue=lambda acc, xv, dres, g: _rms_bwd_residual(acc, xv, dres, g)[1:])

    g_small = dict(norm1_g=g_norm1, hg_lower_bounds=g_lb, hg_norm_g=g_hgn, sw_q_norm_g=g_swq, sw_k_norm_g=g_swk,
                   sw_sinks=g_sinks[:, 0:SW_HEADS], norm2_g=g_norm2, mem_norm_g=g_memn, xa_q_norm_g=g_xq,
                   xa_k_norm_g=g_xk, norm3_g=g_norm3)
    return loss_row, grad_x.reshape(bsz, seq, d), g_small


SMALL_NAMES = ("norm1_g", "hg_lower_bounds", "hg_norm_g", "sw_q_norm_g", "sw_k_norm_g", "sw_sinks", "norm2_g",
               "mem_norm_g", "xa_q_norm_g", "xa_k_norm_g", "norm3_g")
BIG_NAMES = ("w_in", "w_out", "xa_wq", "xa_wkv", "xa_wo", "mlp_up", "mlp_down")
WEIGHT_ORDER = ("norm1_g", "w_in", "hg_lower_bounds", "hg_norm_g", "sw_q_norm_g", "sw_k_norm_g", "sw_sinks", "w_out",
                "norm2_g", "mem_norm_g", "xa_wq", "xa_wkv", "xa_q_norm_g", "xa_k_norm_g", "xa_wo", "norm3_g",
                "mlp_up", "mlp_down")


def _pack_rows(vals, width):
    starts, at = [], 0
    for v in vals:
        starts.append(at)
        at += v.shape[0]
    total = at + (-at) % 8
    out = None
    for v, s in zip(vals, starts):
        placed = jnp.pad(v, ((s, total - s - v.shape[0]), (0, width - v.shape[1])))
        out = placed if out is None else out + placed
    return out, starts


class _MeshWeights:
    LATE = ("w_out", "xa_wq", "xa_wkv", "xa_wo", "mlp_up", "mlp_down")

    def __init__(self, shards, d, ff):
        self.shards, self.d, self.ff = shards, d, ff
        self.c_idx = lax.axis_index("c").astype(jnp.int32).reshape(1)
        chip = (2 * lax.axis_index("x") + lax.axis_index("y")).astype(jnp.int32)
        self.place_idx = jnp.stack([chip, lax.axis_index("c").astype(jnp.int32)])
        self.pending = []
        self.exchanging = None

    def begin(self):
        shard = self.shards["w_in"]
        start, self.in_wait = _gather_chips_split(
            "gather_in", [shard], [_place_shard(shard, self.place_idx, name="place_w_in")])
        self.in_state = start()
        tok = (self.in_state["token"],)
        self.placed = list(_place_shards([self.shards[n] for n in self.LATE], self.place_idx, name="place_late",
                                         after=tok))
        return tok

    def first(self, after):
        _, lands = self.in_wait(self.in_state, (*after, *self.placed))
        (g_in,) = _gather_finish(lands, "gather_in_finish")
        start, self.late_wait = _gather_chips_split("gather_late", [self.shards[n] for n in self.LATE], self.placed)
        self.late_state = start(after=(g_in,))
        return dict(w_in_t=g_in.reshape(-1, self.d), token=self.late_state["token"])

    def rest(self, after):
        _, lands = self.late_wait(self.late_state, (after,))
        g_out, g_q, g_kv, g_o = _gather_finish(lands[:4], "gather_late_finish")
        start, self.mlp_wait = _gather_forward_split("gather_mlp_forward", lands[4:])
        self.mlp_state = start(after=(g_out,))
        d = self.d
        return dict(w_out=g_out.reshape(-1, d), wq=g_q.reshape(d, -1), wkv=g_kv.reshape(d, -1),
                    wo=jnp.concatenate([g_o[k] for k in range(4)], axis=1), token=self.mlp_state["token"])

    def mlp(self, after):
        _, (g_up, g_dn) = self.mlp_wait(self.mlp_state, (after,))
        return dict(up=g_up, down=g_dn.reshape(self.ff, self.d))

    def _scatter(self, tag, names, arrays, recv):
        parts = list(_add_halves(arrays, recv, self.c_idx, name="rs_add_halves_" + tag))
        start, wait = _scatter_chips_split("rs_scatter_" + tag, parts)
        state = start()
        self.pending.append((names, wait, state))
        return state["token"]

    def _advance(self, after):
        if self.exchanging is None:
            return ()
        tag, names, wait, state = self.exchanging
        self.exchanging = None
        arrays, recv = wait(state, (after,))
        return (self._scatter(tag, names, arrays, recv),)

    def poll(self, after):
        return self._advance(after)

    def grads(self, tag, g):
        d, ff = self.d, self.ff
        if tag == "mlp":
            names, arrays = ("mlp_up", "mlp_down"), [g["up"], g["down"].reshape(4, ff // 4, d)]
        elif tag == "mid":
            names = ("w_out", "xa_wq", "xa_wkv", "xa_wo")
            ds = d // 4
            g_wo = jnp.stack([g["wo"][:, ds * k:ds * (k + 1)] for k in range(4)])
            arrays = [g["w_out"].reshape(4, -1, d), g["wq"].reshape(4, d // 4, -1), g["wkv"].reshape(4, d // 4, -1), g_wo]
        else:
            names, arrays = ("w_in",), [g["w_in_t"].reshape(4, -1, d)]
        toks = self._advance(arrays[0])
        if tag == "in":
            return toks + (self._scatter(tag, names, arrays, _exchange_halves(arrays, "rs_exchange_" + tag)),)
        start, wait = _exchange_halves_split("rs_exchange_" + tag, arrays)
        state = start()
        self.exchanging = (tag, names, wait, state)
        return toks + (state["token"],)

    def finish(self, after):
        halves, tok = {}, ()
        for names, wait, state in self.pending:
            srcs, lands = wait(state, tuple(after) + tok)
            fulls = _add_chips(srcs, lands, self.place_idx, name="rs_add_chips_" + names[0], after=tok)
            tok = (fulls[0],)
            halves.update(zip(names, fulls))
        return dict(zip(BIG_NAMES, _join_halves([halves[n] for n in BIG_NAMES])))


def kernel(x, mem, positions, norm1_g, w_in, hg_lower_bounds, hg_norm_g, sw_q_norm_g, sw_k_norm_g, sw_sinks, w_out, norm2_g, mem_norm_g, xa_wq, xa_wkv, xa_q_norm_g, xa_k_norm_g, xa_wo, norm3_g, mlp_up, mlp_down, loss_target, m_norm1_g, m_w_in, m_hg_lower_bounds, m_hg_norm_g, m_sw_q_norm_g, m_sw_k_norm_g, m_sw_sinks, m_w_out, m_norm2_g, m_mem_norm_g, m_xa_wq, m_xa_wkv, m_xa_q_norm_g, m_xa_k_norm_g, m_xa_wo, m_norm3_g, m_mlp_up, m_mlp_down, v_norm1_g, v_w_in, v_hg_lower_bounds, v_hg_norm_g, v_sw_q_norm_g, v_sw_k_norm_g, v_sw_sinks, v_w_out, v_norm2_g, v_mem_norm_g, v_xa_wq, v_xa_wkv, v_xa_q_norm_g, v_xa_k_norm_g, v_xa_wo, v_norm3_g, v_mlp_up, v_mlp_down):
    given = dict(locals())
    weights = {n: given[n] for n in WEIGHT_ORDER}
    moms = {n: given["m_" + n] for n in WEIGHT_ORDER}
    vars_ = {n: given["v_" + n] for n in WEIGHT_ORDER}
    d = x.shape[-1]
    ff = mlp_down.shape[1] * 4
    small = {n: weights[n] for n in SMALL_NAMES}

    def plain(n, a):
        return jnp.swapaxes(a[0], 0, 1) if n == "w_in" else a[0]

    comm = _MeshWeights({n: plain(n, weights[n]).astype(_MXU_DTYPE) for n in BIG_NAMES}, d, ff)
    loss_row, grad_x, g_small = _local_step(x, mem, positions, loss_target, small, comm)
    packed, starts = _pack_rows([g_small[n] for n in SMALL_NAMES] + [loss_row], 1024)
    start, wait = _all_gather_small_split(packed)
    state = start()
    big_grads = comm.finish((grad_x, state["token"]))
    (own,), (gathered,) = wait(state, (big_grads[BIG_NAMES[0]],))
    device = (4 * lax.axis_index("x") + 2 * lax.axis_index("y") + lax.axis_index("c")).astype(jnp.int32).reshape(1)
    summed = _sum_devices(own, gathered, device)
    small_grads = {}
    for n, s in zip(SMALL_NAMES, starts):
        r, c = weights[n].shape
        small_grads[n] = summed[s:s + r, 0:c]
    loss = summed[starts[-1], 0]

    grads, deltas, new_m, new_v = {}, {}, {}, {}
    big_out = _adamw_big([plain(n, weights[n]) for n in BIG_NAMES], [big_grads[n] for n in BIG_NAMES],
                         [plain(n, moms[n]) for n in BIG_NAMES], [plain(n, vars_[n]) for n in BIG_NAMES])
    for n, outs in zip(BIG_NAMES, big_out):
        grads[n], deltas[n], new_m[n], new_v[n] = ((jnp.swapaxes(a, 0, 1) if n == "w_in" else a)[None] for a in outs)
    sm_out = _adamw_small([weights[n] for n in SMALL_NAMES], [small_grads[n] for n in SMALL_NAMES],
                          [moms[n] for n in SMALL_NAMES], [vars_[n] for n in SMALL_NAMES])
    ns = len(SMALL_NAMES)
    for i, n in enumerate(SMALL_NAMES):
        grads[n], deltas[n], new_m[n], new_v[n] = small_grads[n], sm_out[i], sm_out[ns + i], sm_out[2 * ns + i]

    return (loss, grad_x, *[grads[n] for n in WEIGHT_ORDER], *[deltas[n] for n in WEIGHT_ORDER],
            *[new_m[n] for n in WEIGHT_ORDER], *[new_v[n] for n in WEIGHT_ORDER])
```

```python
import numpy as np
import jax
import jax.numpy as jnp
from jax import lax
from jax.experimental import pallas as pl
from jax.experimental.pallas import tpu as pltpu

F32 = jnp.float32
_MXU_DTYPE = jnp.bfloat16

EPS = 1e-6
HG_HEADS = 4
HG_D = 128
HG_CHUNK = 64
HG_TILE = 512
HG_LEVELS = (32, 16, 8, 4, 2, 1)
SW_HEADS = 8
SW_KV_HEADS = 2
SW_GROUP = SW_HEADS // SW_KV_HEADS
SW_HD = 64
SW_BLOCK = 128
ROPE_THETA = 500000.0
ROT_DIM = SW_HD // 4
XA_HEADS = 4
XA_HD = 128
HG_COLS = 4 * HG_HEADS * HG_D
SW_COLS = (SW_HEADS + 2 * SW_KV_HEADS) * SW_HD

ADAM_LR = 0.001
ADAM_B1 = 0.9
ADAM_B2 = 0.999
ADAM_EPS = 1e-08
ADAM_WD = 0.01
ADAM_STEP = 10

VMEM_LIMIT = 56 * 1024 * 1024
MESH = pl.DeviceIdType.MESH

NN = ((1,), (0,))
NT = ((1,), (1,))
TN = ((0,), (0,))


def _mx(v):
    return v.astype(_MXU_DTYPE)


def _dot(a, b, dims=NN):
    return lax.dot_general(_mx(a), _mx(b), (dims, ((), ())), preferred_element_type=F32)


def _split_dot(a, v, dims, parts):
    acc = None
    rest = v
    for p in range(parts):
        piece = _mx(rest)
        term = lax.dot_general(a, piece, (dims, ((), ())), preferred_element_type=F32)
        acc = term if acc is None else acc + term
        if p + 1 < parts:
            rest = rest - piece.astype(F32)
    return acc


def _params(sem):
    return pltpu.CompilerParams(dimension_semantics=sem, vmem_limit_bytes=VMEM_LIMIT)


def _mm(a, b, mode, m, n, k, *, name, tm=1024, tn=1024, tk=1024, a_spec=None, b_spec=None, extras=(), rows=(),
        epilogue=None, out_dtypes=(F32,), row_sums=0, out_shape=None, out_spec=None, after=(), into=None,
        second=None):
    after = tuple(t for t in after if t is not None) + (() if into is None else (into,))
    tm, tn, tk = min(tm, m), min(tn, n), min(tk, k)
    assert m % tm == 0 and n % tn == 0 and k % tk == 0, (name, m, n, k, tm, tn, tk)
    gi, gj, gk = m // tm, n // tn, k // tk
    assert row_sums == 0 or gj == 1, name
    if a_spec is None:
        a_spec = (pl.BlockSpec((tk, tm), lambda i, j, kk: (kk, i)) if mode == TN
                  else pl.BlockSpec((tm, tk), lambda i, j, kk: (i, kk)))
    if b_spec is None:
        b_spec = (pl.BlockSpec((tn, tk), lambda i, j, kk: (j, kk)) if mode == NT
                  else pl.BlockSpec((tk, tn), lambda i, j, kk: (kk, j)))
    mn_spec = pl.BlockSpec((tm, tn), lambda i, j, kk: (i, j))
    if epilogue is None:
        epilogue = lambda acc: (acc,)
    row_spec = pl.BlockSpec((1, tn), lambda i, j, kk: (0, j))
    n_ex, n_out = len(extras) + len(rows), len(out_dtypes)
    if out_shape is None:
        out_shape = tuple(jax.ShapeDtypeStruct((m, n), d) for d in out_dtypes)
        out_spec = tuple(mn_spec for _ in out_dtypes)
    out_shape = tuple(out_shape) + tuple(jax.ShapeDtypeStruct((1, n), F32) for _ in range(row_sums))
    out_spec = tuple(out_spec) + tuple(row_spec for _ in range(row_sums))

    n_after = len(after)
    lead = 2 if second is None else 4
    assert second is None or (mode == NN and gk == 1), name
    second_specs = [] if second is None else [pl.BlockSpec((tm, second[0].shape[1]), lambda i, j, kk: (i, 0)),
                                                pl.BlockSpec((second[1].shape[0], tn), lambda i, j, kk: (0, j))]

    def body(*refs):
        a_ref, b_ref = refs[0], refs[1]
        ex = refs[lead:lead + n_ex]
        outs = refs[lead + n_ex + n_after:lead + n_ex + n_after + n_out + row_sums]
        first_row_tile = pl.program_id(0) == 0

        def finish(acc):
            res = epilogue(acc, *[e[...] for e in ex])
            for o, r in zip(outs[:n_out], res[:n_out]):
                o[...] = r.astype(o.dtype)
            if row_sums:
                @pl.when(first_row_tile)
                def _():
                    for o in outs[n_out:]:
                        o[...] = jnp.zeros_like(o)

                for o, r in zip(outs[n_out:], res[n_out:]):
                    o[...] += r

        if gk == 1 and second is not None:
            finish(_dot(a_ref[...], b_ref[...], mode) + _dot(refs[2][...], refs[3][...], NN))
        elif gk == 1:
            finish(_dot(a_ref[...], b_ref[...], mode))
        else:
            acc_ref = refs[-1]
            kk = pl.program_id(2)

            @pl.when(kk == 0)
            def _():
                acc_ref[...] = jnp.zeros_like(acc_ref)

            acc_ref[...] += _dot(a_ref[...], b_ref[...], mode)

            @pl.when(kk == gk - 1)
            def _():
                finish(acc_ref[...])

    return pl.pallas_call(
        body, name=name, grid=(gi, gj, gk),
        in_specs=([a_spec, b_spec] + second_specs + [mn_spec] * len(extras) + [row_spec] * len(rows)
                  + [pl.BlockSpec(memory_space=pl.ANY)] * n_after),
        out_specs=out_spec, out_shape=out_shape,
        input_output_aliases={} if into is None else {lead + n_ex + n_after - 1: 0},
        scratch_shapes=[pltpu.VMEM((tm, tn), F32)] if gk > 1 else [],
        compiler_params=_params(("arbitrary" if row_sums else "parallel", "parallel", "arbitrary")),
    )(a, b, *(second or ()), *extras, *rows, *after)


def _rms_rows(xv, g):
    return xv * lax.rsqrt(jnp.mean(xv * xv, axis=1, keepdims=True) + EPS) * g


def _rms_rows_bwd(xv, g, dyv):
    r = lax.rsqrt(jnp.mean(xv * xv, axis=1, keepdims=True) + EPS)
    u = dyv * g
    return (r * u - xv * (r * r * r) * jnp.mean(u * xv, axis=1, keepdims=True),
            jnp.sum(dyv * xv * r, axis=0, keepdims=True))


def _residual_rms(acc, res, g):
    h = acc + res
    return h, _rms_rows(h, g)


def _rms_bwd_residual(dhn, xv, dres, g):
    dx, dg = _rms_rows_bwd(xv, g, dhn)
    dx = dx + dres
    return dx, dx, dg


def _rms_fwd(x, g, *, name, tm=512, after=()):
    t, d = x.shape
    tm = min(tm, t)
    after = tuple(a for a in after if a is not None)

    def body(x_ref, g_ref, *rest):
        rest[-1][...] = _rms_rows(x_ref[...], g_ref[...]).astype(rest[-1].dtype)

    return pl.pallas_call(
        body, name=name, grid=(t // tm,),
        in_specs=[pl.BlockSpec((tm, d), lambda i: (i, 0)), pl.BlockSpec((1, d), lambda i: (0, 0))]
        + [pl.BlockSpec(memory_space=pl.ANY)] * len(after),
        out_specs=pl.BlockSpec((tm, d), lambda i: (i, 0)),
        out_shape=jax.ShapeDtypeStruct((t, d), _MXU_DTYPE),
        compiler_params=_params(("parallel",)),
    )(x, g, *after)


def _rms_gain_grad(x, g, dy, *, name, tm=512):
    t, d = x.shape
    tm = min(tm, t)

    def body(x_ref, g_ref, dy_ref, dg_ref):
        @pl.when(pl.program_id(0) == 0)
        def _():
            dg_ref[...] = jnp.zeros_like(dg_ref)

        dg_ref[...] += _rms_rows_bwd(x_ref[...], g_ref[...], dy_ref[...])[1]

    row = pl.BlockSpec((tm, d), lambda i: (i, 0))
    vec = pl.BlockSpec((1, d), lambda i: (0, 0))
    return pl.pallas_call(
        body, name=name, grid=(t // tm,), in_specs=[row, vec, row], out_specs=vec,
        out_shape=jax.ShapeDtypeStruct((1, d), F32), compiler_params=_params(("arbitrary",)),
    )(x, g, dy)


def _hg_constants():
    c = HG_CHUNK
    t = np.arange(c)
    sums = [t[None, :] <= t[:, None]]
    masks = []
    for m in HG_LEVELS:
        base = (t // (2 * m)) * (2 * m)
        mid = base + m - 1
        second = (t - base) >= m
        upper = (t[None, :] > mid[:, None]) & (t[None, :] <= t[:, None])
        lower = (t[None, :] > t[:, None]) & (t[None, :] <= mid[:, None])
        sums.append(np.where(second[:, None], upper, lower))
        masks.append(second[:, None] & (~second)[None, :] & (base[:, None] == base[None, :]))
    return (np.concatenate(sums, axis=0).astype(np.float32), np.stack(masks).astype(np.float32))


HG_HEAD_LANES = tuple(slice(HG_D * h, HG_D * (h + 1)) for h in range(HG_HEADS))


def _per_head(fn, slab):
    return jnp.concatenate([jnp.broadcast_to(fn(slab[:, hs]), (slab.shape[0], HG_D)) for hs in HG_HEAD_LANES], axis=1)


def _lane_sum(v):
    return jnp.sum(v, axis=1, keepdims=True)


def _lane_mean(v):
    return jnp.mean(v, axis=1, keepdims=True)


def _hg_gates(blk, lbp):
    w = HG_HEADS * HG_D
    q, x, v, gl = blk[:, 0:w], blk[:, w:2 * w], blk[:, 2 * w:3 * w], blk[:, 3 * w:4 * w]
    mx = jnp.max(lbp, axis=0, keepdims=True)
    e = jnp.exp(lbp - mx)
    lb = e[0:1, :] / jnp.sum(e, axis=0, keepdims=True)
    sig = jax.nn.sigmoid(x)
    f = lb + (1.0 - lb) * sig
    return q, v, gl, lb, sig, f, 1.0 - f, jnp.log(f)


def _hg_fwd(proj, lbp, ng, bsz, seq, *, y_width):
    t = proj.shape[0]
    nc = seq // HG_CHUNK
    a_np, m_np = _hg_constants()
    a_all = jnp.asarray(a_np, _MXU_DTYPE)
    masks = jnp.asarray(m_np, F32)
    nl = len(HG_LEVELS)

    ts = min(HG_TILE, seq)
    ns, nct = seq // ts, ts // HG_CHUNK
    hw = HG_HEADS * HG_D

    def body(p_ref, lb_ref, ng_ref, a_ref, m_ref, y_ref, o_ref, st_ref, sc_ref, carry):
        a_mat = a_ref[...]
        ngv = ng_ref[...]

        @pl.when(pl.program_id(0) == 0)
        def _():
            carry[...] = jnp.zeros_like(carry)

        ng4 = _tile_lanes(ngv, HG_HEADS)
        heads = range(HG_HEADS)
        exs = range(bsz)
        hl = HG_HEAD_LANES
        lbp_v = lb_ref[...]

        def chunk(c, _):
            rows = pl.ds(pl.multiple_of(c * HG_CHUNK, HG_CHUNK), HG_CHUNK)
            gates = [_hg_gates(p_ref[e, rows, :], lbp_v) for e in exs]
            q, v, gl = [g[0] for g in gates], [g[1] for g in gates], [g[2] for g in gates]
            k = [g[6] for g in gates]
            sts = [[carry[e, h] for h in heads] for e in exs]
            e_all = [_split_dot(a_mat, gates[e][7], NN, 3) for e in exs]
            b = [e_all[e][0:HG_CHUNK] for e in exs]
            qb = [q[e] * jnp.exp(b[e]) for e in exs]
            o = [[_dot(qb[e][:, hl[h]], sts[e][h], NT) for h in heads] for e in exs]
            p = [[jnp.zeros((HG_CHUNK, HG_CHUNK), F32) for _ in heads] for _ in exs]
            for li in range(nl):
                dec = [jnp.exp(e_all[e][HG_CHUNK * (li + 1):HG_CHUNK * (li + 2)]) for e in exs]
                qm, km, mk = [q[e] * dec[e] for e in exs], [k[e] * dec[e] for e in exs], m_ref[li]
                p = [[p[e][h] + mk * _dot(qm[e][:, hl[h]], km[e][:, hl[h]], NT) for h in heads] for e in exs]
            bl = [b[e][HG_CHUNK - 1:HG_CHUNK, :] for e in exs]
            kd = [k[e] * jnp.exp(bl[e] - b[e]) for e in exs]
            pv = [[_dot(p[e][h], v[e][:, hl[h]]) for h in heads] for e in exs]
            upd = [[_dot(v[e][:, hl[h]], kd[e][:, hl[h]], TN) for h in heads] for e in exs]
            for e in exs:
                o_all = (jnp.concatenate([o[e][h] + pv[e][h] for h in heads], axis=1)
                         + _per_head(_lane_sum, q[e] * k[e]) * v[e])
                r = lax.rsqrt(_per_head(_lane_mean, o_all * o_all) + EPS)
                ebl = jnp.exp(bl[e])
                for h in heads:
                    st_ref[e, h, c] = sts[e][h]
                    sc_ref[e, h, c] = _mx(p[e][h])
                    carry[e, h] = sts[e][h] * ebl[:, hl[h]] + upd[e][h]
                o_ref[e, rows, :] = o_all
                y_ref[e, rows, :] = (o_all * r * ng4) * (gl[e] * jax.nn.sigmoid(gl[e]))
            return 0

        lax.fori_loop(0, nct, chunk, 0)

    y3, o3, states, scores = pl.pallas_call(
        body, name="hgrn2_fwd", grid=(ns,),
        in_specs=[pl.BlockSpec((bsz, ts, HG_COLS), lambda s: (0, s, 0)),
                  pl.BlockSpec((2, hw), lambda s: (0, 0)),
                  pl.BlockSpec((1, HG_D), lambda s: (0, 0)),
                  pl.BlockSpec(a_all.shape, lambda s: (0, 0)),
                  pl.BlockSpec(masks.shape, lambda s: (0, 0, 0))],
        out_specs=(pl.BlockSpec((bsz, ts, hw), lambda s: (0, s, 0)),
                   pl.BlockSpec((bsz, ts, hw), lambda s: (0, s, 0)),
                   pl.BlockSpec((bsz, HG_HEADS, nct, HG_D, HG_D), lambda s: (0, 0, s, 0, 0)),
                   pl.BlockSpec((bsz, HG_HEADS, nct, HG_CHUNK, HG_CHUNK), lambda s: (0, 0, s, 0, 0))),
        out_shape=(jax.ShapeDtypeStruct((bsz, seq, y_width), F32),
                   jax.ShapeDtypeStruct((bsz, seq, hw), F32),
                   jax.ShapeDtypeStruct((bsz, HG_HEADS, nc, HG_D, HG_D), F32),
                   jax.ShapeDtypeStruct((bsz, HG_HEADS, nc, HG_CHUNK, HG_CHUNK), _MXU_DTYPE)),
        scratch_shapes=[pltpu.VMEM((bsz, HG_HEADS, HG_D, HG_D), F32)],
        compiler_params=_params(("arbitrary",)),
    )(proj.reshape(bsz, seq, HG_COLS), lbp, ng, a_all, masks)
    return y3.reshape(t, y_width), o3.reshape(t, hw), states, scores


def _hg_bwd(proj, lbp, ng, o_all, states, scores, dy, bsz, seq, after=()):
    after = tuple(a for a in after if a is not None)
    t = proj.shape[0]
    nc = seq // HG_CHUNK
    a_np, m_np = _hg_constants()
    a_all = jnp.asarray(a_np, _MXU_DTYPE)
    masks = jnp.asarray(m_np, F32)
    nl = len(HG_LEVELS)
    cs = HG_CHUNK

    ts = min(HG_TILE, seq)
    ns, nct = seq // ts, ts // cs
    hw = HG_HEADS * HG_D

    def body(p_ref, lb_ref, ng_ref, a_ref, m_ref, o_ref, st_ref, sc_ref, dy_ref, *rest):
        dp_ref, dlb_ref, dng_ref, dst_ref = rest[len(after):]
        a_mat = a_ref[...]
        ngv = ng_ref[...]
        ng4 = _tile_lanes(ngv, HG_HEADS)
        last_row = lax.broadcasted_iota(jnp.int32, (cs, hw), 0) == cs - 1
        first = pl.program_id(0) == 0
        heads = range(HG_HEADS)
        exs = range(bsz)
        hl = HG_HEAD_LANES
        lbp_v = lb_ref[...]

        @pl.when(first)
        def _():
            dst_ref[...] = jnp.zeros_like(dst_ref)

        def side_by_side(parts):
            return jnp.concatenate(parts, axis=1)

        def chunk(i, carry):
            dlb_acc, dng_acc = carry
            c = nct - 1 - i
            rows = pl.ds(pl.multiple_of(c * cs, cs), cs)
            gates = [_hg_gates(p_ref[e, rows, :], lbp_v) for e in exs]
            q, v, gl = [g[0] for g in gates], [g[1] for g in gates], [g[2] for g in gates]
            lb, sig, f, k = gates[0][3], [g[4] for g in gates], [g[5] for g in gates], [g[6] for g in gates]
            o = [o_ref[e, rows, :] for e in exs]
            dyv = [dy_ref[e, rows, :] for e in exs]
            sts = [[st_ref[e, h, c] for h in heads] for e in exs]
            dsts = [[dst_ref[e, h] for h in heads] for e in exs]
            e_all = [_split_dot(a_mat, gates[e][7], NN, 3) for e in exs]
            b = [e_all[e][0:cs] for e in exs]
            eb = [jnp.exp(b[e]) for e in exs]
            bl = [b[e][cs - 1:cs, :] for e in exs]
            ebl = [jnp.exp(bl[e]) for e in exs]
            ekd = [jnp.exp(bl[e] - b[e]) for e in exs]
            qb = [q[e] * eb[e] for e in exs]
            kd = [k[e] * ekd[e] for e in exs]
            do, dgl = [], []
            for e in exs:
                sg = jax.nn.sigmoid(gl[e])
                silu = gl[e] * sg
                r = lax.rsqrt(_per_head(_lane_mean, o[e] * o[e]) + EPS)
                dgl.append(dyv[e] * (o[e] * r * ng4) * (sg * (1.0 + gl[e] * (1.0 - sg))))
                u = dyv[e] * silu * ng4
                do.append(r * u - o[e] * (r * r * r) * _per_head(_lane_mean, u * o[e]))
                dng4 = jnp.sum(dyv[e] * silu * o[e] * r, axis=0, keepdims=True)
                dng_acc = dng_acc + ((dng4[:, hl[0]] + dng4[:, hl[1]]) + (dng4[:, hl[2]] + dng4[:, hl[3]]))
            es, qm, km = [], [], []
            for li in range(nl):
                dec = [jnp.exp(e_all[e][cs * (li + 1):cs * (li + 2)]) for e in exs]
                es.append(dec)
                qm.append([q[e] * dec[e] for e in exs])
                km.append([k[e] * dec[e] for e in exs])
            dp = [[_dot(do[e][:, hl[h]], v[e][:, hl[h]], NT) for h in heads] for e in exs]
            dv_p = [[_dot(sc_ref[e, h, c], do[e][:, hl[h]], TN) for h in heads] for e in exs]
            dv_s = [[_dot(kd[e][:, hl[h]], dsts[e][h], NT) for h in heads] for e in exs]
            dqb = [side_by_side([_dot(do[e][:, hl[h]], sts[e][h]) for h in heads]) for e in exs]
            dkd = [side_by_side([_dot(v[e][:, hl[h]], dsts[e][h]) for h in heads]) for e in exs]
            new_dst = [[_dot(do[e][:, hl[h]], qb[e][:, hl[h]], TN) for h in heads] for e in exs]
            dv = [side_by_side([dv_p[e][h] + dv_s[e][h] for h in heads]) + _per_head(_lane_sum, q[e] * k[e]) * do[e]
                  for e in exs]
            dq = [dqb[e] * eb[e] for e in exs]
            dk = [dkd[e] * ekd[e] for e in exs]
            de = []
            for e in exs:
                dbl = (jnp.sum(dkd[e] * kd[e], axis=0, keepdims=True)
                       + side_by_side([jnp.sum(dsts[e][h] * sts[e][h], axis=0, keepdims=True) for h in heads]) * ebl[e])
                de.append([dqb[e] * qb[e] - dkd[e] * kd[e] + jnp.where(last_row, dbl, 0.0)])
            for li in range(nl):
                mk = m_ref[li]
                dpm = [[mk * dp[e][h] for h in heads] for e in exs]
                dqm = [side_by_side([_dot(dpm[e][h], km[li][e][:, hl[h]]) for h in heads]) for e in exs]
                dkm = [side_by_side([_dot(dpm[e][h], qm[li][e][:, hl[h]], TN) for h in heads]) for e in exs]
                for e in exs:
                    dq[e] = dq[e] + dqm[e] * es[li][e]
                    dk[e] = dk[e] + dkm[e] * es[li][e]
                    de[e].append(dqm[e] * qm[li][e] + dkm[e] * km[li][e])
            dg = [_split_dot(a_mat, jnp.concatenate(de[e], axis=0), TN, 2) for e in exs]
            for e in exs:
                dpd = _per_head(_lane_sum, do[e] * v[e])
                df = dg[e] / f[e] - (dk[e] + dpd * q[e])
                dp_ref[e, rows, 0:hw] = _mx(dq[e] + dpd * k[e])
                dp_ref[e, rows, hw:2 * hw] = _mx(df * (1.0 - lb) * sig[e] * (1.0 - sig[e]))
                dp_ref[e, rows, 2 * hw:3 * hw] = _mx(dv[e])
                dp_ref[e, rows, 3 * hw:4 * hw] = _mx(dgl[e])
                for h in heads:
                    dst_ref[e, h] = dsts[e][h] * ebl[e][:, hl[h]] + new_dst[e][h]
                dlb_acc = dlb_acc + jnp.sum(df * (1.0 - sig[e]), axis=0, keepdims=True)
            return dlb_acc, dng_acc

        dlb, dng = lax.fori_loop(0, nct, chunk, (jnp.zeros((1, hw), F32), jnp.zeros((1, HG_D), F32)))

        @pl.when(first)
        def _():
            dlb_ref[...] = jnp.zeros_like(dlb_ref)
            dng_ref[...] = jnp.zeros_like(dng_ref)

        mx = jnp.max(lbp_v, axis=0, keepdims=True)
        e = jnp.exp(lbp_v - mx)
        s0 = e[0:1, :] / jnp.sum(e, axis=0, keepdims=True)
        da0 = dlb * s0 * (1.0 - s0)
        dlb_ref[...] += jnp.concatenate([da0, -da0], axis=0)
        dng_ref[...] += dng

    rows3 = lambda w: pl.BlockSpec((bsz, ts, w), lambda s: (0, ns - 1 - s, 0))
    dproj, dlb, dng = pl.pallas_call(
        body, name="hgrn2_bwd", grid=(ns,),
        in_specs=[rows3(HG_COLS),
                  pl.BlockSpec((2, hw), lambda s: (0, 0)),
                  pl.BlockSpec((1, HG_D), lambda s: (0, 0)),
                  pl.BlockSpec(a_all.shape, lambda s: (0, 0)),
                  pl.BlockSpec(masks.shape, lambda s: (0, 0, 0)),
                  rows3(hw),
                  pl.BlockSpec((bsz, HG_HEADS, nct, HG_D, HG_D), lambda s: (0, 0, ns - 1 - s, 0, 0)),
                  pl.BlockSpec((bsz, HG_HEADS, nct, cs, cs), lambda s: (0, 0, ns - 1 - s, 0, 0)),
                  rows3(hw)] + [pl.BlockSpec(memory_space=pl.ANY)] * len(after),
        out_specs=(rows3(HG_COLS),
                   pl.BlockSpec((2, hw), lambda s: (0, 0)),
                   pl.BlockSpec((1, HG_D), lambda s: (0, 0))),
        out_shape=(jax.ShapeDtypeStruct((bsz, seq, HG_COLS), _MXU_DTYPE),
                   jax.ShapeDtypeStruct((2, hw), F32),
                   jax.ShapeDtypeStruct((1, HG_D), F32)),
        scratch_shapes=[pltpu.VMEM((bsz, HG_HEADS, HG_D, HG_D), F32)],
        compiler_params=_params(("arbitrary",)),
    )(proj.reshape(bsz, seq, HG_COLS), lbp, ng, a_all, masks, o_all.reshape(bsz, seq, hw), states, scores,
      dy.reshape(bsz, seq, dy.shape[1]), *after)
    return dproj.reshape(t, HG_COLS), dlb, dng


def _sw_constants():
    half = ROT_DIM // 2
    inv = (np.float32(ROPE_THETA) ** (-(np.arange(half, dtype=np.float32) * np.float32(2.0) / np.float32(ROT_DIM)))
           ).astype(np.float32)
    freq = np.zeros((1, 128), np.float32)
    sign = np.zeros((1, 128), np.float32)
    for h in range(2):
        freq[0, 64 * h:64 * h + half] = inv
        freq[0, 64 * h + half:64 * h + 2 * half] = inv
        sign[0, 64 * h:64 * h + half] = -1.0
        sign[0, 64 * h + half:64 * h + 2 * half] = 1.0
    seg = np.kron(np.eye(8, dtype=np.float32), np.full((64, 64), 1.0 / 64.0, np.float32))
    return freq, sign, seg


def _rope_table(pos, *, tm=512, after=()):
    t = pos.shape[0]
    tm = min(tm, t)
    freq_np, sign_np, _ = _sw_constants()
    after = tuple(a for a in after if a is not None)

    def body(p_ref, f_ref, s_ref, *rest):
        o_ref = rest[-1]
        ang = p_ref[...].astype(F32) * f_ref[...]
        o_ref[:, 0:128] = jnp.cos(ang)
        o_ref[:, 128:256] = jnp.sin(ang) * s_ref[...]

    vec = pl.BlockSpec((1, 128), lambda i: (0, 0))
    return pl.pallas_call(
        body, name="rope_table", grid=(t // tm,),
        in_specs=[pl.BlockSpec((tm, 1), lambda i: (i, 0)), vec, vec] + [pl.BlockSpec(memory_space=pl.ANY)] * len(after),
        out_specs=pl.BlockSpec((tm, 256), lambda i: (i, 0)),
        out_shape=jax.ShapeDtypeStruct((t, 256), F32),
        compiler_params=_params(("parallel",)),
    )(pos, jnp.asarray(freq_np), jnp.asarray(sign_np), *after)


def _tile_lanes(v, times):
    return v if times == 1 else jnp.concatenate([v] * times, axis=1)


def _swap_halves(v):
    w = v.shape[1]
    half = ROT_DIM // 2
    lane = lax.broadcasted_iota(jnp.int32, v.shape, 1) % SW_HD
    return jnp.where(lane < half, pltpu.roll(v, w - half, 1), jnp.where(lane < 2 * half, pltpu.roll(v, half, 1), 0.0))


def _sw_norm_rope(tv, gain, seg, cosv, sinv):
    w = tv.shape[1]
    ms = _split_dot_rhs(tv * tv, seg[0:w, 0:w])
    r = lax.rsqrt(ms + EPS)
    tn = tv * r * gain
    reps = w // 128
    return tn * _tile_lanes(cosv, reps) + _swap_halves(tn) * _tile_lanes(sinv, reps), r


def _split_dot_rhs(v, a):
    hi = _mx(v)
    lo = _mx(v - hi.astype(F32))
    return (lax.dot_general(hi, a, (NN, ((), ())), preferred_element_type=F32)
            + lax.dot_general(lo, a, (NN, ((), ())), preferred_element_type=F32))


def _sw_norm_rope_bwd(dt, tv, r, gain, seg, cosv, sinv):
    w = tv.shape[1]
    reps = w // 128
    dtn = dt * _tile_lanes(cosv, reps) + _swap_halves(dt * _tile_lanes(sinv, reps))
    u = dtn * gain
    dtv = r * u - tv * (r * r * r) * _split_dot_rhs(u * tv, seg[0:w, 0:w])
    return dtv, jnp.sum(dtn * tv * r, axis=0, keepdims=True)


def _sw_scores(qh, kp, kc):
    return _dot(qh, kp, NT), _dot(qh, kc, NT)


SW_SCALE = SW_HD ** -0.5


def _sw_probs(raw, sink, first_block):
    qi = lax.broadcasted_iota(jnp.int32, (SW_BLOCK, SW_BLOCK), 0)
    kj = lax.broadcasted_iota(jnp.int32, (SW_BLOCK, SW_BLOCK), 1)
    ok_prev = jnp.logical_and(kj > qi, jnp.logical_not(first_block))
    ok_cur = kj <= qi
    sp = jnp.where(ok_prev, raw[0], -jnp.inf)
    sc = jnp.where(ok_cur, raw[1], -jnp.inf)
    m = jnp.maximum(jnp.maximum(jnp.max(sp, axis=1, keepdims=True), jnp.max(sc, axis=1, keepdims=True)), sink)
    pp, pc = jnp.exp(sp - m), jnp.exp(sc - m)
    es = jnp.exp(sink - m)
    inv = 1.0 / (jnp.sum(pp, axis=1, keepdims=True) + jnp.sum(pc, axis=1, keepdims=True) + es)
    return pp * inv, pc * inv, es * inv


def _sw_specs(nb):
    def cur(b, n):
        return b * nb + jnp.minimum(n, nb - 1)

    def prev(b, n):
        return b * nb + jnp.maximum(jnp.minimum(n, nb - 1) - 1, 0)

    return cur, prev


def _sw_fwd(proj, rope, qg, kg, sinks, y_in, bsz, seq):
    t = proj.shape[0]
    nb = seq // SW_BLOCK
    seg = jnp.asarray(_sw_constants()[2], _MXU_DTYPE)
    cur, prev = _sw_specs(nb)

    def body(q_ref, kc_ref, kp_ref, vc_ref, vp_ref, rc_ref, rp_ref, qg_ref, kg_ref, sk_ref, seg_ref, yin_ref, y_ref):
        del yin_ref
        n = pl.program_id(1)
        segv = seg_ref[...]
        cos_c, sin_c = rc_ref[:, 0:128], rc_ref[:, 128:256]
        cos_p, sin_p = rp_ref[:, 0:128], rp_ref[:, 128:256]
        qr, _ = _sw_norm_rope(q_ref[...], qg_ref[...] * SW_SCALE, segv, cos_c, sin_c)
        kcr, _ = _sw_norm_rope(kc_ref[...], kg_ref[...], segv, cos_c, sin_c)
        kpr, _ = _sw_norm_rope(kp_ref[...], kg_ref[...], segv, cos_p, sin_p)
        vc, vp = vc_ref[...], vp_ref[...]
        ks = [slice(SW_HD * (h // SW_GROUP), SW_HD * (h // SW_GROUP + 1)) for h in range(SW_HEADS)]
        raw = [_sw_scores(qr[:, SW_HD * h:SW_HD * (h + 1)], kpr[:, ks[h]], kcr[:, ks[h]]) for h in range(SW_HEADS)]
        probs = [_sw_probs(raw[h], sk_ref[0, h], n == 0) for h in range(SW_HEADS)]
        for h in range(SW_HEADS):
            y_ref[:, SW_HD * h:SW_HD * (h + 1)] = _dot(probs[h][0], vp[:, ks[h]]) + _dot(probs[h][1], vc[:, ks[h]])

    rowq = pl.BlockSpec((SW_BLOCK, 512), lambda b, n: (cur(b, n), 0))
    full = lambda a: pl.BlockSpec(a.shape, lambda b, n: (0,) * a.ndim)
    yw = y_in.shape[1]
    return pl.pallas_call(
        body, name="swa_fwd", grid=(bsz, nb),
        in_specs=[rowq,
                  pl.BlockSpec((SW_BLOCK, 128), lambda b, n: (cur(b, n), 4)),
                  pl.BlockSpec((SW_BLOCK, 128), lambda b, n: (prev(b, n), 4)),
                  pl.BlockSpec((SW_BLOCK, 128), lambda b, n: (cur(b, n), 5)),
                  pl.BlockSpec((SW_BLOCK, 128), lambda b, n: (prev(b, n), 5)),
                  pl.BlockSpec((SW_BLOCK, 256), lambda b, n: (cur(b, n), 0)),
                  pl.BlockSpec((SW_BLOCK, 256), lambda b, n: (prev(b, n), 0)),
                  full(qg), full(kg),
                  pl.BlockSpec(memory_space=pltpu.SMEM),
                  full(seg),
                  pl.BlockSpec(memory_space=pl.ANY)],
        out_specs=pl.BlockSpec((SW_BLOCK, 512), lambda b, n: (cur(b, n), 1)),
        out_shape=jax.ShapeDtypeStruct((t, yw), F32),
        input_output_aliases={11: 0},
        compiler_params=_params(("parallel", "parallel")),
    )(proj, proj, proj, proj, proj, rope, rope, qg, kg, sinks, seg, y_in)


def _sw_bwd(proj, rope, qg, kg, sinks, y, dy, bsz, seq):
    t = proj.shape[0]
    nb = seq // SW_BLOCK
    seg = jnp.asarray(_sw_constants()[2], _MXU_DTYPE)
    cur, prev = _sw_specs(nb)

    def body(q_ref, kc_ref, kp_ref, vc_ref, vp_ref, rc_ref, rp_ref, qg_ref, kg_ref, sk_ref, seg_ref,
             y_ref, dy_ref, dp_ref, dqg_ref, dkg_ref, dsk_ref,
             dq_car, dkv_car, dqr_s, dkc_s, dkp_s, dvc_s, dvp_s, gq_acc, gk_acc, sk_acc):
        b, n = pl.program_id(0), pl.program_id(1)
        first = jnp.logical_and(b == 0, n == 0)
        last = jnp.logical_and(b == pl.num_programs(0) - 1, n == nb)

        @pl.when(first)
        def _():
            gq_acc[...] = jnp.zeros_like(gq_acc)
            gk_acc[...] = jnp.zeros_like(gk_acc)
            sk_acc[...] = jnp.zeros_like(sk_acc)

        @pl.when(n < nb)
        def _():
            segv = seg_ref[...]
            cos_c, sin_c = rc_ref[:, 0:128], rc_ref[:, 128:256]
            cos_p, sin_p = rp_ref[:, 0:128], rp_ref[:, 128:256]
            qv, kcv, kpv = q_ref[...], kc_ref[...], kp_ref[...]
            qgain = qg_ref[...] * SW_SCALE
            qr, rq = _sw_norm_rope(qv, qgain, segv, cos_c, sin_c)
            kcr, rkc = _sw_norm_rope(kcv, kg_ref[...], segv, cos_c, sin_c)
            kpr, rkp = _sw_norm_rope(kpv, kg_ref[...], segv, cos_p, sin_p)
            vc, vp = vc_ref[...], vp_ref[...]
            lane = lax.broadcasted_iota(jnp.int32, (1, 128), 1)
            dsk = jnp.zeros((1, 128), F32)
            heads = range(SW_HEADS)
            ks = [slice(SW_HD * (h // SW_GROUP), SW_HD * (h // SW_GROUP + 1)) for h in heads]
            hs = [slice(SW_HD * h, SW_HD * (h + 1)) for h in heads]
            qh = [qr[:, hs[h]] for h in heads]
            doh = [dy_ref[:, hs[h]] for h in heads]
            raw = [_sw_scores(qh[h], kpr[:, ks[h]], kcr[:, ks[h]]) for h in heads]
            dpp = [_dot(doh[h], vp[:, ks[h]], NT) for h in heads]
            dpc = [_dot(doh[h], vc[:, ks[h]], NT) for h in heads]
            probs = [_sw_probs(raw[h], sk_ref[0, h], n == 0) for h in heads]
            dsp, dsc = [], []
            for h in heads:
                pp, pc, ps = probs[h]
                delta = jnp.sum(doh[h] * y_ref[:, hs[h]], axis=1, keepdims=True)
                dsp.append(pp * (dpp[h] - delta))
                dsc.append(pc * (dpc[h] - delta))
                dsk = dsk + jnp.where(lane == h, -jnp.sum(ps * delta), 0.0)
            for h in heads:
                dqr_s[:, hs[h]] = _dot(dsp[h], kpr[:, ks[h]]) + _dot(dsc[h], kcr[:, ks[h]])
            for kv in range(SW_KV_HEADS):
                group = range(SW_GROUP * kv, SW_GROUP * (kv + 1))
                kvs = slice(SW_HD * kv, SW_HD * (kv + 1))
                dvp_s[:, kvs] = sum(_dot(probs[h][0], doh[h], TN) for h in group)
                dvc_s[:, kvs] = sum(_dot(probs[h][1], doh[h], TN) for h in group)
                dkp_s[:, kvs] = sum(_dot(dsp[h], qh[h], TN) for h in group)
                dkc_s[:, kvs] = sum(_dot(dsc[h], qh[h], TN) for h in group)
            dq, gq = _sw_norm_rope_bwd(dqr_s[...], qv, rq, qgain, segv, cos_c, sin_c)
            dkc, gkc = _sw_norm_rope_bwd(dkc_s[...], kcv, rkc, kg_ref[...], segv, cos_c, sin_c)
            dkp, gkp = _sw_norm_rope_bwd(dkp_s[...], kpv, rkp, kg_ref[...], segv, cos_p, sin_p)
            gq_acc[...] += gq
            gk_acc[...] += gkc + gkp
            sk_acc[...] += dsk

            @pl.when(n > 0)
            def _():
                dp_ref[:, 0:512] = _mx(dq_car[...])
                dp_ref[:, 512:640] = _mx(dkv_car[:, 0:128] + dkp)
                dp_ref[:, 640:768] = _mx(dkv_car[:, 128:256] + dvp_s[...])

            dq_car[...] = dq
            dkv_car[:, 0:128] = dkc
            dkv_car[:, 128:256] = dvc_s[...]

        @pl.when(n == nb)
        def _():
            dp_ref[:, 0:512] = _mx(dq_car[...])
            dp_ref[:, 512:768] = _mx(dkv_car[...])

        @pl.when(last)
        def _():
            gq = gq_acc[...] * SW_SCALE
            acc = gq[:, 0:SW_HD]
            for h in range(1, SW_HEADS):
                acc = acc + gq[:, SW_HD * h:SW_HD * (h + 1)]
            dqg_ref[...] = acc
            gk = gk_acc[...]
            dkg_ref[...] = gk[:, 0:SW_HD] + gk[:, SW_HD:2 * SW_HD]
            dsk_ref[...] = sk_acc[...]

    rowq = pl.BlockSpec((SW_BLOCK, 512), lambda b, n: (cur(b, n), 0))
    full = lambda a: pl.BlockSpec(a.shape, lambda b, n: (0,) * a.ndim)

    def out_row(b, n):
        return b * nb + jnp.maximum(n - 1, 0)

    return pl.pallas_call(
        body, name="swa_bwd", grid=(bsz, nb + 1),
        in_specs=[rowq,
                  pl.BlockSpec((SW_BLOCK, 128), lambda b, n: (cur(b, n), 4)),
                  pl.BlockSpec((SW_BLOCK, 128), lambda b, n: (prev(b, n), 4)),
                  pl.BlockSpec((SW_BLOCK, 128), lambda b, n: (cur(b, n), 5)),
                  pl.BlockSpec((SW_BLOCK, 128), lambda b, n: (prev(b, n), 5)),
                  pl.BlockSpec((SW_BLOCK, 256), lambda b, n: (cur(b, n), 0)),
                  pl.BlockSpec((SW_BLOCK, 256), lambda b, n: (prev(b, n), 0)),
                  full(qg), full(kg),
                  pl.BlockSpec(memory_space=pltpu.SMEM),
                  full(seg),
                  pl.BlockSpec((SW_BLOCK, 512), lambda b, n: (cur(b, n), 1)),
                  pl.BlockSpec((SW_BLOCK, 512), lambda b, n: (cur(b, n), 1))],
        out_specs=(pl.BlockSpec((SW_BLOCK, SW_COLS), lambda b, n: (out_row(b, n), 0)),
                   pl.BlockSpec((1, SW_HD), lambda b, n: (0, 0)),
                   pl.BlockSpec((1, SW_HD), lambda b, n: (0, 0)),
                   pl.BlockSpec((1, 128), lambda b, n: (0, 0))),
        out_shape=(jax.ShapeDtypeStruct((t, SW_COLS), _MXU_DTYPE),
                   jax.ShapeDtypeStruct((1, SW_HD), F32),
                   jax.ShapeDtypeStruct((1, SW_HD), F32),
                   jax.ShapeDtypeStruct((1, 128), F32)),
        scratch_shapes=[pltpu.VMEM((SW_BLOCK, 512), F32), pltpu.VMEM((SW_BLOCK, 256), F32),
                        pltpu.VMEM((SW_BLOCK, 512), F32),
                        pltpu.VMEM((SW_BLOCK, 128), F32), pltpu.VMEM((SW_BLOCK, 128), F32),
                        pltpu.VMEM((SW_BLOCK, 128), F32), pltpu.VMEM((SW_BLOCK, 128), F32),
                        pltpu.VMEM((1, 512), F32), pltpu.VMEM((1, 128), F32), pltpu.VMEM((1, 128), F32)],
        compiler_params=_params(("arbitrary", "arbitrary")),
    )(proj, proj, proj, proj, proj, rope, rope, qg, kg, sinks, seg, y, dy)


def _head_rms(tv, gain):
    r = lax.rsqrt(jnp.mean(tv * tv, axis=1, keepdims=True) + EPS)
    return tv * r * gain, r


def _head_rms_bwd(dtn, tv, r, gain):
    u = dtn * gain
    return r * u - tv * (r * r * r) * jnp.mean(u * tv, axis=1, keepdims=True), jnp.sum(dtn * tv * r, axis=0, keepdims=True)


def _xa_softmax(raw):
    s = raw * (XA_HD ** -0.5)
    e = jnp.exp(s - jnp.max(s, axis=1, keepdims=True))
    return e * (1.0 / jnp.sum(e, axis=1, keepdims=True))


def _xa_fwd(qx, kvx, qg, kg, bsz, seq, mlen, *, tq=512):
    t = qx.shape[0]
    tq = min(tq, seq)
    nq = seq // tq
    w = XA_HEADS * XA_HD

    def body(q_ref, kv_ref, qg_ref, kg_ref, o_ref):
        heads = range(XA_HEADS)
        hs = [slice(XA_HD * h, XA_HD * (h + 1)) for h in heads]
        qn = [_head_rms(q_ref[:, hs[h]], qg_ref[...])[0] for h in heads]
        kn = [_head_rms(kv_ref[:, hs[h]], kg_ref[...])[0] for h in heads]
        raw = [_dot(qn[h], kn[h], NT) for h in heads]
        p = [_xa_softmax(raw[h]) for h in heads]
        for h in heads:
            o_ref[:, hs[h]] = _dot(p[h], kv_ref[:, w + XA_HD * h:w + XA_HD * (h + 1)]).astype(o_ref.dtype)

    vec = pl.BlockSpec((1, XA_HD), lambda b, i: (0, 0))
    return pl.pallas_call(
        body, name="xattn_fwd", grid=(bsz, nq),
        in_specs=[pl.BlockSpec((tq, w), lambda b, i: (b * nq + i, 0)),
                  pl.BlockSpec((mlen, 2 * w), lambda b, i: (b, 0)), vec, vec],
        out_specs=pl.BlockSpec((tq, w), lambda b, i: (b * nq + i, 0)),
        out_shape=jax.ShapeDtypeStruct((t, w), _MXU_DTYPE),
        compiler_params=_params(("parallel", "parallel")),
    )(qx, kvx, qg, kg)


def _xa_bwd(qx, kvx, qg, kg, do, bsz, seq, mlen, *, tq=1024):
    t = qx.shape[0]
    tq = min(tq, seq)
    nq = seq // tq
    w = XA_HEADS * XA_HD
    scale = XA_HD ** -0.5

    def body(q_ref, kv_ref, qg_ref, kg_ref, do_ref, dq_ref, dkv_ref, dqg_ref, dkg_ref):
        b, i = pl.program_id(0), pl.program_id(1)

        @pl.when(jnp.logical_and(b == 0, i == 0))
        def _():
            dqg_ref[...] = jnp.zeros_like(dqg_ref)
            dkg_ref[...] = jnp.zeros_like(dkg_ref)

        @pl.when(i == 0)
        def _():
            dkv_ref[...] = jnp.zeros_like(dkv_ref)

        heads = range(XA_HEADS)
        hs = [slice(XA_HD * h, XA_HD * (h + 1)) for h in heads]
        vs = [slice(w + XA_HD * h, w + XA_HD * (h + 1)) for h in heads]
        qv = [q_ref[:, hs[h]] for h in heads]
        kv = [kv_ref[:, hs[h]] for h in heads]
        doh = [do_ref[:, hs[h]] for h in heads]
        qn = [_head_rms(qv[h], qg_ref[...]) for h in heads]
        kn = [_head_rms(kv[h], kg_ref[...]) for h in heads]
        raw = [_dot(qn[h][0], kn[h][0], NT) for h in heads]
        dp = [_dot(doh[h], kv_ref[:, vs[h]], NT) for h in heads]
        p = [_xa_softmax(raw[h]) for h in heads]
        ds = [p[h] * (dp[h] - jnp.sum(p[h] * dp[h], axis=1, keepdims=True)) * scale for h in heads]
        dqn = [_dot(ds[h], kn[h][0]) for h in heads]
        dkn = [_dot(ds[h], qn[h][0], TN) for h in heads]
        dvv = [_dot(p[h], doh[h], TN) for h in heads]
        gq_sum = jnp.zeros((1, XA_HD), F32)
        gk_sum = jnp.zeros((1, XA_HD), F32)
        for h in heads:
            dqv, gq = _head_rms_bwd(dqn[h], qv[h], qn[h][1], qg_ref[...])
            dkv, gk = _head_rms_bwd(dkn[h], kv[h], kn[h][1], kg_ref[...])
            dq_ref[:, hs[h]] = dqv.astype(dq_ref.dtype)
            dkv_ref[:, hs[h]] += dkv
            dkv_ref[:, vs[h]] += dvv[h]
            gq_sum = gq_sum + gq
            gk_sum = gk_sum + gk
        dqg_ref[...] += gq_sum
        dkg_ref[...] += gk_sum

    vec = pl.BlockSpec((1, XA_HD), lambda b, i: (0, 0))
    row = pl.BlockSpec((tq, w), lambda b, i: (b * nq + i, 0))
    mem = pl.BlockSpec((mlen, 2 * w), lambda b, i: (b, 0))
    return pl.pallas_call(
        body, name="xattn_bwd", grid=(bsz, nq),
        in_specs=[row, mem, vec, vec, row],
        out_specs=(row, mem, vec, vec),
        out_shape=(jax.ShapeDtypeStruct((t, w), _MXU_DTYPE), jax.ShapeDtypeStruct((bsz * mlen, 2 * w), F32),
                   jax.ShapeDtypeStruct((1, XA_HD), F32), jax.ShapeDtypeStruct((1, XA_HD), F32)),
        compiler_params=_params(("arbitrary", "arbitrary")),
    )(qx, kvx, qg, kg, do)


def _loss_finish(sq_row, d_model):
    def body(s_ref, o_ref):
        o_ref[...] = jnp.zeros_like(o_ref) + 0.5 * jnp.sum(s_ref[...]) / float(d_model)

    return pl.pallas_call(body, name="loss_finish", out_shape=jax.ShapeDtypeStruct((1, 128), F32))(sq_row)


def _adamw_math(w, g, m, v):
    m = ADAM_B1 * m + (1.0 - ADAM_B1) * g
    v = ADAM_B2 * v + (1.0 - ADAM_B2) * (g * g)
    m_hat = m / (1.0 - ADAM_B1 ** ADAM_STEP)
    v_hat = v / (1.0 - ADAM_B2 ** ADAM_STEP)
    return -ADAM_LR * (m_hat / (jnp.sqrt(v_hat) + ADAM_EPS) + ADAM_WD * w), m, v


def _adamw_big(ws, gs, ms, vs, *, steps=8):
    n = len(ws)

    def body(*refs):
        for a in range(n):
            gv = refs[n + a][...]
            d, mn, vn = _adamw_math(refs[a][...], gv, refs[2 * n + a][...], refs[3 * n + a][...])
            refs[4 * n + 4 * a][...] = gv
            refs[4 * n + 4 * a + 1][...] = d
            refs[4 * n + 4 * a + 2][...] = mn
            refs[4 * n + 4 * a + 3][...] = vn

    def spec(w):
        assert w.shape[0] % (8 * steps) == 0, w.shape
        return pl.BlockSpec((w.shape[0] // steps, w.shape[1]), lambda i: (i, 0))

    specs = [spec(w) for w in ws]
    out = pl.pallas_call(
        body, name="adamw_big", grid=(steps,), in_specs=specs * 4,
        out_specs=tuple(s for s in specs for _ in range(4)),
        out_shape=tuple(jax.ShapeDtypeStruct(w.shape, F32) for w in ws for _ in range(4)),
        compiler_params=_params(("parallel",)),
    )(*ws, *gs, *ms, *vs)
    return [out[4 * a:4 * a + 4] for a in range(n)]


def _adamw_small(ws, gs, ms, vs):
    n = len(ws)

    def body(*refs):
        for i in range(n):
            d, mn, vn = _adamw_math(refs[i][...], refs[n + i][...], refs[2 * n + i][...], refs[3 * n + i][...])
            refs[4 * n + i][...] = d
            refs[5 * n + i][...] = mn
            refs[6 * n + i][...] = vn

    shapes = tuple(jax.ShapeDtypeStruct(w.shape, F32) for w in ws)
    return pl.pallas_call(body, name="adamw_small", out_shape=shapes * 3)(*ws, *gs, *ms, *vs)


def _add_halves(gs, recvs, c_idx, *, name):
    n = len(gs)

    def body(c_ref, *refs):
        del c_ref
        for a in range(n):
            refs[2 * n + a][...] = refs[a][...] + refs[n + a][...]

    def half(g):
        return pl.BlockSpec((None, g.shape[1] // 2, g.shape[2]), lambda k, cr: (k, cr[0], 0))

    def whole(g):
        return pl.BlockSpec((None, g.shape[1] // 2, g.shape[2]), lambda k, cr: (k, 0, 0))

    return pl.pallas_call(
        body, name=name,
        grid_spec=pltpu.PrefetchScalarGridSpec(
            num_scalar_prefetch=1, grid=(4,),
            in_specs=[half(g) for g in gs] + [whole(g) for g in gs],
            out_specs=tuple(whole(g) for g in gs)),
        out_shape=tuple(jax.ShapeDtypeStruct((4, g.shape[1] // 2, g.shape[2]), F32) for g in gs),
        compiler_params=_params(("parallel",)),
    )(c_idx, *gs, *recvs)


def _add_chips(ps, recvs, place_idx, *, name, steps=4, after=()):
    n = len(ps)

    def body(pi_ref, *refs):
        del pi_ref
        outs = refs[2 * n + len(after):]
        for a in range(n):
            r_ref = refs[n + a]
            outs[a][...] = ((refs[a][...] + r_ref[0]) + r_ref[1]) + r_ref[2]

    def tile(p):
        assert p.shape[1] % (8 * steps) == 0, (name, p.shape)
        return p.shape[1] // steps

    return pl.pallas_call(
        body, name=name,
        grid_spec=pltpu.PrefetchScalarGridSpec(
            num_scalar_prefetch=1, grid=(steps,),
            in_specs=[pl.BlockSpec((None, tile(p), p.shape[2]), lambda i, pi: (pi[0], i, 0)) for p in ps]
            + [pl.BlockSpec((3, tile(p), p.shape[2]), lambda i, pi: (0, i, 0)) for p in ps]
            + [pl.BlockSpec(memory_space=pl.ANY)] * len(after),
            out_specs=tuple(pl.BlockSpec((tile(p), p.shape[2]), lambda i, pi: (pi[1] * steps + i, 0)) for p in ps)),
        out_shape=tuple(jax.ShapeDtypeStruct((2 * p.shape[1], p.shape[2]), F32) for p in ps),
        compiler_params=_params(("parallel",)),
    )(place_idx, *ps, *recvs, *after)


def _place_shards(shards, place_idx, *, name, after=()):
    n = len(shards)

    def body(pi_ref, *refs):
        del pi_ref
        for i in range(n):
            refs[n + len(after) + i][...] = refs[i][...]

    return pl.pallas_call(
        body, name=name,
        grid_spec=pltpu.PrefetchScalarGridSpec(
            num_scalar_prefetch=1, grid=(1,),
            in_specs=[pl.BlockSpec(s.shape, lambda i, pi: (0, 0)) for s in shards]
            + [pl.BlockSpec(memory_space=pl.ANY)] * len(after),
            out_specs=tuple(pl.BlockSpec((None,) + s.shape, lambda i, pi: (pi[0], 0, 0)) for s in shards)),
        out_shape=tuple(jax.ShapeDtypeStruct((4,) + s.shape, s.dtype) for s in shards),
        compiler_params=_params(("arbitrary",)),
    )(place_idx, *shards, *after)


def _place_shard(shard, place_idx, *, name, tr=512, after=()):
    r, c = shard.shape
    tr = min(tr, r)
    if r % tr:
        tr = r // 2
    assert r % tr == 0 and tr % 16 == 0, (name, r, tr)

    def body(pi_ref, s_ref, *rest):
        del pi_ref
        rest[-1][...] = s_ref[...]

    return pl.pallas_call(
        body, name=name,
        grid_spec=pltpu.PrefetchScalarGridSpec(
            num_scalar_prefetch=1, grid=(r // tr,),
            in_specs=[pl.BlockSpec((tr, c), lambda i, pi: (i, 0))] + [pl.BlockSpec(memory_space=pl.ANY)] * len(after),
            out_specs=pl.BlockSpec((None, tr, c), lambda i, pi: (pi[0], i, 0))),
        out_shape=jax.ShapeDtypeStruct((4, r, c), shard.dtype),
        compiler_params=_params(("parallel",)),
    )(place_idx, shard, *after)


def _place():
    x, y, c = lax.axis_index("x"), lax.axis_index("y"), lax.axis_index("c")
    chips = [(1 - x, y), (x, 1 - y), (1 - x, 1 - y)]
    return x, y, c, chips


ANY = pl.BlockSpec(memory_space=pl.ANY)


def _exchange_halves(grads, name):
    n = len(grads)

    def body(*refs):
        ins, outs = refs[:n], refs[n:2 * n]
        send_sems, recv_sems = refs[2 * n:]
        x, y, c, _ = _place()

        def copy(a):
            h = ins[a].shape[1] // 2
            return pltpu.make_async_remote_copy(
                src_ref=ins[a].at[:, pl.ds((1 - c) * h, h), :], dst_ref=outs[a],
                send_sem=send_sems.at[a], recv_sem=recv_sems.at[a], device_id=(x, y, 1 - c), device_id_type=MESH)

        for a in range(n):
            copy(a).start()
        for a in range(n):
            copy(a).wait_recv()
        for a in range(n):
            copy(a).wait_send()

    return pl.pallas_call(
        body, name=name,
        in_specs=[ANY] * n, out_specs=tuple([ANY] * n),
        out_shape=tuple(jax.ShapeDtypeStruct((4, g.shape[1] // 2, g.shape[2]), g.dtype) for g in grads),
        scratch_shapes=[pltpu.SemaphoreType.DMA((n,)), pltpu.SemaphoreType.DMA((n,))],
    )(*grads)


HBM = pl.BlockSpec(memory_space=pltpu.HBM)
SEM = pl.BlockSpec(memory_space=pltpu.SEMAPHORE)
EFFECT = pltpu.SideEffectType.DATAFLOW_SIDE_EFFECTING


def _in_hbm(a):
    return pltpu.with_memory_space_constraint(a, pltpu.HBM)


def _split_copy_calls(name, srcs, lands, n_copies, make_copies):
    ns, nl = len(srcs), len(lands)
    nb = ns + nl

    def start(after=()):
        n_after = len(after)

        def body(*refs):
            outs = refs[nb + n_after:]
            copies = make_copies(refs[:ns], refs[ns:nb], outs[0], outs[1])
            for cp in copies:
                cp.start()
            token = refs[-1]
            token[...] = jnp.zeros_like(token)

        bufs = [_in_hbm(a) for a in list(srcs) + list(lands)]
        out = pl.pallas_call(
            body, name=name + "_start",
            out_shape=(pltpu.SemaphoreType.DMA((n_copies,)), pltpu.SemaphoreType.DMA((n_copies,)),
                       *[pltpu.HBM(a.shape, a.dtype) for a in bufs], jax.ShapeDtypeStruct((8, 128), F32)),
            in_specs=[HBM] * nb + [pl.BlockSpec(memory_space=pl.ANY)] * n_after,
            out_specs=(SEM, SEM, *[HBM] * nb, pl.BlockSpec(memory_space=pltpu.VMEM)),
            input_output_aliases={i: 2 + i for i in range(nb)},
            compiler_params=pltpu.CompilerParams(has_side_effects=EFFECT),
        )(*bufs, *after)
        return dict(send=out[0], recv=out[1], bufs=list(out[2:2 + nb]), token=out[-1])

    def wait(state, after):
        def body(*refs):
            copies = make_copies(refs[:ns], refs[ns:nb], refs[nb], refs[nb + 1])
            for cp in copies:
                cp.wait_send()
            for cp in copies:
                cp.wait_recv()

        bufs = state["bufs"]
        out = pl.pallas_call(
            body, name=name + "_wait",
            out_shape=tuple(pltpu.HBM(a.shape, a.dtype) for a in bufs),
            in_specs=[HBM] * nb + [SEM, SEM] + [pl.BlockSpec(memory_space=pl.ANY)] * len(after),
            out_specs=tuple([HBM] * nb),
            input_output_aliases={i: i for i in range(nb)},
            compiler_params=pltpu.CompilerParams(has_side_effects=EFFECT),
        )(*bufs, state["send"], state["recv"], *after)
        return list(out[:ns]), list(out[ns:])

    return start, wait


def _scatter_chips_split(name, parts):
    n = len(parts)
    lands = [lax.empty((3,) + p.shape[1:], p.dtype) for p in parts]

    def make_copies(srcs, lnds, send_sems, recv_sems):
        _, _, c, chips = _place()
        return [pltpu.make_async_remote_copy(
            src_ref=srcs[a].at[2 * px + py], dst_ref=lnds[a].at[j], send_sem=send_sems.at[a * 3 + j],
            recv_sem=recv_sems.at[a * 3 + j], device_id=(px, py, c), device_id_type=MESH)
            for a in range(n) for j, (px, py) in enumerate(chips)]

    return _split_copy_calls(name, parts, lands, 3 * n, make_copies)


def _exchange_halves_split(name, grads):
    n = len(grads)
    lands = [lax.empty((4, g.shape[1] // 2, g.shape[2]), g.dtype) for g in grads]

    def make_copies(srcs, lnds, send_sems, recv_sems):
        x, y, c, _ = _place()
        out = []
        for a in range(n):
            h = srcs[a].shape[1] // 2
            out.append(pltpu.make_async_remote_copy(
                src_ref=srcs[a].at[:, pl.ds((1 - c) * h, h), :], dst_ref=lnds[a], send_sem=send_sems.at[a],
                recv_sem=recv_sems.at[a], device_id=(x, y, 1 - c), device_id_type=MESH))
        return out

    return _split_copy_calls(name, grads, lands, n, make_copies)


def _gather_chips_split(name, shards, lands):
    n = len(shards)

    def make_copies(srcs, lnds, send_sems, recv_sems):
        x, y, c, chips = _place()
        out = []
        for a in range(n):
            h = srcs[a].shape[0] // 2
            for j, (px, py) in enumerate(chips):
                out.append(pltpu.make_async_remote_copy(
                    src_ref=srcs[a].at[pl.ds(c * h, h), :], dst_ref=lnds[a].at[2 * x + y, pl.ds(c * h, h), :],
                    send_sem=send_sems.at[a * 3 + j], recv_sem=recv_sems.at[a * 3 + j],
                    device_id=(px, py, c), device_id_type=MESH))
        return out

    return _split_copy_calls(name, shards, lands, 3 * n, make_copies)


def _gather_finish(gathered, name):
    n = len(gathered)

    def body(*refs):
        outs = refs[n:2 * n]
        send_sems, recv_sems = refs[2 * n:]
        x, y, c, chips = _place()

        def copy(a, j, chip_idx, which):
            h = outs[a].shape[1] // 2
            rows = outs[a].at[chip_idx, pl.ds(which * h, h), :]
            return pltpu.make_async_remote_copy(
                src_ref=rows, dst_ref=rows, send_sem=send_sems.at[a * 3 + j], recv_sem=recv_sems.at[a * 3 + j],
                device_id=(x, y, 1 - c), device_id_type=MESH)

        for a in range(n):
            for j, (px, py) in enumerate(chips):
                copy(a, j, 2 * px + py, c).start()
        for a in range(n):
            for j, (px, py) in enumerate(chips):
                copy(a, j, 2 * px + py, 1 - c).wait_recv()
        for a in range(n):
            for j, (px, py) in enumerate(chips):
                copy(a, j, 2 * px + py, c).wait_send()

    return pl.pallas_call(
        body, name=name,
        in_specs=[ANY] * n, out_specs=tuple([ANY] * n),
        out_shape=tuple(jax.ShapeDtypeStruct(g.shape, g.dtype) for g in gathered),
        input_output_aliases={i: i for i in range(n)},
        scratch_shapes=[pltpu.SemaphoreType.DMA((3 * n,)), pltpu.SemaphoreType.DMA((3 * n,))],
    )(*gathered)


def _gather_forward_split(name, gathered):
    n = len(gathered)

    def make_copies(srcs, lnds, send_sems, recv_sems):
        x, y, c, chips = _place()
        out = []
        for a in range(n):
            h = lnds[a].shape[1] // 2
            for j, (px, py) in enumerate(chips):
                rows = lnds[a].at[2 * px + py, pl.ds(c * h, h), :]
                out.append(pltpu.make_async_remote_copy(
                    src_ref=rows, dst_ref=rows, send_sem=send_sems.at[a * 3 + j], recv_sem=recv_sems.at[a * 3 + j],
                    device_id=(x, y, 1 - c), device_id_type=MESH))
        return out

    return _split_copy_calls(name, [], gathered, 3 * n, make_copies)


def _join_halves(fulls):
    n = len(fulls)

    def body(*refs):
        outs = refs[n:2 * n]
        send_sems, recv_sems = refs[2 * n:]
        x, y, c, _ = _place()

        def copy(a, which):
            h = outs[a].shape[0] // 2
            rows = outs[a].at[pl.ds(which * h, h), :]
            return pltpu.make_async_remote_copy(
                src_ref=rows, dst_ref=rows, send_sem=send_sems.at[a], recv_sem=recv_sems.at[a],
                device_id=(x, y, 1 - c), device_id_type=MESH)

        for a in range(n):
            copy(a, c).start()
        for a in range(n):
            copy(a, 1 - c).wait_recv()
        for a in range(n):
            copy(a, c).wait_send()

    return pl.pallas_call(
        body, name="rs_join_halves",
        in_specs=[ANY] * n, out_specs=tuple([ANY] * n),
        out_shape=tuple(jax.ShapeDtypeStruct(p.shape, p.dtype) for p in fulls),
        input_output_aliases={i: i for i in range(n)},
        scratch_shapes=[pltpu.SemaphoreType.DMA((n,)), pltpu.SemaphoreType.DMA((n,))],
    )(*fulls)


def _all_gather_small_split(sm):
    r, w = sm.shape

    def make_copies(srcs, lnds, send_sems, recv_sems):
        x, y, c, _ = _place()
        me = 4 * x + 2 * y + c
        rel = [(dx, dy, dc) for dx in (0, 1) for dy in (0, 1) for dc in (0, 1)][1:]
        return [pltpu.make_async_remote_copy(
            src_ref=srcs[0], dst_ref=lnds[0].at[me], send_sem=send_sems.at[k], recv_sem=recv_sems.at[k],
            device_id=(1 - x if dx else x, 1 - y if dy else y, 1 - c if dc else c), device_id_type=MESH)
            for k, (dx, dy, dc) in enumerate(rel)]

    return _split_copy_calls("all_gather_small", [sm], [lax.empty((8, r, w), sm.dtype)], 7, make_copies)


def _sum_devices(sm, gathered, me_idx):
    def body(me_ref, sm_ref, g_ref, o_ref):
        own = sm_ref[...]
        acc = jnp.where(me_ref[0] == 0, own, g_ref[0])
        for d in range(1, 8):
            acc = acc + jnp.where(me_ref[0] == d, own, g_ref[d])
        o_ref[...] = acc

    vm = pl.BlockSpec(memory_space=pltpu.VMEM)
    return pl.pallas_call(
        body, name="sum_devices", in_specs=[pl.BlockSpec(memory_space=pltpu.SMEM), vm, vm], out_specs=vm,
        out_shape=jax.ShapeDtypeStruct(sm.shape, F32),
    )(me_idx, sm, gathered)


def _local_step(x3, mem3, pos2, target3, small, comm):
    bsz, seq, d = x3.shape
    mlen = mem3.shape[1]
    t = bsz * seq
    tok = comm.begin()
    x = x3.reshape(t, d)
    mem = mem3.reshape(bsz * mlen, d)
    target = target3.reshape(t, d)
    rope = _rope_table(pos2.reshape(t, 1), after=tok)
    qg_t = jnp.tile(small["sw_q_norm_g"], (1, SW_HEADS))
    kg_t = jnp.tile(small["sw_k_norm_g"], (1, SW_KV_HEADS))

    hn1 = _rms_fwd(x, small["norm1_g"], name="rms1_fwd", after=tok)
    w = comm.first((hn1, rope))
    w_in_t = w["w_in_t"]
    w_sw_t = w_in_t[HG_COLS:]
    proj_hg = _mm(hn1, w_in_t, NT, t, HG_COLS, d, name="proj_hg", tk=d, after=(w.get("token"),))[0]
    proj_sw = _mm(hn1, w_sw_t, NT, t, SW_COLS, d, name="proj_sw", tk=d)[0]
    y_mix, o_hg, states, hg_scores = _hg_fwd(proj_hg, small["hg_lower_bounds"], small["hg_norm_g"], bsz, seq, y_width=1024)
    y_mix = _sw_fwd(proj_sw, rope, qg_t, kg_t, small["sw_sinks"], y_mix, bsz, seq)
    w = comm.rest(y_mix)
    h1, hn2 = _mm(y_mix, w["w_out"], NN, t, d, 1024, name="out_proj", tk=1024, extras=(x,), rows=(small["norm2_g"],),
                  epilogue=_residual_rms, out_dtypes=(F32, _MXU_DTYPE), after=(w.get("token"),))
    mn = _rms_fwd(mem, small["mem_norm_g"], name="rms_mem_fwd")
    qx = _mm(hn2, w["wq"], NN, t, 512, d, name="xa_q", tk=d)[0]
    kvx = _mm(mn, w["wkv"], NN, bsz * mlen, 1024, d, name="xa_kv", tk=d)[0]
    ox = _xa_fwd(qx, kvx, small["xa_q_norm_g"], small["xa_k_norm_g"], bsz, seq, mlen)
    h2, hn3 = _mm(ox, w["wo"], NN, t, d, 512, name="xa_o", tk=512, extras=(h1,), rows=(small["norm3_g"],),
                  epilogue=_residual_rms, out_dtypes=(F32, _MXU_DTYPE))
    w = {**w, **comm.mlp(hn3)}
    ff = w["down"].shape[0]
    ffs = ff // 4

    def relu_sq(acc):
        a = jnp.maximum(acc, 0.0)
        return a, a * a

    act, act2 = _mm(hn3, w["up"], NN, t, ff, d, name="mlp_up", tm=2048, tn=ffs, tk=d,
                    b_spec=pl.BlockSpec((None, d, ffs), lambda i, j, kk: (j, 0, 0)),
                    epilogue=relu_sq, out_dtypes=(_MXU_DTYPE, _MXU_DTYPE))
    inv_d = 1.0 / d

    def loss_cotangent(acc, res, tgt):
        diff = acc + res - tgt
        v = diff * inv_d
        return v, v, jnp.sum(diff * diff, axis=0, keepdims=True)

    dy, dy_mx, sq_row = _mm(act2, w["down"], NN, t, d, ff, name="mlp_down", tk=2048, extras=(h2, target),
                            epilogue=loss_cotangent, out_dtypes=(F32, _MXU_DTYPE), row_sums=1)
    loss_row = _loss_finish(sq_row, d)

    dz = _mm(dy_mx, w["down"], NT, t, ff, d, name="d_act", tm=2048, tk=d, extras=(act,),
             epilogue=lambda acc, a: (acc * (2.0 * a.astype(F32)),), out_dtypes=(_MXU_DTYPE,))[0]
    g_down = _mm(act2, dy_mx, TN, ff, d, t, name="g_down", tk=t)[0]
    g_up = _mm(hn3, dz, TN, d, ff, t, name="g_up", tn=ffs, tk=t,
               out_shape=(jax.ShapeDtypeStruct((4, d, ffs), F32),),
               out_spec=(pl.BlockSpec((None, min(1024, d), ffs), lambda i, j, kk: (j, i, 0)),))[0]
    tok = comm.grads("mlp", dict(up=g_up, down=g_down))
    dh2, dh2_mx, g_norm3 = _mm(dz, w["up"], NT, t, d, ff, name="d_hn3", tk=ffs, after=tok,
                               b_spec=pl.BlockSpec((None, min(1024, d), ffs), lambda i, j, kk: (kk, j, 0)),
                               extras=(h2, dy), rows=(small["norm3_g"],), epilogue=_rms_bwd_residual,
                               out_dtypes=(F32, _MXU_DTYPE), row_sums=1)
    d_ox = _mm(dh2_mx, w["wo"], NT, t, 512, d, name="d_ox", tk=d)[0]
    g_wo = _mm(ox, dh2_mx, TN, 512, d, t, name="g_wo", tk=t)[0]
    d_qx, d_kvx, g_xq, g_xk = _xa_bwd(qx, kvx, small["xa_q_norm_g"], small["xa_k_norm_g"], d_ox, bsz, seq, mlen)
    g_wq = _mm(hn2, d_qx, TN, d, 512, t, name="g_wq")[0]
    g_wkv = _mm(mn, d_kvx, TN, d, 1024, bsz * mlen, name="g_wkv")[0]
    dh1, dh1_mx, g_norm2 = _mm(d_qx, w["wq"], NT, t, d, 512, name="d_hn2", tk=512, extras=(h1, dh2),
                               rows=(small["norm2_g"],), epilogue=_rms_bwd_residual, out_dtypes=(F32, _MXU_DTYPE),
                               row_sums=1)
    dmn = _mm(d_kvx, w["wkv"], NT, bsz * mlen, d, 1024, name="d_mn", tk=1024)[0]
    g_memn = _rms_gain_grad(mem, small["mem_norm_g"], dmn, name="rms_mem_bwd")
    g_wout = _mm(y_mix, dh1_mx, TN, 1024, d, t, name="g_wout", tk=2048)[0]
    tok = comm.grads("mid", dict(w_out=g_wout, wq=g_wq, wkv=g_wkv, wo=g_wo))
    d_mix = _mm(dh1_mx, w["w_out"], NT, t, 1024, d, name="d_mix", tk=d, after=tok)[0]
    dproj_sw, g_swq, g_swk, g_sinks = _sw_bwd(proj_sw, rope, qg_t, kg_t, small["sw_sinks"], y_mix, d_mix, bsz, seq)
    tok = comm.poll(dproj_sw)
    dproj_hg, g_lb, g_hgn = _hg_bwd(proj_hg, small["hg_lower_bounds"], small["hg_norm_g"], o_hg, states, hg_scores, d_mix, bsz, seq,
                                    after=tok)
    in_rows = HG_COLS + SW_COLS
    sw_tile = 256
    g_in_t = _mm(dproj_hg, hn1, TN, HG_COLS, d, t, name="g_in_hg", tk=t,
                 out_shape=(jax.ShapeDtypeStruct((in_rows, d), F32),),
                 out_spec=(pl.BlockSpec((1024, min(1024, d)), lambda i, j, kk: (i, j)),))[0]
    g_in_t = _mm(dproj_sw, hn1, TN, SW_COLS, d, t, name="g_in_sw", tm=sw_tile, into=g_in_t,
                 out_shape=(jax.ShapeDtypeStruct((in_rows, d), F32),),
                 out_spec=(pl.BlockSpec((sw_tile, min(1024, d)), lambda i, j, kk: (HG_COLS // sw_tile + i, j)),))[0]
    tok = comm.grads("in", dict(w_in_t=g_in_t))
    grad_x, g_norm1 = _mm(dproj_hg, w_in_t, NN, t, d, HG_COLS, name="d_hn1", tk=HG_COLS, second=(dproj_sw, w_sw_t),
                          extras=(x, dh1), rows=(small["norm1_g"],), row_sums=1, after=tok,
                          epilogue=lambda acc, xv, dres, g: _rms_bwd_residual(acc, xv, dres, g)[1:])

    g_small = dict(norm1_g=g_norm1, hg_lower_bounds=g_lb, hg_norm_g=g_hgn, sw_q_norm_g=g_swq, sw_k_norm_g=g_swk,
                   sw_sinks=g_sinks[:, 0:SW_HEADS], norm2_g=g_norm2, mem_norm_g=g_memn, xa_q_norm_g=g_xq,
                   xa_k_norm_g=g_xk, norm3_g=g_norm3)
    return loss_row, grad_x.reshape(bsz, seq, d), g_small


SMALL_NAMES = ("norm1_g", "hg_lower_bounds", "hg_norm_g", "sw_q_norm_g", "sw_k_norm_g", "sw_sinks", "norm2_g",
               "mem_norm_g", "xa_q_norm_g", "xa_k_norm_g", "norm3_g")
BIG_NAMES = ("w_in", "w_out", "xa_wq", "xa_wkv", "xa_wo", "mlp_up", "mlp_down")
WEIGHT_ORDER = ("norm1_g", "w_in", "hg_lower_bounds", "hg_norm_g", "sw_q_norm_g", "sw_k_norm_g", "sw_sinks", "w_out",
                "norm2_g", "mem_norm_g", "xa_wq", "xa_wkv", "xa_q_norm_g", "xa_k_norm_g", "xa_wo", "norm3_g",
                "mlp_up", "mlp_down")


def _pack_rows(vals, width):
    starts, at = [], 0
    for v in vals:
        starts.append(at)
        at += v.shape[0]
    total = at + (-at) % 8
    out = None
    for v, s in zip(vals, starts):
        placed = jnp.pad(v, ((s, total - s - v.shape[0]), (0, width - v.shape[1])))
        out = placed if out is None else out + placed
    return out, starts


class _MeshWeights:
    LATE = ("w_out", "xa_wq", "xa_wkv", "xa_wo", "mlp_up", "mlp_down")

    def __init__(self, shards, d, ff):
        self.shards, self.d, self.ff = shards, d, ff
        self.c_idx = lax.axis_index("c").astype(jnp.int32).reshape(1)
        chip = (2 * lax.axis_index("x") + lax.axis_index("y")).astype(jnp.int32)
        self.place_idx = jnp.stack([chip, lax.axis_index("c").astype(jnp.int32)])
        self.pending = []
        self.exchanging = None

    def begin(self):
        shard = self.shards["w_in"]
        start, self.in_wait = _gather_chips_split(
            "gather_in", [shard], [_place_shard(shard, self.place_idx, name="place_w_in")])
        self.in_state = start()
        tok = (self.in_state["token"],)
        self.placed = list(_place_shards([self.shards[n] for n in self.LATE], self.place_idx, name="place_late",
                                         after=tok))
        return tok

    def first(self, after):
        _, lands = self.in_wait(self.in_state, (*after, *self.placed))
        (g_in,) = _gather_finish(lands, "gather_in_finish")
        start, self.late_wait = _gather_chips_split("gather_late", [self.shards[n] for n in self.LATE], self.placed)
        self.late_state = start(after=(g_in,))
        return dict(w_in_t=g_in.reshape(-1, self.d), token=self.late_state["token"])

    def rest(self, after):
        _, lands = self.late_wait(self.late_state, (after,))
        g_out, g_q, g_kv, g_o = _gather_finish(lands[:4], "gather_late_finish")
        start, self.mlp_wait = _gather_forward_split("gather_mlp_forward", lands[4:])
        self.mlp_state = start(after=(g_out,))
        d = self.d
        return dict(w_out=g_out.reshape(-1, d), wq=g_q.reshape(d, -1), wkv=g_kv.reshape(d, -1),
                    wo=jnp.concatenate([g_o[k] for k in range(4)], axis=1), token=self.mlp_state["token"])

    def mlp(self, after):
        _, (g_up, g_dn) = self.mlp_wait(self.mlp_state, (after,))
        return dict(up=g_up, down=g_dn.reshape(self.ff, self.d))

    def _scatter(self, tag, names, arrays, recv):
        parts = list(_add_halves(arrays, recv, self.c_idx, name="rs_add_halves_" + tag))
        start, wait = _scatter_chips_split("rs_scatter_" + tag, parts)
        state = start()
        self.pending.append((names, wait, state))
        return state["token"]

    def _advance(self, after):
        if self.exchanging is None:
            return ()
        tag, names, wait, state = self.exchanging
        self.exchanging = None
        arrays, recv = wait(state, (after,))
        return (self._scatter(tag, names, arrays, recv),)

    def poll(self, after):
        return self._advance(after)

    def grads(self, tag, g):
        d, ff = self.d, self.ff
        if tag == "mlp":
            names, arrays = ("mlp_up", "mlp_down"), [g["up"], g["down"].reshape(4, ff // 4, d)]
        elif tag == "mid":
            names = ("w_out", "xa_wq", "xa_wkv", "xa_wo")
            ds = d // 4
            g_wo = jnp.stack([g["wo"][:, ds * k:ds * (k + 1)] for k in range(4)])
            arrays = [g["w_out"].reshape(4, -1, d), g["wq"].reshape(4, d // 4, -1), g["wkv"].reshape(4, d // 4, -1), g_wo]
        else:
            names, arrays = ("w_in",), [g["w_in_t"].reshape(4, -1, d)]
        toks = self._advance(arrays[0])
        if tag == "in":
            return toks + (self._scatter(tag, names, arrays, _exchange_halves(arrays, "rs_exchange_" + tag)),)
        start, wait = _exchange_halves_split("rs_exchange_" + tag, arrays)
        state = start()
        self.exchanging = (tag, names, wait, state)
        return toks + (state["token"],)

    def finish(self, after):
        halves, tok = {}, ()
        for names, wait, state in self.pending:
            srcs, lands = wait(state, tuple(after) + tok)
            fulls = _add_chips(srcs, lands, self.place_idx, name="rs_add_chips_" + names[0], after=tok)
            tok = (fulls[0],)
            halves.update(zip(names, fulls))
        return dict(zip(BIG_NAMES, _join_halves([halves[n] for n in BIG_NAMES])))


def kernel(x, mem, positions, norm1_g, w_in, hg_lower_bounds, hg_norm_g, sw_q_norm_g, sw_k_norm_g, sw_sinks, w_out, norm2_g, mem_norm_g, xa_wq, xa_wkv, xa_q_norm_g, xa_k_norm_g, xa_wo, norm3_g, mlp_up, mlp_down, loss_target, m_norm1_g, m_w_in, m_hg_lower_bounds, m_hg_norm_g, m_sw_q_norm_g, m_sw_k_norm_g, m_sw_sinks, m_w_out, m_norm2_g, m_mem_norm_g, m_xa_wq, m_xa_wkv, m_xa_q_norm_g, m_xa_k_norm_g, m_xa_wo, m_norm3_g, m_mlp_up, m_mlp_down, v_norm1_g, v_w_in, v_hg_lower_bounds, v_hg_norm_g, v_sw_q_norm_g, v_sw_k_norm_g, v_sw_sinks, v_w_out, v_norm2_g, v_mem_norm_g, v_xa_wq, v_xa_wkv, v_xa_q_norm_g, v_xa_k_norm_g, v_xa_wo, v_norm3_g, v_mlp_up, v_mlp_down):
    given = dict(locals())
    weights = {n: given[n] for n in WEIGHT_ORDER}
    moms = {n: given["m_" + n] for n in WEIGHT_ORDER}
    vars_ = {n: given["v_" + n] for n in WEIGHT_ORDER}
    d = x.shape[-1]
    ff = mlp_down.shape[1] * 4
    small = {n: weights[n] for n in SMALL_NAMES}

    def plain(n, a):
        return jnp.swapaxes(a[0], 0, 1) if n == "w_in" else a[0]

    comm = _MeshWeights({n: plain(n, weights[n]).astype(_MXU_DTYPE) for n in BIG_NAMES}, d, ff)
    loss_row, grad_x, g_small = _local_step(x, mem, positions, loss_target, small, comm)
    packed, starts = _pack_rows([g_small[n] for n in SMALL_NAMES] + [loss_row], 1024)
    start, wait = _all_gather_small_split(packed)
    state = start()
    big_grads = comm.finish((grad_x, state["token"]))
    (own,), (gathered,) = wait(state, (big_grads[BIG_NAMES[0]],))
    device = (4 * lax.axis_index("x") + 2 * lax.axis_index("y") + lax.axis_index("c")).astype(jnp.int32).reshape(1)
    summed = _sum_devices(own, gathered, device)
    small_grads = {}
    for n, s in zip(SMALL_NAMES, starts):
        r, c = weights[n].shape
        small_grads[n] = summed[s:s + r, 0:c]
    loss = summed[starts[-1], 0]

    grads, deltas, new_m, new_v = {}, {}, {}, {}
    big_out = _adamw_big([plain(n, weights[n]) for n in BIG_NAMES], [big_grads[n] for n in BIG_NAMES],
                         [plain(n, moms[n]) for n in BIG_NAMES], [plain(n, vars_[n]) for n in BIG_NAMES])
    for n, outs in zip(BIG_NAMES, big_out):
        grads[n], deltas[n], new_m[n], new_v[n] = ((jnp.swapaxes(a, 0, 1) if n == "w_in" else a)[None] for a in outs)
    sm_out = _adamw_small([weights[n] for n in SMALL_NAMES], [small_grads[n] for n in SMALL_NAMES],
                          [moms[n] for n in SMALL_NAMES], [vars_[n] for n in SMALL_NAMES])
    ns = len(SMALL_NAMES)
    for i, n in enumerate(SMALL_NAMES):
        grads[n], deltas[n], new_m[n], new_v[n] = small_grads[n], sm_out[i], sm_out[ns + i], sm_out[2 * ns + i]

    return (loss, grad_x, *[grads[n] for n in WEIGHT_ORDER], *[deltas[n] for n in WEIGHT_ORDER],
            *[new_m[n] for n in WEIGHT_ORDER], *[new_v[n] for n in WEIGHT_ORDER])
```

```python
import numpy as np
import jax
import jax.numpy as jnp
from jax import lax
from jax.experimental import pallas as pl
from jax.experimental.pallas import tpu as pltpu

F32 = jnp.float32
_MXU_DTYPE = jnp.bfloat16

EPS = 1e-6
HG_HEADS = 4
HG_D = 128
HG_CHUNK = 64
HG_TILE = 512
HG_LEVELS = (32, 16, 8, 4, 2, 1)
SW_HEADS = 8
SW_KV_HEADS = 2
SW_GROUP = SW_HEADS // SW_KV_HEADS
SW_HD = 64
SW_BLOCK = 128
ROPE_THETA = 500000.0
ROT_DIM = SW_HD // 4
XA_HEADS = 4
XA_HD = 128
HG_COLS = 4 * HG_HEADS * HG_D
SW_COLS = (SW_HEADS + 2 * SW_KV_HEADS) * SW_HD

ADAM_LR = 0.001
ADAM_B1 = 0.9
ADAM_B2 = 0.999
ADAM_EPS = 1e-08
ADAM_WD = 0.01
ADAM_STEP = 10

VMEM_LIMIT = 56 * 1024 * 1024
MESH = pl.DeviceIdType.MESH

NN = ((1,), (0,))
NT = ((1,), (1,))
TN = ((0,), (0,))


def _mx(v):
    return v.astype(_MXU_DTYPE)


def _dot(a, b, dims=NN):
    return lax.dot_general(_mx(a), _mx(b), (dims, ((), ())), preferred_element_type=F32)


def _split_dot(a, v, dims, parts):
    acc = None
    rest = v
    for p in range(parts):
        piece = _mx(rest)
        term = lax.dot_general(a, piece, (dims, ((), ())), preferred_element_type=F32)
        acc = term if acc is None else acc + term
        if p + 1 < parts:
            rest = rest - piece.astype(F32)
    return acc


def _params(sem):
    return pltpu.CompilerParams(dimension_semantics=sem, vmem_limit_bytes=VMEM_LIMIT)


def _mm(a, b, mode, m, n, k, *, name, tm=1024, tn=1024, tk=1024, a_spec=None, b_spec=None, extras=(), rows=(),
        epilogue=None, out_dtypes=(F32,), row_sums=0, out_shape=None, out_spec=None, after=(), into=None,
        second=None):
    after = tuple(t for t in after if t is not None) + (() if into is None else (into,))
    tm, tn, tk = min(tm, m), min(tn, n), min(tk, k)
    assert m % tm == 0 and n % tn == 0 and k % tk == 0, (name, m, n, k, tm, tn, tk)
    gi, gj, gk = m // tm, n // tn, k // tk
    assert row_sums == 0 or gj == 1, name
    if a_spec is None:
        a_spec = (pl.BlockSpec((tk, tm), lambda i, j, kk: (kk, i)) if mode == TN
                  else pl.BlockSpec((tm, tk), lambda i, j, kk: (i, kk)))
    if b_spec is None:
        b_spec = (pl.BlockSpec((tn, tk), lambda i, j, kk: (j, kk)) if mode == NT
                  else pl.BlockSpec((tk, tn), lambda i, j, kk: (kk, j)))
    mn_spec = pl.BlockSpec((tm, tn), lambda i, j, kk: (i, j))
    if epilogue is None:
        epilogue = lambda acc: (acc,)
    row_spec = pl.BlockSpec((1, tn), lambda i, j, kk: (0, j))
    n_ex, n_out = len(extras) + len(rows), len(out_dtypes)
    if out_shape is None:
        out_shape = tuple(jax.ShapeDtypeStruct((m, n), d) for d in out_dtypes)
        out_spec = tuple(mn_spec for _ in out_dtypes)
    out_shape = tuple(out_shape) + tuple(jax.ShapeDtypeStruct((1, n), F32) for _ in range(row_sums))
    out_spec = tuple(out_spec) + tuple(row_spec for _ in range(row_sums))

    n_after = len(after)
    lead = 2 if second is None else 4
    assert second is None or (mode == NN and gk == 1), name
    second_specs = [] if second is None else [pl.BlockSpec((tm, second[0].shape[1]), lambda i, j, kk: (i, 0)),
                                                pl.BlockSpec((second[1].shape[0], tn), lambda i, j, kk: (0, j))]

    def body(*refs):
        a_ref, b_ref = refs[0], refs[1]
        ex = refs[lead:lead + n_ex]
        outs = refs[lead + n_ex + n_after:lead + n_ex + n_after + n_out + row_sums]
        first_row_tile = pl.program_id(0) == 0

        def finish(acc):
            res = epilogue(acc, *[e[...] for e in ex])
            for o, r in zip(outs[:n_out], res[:n_out]):
                o[...] = r.astype(o.dtype)
            if row_sums:
                @pl.when(first_row_tile)
                def _():
                    for o in outs[n_out:]:
                        o[...] = jnp.zeros_like(o)

                for o, r in zip(outs[n_out:], res[n_out:]):
                    o[...] += r

        if gk == 1 and second is not None:
            finish(_dot(a_ref[...], b_ref[...], mode) + _dot(refs[2][...], refs[3][...], NN))
        elif gk == 1:
            finish(_dot(a_ref[...], b_ref[...], mode))
        else:
            acc_ref = refs[-1]
            kk = pl.program_id(2)

            @pl.when(kk == 0)
            def _():
                acc_ref[...] = jnp.zeros_like(acc_ref)

            acc_ref[...] += _dot(a_ref[...], b_ref[...], mode)

            @pl.when(kk == gk - 1)
            def _():
                finish(acc_ref[...])

    return pl.pallas_call(
        body, name=name, grid=(gi, gj, gk),
        in_specs=([a_spec, b_spec] + second_specs + [mn_spec] * len(extras) + [row_spec] * len(rows)
                  + [pl.BlockSpec(memory_space=pl.ANY)] * n_after),
        out_specs=out_spec, out_shape=out_shape,
        input_output_aliases={} if into is None else {lead + n_ex + n_after - 1: 0},
        scratch_shapes=[pltpu.VMEM((tm, tn), F32)] if gk > 1 else [],
        compiler_params=_params(("arbitrary" if row_sums else "parallel", "parallel", "arbitrary")),
    )(a, b, *(second or ()), *extras, *rows, *after)


def _rms_rows(xv, g):
    return xv * lax.rsqrt(jnp.mean(xv * xv, axis=1, keepdims=True) + EPS) * g


def _rms_rows_bwd(xv, g, dyv):
    r = lax.rsqrt(jnp.mean(xv * xv, axis=1, keepdims=True) + EPS)
    u = dyv * g
    return (r * u - xv * (r * r * r) * jnp.mean(u * xv, axis=1, keepdims=True),
            jnp.sum(dyv * xv * r, axis=0, keepdims=True))


def _residual_rms(acc, res, g):
    h = acc + res
    return h, _rms_rows(h, g)


def _rms_bwd_residual(dhn, xv, dres, g):
    dx, dg = _rms_rows_bwd(xv, g, dhn)
    dx = dx + dres
    return dx, dx, dg


def _rms_fwd(x, g, *, name, tm=512, after=()):
    t, d = x.shape
    tm = min(tm, t)
    after = tuple(a for a in after if a is not None)

    def body(x_ref, g_ref, *rest):
        rest[-1][...] = _rms_rows(x_ref[...], g_ref[...]).astype(rest[-1].dtype)

    return pl.pallas_call(
        body, name=name, grid=(t // tm,),
        in_specs=[pl.BlockSpec((tm, d), lambda i: (i, 0)), pl.BlockSpec((1, d), lambda i: (0, 0))]
        + [pl.BlockSpec(memory_space=pl.ANY)] * len(after),
        out_specs=pl.BlockSpec((tm, d), lambda i: (i, 0)),
        out_shape=jax.ShapeDtypeStruct((t, d), _MXU_DTYPE),
        compiler_params=_params(("parallel",)),
    )(x, g, *after)


def _rms_gain_grad(x, g, dy, *, name, tm=512):
    t, d = x.shape
    tm = min(tm, t)

    def body(x_ref, g_ref, dy_ref, dg_ref):
        @pl.when(pl.program_id(0) == 0)
        def _():
            dg_ref[...] = jnp.zeros_like(dg_ref)

        dg_ref[...] += _rms_rows_bwd(x_ref[...], g_ref[...], dy_ref[...])[1]

    row = pl.BlockSpec((tm, d), lambda i: (i, 0))
    vec = pl.BlockSpec((1, d), lambda i: (0, 0))
    return pl.pallas_call(
        body, name=name, grid=(t // tm,), in_specs=[row, vec, row], out_specs=vec,
        out_shape=jax.ShapeDtypeStruct((1, d), F32), compiler_params=_params(("arbitrary",)),
    )(x, g, dy)


def _hg_constants():
    c = HG_CHUNK
    t = np.arange(c)
    sums = [t[None, :] <= t[:, None]]
    masks = []
    for m in HG_LEVELS:
        base = (t // (2 * m)) * (2 * m)
        mid = base + m - 1
        second = (t - base) >= m
        upper = (t[None, :] > mid[:, None]) & (t[None, :] <= t[:, None])
        lower = (t[None, :] > t[:, None]) & (t[None, :] <= mid[:, None])
        sums.append(np.where(second[:, None], upper, lower))
        masks.append(second[:, None] & (~second)[None, :] & (base[:, None] == base[None, :]))
    return (np.concatenate(sums, axis=0).astype(np.float32), np.stack(masks).astype(np.float32))


HG_HEAD_LANES = tuple(slice(HG_D * h, HG_D * (h + 1)) for h in range(HG_HEADS))


def _per_head(fn, slab):
    return jnp.concatenate([jnp.broadcast_to(fn(slab[:, hs]), (slab.shape[0], HG_D)) for hs in HG_HEAD_LANES], axis=1)


def _lane_sum(v):
    return jnp.sum(v, axis=1, keepdims=True)


def _lane_mean(v):
    return jnp.mean(v, axis=1, keepdims=True)


def _hg_gates(blk, lbp):
    w = HG_HEADS * HG_D
    q, x, v, gl = blk[:, 0:w], blk[:, w:2 * w], blk[:, 2 * w:3 * w], blk[:, 3 * w:4 * w]
    mx = jnp.max(lbp, axis=0, keepdims=True)
    e = jnp.exp(lbp - mx)
    lb = e[0:1, :] / jnp.sum(e, axis=0, keepdims=True)
    sig = jax.nn.sigmoid(x)
    f = lb + (1.0 - lb) * sig
    return q, v, gl, lb, sig, f, 1.0 - f, jnp.log(f)


def _hg_fwd(proj, lbp, ng, bsz, seq, *, y_width):
    t = proj.shape[0]
    nc = seq // HG_CHUNK
    a_np, m_np = _hg_constants()
    a_all = jnp.asarray(a_np, _MXU_DTYPE)
    masks = jnp.asarray(m_np, F32)
    nl = len(HG_LEVELS)

    ts = min(HG_TILE, seq)
    ns, nct = seq // ts, ts // HG_CHUNK
    hw = HG_HEADS * HG_D

    def body(p_ref, lb_ref, ng_ref, a_ref, m_ref, y_ref, o_ref, st_ref, sc_ref, carry):
        a_mat = a_ref[...]
        ngv = ng_ref[...]

        @pl.when(pl.program_id(0) == 0)
        def _():
            carry[...] = jnp.zeros_like(carry)

        ng4 = _tile_lanes(ngv, HG_HEADS)
        heads = range(HG_HEADS)
        exs = range(bsz)
        hl = HG_HEAD_LANES
        lbp_v = lb_ref[...]

        def chunk(c, _):
            rows = pl.ds(pl.multiple_of(c * HG_CHUNK, HG_CHUNK), HG_CHUNK)
            gates = [_hg_gates(p_ref[e, rows, :], lbp_v) for e in exs]
            q, v, gl = [g[0] for g in gates], [g[1] for g in gates], [g[2] for g in gates]
            k = [g[6] for g in gates]
            sts = [[carry[e, h] for h in heads] for e in exs]
            e_all = [_split_dot(a_mat, gates[e][7], NN, 3) for e in exs]
            b = [e_all[e][0:HG_CHUNK] for e in exs]
            qb = [q[e] * jnp.exp(b[e]) for e in exs]
            o = [[_dot(qb[e][:, hl[h]], sts[e][h], NT) for h in heads] for e in exs]
            p = [[jnp.zeros((HG_CHUNK, HG_CHUNK), F32) for _ in heads] for _ in exs]
            for li in range(nl):
                dec = [jnp.exp(e_all[e][HG_CHUNK * (li + 1):HG_CHUNK * (li + 2)]) for e in exs]
                qm, km, mk = [q[e] * dec[e] for e in exs], [k[e] * dec[e] for e in exs], m_ref[li]
                p = [[p[e][h] + mk * _dot(qm[e][:, hl[h]], km[e][:, hl[h]], NT) for h in heads] for e in exs]
            bl = [b[e][HG_CHUNK - 1:HG_CHUNK, :] for e in exs]
            kd = [k[e] * jnp.exp(bl[e] - b[e]) for e in exs]
            pv = [[_dot(p[e][h], v[e][:, hl[h]]) for h in heads] for e in exs]
            upd = [[_dot(v[e][:, hl[h]], kd[e][:, hl[h]], TN) for h in heads] for e in exs]
            for e in exs:
                o_all = (jnp.concatenate([o[e][h] + pv[e][h] for h in heads], axis=1)
                         + _per_head(_lane_sum, q[e] * k[e]) * v[e])
                r = lax.rsqrt(_per_head(_lane_mean, o_all * o_all) + EPS)
                ebl = jnp.exp(bl[e])
                for h in heads:
                    st_ref[e, h, c] = sts[e][h]
                    sc_ref[e, h, c] = _mx(p[e][h])
                    carry[e, h] = sts[e][h] * ebl[:, hl[h]] + upd[e][h]
                o_ref[e, rows, :] = o_all
                y_ref[e, rows, :] = (o_all * r * ng4) * (gl[e] * jax.nn.sigmoid(gl[e]))
            return 0

        lax.fori_loop(0, nct, chunk, 0, unroll=2)

    y3, o3, states, scores = pl.pallas_call(
        body, name="hgrn2_fwd", grid=(ns,),
        in_specs=[pl.BlockSpec((bsz, ts, HG_COLS), lambda s: (0, s, 0)),
                  pl.BlockSpec((2, hw), lambda s: (0, 0)),
                  pl.BlockSpec((1, HG_D), lambda s: (0, 0)),
                  pl.BlockSpec(a_all.shape, lambda s: (0, 0)),
                  pl.BlockSpec(masks.shape, lambda s: (0, 0, 0))],
        out_specs=(pl.BlockSpec((bsz, ts, hw), lambda s: (0, s, 0)),
                   pl.BlockSpec((bsz, ts, hw), lambda s: (0, s, 0)),
                   pl.BlockSpec((bsz, HG_HEADS, nct, HG_D, HG_D), lambda s: (0, 0, s, 0, 0)),
                   pl.BlockSpec((bsz, HG_HEADS, nct, HG_CHUNK, HG_CHUNK), lambda s: (0, 0, s, 0, 0))),
        out_shape=(jax.ShapeDtypeStruct((bsz, seq, y_width), F32),
                   jax.ShapeDtypeStruct((bsz, seq, hw), F32),
                   jax.ShapeDtypeStruct((bsz, HG_HEADS, nc, HG_D, HG_D), F32),
                   jax.ShapeDtypeStruct((bsz, HG_HEADS, nc, HG_CHUNK, HG_CHUNK), _MXU_DTYPE)),
        scratch_shapes=[pltpu.VMEM((bsz, HG_HEADS, HG_D, HG_D), F32)],
        compiler_params=_params(("arbitrary",)),
    )(proj.reshape(bsz, seq, HG_COLS), lbp, ng, a_all, masks)
    return y3.reshape(t, y_width), o3.reshape(t, hw), states, scores


def _hg_bwd(proj, lbp, ng, o_all, states, scores, dy, bsz, seq, after=()):
    after = tuple(a for a in after if a is not None)
    t = proj.shape[0]
    nc = seq // HG_CHUNK
    a_np, m_np = _hg_constants()
    a_all = jnp.asarray(a_np, _MXU_DTYPE)
    masks = jnp.asarray(m_np, F32)
    nl = len(HG_LEVELS)
    cs = HG_CHUNK

    ts = min(HG_TILE, seq)
    ns, nct = seq // ts, ts // cs
    hw = HG_HEADS * HG_D

    def body(p_ref, lb_ref, ng_ref, a_ref, m_ref, o_ref, st_ref, sc_ref, dy_ref, *rest):
        dp_ref, dlb_ref, dng_ref, dst_ref = rest[len(after):]
        a_mat = a_ref[...]
        ngv = ng_ref[...]
        ng4 = _tile_lanes(ngv, HG_HEADS)
        last_row = lax.broadcasted_iota(jnp.int32, (cs, hw), 0) == cs - 1
        first = pl.program_id(0) == 0
        heads = range(HG_HEADS)
        exs = range(bsz)
        hl = HG_HEAD_LANES
        lbp_v = lb_ref[...]

        @pl.when(first)
        def _():
            dst_ref[...] = jnp.zeros_like(dst_ref)

        def side_by_side(parts):
            return jnp.concatenate(parts, axis=1)

        def chunk(i, carry):
            dlb_acc, dng_acc = carry
            c = nct - 1 - i
            rows = pl.ds(pl.multiple_of(c * cs, cs), cs)
            gates = [_hg_gates(p_ref[e, rows, :], lbp_v) for e in exs]
            q, v, gl = [g[0] for g in gates], [g[1] for g in gates], [g[2] for g in gates]
            lb, sig, f, k = gates[0][3], [g[4] for g in gates], [g[5] for g in gates], [g[6] for g in gates]
            o = [o_ref[e, rows, :] for e in exs]
            dyv = [dy_ref[e, rows, :] for e in exs]
            sts = [[st_ref[e, h, c] for h in heads] for e in exs]
            dsts = [[dst_ref[e, h] for h in heads] for e in exs]
            e_all = [_split_dot(a_mat, gates[e][7], NN, 3) for e in exs]
            b = [e_all[e][0:cs] for e in exs]
            eb = [jnp.exp(b[e]) for e in exs]
            bl = [b[e][cs - 1:cs, :] for e in exs]
            ebl = [jnp.exp(bl[e]) for e in exs]
            ekd = [jnp.exp(bl[e] - b[e]) for e in exs]
            qb = [q[e] * eb[e] for e in exs]
            kd = [k[e] * ekd[e] for e in exs]
            do, dgl = [], []
            for e in exs:
                sg = jax.nn.sigmoid(gl[e])
                silu = gl[e] * sg
                r = lax.rsqrt(_per_head(_lane_mean, o[e] * o[e]) + EPS)
                dgl.append(dyv[e] * (o[e] * r * ng4) * (sg * (1.0 + gl[e] * (1.0 - sg))))
                u = dyv[e] * silu * ng4
                do.append(r * u - o[e] * (r * r * r) * _per_head(_lane_mean, u * o[e]))
                dng4 = jnp.sum(dyv[e] * silu * o[e] * r, axis=0, keepdims=True)
                dng_acc = dng_acc + ((dng4[:, hl[0]] + dng4[:, hl[1]]) + (dng4[:, hl[2]] + dng4[:, hl[3]]))
            es, qm, km = [], [], []
            for li in range(nl):
                dec = [jnp.exp(e_all[e][cs * (li + 1):cs * (li + 2)]) for e in exs]
                es.append(dec)
                qm.append([q[e] * dec[e] for e in exs])
                km.append([k[e] * dec[e] for e in exs])
            dp = [[_dot(do[e][:, hl[h]], v[e][:, hl[h]], NT) for h in heads] for e in exs]
            dv_p = [[_dot(sc_ref[e, h, c], do[e][:, hl[h]], TN) for h in heads] for e in exs]
            dv_s = [[_dot(kd[e][:, hl[h]], dsts[e][h], NT) for h in heads] for e in exs]
            dqb = [side_by_side([_dot(do[e][:, hl[h]], sts[e][h]) for h in heads]) for e in exs]
            dkd = [side_by_side([_dot(v[e][:, hl[h]], dsts[e][h]) for h in heads]) for e in exs]
            new_dst = [[_dot(do[e][:, hl[h]], qb[e][:, hl[h]], TN) for h in heads] for e in exs]
            dv = [side_by_side([dv_p[e][h] + dv_s[e][h] for h in heads]) + _per_head(_lane_sum, q[e] * k[e]) * do[e]
                  for e in exs]
            dq = [dqb[e] * eb[e] for e in exs]
            dk = [dkd[e] * ekd[e] for e in exs]
            de = []
            for e in exs:
                dbl = (jnp.sum(dkd[e] * kd[e], axis=0, keepdims=True)
                       + side_by_side([jnp.sum(dsts[e][h] * sts[e][h], axis=0, keepdims=True) for h in heads]) * ebl[e])
                de.append([dqb[e] * qb[e] - dkd[e] * kd[e] + jnp.where(last_row, dbl, 0.0)])
            for li in range(nl):
                mk = m_ref[li]
                dpm = [[mk * dp[e][h] for h in heads] for e in exs]
                dqm = [side_by_side([_dot(dpm[e][h], km[li][e][:, hl[h]]) for h in heads]) for e in exs]
                dkm = [side_by_side([_dot(dpm[e][h], qm[li][e][:, hl[h]], TN) for h in heads]) for e in exs]
                for e in exs:
                    dq[e] = dq[e] + dqm[e] * es[li][e]
                    dk[e] = dk[e] + dkm[e] * es[li][e]
                    de[e].append(dqm[e] * qm[li][e] + dkm[e] * km[li][e])
            dg = [_split_dot(a_mat, jnp.concatenate(de[e], axis=0), TN, 2) for e in exs]
            for e in exs:
                dpd = _per_head(_lane_sum, do[e] * v[e])
                df = dg[e] / f[e] - (dk[e] + dpd * q[e])
                dp_ref[e, rows, 0:hw] = _mx(dq[e] + dpd * k[e])
                dp_ref[e, rows, hw:2 * hw] = _mx(df * (1.0 - lb) * sig[e] * (1.0 - sig[e]))
                dp_ref[e, rows, 2 * hw:3 * hw] = _mx(dv[e])
                dp_ref[e, rows, 3 * hw:4 * hw] = _mx(dgl[e])
                for h in heads:
                    dst_ref[e, h] = dsts[e][h] * ebl[e][:, hl[h]] + new_dst[e][h]
                dlb_acc = dlb_acc + jnp.sum(df * (1.0 - sig[e]), axis=0, keepdims=True)
            return dlb_acc, dng_acc

        dlb, dng = lax.fori_loop(0, nct, chunk, (jnp.zeros((1, hw), F32), jnp.zeros((1, HG_D), F32)))

        @pl.when(first)
        def _():
            dlb_ref[...] = jnp.zeros_like(dlb_ref)
            dng_ref[...] = jnp.zeros_like(dng_ref)

        mx = jnp.max(lbp_v, axis=0, keepdims=True)
        e = jnp.exp(lbp_v - mx)
        s0 = e[0:1, :] / jnp.sum(e, axis=0, keepdims=True)
        da0 = dlb * s0 * (1.0 - s0)
        dlb_ref[...] += jnp.concatenate([da0, -da0], axis=0)
        dng_ref[...] += dng

    rows3 = lambda w: pl.BlockSpec((bsz, ts, w), lambda s: (0, ns - 1 - s, 0))
    dproj, dlb, dng = pl.pallas_call(
        body, name="hgrn2_bwd", grid=(ns,),
        in_specs=[rows3(HG_COLS),
                  pl.BlockSpec((2, hw), lambda s: (0, 0)),
                  pl.BlockSpec((1, HG_D), lambda s: (0, 0)),
                  pl.BlockSpec(a_all.shape, lambda s: (0, 0)),
                  pl.BlockSpec(masks.shape, lambda s: (0, 0, 0)),
                  rows3(hw),
                  pl.BlockSpec((bsz, HG_HEADS, nct, HG_D, HG_D), lambda s: (0, 0, ns - 1 - s, 0, 0)),
                  pl.BlockSpec((bsz, HG_HEADS, nct, cs, cs), lambda s: (0, 0, ns - 1 - s, 0, 0)),
                  rows3(hw)] + [pl.BlockSpec(memory_space=pl.ANY)] * len(after),
        out_specs=(rows3(HG_COLS),
                   pl.BlockSpec((2, hw), lambda s: (0, 0)),
                   pl.BlockSpec((1, HG_D), lambda s: (0, 0))),
        out_shape=(jax.ShapeDtypeStruct((bsz, seq, HG_COLS), _MXU_DTYPE),
                   jax.ShapeDtypeStruct((2, hw), F32),
                   jax.ShapeDtypeStruct((1, HG_D), F32)),
        scratch_shapes=[pltpu.VMEM((bsz, HG_HEADS, HG_D, HG_D), F32)],
        compiler_params=_params(("arbitrary",)),
    )(proj.reshape(bsz, seq, HG_COLS), lbp, ng, a_all, masks, o_all.reshape(bsz, seq, hw), states, scores,
      dy.reshape(bsz, seq, dy.shape[1]), *after)
    return dproj.reshape(t, HG_COLS), dlb, dng


def _sw_constants():
    half = ROT_DIM // 2
    inv = (np.float32(ROPE_THETA) ** (-(np.arange(half, dtype=np.float32) * np.float32(2.0) / np.float32(ROT_DIM)))
           ).astype(np.float32)
    freq = np.zeros((1, 128), np.float32)
    sign = np.zeros((1, 128), np.float32)
    for h in range(2):
        freq[0, 64 * h:64 * h + half] = inv
        freq[0, 64 * h + half:64 * h + 2 * half] = inv
        sign[0, 64 * h:64 * h + half] = -1.0
        sign[0, 64 * h + half:64 * h + 2 * half] = 1.0
    seg = np.kron(np.eye(8, dtype=np.float32), np.full((64, 64), 1.0 / 64.0, np.float32))
    return freq, sign, seg


def _rope_table(pos, *, tm=512, after=()):
    t = pos.shape[0]
    tm = min(tm, t)
    freq_np, sign_np, _ = _sw_constants()
    after = tuple(a for a in after if a is not None)

    def body(p_ref, f_ref, s_ref, *rest):
        o_ref = rest[-1]
        ang = p_ref[...].astype(F32) * f_ref[...]
        o_ref[:, 0:128] = jnp.cos(ang)
        o_ref[:, 128:256] = jnp.sin(ang) * s_ref[...]

    vec = pl.BlockSpec((1, 128), lambda i: (0, 0))
    return pl.pallas_call(
        body, name="rope_table", grid=(t // tm,),
        in_specs=[pl.BlockSpec((tm, 1), lambda i: (i, 0)), vec, vec] + [pl.BlockSpec(memory_space=pl.ANY)] * len(after),
        out_specs=pl.BlockSpec((tm, 256), lambda i: (i, 0)),
        out_shape=jax.ShapeDtypeStruct((t, 256), F32),
        compiler_params=_params(("parallel",)),
    )(pos, jnp.asarray(freq_np), jnp.asarray(sign_np), *after)


def _tile_lanes(v, times):
    return v if times == 1 else jnp.concatenate([v] * times, axis=1)


def _swap_halves(v):
    w = v.shape[1]
    half = ROT_DIM // 2
    lane = lax.broadcasted_iota(jnp.int32, v.shape, 1) % SW_HD
    return jnp.where(lane < half, pltpu.roll(v, w - half, 1), jnp.where(lane < 2 * half, pltpu.roll(v, half, 1), 0.0))


def _sw_norm_rope(tv, gain, seg, cosv, sinv):
    w = tv.shape[1]
    ms = _split_dot_rhs(tv * tv, seg[0:w, 0:w])
    r = lax.rsqrt(ms + EPS)
    tn = tv * r * gain
    reps = w // 128
    return tn * _tile_lanes(cosv, reps) + _swap_halves(tn) * _tile_lanes(sinv, reps), r


def _split_dot_rhs(v, a):
    hi = _mx(v)
    lo = _mx(v - hi.astype(F32))
    return (lax.dot_general(hi, a, (NN, ((), ())), preferred_element_type=F32)
            + lax.dot_general(lo, a, (NN, ((), ())), preferred_element_type=F32))


def _sw_norm_rope_bwd(dt, tv, r, gain, seg, cosv, sinv):
    w = tv.shape[1]
    reps = w // 128
    dtn = dt * _tile_lanes(cosv, reps) + _swap_halves(dt * _tile_lanes(sinv, reps))
    u = dtn * gain
    dtv = r * u - tv * (r * r * r) * _split_dot_rhs(u * tv, seg[0:w, 0:w])
    return dtv, jnp.sum(dtn * tv * r, axis=0, keepdims=True)


def _sw_scores(qh, kp, kc):
    return _dot(qh, kp, NT), _dot(qh, kc, NT)


SW_SCALE = SW_HD ** -0.5


def _sw_probs(raw, sink, first_block):
    qi = lax.broadcasted_iota(jnp.int32, (SW_BLOCK, SW_BLOCK), 0)
    kj = lax.broadcasted_iota(jnp.int32, (SW_BLOCK, SW_BLOCK), 1)
    ok_prev = jnp.logical_and(kj > qi, jnp.logical_not(first_block))
    ok_cur = kj <= qi
    sp = jnp.where(ok_prev, raw[0], -jnp.inf)
    sc = jnp.where(ok_cur, raw[1], -jnp.inf)
    m = jnp.maximum(jnp.maximum(jnp.max(sp, axis=1, keepdims=True), jnp.max(sc, axis=1, keepdims=True)), sink)
    pp, pc = jnp.exp(sp - m), jnp.exp(sc - m)
    es = jnp.exp(sink - m)
    inv = 1.0 / (jnp.sum(pp, axis=1, keepdims=True) + jnp.sum(pc, axis=1, keepdims=True) + es)
    return pp * inv, pc * inv, es * inv


def _sw_specs(nb):
    def cur(b, n):
        return b * nb + jnp.minimum(n, nb - 1)

    def prev(b, n):
        return b * nb + jnp.maximum(jnp.minimum(n, nb - 1) - 1, 0)

    return cur, prev


def _sw_fwd(proj, rope, qg, kg, sinks, y_in, bsz, seq):
    t = proj.shape[0]
    nb = seq // SW_BLOCK
    seg = jnp.asarray(_sw_constants()[2], _MXU_DTYPE)
    cur, prev = _sw_specs(nb)

    def body(q_ref, kc_ref, kp_ref, vc_ref, vp_ref, rc_ref, rp_ref, qg_ref, kg_ref, sk_ref, seg_ref, yin_ref, y_ref):
        del yin_ref
        n = pl.program_id(1)
        segv = seg_ref[...]
        cos_c, sin_c = rc_ref[:, 0:128], rc_ref[:, 128:256]
        cos_p, sin_p = rp_ref[:, 0:128], rp_ref[:, 128:256]
        qr, _ = _sw_norm_rope(q_ref[...], qg_ref[...] * SW_SCALE, segv, cos_c, sin_c)
        kcr, _ = _sw_norm_rope(kc_ref[...], kg_ref[...], segv, cos_c, sin_c)
        kpr, _ = _sw_norm_rope(kp_ref[...], kg_ref[...], segv, cos_p, sin_p)
        vc, vp = vc_ref[...], vp_ref[...]
        ks = [slice(SW_HD * (h // SW_GROUP), SW_HD * (h // SW_GROUP + 1)) for h in range(SW_HEADS)]
        raw = [_sw_scores(qr[:, SW_HD * h:SW_HD * (h + 1)], kpr[:, ks[h]], kcr[:, ks[h]]) for h in range(SW_HEADS)]
        probs = [_sw_probs(raw[h], sk_ref[0, h], n == 0) for h in range(SW_HEADS)]
        for h in range(SW_HEADS):
            y_ref[:, SW_HD * h:SW_HD * (h + 1)] = _dot(probs[h][0], vp[:, ks[h]]) + _dot(probs[h][1], vc[:, ks[h]])

    rowq = pl.BlockSpec((SW_BLOCK, 512), lambda b, n: (cur(b, n), 0))
    full = lambda a: pl.BlockSpec(a.shape, lambda b, n: (0,) * a.ndim)
    yw = y_in.shape[1]
    return pl.pallas_call(
        body, name="swa_fwd", grid=(bsz, nb),
        in_specs=[rowq,
                  pl.BlockSpec((SW_BLOCK, 128), lambda b, n: (cur(b, n), 4)),
                  pl.BlockSpec((SW_BLOCK, 128), lambda b, n: (prev(b, n), 4)),
                  pl.BlockSpec((SW_BLOCK, 128), lambda b, n: (cur(b, n), 5)),
                  pl.BlockSpec((SW_BLOCK, 128), lambda b, n: (prev(b, n), 5)),
                  pl.BlockSpec((SW_BLOCK, 256), lambda b, n: (cur(b, n), 0)),
                  pl.BlockSpec((SW_BLOCK, 256), lambda b, n: (prev(b, n), 0)),
                  full(qg), full(kg),
                  pl.BlockSpec(memory_space=pltpu.SMEM),
                  full(seg),
                  pl.BlockSpec(memory_space=pl.ANY)],
        out_specs=pl.BlockSpec((SW_BLOCK, 512), lambda b, n: (cur(b, n), 1)),
        out_shape=jax.ShapeDtypeStruct((t, yw), F32),
        input_output_aliases={11: 0},
        compiler_params=_params(("parallel", "parallel")),
    )(proj, proj, proj, proj, proj, rope, rope, qg, kg, sinks, seg, y_in)


def _sw_bwd(proj, rope, qg, kg, sinks, y, dy, bsz, seq):
    t = proj.shape[0]
    nb = seq // SW_BLOCK
    seg = jnp.asarray(_sw_constants()[2], _MXU_DTYPE)
    cur, prev = _sw_specs(nb)

    def body(q_ref, kc_ref, kp_ref, vc_ref, vp_ref, rc_ref, rp_ref, qg_ref, kg_ref, sk_ref, seg_ref,
             y_ref, dy_ref, dp_ref, dqg_ref, dkg_ref, dsk_ref,
             dq_car, dkv_car, dqr_s, dkc_s, dkp_s, dvc_s, dvp_s, gq_acc, gk_acc, sk_acc):
        b, n = pl.program_id(0), pl.program_id(1)
        first = jnp.logical_and(b == 0, n == 0)
        last = jnp.logical_and(b == pl.num_programs(0) - 1, n == nb)

        @pl.when(first)
        def _():
            gq_acc[...] = jnp.zeros_like(gq_acc)
            gk_acc[...] = jnp.zeros_like(gk_acc)
            sk_acc[...] = jnp.zeros_like(sk_acc)

        @pl.when(n < nb)
        def _():
            segv = seg_ref[...]
            cos_c, sin_c = rc_ref[:, 0:128], rc_ref[:, 128:256]
            cos_p, sin_p = rp_ref[:, 0:128], rp_ref[:, 128:256]
            qv, kcv, kpv = q_ref[...], kc_ref[...], kp_ref[...]
            qgain = qg_ref[...] * SW_SCALE
            qr, rq = _sw_norm_rope(qv, qgain, segv, cos_c, sin_c)
            kcr, rkc = _sw_norm_rope(kcv, kg_ref[...], segv, cos_c, sin_c)
            kpr, rkp = _sw_norm_rope(kpv, kg_ref[...], segv, cos_p, sin_p)
            vc, vp = vc_ref[...], vp_ref[...]
            lane = lax.broadcasted_iota(jnp.int32, (1, 128), 1)
            dsk = jnp.zeros((1, 128), F32)
            heads = range(SW_HEADS)
            ks = [slice(SW_HD * (h // SW_GROUP), SW_HD * (h // SW_GROUP + 1)) for h in heads]
            hs = [slice(SW_HD * h, SW_HD * (h + 1)) for h in heads]
            qh = [qr[:, hs[h]] for h in heads]
            doh = [dy_ref[:, hs[h]] for h in heads]
            raw = [_sw_scores(qh[h], kpr[:, ks[h]], kcr[:, ks[h]]) for h in heads]
            dpp = [_dot(doh[h], vp[:, ks[h]], NT) for h in heads]
            dpc = [_dot(doh[h], vc[:, ks[h]], NT) for h in heads]
            probs = [_sw_probs(raw[h], sk_ref[0, h], n == 0) for h in heads]
            dsp, dsc = [], []
            for h in heads:
                pp, pc, ps = probs[h]
                delta = jnp.sum(doh[h] * y_ref[:, hs[h]], axis=1, keepdims=True)
                dsp.append(pp * (dpp[h] - delta))
                dsc.append(pc * (dpc[h] - delta))
                dsk = dsk + jnp.where(lane == h, -jnp.sum(ps * delta), 0.0)
            for h in heads:
                dqr_s[:, hs[h]] = _dot(dsp[h], kpr[:, ks[h]]) + _dot(dsc[h], kcr[:, ks[h]])
            for kv in range(SW_KV_HEADS):
                group = range(SW_GROUP * kv, SW_GROUP * (kv + 1))
                kvs = slice(SW_HD * kv, SW_HD * (kv + 1))
                dvp_s[:, kvs] = sum(_dot(probs[h][0], doh[h], TN) for h in group)
                dvc_s[:, kvs] = sum(_dot(probs[h][1], doh[h], TN) for h in group)
                dkp_s[:, kvs] = sum(_dot(dsp[h], qh[h], TN) for h in group)
                dkc_s[:, kvs] = sum(_dot(dsc[h], qh[h], TN) for h in group)
            dq, gq = _sw_norm_rope_bwd(dqr_s[...], qv, rq, qgain, segv, cos_c, sin_c)
            dkc, gkc = _sw_norm_rope_bwd(dkc_s[...], kcv, rkc, kg_ref[...], segv, cos_c, sin_c)
            dkp, gkp = _sw_norm_rope_bwd(dkp_s[...], kpv, rkp, kg_ref[...], segv, cos_p, sin_p)
            gq_acc[...] += gq
            gk_acc[...] += gkc + gkp
            sk_acc[...] += dsk

            @pl.when(n > 0)
            def _():
                dp_ref[:, 0:512] = _mx(dq_car[...])
                dp_ref[:, 512:640] = _mx(dkv_car[:, 0:128] + dkp)
                dp_ref[:, 640:768] = _mx(dkv_car[:, 128:256] + dvp_s[...])

            dq_car[...] = dq
            dkv_car[:, 0:128] = dkc
            dkv_car[:, 128:256] = dvc_s[...]

        @pl.when(n == nb)
        def _():
            dp_ref[:, 0:512] = _mx(dq_car[...])
            dp_ref[:, 512:768] = _mx(dkv_car[...])

        @pl.when(last)
        def _():
            gq = gq_acc[...] * SW_SCALE
            acc = gq[:, 0:SW_HD]
            for h in range(1, SW_HEADS):
                acc = acc + gq[:, SW_HD * h:SW_HD * (h + 1)]
            dqg_ref[...] = acc
            gk = gk_acc[...]
            dkg_ref[...] = gk[:, 0:SW_HD] + gk[:, SW_HD:2 * SW_HD]
            dsk_ref[...] = sk_acc[...]

    rowq = pl.BlockSpec((SW_BLOCK, 512), lambda b, n: (cur(b, n), 0))
    full = lambda a: pl.BlockSpec(a.shape, lambda b, n: (0,) * a.ndim)

    def out_row(b, n):
        return b * nb + jnp.maximum(n - 1, 0)

    return pl.pallas_call(
        body, name="swa_bwd", grid=(bsz, nb + 1),
        in_specs=[rowq,
                  pl.BlockSpec((SW_BLOCK, 128), lambda b, n: (cur(b, n), 4)),
                  pl.BlockSpec((SW_BLOCK, 128), lambda b, n: (prev(b, n), 4)),
                  pl.BlockSpec((SW_BLOCK, 128), lambda b, n: (cur(b, n), 5)),
                  pl.BlockSpec((SW_BLOCK, 128), lambda b, n: (prev(b, n), 5)),
                  pl.BlockSpec((SW_BLOCK, 256), lambda b, n: (cur(b, n), 0)),
                  pl.BlockSpec((SW_BLOCK, 256), lambda b, n: (prev(b, n), 0)),
                  full(qg), full(kg),
                  pl.BlockSpec(memory_space=pltpu.SMEM),
                  full(seg),
                  pl.BlockSpec((SW_BLOCK, 512), lambda b, n: (cur(b, n), 1)),
                  pl.BlockSpec((SW_BLOCK, 512), lambda b, n: (cur(b, n), 1))],
        out_specs=(pl.BlockSpec((SW_BLOCK, SW_COLS), lambda b, n: (out_row(b, n), 0)),
                   pl.BlockSpec((1, SW_HD), lambda b, n: (0, 0)),
                   pl.BlockSpec((1, SW_HD), lambda b, n: (0, 0)),
                   pl.BlockSpec((1, 128), lambda b, n: (0, 0))),
        out_shape=(jax.ShapeDtypeStruct((t, SW_COLS), _MXU_DTYPE),
                   jax.ShapeDtypeStruct((1, SW_HD), F32),
                   jax.ShapeDtypeStruct((1, SW_HD), F32),
                   jax.ShapeDtypeStruct((1, 128), F32)),
        scratch_shapes=[pltpu.VMEM((SW_BLOCK, 512), F32), pltpu.VMEM((SW_BLOCK, 256), F32),
                        pltpu.VMEM((SW_BLOCK, 512), F32),
                        pltpu.VMEM((SW_BLOCK, 128), F32), pltpu.VMEM((SW_BLOCK, 128), F32),
                        pltpu.VMEM((SW_BLOCK, 128), F32), pltpu.VMEM((SW_BLOCK, 128), F32),
                        pltpu.VMEM((1, 512), F32), pltpu.VMEM((1, 128), F32), pltpu.VMEM((1, 128), F32)],
        compiler_params=_params(("arbitrary", "arbitrary")),
    )(proj, proj, proj, proj, proj, rope, rope, qg, kg, sinks, seg, y, dy)


def _head_rms(tv, gain):
    r = lax.rsqrt(jnp.mean(tv * tv, axis=1, keepdims=True) + EPS)
    return tv * r * gain, r


def _head_rms_bwd(dtn, tv, r, gain):
    u = dtn * gain
    return r * u - tv * (r * r * r) * jnp.mean(u * tv, axis=1, keepdims=True), jnp.sum(dtn * tv * r, axis=0, keepdims=True)


def _xa_softmax(raw):
    s = raw * (XA_HD ** -0.5)
    e = jnp.exp(s - jnp.max(s, axis=1, keepdims=True))
    return e * (1.0 / jnp.sum(e, axis=1, keepdims=True))


def _xa_fwd(qx, kvx, qg, kg, bsz, seq, mlen, *, tq=512):
    t = qx.shape[0]
    tq = min(tq, seq)
    nq = seq // tq
    w = XA_HEADS * XA_HD

    def body(q_ref, kv_ref, qg_ref, kg_ref, o_ref):
        heads = range(XA_HEADS)
        hs = [slice(XA_HD * h, XA_HD * (h + 1)) for h in heads]
        qn = [_head_rms(q_ref[:, hs[h]], qg_ref[...])[0] for h in heads]
        kn = [_head_rms(kv_ref[:, hs[h]], kg_ref[...])[0] for h in heads]
        raw = [_dot(qn[h], kn[h], NT) for h in heads]
        p = [_xa_softmax(raw[h]) for h in heads]
        for h in heads:
            o_ref[:, hs[h]] = _dot(p[h], kv_ref[:, w + XA_HD * h:w + XA_HD * (h + 1)]).astype(o_ref.dtype)

    vec = pl.BlockSpec((1, XA_HD), lambda b, i: (0, 0))
    return pl.pallas_call(
        body, name="xattn_fwd", grid=(bsz, nq),
        in_specs=[pl.BlockSpec((tq, w), lambda b, i: (b * nq + i, 0)),
                  pl.BlockSpec((mlen, 2 * w), lambda b, i: (b, 0)), vec, vec],
        out_specs=pl.BlockSpec((tq, w), lambda b, i: (b * nq + i, 0)),
        out_shape=jax.ShapeDtypeStruct((t, w), _MXU_DTYPE),
        compiler_params=_params(("parallel", "parallel")),
    )(qx, kvx, qg, kg)


def _xa_bwd(qx, kvx, qg, kg, do, bsz, seq, mlen, *, tq=1024):
    t = qx.shape[0]
    tq = min(tq, seq)
    nq = seq // tq
    w = XA_HEADS * XA_HD
    scale = XA_HD ** -0.5

    def body(q_ref, kv_ref, qg_ref, kg_ref, do_ref, dq_ref, dkv_ref, dqg_ref, dkg_ref):
        b, i = pl.program_id(0), pl.program_id(1)

        @pl.when(jnp.logical_and(b == 0, i == 0))
        def _():
            dqg_ref[...] = jnp.zeros_like(dqg_ref)
            dkg_ref[...] = jnp.zeros_like(dkg_ref)

        @pl.when(i == 0)
        def _():
            dkv_ref[...] = jnp.zeros_like(dkv_ref)

        heads = range(XA_HEADS)
        hs = [slice(XA_HD * h, XA_HD * (h + 1)) for h in heads]
        vs = [slice(w + XA_HD * h, w + XA_HD * (h + 1)) for h in heads]
        qv = [q_ref[:, hs[h]] for h in heads]
        kv = [kv_ref[:, hs[h]] for h in heads]
        doh = [do_ref[:, hs[h]] for h in heads]
        qn = [_head_rms(qv[h], qg_ref[...]) for h in heads]
        kn = [_head_rms(kv[h], kg_ref[...]) for h in heads]
        raw = [_dot(qn[h][0], kn[h][0], NT) for h in heads]
        dp = [_dot(doh[h], kv_ref[:, vs[h]], NT) for h in heads]
        p = [_xa_softmax(raw[h]) for h in heads]
        ds = [p[h] * (dp[h] - jnp.sum(p[h] * dp[h], axis=1, keepdims=True)) * scale for h in heads]
        dqn = [_dot(ds[h], kn[h][0]) for h in heads]
        dkn = [_dot(ds[h], qn[h][0], TN) for h in heads]
        dvv = [_dot(p[h], doh[h], TN) for h in heads]
        gq_sum = jnp.zeros((1, XA_HD), F32)
        gk_sum = jnp.zeros((1, XA_HD), F32)
        for h in heads:
            dqv, gq = _head_rms_bwd(dqn[h], qv[h], qn[h][1], qg_ref[...])
            dkv, gk = _head_rms_bwd(dkn[h], kv[h], kn[h][1], kg_ref[...])
            dq_ref[:, hs[h]] = dqv.astype(dq_ref.dtype)
            dkv_ref[:, hs[h]] += dkv
            dkv_ref[:, vs[h]] += dvv[h]
            gq_sum = gq_sum + gq
            gk_sum = gk_sum + gk
        dqg_ref[...] += gq_sum
        dkg_ref[...] += gk_sum

    vec = pl.BlockSpec((1, XA_HD), lambda b, i: (0, 0))
    row = pl.BlockSpec((tq, w), lambda b, i: (b * nq + i, 0))
    mem = pl.BlockSpec((mlen, 2 * w), lambda b, i: (b, 0))
    return pl.pallas_call(
        body, name="xattn_bwd", grid=(bsz, nq),
        in_specs=[row, mem, vec, vec, row],
        out_specs=(row, mem, vec, vec),
        out_shape=(jax.ShapeDtypeStruct((t, w), _MXU_DTYPE), jax.ShapeDtypeStruct((bsz * mlen, 2 * w), F32),
                   jax.ShapeDtypeStruct((1, XA_HD), F32), jax.ShapeDtypeStruct((1, XA_HD), F32)),
        compiler_params=_params(("arbitrary", "arbitrary")),
    )(qx, kvx, qg, kg, do)


def _loss_finish(sq_row, d_model):
    def body(s_ref, o_ref):
        o_ref[...] = jnp.zeros_like(o_ref) + 0.5 * jnp.sum(s_ref[...]) / float(d_model)

    return pl.pallas_call(body, name="loss_finish", out_shape=jax.ShapeDtypeStruct((1, 128), F32))(sq_row)


def _adamw_math(w, g, m, v):
    m = ADAM_B1 * m + (1.0 - ADAM_B1) * g
    v = ADAM_B2 * v + (1.0 - ADAM_B2) * (g * g)
    m_hat = m / (1.0 - ADAM_B1 ** ADAM_STEP)
    v_hat = v / (1.0 - ADAM_B2 ** ADAM_STEP)
    return -ADAM_LR * (m_hat / (jnp.sqrt(v_hat) + ADAM_EPS) + ADAM_WD * w), m, v


def _adamw_big(ws, gs, ms, vs, *, steps=8):
    n = len(ws)

    def body(*refs):
        for a in range(n):
            gv = refs[n + a][...]
            d, mn, vn = _adamw_math(refs[a][...], gv, refs[2 * n + a][...], refs[3 * n + a][...])
            refs[4 * n + 4 * a][...] = gv
            refs[4 * n + 4 * a + 1][...] = d
            refs[4 * n + 4 * a + 2][...] = mn
            refs[4 * n + 4 * a + 3][...] = vn

    def spec(w):
        assert w.shape[0] % (8 * steps) == 0, w.shape
        return pl.BlockSpec((w.shape[0] // steps, w.shape[1]), lambda i: (i, 0))

    specs = [spec(w) for w in ws]
    out = pl.pallas_call(
        body, name="adamw_big", grid=(steps,), in_specs=specs * 4,
        out_specs=tuple(s for s in specs for _ in range(4)),
        out_shape=tuple(jax.ShapeDtypeStruct(w.shape, F32) for w in ws for _ in range(4)),
        compiler_params=_params(("parallel",)),
    )(*ws, *gs, *ms, *vs)
    return [out[4 * a:4 * a + 4] for a in range(n)]


def _adamw_small(ws, gs, ms, vs):
    n = len(ws)

    def body(*refs):
        for i in range(n):
            d, mn, vn = _adamw_math(refs[i][...], refs[n + i][...], refs[2 * n + i][...], refs[3 * n + i][...])
            refs[4 * n + i][...] = d
            refs[5 * n + i][...] = mn
            refs[6 * n + i][...] = vn

    shapes = tuple(jax.ShapeDtypeStruct(w.shape, F32) for w in ws)
    return pl.pallas_call(body, name="adamw_small", out_shape=shapes * 3)(*ws, *gs, *ms, *vs)


def _add_halves(gs, recvs, c_idx, *, name):
    n = len(gs)

    def body(c_ref, *refs):
        del c_ref
        for a in range(n):
            refs[2 * n + a][...] = refs[a][...] + refs[n + a][...]

    def half(g):
        return pl.BlockSpec((None, g.shape[1] // 2, g.shape[2]), lambda k, cr: (k, cr[0], 0))

    def whole(g):
        return pl.BlockSpec((None, g.shape[1] // 2, g.shape[2]), lambda k, cr: (k, 0, 0))

    return pl.pallas_call(
        body, name=name,
        grid_spec=pltpu.PrefetchScalarGridSpec(
            num_scalar_prefetch=1, grid=(4,),
            in_specs=[half(g) for g in gs] + [whole(g) for g in gs],
            out_specs=tuple(whole(g) for g in gs)),
        out_shape=tuple(jax.ShapeDtypeStruct((4, g.shape[1] // 2, g.shape[2]), F32) for g in gs),
        compiler_params=_params(("parallel",)),
    )(c_idx, *gs, *recvs)


def _add_chips(ps, recvs, place_idx, *, name, steps=4, after=()):
    n = len(ps)

    def body(pi_ref, *refs):
        del pi_ref
        outs = refs[2 * n + len(after):]
        for a in range(n):
            r_ref = refs[n + a]
            outs[a][...] = ((refs[a][...] + r_ref[0]) + r_ref[1]) + r_ref[2]

    def tile(p):
        assert p.shape[1] % (8 * steps) == 0, (name, p.shape)
        return p.shape[1] // steps

    return pl.pallas_call(
        body, name=name,
        grid_spec=pltpu.PrefetchScalarGridSpec(
            num_scalar_prefetch=1, grid=(steps,),
            in_specs=[pl.BlockSpec((None, tile(p), p.shape[2]), lambda i, pi: (pi[0], i, 0)) for p in ps]
            + [pl.BlockSpec((3, tile(p), p.shape[2]), lambda i, pi: (0, i, 0)) for p in ps]
            + [pl.BlockSpec(memory_space=pl.ANY)] * len(after),
            out_specs=tuple(pl.BlockSpec((tile(p), p.shape[2]), lambda i, pi: (pi[1] * steps + i, 0)) for p in ps)),
        out_shape=tuple(jax.ShapeDtypeStruct((2 * p.shape[1], p.shape[2]), F32) for p in ps),
        compiler_params=_params(("parallel",)),
    )(place_idx, *ps, *recvs, *after)


def _place_shards(shards, place_idx, *, name, after=()):
    n = len(shards)

    def body(pi_ref, *refs):
        del pi_ref
        for i in range(n):
            refs[n + len(after) + i][...] = refs[i][...]

    return pl.pallas_call(
        body, name=name,
        grid_spec=pltpu.PrefetchScalarGridSpec(
            num_scalar_prefetch=1, grid=(1,),
            in_specs=[pl.BlockSpec(s.shape, lambda i, pi: (0, 0)) for s in shards]
            + [pl.BlockSpec(memory_space=pl.ANY)] * len(after),
            out_specs=tuple(pl.BlockSpec((None,) + s.shape, lambda i, pi: (pi[0], 0, 0)) for s in shards)),
        out_shape=tuple(jax.ShapeDtypeStruct((4,) + s.shape, s.dtype) for s in shards),
        compiler_params=_params(("arbitrary",)),
    )(place_idx, *shards, *after)


def _place_shard(shard, place_idx, *, name, tr=512, after=()):
    r, c = shard.shape
    tr = min(tr, r)
    if r % tr:
        tr = r // 2
    assert r % tr == 0 and tr % 16 == 0, (name, r, tr)

    def body(pi_ref, s_ref, *rest):
        del pi_ref
        rest[-1][...] = s_ref[...]

    return pl.pallas_call(
        body, name=name,
        grid_spec=pltpu.PrefetchScalarGridSpec(
            num_scalar_prefetch=1, grid=(r // tr,),
            in_specs=[pl.BlockSpec((tr, c), lambda i, pi: (i, 0))] + [pl.BlockSpec(memory_space=pl.ANY)] * len(after),
            out_specs=pl.BlockSpec((None, tr, c), lambda i, pi: (pi[0], i, 0))),
        out_shape=jax.ShapeDtypeStruct((4, r, c), shard.dtype),
        compiler_params=_params(("parallel",)),
    )(place_idx, shard, *after)


def _place():
    x, y, c = lax.axis_index("x"), lax.axis_index("y"), lax.axis_index("c")
    chips = [(1 - x, y), (x, 1 - y), (1 - x, 1 - y)]
    return x, y, c, chips


ANY = pl.BlockSpec(memory_space=pl.ANY)


def _exchange_halves(grads, name):
    n = len(grads)

    def body(*refs):
        ins, outs = refs[:n], refs[n:2 * n]
        send_sems, recv_sems = refs[2 * n:]
        x, y, c, _ = _place()

        def copy(a):
            h = ins[a].shape[1] // 2
            return pltpu.make_async_remote_copy(
                src_ref=ins[a].at[:, pl.ds((1 - c) * h, h), :], dst_ref=outs[a],
                send_sem=send_sems.at[a], recv_sem=recv_sems.at[a], device_id=(x, y, 1 - c), device_id_type=MESH)

        for a in range(n):
            copy(a).start()
        for a in range(n):
            copy(a).wait_recv()
        for a in range(n):
            copy(a).wait_send()

    return pl.pallas_call(
        body, name=name,
        in_specs=[ANY] * n, out_specs=tuple([ANY] * n),
        out_shape=tuple(jax.ShapeDtypeStruct((4, g.shape[1] // 2, g.shape[2]), g.dtype) for g in grads),
        scratch_shapes=[pltpu.SemaphoreType.DMA((n,)), pltpu.SemaphoreType.DMA((n,))],
    )(*grads)


HBM = pl.BlockSpec(memory_space=pltpu.HBM)
SEM = pl.BlockSpec(memory_space=pltpu.SEMAPHORE)
EFFECT = pltpu.SideEffectType.DATAFLOW_SIDE_EFFECTING


def _in_hbm(a):
    return pltpu.with_memory_space_constraint(a, pltpu.HBM)


def _split_copy_calls(name, srcs, lands, n_copies, make_copies):
    ns, nl = len(srcs), len(lands)
    nb = ns + nl

    def start(after=()):
        n_after = len(after)

        def body(*refs):
            outs = refs[nb + n_after:]
            copies = make_copies(refs[:ns], refs[ns:nb], outs[0], outs[1])
            for cp in copies:
                cp.start()
            token = refs[-1]
            token[...] = jnp.zeros_like(token)

        bufs = [_in_hbm(a) for a in list(srcs) + list(lands)]
        out = pl.pallas_call(
            body, name=name + "_start",
            out_shape=(pltpu.SemaphoreType.DMA((n_copies,)), pltpu.SemaphoreType.DMA((n_copies,)),
                       *[pltpu.HBM(a.shape, a.dtype) for a in bufs], jax.ShapeDtypeStruct((8, 128), F32)),
            in_specs=[HBM] * nb + [pl.BlockSpec(memory_space=pl.ANY)] * n_after,
            out_specs=(SEM, SEM, *[HBM] * nb, pl.BlockSpec(memory_space=pltpu.VMEM)),
            input_output_aliases={i: 2 + i for i in range(nb)},
            compiler_params=pltpu.CompilerParams(has_side_effects=EFFECT),
        )(*bufs, *after)
        return dict(send=out[0], recv=out[1], bufs=list(out[2:2 + nb]), token=out[-1])

    def wait(state, after):
        def body(*refs):
            copies = make_copies(refs[:ns], refs[ns:nb], refs[nb], refs[nb + 1])
            for cp in copies:
                cp.wait_send()
            for cp in copies:
                cp.wait_recv()

        bufs = state["bufs"]
        out = pl.pallas_call(
            body, name=name + "_wait",
            out_shape=tuple(pltpu.HBM(a.shape, a.dtype) for a in bufs),
            in_specs=[HBM] * nb + [SEM, SEM] + [pl.BlockSpec(memory_space=pl.ANY)] * len(after),
            out_specs=tuple([HBM] * nb),
            input_output_aliases={i: i for i in range(nb)},
            compiler_params=pltpu.CompilerParams(has_side_effects=EFFECT),
        )(*bufs, state["send"], state["recv"], *after)
        return list(out[:ns]), list(out[ns:])

    return start, wait


def _scatter_chips_split(name, parts):
    n = len(parts)
    lands = [lax.empty((3,) + p.shape[1:], p.dtype) for p in parts]

    def make_copies(srcs, lnds, send_sems, recv_sems):
        _, _, c, chips = _place()
        return [pltpu.make_async_remote_copy(
            src_ref=srcs[a].at[2 * px + py], dst_ref=lnds[a].at[j], send_sem=send_sems.at[a * 3 + j],
            recv_sem=recv_sems.at[a * 3 + j], device_id=(px, py, c), device_id_type=MESH)
            for a in range(n) for j, (px, py) in enumerate(chips)]

    return _split_copy_calls(name, parts, lands, 3 * n, make_copies)


def _exchange_halves_split(name, grads):
    n = len(grads)
    lands = [lax.empty((4, g.shape[1] // 2, g.shape[2]), g.dtype) for g in grads]

    def make_copies(srcs, lnds, send_sems, recv_sems):
        x, y, c, _ = _place()
        out = []
        for a in range(n):
            h = srcs[a].shape[1] // 2
            out.append(pltpu.make_async_remote_copy(
                src_ref=srcs[a].at[:, pl.ds((1 - c) * h, h), :], dst_ref=lnds[a], send_sem=send_sems.at[a],
                recv_sem=recv_sems.at[a], device_id=(x, y, 1 - c), device_id_type=MESH))
        return out

    return _split_copy_calls(name, grads, lands, n, make_copies)


def _gather_chips_split(name, shards, lands):
    n = len(shards)

    def make_copies(srcs, lnds, send_sems, recv_sems):
        x, y, c, chips = _place()
        out = []
        for a in range(n):
            h = srcs[a].shape[0] // 2
            for j, (px, py) in enumerate(chips):
                out.append(pltpu.make_async_remote_copy(
                    src_ref=srcs[a].at[pl.ds(c * h, h), :], dst_ref=lnds[a].at[2 * x + y, pl.ds(c * h, h), :],
                    send_sem=send_sems.at[a * 3 + j], recv_sem=recv_sems.at[a * 3 + j],
                    device_id=(px, py, c), device_id_type=MESH))
        return out

    return _split_copy_calls(name, shards, lands, 3 * n, make_copies)


def _gather_finish(gathered, name):
    n = len(gathered)

    def body(*refs):
        outs = refs[n:2 * n]
        send_sems, recv_sems = refs[2 * n:]
        x, y, c, chips = _place()

        def copy(a, j, chip_idx, which):
            h = outs[a].shape[1] // 2
            rows = outs[a].at[chip_idx, pl.ds(which * h, h), :]
            return pltpu.make_async_remote_copy(
                src_ref=rows, dst_ref=rows, send_sem=send_sems.at[a * 3 + j], recv_sem=recv_sems.at[a * 3 + j],
                device_id=(x, y, 1 - c), device_id_type=MESH)

        for a in range(n):
            for j, (px, py) in enumerate(chips):
                copy(a, j, 2 * px + py, c).start()
        for a in range(n):
            for j, (px, py) in enumerate(chips):
                copy(a, j, 2 * px + py, 1 - c).wait_recv()
        for a in range(n):
            for j, (px, py) in enumerate(chips):
                copy(a, j, 2 * px + py, c).wait_send()

    return pl.pallas_call(
        body, name=name,
        in_specs=[ANY] * n, out_specs=tuple([ANY] * n),
        out_shape=tuple(jax.ShapeDtypeStruct(g.shape, g.dtype) for g in gathered),
        input_output_aliases={i: i for i in range(n)},
        scratch_shapes=[pltpu.SemaphoreType.DMA((3 * n,)), pltpu.SemaphoreType.DMA((3 * n,))],
    )(*gathered)


def _gather_forward_split(name, gathered):
    n = len(gathered)

    def make_copies(srcs, lnds, send_sems, recv_sems):
        x, y, c, chips = _place()
        out = []
        for a in range(n):
            h = lnds[a].shape[1] // 2
            for j, (px, py) in enumerate(chips):
                rows = lnds[a].at[2 * px + py, pl.ds(c * h, h), :]
                out.append(pltpu.make_async_remote_copy(
                    src_ref=rows, dst_ref=rows, send_sem=send_sems.at[a * 3 + j], recv_sem=recv_sems.at[a * 3 + j],
                    device_id=(x, y, 1 - c), device_id_type=MESH))
        return out

    return _split_copy_calls(name, [], gathered, 3 * n, make_copies)


def _join_halves(fulls):
    n = len(fulls)

    def body(*refs):
        outs = refs[n:2 * n]
        send_sems, recv_sems = refs[2 * n:]
        x, y, c, _ = _place()

        def copy(a, which):
            h = outs[a].shape[0] // 2
            rows = outs[a].at[pl.ds(which * h, h), :]
            return pltpu.make_async_remote_copy(
                src_ref=rows, dst_ref=rows, send_sem=send_sems.at[a], recv_sem=recv_sems.at[a],
                device_id=(x, y, 1 - c), device_id_type=MESH)

        for a in range(n):
            copy(a, c).start()
        for a in range(n):
            copy(a, 1 - c).wait_recv()
        for a in range(n):
            copy(a, c).wait_send()

    return pl.pallas_call(
        body, name="rs_join_halves",
        in_specs=[ANY] * n, out_specs=tuple([ANY] * n),
        out_shape=tuple(jax.ShapeDtypeStruct(p.shape, p.dtype) for p in fulls),
        input_output_aliases={i: i for i in range(n)},
        scratch_shapes=[pltpu.SemaphoreType.DMA((n,)), pltpu.SemaphoreType.DMA((n,))],
    )(*fulls)


def _all_gather_small_split(sm):
    r, w = sm.shape

    def make_copies(srcs, lnds, send_sems, recv_sems):
        x, y, c, _ = _place()
        me = 4 * x + 2 * y + c
        rel = [(dx, dy, dc) for dx in (0, 1) for dy in (0, 1) for dc in (0, 1)][1:]
        return [pltpu.make_async_remote_copy(
            src_ref=srcs[0], dst_ref=lnds[0].at[me], send_sem=send_sems.at[k], recv_sem=recv_sems.at[k],
            device_id=(1 - x if dx else x, 1 - y if dy else y, 1 - c if dc else c), device_id_type=MESH)
            for k, (dx, dy, dc) in enumerate(rel)]

    return _split_copy_calls("all_gather_small", [sm], [lax.empty((8, r, w), sm.dtype)], 7, make_copies)


def _sum_devices(sm, gathered, me_idx):
    def body(me_ref, sm_ref, g_ref, o_ref):
        own = sm_ref[...]
        acc = jnp.where(me_ref[0] == 0, own, g_ref[0])
        for d in range(1, 8):
            acc = acc + jnp.where(me_ref[0] == d, own, g_ref[d])
        o_ref[...] = acc

    vm = pl.BlockSpec(memory_space=pltpu.VMEM)
    return pl.pallas_call(
        body, name="sum_devices", in_specs=[pl.BlockSpec(memory_space=pltpu.SMEM), vm, vm], out_specs=vm,
        out_shape=jax.ShapeDtypeStruct(sm.shape, F32),
    )(me_idx, sm, gathered)


def _local_step(x3, mem3, pos2, target3, small, comm):
    bsz, seq, d = x3.shape
    mlen = mem3.shape[1]
    t = bsz * seq
    tok = comm.begin()
    x = x3.reshape(t, d)
    mem = mem3.reshape(bsz * mlen, d)
    target = target3.reshape(t, d)
    rope = _rope_table(pos2.reshape(t, 1), after=tok)
    qg_t = jnp.tile(small["sw_q_norm_g"], (1, SW_HEADS))
    kg_t = jnp.tile(small["sw_k_norm_g"], (1, SW_KV_HEADS))

    hn1 = _rms_fwd(x, small["norm1_g"], name="rms1_fwd", after=tok)
    w = comm.first((hn1, rope))
    w_in_t = w["w_in_t"]
    w_sw_t = w_in_t[HG_COLS:]
    proj_hg = _mm(hn1, w_in_t, NT, t, HG_COLS, d, name="proj_hg", tk=d, after=(w.get("token"),))[0]
    proj_sw = _mm(hn1, w_sw_t, NT, t, SW_COLS, d, name="proj_sw", tk=d)[0]
    y_mix, o_hg, states, hg_scores = _hg_fwd(proj_hg, small["hg_lower_bounds"], small["hg_norm_g"], bsz, seq, y_width=1024)
    y_mix = _sw_fwd(proj_sw, rope, qg_t, kg_t, small["sw_sinks"], y_mix, bsz, seq)
    w = comm.rest(y_mix)
    h1, hn2 = _mm(y_mix, w["w_out"], NN, t, d, 1024, name="out_proj", tk=1024, extras=(x,), rows=(small["norm2_g"],),
                  epilogue=_residual_rms, out_dtypes=(F32, _MXU_DTYPE), after=(w.get("token"),))
    mn = _rms_fwd(mem, small["mem_norm_g"], name="rms_mem_fwd")
    qx = _mm(hn2, w["wq"], NN, t, 512, d, name="xa_q", tk=d)[0]
    kvx = _mm(mn, w["wkv"], NN, bsz * mlen, 1024, d, name="xa_kv", tk=d)[0]
    ox = _xa_fwd(qx, kvx, small["xa_q_norm_g"], small["xa_k_norm_g"], bsz, seq, mlen)
    h2, hn3 = _mm(ox, w["wo"], NN, t, d, 512, name="xa_o", tk=512, extras=(h1,), rows=(small["norm3_g"],),
                  epilogue=_residual_rms, out_dtypes=(F32, _MXU_DTYPE))
    w = {**w, **comm.mlp(hn3)}
    ff = w["down"].shape[0]
    ffs = ff // 4

    def relu_sq(acc):
        a = jnp.maximum(acc, 0.0)
        return a, a * a

    act, act2 = _mm(hn3, w["up"], NN, t, ff, d, name="mlp_up", tm=2048, tn=ffs, tk=d,
                    b_spec=pl.BlockSpec((None, d, ffs), lambda i, j, kk: (j, 0, 0)),
                    epilogue=relu_sq, out_dtypes=(_MXU_DTYPE, _MXU_DTYPE))
    inv_d = 1.0 / d

    def loss_cotangent(acc, res, tgt):
        diff = acc + res - tgt
        v = diff * inv_d
        return v, v, jnp.sum(diff * diff, axis=0, keepdims=True)

    dy, dy_mx, sq_row = _mm(act2, w["down"], NN, t, d, ff, name="mlp_down", tk=2048, extras=(h2, target),
                            epilogue=loss_cotangent, out_dtypes=(F32, _MXU_DTYPE), row_sums=1)
    loss_row = _loss_finish(sq_row, d)

    dz = _mm(dy_mx, w["down"], NT, t, ff, d, name="d_act", tm=2048, tk=d, extras=(act,),
             epilogue=lambda acc, a: (acc * (2.0 * a.astype(F32)),), out_dtypes=(_MXU_DTYPE,))[0]
    g_down = _mm(act2, dy_mx, TN, ff, d, t, name="g_down", tk=t)[0]
    g_up = _mm(hn3, dz, TN, d, ff, t, name="g_up", tn=ffs, tk=t,
               out_shape=(jax.ShapeDtypeStruct((4, d, ffs), F32),),
               out_spec=(pl.BlockSpec((None, min(1024, d), ffs), lambda i, j, kk: (j, i, 0)),))[0]
    tok = comm.grads("mlp", dict(up=g_up, down=g_down))
    dh2, dh2_mx, g_norm3 = _mm(dz, w["up"], NT, t, d, ff, name="d_hn3", tk=ffs, after=tok,
                               b_spec=pl.BlockSpec((None, min(1024, d), ffs), lambda i, j, kk: (kk, j, 0)),
                               extras=(h2, dy), rows=(small["norm3_g"],), epilogue=_rms_bwd_residual,
                               out_dtypes=(F32, _MXU_DTYPE), row_sums=1)
    d_ox = _mm(dh2_mx, w["wo"], NT, t, 512, d, name="d_ox", tk=d)[0]
    g_wo = _mm(ox, dh2_mx, TN, 512, d, t, name="g_wo", tk=t)[0]
    d_qx, d_kvx, g_xq, g_xk = _xa_bwd(qx, kvx, small["xa_q_norm_g"], small["xa_k_norm_g"], d_ox, bsz, seq, mlen)
    g_wq = _mm(hn2, d_qx, TN, d, 512, t, name="g_wq")[0]
    g_wkv = _mm(mn, d_kvx, TN, d, 1024, bsz * mlen, name="g_wkv")[0]
    dh1, dh1_mx, g_norm2 = _mm(d_qx, w["wq"], NT, t, d, 512, name="d_hn2", tk=512, extras=(h1, dh2),
                               rows=(small["norm2_g"],), epilogue=_rms_bwd_residual, out_dtypes=(F32, _MXU_DTYPE),
                               row_sums=1)
    dmn = _mm(d_kvx, w["wkv"], NT, bsz * mlen, d, 1024, name="d_mn", tk=1024)[0]
    g_memn = _rms_gain_grad(mem, small["mem_norm_g"], dmn, name="rms_mem_bwd")
    g_wout = _mm(y_mix, dh1_mx, TN, 1024, d, t, name="g_wout", tk=2048)[0]
    tok = comm.grads("mid", dict(w_out=g_wout, wq=g_wq, wkv=g_wkv, wo=g_wo))
    d_mix = _mm(dh1_mx, w["w_out"], NT, t, 1024, d, name="d_mix", tk=d, after=tok)[0]
    dproj_sw, g_swq, g_swk, g_sinks = _sw_bwd(proj_sw, rope, qg_t, kg_t, small["sw_sinks"], y_mix, d_mix, bsz, seq)
    tok = comm.poll(dproj_sw)
    dproj_hg, g_lb, g_hgn = _hg_bwd(proj_hg, small["hg_lower_bounds"], small["hg_norm_g"], o_hg, states, hg_scores, d_mix, bsz, seq,
                                    after=tok)
    in_rows = HG_COLS + SW_COLS
    sw_tile = 256
    g_in_t = _mm(dproj_hg, hn1, TN, HG_COLS, d, t, name="g_in_hg", tk=t,
                 out_shape=(jax.ShapeDtypeStruct((in_rows, d), F32),),
                 out_spec=(pl.BlockSpec((1024, min(1024, d)), lambda i, j, kk: (i, j)),))[0]
    g_in_t = _mm(dproj_sw, hn1, TN, SW_COLS, d, t, name="g_in_sw", tm=sw_tile, into=g_in_t,
                 out_shape=(jax.ShapeDtypeStruct((in_rows, d), F32),),
                 out_spec=(pl.BlockSpec((sw_tile, min(1024, d)), lambda i, j, kk: (HG_COLS // sw_tile + i, j)),))[0]
    tok = comm.grads("in", dict(w_in_t=g_in_t))
    grad_x, g_norm1 = _mm(dproj_hg, w_in_t, NN, t, d, HG_COLS, name="d_hn1", tk=HG_COLS, second=(dproj_sw, w_sw_t),
                          extras=(x, dh1), rows=(small["norm1_g"],), row_sums=1, after=tok,
                          epilogue=lambda acc, xv, dres, g: _rms_bwd_residual(acc, xv, dres, g)[1:])

    g_small = dict(norm1_g=g_norm1, hg_lower_bounds=g_lb, hg_norm_g=g_hgn, sw_q_norm_g=g_swq, sw_k_norm_g=g_swk,
                   sw_sinks=g_sinks[:, 0:SW_HEADS], norm2_g=g_norm2, mem_norm_g=g_memn, xa_q_norm_g=g_xq,
                   xa_k_norm_g=g_xk, norm3_g=g_norm3)
    return loss_row, grad_x.reshape(bsz, seq, d), g_small


SMALL_NAMES = ("norm1_g", "hg_lower_bounds", "hg_norm_g", "sw_q_norm_g", "sw_k_norm_g", "sw_sinks", "norm2_g",
               "mem_norm_g", "xa_q_norm_g", "xa_k_norm_g", "norm3_g")
BIG_NAMES = ("w_in", "w_out", "xa_wq", "xa_wkv", "xa_wo", "mlp_up", "mlp_down")
WEIGHT_ORDER = ("norm1_g", "w_in", "hg_lower_bounds", "hg_norm_g", "sw_q_norm_g", "sw_k_norm_g", "sw_sinks", "w_out",
                "norm2_g", "mem_norm_g", "xa_wq", "xa_wkv", "xa_q_norm_g", "xa_k_norm_g", "xa_wo", "norm3_g",
                "mlp_up", "mlp_down")


def _pack_rows(vals, width):
    starts, at = [], 0
    for v in vals:
        starts.append(at)
        at += v.shape[0]
    total = at + (-at) % 8
    out = None
    for v, s in zip(vals, starts):
        placed = jnp.pad(v, ((s, total - s - v.shape[0]), (0, width - v.shape[1])))
        out = placed if out is None else out + placed
    return out, starts


class _MeshWeights:
    LATE = ("w_out", "xa_wq", "xa_wkv", "xa_wo", "mlp_up", "mlp_down")

    def __init__(self, shards, d, ff):
        self.shards, self.d, self.ff = shards, d, ff
        self.c_idx = lax.axis_index("c").astype(jnp.int32).reshape(1)
        chip = (2 * lax.axis_index("x") + lax.axis_index("y")).astype(jnp.int32)
        self.place_idx = jnp.stack([chip, lax.axis_index("c").astype(jnp.int32)])
        self.pending = []
        self.exchanging = None

    def begin(self):
        shard = self.shards["w_in"]
        start, self.in_wait = _gather_chips_split(
            "gather_in", [shard], [_place_shard(shard, self.place_idx, name="place_w_in")])
        self.in_state = start()
        tok = (self.in_state["token"],)
        self.placed = list(_place_shards([self.shards[n] for n in self.LATE], self.place_idx, name="place_late",
                                         after=tok))
        return tok

    def first(self, after):
        _, lands = self.in_wait(self.in_state, (*after, *self.placed))
        (g_in,) = _gather_finish(lands, "gather_in_finish")
        start, self.late_wait = _gather_chips_split("gather_late", [self.shards[n] for n in self.LATE], self.placed)
        self.late_state = start(after=(g_in,))
        return dict(w_in_t=g_in.reshape(-1, self.d), token=self.late_state["token"])

    def rest(self, after):
        _, lands = self.late_wait(self.late_state, (after,))
        g_out, g_q, g_kv, g_o = _gather_finish(lands[:4], "gather_late_finish")
        start, self.mlp_wait = _gather_forward_split("gather_mlp_forward", lands[4:])
        self.mlp_state = start(after=(g_out,))
        d = self.d
        return dict(w_out=g_out.reshape(-1, d), wq=g_q.reshape(d, -1), wkv=g_kv.reshape(d, -1),
                    wo=jnp.concatenate([g_o[k] for k in range(4)], axis=1), token=self.mlp_state["token"])

    def mlp(self, after):
        _, (g_up, g_dn) = self.mlp_wait(self.mlp_state, (after,))
        return dict(up=g_up, down=g_dn.reshape(self.ff, self.d))

    def _scatter(self, tag, names, arrays, recv):
        parts = list(_add_halves(arrays, recv, self.c_idx, name="rs_add_halves_" + tag))
        start, wait = _scatter_chips_split("rs_scatter_" + tag, parts)
        state = start()
        self.pending.append((names, wait, state))
        return state["token"]

    def _advance(self, after):
        if self.exchanging is None:
            return ()
        tag, names, wait, state = self.exchanging
        self.exchanging = None
        arrays, recv = wait(state, (after,))
        return (self._scatter(tag, names, arrays, recv),)

    def poll(self, after):
        return self._advance(after)

    def grads(self, tag, g):
        d, ff = self.d, self.ff
        if tag == "mlp":
            names, arrays = ("mlp_up", "mlp_down"), [g["up"], g["down"].reshape(4, ff // 4, d)]
        elif tag == "mid":
            names = ("w_out", "xa_wq", "xa_wkv", "xa_wo")
            ds = d // 4
            g_wo = jnp.stack([g["wo"][:, ds * k:ds * (k + 1)] for k in range(4)])
            arrays = [g["w_out"].reshape(4, -1, d), g["wq"].reshape(4, d // 4, -1), g["wkv"].reshape(4, d // 4, -1), g_wo]
        else:
            names, arrays = ("w_in",), [g["w_in_t"].reshape(4, -1, d)]
        toks = self._advance(arrays[0])
        if tag == "in":
            return toks + (self._scatter(tag, names, arrays, _exchange_halves(arrays, "rs_exchange_" + tag)),)
        start, wait = _exchange_halves_split("rs_exchange_" + tag, arrays)
        state = start()
        self.exchanging = (tag, names, wait, state)
        return toks + (state["token"],)

    def finish(self, after):
        halves, tok = {}, ()
        for names, wait, state in self.pending:
            srcs, lands = wait(state, tuple(after) + tok)
            fulls = _add_chips(srcs, lands, self.place_idx, name="rs_add_chips_" + names[0], after=tok)
            tok = (fulls[0],)
            halves.update(zip(names, fulls))
        return dict(zip(BIG_NAMES, _join_halves([halves[n] for n in BIG_NAMES])))


def kernel(x, mem, positions, norm1_g, w_in, hg_lower_bounds, hg_norm_g, sw_q_norm_g, sw_k_norm_g, sw_sinks, w_out, norm2_g, mem_norm_g, xa_wq, xa_wkv, xa_q_norm_g, xa_k_norm_g, xa_wo, norm3_g, mlp_up, mlp_down, loss_target, m_norm1_g, m_w_in, m_hg_lower_bounds, m_hg_norm_g, m_sw_q_norm_g, m_sw_k_norm_g, m_sw_sinks, m_w_out, m_norm2_g, m_mem_norm_g, m_xa_wq, m_xa_wkv, m_xa_q_norm_g, m_xa_k_norm_g, m_xa_wo, m_norm3_g, m_mlp_up, m_mlp_down, v_norm1_g, v_w_in, v_hg_lower_bounds, v_hg_norm_g, v_sw_q_norm_g, v_sw_k_norm_g, v_sw_sinks, v_w_out, v_norm2_g, v_mem_norm_g, v_xa_wq, v_xa_wkv, v_xa_q_norm_g, v_xa_k_norm_g, v_xa_wo, v_norm3_g, v_mlp_up, v_mlp_down):
    given = dict(locals())
    weights = {n: given[n] for n in WEIGHT_ORDER}
    moms = {n: given["m_" + n] for n in WEIGHT_ORDER}
    vars_ = {n: given["v_" + n] for n in WEIGHT_ORDER}
    d = x.shape[-1]
    ff = mlp_down.shape[1] * 4
    small = {n: weights[n] for n in SMALL_NAMES}

    def plain(n, a):
        return jnp.swapaxes(a[0], 0, 1) if n == "w_in" else a[0]

    comm = _MeshWeights({n: plain(n, weights[n]).astype(_MXU_DTYPE) for n in BIG_NAMES}, d, ff)
    loss_row, grad_x, g_small = _local_step(x, mem, positions, loss_target, small, comm)
    packed, starts = _pack_rows([g_small[n] for n in SMALL_NAMES] + [loss_row], 1024)
    start, wait = _all_gather_small_split(packed)
    state = start()
    big_grads = comm.finish((grad_x, state["token"]))
    (own,), (gathered,) = wait(state, (big_grads[BIG_NAMES[0]],))
    device = (4 * lax.axis_index("x") + 2 * lax.axis_index("y") + lax.axis_index("c")).astype(jnp.int32).reshape(1)
    summed = _sum_devices(own, gathered, device)
    small_grads = {}
    for n, s in zip(SMALL_NAMES, starts):
        r, c = weights[n].shape
        small_grads[n] = summed[s:s + r, 0:c]
    loss = summed[starts[-1], 0]

    grads, deltas, new_m, new_v = {}, {}, {}, {}
    big_out = _adamw_big([plain(n, weights[n]) for n in BIG_NAMES], [big_grads[n] for n in BIG_NAMES],
                         [plain(n, moms[n]) for n in BIG_NAMES], [plain(n, vars_[n]) for n in BIG_NAMES])
    for n, outs in zip(BIG_NAMES, big_out):
        grads[n], deltas[n], new_m[n], new_v[n] = ((jnp.swapaxes(a, 0, 1) if n == "w_in" else a)[None] for a in outs)
    sm_out = _adamw_small([weights[n] for n in SMALL_NAMES], [small_grads[n] for n in SMALL_NAMES],
                          [moms[n] for n in SMALL_NAMES], [vars_[n] for n in SMALL_NAMES])
    ns = len(SMALL_NAMES)
    for i, n in enumerate(SMALL_NAMES):
        grads[n], deltas[n], new_m[n], new_v[n] = small_grads[n], sm_out[i], sm_out[ns + i], sm_out[2 * ns + i]

    return (loss, grad_x, *[grads[n] for n in WEIGHT_ORDER], *[deltas[n] for n in WEIGHT_ORDER],
            *[new_m[n] for n in WEIGHT_ORDER], *[new_v[n] for n in WEIGHT_ORDER])
```
